```python
import jax, jax.numpy as jnp
from jax import lax
import numpy as np

D_MODEL = 1024
BATCH = 8
SEQ = 4096
DEPTH = 1

MIX_WIDTH = D_MODEL
ATTN_WIDTH = D_MODEL // 2
HGRN_WIDTH = MIX_WIDTH - ATTN_WIDTH

HEAD_DIM = 64
N_Q_HEADS = ATTN_WIDTH // HEAD_DIM
N_KV_HEADS = 2
Q_PER_KV = N_Q_HEADS // N_KV_HEADS
WINDOW = 128
ATTN_BLOCK = 128
ROPE_THETA = 500000.0
ROPE_DIM = HEAD_DIM // 4

HGRN_EXPAND = 128
N_HGRN_HEADS = HGRN_WIDTH // HGRN_EXPAND
HGRN_DK = HGRN_EXPAND
HGRN_DV = HGRN_WIDTH // N_HGRN_HEADS
HGRN_CHUNK = 64

D_FF = 2816
CONV_WIDTH = 3

LN_EPS = 1e-5
RMS_EPS = 1e-6
DEEPNORM_ALPHA = (2 * DEPTH) ** 0.25
DEEPNORM_BETA = (8 * DEPTH) ** -0.25

COL_SIZES = (
    N_Q_HEADS * HEAD_DIM,
    N_KV_HEADS * HEAD_DIM,
    N_KV_HEADS * HEAD_DIM,
    N_HGRN_HEADS * HGRN_DK,
    N_HGRN_HEADS * HGRN_DK,
    N_HGRN_HEADS * HGRN_DV,
    N_HGRN_HEADS * HGRN_DV,
)
IN_COLS = sum(COL_SIZES)
COL_SPLITS = tuple(int(v) for v in np.cumsum(COL_SIZES)[:-1])

kernel_name = "hymba_swa_sink_hgrn2_deepnorm_convffn"


def layer_norm(x, g, b):
    xf = x.astype(jnp.float32)
    mu = xf.mean(-1, keepdims=True)
    var = jnp.square(xf - mu).mean(-1, keepdims=True)
    y = (xf - mu) * lax.rsqrt(var + LN_EPS) * g.astype(jnp.float32) + b.astype(jnp.float32)
    return y.astype(x.dtype)


def rms_norm(x, g):
    xf = x.astype(jnp.float32)
    y = xf * lax.rsqrt(jnp.mean(jnp.square(xf), -1, keepdims=True) + RMS_EPS) * g.astype(jnp.float32)
    return y


def partial_rope(t, positions):
    half = ROPE_DIM // 2
    inv_freq = ROPE_THETA ** (-jnp.arange(half, dtype=jnp.float32) * (2.0 / ROPE_DIM))
    ang = positions.astype(jnp.float32)[..., None] * inv_freq
    cos = jnp.cos(ang)[:, :, None, :]
    sin = jnp.sin(ang)[:, :, None, :]
    tr = t[..., :ROPE_DIM].astype(jnp.float32)
    t1, t2 = tr[..., :half], tr[..., half:]
    rot = jnp.concatenate([t1 * cos - t2 * sin, t2 * cos + t1 * sin], -1).astype(t.dtype)
    return jnp.concatenate([rot, t[..., ROPE_DIM:]], -1)


def sliding_window_attention(q, k, v, sinks):
    B, S = q.shape[0], q.shape[1]
    nb = S // ATTN_BLOCK
    qb = q.reshape(B, nb, ATTN_BLOCK, N_KV_HEADS, Q_PER_KV, HEAD_DIM)

    def band(t):
        tp = jnp.pad(t, ((0, 0), (ATTN_BLOCK, 0), (0, 0), (0, 0)))
        tp = tp.reshape(B, nb + 1, ATTN_BLOCK, N_KV_HEADS, HEAD_DIM)
        return jnp.concatenate([tp[:, :-1], tp[:, 1:]], axis=2)

    kb, vb = band(k), band(v)
    scale = HEAD_DIM ** -0.5
    scores = jnp.einsum('bnqhgd,bnkhd->bnhgqk', qb, kb).astype(jnp.float32) * scale

    qi = jnp.arange(ATTN_BLOCK)[:, None]
    kj = jnp.arange(2 * ATTN_BLOCK)[None, :]
    dist = qi + ATTN_BLOCK - kj
    in_band = (dist >= 0) & (dist < WINDOW)
    key_pos = jnp.arange(nb)[:, None, None] * ATTN_BLOCK - ATTN_BLOCK + kj[None]
    mask = in_band[None] & (key_pos >= 0)
    s = jnp.where(mask[None, :, None, None], scores, -jnp.inf)

    sink = sinks.astype(jnp.float32).reshape(N_KV_HEADS, Q_PER_KV)[None, None, :, :, None, None]
    m = jnp.maximum(s.max(-1, keepdims=True), sink)
    p = jnp.exp(s - m)
    probs = p / (p.sum(-1, keepdims=True) + jnp.exp(sink - m))
    o = jnp.einsum('bnhgqk,bnkhd->bnqhgd', probs.astype(v.dtype), vb)
    return o.reshape(B, S, N_Q_HEADS * HEAD_DIM)


def hgrn2_chunkwise(q, k, log_f, v):
    B, S, H, DK = q.shape
    DV = v.shape[-1]
    nc = S // HGRN_CHUNK

    def to_chunks(t):
        return t.reshape(B, nc, HGRN_CHUNK, H, t.shape[-1]).transpose(1, 0, 3, 2, 4)

    causal = jnp.tril(jnp.ones((HGRN_CHUNK, HGRN_CHUNK), dtype=bool))

    def step(state, inp):
        qc, kc, gc, vc = inp
        b = jnp.cumsum(gc, axis=2)
        diff = b[:, :, :, None, :] - b[:, :, None, :, :]
        decay = jnp.exp(jnp.where(causal[:, :, None], diff, -jnp.inf))
        attn = jnp.einsum('bhtd,bhsd,bhtsd->bhts', qc, kc, decay)
        o = jnp.einsum('bhts,bhsv->bhtv', attn, vc) + \
            jnp.einsum('bhtd,bhdv->bhtv', qc * jnp.exp(b), state)
        b_last = b[:, :, -1:, :]
        k_dec = kc * jnp.exp(b_last - b)
        state = state * jnp.exp(b_last)[:, :, 0, :, None] + jnp.einsum('bhsd,bhsv->bhdv', k_dec, vc)
        return state, o

    s0 = jnp.zeros((B, H, DK, DV), jnp.float32)
    _, o = lax.scan(step, s0, (to_chunks(q), to_chunks(k), to_chunks(log_f), to_chunks(v)))
    return o.transpose(1, 0, 3, 2, 4).reshape(B, S, H, DV)


def hgrn2_mixer(hq, hf, hi, hg, lower_bound, norm_g):
    B, S = hq.shape[0], hq.shape[1]
    q = jax.nn.silu(hq.astype(jnp.float32)).reshape(B, S, N_HGRN_HEADS, HGRN_DK)
    lb = lower_bound.reshape(N_HGRN_HEADS, HGRN_DK)
    f = lb + (1.0 - lb) * jax.nn.sigmoid(hf.astype(jnp.float32).reshape(B, S, N_HGRN_HEADS, HGRN_DK))
    k = 1.0 - f
    log_f = jnp.log(f)
    v = hi.astype(jnp.float32).reshape(B, S, N_HGRN_HEADS, HGRN_DV)
    o = hgrn2_chunkwise(q, k, log_f, v)
    gate = jax.nn.silu(hg.astype(jnp.float32).reshape(B, S, N_HGRN_HEADS, HGRN_DV))
    o = rms_norm(o, norm_g) * gate
    return o.reshape(B, S, HGRN_WIDTH).astype(hq.dtype)


def conv_gated_mlp(x, w_up, conv_w, conv_b, w_down):
    S = x.shape[1]
    u = x @ w_up
    gate, val = u[..., :D_FF], u[..., D_FF:]
    gp = jnp.pad(gate, ((0, 0), (CONV_WIDTH - 1, 0), (0, 0)))
    gate = sum(gp[:, j:j + S] * conv_w[j] for j in range(CONV_WIDTH)) + conv_b
    h = jax.nn.silu(gate) * val
    return h @ w_down


def _fwd_setup_inputs(seed: int = 0) -> dict:
    key = jax.random.key(seed)
    ks = jax.random.split(key, 20)
    f32 = jnp.float32
    x = jax.random.normal(ks[0], (BATCH, SEQ, D_MODEL), f32)
    offs = jax.random.randint(ks[1], (BATCH, 1), 0, 1024, dtype=jnp.int32)
    positions = jnp.arange(SEQ, dtype=jnp.int32)[None, :] + offs

    ln1_g = 1.0 + 0.02 * jax.random.normal(ks[2], (DEPTH, D_MODEL), f32)
    ln1_b = 0.02 * jax.random.normal(ks[3], (DEPTH, D_MODEL), f32)
    col_scale = np.ones((IN_COLS,), np.float32)
    o0 = COL_SIZES[0] + COL_SIZES[1]
    col_scale[o0:o0 + COL_SIZES[2]] = DEEPNORM_BETA
    o1 = sum(COL_SIZES[:5])
    col_scale[o1:o1 + COL_SIZES[5]] = DEEPNORM_BETA
    w_in = jax.random.normal(ks[4], (DEPTH, D_MODEL, IN_COLS), f32) * D_MODEL ** -0.5 * jnp.asarray(col_scale)
    b_in = 0.02 * jax.random.normal(ks[5], (DEPTH, IN_COLS), f32)
    sinks = 0.5 * jax.random.normal(ks[6], (DEPTH, N_Q_HEADS), f32)
    hgrn_lb = 0.5 * jax.random.normal(ks[7], (DEPTH + 1, HGRN_WIDTH), f32)
    hgrn_norm_g = 1.0 + 0.02 * jax.random.normal(ks[8], (DEPTH, HGRN_DV), f32)
    w_o = jax.random.normal(ks[9], (DEPTH, MIX_WIDTH, D_MODEL), f32) * MIX_WIDTH ** -0.5 * DEEPNORM_BETA
    ln2_g = 1.0 + 0.02 * jax.random.normal(ks[10], (DEPTH, D_MODEL), f32)
    ln2_b = 0.02 * jax.random.normal(ks[11], (DEPTH, D_MODEL), f32)
    w_up = jax.random.normal(ks[12], (DEPTH, D_MODEL, 2 * D_FF), f32) * D_MODEL ** -0.5 * DEEPNORM_BETA
    conv_w = jax.random.normal(ks[13], (DEPTH, CONV_WIDTH, D_FF), f32) * CONV_WIDTH ** -0.5
    conv_b = 0.02 * jax.random.normal(ks[14], (DEPTH, D_FF), f32)
    w_down = jax.random.normal(ks[15], (DEPTH, D_FF, D_MODEL), f32) * D_FF ** -0.5 * DEEPNORM_BETA
    return {"x": x, "positions": positions, "ln1_g": ln1_g, "ln1_b": ln1_b, "w_in": w_in,
            "b_in": b_in, "sinks": sinks, "hgrn_lb": hgrn_lb, "hgrn_norm_g": hgrn_norm_g,
            "w_o": w_o, "ln2_g": ln2_g, "ln2_b": ln2_b, "w_up": w_up, "conv_w": conv_w,
            "conv_b": conv_b, "w_down": w_down}


def _fwd_reference(x, positions, ln1_g, ln1_b, w_in, b_in, sinks, hgrn_lb, hgrn_norm_g,
              w_o, ln2_g, ln2_b, w_up, conv_w, conv_b, w_down):
    B, S, _ = x.shape
    lbs = jnp.cumsum(jax.nn.softmax(hgrn_lb.astype(jnp.float32), axis=0), axis=0)
    for l in range(DEPTH):
        u = x @ w_in[l] + b_in[l]
        aq, ak, av, hq, hf, hi, hg = jnp.split(u, COL_SPLITS, axis=-1)
        aq = partial_rope(aq.reshape(B, S, N_Q_HEADS, HEAD_DIM), positions)
        ak = partial_rope(ak.reshape(B, S, N_KV_HEADS, HEAD_DIM), positions)
        av = av.reshape(B, S, N_KV_HEADS, HEAD_DIM)
        a_out = sliding_window_attention(aq, ak, av, sinks[l])
        r_out = hgrn2_mixer(hq, hf, hi, hg, lbs[l], hgrn_norm_g[l])
        mix = jnp.concatenate([a_out, r_out], axis=-1) @ w_o[l]
        x = layer_norm(DEEPNORM_ALPHA * x + mix, ln1_g[l], ln1_b[l])
        ffn = conv_gated_mlp(x, w_up[l], conv_w[l], conv_b[l], w_down[l])
        x = layer_norm(DEEPNORM_ALPHA * x + ffn, ln2_g[l], ln2_b[l])
    return x


import jax as _jax
import jax.numpy as _jnp

TWIN_FORMAT = 'train_step'
FWD_PARAMS = ['x', 'positions', 'ln1_g', 'ln1_b', 'w_in', 'b_in', 'sinks', 'hgrn_lb', 'hgrn_norm_g', 'w_o', 'ln2_g', 'ln2_b', 'w_up', 'conv_w', 'conv_b', 'w_down']
TWIN_WEIGHTS = ['ln1_g', 'ln1_b', 'w_in', 'b_in', 'sinks', 'hgrn_lb', 'hgrn_norm_g', 'w_o', 'ln2_g', 'ln2_b', 'w_up', 'conv_w', 'conv_b', 'w_down']
TWIN_DIFF_INPUT = 'x'
TWIN_INPUTS = ['x', 'positions', 'ln1_g', 'ln1_b', 'w_in', 'b_in', 'sinks', 'hgrn_lb', 'hgrn_norm_g', 'w_o', 'ln2_g', 'ln2_b', 'w_up', 'conv_w', 'conv_b', 'w_down', 'loss_target', 'm_ln1_g', 'm_ln1_b', 'm_w_in', 'm_b_in', 'm_sinks', 'm_hgrn_lb', 'm_hgrn_norm_g', 'm_w_o', 'm_ln2_g', 'm_ln2_b', 'm_w_up', 'm_conv_w', 'm_conv_b', 'm_w_down', 'v_ln1_g', 'v_ln1_b', 'v_w_in', 'v_b_in', 'v_sinks', 'v_hgrn_lb', 'v_hgrn_norm_g', 'v_w_o', 'v_ln2_g', 'v_ln2_b', 'v_w_up', 'v_conv_w', 'v_conv_b', 'v_w_down']
TWIN_OUTPUTS = ['loss', 'grad_x', 'grad_ln1_g', 'grad_ln1_b', 'grad_w_in', 'grad_b_in', 'grad_sinks', 'grad_hgrn_lb', 'grad_hgrn_norm_g', 'grad_w_o', 'grad_ln2_g', 'grad_ln2_b', 'grad_w_up', 'grad_conv_w', 'grad_conv_b', 'grad_w_down', 'delta_ln1_g', 'delta_ln1_b', 'delta_w_in', 'delta_b_in', 'delta_sinks', 'delta_hgrn_lb', 'delta_hgrn_norm_g', 'delta_w_o', 'delta_ln2_g', 'delta_ln2_b', 'delta_w_up', 'delta_conv_w', 'delta_conv_b', 'delta_w_down', 'new_m_ln1_g', 'new_m_ln1_b', 'new_m_w_in', 'new_m_b_in', 'new_m_sinks', 'new_m_hgrn_lb', 'new_m_hgrn_norm_g', 'new_m_w_o', 'new_m_ln2_g', 'new_m_ln2_b', 'new_m_w_up', 'new_m_conv_w', 'new_m_conv_b', 'new_m_w_down', 'new_v_ln1_g', 'new_v_ln1_b', 'new_v_w_in', 'new_v_b_in', 'new_v_sinks', 'new_v_hgrn_lb', 'new_v_hgrn_norm_g', 'new_v_w_o', 'new_v_ln2_g', 'new_v_ln2_b', 'new_v_w_up', 'new_v_conv_w', 'new_v_conv_b', 'new_v_w_down']
TWIN_LEAF_KINDS = {'loss': 'loss', 'grad_x': 'grad_x', 'grad_ln1_g': 'grad_w', 'grad_ln1_b': 'grad_w', 'grad_w_in': 'grad_w', 'grad_b_in': 'grad_w', 'grad_sinks': 'grad_w', 'grad_hgrn_lb': 'grad_w', 'grad_hgrn_norm_g': 'grad_w', 'grad_w_o': 'grad_w', 'grad_ln2_g': 'grad_w', 'grad_ln2_b': 'grad_w', 'grad_w_up': 'grad_w', 'grad_conv_w': 'grad_w', 'grad_conv_b': 'grad_w', 'grad_w_down': 'grad_w', 'delta_ln1_g': 'delta_w', 'delta_ln1_b': 'delta_w', 'delta_w_in': 'delta_w', 'delta_b_in': 'delta_w', 'delta_sinks': 'delta_w', 'delta_hgrn_lb': 'delta_w', 'delta_hgrn_norm_g': 'delta_w', 'delta_w_o': 'delta_w', 'delta_ln2_g': 'delta_w', 'delta_ln2_b': 'delta_w', 'delta_w_up': 'delta_w', 'delta_conv_w': 'delta_w', 'delta_conv_b': 'delta_w', 'delta_w_down': 'delta_w', 'new_m_ln1_g': 'new_m', 'new_m_ln1_b': 'new_m', 'new_m_w_in': 'new_m', 'new_m_b_in': 'new_m', 'new_m_sinks': 'new_m', 'new_m_hgrn_lb': 'new_m', 'new_m_hgrn_norm_g': 'new_m', 'new_m_w_o': 'new_m', 'new_m_ln2_g': 'new_m', 'new_m_ln2_b': 'new_m', 'new_m_w_up': 'new_m', 'new_m_conv_w': 'new_m', 'new_m_conv_b': 'new_m', 'new_m_w_down': 'new_m', 'new_v_ln1_g': 'new_v', 'new_v_ln1_b': 'new_v', 'new_v_w_in': 'new_v', 'new_v_b_in': 'new_v', 'new_v_sinks': 'new_v', 'new_v_hgrn_lb': 'new_v', 'new_v_hgrn_norm_g': 'new_v', 'new_v_w_o': 'new_v', 'new_v_ln2_g': 'new_v', 'new_v_ln2_b': 'new_v', 'new_v_w_up': 'new_v', 'new_v_conv_w': 'new_v', 'new_v_conv_b': 'new_v', 'new_v_w_down': 'new_v'}


def _forward(args):
    return _fwd_reference(*[args[k] for k in FWD_PARAMS])


def _output_shape():
    out = _jax.eval_shape(lambda: _forward(_fwd_setup_inputs(0)))
    return out.shape, out.dtype

N_MICROBATCH = 1
ADAM_LR = 0.001
ADAM_B1 = 0.9
ADAM_B2 = 0.999
ADAM_EPS = 1e-08
ADAM_WD = 0.01
ADAM_STEP = 10
PER_EXAMPLE_BATCH_AXIS = {'x': 0, 'positions': 0, 'loss_target': 0}
SHARED_INPUTS = []
_WEIGHT_DTYPES = {'ln1_g': _jnp.float32, 'ln1_b': _jnp.float32, 'w_in': _jnp.float32, 'b_in': _jnp.float32, 'sinks': _jnp.float32, 'hgrn_lb': _jnp.float32, 'hgrn_norm_g': _jnp.float32, 'w_o': _jnp.float32, 'ln2_g': _jnp.float32, 'ln2_b': _jnp.float32, 'w_up': _jnp.float32, 'conv_w': _jnp.float32, 'conv_b': _jnp.float32, 'w_down': _jnp.float32}
MOMENT_SCALE = {'ln1_g': 1.051700e+00, 'ln1_b': 4.271237e-01, 'w_in': 4.489181e-02, 'b_in': 1.490628e-01, 'sinks': 1.053017e-02, 'hgrn_lb': 5.044809e-03, 'hgrn_norm_g': 1.014083e-01, 'w_o': 6.289480e-02, 'ln2_g': 3.202690e+01, 'ln2_b': 6.606578e-01, 'w_up': 1.829600e-02, 'conv_w': 1.128432e-02, 'conv_b': 1.885376e-02, 'w_down': 2.980862e-02}


def _to_microbatches(a, axis):
    t = _jnp.moveaxis(a, axis, 0)
    t = t.reshape((N_MICROBATCH, t.shape[0] // N_MICROBATCH) + t.shape[1:])
    return _jnp.moveaxis(t, 1, axis + 1)


def setup_inputs(seed: int = 0) -> dict:
    inp = _fwd_setup_inputs(seed)
    key = _jax.random.fold_in(_jax.random.key(seed), 7919)
    shape, _ = _output_shape()
    out = dict(inp)
    out["loss_target"] = _jax.random.normal(_jax.random.fold_in(key, 0), shape, _jnp.float32)
    for i, name in enumerate(TWIN_WEIGHTS):
        w = inp[name].astype(_jnp.float32)
        if MOMENT_SCALE is None:
            s = _jnp.sqrt(_jnp.mean(_jnp.square(w)) + 1e-30)
        else:
            s = MOMENT_SCALE[name]
        km, kv = _jax.random.split(_jax.random.fold_in(key, i + 1))
        out[name] = w
        out["m_" + name] = s * _jax.random.normal(km, w.shape, _jnp.float32)
        out["v_" + name] = (s * s) * _jax.random.uniform(kv, w.shape, _jnp.float32, 0.5, 1.5)
    if N_MICROBATCH > 1:
        for name, axis in PER_EXAMPLE_BATCH_AXIS.items():
            out[name] = _to_microbatches(out[name], axis)
    return {'x': out['x'], 'positions': out['positions'], 'ln1_g': out['ln1_g'], 'ln1_b': out['ln1_b'], 'w_in': out['w_in'], 'b_in': out['b_in'], 'sinks': out['sinks'], 'hgrn_lb': out['hgrn_lb'], 'hgrn_norm_g': out['hgrn_norm_g'], 'w_o': out['w_o'], 'ln2_g': out['ln2_g'], 'ln2_b': out['ln2_b'], 'w_up': out['w_up'], 'conv_w': out['conv_w'], 'conv_b': out['conv_b'], 'w_down': out['w_down'], 'loss_target': out['loss_target'], 'm_ln1_g': out['m_ln1_g'], 'm_ln1_b': out['m_ln1_b'], 'm_w_in': out['m_w_in'], 'm_b_in': out['m_b_in'], 'm_sinks': out['m_sinks'], 'm_hgrn_lb': out['m_hgrn_lb'], 'm_hgrn_norm_g': out['m_hgrn_norm_g'], 'm_w_o': out['m_w_o'], 'm_ln2_g': out['m_ln2_g'], 'm_ln2_b': out['m_ln2_b'], 'm_w_up': out['m_w_up'], 'm_conv_w': out['m_conv_w'], 'm_conv_b': out['m_conv_b'], 'm_w_down': out['m_w_down'], 'v_ln1_g': out['v_ln1_g'], 'v_ln1_b': out['v_ln1_b'], 'v_w_in': out['v_w_in'], 'v_b_in': out['v_b_in'], 'v_sinks': out['v_sinks'], 'v_hgrn_lb': out['v_hgrn_lb'], 'v_hgrn_norm_g': out['v_hgrn_norm_g'], 'v_w_o': out['v_w_o'], 'v_ln2_g': out['v_ln2_g'], 'v_ln2_b': out['v_ln2_b'], 'v_w_up': out['v_w_up'], 'v_conv_w': out['v_conv_w'], 'v_conv_b': out['v_conv_b'], 'v_w_down': out['v_w_down']}


def _loss(weights, diff, rest, loss_target):
    with _jax.named_scope("forward"):
        args = {**rest, TWIN_DIFF_INPUT: diff, **{k: w.astype(_WEIGHT_DTYPES[k]) for k, w in weights.items()}}
        y = _forward(args)
    with _jax.named_scope("loss_head"):
        err = _jnp.square(y.astype(_jnp.float32) - loss_target)
        return 0.5 * _jnp.sum(_jnp.mean(err, axis=-1)) if err.ndim else 0.5 * err


def _adamw(w, g, m, v):
    m = ADAM_B1 * m + (1.0 - ADAM_B1) * g
    v = ADAM_B2 * v + (1.0 - ADAM_B2) * _jnp.square(g)
    m_hat = m / (1.0 - ADAM_B1 ** ADAM_STEP)
    v_hat = v / (1.0 - ADAM_B2 ** ADAM_STEP)
    delta = -ADAM_LR * (m_hat / (_jnp.sqrt(v_hat) + ADAM_EPS) + ADAM_WD * w)
    return delta, m, v


def reference(x, positions, ln1_g, ln1_b, w_in, b_in, sinks, hgrn_lb, hgrn_norm_g, w_o, ln2_g, ln2_b, w_up, conv_w, conv_b, w_down, loss_target, m_ln1_g, m_ln1_b, m_w_in, m_b_in, m_sinks, m_hgrn_lb, m_hgrn_norm_g, m_w_o, m_ln2_g, m_ln2_b, m_w_up, m_conv_w, m_conv_b, m_w_down, v_ln1_g, v_ln1_b, v_w_in, v_b_in, v_sinks, v_hgrn_lb, v_hgrn_norm_g, v_w_o, v_ln2_g, v_ln2_b, v_w_up, v_conv_w, v_conv_b, v_w_down):
    given = dict(x=x, positions=positions, ln1_g=ln1_g, ln1_b=ln1_b, w_in=w_in, b_in=b_in, sinks=sinks, hgrn_lb=hgrn_lb, hgrn_norm_g=hgrn_norm_g, w_o=w_o, ln2_g=ln2_g, ln2_b=ln2_b, w_up=w_up, conv_w=conv_w, conv_b=conv_b, w_down=w_down, loss_target=loss_target, m_ln1_g=m_ln1_g, m_ln1_b=m_ln1_b, m_w_in=m_w_in, m_b_in=m_b_in, m_sinks=m_sinks, m_hgrn_lb=m_hgrn_lb, m_hgrn_norm_g=m_hgrn_norm_g, m_w_o=m_w_o, m_ln2_g=m_ln2_g, m_ln2_b=m_ln2_b, m_w_up=m_w_up, m_conv_w=m_conv_w, m_conv_b=m_conv_b, m_w_down=m_w_down, v_ln1_g=v_ln1_g, v_ln1_b=v_ln1_b, v_w_in=v_w_in, v_b_in=v_b_in, v_sinks=v_sinks, v_hgrn_lb=v_hgrn_lb, v_hgrn_norm_g=v_hgrn_norm_g, v_w_o=v_w_o, v_ln2_g=v_ln2_g, v_ln2_b=v_ln2_b, v_w_up=v_w_up, v_conv_w=v_conv_w, v_conv_b=v_conv_b, v_w_down=v_w_down)
    weights = {n: given[n] for n in TWIN_WEIGHTS}
    shared = {n: given[n] for n in SHARED_INPUTS}
    per_example = {n: given[n] for n in ['x', 'positions']}
    grad_fn = _jax.value_and_grad(_loss, argnums=(0, 1))

    def one_microbatch(ex, loss_target):
        ex = dict(ex)
        diff = ex.pop(TWIN_DIFF_INPUT)
        return grad_fn(weights, diff, {**shared, **ex}, loss_target)

    if N_MICROBATCH == 1:
        loss, (grad_w, grad_x) = one_microbatch(per_example, given["loss_target"])
    else:
        def body(carry, xs):
            loss_sum, grad_sum = carry
            l_k, (gw_k, gx_k) = one_microbatch(xs[0], xs[1])
            with _jax.named_scope("update"):
                return (loss_sum + l_k, _jax.tree.map(_jnp.add, grad_sum, gw_k)), gx_k

        init = (_jnp.zeros((), _jnp.float32), _jax.tree.map(_jnp.zeros_like, weights))
        (loss, grad_w), grad_x = _jax.lax.scan(body, init, (per_example, given["loss_target"]))
    with _jax.named_scope("update"):
        delta_w, new_m, new_v = {}, {}, {}
        for n in TWIN_WEIGHTS:
            delta_w[n], new_m[n], new_v[n] = _adamw(weights[n], grad_w[n], given["m_" + n], given["v_" + n])
    return (loss, grad_x, *[grad_w[n] for n in TWIN_WEIGHTS], *[delta_w[n] for n in TWIN_WEIGHTS],
            *[new_m[n] for n in TWIN_WEIGHTS], *[new_v[n] for n in TWIN_WEIGHTS])
```

```python
import functools

import jax
import jax.numpy as jnp
import numpy as np
from jax import lax
from jax.experimental import pallas as pl
from jax.experimental.pallas import tpu as pltpu

F32 = jnp.float32
BF16 = jnp.bfloat16

N_DEV = 8
D_MODEL = 1024
D_FF = 2816
ATTN_W = 512
KV_W = 128
UA_W = ATTN_W + 2 * KV_W
UH_W = 2048
HG_W = 512
ATTN_BLOCK = 128
HGRN_CHUNK = 64
HGRN_SUB = 16
EXP_CLAMP = 85.0
NEG_BIG = -1e30
LN_EPS = 1e-5
RMS_EPS = 1e-6
ALPHA = 2.0 ** 0.25
ATTN_SCALE = 0.125
ROPE_THETA = 500000.0

ADAM_LR = 0.001
ADAM_B1 = 0.9
ADAM_B2 = 0.999
ADAM_EPS = 1e-08
ADAM_WD = 0.01
ADAM_STEP = 10

LANES = 128
VMEM_LIMIT_BYTES = 56 * 1024 * 1024

SHARD_IN = D_FF // N_DEV
SHARD_UP = 2 * D_FF // N_DEV
SHARD_O = D_MODEL // N_DEV
SHARD_DOWN = D_FF // N_DEV
ROWS_IN = D_MODEL * SHARD_IN // LANES
ROWS_O = SHARD_O * D_MODEL // LANES
ROWS_UP = D_MODEL * SHARD_UP // LANES
ROWS_DOWN = SHARD_DOWN * D_MODEL // LANES
ROWS_W = ROWS_IN + ROWS_O + ROWS_UP + ROWS_DOWN
CONVW_ELEMS = 3 * SHARD_IN
ROWS_CONVW = 16
GPACK_TILE = 256
ROWS_G = 12544
SMALL_ROWS = 88

_MESH = pl.DeviceIdType.MESH
_NT = (((1,), (1,)), ((), ()))
_NN = (((1,), (0,)), ((), ()))
_TN = (((0,), (0,)), ((), ()))


def _cp(*sem):
    if sem:
        return pltpu.CompilerParams(dimension_semantics=sem, vmem_limit_bytes=VMEM_LIMIT_BYTES)
    return pltpu.CompilerParams(vmem_limit_bytes=VMEM_LIMIT_BYTES)


def _sig(x):
    return 1.0 / (1.0 + jnp.exp(-x))


def _dsilu(x, s):
    return s * (1.0 + x * (1.0 - s))


def _dot(a, b, dims):
    return lax.dot_general(a.astype(BF16), b.astype(BF16), dims, preferred_element_type=F32)


def _split(a):
    hi = a.astype(BF16)
    return hi, (a - hi.astype(F32)).astype(BF16)


def _dot3(a, b, dims):
    ah, al = _split(a)
    bh, bl = _split(b)
    d = functools.partial(lax.dot_general, dimension_numbers=dims, preferred_element_type=F32)
    return d(ah, bh) + (d(ah, bl) + d(al, bh))


def _pick(n, pref):
    for t in pref:
        if t <= n and n % t == 0:
            return t
    return n


def _my_coords():
    return lax.axis_index("x"), lax.axis_index("y"), lax.axis_index("c")


def _peer(k):
    x, y, c = _my_coords()
    return (1 - x if k & 4 else x, 1 - y if k & 2 else y, 1 - c if k & 1 else c)


def _me():
    x, y, c = _my_coords()
    return 4 * x + 2 * y + c


def _all_gather(x, name, space):
    rows, cols = x.shape

    def body(x_ref, out_ref, send_sems, recv_sems, local_sem):
        me = _me()
        local = pltpu.make_async_copy(x_ref, out_ref.at[me], local_sem)
        local.start()
        sends = []
        for k in range(1, N_DEV):
            cp = pltpu.make_async_remote_copy(
                src_ref=x_ref, dst_ref=out_ref.at[me], send_sem=send_sems.at[k - 1],
                recv_sem=recv_sems.at[k - 1], device_id=_peer(k), device_id_type=_MESH)
            cp.start()
            sends.append(cp)
        for k in range(1, N_DEV):
            src = jnp.bitwise_xor(me, k)
            pltpu.make_async_remote_copy(
                src_ref=x_ref, dst_ref=out_ref.at[src], send_sem=send_sems.at[k - 1],
                recv_sem=recv_sems.at[k - 1], device_id=_peer(k), device_id_type=_MESH).wait_recv()
        for cp in sends:
            cp.wait_send()
        local.wait()

    return pl.pallas_call(
        body, name=name,
        out_shape=jax.ShapeDtypeStruct((N_DEV, rows, cols), x.dtype),
        in_specs=[pl.BlockSpec(memory_space=space)],
        out_specs=pl.BlockSpec(memory_space=space),
        scratch_shapes=[pltpu.SemaphoreType.DMA((N_DEV - 1,)), pltpu.SemaphoreType.DMA((N_DEV - 1,)),
                        pltpu.SemaphoreType.DMA],
        compiler_params=_cp(),
    )(x)


def _exchange(gpack, name):
    _, rows, cols = gpack.shape

    def body(g_ref, out_ref, send_sems, recv_sems, local_sem):
        me = _me()
        local = pltpu.make_async_copy(g_ref.at[me], out_ref.at[me], local_sem)
        local.start()
        sends = []
        for k in range(1, N_DEV):
            dst_dev = jnp.bitwise_xor(me, k)
            cp = pltpu.make_async_remote_copy(
                src_ref=g_ref.at[dst_dev], dst_ref=out_ref.at[me], send_sem=send_sems.at[k - 1],
                recv_sem=recv_sems.at[k - 1], device_id=_peer(k), device_id_type=_MESH)
            cp.start()
            sends.append(cp)
        for k in range(1, N_DEV):
            src = jnp.bitwise_xor(me, k)
            pltpu.make_async_remote_copy(
                src_ref=g_ref.at[src], dst_ref=out_ref.at[src], send_sem=send_sems.at[k - 1],
                recv_sem=recv_sems.at[k - 1], device_id=_peer(k), device_id_type=_MESH).wait_recv()
        for cp in sends:
            cp.wait_send()
        local.wait()

    return pl.pallas_call(
        body, name=name,
        out_shape=jax.ShapeDtypeStruct(gpack.shape, gpack.dtype),
        in_specs=[pl.BlockSpec(memory_space=pl.ANY)],
        out_specs=pl.BlockSpec(memory_space=pl.ANY),
        scratch_shapes=[pltpu.SemaphoreType.DMA((N_DEV - 1,)), pltpu.SemaphoreType.DMA((N_DEV - 1,)),
                        pltpu.SemaphoreType.DMA],
        compiler_params=_cp(),
    )(gpack)


def _sum_slots_mine_f32(recv, own, name):
    _, rows, cols = recv.shape
    tr = GPACK_TILE

    def body(recv_ref, own_ref, out_ref):
        me = _me()
        acc = jnp.zeros((tr, cols), F32)
        for s in range(N_DEV):
            acc = acc + jnp.where(me == s, own_ref[...], recv_ref[s].astype(F32))
        out_ref[...] = acc

    return pl.pallas_call(
        body, name=name, grid=(rows // tr,),
        in_specs=[pl.BlockSpec((N_DEV, tr, cols), lambda i: (0, i, 0)),
                  pl.BlockSpec((tr, cols), lambda i: (i, 0))],
        out_specs=pl.BlockSpec((tr, cols), lambda i: (i, 0)),
        out_shape=jax.ShapeDtypeStruct((rows, cols), F32),
        compiler_params=_cp("parallel"),
    )(recv, own)


def _sum_slots(gathered, name):
    _, rows, cols = gathered.shape

    def body(g_ref, out_ref):
        acc = g_ref[0]
        for s in range(1, N_DEV):
            acc = acc + g_ref[s]
        out_ref[...] = acc

    return pl.pallas_call(
        body, name=name,
        out_shape=jax.ShapeDtypeStruct((rows, cols), F32),
        compiler_params=_cp(),
    )(gathered)


def _mm(a, b, *, name, ta=False, tb=False, out_dtype=F32, bias=None, addend=None, addend_scale=1.0,
        tm=512, tn=1024, tk=512):
    kdim, m = a.shape if ta else a.shape[::-1]
    n = b.shape[0] if tb else b.shape[1]
    tm = _pick(m, (tm, 512, 256, 128))
    tn = _pick(n, (tn, 1408, 1024, 768, 512, 256, 128))
    tk = _pick(kdim, (tk, 1408, 768, 512, 256, 128))
    nk = kdim // tk
    a_spec = pl.BlockSpec((tk, tm), lambda i, j, k: (k, i)) if ta else pl.BlockSpec((tm, tk), lambda i, j, k: (i, k))
    b_spec = pl.BlockSpec((tn, tk), lambda i, j, k: (j, k)) if tb else pl.BlockSpec((tk, tn), lambda i, j, k: (k, j))
    ins, specs = [a, b], [a_spec, b_spec]
    if bias is not None:
        ins.append(bias)
        specs.append(pl.BlockSpec((1, tn), lambda i, j, k: (0, j)))
    if addend is not None:
        ins.append(addend)
        specs.append(pl.BlockSpec((tm, tn), lambda i, j, k: (i, j)))
    dims = (((0,) if ta else (1,), (1,) if tb else (0,)), ((), ()))
    has_bias, has_addend = bias is not None, addend is not None

    def body(*refs):
        a_ref, b_ref = refs[0], refs[1]
        pos = 2
        bias_ref = addend_ref = None
        if has_bias:
            bias_ref = refs[pos]
            pos += 1
        if has_addend:
            addend_ref = refs[pos]
            pos += 1
        o_ref, acc_ref = refs[pos], refs[pos + 1]
        k = pl.program_id(2)

        @pl.when(k == 0)
        def _():
            acc_ref[...] = jnp.zeros_like(acc_ref)

        acc_ref[...] += _dot(a_ref[...], b_ref[...], dims)

        @pl.when(k == nk - 1)
        def _():
            r = acc_ref[...]
            if has_bias:
                r = r + bias_ref[...]
            if has_addend:
                r = r + addend_scale * addend_ref[...].astype(F32)
            o_ref[...] = r.astype(out_dtype)

    return pl.pallas_call(
        body, name=name, grid=(m // tm, n // tn, nk),
        in_specs=specs, out_specs=pl.BlockSpec((tm, tn), lambda i, j, k: (i, j)),
        out_shape=jax.ShapeDtypeStruct((m, n), out_dtype),
        scratch_shapes=[pltpu.VMEM((tm, tn), F32)],
        compiler_params=_cp("parallel", "parallel", "arbitrary"),
    )(*ins)


def _rope_lane_constants():
    inv_freq = np.float32(ROPE_THETA) ** (-np.arange(8, dtype=np.float32) * np.float32(2.0 / 16.0))
    lane = np.arange(LANES) % 64
    freq = np.where(lane < 16, inv_freq[lane % 8], 0.0).astype(np.float32)
    sign = np.where(lane < 8, -1.0, np.where(lane < 16, 1.0, 0.0)).astype(np.float32)
    return jnp.asarray(freq)[None, :], jnp.asarray(sign)[None, :]


def _rope_tables(pos_col, name):
    t = pos_col.shape[0]
    tr = _pick(t, (512, 256, 128))
    freq, sign = _rope_lane_constants()

    def body(pos_ref, freq_ref, sign_ref, c_ref, s_ref):
        ang = pos_ref[...].astype(F32) * freq_ref[...]
        c_ref[...] = jnp.cos(ang)
        s_ref[...] = sign_ref[...] * jnp.sin(ang)

    return pl.pallas_call(
        body, name=name, grid=(t // tr,),
        in_specs=[pl.BlockSpec((tr, 1), lambda i: (i, 0)),
                  pl.BlockSpec((1, LANES), lambda i: (0, 0)),
                  pl.BlockSpec((1, LANES), lambda i: (0, 0))],
        out_specs=[pl.BlockSpec((tr, LANES), lambda i: (i, 0)), pl.BlockSpec((tr, LANES), lambda i: (i, 0))],
        out_shape=[jax.ShapeDtypeStruct((t, LANES), F32), jax.ShapeDtypeStruct((t, LANES), F32)],
        compiler_params=_cp("parallel"),
    )(pos_col, freq, sign)


def _swap8(t):
    width = t.shape[1]
    lane = jnp.bitwise_and(lax.broadcasted_iota(jnp.int32, t.shape, 1), 63)
    return jnp.where(lane < 8, pltpu.roll(t, width - 8, 1), jnp.where(lane < 16, pltpu.roll(t, 8, 1), 0.0))


def _rope(t, c, s):
    return t * c + _swap8(t) * s


def _rope_bwd(d, c, s):
    return d * c + _swap8(d * s)


def _tile4(a):
    return jnp.concatenate([a, a, a, a], axis=1)


def _attn_band(n, k_cur, k_prev, v_cur, v_prev, c_cur, s_cur, c_prev, s_prev):
    kband = jnp.concatenate([_rope(k_prev, c_prev, s_prev), _rope(k_cur, c_cur, s_cur)], axis=0)
    vband = jnp.concatenate([v_prev, v_cur], axis=0)
    qi = lax.broadcasted_iota(jnp.int32, (ATTN_BLOCK, 2 * ATTN_BLOCK), 0)
    kj = lax.broadcasted_iota(jnp.int32, (ATTN_BLOCK, 2 * ATTN_BLOCK), 1)
    dist = qi + ATTN_BLOCK - kj
    valid = (dist >= 0) & (dist < ATTN_BLOCK) & (n * ATTN_BLOCK - ATTN_BLOCK + kj >= 0)
    return (kband.astype(BF16), pltpu.roll(kband, 64, 1).astype(BF16),
            vband.astype(BF16), pltpu.roll(vband, 64, 1).astype(BF16), valid)


def _attn_probs(qm, kk, valid, sink):
    s = lax.dot_general(qm, kk, _NT, preferred_element_type=F32) * ATTN_SCALE
    s = jnp.where(valid, s, NEG_BIG)
    m = jnp.maximum(jnp.max(s, axis=1, keepdims=True), sink)
    p = jnp.exp(s - m)
    esink = jnp.exp(sink - m)
    z = jnp.sum(p, axis=1, keepdims=True) + esink
    return p / z, esink / z


def _attn_specs(nb):
    def cur(col, width=KV_W):
        return pl.BlockSpec((ATTN_BLOCK, width), lambda n: (jnp.minimum(n, nb - 1), col))

    def prev(col):
        return pl.BlockSpec((ATTN_BLOCK, KV_W), lambda n: (jnp.maximum(n - 1, 0), col))

    ua_specs = [cur(0, ATTN_W), cur(4), prev(4), cur(5), prev(5)]
    tab_specs = [cur(0), cur(0), prev(0), prev(0)]
    return ua_specs, tab_specs


def _attn_fwd(ua, ctab, stab, sinks, name):
    t = ua.shape[0]
    nb = t // ATTN_BLOCK
    ua_specs, tab_specs = _attn_specs(nb)

    def body(q_ref, kc_ref, kp_ref, vc_ref, vp_ref, cc_ref, sc_ref, cp_ref, sp_ref, sink_ref, o_ref):
        n = pl.program_id(0)
        cc, sc = cc_ref[...], sc_ref[...]
        kb, kb_r, vb, vb_r, valid = _attn_band(n, kc_ref[...], kp_ref[...], vc_ref[...], vp_ref[...],
                                               cc, sc, cp_ref[...], sp_ref[...])
        qr = _rope(q_ref[...], _tile4(cc), _tile4(sc))
        lo = lax.broadcasted_iota(jnp.int32, (ATTN_BLOCK, LANES), 1) < 64
        outs = []
        for j in range(4):
            kv0 = j < 2
            qj = qr[:, j * LANES:(j + 1) * LANES]
            halves = []
            for is_lo in (True, False):
                aligned = is_lo == kv0
                msk = lo if is_lo else jnp.logical_not(lo)
                qm = jnp.where(msk, qj, 0.0).astype(BF16)
                probs, _ = _attn_probs(qm, kb if aligned else kb_r, valid, sink_ref[0, 2 * j + (0 if is_lo else 1)])
                halves.append(lax.dot_general(probs.astype(BF16), vb if aligned else vb_r, _NN,
                                              preferred_element_type=F32))
            outs.append(jnp.where(lo, halves[0], halves[1]))
        o_ref[...] = jnp.concatenate(outs, axis=1).astype(o_ref.dtype)

    return pl.pallas_call(
        body, name=name, grid=(nb,),
        in_specs=ua_specs + tab_specs + [pl.BlockSpec(memory_space=pltpu.SMEM)],
        out_specs=pl.BlockSpec((ATTN_BLOCK, ATTN_W), lambda n: (n, 0)),
        out_shape=jax.ShapeDtypeStruct((t, ATTN_W), BF16),
        compiler_params=_cp("parallel"),
    )(ua, ua, ua, ua, ua, ctab, stab, ctab, stab, sinks)


def _attn_bwd(ua, d_out, ctab, stab, sinks, name):
    t = ua.shape[0]
    nb = t // ATTN_BLOCK
    ua_specs, tab_specs = _attn_specs(nb)

    def body(q_ref, kc_ref, kp_ref, vc_ref, vp_ref, cc_ref, sc_ref, cp_ref, sp_ref, do_ref, sink_ref,
             dua_ref, dbias_ref, dsink_ref, dq_c, dk_c, dv_c, dq_n, dk_n, dv_n):
        n = pl.program_id(0)

        @pl.when(n == 0)
        def _():
            dq_c[...] = jnp.zeros_like(dq_c)
            dk_c[...] = jnp.zeros_like(dk_c)
            dv_c[...] = jnp.zeros_like(dv_c)
            dbias_ref[...] = jnp.zeros_like(dbias_ref)
            dsink_ref[...] = jnp.zeros_like(dsink_ref)

        @pl.when(n == nb)
        def _():
            dq_n[...] = jnp.zeros_like(dq_n)
            dk_n[...] = jnp.zeros_like(dk_n)
            dv_n[...] = jnp.zeros_like(dv_n)

        @pl.when(n < nb)
        def _():
            cc, sc = cc_ref[...], sc_ref[...]
            kb, kb_r, vb, vb_r, valid = _attn_band(n, kc_ref[...], kp_ref[...], vc_ref[...], vp_ref[...],
                                                   cc, sc, cp_ref[...], sp_ref[...])
            c4, s4 = _tile4(cc), _tile4(sc)
            qr = _rope(q_ref[...], c4, s4)
            do = do_ref[...].astype(F32)
            lane = lax.broadcasted_iota(jnp.int32, (ATTN_BLOCK, LANES), 1)
            lo = lane < 64
            lane_row = lax.broadcasted_iota(jnp.int32, (1, LANES), 1)
            dk_band = jnp.zeros((2 * ATTN_BLOCK, LANES), F32)
            dv_band = jnp.zeros((2 * ATTN_BLOCK, LANES), F32)
            dsink = jnp.zeros((1, LANES), F32)
            dqs = []
            for j in range(4):
                kv0 = j < 2
                qj = qr[:, j * LANES:(j + 1) * LANES]
                doj = do[:, j * LANES:(j + 1) * LANES]
                halves = []
                for is_lo in (True, False):
                    aligned = is_lo == kv0
                    head = 2 * j + (0 if is_lo else 1)
                    msk = lo if is_lo else jnp.logical_not(lo)
                    kk = kb if aligned else kb_r
                    vv = vb if aligned else vb_r
                    qm = jnp.where(msk, qj, 0.0).astype(BF16)
                    dom = jnp.where(msk, doj, 0.0).astype(BF16)
                    probs, psink = _attn_probs(qm, kk, valid, sink_ref[0, head])
                    dp = lax.dot_general(dom, vv, _NT, preferred_element_type=F32)
                    delta = jnp.sum(probs * dp, axis=1, keepdims=True)
                    ds = (probs * (dp - delta) * ATTN_SCALE).astype(BF16)
                    dsink = dsink + jnp.where(lane_row == head, -jnp.sum(psink * delta), 0.0)
                    halves.append(lax.dot_general(ds, kk, _NN, preferred_element_type=F32))
                    dk_h = lax.dot_general(ds, qm, _TN, preferred_element_type=F32)
                    dv_h = lax.dot_general(probs.astype(BF16), dom, _TN, preferred_element_type=F32)
                    if not aligned:
                        dk_h = pltpu.roll(dk_h, 64, 1)
                        dv_h = pltpu.roll(dv_h, 64, 1)
                    dk_band = dk_band + dk_h
                    dv_band = dv_band + dv_h
                dqs.append(jnp.where(lo, halves[0], halves[1]))
            dq_n[...] = _rope_bwd(jnp.concatenate(dqs, axis=1), c4, s4)
            dk_n[...] = dk_band
            dv_n[...] = dv_band
            dsink_ref[...] += dsink

        dk_prev = _rope_bwd(dk_c[...] + dk_n[0:ATTN_BLOCK, :], cp_ref[...], sp_ref[...])
        dv_prev = dv_c[...] + dv_n[0:ATTN_BLOCK, :]
        full = jnp.concatenate([dq_c[...], dk_prev, dv_prev], axis=1)
        dua_ref[...] = full.astype(dua_ref.dtype)
        dbias_ref[...] += jnp.sum(full, axis=0, keepdims=True)
        dq_c[...] = dq_n[...]
        dk_c[...] = dk_n[ATTN_BLOCK:, :]
        dv_c[...] = dv_n[ATTN_BLOCK:, :]

    return pl.pallas_call(
        body, name=name, grid=(nb + 1,),
        in_specs=ua_specs + tab_specs + [
            pl.BlockSpec((ATTN_BLOCK, ATTN_W), lambda n: (jnp.minimum(n, nb - 1), 0)),
            pl.BlockSpec(memory_space=pltpu.SMEM)],
        out_specs=[pl.BlockSpec((ATTN_BLOCK, UA_W), lambda n: (jnp.maximum(n - 1, 0), 0)),
                   pl.BlockSpec((1, UA_W), lambda n: (0, 0)),
                   pl.BlockSpec((1, LANES), lambda n: (0, 0))],
        out_shape=[jax.ShapeDtypeStruct((t, UA_W), BF16),
                   jax.ShapeDtypeStruct((1, UA_W), F32),
                   jax.ShapeDtypeStruct((1, LANES), F32)],
        scratch_shapes=[pltpu.VMEM((ATTN_BLOCK, ATTN_W), F32), pltpu.VMEM((ATTN_BLOCK, KV_W), F32),
                        pltpu.VMEM((ATTN_BLOCK, KV_W), F32), pltpu.VMEM((ATTN_BLOCK, ATTN_W), F32),
                        pltpu.VMEM((2 * ATTN_BLOCK, KV_W), F32), pltpu.VMEM((2 * ATTN_BLOCK, KV_W), F32)],
        compiler_params=_cp("arbitrary"),
    )(ua, ua, ua, ua, ua, ctab, stab, ctab, stab, d_out, sinks)


def _tri_mats():
    r = lax.broadcasted_iota(jnp.int32, (HGRN_CHUNK, LANES), 0)
    c = lax.broadcasted_iota(jnp.int32, (HGRN_CHUNK, LANES), 1)
    lower = ((c <= r) & (c < HGRN_CHUNK)).astype(F32)
    upper = ((c >= r) & (c < HGRN_CHUNK)).astype(F32)
    return lower, upper


def _tri_apply(tri, g):
    pad = jnp.concatenate([g, jnp.zeros_like(g)], axis=0)
    return lax.dot_general(tri, pad, _NN, precision=lax.Precision.HIGHEST, preferred_element_type=F32)


def _sub_masks():
    s = lax.broadcasted_iota(jnp.int32, (HGRN_CHUNK, LANES), 0)
    tt = lax.broadcasted_iota(jnp.int32, (HGRN_CHUNK, LANES), 1)
    return [(tt >= HGRN_SUB * i) & (tt < HGRN_SUB * (i + 1)) & (s <= tt) for i in range(HGRN_CHUNK // HGRN_SUB)]


def _hgrn_gates(hq, hf, lb_ref, b_scr):
    lb = _sig(lb_ref[0:1, :] - lb_ref[1:2, :])
    q = hq * _sig(hq)
    sg = _sig(hf)
    f = lb + (1.0 - lb) * sg
    k = 1.0 - f
    lower, _ = _tri_mats()
    b = _tri_apply(lower, jnp.log(f))
    b_scr[...] = b
    nsub = HGRN_CHUNK // HGRN_SUB
    starts = [jnp.zeros((1, HG_W), F32)] + [b_scr[HGRN_SUB * i - 1:HGRN_SUB * i, :] for i in range(1, nsub)]
    pq = jnp.concatenate([jnp.broadcast_to(p, (HGRN_SUB, HG_W)) for p in starts], axis=0)
    b_last = b_scr[HGRN_CHUNK - 1:HGRN_CHUNK, :]
    e_q = jnp.exp(b - pq)
    e_k = [jnp.exp(jnp.minimum(p - b, EXP_CLAMP)) for p in starts]
    e_b = jnp.exp(b)
    e_bl = jnp.exp(b_last - b)
    e_last = jnp.exp(b_last)
    return q, sg, f, k, lb, e_q, e_k, e_b, e_bl, e_last


def _intra_scores(kst, qt_pad, masks):
    ats = _dot3(kst, qt_pad, _NT)
    at = jnp.zeros((HGRN_CHUNK, LANES), F32)
    for i, msk in enumerate(masks):
        at = at + jnp.where(msk, ats[HGRN_CHUNK * i:HGRN_CHUNK * (i + 1), :], 0.0)
    return at


def _hgrn_fwd(uh, lb_raw, norm_g, name):
    t = uh.shape[0]
    nc = t // HGRN_CHUNK

    def body(hq_ref, hf_ref, hi_ref, hg_ref, lb_ref, ng_ref, r_ref, o_ref, st_out_ref, st_ref, b_scr):
        c = pl.program_id(0)

        @pl.when(c == 0)
        def _():
            st_ref[...] = jnp.zeros_like(st_ref)

        hq, hf, v, hg = hq_ref[...], hf_ref[...], hi_ref[...], hg_ref[...]
        q, _, _, k, _, e_q, e_k, e_b, e_bl, e_last = _hgrn_gates(hq, hf, lb_ref, b_scr)
        qt, qb, kd = q * e_q, q * e_b, k * e_bl
        khat = [k * e for e in e_k]
        masks = _sub_masks()
        ng = ng_ref[...]
        zpad = jnp.zeros((HGRN_CHUNK, LANES), F32)
        o_heads, y_heads = [], []
        for h in range(4):
            sl = slice(h * LANES, (h + 1) * LANES)
            kst = jnp.concatenate([kh[:, sl] for kh in khat], axis=0)
            qt_pad = jnp.concatenate([qt[:, sl], zpad], axis=0)
            at = _intra_scores(kst, qt_pad, masks)
            vh = v[:, sl].astype(BF16)
            o_intra = lax.dot_general(at.astype(BF16), vh, _TN, preferred_element_type=F32)[0:HGRN_CHUNK, :]
            st = st_ref[h]
            st_out_ref[0, h] = st
            o_inter = _dot(qb[:, sl], st, _NT)
            st_ref[h] = st * e_last[:, sl] + _dot(vh, kd[:, sl], _TN)
            oh = o_intra + o_inter
            rs = lax.rsqrt(jnp.mean(oh * oh, axis=1, keepdims=True) + RMS_EPS)
            o_heads.append(oh)
            y_heads.append(oh * rs * ng)
        o_ref[...] = jnp.concatenate(o_heads, axis=1)
        r_ref[...] = (jnp.concatenate(y_heads, axis=1) * (hg * _sig(hg))).astype(r_ref.dtype)

    col = lambda j: pl.BlockSpec((HGRN_CHUNK, HG_W), lambda c: (c, j))
    return pl.pallas_call(
        body, name=name, grid=(nc,),
        in_specs=[col(0), col(1), col(2), col(3),
                  pl.BlockSpec((2, HG_W), lambda c: (0, 0)), pl.BlockSpec((1, LANES), lambda c: (0, 0))],
        out_specs=[pl.BlockSpec((HGRN_CHUNK, HG_W), lambda c: (c, 0)),
                   pl.BlockSpec((HGRN_CHUNK, HG_W), lambda c: (c, 0)),
                   pl.BlockSpec((1, 4, LANES, LANES), lambda c: (c, 0, 0, 0))],
        out_shape=[jax.ShapeDtypeStruct((t, HG_W), BF16), jax.ShapeDtypeStruct((t, HG_W), F32),
                   jax.ShapeDtypeStruct((nc, 4, LANES, LANES), F32)],
        scratch_shapes=[pltpu.VMEM((4, LANES, LANES), F32), pltpu.VMEM((HGRN_CHUNK, HG_W), F32)],
        compiler_params=_cp("arbitrary"),
    )(uh, uh, uh, uh, lb_raw, norm_g)


def _hgrn_bwd(uh, o_pre, d_r, states, lb_raw, norm_g, name):
    t = uh.shape[0]
    nc = t // HGRN_CHUNK

    def body(hq_ref, hf_ref, hi_ref, hg_ref, o_ref, dr_ref, st_in_ref, lb_ref, ng_ref,
             duh_ref, dbias_ref, dng_ref, dlb_ref, dst_ref, b_scr, dlb_acc):
        i = pl.program_id(0)

        @pl.when(i == 0)
        def _():
            dst_ref[...] = jnp.zeros_like(dst_ref)
            dbias_ref[...] = jnp.zeros_like(dbias_ref)
            dng_ref[...] = jnp.zeros_like(dng_ref)
            dlb_acc[...] = jnp.zeros_like(dlb_acc)

        hq, hf, v, hg = hq_ref[...], hf_ref[...], hi_ref[...], hg_ref[...]
        q, sg, f, k, lb, e_q, e_k, e_b, e_bl, e_last = _hgrn_gates(hq, hf, lb_ref, b_scr)
        qt, qb, kd = q * e_q, q * e_b, k * e_bl
        khat = [k * e for e in e_k]
        masks = _sub_masks()
        ng = ng_ref[...]
        o = o_ref[...]
        dr = dr_ref[...].astype(F32)
        sgg = _sig(hg)
        gate = hg * sgg
        dy = dr * gate
        zpad = jnp.zeros((HGRN_CHUNK, LANES), F32)
        nsub = HGRN_CHUNK // HGRN_SUB
        dq_h, dk_h, dv_h, y_h, extra_h = [], [], [], [], []
        dng = jnp.zeros((1, LANES), F32)
        for h in range(4):
            sl = slice(h * LANES, (h + 1) * LANES)
            oh = o[:, sl]
            rs = lax.rsqrt(jnp.mean(oh * oh, axis=1, keepdims=True) + RMS_EPS)
            y_h.append(oh * rs * ng)
            dng = dng + jnp.sum(dy[:, sl] * oh * rs, axis=0, keepdims=True)
            w = dy[:, sl] * ng
            do = rs * (w - oh * (rs * rs) * jnp.mean(w * oh, axis=1, keepdims=True))
            do_b = do.astype(BF16)
            do_pad = jnp.concatenate([do, zpad], axis=0).astype(BF16)
            vh = v[:, sl].astype(BF16)
            kst = jnp.concatenate([kh[:, sl] for kh in khat], axis=0)
            qt_pad = jnp.concatenate([qt[:, sl], zpad], axis=0)
            at = _intra_scores(kst, qt_pad, masks)
            d_at = lax.dot_general(vh, do_pad, _NT, preferred_element_type=F32)
            d_ats = jnp.concatenate([jnp.where(m, d_at, 0.0) for m in masks], axis=0)
            d_kst = _dot3(d_ats, qt_pad, _NN)
            d_qt = _dot3(d_ats, kst, _TN)[0:HGRN_CHUNK, :]
            st_prev = st_in_ref[0, h]
            d_st = dst_ref[h]
            d_st_b = d_st.astype(BF16)
            dv = (lax.dot_general(at.astype(BF16), do_pad, _NN, preferred_element_type=F32)
                  + _dot(kd[:, sl], d_st_b, _NT))
            d_qb = _dot(do_b, st_prev, _NN)
            d_kd = lax.dot_general(vh, d_st_b, _NN, preferred_element_type=F32)
            extra_h.append(jnp.sum(st_prev * d_st, axis=0, keepdims=True) * e_last[:, sl]
                           + jnp.sum(kd[:, sl] * d_kd, axis=0, keepdims=True))
            dst_ref[h] = d_st * e_last[:, sl] + _dot(do_b, qb[:, sl], _TN)
            dq_h.append(d_qt * e_q[:, sl] + d_qb * e_b[:, sl])
            dkk = d_kd * e_bl[:, sl]
            for s in range(nsub):
                dkk = dkk + d_kst[HGRN_CHUNK * s:HGRN_CHUNK * (s + 1), :] * e_k[s][:, sl]
            dk_h.append(dkk)
            dv_h.append(dv)
        dq = jnp.concatenate(dq_h, axis=1)
        dk = jnp.concatenate(dk_h, axis=1)
        dv = jnp.concatenate(dv_h, axis=1)
        y = jnp.concatenate(y_h, axis=1)
        extra = jnp.concatenate(extra_h, axis=1)
        row = lax.broadcasted_iota(jnp.int32, (HGRN_CHUNK, HG_W), 0)
        db = q * dq - k * dk + jnp.where(row == HGRN_CHUNK - 1, extra, 0.0)
        _, upper = _tri_mats()
        dg = _tri_apply(upper, db)
        df = dg / f - dk
        dhf = df * (1.0 - lb) * sg * (1.0 - sg)
        dhq = dq * _dsilu(hq, _sig(hq))
        dhg = dr * y * _dsilu(hg, sgg)
        full = jnp.concatenate([dhq, dhf, dv, dhg], axis=1)
        duh_ref[...] = full.astype(duh_ref.dtype)
        dbias_ref[...] += jnp.sum(full, axis=0, keepdims=True)
        dng_ref[...] += dng
        dlb_acc[...] += jnp.sum(df * (1.0 - sg), axis=0, keepdims=True)

        @pl.when(i == nc - 1)
        def _():
            d_a0 = dlb_acc[...] * lb * (1.0 - lb)
            r8 = lax.broadcasted_iota(jnp.int32, (8, HG_W), 0)
            dlb_ref[...] = jnp.where(r8 == 0, d_a0, jnp.where(r8 == 1, -d_a0, 0.0))

    col = lambda j: pl.BlockSpec((HGRN_CHUNK, HG_W), lambda i: (nc - 1 - i, j))
    return pl.pallas_call(
        body, name=name, grid=(nc,),
        in_specs=[col(0), col(1), col(2), col(3), col(0), col(0),
                  pl.BlockSpec((1, 4, LANES, LANES), lambda i: (nc - 1 - i, 0, 0, 0)),
                  pl.BlockSpec((2, HG_W), lambda i: (0, 0)), pl.BlockSpec((1, LANES), lambda i: (0, 0))],
        out_specs=[pl.BlockSpec((HGRN_CHUNK, UH_W), lambda i: (nc - 1 - i, 0)),
                   pl.BlockSpec((1, UH_W), lambda i: (0, 0)),
                   pl.BlockSpec((1, LANES), lambda i: (0, 0)),
                   pl.BlockSpec((8, HG_W), lambda i: (0, 0))],
        out_shape=[jax.ShapeDtypeStruct((t, UH_W), BF16), jax.ShapeDtypeStruct((1, UH_W), F32),
                   jax.ShapeDtypeStruct((1, LANES), F32), jax.ShapeDtypeStruct((8, HG_W), F32)],
        scratch_shapes=[pltpu.VMEM((4, LANES, LANES), F32), pltpu.VMEM((HGRN_CHUNK, HG_W), F32),
                        pltpu.VMEM((1, HG_W), F32)],
        compiler_params=_cp("arbitrary"),
    )(uh, uh, uh, uh, o_pre, d_r, states, lb_raw, norm_g)


def _ln_fwd(z, g, b, name):
    t, d = z.shape
    tr = _pick(t, (256, 128))

    def body(z_ref, g_ref, b_ref, h_ref, xhat_ref, rstd_ref):
        zz = z_ref[...]
        mu = jnp.mean(zz, axis=1, keepdims=True)
        zc = zz - mu
        rstd = lax.rsqrt(jnp.mean(zc * zc, axis=1, keepdims=True) + LN_EPS)
        xhat = zc * rstd
        xhat_ref[...] = xhat
        rstd_ref[...] = rstd
        h_ref[...] = xhat * g_ref[...] + b_ref[...]

    row = pl.BlockSpec((tr, d), lambda i: (i, 0))
    vec = pl.BlockSpec((1, d), lambda i: (0, 0))
    return pl.pallas_call(
        body, name=name, grid=(t // tr,),
        in_specs=[row, vec, vec],
        out_specs=[row, row, pl.BlockSpec((tr, 1), lambda i: (i, 0))],
        out_shape=[jax.ShapeDtypeStruct((t, d), F32), jax.ShapeDtypeStruct((t, d), F32),
                   jax.ShapeDtypeStruct((t, 1), F32)],
        compiler_params=_cp("parallel"),
    )(z, g, b)


def _ln_bwd_math(dy, xhat, rstd, g):
    dxh = dy * g
    return rstd * (dxh - jnp.mean(dxh, axis=1, keepdims=True)
                   - xhat * jnp.mean(dxh * xhat, axis=1, keepdims=True))


def _ln_bwd(dy, xhat, rstd, g, name):
    t, d = dy.shape
    tr = _pick(t, (256, 128))

    def body(dy_ref, xhat_ref, rstd_ref, g_ref, dz_ref, dg_ref, db_ref):
        @pl.when(pl.program_id(0) == 0)
        def _():
            dg_ref[...] = jnp.zeros_like(dg_ref)
            db_ref[...] = jnp.zeros_like(db_ref)

        dyv, xh = dy_ref[...], xhat_ref[...]
        dz_ref[...] = _ln_bwd_math(dyv, xh, rstd_ref[...], g_ref[...])
        dg_ref[...] += jnp.sum(dyv * xh, axis=0, keepdims=True)
        db_ref[...] += jnp.sum(dyv, axis=0, keepdims=True)

    row = pl.BlockSpec((tr, d), lambda i: (i, 0))
    vec = pl.BlockSpec((1, d), lambda i: (0, 0))
    return pl.pallas_call(
        body, name=name, grid=(t // tr,),
        in_specs=[row, row, pl.BlockSpec((tr, 1), lambda i: (i, 0)), vec],
        out_specs=[row, vec, vec],
        out_shape=[jax.ShapeDtypeStruct((t, d), F32), jax.ShapeDtypeStruct((1, d), F32),
                   jax.ShapeDtypeStruct((1, d), F32)],
        compiler_params=_cp("arbitrary"),
    )(dy, xhat, rstd, g)


def _ln_loss_bwd(z, target, g, b, name):
    t, d = z.shape
    tr = _pick(t, (256, 128))

    def body(z_ref, tgt_ref, g_ref, b_ref, dz_ref, dg_ref, db_ref, loss_ref):
        @pl.when(pl.program_id(0) == 0)
        def _():
            dg_ref[...] = jnp.zeros_like(dg_ref)
            db_ref[...] = jnp.zeros_like(db_ref)
            loss_ref[...] = jnp.zeros_like(loss_ref)

        zz = z_ref[...]
        gg = g_ref[...]
        mu = jnp.mean(zz, axis=1, keepdims=True)
        zc = zz - mu
        rstd = lax.rsqrt(jnp.mean(zc * zc, axis=1, keepdims=True) + LN_EPS)
        xhat = zc * rstd
        err = xhat * gg + b_ref[...] - tgt_ref[...]
        loss_ref[...] += 0.5 * jnp.sum(jnp.mean(err * err, axis=1, keepdims=True))
        dy = err * (1.0 / d)
        dz_ref[...] = _ln_bwd_math(dy, xhat, rstd, gg)
        dg_ref[...] += jnp.sum(dy * xhat, axis=0, keepdims=True)
        db_ref[...] += jnp.sum(dy, axis=0, keepdims=True)

    row = pl.BlockSpec((tr, d), lambda i: (i, 0))
    vec = pl.BlockSpec((1, d), lambda i: (0, 0))
    return pl.pallas_call(
        body, name=name, grid=(t // tr,),
        in_specs=[row, row, vec, vec],
        out_specs=[row, vec, vec, pl.BlockSpec((1, LANES), lambda i: (0, 0))],
        out_shape=[jax.ShapeDtypeStruct((t, d), F32), jax.ShapeDtypeStruct((1, d), F32),
                   jax.ShapeDtypeStruct((1, d), F32), jax.ShapeDtypeStruct((1, LANES), F32)],
        compiler_params=_cp("arbitrary"),
    )(z, target, g, b)


CONV_TILE = 128
HALO = 8


def _conv_fwd(u2, conv_w, conv_b, name):
    t = u2.shape[0]
    tr = _pick(t, (CONV_TILE,))
    hb = tr // HALO

    def body(gp_ref, val_ref, prev_ref, w_ref, b_ref, out_ref, ext):
        i = pl.program_id(0)
        ext[0:HALO, :] = jnp.where(i == 0, 0.0, prev_ref[...])
        ext[HALO:, :] = gp_ref[...]
        gate = (ext[HALO - 2:HALO - 2 + tr, :] * w_ref[0:1, :] + ext[HALO - 1:HALO - 1 + tr, :] * w_ref[1:2, :]
                + ext[HALO:, :] * w_ref[2:3, :] + b_ref[...])
        out_ref[...] = (gate * _sig(gate) * val_ref[...]).astype(out_ref.dtype)

    return pl.pallas_call(
        body, name=name, grid=(t // tr,),
        in_specs=[pl.BlockSpec((tr, D_FF), lambda i: (i, 0)), pl.BlockSpec((tr, D_FF), lambda i: (i, 1)),
                  pl.BlockSpec((HALO, D_FF), lambda i: (jnp.maximum(i * hb - 1, 0), 0)),
                  pl.BlockSpec((3, D_FF), lambda i: (0, 0)), pl.BlockSpec((1, D_FF), lambda i: (0, 0))],
        out_specs=pl.BlockSpec((tr, D_FF), lambda i: (i, 0)),
        out_shape=jax.ShapeDtypeStruct((t, D_FF), BF16),
        scratch_shapes=[pltpu.VMEM((tr + HALO, D_FF), F32)],
        compiler_params=_cp("parallel"),
    )(u2, u2, u2, conv_w, conv_b)


def _conv_bwd(d_hmid, u2, conv_w, conv_b, name):
    t = u2.shape[0]
    tr = _pick(t, (CONV_TILE,))
    hb = tr // HALO
    last = t // HALO - 1

    def body(gp_ref, gp_prev_ref, gp_next_ref, val_ref, val_next_ref, dh_ref, dh_next_ref, w_ref, b_ref,
             du_ref, dw_ref, dcb_ref, ext, dgate_s):
        i = pl.program_id(0)

        @pl.when(i == 0)
        def _():
            dw_ref[...] = jnp.zeros_like(dw_ref)
            dcb_ref[...] = jnp.zeros_like(dcb_ref)

        ext[0:HALO, :] = jnp.where(i == 0, 0.0, gp_prev_ref[...])
        ext[HALO:HALO + tr, :] = gp_ref[...]
        ext[HALO + tr:, :] = gp_next_ref[...]
        w0, w1, w2 = w_ref[0:1, :], w_ref[1:2, :], w_ref[2:3, :]
        n_ext = tr + HALO
        gate = (ext[HALO - 2:HALO - 2 + n_ext, :] * w0 + ext[HALO - 1:HALO - 1 + n_ext, :] * w1
                + ext[HALO:HALO + n_ext, :] * w2 + b_ref[...])
        sg = _sig(gate)
        val = jnp.concatenate([val_ref[...], val_next_ref[...]], axis=0)
        dh = jnp.concatenate([dh_ref[...], dh_next_ref[...]], axis=0).astype(F32)
        row = lax.broadcasted_iota(jnp.int32, (n_ext, 1), 0)
        in_seq = (i * tr + row) < t
        dgate = jnp.where(in_seq, dh * val * _dsilu(gate, sg), 0.0)
        dgate_s[...] = dgate
        dg0 = dgate[0:tr, :]
        d_gp = dgate_s[2:2 + tr, :] * w0 + dgate_s[1:1 + tr, :] * w1 + dg0 * w2
        du_ref[:, 0:D_FF] = d_gp.astype(du_ref.dtype)
        du_ref[:, D_FF:] = (dh[0:tr, :] * (gate[0:tr, :] * sg[0:tr, :])).astype(du_ref.dtype)
        dcb_ref[...] += jnp.sum(dg0, axis=0, keepdims=True)
        dw_ref[0:1, :] += jnp.sum(dg0 * ext[HALO - 2:HALO - 2 + tr, :], axis=0, keepdims=True)
        dw_ref[1:2, :] += jnp.sum(dg0 * ext[HALO - 1:HALO - 1 + tr, :], axis=0, keepdims=True)
        dw_ref[2:3, :] += jnp.sum(dg0 * ext[HALO:HALO + tr, :], axis=0, keepdims=True)

    cur = lambda col: pl.BlockSpec((tr, D_FF), lambda i: (i, col))
    nxt = lambda col: pl.BlockSpec((HALO, D_FF), lambda i: (jnp.minimum((i + 1) * hb, last), col))
    return pl.pallas_call(
        body, name=name, grid=(t // tr,),
        in_specs=[cur(0), pl.BlockSpec((HALO, D_FF), lambda i: (jnp.maximum(i * hb - 1, 0), 0)), nxt(0),
                  cur(1), nxt(1), cur(0), nxt(0),
                  pl.BlockSpec((3, D_FF), lambda i: (0, 0)), pl.BlockSpec((1, D_FF), lambda i: (0, 0))],
        out_specs=[pl.BlockSpec((tr, 2 * D_FF), lambda i: (i, 0)),
                   pl.BlockSpec((8, D_FF), lambda i: (0, 0)), pl.BlockSpec((1, D_FF), lambda i: (0, 0))],
        out_shape=[jax.ShapeDtypeStruct((t, 2 * D_FF), BF16), jax.ShapeDtypeStruct((8, D_FF), F32),
                   jax.ShapeDtypeStruct((1, D_FF), F32)],
        scratch_shapes=[pltpu.VMEM((tr + 2 * HALO, D_FF), F32), pltpu.VMEM((tr + HALO, D_FF), F32)],
        compiler_params=_cp("arbitrary"),
    )(u2, u2, u2, u2, u2, d_hmid, d_hmid, conv_w, conv_b)


def _adamw(w, g, m, v, name):
    rows, cols = w.shape
    tr = _pick(rows, (256, 128, 64, 32, 16, 8))

    def body(w_ref, g_ref, m_ref, v_ref, d_ref, nm_ref, nv_ref):
        gg = g_ref[...]
        nm = ADAM_B1 * m_ref[...] + (1.0 - ADAM_B1) * gg
        nv = ADAM_B2 * v_ref[...] + (1.0 - ADAM_B2) * (gg * gg)
        m_hat = nm / (1.0 - ADAM_B1 ** ADAM_STEP)
        v_hat = nv / (1.0 - ADAM_B2 ** ADAM_STEP)
        d_ref[...] = -ADAM_LR * (m_hat / (jnp.sqrt(v_hat) + ADAM_EPS) + ADAM_WD * w_ref[...])
        nm_ref[...] = nm
        nv_ref[...] = nv

    spec = pl.BlockSpec((tr, cols), lambda i: (i, 0))
    shp = jax.ShapeDtypeStruct((rows, cols), F32)
    return pl.pallas_call(
        body, name=name, grid=(rows // tr,),
        in_specs=[spec, spec, spec, spec], out_specs=[spec, spec, spec], out_shape=[shp, shp, shp],
        compiler_params=_cp("parallel"),
    )(w, g, m, v)


def _rows128(a):
    return a.reshape(-1, LANES)


def _pad_rows(a, rows):
    return jnp.pad(a, ((0, rows - a.shape[0]), (0, 0)))


def _cols_to_shards(w, width):
    r = w.shape[0]
    return w.reshape(r, N_DEV, width).transpose(1, 0, 2)


def _shards_to_cols(s):
    n, r, width = s.shape
    return s.transpose(1, 0, 2).reshape(r, n * width)


SMALL_LAYOUT = (("ln1_g", 1024), ("ln1_b", 1024), ("b_in", 2816), ("sinks", 8), ("hgrn_lb", 1024),
                ("hgrn_norm_g", 128), ("ln2_g", 1024), ("ln2_b", 1024), ("conv_b", 2816), ("loss", 1))


def _pack_small(parts):
    rows = []
    for name, size in SMALL_LAYOUT:
        flat = parts[name].reshape(-1).astype(F32)
        padded = -(-size // LANES) * LANES
        rows.append(jnp.pad(flat, (0, padded - size)).reshape(-1, LANES))
    return _pad_rows(jnp.concatenate(rows, axis=0), SMALL_ROWS)


def _unpack_small(pack, shapes):
    out, r = {}, 0
    for name, size in SMALL_LAYOUT:
        nrows = -(-size // LANES)
        out[name] = pack[r:r + nrows].reshape(-1)[:size].reshape(shapes[name])
        r += nrows
    return out


def kernel(x, positions, ln1_g, ln1_b, w_in, b_in, sinks, hgrn_lb, hgrn_norm_g, w_o, ln2_g, ln2_b, w_up, conv_w, conv_b, w_down, loss_target, m_ln1_g, m_ln1_b, m_w_in, m_b_in, m_sinks, m_hgrn_lb, m_hgrn_norm_g, m_w_o, m_ln2_g, m_ln2_b, m_w_up, m_conv_w, m_conv_b, m_w_down, v_ln1_g, v_ln1_b, v_w_in, v_b_in, v_sinks, v_hgrn_lb, v_hgrn_norm_g, v_w_o, v_ln2_g, v_ln2_b, v_w_up, v_conv_w, v_conv_b, v_w_down):
    t = x.shape[1]
    x2 = x[0]
    target = loss_target[0]
    pos_col = positions.reshape(t, 1)

    wpack = jnp.concatenate([_rows128(w_in[0].astype(BF16)), _rows128(w_o[0].astype(BF16)),
                             _rows128(w_up[0].astype(BF16)), _rows128(w_down[0].astype(BF16))], axis=0)
    gathered = _all_gather(wpack, "ag_weights", pl.ANY)
    w_in_f = _shards_to_cols(gathered[:, 0:ROWS_IN].reshape(N_DEV, D_MODEL, SHARD_IN))
    w_a, w_h = w_in_f[:, :UA_W], w_in_f[:, UA_W:]
    r0 = ROWS_IN
    w_o_f = gathered[:, r0:r0 + ROWS_O].reshape(D_MODEL, D_MODEL)
    r0 += ROWS_O
    w_up_f = _shards_to_cols(gathered[:, r0:r0 + ROWS_UP].reshape(N_DEV, D_MODEL, SHARD_UP))
    r0 += ROWS_UP
    w_down_f = gathered[:, r0:r0 + ROWS_DOWN].reshape(D_FF, D_MODEL)
    cw_pack = _pad_rows(_rows128(jnp.pad(conv_w[0].reshape(-1), (0, ROWS_CONVW * LANES - CONVW_ELEMS))), ROWS_CONVW)
    cw_g = _all_gather(cw_pack, "ag_conv_w", pltpu.VMEM)
    conv_w_f = _shards_to_cols(cw_g.reshape(N_DEV, -1)[:, :CONVW_ELEMS].reshape(N_DEV, 3, SHARD_IN))

    ua = _mm(x2, w_a, bias=b_in[:, :UA_W], name="fwd_in_attn")
    uh = _mm(x2, w_h, bias=b_in[:, UA_W:], name="fwd_in_hgrn")
    ctab, stab = _rope_tables(pos_col, "rope_tables")
    a_out = _attn_fwd(ua, ctab, stab, sinks, "attn_fwd")
    r_out, o_pre, states = _hgrn_fwd(uh, hgrn_lb, hgrn_norm_g, "hgrn_fwd")
    z1 = _mm(a_out, w_o_f[:ATTN_W], addend=x2, addend_scale=ALPHA, name="fwd_o_attn")
    z1 = _mm(r_out, w_o_f[ATTN_W:], addend=z1, name="fwd_o_hgrn")
    h1, xhat1, rstd1 = _ln_fwd(z1, ln1_g, ln1_b, "ln1_fwd")
    u2 = _mm(h1, w_up_f, tn=1408, name="fwd_up")
    hmid = _conv_fwd(u2, conv_w_f, conv_b, "conv_fwd")
    z2 = _mm(hmid, w_down_f, addend=h1, addend_scale=ALPHA, tk=1408, name="fwd_down")
    dz2, d_ln2_g, d_ln2_b, loss_part = _ln_loss_bwd(z2, target, ln2_g, ln2_b, "ln2_loss_bwd")

    d_hmid = _mm(dz2, w_down_f, tb=True, tn=1408, name="bwd_down_dx")
    d_w_down = _mm(hmid, dz2, ta=True, tm=1408, name="bwd_down_dw")
    d_u2, d_conv_w8, d_conv_b = _conv_bwd(d_hmid, u2, conv_w_f, conv_b, "conv_bwd")
    d_h1 = _mm(d_u2, w_up_f, tb=True, addend=dz2, addend_scale=ALPHA, tk=1408, name="bwd_up_dx")
    d_w_up = _mm(h1, d_u2, ta=True, tn=1408, name="bwd_up_dw")
    dz1, d_ln1_g, d_ln1_b = _ln_bwd(d_h1, xhat1, rstd1, ln1_g, "ln1_bwd")
    d_a = _mm(dz1, w_o_f[:ATTN_W], tb=True, name="bwd_o_dx_attn")
    d_r = _mm(dz1, w_o_f[ATTN_W:], tb=True, name="bwd_o_dx_hgrn")
    d_w_o = jnp.concatenate([_mm(a_out, dz1, ta=True, name="bwd_o_dw_attn"),
                             _mm(r_out, dz1, ta=True, name="bwd_o_dw_hgrn")], axis=0)
    d_uh, d_bias_h, d_norm_g, d_lb8 = _hgrn_bwd(uh, o_pre, d_r, states, hgrn_lb, hgrn_norm_g, "hgrn_bwd")
    d_ua, d_bias_a, d_sinks = _attn_bwd(ua, d_a, ctab, stab, sinks, "attn_bwd")
    dx = _mm(d_ua, w_a, tb=True, addend=dz1, addend_scale=ALPHA, tk=768, name="bwd_in_dx_attn")
    dx = _mm(d_uh, w_h, tb=True, addend=dx, name="bwd_in_dx_hgrn")
    d_w_in = jnp.concatenate([_mm(x2, d_ua, ta=True, tn=768, name="bwd_in_dw_attn"),
                              _mm(x2, d_uh, ta=True, name="bwd_in_dw_hgrn")], axis=1)

    gparts = [_cols_to_shards(d_w_in, SHARD_IN).reshape(N_DEV, ROWS_IN, LANES),
              d_w_o.reshape(N_DEV, ROWS_O, LANES),
              _cols_to_shards(d_w_up, SHARD_UP).reshape(N_DEV, ROWS_UP, LANES),
              d_w_down.reshape(N_DEV, ROWS_DOWN, LANES),
              jnp.pad(_cols_to_shards(d_conv_w8[0:3], SHARD_IN).reshape(N_DEV, CONVW_ELEMS),
                      ((0, 0), (0, (ROWS_G - ROWS_W) * LANES - CONVW_ELEMS))).reshape(N_DEV, ROWS_G - ROWS_W, LANES)]
    gpack = jnp.concatenate(gparts, axis=1)
    recv = _exchange(gpack.astype(BF16), "rs_weight_grads")
    own = lax.dynamic_index_in_dim(gpack, _me(), axis=0, keepdims=False)
    gsum = _sum_slots_mine_f32(recv, own, "rs_sum")
    g_w_in = gsum[0:ROWS_IN].reshape(D_MODEL, SHARD_IN)
    r0 = ROWS_IN
    g_w_o = gsum[r0:r0 + ROWS_O].reshape(SHARD_O, D_MODEL)
    r0 += ROWS_O
    g_w_up = gsum[r0:r0 + ROWS_UP].reshape(D_MODEL, SHARD_UP)
    r0 += ROWS_UP
    g_w_down = gsum[r0:r0 + ROWS_DOWN].reshape(SHARD_DOWN, D_MODEL)
    g_conv_w = gsum[ROWS_W:].reshape(-1)[:CONVW_ELEMS].reshape(3, SHARD_IN)

    small_shapes = {"ln1_g": (1, 1024), "ln1_b": (1, 1024), "b_in": (1, 2816), "sinks": (1, 8),
                    "hgrn_lb": (2, 512), "hgrn_norm_g": (1, 128), "ln2_g": (1, 1024), "ln2_b": (1, 1024),
                    "conv_b": (1, 2816), "loss": (1,)}
    small_local = _pack_small({
        "ln1_g": d_ln1_g, "ln1_b": d_ln1_b, "b_in": jnp.concatenate([d_bias_a, d_bias_h], axis=1),
        "sinks": d_sinks[:, :8], "hgrn_lb": d_lb8[0:2], "hgrn_norm_g": d_norm_g, "ln2_g": d_ln2_g,
        "ln2_b": d_ln2_b, "conv_b": d_conv_b, "loss": loss_part[:, :1]})
    small_sum = _sum_slots(_all_gather(small_local, "ar_small", pltpu.VMEM), "ar_small_sum")
    gs = _unpack_small(small_sum, small_shapes)
    loss = gs["loss"][0]

    zero1 = jnp.zeros((1,), F32)
    w_small = _pack_small({"ln1_g": ln1_g, "ln1_b": ln1_b, "b_in": b_in, "sinks": sinks, "hgrn_lb": hgrn_lb,
                           "hgrn_norm_g": hgrn_norm_g, "ln2_g": ln2_g, "ln2_b": ln2_b, "conv_b": conv_b,
                           "loss": zero1})
    m_small = _pack_small({"ln1_g": m_ln1_g, "ln1_b": m_ln1_b, "b_in": m_b_in, "sinks": m_sinks,
                           "hgrn_lb": m_hgrn_lb, "hgrn_norm_g": m_hgrn_norm_g, "ln2_g": m_ln2_g,
                           "ln2_b": m_ln2_b, "conv_b": m_conv_b, "loss": zero1})
    v_small = _pack_small({"ln1_g": v_ln1_g, "ln1_b": v_ln1_b, "b_in": v_b_in, "sinks": v_sinks,
                           "hgrn_lb": v_hgrn_lb, "hgrn_norm_g": v_hgrn_norm_g, "ln2_g": v_ln2_g,
                           "ln2_b": v_ln2_b, "conv_b": v_conv_b, "loss": zero1})
    d_small, nm_small, nv_small = _adamw(w_small, small_sum, m_small, v_small, "adamw_small")
    ds = _unpack_small(d_small, small_shapes)
    nms = _unpack_small(nm_small, small_shapes)
    nvs = _unpack_small(nv_small, small_shapes)

    big = {}
    for nm_, w_, g_, m_, v_ in (("w_in", w_in, g_w_in, m_w_in, v_w_in), ("w_o", w_o, g_w_o, m_w_o, v_w_o),
                                ("w_up", w_up, g_w_up, m_w_up, v_w_up),
                                ("w_down", w_down, g_w_down, m_w_down, v_w_down)):
        d_, nmm, nvv = _adamw(w_[0], g_, m_[0], v_[0], "adamw_" + nm_)
        big[nm_] = (g_[None], d_[None], nmm[None], nvv[None])
    cw8 = lambda a: _pad_rows(a, 8)
    d_cw, nm_cw, nv_cw = _adamw(cw8(conv_w[0]), cw8(g_conv_w), cw8(m_conv_w[0]), cw8(v_conv_w[0]), "adamw_conv_w")
    big["conv_w"] = (g_conv_w[None], d_cw[None, 0:3], nm_cw[None, 0:3], nv_cw[None, 0:3])

    order = ["ln1_g", "ln1_b", "w_in", "b_in", "sinks", "hgrn_lb", "hgrn_norm_g", "w_o", "ln2_g", "ln2_b",
             "w_up", "conv_w", "conv_b", "w_down"]

    def pick(idx, small):
        return [big[n][idx] if n in big else small[n] for n in order]

    return (loss, dx[None], *pick(0, gs), *pick(1, ds), *pick(2, nms), *pick(3, nvs))
```

```python
import functools

import jax
import jax.numpy as jnp
import numpy as np
from jax import lax
from jax.experimental import pallas as pl
from jax.experimental.pallas import tpu as pltpu

F32 = jnp.float32
BF16 = jnp.bfloat16

N_DEV = 8
D_MODEL = 1024
D_FF = 2816
ATTN_W = 512
KV_W = 128
UA_W = ATTN_W + 2 * KV_W
UH_W = 2048
HG_W = 512
ATTN_BLOCK = 128
HGRN_CHUNK = 64
HGRN_SUB = 16
EXP_CLAMP = 85.0
NEG_BIG = -1e30
LN_EPS = 1e-5
RMS_EPS = 1e-6
ALPHA = 2.0 ** 0.25
ATTN_SCALE = 0.125
ROPE_THETA = 500000.0

ADAM_LR = 0.001
ADAM_B1 = 0.9
ADAM_B2 = 0.999
ADAM_EPS = 1e-08
ADAM_WD = 0.01
ADAM_STEP = 10

LANES = 128
VMEM_LIMIT_BYTES = 56 * 1024 * 1024

SHARD_IN = D_FF // N_DEV
SHARD_UP = 2 * D_FF // N_DEV
SHARD_O = D_MODEL // N_DEV
SHARD_DOWN = D_FF // N_DEV
SMALL_ROWS = 88

_MESH = pl.DeviceIdType.MESH
_NT = (((1,), (1,)), ((), ()))
_NN = (((1,), (0,)), ((), ()))
_TN = (((0,), (0,)), ((), ()))


def _cp(*sem):
    if sem:
        return pltpu.CompilerParams(dimension_semantics=sem, vmem_limit_bytes=VMEM_LIMIT_BYTES)
    return pltpu.CompilerParams(vmem_limit_bytes=VMEM_LIMIT_BYTES)


def _sig(x):
    return 1.0 / (1.0 + jnp.exp(-x))


def _dsilu(x, s):
    return s * (1.0 + x * (1.0 - s))


def _dot(a, b, dims):
    return lax.dot_general(a.astype(BF16), b.astype(BF16), dims, preferred_element_type=F32)


def _split(a):
    hi = a.astype(BF16)
    return hi, (a - hi.astype(F32)).astype(BF16)


def _dot3(a, b, dims):
    ah, al = _split(a)
    bh, bl = _split(b)
    d = functools.partial(lax.dot_general, dimension_numbers=dims, preferred_element_type=F32)
    return d(ah, bh) + (d(ah, bl) + d(al, bh))


def _pick(n, pref):
    for t in pref:
        if t <= n and n % t == 0:
            return t
    return n


def _my_coords():
    return lax.axis_index("x"), lax.axis_index("y"), lax.axis_index("c")


def _peer(k):
    x, y, c = _my_coords()
    return (1 - x if k & 4 else x, 1 - y if k & 2 else y, 1 - c if k & 1 else c)


def _me():
    x, y, c = _my_coords()
    return 4 * x + 2 * y + c


class _Comm:
    def __init__(self, kind, arrays):
        self.kind, self.arrays, self.n = kind, list(arrays), len(arrays)

    def out_shapes(self):
        if self.kind == "gather":
            return [jax.ShapeDtypeStruct((N_DEV,) + a.shape, a.dtype) for a in self.arrays]
        return [jax.ShapeDtypeStruct(a.shape, a.dtype) for a in self.arrays]

    def specs(self):
        return [pl.BlockSpec(memory_space=pl.ANY)] * self.n

    def scratch(self):
        return [pltpu.SemaphoreType.DMA(((N_DEV - 1) * self.n,)), pltpu.SemaphoreType.DMA(((N_DEV - 1) * self.n,)),
                pltpu.SemaphoreType.DMA((self.n,))]

    def _src(self, ref, dev):
        return ref if self.kind == "gather" else ref.at[dev]

    def _copies(self, srcs, dsts, sems):
        send_sems, recv_sems, local_sems = sems
        me = _me()
        out = []
        for a, (src, dst) in enumerate(zip(srcs, dsts)):
            local = pltpu.make_async_copy(self._src(src, me), dst.at[me], local_sems.at[a])
            sends, recvs = [], []
            for k in range(1, N_DEV):
                other = jnp.bitwise_xor(me, k)
                idx = a * (N_DEV - 1) + k - 1
                sends.append(pltpu.make_async_remote_copy(
                    src_ref=self._src(src, other), dst_ref=dst.at[me], send_sem=send_sems.at[idx],
                    recv_sem=recv_sems.at[idx], device_id=_peer(k), device_id_type=_MESH))
                recvs.append(pltpu.make_async_remote_copy(
                    src_ref=self._src(src, other), dst_ref=dst.at[other], send_sem=send_sems.at[idx],
                    recv_sem=recv_sems.at[idx], device_id=_peer(k), device_id_type=_MESH))
            out.append((local, sends, recvs))
        return out

    def start(self, srcs, dsts, sems):
        for local, sends, _ in self._copies(srcs, dsts, sems):
            local.start()
            for cp in sends:
                cp.start()

    def wait(self, srcs, dsts, sems):
        for local, sends, recvs in self._copies(srcs, dsts, sems):
            for cp in recvs:
                cp.wait_recv()
            for cp in sends:
                cp.wait_send()
            local.wait()


def _call(body, *, name, grid, ins, in_specs, out_specs, out_shape, scratch_shapes=(), sem, comm=None):
    n_in, n_out, n_scr = len(ins), len(out_shape), len(scratch_shapes)
    if comm is None:
        outs = pl.pallas_call(
            body, name=name, grid=grid, in_specs=list(in_specs), out_specs=list(out_specs),
            out_shape=list(out_shape), scratch_shapes=list(scratch_shapes), compiler_params=_cp(*sem))(*ins)
        return list(outs), []
    nc = comm.n

    def hosted(*refs):
        pos = n_in
        c_in = refs[pos:pos + nc]
        pos += nc
        outs = refs[pos:pos + n_out]
        pos += n_out
        c_out = refs[pos:pos + nc]
        pos += nc
        scr = refs[pos:pos + n_scr]
        sems = refs[pos + n_scr:]
        ids = [pl.program_id(d) for d in range(len(grid))]
        first = functools.reduce(jnp.logical_and, [i == 0 for i in ids])
        last = functools.reduce(jnp.logical_and, [i == g - 1 for i, g in zip(ids, grid)])

        @pl.when(first)
        def _():
            comm.start(c_in, c_out, sems)

        body(*refs[:n_in], *outs, *scr)

        @pl.when(last)
        def _():
            comm.wait(c_in, c_out, sems)

    outs = pl.pallas_call(
        hosted, name=name, grid=grid, in_specs=list(in_specs) + comm.specs(),
        out_specs=list(out_specs) + comm.specs(), out_shape=list(out_shape) + comm.out_shapes(),
        scratch_shapes=list(scratch_shapes) + comm.scratch(),
        compiler_params=_cp(*(["arbitrary"] * len(grid))))(*ins, *comm.arrays)
    return list(outs[:n_out]), list(outs[n_out:])


def _comm_only(comm, name):
    def body(*refs):
        srcs, dsts, sems = refs[:comm.n], refs[comm.n:2 * comm.n], refs[2 * comm.n:]
        comm.start(srcs, dsts, sems)
        comm.wait(srcs, dsts, sems)

    return list(pl.pallas_call(
        body, name=name, in_specs=comm.specs(), out_specs=comm.specs(), out_shape=comm.out_shapes(),
        scratch_shapes=comm.scratch(), compiler_params=_cp())(*comm.arrays))


def _all_gather_vmem(x, name):
    def body(x_ref, out_ref, send_sems, recv_sems, local_sem):
        me = _me()
        local = pltpu.make_async_copy(x_ref, out_ref.at[me], local_sem)
        local.start()
        sends = []
        for k in range(1, N_DEV):
            cp = pltpu.make_async_remote_copy(
                src_ref=x_ref, dst_ref=out_ref.at[me], send_sem=send_sems.at[k - 1],
                recv_sem=recv_sems.at[k - 1], device_id=_peer(k), device_id_type=_MESH)
            cp.start()
            sends.append(cp)
        for k in range(1, N_DEV):
            pltpu.make_async_remote_copy(
                src_ref=x_ref, dst_ref=out_ref.at[jnp.bitwise_xor(me, k)], send_sem=send_sems.at[k - 1],
                recv_sem=recv_sems.at[k - 1], device_id=_peer(k), device_id_type=_MESH).wait_recv()
        for cp in sends:
            cp.wait_send()
        local.wait()

    return pl.pallas_call(
        body, name=name,
        out_shape=jax.ShapeDtypeStruct((N_DEV,) + x.shape, x.dtype),
        in_specs=[pl.BlockSpec(memory_space=pltpu.VMEM)],
        out_specs=pl.BlockSpec(memory_space=pltpu.VMEM),
        scratch_shapes=[pltpu.SemaphoreType.DMA((N_DEV - 1,)), pltpu.SemaphoreType.DMA((N_DEV - 1,)),
                        pltpu.SemaphoreType.DMA],
        compiler_params=_cp(),
    )(x)


def _sum_slots(gathered, name):
    _, rows, cols = gathered.shape

    def body(g_ref, out_ref):
        acc = g_ref[0]
        for s in range(1, N_DEV):
            acc = acc + g_ref[s]
        out_ref[...] = acc

    return pl.pallas_call(
        body, name=name,
        out_shape=jax.ShapeDtypeStruct((rows, cols), F32),
        compiler_params=_cp(),
    )(gathered)


def _slot_sum(recv_ref, own_ref, shape):
    me = _me()
    acc = jnp.zeros(shape, F32)
    for s in range(N_DEV):
        acc = acc + jnp.where(me == s, own_ref[...], recv_ref[s].astype(F32))
    return acc


def _adamw_math(w, g, m, v):
    nm = ADAM_B1 * m + (1.0 - ADAM_B1) * g
    nv = ADAM_B2 * v + (1.0 - ADAM_B2) * (g * g)
    m_hat = nm / (1.0 - ADAM_B1 ** ADAM_STEP)
    v_hat = nv / (1.0 - ADAM_B2 ** ADAM_STEP)
    return -ADAM_LR * (m_hat / (jnp.sqrt(v_hat) + ADAM_EPS) + ADAM_WD * w), nm, nv


def _sum_shards(recv, own, name):
    _, rows, cols = recv.shape
    tr = _pick(rows, (256, 128, 176, 64, 32, 16, 8))

    def body(recv_ref, own_ref, out_ref):
        out_ref[...] = _slot_sum(recv_ref, own_ref, (tr, cols))

    return pl.pallas_call(
        body, name=name, grid=(rows // tr,),
        in_specs=[pl.BlockSpec((N_DEV, tr, cols), lambda i: (0, i, 0)), pl.BlockSpec((tr, cols), lambda i: (i, 0))],
        out_specs=pl.BlockSpec((tr, cols), lambda i: (i, 0)),
        out_shape=jax.ShapeDtypeStruct((rows, cols), F32),
        compiler_params=_cp("parallel"),
    )(recv, own)


def _sum_shards_adamw(recv, own, w, m, v, name):
    _, rows, cols = recv.shape
    tr = _pick(rows, (128, 176, 64, 32, 16, 8))

    def body(recv_ref, own_ref, w_ref, m_ref, v_ref, g_ref, d_ref, nm_ref, nv_ref):
        g = _slot_sum(recv_ref, own_ref, (tr, cols))
        g_ref[...] = g
        d_ref[...], nm_ref[...], nv_ref[...] = _adamw_math(w_ref[...], g, m_ref[...], v_ref[...])

    spec = pl.BlockSpec((tr, cols), lambda i: (i, 0))
    shp = jax.ShapeDtypeStruct((rows, cols), F32)
    return pl.pallas_call(
        body, name=name, grid=(rows // tr,),
        in_specs=[pl.BlockSpec((N_DEV, tr, cols), lambda i: (0, i, 0)), spec, spec, spec, spec],
        out_specs=[spec, spec, spec, spec], out_shape=[shp, shp, shp, shp],
        compiler_params=_cp("parallel"),
    )(recv, own, w, m, v)


def _mm(a, b, *, name, ta=False, tb=False, out_dtype=F32, out_dtype2=None, bias=None, addend=None,
        addend_scale=1.0, tm=1024, tn=1024, tk=1024, comm=None):
    kdim, m = a.shape if ta else a.shape[::-1]
    n = b.shape[0] if tb else b.shape[1]
    tm = _pick(m, (tm, 1408, 1024, 768, 512, 256, 128))
    tn = _pick(n, (tn, 1408, 1024, 768, 512, 256, 128))
    tk = _pick(kdim, (tk, 1408, 1024, 768, 512, 256, 128))
    nk = kdim // tk
    a_spec = pl.BlockSpec((tk, tm), lambda i, j, k: (k, i)) if ta else pl.BlockSpec((tm, tk), lambda i, j, k: (i, k))
    b_spec = pl.BlockSpec((tn, tk), lambda i, j, k: (j, k)) if tb else pl.BlockSpec((tk, tn), lambda i, j, k: (k, j))
    ins, specs = [a, b], [a_spec, b_spec]
    if bias is not None:
        ins.append(bias)
        specs.append(pl.BlockSpec((1, tn), lambda i, j, k: (0, j)))
    if addend is not None:
        ins.append(addend)
        specs.append(pl.BlockSpec((tm, tn), lambda i, j, k: (i, j)))
    dims = (((0,) if ta else (1,), (1,) if tb else (0,)), ((), ()))
    has_bias, has_addend, two = bias is not None, addend is not None, out_dtype2 is not None

    def body(*refs):
        a_ref, b_ref = refs[0], refs[1]
        pos = 2
        bias_ref = addend_ref = None
        if has_bias:
            bias_ref = refs[pos]
            pos += 1
        if has_addend:
            addend_ref = refs[pos]
            pos += 1
        o_refs, acc_ref = refs[pos:-1], refs[-1]
        k = pl.program_id(2)

        @pl.when(k == 0)
        def _():
            acc_ref[...] = jnp.zeros_like(acc_ref)

        acc_ref[...] += _dot(a_ref[...], b_ref[...], dims)

        @pl.when(k == nk - 1)
        def _():
            r = acc_ref[...]
            if has_bias:
                r = r + bias_ref[...]
            if has_addend:
                r = r + addend_scale * addend_ref[...].astype(F32)
            for o_ref in o_refs:
                o_ref[...] = r.astype(o_ref.dtype)

    ospec = pl.BlockSpec((tm, tn), lambda i, j, k: (i, j))
    dtypes = [out_dtype] + ([out_dtype2] if two else [])
    outs, couts = _call(
        body, name=name, grid=(m // tm, n // tn, nk), ins=ins, in_specs=specs,
        out_specs=[ospec] * len(dtypes), out_shape=[jax.ShapeDtypeStruct((m, n), d) for d in dtypes],
        scratch_shapes=[pltpu.VMEM((tm, tn), F32)], sem=("parallel", "parallel", "arbitrary"), comm=comm)
    primary = tuple(outs) if two else outs[0]
    return (primary, couts) if comm is not None else primary


def _rope_lane_constants():
    inv_freq = np.float32(ROPE_THETA) ** (-np.arange(8, dtype=np.float32) * np.float32(2.0 / 16.0))
    lane = np.arange(LANES) % 64
    freq = np.where(lane < 16, inv_freq[lane % 8], 0.0).astype(np.float32)
    sign = np.where(lane < 8, -1.0, np.where(lane < 16, 1.0, 0.0)).astype(np.float32)
    return jnp.asarray(freq)[None, :], jnp.asarray(sign)[None, :]


def _rope_tables(pos_col, name):
    t = pos_col.shape[0]
    tr = _pick(t, (512, 256, 128))
    freq, sign = _rope_lane_constants()

    def body(pos_ref, freq_ref, sign_ref, c_ref, s_ref):
        ang = pos_ref[...].astype(F32) * freq_ref[...]
        c_ref[...] = jnp.cos(ang)
        s_ref[...] = sign_ref[...] * jnp.sin(ang)

    return pl.pallas_call(
        body, name=name, grid=(t // tr,),
        in_specs=[pl.BlockSpec((tr, 1), lambda i: (i, 0)),
                  pl.BlockSpec((1, LANES), lambda i: (0, 0)),
                  pl.BlockSpec((1, LANES), lambda i: (0, 0))],
        out_specs=[pl.BlockSpec((tr, LANES), lambda i: (i, 0)), pl.BlockSpec((tr, LANES), lambda i: (i, 0))],
        out_shape=[jax.ShapeDtypeStruct((t, LANES), F32), jax.ShapeDtypeStruct((t, LANES), F32)],
        compiler_params=_cp("parallel"),
    )(pos_col, freq, sign)


def _swap8(t):
    width = t.shape[1]
    lane = jnp.bitwise_and(lax.broadcasted_iota(jnp.int32, t.shape, 1), 63)
    return jnp.where(lane < 8, pltpu.roll(t, width - 8, 1), jnp.where(lane < 16, pltpu.roll(t, 8, 1), 0.0))


def _rope(t, c, s):
    return t * c + _swap8(t) * s


def _rope_bwd(d, c, s):
    return d * c + _swap8(d * s)


def _tile4(a):
    return jnp.concatenate([a, a, a, a], axis=1)


def _attn_band(n, k_cur, k_prev, v_cur, v_prev, c_cur, s_cur, c_prev, s_prev):
    kband = jnp.concatenate([_rope(k_prev, c_prev, s_prev), _rope(k_cur, c_cur, s_cur)], axis=0)
    vband = jnp.concatenate([v_prev, v_cur], axis=0)
    qi = lax.broadcasted_iota(jnp.int32, (ATTN_BLOCK, 2 * ATTN_BLOCK), 0)
    kj = lax.broadcasted_iota(jnp.int32, (ATTN_BLOCK, 2 * ATTN_BLOCK), 1)
    dist = qi + ATTN_BLOCK - kj
    valid = (dist >= 0) & (dist < ATTN_BLOCK) & (n * ATTN_BLOCK - ATTN_BLOCK + kj >= 0)
    return (kband.astype(BF16), pltpu.roll(kband, 64, 1).astype(BF16),
            vband.astype(BF16), pltpu.roll(vband, 64, 1).astype(BF16), valid)


def _attn_probs(qm, kk, valid, sink):
    s = lax.dot_general(qm, kk, _NT, preferred_element_type=F32) * ATTN_SCALE
    s = jnp.where(valid, s, NEG_BIG)
    m = jnp.maximum(jnp.max(s, axis=1, keepdims=True), sink)
    p = jnp.exp(s - m)
    esink = jnp.exp(sink - m)
    z = jnp.sum(p, axis=1, keepdims=True) + esink
    return p / z, esink / z


def _attn_specs(nb):
    def cur(col, width=KV_W):
        return pl.BlockSpec((ATTN_BLOCK, width), lambda n: (jnp.minimum(n, nb - 1), col))

    def prev(col):
        return pl.BlockSpec((ATTN_BLOCK, KV_W), lambda n: (jnp.maximum(n - 1, 0), col))

    ua_specs = [cur(0, ATTN_W), cur(4), prev(4), cur(5), prev(5)]
    tab_specs = [cur(0), cur(0), prev(0), prev(0)]
    return ua_specs, tab_specs


def _attn_fwd(ua, ctab, stab, sinks, name, comm=None):
    t = ua.shape[0]
    nb = t // ATTN_BLOCK
    ua_specs, tab_specs = _attn_specs(nb)

    def body(q_ref, kc_ref, kp_ref, vc_ref, vp_ref, cc_ref, sc_ref, cp_ref, sp_ref, sink_ref, o_ref):
        n = pl.program_id(0)
        cc, sc = cc_ref[...], sc_ref[...]
        kb, kb_r, vb, vb_r, valid = _attn_band(n, kc_ref[...], kp_ref[...], vc_ref[...], vp_ref[...],
                                               cc, sc, cp_ref[...], sp_ref[...])
        qr = _rope(q_ref[...], _tile4(cc), _tile4(sc))
        lo = lax.broadcasted_iota(jnp.int32, (ATTN_BLOCK, LANES), 1) < 64
        outs = []
        for j in range(4):
            kv0 = j < 2
            qj = qr[:, j * LANES:(j + 1) * LANES]
            halves = []
            for is_lo in (True, False):
                aligned = is_lo == kv0
                msk = lo if is_lo else jnp.logical_not(lo)
                qm = jnp.where(msk, qj, 0.0).astype(BF16)
                probs, _ = _attn_probs(qm, kb if aligned else kb_r, valid, sink_ref[0, 2 * j + (0 if is_lo else 1)])
                halves.append(lax.dot_general(probs.astype(BF16), vb if aligned else vb_r, _NN,
                                              preferred_element_type=F32))
            outs.append(jnp.where(lo, halves[0], halves[1]))
        o_ref[...] = jnp.concatenate(outs, axis=1).astype(o_ref.dtype)

    return _call(
        body, name=name, grid=(nb,), ins=[ua, ua, ua, ua, ua, ctab, stab, ctab, stab, sinks],
        in_specs=ua_specs + tab_specs + [pl.BlockSpec(memory_space=pltpu.SMEM)],
        out_specs=[pl.BlockSpec((ATTN_BLOCK, ATTN_W), lambda n: (n, 0))],
        out_shape=[jax.ShapeDtypeStruct((t, ATTN_W), BF16)], sem=("parallel",), comm=comm)


def _attn_bwd(ua, d_out, ctab, stab, sinks, name, comm=None):
    t = ua.shape[0]
    nb = t // ATTN_BLOCK
    ua_specs, tab_specs = _attn_specs(nb)

    def body(q_ref, kc_ref, kp_ref, vc_ref, vp_ref, cc_ref, sc_ref, cp_ref, sp_ref, do_ref, sink_ref,
             dua_ref, dbias_ref, dsink_ref, dq_c, dk_c, dv_c, dq_n, dk_n, dv_n):
        n = pl.program_id(0)

        @pl.when(n == 0)
        def _():
            dq_c[...] = jnp.zeros_like(dq_c)
            dk_c[...] = jnp.zeros_like(dk_c)
            dv_c[...] = jnp.zeros_like(dv_c)
            dbias_ref[...] = jnp.zeros_like(dbias_ref)
            dsink_ref[...] = jnp.zeros_like(dsink_ref)

        @pl.when(n == nb)
        def _():
            dq_n[...] = jnp.zeros_like(dq_n)
            dk_n[...] = jnp.zeros_like(dk_n)
            dv_n[...] = jnp.zeros_like(dv_n)

        @pl.when(n < nb)
        def _():
            cc, sc = cc_ref[...], sc_ref[...]
            kb, kb_r, vb, vb_r, valid = _attn_band(n, kc_ref[...], kp_ref[...], vc_ref[...], vp_ref[...],
                                                   cc, sc, cp_ref[...], sp_ref[...])
            c4, s4 = _tile4(cc), _tile4(sc)
            qr = _rope(q_ref[...], c4, s4)
            do = do_ref[...].astype(F32)
            lane = lax.broadcasted_iota(jnp.int32, (ATTN_BLOCK, LANES), 1)
            lo = lane < 64
            lane_row = lax.broadcasted_iota(jnp.int32, (1, LANES), 1)
            dk_band = jnp.zeros((2 * ATTN_BLOCK, LANES), F32)
            dv_band = jnp.zeros((2 * ATTN_BLOCK, LANES), F32)
            dsink = jnp.zeros((1, LANES), F32)
            dqs = []
            for j in range(4):
                kv0 = j < 2
                qj = qr[:, j * LANES:(j + 1) * LANES]
                doj = do[:, j * LANES:(j + 1) * LANES]
                halves = []
                for is_lo in (True, False):
                    aligned = is_lo == kv0
                    head = 2 * j + (0 if is_lo else 1)
                    msk = lo if is_lo else jnp.logical_not(lo)
                    kk = kb if aligned else kb_r
                    vv = vb if aligned else vb_r
                    qm = jnp.where(msk, qj, 0.0).astype(BF16)
                    dom = jnp.where(msk, doj, 0.0).astype(BF16)
                    probs, psink = _attn_probs(qm, kk, valid, sink_ref[0, head])
                    dp = lax.dot_general(dom, vv, _NT, preferred_element_type=F32)
                    delta = jnp.sum(probs * dp, axis=1, keepdims=True)
                    ds = (probs * (dp - delta) * ATTN_SCALE).astype(BF16)
                    dsink = dsink + jnp.where(lane_row == head, -jnp.sum(psink * delta), 0.0)
                    halves.append(lax.dot_general(ds, kk, _NN, preferred_element_type=F32))
                    dk_h = lax.dot_general(ds, qm, _TN, preferred_element_type=F32)
                    dv_h = lax.dot_general(probs.astype(BF16), dom, _TN, preferred_element_type=F32)
                    if not aligned:
                        dk_h = pltpu.roll(dk_h, 64, 1)
                        dv_h = pltpu.roll(dv_h, 64, 1)
                    dk_band = dk_band + dk_h
                    dv_band = dv_band + dv_h
                dqs.append(jnp.where(lo, halves[0], halves[1]))
            dq_n[...] = _rope_bwd(jnp.concatenate(dqs, axis=1), c4, s4)
            dk_n[...] = dk_band
            dv_n[...] = dv_band
            dsink_ref[...] += dsink

        dk_prev = _rope_bwd(dk_c[...] + dk_n[0:ATTN_BLOCK, :], cp_ref[...], sp_ref[...])
        dv_prev = dv_c[...] + dv_n[0:ATTN_BLOCK, :]
        full = jnp.concatenate([dq_c[...], dk_prev, dv_prev], axis=1)
        dua_ref[...] = full.astype(dua_ref.dtype)
        dbias_ref[...] += jnp.sum(full, axis=0, keepdims=True)
        dq_c[...] = dq_n[...]
        dk_c[...] = dk_n[ATTN_BLOCK:, :]
        dv_c[...] = dv_n[ATTN_BLOCK:, :]

    return _call(
        body, name=name, grid=(nb + 1,), ins=[ua, ua, ua, ua, ua, ctab, stab, ctab, stab, d_out, sinks],
        in_specs=ua_specs + tab_specs + [
            pl.BlockSpec((ATTN_BLOCK, ATTN_W), lambda n: (jnp.minimum(n, nb - 1), 0)),
            pl.BlockSpec(memory_space=pltpu.SMEM)],
        out_specs=[pl.BlockSpec((ATTN_BLOCK, UA_W), lambda n: (jnp.maximum(n - 1, 0), 0)),
                   pl.BlockSpec((1, UA_W), lambda n: (0, 0)),
                   pl.BlockSpec((1, LANES), lambda n: (0, 0))],
        out_shape=[jax.ShapeDtypeStruct((t, UA_W), BF16),
                   jax.ShapeDtypeStruct((1, UA_W), F32),
                   jax.ShapeDtypeStruct((1, LANES), F32)],
        scratch_shapes=[pltpu.VMEM((ATTN_BLOCK, ATTN_W), F32), pltpu.VMEM((ATTN_BLOCK, KV_W), F32),
                        pltpu.VMEM((ATTN_BLOCK, KV_W), F32), pltpu.VMEM((ATTN_BLOCK, ATTN_W), F32),
                        pltpu.VMEM((2 * ATTN_BLOCK, KV_W), F32), pltpu.VMEM((2 * ATTN_BLOCK, KV_W), F32)],
        sem=("arbitrary",), comm=comm)


def _tri_mats():
    r = lax.broadcasted_iota(jnp.int32, (HGRN_CHUNK, LANES), 0)
    c = lax.broadcasted_iota(jnp.int32, (HGRN_CHUNK, LANES), 1)
    lower = ((c <= r) & (c < HGRN_CHUNK)).astype(F32)
    upper = ((c >= r) & (c < HGRN_CHUNK)).astype(F32)
    return lower, upper


def _tri_apply(tri, g):
    pad = jnp.concatenate([g, jnp.zeros_like(g)], axis=0)
    return lax.dot_general(tri, pad, _NN, precision=lax.Precision.HIGHEST, preferred_element_type=F32)


def _sub_masks():
    s = lax.broadcasted_iota(jnp.int32, (HGRN_CHUNK, LANES), 0)
    tt = lax.broadcasted_iota(jnp.int32, (HGRN_CHUNK, LANES), 1)
    return [(tt >= HGRN_SUB * i) & (tt < HGRN_SUB * (i + 1)) & (s <= tt) for i in range(HGRN_CHUNK // HGRN_SUB)]


def _hgrn_gates(hq, hf, lb_ref, b_scr):
    lb = _sig(lb_ref[0:1, :] - lb_ref[1:2, :])
    q = hq * _sig(hq)
    sg = _sig(hf)
    f = lb + (1.0 - lb) * sg
    k = 1.0 - f
    lower, _ = _tri_mats()
    b = _tri_apply(lower, jnp.log(f))
    b_scr[...] = b
    nsub = HGRN_CHUNK // HGRN_SUB
    starts = [jnp.zeros((1, HG_W), F32)] + [b_scr[HGRN_SUB * i - 1:HGRN_SUB * i, :] for i in range(1, nsub)]
    pq = jnp.concatenate([jnp.broadcast_to(p, (HGRN_SUB, HG_W)) for p in starts], axis=0)
    b_last = b_scr[HGRN_CHUNK - 1:HGRN_CHUNK, :]
    e_q = jnp.exp(b - pq)
    e_k = [jnp.exp(jnp.minimum(p - b, EXP_CLAMP)) for p in starts]
    e_b = jnp.exp(b)
    e_bl = jnp.exp(b_last - b)
    e_last = jnp.exp(b_last)
    return q, sg, f, k, lb, e_q, e_k, e_b, e_bl, e_last


def _intra_scores(kst, qt_pad, masks):
    ats = _dot3(kst, qt_pad, _NT)
    at = jnp.zeros((HGRN_CHUNK, LANES), F32)
    for i, msk in enumerate(masks):
        at = at + jnp.where(msk, ats[HGRN_CHUNK * i:HGRN_CHUNK * (i + 1), :], 0.0)
    return at


def _hgrn_fwd(uh, lb_raw, norm_g, name, comm=None):
    t = uh.shape[0]
    nc = t // HGRN_CHUNK

    def body(hq_ref, hf_ref, hi_ref, hg_ref, lb_ref, ng_ref, r_ref, o_ref, st_out_ref, st_ref, b_scr):
        c = pl.program_id(0)

        @pl.when(c == 0)
        def _():
            st_ref[...] = jnp.zeros_like(st_ref)

        hq, hf, v, hg = hq_ref[...], hf_ref[...], hi_ref[...], hg_ref[...]
        q, _, _, k, _, e_q, e_k, e_b, e_bl, e_last = _hgrn_gates(hq, hf, lb_ref, b_scr)
        qt, qb, kd = q * e_q, q * e_b, k * e_bl
        khat = [k * e for e in e_k]
        masks = _sub_masks()
        ng = ng_ref[...]
        zpad = jnp.zeros((HGRN_CHUNK, LANES), F32)
        o_heads, y_heads = [], []
        for h in range(4):
            sl = slice(h * LANES, (h + 1) * LANES)
            kst = jnp.concatenate([kh[:, sl] for kh in khat], axis=0)
            qt_pad = jnp.concatenate([qt[:, sl], zpad], axis=0)
            at = _intra_scores(kst, qt_pad, masks)
            vh = v[:, sl].astype(BF16)
            o_intra = lax.dot_general(at.astype(BF16), vh, _TN, preferred_element_type=F32)[0:HGRN_CHUNK, :]
            st = st_ref[h]
            st_out_ref[0, h] = st
            o_inter = _dot(qb[:, sl], st, _NT)
            st_ref[h] = st * e_last[:, sl] + _dot(vh, kd[:, sl], _TN)
            oh = o_intra + o_inter
            rs = lax.rsqrt(jnp.mean(oh * oh, axis=1, keepdims=True) + RMS_EPS)
            o_heads.append(oh)
            y_heads.append(oh * rs * ng)
        o_ref[...] = jnp.concatenate(o_heads, axis=1)
        r_ref[...] = (jnp.concatenate(y_heads, axis=1) * (hg * _sig(hg))).astype(r_ref.dtype)

    col = lambda j: pl.BlockSpec((HGRN_CHUNK, HG_W), lambda c: (c, j))
    return _call(
        body, name=name, grid=(nc,), ins=[uh, uh, uh, uh, lb_raw, norm_g],
        in_specs=[col(0), col(1), col(2), col(3),
                  pl.BlockSpec((2, HG_W), lambda c: (0, 0)), pl.BlockSpec((1, LANES), lambda c: (0, 0))],
        out_specs=[pl.BlockSpec((HGRN_CHUNK, HG_W), lambda c: (c, 0)),
                   pl.BlockSpec((HGRN_CHUNK, HG_W), lambda c: (c, 0)),
                   pl.BlockSpec((1, 4, LANES, LANES), lambda c: (c, 0, 0, 0))],
        out_shape=[jax.ShapeDtypeStruct((t, HG_W), BF16), jax.ShapeDtypeStruct((t, HG_W), F32),
                   jax.ShapeDtypeStruct((nc, 4, LANES, LANES), F32)],
        scratch_shapes=[pltpu.VMEM((4, LANES, LANES), F32), pltpu.VMEM((HGRN_CHUNK, HG_W), F32)],
        sem=("arbitrary",), comm=comm)


def _hgrn_bwd(uh, o_pre, d_r, states, lb_raw, norm_g, name, comm=None):
    t = uh.shape[0]
    nc = t // HGRN_CHUNK

    def body(hq_ref, hf_ref, hi_ref, hg_ref, o_ref, dr_ref, st_in_ref, lb_ref, ng_ref,
             duh_ref, dbias_ref, dng_ref, dlb_ref, dst_ref, b_scr, dlb_acc):
        i = pl.program_id(0)

        @pl.when(i == 0)
        def _():
            dst_ref[...] = jnp.zeros_like(dst_ref)
            dbias_ref[...] = jnp.zeros_like(dbias_ref)
            dng_ref[...] = jnp.zeros_like(dng_ref)
            dlb_acc[...] = jnp.zeros_like(dlb_acc)

        hq, hf, v, hg = hq_ref[...], hf_ref[...], hi_ref[...], hg_ref[...]
        q, sg, f, k, lb, e_q, e_k, e_b, e_bl, e_last = _hgrn_gates(hq, hf, lb_ref, b_scr)
        qt, qb, kd = q * e_q, q * e_b, k * e_bl
        khat = [k * e for e in e_k]
        masks = _sub_masks()
        ng = ng_ref[...]
        o = o_ref[...]
        dr = dr_ref[...].astype(F32)
        sgg = _sig(hg)
        gate = hg * sgg
        dy = dr * gate
        zpad = jnp.zeros((HGRN_CHUNK, LANES), F32)
        nsub = HGRN_CHUNK // HGRN_SUB
        dq_h, dk_h, dv_h, y_h, extra_h = [], [], [], [], []
        dng = jnp.zeros((1, LANES), F32)
        for h in range(4):
            sl = slice(h * LANES, (h + 1) * LANES)
            oh = o[:, sl]
            rs = lax.rsqrt(jnp.mean(oh * oh, axis=1, keepdims=True) + RMS_EPS)
            y_h.append(oh * rs * ng)
            dng = dng + jnp.sum(dy[:, sl] * oh * rs, axis=0, keepdims=True)
            w = dy[:, sl] * ng
            do = rs * (w - oh * (rs * rs) * jnp.mean(w * oh, axis=1, keepdims=True))
            do_b = do.astype(BF16)
            do_pad = jnp.concatenate([do, zpad], axis=0).astype(BF16)
            vh = v[:, sl].astype(BF16)
            kst = jnp.concatenate([kh[:, sl] for kh in khat], axis=0)
            qt_pad = jnp.concatenate([qt[:, sl], zpad], axis=0)
            at = _intra_scores(kst, qt_pad, masks)
            d_at = lax.dot_general(vh, do_pad, _NT, preferred_element_type=F32)
            d_ats = jnp.concatenate([jnp.where(m, d_at, 0.0) for m in masks], axis=0)
            d_kst = _dot3(d_ats, qt_pad, _NN)
            d_qt = _dot3(d_ats, kst, _TN)[0:HGRN_CHUNK, :]
            st_prev = st_in_ref[0, h]
            d_st = dst_ref[h]
            d_st_b = d_st.astype(BF16)
            dv = (lax.dot_general(at.astype(BF16), do_pad, _NN, preferred_element_type=F32)
                  + _dot(kd[:, sl], d_st_b, _NT))
            d_qb = _dot(do_b, st_prev, _NN)
            d_kd = lax.dot_general(vh, d_st_b, _NN, preferred_element_type=F32)
            extra_h.append(jnp.sum(st_prev * d_st, axis=0, keepdims=True) * e_last[:, sl]
                           + jnp.sum(kd[:, sl] * d_kd, axis=0, keepdims=True))
            dst_ref[h] = d_st * e_last[:, sl] + _dot(do_b, qb[:, sl], _TN)
            dq_h.append(d_qt * e_q[:, sl] + d_qb * e_b[:, sl])
            dkk = d_kd * e_bl[:, sl]
            for s in range(nsub):
                dkk = dkk + d_kst[HGRN_CHUNK * s:HGRN_CHUNK * (s + 1), :] * e_k[s][:, sl]
            dk_h.append(dkk)
            dv_h.append(dv)
        dq = jnp.concatenate(dq_h, axis=1)
        dk = jnp.concatenate(dk_h, axis=1)
        dv = jnp.concatenate(dv_h, axis=1)
        y = jnp.concatenate(y_h, axis=1)
        extra = jnp.concatenate(extra_h, axis=1)
        row = lax.broadcasted_iota(jnp.int32, (HGRN_CHUNK, HG_W), 0)
        db = q * dq - k * dk + jnp.where(row == HGRN_CHUNK - 1, extra, 0.0)
        _, upper = _tri_mats()
        dg = _tri_apply(upper, db)
        df = dg / f - dk
        dhf = df * (1.0 - lb) * sg * (1.0 - sg)
        dhq = dq * _dsilu(hq, _sig(hq))
        dhg = dr * y * _dsilu(hg, sgg)
        full = jnp.concatenate([dhq, dhf, dv, dhg], axis=1)
        duh_ref[...] = full.astype(duh_ref.dtype)
        dbias_ref[...] += jnp.sum(full, axis=0, keepdims=True)
        dng_ref[...] += dng
        dlb_acc[...] += jnp.sum(df * (1.0 - sg), axis=0, keepdims=True)

        @pl.when(i == nc - 1)
        def _():
            d_a0 = dlb_acc[...] * lb * (1.0 - lb)
            r8 = lax.broadcasted_iota(jnp.int32, (8, HG_W), 0)
            dlb_ref[...] = jnp.where(r8 == 0, d_a0, jnp.where(r8 == 1, -d_a0, 0.0))

    col = lambda j: pl.BlockSpec((HGRN_CHUNK, HG_W), lambda i: (nc - 1 - i, j))
    return _call(
        body, name=name, grid=(nc,), ins=[uh, uh, uh, uh, o_pre, d_r, states, lb_raw, norm_g],
        in_specs=[col(0), col(1), col(2), col(3), col(0), col(0),
                  pl.BlockSpec((1, 4, LANES, LANES), lambda i: (nc - 1 - i, 0, 0, 0)),
                  pl.BlockSpec((2, HG_W), lambda i: (0, 0)), pl.BlockSpec((1, LANES), lambda i: (0, 0))],
        out_specs=[pl.BlockSpec((HGRN_CHUNK, UH_W), lambda i: (nc - 1 - i, 0)),
                   pl.BlockSpec((1, UH_W), lambda i: (0, 0)),
                   pl.BlockSpec((1, LANES), lambda i: (0, 0)),
                   pl.BlockSpec((8, HG_W), lambda i: (0, 0))],
        out_shape=[jax.ShapeDtypeStruct((t, UH_W), BF16), jax.ShapeDtypeStruct((1, UH_W), F32),
                   jax.ShapeDtypeStruct((1, LANES), F32), jax.ShapeDtypeStruct((8, HG_W), F32)],
        scratch_shapes=[pltpu.VMEM((4, LANES, LANES), F32), pltpu.VMEM((HGRN_CHUNK, HG_W), F32),
                        pltpu.VMEM((1, HG_W), F32)],
        sem=("arbitrary",), comm=comm)


def _ln_fwd(z, g, b, name):
    t, d = z.shape
    tr = _pick(t, (256, 128))

    def body(z_ref, g_ref, b_ref, h_ref, hb_ref, xhat_ref, rstd_ref):
        zz = z_ref[...]
        mu = jnp.mean(zz, axis=1, keepdims=True)
        zc = zz - mu
        rstd = lax.rsqrt(jnp.mean(zc * zc, axis=1, keepdims=True) + LN_EPS)
        xhat = zc * rstd
        xhat_ref[...] = xhat
        rstd_ref[...] = rstd
        h = xhat * g_ref[...] + b_ref[...]
        h_ref[...] = h
        hb_ref[...] = h.astype(BF16)

    row = pl.BlockSpec((tr, d), lambda i: (i, 0))
    vec = pl.BlockSpec((1, d), lambda i: (0, 0))
    return pl.pallas_call(
        body, name=name, grid=(t // tr,),
        in_specs=[row, vec, vec],
        out_specs=[row, row, row, pl.BlockSpec((tr, 1), lambda i: (i, 0))],
        out_shape=[jax.ShapeDtypeStruct((t, d), F32), jax.ShapeDtypeStruct((t, d), BF16),
                   jax.ShapeDtypeStruct((t, d), F32), jax.ShapeDtypeStruct((t, 1), F32)],
        compiler_params=_cp("parallel"),
    )(z, g, b)


def _ln_bwd_math(dy, xhat, rstd, g):
    dxh = dy * g
    return rstd * (dxh - jnp.mean(dxh, axis=1, keepdims=True)
                   - xhat * jnp.mean(dxh * xhat, axis=1, keepdims=True))


def _ln_bwd(dy, xhat, rstd, g, name):
    t, d = dy.shape
    tr = _pick(t, (256, 128))

    def body(dy_ref, xhat_ref, rstd_ref, g_ref, dz_ref, dg_ref, db_ref):
        @pl.when(pl.program_id(0) == 0)
        def _():
            dg_ref[...] = jnp.zeros_like(dg_ref)
            db_ref[...] = jnp.zeros_like(db_ref)

        dyv, xh = dy_ref[...], xhat_ref[...]
        dz_ref[...] = _ln_bwd_math(dyv, xh, rstd_ref[...], g_ref[...])
        dg_ref[...] += jnp.sum(dyv * xh, axis=0, keepdims=True)
        db_ref[...] += jnp.sum(dyv, axis=0, keepdims=True)

    row = pl.BlockSpec((tr, d), lambda i: (i, 0))
    vec = pl.BlockSpec((1, d), lambda i: (0, 0))
    return pl.pallas_call(
        body, name=name, grid=(t // tr,),
        in_specs=[row, row, pl.BlockSpec((tr, 1), lambda i: (i, 0)), vec],
        out_specs=[row, vec, vec],
        out_shape=[jax.ShapeDtypeStruct((t, d), F32), jax.ShapeDtypeStruct((1, d), F32),
                   jax.ShapeDtypeStruct((1, d), F32)],
        compiler_params=_cp("arbitrary"),
    )(dy, xhat, rstd, g)


def _ln_loss_bwd(z, target, g, b, name):
    t, d = z.shape
    tr = _pick(t, (256, 128))

    def body(z_ref, tgt_ref, g_ref, b_ref, dz_ref, dg_ref, db_ref, loss_ref):
        @pl.when(pl.program_id(0) == 0)
        def _():
            dg_ref[...] = jnp.zeros_like(dg_ref)
            db_ref[...] = jnp.zeros_like(db_ref)
            loss_ref[...] = jnp.zeros_like(loss_ref)

        zz = z_ref[...]
        gg = g_ref[...]
        mu = jnp.mean(zz, axis=1, keepdims=True)
        zc = zz - mu
        rstd = lax.rsqrt(jnp.mean(zc * zc, axis=1, keepdims=True) + LN_EPS)
        xhat = zc * rstd
        err = xhat * gg + b_ref[...] - tgt_ref[...]
        loss_ref[...] += 0.5 * jnp.sum(jnp.mean(err * err, axis=1, keepdims=True))
        dy = err * (1.0 / d)
        dz_ref[...] = _ln_bwd_math(dy, xhat, rstd, gg)
        dg_ref[...] += jnp.sum(dy * xhat, axis=0, keepdims=True)
        db_ref[...] += jnp.sum(dy, axis=0, keepdims=True)

    row = pl.BlockSpec((tr, d), lambda i: (i, 0))
    vec = pl.BlockSpec((1, d), lambda i: (0, 0))
    return pl.pallas_call(
        body, name=name, grid=(t // tr,),
        in_specs=[row, row, vec, vec],
        out_specs=[row, vec, vec, pl.BlockSpec((1, LANES), lambda i: (0, 0))],
        out_shape=[jax.ShapeDtypeStruct((t, d), F32), jax.ShapeDtypeStruct((1, d), F32),
                   jax.ShapeDtypeStruct((1, d), F32), jax.ShapeDtypeStruct((1, LANES), F32)],
        compiler_params=_cp("arbitrary"),
    )(z, target, g, b)


CONV_TILE = 128
HALO = 8


def _conv_fwd(u2, conv_w, conv_b, name):
    t = u2.shape[0]
    tr = _pick(t, (CONV_TILE,))
    hb = tr // HALO

    def body(gp_ref, val_ref, prev_ref, w_ref, b_ref, out_ref, ext):
        i = pl.program_id(0)
        ext[0:HALO, :] = jnp.where(i == 0, 0.0, prev_ref[...])
        ext[HALO:, :] = gp_ref[...]
        gate = (ext[HALO - 2:HALO - 2 + tr, :] * w_ref[0:1, :] + ext[HALO - 1:HALO - 1 + tr, :] * w_ref[1:2, :]
                + ext[HALO:, :] * w_ref[2:3, :] + b_ref[...])
        out_ref[...] = (gate * _sig(gate) * val_ref[...]).astype(out_ref.dtype)

    return pl.pallas_call(
        body, name=name, grid=(t // tr,),
        in_specs=[pl.BlockSpec((tr, D_FF), lambda i: (i, 0)), pl.BlockSpec((tr, D_FF), lambda i: (i, 1)),
                  pl.BlockSpec((HALO, D_FF), lambda i: (jnp.maximum(i * hb - 1, 0), 0)),
                  pl.BlockSpec((3, D_FF), lambda i: (0, 0)), pl.BlockSpec((1, D_FF), lambda i: (0, 0))],
        out_specs=pl.BlockSpec((tr, D_FF), lambda i: (i, 0)),
        out_shape=jax.ShapeDtypeStruct((t, D_FF), BF16),
        scratch_shapes=[pltpu.VMEM((tr + HALO, D_FF), F32)],
        compiler_params=_cp("parallel"),
    )(u2, u2, u2, conv_w, conv_b)


def _conv_bwd(d_hmid, u2, conv_w, conv_b, name):
    t = u2.shape[0]
    tr = _pick(t, (CONV_TILE,))
    hb = tr // HALO
    last = t // HALO - 1

    def body(gp_ref, gp_prev_ref, gp_next_ref, val_ref, val_next_ref, dh_ref, dh_next_ref, w_ref, b_ref,
             du_ref, dw_ref, dcb_ref, ext, dgate_s):
        i = pl.program_id(0)

        @pl.when(i == 0)
        def _():
            dw_ref[...] = jnp.zeros_like(dw_ref)
            dcb_ref[...] = jnp.zeros_like(dcb_ref)

        ext[0:HALO, :] = jnp.where(i == 0, 0.0, gp_prev_ref[...])
        ext[HALO:HALO + tr, :] = gp_ref[...]
        ext[HALO + tr:, :] = gp_next_ref[...]
        w0, w1, w2 = w_ref[0:1, :], w_ref[1:2, :], w_ref[2:3, :]
        n_ext = tr + HALO
        gate = (ext[HALO - 2:HALO - 2 + n_ext, :] * w0 + ext[HALO - 1:HALO - 1 + n_ext, :] * w1
                + ext[HALO:HALO + n_ext, :] * w2 + b_ref[...])
        sg = _sig(gate)
        val = jnp.concatenate([val_ref[...], val_next_ref[...]], axis=0)
        dh = jnp.concatenate([dh_ref[...], dh_next_ref[...]], axis=0).astype(F32)
        row = lax.broadcasted_iota(jnp.int32, (n_ext, 1), 0)
        in_seq = (i * tr + row) < t
        dgate = jnp.where(in_seq, dh * val * _dsilu(gate, sg), 0.0)
        dgate_s[...] = dgate
        dg0 = dgate[0:tr, :]
        d_gp = dgate_s[2:2 + tr, :] * w0 + dgate_s[1:1 + tr, :] * w1 + dg0 * w2
        du_ref[:, 0:D_FF] = d_gp.astype(du_ref.dtype)
        du_ref[:, D_FF:] = (dh[0:tr, :] * (gate[0:tr, :] * sg[0:tr, :])).astype(du_ref.dtype)
        dcb_ref[...] += jnp.sum(dg0, axis=0, keepdims=True)
        dw_ref[0:1, :] += jnp.sum(dg0 * ext[HALO - 2:HALO - 2 + tr, :], axis=0, keepdims=True)
        dw_ref[1:2, :] += jnp.sum(dg0 * ext[HALO - 1:HALO - 1 + tr, :], axis=0, keepdims=True)
        dw_ref[2:3, :] += jnp.sum(dg0 * ext[HALO:HALO + tr, :], axis=0, keepdims=True)

    cur = lambda col: pl.BlockSpec((tr, D_FF), lambda i: (i, col))
    nxt = lambda col: pl.BlockSpec((HALO, D_FF), lambda i: (jnp.minimum((i + 1) * hb, last), col))
    return pl.pallas_call(
        body, name=name, grid=(t // tr,),
        in_specs=[cur(0), pl.BlockSpec((HALO, D_FF), lambda i: (jnp.maximum(i * hb - 1, 0), 0)), nxt(0),
                  cur(1), nxt(1), cur(0), nxt(0),
                  pl.BlockSpec((3, D_FF), lambda i: (0, 0)), pl.BlockSpec((1, D_FF), lambda i: (0, 0))],
        out_specs=[pl.BlockSpec((tr, 2 * D_FF), lambda i: (i, 0)),
                   pl.BlockSpec((8, D_FF), lambda i: (0, 0)), pl.BlockSpec((1, D_FF), lambda i: (0, 0))],
        out_shape=[jax.ShapeDtypeStruct((t, 2 * D_FF), BF16), jax.ShapeDtypeStruct((8, D_FF), F32),
                   jax.ShapeDtypeStruct((1, D_FF), F32)],
        scratch_shapes=[pltpu.VMEM((tr + 2 * HALO, D_FF), F32), pltpu.VMEM((tr + HALO, D_FF), F32)],
        compiler_params=_cp("arbitrary"),
    )(u2, u2, u2, u2, u2, d_hmid, d_hmid, conv_w, conv_b)


def _adamw(w, g, m, v, name):
    rows, cols = w.shape
    tr = _pick(rows, (256, 128, 64, 32, 16, 8))

    def body(w_ref, g_ref, m_ref, v_ref, d_ref, nm_ref, nv_ref):
        d_ref[...], nm_ref[...], nv_ref[...] = _adamw_math(w_ref[...], g_ref[...], m_ref[...], v_ref[...])

    spec = pl.BlockSpec((tr, cols), lambda i: (i, 0))
    shp = jax.ShapeDtypeStruct((rows, cols), F32)
    return pl.pallas_call(
        body, name=name, grid=(rows // tr,),
        in_specs=[spec, spec, spec, spec], out_specs=[spec, spec, spec], out_shape=[shp, shp, shp],
        compiler_params=_cp("parallel"),
    )(w, g, m, v)


def _pad_rows(a, rows):
    return jnp.pad(a, ((0, rows - a.shape[0]), (0, 0)))


SMALL_LAYOUT = (("ln1_g", 1024), ("ln1_b", 1024), ("b_in", 2816), ("sinks", 8), ("hgrn_lb", 1024),
                ("hgrn_norm_g", 128), ("ln2_g", 1024), ("ln2_b", 1024), ("conv_b", 2816), ("loss", 1))
SMALL_SHAPES = {"ln1_g": (1, 1024), "ln1_b": (1, 1024), "b_in": (1, 2816), "sinks": (1, 8), "hgrn_lb": (2, 512),
                "hgrn_norm_g": (1, 128), "ln2_g": (1, 1024), "ln2_b": (1, 1024), "conv_b": (1, 2816),
                "loss": (1,)}


def _pack_small(parts):
    rows = []
    for name, size in SMALL_LAYOUT:
        flat = parts[name].reshape(-1).astype(F32)
        padded = -(-size // LANES) * LANES
        rows.append(jnp.pad(flat, (0, padded - size)).reshape(-1, LANES))
    return _pad_rows(jnp.concatenate(rows, axis=0), SMALL_ROWS)


def _unpack_small(pack):
    out, r = {}, 0
    for name, size in SMALL_LAYOUT:
        nrows = -(-size // LANES)
        out[name] = pack[r:r + nrows].reshape(-1)[:size].reshape(SMALL_SHAPES[name])
        r += nrows
    return out


def _own(full, rows):
    return lax.dynamic_slice_in_dim(full, _me() * rows, rows, axis=0)


def kernel(x, positions, ln1_g, ln1_b, w_in, b_in, sinks, hgrn_lb, hgrn_norm_g, w_o, ln2_g, ln2_b, w_up, conv_w, conv_b, w_down, loss_target, m_ln1_g, m_ln1_b, m_w_in, m_b_in, m_sinks, m_hgrn_lb, m_hgrn_norm_g, m_w_o, m_ln2_g, m_ln2_b, m_w_up, m_conv_w, m_conv_b, m_w_down, v_ln1_g, v_ln1_b, v_w_in, v_b_in, v_sinks, v_hgrn_lb, v_hgrn_norm_g, v_w_o, v_ln2_g, v_ln2_b, v_w_up, v_conv_w, v_conv_b, v_w_down):
    t = x.shape[1]
    x2 = x[0]
    xb = x2.astype(BF16)
    target = loss_target[0]
    pos_col = positions.reshape(t, 1)

    w_in_t_s = w_in[0].T.astype(BF16)
    w_up_t_s = w_up[0].T.astype(BF16)
    w_o_s = w_o[0].astype(BF16)
    w_down_s = w_down[0].astype(BF16)
    w_in_t_g, cw_g = _comm_only(_Comm("gather", [w_in_t_s, _pad_rows(conv_w[0], 8)]), "ag_w_in")
    w_in_t = w_in_t_g.reshape(D_FF, D_MODEL)
    w_a_t, w_h_t = w_in_t[:UA_W], w_in_t[UA_W:]
    conv_w_f = cw_g[:, 0:3].transpose(1, 0, 2).reshape(3, D_FF)

    ua = _mm(xb, w_a_t, tb=True, bias=b_in[:, :UA_W], name="fwd_in_attn")
    uh = _mm(xb, w_h_t, tb=True, bias=b_in[:, UA_W:], name="fwd_in_hgrn")
    ctab, stab = _rope_tables(pos_col, "rope_tables")
    (a_out,), (w_o_g,) = _attn_fwd(ua, ctab, stab, sinks, "attn_fwd", comm=_Comm("gather", [w_o_s]))
    (r_out, o_pre, states), (w_up_t_g,) = _hgrn_fwd(uh, hgrn_lb, hgrn_norm_g, "hgrn_fwd",
                                                     comm=_Comm("gather", [w_up_t_s]))
    w_o_f = w_o_g.reshape(D_MODEL, D_MODEL)
    w_up_t = w_up_t_g.reshape(2 * D_FF, D_MODEL)
    z1 = _mm(a_out, w_o_f[:ATTN_W], addend=x2, addend_scale=ALPHA, name="fwd_o_attn")
    z1 = _mm(r_out, w_o_f[ATTN_W:], addend=z1, name="fwd_o_hgrn")
    h1, h1b, xhat1, rstd1 = _ln_fwd(z1, ln1_g, ln1_b, "ln1_fwd")
    u2, (w_down_g,) = _mm(h1b, w_up_t, tb=True, tn=1408, name="fwd_up", comm=_Comm("gather", [w_down_s]))
    w_down_f = w_down_g.reshape(D_FF, D_MODEL)
    hmid = _conv_fwd(u2, conv_w_f, conv_b, "conv_fwd")
    z2 = _mm(hmid, w_down_f, addend=h1, addend_scale=ALPHA, tk=1408, name="fwd_down")
    dz2, d_ln2_g, d_ln2_b, loss_part = _ln_loss_bwd(z2, target, ln2_g, ln2_b, "ln2_loss_bwd")

    d_hmid = _mm(dz2, w_down_f, tb=True, tn=1408, name="bwd_down_dx")
    d_w_down, d_w_down_b = _mm(hmid, dz2, ta=True, out_dtype2=BF16, tm=1408, tk=512, name="bwd_down_dw")
    d_u2, d_conv_w8, d_conv_b = _conv_bwd(d_hmid, u2, conv_w_f, conv_b, "conv_bwd")
    d_h1 = _mm(d_u2, w_up_t, addend=dz2, addend_scale=ALPHA, tk=1408, name="bwd_up_dx")
    d_w_up_t, d_w_up_t_b = _mm(d_u2, h1b, ta=True, out_dtype2=BF16, tm=1408, tk=512, name="bwd_up_dw")
    dz1, d_ln1_g, d_ln1_b = _ln_bwd(d_h1, xhat1, rstd1, ln1_g, "ln1_bwd")
    d_a = _mm(dz1, w_o_f[:ATTN_W], tb=True, name="bwd_o_dx_attn")
    d_r = _mm(dz1, w_o_f[ATTN_W:], tb=True, name="bwd_o_dx_hgrn")
    d_w_o_a, d_w_o_a_b = _mm(a_out, dz1, ta=True, out_dtype2=BF16, tk=512, name="bwd_o_dw_attn")
    d_w_o_r, d_w_o_r_b = _mm(r_out, dz1, ta=True, out_dtype2=BF16, tk=512, name="bwd_o_dw_hgrn")
    d_w_o = jnp.concatenate([d_w_o_a, d_w_o_r], axis=0)
    d_w_o_b = jnp.concatenate([d_w_o_a_b, d_w_o_r_b], axis=0)
    (d_uh, d_bias_h, d_norm_g, d_lb8), (recv_up,) = _hgrn_bwd(
        uh, o_pre, d_r, states, hgrn_lb, hgrn_norm_g, "hgrn_bwd",
        comm=_Comm("exchange", [d_w_up_t_b.reshape(N_DEV, SHARD_UP, D_MODEL)]))
    d_cw_x = d_conv_w8.reshape(8, N_DEV, SHARD_IN).transpose(1, 0, 2)
    (d_ua, d_bias_a, d_sinks), (recv_down, recv_o, recv_cw) = _attn_bwd(
        ua, d_a, ctab, stab, sinks, "attn_bwd",
        comm=_Comm("exchange", [d_w_down_b.reshape(N_DEV, SHARD_DOWN, D_MODEL),
                                d_w_o_b.reshape(N_DEV, SHARD_O, D_MODEL), d_cw_x]))
    d_w_a_t, d_w_a_t_b = _mm(d_ua, xb, ta=True, out_dtype2=BF16, tk=512, name="bwd_in_dw_attn")
    d_w_h_t, d_w_h_t_b = _mm(d_uh, xb, ta=True, out_dtype2=BF16, tk=512, name="bwd_in_dw_hgrn")
    d_w_in_t = jnp.concatenate([d_w_a_t, d_w_h_t], axis=0)
    d_w_in_t_b = jnp.concatenate([d_w_a_t_b, d_w_h_t_b], axis=0)
    dx = _mm(d_ua, w_a_t, addend=dz1, addend_scale=ALPHA, tk=768, name="bwd_in_dx_attn")
    dx, (recv_in,) = _mm(d_uh, w_h_t, addend=dx, name="bwd_in_dx_hgrn",
                         comm=_Comm("exchange", [d_w_in_t_b.reshape(N_DEV, SHARD_IN, D_MODEL)]))

    g_w_in = _sum_shards(recv_in, _own(d_w_in_t, SHARD_IN), "sum_w_in").T
    g_w_up = _sum_shards(recv_up, _own(d_w_up_t, SHARD_UP), "sum_w_up").T
    res_in = (g_w_in,) + tuple(_adamw(w_in[0], g_w_in, m_w_in[0], v_w_in[0], "adamw_w_in"))
    res_up = (g_w_up,) + tuple(_adamw(w_up[0], g_w_up, m_w_up[0], v_w_up[0], "adamw_w_up"))
    res_o = _sum_shards_adamw(recv_o, _own(d_w_o, SHARD_O), w_o[0], m_w_o[0], v_w_o[0], "adamw_w_o")
    res_down = _sum_shards_adamw(recv_down, _own(d_w_down, SHARD_DOWN), w_down[0], m_w_down[0], v_w_down[0],
                                 "adamw_w_down")
    g_cw = _sum_slots(recv_cw, "sum_conv_w")
    cw8 = lambda a: _pad_rows(a, 8)
    res_cw = (g_cw,) + tuple(_adamw(cw8(conv_w[0]), g_cw, cw8(m_conv_w[0]), cw8(v_conv_w[0]), "adamw_conv_w"))
    big = {"w_in": [r[None] for r in res_in], "w_up": [r[None] for r in res_up],
           "w_o": [r[None] for r in res_o], "w_down": [r[None] for r in res_down],
           "conv_w": [r[None, 0:3] for r in res_cw]}

    small_local = _pack_small({
        "ln1_g": d_ln1_g, "ln1_b": d_ln1_b, "b_in": jnp.concatenate([d_bias_a, d_bias_h], axis=1),
        "sinks": d_sinks[:, :8], "hgrn_lb": d_lb8[0:2], "hgrn_norm_g": d_norm_g, "ln2_g": d_ln2_g,
        "ln2_b": d_ln2_b, "conv_b": d_conv_b, "loss": loss_part[:, :1]})
    small_sum = _sum_slots(_all_gather_vmem(small_local, "ar_small"), "ar_small_sum")
    gs = _unpack_small(small_sum)
    loss = gs["loss"][0]
    zero1 = jnp.zeros((1,), F32)
    w_small = _pack_small({"ln1_g": ln1_g, "ln1_b": ln1_b, "b_in": b_in, "sinks": sinks, "hgrn_lb": hgrn_lb,
                           "hgrn_norm_g": hgrn_norm_g, "ln2_g": ln2_g, "ln2_b": ln2_b, "conv_b": conv_b,
                           "loss": zero1})
    m_small = _pack_small({"ln1_g": m_ln1_g, "ln1_b": m_ln1_b, "b_in": m_b_in, "sinks": m_sinks,
                           "hgrn_lb": m_hgrn_lb, "hgrn_norm_g": m_hgrn_norm_g, "ln2_g": m_ln2_g,
                           "ln2_b": m_ln2_b, "conv_b": m_conv_b, "loss": zero1})
    v_small = _pack_small({"ln1_g": v_ln1_g, "ln1_b": v_ln1_b, "b_in": v_b_in, "sinks": v_sinks,
                           "hgrn_lb": v_hgrn_lb, "hgrn_norm_g": v_hgrn_norm_g, "ln2_g": v_ln2_g,
                           "ln2_b": v_ln2_b, "conv_b": v_conv_b, "loss": zero1})
    small = [gs] + [_unpack_small(p) for p in _adamw(w_small, small_sum, m_small, v_small, "adamw_small")]

    order = ["ln1_g", "ln1_b", "w_in", "b_in", "sinks", "hgrn_lb", "hgrn_norm_g", "w_o", "ln2_g", "ln2_b",
             "w_up", "conv_w", "conv_b", "w_down"]

    def pick(idx):
        return [big[n][idx] if n in big else small[idx][n] for n in order]

    return (loss, dx[None], *pick(0), *pick(1), *pick(2), *pick(3))
```

```python
import functools

import jax
import jax.numpy as jnp
import numpy as np
from jax import lax
from jax.experimental import pallas as pl
from jax.experimental.pallas import tpu as pltpu

F32 = jnp.float32
BF16 = jnp.bfloat16

N_DEV = 8
D_MODEL = 1024
D_FF = 2816
ATTN_W = 512
KV_W = 128
UA_W = ATTN_W + 2 * KV_W
UH_W = 2048
HG_W = 512
ATTN_BLOCK = 128
HGRN_CHUNK = 64
HGRN_SUB = 16
HGRN_CHUNKS_PER_STEP = 4
EXP_CLAMP = 85.0
NEG_BIG = -1e30
LN_EPS = 1e-5
RMS_EPS = 1e-6
ALPHA = 2.0 ** 0.25
ATTN_SCALE = 0.125
ROPE_THETA = 500000.0

ADAM_LR = 0.001
ADAM_B1 = 0.9
ADAM_B2 = 0.999
ADAM_EPS = 1e-08
ADAM_WD = 0.01
ADAM_STEP = 10

LANES = 128
VMEM_LIMIT_BYTES = 56 * 1024 * 1024

SHARD_IN = D_FF // N_DEV
SHARD_UP = 2 * D_FF // N_DEV
SHARD_O = D_MODEL // N_DEV
SHARD_DOWN = D_FF // N_DEV
SMALL_ROWS = 88

_MESH = pl.DeviceIdType.MESH
_NT = (((1,), (1,)), ((), ()))
_NN = (((1,), (0,)), ((), ()))
_TN = (((0,), (0,)), ((), ()))


def _cp(*sem):
    if sem:
        return pltpu.CompilerParams(dimension_semantics=sem, vmem_limit_bytes=VMEM_LIMIT_BYTES)
    return pltpu.CompilerParams(vmem_limit_bytes=VMEM_LIMIT_BYTES)


def _sig(x):
    return 1.0 / (1.0 + jnp.exp(-x))


def _dsilu(x, s):
    return s * (1.0 + x * (1.0 - s))


def _dot(a, b, dims):
    return lax.dot_general(a.astype(BF16), b.astype(BF16), dims, preferred_element_type=F32)


def _split(a):
    hi = a.astype(BF16)
    return hi, (a - hi.astype(F32)).astype(BF16)


def _dot3(a, b, dims):
    ah, al = _split(a)
    bh, bl = _split(b)
    d = functools.partial(lax.dot_general, dimension_numbers=dims, preferred_element_type=F32)
    return d(ah, bh) + (d(ah, bl) + d(al, bh))


def _pick(n, pref):
    for t in pref:
        if t <= n and n % t == 0:
            return t
    return n


def _my_coords():
    return lax.axis_index("x"), lax.axis_index("y"), lax.axis_index("c")


def _peer(k):
    x, y, c = _my_coords()
    return (1 - x if k & 4 else x, 1 - y if k & 2 else y, 1 - c if k & 1 else c)


def _me():
    x, y, c = _my_coords()
    return 4 * x + 2 * y + c


class _Comm:
    def __init__(self, kind, arrays):
        self.kind, self.arrays, self.n = kind, list(arrays), len(arrays)

    def out_shapes(self):
        if self.kind == "gather":
            return [jax.ShapeDtypeStruct((N_DEV,) + a.shape, a.dtype) for a in self.arrays]
        return [jax.ShapeDtypeStruct(a.shape, a.dtype) for a in self.arrays]

    def specs(self):
        return [pl.BlockSpec(memory_space=pl.ANY)] * self.n

    def scratch(self):
        return [pltpu.SemaphoreType.DMA(((N_DEV - 1) * self.n,)), pltpu.SemaphoreType.DMA(((N_DEV - 1) * self.n,)),
                pltpu.SemaphoreType.DMA((self.n,))]

    def _src(self, ref, dev):
        return ref if self.kind == "gather" else ref.at[dev]

    def _copy(self, a, k, src, dst, sems, me, slot):
        other = jnp.bitwise_xor(me, k)
        idx = a * (N_DEV - 1) + k - 1
        return pltpu.make_async_remote_copy(
            src_ref=self._src(src, other), dst_ref=dst.at[me if slot == "mine" else other],
            send_sem=sems[0].at[idx], recv_sem=sems[1].at[idx], device_id=_peer(k), device_id_type=_MESH)

    def start(self, srcs, dsts, sems):
        me = _me()
        for a, (src, dst) in enumerate(zip(srcs, dsts)):
            pltpu.make_async_copy(self._src(src, me), dst.at[me], sems[2].at[a]).start()
            for k in range(1, N_DEV):
                self._copy(a, k, src, dst, sems, me, "mine").start()

    def wait(self, srcs, dsts, sems):
        me = _me()
        for a, (src, dst) in enumerate(zip(srcs, dsts)):
            for k in range(1, N_DEV):
                self._copy(a, k, src, dst, sems, me, "theirs").wait_recv()
            for k in range(1, N_DEV):
                self._copy(a, k, src, dst, sems, me, "mine").wait_send()
            pltpu.make_async_copy(self._src(src, me), dst.at[me], sems[2].at[a]).wait()


def _call(body, *, name, grid, ins, in_specs, out_specs, out_shape, scratch_shapes=(), sem, comm=None):
    n_in, n_out, n_scr = len(ins), len(out_shape), len(scratch_shapes)
    if comm is None:
        outs = pl.pallas_call(
            body, name=name, grid=grid, in_specs=list(in_specs), out_specs=list(out_specs),
            out_shape=list(out_shape), scratch_shapes=list(scratch_shapes), compiler_params=_cp(*sem))(*ins)
        return list(outs), []
    nc = comm.n

    def hosted(*refs):
        pos = n_in
        c_in = refs[pos:pos + nc]
        pos += nc
        outs = refs[pos:pos + n_out]
        pos += n_out
        c_out = refs[pos:pos + nc]
        pos += nc
        scr = refs[pos:pos + n_scr]
        sems = refs[pos + n_scr:]
        ids = [pl.program_id(d) for d in range(len(grid))]
        first = functools.reduce(jnp.logical_and, [i == 0 for i in ids])
        last = functools.reduce(jnp.logical_and, [i == g - 1 for i, g in zip(ids, grid)])

        @pl.when(first)
        def _():
            comm.start(c_in, c_out, sems)

        body(*refs[:n_in], *outs, *scr)

        @pl.when(last)
        def _():
            comm.wait(c_in, c_out, sems)

    outs = pl.pallas_call(
        hosted, name=name, grid=grid, in_specs=list(in_specs) + comm.specs(),
        out_specs=list(out_specs) + comm.specs(), out_shape=list(out_shape) + comm.out_shapes(),
        scratch_shapes=list(scratch_shapes) + comm.scratch(),
        compiler_params=_cp(*(["arbitrary"] * len(grid))))(*ins, *comm.arrays)
    return list(outs[:n_out]), list(outs[n_out:])


def _comm_only(comm, name):
    def body(*refs):
        srcs, dsts, sems = refs[:comm.n], refs[comm.n:2 * comm.n], refs[2 * comm.n:]
        comm.start(srcs, dsts, sems)
        comm.wait(srcs, dsts, sems)

    return list(pl.pallas_call(
        body, name=name, in_specs=comm.specs(), out_specs=comm.specs(), out_shape=comm.out_shapes(),
        scratch_shapes=comm.scratch(), compiler_params=_cp())(*comm.arrays))


def _all_gather_vmem(x, name):
    def body(x_ref, out_ref, send_sems, recv_sems, local_sem):
        me = _me()
        local = pltpu.make_async_copy(x_ref, out_ref.at[me], local_sem)
        local.start()
        sends = []
        for k in range(1, N_DEV):
            cp = pltpu.make_async_remote_copy(
                src_ref=x_ref, dst_ref=out_ref.at[me], send_sem=send_sems.at[k - 1],
                recv_sem=recv_sems.at[k - 1], device_id=_peer(k), device_id_type=_MESH)
            cp.start()
            sends.append(cp)
        for k in range(1, N_DEV):
            pltpu.make_async_remote_copy(
                src_ref=x_ref, dst_ref=out_ref.at[jnp.bitwise_xor(me, k)], send_sem=send_sems.at[k - 1],
                recv_sem=recv_sems.at[k - 1], device_id=_peer(k), device_id_type=_MESH).wait_recv()
        for cp in sends:
            cp.wait_send()
        local.wait()

    return pl.pallas_call(
        body, name=name,
        out_shape=jax.ShapeDtypeStruct((N_DEV,) + x.shape, x.dtype),
        in_specs=[pl.BlockSpec(memory_space=pltpu.VMEM)],
        out_specs=pl.BlockSpec(memory_space=pltpu.VMEM),
        scratch_shapes=[pltpu.SemaphoreType.DMA((N_DEV - 1,)), pltpu.SemaphoreType.DMA((N_DEV - 1,)),
                        pltpu.SemaphoreType.DMA],
        compiler_params=_cp(),
    )(x)


def _sum_slots(gathered, name):
    _, rows, cols = gathered.shape

    def body(g_ref, out_ref):
        acc = g_ref[0]
        for s in range(1, N_DEV):
            acc = acc + g_ref[s]
        out_ref[...] = acc

    return pl.pallas_call(
        body, name=name,
        out_shape=jax.ShapeDtypeStruct((rows, cols), F32),
        compiler_params=_cp(),
    )(gathered)


def _slot_sum(recv_ref, own_ref, shape):
    me = _me()
    acc = jnp.zeros(shape, F32)
    for s in range(N_DEV):
        acc = acc + jnp.where(me == s, own_ref[...], recv_ref[s].astype(F32))
    return acc


def _adamw_math(w, g, m, v):
    nm = ADAM_B1 * m + (1.0 - ADAM_B1) * g
    nv = ADAM_B2 * v + (1.0 - ADAM_B2) * (g * g)
    m_hat = nm / (1.0 - ADAM_B1 ** ADAM_STEP)
    v_hat = nv / (1.0 - ADAM_B2 ** ADAM_STEP)
    return -ADAM_LR * (m_hat / (jnp.sqrt(v_hat) + ADAM_EPS) + ADAM_WD * w), nm, nv


def _sum_shards(recv, own, name):
    _, rows, cols = recv.shape
    tr = _pick(rows, (256, 128, 176, 64, 32, 16, 8))

    def body(recv_ref, own_ref, out_ref):
        out_ref[...] = _slot_sum(recv_ref, own_ref, (tr, cols))

    return pl.pallas_call(
        body, name=name, grid=(rows // tr,),
        in_specs=[pl.BlockSpec((N_DEV, tr, cols), lambda i: (0, i, 0)), pl.BlockSpec((tr, cols), lambda i: (i, 0))],
        out_specs=pl.BlockSpec((tr, cols), lambda i: (i, 0)),
        out_shape=jax.ShapeDtypeStruct((rows, cols), F32),
        compiler_params=_cp("parallel"),
    )(recv, own)


def _sum_shards_adamw(recv, own, w, m, v, name):
    _, rows, cols = recv.shape
    tr = _pick(rows, (128, 176, 64, 32, 16, 8))

    def body(recv_ref, own_ref, w_ref, m_ref, v_ref, g_ref, d_ref, nm_ref, nv_ref):
        g = _slot_sum(recv_ref, own_ref, (tr, cols))
        g_ref[...] = g
        d_ref[...], nm_ref[...], nv_ref[...] = _adamw_math(w_ref[...], g, m_ref[...], v_ref[...])

    spec = pl.BlockSpec((tr, cols), lambda i: (i, 0))
    shp = jax.ShapeDtypeStruct((rows, cols), F32)
    return pl.pallas_call(
        body, name=name, grid=(rows // tr,),
        in_specs=[pl.BlockSpec((N_DEV, tr, cols), lambda i: (0, i, 0)), spec, spec, spec, spec],
        out_specs=[spec, spec, spec, spec], out_shape=[shp, shp, shp, shp],
        compiler_params=_cp("parallel"),
    )(recv, own, w, m, v)


def _mm(a, b, *, name, ta=False, tb=False, out_dtype=F32, out_dtype2=None, bias=None, addend=None,
        addend_scale=1.0, tm=1024, tn=1024, tk=1024, comm=None):
    kdim, m = a.shape if ta else a.shape[::-1]
    n = b.shape[0] if tb else b.shape[1]
    tm = _pick(m, (tm, 1408, 1024, 768, 512, 256, 128))
    tn = _pick(n, (tn, 1408, 1024, 768, 512, 256, 128))
    tk = _pick(kdim, (tk, 1408, 1024, 768, 512, 256, 128))
    nk = kdim // tk
    a_spec = pl.BlockSpec((tk, tm), lambda i, j, k: (k, i)) if ta else pl.BlockSpec((tm, tk), lambda i, j, k: (i, k))
    b_spec = pl.BlockSpec((tn, tk), lambda i, j, k: (j, k)) if tb else pl.BlockSpec((tk, tn), lambda i, j, k: (k, j))
    ins, specs = [a, b], [a_spec, b_spec]
    if bias is not None:
        ins.append(bias)
        specs.append(pl.BlockSpec((1, tn), lambda i, j, k: (0, j)))
    if addend is not None:
        ins.append(addend)
        specs.append(pl.BlockSpec((tm, tn), lambda i, j, k: (i, j)))
    dims = (((0,) if ta else (1,), (1,) if tb else (0,)), ((), ()))
    has_bias, has_addend, two = bias is not None, addend is not None, out_dtype2 is not None

    def body(*refs):
        a_ref, b_ref = refs[0], refs[1]
        pos = 2
        bias_ref = addend_ref = None
        if has_bias:
            bias_ref = refs[pos]
            pos += 1
        if has_addend:
            addend_ref = refs[pos]
            pos += 1
        o_refs, acc_ref = refs[pos:-1], refs[-1]
        k = pl.program_id(2)

        @pl.when(k == 0)
        def _():
            acc_ref[...] = jnp.zeros_like(acc_ref)

        acc_ref[...] += _dot(a_ref[...], b_ref[...], dims)

        @pl.when(k == nk - 1)
        def _():
            r = acc_ref[...]
            if has_bias:
                r = r + bias_ref[...]
            if has_addend:
                r = r + addend_scale * addend_ref[...].astype(F32)
            for o_ref in o_refs:
                o_ref[...] = r.astype(o_ref.dtype)

    ospec = pl.BlockSpec((tm, tn), lambda i, j, k: (i, j))
    dtypes = [out_dtype] + ([out_dtype2] if two else [])
    outs, couts = _call(
        body, name=name, grid=(m // tm, n // tn, nk), ins=ins, in_specs=specs,
        out_specs=[ospec] * len(dtypes), out_shape=[jax.ShapeDtypeStruct((m, n), d) for d in dtypes],
        scratch_shapes=[pltpu.VMEM((tm, tn), F32)], sem=("parallel", "parallel", "arbitrary"), comm=comm)
    primary = tuple(outs) if two else outs[0]
    return (primary, couts) if comm is not None else primary


def _rope_lane_constants():
    inv_freq = np.float32(ROPE_THETA) ** (-np.arange(8, dtype=np.float32) * np.float32(2.0 / 16.0))
    lane = np.arange(LANES) % 64
    freq = np.where(lane < 16, inv_freq[lane % 8], 0.0).astype(np.float32)
    sign = np.where(lane < 8, -1.0, np.where(lane < 16, 1.0, 0.0)).astype(np.float32)
    return jnp.asarray(freq)[None, :], jnp.asarray(sign)[None, :]


def _rope_tables(pos_col, name):
    t = pos_col.shape[0]
    tr = _pick(t, (512, 256, 128))
    freq, sign = _rope_lane_constants()

    def body(pos_ref, freq_ref, sign_ref, c_ref, s_ref):
        ang = pos_ref[...].astype(F32) * freq_ref[...]
        c_ref[...] = jnp.cos(ang)
        s_ref[...] = sign_ref[...] * jnp.sin(ang)

    return pl.pallas_call(
        body, name=name, grid=(t // tr,),
        in_specs=[pl.BlockSpec((tr, 1), lambda i: (i, 0)),
                  pl.BlockSpec((1, LANES), lambda i: (0, 0)),
                  pl.BlockSpec((1, LANES), lambda i: (0, 0))],
        out_specs=[pl.BlockSpec((tr, LANES), lambda i: (i, 0)), pl.BlockSpec((tr, LANES), lambda i: (i, 0))],
        out_shape=[jax.ShapeDtypeStruct((t, LANES), F32), jax.ShapeDtypeStruct((t, LANES), F32)],
        compiler_params=_cp("parallel"),
    )(pos_col, freq, sign)


def _swap8(t):
    width = t.shape[1]
    lane = jnp.bitwise_and(lax.broadcasted_iota(jnp.int32, t.shape, 1), 63)
    return jnp.where(lane < 8, pltpu.roll(t, width - 8, 1), jnp.where(lane < 16, pltpu.roll(t, 8, 1), 0.0))


def _rope(t, c, s):
    return t * c + _swap8(t) * s


def _rope_bwd(d, c, s):
    return d * c + _swap8(d * s)


def _tile4(a):
    return jnp.concatenate([a, a, a, a], axis=1)


def _attn_band(n, k_cur, k_prev, v_cur, v_prev, c_cur, s_cur, c_prev, s_prev):
    kband = jnp.concatenate([_rope(k_prev, c_prev, s_prev), _rope(k_cur, c_cur, s_cur)], axis=0)
    vband = jnp.concatenate([v_prev, v_cur], axis=0)
    qi = lax.broadcasted_iota(jnp.int32, (ATTN_BLOCK, 2 * ATTN_BLOCK), 0)
    kj = lax.broadcasted_iota(jnp.int32, (ATTN_BLOCK, 2 * ATTN_BLOCK), 1)
    dist = qi + ATTN_BLOCK - kj
    valid = (dist >= 0) & (dist < ATTN_BLOCK) & (n * ATTN_BLOCK - ATTN_BLOCK + kj >= 0)
    return (kband.astype(BF16), pltpu.roll(kband, 64, 1).astype(BF16),
            vband.astype(BF16), pltpu.roll(vband, 64, 1).astype(BF16), valid)


def _attn_probs(raw, valid, sink, axis):
    s = jnp.where(valid, raw * ATTN_SCALE, NEG_BIG)
    m = jnp.maximum(jnp.max(s, axis=axis, keepdims=True), sink)
    p = jnp.exp(s - m)
    esink = jnp.exp(sink - m)
    z = jnp.sum(p, axis=axis, keepdims=True) + esink
    return p / z, esink / z


def _attn_valid_t(n):
    kj = lax.broadcasted_iota(jnp.int32, (2 * ATTN_BLOCK, ATTN_BLOCK), 0)
    qi = lax.broadcasted_iota(jnp.int32, (2 * ATTN_BLOCK, ATTN_BLOCK), 1)
    dist = qi + ATTN_BLOCK - kj
    return (dist >= 0) & (dist < ATTN_BLOCK) & (n * ATTN_BLOCK - ATTN_BLOCK + kj >= 0)


def _attn_specs(nb):
    def cur(col, width=KV_W):
        return pl.BlockSpec((ATTN_BLOCK, width), lambda n: (jnp.minimum(n, nb - 1), col))

    def prev(col):
        return pl.BlockSpec((ATTN_BLOCK, KV_W), lambda n: (jnp.maximum(n - 1, 0), col))

    ua_specs = [cur(0, ATTN_W), cur(4), prev(4), cur(5), prev(5)]
    tab_specs = [cur(0), cur(0), prev(0), prev(0)]
    return ua_specs, tab_specs


def _attn_fwd(ua, ctab, stab, sinks, name, comm=None):
    t = ua.shape[0]
    nb = t // ATTN_BLOCK
    ua_specs, tab_specs = _attn_specs(nb)

    def body(q_ref, kc_ref, kp_ref, vc_ref, vp_ref, cc_ref, sc_ref, cp_ref, sp_ref, sink_ref, o_ref):
        n = pl.program_id(0)
        cc, sc = cc_ref[...], sc_ref[...]
        kb, kb_r, vb, vb_r, valid = _attn_band(n, kc_ref[...], kp_ref[...], vc_ref[...], vp_ref[...],
                                               cc, sc, cp_ref[...], sp_ref[...])
        qr = _rope(q_ref[...], _tile4(cc), _tile4(sc))
        lo = lax.broadcasted_iota(jnp.int32, (ATTN_BLOCK, LANES), 1) < 64
        heads = []
        for j in range(4):
            qj = qr[:, j * LANES:(j + 1) * LANES]
            for is_lo in (True, False):
                aligned = is_lo == (j < 2)
                qm = jnp.where(lo if is_lo else jnp.logical_not(lo), qj, 0.0).astype(BF16)
                raw = lax.dot_general(qm, kb if aligned else kb_r, _NT, preferred_element_type=F32)
                heads.append((raw, vb if aligned else vb_r, sink_ref[0, len(heads)]))
        halves = []
        for raw, vv, sink in heads:
            probs, _ = _attn_probs(raw, valid, sink, 1)
            halves.append(lax.dot_general(probs.astype(BF16), vv, _NN, preferred_element_type=F32))
        outs = [jnp.where(lo, halves[2 * j], halves[2 * j + 1]) for j in range(4)]
        o_ref[...] = jnp.concatenate(outs, axis=1).astype(o_ref.dtype)

    return _call(
        body, name=name, grid=(nb,), ins=[ua, ua, ua, ua, ua, ctab, stab, ctab, stab, sinks],
        in_specs=ua_specs + tab_specs + [pl.BlockSpec(memory_space=pltpu.SMEM)],
        out_specs=[pl.BlockSpec((ATTN_BLOCK, ATTN_W), lambda n: (n, 0))],
        out_shape=[jax.ShapeDtypeStruct((t, ATTN_W), BF16)], sem=("parallel",), comm=comm)


def _attn_bwd(ua, d_out, ctab, stab, sinks, name, comm=None):
    t = ua.shape[0]
    nb = t // ATTN_BLOCK
    ua_specs, tab_specs = _attn_specs(nb)

    def body(q_ref, kc_ref, kp_ref, vc_ref, vp_ref, cc_ref, sc_ref, cp_ref, sp_ref, do_ref, sink_ref,
             dua_ref, dbias_ref, dsink_ref, dq_c, dk_c, dv_c, dq_n, dk_n, dv_n):
        n = pl.program_id(0)

        @pl.when(n == 0)
        def _():
            dq_c[...] = jnp.zeros_like(dq_c)
            dk_c[...] = jnp.zeros_like(dk_c)
            dv_c[...] = jnp.zeros_like(dv_c)
            dbias_ref[...] = jnp.zeros_like(dbias_ref)
            dsink_ref[...] = jnp.zeros_like(dsink_ref)

        @pl.when(n == nb)
        def _():
            dq_n[...] = jnp.zeros_like(dq_n)
            dk_n[...] = jnp.zeros_like(dk_n)
            dv_n[...] = jnp.zeros_like(dv_n)

        @pl.when(n < nb)
        def _():
            cc, sc = cc_ref[...], sc_ref[...]
            kb, kb_r, vb, vb_r, valid = _attn_band(n, kc_ref[...], kp_ref[...], vc_ref[...], vp_ref[...],
                                                   cc, sc, cp_ref[...], sp_ref[...])
            valid_t = _attn_valid_t(n)
            c4, s4 = _tile4(cc), _tile4(sc)
            qr = _rope(q_ref[...], c4, s4)
            do = do_ref[...].astype(F32)
            lane = lax.broadcasted_iota(jnp.int32, (ATTN_BLOCK, LANES), 1)
            lo = lane < 64
            lane_row = lax.broadcasted_iota(jnp.int32, (1, LANES), 1)
            heads = []
            for j in range(4):
                qj = qr[:, j * LANES:(j + 1) * LANES]
                doj = do[:, j * LANES:(j + 1) * LANES]
                for is_lo in (True, False):
                    aligned = is_lo == (j < 2)
                    msk = lo if is_lo else jnp.logical_not(lo)
                    kk = kb if aligned else kb_r
                    vv = vb if aligned else vb_r
                    qm = jnp.where(msk, qj, 0.0).astype(BF16)
                    dom = jnp.where(msk, doj, 0.0).astype(BF16)
                    heads.append(dict(
                        aligned=aligned, kk=kk, qm=qm, dom=dom, sink=sink_ref[0, len(heads)],
                        raw=lax.dot_general(qm, kk, _NT, preferred_element_type=F32),
                        dp=lax.dot_general(dom, vv, _NT, preferred_element_type=F32),
                        raw_t=lax.dot_general(kk, qm, _NT, preferred_element_type=F32),
                        dp_t=lax.dot_general(vv, dom, _NT, preferred_element_type=F32)))
            dk_band = jnp.zeros((2 * ATTN_BLOCK, LANES), F32)
            dv_band = jnp.zeros((2 * ATTN_BLOCK, LANES), F32)
            dsink = jnp.zeros((1, LANES), F32)
            halves = []
            for head, hd in enumerate(heads):
                probs, psink = _attn_probs(hd["raw"], valid, hd["sink"], 1)
                delta = jnp.sum(probs * hd["dp"], axis=1, keepdims=True)
                ds = (probs * (hd["dp"] - delta) * ATTN_SCALE).astype(BF16)
                dsink = dsink + jnp.where(lane_row == head, -jnp.sum(psink * delta), 0.0)
                halves.append(lax.dot_general(ds, hd["kk"], _NN, preferred_element_type=F32))
                probs_t, _ = _attn_probs(hd["raw_t"], valid_t, hd["sink"], 0)
                delta_t = jnp.sum(probs_t * hd["dp_t"], axis=0, keepdims=True)
                ds_t = (probs_t * (hd["dp_t"] - delta_t) * ATTN_SCALE).astype(BF16)
                dk_h = lax.dot_general(ds_t, hd["qm"], _NN, preferred_element_type=F32)
                dv_h = lax.dot_general(probs_t.astype(BF16), hd["dom"], _NN, preferred_element_type=F32)
                if not hd["aligned"]:
                    dk_h = pltpu.roll(dk_h, 64, 1)
                    dv_h = pltpu.roll(dv_h, 64, 1)
                dk_band = dk_band + dk_h
                dv_band = dv_band + dv_h
            dqs = [jnp.where(lo, halves[2 * j], halves[2 * j + 1]) for j in range(4)]
            dq_n[...] = _rope_bwd(jnp.concatenate(dqs, axis=1), c4, s4)
            dk_n[...] = dk_band
            dv_n[...] = dv_band
            dsink_ref[...] += dsink

        dk_prev = _rope_bwd(dk_c[...] + dk_n[0:ATTN_BLOCK, :], cp_ref[...], sp_ref[...])
        dv_prev = dv_c[...] + dv_n[0:ATTN_BLOCK, :]
        full = jnp.concatenate([dq_c[...], dk_prev, dv_prev], axis=1)
        dua_ref[...] = full.astype(dua_ref.dtype)
        dbias_ref[...] += jnp.sum(full, axis=0, keepdims=True)
        dq_c[...] = dq_n[...]
        dk_c[...] = dk_n[ATTN_BLOCK:, :]
        dv_c[...] = dv_n[ATTN_BLOCK:, :]

    return _call(
        body, name=name, grid=(nb + 1,), ins=[ua, ua, ua, ua, ua, ctab, stab, ctab, stab, d_out, sinks],
        in_specs=ua_specs + tab_specs + [
            pl.BlockSpec((ATTN_BLOCK, ATTN_W), lambda n: (jnp.minimum(n, nb - 1), 0)),
            pl.BlockSpec(memory_space=pltpu.SMEM)],
        out_specs=[pl.BlockSpec((ATTN_BLOCK, UA_W), lambda n: (jnp.maximum(n - 1, 0), 0)),
                   pl.BlockSpec((1, UA_W), lambda n: (0, 0)),
                   pl.BlockSpec((1, LANES), lambda n: (0, 0))],
        out_shape=[jax.ShapeDtypeStruct((t, UA_W), BF16),
                   jax.ShapeDtypeStruct((1, UA_W), F32),
                   jax.ShapeDtypeStruct((1, LANES), F32)],
        scratch_shapes=[pltpu.VMEM((ATTN_BLOCK, ATTN_W), F32), pltpu.VMEM((ATTN_BLOCK, KV_W), F32),
                        pltpu.VMEM((ATTN_BLOCK, KV_W), F32), pltpu.VMEM((ATTN_BLOCK, ATTN_W), F32),
                        pltpu.VMEM((2 * ATTN_BLOCK, KV_W), F32), pltpu.VMEM((2 * ATTN_BLOCK, KV_W), F32)],
        sem=("arbitrary",), comm=comm)


def _tri_mats():
    r = lax.broadcasted_iota(jnp.int32, (HGRN_CHUNK, LANES), 0)
    c = lax.broadcasted_iota(jnp.int32, (HGRN_CHUNK, LANES), 1)
    lower = ((c <= r) & (c < HGRN_CHUNK)).astype(F32)
    upper = ((c >= r) & (c < HGRN_CHUNK)).astype(F32)
    return lower, upper


def _tri_apply(tri, g):
    pad = jnp.concatenate([g, jnp.zeros_like(g)], axis=0)
    return lax.dot_general(tri, pad, _NN, precision=lax.Precision.HIGHEST, preferred_element_type=F32)


def _sub_masks():
    s = lax.broadcasted_iota(jnp.int32, (HGRN_CHUNK, LANES), 0)
    tt = lax.broadcasted_iota(jnp.int32, (HGRN_CHUNK, LANES), 1)
    return [(tt >= HGRN_SUB * i) & (tt < HGRN_SUB * (i + 1)) & (s <= tt) for i in range(HGRN_CHUNK // HGRN_SUB)]


def _hgrn_gates(hq, hf, lb_ref, b_scr):
    lb = _sig(lb_ref[0:1, :] - lb_ref[1:2, :])
    q = hq * _sig(hq)
    sg = _sig(hf)
    f = lb + (1.0 - lb) * sg
    k = 1.0 - f
    lower, _ = _tri_mats()
    b = _tri_apply(lower, jnp.log(f))
    b_scr[...] = b
    nsub = HGRN_CHUNK // HGRN_SUB
    starts = [jnp.zeros((1, HG_W), F32)] + [b_scr[HGRN_SUB * i - 1:HGRN_SUB * i, :] for i in range(1, nsub)]
    pq = jnp.concatenate([jnp.broadcast_to(p, (HGRN_SUB, HG_W)) for p in starts], axis=0)
    b_last = b_scr[HGRN_CHUNK - 1:HGRN_CHUNK, :]
    e_q = jnp.exp(b - pq)
    e_k = [jnp.exp(jnp.minimum(p - b, EXP_CLAMP)) for p in starts]
    e_b = jnp.exp(b)
    e_bl = jnp.exp(b_last - b)
    e_last = jnp.exp(b_last)
    return q, sg, f, k, lb, e_q, e_k, e_b, e_bl, e_last


def _sub_masks_ts():
    tt = lax.broadcasted_iota(jnp.int32, (HGRN_CHUNK, LANES), 0)
    s = lax.broadcasted_iota(jnp.int32, (HGRN_CHUNK, LANES), 1)
    return [(tt >= HGRN_SUB * i) & (tt < HGRN_SUB * (i + 1)) & (s <= tt) for i in range(HGRN_CHUNK // HGRN_SUB)]


def _masked_sum(blocks, masks, axis):
    step = HGRN_CHUNK if axis == 0 else LANES
    acc = jnp.zeros((HGRN_CHUNK, LANES), F32)
    for i, msk in enumerate(masks):
        blk = blocks[step * i:step * (i + 1), :] if axis == 0 else blocks[:, step * i:step * (i + 1)]
        acc = acc + jnp.where(msk, blk, 0.0)
    return acc


def _hgrn_chunk_inputs(j, hq_ref, hf_ref, hi_ref, hg_ref, lb_ref, b_scr):
    rows = slice(j * HGRN_CHUNK, (j + 1) * HGRN_CHUNK)
    hq, hf, v, hg = hq_ref[rows, :], hf_ref[rows, :], hi_ref[rows, :], hg_ref[rows, :]
    q, sg, f, k, lb, e_q, e_k, e_b, e_bl, e_last = _hgrn_gates(hq, hf, lb_ref, b_scr.at[j])
    return dict(rows=rows, hq=hq, v=v, hg=hg, q=q, sg=sg, f=f, k=k, lb=lb, e_q=e_q, e_k=e_k, e_b=e_b, e_bl=e_bl,
                e_last=e_last, qt=q * e_q, qb=q * e_b, kd=k * e_bl, khat=[k * e for e in e_k])


def _hgrn_fwd(uh, lb_raw, norm_g, name, comm=None):
    t = uh.shape[0]
    nc = t // HGRN_CHUNK
    cps = _pick(nc, (HGRN_CHUNKS_PER_STEP, 2, 1))
    rows_step = cps * HGRN_CHUNK

    def body(hq_ref, hf_ref, hi_ref, hg_ref, lb_ref, ng_ref, r_ref, o_ref, st_out_ref, st_ref, b_scr):
        @pl.when(pl.program_id(0) == 0)
        def _():
            st_ref[...] = jnp.zeros_like(st_ref)

        masks = _sub_masks_ts()
        ng = ng_ref[...]
        zpad = jnp.zeros((HGRN_CHUNK, LANES), F32)
        heads = [slice(h * LANES, (h + 1) * LANES) for h in range(4)]
        chunks = [_hgrn_chunk_inputs(j, hq_ref, hf_ref, hi_ref, hg_ref, lb_ref, b_scr) for j in range(cps)]
        for ch in chunks:
            ch["scores"] = [_dot3(ch["qt"][:, sl],
                                  jnp.concatenate([x for kh in ch["khat"] for x in (kh[:, sl], zpad)], axis=0), _NT)
                            for sl in heads]
        for j, ch in enumerate(chunks):
            o_heads, y_heads = [], []
            for h, sl in enumerate(heads):
                a_ts = _masked_sum(ch["scores"][h], masks, 1)
                vh = ch["v"][:, sl].astype(BF16)
                v_pad = jnp.concatenate([vh, jnp.zeros_like(vh)], axis=0)
                o_intra = lax.dot_general(a_ts.astype(BF16), v_pad, _NN, preferred_element_type=F32)
                st = st_ref[h]
                st_out_ref[j, h] = st
                o_inter = _dot(ch["qb"][:, sl], st, _NT)
                st_ref[h] = st * ch["e_last"][:, sl] + _dot(vh, ch["kd"][:, sl], _TN)
                oh = o_intra + o_inter
                rs = lax.rsqrt(jnp.mean(oh * oh, axis=1, keepdims=True) + RMS_EPS)
                o_heads.append(oh)
                y_heads.append(oh * rs * ng)
            hg = ch["hg"]
            o_ref[ch["rows"], :] = jnp.concatenate(o_heads, axis=1)
            r_ref[ch["rows"], :] = (jnp.concatenate(y_heads, axis=1) * (hg * _sig(hg))).astype(r_ref.dtype)

    col = lambda j: pl.BlockSpec((rows_step, HG_W), lambda c: (c, j))
    return _call(
        body, name=name, grid=(nc // cps,), ins=[uh, uh, uh, uh, lb_raw, norm_g],
        in_specs=[col(0), col(1), col(2), col(3),
                  pl.BlockSpec((2, HG_W), lambda c: (0, 0)), pl.BlockSpec((1, LANES), lambda c: (0, 0))],
        out_specs=[pl.BlockSpec((rows_step, HG_W), lambda c: (c, 0)),
                   pl.BlockSpec((rows_step, HG_W), lambda c: (c, 0)),
                   pl.BlockSpec((cps, 4, LANES, LANES), lambda c: (c, 0, 0, 0))],
        out_shape=[jax.ShapeDtypeStruct((t, HG_W), BF16), jax.ShapeDtypeStruct((t, HG_W), F32),
                   jax.ShapeDtypeStruct((nc, 4, LANES, LANES), F32)],
        scratch_shapes=[pltpu.VMEM((4, LANES, LANES), F32), pltpu.VMEM((cps, HGRN_CHUNK, HG_W), F32)],
        sem=("arbitrary",), comm=comm)


def _hgrn_bwd(uh, o_pre, d_r, states, lb_raw, norm_g, name, comm=None):
    t = uh.shape[0]
    nc = t // HGRN_CHUNK
    cps = _pick(nc, (HGRN_CHUNKS_PER_STEP, 2, 1))
    ns = nc // cps
    rows_step = cps * HGRN_CHUNK
    nsub = HGRN_CHUNK // HGRN_SUB

    def body(hq_ref, hf_ref, hi_ref, hg_ref, o_ref, dr_ref, st_in_ref, lb_ref, ng_ref,
             duh_ref, dbias_ref, dng_ref, dlb_ref, dst_ref, b_scr, dlb_acc):
        i = pl.program_id(0)

        @pl.when(i == 0)
        def _():
            dst_ref[...] = jnp.zeros_like(dst_ref)
            dbias_ref[...] = jnp.zeros_like(dbias_ref)
            dng_ref[...] = jnp.zeros_like(dng_ref)
            dlb_acc[...] = jnp.zeros_like(dlb_acc)

        masks_st = _sub_masks()
        masks_ts = _sub_masks_ts()
        ng = ng_ref[...]
        zpad = jnp.zeros((HGRN_CHUNK, LANES), F32)
        _, upper = _tri_mats()
        heads = [slice(h * LANES, (h + 1) * LANES) for h in range(4)]
        row = lax.broadcasted_iota(jnp.int32, (HGRN_CHUNK, HG_W), 0)

        chunks = [_hgrn_chunk_inputs(j, hq_ref, hf_ref, hi_ref, hg_ref, lb_ref, b_scr) for j in range(cps)]
        dng = jnp.zeros((1, LANES), F32)
        for ch in chunks:
            o = o_ref[ch["rows"], :]
            dr = dr_ref[ch["rows"], :].astype(F32)
            hg = ch["hg"]
            sgg = _sig(hg)
            dy = dr * (hg * sgg)
            do_h, y_h = [], []
            for sl in heads:
                oh = o[:, sl]
                rs = lax.rsqrt(jnp.mean(oh * oh, axis=1, keepdims=True) + RMS_EPS)
                y_h.append(oh * rs * ng)
                dng = dng + jnp.sum(dy[:, sl] * oh * rs, axis=0, keepdims=True)
                w = dy[:, sl] * ng
                do_h.append(rs * (w - oh * (rs * rs) * jnp.mean(w * oh, axis=1, keepdims=True)))
            ch["do"] = do_h
            ch["dhg"] = dr * jnp.concatenate(y_h, axis=1) * _dsilu(hg, sgg)

        for ch in chunks:
            ch["kst"], ch["kpad"], ch["qt_pad"], ch["v_b"], ch["do_pad"] = [], [], [], [], []
            ch["ats"], ch["d_at"], ch["d_a"] = [], [], []
            for h, sl in enumerate(heads):
                kst = jnp.concatenate([kh[:, sl] for kh in ch["khat"]], axis=0)
                kpad = jnp.concatenate([x for kh in ch["khat"] for x in (kh[:, sl], zpad)], axis=0)
                qt_pad = jnp.concatenate([ch["qt"][:, sl], zpad], axis=0)
                vh = ch["v"][:, sl].astype(BF16)
                v_pad = jnp.concatenate([vh, jnp.zeros_like(vh)], axis=0)
                do_b = ch["do"][h].astype(BF16)
                do_pad = jnp.concatenate([do_b, jnp.zeros_like(do_b)], axis=0)
                ch["kst"].append(kst)
                ch["kpad"].append(kpad)
                ch["qt_pad"].append(qt_pad)
                ch["v_b"].append(vh)
                ch["do_pad"].append(do_pad)
                ch["ats"].append(_dot3(kst, qt_pad, _NT))
                ch["d_at"].append(lax.dot_general(vh, do_pad, _NT, preferred_element_type=F32))
                ch["d_a"].append(lax.dot_general(do_b, v_pad, _NT, preferred_element_type=F32))

        for ch in chunks:
            ch["d_kst"], ch["d_qt"], ch["dv"] = [], [], []
            for h in range(4):
                at = _masked_sum(ch["ats"][h], masks_st, 0)
                d_ats = jnp.concatenate([jnp.where(m, ch["d_at"][h], 0.0) for m in masks_st], axis=0)
                d_a_cat = jnp.concatenate([jnp.where(m, ch["d_a"][h], 0.0) for m in masks_ts], axis=1)
                ch["d_kst"].append(_dot3(d_ats, ch["qt_pad"][h], _NN))
                ch["d_qt"].append(_dot3(d_a_cat, ch["kpad"][h], _NN))
                ch["dv"].append(lax.dot_general(at.astype(BF16), ch["do_pad"][h], _NN, preferred_element_type=F32))

        for j in reversed(range(cps)):
            ch = chunks[j]
            q, k, sg, f, lb = ch["q"], ch["k"], ch["sg"], ch["f"], ch["lb"]
            dq_h, dk_h, dv_h, extra_h = [], [], [], []
            for h, sl in enumerate(heads):
                st_prev = st_in_ref[j, h]
                d_st = dst_ref[h]
                d_st_b = d_st.astype(BF16)
                do_b = ch["do_pad"][h][0:HGRN_CHUNK, :]
                kd, e_last = ch["kd"][:, sl], ch["e_last"][:, sl]
                dv = ch["dv"][h] + _dot(kd, d_st_b, _NT)
                d_qb = _dot(do_b, st_prev, _NN)
                d_kd = lax.dot_general(ch["v_b"][h], d_st_b, _NN, preferred_element_type=F32)
                extra_h.append(jnp.sum(st_prev * d_st, axis=0, keepdims=True) * e_last
                               + jnp.sum(kd * d_kd, axis=0, keepdims=True))
                dst_ref[h] = d_st * e_last + _dot(do_b, ch["qb"][:, sl], _TN)
                dq_h.append(ch["d_qt"][h] * ch["e_q"][:, sl] + d_qb * ch["e_b"][:, sl])
                dkk = d_kd * ch["e_bl"][:, sl]
                for s_ in range(nsub):
                    dkk = dkk + ch["d_kst"][h][HGRN_CHUNK * s_:HGRN_CHUNK * (s_ + 1), :] * ch["e_k"][s_][:, sl]
                dk_h.append(dkk)
                dv_h.append(dv)
            dq = jnp.concatenate(dq_h, axis=1)
            dk = jnp.concatenate(dk_h, axis=1)
            dv = jnp.concatenate(dv_h, axis=1)
            extra = jnp.concatenate(extra_h, axis=1)
            db = q * dq - k * dk + jnp.where(row == HGRN_CHUNK - 1, extra, 0.0)
            dg = _tri_apply(upper, db)
            df = dg / f - dk
            dhf = df * (1.0 - lb) * sg * (1.0 - sg)
            dhq = dq * _dsilu(ch["hq"], _sig(ch["hq"]))
            full = jnp.concatenate([dhq, dhf, dv, ch["dhg"]], axis=1)
            duh_ref[ch["rows"], :] = full.astype(duh_ref.dtype)
            dbias_ref[...] += jnp.sum(full, axis=0, keepdims=True)
            dlb_acc[...] += jnp.sum(df * (1.0 - sg), axis=0, keepdims=True)
        dng_ref[...] += dng

        @pl.when(i == ns - 1)
        def _():
            lb = chunks[0]["lb"]
            d_a0 = dlb_acc[...] * lb * (1.0 - lb)
            r8 = lax.broadcasted_iota(jnp.int32, (8, HG_W), 0)
            dlb_ref[...] = jnp.where(r8 == 0, d_a0, jnp.where(r8 == 1, -d_a0, 0.0))

    col = lambda j: pl.BlockSpec((rows_step, HG_W), lambda i: (ns - 1 - i, j))
    return _call(
        body, name=name, grid=(ns,), ins=[uh, uh, uh, uh, o_pre, d_r, states, lb_raw, norm_g],
        in_specs=[col(0), col(1), col(2), col(3), col(0), col(0),
                  pl.BlockSpec((cps, 4, LANES, LANES), lambda i: (ns - 1 - i, 0, 0, 0)),
                  pl.BlockSpec((2, HG_W), lambda i: (0, 0)), pl.BlockSpec((1, LANES), lambda i: (0, 0))],
        out_specs=[pl.BlockSpec((rows_step, UH_W), lambda i: (ns - 1 - i, 0)),
                   pl.BlockSpec((1, UH_W), lambda i: (0, 0)),
                   pl.BlockSpec((1, LANES), lambda i: (0, 0)),
                   pl.BlockSpec((8, HG_W), lambda i: (0, 0))],
        out_shape=[jax.ShapeDtypeStruct((t, UH_W), BF16), jax.ShapeDtypeStruct((1, UH_W), F32),
                   jax.ShapeDtypeStruct((1, LANES), F32), jax.ShapeDtypeStruct((8, HG_W), F32)],
        scratch_shapes=[pltpu.VMEM((4, LANES, LANES), F32), pltpu.VMEM((cps, HGRN_CHUNK, HG_W), F32),
                        pltpu.VMEM((1, HG_W), F32)],
        sem=("arbitrary",), comm=comm)


def _ln_fwd(z, g, b, name):
    t, d = z.shape
    tr = _pick(t, (256, 128))

    def body(z_ref, g_ref, b_ref, h_ref, hb_ref, xhat_ref, rstd_ref):
        zz = z_ref[...]
        mu = jnp.mean(zz, axis=1, keepdims=True)
        zc = zz - mu
        rstd = lax.rsqrt(jnp.mean(zc * zc, axis=1, keepdims=True) + LN_EPS)
        xhat = zc * rstd
        xhat_ref[...] = xhat
        rstd_ref[...] = rstd
        h = xhat * g_ref[...] + b_ref[...]
        h_ref[...] = h
        hb_ref[...] = h.astype(BF16)

    row = pl.BlockSpec((tr, d), lambda i: (i, 0))
    vec = pl.BlockSpec((1, d), lambda i: (0, 0))
    return pl.pallas_call(
        body, name=name, grid=(t // tr,),
        in_specs=[row, vec, vec],
        out_specs=[row, row, row, pl.BlockSpec((tr, 1), lambda i: (i, 0))],
        out_shape=[jax.ShapeDtypeStruct((t, d), F32), jax.ShapeDtypeStruct((t, d), BF16),
                   jax.ShapeDtypeStruct((t, d), F32), jax.ShapeDtypeStruct((t, 1), F32)],
        compiler_params=_cp("parallel"),
    )(z, g, b)


def _ln_bwd_math(dy, xhat, rstd, g):
    dxh = dy * g
    return rstd * (dxh - jnp.mean(dxh, axis=1, keepdims=True)
                   - xhat * jnp.mean(dxh * xhat, axis=1, keepdims=True))


def _ln_bwd(dy, xhat, rstd, g, name):
    t, d = dy.shape
    tr = _pick(t, (256, 128))

    def body(dy_ref, xhat_ref, rstd_ref, g_ref, dz_ref, dg_ref, db_ref):
        @pl.when(pl.program_id(0) == 0)
        def _():
            dg_ref[...] = jnp.zeros_like(dg_ref)
            db_ref[...] = jnp.zeros_like(db_ref)

        dyv, xh = dy_ref[...], xhat_ref[...]
        dz_ref[...] = _ln_bwd_math(dyv, xh, rstd_ref[...], g_ref[...])
        dg_ref[...] += jnp.sum(dyv * xh, axis=0, keepdims=True)
        db_ref[...] += jnp.sum(dyv, axis=0, keepdims=True)

    row = pl.BlockSpec((tr, d), lambda i: (i, 0))
    vec = pl.BlockSpec((1, d), lambda i: (0, 0))
    return pl.pallas_call(
        body, name=name, grid=(t // tr,),
        in_specs=[row, row, pl.BlockSpec((tr, 1), lambda i: (i, 0)), vec],
        out_specs=[row, vec, vec],
        out_shape=[jax.ShapeDtypeStruct((t, d), F32), jax.ShapeDtypeStruct((1, d), F32),
                   jax.ShapeDtypeStruct((1, d), F32)],
        compiler_params=_cp("arbitrary"),
    )(dy, xhat, rstd, g)


def _ln_loss_bwd(z, target, g, b, name):
    t, d = z.shape
    tr = _pick(t, (256, 128))

    def body(z_ref, tgt_ref, g_ref, b_ref, dz_ref, dg_ref, db_ref, loss_ref):
        @pl.when(pl.program_id(0) == 0)
        def _():
            dg_ref[...] = jnp.zeros_like(dg_ref)
            db_ref[...] = jnp.zeros_like(db_ref)
            loss_ref[...] = jnp.zeros_like(loss_ref)

        zz = z_ref[...]
        gg = g_ref[...]
        mu = jnp.mean(zz, axis=1, keepdims=True)
        zc = zz - mu
        rstd = lax.rsqrt(jnp.mean(zc * zc, axis=1, keepdims=True) + LN_EPS)
        xhat = zc * rstd
        err = xhat * gg + b_ref[...] - tgt_ref[...]
        loss_ref[...] += 0.5 * jnp.sum(jnp.mean(err * err, axis=1, keepdims=True))
        dy = err * (1.0 / d)
        dz_ref[...] = _ln_bwd_math(dy, xhat, rstd, gg)
        dg_ref[...] += jnp.sum(dy * xhat, axis=0, keepdims=True)
        db_ref[...] += jnp.sum(dy, axis=0, keepdims=True)

    row = pl.BlockSpec((tr, d), lambda i: (i, 0))
    vec = pl.BlockSpec((1, d), lambda i: (0, 0))
    return pl.pallas_call(
        body, name=name, grid=(t // tr,),
        in_specs=[row, row, vec, vec],
        out_specs=[row, vec, vec, pl.BlockSpec((1, LANES), lambda i: (0, 0))],
        out_shape=[jax.ShapeDtypeStruct((t, d), F32), jax.ShapeDtypeStruct((1, d), F32),
                   jax.ShapeDtypeStruct((1, d), F32), jax.ShapeDtypeStruct((1, LANES), F32)],
        compiler_params=_cp("arbitrary"),
    )(z, target, g, b)


CONV_TILE = 128
CONV_RB = 32
HALO = 8


def _sum8(x):
    acc = x[0:8]
    for r in range(8, x.shape[0], 8):
        acc = acc + x[r:r + 8]
    return acc


def _conv_fwd(u2, conv_w, conv_b, name):
    t = u2.shape[0]
    tr = _pick(t, (CONV_TILE,))
    hb = tr // HALO
    rb = CONV_RB

    def body(gp_ref, val_ref, prev_ref, w_ref, b_ref, out_ref, ext):
        i = pl.program_id(0)
        ext[0:HALO, :] = jnp.where(i == 0, 0.0, prev_ref[...])
        ext[HALO:, :] = gp_ref[...]
        for c in range(D_FF // LANES):
            ln = slice(c * LANES, (c + 1) * LANES)
            w0, w1, w2, bb = w_ref[0:1, ln], w_ref[1:2, ln], w_ref[2:3, ln], b_ref[:, ln]
            for r0 in range(0, tr, rb):
                gate = (ext[r0 + HALO - 2:r0 + HALO - 2 + rb, ln] * w0 + ext[r0 + HALO - 1:r0 + HALO - 1 + rb, ln] * w1
                        + ext[r0 + HALO:r0 + HALO + rb, ln] * w2 + bb)
                out_ref[r0:r0 + rb, ln] = (gate * _sig(gate) * val_ref[r0:r0 + rb, ln]).astype(out_ref.dtype)

    return pl.pallas_call(
        body, name=name, grid=(t // tr,),
        in_specs=[pl.BlockSpec((tr, D_FF), lambda i: (i, 0)), pl.BlockSpec((tr, D_FF), lambda i: (i, 1)),
                  pl.BlockSpec((HALO, D_FF), lambda i: (jnp.maximum(i * hb - 1, 0), 0)),
                  pl.BlockSpec((3, D_FF), lambda i: (0, 0)), pl.BlockSpec((1, D_FF), lambda i: (0, 0))],
        out_specs=pl.BlockSpec((tr, D_FF), lambda i: (i, 0)),
        out_shape=jax.ShapeDtypeStruct((t, D_FF), BF16),
        scratch_shapes=[pltpu.VMEM((tr + HALO, D_FF), F32)],
        compiler_params=_cp("parallel"),
    )(u2, u2, u2, conv_w, conv_b)


def _conv_bwd(d_hmid, u2, conv_w, conv_b, name):
    t = u2.shape[0]
    tr = _pick(t, (CONV_TILE,))
    hb = tr // HALO
    last = t // HALO - 1
    rb = CONV_RB
    re = rb + HALO

    def body(gp_ref, gp_prev_ref, gp_next_ref, val_ref, val_next_ref, dh_ref, dh_next_ref, w_ref, b_ref,
             du_ref, dw_ref, dcb_ref, ext, dg_s):
        i = pl.program_id(0)

        @pl.when(i == 0)
        def _():
            dw_ref[...] = jnp.zeros_like(dw_ref)
            dcb_ref[...] = jnp.zeros_like(dcb_ref)

        ext[0:HALO, :] = jnp.where(i == 0, 0.0, gp_prev_ref[...])
        ext[HALO:HALO + tr, :] = gp_ref[...]
        ext[HALO + tr:, :] = gp_next_ref[...]
        next_in_seq = (i + 1) * tr < t
        for c in range(D_FF // LANES):
            ln = slice(c * LANES, (c + 1) * LANES)
            w0, w1, w2, bb = w_ref[0:1, ln], w_ref[1:2, ln], w_ref[2:3, ln], b_ref[:, ln]
            acc_b = jnp.zeros((8, LANES), F32)
            acc_w = [jnp.zeros((8, LANES), F32) for _ in range(3)]
            for r0 in range(0, tr, rb):
                g_m2 = ext[r0 + HALO - 2:r0 + HALO - 2 + re, ln]
                g_m1 = ext[r0 + HALO - 1:r0 + HALO - 1 + re, ln]
                g_0 = ext[r0 + HALO:r0 + HALO + re, ln]
                gate = g_m2 * w0 + g_m1 * w1 + g_0 * w2 + bb
                sg = _sig(gate)
                if r0 + re <= tr:
                    val = val_ref[r0:r0 + re, ln]
                    dh = dh_ref[r0:r0 + re, ln]
                else:
                    val = jnp.concatenate([val_ref[r0:r0 + rb, ln], val_next_ref[:, ln]], axis=0)
                    dh = jnp.concatenate([dh_ref[r0:r0 + rb, ln],
                                          jnp.where(next_in_seq, dh_next_ref[:, ln], 0.0)], axis=0)
                dgate = dh * val * _dsilu(gate, sg)
                dg_s[:, ln] = dgate
                dg0 = dgate[0:rb]
                d_gp = dg_s[2:2 + rb, ln] * w0 + dg_s[1:1 + rb, ln] * w1 + dg0 * w2
                du_ref[r0:r0 + rb, ln] = d_gp.astype(du_ref.dtype)
                du_ref[r0:r0 + rb, D_FF + c * LANES:D_FF + (c + 1) * LANES] = (
                    dh[0:rb] * (gate[0:rb] * sg[0:rb])).astype(du_ref.dtype)
                acc_b = acc_b + _sum8(dg0)
                acc_w[0] = acc_w[0] + _sum8(dg0 * g_m2[0:rb])
                acc_w[1] = acc_w[1] + _sum8(dg0 * g_m1[0:rb])
                acc_w[2] = acc_w[2] + _sum8(dg0 * g_0[0:rb])
            dcb_ref[:, ln] += jnp.sum(acc_b, axis=0, keepdims=True)
            for j in range(3):
                dw_ref[j:j + 1, ln] += jnp.sum(acc_w[j], axis=0, keepdims=True)

    cur = lambda col: pl.BlockSpec((tr, D_FF), lambda i: (i, col))
    nxt = lambda col: pl.BlockSpec((HALO, D_FF), lambda i: (jnp.minimum((i + 1) * hb, last), col))
    return pl.pallas_call(
        body, name=name, grid=(t // tr,),
        in_specs=[cur(0), pl.BlockSpec((HALO, D_FF), lambda i: (jnp.maximum(i * hb - 1, 0), 0)), nxt(0),
                  cur(1), nxt(1), cur(0), nxt(0),
                  pl.BlockSpec((3, D_FF), lambda i: (0, 0)), pl.BlockSpec((1, D_FF), lambda i: (0, 0))],
        out_specs=[pl.BlockSpec((tr, 2 * D_FF), lambda i: (i, 0)),
                   pl.BlockSpec((8, D_FF), lambda i: (0, 0)), pl.BlockSpec((1, D_FF), lambda i: (0, 0))],
        out_shape=[jax.ShapeDtypeStruct((t, 2 * D_FF), BF16), jax.ShapeDtypeStruct((8, D_FF), F32),
                   jax.ShapeDtypeStruct((1, D_FF), F32)],
        scratch_shapes=[pltpu.VMEM((tr + 2 * HALO, D_FF), F32), pltpu.VMEM((re, D_FF), F32)],
        compiler_params=_cp("arbitrary"),
    )(u2, u2, u2, u2, u2, d_hmid, d_hmid, conv_w, conv_b)


def _adamw(w, g, m, v, name):
    rows, cols = w.shape
    tr = _pick(rows, (256, 128, 64, 32, 16, 8))

    def body(w_ref, g_ref, m_ref, v_ref, d_ref, nm_ref, nv_ref):
        d_ref[...], nm_ref[...], nv_ref[...] = _adamw_math(w_ref[...], g_ref[...], m_ref[...], v_ref[...])

    spec = pl.BlockSpec((tr, cols), lambda i: (i, 0))
    shp = jax.ShapeDtypeStruct((rows, cols), F32)
    return pl.pallas_call(
        body, name=name, grid=(rows // tr,),
        in_specs=[spec, spec, spec, spec], out_specs=[spec, spec, spec], out_shape=[shp, shp, shp],
        compiler_params=_cp("parallel"),
    )(w, g, m, v)


def _pad_rows(a, rows):
    return jnp.pad(a, ((0, rows - a.shape[0]), (0, 0)))


SMALL_LAYOUT = (("ln1_g", 1024), ("ln1_b", 1024), ("b_in", 2816), ("sinks", 8), ("hgrn_lb", 1024),
                ("hgrn_norm_g", 128), ("ln2_g", 1024), ("ln2_b", 1024), ("conv_b", 2816), ("loss", 1))
SMALL_SHAPES = {"ln1_g": (1, 1024), "ln1_b": (1, 1024), "b_in": (1, 2816), "sinks": (1, 8), "hgrn_lb": (2, 512),
                "hgrn_norm_g": (1, 128), "ln2_g": (1, 1024), "ln2_b": (1, 1024), "conv_b": (1, 2816),
                "loss": (1,)}


def _pack_small(parts):
    rows = []
    for name, size in SMALL_LAYOUT:
        flat = parts[name].reshape(-1).astype(F32)
        padded = -(-size // LANES) * LANES
        rows.append(jnp.pad(flat, (0, padded - size)).reshape(-1, LANES))
    return _pad_rows(jnp.concatenate(rows, axis=0), SMALL_ROWS)


def _unpack_small(pack):
    out, r = {}, 0
    for name, size in SMALL_LAYOUT:
        nrows = -(-size // LANES)
        out[name] = pack[r:r + nrows].reshape(-1)[:size].reshape(SMALL_SHAPES[name])
        r += nrows
    return out


def _own(full, rows):
    return lax.dynamic_slice_in_dim(full, _me() * rows, rows, axis=0)


def kernel(x, positions, ln1_g, ln1_b, w_in, b_in, sinks, hgrn_lb, hgrn_norm_g, w_o, ln2_g, ln2_b, w_up, conv_w, conv_b, w_down, loss_target, m_ln1_g, m_ln1_b, m_w_in, m_b_in, m_sinks, m_hgrn_lb, m_hgrn_norm_g, m_w_o, m_ln2_g, m_ln2_b, m_w_up, m_conv_w, m_conv_b, m_w_down, v_ln1_g, v_ln1_b, v_w_in, v_b_in, v_sinks, v_hgrn_lb, v_hgrn_norm_g, v_w_o, v_ln2_g, v_ln2_b, v_w_up, v_conv_w, v_conv_b, v_w_down):
    t = x.shape[1]
    x2 = x[0]
    xb = x2.astype(BF16)
    target = loss_target[0]
    pos_col = positions.reshape(t, 1)

    w_in_t_s = w_in[0].T.astype(BF16)
    w_up_t_s = w_up[0].T.astype(BF16)
    w_o_s = w_o[0].astype(BF16)
    w_down_s = w_down[0].astype(BF16)
    w_in_t_g, cw_g = _comm_only(_Comm("gather", [w_in_t_s, _pad_rows(conv_w[0], 8)]), "ag_w_in")
    w_in_t = w_in_t_g.reshape(D_FF, D_MODEL)
    w_a_t, w_h_t = w_in_t[:UA_W], w_in_t[UA_W:]
    conv_w_f = cw_g[:, 0:3].transpose(1, 0, 2).reshape(3, D_FF)

    ua = _mm(xb, w_a_t, tb=True, bias=b_in[:, :UA_W], name="fwd_in_attn")
    uh = _mm(xb, w_h_t, tb=True, bias=b_in[:, UA_W:], name="fwd_in_hgrn")
    ctab, stab = _rope_tables(pos_col, "rope_tables")
    (a_out,), (w_o_g,) = _attn_fwd(ua, ctab, stab, sinks, "attn_fwd", comm=_Comm("gather", [w_o_s]))
    (r_out, o_pre, states), (w_up_t_g,) = _hgrn_fwd(uh, hgrn_lb, hgrn_norm_g, "hgrn_fwd",
                                                     comm=_Comm("gather", [w_up_t_s]))
    w_o_f = w_o_g.reshape(D_MODEL, D_MODEL)
    w_up_t = w_up_t_g.reshape(2 * D_FF, D_MODEL)
    z1 = _mm(a_out, w_o_f[:ATTN_W], addend=x2, addend_scale=ALPHA, name="fwd_o_attn")
    z1 = _mm(r_out, w_o_f[ATTN_W:], addend=z1, name="fwd_o_hgrn")
    h1, h1b, xhat1, rstd1 = _ln_fwd(z1, ln1_g, ln1_b, "ln1_fwd")
    u2, (w_down_g,) = _mm(h1b, w_up_t, tb=True, tn=1408, name="fwd_up", comm=_Comm("gather", [w_down_s]))
    w_down_f = w_down_g.reshape(D_FF, D_MODEL)
    hmid = _conv_fwd(u2, conv_w_f, conv_b, "conv_fwd")
    z2 = _mm(hmid, w_down_f, addend=h1, addend_scale=ALPHA, tk=1408, name="fwd_down")
    dz2, d_ln2_g, d_ln2_b, loss_part = _ln_loss_bwd(z2, target, ln2_g, ln2_b, "ln2_loss_bwd")

    d_hmid = _mm(dz2, w_down_f, tb=True, tn=1408, name="bwd_down_dx")
    d_w_down, d_w_down_b = _mm(hmid, dz2, ta=True, out_dtype2=BF16, tm=1408, tk=512, name="bwd_down_dw")
    d_u2, d_conv_w8, d_conv_b = _conv_bwd(d_hmid, u2, conv_w_f, conv_b, "conv_bwd")
    d_h1 = _mm(d_u2, w_up_t, addend=dz2, addend_scale=ALPHA, tk=1408, name="bwd_up_dx")
    d_w_up_t, d_w_up_t_b = _mm(d_u2, h1b, ta=True, out_dtype2=BF16, tm=1408, tk=512, name="bwd_up_dw")
    dz1, d_ln1_g, d_ln1_b = _ln_bwd(d_h1, xhat1, rstd1, ln1_g, "ln1_bwd")
    d_a = _mm(dz1, w_o_f[:ATTN_W], tb=True, name="bwd_o_dx_attn")
    d_r = _mm(dz1, w_o_f[ATTN_W:], tb=True, name="bwd_o_dx_hgrn")
    d_w_o_a, d_w_o_a_b = _mm(a_out, dz1, ta=True, out_dtype2=BF16, tk=512, name="bwd_o_dw_attn")
    d_w_o_r, d_w_o_r_b = _mm(r_out, dz1, ta=True, out_dtype2=BF16, tk=512, name="bwd_o_dw_hgrn")
    d_w_o = jnp.concatenate([d_w_o_a, d_w_o_r], axis=0)
    d_w_o_b = jnp.concatenate([d_w_o_a_b, d_w_o_r_b], axis=0)
    (d_uh, d_bias_h, d_norm_g, d_lb8), (recv_up,) = _hgrn_bwd(
        uh, o_pre, d_r, states, hgrn_lb, hgrn_norm_g, "hgrn_bwd",
        comm=_Comm("exchange", [d_w_up_t_b.reshape(N_DEV, SHARD_UP, D_MODEL)]))
    d_cw_x = d_conv_w8.reshape(8, N_DEV, SHARD_IN).transpose(1, 0, 2)
    (d_ua, d_bias_a, d_sinks), (recv_down, recv_o, recv_cw) = _attn_bwd(
        ua, d_a, ctab, stab, sinks, "attn_bwd",
        comm=_Comm("exchange", [d_w_down_b.reshape(N_DEV, SHARD_DOWN, D_MODEL),
                                d_w_o_b.reshape(N_DEV, SHARD_O, D_MODEL), d_cw_x]))
    d_w_a_t, d_w_a_t_b = _mm(d_ua, xb, ta=True, out_dtype2=BF16, tk=512, name="bwd_in_dw_attn")
    d_w_h_t, d_w_h_t_b = _mm(d_uh, xb, ta=True, out_dtype2=BF16, tk=512, name="bwd_in_dw_hgrn")
    d_w_in_t = jnp.concatenate([d_w_a_t, d_w_h_t], axis=0)
    d_w_in_t_b = jnp.concatenate([d_w_a_t_b, d_w_h_t_b], axis=0)
    dx = _mm(d_ua, w_a_t, addend=dz1, addend_scale=ALPHA, tk=768, name="bwd_in_dx_attn")
    dx, (recv_in,) = _mm(d_uh, w_h_t, addend=dx, name="bwd_in_dx_hgrn",
                         comm=_Comm("exchange", [d_w_in_t_b.reshape(N_DEV, SHARD_IN, D_MODEL)]))

    g_w_in = _sum_shards(recv_in, _own(d_w_in_t, SHARD_IN), "sum_w_in").T
    g_w_up = _sum_shards(recv_up, _own(d_w_up_t, SHARD_UP), "sum_w_up").T
    res_in = (g_w_in,) + tuple(_adamw(w_in[0], g_w_in, m_w_in[0], v_w_in[0], "adamw_w_in"))
    res_up = (g_w_up,) + tuple(_adamw(w_up[0], g_w_up, m_w_up[0], v_w_up[0], "adamw_w_up"))
    res_o = _sum_shards_adamw(recv_o, _own(d_w_o, SHARD_O), w_o[0], m_w_o[0], v_w_o[0], "adamw_w_o")
    res_down = _sum_shards_adamw(recv_down, _own(d_w_down, SHARD_DOWN), w_down[0], m_w_down[0], v_w_down[0],
                                 "adamw_w_down")
    g_cw = _sum_slots(recv_cw, "sum_conv_w")
    cw8 = lambda a: _pad_rows(a, 8)
    res_cw = (g_cw,) + tuple(_adamw(cw8(conv_w[0]), g_cw, cw8(m_conv_w[0]), cw8(v_conv_w[0]), "adamw_conv_w"))
    big = {"w_in": [r[None] for r in res_in], "w_up": [r[None] for r in res_up],
           "w_o": [r[None] for r in res_o], "w_down": [r[None] for r in res_down],
           "conv_w": [r[None, 0:3] for r in res_cw]}

    small_local = _pack_small({
        "ln1_g": d_ln1_g, "ln1_b": d_ln1_b, "b_in": jnp.concatenate([d_bias_a, d_bias_h], axis=1),
        "sinks": d_sinks[:, :8], "hgrn_lb": d_lb8[0:2], "hgrn_norm_g": d_norm_g, "ln2_g": d_ln2_g,
        "ln2_b": d_ln2_b, "conv_b": d_conv_b, "loss": loss_part[:, :1]})
    small_sum = _sum_slots(_all_gather_vmem(small_local, "ar_small"), "ar_small_sum")
    gs = _unpack_small(small_sum)
    loss = gs["loss"][0]
    zero1 = jnp.zeros((1,), F32)
    w_small = _pack_small({"ln1_g": ln1_g, "ln1_b": ln1_b, "b_in": b_in, "sinks": sinks, "hgrn_lb": hgrn_lb,
                           "hgrn_norm_g": hgrn_norm_g, "ln2_g": ln2_g, "ln2_b": ln2_b, "conv_b": conv_b,
                           "loss": zero1})
    m_small = _pack_small({"ln1_g": m_ln1_g, "ln1_b": m_ln1_b, "b_in": m_b_in, "sinks": m_sinks,
                           "hgrn_lb": m_hgrn_lb, "hgrn_norm_g": m_hgrn_norm_g, "ln2_g": m_ln2_g,
                           "ln2_b": m_ln2_b, "conv_b": m_conv_b, "loss": zero1})
    v_small = _pack_small({"ln1_g": v_ln1_g, "ln1_b": v_ln1_b, "b_in": v_b_in, "sinks": v_sinks,
                           "hgrn_lb": v_hgrn_lb, "hgrn_norm_g": v_hgrn_norm_g, "ln2_g": v_ln2_g,
                           "ln2_b": v_ln2_b, "conv_b": v_conv_b, "loss": zero1})
    small = [gs] + [_unpack_small(p) for p in _adamw(w_small, small_sum, m_small, v_small, "adamw_small")]

    order = ["ln1_g", "ln1_b", "w_in", "b_in", "sinks", "hgrn_lb", "hgrn_norm_g", "w_o", "ln2_g", "ln2_b",
             "w_up", "conv_w", "conv_b", "w_down"]

    def pick(idx):
        return [big[n][idx] if n in big else small[idx][n] for n in order]

    return (loss, dx[None], *pick(0), *pick(1), *pick(2), *pick(3))
```

```python
import functools

import jax
import jax.numpy as jnp
import numpy as np
from jax import lax
from jax.experimental import pallas as pl
from jax.experimental.pallas import tpu as pltpu

F32 = jnp.float32
BF16 = jnp.bfloat16

N_DEV = 8
D_MODEL = 1024
D_FF = 2816
ATTN_W = 512
KV_W = 128
UA_W = ATTN_W + 2 * KV_W
UH_W = 2048
HG_W = 512
ATTN_BLOCK = 128
HGRN_CHUNK = 64
HGRN_SUB = 16
HGRN_CHUNKS_PER_STEP = 4
EXP_CLAMP = 85.0
NEG_BIG = -1e30
LN_EPS = 1e-5
RMS_EPS = 1e-6
ALPHA = 2.0 ** 0.25
ATTN_SCALE = 0.125
ROPE_THETA = 500000.0

ADAM_LR = 0.001
ADAM_B1 = 0.9
ADAM_B2 = 0.999
ADAM_EPS = 1e-08
ADAM_WD = 0.01
ADAM_STEP = 10

LANES = 128
VMEM_LIMIT_BYTES = 56 * 1024 * 1024

SHARD_IN = D_FF // N_DEV
SHARD_UP = 2 * D_FF // N_DEV
SHARD_O = D_MODEL // N_DEV
SHARD_DOWN = D_FF // N_DEV
SMALL_ROWS = 88

_MESH = pl.DeviceIdType.MESH
_NT = (((1,), (1,)), ((), ()))
_NN = (((1,), (0,)), ((), ()))
_TN = (((0,), (0,)), ((), ()))


def _cp(*sem):
    if sem:
        return pltpu.CompilerParams(dimension_semantics=sem, vmem_limit_bytes=VMEM_LIMIT_BYTES)
    return pltpu.CompilerParams(vmem_limit_bytes=VMEM_LIMIT_BYTES)


def _sig(x):
    return 1.0 / (1.0 + jnp.exp(-x))


def _dsilu(x, s):
    return s * (1.0 + x * (1.0 - s))


def _dot(a, b, dims):
    return lax.dot_general(a.astype(BF16), b.astype(BF16), dims, preferred_element_type=F32)


def _split(a):
    hi = a.astype(BF16)
    return hi, (a - hi.astype(F32)).astype(BF16)


def _dot3(a, b, dims):
    ah, al = _split(a)
    bh, bl = _split(b)
    d = functools.partial(lax.dot_general, dimension_numbers=dims, preferred_element_type=F32)
    return d(ah, bh) + (d(ah, bl) + d(al, bh))


def _pick(n, pref):
    for t in pref:
        if t <= n and n % t == 0:
            return t
    return n


def _my_coords():
    return lax.axis_index("x"), lax.axis_index("y"), lax.axis_index("c")


def _peer(k):
    x, y, c = _my_coords()
    return (1 - x if k & 4 else x, 1 - y if k & 2 else y, 1 - c if k & 1 else c)


def _me():
    x, y, c = _my_coords()
    return 4 * x + 2 * y + c


class _Comm:
    def __init__(self, kind, arrays):
        self.kind, self.arrays, self.n = kind, list(arrays), len(arrays)

    def out_shapes(self):
        if self.kind == "gather":
            return [jax.ShapeDtypeStruct((N_DEV,) + a.shape, a.dtype) for a in self.arrays]
        return [jax.ShapeDtypeStruct(a.shape, a.dtype) for a in self.arrays]

    def specs(self):
        return [pl.BlockSpec(memory_space=pl.ANY)] * self.n

    def scratch(self):
        return [pltpu.SemaphoreType.DMA(((N_DEV - 1) * self.n,)), pltpu.SemaphoreType.DMA(((N_DEV - 1) * self.n,)),
                pltpu.SemaphoreType.DMA((self.n,))]

    def _src(self, ref, dev):
        return ref if self.kind == "gather" else ref.at[dev]

    def _copy(self, a, k, src, dst, sems, me, slot):
        other = jnp.bitwise_xor(me, k)
        idx = a * (N_DEV - 1) + k - 1
        return pltpu.make_async_remote_copy(
            src_ref=self._src(src, other), dst_ref=dst.at[me if slot == "mine" else other],
            send_sem=sems[0].at[idx], recv_sem=sems[1].at[idx], device_id=_peer(k), device_id_type=_MESH)

    def _pass_on(self, a, k, dst, sems, me):
        slot = dst.at[jnp.bitwise_xor(me, k)]
        idx = a * (N_DEV - 1) + k
        return pltpu.make_async_remote_copy(
            src_ref=slot, dst_ref=slot, send_sem=sems[0].at[idx], recv_sem=sems[1].at[idx],
            device_id=_peer(1), device_id_type=_MESH)

    def start(self, srcs, dsts, sems):
        me = _me()
        direct = (1, 2, 4, 6) if self.kind == "gather" else range(1, N_DEV)
        for a, (src, dst) in enumerate(zip(srcs, dsts)):
            pltpu.make_async_copy(self._src(src, me), dst.at[me], sems[2].at[a]).start()
            for k in direct:
                self._copy(a, k, src, dst, sems, me, "mine").start()

    def wait(self, srcs, dsts, sems):
        me = _me()
        for a, (src, dst) in enumerate(zip(srcs, dsts)):
            if self.kind == "gather":
                for k in (2, 4, 6):
                    self._copy(a, k, src, dst, sems, me, "theirs").wait_recv()
                    self._pass_on(a, k, dst, sems, me).start()
                for k in (1, 3, 5, 7):
                    self._copy(a, k, src, dst, sems, me, "theirs").wait_recv()
                for k in (1, 2, 4, 6):
                    self._copy(a, k, src, dst, sems, me, "mine").wait_send()
                for k in (2, 4, 6):
                    self._pass_on(a, k, dst, sems, me).wait_send()
            else:
                for k in range(1, N_DEV):
                    self._copy(a, k, src, dst, sems, me, "theirs").wait_recv()
                for k in range(1, N_DEV):
                    self._copy(a, k, src, dst, sems, me, "mine").wait_send()
            pltpu.make_async_copy(self._src(src, me), dst.at[me], sems[2].at[a]).wait()


def _call(body, *, name, grid, ins, in_specs, out_specs, out_shape, scratch_shapes=(), sem, comm=None):
    n_in, n_out, n_scr = len(ins), len(out_shape), len(scratch_shapes)
    if comm is None:
        outs = pl.pallas_call(
            body, name=name, grid=grid, in_specs=list(in_specs), out_specs=list(out_specs),
            out_shape=list(out_shape), scratch_shapes=list(scratch_shapes), compiler_params=_cp(*sem))(*ins)
        return list(outs), []
    nc = comm.n

    def hosted(*refs):
        pos = n_in
        c_in = refs[pos:pos + nc]
        pos += nc
        outs = refs[pos:pos + n_out]
        pos += n_out
        c_out = refs[pos:pos + nc]
        pos += nc
        scr = refs[pos:pos + n_scr]
        sems = refs[pos + n_scr:]
        ids = [pl.program_id(d) for d in range(len(grid))]
        first = functools.reduce(jnp.logical_and, [i == 0 for i in ids])
        last = functools.reduce(jnp.logical_and, [i == g - 1 for i, g in zip(ids, grid)])

        @pl.when(first)
        def _():
            comm.start(c_in, c_out, sems)

        body(*refs[:n_in], *outs, *scr)

        @pl.when(last)
        def _():
            comm.wait(c_in, c_out, sems)

    outs = pl.pallas_call(
        hosted, name=name, grid=grid, in_specs=list(in_specs) + comm.specs(),
        out_specs=list(out_specs) + comm.specs(), out_shape=list(out_shape) + comm.out_shapes(),
        scratch_shapes=list(scratch_shapes) + comm.scratch(),
        compiler_params=_cp(*(["arbitrary"] * len(grid))))(*ins, *comm.arrays)
    return list(outs[:n_out]), list(outs[n_out:])


def _comm_only(comm, name):
    def body(*refs):
        srcs, dsts, sems = refs[:comm.n], refs[comm.n:2 * comm.n], refs[2 * comm.n:]
        comm.start(srcs, dsts, sems)
        comm.wait(srcs, dsts, sems)

    return list(pl.pallas_call(
        body, name=name, in_specs=comm.specs(), out_specs=comm.specs(), out_shape=comm.out_shapes(),
        scratch_shapes=comm.scratch(), compiler_params=_cp())(*comm.arrays))


def _all_gather_vmem(x, name):
    def body(x_ref, out_ref, send_sems, recv_sems, local_sem):
        me = _me()
        local = pltpu.make_async_copy(x_ref, out_ref.at[me], local_sem)
        local.start()
        sends = []
        for k in range(1, N_DEV):
            cp = pltpu.make_async_remote_copy(
                src_ref=x_ref, dst_ref=out_ref.at[me], send_sem=send_sems.at[k - 1],
                recv_sem=recv_sems.at[k - 1], device_id=_peer(k), device_id_type=_MESH)
            cp.start()
            sends.append(cp)
        for k in range(1, N_DEV):
            pltpu.make_async_remote_copy(
                src_ref=x_ref, dst_ref=out_ref.at[jnp.bitwise_xor(me, k)], send_sem=send_sems.at[k - 1],
                recv_sem=recv_sems.at[k - 1], device_id=_peer(k), device_id_type=_MESH).wait_recv()
        for cp in sends:
            cp.wait_send()
        local.wait()

    return pl.pallas_call(
        body, name=name,
        out_shape=jax.ShapeDtypeStruct((N_DEV,) + x.shape, x.dtype),
        in_specs=[pl.BlockSpec(memory_space=pltpu.VMEM)],
        out_specs=pl.BlockSpec(memory_space=pltpu.VMEM),
        scratch_shapes=[pltpu.SemaphoreType.DMA((N_DEV - 1,)), pltpu.SemaphoreType.DMA((N_DEV - 1,)),
                        pltpu.SemaphoreType.DMA],
        compiler_params=_cp(),
    )(x)


def _sum_slots(gathered, name):
    _, rows, cols = gathered.shape

    def body(g_ref, out_ref):
        acc = g_ref[0]
        for s in range(1, N_DEV):
            acc = acc + g_ref[s]
        out_ref[...] = acc

    return pl.pallas_call(
        body, name=name,
        out_shape=jax.ShapeDtypeStruct((rows, cols), F32),
        compiler_params=_cp(),
    )(gathered)


def _slot_sum(recv_ref, own_ref, shape):
    me = _me()
    acc = jnp.zeros(shape, F32)
    for s in range(N_DEV):
        acc = acc + jnp.where(me == s, own_ref[...], recv_ref[s].astype(F32))
    return acc


def _adamw_math(w, g, m, v):
    nm = ADAM_B1 * m + (1.0 - ADAM_B1) * g
    nv = ADAM_B2 * v + (1.0 - ADAM_B2) * (g * g)
    m_hat = nm / (1.0 - ADAM_B1 ** ADAM_STEP)
    v_hat = nv / (1.0 - ADAM_B2 ** ADAM_STEP)
    return -ADAM_LR * (m_hat / (jnp.sqrt(v_hat) + ADAM_EPS) + ADAM_WD * w), nm, nv


def _sum_shards(recv, own, name):
    _, rows, cols = recv.shape
    tr = _pick(rows, (256, 128, 176, 64, 32, 16, 8))

    def body(recv_ref, own_ref, out_ref):
        out_ref[...] = _slot_sum(recv_ref, own_ref, (tr, cols))

    return pl.pallas_call(
        body, name=name, grid=(rows // tr,),
        in_specs=[pl.BlockSpec((N_DEV, tr, cols), lambda i: (0, i, 0)), pl.BlockSpec((tr, cols), lambda i: (i, 0))],
        out_specs=pl.BlockSpec((tr, cols), lambda i: (i, 0)),
        out_shape=jax.ShapeDtypeStruct((rows, cols), F32),
        compiler_params=_cp("parallel"),
    )(recv, own)


def _sum_shards_adamw(recv, own, w, m, v, name):
    _, rows, cols = recv.shape
    tr = _pick(rows, (128, 176, 64, 32, 16, 8))

    def body(recv_ref, own_ref, w_ref, m_ref, v_ref, g_ref, d_ref, nm_ref, nv_ref):
        g = _slot_sum(recv_ref, own_ref, (tr, cols))
        g_ref[...] = g
        d_ref[...], nm_ref[...], nv_ref[...] = _adamw_math(w_ref[...], g, m_ref[...], v_ref[...])

    spec = pl.BlockSpec((tr, cols), lambda i: (i, 0))
    shp = jax.ShapeDtypeStruct((rows, cols), F32)
    return pl.pallas_call(
        body, name=name, grid=(rows // tr,),
        in_specs=[pl.BlockSpec((N_DEV, tr, cols), lambda i: (0, i, 0)), spec, spec, spec, spec],
        out_specs=[spec, spec, spec, spec], out_shape=[shp, shp, shp, shp],
        compiler_params=_cp("parallel"),
    )(recv, own, w, m, v)


def _mm(a, b, *, name, ta=False, tb=False, out_dtype=F32, out_dtype2=None, bias=None, addend=None,
        addend_scale=1.0, tm=1024, tn=1024, tk=1024, comm=None):
    kdim, m = a.shape if ta else a.shape[::-1]
    n = b.shape[0] if tb else b.shape[1]
    tm = _pick(m, (tm, 1408, 1024, 768, 512, 256, 128))
    tn = _pick(n, (tn, 1408, 1024, 768, 512, 256, 128))
    tk = _pick(kdim, (tk, 1408, 1024, 768, 512, 256, 128))
    nk = kdim // tk
    a_spec = pl.BlockSpec((tk, tm), lambda i, j, k: (k, i)) if ta else pl.BlockSpec((tm, tk), lambda i, j, k: (i, k))
    b_spec = pl.BlockSpec((tn, tk), lambda i, j, k: (j, k)) if tb else pl.BlockSpec((tk, tn), lambda i, j, k: (k, j))
    ins, specs = [a, b], [a_spec, b_spec]
    if bias is not None:
        ins.append(bias)
        specs.append(pl.BlockSpec((1, tn), lambda i, j, k: (0, j)))
    if addend is not None:
        ins.append(addend)
        specs.append(pl.BlockSpec((tm, tn), lambda i, j, k: (i, j)))
    dims = (((0,) if ta else (1,), (1,) if tb else (0,)), ((), ()))
    has_bias, has_addend, two = bias is not None, addend is not None, out_dtype2 is not None

    def body(*refs):
        a_ref, b_ref = refs[0], refs[1]
        pos = 2
        bias_ref = addend_ref = None
        if has_bias:
            bias_ref = refs[pos]
            pos += 1
        if has_addend:
            addend_ref = refs[pos]
            pos += 1
        o_refs, acc_ref = refs[pos:-1], refs[-1]
        k = pl.program_id(2)

        @pl.when(k == 0)
        def _():
            acc_ref[...] = jnp.zeros_like(acc_ref)

        acc_ref[...] += _dot(a_ref[...], b_ref[...], dims)

        @pl.when(k == nk - 1)
        def _():
            r = acc_ref[...]
            if has_bias:
                r = r + bias_ref[...]
            if has_addend:
                r = r + addend_scale * addend_ref[...].astype(F32)
            for o_ref in o_refs:
                o_ref[...] = r.astype(o_ref.dtype)

    ospec = pl.BlockSpec((tm, tn), lambda i, j, k: (i, j))
    dtypes = [out_dtype] + ([out_dtype2] if two else [])
    outs, couts = _call(
        body, name=name, grid=(m // tm, n // tn, nk), ins=ins, in_specs=specs,
        out_specs=[ospec] * len(dtypes), out_shape=[jax.ShapeDtypeStruct((m, n), d) for d in dtypes],
        scratch_shapes=[pltpu.VMEM((tm, tn), F32)], sem=("parallel", "parallel", "arbitrary"), comm=comm)
    primary = tuple(outs) if two else outs[0]
    return (primary, couts) if comm is not None else primary


def _rope_lane_constants():
    inv_freq = np.float32(ROPE_THETA) ** (-np.arange(8, dtype=np.float32) * np.float32(2.0 / 16.0))
    lane = np.arange(LANES) % 64
    freq = np.where(lane < 16, inv_freq[lane % 8], 0.0).astype(np.float32)
    sign = np.where(lane < 8, -1.0, np.where(lane < 16, 1.0, 0.0)).astype(np.float32)
    return jnp.asarray(freq)[None, :], jnp.asarray(sign)[None, :]


def _rope_tables(pos_col, name):
    t = pos_col.shape[0]
    tr = _pick(t, (512, 256, 128))
    freq, sign = _rope_lane_constants()

    def body(pos_ref, freq_ref, sign_ref, c_ref, s_ref):
        ang = pos_ref[...].astype(F32) * freq_ref[...]
        c_ref[...] = jnp.cos(ang)
        s_ref[...] = sign_ref[...] * jnp.sin(ang)

    return pl.pallas_call(
        body, name=name, grid=(t // tr,),
        in_specs=[pl.BlockSpec((tr, 1), lambda i: (i, 0)),
                  pl.BlockSpec((1, LANES), lambda i: (0, 0)),
                  pl.BlockSpec((1, LANES), lambda i: (0, 0))],
        out_specs=[pl.BlockSpec((tr, LANES), lambda i: (i, 0)), pl.BlockSpec((tr, LANES), lambda i: (i, 0))],
        out_shape=[jax.ShapeDtypeStruct((t, LANES), F32), jax.ShapeDtypeStruct((t, LANES), F32)],
        compiler_params=_cp("parallel"),
    )(pos_col, freq, sign)


def _swap8(t):
    width = t.shape[1]
    lane = jnp.bitwise_and(lax.broadcasted_iota(jnp.int32, t.shape, 1), 63)
    return jnp.where(lane < 8, pltpu.roll(t, width - 8, 1), jnp.where(lane < 16, pltpu.roll(t, 8, 1), 0.0))


def _rope(t, c, s):
    return t * c + _swap8(t) * s


def _rope_bwd(d, c, s):
    return d * c + _swap8(d * s)


def _tile4(a):
    return jnp.concatenate([a, a, a, a], axis=1)


def _attn_band(n, k_cur, k_prev, v_cur, v_prev, c_cur, s_cur, c_prev, s_prev):
    kband = jnp.concatenate([_rope(k_prev, c_prev, s_prev), _rope(k_cur, c_cur, s_cur)], axis=0)
    vband = jnp.concatenate([v_prev, v_cur], axis=0)
    qi = lax.broadcasted_iota(jnp.int32, (ATTN_BLOCK, 2 * ATTN_BLOCK), 0)
    kj = lax.broadcasted_iota(jnp.int32, (ATTN_BLOCK, 2 * ATTN_BLOCK), 1)
    dist = qi + ATTN_BLOCK - kj
    valid = (dist >= 0) & (dist < ATTN_BLOCK) & (n * ATTN_BLOCK - ATTN_BLOCK + kj >= 0)
    return (kband.astype(BF16), pltpu.roll(kband, 64, 1).astype(BF16),
            vband.astype(BF16), pltpu.roll(vband, 64, 1).astype(BF16), valid)


def _attn_probs(raw, valid, sink, axis):
    s = jnp.where(valid, raw * ATTN_SCALE, NEG_BIG)
    m = jnp.maximum(jnp.max(s, axis=axis, keepdims=True), sink)
    p = jnp.exp(s - m)
    esink = jnp.exp(sink - m)
    z = jnp.sum(p, axis=axis, keepdims=True) + esink
    return p / z, esink / z


def _attn_valid_t(n):
    kj = lax.broadcasted_iota(jnp.int32, (2 * ATTN_BLOCK, ATTN_BLOCK), 0)
    qi = lax.broadcasted_iota(jnp.int32, (2 * ATTN_BLOCK, ATTN_BLOCK), 1)
    dist = qi + ATTN_BLOCK - kj
    return (dist >= 0) & (dist < ATTN_BLOCK) & (n * ATTN_BLOCK - ATTN_BLOCK + kj >= 0)


def _attn_specs(nb):
    def cur(col, width=KV_W):
        return pl.BlockSpec((ATTN_BLOCK, width), lambda n: (jnp.minimum(n, nb - 1), col))

    def prev(col):
        return pl.BlockSpec((ATTN_BLOCK, KV_W), lambda n: (jnp.maximum(n - 1, 0), col))

    ua_specs = [cur(0, ATTN_W), cur(4), prev(4), cur(5), prev(5)]
    tab_specs = [cur(0), cur(0), prev(0), prev(0)]
    return ua_specs, tab_specs


def _attn_fwd(ua, ctab, stab, sinks, name, comm=None):
    t = ua.shape[0]
    nb = t // ATTN_BLOCK
    ua_specs, tab_specs = _attn_specs(nb)

    def body(q_ref, kc_ref, kp_ref, vc_ref, vp_ref, cc_ref, sc_ref, cp_ref, sp_ref, sink_ref, o_ref):
        n = pl.program_id(0)
        cc, sc = cc_ref[...], sc_ref[...]
        kb, kb_r, vb, vb_r, valid = _attn_band(n, kc_ref[...], kp_ref[...], vc_ref[...], vp_ref[...],
                                               cc, sc, cp_ref[...], sp_ref[...])
        qr = _rope(q_ref[...], _tile4(cc), _tile4(sc))
        lo = lax.broadcasted_iota(jnp.int32, (ATTN_BLOCK, LANES), 1) < 64
        heads = []
        for j in range(4):
            qj = qr[:, j * LANES:(j + 1) * LANES]
            for is_lo in (True, False):
                aligned = is_lo == (j < 2)
                qm = jnp.where(lo if is_lo else jnp.logical_not(lo), qj, 0.0).astype(BF16)
                raw = lax.dot_general(qm, kb if aligned else kb_r, _NT, preferred_element_type=F32)
                heads.append((raw, vb if aligned else vb_r, sink_ref[0, len(heads)]))
        halves = []
        for raw, vv, sink in heads:
            probs, _ = _attn_probs(raw, valid, sink, 1)
            halves.append(lax.dot_general(probs.astype(BF16), vv, _NN, preferred_element_type=F32))
        outs = [jnp.where(lo, halves[2 * j], halves[2 * j + 1]) for j in range(4)]
        o_ref[...] = jnp.concatenate(outs, axis=1).astype(o_ref.dtype)

    return _call(
        body, name=name, grid=(nb,), ins=[ua, ua, ua, ua, ua, ctab, stab, ctab, stab, sinks],
        in_specs=ua_specs + tab_specs + [pl.BlockSpec(memory_space=pltpu.SMEM)],
        out_specs=[pl.BlockSpec((ATTN_BLOCK, ATTN_W), lambda n: (n, 0))],
        out_shape=[jax.ShapeDtypeStruct((t, ATTN_W), BF16)], sem=("parallel",), comm=comm)


def _attn_bwd(ua, d_out, ctab, stab, sinks, name, comm=None):
    t = ua.shape[0]
    nb = t // ATTN_BLOCK
    ua_specs, tab_specs = _attn_specs(nb)

    def body(q_ref, kc_ref, kp_ref, vc_ref, vp_ref, cc_ref, sc_ref, cp_ref, sp_ref, do_ref, sink_ref,
             dua_ref, dbias_ref, dsink_ref, dq_c, dk_c, dv_c, dq_n, dk_n, dv_n):
        n = pl.program_id(0)

        @pl.when(n == 0)
        def _():
            dq_c[...] = jnp.zeros_like(dq_c)
            dk_c[...] = jnp.zeros_like(dk_c)
            dv_c[...] = jnp.zeros_like(dv_c)
            dbias_ref[...] = jnp.zeros_like(dbias_ref)
            dsink_ref[...] = jnp.zeros_like(dsink_ref)

        @pl.when(n == nb)
        def _():
            dq_n[...] = jnp.zeros_like(dq_n)
            dk_n[...] = jnp.zeros_like(dk_n)
            dv_n[...] = jnp.zeros_like(dv_n)

        @pl.when(n < nb)
        def _():
            cc, sc = cc_ref[...], sc_ref[...]
            kb, kb_r, vb, vb_r, valid = _attn_band(n, kc_ref[...], kp_ref[...], vc_ref[...], vp_ref[...],
                                                   cc, sc, cp_ref[...], sp_ref[...])
            valid_t = _attn_valid_t(n)
            c4, s4 = _tile4(cc), _tile4(sc)
            qr = _rope(q_ref[...], c4, s4)
            do = do_ref[...].astype(F32)
            lane = lax.broadcasted_iota(jnp.int32, (ATTN_BLOCK, LANES), 1)
            lo = lane < 64
            lane_row = lax.broadcasted_iota(jnp.int32, (1, LANES), 1)
            heads = []
            for j in range(4):
                qj = qr[:, j * LANES:(j + 1) * LANES]
                doj = do[:, j * LANES:(j + 1) * LANES]
                for is_lo in (True, False):
                    aligned = is_lo == (j < 2)
                    msk = lo if is_lo else jnp.logical_not(lo)
                    kk = kb if aligned else kb_r
                    vv = vb if aligned else vb_r
                    qm = jnp.where(msk, qj, 0.0).astype(BF16)
                    dom = jnp.where(msk, doj, 0.0).astype(BF16)
                    heads.append(dict(
                        aligned=aligned, kk=kk, qm=qm, dom=dom, sink=sink_ref[0, len(heads)],
                        raw=lax.dot_general(qm, kk, _NT, preferred_element_type=F32),
                        dp=lax.dot_general(dom, vv, _NT, preferred_element_type=F32),
                        raw_t=lax.dot_general(kk, qm, _NT, preferred_element_type=F32),
                        dp_t=lax.dot_general(vv, dom, _NT, preferred_element_type=F32)))
            dk_band = jnp.zeros((2 * ATTN_BLOCK, LANES), F32)
            dv_band = jnp.zeros((2 * ATTN_BLOCK, LANES), F32)
            dsink = jnp.zeros((1, LANES), F32)
            halves = []
            for head, hd in enumerate(heads):
                probs, psink = _attn_probs(hd["raw"], valid, hd["sink"], 1)
                delta = jnp.sum(probs * hd["dp"], axis=1, keepdims=True)
                ds = (probs * (hd["dp"] - delta) * ATTN_SCALE).astype(BF16)
                dsink = dsink + jnp.where(lane_row == head, -jnp.sum(psink * delta), 0.0)
                halves.append(lax.dot_general(ds, hd["kk"], _NN, preferred_element_type=F32))
                probs_t, _ = _attn_probs(hd["raw_t"], valid_t, hd["sink"], 0)
                delta_t = jnp.sum(probs_t * hd["dp_t"], axis=0, keepdims=True)
                ds_t = (probs_t * (hd["dp_t"] - delta_t) * ATTN_SCALE).astype(BF16)
                dk_h = lax.dot_general(ds_t, hd["qm"], _NN, preferred_element_type=F32)
                dv_h = lax.dot_general(probs_t.astype(BF16), hd["dom"], _NN, preferred_element_type=F32)
                if not hd["aligned"]:
                    dk_h = pltpu.roll(dk_h, 64, 1)
                    dv_h = pltpu.roll(dv_h, 64, 1)
                dk_band = dk_band + dk_h
                dv_band = dv_band + dv_h
            dqs = [jnp.where(lo, halves[2 * j], halves[2 * j + 1]) for j in range(4)]
            dq_n[...] = _rope_bwd(jnp.concatenate(dqs, axis=1), c4, s4)
            dk_n[...] = dk_band
            dv_n[...] = dv_band
            dsink_ref[...] += dsink

        dk_prev = _rope_bwd(dk_c[...] + dk_n[0:ATTN_BLOCK, :], cp_ref[...], sp_ref[...])
        dv_prev = dv_c[...] + dv_n[0:ATTN_BLOCK, :]
        full = jnp.concatenate([dq_c[...], dk_prev, dv_prev], axis=1)
        dua_ref[...] = full.astype(dua_ref.dtype)
        dbias_ref[...] += jnp.sum(full, axis=0, keepdims=True)
        dq_c[...] = dq_n[...]
        dk_c[...] = dk_n[ATTN_BLOCK:, :]
        dv_c[...] = dv_n[ATTN_BLOCK:, :]

    return _call(
        body, name=name, grid=(nb + 1,), ins=[ua, ua, ua, ua, ua, ctab, stab, ctab, stab, d_out, sinks],
        in_specs=ua_specs + tab_specs + [
            pl.BlockSpec((ATTN_BLOCK, ATTN_W), lambda n: (jnp.minimum(n, nb - 1), 0)),
            pl.BlockSpec(memory_space=pltpu.SMEM)],
        out_specs=[pl.BlockSpec((ATTN_BLOCK, UA_W), lambda n: (jnp.maximum(n - 1, 0), 0)),
                   pl.BlockSpec((1, UA_W), lambda n: (0, 0)),
                   pl.BlockSpec((1, LANES), lambda n: (0, 0))],
        out_shape=[jax.ShapeDtypeStruct((t, UA_W), BF16),
                   jax.ShapeDtypeStruct((1, UA_W), F32),
                   jax.ShapeDtypeStruct((1, LANES), F32)],
        scratch_shapes=[pltpu.VMEM((ATTN_BLOCK, ATTN_W), F32), pltpu.VMEM((ATTN_BLOCK, KV_W), F32),
                        pltpu.VMEM((ATTN_BLOCK, KV_W), F32), pltpu.VMEM((ATTN_BLOCK, ATTN_W), F32),
                        pltpu.VMEM((2 * ATTN_BLOCK, KV_W), F32), pltpu.VMEM((2 * ATTN_BLOCK, KV_W), F32)],
        sem=("arbitrary",), comm=comm)


def _tri_mats():
    r = lax.broadcasted_iota(jnp.int32, (HGRN_CHUNK, LANES), 0)
    c = lax.broadcasted_iota(jnp.int32, (HGRN_CHUNK, LANES), 1)
    lower = ((c <= r) & (c < HGRN_CHUNK)).astype(F32)
    upper = ((c >= r) & (c < HGRN_CHUNK)).astype(F32)
    return lower, upper


def _tri_apply(tri, g):
    pad = jnp.concatenate([g, jnp.zeros_like(g)], axis=0)
    return lax.dot_general(tri, pad, _NN, precision=lax.Precision.HIGHEST, preferred_element_type=F32)


def _sub_masks():
    s = lax.broadcasted_iota(jnp.int32, (HGRN_CHUNK, LANES), 0)
    tt = lax.broadcasted_iota(jnp.int32, (HGRN_CHUNK, LANES), 1)
    return [(tt >= HGRN_SUB * i) & (tt < HGRN_SUB * (i + 1)) & (s <= tt) for i in range(HGRN_CHUNK // HGRN_SUB)]


def _hgrn_gates(hq, hf, lb_ref, b_scr):
    lb = _sig(lb_ref[0:1, :] - lb_ref[1:2, :])
    q = hq * _sig(hq)
    sg = _sig(hf)
    f = lb + (1.0 - lb) * sg
    k = 1.0 - f
    lower, _ = _tri_mats()
    b = _tri_apply(lower, jnp.log(f))
    b_scr[...] = b
    nsub = HGRN_CHUNK // HGRN_SUB
    starts = [jnp.zeros((1, HG_W), F32)] + [b_scr[HGRN_SUB * i - 1:HGRN_SUB * i, :] for i in range(1, nsub)]
    pq = jnp.concatenate([jnp.broadcast_to(p, (HGRN_SUB, HG_W)) for p in starts], axis=0)
    b_last = b_scr[HGRN_CHUNK - 1:HGRN_CHUNK, :]
    e_q = jnp.exp(b - pq)
    e_k = [jnp.exp(jnp.minimum(p - b, EXP_CLAMP)) for p in starts]
    e_b = jnp.exp(b)
    e_bl = jnp.exp(b_last - b)
    e_last = jnp.exp(b_last)
    return q, sg, f, k, lb, e_q, e_k, e_b, e_bl, e_last


def _sub_masks_ts():
    tt = lax.broadcasted_iota(jnp.int32, (HGRN_CHUNK, LANES), 0)
    s = lax.broadcasted_iota(jnp.int32, (HGRN_CHUNK, LANES), 1)
    return [(tt >= HGRN_SUB * i) & (tt < HGRN_SUB * (i + 1)) & (s <= tt) for i in range(HGRN_CHUNK // HGRN_SUB)]


def _masked_sum(blocks, masks, axis):
    step = HGRN_CHUNK if axis == 0 else LANES
    acc = jnp.zeros((HGRN_CHUNK, LANES), F32)
    for i, msk in enumerate(masks):
        blk = blocks[step * i:step * (i + 1), :] if axis == 0 else blocks[:, step * i:step * (i + 1)]
        acc = acc + jnp.where(msk, blk, 0.0)
    return acc


def _hgrn_chunk_inputs(j, hq_ref, hf_ref, hi_ref, hg_ref, lb_ref, b_scr):
    rows = slice(j * HGRN_CHUNK, (j + 1) * HGRN_CHUNK)
    hq, hf, v, hg = hq_ref[rows, :], hf_ref[rows, :], hi_ref[rows, :], hg_ref[rows, :]
    q, sg, f, k, lb, e_q, e_k, e_b, e_bl, e_last = _hgrn_gates(hq, hf, lb_ref, b_scr.at[j])
    return dict(rows=rows, hq=hq, v=v, hg=hg, q=q, sg=sg, f=f, k=k, lb=lb, e_q=e_q, e_k=e_k, e_b=e_b, e_bl=e_bl,
                e_last=e_last, qt=q * e_q, qb=q * e_b, kd=k * e_bl, khat=[k * e for e in e_k])


def _hgrn_fwd(uh, lb_raw, norm_g, name, comm=None):
    t = uh.shape[0]
    nc = t // HGRN_CHUNK
    cps = _pick(nc, (HGRN_CHUNKS_PER_STEP, 2, 1))
    rows_step = cps * HGRN_CHUNK

    def body(hq_ref, hf_ref, hi_ref, hg_ref, lb_ref, ng_ref, r_ref, o_ref, st_out_ref, st_ref, b_scr):
        @pl.when(pl.program_id(0) == 0)
        def _():
            st_ref[...] = jnp.zeros_like(st_ref)

        masks = _sub_masks_ts()
        ng = ng_ref[...]
        zpad = jnp.zeros((HGRN_CHUNK, LANES), F32)
        heads = [slice(h * LANES, (h + 1) * LANES) for h in range(4)]
        chunks = [_hgrn_chunk_inputs(j, hq_ref, hf_ref, hi_ref, hg_ref, lb_ref, b_scr) for j in range(cps)]
        for ch in chunks:
            ch["scores"] = [_dot3(ch["qt"][:, sl],
                                  jnp.concatenate([x for kh in ch["khat"] for x in (kh[:, sl], zpad)], axis=0), _NT)
                            for sl in heads]
        for j, ch in enumerate(chunks):
            o_heads, y_heads = [], []
            for h, sl in enumerate(heads):
                a_ts = _masked_sum(ch["scores"][h], masks, 1)
                vh = ch["v"][:, sl].astype(BF16)
                v_pad = jnp.concatenate([vh, jnp.zeros_like(vh)], axis=0)
                o_intra = lax.dot_general(a_ts.astype(BF16), v_pad, _NN, preferred_element_type=F32)
                st = st_ref[h]
                st_out_ref[j, h] = st
                o_inter = _dot(ch["qb"][:, sl], st, _NT)
                st_ref[h] = st * ch["e_last"][:, sl] + _dot(vh, ch["kd"][:, sl], _TN)
                oh = o_intra + o_inter
                rs = lax.rsqrt(jnp.mean(oh * oh, axis=1, keepdims=True) + RMS_EPS)
                o_heads.append(oh)
                y_heads.append(oh * rs * ng)
            hg = ch["hg"]
            o_ref[ch["rows"], :] = jnp.concatenate(o_heads, axis=1)
            r_ref[ch["rows"], :] = (jnp.concatenate(y_heads, axis=1) * (hg * _sig(hg))).astype(r_ref.dtype)

    col = lambda j: pl.BlockSpec((rows_step, HG_W), lambda c: (c, j))
    return _call(
        body, name=name, grid=(nc // cps,), ins=[uh, uh, uh, uh, lb_raw, norm_g],
        in_specs=[col(0), col(1), col(2), col(3),
                  pl.BlockSpec((2, HG_W), lambda c: (0, 0)), pl.BlockSpec((1, LANES), lambda c: (0, 0))],
        out_specs=[pl.BlockSpec((rows_step, HG_W), lambda c: (c, 0)),
                   pl.BlockSpec((rows_step, HG_W), lambda c: (c, 0)),
                   pl.BlockSpec((cps, 4, LANES, LANES), lambda c: (c, 0, 0, 0))],
        out_shape=[jax.ShapeDtypeStruct((t, HG_W), BF16), jax.ShapeDtypeStruct((t, HG_W), F32),
                   jax.ShapeDtypeStruct((nc, 4, LANES, LANES), F32)],
        scratch_shapes=[pltpu.VMEM((4, LANES, LANES), F32), pltpu.VMEM((cps, HGRN_CHUNK, HG_W), F32)],
        sem=("arbitrary",), comm=comm)


def _hgrn_bwd(uh, o_pre, d_r, states, lb_raw, norm_g, name, comm=None):
    t = uh.shape[0]
    nc = t // HGRN_CHUNK
    cps = _pick(nc, (HGRN_CHUNKS_PER_STEP, 2, 1))
    ns = nc // cps
    rows_step = cps * HGRN_CHUNK
    nsub = HGRN_CHUNK // HGRN_SUB

    def body(hq_ref, hf_ref, hi_ref, hg_ref, o_ref, dr_ref, st_in_ref, lb_ref, ng_ref,
             duh_ref, dbias_ref, dng_ref, dlb_ref, dst_ref, b_scr, dlb_acc):
        i = pl.program_id(0)

        @pl.when(i == 0)
        def _():
            dst_ref[...] = jnp.zeros_like(dst_ref)
            dbias_ref[...] = jnp.zeros_like(dbias_ref)
            dng_ref[...] = jnp.zeros_like(dng_ref)
            dlb_acc[...] = jnp.zeros_like(dlb_acc)

        masks_st = _sub_masks()
        masks_ts = _sub_masks_ts()
        ng = ng_ref[...]
        zpad = jnp.zeros((HGRN_CHUNK, LANES), F32)
        _, upper = _tri_mats()
        heads = [slice(h * LANES, (h + 1) * LANES) for h in range(4)]
        row = lax.broadcasted_iota(jnp.int32, (HGRN_CHUNK, HG_W), 0)

        chunks = [_hgrn_chunk_inputs(j, hq_ref, hf_ref, hi_ref, hg_ref, lb_ref, b_scr) for j in range(cps)]
        dng = jnp.zeros((1, LANES), F32)
        for ch in chunks:
            o = o_ref[ch["rows"], :]
            dr = dr_ref[ch["rows"], :].astype(F32)
            hg = ch["hg"]
            sgg = _sig(hg)
            dy = dr * (hg * sgg)
            do_h, y_h = [], []
            for sl in heads:
                oh = o[:, sl]
                rs = lax.rsqrt(jnp.mean(oh * oh, axis=1, keepdims=True) + RMS_EPS)
                y_h.append(oh * rs * ng)
                dng = dng + jnp.sum(dy[:, sl] * oh * rs, axis=0, keepdims=True)
                w = dy[:, sl] * ng
                do_h.append(rs * (w - oh * (rs * rs) * jnp.mean(w * oh, axis=1, keepdims=True)))
            ch["do"] = do_h
            ch["dhg"] = dr * jnp.concatenate(y_h, axis=1) * _dsilu(hg, sgg)

        for ch in chunks:
            ch["kst"], ch["kpad"], ch["qt_pad"], ch["v_b"], ch["do_pad"] = [], [], [], [], []
            ch["ats"], ch["d_at"], ch["d_a"] = [], [], []
            for h, sl in enumerate(heads):
                kst = jnp.concatenate([kh[:, sl] for kh in ch["khat"]], axis=0)
                kpad = jnp.concatenate([x for kh in ch["khat"] for x in (kh[:, sl], zpad)], axis=0)
                qt_pad = jnp.concatenate([ch["qt"][:, sl], zpad], axis=0)
                vh = ch["v"][:, sl].astype(BF16)
                v_pad = jnp.concatenate([vh, jnp.zeros_like(vh)], axis=0)
                do_b = ch["do"][h].astype(BF16)
                do_pad = jnp.concatenate([do_b, jnp.zeros_like(do_b)], axis=0)
                ch["kst"].append(kst)
                ch["kpad"].append(kpad)
                ch["qt_pad"].append(qt_pad)
                ch["v_b"].append(vh)
                ch["do_pad"].append(do_pad)
                ch["ats"].append(_dot3(kst, qt_pad, _NT))
                ch["d_at"].append(lax.dot_general(vh, do_pad, _NT, preferred_element_type=F32))
                ch["d_a"].append(lax.dot_general(do_b, v_pad, _NT, preferred_element_type=F32))

        for ch in chunks:
            ch["d_kst"], ch["d_qt"], ch["dv"] = [], [], []
            for h in range(4):
                at = _masked_sum(ch["ats"][h], masks_st, 0)
                d_ats = jnp.concatenate([jnp.where(m, ch["d_at"][h], 0.0) for m in masks_st], axis=0)
                d_a_cat = jnp.concatenate([jnp.where(m, ch["d_a"][h], 0.0) for m in masks_ts], axis=1)
                ch["d_kst"].append(_dot3(d_ats, ch["qt_pad"][h], _NN))
                ch["d_qt"].append(_dot3(d_a_cat, ch["kpad"][h], _NN))
                ch["dv"].append(lax.dot_general(at.astype(BF16), ch["do_pad"][h], _NN, preferred_element_type=F32))

        for j in reversed(range(cps)):
            ch = chunks[j]
            q, k, sg, f, lb = ch["q"], ch["k"], ch["sg"], ch["f"], ch["lb"]
            dq_h, dk_h, dv_h, extra_h = [], [], [], []
            for h, sl in enumerate(heads):
                st_prev = st_in_ref[j, h]
                d_st = dst_ref[h]
                d_st_b = d_st.astype(BF16)
                do_b = ch["do_pad"][h][0:HGRN_CHUNK, :]
                kd, e_last = ch["kd"][:, sl], ch["e_last"][:, sl]
                dv = ch["dv"][h] + _dot(kd, d_st_b, _NT)
                d_qb = _dot(do_b, st_prev, _NN)
                d_kd = lax.dot_general(ch["v_b"][h], d_st_b, _NN, preferred_element_type=F32)
                extra_h.append(jnp.sum(st_prev * d_st, axis=0, keepdims=True) * e_last
                               + jnp.sum(kd * d_kd, axis=0, keepdims=True))
                dst_ref[h] = d_st * e_last + _dot(do_b, ch["qb"][:, sl], _TN)
                dq_h.append(ch["d_qt"][h] * ch["e_q"][:, sl] + d_qb * ch["e_b"][:, sl])
                dkk = d_kd * ch["e_bl"][:, sl]
                for s_ in range(nsub):
                    dkk = dkk + ch["d_kst"][h][HGRN_CHUNK * s_:HGRN_CHUNK * (s_ + 1), :] * ch["e_k"][s_][:, sl]
                dk_h.append(dkk)
                dv_h.append(dv)
            dq = jnp.concatenate(dq_h, axis=1)
            dk = jnp.concatenate(dk_h, axis=1)
            dv = jnp.concatenate(dv_h, axis=1)
            extra = jnp.concatenate(extra_h, axis=1)
            db = q * dq - k * dk + jnp.where(row == HGRN_CHUNK - 1, extra, 0.0)
            dg = _tri_apply(upper, db)
            df = dg / f - dk
            dhf = df * (1.0 - lb) * sg * (1.0 - sg)
            dhq = dq * _dsilu(ch["hq"], _sig(ch["hq"]))
            full = jnp.concatenate([dhq, dhf, dv, ch["dhg"]], axis=1)
            duh_ref[ch["rows"], :] = full.astype(duh_ref.dtype)
            dbias_ref[...] += jnp.sum(full, axis=0, keepdims=True)
            dlb_acc[...] += jnp.sum(df * (1.0 - sg), axis=0, keepdims=True)
        dng_ref[...] += dng

        @pl.when(i == ns - 1)
        def _():
            lb = chunks[0]["lb"]
            d_a0 = dlb_acc[...] * lb * (1.0 - lb)
            r8 = lax.broadcasted_iota(jnp.int32, (8, HG_W), 0)
            dlb_ref[...] = jnp.where(r8 == 0, d_a0, jnp.where(r8 == 1, -d_a0, 0.0))

    col = lambda j: pl.BlockSpec((rows_step, HG_W), lambda i: (ns - 1 - i, j))
    return _call(
        body, name=name, grid=(ns,), ins=[uh, uh, uh, uh, o_pre, d_r, states, lb_raw, norm_g],
        in_specs=[col(0), col(1), col(2), col(3), col(0), col(0),
                  pl.BlockSpec((cps, 4, LANES, LANES), lambda i: (ns - 1 - i, 0, 0, 0)),
                  pl.BlockSpec((2, HG_W), lambda i: (0, 0)), pl.BlockSpec((1, LANES), lambda i: (0, 0))],
        out_specs=[pl.BlockSpec((rows_step, UH_W), lambda i: (ns - 1 - i, 0)),
                   pl.BlockSpec((1, UH_W), lambda i: (0, 0)),
                   pl.BlockSpec((1, LANES), lambda i: (0, 0)),
                   pl.BlockSpec((8, HG_W), lambda i: (0, 0))],
        out_shape=[jax.ShapeDtypeStruct((t, UH_W), BF16), jax.ShapeDtypeStruct((1, UH_W), F32),
                   jax.ShapeDtypeStruct((1, LANES), F32), jax.ShapeDtypeStruct((8, HG_W), F32)],
        scratch_shapes=[pltpu.VMEM((4, LANES, LANES), F32), pltpu.VMEM((cps, HGRN_CHUNK, HG_W), F32),
                        pltpu.VMEM((1, HG_W), F32)],
        sem=("arbitrary",), comm=comm)


def _ln_fwd(z, g, b, name):
    t, d = z.shape
    tr = _pick(t, (256, 128))

    def body(z_ref, g_ref, b_ref, h_ref, hb_ref, xhat_ref, rstd_ref):
        zz = z_ref[...]
        mu = jnp.mean(zz, axis=1, keepdims=True)
        zc = zz - mu
        rstd = lax.rsqrt(jnp.mean(zc * zc, axis=1, keepdims=True) + LN_EPS)
        xhat = zc * rstd
        xhat_ref[...] = xhat
        rstd_ref[...] = rstd
        h = xhat * g_ref[...] + b_ref[...]
        h_ref[...] = h
        hb_ref[...] = h.astype(BF16)

    row = pl.BlockSpec((tr, d), lambda i: (i, 0))
    vec = pl.BlockSpec((1, d), lambda i: (0, 0))
    return pl.pallas_call(
        body, name=name, grid=(t // tr,),
        in_specs=[row, vec, vec],
        out_specs=[row, row, row, pl.BlockSpec((tr, 1), lambda i: (i, 0))],
        out_shape=[jax.ShapeDtypeStruct((t, d), F32), jax.ShapeDtypeStruct((t, d), BF16),
                   jax.ShapeDtypeStruct((t, d), F32), jax.ShapeDtypeStruct((t, 1), F32)],
        compiler_params=_cp("parallel"),
    )(z, g, b)


def _ln_bwd_math(dy, xhat, rstd, g):
    dxh = dy * g
    return rstd * (dxh - jnp.mean(dxh, axis=1, keepdims=True)
                   - xhat * jnp.mean(dxh * xhat, axis=1, keepdims=True))


def _ln_bwd(dy, xhat, rstd, g, name):
    t, d = dy.shape
    tr = _pick(t, (256, 128))

    def body(dy_ref, xhat_ref, rstd_ref, g_ref, dz_ref, dg_ref, db_ref):
        @pl.when(pl.program_id(0) == 0)
        def _():
            dg_ref[...] = jnp.zeros_like(dg_ref)
            db_ref[...] = jnp.zeros_like(db_ref)

        dyv, xh = dy_ref[...], xhat_ref[...]
        dz_ref[...] = _ln_bwd_math(dyv, xh, rstd_ref[...], g_ref[...])
        dg_ref[...] += jnp.sum(dyv * xh, axis=0, keepdims=True)
        db_ref[...] += jnp.sum(dyv, axis=0, keepdims=True)

    row = pl.BlockSpec((tr, d), lambda i: (i, 0))
    vec = pl.BlockSpec((1, d), lambda i: (0, 0))
    return pl.pallas_call(
        body, name=name, grid=(t // tr,),
        in_specs=[row, row, pl.BlockSpec((tr, 1), lambda i: (i, 0)), vec],
        out_specs=[row, vec, vec],
        out_shape=[jax.ShapeDtypeStruct((t, d), F32), jax.ShapeDtypeStruct((1, d), F32),
                   jax.ShapeDtypeStruct((1, d), F32)],
        compiler_params=_cp("arbitrary"),
    )(dy, xhat, rstd, g)


def _ln_loss_bwd(z, target, g, b, name):
    t, d = z.shape
    tr = _pick(t, (256, 128))

    def body(z_ref, tgt_ref, g_ref, b_ref, dz_ref, dg_ref, db_ref, loss_ref):
        @pl.when(pl.program_id(0) == 0)
        def _():
            dg_ref[...] = jnp.zeros_like(dg_ref)
            db_ref[...] = jnp.zeros_like(db_ref)
            loss_ref[...] = jnp.zeros_like(loss_ref)

        zz = z_ref[...]
        gg = g_ref[...]
        mu = jnp.mean(zz, axis=1, keepdims=True)
        zc = zz - mu
        rstd = lax.rsqrt(jnp.mean(zc * zc, axis=1, keepdims=True) + LN_EPS)
        xhat = zc * rstd
        err = xhat * gg + b_ref[...] - tgt_ref[...]
        loss_ref[...] += 0.5 * jnp.sum(jnp.mean(err * err, axis=1, keepdims=True))
        dy = err * (1.0 / d)
        dz_ref[...] = _ln_bwd_math(dy, xhat, rstd, gg)
        dg_ref[...] += jnp.sum(dy * xhat, axis=0, keepdims=True)
        db_ref[...] += jnp.sum(dy, axis=0, keepdims=True)

    row = pl.BlockSpec((tr, d), lambda i: (i, 0))
    vec = pl.BlockSpec((1, d), lambda i: (0, 0))
    return pl.pallas_call(
        body, name=name, grid=(t // tr,),
        in_specs=[row, row, vec, vec],
        out_specs=[row, vec, vec, pl.BlockSpec((1, LANES), lambda i: (0, 0))],
        out_shape=[jax.ShapeDtypeStruct((t, d), F32), jax.ShapeDtypeStruct((1, d), F32),
                   jax.ShapeDtypeStruct((1, d), F32), jax.ShapeDtypeStruct((1, LANES), F32)],
        compiler_params=_cp("arbitrary"),
    )(z, target, g, b)


CONV_TILE = 128
CONV_RB = 32
HALO = 8


def _sum8(x):
    acc = x[0:8]
    for r in range(8, x.shape[0], 8):
        acc = acc + x[r:r + 8]
    return acc


def _conv_fwd(u2, conv_w, conv_b, name):
    t = u2.shape[0]
    tr = _pick(t, (CONV_TILE,))
    hb = tr // HALO
    rb = CONV_RB

    def body(gp_ref, val_ref, prev_ref, w_ref, b_ref, out_ref, ext):
        i = pl.program_id(0)
        ext[0:HALO, :] = jnp.where(i == 0, 0.0, prev_ref[...])
        ext[HALO:, :] = gp_ref[...]
        for c in range(D_FF // LANES):
            ln = slice(c * LANES, (c + 1) * LANES)
            w0, w1, w2, bb = w_ref[0:1, ln], w_ref[1:2, ln], w_ref[2:3, ln], b_ref[:, ln]
            for r0 in range(0, tr, rb):
                gate = (ext[r0 + HALO - 2:r0 + HALO - 2 + rb, ln] * w0 + ext[r0 + HALO - 1:r0 + HALO - 1 + rb, ln] * w1
                        + ext[r0 + HALO:r0 + HALO + rb, ln] * w2 + bb)
                out_ref[r0:r0 + rb, ln] = (gate * _sig(gate) * val_ref[r0:r0 + rb, ln]).astype(out_ref.dtype)

    return pl.pallas_call(
        body, name=name, grid=(t // tr,),
        in_specs=[pl.BlockSpec((tr, D_FF), lambda i: (i, 0)), pl.BlockSpec((tr, D_FF), lambda i: (i, 1)),
                  pl.BlockSpec((HALO, D_FF), lambda i: (jnp.maximum(i * hb - 1, 0), 0)),
                  pl.BlockSpec((3, D_FF), lambda i: (0, 0)), pl.BlockSpec((1, D_FF), lambda i: (0, 0))],
        out_specs=pl.BlockSpec((tr, D_FF), lambda i: (i, 0)),
        out_shape=jax.ShapeDtypeStruct((t, D_FF), BF16),
        scratch_shapes=[pltpu.VMEM((tr + HALO, D_FF), F32)],
        compiler_params=_cp("parallel"),
    )(u2, u2, u2, conv_w, conv_b)


def _conv_bwd(d_hmid, u2, conv_w, conv_b, name, comm=None):
    t = u2.shape[0]
    tr = _pick(t, (CONV_TILE,))
    hb = tr // HALO
    last = t // HALO - 1
    rb = CONV_RB
    re = rb + HALO

    def body(gp_ref, gp_prev_ref, gp_next_ref, val_ref, val_next_ref, dh_ref, dh_next_ref, w_ref, b_ref,
             du_ref, dw_ref, dcb_ref, ext, dg_s):
        i = pl.program_id(0)

        @pl.when(i == 0)
        def _():
            dw_ref[...] = jnp.zeros_like(dw_ref)
            dcb_ref[...] = jnp.zeros_like(dcb_ref)

        ext[0:HALO, :] = jnp.where(i == 0, 0.0, gp_prev_ref[...])
        ext[HALO:HALO + tr, :] = gp_ref[...]
        ext[HALO + tr:, :] = gp_next_ref[...]
        next_in_seq = (i + 1) * tr < t
        for c in range(D_FF // LANES):
            ln = slice(c * LANES, (c + 1) * LANES)
            w0, w1, w2, bb = w_ref[0:1, ln], w_ref[1:2, ln], w_ref[2:3, ln], b_ref[:, ln]
            acc_b = jnp.zeros((8, LANES), F32)
            acc_w = [jnp.zeros((8, LANES), F32) for _ in range(3)]
            for r0 in range(0, tr, rb):
                g_m2 = ext[r0 + HALO - 2:r0 + HALO - 2 + re, ln]
                g_m1 = ext[r0 + HALO - 1:r0 + HALO - 1 + re, ln]
                g_0 = ext[r0 + HALO:r0 + HALO + re, ln]
                gate = g_m2 * w0 + g_m1 * w1 + g_0 * w2 + bb
                sg = _sig(gate)
                if r0 + re <= tr:
                    val = val_ref[r0:r0 + re, ln]
                    dh = dh_ref[r0:r0 + re, ln]
                else:
                    val = jnp.concatenate([val_ref[r0:r0 + rb, ln], val_next_ref[:, ln]], axis=0)
                    dh = jnp.concatenate([dh_ref[r0:r0 + rb, ln],
                                          jnp.where(next_in_seq, dh_next_ref[:, ln], 0.0)], axis=0)
                dgate = dh * val * _dsilu(gate, sg)
                dg_s[:, ln] = dgate
                dg0 = dgate[0:rb]
                d_gp = dg_s[2:2 + rb, ln] * w0 + dg_s[1:1 + rb, ln] * w1 + dg0 * w2
                du_ref[r0:r0 + rb, ln] = d_gp.astype(du_ref.dtype)
                du_ref[r0:r0 + rb, D_FF + c * LANES:D_FF + (c + 1) * LANES] = (
                    dh[0:rb] * (gate[0:rb] * sg[0:rb])).astype(du_ref.dtype)
                acc_b = acc_b + _sum8(dg0)
                acc_w[0] = acc_w[0] + _sum8(dg0 * g_m2[0:rb])
                acc_w[1] = acc_w[1] + _sum8(dg0 * g_m1[0:rb])
                acc_w[2] = acc_w[2] + _sum8(dg0 * g_0[0:rb])
            dcb_ref[:, ln] += jnp.sum(acc_b, axis=0, keepdims=True)
            for j in range(3):
                dw_ref[j:j + 1, ln] += jnp.sum(acc_w[j], axis=0, keepdims=True)

    cur = lambda col: pl.BlockSpec((tr, D_FF), lambda i: (i, col))
    nxt = lambda col: pl.BlockSpec((HALO, D_FF), lambda i: (jnp.minimum((i + 1) * hb, last), col))
    return _call(
        body, name=name, grid=(t // tr,), ins=[u2, u2, u2, u2, u2, d_hmid, d_hmid, conv_w, conv_b],
        in_specs=[cur(0), pl.BlockSpec((HALO, D_FF), lambda i: (jnp.maximum(i * hb - 1, 0), 0)), nxt(0),
                  cur(1), nxt(1), cur(0), nxt(0),
                  pl.BlockSpec((3, D_FF), lambda i: (0, 0)), pl.BlockSpec((1, D_FF), lambda i: (0, 0))],
        out_specs=[pl.BlockSpec((tr, 2 * D_FF), lambda i: (i, 0)),
                   pl.BlockSpec((8, D_FF), lambda i: (0, 0)), pl.BlockSpec((1, D_FF), lambda i: (0, 0))],
        out_shape=[jax.ShapeDtypeStruct((t, 2 * D_FF), BF16), jax.ShapeDtypeStruct((8, D_FF), F32),
                   jax.ShapeDtypeStruct((1, D_FF), F32)],
        scratch_shapes=[pltpu.VMEM((tr + 2 * HALO, D_FF), F32), pltpu.VMEM((re, D_FF), F32)],
        sem=("arbitrary",), comm=comm)


def _adamw(w, g, m, v, name):
    rows, cols = w.shape
    tr = _pick(rows, (256, 128, 64, 32, 16, 8))

    def body(w_ref, g_ref, m_ref, v_ref, d_ref, nm_ref, nv_ref):
        d_ref[...], nm_ref[...], nv_ref[...] = _adamw_math(w_ref[...], g_ref[...], m_ref[...], v_ref[...])

    spec = pl.BlockSpec((tr, cols), lambda i: (i, 0))
    shp = jax.ShapeDtypeStruct((rows, cols), F32)
    return pl.pallas_call(
        body, name=name, grid=(rows // tr,),
        in_specs=[spec, spec, spec, spec], out_specs=[spec, spec, spec], out_shape=[shp, shp, shp],
        compiler_params=_cp("parallel"),
    )(w, g, m, v)


def _pad_rows(a, rows):
    return jnp.pad(a, ((0, rows - a.shape[0]), (0, 0)))


SMALL_LAYOUT = (("ln1_g", 1024), ("ln1_b", 1024), ("b_in", 2816), ("sinks", 8), ("hgrn_lb", 1024),
                ("hgrn_norm_g", 128), ("ln2_g", 1024), ("ln2_b", 1024), ("conv_b", 2816), ("loss", 1))
SMALL_SHAPES = {"ln1_g": (1, 1024), "ln1_b": (1, 1024), "b_in": (1, 2816), "sinks": (1, 8), "hgrn_lb": (2, 512),
                "hgrn_norm_g": (1, 128), "ln2_g": (1, 1024), "ln2_b": (1, 1024), "conv_b": (1, 2816),
                "loss": (1,)}


def _pack_small(parts):
    rows = []
    for name, size in SMALL_LAYOUT:
        flat = parts[name].reshape(-1).astype(F32)
        padded = -(-size // LANES) * LANES
        rows.append(jnp.pad(flat, (0, padded - size)).reshape(-1, LANES))
    return _pad_rows(jnp.concatenate(rows, axis=0), SMALL_ROWS)


def _unpack_small(pack):
    out, r = {}, 0
    for name, size in SMALL_LAYOUT:
        nrows = -(-size // LANES)
        out[name] = pack[r:r + nrows].reshape(-1)[:size].reshape(SMALL_SHAPES[name])
        r += nrows
    return out


def _own(full, rows):
    return lax.dynamic_slice_in_dim(full, _me() * rows, rows, axis=0)


def kernel(x, positions, ln1_g, ln1_b, w_in, b_in, sinks, hgrn_lb, hgrn_norm_g, w_o, ln2_g, ln2_b, w_up, conv_w, conv_b, w_down, loss_target, m_ln1_g, m_ln1_b, m_w_in, m_b_in, m_sinks, m_hgrn_lb, m_hgrn_norm_g, m_w_o, m_ln2_g, m_ln2_b, m_w_up, m_conv_w, m_conv_b, m_w_down, v_ln1_g, v_ln1_b, v_w_in, v_b_in, v_sinks, v_hgrn_lb, v_hgrn_norm_g, v_w_o, v_ln2_g, v_ln2_b, v_w_up, v_conv_w, v_conv_b, v_w_down):
    t = x.shape[1]
    x2 = x[0]
    xb = x2.astype(BF16)
    target = loss_target[0]
    pos_col = positions.reshape(t, 1)

    w_in_t_s = w_in[0].T.astype(BF16)
    w_up_t_s = w_up[0].T.astype(BF16)
    w_o_s = w_o[0].astype(BF16)
    w_down_s = w_down[0].astype(BF16)
    w_in_t_g, cw_g = _comm_only(_Comm("gather", [w_in_t_s, _pad_rows(conv_w[0], 8)]), "ag_w_in")
    w_in_t = w_in_t_g.reshape(D_FF, D_MODEL)
    w_a_t, w_h_t = w_in_t[:UA_W], w_in_t[UA_W:]
    conv_w_f = cw_g[:, 0:3].transpose(1, 0, 2).reshape(3, D_FF)

    ua = _mm(xb, w_a_t, tb=True, bias=b_in[:, :UA_W], name="fwd_in_attn")
    uh = _mm(xb, w_h_t, tb=True, bias=b_in[:, UA_W:], name="fwd_in_hgrn")
    ctab, stab = _rope_tables(pos_col, "rope_tables")
    (a_out,), (w_o_g,) = _attn_fwd(ua, ctab, stab, sinks, "attn_fwd", comm=_Comm("gather", [w_o_s]))
    (r_out, o_pre, states), (w_up_t_g,) = _hgrn_fwd(uh, hgrn_lb, hgrn_norm_g, "hgrn_fwd",
                                                     comm=_Comm("gather", [w_up_t_s]))
    w_o_f = w_o_g.reshape(D_MODEL, D_MODEL)
    w_up_t = w_up_t_g.reshape(2 * D_FF, D_MODEL)
    z1 = _mm(a_out, w_o_f[:ATTN_W], addend=x2, addend_scale=ALPHA, name="fwd_o_attn")
    z1 = _mm(r_out, w_o_f[ATTN_W:], addend=z1, name="fwd_o_hgrn")
    h1, h1b, xhat1, rstd1 = _ln_fwd(z1, ln1_g, ln1_b, "ln1_fwd")
    u2, (w_down_g,) = _mm(h1b, w_up_t, tb=True, tn=1408, name="fwd_up", comm=_Comm("gather", [w_down_s]))
    w_down_f = w_down_g.reshape(D_FF, D_MODEL)
    hmid = _conv_fwd(u2, conv_w_f, conv_b, "conv_fwd")
    z2 = _mm(hmid, w_down_f, addend=h1, addend_scale=ALPHA, tk=1408, name="fwd_down")
    dz2, d_ln2_g, d_ln2_b, loss_part = _ln_loss_bwd(z2, target, ln2_g, ln2_b, "ln2_loss_bwd")

    d_hmid = _mm(dz2, w_down_f, tb=True, tn=1408, name="bwd_down_dx")
    d_w_down, d_w_down_b = _mm(hmid, dz2, ta=True, out_dtype2=BF16, tm=1408, tk=512, name="bwd_down_dw")
    (d_u2, d_conv_w8, d_conv_b), (recv_down,) = _conv_bwd(
        d_hmid, u2, conv_w_f, conv_b, "conv_bwd",
        comm=_Comm("exchange", [d_w_down_b.reshape(N_DEV, SHARD_DOWN, D_MODEL)]))
    d_h1 = _mm(d_u2, w_up_t, addend=dz2, addend_scale=ALPHA, tk=1408, name="bwd_up_dx")
    d_w_up_t, d_w_up_t_b = _mm(d_u2, h1b, ta=True, out_dtype2=BF16, tm=1408, tk=512, name="bwd_up_dw")
    dz1, d_ln1_g, d_ln1_b = _ln_bwd(d_h1, xhat1, rstd1, ln1_g, "ln1_bwd")
    d_a = _mm(dz1, w_o_f[:ATTN_W], tb=True, name="bwd_o_dx_attn")
    d_r = _mm(dz1, w_o_f[ATTN_W:], tb=True, name="bwd_o_dx_hgrn")
    d_w_o_a, d_w_o_a_b = _mm(a_out, dz1, ta=True, out_dtype2=BF16, tk=512, name="bwd_o_dw_attn")
    d_w_o_r, d_w_o_r_b = _mm(r_out, dz1, ta=True, out_dtype2=BF16, tk=512, name="bwd_o_dw_hgrn")
    d_w_o = jnp.concatenate([d_w_o_a, d_w_o_r], axis=0)
    d_w_o_b = jnp.concatenate([d_w_o_a_b, d_w_o_r_b], axis=0)
    d_w_up_x = d_w_up_t_b.reshape(N_DEV, SHARD_UP, D_MODEL)
    half = SHARD_UP // 2
    (d_uh, d_bias_h, d_norm_g, d_lb8), (recv_up_a,) = _hgrn_bwd(
        uh, o_pre, d_r, states, hgrn_lb, hgrn_norm_g, "hgrn_bwd",
        comm=_Comm("exchange", [d_w_up_x[:, :half]]))
    d_cw_x = d_conv_w8.reshape(8, N_DEV, SHARD_IN).transpose(1, 0, 2)
    (d_ua, d_bias_a, d_sinks), (recv_up_b, recv_o, recv_cw) = _attn_bwd(
        ua, d_a, ctab, stab, sinks, "attn_bwd",
        comm=_Comm("exchange", [d_w_up_x[:, half:], d_w_o_b.reshape(N_DEV, SHARD_O, D_MODEL), d_cw_x]))
    d_w_a_t, d_w_a_t_b = _mm(d_ua, xb, ta=True, out_dtype2=BF16, tk=512, name="bwd_in_dw_attn")
    d_w_h_t, d_w_h_t_b = _mm(d_uh, xb, ta=True, out_dtype2=BF16, tk=512, name="bwd_in_dw_hgrn")
    d_w_in_t = jnp.concatenate([d_w_a_t, d_w_h_t], axis=0)
    d_w_in_t_b = jnp.concatenate([d_w_a_t_b, d_w_h_t_b], axis=0)
    dx = _mm(d_ua, w_a_t, addend=dz1, addend_scale=ALPHA, tk=768, name="bwd_in_dx_attn")
    dx, (recv_in,) = _mm(d_uh, w_h_t, addend=dx, name="bwd_in_dx_hgrn",
                         comm=_Comm("exchange", [d_w_in_t_b.reshape(N_DEV, SHARD_IN, D_MODEL)]))

    g_w_in = _sum_shards(recv_in, _own(d_w_in_t, SHARD_IN), "sum_w_in").T
    own_up = _own(d_w_up_t, SHARD_UP)
    g_w_up = jnp.concatenate([_sum_shards(recv_up_a, own_up[:half], "sum_w_up_a"),
                              _sum_shards(recv_up_b, own_up[half:], "sum_w_up_b")], axis=0).T
    res_in = (g_w_in,) + tuple(_adamw(w_in[0], g_w_in, m_w_in[0], v_w_in[0], "adamw_w_in"))
    res_up = (g_w_up,) + tuple(_adamw(w_up[0], g_w_up, m_w_up[0], v_w_up[0], "adamw_w_up"))
    res_o = _sum_shards_adamw(recv_o, _own(d_w_o, SHARD_O), w_o[0], m_w_o[0], v_w_o[0], "adamw_w_o")
    res_down = _sum_shards_adamw(recv_down, _own(d_w_down, SHARD_DOWN), w_down[0], m_w_down[0], v_w_down[0],
                                 "adamw_w_down")
    g_cw = _sum_slots(recv_cw, "sum_conv_w")
    cw8 = lambda a: _pad_rows(a, 8)
    res_cw = (g_cw,) + tuple(_adamw(cw8(conv_w[0]), g_cw, cw8(m_conv_w[0]), cw8(v_conv_w[0]), "adamw_conv_w"))
    big = {"w_in": [r[None] for r in res_in], "w_up": [r[None] for r in res_up],
           "w_o": [r[None] for r in res_o], "w_down": [r[None] for r in res_down],
           "conv_w": [r[None, 0:3] for r in res_cw]}

    small_local = _pack_small({
        "ln1_g": d_ln1_g, "ln1_b": d_ln1_b, "b_in": jnp.concatenate([d_bias_a, d_bias_h], axis=1),
        "sinks": d_sinks[:, :8], "hgrn_lb": d_lb8[0:2], "hgrn_norm_g": d_norm_g, "ln2_g": d_ln2_g,
        "ln2_b": d_ln2_b, "conv_b": d_conv_b, "loss": loss_part[:, :1]})
    small_sum = _sum_slots(_all_gather_vmem(small_local, "ar_small"), "ar_small_sum")
    gs = _unpack_small(small_sum)
    loss = gs["loss"][0]
    zero1 = jnp.zeros((1,), F32)
    w_small = _pack_small({"ln1_g": ln1_g, "ln1_b": ln1_b, "b_in": b_in, "sinks": sinks, "hgrn_lb": hgrn_lb,
                           "hgrn_norm_g": hgrn_norm_g, "ln2_g": ln2_g, "ln2_b": ln2_b, "conv_b": conv_b,
                           "loss": zero1})
    m_small = _pack_small({"ln1_g": m_ln1_g, "ln1_b": m_ln1_b, "b_in": m_b_in, "sinks": m_sinks,
                           "hgrn_lb": m_hgrn_lb, "hgrn_norm_g": m_hgrn_norm_g, "ln2_g": m_ln2_g,
                           "ln2_b": m_ln2_b, "conv_b": m_conv_b, "loss": zero1})
    v_small = _pack_small({"ln1_g": v_ln1_g, "ln1_b": v_ln1_b, "b_in": v_b_in, "sinks": v_sinks,
                           "hgrn_lb": v_hgrn_lb, "hgrn_norm_g": v_hgrn_norm_g, "ln2_g": v_ln2_g,
                           "ln2_b": v_ln2_b, "conv_b": v_conv_b, "loss": zero1})
    small = [gs] + [_unpack_small(p) for p in _adamw(w_small, small_sum, m_small, v_small, "adamw_small")]

    order = ["ln1_g", "ln1_b", "w_in", "b_in", "sinks", "hgrn_lb", "hgrn_norm_g", "w_o", "ln2_g", "ln2_b",
             "w_up", "conv_w", "conv_b", "w_down"]

    def pick(idx):
        return [big[n][idx] if n in big else small[idx][n] for n in order]

    return (loss, dx[None], *pick(0), *pick(1), *pick(2), *pick(3))
```

```python
import functools

import jax
import jax.numpy as jnp
import numpy as np
from jax import lax
from jax.experimental import pallas as pl
from jax.experimental.pallas import tpu as pltpu

F32 = jnp.float32
BF16 = jnp.bfloat16

N_DEV = 8
D_MODEL = 1024
D_FF = 2816
ATTN_W = 512
KV_W = 128
UA_W = ATTN_W + 2 * KV_W
UH_W = 2048
HG_W = 512
ATTN_BLOCK = 128
HGRN_CHUNK = 64
HGRN_SUB = 16
HGRN_CHUNKS_PER_STEP = 4
EXP_CLAMP = 85.0
NEG_BIG = -1e30
LN_EPS = 1e-5
RMS_EPS = 1e-6
ALPHA = 2.0 ** 0.25
ATTN_SCALE = 0.125
ROPE_THETA = 500000.0

ADAM_LR = 0.001
ADAM_B1 = 0.9
ADAM_B2 = 0.999
ADAM_EPS = 1e-08
ADAM_WD = 0.01
ADAM_STEP = 10

LANES = 128
VMEM_LIMIT_BYTES = 56 * 1024 * 1024

SHARD_IN = D_FF // N_DEV
SHARD_UP = 2 * D_FF // N_DEV
SHARD_O = D_MODEL // N_DEV
SHARD_DOWN = D_FF // N_DEV
SMALL_ROWS = 88

_MESH = pl.DeviceIdType.MESH
_NT = (((1,), (1,)), ((), ()))
_NN = (((1,), (0,)), ((), ()))
_TN = (((0,), (0,)), ((), ()))


def _cp(*sem):
    if sem:
        return pltpu.CompilerParams(dimension_semantics=sem, vmem_limit_bytes=VMEM_LIMIT_BYTES)
    return pltpu.CompilerParams(vmem_limit_bytes=VMEM_LIMIT_BYTES)


def _sig(x):
    return 1.0 / (1.0 + jnp.exp(-x))


def _dsilu(x, s):
    return s * (1.0 + x * (1.0 - s))


def _dot(a, b, dims):
    return lax.dot_general(a.astype(BF16), b.astype(BF16), dims, preferred_element_type=F32)


def _split(a):
    hi = a.astype(BF16)
    return hi, (a - hi.astype(F32)).astype(BF16)


def _dot3(a, b, dims):
    ah, al = _split(a)
    bh, bl = _split(b)
    d = functools.partial(lax.dot_general, dimension_numbers=dims, preferred_element_type=F32)
    return d(ah, bh) + (d(ah, bl) + d(al, bh))


def _pick(n, pref):
    for t in pref:
        if t <= n and n % t == 0:
            return t
    return n


def _my_coords():
    return lax.axis_index("x"), lax.axis_index("y"), lax.axis_index("c")


def _peer(k):
    x, y, c = _my_coords()
    return (1 - x if k & 4 else x, 1 - y if k & 2 else y, 1 - c if k & 1 else c)


def _me():
    x, y, c = _my_coords()
    return 4 * x + 2 * y + c


class _Comm:
    def __init__(self, kind, arrays):
        self.kind, self.arrays, self.n = kind, list(arrays), len(arrays)

    def out_shapes(self):
        if self.kind == "gather":
            return [jax.ShapeDtypeStruct((N_DEV,) + a.shape, a.dtype) for a in self.arrays]
        return [jax.ShapeDtypeStruct(a.shape, a.dtype) for a in self.arrays]

    def specs(self):
        return [pl.BlockSpec(memory_space=pl.ANY)] * self.n

    def scratch(self):
        return [pltpu.SemaphoreType.DMA(((N_DEV - 1) * self.n,)), pltpu.SemaphoreType.DMA(((N_DEV - 1) * self.n,)),
                pltpu.SemaphoreType.DMA((self.n,))]

    def _src(self, ref, dev):
        return ref if self.kind == "gather" else ref.at[dev]

    def _copy(self, a, k, src, dst, sems, me, slot):
        other = jnp.bitwise_xor(me, k)
        idx = a * (N_DEV - 1) + k - 1
        return pltpu.make_async_remote_copy(
            src_ref=self._src(src, other), dst_ref=dst.at[me if slot == "mine" else other],
            send_sem=sems[0].at[idx], recv_sem=sems[1].at[idx], device_id=_peer(k), device_id_type=_MESH)

    def _pass_on(self, a, k, dst, sems, me):
        slot = dst.at[jnp.bitwise_xor(me, k)]
        idx = a * (N_DEV - 1) + k
        return pltpu.make_async_remote_copy(
            src_ref=slot, dst_ref=slot, send_sem=sems[0].at[idx], recv_sem=sems[1].at[idx],
            device_id=_peer(1), device_id_type=_MESH)

    def start(self, srcs, dsts, sems):
        me = _me()
        direct = (1, 2, 4, 6) if self.kind == "gather" else range(1, N_DEV)
        for a, (src, dst) in enumerate(zip(srcs, dsts)):
            pltpu.make_async_copy(self._src(src, me), dst.at[me], sems[2].at[a]).start()
            for k in direct:
                self._copy(a, k, src, dst, sems, me, "mine").start()

    def wait(self, srcs, dsts, sems):
        me = _me()
        for a, (src, dst) in enumerate(zip(srcs, dsts)):
            if self.kind == "gather":
                for k in (2, 4, 6):
                    self._copy(a, k, src, dst, sems, me, "theirs").wait_recv()
                    self._pass_on(a, k, dst, sems, me).start()
                for k in (1, 3, 5, 7):
                    self._copy(a, k, src, dst, sems, me, "theirs").wait_recv()
                for k in (1, 2, 4, 6):
                    self._copy(a, k, src, dst, sems, me, "mine").wait_send()
                for k in (2, 4, 6):
                    self._pass_on(a, k, dst, sems, me).wait_send()
            else:
                for k in range(1, N_DEV):
                    self._copy(a, k, src, dst, sems, me, "theirs").wait_recv()
                for k in range(1, N_DEV):
                    self._copy(a, k, src, dst, sems, me, "mine").wait_send()
            pltpu.make_async_copy(self._src(src, me), dst.at[me], sems[2].at[a]).wait()


def _call(body, *, name, grid, ins, in_specs, out_specs, out_shape, scratch_shapes=(), sem, comm=None):
    n_in, n_out, n_scr = len(ins), len(out_shape), len(scratch_shapes)
    if comm is None:
        outs = pl.pallas_call(
            body, name=name, grid=grid, in_specs=list(in_specs), out_specs=list(out_specs),
            out_shape=list(out_shape), scratch_shapes=list(scratch_shapes), compiler_params=_cp(*sem))(*ins)
        return list(outs), []
    nc = comm.n

    def hosted(*refs):
        pos = n_in
        c_in = refs[pos:pos + nc]
        pos += nc
        outs = refs[pos:pos + n_out]
        pos += n_out
        c_out = refs[pos:pos + nc]
        pos += nc
        scr = refs[pos:pos + n_scr]
        sems = refs[pos + n_scr:]
        ids = [pl.program_id(d) for d in range(len(grid))]
        first = functools.reduce(jnp.logical_and, [i == 0 for i in ids])
        last = functools.reduce(jnp.logical_and, [i == g - 1 for i, g in zip(ids, grid)])

        @pl.when(first)
        def _():
            comm.start(c_in, c_out, sems)

        body(*refs[:n_in], *outs, *scr)

        @pl.when(last)
        def _():
            comm.wait(c_in, c_out, sems)

    outs = pl.pallas_call(
        hosted, name=name, grid=grid, in_specs=list(in_specs) + comm.specs(),
        out_specs=list(out_specs) + comm.specs(), out_shape=list(out_shape) + comm.out_shapes(),
        scratch_shapes=list(scratch_shapes) + comm.scratch(),
        compiler_params=_cp(*(["arbitrary"] * len(grid))))(*ins, *comm.arrays)
    return list(outs[:n_out]), list(outs[n_out:])


def _comm_only(comm, name):
    def body(*refs):
        srcs, dsts, sems = refs[:comm.n], refs[comm.n:2 * comm.n], refs[2 * comm.n:]
        comm.start(srcs, dsts, sems)
        comm.wait(srcs, dsts, sems)

    return list(pl.pallas_call(
        body, name=name, in_specs=comm.specs(), out_specs=comm.specs(), out_shape=comm.out_shapes(),
        scratch_shapes=comm.scratch(), compiler_params=_cp())(*comm.arrays))


def _all_gather_vmem(x, name):
    def body(x_ref, out_ref, send_sems, recv_sems, local_sem):
        me = _me()
        local = pltpu.make_async_copy(x_ref, out_ref.at[me], local_sem)
        local.start()
        sends = []
        for k in range(1, N_DEV):
            cp = pltpu.make_async_remote_copy(
                src_ref=x_ref, dst_ref=out_ref.at[me], send_sem=send_sems.at[k - 1],
                recv_sem=recv_sems.at[k - 1], device_id=_peer(k), device_id_type=_MESH)
            cp.start()
            sends.append(cp)
        for k in range(1, N_DEV):
            pltpu.make_async_remote_copy(
                src_ref=x_ref, dst_ref=out_ref.at[jnp.bitwise_xor(me, k)], send_sem=send_sems.at[k - 1],
                recv_sem=recv_sems.at[k - 1], device_id=_peer(k), device_id_type=_MESH).wait_recv()
        for cp in sends:
            cp.wait_send()
        local.wait()

    return pl.pallas_call(
        body, name=name,
        out_shape=jax.ShapeDtypeStruct((N_DEV,) + x.shape, x.dtype),
        in_specs=[pl.BlockSpec(memory_space=pltpu.VMEM)],
        out_specs=pl.BlockSpec(memory_space=pltpu.VMEM),
        scratch_shapes=[pltpu.SemaphoreType.DMA((N_DEV - 1,)), pltpu.SemaphoreType.DMA((N_DEV - 1,)),
                        pltpu.SemaphoreType.DMA],
        compiler_params=_cp(),
    )(x)


def _sum_slots(gathered, name):
    _, rows, cols = gathered.shape

    def body(g_ref, out_ref):
        acc = g_ref[0]
        for s in range(1, N_DEV):
            acc = acc + g_ref[s]
        out_ref[...] = acc

    return pl.pallas_call(
        body, name=name,
        out_shape=jax.ShapeDtypeStruct((rows, cols), F32),
        compiler_params=_cp(),
    )(gathered)


def _slot_sum(recv_ref, own_ref, shape):
    me = _me()
    acc = jnp.zeros(shape, F32)
    for s in range(N_DEV):
        acc = acc + jnp.where(me == s, own_ref[...], recv_ref[s].astype(F32))
    return acc


def _adamw_math(w, g, m, v):
    nm = ADAM_B1 * m + (1.0 - ADAM_B1) * g
    nv = ADAM_B2 * v + (1.0 - ADAM_B2) * (g * g)
    m_hat = nm / (1.0 - ADAM_B1 ** ADAM_STEP)
    v_hat = nv / (1.0 - ADAM_B2 ** ADAM_STEP)
    return -ADAM_LR * (m_hat / (jnp.sqrt(v_hat) + ADAM_EPS) + ADAM_WD * w), nm, nv


def _sum_shards(recv, own, name):
    _, rows, cols = recv.shape
    tr = _pick(rows, (256, 128, 176, 64, 32, 16, 8))

    def body(recv_ref, own_ref, out_ref):
        out_ref[...] = _slot_sum(recv_ref, own_ref, (tr, cols))

    return pl.pallas_call(
        body, name=name, grid=(rows // tr,),
        in_specs=[pl.BlockSpec((N_DEV, tr, cols), lambda i: (0, i, 0)), pl.BlockSpec((tr, cols), lambda i: (i, 0))],
        out_specs=pl.BlockSpec((tr, cols), lambda i: (i, 0)),
        out_shape=jax.ShapeDtypeStruct((rows, cols), F32),
        compiler_params=_cp("parallel"),
    )(recv, own)


def _sum_shards_adamw(recv, own, w, m, v, name):
    _, rows, cols = recv.shape
    tr = _pick(rows, (128, 176, 64, 32, 16, 8))

    def body(recv_ref, own_ref, w_ref, m_ref, v_ref, g_ref, d_ref, nm_ref, nv_ref):
        g = _slot_sum(recv_ref, own_ref, (tr, cols))
        g_ref[...] = g
        d_ref[...], nm_ref[...], nv_ref[...] = _adamw_math(w_ref[...], g, m_ref[...], v_ref[...])

    spec = pl.BlockSpec((tr, cols), lambda i: (i, 0))
    shp = jax.ShapeDtypeStruct((rows, cols), F32)
    return pl.pallas_call(
        body, name=name, grid=(rows // tr,),
        in_specs=[pl.BlockSpec((N_DEV, tr, cols), lambda i: (0, i, 0)), spec, spec, spec, spec],
        out_specs=[spec, spec, spec, spec], out_shape=[shp, shp, shp, shp],
        compiler_params=_cp("parallel"),
    )(recv, own, w, m, v)


def _mm(a, b, *, name, ta=False, tb=False, out_dtype=F32, out_dtype2=None, bias=None, addend=None,
        addend_scale=1.0, tm=1024, tn=1024, tk=1024, comm=None):
    kdim, m = a.shape if ta else a.shape[::-1]
    n = b.shape[0] if tb else b.shape[1]
    tm = _pick(m, (tm, 1408, 1024, 768, 512, 256, 128))
    tn = _pick(n, (tn, 1408, 1024, 768, 512, 256, 128))
    tk = _pick(kdim, (tk, 1408, 1024, 768, 512, 256, 128))
    nk = kdim // tk
    a_spec = pl.BlockSpec((tk, tm), lambda i, j, k: (k, i)) if ta else pl.BlockSpec((tm, tk), lambda i, j, k: (i, k))
    b_spec = pl.BlockSpec((tn, tk), lambda i, j, k: (j, k)) if tb else pl.BlockSpec((tk, tn), lambda i, j, k: (k, j))
    ins, specs = [a, b], [a_spec, b_spec]
    if bias is not None:
        ins.append(bias)
        specs.append(pl.BlockSpec((1, tn), lambda i, j, k: (0, j)))
    if addend is not None:
        ins.append(addend)
        specs.append(pl.BlockSpec((tm, tn), lambda i, j, k: (i, j)))
    dims = (((0,) if ta else (1,), (1,) if tb else (0,)), ((), ()))
    has_bias, has_addend, two = bias is not None, addend is not None, out_dtype2 is not None

    def body(*refs):
        a_ref, b_ref = refs[0], refs[1]
        pos = 2
        bias_ref = addend_ref = None
        if has_bias:
            bias_ref = refs[pos]
            pos += 1
        if has_addend:
            addend_ref = refs[pos]
            pos += 1
        o_refs, acc_ref = refs[pos:-1], refs[-1]
        k = pl.program_id(2)

        @pl.when(k == 0)
        def _():
            acc_ref[...] = jnp.zeros_like(acc_ref)

        acc_ref[...] += _dot(a_ref[...], b_ref[...], dims)

        @pl.when(k == nk - 1)
        def _():
            r = acc_ref[...]
            if has_bias:
                r = r + bias_ref[...]
            if has_addend:
                r = r + addend_scale * addend_ref[...].astype(F32)
            for o_ref in o_refs:
                o_ref[...] = r.astype(o_ref.dtype)

    ospec = pl.BlockSpec((tm, tn), lambda i, j, k: (i, j))
    dtypes = [out_dtype] + ([out_dtype2] if two else [])
    outs, couts = _call(
        body, name=name, grid=(m // tm, n // tn, nk), ins=ins, in_specs=specs,
        out_specs=[ospec] * len(dtypes), out_shape=[jax.ShapeDtypeStruct((m, n), d) for d in dtypes],
        scratch_shapes=[pltpu.VMEM((tm, tn), F32)], sem=("parallel", "parallel", "arbitrary"), comm=comm)
    primary = tuple(outs) if two else outs[0]
    return (primary, couts) if comm is not None else primary


def _rope_lane_constants():
    inv_freq = np.float32(ROPE_THETA) ** (-np.arange(8, dtype=np.float32) * np.float32(2.0 / 16.0))
    lane = np.arange(LANES) % 64
    freq = np.where(lane < 16, inv_freq[lane % 8], 0.0).astype(np.float32)
    sign = np.where(lane < 8, -1.0, np.where(lane < 16, 1.0, 0.0)).astype(np.float32)
    return jnp.asarray(freq)[None, :], jnp.asarray(sign)[None, :]


def _rope_tables(pos_col, name):
    t = pos_col.shape[0]
    tr = _pick(t, (512, 256, 128))
    freq, sign = _rope_lane_constants()

    def body(pos_ref, freq_ref, sign_ref, c_ref, s_ref):
        ang = pos_ref[...].astype(F32) * freq_ref[...]
        c_ref[...] = jnp.cos(ang)
        s_ref[...] = sign_ref[...] * jnp.sin(ang)

    return pl.pallas_call(
        body, name=name, grid=(t // tr,),
        in_specs=[pl.BlockSpec((tr, 1), lambda i: (i, 0)),
                  pl.BlockSpec((1, LANES), lambda i: (0, 0)),
                  pl.BlockSpec((1, LANES), lambda i: (0, 0))],
        out_specs=[pl.BlockSpec((tr, LANES), lambda i: (i, 0)), pl.BlockSpec((tr, LANES), lambda i: (i, 0))],
        out_shape=[jax.ShapeDtypeStruct((t, LANES), F32), jax.ShapeDtypeStruct((t, LANES), F32)],
        compiler_params=_cp("parallel"),
    )(pos_col, freq, sign)


def _swap8(t):
    width = t.shape[1]
    lane = jnp.bitwise_and(lax.broadcasted_iota(jnp.int32, t.shape, 1), 63)
    return jnp.where(lane < 8, pltpu.roll(t, width - 8, 1), jnp.where(lane < 16, pltpu.roll(t, 8, 1), 0.0))


def _rope(t, c, s):
    return t * c + _swap8(t) * s


def _rope_bwd(d, c, s):
    return d * c + _swap8(d * s)


def _tile4(a):
    return jnp.concatenate([a, a, a, a], axis=1)


def _attn_band(n, k_cur, k_prev, v_cur, v_prev, c_cur, s_cur, c_prev, s_prev):
    kband = jnp.concatenate([_rope(k_prev, c_prev, s_prev), _rope(k_cur, c_cur, s_cur)], axis=0)
    vband = jnp.concatenate([v_prev, v_cur], axis=0)
    qi = lax.broadcasted_iota(jnp.int32, (ATTN_BLOCK, 2 * ATTN_BLOCK), 0)
    kj = lax.broadcasted_iota(jnp.int32, (ATTN_BLOCK, 2 * ATTN_BLOCK), 1)
    dist = qi + ATTN_BLOCK - kj
    valid = (dist >= 0) & (dist < ATTN_BLOCK) & (n * ATTN_BLOCK - ATTN_BLOCK + kj >= 0)
    return (kband.astype(BF16), pltpu.roll(kband, 64, 1).astype(BF16),
            vband.astype(BF16), pltpu.roll(vband, 64, 1).astype(BF16), valid)


def _attn_probs(raw, valid, sink, axis):
    s = jnp.where(valid, raw * ATTN_SCALE, NEG_BIG)
    m = jnp.maximum(jnp.max(s, axis=axis, keepdims=True), sink)
    p = jnp.exp(s - m)
    esink = jnp.exp(sink - m)
    z = jnp.sum(p, axis=axis, keepdims=True) + esink
    return p / z, esink / z


def _attn_valid_t(n):
    kj = lax.broadcasted_iota(jnp.int32, (2 * ATTN_BLOCK, ATTN_BLOCK), 0)
    qi = lax.broadcasted_iota(jnp.int32, (2 * ATTN_BLOCK, ATTN_BLOCK), 1)
    dist = qi + ATTN_BLOCK - kj
    return (dist >= 0) & (dist < ATTN_BLOCK) & (n * ATTN_BLOCK - ATTN_BLOCK + kj >= 0)


def _attn_specs(nb):
    def cur(col, width=KV_W):
        return pl.BlockSpec((ATTN_BLOCK, width), lambda n: (jnp.minimum(n, nb - 1), col))

    def prev(col):
        return pl.BlockSpec((ATTN_BLOCK, KV_W), lambda n: (jnp.maximum(n - 1, 0), col))

    ua_specs = [cur(0, ATTN_W), cur(4), prev(4), cur(5), prev(5)]
    tab_specs = [cur(0), cur(0), prev(0), prev(0)]
    return ua_specs, tab_specs


def _attn_fwd(ua, ctab, stab, sinks, name, comm=None):
    t = ua.shape[0]
    nb = t // ATTN_BLOCK
    ua_specs, tab_specs = _attn_specs(nb)

    def body(q_ref, kc_ref, kp_ref, vc_ref, vp_ref, cc_ref, sc_ref, cp_ref, sp_ref, sink_ref, o_ref, o_t_ref):
        n = pl.program_id(0)
        cc, sc = cc_ref[...], sc_ref[...]
        kb, kb_r, vb, vb_r, valid = _attn_band(n, kc_ref[...], kp_ref[...], vc_ref[...], vp_ref[...],
                                               cc, sc, cp_ref[...], sp_ref[...])
        qr = _rope(q_ref[...], _tile4(cc), _tile4(sc))
        lo = lax.broadcasted_iota(jnp.int32, (ATTN_BLOCK, LANES), 1) < 64
        heads = []
        for j in range(4):
            qj = qr[:, j * LANES:(j + 1) * LANES]
            for is_lo in (True, False):
                aligned = is_lo == (j < 2)
                qm = jnp.where(lo if is_lo else jnp.logical_not(lo), qj, 0.0).astype(BF16)
                raw = lax.dot_general(qm, kb if aligned else kb_r, _NT, preferred_element_type=F32)
                heads.append((raw, vb if aligned else vb_r, sink_ref[0, len(heads)]))
        halves = []
        for raw, vv, sink in heads:
            probs, _ = _attn_probs(raw, valid, sink, 1)
            halves.append(lax.dot_general(probs.astype(BF16), vv, _NN, preferred_element_type=F32))
        outs = [jnp.where(lo, halves[2 * j], halves[2 * j + 1]) for j in range(4)]
        o_ref[...] = jnp.concatenate(outs, axis=1).astype(o_ref.dtype)
        for j in range(4):
            o_t_ref[j * LANES:(j + 1) * LANES, :] = outs[j].T.astype(o_t_ref.dtype)

    return _call(
        body, name=name, grid=(nb,), ins=[ua, ua, ua, ua, ua, ctab, stab, ctab, stab, sinks],
        in_specs=ua_specs + tab_specs + [pl.BlockSpec(memory_space=pltpu.SMEM)],
        out_specs=[pl.BlockSpec((ATTN_BLOCK, ATTN_W), lambda n: (n, 0)),
                   pl.BlockSpec((ATTN_W, ATTN_BLOCK), lambda n: (0, n))],
        out_shape=[jax.ShapeDtypeStruct((t, ATTN_W), BF16), jax.ShapeDtypeStruct((ATTN_W, t), BF16)],
        sem=("parallel",), comm=comm)


def _attn_bwd(ua, d_out, ctab, stab, sinks, name, comm=None):
    t = ua.shape[0]
    nb = t // ATTN_BLOCK
    ua_specs, tab_specs = _attn_specs(nb)

    def body(q_ref, kc_ref, kp_ref, vc_ref, vp_ref, cc_ref, sc_ref, cp_ref, sp_ref, do_ref, sink_ref,
             dua_ref, dua_t_ref, dbias_ref, dsink_ref, dq_c, dk_c, dv_c, dq_n, dk_n, dv_n):
        n = pl.program_id(0)

        @pl.when(n == 0)
        def _():
            dq_c[...] = jnp.zeros_like(dq_c)
            dk_c[...] = jnp.zeros_like(dk_c)
            dv_c[...] = jnp.zeros_like(dv_c)
            dbias_ref[...] = jnp.zeros_like(dbias_ref)
            dsink_ref[...] = jnp.zeros_like(dsink_ref)

        @pl.when(n == nb)
        def _():
            dq_n[...] = jnp.zeros_like(dq_n)
            dk_n[...] = jnp.zeros_like(dk_n)
            dv_n[...] = jnp.zeros_like(dv_n)

        @pl.when(n < nb)
        def _():
            cc, sc = cc_ref[...], sc_ref[...]
            kb, kb_r, vb, vb_r, valid = _attn_band(n, kc_ref[...], kp_ref[...], vc_ref[...], vp_ref[...],
                                                   cc, sc, cp_ref[...], sp_ref[...])
            valid_t = _attn_valid_t(n)
            c4, s4 = _tile4(cc), _tile4(sc)
            qr = _rope(q_ref[...], c4, s4)
            do = do_ref[...].astype(F32)
            lane = lax.broadcasted_iota(jnp.int32, (ATTN_BLOCK, LANES), 1)
            lo = lane < 64
            lane_row = lax.broadcasted_iota(jnp.int32, (1, LANES), 1)
            heads = []
            for j in range(4):
                qj = qr[:, j * LANES:(j + 1) * LANES]
                doj = do[:, j * LANES:(j + 1) * LANES]
                for is_lo in (True, False):
                    aligned = is_lo == (j < 2)
                    msk = lo if is_lo else jnp.logical_not(lo)
                    kk = kb if aligned else kb_r
                    vv = vb if aligned else vb_r
                    qm = jnp.where(msk, qj, 0.0).astype(BF16)
                    dom = jnp.where(msk, doj, 0.0).astype(BF16)
                    heads.append(dict(
                        aligned=aligned, kk=kk, qm=qm, dom=dom, sink=sink_ref[0, len(heads)],
                        raw=lax.dot_general(qm, kk, _NT, preferred_element_type=F32),
                        dp=lax.dot_general(dom, vv, _NT, preferred_element_type=F32),
                        raw_t=lax.dot_general(kk, qm, _NT, preferred_element_type=F32),
                        dp_t=lax.dot_general(vv, dom, _NT, preferred_element_type=F32)))
            dk_band = jnp.zeros((2 * ATTN_BLOCK, LANES), F32)
            dv_band = jnp.zeros((2 * ATTN_BLOCK, LANES), F32)
            dsink = jnp.zeros((1, LANES), F32)
            halves = []
            for head, hd in enumerate(heads):
                probs, psink = _attn_probs(hd["raw"], valid, hd["sink"], 1)
                delta = jnp.sum(probs * hd["dp"], axis=1, keepdims=True)
                ds = (probs * (hd["dp"] - delta) * ATTN_SCALE).astype(BF16)
                dsink = dsink + jnp.where(lane_row == head, -jnp.sum(psink * delta), 0.0)
                halves.append(lax.dot_general(ds, hd["kk"], _NN, preferred_element_type=F32))
                probs_t, _ = _attn_probs(hd["raw_t"], valid_t, hd["sink"], 0)
                delta_t = jnp.sum(probs_t * hd["dp_t"], axis=0, keepdims=True)
                ds_t = (probs_t * (hd["dp_t"] - delta_t) * ATTN_SCALE).astype(BF16)
                dk_h = lax.dot_general(ds_t, hd["qm"], _NN, preferred_element_type=F32)
                dv_h = lax.dot_general(probs_t.astype(BF16), hd["dom"], _NN, preferred_element_type=F32)
                if not hd["aligned"]:
                    dk_h = pltpu.roll(dk_h, 64, 1)
                    dv_h = pltpu.roll(dv_h, 64, 1)
                dk_band = dk_band + dk_h
                dv_band = dv_band + dv_h
            dqs = [jnp.where(lo, halves[2 * j], halves[2 * j + 1]) for j in range(4)]
            dq_n[...] = _rope_bwd(jnp.concatenate(dqs, axis=1), c4, s4)
            dk_n[...] = dk_band
            dv_n[...] = dv_band
            dsink_ref[...] += dsink

        dk_prev = _rope_bwd(dk_c[...] + dk_n[0:ATTN_BLOCK, :], cp_ref[...], sp_ref[...])
        dv_prev = dv_c[...] + dv_n[0:ATTN_BLOCK, :]
        full = jnp.concatenate([dq_c[...], dk_prev, dv_prev], axis=1)
        dua_ref[...] = full.astype(dua_ref.dtype)
        for j in range(UA_W // LANES):
            dua_t_ref[j * LANES:(j + 1) * LANES, :] = full[:, j * LANES:(j + 1) * LANES].T.astype(dua_t_ref.dtype)
        dbias_ref[...] += jnp.sum(full, axis=0, keepdims=True)
        dq_c[...] = dq_n[...]
        dk_c[...] = dk_n[ATTN_BLOCK:, :]
        dv_c[...] = dv_n[ATTN_BLOCK:, :]

    return _call(
        body, name=name, grid=(nb + 1,), ins=[ua, ua, ua, ua, ua, ctab, stab, ctab, stab, d_out, sinks],
        in_specs=ua_specs + tab_specs + [
            pl.BlockSpec((ATTN_BLOCK, ATTN_W), lambda n: (jnp.minimum(n, nb - 1), 0)),
            pl.BlockSpec(memory_space=pltpu.SMEM)],
        out_specs=[pl.BlockSpec((ATTN_BLOCK, UA_W), lambda n: (jnp.maximum(n - 1, 0), 0)),
                   pl.BlockSpec((UA_W, ATTN_BLOCK), lambda n: (0, jnp.maximum(n - 1, 0))),
                   pl.BlockSpec((1, UA_W), lambda n: (0, 0)),
                   pl.BlockSpec((1, LANES), lambda n: (0, 0))],
        out_shape=[jax.ShapeDtypeStruct((t, UA_W), BF16), jax.ShapeDtypeStruct((UA_W, t), BF16),
                   jax.ShapeDtypeStruct((1, UA_W), F32),
                   jax.ShapeDtypeStruct((1, LANES), F32)],
        scratch_shapes=[pltpu.VMEM((ATTN_BLOCK, ATTN_W), F32), pltpu.VMEM((ATTN_BLOCK, KV_W), F32),
                        pltpu.VMEM((ATTN_BLOCK, KV_W), F32), pltpu.VMEM((ATTN_BLOCK, ATTN_W), F32),
                        pltpu.VMEM((2 * ATTN_BLOCK, KV_W), F32), pltpu.VMEM((2 * ATTN_BLOCK, KV_W), F32)],
        sem=("arbitrary",), comm=comm)


def _tri_mats():
    r = lax.broadcasted_iota(jnp.int32, (HGRN_CHUNK, LANES), 0)
    c = lax.broadcasted_iota(jnp.int32, (HGRN_CHUNK, LANES), 1)
    lower = ((c <= r) & (c < HGRN_CHUNK)).astype(F32)
    upper = ((c >= r) & (c < HGRN_CHUNK)).astype(F32)
    return lower, upper


def _tri_apply(tri, g):
    pad = jnp.concatenate([g, jnp.zeros_like(g)], axis=0)
    return lax.dot_general(tri, pad, _NN, precision=lax.Precision.HIGHEST, preferred_element_type=F32)


def _sub_masks():
    s = lax.broadcasted_iota(jnp.int32, (HGRN_CHUNK, LANES), 0)
    tt = lax.broadcasted_iota(jnp.int32, (HGRN_CHUNK, LANES), 1)
    return [(tt >= HGRN_SUB * i) & (tt < HGRN_SUB * (i + 1)) & (s <= tt) for i in range(HGRN_CHUNK // HGRN_SUB)]


def _hgrn_gates(hq, hf, lb_ref, b_scr):
    lb = _sig(lb_ref[0:1, :] - lb_ref[1:2, :])
    q = hq * _sig(hq)
    sg = _sig(hf)
    f = lb + (1.0 - lb) * sg
    k = 1.0 - f
    lower, _ = _tri_mats()
    b = _tri_apply(lower, jnp.log(f))
    b_scr[...] = b
    nsub = HGRN_CHUNK // HGRN_SUB
    starts = [jnp.zeros((1, HG_W), F32)] + [b_scr[HGRN_SUB * i - 1:HGRN_SUB * i, :] for i in range(1, nsub)]
    pq = jnp.concatenate([jnp.broadcast_to(p, (HGRN_SUB, HG_W)) for p in starts], axis=0)
    b_last = b_scr[HGRN_CHUNK - 1:HGRN_CHUNK, :]
    e_q = jnp.exp(b - pq)
    e_k = [jnp.exp(jnp.minimum(p - b, EXP_CLAMP)) for p in starts]
    e_b = jnp.exp(b)
    e_bl = jnp.exp(b_last - b)
    e_last = jnp.exp(b_last)
    return q, sg, f, k, lb, e_q, e_k, e_b, e_bl, e_last


def _sub_masks_ts():
    tt = lax.broadcasted_iota(jnp.int32, (HGRN_CHUNK, LANES), 0)
    s = lax.broadcasted_iota(jnp.int32, (HGRN_CHUNK, LANES), 1)
    return [(tt >= HGRN_SUB * i) & (tt < HGRN_SUB * (i + 1)) & (s <= tt) for i in range(HGRN_CHUNK // HGRN_SUB)]


def _masked_sum(blocks, masks, axis):
    step = HGRN_CHUNK if axis == 0 else LANES
    acc = jnp.zeros((HGRN_CHUNK, LANES), F32)
    for i, msk in enumerate(masks):
        blk = blocks[step * i:step * (i + 1), :] if axis == 0 else blocks[:, step * i:step * (i + 1)]
        acc = acc + jnp.where(msk, blk, 0.0)
    return acc


def _store_transposed(out_t_ref, chunk_rows):
    width = chunk_rows[0].shape[1]
    if len(chunk_rows) == 1:
        groups = [jnp.concatenate([chunk_rows[0], jnp.zeros_like(chunk_rows[0])], axis=0)]
    else:
        groups = [jnp.concatenate(chunk_rows[g:g + 2], axis=0) for g in range(0, len(chunk_rows), 2)]
    for g, rows in enumerate(groups):
        for c in range(width // LANES):
            tile = rows[:, c * LANES:(c + 1) * LANES].T.astype(out_t_ref.dtype)
            if len(chunk_rows) == 1:
                out_t_ref[c * LANES:(c + 1) * LANES, :] = tile[:, 0:HGRN_CHUNK]
            else:
                out_t_ref[c * LANES:(c + 1) * LANES, g * LANES:(g + 1) * LANES] = tile


def _hgrn_chunk_inputs(j, hq_ref, hf_ref, hi_ref, hg_ref, lb_ref, b_scr):
    rows = slice(j * HGRN_CHUNK, (j + 1) * HGRN_CHUNK)
    hq, hf, v, hg = hq_ref[rows, :], hf_ref[rows, :], hi_ref[rows, :], hg_ref[rows, :]
    q, sg, f, k, lb, e_q, e_k, e_b, e_bl, e_last = _hgrn_gates(hq, hf, lb_ref, b_scr.at[j])
    return dict(rows=rows, hq=hq, v=v, hg=hg, q=q, sg=sg, f=f, k=k, lb=lb, e_q=e_q, e_k=e_k, e_b=e_b, e_bl=e_bl,
                e_last=e_last, qt=q * e_q, qb=q * e_b, kd=k * e_bl, khat=[k * e for e in e_k])


def _hgrn_fwd(uh, lb_raw, norm_g, name, comm=None):
    t = uh.shape[0]
    nc = t // HGRN_CHUNK
    cps = _pick(nc, (HGRN_CHUNKS_PER_STEP, 2, 1))
    rows_step = cps * HGRN_CHUNK

    def body(hq_ref, hf_ref, hi_ref, hg_ref, lb_ref, ng_ref, r_ref, r_t_ref, o_ref, st_out_ref, st_ref, b_scr):
        @pl.when(pl.program_id(0) == 0)
        def _():
            st_ref[...] = jnp.zeros_like(st_ref)

        masks = _sub_masks_ts()
        ng = ng_ref[...]
        zpad = jnp.zeros((HGRN_CHUNK, LANES), F32)
        heads = [slice(h * LANES, (h + 1) * LANES) for h in range(4)]
        chunks = [_hgrn_chunk_inputs(j, hq_ref, hf_ref, hi_ref, hg_ref, lb_ref, b_scr) for j in range(cps)]
        for ch in chunks:
            ch["scores"] = [_dot3(ch["qt"][:, sl],
                                  jnp.concatenate([x for kh in ch["khat"] for x in (kh[:, sl], zpad)], axis=0), _NT)
                            for sl in heads]
        for j, ch in enumerate(chunks):
            o_heads, y_heads = [], []
            for h, sl in enumerate(heads):
                a_ts = _masked_sum(ch["scores"][h], masks, 1)
                vh = ch["v"][:, sl].astype(BF16)
                v_pad = jnp.concatenate([vh, jnp.zeros_like(vh)], axis=0)
                o_intra = lax.dot_general(a_ts.astype(BF16), v_pad, _NN, preferred_element_type=F32)
                st = st_ref[h]
                st_out_ref[j, h] = st
                o_inter = _dot(ch["qb"][:, sl], st, _NT)
                st_ref[h] = st * ch["e_last"][:, sl] + _dot(vh, ch["kd"][:, sl], _TN)
                oh = o_intra + o_inter
                rs = lax.rsqrt(jnp.mean(oh * oh, axis=1, keepdims=True) + RMS_EPS)
                o_heads.append(oh)
                y_heads.append(oh * rs * ng)
            hg = ch["hg"]
            o_ref[ch["rows"], :] = jnp.concatenate(o_heads, axis=1)
            ch["r"] = jnp.concatenate(y_heads, axis=1) * (hg * _sig(hg))
            r_ref[ch["rows"], :] = ch["r"].astype(r_ref.dtype)
        _store_transposed(r_t_ref, [ch["r"] for ch in chunks])

    col = lambda j: pl.BlockSpec((rows_step, HG_W), lambda c: (c, j))
    return _call(
        body, name=name, grid=(nc // cps,), ins=[uh, uh, uh, uh, lb_raw, norm_g],
        in_specs=[col(0), col(1), col(2), col(3),
                  pl.BlockSpec((2, HG_W), lambda c: (0, 0)), pl.BlockSpec((1, LANES), lambda c: (0, 0))],
        out_specs=[pl.BlockSpec((rows_step, HG_W), lambda c: (c, 0)),
                   pl.BlockSpec((HG_W, rows_step), lambda c: (0, c)),
                   pl.BlockSpec((rows_step, HG_W), lambda c: (c, 0)),
                   pl.BlockSpec((cps, 4, LANES, LANES), lambda c: (c, 0, 0, 0))],
        out_shape=[jax.ShapeDtypeStruct((t, HG_W), BF16), jax.ShapeDtypeStruct((HG_W, t), BF16),
                   jax.ShapeDtypeStruct((t, HG_W), F32), jax.ShapeDtypeStruct((nc, 4, LANES, LANES), F32)],
        scratch_shapes=[pltpu.VMEM((4, LANES, LANES), F32), pltpu.VMEM((cps, HGRN_CHUNK, HG_W), F32)],
        sem=("arbitrary",), comm=comm)


def _hgrn_bwd(uh, o_pre, d_r, states, lb_raw, norm_g, name, comm=None):
    t = uh.shape[0]
    nc = t // HGRN_CHUNK
    cps = _pick(nc, (HGRN_CHUNKS_PER_STEP, 2, 1))
    ns = nc // cps
    rows_step = cps * HGRN_CHUNK
    nsub = HGRN_CHUNK // HGRN_SUB

    def body(hq_ref, hf_ref, hi_ref, hg_ref, o_ref, dr_ref, st_in_ref, lb_ref, ng_ref,
             duh_ref, duh_t_ref, dbias_ref, dng_ref, dlb_ref, dst_ref, b_scr, dlb_acc):
        i = pl.program_id(0)

        @pl.when(i == 0)
        def _():
            dst_ref[...] = jnp.zeros_like(dst_ref)
            dbias_ref[...] = jnp.zeros_like(dbias_ref)
            dng_ref[...] = jnp.zeros_like(dng_ref)
            dlb_acc[...] = jnp.zeros_like(dlb_acc)

        masks_st = _sub_masks()
        masks_ts = _sub_masks_ts()
        ng = ng_ref[...]
        zpad = jnp.zeros((HGRN_CHUNK, LANES), F32)
        _, upper = _tri_mats()
        heads = [slice(h * LANES, (h + 1) * LANES) for h in range(4)]
        row = lax.broadcasted_iota(jnp.int32, (HGRN_CHUNK, HG_W), 0)

        chunks = [_hgrn_chunk_inputs(j, hq_ref, hf_ref, hi_ref, hg_ref, lb_ref, b_scr) for j in range(cps)]
        dng = jnp.zeros((1, LANES), F32)
        for ch in chunks:
            o = o_ref[ch["rows"], :]
            dr = dr_ref[ch["rows"], :].astype(F32)
            hg = ch["hg"]
            sgg = _sig(hg)
            dy = dr * (hg * sgg)
            do_h, y_h = [], []
            for sl in heads:
                oh = o[:, sl]
                rs = lax.rsqrt(jnp.mean(oh * oh, axis=1, keepdims=True) + RMS_EPS)
                y_h.append(oh * rs * ng)
                dng = dng + jnp.sum(dy[:, sl] * oh * rs, axis=0, keepdims=True)
                w = dy[:, sl] * ng
                do_h.append(rs * (w - oh * (rs * rs) * jnp.mean(w * oh, axis=1, keepdims=True)))
            ch["do"] = do_h
            ch["dhg"] = dr * jnp.concatenate(y_h, axis=1) * _dsilu(hg, sgg)

        for ch in chunks:
            ch["kst"], ch["kpad"], ch["qt_pad"], ch["v_b"], ch["do_pad"] = [], [], [], [], []
            ch["ats"], ch["d_at"], ch["d_a"] = [], [], []
            for h, sl in enumerate(heads):
                kst = jnp.concatenate([kh[:, sl] for kh in ch["khat"]], axis=0)
                kpad = jnp.concatenate([x for kh in ch["khat"] for x in (kh[:, sl], zpad)], axis=0)
                qt_pad = jnp.concatenate([ch["qt"][:, sl], zpad], axis=0)
                vh = ch["v"][:, sl].astype(BF16)
                v_pad = jnp.concatenate([vh, jnp.zeros_like(vh)], axis=0)
                do_b = ch["do"][h].astype(BF16)
                do_pad = jnp.concatenate([do_b, jnp.zeros_like(do_b)], axis=0)
                ch["kst"].append(kst)
                ch["kpad"].append(kpad)
                ch["qt_pad"].append(qt_pad)
                ch["v_b"].append(vh)
                ch["do_pad"].append(do_pad)
                ch["ats"].append(_dot3(kst, qt_pad, _NT))
                ch["d_at"].append(lax.dot_general(vh, do_pad, _NT, preferred_element_type=F32))
                ch["d_a"].append(lax.dot_general(do_b, v_pad, _NT, preferred_element_type=F32))

        for ch in chunks:
            ch["d_kst"], ch["d_qt"], ch["dv"] = [], [], []
            for h in range(4):
                at = _masked_sum(ch["ats"][h], masks_st, 0)
                d_ats = jnp.concatenate([jnp.where(m, ch["d_at"][h], 0.0) for m in masks_st], axis=0)
                d_a_cat = jnp.concatenate([jnp.where(m, ch["d_a"][h], 0.0) for m in masks_ts], axis=1)
                ch["d_kst"].append(_dot3(d_ats, ch["qt_pad"][h], _NN))
                ch["d_qt"].append(_dot3(d_a_cat, ch["kpad"][h], _NN))
                ch["dv"].append(lax.dot_general(at.astype(BF16), ch["do_pad"][h], _NN, preferred_element_type=F32))

        for j in reversed(range(cps)):
            ch = chunks[j]
            q, k, sg, f, lb = ch["q"], ch["k"], ch["sg"], ch["f"], ch["lb"]
            dq_h, dk_h, dv_h, extra_h = [], [], [], []
            for h, sl in enumerate(heads):
                st_prev = st_in_ref[j, h]
                d_st = dst_ref[h]
                d_st_b = d_st.astype(BF16)
                do_b = ch["do_pad"][h][0:HGRN_CHUNK, :]
                kd, e_last = ch["kd"][:, sl], ch["e_last"][:, sl]
                dv = ch["dv"][h] + _dot(kd, d_st_b, _NT)
                d_qb = _dot(do_b, st_prev, _NN)
                d_kd = lax.dot_general(ch["v_b"][h], d_st_b, _NN, preferred_element_type=F32)
                extra_h.append(jnp.sum(st_prev * d_st, axis=0, keepdims=True) * e_last
                               + jnp.sum(kd * d_kd, axis=0, keepdims=True))
                dst_ref[h] = d_st * e_last + _dot(do_b, ch["qb"][:, sl], _TN)
                dq_h.append(ch["d_qt"][h] * ch["e_q"][:, sl] + d_qb * ch["e_b"][:, sl])
                dkk = d_kd * ch["e_bl"][:, sl]
                for s_ in range(nsub):
                    dkk = dkk + ch["d_kst"][h][HGRN_CHUNK * s_:HGRN_CHUNK * (s_ + 1), :] * ch["e_k"][s_][:, sl]
                dk_h.append(dkk)
                dv_h.append(dv)
            dq = jnp.concatenate(dq_h, axis=1)
            dk = jnp.concatenate(dk_h, axis=1)
            dv = jnp.concatenate(dv_h, axis=1)
            extra = jnp.concatenate(extra_h, axis=1)
            db = q * dq - k * dk + jnp.where(row == HGRN_CHUNK - 1, extra, 0.0)
            dg = _tri_apply(upper, db)
            df = dg / f - dk
            dhf = df * (1.0 - lb) * sg * (1.0 - sg)
            dhq = dq * _dsilu(ch["hq"], _sig(ch["hq"]))
            full = jnp.concatenate([dhq, dhf, dv, ch["dhg"]], axis=1)
            duh_ref[ch["rows"], :] = full.astype(duh_ref.dtype)
            ch["full"] = full
            dbias_ref[...] += jnp.sum(full, axis=0, keepdims=True)
            dlb_acc[...] += jnp.sum(df * (1.0 - sg), axis=0, keepdims=True)
        dng_ref[...] += dng
        _store_transposed(duh_t_ref, [ch["full"] for ch in chunks])

        @pl.when(i == ns - 1)
        def _():
            lb = chunks[0]["lb"]
            d_a0 = dlb_acc[...] * lb * (1.0 - lb)
            r8 = lax.broadcasted_iota(jnp.int32, (8, HG_W), 0)
            dlb_ref[...] = jnp.where(r8 == 0, d_a0, jnp.where(r8 == 1, -d_a0, 0.0))

    col = lambda j: pl.BlockSpec((rows_step, HG_W), lambda i: (ns - 1 - i, j))
    return _call(
        body, name=name, grid=(ns,), ins=[uh, uh, uh, uh, o_pre, d_r, states, lb_raw, norm_g],
        in_specs=[col(0), col(1), col(2), col(3), col(0), col(0),
                  pl.BlockSpec((cps, 4, LANES, LANES), lambda i: (ns - 1 - i, 0, 0, 0)),
                  pl.BlockSpec((2, HG_W), lambda i: (0, 0)), pl.BlockSpec((1, LANES), lambda i: (0, 0))],
        out_specs=[pl.BlockSpec((rows_step, UH_W), lambda i: (ns - 1 - i, 0)),
                   pl.BlockSpec((UH_W, rows_step), lambda i: (0, ns - 1 - i)),
                   pl.BlockSpec((1, UH_W), lambda i: (0, 0)),
                   pl.BlockSpec((1, LANES), lambda i: (0, 0)),
                   pl.BlockSpec((8, HG_W), lambda i: (0, 0))],
        out_shape=[jax.ShapeDtypeStruct((t, UH_W), BF16), jax.ShapeDtypeStruct((UH_W, t), BF16),
                   jax.ShapeDtypeStruct((1, UH_W), F32),
                   jax.ShapeDtypeStruct((1, LANES), F32), jax.ShapeDtypeStruct((8, HG_W), F32)],
        scratch_shapes=[pltpu.VMEM((4, LANES, LANES), F32), pltpu.VMEM((cps, HGRN_CHUNK, HG_W), F32),
                        pltpu.VMEM((1, HG_W), F32)],
        sem=("arbitrary",), comm=comm)


def _ln_bwd_math(dy, xhat, rstd, g):
    dxh = dy * g
    return rstd * (dxh - jnp.mean(dxh, axis=1, keepdims=True)
                   - xhat * jnp.mean(dxh * xhat, axis=1, keepdims=True))


def _mm_rows(a, b, extras, *, name, epilogue, out_shape, out_specs, tb=False, tm=512, tk=1408):
    m, kdim = a.shape
    n = b.shape[0] if tb else b.shape[1]
    tm = _pick(m, (tm, 256, 128))
    tk = _pick(kdim, (tk, 1408, 1024, 768, 512, 256, 128))
    nk = kdim // tk
    b_spec = pl.BlockSpec((n, tk), lambda i, k: (0, k)) if tb else pl.BlockSpec((tk, n), lambda i, k: (k, 0))
    dims = _NT if tb else _NN
    n_ex, n_out = len(extras), len(out_shape)

    def body(*refs):
        a_ref, b_ref = refs[0], refs[1]
        ex_refs = refs[2:2 + n_ex]
        o_refs = refs[2 + n_ex:2 + n_ex + n_out]
        acc_ref = refs[-1]
        i, k = pl.program_id(0), pl.program_id(1)

        @pl.when(k == 0)
        def _():
            acc_ref[...] = jnp.zeros_like(acc_ref)

        acc_ref[...] += _dot(a_ref[...], b_ref[...], dims)

        @pl.when(k == nk - 1)
        def _():
            epilogue(acc_ref[...], ex_refs, o_refs, i == 0)

    return pl.pallas_call(
        body, name=name, grid=(m // tm, nk),
        in_specs=[pl.BlockSpec((tm, tk), lambda i, k: (i, k)), b_spec] + [sp for _, sp in extras],
        out_specs=list(out_specs), out_shape=list(out_shape),
        scratch_shapes=[pltpu.VMEM((tm, n), F32)],
        compiler_params=_cp("arbitrary", "arbitrary"),
    )(a, b, *[arr for arr, _ in extras])


def _rows_specs(tm, d):
    row = pl.BlockSpec((tm, d), lambda i, k: (i, 0))
    vec = pl.BlockSpec((1, d), lambda i, k: (0, 0))
    col = pl.BlockSpec((tm, 1), lambda i, k: (i, 0))
    return row, vec, col


def _mm_ln_fwd(a, b, addend, g, beta, name, tm=512):
    t, d = addend.shape
    tm = _pick(t, (tm, 256, 128))
    row, vec, col = _rows_specs(tm, d)

    def epilogue(acc, ex, outs, first):
        z = acc + ex[0][...]
        mu = jnp.mean(z, axis=1, keepdims=True)
        zc = z - mu
        rstd = lax.rsqrt(jnp.mean(zc * zc, axis=1, keepdims=True) + LN_EPS)
        xhat = zc * rstd
        h = xhat * ex[1][...] + ex[2][...]
        outs[0][...] = h
        outs[1][...] = h.astype(BF16)
        outs[2][...] = xhat
        outs[3][...] = rstd

    return _mm_rows(a, b, [(addend, row), (g, vec), (beta, vec)], name=name, epilogue=epilogue, tm=tm,
                    out_shape=[jax.ShapeDtypeStruct((t, d), F32), jax.ShapeDtypeStruct((t, d), BF16),
                               jax.ShapeDtypeStruct((t, d), F32), jax.ShapeDtypeStruct((t, 1), F32)],
                    out_specs=[row, row, row, col])


def _mm_ln_loss_bwd(a, b, addend, target, g, beta, name, tm=1024):
    t, d = addend.shape
    tm = _pick(t, (tm, 256, 128))
    row, vec, _ = _rows_specs(tm, d)

    def epilogue(acc, ex, outs, first):
        dz_ref, dg_ref, db_ref, loss_ref = outs

        @pl.when(first)
        def _():
            dg_ref[...] = jnp.zeros_like(dg_ref)
            db_ref[...] = jnp.zeros_like(db_ref)
            loss_ref[...] = jnp.zeros_like(loss_ref)

        z = acc + ALPHA * ex[0][...]
        gg = ex[2][...]
        mu = jnp.mean(z, axis=1, keepdims=True)
        zc = z - mu
        rstd = lax.rsqrt(jnp.mean(zc * zc, axis=1, keepdims=True) + LN_EPS)
        xhat = zc * rstd
        err = xhat * gg + ex[3][...] - ex[1][...]
        loss_ref[...] += 0.5 * jnp.sum(jnp.mean(err * err, axis=1, keepdims=True))
        dy = err * (1.0 / d)
        dz_ref[...] = _ln_bwd_math(dy, xhat, rstd, gg)
        dg_ref[...] += jnp.sum(dy * xhat, axis=0, keepdims=True)
        db_ref[...] += jnp.sum(dy, axis=0, keepdims=True)

    return _mm_rows(a, b, [(addend, row), (target, row), (g, vec), (beta, vec)], name=name, epilogue=epilogue,
                    tm=tm,
                    out_shape=[jax.ShapeDtypeStruct((t, d), F32), jax.ShapeDtypeStruct((1, d), F32),
                               jax.ShapeDtypeStruct((1, d), F32), jax.ShapeDtypeStruct((1, LANES), F32)],
                    out_specs=[row, vec, vec, pl.BlockSpec((1, LANES), lambda i, k: (0, 0))])


def _mm_ln_bwd(a, b, addend, xhat, rstd, g, name, tm=1024):
    t, d = addend.shape
    tm = _pick(t, (tm, 256, 128))
    row, vec, col = _rows_specs(tm, d)

    def epilogue(acc, ex, outs, first):
        dz_ref, dg_ref, db_ref = outs

        @pl.when(first)
        def _():
            dg_ref[...] = jnp.zeros_like(dg_ref)
            db_ref[...] = jnp.zeros_like(db_ref)

        dy = acc + ALPHA * ex[0][...]
        xh = ex[1][...]
        dz_ref[...] = _ln_bwd_math(dy, xh, ex[2][...], ex[3][...])
        dg_ref[...] += jnp.sum(dy * xh, axis=0, keepdims=True)
        db_ref[...] += jnp.sum(dy, axis=0, keepdims=True)

    return _mm_rows(a, b, [(addend, row), (xhat, row), (rstd, col), (g, vec)], name=name, epilogue=epilogue, tm=tm,
                    out_shape=[jax.ShapeDtypeStruct((t, d), F32), jax.ShapeDtypeStruct((1, d), F32),
                               jax.ShapeDtypeStruct((1, d), F32)],
                    out_specs=[row, vec, vec])


CONV_TILE = 128
CONV_RB = 32
HALO = 8


def _sum8(x):
    acc = x[0:8]
    for r in range(8, x.shape[0], 8):
        acc = acc + x[r:r + 8]
    return acc


def _conv_fwd(u2, conv_w, conv_b, name):
    t = u2.shape[0]
    tr = _pick(t, (CONV_TILE,))
    hb = tr // HALO
    rb = CONV_RB

    def body(gp_ref, val_ref, prev_ref, w_ref, b_ref, out_ref, out_t_ref, ext):
        i = pl.program_id(0)
        ext[0:HALO, :] = jnp.where(i == 0, 0.0, prev_ref[...])
        ext[HALO:, :] = gp_ref[...]
        for c in range(D_FF // LANES):
            ln = slice(c * LANES, (c + 1) * LANES)
            w0, w1, w2, bb = w_ref[0:1, ln], w_ref[1:2, ln], w_ref[2:3, ln], b_ref[:, ln]
            pieces = []
            for r0 in range(0, tr, rb):
                gate = (ext[r0 + HALO - 2:r0 + HALO - 2 + rb, ln] * w0 + ext[r0 + HALO - 1:r0 + HALO - 1 + rb, ln] * w1
                        + ext[r0 + HALO:r0 + HALO + rb, ln] * w2 + bb)
                hm = gate * _sig(gate) * val_ref[r0:r0 + rb, ln]
                out_ref[r0:r0 + rb, ln] = hm.astype(out_ref.dtype)
                pieces.append(hm)
            out_t_ref[ln, :] = jnp.concatenate(pieces, axis=0).T.astype(out_t_ref.dtype)

    return pl.pallas_call(
        body, name=name, grid=(t // tr,),
        in_specs=[pl.BlockSpec((tr, D_FF), lambda i: (i, 0)), pl.BlockSpec((tr, D_FF), lambda i: (i, 1)),
                  pl.BlockSpec((HALO, D_FF), lambda i: (jnp.maximum(i * hb - 1, 0), 0)),
                  pl.BlockSpec((3, D_FF), lambda i: (0, 0)), pl.BlockSpec((1, D_FF), lambda i: (0, 0))],
        out_specs=[pl.BlockSpec((tr, D_FF), lambda i: (i, 0)), pl.BlockSpec((D_FF, tr), lambda i: (0, i))],
        out_shape=[jax.ShapeDtypeStruct((t, D_FF), BF16), jax.ShapeDtypeStruct((D_FF, t), BF16)],
        scratch_shapes=[pltpu.VMEM((tr + HALO, D_FF), F32)],
        compiler_params=_cp("parallel"),
    )(u2, u2, u2, conv_w, conv_b)


def _conv_bwd(d_hmid, u2, conv_w, conv_b, name, comm=None):
    t = u2.shape[0]
    tr = _pick(t, (CONV_TILE,))
    hb = tr // HALO
    last = t // HALO - 1
    rb = CONV_RB
    re = rb + HALO

    def body(gp_ref, gp_prev_ref, gp_next_ref, val_ref, val_next_ref, dh_ref, dh_next_ref, w_ref, b_ref,
             du_ref, du_t_ref, dw_ref, dcb_ref, ext, dg_s):
        i = pl.program_id(0)

        @pl.when(i == 0)
        def _():
            dw_ref[...] = jnp.zeros_like(dw_ref)
            dcb_ref[...] = jnp.zeros_like(dcb_ref)

        ext[0:HALO, :] = jnp.where(i == 0, 0.0, gp_prev_ref[...])
        ext[HALO:HALO + tr, :] = gp_ref[...]
        ext[HALO + tr:, :] = gp_next_ref[...]
        next_in_seq = (i + 1) * tr < t
        for c in range(D_FF // LANES):
            ln = slice(c * LANES, (c + 1) * LANES)
            w0, w1, w2, bb = w_ref[0:1, ln], w_ref[1:2, ln], w_ref[2:3, ln], b_ref[:, ln]
            acc_b = jnp.zeros((8, LANES), F32)
            acc_w = [jnp.zeros((8, LANES), F32) for _ in range(3)]
            gp_pieces, val_pieces = [], []
            for r0 in range(0, tr, rb):
                g_m2 = ext[r0 + HALO - 2:r0 + HALO - 2 + re, ln]
                g_m1 = ext[r0 + HALO - 1:r0 + HALO - 1 + re, ln]
                g_0 = ext[r0 + HALO:r0 + HALO + re, ln]
                gate = g_m2 * w0 + g_m1 * w1 + g_0 * w2 + bb
                sg = _sig(gate)
                if r0 + re <= tr:
                    val = val_ref[r0:r0 + re, ln]
                    dh = dh_ref[r0:r0 + re, ln]
                else:
                    val = jnp.concatenate([val_ref[r0:r0 + rb, ln], val_next_ref[:, ln]], axis=0)
                    dh = jnp.concatenate([dh_ref[r0:r0 + rb, ln],
                                          jnp.where(next_in_seq, dh_next_ref[:, ln], 0.0)], axis=0)
                dgate = dh * val * _dsilu(gate, sg)
                dg_s[:, ln] = dgate
                dg0 = dgate[0:rb]
                d_gp = dg_s[2:2 + rb, ln] * w0 + dg_s[1:1 + rb, ln] * w1 + dg0 * w2
                d_val = dh[0:rb] * (gate[0:rb] * sg[0:rb])
                du_ref[r0:r0 + rb, ln] = d_gp.astype(du_ref.dtype)
                du_ref[r0:r0 + rb, D_FF + c * LANES:D_FF + (c + 1) * LANES] = d_val.astype(du_ref.dtype)
                gp_pieces.append(d_gp)
                val_pieces.append(d_val)
                acc_b = acc_b + _sum8(dg0)
                acc_w[0] = acc_w[0] + _sum8(dg0 * g_m2[0:rb])
                acc_w[1] = acc_w[1] + _sum8(dg0 * g_m1[0:rb])
                acc_w[2] = acc_w[2] + _sum8(dg0 * g_0[0:rb])
            du_t_ref[ln, :] = jnp.concatenate(gp_pieces, axis=0).T.astype(du_t_ref.dtype)
            du_t_ref[D_FF + c * LANES:D_FF + (c + 1) * LANES, :] = (
                jnp.concatenate(val_pieces, axis=0).T.astype(du_t_ref.dtype))
            dcb_ref[:, ln] += jnp.sum(acc_b, axis=0, keepdims=True)
            for j in range(3):
                dw_ref[j:j + 1, ln] += jnp.sum(acc_w[j], axis=0, keepdims=True)

    cur = lambda col: pl.BlockSpec((tr, D_FF), lambda i: (i, col))
    nxt = lambda col: pl.BlockSpec((HALO, D_FF), lambda i: (jnp.minimum((i + 1) * hb, last), col))
    return _call(
        body, name=name, grid=(t // tr,), ins=[u2, u2, u2, u2, u2, d_hmid, d_hmid, conv_w, conv_b],
        in_specs=[cur(0), pl.BlockSpec((HALO, D_FF), lambda i: (jnp.maximum(i * hb - 1, 0), 0)), nxt(0),
                  cur(1), nxt(1), cur(0), nxt(0),
                  pl.BlockSpec((3, D_FF), lambda i: (0, 0)), pl.BlockSpec((1, D_FF), lambda i: (0, 0))],
        out_specs=[pl.BlockSpec((tr, 2 * D_FF), lambda i: (i, 0)), pl.BlockSpec((2 * D_FF, tr), lambda i: (0, i)),
                   pl.BlockSpec((8, D_FF), lambda i: (0, 0)), pl.BlockSpec((1, D_FF), lambda i: (0, 0))],
        out_shape=[jax.ShapeDtypeStruct((t, 2 * D_FF), BF16), jax.ShapeDtypeStruct((2 * D_FF, t), BF16),
                   jax.ShapeDtypeStruct((8, D_FF), F32), jax.ShapeDtypeStruct((1, D_FF), F32)],
        scratch_shapes=[pltpu.VMEM((tr + 2 * HALO, D_FF), F32), pltpu.VMEM((re, D_FF), F32)],
        sem=("arbitrary",), comm=comm)


def _adamw(w, g, m, v, name):
    rows, cols = w.shape
    tr = _pick(rows, (256, 128, 64, 32, 16, 8))

    def body(w_ref, g_ref, m_ref, v_ref, d_ref, nm_ref, nv_ref):
        d_ref[...], nm_ref[...], nv_ref[...] = _adamw_math(w_ref[...], g_ref[...], m_ref[...], v_ref[...])

    spec = pl.BlockSpec((tr, cols), lambda i: (i, 0))
    shp = jax.ShapeDtypeStruct((rows, cols), F32)
    return pl.pallas_call(
        body, name=name, grid=(rows // tr,),
        in_specs=[spec, spec, spec, spec], out_specs=[spec, spec, spec], out_shape=[shp, shp, shp],
        compiler_params=_cp("parallel"),
    )(w, g, m, v)


def _pad_rows(a, rows):
    return jnp.pad(a, ((0, rows - a.shape[0]), (0, 0)))


SMALL_LAYOUT = (("ln1_g", 1024), ("ln1_b", 1024), ("b_in", 2816), ("sinks", 8), ("hgrn_lb", 1024),
                ("hgrn_norm_g", 128), ("ln2_g", 1024), ("ln2_b", 1024), ("conv_b", 2816), ("loss", 1))
SMALL_SHAPES = {"ln1_g": (1, 1024), "ln1_b": (1, 1024), "b_in": (1, 2816), "sinks": (1, 8), "hgrn_lb": (2, 512),
                "hgrn_norm_g": (1, 128), "ln2_g": (1, 1024), "ln2_b": (1, 1024), "conv_b": (1, 2816),
                "loss": (1,)}


def _pack_small(parts):
    rows = []
    for name, size in SMALL_LAYOUT:
        flat = parts[name].reshape(-1).astype(F32)
        padded = -(-size // LANES) * LANES
        rows.append(jnp.pad(flat, (0, padded - size)).reshape(-1, LANES))
    return _pad_rows(jnp.concatenate(rows, axis=0), SMALL_ROWS)


def _unpack_small(pack):
    out, r = {}, 0
    for name, size in SMALL_LAYOUT:
        nrows = -(-size // LANES)
        out[name] = pack[r:r + nrows].reshape(-1)[:size].reshape(SMALL_SHAPES[name])
        r += nrows
    return out


def _own(full, rows):
    return lax.dynamic_slice_in_dim(full, _me() * rows, rows, axis=0)


def kernel(x, positions, ln1_g, ln1_b, w_in, b_in, sinks, hgrn_lb, hgrn_norm_g, w_o, ln2_g, ln2_b, w_up, conv_w, conv_b, w_down, loss_target, m_ln1_g, m_ln1_b, m_w_in, m_b_in, m_sinks, m_hgrn_lb, m_hgrn_norm_g, m_w_o, m_ln2_g, m_ln2_b, m_w_up, m_conv_w, m_conv_b, m_w_down, v_ln1_g, v_ln1_b, v_w_in, v_b_in, v_sinks, v_hgrn_lb, v_hgrn_norm_g, v_w_o, v_ln2_g, v_ln2_b, v_w_up, v_conv_w, v_conv_b, v_w_down):
    t = x.shape[1]
    x2 = x[0]
    xb = x2.astype(BF16)
    target = loss_target[0]
    pos_col = positions.reshape(t, 1)

    w_in_t_s = w_in[0].T.astype(BF16)
    w_up_t_s = w_up[0].T.astype(BF16)
    w_o_s = w_o[0].astype(BF16)
    w_down_s = w_down[0].astype(BF16)
    w_in_t_g, cw_g = _comm_only(_Comm("gather", [w_in_t_s, _pad_rows(conv_w[0], 8)]), "ag_w_in")
    w_in_t = w_in_t_g.reshape(D_FF, D_MODEL)
    w_a_t, w_h_t = w_in_t[:UA_W], w_in_t[UA_W:]
    conv_w_f = cw_g[:, 0:3].transpose(1, 0, 2).reshape(3, D_FF)

    ua = _mm(xb, w_a_t, tb=True, bias=b_in[:, :UA_W], name="fwd_in_attn")
    uh = _mm(xb, w_h_t, tb=True, bias=b_in[:, UA_W:], name="fwd_in_hgrn")
    ctab, stab = _rope_tables(pos_col, "rope_tables")
    (a_out, a_out_t), (w_o_g,) = _attn_fwd(ua, ctab, stab, sinks, "attn_fwd", comm=_Comm("gather", [w_o_s]))
    (r_out, r_out_t, o_pre, states), (w_up_t_g,) = _hgrn_fwd(uh, hgrn_lb, hgrn_norm_g, "hgrn_fwd",
                                                              comm=_Comm("gather", [w_up_t_s]))
    w_o_f = w_o_g.reshape(D_MODEL, D_MODEL)
    w_up_t = w_up_t_g.reshape(2 * D_FF, D_MODEL)
    z1 = _mm(a_out, w_o_f[:ATTN_W], addend=x2, addend_scale=ALPHA, name="fwd_o_attn")
    h1, h1b, xhat1, rstd1 = _mm_ln_fwd(r_out, w_o_f[ATTN_W:], z1, ln1_g, ln1_b, "fwd_o_hgrn_ln1")
    u2, (w_down_g,) = _mm(h1b, w_up_t, tb=True, tn=1408, name="fwd_up", comm=_Comm("gather", [w_down_s]))
    w_down_f = w_down_g.reshape(D_FF, D_MODEL)
    hmid, hmid_t = _conv_fwd(u2, conv_w_f, conv_b, "conv_fwd")
    dz2, d_ln2_g, d_ln2_b, loss_part = _mm_ln_loss_bwd(hmid, w_down_f, h1, target, ln2_g, ln2_b,
                                                       "fwd_down_ln2_loss")

    d_hmid = _mm(dz2, w_down_f, tb=True, tn=1408, name="bwd_down_dx")
    d_w_down, d_w_down_b = _mm(hmid_t, dz2, out_dtype2=BF16, tm=1408, name="bwd_down_dw")
    (d_u2, d_u2_t, d_conv_w8, d_conv_b), (recv_down,) = _conv_bwd(
        d_hmid, u2, conv_w_f, conv_b, "conv_bwd",
        comm=_Comm("exchange", [d_w_down_b.reshape(N_DEV, SHARD_DOWN, D_MODEL)]))
    dz1, d_ln1_g, d_ln1_b = _mm_ln_bwd(d_u2, w_up_t, dz2, xhat1, rstd1, ln1_g, "bwd_up_dx_ln1")
    d_w_up_t, d_w_up_t_b = _mm(d_u2_t, h1b, out_dtype2=BF16, tm=1408, name="bwd_up_dw")
    d_a = _mm(dz1, w_o_f[:ATTN_W], tb=True, name="bwd_o_dx_attn")
    d_r = _mm(dz1, w_o_f[ATTN_W:], tb=True, name="bwd_o_dx_hgrn")
    d_w_o_a, d_w_o_a_b = _mm(a_out_t, dz1, out_dtype2=BF16, name="bwd_o_dw_attn")
    d_w_o_r, d_w_o_r_b = _mm(r_out_t, dz1, out_dtype2=BF16, name="bwd_o_dw_hgrn")
    d_w_o = jnp.concatenate([d_w_o_a, d_w_o_r], axis=0)
    d_w_o_b = jnp.concatenate([d_w_o_a_b, d_w_o_r_b], axis=0)
    d_w_up_x = d_w_up_t_b.reshape(N_DEV, SHARD_UP, D_MODEL)
    half = SHARD_UP // 2
    (d_uh, d_uh_t, d_bias_h, d_norm_g, d_lb8), (recv_up_a,) = _hgrn_bwd(
        uh, o_pre, d_r, states, hgrn_lb, hgrn_norm_g, "hgrn_bwd",
        comm=_Comm("exchange", [d_w_up_x[:, :half]]))
    d_cw_x = d_conv_w8.reshape(8, N_DEV, SHARD_IN).transpose(1, 0, 2)
    (d_ua, d_ua_t, d_bias_a, d_sinks), (recv_up_b, recv_o, recv_cw) = _attn_bwd(
        ua, d_a, ctab, stab, sinks, "attn_bwd",
        comm=_Comm("exchange", [d_w_up_x[:, half:], d_w_o_b.reshape(N_DEV, SHARD_O, D_MODEL), d_cw_x]))
    d_w_a_t, d_w_a_t_b = _mm(d_ua_t, xb, out_dtype2=BF16, name="bwd_in_dw_attn")
    d_w_h_t, d_w_h_t_b = _mm(d_uh_t, xb, out_dtype2=BF16, name="bwd_in_dw_hgrn")
    d_w_in_t = jnp.concatenate([d_w_a_t, d_w_h_t], axis=0)
    d_w_in_t_b = jnp.concatenate([d_w_a_t_b, d_w_h_t_b], axis=0)
    dx, (recv_in,) = _mm(d_uh, w_h_t, addend=dz1, addend_scale=ALPHA, name="bwd_in_dx_hgrn",
                         comm=_Comm("exchange", [d_w_in_t_b.reshape(N_DEV, SHARD_IN, D_MODEL)]))
    dx = _mm(d_ua, w_a_t, addend=dx, tk=768, name="bwd_in_dx_attn")

    g_w_in = _sum_shards(recv_in, _own(d_w_in_t, SHARD_IN), "sum_w_in").T
    own_up = _own(d_w_up_t, SHARD_UP)
    g_w_up = jnp.concatenate([_sum_shards(recv_up_a, own_up[:half], "sum_w_up_a"),
                              _sum_shards(recv_up_b, own_up[half:], "sum_w_up_b")], axis=0).T
    res_in = (g_w_in,) + tuple(_adamw(w_in[0], g_w_in, m_w_in[0], v_w_in[0], "adamw_w_in"))
    res_up = (g_w_up,) + tuple(_adamw(w_up[0], g_w_up, m_w_up[0], v_w_up[0], "adamw_w_up"))
    res_o = _sum_shards_adamw(recv_o, _own(d_w_o, SHARD_O), w_o[0], m_w_o[0], v_w_o[0], "adamw_w_o")
    res_down = _sum_shards_adamw(recv_down, _own(d_w_down, SHARD_DOWN), w_down[0], m_w_down[0], v_w_down[0],
                                 "adamw_w_down")
    g_cw = _sum_slots(recv_cw, "sum_conv_w")
    cw8 = lambda a: _pad_rows(a, 8)
    res_cw = (g_cw,) + tuple(_adamw(cw8(conv_w[0]), g_cw, cw8(m_conv_w[0]), cw8(v_conv_w[0]), "adamw_conv_w"))
    big = {"w_in": [r[None] for r in res_in], "w_up": [r[None] for r in res_up],
           "w_o": [r[None] for r in res_o], "w_down": [r[None] for r in res_down],
           "conv_w": [r[None, 0:3] for r in res_cw]}

    small_local = _pack_small({
        "ln1_g": d_ln1_g, "ln1_b": d_ln1_b, "b_in": jnp.concatenate([d_bias_a, d_bias_h], axis=1),
        "sinks": d_sinks[:, :8], "hgrn_lb": d_lb8[0:2], "hgrn_norm_g": d_norm_g, "ln2_g": d_ln2_g,
        "ln2_b": d_ln2_b, "conv_b": d_conv_b, "loss": loss_part[:, :1]})
    small_sum = _sum_slots(_all_gather_vmem(small_local, "ar_small"), "ar_small_sum")
    gs = _unpack_small(small_sum)
    loss = gs["loss"][0]
    zero1 = jnp.zeros((1,), F32)
    w_small = _pack_small({"ln1_g": ln1_g, "ln1_b": ln1_b, "b_in": b_in, "sinks": sinks, "hgrn_lb": hgrn_lb,
                           "hgrn_norm_g": hgrn_norm_g, "ln2_g": ln2_g, "ln2_b": ln2_b, "conv_b": conv_b,
                           "loss": zero1})
    m_small = _pack_small({"ln1_g": m_ln1_g, "ln1_b": m_ln1_b, "b_in": m_b_in, "sinks": m_sinks,
                           "hgrn_lb": m_hgrn_lb, "hgrn_norm_g": m_hgrn_norm_g, "ln2_g": m_ln2_g,
                           "ln2_b": m_ln2_b, "conv_b": m_conv_b, "loss": zero1})
    v_small = _pack_small({"ln1_g": v_ln1_g, "ln1_b": v_ln1_b, "b_in": v_b_in, "sinks": v_sinks,
                           "hgrn_lb": v_hgrn_lb, "hgrn_norm_g": v_hgrn_norm_g, "ln2_g": v_ln2_g,
                           "ln2_b": v_ln2_b, "conv_b": v_conv_b, "loss": zero1})
    small = [gs] + [_unpack_small(p) for p in _adamw(w_small, small_sum, m_small, v_small, "adamw_small")]

    order = ["ln1_g", "ln1_b", "w_in", "b_in", "sinks", "hgrn_lb", "hgrn_norm_g", "w_o", "ln2_g", "ln2_b",
             "w_up", "conv_w", "conv_b", "w_down"]

    def pick(idx):
        return [big[n][idx] if n in big else small[idx][n] for n in order]

    return (loss, dx[None], *pick(0), *pick(1), *pick(2), *pick(3))
```

```python
import functools

import jax
import jax.numpy as jnp
import numpy as np
from jax import lax
from jax.experimental import pallas as pl
from jax.experimental.pallas import tpu as pltpu

F32 = jnp.float32
BF16 = jnp.bfloat16

N_DEV = 8
D_MODEL = 1024
D_FF = 2816
ATTN_W = 512
KV_W = 128
UA_W = ATTN_W + 2 * KV_W
UH_W = 2048
HG_W = 512
ATTN_BLOCK = 128
HGRN_CHUNK = 64
HGRN_SUB = 16
HGRN_CHUNKS_PER_STEP = 4
EXP_CLAMP = 85.0
NEG_BIG = -1e30
LN_EPS = 1e-5
RMS_EPS = 1e-6
ALPHA = 2.0 ** 0.25
ATTN_SCALE = 0.125
ROPE_THETA = 500000.0

ADAM_LR = 0.001
ADAM_B1 = 0.9
ADAM_B2 = 0.999
ADAM_EPS = 1e-08
ADAM_WD = 0.01
ADAM_STEP = 10

LANES = 128
VMEM_LIMIT_BYTES = 56 * 1024 * 1024

SHARD_IN = D_FF // N_DEV
SHARD_UP = 2 * D_FF // N_DEV
SHARD_O = D_MODEL // N_DEV
SHARD_DOWN = D_FF // N_DEV
SMALL_ROWS = 88

_MESH = pl.DeviceIdType.MESH
_NT = (((1,), (1,)), ((), ()))
_NN = (((1,), (0,)), ((), ()))
_TN = (((0,), (0,)), ((), ()))


def _cp(*sem):
    if sem:
        return pltpu.CompilerParams(dimension_semantics=sem, vmem_limit_bytes=VMEM_LIMIT_BYTES)
    return pltpu.CompilerParams(vmem_limit_bytes=VMEM_LIMIT_BYTES)


def _sig(x):
    return 0.5 * jnp.tanh(0.5 * x) + 0.5


def _dsilu(x, s):
    return s * (1.0 + x * (1.0 - s))


def _dot(a, b, dims):
    return lax.dot_general(a.astype(BF16), b.astype(BF16), dims, preferred_element_type=F32)


def _split(a):
    hi = a.astype(BF16)
    return hi, (a - hi.astype(F32)).astype(BF16)


def _dot3(a, b, dims):
    ah, al = _split(a)
    bh, bl = _split(b)
    d = functools.partial(lax.dot_general, dimension_numbers=dims, preferred_element_type=F32)
    return d(ah, bh) + (d(ah, bl) + d(al, bh))


def _pick(n, pref):
    for t in pref:
        if t <= n and n % t == 0:
            return t
    return n


def _my_coords():
    return lax.axis_index("x"), lax.axis_index("y"), lax.axis_index("c")


def _peer(k):
    x, y, c = _my_coords()
    return (1 - x if k & 4 else x, 1 - y if k & 2 else y, 1 - c if k & 1 else c)


def _me():
    x, y, c = _my_coords()
    return 4 * x + 2 * y + c


class _Comm:
    def __init__(self, items):
        self.items = []
        for it in items:
            kind, arr = it[0], it[1]
            full = arr.shape[0] if kind == "gather" else arr.shape[1]
            r0, rows = (it[2], it[3]) if len(it) == 4 else (0, full)
            self.items.append((kind, arr, r0, rows))
        self.n = len(self.items)
        self.arrays = [it[1] for it in self.items]

    def out_shapes(self):
        return [jax.ShapeDtypeStruct((N_DEV, rows, arr.shape[-1]), arr.dtype) for _, arr, _, rows in self.items]

    def specs(self):
        return [pl.BlockSpec(memory_space=pl.ANY)] * self.n

    def scratch(self):
        return [pltpu.SemaphoreType.DMA(((N_DEV - 1) * self.n,)), pltpu.SemaphoreType.DMA(((N_DEV - 1) * self.n,)),
                pltpu.SemaphoreType.DMA((self.n,))]

    def _src(self, a, ref, dev):
        kind, _, r0, rows = self.items[a]
        blk = ref if kind == "gather" else ref.at[dev]
        return blk.at[pl.ds(r0, rows)]

    def _copy(self, a, k, src, dst, sems, me, slot):
        other = jnp.bitwise_xor(me, k)
        idx = a * (N_DEV - 1) + k - 1
        return pltpu.make_async_remote_copy(
            src_ref=self._src(a, src, other), dst_ref=dst.at[me if slot == "mine" else other],
            send_sem=sems[0].at[idx], recv_sem=sems[1].at[idx], device_id=_peer(k), device_id_type=_MESH)

    def _pass_on(self, a, k, dst, sems, me):
        slot = dst.at[jnp.bitwise_xor(me, k)]
        idx = a * (N_DEV - 1) + k
        return pltpu.make_async_remote_copy(
            src_ref=slot, dst_ref=slot, send_sem=sems[0].at[idx], recv_sem=sems[1].at[idx],
            device_id=_peer(1), device_id_type=_MESH)

    def start(self, srcs, dsts, sems):
        me = _me()
        for a, (src, dst) in enumerate(zip(srcs, dsts)):
            direct = (1, 2, 4, 6) if self.items[a][0] == "gather" else range(1, N_DEV)
            pltpu.make_async_copy(self._src(a, src, me), dst.at[me], sems[2].at[a]).start()
            for k in direct:
                self._copy(a, k, src, dst, sems, me, "mine").start()

    def wait(self, srcs, dsts, sems):
        me = _me()
        for a, (src, dst) in enumerate(zip(srcs, dsts)):
            if self.items[a][0] == "gather":
                for k in (2, 4, 6):
                    self._copy(a, k, src, dst, sems, me, "theirs").wait_recv()
                    self._pass_on(a, k, dst, sems, me).start()
                for k in (1, 3, 5, 7):
                    self._copy(a, k, src, dst, sems, me, "theirs").wait_recv()
                for k in (1, 2, 4, 6):
                    self._copy(a, k, src, dst, sems, me, "mine").wait_send()
                for k in (2, 4, 6):
                    self._pass_on(a, k, dst, sems, me).wait_send()
            else:
                for k in range(1, N_DEV):
                    self._copy(a, k, src, dst, sems, me, "theirs").wait_recv()
                for k in range(1, N_DEV):
                    self._copy(a, k, src, dst, sems, me, "mine").wait_send()
            pltpu.make_async_copy(self._src(a, src, me), dst.at[me], sems[2].at[a]).wait()


def _call(body, *, name, grid, ins, in_specs, out_specs, out_shape, scratch_shapes=(), sem, comm=None):
    n_in, n_out, n_scr = len(ins), len(out_shape), len(scratch_shapes)
    if comm is None:
        outs = pl.pallas_call(
            body, name=name, grid=grid, in_specs=list(in_specs), out_specs=list(out_specs),
            out_shape=list(out_shape), scratch_shapes=list(scratch_shapes), compiler_params=_cp(*sem))(*ins)
        return list(outs), []
    nc = comm.n

    def hosted(*refs):
        pos = n_in
        c_in = refs[pos:pos + nc]
        pos += nc
        outs = refs[pos:pos + n_out]
        pos += n_out
        c_out = refs[pos:pos + nc]
        pos += nc
        scr = refs[pos:pos + n_scr]
        sems = refs[pos + n_scr:]
        ids = [pl.program_id(d) for d in range(len(grid))]
        first = functools.reduce(jnp.logical_and, [i == 0 for i in ids])
        last = functools.reduce(jnp.logical_and, [i == g - 1 for i, g in zip(ids, grid)])

        @pl.when(first)
        def _():
            comm.start(c_in, c_out, sems)

        body(*refs[:n_in], *outs, *scr)

        @pl.when(last)
        def _():
            comm.wait(c_in, c_out, sems)

    outs = pl.pallas_call(
        hosted, name=name, grid=grid, in_specs=list(in_specs) + comm.specs(),
        out_specs=list(out_specs) + comm.specs(), out_shape=list(out_shape) + comm.out_shapes(),
        scratch_shapes=list(scratch_shapes) + comm.scratch(),
        compiler_params=_cp(*(["arbitrary"] * len(grid))))(*ins, *comm.arrays)
    return list(outs[:n_out]), list(outs[n_out:])


def _comm_only(comm, name):
    def body(*refs):
        srcs, dsts, sems = refs[:comm.n], refs[comm.n:2 * comm.n], refs[2 * comm.n:]
        comm.start(srcs, dsts, sems)
        comm.wait(srcs, dsts, sems)

    return list(pl.pallas_call(
        body, name=name, in_specs=comm.specs(), out_specs=comm.specs(), out_shape=comm.out_shapes(),
        scratch_shapes=comm.scratch(), compiler_params=_cp())(*comm.arrays))


def _sum_slots(gathered, name):
    _, rows, cols = gathered.shape

    def body(g_ref, out_ref):
        acc = g_ref[0]
        for s in range(1, N_DEV):
            acc = acc + g_ref[s]
        out_ref[...] = acc

    return pl.pallas_call(
        body, name=name,
        out_shape=jax.ShapeDtypeStruct((rows, cols), F32),
        compiler_params=_cp(),
    )(gathered)


def _slot_sum(recv_ref, own_ref, shape):
    me = _me()
    acc = jnp.zeros(shape, F32)
    for s in range(N_DEV):
        acc = acc + jnp.where(me == s, own_ref[...], recv_ref[s].astype(F32))
    return acc


def _adamw_math(w, g, m, v):
    nm = ADAM_B1 * m + (1.0 - ADAM_B1) * g
    nv = ADAM_B2 * v + (1.0 - ADAM_B2) * (g * g)
    m_hat = nm / (1.0 - ADAM_B1 ** ADAM_STEP)
    v_hat = nv / (1.0 - ADAM_B2 ** ADAM_STEP)
    return -ADAM_LR * (m_hat / (jnp.sqrt(v_hat) + ADAM_EPS) + ADAM_WD * w), nm, nv


def _sum_shards_t(recvs, own, name):
    cols = own.shape[1]
    sizes = [r.shape[1] for r in recvs]
    total = sum(sizes)
    padded = -(-total // LANES) * LANES
    offs = [sum(sizes[:i]) for i in range(len(sizes))]

    def body(*refs):
        recv_refs, own_refs, out_ref = refs[:len(sizes)], refs[len(sizes):2 * len(sizes)], refs[-1]
        parts = [_slot_sum(r, o, (n, LANES)) for r, o, n in zip(recv_refs, own_refs, sizes)]
        if padded > total:
            parts.append(jnp.zeros((padded - total, LANES), F32))
        full = jnp.concatenate(parts, axis=0)
        for piece in range(padded // LANES):
            out_ref[:, piece * LANES:(piece + 1) * LANES] = full[piece * LANES:(piece + 1) * LANES, :].T

    return pl.pallas_call(
        body, name=name, grid=(cols // LANES,),
        in_specs=[pl.BlockSpec((N_DEV, n, LANES), lambda c: (0, 0, c)) for n in sizes]
                 + [pl.BlockSpec((n, LANES), functools.partial(lambda c, blk: (blk, c), blk=o // n))
                    for n, o in zip(sizes, offs)],
        out_specs=pl.BlockSpec((LANES, padded), lambda c: (c, 0)),
        out_shape=jax.ShapeDtypeStruct((cols, padded), F32),
        compiler_params=_cp("parallel"),
    )(*recvs, *([own] * len(sizes)))


def _adamw_padded_g(w, g_pad, m, v, name):
    rows, cols = w.shape
    tr = _pick(rows, (256, 128, 64, 32, 16, 8))

    def body(w_ref, g_ref, m_ref, v_ref, go_ref, d_ref, nm_ref, nv_ref):
        g = g_ref[:, 0:cols]
        go_ref[...] = g
        d_ref[...], nm_ref[...], nv_ref[...] = _adamw_math(w_ref[...], g, m_ref[...], v_ref[...])

    spec = pl.BlockSpec((tr, cols), lambda i: (i, 0))
    shp = jax.ShapeDtypeStruct((rows, cols), F32)
    return pl.pallas_call(
        body, name=name, grid=(rows // tr,),
        in_specs=[spec, pl.BlockSpec((tr, g_pad.shape[1]), lambda i: (i, 0)), spec, spec],
        out_specs=[spec, spec, spec, spec], out_shape=[shp, shp, shp, shp],
        compiler_params=_cp("parallel"),
    )(w, g_pad, m, v)


def _sum_shards_adamw(recv, own, w, m, v, name):
    _, rows, cols = recv.shape
    tr = _pick(rows, (128, 176, 64, 32, 16, 8))

    def body(recv_ref, own_ref, w_ref, m_ref, v_ref, g_ref, d_ref, nm_ref, nv_ref):
        g = _slot_sum(recv_ref, own_ref, (tr, cols))
        g_ref[...] = g
        d_ref[...], nm_ref[...], nv_ref[...] = _adamw_math(w_ref[...], g, m_ref[...], v_ref[...])

    spec = pl.BlockSpec((tr, cols), lambda i: (i, 0))
    shp = jax.ShapeDtypeStruct((rows, cols), F32)
    return pl.pallas_call(
        body, name=name, grid=(rows // tr,),
        in_specs=[pl.BlockSpec((N_DEV, tr, cols), lambda i: (0, i, 0)), spec, spec, spec, spec],
        out_specs=[spec, spec, spec, spec], out_shape=[shp, shp, shp, shp],
        compiler_params=_cp("parallel"),
    )(recv, own, w, m, v)


def _mm(a, b, *, name, ta=False, tb=False, out_dtype=F32, out_dtype2=None, bias=None, addend=None,
        addend_scale=1.0, tm=1024, tn=1024, tk=1024, comm=None):
    kdim, m = a.shape if ta else a.shape[::-1]
    n = b.shape[0] if tb else b.shape[1]
    tm = _pick(m, (tm, 1408, 1024, 768, 512, 256, 128))
    tn = _pick(n, (tn, 1408, 1024, 768, 512, 256, 128))
    tk = _pick(kdim, (tk, 1408, 1024, 768, 512, 256, 128))
    nk = kdim // tk
    a_spec = pl.BlockSpec((tk, tm), lambda i, j, k: (k, i)) if ta else pl.BlockSpec((tm, tk), lambda i, j, k: (i, k))
    b_spec = pl.BlockSpec((tn, tk), lambda i, j, k: (j, k)) if tb else pl.BlockSpec((tk, tn), lambda i, j, k: (k, j))
    ins, specs = [a, b], [a_spec, b_spec]
    if bias is not None:
        ins.append(bias)
        specs.append(pl.BlockSpec((1, tn), lambda i, j, k: (0, j)))
    if addend is not None:
        ins.append(addend)
        specs.append(pl.BlockSpec((tm, tn), lambda i, j, k: (i, j)))
    dims = (((0,) if ta else (1,), (1,) if tb else (0,)), ((), ()))
    has_bias, has_addend, two = bias is not None, addend is not None, out_dtype2 is not None

    def body(*refs):
        a_ref, b_ref = refs[0], refs[1]
        pos = 2
        bias_ref = addend_ref = None
        if has_bias:
            bias_ref = refs[pos]
            pos += 1
        if has_addend:
            addend_ref = refs[pos]
            pos += 1
        o_refs, acc_ref = refs[pos:-1], refs[-1]
        k = pl.program_id(2)

        @pl.when(k == 0)
        def _():
            acc_ref[...] = jnp.zeros_like(acc_ref)

        acc_ref[...] += _dot(a_ref[...], b_ref[...], dims)

        @pl.when(k == nk - 1)
        def _():
            r = acc_ref[...]
            if has_bias:
                r = r + bias_ref[...]
            if has_addend:
                r = r + addend_scale * addend_ref[...].astype(F32)
            for o_ref in o_refs:
                o_ref[...] = r.astype(o_ref.dtype)

    ospec = pl.BlockSpec((tm, tn), lambda i, j, k: (i, j))
    dtypes = [out_dtype] + ([out_dtype2] if two else [])
    outs, couts = _call(
        body, name=name, grid=(m // tm, n // tn, nk), ins=ins, in_specs=specs,
        out_specs=[ospec] * len(dtypes), out_shape=[jax.ShapeDtypeStruct((m, n), d) for d in dtypes],
        scratch_shapes=[pltpu.VMEM((tm, tn), F32)], sem=("parallel", "parallel", "arbitrary"), comm=comm)
    primary = tuple(outs) if two else outs[0]
    return (primary, couts) if comm is not None else primary


def _rope_lane_constants():
    inv_freq = np.float32(ROPE_THETA) ** (-np.arange(8, dtype=np.float32) * np.float32(2.0 / 16.0))
    lane = np.arange(LANES) % 64
    freq = np.where(lane < 16, inv_freq[lane % 8], 0.0).astype(np.float32)
    sign = np.where(lane < 8, -1.0, np.where(lane < 16, 1.0, 0.0)).astype(np.float32)
    return jnp.asarray(freq)[None, :], jnp.asarray(sign)[None, :]


def _rope_tables(pos_col, name):
    t = pos_col.shape[0]
    tr = _pick(t, (512, 256, 128))
    freq, sign = _rope_lane_constants()

    def body(pos_ref, freq_ref, sign_ref, c_ref, s_ref):
        ang = pos_ref[...].astype(F32) * freq_ref[...]
        c_ref[...] = jnp.cos(ang)
        s_ref[...] = sign_ref[...] * jnp.sin(ang)

    return pl.pallas_call(
        body, name=name, grid=(t // tr,),
        in_specs=[pl.BlockSpec((tr, 1), lambda i: (i, 0)),
                  pl.BlockSpec((1, LANES), lambda i: (0, 0)),
                  pl.BlockSpec((1, LANES), lambda i: (0, 0))],
        out_specs=[pl.BlockSpec((tr, LANES), lambda i: (i, 0)), pl.BlockSpec((tr, LANES), lambda i: (i, 0))],
        out_shape=[jax.ShapeDtypeStruct((t, LANES), F32), jax.ShapeDtypeStruct((t, LANES), F32)],
        compiler_params=_cp("parallel"),
    )(pos_col, freq, sign)


def _swap8(t):
    width = t.shape[1]
    lane = jnp.bitwise_and(lax.broadcasted_iota(jnp.int32, t.shape, 1), 63)
    return jnp.where(lane < 8, pltpu.roll(t, width - 8, 1), jnp.where(lane < 16, pltpu.roll(t, 8, 1), 0.0))


def _rope(t, c, s):
    return t * c + _swap8(t) * s


def _rope_bwd(d, c, s):
    return d * c + _swap8(d * s)


def _tile4(a):
    return jnp.concatenate([a, a, a, a], axis=1)


def _attn_band(n, k_cur, k_prev, v_cur, v_prev, c_cur, s_cur, c_prev, s_prev):
    kband = jnp.concatenate([_rope(k_prev, c_prev, s_prev), _rope(k_cur, c_cur, s_cur)], axis=0)
    vband = jnp.concatenate([v_prev, v_cur], axis=0)
    qi = lax.broadcasted_iota(jnp.int32, (ATTN_BLOCK, 2 * ATTN_BLOCK), 0)
    kj = lax.broadcasted_iota(jnp.int32, (ATTN_BLOCK, 2 * ATTN_BLOCK), 1)
    dist = qi + ATTN_BLOCK - kj
    valid = (dist >= 0) & (dist < ATTN_BLOCK) & (n * ATTN_BLOCK - ATTN_BLOCK + kj >= 0)
    return (kband.astype(BF16), pltpu.roll(kband, 64, 1).astype(BF16),
            vband.astype(BF16), pltpu.roll(vband, 64, 1).astype(BF16), valid)


def _attn_probs(raw, valid, sink, axis):
    s = jnp.where(valid, raw * ATTN_SCALE, NEG_BIG)
    m = jnp.maximum(jnp.max(s, axis=axis, keepdims=True), sink)
    p = jnp.exp(s - m)
    esink = jnp.exp(sink - m)
    z = jnp.sum(p, axis=axis, keepdims=True) + esink
    return p / z, esink / z


def _attn_valid_t(n):
    kj = lax.broadcasted_iota(jnp.int32, (2 * ATTN_BLOCK, ATTN_BLOCK), 0)
    qi = lax.broadcasted_iota(jnp.int32, (2 * ATTN_BLOCK, ATTN_BLOCK), 1)
    dist = qi + ATTN_BLOCK - kj
    return (dist >= 0) & (dist < ATTN_BLOCK) & (n * ATTN_BLOCK - ATTN_BLOCK + kj >= 0)


def _attn_specs(nb):
    def cur(col, width=KV_W):
        return pl.BlockSpec((ATTN_BLOCK, width), lambda n: (jnp.minimum(n, nb - 1), col))

    def prev(col):
        return pl.BlockSpec((ATTN_BLOCK, KV_W), lambda n: (jnp.maximum(n - 1, 0), col))

    ua_specs = [cur(0, ATTN_W), cur(4), prev(4), cur(5), prev(5)]
    tab_specs = [cur(0), cur(0), prev(0), prev(0)]
    return ua_specs, tab_specs


def _attn_fwd(ua, ctab, stab, sinks, name, comm=None):
    t = ua.shape[0]
    nb = t // ATTN_BLOCK
    ua_specs, tab_specs = _attn_specs(nb)

    def body(q_ref, kc_ref, kp_ref, vc_ref, vp_ref, cc_ref, sc_ref, cp_ref, sp_ref, sink_ref, o_ref, o_t_ref):
        n = pl.program_id(0)
        cc, sc = cc_ref[...], sc_ref[...]
        kb, kb_r, vb, vb_r, valid = _attn_band(n, kc_ref[...], kp_ref[...], vc_ref[...], vp_ref[...],
                                               cc, sc, cp_ref[...], sp_ref[...])
        qr = _rope(q_ref[...], _tile4(cc), _tile4(sc))
        lo = lax.broadcasted_iota(jnp.int32, (ATTN_BLOCK, LANES), 1) < 64
        heads = []
        for j in range(4):
            qj = qr[:, j * LANES:(j + 1) * LANES]
            for is_lo in (True, False):
                aligned = is_lo == (j < 2)
                qm = jnp.where(lo if is_lo else jnp.logical_not(lo), qj, 0.0).astype(BF16)
                raw = lax.dot_general(qm, kb if aligned else kb_r, _NT, preferred_element_type=F32)
                heads.append((raw, vb if aligned else vb_r, sink_ref[0, len(heads)]))
        halves = []
        for raw, vv, sink in heads:
            probs, _ = _attn_probs(raw, valid, sink, 1)
            halves.append(lax.dot_general(probs.astype(BF16), vv, _NN, preferred_element_type=F32))
        outs = [jnp.where(lo, halves[2 * j], halves[2 * j + 1]) for j in range(4)]
        o_ref[...] = jnp.concatenate(outs, axis=1).astype(o_ref.dtype)
        for j in range(4):
            o_t_ref[j * LANES:(j + 1) * LANES, :] = outs[j].T.astype(o_t_ref.dtype)

    return _call(
        body, name=name, grid=(nb,), ins=[ua, ua, ua, ua, ua, ctab, stab, ctab, stab, sinks],
        in_specs=ua_specs + tab_specs + [pl.BlockSpec(memory_space=pltpu.SMEM)],
        out_specs=[pl.BlockSpec((ATTN_BLOCK, ATTN_W), lambda n: (n, 0)),
                   pl.BlockSpec((ATTN_W, ATTN_BLOCK), lambda n: (0, n))],
        out_shape=[jax.ShapeDtypeStruct((t, ATTN_W), BF16), jax.ShapeDtypeStruct((ATTN_W, t), BF16)],
        sem=("parallel",), comm=comm)


def _attn_bwd(ua, d_out, ctab, stab, sinks, name, comm=None):
    t = ua.shape[0]
    nb = t // ATTN_BLOCK
    ua_specs, tab_specs = _attn_specs(nb)

    def body(q_ref, kc_ref, kp_ref, vc_ref, vp_ref, cc_ref, sc_ref, cp_ref, sp_ref, do_ref, sink_ref,
             dua_ref, dua_t_ref, dbias_ref, dsink_ref, dq_c, dk_c, dv_c, dq_n, dk_n, dv_n):
        n = pl.program_id(0)

        @pl.when(n == 0)
        def _():
            dq_c[...] = jnp.zeros_like(dq_c)
            dk_c[...] = jnp.zeros_like(dk_c)
            dv_c[...] = jnp.zeros_like(dv_c)
            dbias_ref[...] = jnp.zeros_like(dbias_ref)
            dsink_ref[...] = jnp.zeros_like(dsink_ref)

        @pl.when(n == nb)
        def _():
            dq_n[...] = jnp.zeros_like(dq_n)
            dk_n[...] = jnp.zeros_like(dk_n)
            dv_n[...] = jnp.zeros_like(dv_n)

        @pl.when(n < nb)
        def _():
            cc, sc = cc_ref[...], sc_ref[...]
            kb, kb_r, vb, vb_r, valid = _attn_band(n, kc_ref[...], kp_ref[...], vc_ref[...], vp_ref[...],
                                                   cc, sc, cp_ref[...], sp_ref[...])
            valid_t = _attn_valid_t(n)
            c4, s4 = _tile4(cc), _tile4(sc)
            qr = _rope(q_ref[...], c4, s4)
            do = do_ref[...].astype(F32)
            lane = lax.broadcasted_iota(jnp.int32, (ATTN_BLOCK, LANES), 1)
            lo = lane < 64
            lane_row = lax.broadcasted_iota(jnp.int32, (1, LANES), 1)
            heads = []
            for j in range(4):
                qj = qr[:, j * LANES:(j + 1) * LANES]
                doj = do[:, j * LANES:(j + 1) * LANES]
                for is_lo in (True, False):
                    aligned = is_lo == (j < 2)
                    msk = lo if is_lo else jnp.logical_not(lo)
                    kk = kb if aligned else kb_r
                    vv = vb if aligned else vb_r
                    qm = jnp.where(msk, qj, 0.0).astype(BF16)
                    dom = jnp.where(msk, doj, 0.0).astype(BF16)
                    heads.append(dict(
                        aligned=aligned, kk=kk, qm=qm, dom=dom, sink=sink_ref[0, len(heads)],
                        raw=lax.dot_general(qm, kk, _NT, preferred_element_type=F32),
                        dp=lax.dot_general(dom, vv, _NT, preferred_element_type=F32),
                        raw_t=lax.dot_general(kk, qm, _NT, preferred_element_type=F32),
                        dp_t=lax.dot_general(vv, dom, _NT, preferred_element_type=F32)))
            dk_band = jnp.zeros((2 * ATTN_BLOCK, LANES), F32)
            dv_band = jnp.zeros((2 * ATTN_BLOCK, LANES), F32)
            dsink = jnp.zeros((1, LANES), F32)
            halves = []
            for head, hd in enumerate(heads):
                probs, psink = _attn_probs(hd["raw"], valid, hd["sink"], 1)
                delta = jnp.sum(probs * hd["dp"], axis=1, keepdims=True)
                ds = (probs * (hd["dp"] - delta) * ATTN_SCALE).astype(BF16)
                dsink = dsink + jnp.where(lane_row == head, -jnp.sum(psink * delta), 0.0)
                halves.append(lax.dot_general(ds, hd["kk"], _NN, preferred_element_type=F32))
                probs_t, _ = _attn_probs(hd["raw_t"], valid_t, hd["sink"], 0)
                delta_t = jnp.sum(probs_t * hd["dp_t"], axis=0, keepdims=True)
                ds_t = (probs_t * (hd["dp_t"] - delta_t) * ATTN_SCALE).astype(BF16)
                dk_h = lax.dot_general(ds_t, hd["qm"], _NN, preferred_element_type=F32)
                dv_h = lax.dot_general(probs_t.astype(BF16), hd["dom"], _NN, preferred_element_type=F32)
                if not hd["aligned"]:
                    dk_h = pltpu.roll(dk_h, 64, 1)
                    dv_h = pltpu.roll(dv_h, 64, 1)
                dk_band = dk_band + dk_h
                dv_band = dv_band + dv_h
            dqs = [jnp.where(lo, halves[2 * j], halves[2 * j + 1]) for j in range(4)]
            dq_n[...] = _rope_bwd(jnp.concatenate(dqs, axis=1), c4, s4)
            dk_n[...] = dk_band
            dv_n[...] = dv_band
            dsink_ref[...] += dsink

        dk_prev = _rope_bwd(dk_c[...] + dk_n[0:ATTN_BLOCK, :], cp_ref[...], sp_ref[...])
        dv_prev = dv_c[...] + dv_n[0:ATTN_BLOCK, :]
        full = jnp.concatenate([dq_c[...], dk_prev, dv_prev], axis=1)
        dua_ref[...] = full.astype(dua_ref.dtype)
        for j in range(UA_W // LANES):
            dua_t_ref[j * LANES:(j + 1) * LANES, :] = full[:, j * LANES:(j + 1) * LANES].T.astype(dua_t_ref.dtype)
        dbias_ref[...] += jnp.sum(full, axis=0, keepdims=True)
        dq_c[...] = dq_n[...]
        dk_c[...] = dk_n[ATTN_BLOCK:, :]
        dv_c[...] = dv_n[ATTN_BLOCK:, :]

    return _call(
        body, name=name, grid=(nb + 1,), ins=[ua, ua, ua, ua, ua, ctab, stab, ctab, stab, d_out, sinks],
        in_specs=ua_specs + tab_specs + [
            pl.BlockSpec((ATTN_BLOCK, ATTN_W), lambda n: (jnp.minimum(n, nb - 1), 0)),
            pl.BlockSpec(memory_space=pltpu.SMEM)],
        out_specs=[pl.BlockSpec((ATTN_BLOCK, UA_W), lambda n: (jnp.maximum(n - 1, 0), 0)),
                   pl.BlockSpec((UA_W, ATTN_BLOCK), lambda n: (0, jnp.maximum(n - 1, 0))),
                   pl.BlockSpec((1, UA_W), lambda n: (0, 0)),
                   pl.BlockSpec((1, LANES), lambda n: (0, 0))],
        out_shape=[jax.ShapeDtypeStruct((t, UA_W), BF16), jax.ShapeDtypeStruct((UA_W, t), BF16),
                   jax.ShapeDtypeStruct((1, UA_W), F32),
                   jax.ShapeDtypeStruct((1, LANES), F32)],
        scratch_shapes=[pltpu.VMEM((ATTN_BLOCK, ATTN_W), F32), pltpu.VMEM((ATTN_BLOCK, KV_W), F32),
                        pltpu.VMEM((ATTN_BLOCK, KV_W), F32), pltpu.VMEM((ATTN_BLOCK, ATTN_W), F32),
                        pltpu.VMEM((2 * ATTN_BLOCK, KV_W), F32), pltpu.VMEM((2 * ATTN_BLOCK, KV_W), F32)],
        sem=("arbitrary",), comm=comm)


def _tri_mats():
    r = lax.broadcasted_iota(jnp.int32, (HGRN_CHUNK, LANES), 0)
    c = lax.broadcasted_iota(jnp.int32, (HGRN_CHUNK, LANES), 1)
    lower = ((c <= r) & (c < HGRN_CHUNK)).astype(F32)
    upper = ((c >= r) & (c < HGRN_CHUNK)).astype(F32)
    return lower, upper


def _tri_apply(tri, g):
    pad = jnp.concatenate([g, jnp.zeros_like(g)], axis=0)
    return lax.dot_general(tri, pad, _NN, precision=lax.Precision.HIGHEST, preferred_element_type=F32)


def _sub_masks():
    s = lax.broadcasted_iota(jnp.int32, (HGRN_CHUNK, LANES), 0)
    tt = lax.broadcasted_iota(jnp.int32, (HGRN_CHUNK, LANES), 1)
    return [(tt >= HGRN_SUB * i) & (tt < HGRN_SUB * (i + 1)) & (s <= tt) for i in range(HGRN_CHUNK // HGRN_SUB)]


def _hgrn_gates(hq, hf, lb_ref, b_scr):
    lb = _sig(lb_ref[0:1, :] - lb_ref[1:2, :])
    q = hq * _sig(hq)
    sg = _sig(hf)
    f = lb + (1.0 - lb) * sg
    k = 1.0 - f
    lower, _ = _tri_mats()
    b = _tri_apply(lower, jnp.log(f))
    b_scr[...] = b
    nsub = HGRN_CHUNK // HGRN_SUB
    starts = [jnp.zeros((1, HG_W), F32)] + [b_scr[HGRN_SUB * i - 1:HGRN_SUB * i, :] for i in range(1, nsub)]
    pq = jnp.concatenate([jnp.broadcast_to(p, (HGRN_SUB, HG_W)) for p in starts], axis=0)
    b_last = b_scr[HGRN_CHUNK - 1:HGRN_CHUNK, :]
    e_q = jnp.exp(b - pq)
    e_k = [jnp.exp(jnp.minimum(p - b, EXP_CLAMP)) for p in starts]
    e_b = jnp.exp(b)
    e_bl = jnp.exp(b_last - b)
    e_last = jnp.exp(b_last)
    return q, sg, f, k, lb, e_q, e_k, e_b, e_bl, e_last


def _sub_masks_ts():
    tt = lax.broadcasted_iota(jnp.int32, (HGRN_CHUNK, LANES), 0)
    s = lax.broadcasted_iota(jnp.int32, (HGRN_CHUNK, LANES), 1)
    return [(tt >= HGRN_SUB * i) & (tt < HGRN_SUB * (i + 1)) & (s <= tt) for i in range(HGRN_CHUNK // HGRN_SUB)]


def _masked_sum(blocks, masks, axis):
    step = HGRN_CHUNK if axis == 0 else LANES
    acc = jnp.zeros((HGRN_CHUNK, LANES), F32)
    for i, msk in enumerate(masks):
        blk = blocks[step * i:step * (i + 1), :] if axis == 0 else blocks[:, step * i:step * (i + 1)]
        acc = acc + jnp.where(msk, blk, 0.0)
    return acc


def _store_transposed(out_t_ref, chunk_rows):
    width = chunk_rows[0].shape[1]
    if len(chunk_rows) == 1:
        groups = [jnp.concatenate([chunk_rows[0], jnp.zeros_like(chunk_rows[0])], axis=0)]
    else:
        groups = [jnp.concatenate(chunk_rows[g:g + 2], axis=0) for g in range(0, len(chunk_rows), 2)]
    for g, rows in enumerate(groups):
        for c in range(width // LANES):
            tile = rows[:, c * LANES:(c + 1) * LANES].T.astype(out_t_ref.dtype)
            if len(chunk_rows) == 1:
                out_t_ref[c * LANES:(c + 1) * LANES, :] = tile[:, 0:HGRN_CHUNK]
            else:
                out_t_ref[c * LANES:(c + 1) * LANES, g * LANES:(g + 1) * LANES] = tile


def _hgrn_chunk_inputs(j, hq_ref, hf_ref, hi_ref, hg_ref, lb_ref, b_scr):
    rows = slice(j * HGRN_CHUNK, (j + 1) * HGRN_CHUNK)
    hq, hf, v, hg = hq_ref[rows, :], hf_ref[rows, :], hi_ref[rows, :], hg_ref[rows, :]
    q, sg, f, k, lb, e_q, e_k, e_b, e_bl, e_last = _hgrn_gates(hq, hf, lb_ref, b_scr.at[j])
    return dict(rows=rows, hq=hq, v=v, hg=hg, q=q, sg=sg, f=f, k=k, lb=lb, e_q=e_q, e_k=e_k, e_b=e_b, e_bl=e_bl,
                e_last=e_last, qt=q * e_q, qb=q * e_b, kd=k * e_bl, khat=[k * e for e in e_k])


def _hgrn_fwd(uh, lb_raw, norm_g, name, comm=None):
    t = uh.shape[0]
    nc = t // HGRN_CHUNK
    cps = _pick(nc, (HGRN_CHUNKS_PER_STEP, 2, 1))
    rows_step = cps * HGRN_CHUNK

    def body(hq_ref, hf_ref, hi_ref, hg_ref, lb_ref, ng_ref, r_ref, r_t_ref, o_ref, st_out_ref, st_ref, b_scr):
        @pl.when(pl.program_id(0) == 0)
        def _():
            st_ref[...] = jnp.zeros_like(st_ref)

        masks = _sub_masks_ts()
        ng = ng_ref[...]
        zpad = jnp.zeros((HGRN_CHUNK, LANES), F32)
        heads = [slice(h * LANES, (h + 1) * LANES) for h in range(4)]
        chunks = [_hgrn_chunk_inputs(j, hq_ref, hf_ref, hi_ref, hg_ref, lb_ref, b_scr) for j in range(cps)]
        for ch in chunks:
            ch["scores"] = [_dot3(ch["qt"][:, sl],
                                  jnp.concatenate([x for kh in ch["khat"] for x in (kh[:, sl], zpad)], axis=0), _NT)
                            for sl in heads]
        for j, ch in enumerate(chunks):
            o_heads, y_heads = [], []
            for h, sl in enumerate(heads):
                a_ts = _masked_sum(ch["scores"][h], masks, 1)
                vh = ch["v"][:, sl].astype(BF16)
                v_pad = jnp.concatenate([vh, jnp.zeros_like(vh)], axis=0)
                o_intra = lax.dot_general(a_ts.astype(BF16), v_pad, _NN, preferred_element_type=F32)
                st = st_ref[h]
                st_out_ref[j, h] = st
                o_inter = _dot(ch["qb"][:, sl], st, _NT)
                st_ref[h] = st * ch["e_last"][:, sl] + _dot(vh, ch["kd"][:, sl], _TN)
                oh = o_intra + o_inter
                rs = lax.rsqrt(jnp.mean(oh * oh, axis=1, keepdims=True) + RMS_EPS)
                o_heads.append(oh)
                y_heads.append(oh * rs * ng)
            hg = ch["hg"]
            o_ref[ch["rows"], :] = jnp.concatenate(o_heads, axis=1)
            ch["r"] = jnp.concatenate(y_heads, axis=1) * (hg * _sig(hg))
            r_ref[ch["rows"], :] = ch["r"].astype(r_ref.dtype)
        _store_transposed(r_t_ref, [ch["r"] for ch in chunks])

    col = lambda j: pl.BlockSpec((rows_step, HG_W), lambda c: (c, j))
    return _call(
        body, name=name, grid=(nc // cps,), ins=[uh, uh, uh, uh, lb_raw, norm_g],
        in_specs=[col(0), col(1), col(2), col(3),
                  pl.BlockSpec((2, HG_W), lambda c: (0, 0)), pl.BlockSpec((1, LANES), lambda c: (0, 0))],
        out_specs=[pl.BlockSpec((rows_step, HG_W), lambda c: (c, 0)),
                   pl.BlockSpec((HG_W, rows_step), lambda c: (0, c)),
                   pl.BlockSpec((rows_step, HG_W), lambda c: (c, 0)),
                   pl.BlockSpec((cps, 4, LANES, LANES), lambda c: (c, 0, 0, 0))],
        out_shape=[jax.ShapeDtypeStruct((t, HG_W), BF16), jax.ShapeDtypeStruct((HG_W, t), BF16),
                   jax.ShapeDtypeStruct((t, HG_W), F32), jax.ShapeDtypeStruct((nc, 4, LANES, LANES), F32)],
        scratch_shapes=[pltpu.VMEM((4, LANES, LANES), F32), pltpu.VMEM((cps, HGRN_CHUNK, HG_W), F32)],
        sem=("arbitrary",), comm=comm)


def _hgrn_bwd(uh, o_pre, d_r, states, lb_raw, norm_g, name, comm=None):
    t = uh.shape[0]
    nc = t // HGRN_CHUNK
    cps = _pick(nc, (HGRN_CHUNKS_PER_STEP, 2, 1))
    ns = nc // cps
    rows_step = cps * HGRN_CHUNK
    nsub = HGRN_CHUNK // HGRN_SUB

    def body(hq_ref, hf_ref, hi_ref, hg_ref, o_ref, dr_ref, st_in_ref, lb_ref, ng_ref,
             duh_ref, duh_t_ref, dbias_ref, dng_ref, dlb_ref, dst_ref, b_scr, dlb_acc):
        i = pl.program_id(0)

        @pl.when(i == 0)
        def _():
            dst_ref[...] = jnp.zeros_like(dst_ref)
            dbias_ref[...] = jnp.zeros_like(dbias_ref)
            dng_ref[...] = jnp.zeros_like(dng_ref)
            dlb_acc[...] = jnp.zeros_like(dlb_acc)

        masks_st = _sub_masks()
        masks_ts = _sub_masks_ts()
        ng = ng_ref[...]
        zpad = jnp.zeros((HGRN_CHUNK, LANES), F32)
        _, upper = _tri_mats()
        heads = [slice(h * LANES, (h + 1) * LANES) for h in range(4)]
        row = lax.broadcasted_iota(jnp.int32, (HGRN_CHUNK, HG_W), 0)

        chunks = [_hgrn_chunk_inputs(j, hq_ref, hf_ref, hi_ref, hg_ref, lb_ref, b_scr) for j in range(cps)]
        dng = jnp.zeros((1, LANES), F32)
        for ch in chunks:
            o = o_ref[ch["rows"], :]
            dr = dr_ref[ch["rows"], :].astype(F32)
            hg = ch["hg"]
            sgg = _sig(hg)
            dy = dr * (hg * sgg)
            do_h, y_h = [], []
            for sl in heads:
                oh = o[:, sl]
                rs = lax.rsqrt(jnp.mean(oh * oh, axis=1, keepdims=True) + RMS_EPS)
                y_h.append(oh * rs * ng)
                dng = dng + jnp.sum(dy[:, sl] * oh * rs, axis=0, keepdims=True)
                w = dy[:, sl] * ng
                do_h.append(rs * (w - oh * (rs * rs) * jnp.mean(w * oh, axis=1, keepdims=True)))
            ch["do"] = do_h
            ch["dhg"] = dr * jnp.concatenate(y_h, axis=1) * _dsilu(hg, sgg)

        for ch in chunks:
            ch["kst"], ch["kpad"], ch["qt_pad"], ch["v_b"], ch["do_pad"] = [], [], [], [], []
            ch["ats"], ch["d_at"], ch["d_a"] = [], [], []
            for h, sl in enumerate(heads):
                kst = jnp.concatenate([kh[:, sl] for kh in ch["khat"]], axis=0)
                kpad = jnp.concatenate([x for kh in ch["khat"] for x in (kh[:, sl], zpad)], axis=0)
                qt_pad = jnp.concatenate([ch["qt"][:, sl], zpad], axis=0)
                vh = ch["v"][:, sl].astype(BF16)
                v_pad = jnp.concatenate([vh, jnp.zeros_like(vh)], axis=0)
                do_b = ch["do"][h].astype(BF16)
                do_pad = jnp.concatenate([do_b, jnp.zeros_like(do_b)], axis=0)
                ch["kst"].append(kst)
                ch["kpad"].append(kpad)
                ch["qt_pad"].append(qt_pad)
                ch["v_b"].append(vh)
                ch["do_pad"].append(do_pad)
                ch["ats"].append(_dot3(kst, qt_pad, _NT))
                ch["d_at"].append(lax.dot_general(vh, do_pad, _NT, preferred_element_type=F32))
                ch["d_a"].append(lax.dot_general(do_b, v_pad, _NT, preferred_element_type=F32))

        for ch in chunks:
            ch["d_kst"], ch["d_qt"], ch["dv"] = [], [], []
            for h in range(4):
                at = _masked_sum(ch["ats"][h], masks_st, 0)
                d_ats = jnp.concatenate([jnp.where(m, ch["d_at"][h], 0.0) for m in masks_st], axis=0)
                d_a_cat = jnp.concatenate([jnp.where(m, ch["d_a"][h], 0.0) for m in masks_ts], axis=1)
                ch["d_kst"].append(_dot3(d_ats, ch["qt_pad"][h], _NN))
                ch["d_qt"].append(_dot3(d_a_cat, ch["kpad"][h], _NN))
                ch["dv"].append(lax.dot_general(at.astype(BF16), ch["do_pad"][h], _NN, preferred_element_type=F32))

        for j in reversed(range(cps)):
            ch = chunks[j]
            q, k, sg, f, lb = ch["q"], ch["k"], ch["sg"], ch["f"], ch["lb"]
            dq_h, dk_h, dv_h, extra_h = [], [], [], []
            for h, sl in enumerate(heads):
                st_prev = st_in_ref[j, h]
                d_st = dst_ref[h]
                d_st_b = d_st.astype(BF16)
                do_b = ch["do_pad"][h][0:HGRN_CHUNK, :]
                kd, e_last = ch["kd"][:, sl], ch["e_last"][:, sl]
                dv = ch["dv"][h] + _dot(kd, d_st_b, _NT)
                d_qb = _dot(do_b, st_prev, _NN)
                d_kd = lax.dot_general(ch["v_b"][h], d_st_b, _NN, preferred_element_type=F32)
                extra_h.append(jnp.sum(st_prev * d_st, axis=0, keepdims=True) * e_last
                               + jnp.sum(kd * d_kd, axis=0, keepdims=True))
                dst_ref[h] = d_st * e_last + _dot(do_b, ch["qb"][:, sl], _TN)
                dq_h.append(ch["d_qt"][h] * ch["e_q"][:, sl] + d_qb * ch["e_b"][:, sl])
                dkk = d_kd * ch["e_bl"][:, sl]
                for s_ in range(nsub):
                    dkk = dkk + ch["d_kst"][h][HGRN_CHUNK * s_:HGRN_CHUNK * (s_ + 1), :] * ch["e_k"][s_][:, sl]
                dk_h.append(dkk)
                dv_h.append(dv)
            dq = jnp.concatenate(dq_h, axis=1)
            dk = jnp.concatenate(dk_h, axis=1)
            dv = jnp.concatenate(dv_h, axis=1)
            extra = jnp.concatenate(extra_h, axis=1)
            db = q * dq - k * dk + jnp.where(row == HGRN_CHUNK - 1, extra, 0.0)
            dg = _tri_apply(upper, db)
            df = dg / f - dk
            dhf = df * (1.0 - lb) * sg * (1.0 - sg)
            dhq = dq * _dsilu(ch["hq"], _sig(ch["hq"]))
            full = jnp.concatenate([dhq, dhf, dv, ch["dhg"]], axis=1)
            duh_ref[ch["rows"], :] = full.astype(duh_ref.dtype)
            ch["full"] = full
            dbias_ref[...] += jnp.sum(full, axis=0, keepdims=True)
            dlb_acc[...] += jnp.sum(df * (1.0 - sg), axis=0, keepdims=True)
        dng_ref[...] += dng
        _store_transposed(duh_t_ref, [ch["full"] for ch in chunks])

        @pl.when(i == ns - 1)
        def _():
            lb = chunks[0]["lb"]
            d_a0 = dlb_acc[...] * lb * (1.0 - lb)
            r8 = lax.broadcasted_iota(jnp.int32, (8, HG_W), 0)
            dlb_ref[...] = jnp.where(r8 == 0, d_a0, jnp.where(r8 == 1, -d_a0, 0.0))

    col = lambda j: pl.BlockSpec((rows_step, HG_W), lambda i: (ns - 1 - i, j))
    return _call(
        body, name=name, grid=(ns,), ins=[uh, uh, uh, uh, o_pre, d_r, states, lb_raw, norm_g],
        in_specs=[col(0), col(1), col(2), col(3), col(0), col(0),
                  pl.BlockSpec((cps, 4, LANES, LANES), lambda i: (ns - 1 - i, 0, 0, 0)),
                  pl.BlockSpec((2, HG_W), lambda i: (0, 0)), pl.BlockSpec((1, LANES), lambda i: (0, 0))],
        out_specs=[pl.BlockSpec((rows_step, UH_W), lambda i: (ns - 1 - i, 0)),
                   pl.BlockSpec((UH_W, rows_step), lambda i: (0, ns - 1 - i)),
                   pl.BlockSpec((1, UH_W), lambda i: (0, 0)),
                   pl.BlockSpec((1, LANES), lambda i: (0, 0)),
                   pl.BlockSpec((8, HG_W), lambda i: (0, 0))],
        out_shape=[jax.ShapeDtypeStruct((t, UH_W), BF16), jax.ShapeDtypeStruct((UH_W, t), BF16),
                   jax.ShapeDtypeStruct((1, UH_W), F32),
                   jax.ShapeDtypeStruct((1, LANES), F32), jax.ShapeDtypeStruct((8, HG_W), F32)],
        scratch_shapes=[pltpu.VMEM((4, LANES, LANES), F32), pltpu.VMEM((cps, HGRN_CHUNK, HG_W), F32),
                        pltpu.VMEM((1, HG_W), F32)],
        sem=("arbitrary",), comm=comm)


def _ln_bwd_math(dy, xhat, rstd, g):
    dxh = dy * g
    return rstd * (dxh - jnp.mean(dxh, axis=1, keepdims=True)
                   - xhat * jnp.mean(dxh * xhat, axis=1, keepdims=True))


def _mm_rows(a, b, extras, *, name, epilogue, out_shape, out_specs, tb=False, tm=512, tk=1408):
    m, kdim = a.shape
    n = b.shape[0] if tb else b.shape[1]
    tm = _pick(m, (tm, 256, 128))
    tk = _pick(kdim, (tk, 1408, 1024, 768, 512, 256, 128))
    nk = kdim // tk
    b_spec = pl.BlockSpec((n, tk), lambda i, k: (0, k)) if tb else pl.BlockSpec((tk, n), lambda i, k: (k, 0))
    dims = _NT if tb else _NN
    n_ex, n_out = len(extras), len(out_shape)

    def body(*refs):
        a_ref, b_ref = refs[0], refs[1]
        ex_refs = refs[2:2 + n_ex]
        o_refs = refs[2 + n_ex:2 + n_ex + n_out]
        acc_ref = refs[-1]
        i, k = pl.program_id(0), pl.program_id(1)

        @pl.when(k == 0)
        def _():
            acc_ref[...] = jnp.zeros_like(acc_ref)

        acc_ref[...] += _dot(a_ref[...], b_ref[...], dims)

        @pl.when(k == nk - 1)
        def _():
            epilogue(acc_ref[...], ex_refs, o_refs, i == 0)

    return pl.pallas_call(
        body, name=name, grid=(m // tm, nk),
        in_specs=[pl.BlockSpec((tm, tk), lambda i, k: (i, k)), b_spec] + [sp for _, sp in extras],
        out_specs=list(out_specs), out_shape=list(out_shape),
        scratch_shapes=[pltpu.VMEM((tm, n), F32)],
        compiler_params=_cp("arbitrary", "arbitrary"),
    )(a, b, *[arr for arr, _ in extras])


def _rows_specs(tm, d):
    row = pl.BlockSpec((tm, d), lambda i, k: (i, 0))
    vec = pl.BlockSpec((1, d), lambda i, k: (0, 0))
    col = pl.BlockSpec((tm, 1), lambda i, k: (i, 0))
    return row, vec, col


def _mm_ln_fwd(a, b, addend, g, beta, name, tm=512):
    t, d = addend.shape
    tm = _pick(t, (tm, 256, 128))
    row, vec, col = _rows_specs(tm, d)

    def epilogue(acc, ex, outs, first):
        z = acc + ex[0][...]
        mu = jnp.mean(z, axis=1, keepdims=True)
        zc = z - mu
        rstd = lax.rsqrt(jnp.mean(zc * zc, axis=1, keepdims=True) + LN_EPS)
        xhat = zc * rstd
        h = xhat * ex[1][...] + ex[2][...]
        outs[0][...] = h
        outs[1][...] = h.astype(BF16)
        outs[2][...] = xhat
        outs[3][...] = rstd

    return _mm_rows(a, b, [(addend, row), (g, vec), (beta, vec)], name=name, epilogue=epilogue, tm=tm,
                    out_shape=[jax.ShapeDtypeStruct((t, d), F32), jax.ShapeDtypeStruct((t, d), BF16),
                               jax.ShapeDtypeStruct((t, d), F32), jax.ShapeDtypeStruct((t, 1), F32)],
                    out_specs=[row, row, row, col])


def _mm_ln_loss_bwd(a, b, addend, target, g, beta, name, tm=1024):
    t, d = addend.shape
    tm = _pick(t, (tm, 256, 128))
    row, vec, _ = _rows_specs(tm, d)

    def epilogue(acc, ex, outs, first):
        dz_ref, dg_ref, db_ref, loss_ref = outs

        @pl.when(first)
        def _():
            dg_ref[...] = jnp.zeros_like(dg_ref)
            db_ref[...] = jnp.zeros_like(db_ref)
            loss_ref[...] = jnp.zeros_like(loss_ref)

        z = acc + ALPHA * ex[0][...]
        gg = ex[2][...]
        mu = jnp.mean(z, axis=1, keepdims=True)
        zc = z - mu
        rstd = lax.rsqrt(jnp.mean(zc * zc, axis=1, keepdims=True) + LN_EPS)
        xhat = zc * rstd
        err = xhat * gg + ex[3][...] - ex[1][...]
        loss_ref[...] += 0.5 * jnp.sum(jnp.mean(err * err, axis=1, keepdims=True))
        dy = err * (1.0 / d)
        dz_ref[...] = _ln_bwd_math(dy, xhat, rstd, gg)
        dg_ref[...] += jnp.sum(dy * xhat, axis=0, keepdims=True)
        db_ref[...] += jnp.sum(dy, axis=0, keepdims=True)

    return _mm_rows(a, b, [(addend, row), (target, row), (g, vec), (beta, vec)], name=name, epilogue=epilogue,
                    tm=tm,
                    out_shape=[jax.ShapeDtypeStruct((t, d), F32), jax.ShapeDtypeStruct((1, d), F32),
                               jax.ShapeDtypeStruct((1, d), F32), jax.ShapeDtypeStruct((1, LANES), F32)],
                    out_specs=[row, vec, vec, pl.BlockSpec((1, LANES), lambda i, k: (0, 0))])


def _mm_ln_bwd(a, b, addend, xhat, rstd, g, name, tm=1024):
    t, d = addend.shape
    tm = _pick(t, (tm, 256, 128))
    row, vec, col = _rows_specs(tm, d)

    def epilogue(acc, ex, outs, first):
        dz_ref, dg_ref, db_ref = outs

        @pl.when(first)
        def _():
            dg_ref[...] = jnp.zeros_like(dg_ref)
            db_ref[...] = jnp.zeros_like(db_ref)

        dy = acc + ALPHA * ex[0][...]
        xh = ex[1][...]
        dz_ref[...] = _ln_bwd_math(dy, xh, ex[2][...], ex[3][...])
        dg_ref[...] += jnp.sum(dy * xh, axis=0, keepdims=True)
        db_ref[...] += jnp.sum(dy, axis=0, keepdims=True)

    return _mm_rows(a, b, [(addend, row), (xhat, row), (rstd, col), (g, vec)], name=name, epilogue=epilogue, tm=tm,
                    out_shape=[jax.ShapeDtypeStruct((t, d), F32), jax.ShapeDtypeStruct((1, d), F32),
                               jax.ShapeDtypeStruct((1, d), F32)],
                    out_specs=[row, vec, vec])


CONV_TILE = 128
CONV_RB = 32
HALO = 8


def _sum8(x):
    acc = x[0:8]
    for r in range(8, x.shape[0], 8):
        acc = acc + x[r:r + 8]
    return acc


def _conv_fwd(u2, conv_w, conv_b, name):
    t = u2.shape[0]
    tr = _pick(t, (CONV_TILE,))
    hb = tr // HALO
    rb = CONV_RB

    def body(gp_ref, val_ref, prev_ref, w_ref, b_ref, out_ref, out_t_ref, ext):
        i = pl.program_id(0)
        ext[0:HALO, :] = jnp.where(i == 0, 0.0, prev_ref[...])
        ext[HALO:, :] = gp_ref[...]
        for c in range(D_FF // LANES):
            ln = slice(c * LANES, (c + 1) * LANES)
            w0, w1, w2, bb = w_ref[0:1, ln], w_ref[1:2, ln], w_ref[2:3, ln], b_ref[:, ln]
            pieces = []
            for r0 in range(0, tr, rb):
                gate = (ext[r0 + HALO - 2:r0 + HALO - 2 + rb, ln] * w0 + ext[r0 + HALO - 1:r0 + HALO - 1 + rb, ln] * w1
                        + ext[r0 + HALO:r0 + HALO + rb, ln] * w2 + bb)
                hm = gate * _sig(gate) * val_ref[r0:r0 + rb, ln]
                out_ref[r0:r0 + rb, ln] = hm.astype(out_ref.dtype)
                pieces.append(hm)
            out_t_ref[ln, :] = jnp.concatenate(pieces, axis=0).T.astype(out_t_ref.dtype)

    return pl.pallas_call(
        body, name=name, grid=(t // tr,),
        in_specs=[pl.BlockSpec((tr, D_FF), lambda i: (i, 0)), pl.BlockSpec((tr, D_FF), lambda i: (i, 1)),
                  pl.BlockSpec((HALO, D_FF), lambda i: (jnp.maximum(i * hb - 1, 0), 0)),
                  pl.BlockSpec((3, D_FF), lambda i: (0, 0)), pl.BlockSpec((1, D_FF), lambda i: (0, 0))],
        out_specs=[pl.BlockSpec((tr, D_FF), lambda i: (i, 0)), pl.BlockSpec((D_FF, tr), lambda i: (0, i))],
        out_shape=[jax.ShapeDtypeStruct((t, D_FF), BF16), jax.ShapeDtypeStruct((D_FF, t), BF16)],
        scratch_shapes=[pltpu.VMEM((tr + HALO, D_FF), F32)],
        compiler_params=_cp("parallel"),
    )(u2, u2, u2, conv_w, conv_b)


def _conv_bwd(d_hmid, u2, conv_w, conv_b, name, comm=None):
    t = u2.shape[0]
    tr = _pick(t, (CONV_TILE,))
    hb = tr // HALO
    last = t // HALO - 1
    rb = CONV_RB
    re = rb + HALO

    def body(gp_ref, gp_prev_ref, gp_next_ref, val_ref, val_next_ref, dh_ref, dh_next_ref, w_ref, b_ref,
             du_ref, du_t_ref, dw_ref, dcb_ref, ext, dg_s):
        i = pl.program_id(0)

        @pl.when(i == 0)
        def _():
            dw_ref[...] = jnp.zeros_like(dw_ref)
            dcb_ref[...] = jnp.zeros_like(dcb_ref)

        ext[0:HALO, :] = jnp.where(i == 0, 0.0, gp_prev_ref[...])
        ext[HALO:HALO + tr, :] = gp_ref[...]
        ext[HALO + tr:, :] = gp_next_ref[...]
        next_in_seq = (i + 1) * tr < t
        for c in range(D_FF // LANES):
            ln = slice(c * LANES, (c + 1) * LANES)
            w0, w1, w2, bb = w_ref[0:1, ln], w_ref[1:2, ln], w_ref[2:3, ln], b_ref[:, ln]
            acc_b = jnp.zeros((8, LANES), F32)
            acc_w = [jnp.zeros((8, LANES), F32) for _ in range(3)]
            gp_pieces, val_pieces = [], []
            for r0 in range(0, tr, rb):
                g_m2 = ext[r0 + HALO - 2:r0 + HALO - 2 + re, ln]
                g_m1 = ext[r0 + HALO - 1:r0 + HALO - 1 + re, ln]
                g_0 = ext[r0 + HALO:r0 + HALO + re, ln]
                gate = g_m2 * w0 + g_m1 * w1 + g_0 * w2 + bb
                sg = _sig(gate)
                if r0 + re <= tr:
                    val = val_ref[r0:r0 + re, ln]
                    dh = dh_ref[r0:r0 + re, ln]
                else:
                    val = jnp.concatenate([val_ref[r0:r0 + rb, ln], val_next_ref[:, ln]], axis=0)
                    dh = jnp.concatenate([dh_ref[r0:r0 + rb, ln],
                                          jnp.where(next_in_seq, dh_next_ref[:, ln], 0.0)], axis=0)
                dgate = dh * val * _dsilu(gate, sg)
                dg_s[:, ln] = dgate
                dg0 = dgate[0:rb]
                d_gp = dg_s[2:2 + rb, ln] * w0 + dg_s[1:1 + rb, ln] * w1 + dg0 * w2
                d_val = dh[0:rb] * (gate[0:rb] * sg[0:rb])
                du_ref[r0:r0 + rb, ln] = d_gp.astype(du_ref.dtype)
                du_ref[r0:r0 + rb, D_FF + c * LANES:D_FF + (c + 1) * LANES] = d_val.astype(du_ref.dtype)
                gp_pieces.append(d_gp)
                val_pieces.append(d_val)
                acc_b = acc_b + _sum8(dg0)
                acc_w[0] = acc_w[0] + _sum8(dg0 * g_m2[0:rb])
                acc_w[1] = acc_w[1] + _sum8(dg0 * g_m1[0:rb])
                acc_w[2] = acc_w[2] + _sum8(dg0 * g_0[0:rb])
            du_t_ref[ln, :] = jnp.concatenate(gp_pieces, axis=0).T.astype(du_t_ref.dtype)
            du_t_ref[D_FF + c * LANES:D_FF + (c + 1) * LANES, :] = (
                jnp.concatenate(val_pieces, axis=0).T.astype(du_t_ref.dtype))
            dcb_ref[:, ln] += jnp.sum(acc_b, axis=0, keepdims=True)
            for j in range(3):
                dw_ref[j:j + 1, ln] += jnp.sum(acc_w[j], axis=0, keepdims=True)

    cur = lambda col: pl.BlockSpec((tr, D_FF), lambda i: (i, col))
    nxt = lambda col: pl.BlockSpec((HALO, D_FF), lambda i: (jnp.minimum((i + 1) * hb, last), col))
    return _call(
        body, name=name, grid=(t // tr,), ins=[u2, u2, u2, u2, u2, d_hmid, d_hmid, conv_w, conv_b],
        in_specs=[cur(0), pl.BlockSpec((HALO, D_FF), lambda i: (jnp.maximum(i * hb - 1, 0), 0)), nxt(0),
                  cur(1), nxt(1), cur(0), nxt(0),
                  pl.BlockSpec((3, D_FF), lambda i: (0, 0)), pl.BlockSpec((1, D_FF), lambda i: (0, 0))],
        out_specs=[pl.BlockSpec((tr, 2 * D_FF), lambda i: (i, 0)), pl.BlockSpec((2 * D_FF, tr), lambda i: (0, i)),
                   pl.BlockSpec((8, D_FF), lambda i: (0, 0)), pl.BlockSpec((1, D_FF), lambda i: (0, 0))],
        out_shape=[jax.ShapeDtypeStruct((t, 2 * D_FF), BF16), jax.ShapeDtypeStruct((2 * D_FF, t), BF16),
                   jax.ShapeDtypeStruct((8, D_FF), F32), jax.ShapeDtypeStruct((1, D_FF), F32)],
        scratch_shapes=[pltpu.VMEM((tr + 2 * HALO, D_FF), F32), pltpu.VMEM((re, D_FF), F32)],
        sem=("arbitrary",), comm=comm)


def _adamw(w, g, m, v, name):
    rows, cols = w.shape
    tr = _pick(rows, (256, 128, 64, 32, 16, 8))

    def body(w_ref, g_ref, m_ref, v_ref, d_ref, nm_ref, nv_ref):
        d_ref[...], nm_ref[...], nv_ref[...] = _adamw_math(w_ref[...], g_ref[...], m_ref[...], v_ref[...])

    spec = pl.BlockSpec((tr, cols), lambda i: (i, 0))
    shp = jax.ShapeDtypeStruct((rows, cols), F32)
    return pl.pallas_call(
        body, name=name, grid=(rows // tr,),
        in_specs=[spec, spec, spec, spec], out_specs=[spec, spec, spec], out_shape=[shp, shp, shp],
        compiler_params=_cp("parallel"),
    )(w, g, m, v)


def _pad_rows(a, rows):
    return jnp.pad(a, ((0, rows - a.shape[0]), (0, 0)))


SMALL_LAYOUT = (("ln1_g", 1024), ("ln1_b", 1024), ("b_in", 2816), ("sinks", 8), ("hgrn_lb", 1024),
                ("hgrn_norm_g", 128), ("ln2_g", 1024), ("ln2_b", 1024), ("conv_b", 2816), ("loss", 1))
SMALL_SHAPES = {"ln1_g": (1, 1024), "ln1_b": (1, 1024), "b_in": (1, 2816), "sinks": (1, 8), "hgrn_lb": (2, 512),
                "hgrn_norm_g": (1, 128), "ln2_g": (1, 1024), "ln2_b": (1, 1024), "conv_b": (1, 2816),
                "loss": (1,)}


def _pack_small(parts):
    rows = []
    for name, size in SMALL_LAYOUT:
        flat = parts[name].reshape(-1).astype(F32)
        padded = -(-size // LANES) * LANES
        rows.append(jnp.pad(flat, (0, padded - size)).reshape(-1, LANES))
    return _pad_rows(jnp.concatenate(rows, axis=0), SMALL_ROWS)


def _unpack_small(pack):
    out, r = {}, 0
    for name, size in SMALL_LAYOUT:
        nrows = -(-size // LANES)
        out[name] = pack[r:r + nrows].reshape(-1)[:size].reshape(SMALL_SHAPES[name])
        r += nrows
    return out


def _own(full, rows):
    return lax.dynamic_slice_in_dim(full, _me() * rows, rows, axis=0)


def kernel(x, positions, ln1_g, ln1_b, w_in, b_in, sinks, hgrn_lb, hgrn_norm_g, w_o, ln2_g, ln2_b, w_up, conv_w, conv_b, w_down, loss_target, m_ln1_g, m_ln1_b, m_w_in, m_b_in, m_sinks, m_hgrn_lb, m_hgrn_norm_g, m_w_o, m_ln2_g, m_ln2_b, m_w_up, m_conv_w, m_conv_b, m_w_down, v_ln1_g, v_ln1_b, v_w_in, v_b_in, v_sinks, v_hgrn_lb, v_hgrn_norm_g, v_w_o, v_ln2_g, v_ln2_b, v_w_up, v_conv_w, v_conv_b, v_w_down):
    t = x.shape[1]
    x2 = x[0]
    xb = x2.astype(BF16)
    target = loss_target[0]
    pos_col = positions.reshape(t, 1)

    w_in_t_s = w_in[0].T.astype(BF16)
    w_up_t_s = w_up[0].T.astype(BF16)
    w_o_s = w_o[0].astype(BF16)
    w_down_s = w_down[0].astype(BF16)
    w_in_t_g, cw_g = _comm_only(_Comm([("gather", w_in_t_s), ("gather", _pad_rows(conv_w[0], 8))]), "ag_w_in")
    w_in_t = w_in_t_g.reshape(D_FF, D_MODEL)
    w_a_t, w_h_t = w_in_t[:UA_W], w_in_t[UA_W:]
    conv_w_f = cw_g[:, 0:3].transpose(1, 0, 2).reshape(3, D_FF)

    ua = _mm(xb, w_a_t, tb=True, bias=b_in[:, :UA_W], name="fwd_in_attn")
    uh = _mm(xb, w_h_t, tb=True, bias=b_in[:, UA_W:], name="fwd_in_hgrn")
    ctab, stab = _rope_tables(pos_col, "rope_tables")
    (a_out, a_out_t), (w_o_g,) = _attn_fwd(ua, ctab, stab, sinks, "attn_fwd", comm=_Comm([("gather", w_o_s)]))
    (r_out, r_out_t, o_pre, states), (w_up_t_g,) = _hgrn_fwd(uh, hgrn_lb, hgrn_norm_g, "hgrn_fwd",
                                                              comm=_Comm([("gather", w_up_t_s)]))
    w_o_f = w_o_g.reshape(D_MODEL, D_MODEL)
    w_up_t = w_up_t_g.reshape(2 * D_FF, D_MODEL)
    z1 = _mm(a_out, w_o_f[:ATTN_W], addend=x2, addend_scale=ALPHA, name="fwd_o_attn")
    h1, h1b, xhat1, rstd1 = _mm_ln_fwd(r_out, w_o_f[ATTN_W:], z1, ln1_g, ln1_b, "fwd_o_hgrn_ln1")
    u2, (w_down_g,) = _mm(h1b, w_up_t, tb=True, tn=1408, name="fwd_up", comm=_Comm([("gather", w_down_s)]))
    w_down_f = w_down_g.reshape(D_FF, D_MODEL)
    hmid, hmid_t = _conv_fwd(u2, conv_w_f, conv_b, "conv_fwd")
    dz2, d_ln2_g, d_ln2_b, loss_part = _mm_ln_loss_bwd(hmid, w_down_f, h1, target, ln2_g, ln2_b,
                                                       "fwd_down_ln2_loss")

    d_hmid = _mm(dz2, w_down_f, tb=True, tn=1408, name="bwd_down_dx")
    d_w_down, d_w_down_b = _mm(hmid_t, dz2, out_dtype2=BF16, tm=1408, name="bwd_down_dw")
    (d_u2, d_u2_t, d_conv_w8, d_conv_b), (recv_down,) = _conv_bwd(
        d_hmid, u2, conv_w_f, conv_b, "conv_bwd",
        comm=_Comm([("exchange", d_w_down_b.reshape(N_DEV, SHARD_DOWN, D_MODEL))]))
    dz1, d_ln1_g, d_ln1_b = _mm_ln_bwd(d_u2, w_up_t, dz2, xhat1, rstd1, ln1_g, "bwd_up_dx_ln1")
    d_w_up_t, d_w_up_t_b = _mm(d_u2_t, h1b, out_dtype2=BF16, tm=1408, name="bwd_up_dw")
    d_a = _mm(dz1, w_o_f[:ATTN_W], tb=True, name="bwd_o_dx_attn")
    d_r = _mm(dz1, w_o_f[ATTN_W:], tb=True, name="bwd_o_dx_hgrn")
    d_w_o_a, d_w_o_a_b = _mm(a_out_t, dz1, out_dtype2=BF16, name="bwd_o_dw_attn")
    d_w_o_r, d_w_o_r_b = _mm(r_out_t, dz1, out_dtype2=BF16, name="bwd_o_dw_hgrn")
    d_w_o = jnp.concatenate([d_w_o_a, d_w_o_r], axis=0)
    d_w_o_b = jnp.concatenate([d_w_o_a_b, d_w_o_r_b], axis=0)
    d_w_up_x = d_w_up_t_b.reshape(N_DEV, SHARD_UP, D_MODEL)
    half = SHARD_UP // 2
    (d_uh, d_uh_t, d_bias_h, d_norm_g, d_lb8), (recv_up_a,) = _hgrn_bwd(
        uh, o_pre, d_r, states, hgrn_lb, hgrn_norm_g, "hgrn_bwd",
        comm=_Comm([("exchange", d_w_up_x, 0, half)]))
    d_cw_x = d_conv_w8.reshape(8, N_DEV, SHARD_IN).transpose(1, 0, 2)
    (d_ua, d_ua_t, d_bias_a, d_sinks), (recv_up_b, recv_o, recv_cw) = _attn_bwd(
        ua, d_a, ctab, stab, sinks, "attn_bwd",
        comm=_Comm([("exchange", d_w_up_x, half, half), ("exchange", d_w_o_b.reshape(N_DEV, SHARD_O, D_MODEL)),
                    ("exchange", d_cw_x)]))
    d_w_a_t, d_w_a_t_b = _mm(d_ua_t, xb, out_dtype2=BF16, name="bwd_in_dw_attn")
    d_w_h_t, d_w_h_t_b = _mm(d_uh_t, xb, out_dtype2=BF16, name="bwd_in_dw_hgrn")
    d_w_in_t = jnp.concatenate([d_w_a_t, d_w_h_t], axis=0)
    d_w_in_t_b = jnp.concatenate([d_w_a_t_b, d_w_h_t_b], axis=0)
    small_local = _pack_small({
        "ln1_g": d_ln1_g, "ln1_b": d_ln1_b, "b_in": jnp.concatenate([d_bias_a, d_bias_h], axis=1),
        "sinks": d_sinks[:, :8], "hgrn_lb": d_lb8[0:2], "hgrn_norm_g": d_norm_g, "ln2_g": d_ln2_g,
        "ln2_b": d_ln2_b, "conv_b": d_conv_b, "loss": loss_part[:, :1]})
    dx, (recv_in, small_g) = _mm(d_uh, w_h_t, addend=dz1, addend_scale=ALPHA, name="bwd_in_dx_hgrn",
                                 comm=_Comm([("exchange", d_w_in_t_b.reshape(N_DEV, SHARD_IN, D_MODEL)),
                                             ("gather", small_local)]))
    dx = _mm(d_ua, w_a_t, addend=dx, tk=768, name="bwd_in_dx_attn")

    g_in_pad = _sum_shards_t([recv_in], _own(d_w_in_t, SHARD_IN), "sum_w_in")
    g_up_pad = _sum_shards_t([recv_up_a, recv_up_b], _own(d_w_up_t, SHARD_UP), "sum_w_up")
    res_in = _adamw_padded_g(w_in[0], g_in_pad, m_w_in[0], v_w_in[0], "adamw_w_in")
    res_up = _adamw_padded_g(w_up[0], g_up_pad, m_w_up[0], v_w_up[0], "adamw_w_up")
    res_o = _sum_shards_adamw(recv_o, _own(d_w_o, SHARD_O), w_o[0], m_w_o[0], v_w_o[0], "adamw_w_o")
    res_down = _sum_shards_adamw(recv_down, _own(d_w_down, SHARD_DOWN), w_down[0], m_w_down[0], v_w_down[0],
                                 "adamw_w_down")
    g_cw = _sum_slots(recv_cw, "sum_conv_w")
    cw8 = lambda a: _pad_rows(a, 8)
    res_cw = (g_cw,) + tuple(_adamw(cw8(conv_w[0]), g_cw, cw8(m_conv_w[0]), cw8(v_conv_w[0]), "adamw_conv_w"))
    big = {"w_in": [r[None] for r in res_in], "w_up": [r[None] for r in res_up],
           "w_o": [r[None] for r in res_o], "w_down": [r[None] for r in res_down],
           "conv_w": [r[None, 0:3] for r in res_cw]}

    small_sum = _sum_slots(small_g, "ar_small_sum")
    gs = _unpack_small(small_sum)
    loss = gs["loss"][0]
    zero1 = jnp.zeros((1,), F32)
    w_small = _pack_small({"ln1_g": ln1_g, "ln1_b": ln1_b, "b_in": b_in, "sinks": sinks, "hgrn_lb": hgrn_lb,
                           "hgrn_norm_g": hgrn_norm_g, "ln2_g": ln2_g, "ln2_b": ln2_b, "conv_b": conv_b,
                           "loss": zero1})
    m_small = _pack_small({"ln1_g": m_ln1_g, "ln1_b": m_ln1_b, "b_in": m_b_in, "sinks": m_sinks,
                           "hgrn_lb": m_hgrn_lb, "hgrn_norm_g": m_hgrn_norm_g, "ln2_g": m_ln2_g,
                           "ln2_b": m_ln2_b, "conv_b": m_conv_b, "loss": zero1})
    v_small = _pack_small({"ln1_g": v_ln1_g, "ln1_b": v_ln1_b, "b_in": v_b_in, "sinks": v_sinks,
                           "hgrn_lb": v_hgrn_lb, "hgrn_norm_g": v_hgrn_norm_g, "ln2_g": v_ln2_g,
                           "ln2_b": v_ln2_b, "conv_b": v_conv_b, "loss": zero1})
    small = [gs] + [_unpack_small(p) for p in _adamw(w_small, small_sum, m_small, v_small, "adamw_small")]

    order = ["ln1_g", "ln1_b", "w_in", "b_in", "sinks", "hgrn_lb", "hgrn_norm_g", "w_o", "ln2_g", "ln2_b",
             "w_up", "conv_w", "conv_b", "w_down"]

    def pick(idx):
        return [big[n][idx] if n in big else small[idx][n] for n in order]

    return (loss, dx[None], *pick(0), *pick(1), *pick(2), *pick(3))
```

```python
import functools

import jax
import jax.numpy as jnp
import numpy as np
from jax import lax
from jax.experimental import pallas as pl
from jax.experimental.pallas import tpu as pltpu

F32 = jnp.float32
BF16 = jnp.bfloat16

N_DEV = 8
D_MODEL = 1024
D_FF = 2816
ATTN_W = 512
KV_W = 128
UA_W = ATTN_W + 2 * KV_W
UH_W = 2048
HG_W = 512
ATTN_BLOCK = 128
HGRN_CHUNK = 64
HGRN_SUB = 16
HGRN_CHUNKS_PER_STEP = 4
EXP_CLAMP = 85.0
NEG_BIG = -1e30
LN_EPS = 1e-5
RMS_EPS = 1e-6
ALPHA = 2.0 ** 0.25
ATTN_SCALE = 0.125
ROPE_THETA = 500000.0

ADAM_LR = 0.001
ADAM_B1 = 0.9
ADAM_B2 = 0.999
ADAM_EPS = 1e-08
ADAM_WD = 0.01
ADAM_STEP = 10

LANES = 128
VMEM_LIMIT_BYTES = 56 * 1024 * 1024

SHARD_IN = D_FF // N_DEV
SHARD_UP = 2 * D_FF // N_DEV
SHARD_O = D_MODEL // N_DEV
SHARD_DOWN = D_FF // N_DEV
SMALL_ROWS = 88

_MESH = pl.DeviceIdType.MESH
_NT = (((1,), (1,)), ((), ()))
_NN = (((1,), (0,)), ((), ()))
_TN = (((0,), (0,)), ((), ()))


def _cp(*sem):
    if sem:
        return pltpu.CompilerParams(dimension_semantics=sem, vmem_limit_bytes=VMEM_LIMIT_BYTES)
    return pltpu.CompilerParams(vmem_limit_bytes=VMEM_LIMIT_BYTES)


def _sig(x):
    return 0.5 * jnp.tanh(0.5 * x) + 0.5


def _dsilu(x, s):
    return s * (1.0 + x * (1.0 - s))


def _dot(a, b, dims):
    return lax.dot_general(a.astype(BF16), b.astype(BF16), dims, preferred_element_type=F32)


def _split(a):
    hi = a.astype(BF16)
    return hi, (a - hi.astype(F32)).astype(BF16)


def _dot3(a, b, dims):
    ah, al = _split(a)
    bh, bl = _split(b)
    d = functools.partial(lax.dot_general, dimension_numbers=dims, preferred_element_type=F32)
    return d(ah, bh) + (d(ah, bl) + d(al, bh))


def _pick(n, pref):
    for t in pref:
        if t <= n and n % t == 0:
            return t
    return n


def _my_coords():
    return lax.axis_index("x"), lax.axis_index("y"), lax.axis_index("c")


def _peer(k):
    x, y, c = _my_coords()
    return (1 - x if k & 4 else x, 1 - y if k & 2 else y, 1 - c if k & 1 else c)


def _me():
    x, y, c = _my_coords()
    return 4 * x + 2 * y + c


class _Comm:
    def __init__(self, items):
        self.items = []
        for it in items:
            arr = it["arr"]
            full = arr.shape[0] if it["kind"] == "gather" else arr.shape[1]
            first, count = it.get("rows", (0, full))
            self.items.append(dict(kind=it["kind"], arr=arr, first=first, count=count,
                                   dst_rows=it.get("dst_rows", count), dst_first=it.get("dst_first", 0),
                                   into=it.get("into")))
        self.n = len(self.items)
        self.arrays = [it["arr"] for it in self.items]
        self.intos = [(a, it["into"]) for a, it in enumerate(self.items) if it["into"] is not None]

    def out_shapes(self):
        return [jax.ShapeDtypeStruct((N_DEV, it["dst_rows"], it["arr"].shape[-1]), it["arr"].dtype)
                for it in self.items]

    def specs(self, n=None):
        return [pl.BlockSpec(memory_space=pl.ANY)] * (self.n if n is None else n)

    def scratch(self):
        return [pltpu.SemaphoreType.DMA(((N_DEV - 1) * self.n,)), pltpu.SemaphoreType.DMA(((N_DEV - 1) * self.n,)),
                pltpu.SemaphoreType.DMA((self.n,))]

    def _src(self, a, ref, dev):
        it = self.items[a]
        blk = ref if it["kind"] == "gather" else ref.at[dev]
        return blk.at[pl.ds(it["first"], it["count"])]

    def _dst(self, a, ref, slot):
        it = self.items[a]
        return ref.at[slot].at[pl.ds(it["dst_first"], it["count"])]

    def _copy(self, a, k, src, dst, sems, me, slot):
        other = jnp.bitwise_xor(me, k)
        idx = a * (N_DEV - 1) + k - 1
        return pltpu.make_async_remote_copy(
            src_ref=self._src(a, src, other), dst_ref=self._dst(a, dst, me if slot == "mine" else other),
            send_sem=sems[0].at[idx], recv_sem=sems[1].at[idx], device_id=_peer(k), device_id_type=_MESH)

    def _pass_on(self, a, k, dst, sems, me):
        slot = self._dst(a, dst, jnp.bitwise_xor(me, k))
        idx = a * (N_DEV - 1) + k
        return pltpu.make_async_remote_copy(
            src_ref=slot, dst_ref=slot, send_sem=sems[0].at[idx], recv_sem=sems[1].at[idx],
            device_id=_peer(1), device_id_type=_MESH)

    def _local(self, a, src, dst, sems, me):
        return pltpu.make_async_copy(self._src(a, src, me), self._dst(a, dst, me), sems[2].at[a])

    def start(self, srcs, dsts, sems):
        me = _me()
        for a, (src, dst) in enumerate(zip(srcs, dsts)):
            direct = (1, 2, 4, 6) if self.items[a]["kind"] == "gather" else range(1, N_DEV)
            self._local(a, src, dst, sems, me).start()
            for k in direct:
                self._copy(a, k, src, dst, sems, me, "mine").start()

    def wait(self, srcs, dsts, sems):
        me = _me()
        for a, (src, dst) in enumerate(zip(srcs, dsts)):
            if self.items[a]["kind"] == "gather":
                for k in (2, 4, 6):
                    self._copy(a, k, src, dst, sems, me, "theirs").wait_recv()
                    self._pass_on(a, k, dst, sems, me).start()
                for k in (1, 3, 5, 7):
                    self._copy(a, k, src, dst, sems, me, "theirs").wait_recv()
                for k in (1, 2, 4, 6):
                    self._copy(a, k, src, dst, sems, me, "mine").wait_send()
                for k in (2, 4, 6):
                    self._pass_on(a, k, dst, sems, me).wait_send()
            else:
                for k in range(1, N_DEV):
                    self._copy(a, k, src, dst, sems, me, "theirs").wait_recv()
                for k in range(1, N_DEV):
                    self._copy(a, k, src, dst, sems, me, "mine").wait_send()
            self._local(a, src, dst, sems, me).wait()


def _call(body, *, name, grid, ins, in_specs, out_specs, out_shape, scratch_shapes=(), sem, comm=None):
    n_in, n_out, n_scr = len(ins), len(out_shape), len(scratch_shapes)
    if comm is None:
        outs = pl.pallas_call(
            body, name=name, grid=grid, in_specs=list(in_specs), out_specs=list(out_specs),
            out_shape=list(out_shape), scratch_shapes=list(scratch_shapes), compiler_params=_cp(*sem))(*ins)
        return list(outs), []
    nc, n_into = comm.n, len(comm.intos)

    def hosted(*refs):
        pos = n_in
        c_in = refs[pos:pos + nc]
        pos += nc + n_into
        outs = refs[pos:pos + n_out]
        pos += n_out
        c_out = refs[pos:pos + nc]
        pos += nc
        scr = refs[pos:pos + n_scr]
        sems = refs[pos + n_scr:]
        ids = [pl.program_id(d) for d in range(len(grid))]
        first = functools.reduce(jnp.logical_and, [i == 0 for i in ids])
        last = functools.reduce(jnp.logical_and, [i == g - 1 for i, g in zip(ids, grid)])

        @pl.when(first)
        def _():
            comm.start(c_in, c_out, sems)

        body(*refs[:n_in], *outs, *scr)

        @pl.when(last)
        def _():
            comm.wait(c_in, c_out, sems)

    aliases = {n_in + nc + j: n_out + a for j, (a, _) in enumerate(comm.intos)}
    outs = pl.pallas_call(
        hosted, name=name, grid=grid, in_specs=list(in_specs) + comm.specs() + comm.specs(n_into),
        out_specs=list(out_specs) + comm.specs(), out_shape=list(out_shape) + comm.out_shapes(),
        scratch_shapes=list(scratch_shapes) + comm.scratch(), input_output_aliases=aliases,
        compiler_params=_cp(*(["arbitrary"] * len(grid))))(*ins, *comm.arrays, *[arr for _, arr in comm.intos])
    return list(outs[:n_out]), list(outs[n_out:])


def _sum_slots(gathered, name):
    _, rows, cols = gathered.shape

    def body(g_ref, out_ref):
        acc = g_ref[0]
        for s in range(1, N_DEV):
            acc = acc + g_ref[s]
        out_ref[...] = acc

    return pl.pallas_call(
        body, name=name,
        out_shape=jax.ShapeDtypeStruct((rows, cols), F32),
        compiler_params=_cp(),
    )(gathered)


def _slot_sum(recv_ref, own_ref, shape):
    me = _me()
    acc = jnp.zeros(shape, F32)
    for s in range(N_DEV):
        acc = acc + jnp.where(me == s, own_ref[...], recv_ref[s].astype(F32))
    return acc


def _adamw_math(w, g, m, v):
    nm = ADAM_B1 * m + (1.0 - ADAM_B1) * g
    nv = ADAM_B2 * v + (1.0 - ADAM_B2) * (g * g)
    m_hat = nm / (1.0 - ADAM_B1 ** ADAM_STEP)
    v_hat = nv / (1.0 - ADAM_B2 ** ADAM_STEP)
    return -ADAM_LR * (m_hat / (jnp.sqrt(v_hat) + ADAM_EPS) + ADAM_WD * w), nm, nv


def _sum_shards_adamw(recvs, own, w, m, v, name):
    rows_p, cols = recvs[0].shape[1], recvs[0].shape[2]
    n_p = len(recvs)
    tr = _pick(rows_p, (176, 128, 64, 32, 16, 8))
    tiles = rows_p // tr

    def body(*refs):
        recv_refs = refs[:n_p]
        own_ref, w_ref, m_ref, v_ref, g_ref, d_ref, nm_ref, nv_ref = refs[n_p:]
        for j in range(n_p):
            @pl.when(pl.program_id(0) == j)
            def _():
                g = _slot_sum(recv_refs[j], own_ref, (tr, cols))
                g_ref[...] = g
                d_ref[...], nm_ref[...], nv_ref[...] = _adamw_math(w_ref[...], g, m_ref[...], v_ref[...])

    spec = pl.BlockSpec((tr, cols), lambda p_, i: (p_ * tiles + i, 0))
    shp = jax.ShapeDtypeStruct((rows_p * n_p, cols), F32)
    return pl.pallas_call(
        body, name=name, grid=(n_p, tiles),
        in_specs=[pl.BlockSpec((N_DEV, tr, cols), functools.partial(lambda p_, i, j: (0, jnp.where(p_ == j, i, 0), 0), j=j))
                  for j in range(n_p)] + [spec, spec, spec, spec],
        out_specs=[spec, spec, spec, spec], out_shape=[shp, shp, shp, shp],
        compiler_params=_cp("arbitrary", "arbitrary"),
    )(*recvs, own, w, m, v)


def _mm(a, b, *, name, ta=False, tb=False, out_dtype=F32, out_dtype2=None, bias=None, addend=None,
        addend_scale=1.0, tm=1024, tn=1024, tk=1024, comm=None):
    kdim, m = a.shape if ta else a.shape[::-1]
    n = b.shape[0] if tb else b.shape[1]
    tm = _pick(m, (tm, 1408, 1024, 768, 512, 256, 128))
    tn = _pick(n, (tn, 1408, 1024, 768, 512, 256, 128))
    tk = _pick(kdim, (tk, 1408, 1024, 768, 512, 256, 128))
    nk = kdim // tk
    a_spec = pl.BlockSpec((tk, tm), lambda i, j, k: (k, i)) if ta else pl.BlockSpec((tm, tk), lambda i, j, k: (i, k))
    b_spec = pl.BlockSpec((tn, tk), lambda i, j, k: (j, k)) if tb else pl.BlockSpec((tk, tn), lambda i, j, k: (k, j))
    ins, specs = [a, b], [a_spec, b_spec]
    if bias is not None:
        ins.append(bias)
        specs.append(pl.BlockSpec((1, tn), lambda i, j, k: (0, j)))
    if addend is not None:
        ins.append(addend)
        specs.append(pl.BlockSpec((tm, tn), lambda i, j, k: (i, j)))
    dims = (((0,) if ta else (1,), (1,) if tb else (0,)), ((), ()))
    has_bias, has_addend, two = bias is not None, addend is not None, out_dtype2 is not None

    def body(*refs):
        a_ref, b_ref = refs[0], refs[1]
        pos = 2
        bias_ref = addend_ref = None
        if has_bias:
            bias_ref = refs[pos]
            pos += 1
        if has_addend:
            addend_ref = refs[pos]
            pos += 1
        o_refs, acc_ref = refs[pos:-1], refs[-1]
        k = pl.program_id(2)

        @pl.when(k == 0)
        def _():
            acc_ref[...] = jnp.zeros_like(acc_ref)

        acc_ref[...] += _dot(a_ref[...], b_ref[...], dims)

        @pl.when(k == nk - 1)
        def _():
            r = acc_ref[...]
            if has_bias:
                r = r + bias_ref[...]
            if has_addend:
                r = r + addend_scale * addend_ref[...].astype(F32)
            for o_ref in o_refs:
                o_ref[...] = r.astype(o_ref.dtype)

    ospec = pl.BlockSpec((tm, tn), lambda i, j, k: (i, j))
    dtypes = [out_dtype] + ([out_dtype2] if two else [])
    outs, couts = _call(
        body, name=name, grid=(m // tm, n // tn, nk), ins=ins, in_specs=specs,
        out_specs=[ospec] * len(dtypes), out_shape=[jax.ShapeDtypeStruct((m, n), d) for d in dtypes],
        scratch_shapes=[pltpu.VMEM((tm, tn), F32)], sem=("parallel", "parallel", "arbitrary"), comm=comm)
    primary = tuple(outs) if two else outs[0]
    return (primary, couts) if comm is not None else primary


def _rope_lane_constants():
    inv_freq = np.float32(ROPE_THETA) ** (-np.arange(8, dtype=np.float32) * np.float32(2.0 / 16.0))
    lane = np.arange(LANES) % 64
    freq = np.where(lane < 16, inv_freq[lane % 8], 0.0).astype(np.float32)
    sign = np.where(lane < 8, -1.0, np.where(lane < 16, 1.0, 0.0)).astype(np.float32)
    return jnp.asarray(freq)[None, :], jnp.asarray(sign)[None, :]


def _prep(pos_col, x2, name, comm):
    t, d = x2.shape
    tr = _pick(t, (512, 256, 128))
    freq, sign = _rope_lane_constants()

    def body(pos_ref, freq_ref, sign_ref, x_ref, c_ref, s_ref, xb_ref):
        ang = pos_ref[...].astype(F32) * freq_ref[...]
        c_ref[...] = jnp.cos(ang)
        s_ref[...] = sign_ref[...] * jnp.sin(ang)
        xb_ref[...] = x_ref[...].astype(BF16)

    tab = pl.BlockSpec((tr, LANES), lambda i: (i, 0))
    return _call(
        body, name=name, grid=(t // tr,), ins=[pos_col, freq, sign, x2],
        in_specs=[pl.BlockSpec((tr, 1), lambda i: (i, 0)), pl.BlockSpec((1, LANES), lambda i: (0, 0)),
                  pl.BlockSpec((1, LANES), lambda i: (0, 0)), pl.BlockSpec((tr, d), lambda i: (i, 0))],
        out_specs=[tab, tab, pl.BlockSpec((tr, d), lambda i: (i, 0))],
        out_shape=[jax.ShapeDtypeStruct((t, LANES), F32), jax.ShapeDtypeStruct((t, LANES), F32),
                   jax.ShapeDtypeStruct((t, d), BF16)],
        sem=("parallel",), comm=comm)


def _swap8(t):
    width = t.shape[1]
    lane = jnp.bitwise_and(lax.broadcasted_iota(jnp.int32, t.shape, 1), 63)
    return jnp.where(lane < 8, pltpu.roll(t, width - 8, 1), jnp.where(lane < 16, pltpu.roll(t, 8, 1), 0.0))


def _rope(t, c, s):
    return t * c + _swap8(t) * s


def _rope_bwd(d, c, s):
    return d * c + _swap8(d * s)


def _tile4(a):
    return jnp.concatenate([a, a, a, a], axis=1)


def _attn_band(n, k_cur, k_prev, v_cur, v_prev, c_cur, s_cur, c_prev, s_prev):
    kband = jnp.concatenate([_rope(k_prev, c_prev, s_prev), _rope(k_cur, c_cur, s_cur)], axis=0)
    vband = jnp.concatenate([v_prev, v_cur], axis=0)
    qi = lax.broadcasted_iota(jnp.int32, (ATTN_BLOCK, 2 * ATTN_BLOCK), 0)
    kj = lax.broadcasted_iota(jnp.int32, (ATTN_BLOCK, 2 * ATTN_BLOCK), 1)
    dist = qi + ATTN_BLOCK - kj
    valid = (dist >= 0) & (dist < ATTN_BLOCK) & (n * ATTN_BLOCK - ATTN_BLOCK + kj >= 0)
    return (kband.astype(BF16), pltpu.roll(kband, 64, 1).astype(BF16),
            vband.astype(BF16), pltpu.roll(vband, 64, 1).astype(BF16), valid)


def _attn_probs(raw, valid, sink, axis):
    s = jnp.where(valid, raw * ATTN_SCALE, NEG_BIG)
    m = jnp.maximum(jnp.max(s, axis=axis, keepdims=True), sink)
    p = jnp.exp(s - m)
    esink = jnp.exp(sink - m)
    z = jnp.sum(p, axis=axis, keepdims=True) + esink
    return p / z, esink / z


def _attn_valid_t(n):
    kj = lax.broadcasted_iota(jnp.int32, (2 * ATTN_BLOCK, ATTN_BLOCK), 0)
    qi = lax.broadcasted_iota(jnp.int32, (2 * ATTN_BLOCK, ATTN_BLOCK), 1)
    dist = qi + ATTN_BLOCK - kj
    return (dist >= 0) & (dist < ATTN_BLOCK) & (n * ATTN_BLOCK - ATTN_BLOCK + kj >= 0)


def _attn_specs(nb):
    def cur(col, width=KV_W):
        return pl.BlockSpec((ATTN_BLOCK, width), lambda n: (jnp.minimum(n, nb - 1), col))

    def prev(col):
        return pl.BlockSpec((ATTN_BLOCK, KV_W), lambda n: (jnp.maximum(n - 1, 0), col))

    ua_specs = [cur(0, ATTN_W), cur(4), prev(4), cur(5), prev(5)]
    tab_specs = [cur(0), cur(0), prev(0), prev(0)]
    return ua_specs, tab_specs


def _attn_fwd(ua, ctab, stab, sinks, name, comm=None):
    t = ua.shape[0]
    nb = t // ATTN_BLOCK
    ua_specs, tab_specs = _attn_specs(nb)

    def body(q_ref, kc_ref, kp_ref, vc_ref, vp_ref, cc_ref, sc_ref, cp_ref, sp_ref, sink_ref, o_ref, o_t_ref):
        n = pl.program_id(0)
        cc, sc = cc_ref[...], sc_ref[...]
        kb, kb_r, vb, vb_r, valid = _attn_band(n, kc_ref[...], kp_ref[...], vc_ref[...], vp_ref[...],
                                               cc, sc, cp_ref[...], sp_ref[...])
        qr = _rope(q_ref[...], _tile4(cc), _tile4(sc))
        lo = lax.broadcasted_iota(jnp.int32, (ATTN_BLOCK, LANES), 1) < 64
        heads = []
        for j in range(4):
            qj = qr[:, j * LANES:(j + 1) * LANES]
            for is_lo in (True, False):
                aligned = is_lo == (j < 2)
                qm = jnp.where(lo if is_lo else jnp.logical_not(lo), qj, 0.0).astype(BF16)
                raw = lax.dot_general(qm, kb if aligned else kb_r, _NT, preferred_element_type=F32)
                heads.append((raw, vb if aligned else vb_r, sink_ref[0, len(heads)]))
        halves = []
        for raw, vv, sink in heads:
            probs, _ = _attn_probs(raw, valid, sink, 1)
            halves.append(lax.dot_general(probs.astype(BF16), vv, _NN, preferred_element_type=F32))
        outs = [jnp.where(lo, halves[2 * j], halves[2 * j + 1]) for j in range(4)]
        o_ref[...] = jnp.concatenate(outs, axis=1).astype(o_ref.dtype)
        for j in range(4):
            o_t_ref[j * LANES:(j + 1) * LANES, :] = outs[j].T.astype(o_t_ref.dtype)

    return _call(
        body, name=name, grid=(nb,), ins=[ua, ua, ua, ua, ua, ctab, stab, ctab, stab, sinks],
        in_specs=ua_specs + tab_specs + [pl.BlockSpec(memory_space=pltpu.SMEM)],
        out_specs=[pl.BlockSpec((ATTN_BLOCK, ATTN_W), lambda n: (n, 0)),
                   pl.BlockSpec((ATTN_W, ATTN_BLOCK), lambda n: (0, n))],
        out_shape=[jax.ShapeDtypeStruct((t, ATTN_W), BF16), jax.ShapeDtypeStruct((ATTN_W, t), BF16)],
        sem=("parallel",), comm=comm)


def _attn_bwd(ua, d_out, ctab, stab, sinks, name, comm=None):
    t = ua.shape[0]
    nb = t // ATTN_BLOCK
    ua_specs, tab_specs = _attn_specs(nb)

    def body(q_ref, kc_ref, kp_ref, vc_ref, vp_ref, cc_ref, sc_ref, cp_ref, sp_ref, do_ref, sink_ref,
             dua_ref, dua_t_ref, dbias_ref, dsink_ref, dq_c, dk_c, dv_c, dq_n, dk_n, dv_n):
        n = pl.program_id(0)

        @pl.when(n == 0)
        def _():
            dq_c[...] = jnp.zeros_like(dq_c)
            dk_c[...] = jnp.zeros_like(dk_c)
            dv_c[...] = jnp.zeros_like(dv_c)
            dbias_ref[...] = jnp.zeros_like(dbias_ref)
            dsink_ref[...] = jnp.zeros_like(dsink_ref)

        @pl.when(n == nb)
        def _():
            dq_n[...] = jnp.zeros_like(dq_n)
            dk_n[...] = jnp.zeros_like(dk_n)
            dv_n[...] = jnp.zeros_like(dv_n)

        @pl.when(n < nb)
        def _():
            cc, sc = cc_ref[...], sc_ref[...]
            kb, kb_r, vb, vb_r, valid = _attn_band(n, kc_ref[...], kp_ref[...], vc_ref[...], vp_ref[...],
                                                   cc, sc, cp_ref[...], sp_ref[...])
            valid_t = _attn_valid_t(n)
            c4, s4 = _tile4(cc), _tile4(sc)
            qr = _rope(q_ref[...], c4, s4)
            do = do_ref[...].astype(F32)
            lane = lax.broadcasted_iota(jnp.int32, (ATTN_BLOCK, LANES), 1)
            lo = lane < 64
            lane_row = lax.broadcasted_iota(jnp.int32, (1, LANES), 1)
            heads = []
            for j in range(4):
                qj = qr[:, j * LANES:(j + 1) * LANES]
                doj = do[:, j * LANES:(j + 1) * LANES]
                for is_lo in (True, False):
                    aligned = is_lo == (j < 2)
                    msk = lo if is_lo else jnp.logical_not(lo)
                    kk = kb if aligned else kb_r
                    vv = vb if aligned else vb_r
                    qm = jnp.where(msk, qj, 0.0).astype(BF16)
                    dom = jnp.where(msk, doj, 0.0).astype(BF16)
                    heads.append(dict(
                        aligned=aligned, kk=kk, qm=qm, dom=dom, sink=sink_ref[0, len(heads)],
                        raw=lax.dot_general(qm, kk, _NT, preferred_element_type=F32),
                        dp=lax.dot_general(dom, vv, _NT, preferred_element_type=F32),
                        raw_t=lax.dot_general(kk, qm, _NT, preferred_element_type=F32),
                        dp_t=lax.dot_general(vv, dom, _NT, preferred_element_type=F32)))
            dk_band = jnp.zeros((2 * ATTN_BLOCK, LANES), F32)
            dv_band = jnp.zeros((2 * ATTN_BLOCK, LANES), F32)
            dsink = jnp.zeros((1, LANES), F32)
            halves = []
            for head, hd in enumerate(heads):
                probs, psink = _attn_probs(hd["raw"], valid, hd["sink"], 1)
                delta = jnp.sum(probs * hd["dp"], axis=1, keepdims=True)
                ds = (probs * (hd["dp"] - delta) * ATTN_SCALE).astype(BF16)
                dsink = dsink + jnp.where(lane_row == head, -jnp.sum(psink * delta), 0.0)
                halves.append(lax.dot_general(ds, hd["kk"], _NN, preferred_element_type=F32))
                probs_t, _ = _attn_probs(hd["raw_t"], valid_t, hd["sink"], 0)
                delta_t = jnp.sum(probs_t * hd["dp_t"], axis=0, keepdims=True)
                ds_t = (probs_t * (hd["dp_t"] - delta_t) * ATTN_SCALE).astype(BF16)
                dk_h = lax.dot_general(ds_t, hd["qm"], _NN, preferred_element_type=F32)
                dv_h = lax.dot_general(probs_t.astype(BF16), hd["dom"], _NN, preferred_element_type=F32)
                if not hd["aligned"]:
                    dk_h = pltpu.roll(dk_h, 64, 1)
                    dv_h = pltpu.roll(dv_h, 64, 1)
                dk_band = dk_band + dk_h
                dv_band = dv_band + dv_h
            dqs = [jnp.where(lo, halves[2 * j], halves[2 * j + 1]) for j in range(4)]
            dq_n[...] = _rope_bwd(jnp.concatenate(dqs, axis=1), c4, s4)
            dk_n[...] = dk_band
            dv_n[...] = dv_band
            dsink_ref[...] += dsink

        dk_prev = _rope_bwd(dk_c[...] + dk_n[0:ATTN_BLOCK, :], cp_ref[...], sp_ref[...])
        dv_prev = dv_c[...] + dv_n[0:ATTN_BLOCK, :]
        full = jnp.concatenate([dq_c[...], dk_prev, dv_prev], axis=1)
        dua_ref[...] = full.astype(dua_ref.dtype)
        for j in range(UA_W // LANES):
            dua_t_ref[j * LANES:(j + 1) * LANES, :] = full[:, j * LANES:(j + 1) * LANES].T.astype(dua_t_ref.dtype)
        dbias_ref[...] += jnp.sum(full, axis=0, keepdims=True)
        dq_c[...] = dq_n[...]
        dk_c[...] = dk_n[ATTN_BLOCK:, :]
        dv_c[...] = dv_n[ATTN_BLOCK:, :]

    return _call(
        body, name=name, grid=(nb + 1,), ins=[ua, ua, ua, ua, ua, ctab, stab, ctab, stab, d_out, sinks],
        in_specs=ua_specs + tab_specs + [
            pl.BlockSpec((ATTN_BLOCK, ATTN_W), lambda n: (jnp.minimum(n, nb - 1), 0)),
            pl.BlockSpec(memory_space=pltpu.SMEM)],
        out_specs=[pl.BlockSpec((ATTN_BLOCK, UA_W), lambda n: (jnp.maximum(n - 1, 0), 0)),
                   pl.BlockSpec((UA_W, ATTN_BLOCK), lambda n: (0, jnp.maximum(n - 1, 0))),
                   pl.BlockSpec((1, UA_W), lambda n: (0, 0)),
                   pl.BlockSpec((1, LANES), lambda n: (0, 0))],
        out_shape=[jax.ShapeDtypeStruct((t, UA_W), BF16), jax.ShapeDtypeStruct((UA_W, t), BF16),
                   jax.ShapeDtypeStruct((1, UA_W), F32),
                   jax.ShapeDtypeStruct((1, LANES), F32)],
        scratch_shapes=[pltpu.VMEM((ATTN_BLOCK, ATTN_W), F32), pltpu.VMEM((ATTN_BLOCK, KV_W), F32),
                        pltpu.VMEM((ATTN_BLOCK, KV_W), F32), pltpu.VMEM((ATTN_BLOCK, ATTN_W), F32),
                        pltpu.VMEM((2 * ATTN_BLOCK, KV_W), F32), pltpu.VMEM((2 * ATTN_BLOCK, KV_W), F32)],
        sem=("arbitrary",), comm=comm)


def _tri_mats():
    r = lax.broadcasted_iota(jnp.int32, (HGRN_CHUNK, LANES), 0)
    c = lax.broadcasted_iota(jnp.int32, (HGRN_CHUNK, LANES), 1)
    lower = ((c <= r) & (c < HGRN_CHUNK)).astype(F32)
    upper = ((c >= r) & (c < HGRN_CHUNK)).astype(F32)
    return lower, upper


def _tri_apply(tri, g):
    pad = jnp.concatenate([g, jnp.zeros_like(g)], axis=0)
    return lax.dot_general(tri, pad, _NN, precision=lax.Precision.HIGHEST, preferred_element_type=F32)


def _sub_masks():
    s = lax.broadcasted_iota(jnp.int32, (HGRN_CHUNK, LANES), 0)
    tt = lax.broadcasted_iota(jnp.int32, (HGRN_CHUNK, LANES), 1)
    return [(tt >= HGRN_SUB * i) & (tt < HGRN_SUB * (i + 1)) & (s <= tt) for i in range(HGRN_CHUNK // HGRN_SUB)]


def _hgrn_gates(hq, hf, lb_ref, b_scr):
    lb = _sig(lb_ref[0:1, :] - lb_ref[1:2, :])
    q = hq * _sig(hq)
    sg = _sig(hf)
    f = lb + (1.0 - lb) * sg
    k = 1.0 - f
    lower, _ = _tri_mats()
    b = _tri_apply(lower, jnp.log(f))
    b_scr[...] = b
    nsub = HGRN_CHUNK // HGRN_SUB
    starts = [jnp.zeros((1, HG_W), F32)] + [b_scr[HGRN_SUB * i - 1:HGRN_SUB * i, :] for i in range(1, nsub)]
    pq = jnp.concatenate([jnp.broadcast_to(p, (HGRN_SUB, HG_W)) for p in starts], axis=0)
    b_last = b_scr[HGRN_CHUNK - 1:HGRN_CHUNK, :]
    e_q = jnp.exp(b - pq)
    e_k = [jnp.exp(jnp.minimum(p - b, EXP_CLAMP)) for p in starts]
    e_b = jnp.exp(b)
    e_bl = jnp.exp(b_last - b)
    e_last = jnp.exp(b_last)
    return q, sg, f, k, lb, e_q, e_k, e_b, e_bl, e_last


def _sub_masks_ts():
    tt = lax.broadcasted_iota(jnp.int32, (HGRN_CHUNK, LANES), 0)
    s = lax.broadcasted_iota(jnp.int32, (HGRN_CHUNK, LANES), 1)
    return [(tt >= HGRN_SUB * i) & (tt < HGRN_SUB * (i + 1)) & (s <= tt) for i in range(HGRN_CHUNK // HGRN_SUB)]


def _masked_sum(blocks, masks, axis):
    step = HGRN_CHUNK if axis == 0 else LANES
    acc = jnp.zeros((HGRN_CHUNK, LANES), F32)
    for i, msk in enumerate(masks):
        blk = blocks[step * i:step * (i + 1), :] if axis == 0 else blocks[:, step * i:step * (i + 1)]
        acc = acc + jnp.where(msk, blk, 0.0)
    return acc


def _store_transposed(out_t_ref, chunk_rows):
    width = chunk_rows[0].shape[1]
    if len(chunk_rows) == 1:
        groups = [jnp.concatenate([chunk_rows[0], jnp.zeros_like(chunk_rows[0])], axis=0)]
    else:
        groups = [jnp.concatenate(chunk_rows[g:g + 2], axis=0) for g in range(0, len(chunk_rows), 2)]
    for g, rows in enumerate(groups):
        for c in range(width // LANES):
            tile = rows[:, c * LANES:(c + 1) * LANES].T.astype(out_t_ref.dtype)
            if len(chunk_rows) == 1:
                out_t_ref[c * LANES:(c + 1) * LANES, :] = tile[:, 0:HGRN_CHUNK]
            else:
                out_t_ref[c * LANES:(c + 1) * LANES, g * LANES:(g + 1) * LANES] = tile


def _hgrn_chunk_inputs(j, hq_ref, hf_ref, hi_ref, hg_ref, lb_ref, b_scr):
    rows = slice(j * HGRN_CHUNK, (j + 1) * HGRN_CHUNK)
    hq, hf, v, hg = hq_ref[rows, :], hf_ref[rows, :], hi_ref[rows, :], hg_ref[rows, :]
    q, sg, f, k, lb, e_q, e_k, e_b, e_bl, e_last = _hgrn_gates(hq, hf, lb_ref, b_scr.at[j])
    return dict(rows=rows, hq=hq, v=v, hg=hg, q=q, sg=sg, f=f, k=k, lb=lb, e_q=e_q, e_k=e_k, e_b=e_b, e_bl=e_bl,
                e_last=e_last, qt=q * e_q, qb=q * e_b, kd=k * e_bl, khat=[k * e for e in e_k])


def _hgrn_fwd(uh, lb_raw, norm_g, name, comm=None):
    t = uh.shape[0]
    nc = t // HGRN_CHUNK
    cps = _pick(nc, (HGRN_CHUNKS_PER_STEP, 2, 1))
    rows_step = cps * HGRN_CHUNK

    def body(hq_ref, hf_ref, hi_ref, hg_ref, lb_ref, ng_ref, r_ref, r_t_ref, o_ref, st_out_ref, st_ref, b_scr):
        @pl.when(pl.program_id(0) == 0)
        def _():
            st_ref[...] = jnp.zeros_like(st_ref)

        masks = _sub_masks_ts()
        ng = ng_ref[...]
        zpad = jnp.zeros((HGRN_CHUNK, LANES), F32)
        heads = [slice(h * LANES, (h + 1) * LANES) for h in range(4)]
        chunks = [_hgrn_chunk_inputs(j, hq_ref, hf_ref, hi_ref, hg_ref, lb_ref, b_scr) for j in range(cps)]
        for ch in chunks:
            ch["scores"] = [_dot3(ch["qt"][:, sl],
                                  jnp.concatenate([x for kh in ch["khat"] for x in (kh[:, sl], zpad)], axis=0), _NT)
                            for sl in heads]
        for j, ch in enumerate(chunks):
            o_heads, y_heads = [], []
            for h, sl in enumerate(heads):
                a_ts = _masked_sum(ch["scores"][h], masks, 1)
                vh = ch["v"][:, sl].astype(BF16)
                v_pad = jnp.concatenate([vh, jnp.zeros_like(vh)], axis=0)
                o_intra = lax.dot_general(a_ts.astype(BF16), v_pad, _NN, preferred_element_type=F32)
                st = st_ref[h]
                st_out_ref[j, h] = st
                o_inter = _dot(ch["qb"][:, sl], st, _NT)
                st_ref[h] = st * ch["e_last"][:, sl] + _dot(vh, ch["kd"][:, sl], _TN)
                oh = o_intra + o_inter
                rs = lax.rsqrt(jnp.mean(oh * oh, axis=1, keepdims=True) + RMS_EPS)
                o_heads.append(oh)
                y_heads.append(oh * rs * ng)
            hg = ch["hg"]
            o_ref[ch["rows"], :] = jnp.concatenate(o_heads, axis=1)
            ch["r"] = jnp.concatenate(y_heads, axis=1) * (hg * _sig(hg))
            r_ref[ch["rows"], :] = ch["r"].astype(r_ref.dtype)
        _store_transposed(r_t_ref, [ch["r"] for ch in chunks])

    col = lambda j: pl.BlockSpec((rows_step, HG_W), lambda c: (c, j))
    return _call(
        body, name=name, grid=(nc // cps,), ins=[uh, uh, uh, uh, lb_raw, norm_g],
        in_specs=[col(0), col(1), col(2), col(3),
                  pl.BlockSpec((2, HG_W), lambda c: (0, 0)), pl.BlockSpec((1, LANES), lambda c: (0, 0))],
        out_specs=[pl.BlockSpec((rows_step, HG_W), lambda c: (c, 0)),
                   pl.BlockSpec((HG_W, rows_step), lambda c: (0, c)),
                   pl.BlockSpec((rows_step, HG_W), lambda c: (c, 0)),
                   pl.BlockSpec((cps, 4, LANES, LANES), lambda c: (c, 0, 0, 0))],
        out_shape=[jax.ShapeDtypeStruct((t, HG_W), BF16), jax.ShapeDtypeStruct((HG_W, t), BF16),
                   jax.ShapeDtypeStruct((t, HG_W), F32), jax.ShapeDtypeStruct((nc, 4, LANES, LANES), F32)],
        scratch_shapes=[pltpu.VMEM((4, LANES, LANES), F32), pltpu.VMEM((cps, HGRN_CHUNK, HG_W), F32)],
        sem=("arbitrary",), comm=comm)


def _hgrn_bwd(uh, o_pre, d_r, states, lb_raw, norm_g, name, comm=None):
    t = uh.shape[0]
    nc = t // HGRN_CHUNK
    cps = _pick(nc, (HGRN_CHUNKS_PER_STEP, 2, 1))
    ns = nc // cps
    rows_step = cps * HGRN_CHUNK
    nsub = HGRN_CHUNK // HGRN_SUB

    def body(hq_ref, hf_ref, hi_ref, hg_ref, o_ref, dr_ref, st_in_ref, lb_ref, ng_ref,
             duh_ref, duh_t_ref, dbias_ref, dng_ref, dlb_ref, dst_ref, b_scr, dlb_acc):
        i = pl.program_id(0)

        @pl.when(i == 0)
        def _():
            dst_ref[...] = jnp.zeros_like(dst_ref)
            dbias_ref[...] = jnp.zeros_like(dbias_ref)
            dng_ref[...] = jnp.zeros_like(dng_ref)
            dlb_acc[...] = jnp.zeros_like(dlb_acc)

        masks_st = _sub_masks()
        masks_ts = _sub_masks_ts()
        ng = ng_ref[...]
        zpad = jnp.zeros((HGRN_CHUNK, LANES), F32)
        _, upper = _tri_mats()
        heads = [slice(h * LANES, (h + 1) * LANES) for h in range(4)]
        row = lax.broadcasted_iota(jnp.int32, (HGRN_CHUNK, HG_W), 0)

        chunks = [_hgrn_chunk_inputs(j, hq_ref, hf_ref, hi_ref, hg_ref, lb_ref, b_scr) for j in range(cps)]
        dng = jnp.zeros((1, LANES), F32)
        for ch in chunks:
            o = o_ref[ch["rows"], :]
            dr = dr_ref[ch["rows"], :].astype(F32)
            hg = ch["hg"]
            sgg = _sig(hg)
            dy = dr * (hg * sgg)
            do_h, y_h = [], []
            for sl in heads:
                oh = o[:, sl]
                rs = lax.rsqrt(jnp.mean(oh * oh, axis=1, keepdims=True) + RMS_EPS)
                y_h.append(oh * rs * ng)
                dng = dng + jnp.sum(dy[:, sl] * oh * rs, axis=0, keepdims=True)
                w = dy[:, sl] * ng
                do_h.append(rs * (w - oh * (rs * rs) * jnp.mean(w * oh, axis=1, keepdims=True)))
            ch["do"] = do_h
            ch["dhg"] = dr * jnp.concatenate(y_h, axis=1) * _dsilu(hg, sgg)

        for ch in chunks:
            ch["kst"], ch["kpad"], ch["qt_pad"], ch["v_b"], ch["do_pad"] = [], [], [], [], []
            ch["ats"], ch["d_at"], ch["d_a"] = [], [], []
            for h, sl in enumerate(heads):
                kst = jnp.concatenate([kh[:, sl] for kh in ch["khat"]], axis=0)
                kpad = jnp.concatenate([x for kh in ch["khat"] for x in (kh[:, sl], zpad)], axis=0)
                qt_pad = jnp.concatenate([ch["qt"][:, sl], zpad], axis=0)
                vh = ch["v"][:, sl].astype(BF16)
                v_pad = jnp.concatenate([vh, jnp.zeros_like(vh)], axis=0)
                do_b = ch["do"][h].astype(BF16)
                do_pad = jnp.concatenate([do_b, jnp.zeros_like(do_b)], axis=0)
                ch["kst"].append(kst)
                ch["kpad"].append(kpad)
                ch["qt_pad"].append(qt_pad)
                ch["v_b"].append(vh)
                ch["do_pad"].append(do_pad)
                ch["ats"].append(_dot3(kst, qt_pad, _NT))
                ch["d_at"].append(lax.dot_general(vh, do_pad, _NT, preferred_element_type=F32))
                ch["d_a"].append(lax.dot_general(do_b, v_pad, _NT, preferred_element_type=F32))

        for ch in chunks:
            ch["d_kst"], ch["d_qt"], ch["dv"] = [], [], []
            for h in range(4):
                at = _masked_sum(ch["ats"][h], masks_st, 0)
                d_ats = jnp.concatenate([jnp.where(m, ch["d_at"][h], 0.0) for m in masks_st], axis=0)
                d_a_cat = jnp.concatenate([jnp.where(m, ch["d_a"][h], 0.0) for m in masks_ts], axis=1)
                ch["d_kst"].append(_dot3(d_ats, ch["qt_pad"][h], _NN))
                ch["d_qt"].append(_dot3(d_a_cat, ch["kpad"][h], _NN))
                ch["dv"].append(lax.dot_general(at.astype(BF16), ch["do_pad"][h], _NN, preferred_element_type=F32))

        for j in reversed(range(cps)):
            ch = chunks[j]
            q, k, sg, f, lb = ch["q"], ch["k"], ch["sg"], ch["f"], ch["lb"]
            dq_h, dk_h, dv_h, extra_h = [], [], [], []
            for h, sl in enumerate(heads):
                st_prev = st_in_ref[j, h]
                d_st = dst_ref[h]
                d_st_b = d_st.astype(BF16)
                do_b = ch["do_pad"][h][0:HGRN_CHUNK, :]
                kd, e_last = ch["kd"][:, sl], ch["e_last"][:, sl]
                dv = ch["dv"][h] + _dot(kd, d_st_b, _NT)
                d_qb = _dot(do_b, st_prev, _NN)
                d_kd = lax.dot_general(ch["v_b"][h], d_st_b, _NN, preferred_element_type=F32)
                extra_h.append(jnp.sum(st_prev * d_st, axis=0, keepdims=True) * e_last
                               + jnp.sum(kd * d_kd, axis=0, keepdims=True))
                dst_ref[h] = d_st * e_last + _dot(do_b, ch["qb"][:, sl], _TN)
                dq_h.append(ch["d_qt"][h] * ch["e_q"][:, sl] + d_qb * ch["e_b"][:, sl])
                dkk = d_kd * ch["e_bl"][:, sl]
                for s_ in range(nsub):
                    dkk = dkk + ch["d_kst"][h][HGRN_CHUNK * s_:HGRN_CHUNK * (s_ + 1), :] * ch["e_k"][s_][:, sl]
                dk_h.append(dkk)
                dv_h.append(dv)
            dq = jnp.concatenate(dq_h, axis=1)
            dk = jnp.concatenate(dk_h, axis=1)
            dv = jnp.concatenate(dv_h, axis=1)
            extra = jnp.concatenate(extra_h, axis=1)
            db = q * dq - k * dk + jnp.where(row == HGRN_CHUNK - 1, extra, 0.0)
            dg = _tri_apply(upper, db)
            df = dg / f - dk
            dhf = df * (1.0 - lb) * sg * (1.0 - sg)
            dhq = dq * _dsilu(ch["hq"], _sig(ch["hq"]))
            full = jnp.concatenate([dhq, dhf, dv, ch["dhg"]], axis=1)
            duh_ref[ch["rows"], :] = full.astype(duh_ref.dtype)
            ch["full"] = full
            dbias_ref[...] += jnp.sum(full, axis=0, keepdims=True)
            dlb_acc[...] += jnp.sum(df * (1.0 - sg), axis=0, keepdims=True)
        dng_ref[...] += dng
        _store_transposed(duh_t_ref, [ch["full"] for ch in chunks])

        @pl.when(i == ns - 1)
        def _():
            lb = chunks[0]["lb"]
            d_a0 = dlb_acc[...] * lb * (1.0 - lb)
            r8 = lax.broadcasted_iota(jnp.int32, (8, HG_W), 0)
            dlb_ref[...] = jnp.where(r8 == 0, d_a0, jnp.where(r8 == 1, -d_a0, 0.0))

    col = lambda j: pl.BlockSpec((rows_step, HG_W), lambda i: (ns - 1 - i, j))
    return _call(
        body, name=name, grid=(ns,), ins=[uh, uh, uh, uh, o_pre, d_r, states, lb_raw, norm_g],
        in_specs=[col(0), col(1), col(2), col(3), col(0), col(0),
                  pl.BlockSpec((cps, 4, LANES, LANES), lambda i: (ns - 1 - i, 0, 0, 0)),
                  pl.BlockSpec((2, HG_W), lambda i: (0, 0)), pl.BlockSpec((1, LANES), lambda i: (0, 0))],
        out_specs=[pl.BlockSpec((rows_step, UH_W), lambda i: (ns - 1 - i, 0)),
                   pl.BlockSpec((UH_W, rows_step), lambda i: (0, ns - 1 - i)),
                   pl.BlockSpec((1, UH_W), lambda i: (0, 0)),
                   pl.BlockSpec((1, LANES), lambda i: (0, 0)),
                   pl.BlockSpec((8, HG_W), lambda i: (0, 0))],
        out_shape=[jax.ShapeDtypeStruct((t, UH_W), BF16), jax.ShapeDtypeStruct((UH_W, t), BF16),
                   jax.ShapeDtypeStruct((1, UH_W), F32),
                   jax.ShapeDtypeStruct((1, LANES), F32), jax.ShapeDtypeStruct((8, HG_W), F32)],
        scratch_shapes=[pltpu.VMEM((4, LANES, LANES), F32), pltpu.VMEM((cps, HGRN_CHUNK, HG_W), F32),
                        pltpu.VMEM((1, HG_W), F32)],
        sem=("arbitrary",), comm=comm)


def _ln_bwd_math(dy, xhat, rstd, g):
    dxh = dy * g
    return rstd * (dxh - jnp.mean(dxh, axis=1, keepdims=True)
                   - xhat * jnp.mean(dxh * xhat, axis=1, keepdims=True))


def _mm_rows(a, b, extras, *, name, epilogue, out_shape, out_specs, tb=False, tm=512, tk=1408):
    m, kdim = a.shape
    n = b.shape[0] if tb else b.shape[1]
    tm = _pick(m, (tm, 256, 128))
    tk = _pick(kdim, (tk, 1408, 1024, 768, 512, 256, 128))
    nk = kdim // tk
    b_spec = pl.BlockSpec((n, tk), lambda i, k: (0, k)) if tb else pl.BlockSpec((tk, n), lambda i, k: (k, 0))
    dims = _NT if tb else _NN
    n_ex, n_out = len(extras), len(out_shape)

    def body(*refs):
        a_ref, b_ref = refs[0], refs[1]
        ex_refs = refs[2:2 + n_ex]
        o_refs = refs[2 + n_ex:2 + n_ex + n_out]
        acc_ref = refs[-1]
        i, k = pl.program_id(0), pl.program_id(1)

        @pl.when(k == 0)
        def _():
            acc_ref[...] = jnp.zeros_like(acc_ref)

        acc_ref[...] += _dot(a_ref[...], b_ref[...], dims)

        @pl.when(k == nk - 1)
        def _():
            epilogue(acc_ref[...], ex_refs, o_refs, i == 0)

    return pl.pallas_call(
        body, name=name, grid=(m // tm, nk),
        in_specs=[pl.BlockSpec((tm, tk), lambda i, k: (i, k)), b_spec] + [sp for _, sp in extras],
        out_specs=list(out_specs), out_shape=list(out_shape),
        scratch_shapes=[pltpu.VMEM((tm, n), F32)],
        compiler_params=_cp("arbitrary", "arbitrary"),
    )(a, b, *[arr for arr, _ in extras])


def _rows_specs(tm, d):
    row = pl.BlockSpec((tm, d), lambda i, k: (i, 0))
    vec = pl.BlockSpec((1, d), lambda i, k: (0, 0))
    col = pl.BlockSpec((tm, 1), lambda i, k: (i, 0))
    return row, vec, col


def _mm_ln_fwd(a, b, addend, g, beta, name, tm=512):
    t, d = addend.shape
    tm = _pick(t, (tm, 256, 128))
    row, vec, col = _rows_specs(tm, d)

    def epilogue(acc, ex, outs, first):
        z = acc + ex[0][...]
        mu = jnp.mean(z, axis=1, keepdims=True)
        zc = z - mu
        rstd = lax.rsqrt(jnp.mean(zc * zc, axis=1, keepdims=True) + LN_EPS)
        xhat = zc * rstd
        h = xhat * ex[1][...] + ex[2][...]
        outs[0][...] = h
        outs[1][...] = h.astype(BF16)
        outs[2][...] = xhat
        outs[3][...] = rstd

    return _mm_rows(a, b, [(addend, row), (g, vec), (beta, vec)], name=name, epilogue=epilogue, tm=tm,
                    out_shape=[jax.ShapeDtypeStruct((t, d), F32), jax.ShapeDtypeStruct((t, d), BF16),
                               jax.ShapeDtypeStruct((t, d), F32), jax.ShapeDtypeStruct((t, 1), F32)],
                    out_specs=[row, row, row, col])


def _mm_ln_loss_bwd(a, b, addend, target, g, beta, name, tm=1024):
    t, d = addend.shape
    tm = _pick(t, (tm, 256, 128))
    row, vec, _ = _rows_specs(tm, d)

    def epilogue(acc, ex, outs, first):
        dz_ref, dg_ref, db_ref, loss_ref = outs

        @pl.when(first)
        def _():
            dg_ref[...] = jnp.zeros_like(dg_ref)
            db_ref[...] = jnp.zeros_like(db_ref)
            loss_ref[...] = jnp.zeros_like(loss_ref)

        z = acc + ALPHA * ex[0][...]
        gg = ex[2][...]
        mu = jnp.mean(z, axis=1, keepdims=True)
        zc = z - mu
        rstd = lax.rsqrt(jnp.mean(zc * zc, axis=1, keepdims=True) + LN_EPS)
        xhat = zc * rstd
        err = xhat * gg + ex[3][...] - ex[1][...]
        loss_ref[...] += 0.5 * jnp.sum(jnp.mean(err * err, axis=1, keepdims=True))
        dy = err * (1.0 / d)
        dz_ref[...] = _ln_bwd_math(dy, xhat, rstd, gg)
        dg_ref[...] += jnp.sum(dy * xhat, axis=0, keepdims=True)
        db_ref[...] += jnp.sum(dy, axis=0, keepdims=True)

    return _mm_rows(a, b, [(addend, row), (target, row), (g, vec), (beta, vec)], name=name, epilogue=epilogue,
                    tm=tm,
                    out_shape=[jax.ShapeDtypeStruct((t, d), F32), jax.ShapeDtypeStruct((1, d), F32),
                               jax.ShapeDtypeStruct((1, d), F32), jax.ShapeDtypeStruct((1, LANES), F32)],
                    out_specs=[row, vec, vec, pl.BlockSpec((1, LANES), lambda i, k: (0, 0))])


def _mm_ln_bwd(a, b, addend, xhat, rstd, g, name, tm=1024):
    t, d = addend.shape
    tm = _pick(t, (tm, 256, 128))
    row, vec, col = _rows_specs(tm, d)

    def epilogue(acc, ex, outs, first):
        dz_ref, dg_ref, db_ref = outs

        @pl.when(first)
        def _():
            dg_ref[...] = jnp.zeros_like(dg_ref)
            db_ref[...] = jnp.zeros_like(db_ref)

        dy = acc + ALPHA * ex[0][...]
        xh = ex[1][...]
        dz_ref[...] = _ln_bwd_math(dy, xh, ex[2][...], ex[3][...])
        dg_ref[...] += jnp.sum(dy * xh, axis=0, keepdims=True)
        db_ref[...] += jnp.sum(dy, axis=0, keepdims=True)

    return _mm_rows(a, b, [(addend, row), (xhat, row), (rstd, col), (g, vec)], name=name, epilogue=epilogue, tm=tm,
                    out_shape=[jax.ShapeDtypeStruct((t, d), F32), jax.ShapeDtypeStruct((1, d), F32),
                               jax.ShapeDtypeStruct((1, d), F32)],
                    out_specs=[row, vec, vec])


CONV_TILE = 128
CONV_RB = 32
HALO = 8


def _sum8(x):
    acc = x[0:8]
    for r in range(8, x.shape[0], 8):
        acc = acc + x[r:r + 8]
    return acc


def _conv_fwd(u2, conv_w, conv_b, name):
    t = u2.shape[0]
    tr = _pick(t, (CONV_TILE,))
    hb = tr // HALO
    rb = CONV_RB

    def body(gp_ref, val_ref, prev_ref, w_ref, b_ref, out_ref, ext):
        i = pl.program_id(0)
        ext[0:HALO, :] = jnp.where(i == 0, 0.0, prev_ref[...])
        ext[HALO:, :] = gp_ref[...]
        for c in range(D_FF // LANES):
            ln = slice(c * LANES, (c + 1) * LANES)
            w0, w1, w2, bb = w_ref[0:1, ln], w_ref[1:2, ln], w_ref[2:3, ln], b_ref[:, ln]
            for r0 in range(0, tr, rb):
                gate = (ext[r0 + HALO - 2:r0 + HALO - 2 + rb, ln] * w0 + ext[r0 + HALO - 1:r0 + HALO - 1 + rb, ln] * w1
                        + ext[r0 + HALO:r0 + HALO + rb, ln] * w2 + bb)
                out_ref[r0:r0 + rb, ln] = (gate * _sig(gate) * val_ref[r0:r0 + rb, ln]).astype(out_ref.dtype)

    return pl.pallas_call(
        body, name=name, grid=(t // tr,),
        in_specs=[pl.BlockSpec((tr, D_FF), lambda i: (i, 0)), pl.BlockSpec((tr, D_FF), lambda i: (i, 1)),
                  pl.BlockSpec((HALO, D_FF), lambda i: (jnp.maximum(i * hb - 1, 0), 0)),
                  pl.BlockSpec((3, D_FF), lambda i: (0, 0)), pl.BlockSpec((1, D_FF), lambda i: (0, 0))],
        out_specs=pl.BlockSpec((tr, D_FF), lambda i: (i, 0)),
        out_shape=jax.ShapeDtypeStruct((t, D_FF), BF16),
        scratch_shapes=[pltpu.VMEM((tr + HALO, D_FF), F32)],
        compiler_params=_cp("parallel"),
    )(u2, u2, u2, conv_w, conv_b)


def _conv_bwd(d_hmid, u2, conv_w, conv_b, name, comm=None):
    t = u2.shape[0]
    tr = _pick(t, (CONV_TILE,))
    hb = tr // HALO
    last = t // HALO - 1
    rb = CONV_RB
    re = rb + HALO

    def body(gp_ref, gp_prev_ref, gp_next_ref, val_ref, val_next_ref, dh_ref, dh_next_ref, w_ref, b_ref,
             du_ref, dw_ref, dcb_ref, ext, dg_s):
        i = pl.program_id(0)

        @pl.when(i == 0)
        def _():
            dw_ref[...] = jnp.zeros_like(dw_ref)
            dcb_ref[...] = jnp.zeros_like(dcb_ref)

        ext[0:HALO, :] = jnp.where(i == 0, 0.0, gp_prev_ref[...])
        ext[HALO:HALO + tr, :] = gp_ref[...]
        ext[HALO + tr:, :] = gp_next_ref[...]
        next_in_seq = (i + 1) * tr < t
        for c in range(D_FF // LANES):
            ln = slice(c * LANES, (c + 1) * LANES)
            w0, w1, w2, bb = w_ref[0:1, ln], w_ref[1:2, ln], w_ref[2:3, ln], b_ref[:, ln]
            acc_b = jnp.zeros((8, LANES), F32)
            acc_w = [jnp.zeros((8, LANES), F32) for _ in range(3)]
            for r0 in range(0, tr, rb):
                g_m2 = ext[r0 + HALO - 2:r0 + HALO - 2 + re, ln]
                g_m1 = ext[r0 + HALO - 1:r0 + HALO - 1 + re, ln]
                g_0 = ext[r0 + HALO:r0 + HALO + re, ln]
                gate = g_m2 * w0 + g_m1 * w1 + g_0 * w2 + bb
                sg = _sig(gate)
                if r0 + re <= tr:
                    val = val_ref[r0:r0 + re, ln]
                    dh = dh_ref[r0:r0 + re, ln]
                else:
                    val = jnp.concatenate([val_ref[r0:r0 + rb, ln], val_next_ref[:, ln]], axis=0)
                    dh = jnp.concatenate([dh_ref[r0:r0 + rb, ln],
                                          jnp.where(next_in_seq, dh_next_ref[:, ln], 0.0)], axis=0)
                dgate = dh * val * _dsilu(gate, sg)
                dg_s[:, ln] = dgate
                dg0 = dgate[0:rb]
                d_gp = dg_s[2:2 + rb, ln] * w0 + dg_s[1:1 + rb, ln] * w1 + dg0 * w2
                d_val = dh[0:rb] * (gate[0:rb] * sg[0:rb])
                du_ref[r0:r0 + rb, ln] = d_gp.astype(du_ref.dtype)
                du_ref[r0:r0 + rb, D_FF + c * LANES:D_FF + (c + 1) * LANES] = d_val.astype(du_ref.dtype)
                acc_b = acc_b + _sum8(dg0)
                acc_w[0] = acc_w[0] + _sum8(dg0 * g_m2[0:rb])
                acc_w[1] = acc_w[1] + _sum8(dg0 * g_m1[0:rb])
                acc_w[2] = acc_w[2] + _sum8(dg0 * g_0[0:rb])
            dcb_ref[:, ln] += jnp.sum(acc_b, axis=0, keepdims=True)
            for j in range(3):
                dw_ref[j:j + 1, ln] += jnp.sum(acc_w[j], axis=0, keepdims=True)

    cur = lambda col: pl.BlockSpec((tr, D_FF), lambda i: (i, col))
    nxt = lambda col: pl.BlockSpec((HALO, D_FF), lambda i: (jnp.minimum((i + 1) * hb, last), col))
    return _call(
        body, name=name, grid=(t // tr,), ins=[u2, u2, u2, u2, u2, d_hmid, d_hmid, conv_w, conv_b],
        in_specs=[cur(0), pl.BlockSpec((HALO, D_FF), lambda i: (jnp.maximum(i * hb - 1, 0), 0)), nxt(0),
                  cur(1), nxt(1), cur(0), nxt(0),
                  pl.BlockSpec((3, D_FF), lambda i: (0, 0)), pl.BlockSpec((1, D_FF), lambda i: (0, 0))],
        out_specs=[pl.BlockSpec((tr, 2 * D_FF), lambda i: (i, 0)),
                   pl.BlockSpec((8, D_FF), lambda i: (0, 0)), pl.BlockSpec((1, D_FF), lambda i: (0, 0))],
        out_shape=[jax.ShapeDtypeStruct((t, 2 * D_FF), BF16), jax.ShapeDtypeStruct((8, D_FF), F32),
                   jax.ShapeDtypeStruct((1, D_FF), F32)],
        scratch_shapes=[pltpu.VMEM((tr + 2 * HALO, D_FF), F32), pltpu.VMEM((re, D_FF), F32)],
        sem=("arbitrary",), comm=comm)


def _adamw(w, g, m, v, name):
    rows, cols = w.shape
    tr = _pick(rows, (256, 128, 64, 32, 16, 8))

    def body(w_ref, g_ref, m_ref, v_ref, d_ref, nm_ref, nv_ref):
        d_ref[...], nm_ref[...], nv_ref[...] = _adamw_math(w_ref[...], g_ref[...], m_ref[...], v_ref[...])

    spec = pl.BlockSpec((tr, cols), lambda i: (i, 0))
    shp = jax.ShapeDtypeStruct((rows, cols), F32)
    return pl.pallas_call(
        body, name=name, grid=(rows // tr,),
        in_specs=[spec, spec, spec, spec], out_specs=[spec, spec, spec], out_shape=[shp, shp, shp],
        compiler_params=_cp("parallel"),
    )(w, g, m, v)


def _pad_rows(a, rows):
    return jnp.pad(a, ((0, rows - a.shape[0]), (0, 0)))


SMALL_LAYOUT = (("ln1_g", 1024), ("ln1_b", 1024), ("b_in", 2816), ("sinks", 8), ("hgrn_lb", 1024),
                ("hgrn_norm_g", 128), ("ln2_g", 1024), ("ln2_b", 1024), ("conv_b", 2816), ("loss", 1))
SMALL_SHAPES = {"ln1_g": (1, 1024), "ln1_b": (1, 1024), "b_in": (1, 2816), "sinks": (1, 8), "hgrn_lb": (2, 512),
                "hgrn_norm_g": (1, 128), "ln2_g": (1, 1024), "ln2_b": (1, 1024), "conv_b": (1, 2816),
                "loss": (1,)}


def _pack_small(parts):
    rows = []
    for name, size in SMALL_LAYOUT:
        flat = parts[name].reshape(-1).astype(F32)
        padded = -(-size // LANES) * LANES
        rows.append(jnp.pad(flat, (0, padded - size)).reshape(-1, LANES))
    return _pad_rows(jnp.concatenate(rows, axis=0), SMALL_ROWS)


def _unpack_small(pack):
    out, r = {}, 0
    for name, size in SMALL_LAYOUT:
        nrows = -(-size // LANES)
        out[name] = pack[r:r + nrows].reshape(-1)[:size].reshape(SMALL_SHAPES[name])
        r += nrows
    return out


def _own(full, rows):
    return lax.dynamic_slice_in_dim(full, _me() * rows, rows, axis=0)


def kernel(x, positions, ln1_g, ln1_b, w_in, b_in, sinks, hgrn_lb, hgrn_norm_g, w_o, ln2_g, ln2_b, w_up, conv_w, conv_b, w_down, loss_target, m_ln1_g, m_ln1_b, m_w_in, m_b_in, m_sinks, m_hgrn_lb, m_hgrn_norm_g, m_w_o, m_ln2_g, m_ln2_b, m_w_up, m_conv_w, m_conv_b, m_w_down, v_ln1_g, v_ln1_b, v_w_in, v_b_in, v_sinks, v_hgrn_lb, v_hgrn_norm_g, v_w_o, v_ln2_g, v_ln2_b, v_w_up, v_conv_w, v_conv_b, v_w_down):
    t = x.shape[1]
    x2 = x[0]
    target = loss_target[0]
    pos_col = positions.reshape(t, 1)

    w_in_t_s = w_in[0].T.astype(BF16)
    w_up_t_s = w_up[0].T.astype(BF16)
    w_o_s = w_o[0].astype(BF16)
    w_down_s = w_down[0].astype(BF16)
    (ctab, stab, xb), (w_in_t_g, cw_g) = _prep(
        pos_col, x2, "prep_ag_w_in", _Comm([{"kind": "gather", "arr": w_in_t_s}, {"kind": "gather", "arr": _pad_rows(conv_w[0], 8)}]))
    w_in_t = w_in_t_g.reshape(D_FF, D_MODEL)
    w_a_t, w_h_t = w_in_t[:UA_W], w_in_t[UA_W:]
    conv_w_f = cw_g[:, 0:3].transpose(1, 0, 2).reshape(3, D_FF)

    ua = _mm(xb, w_a_t, tb=True, bias=b_in[:, :UA_W], name="fwd_in_attn")
    uh = _mm(xb, w_h_t, tb=True, bias=b_in[:, UA_W:], name="fwd_in_hgrn")
    half_up = SHARD_UP // 2
    (a_out, a_out_t), (w_o_g, w_up_half) = _attn_fwd(
        ua, ctab, stab, sinks, "attn_fwd",
        comm=_Comm([{"kind": "gather", "arr": w_o_s},
                    {"kind": "gather", "arr": w_up_t_s, "rows": (0, half_up), "dst_rows": SHARD_UP}]))
    (r_out, r_out_t, o_pre, states), (w_up_t_g,) = _hgrn_fwd(
        uh, hgrn_lb, hgrn_norm_g, "hgrn_fwd",
        comm=_Comm([{"kind": "gather", "arr": w_up_t_s, "rows": (half_up, half_up), "dst_rows": SHARD_UP,
                     "dst_first": half_up, "into": w_up_half}]))
    w_o_f = w_o_g.reshape(D_MODEL, D_MODEL)
    w_up_t = w_up_t_g.reshape(2 * D_FF, D_MODEL)
    z1 = _mm(a_out, w_o_f[:ATTN_W], addend=x2, addend_scale=ALPHA, name="fwd_o_attn")
    h1, h1b, xhat1, rstd1 = _mm_ln_fwd(r_out, w_o_f[ATTN_W:], z1, ln1_g, ln1_b, "fwd_o_hgrn_ln1")
    u2, (w_down_g,) = _mm(h1b, w_up_t, tb=True, tn=1408, name="fwd_up", comm=_Comm([{"kind": "gather", "arr": w_down_s}]))
    w_down_f = w_down_g.reshape(D_FF, D_MODEL)
    hmid = _conv_fwd(u2, conv_w_f, conv_b, "conv_fwd")
    dz2, d_ln2_g, d_ln2_b, loss_part = _mm_ln_loss_bwd(hmid, w_down_f, h1, target, ln2_g, ln2_b,
                                                       "fwd_down_ln2_loss")

    d_hmid = _mm(dz2, w_down_f, tb=True, tn=1408, name="bwd_down_dx")
    d_w_down, d_w_down_b = _mm(hmid, dz2, ta=True, out_dtype2=BF16, tm=1408, tk=512, name="bwd_down_dw")
    (d_u2, d_conv_w8, d_conv_b), (recv_down,) = _conv_bwd(
        d_hmid, u2, conv_w_f, conv_b, "conv_bwd",
        comm=_Comm([{"kind": "exchange", "arr": d_w_down_b.reshape(N_DEV, SHARD_DOWN, D_MODEL)}]))
    dz1, d_ln1_g, d_ln1_b = _mm_ln_bwd(d_u2, w_up_t, dz2, xhat1, rstd1, ln1_g, "bwd_up_dx_ln1")
    d_w_up_t, d_w_up_t_b = _mm(d_u2, h1b, ta=True, out_dtype2=BF16, tm=1408, tk=512, name="bwd_up_dw")
    d_a = _mm(dz1, w_o_f[:ATTN_W], tb=True, name="bwd_o_dx_attn")
    d_r = _mm(dz1, w_o_f[ATTN_W:], tb=True, name="bwd_o_dx_hgrn")
    d_w_o_a, d_w_o_a_b = _mm(a_out_t, dz1, out_dtype2=BF16, name="bwd_o_dw_attn")
    d_w_o_r, d_w_o_r_b = _mm(r_out_t, dz1, out_dtype2=BF16, name="bwd_o_dw_hgrn")
    d_w_o = jnp.concatenate([d_w_o_a, d_w_o_r], axis=0)
    d_w_o_b = jnp.concatenate([d_w_o_a_b, d_w_o_r_b], axis=0)
    d_w_up_x = d_w_up_t_b.reshape(N_DEV, SHARD_UP, D_MODEL)
    half = SHARD_UP // 2
    d_cw_x = d_conv_w8.reshape(8, N_DEV, SHARD_IN).transpose(1, 0, 2)
    (d_ua, d_ua_t, d_bias_a, d_sinks), (recv_up_half, recv_o, recv_cw) = _attn_bwd(
        ua, d_a, ctab, stab, sinks, "attn_bwd",
        comm=_Comm([{"kind": "exchange", "arr": d_w_up_x, "rows": (0, half), "dst_rows": SHARD_UP},
                    {"kind": "exchange", "arr": d_w_o_b.reshape(N_DEV, SHARD_O, D_MODEL)},
                    {"kind": "exchange", "arr": d_cw_x}]))
    (d_uh, d_uh_t, d_bias_h, d_norm_g, d_lb8), (recv_up,) = _hgrn_bwd(
        uh, o_pre, d_r, states, hgrn_lb, hgrn_norm_g, "hgrn_bwd",
        comm=_Comm([{"kind": "exchange", "arr": d_w_up_x, "rows": (half, half), "dst_rows": SHARD_UP,
                     "dst_first": half, "into": recv_up_half}]))
    d_w_a_t, d_w_a_t_b = _mm(d_ua_t, xb, out_dtype2=BF16, name="bwd_in_dw_attn")
    d_w_h_t, d_w_h_t_b = _mm(d_uh_t, xb, out_dtype2=BF16, name="bwd_in_dw_hgrn")
    d_w_in_t = jnp.concatenate([d_w_a_t, d_w_h_t], axis=0)
    d_w_in_t_b = jnp.concatenate([d_w_a_t_b, d_w_h_t_b], axis=0)
    small_local = _pack_small({
        "ln1_g": d_ln1_g, "ln1_b": d_ln1_b, "b_in": jnp.concatenate([d_bias_a, d_bias_h], axis=1),
        "sinks": d_sinks[:, :8], "hgrn_lb": d_lb8[0:2], "hgrn_norm_g": d_norm_g, "ln2_g": d_ln2_g,
        "ln2_b": d_ln2_b, "conv_b": d_conv_b, "loss": loss_part[:, :1]})
    dx, (recv_in, small_g) = _mm(d_uh, w_h_t, addend=dz1, addend_scale=ALPHA, name="bwd_in_dx_hgrn",
                                 comm=_Comm([{"kind": "exchange", "arr": d_w_in_t_b.reshape(N_DEV, SHARD_IN, D_MODEL)},
                                             {"kind": "gather", "arr": small_local}]))
    dx = _mm(d_ua, w_a_t, addend=dx, tk=768, name="bwd_in_dx_attn")

    res_in = [r.T for r in _sum_shards_adamw([recv_in], _own(d_w_in_t, SHARD_IN), w_in[0].T, m_w_in[0].T,
                                             v_w_in[0].T, "adamw_w_in")]
    res_up = [r.T for r in _sum_shards_adamw([recv_up], _own(d_w_up_t, SHARD_UP), w_up[0].T,
                                             m_w_up[0].T, v_w_up[0].T, "adamw_w_up")]
    res_o = _sum_shards_adamw([recv_o], _own(d_w_o, SHARD_O), w_o[0], m_w_o[0], v_w_o[0], "adamw_w_o")
    res_down = _sum_shards_adamw([recv_down], _own(d_w_down, SHARD_DOWN), w_down[0], m_w_down[0], v_w_down[0],
                                 "adamw_w_down")
    g_cw = _sum_slots(recv_cw, "sum_conv_w")
    cw8 = lambda a: _pad_rows(a, 8)
    res_cw = (g_cw,) + tuple(_adamw(cw8(conv_w[0]), g_cw, cw8(m_conv_w[0]), cw8(v_conv_w[0]), "adamw_conv_w"))
    big = {"w_in": [r[None] for r in res_in], "w_up": [r[None] for r in res_up],
           "w_o": [r[None] for r in res_o], "w_down": [r[None] for r in res_down],
           "conv_w": [r[None, 0:3] for r in res_cw]}

    small_sum = _sum_slots(small_g, "ar_small_sum")
    gs = _unpack_small(small_sum)
    loss = gs["loss"][0]
    zero1 = jnp.zeros((1,), F32)
    w_small = _pack_small({"ln1_g": ln1_g, "ln1_b": ln1_b, "b_in": b_in, "sinks": sinks, "hgrn_lb": hgrn_lb,
                           "hgrn_norm_g": hgrn_norm_g, "ln2_g": ln2_g, "ln2_b": ln2_b, "conv_b": conv_b,
                           "loss": zero1})
    m_small = _pack_small({"ln1_g": m_ln1_g, "ln1_b": m_ln1_b, "b_in": m_b_in, "sinks": m_sinks,
                           "hgrn_lb": m_hgrn_lb, "hgrn_norm_g": m_hgrn_norm_g, "ln2_g": m_ln2_g,
                           "ln2_b": m_ln2_b, "conv_b": m_conv_b, "loss": zero1})
    v_small = _pack_small({"ln1_g": v_ln1_g, "ln1_b": v_ln1_b, "b_in": v_b_in, "sinks": v_sinks,
                           "hgrn_lb": v_hgrn_lb, "hgrn_norm_g": v_hgrn_norm_g, "ln2_g": v_ln2_g,
                           "ln2_b": v_ln2_b, "conv_b": v_conv_b, "loss": zero1})
    small = [gs] + [_unpack_small(p) for p in _adamw(w_small, small_sum, m_small, v_small, "adamw_small")]

    order = ["ln1_g", "ln1_b", "w_in", "b_in", "sinks", "hgrn_lb", "hgrn_norm_g", "w_o", "ln2_g", "ln2_b",
             "w_up", "conv_w", "conv_b", "w_down"]

    def pick(idx):
        return [big[n][idx] if n in big else small[idx][n] for n in order]

    return (loss, dx[None], *pick(0), *pick(1), *pick(2), *pick(3))
```

```python
import functools

import jax
import jax.numpy as jnp
import numpy as np
from jax import lax
from jax.experimental import pallas as pl
from jax.experimental.pallas import tpu as pltpu

F32 = jnp.float32
BF16 = jnp.bfloat16

N_DEV = 8
D_MODEL = 1024
D_FF = 2816
ATTN_W = 512
KV_W = 128
UA_W = ATTN_W + 2 * KV_W
UH_W = 2048
HG_W = 512
ATTN_BLOCK = 128
HGRN_CHUNK = 64
HGRN_SUB = 16
HGRN_CHUNKS_PER_STEP = 4
EXP_CLAMP = 85.0
NEG_BIG = -1e30
LN_EPS = 1e-5
RMS_EPS = 1e-6
ALPHA = 2.0 ** 0.25
ATTN_SCALE = 0.125
ROPE_THETA = 500000.0

ADAM_LR = 0.001
ADAM_B1 = 0.9
ADAM_B2 = 0.999
ADAM_EPS = 1e-08
ADAM_WD = 0.01
ADAM_STEP = 10

LANES = 128
VMEM_LIMIT_BYTES = 56 * 1024 * 1024

SHARD_IN = D_FF // N_DEV
SHARD_UP = 2 * D_FF // N_DEV
SHARD_O = D_MODEL // N_DEV
SHARD_DOWN = D_FF // N_DEV
SMALL_ROWS = 88

_MESH = pl.DeviceIdType.MESH
_NT = (((1,), (1,)), ((), ()))
_NN = (((1,), (0,)), ((), ()))
_TN = (((0,), (0,)), ((), ()))


def _cp(*sem):
    if sem:
        return pltpu.CompilerParams(dimension_semantics=sem, vmem_limit_bytes=VMEM_LIMIT_BYTES)
    return pltpu.CompilerParams(vmem_limit_bytes=VMEM_LIMIT_BYTES)


def _sig(x):
    return 0.5 * jnp.tanh(0.5 * x) + 0.5


def _dsilu(x, s):
    return s * (1.0 + x * (1.0 - s))


def _dot(a, b, dims):
    return lax.dot_general(a.astype(BF16), b.astype(BF16), dims, preferred_element_type=F32)


def _split(a):
    hi = a.astype(BF16)
    return hi, (a - hi.astype(F32)).astype(BF16)


def _dot3(a, b, dims):
    ah, al = _split(a)
    bh, bl = _split(b)
    d = functools.partial(lax.dot_general, dimension_numbers=dims, preferred_element_type=F32)
    return d(ah, bh) + (d(ah, bl) + d(al, bh))


def _pick(n, pref):
    for t in pref:
        if t <= n and n % t == 0:
            return t
    return n


def _my_coords():
    return lax.axis_index("x"), lax.axis_index("y"), lax.axis_index("c")


def _peer(k):
    x, y, c = _my_coords()
    return (1 - x if k & 4 else x, 1 - y if k & 2 else y, 1 - c if k & 1 else c)


def _me():
    x, y, c = _my_coords()
    return 4 * x + 2 * y + c


class _Comm:
    def __init__(self, items):
        self.items = []
        for it in items:
            arr = it["arr"]
            full = arr.shape[0] if it["kind"] == "gather" else arr.shape[1]
            first, count = it.get("rows", (0, full))
            self.items.append(dict(kind=it["kind"], arr=arr, first=first, count=count,
                                   dst_rows=it.get("dst_rows", count), dst_first=it.get("dst_first", 0),
                                   into=it.get("into")))
        self.n = len(self.items)
        self.arrays = [it["arr"] for it in self.items]
        self.intos = [(a, it["into"]) for a, it in enumerate(self.items) if it["into"] is not None]

    def out_shapes(self):
        return [jax.ShapeDtypeStruct((N_DEV, it["dst_rows"], it["arr"].shape[-1]), it["arr"].dtype)
                for it in self.items]

    def specs(self, n=None):
        return [pl.BlockSpec(memory_space=pl.ANY)] * (self.n if n is None else n)

    def scratch(self):
        return [pltpu.SemaphoreType.DMA(((N_DEV - 1) * self.n,)), pltpu.SemaphoreType.DMA(((N_DEV - 1) * self.n,)),
                pltpu.SemaphoreType.DMA((self.n,))]

    def _src(self, a, ref, dev):
        it = self.items[a]
        blk = ref if it["kind"] == "gather" else ref.at[dev]
        return blk.at[pl.ds(it["first"], it["count"])]

    def _dst(self, a, ref, slot):
        it = self.items[a]
        return ref.at[slot].at[pl.ds(it["dst_first"], it["count"])]

    def _copy(self, a, k, src, dst, sems, me, slot):
        other = jnp.bitwise_xor(me, k)
        idx = a * (N_DEV - 1) + k - 1
        return pltpu.make_async_remote_copy(
            src_ref=self._src(a, src, other), dst_ref=self._dst(a, dst, me if slot == "mine" else other),
            send_sem=sems[0].at[idx], recv_sem=sems[1].at[idx], device_id=_peer(k), device_id_type=_MESH)

    def _pass_on(self, a, k, dst, sems, me):
        slot = self._dst(a, dst, jnp.bitwise_xor(me, k))
        idx = a * (N_DEV - 1) + k
        return pltpu.make_async_remote_copy(
            src_ref=slot, dst_ref=slot, send_sem=sems[0].at[idx], recv_sem=sems[1].at[idx],
            device_id=_peer(1), device_id_type=_MESH)

    def _local(self, a, src, dst, sems, me):
        return pltpu.make_async_copy(self._src(a, src, me), self._dst(a, dst, me), sems[2].at[a])

    def start(self, srcs, dsts, sems):
        me = _me()
        for a, (src, dst) in enumerate(zip(srcs, dsts)):
            direct = (1, 2, 4, 6) if self.items[a]["kind"] == "gather" else range(1, N_DEV)
            self._local(a, src, dst, sems, me).start()
            for k in direct:
                self._copy(a, k, src, dst, sems, me, "mine").start()

    def wait(self, srcs, dsts, sems):
        me = _me()
        for a, (src, dst) in enumerate(zip(srcs, dsts)):
            if self.items[a]["kind"] == "gather":
                for k in (2, 4, 6):
                    self._copy(a, k, src, dst, sems, me, "theirs").wait_recv()
                    self._pass_on(a, k, dst, sems, me).start()
                for k in (1, 3, 5, 7):
                    self._copy(a, k, src, dst, sems, me, "theirs").wait_recv()
                for k in (1, 2, 4, 6):
                    self._copy(a, k, src, dst, sems, me, "mine").wait_send()
                for k in (2, 4, 6):
                    self._pass_on(a, k, dst, sems, me).wait_send()
            else:
                for k in range(1, N_DEV):
                    self._copy(a, k, src, dst, sems, me, "theirs").wait_recv()
                for k in range(1, N_DEV):
                    self._copy(a, k, src, dst, sems, me, "mine").wait_send()
            self._local(a, src, dst, sems, me).wait()


def _call(body, *, name, grid, ins, in_specs, out_specs, out_shape, scratch_shapes=(), sem, comm=None):
    n_in, n_out, n_scr = len(ins), len(out_shape), len(scratch_shapes)
    if comm is None:
        outs = pl.pallas_call(
            body, name=name, grid=grid, in_specs=list(in_specs), out_specs=list(out_specs),
            out_shape=list(out_shape), scratch_shapes=list(scratch_shapes), compiler_params=_cp(*sem))(*ins)
        return list(outs), []
    nc, n_into = comm.n, len(comm.intos)

    def hosted(*refs):
        pos = n_in
        c_in = refs[pos:pos + nc]
        pos += nc + n_into
        outs = refs[pos:pos + n_out]
        pos += n_out
        c_out = refs[pos:pos + nc]
        pos += nc
        scr = refs[pos:pos + n_scr]
        sems = refs[pos + n_scr:]
        ids = [pl.program_id(d) for d in range(len(grid))]
        first = functools.reduce(jnp.logical_and, [i == 0 for i in ids])
        last = functools.reduce(jnp.logical_and, [i == g - 1 for i, g in zip(ids, grid)])

        @pl.when(first)
        def _():
            comm.start(c_in, c_out, sems)

        body(*refs[:n_in], *outs, *scr)

        @pl.when(last)
        def _():
            comm.wait(c_in, c_out, sems)

    aliases = {n_in + nc + j: n_out + a for j, (a, _) in enumerate(comm.intos)}
    outs = pl.pallas_call(
        hosted, name=name, grid=grid, in_specs=list(in_specs) + comm.specs() + comm.specs(n_into),
        out_specs=list(out_specs) + comm.specs(), out_shape=list(out_shape) + comm.out_shapes(),
        scratch_shapes=list(scratch_shapes) + comm.scratch(), input_output_aliases=aliases,
        compiler_params=_cp(*(["arbitrary"] * len(grid))))(*ins, *comm.arrays, *[arr for _, arr in comm.intos])
    return list(outs[:n_out]), list(outs[n_out:])


def _sum_slots(gathered, name):
    _, rows, cols = gathered.shape

    def body(g_ref, out_ref):
        acc = g_ref[0]
        for s in range(1, N_DEV):
            acc = acc + g_ref[s]
        out_ref[...] = acc

    return pl.pallas_call(
        body, name=name,
        out_shape=jax.ShapeDtypeStruct((rows, cols), F32),
        compiler_params=_cp(),
    )(gathered)


def _slot_sum(recv_ref, own_ref, shape):
    me = _me()
    acc = jnp.zeros(shape, F32)
    for s in range(N_DEV):
        acc = acc + jnp.where(me == s, own_ref[...], recv_ref[s].astype(F32))
    return acc


def _adamw_math(w, g, m, v):
    nm = ADAM_B1 * m + (1.0 - ADAM_B1) * g
    nv = ADAM_B2 * v + (1.0 - ADAM_B2) * (g * g)
    m_hat = nm / (1.0 - ADAM_B1 ** ADAM_STEP)
    v_hat = nv / (1.0 - ADAM_B2 ** ADAM_STEP)
    return -ADAM_LR * (m_hat / (jnp.sqrt(v_hat) + ADAM_EPS) + ADAM_WD * w), nm, nv


def _sum_shards_adamw(recvs, own, w, m, v, name):
    rows_p, cols = recvs[0].shape[1], recvs[0].shape[2]
    n_p = len(recvs)
    tr = _pick(rows_p, (176, 128, 64, 32, 16, 8))
    tiles = rows_p // tr

    def body(*refs):
        recv_refs = refs[:n_p]
        own_ref, w_ref, m_ref, v_ref, g_ref, d_ref, nm_ref, nv_ref = refs[n_p:]
        for j in range(n_p):
            @pl.when(pl.program_id(0) == j)
            def _():
                g = _slot_sum(recv_refs[j], own_ref, (tr, cols))
                g_ref[...] = g
                d_ref[...], nm_ref[...], nv_ref[...] = _adamw_math(w_ref[...], g, m_ref[...], v_ref[...])

    spec = pl.BlockSpec((tr, cols), lambda p_, i: (p_ * tiles + i, 0))
    shp = jax.ShapeDtypeStruct((rows_p * n_p, cols), F32)
    return pl.pallas_call(
        body, name=name, grid=(n_p, tiles),
        in_specs=[pl.BlockSpec((N_DEV, tr, cols), functools.partial(lambda p_, i, j: (0, jnp.where(p_ == j, i, 0), 0), j=j))
                  for j in range(n_p)] + [spec, spec, spec, spec],
        out_specs=[spec, spec, spec, spec], out_shape=[shp, shp, shp, shp],
        compiler_params=_cp("arbitrary", "arbitrary"),
    )(*recvs, own, w, m, v)


def _mm(a, b, *, name, ta=False, tb=False, out_dtype=F32, out_dtype2=None, bias=None, addend=None,
        addend_scale=1.0, tm=1024, tn=1024, tk=1024, comm=None):
    kdim, m = a.shape if ta else a.shape[::-1]
    n = b.shape[0] if tb else b.shape[1]
    tm = _pick(m, (tm, 1408, 1024, 768, 512, 256, 128))
    tn = _pick(n, (tn, 1408, 1024, 768, 512, 256, 128))
    tk = _pick(kdim, (tk, 1408, 1024, 768, 512, 256, 128))
    nk = kdim // tk
    a_spec = pl.BlockSpec((tk, tm), lambda i, j, k: (k, i)) if ta else pl.BlockSpec((tm, tk), lambda i, j, k: (i, k))
    b_spec = pl.BlockSpec((tn, tk), lambda i, j, k: (j, k)) if tb else pl.BlockSpec((tk, tn), lambda i, j, k: (k, j))
    ins, specs = [a, b], [a_spec, b_spec]
    if bias is not None:
        ins.append(bias)
        specs.append(pl.BlockSpec((1, tn), lambda i, j, k: (0, j)))
    if addend is not None:
        ins.append(addend)
        specs.append(pl.BlockSpec((tm, tn), lambda i, j, k: (i, j)))
    dims = (((0,) if ta else (1,), (1,) if tb else (0,)), ((), ()))
    has_bias, has_addend, two = bias is not None, addend is not None, out_dtype2 is not None

    def body(*refs):
        a_ref, b_ref = refs[0], refs[1]
        pos = 2
        bias_ref = addend_ref = None
        if has_bias:
            bias_ref = refs[pos]
            pos += 1
        if has_addend:
            addend_ref = refs[pos]
            pos += 1
        o_refs, acc_ref = refs[pos:-1], refs[-1]
        k = pl.program_id(2)

        @pl.when(k == 0)
        def _():
            acc_ref[...] = jnp.zeros_like(acc_ref)

        acc_ref[...] += _dot(a_ref[...], b_ref[...], dims)

        @pl.when(k == nk - 1)
        def _():
            r = acc_ref[...]
            if has_bias:
                r = r + bias_ref[...]
            if has_addend:
                r = r + addend_scale * addend_ref[...].astype(F32)
            for o_ref in o_refs:
                o_ref[...] = r.astype(o_ref.dtype)

    ospec = pl.BlockSpec((tm, tn), lambda i, j, k: (i, j))
    dtypes = [out_dtype] + ([out_dtype2] if two else [])
    outs, couts = _call(
        body, name=name, grid=(m // tm, n // tn, nk), ins=ins, in_specs=specs,
        out_specs=[ospec] * len(dtypes), out_shape=[jax.ShapeDtypeStruct((m, n), d) for d in dtypes],
        scratch_shapes=[pltpu.VMEM((tm, tn), F32)], sem=("parallel", "parallel", "arbitrary"), comm=comm)
    primary = tuple(outs) if two else outs[0]
    return (primary, couts) if comm is not None else primary


def _rope_lane_constants():
    inv_freq = np.float32(ROPE_THETA) ** (-np.arange(8, dtype=np.float32) * np.float32(2.0 / 16.0))
    lane = np.arange(LANES) % 64
    freq = np.where(lane < 16, inv_freq[lane % 8], 0.0).astype(np.float32)
    sign = np.where(lane < 8, -1.0, np.where(lane < 16, 1.0, 0.0)).astype(np.float32)
    return jnp.asarray(freq)[None, :], jnp.asarray(sign)[None, :]


def _prep(pos_col, x2, name, comm):
    t, d = x2.shape
    tr = _pick(t, (512, 256, 128))
    freq, sign = _rope_lane_constants()

    def body(pos_ref, freq_ref, sign_ref, x_ref, c_ref, s_ref, xb_ref):
        ang = pos_ref[...].astype(F32) * freq_ref[...]
        c_ref[...] = jnp.cos(ang)
        s_ref[...] = sign_ref[...] * jnp.sin(ang)
        xb_ref[...] = x_ref[...].astype(BF16)

    tab = pl.BlockSpec((tr, LANES), lambda i: (i, 0))
    return _call(
        body, name=name, grid=(t // tr,), ins=[pos_col, freq, sign, x2],
        in_specs=[pl.BlockSpec((tr, 1), lambda i: (i, 0)), pl.BlockSpec((1, LANES), lambda i: (0, 0)),
                  pl.BlockSpec((1, LANES), lambda i: (0, 0)), pl.BlockSpec((tr, d), lambda i: (i, 0))],
        out_specs=[tab, tab, pl.BlockSpec((tr, d), lambda i: (i, 0))],
        out_shape=[jax.ShapeDtypeStruct((t, LANES), F32), jax.ShapeDtypeStruct((t, LANES), F32),
                   jax.ShapeDtypeStruct((t, d), BF16)],
        sem=("parallel",), comm=comm)


def _swap8(t):
    width = t.shape[1]
    lane = jnp.bitwise_and(lax.broadcasted_iota(jnp.int32, t.shape, 1), 63)
    return jnp.where(lane < 8, pltpu.roll(t, width - 8, 1), jnp.where(lane < 16, pltpu.roll(t, 8, 1), 0.0))


def _rope(t, c, s):
    return t * c + _swap8(t) * s


def _rope_bwd(d, c, s):
    return d * c + _swap8(d * s)


def _tile4(a):
    return jnp.concatenate([a, a, a, a], axis=1)


def _attn_band(n, k_cur, k_prev, v_cur, v_prev, c_cur, s_cur, c_prev, s_prev):
    kband = jnp.concatenate([_rope(k_prev, c_prev, s_prev), _rope(k_cur, c_cur, s_cur)], axis=0)
    vband = jnp.concatenate([v_prev, v_cur], axis=0)
    qi = lax.broadcasted_iota(jnp.int32, (ATTN_BLOCK, 2 * ATTN_BLOCK), 0)
    kj = lax.broadcasted_iota(jnp.int32, (ATTN_BLOCK, 2 * ATTN_BLOCK), 1)
    dist = qi + ATTN_BLOCK - kj
    valid = (dist >= 0) & (dist < ATTN_BLOCK) & (n * ATTN_BLOCK - ATTN_BLOCK + kj >= 0)
    return (kband.astype(BF16), pltpu.roll(kband, 64, 1).astype(BF16),
            vband.astype(BF16), pltpu.roll(vband, 64, 1).astype(BF16), valid)


def _attn_probs(raw, valid, sink, axis):
    s = jnp.where(valid, raw * ATTN_SCALE, NEG_BIG)
    m = jnp.maximum(jnp.max(s, axis=axis, keepdims=True), sink)
    p = jnp.exp(s - m)
    esink = jnp.exp(sink - m)
    z = jnp.sum(p, axis=axis, keepdims=True) + esink
    return p / z, esink / z


def _attn_valid_t(n):
    kj = lax.broadcasted_iota(jnp.int32, (2 * ATTN_BLOCK, ATTN_BLOCK), 0)
    qi = lax.broadcasted_iota(jnp.int32, (2 * ATTN_BLOCK, ATTN_BLOCK), 1)
    dist = qi + ATTN_BLOCK - kj
    return (dist >= 0) & (dist < ATTN_BLOCK) & (n * ATTN_BLOCK - ATTN_BLOCK + kj >= 0)


def _attn_specs(nb):
    def cur(col, width=KV_W):
        return pl.BlockSpec((ATTN_BLOCK, width), lambda n: (jnp.minimum(n, nb - 1), col))

    def prev(col):
        return pl.BlockSpec((ATTN_BLOCK, KV_W), lambda n: (jnp.maximum(n - 1, 0), col))

    ua_specs = [cur(0, ATTN_W), cur(4), prev(4), cur(5), prev(5)]
    tab_specs = [cur(0), cur(0), prev(0), prev(0)]
    return ua_specs, tab_specs


def _attn_fwd(ua, ctab, stab, sinks, name, comm=None):
    t = ua.shape[0]
    nb = t // ATTN_BLOCK
    ua_specs, tab_specs = _attn_specs(nb)

    def body(q_ref, kc_ref, kp_ref, vc_ref, vp_ref, cc_ref, sc_ref, cp_ref, sp_ref, sink_ref, o_ref, o_t_ref):
        n = pl.program_id(0)
        cc, sc = cc_ref[...], sc_ref[...]
        kb, kb_r, vb, vb_r, valid = _attn_band(n, kc_ref[...], kp_ref[...], vc_ref[...], vp_ref[...],
                                               cc, sc, cp_ref[...], sp_ref[...])
        qr = _rope(q_ref[...], _tile4(cc), _tile4(sc))
        lo = lax.broadcasted_iota(jnp.int32, (ATTN_BLOCK, LANES), 1) < 64
        heads = []
        for j in range(4):
            qj = qr[:, j * LANES:(j + 1) * LANES]
            for is_lo in (True, False):
                aligned = is_lo == (j < 2)
                qm = jnp.where(lo if is_lo else jnp.logical_not(lo), qj, 0.0).astype(BF16)
                raw = lax.dot_general(qm, kb if aligned else kb_r, _NT, preferred_element_type=F32)
                heads.append((raw, vb if aligned else vb_r, sink_ref[0, len(heads)]))
        halves = []
        for raw, vv, sink in heads:
            probs, _ = _attn_probs(raw, valid, sink, 1)
            halves.append(lax.dot_general(probs.astype(BF16), vv, _NN, preferred_element_type=F32))
        outs = [jnp.where(lo, halves[2 * j], halves[2 * j + 1]) for j in range(4)]
        o_ref[...] = jnp.concatenate(outs, axis=1).astype(o_ref.dtype)
        for j in range(4):
            o_t_ref[j * LANES:(j + 1) * LANES, :] = outs[j].T.astype(o_t_ref.dtype)

    return _call(
        body, name=name, grid=(nb,), ins=[ua, ua, ua, ua, ua, ctab, stab, ctab, stab, sinks],
        in_specs=ua_specs + tab_specs + [pl.BlockSpec(memory_space=pltpu.SMEM)],
        out_specs=[pl.BlockSpec((ATTN_BLOCK, ATTN_W), lambda n: (n, 0)),
                   pl.BlockSpec((ATTN_W, ATTN_BLOCK), lambda n: (0, n))],
        out_shape=[jax.ShapeDtypeStruct((t, ATTN_W), BF16), jax.ShapeDtypeStruct((ATTN_W, t), BF16)],
        sem=("parallel",), comm=comm)


def _attn_bwd(ua, d_out, ctab, stab, sinks, name, comm=None):
    t = ua.shape[0]
    nb = t // ATTN_BLOCK
    ua_specs, tab_specs = _attn_specs(nb)

    def body(q_ref, kc_ref, kp_ref, vc_ref, vp_ref, cc_ref, sc_ref, cp_ref, sp_ref, do_ref, sink_ref,
             dua_ref, dua_t_ref, dbias_ref, dsink_ref, dq_c, dk_c, dv_c, dq_n, dk_n, dv_n):
        n = pl.program_id(0)

        @pl.when(n == 0)
        def _():
            dq_c[...] = jnp.zeros_like(dq_c)
            dk_c[...] = jnp.zeros_like(dk_c)
            dv_c[...] = jnp.zeros_like(dv_c)
            dbias_ref[...] = jnp.zeros_like(dbias_ref)
            dsink_ref[...] = jnp.zeros_like(dsink_ref)

        @pl.when(n == nb)
        def _():
            dq_n[...] = jnp.zeros_like(dq_n)
            dk_n[...] = jnp.zeros_like(dk_n)
            dv_n[...] = jnp.zeros_like(dv_n)

        @pl.when(n < nb)
        def _():
            cc, sc = cc_ref[...], sc_ref[...]
            kb, kb_r, vb, vb_r, valid = _attn_band(n, kc_ref[...], kp_ref[...], vc_ref[...], vp_ref[...],
                                                   cc, sc, cp_ref[...], sp_ref[...])
            valid_t = _attn_valid_t(n)
            c4, s4 = _tile4(cc), _tile4(sc)
            qr = _rope(q_ref[...], c4, s4)
            do = do_ref[...].astype(F32)
            lane = lax.broadcasted_iota(jnp.int32, (ATTN_BLOCK, LANES), 1)
            lo = lane < 64
            lane_row = lax.broadcasted_iota(jnp.int32, (1, LANES), 1)
            heads = []
            for j in range(4):
                qj = qr[:, j * LANES:(j + 1) * LANES]
                doj = do[:, j * LANES:(j + 1) * LANES]
                for is_lo in (True, False):
                    aligned = is_lo == (j < 2)
                    msk = lo if is_lo else jnp.logical_not(lo)
                    kk = kb if aligned else kb_r
                    vv = vb if aligned else vb_r
                    qm = jnp.where(msk, qj, 0.0).astype(BF16)
                    dom = jnp.where(msk, doj, 0.0).astype(BF16)
                    heads.append(dict(
                        aligned=aligned, kk=kk, qm=qm, dom=dom, sink=sink_ref[0, len(heads)],
                        raw=lax.dot_general(qm, kk, _NT, preferred_element_type=F32),
                        dp=lax.dot_general(dom, vv, _NT, preferred_element_type=F32),
                        raw_t=lax.dot_general(kk, qm, _NT, preferred_element_type=F32),
                        dp_t=lax.dot_general(vv, dom, _NT, preferred_element_type=F32)))
            dk_band = jnp.zeros((2 * ATTN_BLOCK, LANES), F32)
            dv_band = jnp.zeros((2 * ATTN_BLOCK, LANES), F32)
            dsink = jnp.zeros((1, LANES), F32)
            halves = []
            for head, hd in enumerate(heads):
                probs, psink = _attn_probs(hd["raw"], valid, hd["sink"], 1)
                delta = jnp.sum(probs * hd["dp"], axis=1, keepdims=True)
                ds = (probs * (hd["dp"] - delta) * ATTN_SCALE).astype(BF16)
                dsink = dsink + jnp.where(lane_row == head, -jnp.sum(psink * delta), 0.0)
                halves.append(lax.dot_general(ds, hd["kk"], _NN, preferred_element_type=F32))
                probs_t, _ = _attn_probs(hd["raw_t"], valid_t, hd["sink"], 0)
                delta_t = jnp.sum(probs_t * hd["dp_t"], axis=0, keepdims=True)
                ds_t = (probs_t * (hd["dp_t"] - delta_t) * ATTN_SCALE).astype(BF16)
                dk_h = lax.dot_general(ds_t, hd["qm"], _NN, preferred_element_type=F32)
                dv_h = lax.dot_general(probs_t.astype(BF16), hd["dom"], _NN, preferred_element_type=F32)
                if not hd["aligned"]:
                    dk_h = pltpu.roll(dk_h, 64, 1)
                    dv_h = pltpu.roll(dv_h, 64, 1)
                dk_band = dk_band + dk_h
                dv_band = dv_band + dv_h
            dqs = [jnp.where(lo, halves[2 * j], halves[2 * j + 1]) for j in range(4)]
            dq_n[...] = _rope_bwd(jnp.concatenate(dqs, axis=1), c4, s4)
            dk_n[...] = dk_band
            dv_n[...] = dv_band
            dsink_ref[...] += dsink

        dk_prev = _rope_bwd(dk_c[...] + dk_n[0:ATTN_BLOCK, :], cp_ref[...], sp_ref[...])
        dv_prev = dv_c[...] + dv_n[0:ATTN_BLOCK, :]
        full = jnp.concatenate([dq_c[...], dk_prev, dv_prev], axis=1)
        dua_ref[...] = full.astype(dua_ref.dtype)
        for j in range(UA_W // LANES):
            dua_t_ref[j * LANES:(j + 1) * LANES, :] = full[:, j * LANES:(j + 1) * LANES].T.astype(dua_t_ref.dtype)
        dbias_ref[...] += jnp.sum(full, axis=0, keepdims=True)
        dq_c[...] = dq_n[...]
        dk_c[...] = dk_n[ATTN_BLOCK:, :]
        dv_c[...] = dv_n[ATTN_BLOCK:, :]

    return _call(
        body, name=name, grid=(nb + 1,), ins=[ua, ua, ua, ua, ua, ctab, stab, ctab, stab, d_out, sinks],
        in_specs=ua_specs + tab_specs + [
            pl.BlockSpec((ATTN_BLOCK, ATTN_W), lambda n: (jnp.minimum(n, nb - 1), 0)),
            pl.BlockSpec(memory_space=pltpu.SMEM)],
        out_specs=[pl.BlockSpec((ATTN_BLOCK, UA_W), lambda n: (jnp.maximum(n - 1, 0), 0)),
                   pl.BlockSpec((UA_W, ATTN_BLOCK), lambda n: (0, jnp.maximum(n - 1, 0))),
                   pl.BlockSpec((1, UA_W), lambda n: (0, 0)),
                   pl.BlockSpec((1, LANES), lambda n: (0, 0))],
        out_shape=[jax.ShapeDtypeStruct((t, UA_W), BF16), jax.ShapeDtypeStruct((UA_W, t), BF16),
                   jax.ShapeDtypeStruct((1, UA_W), F32),
                   jax.ShapeDtypeStruct((1, LANES), F32)],
        scratch_shapes=[pltpu.VMEM((ATTN_BLOCK, ATTN_W), F32), pltpu.VMEM((ATTN_BLOCK, KV_W), F32),
                        pltpu.VMEM((ATTN_BLOCK, KV_W), F32), pltpu.VMEM((ATTN_BLOCK, ATTN_W), F32),
                        pltpu.VMEM((2 * ATTN_BLOCK, KV_W), F32), pltpu.VMEM((2 * ATTN_BLOCK, KV_W), F32)],
        sem=("arbitrary",), comm=comm)


def _tri_mats():
    r = lax.broadcasted_iota(jnp.int32, (HGRN_CHUNK, LANES), 0)
    c = lax.broadcasted_iota(jnp.int32, (HGRN_CHUNK, LANES), 1)
    lower = ((c <= r) & (c < HGRN_CHUNK)).astype(F32)
    upper = ((c >= r) & (c < HGRN_CHUNK)).astype(F32)
    return lower, upper


def _tri_apply(tri, g):
    pad = jnp.concatenate([g, jnp.zeros_like(g)], axis=0)
    return lax.dot_general(tri, pad, _NN, precision=lax.Precision.HIGHEST, preferred_element_type=F32)


def _sub_masks():
    s = lax.broadcasted_iota(jnp.int32, (HGRN_CHUNK, LANES), 0)
    tt = lax.broadcasted_iota(jnp.int32, (HGRN_CHUNK, LANES), 1)
    return [(tt >= HGRN_SUB * i) & (tt < HGRN_SUB * (i + 1)) & (s <= tt) for i in range(HGRN_CHUNK // HGRN_SUB)]


def _hgrn_gates(hq, hf, lb_ref, b_scr):
    lb = _sig(lb_ref[0:1, :] - lb_ref[1:2, :])
    q = hq * _sig(hq)
    sg = _sig(hf)
    f = lb + (1.0 - lb) * sg
    k = 1.0 - f
    lower, _ = _tri_mats()
    b = _tri_apply(lower, jnp.log(f))
    b_scr[...] = b
    nsub = HGRN_CHUNK // HGRN_SUB
    starts = [jnp.zeros((1, HG_W), F32)] + [b_scr[HGRN_SUB * i - 1:HGRN_SUB * i, :] for i in range(1, nsub)]
    pq = jnp.concatenate([jnp.broadcast_to(p, (HGRN_SUB, HG_W)) for p in starts], axis=0)
    b_last = b_scr[HGRN_CHUNK - 1:HGRN_CHUNK, :]
    e_q = jnp.exp(b - pq)
    e_k = [jnp.exp(jnp.minimum(p - b, EXP_CLAMP)) for p in starts]
    e_b = jnp.exp(b)
    e_bl = jnp.exp(b_last - b)
    e_last = jnp.exp(b_last)
    return q, sg, f, k, lb, e_q, e_k, e_b, e_bl, e_last


def _sub_masks_ts():
    tt = lax.broadcasted_iota(jnp.int32, (HGRN_CHUNK, LANES), 0)
    s = lax.broadcasted_iota(jnp.int32, (HGRN_CHUNK, LANES), 1)
    return [(tt >= HGRN_SUB * i) & (tt < HGRN_SUB * (i + 1)) & (s <= tt) for i in range(HGRN_CHUNK // HGRN_SUB)]


def _masked_sum(blocks, masks, axis):
    step = HGRN_CHUNK if axis == 0 else LANES
    acc = jnp.zeros((HGRN_CHUNK, LANES), F32)
    for i, msk in enumerate(masks):
        blk = blocks[step * i:step * (i + 1), :] if axis == 0 else blocks[:, step * i:step * (i + 1)]
        acc = acc + jnp.where(msk, blk, 0.0)
    return acc


def _store_transposed(out_t_ref, chunk_rows):
    width = chunk_rows[0].shape[1]
    if len(chunk_rows) == 1:
        groups = [jnp.concatenate([chunk_rows[0], jnp.zeros_like(chunk_rows[0])], axis=0)]
    else:
        groups = [jnp.concatenate(chunk_rows[g:g + 2], axis=0) for g in range(0, len(chunk_rows), 2)]
    for g, rows in enumerate(groups):
        for c in range(width // LANES):
            tile = rows[:, c * LANES:(c + 1) * LANES].T.astype(out_t_ref.dtype)
            if len(chunk_rows) == 1:
                out_t_ref[c * LANES:(c + 1) * LANES, :] = tile[:, 0:HGRN_CHUNK]
            else:
                out_t_ref[c * LANES:(c + 1) * LANES, g * LANES:(g + 1) * LANES] = tile


def _hgrn_chunk_inputs(j, hq_ref, hf_ref, hi_ref, hg_ref, lb_ref, b_scr):
    rows = slice(j * HGRN_CHUNK, (j + 1) * HGRN_CHUNK)
    hq, hf, v, hg = hq_ref[rows, :], hf_ref[rows, :], hi_ref[rows, :], hg_ref[rows, :]
    q, sg, f, k, lb, e_q, e_k, e_b, e_bl, e_last = _hgrn_gates(hq, hf, lb_ref, b_scr.at[j])
    return dict(rows=rows, hq=hq, v=v, hg=hg, q=q, sg=sg, f=f, k=k, lb=lb, e_q=e_q, e_k=e_k, e_b=e_b, e_bl=e_bl,
                e_last=e_last, qt=q * e_q, qb=q * e_b, kd=k * e_bl, khat=[k * e for e in e_k])


def _hgrn_fwd(uh, lb_raw, norm_g, name, comm=None):
    t = uh.shape[0]
    nc = t // HGRN_CHUNK
    cps = _pick(nc, (HGRN_CHUNKS_PER_STEP, 2, 1))
    rows_step = cps * HGRN_CHUNK

    def body(hq_ref, hf_ref, hi_ref, hg_ref, lb_ref, ng_ref, r_ref, r_t_ref, o_ref, st_out_ref, st_ref, b_scr):
        @pl.when(pl.program_id(0) == 0)
        def _():
            st_ref[...] = jnp.zeros_like(st_ref)

        masks = _sub_masks_ts()
        ng = ng_ref[...]
        zpad = jnp.zeros((HGRN_CHUNK, LANES), F32)
        heads = [slice(h * LANES, (h + 1) * LANES) for h in range(4)]
        chunks = [_hgrn_chunk_inputs(j, hq_ref, hf_ref, hi_ref, hg_ref, lb_ref, b_scr) for j in range(cps)]
        for ch in chunks:
            ch["scores"] = [_dot3(ch["qt"][:, sl],
                                  jnp.concatenate([x for kh in ch["khat"] for x in (kh[:, sl], zpad)], axis=0), _NT)
                            for sl in heads]
        for j, ch in enumerate(chunks):
            o_heads, y_heads = [], []
            for h, sl in enumerate(heads):
                a_ts = _masked_sum(ch["scores"][h], masks, 1)
                vh = ch["v"][:, sl].astype(BF16)
                v_pad = jnp.concatenate([vh, jnp.zeros_like(vh)], axis=0)
                o_intra = lax.dot_general(a_ts.astype(BF16), v_pad, _NN, preferred_element_type=F32)
                st = st_ref[h]
                st_out_ref[j, h] = st
                o_inter = _dot(ch["qb"][:, sl], st, _NT)
                st_ref[h] = st * ch["e_last"][:, sl] + _dot(vh, ch["kd"][:, sl], _TN)
                oh = o_intra + o_inter
                rs = lax.rsqrt(jnp.mean(oh * oh, axis=1, keepdims=True) + RMS_EPS)
                o_heads.append(oh)
                y_heads.append(oh * rs * ng)
            hg = ch["hg"]
            o_ref[ch["rows"], :] = jnp.concatenate(o_heads, axis=1)
            ch["r"] = jnp.concatenate(y_heads, axis=1) * (hg * _sig(hg))
            r_ref[ch["rows"], :] = ch["r"].astype(r_ref.dtype)
        _store_transposed(r_t_ref, [ch["r"] for ch in chunks])

    col = lambda j: pl.BlockSpec((rows_step, HG_W), lambda c: (c, j))
    return _call(
        body, name=name, grid=(nc // cps,), ins=[uh, uh, uh, uh, lb_raw, norm_g],
        in_specs=[col(0), col(1), col(2), col(3),
                  pl.BlockSpec((2, HG_W), lambda c: (0, 0)), pl.BlockSpec((1, LANES), lambda c: (0, 0))],
        out_specs=[pl.BlockSpec((rows_step, HG_W), lambda c: (c, 0)),
                   pl.BlockSpec((HG_W, rows_step), lambda c: (0, c)),
                   pl.BlockSpec((rows_step, HG_W), lambda c: (c, 0)),
                   pl.BlockSpec((cps, 4, LANES, LANES), lambda c: (c, 0, 0, 0))],
        out_shape=[jax.ShapeDtypeStruct((t, HG_W), BF16), jax.ShapeDtypeStruct((HG_W, t), BF16),
                   jax.ShapeDtypeStruct((t, HG_W), F32), jax.ShapeDtypeStruct((nc, 4, LANES, LANES), F32)],
        scratch_shapes=[pltpu.VMEM((4, LANES, LANES), F32), pltpu.VMEM((cps, HGRN_CHUNK, HG_W), F32)],
        sem=("arbitrary",), comm=comm)


def _hgrn_bwd(uh, o_pre, d_r, states, lb_raw, norm_g, name, comm=None):
    t = uh.shape[0]
    nc = t // HGRN_CHUNK
    cps = _pick(nc, (HGRN_CHUNKS_PER_STEP, 2, 1))
    ns = nc // cps
    rows_step = cps * HGRN_CHUNK
    nsub = HGRN_CHUNK // HGRN_SUB

    def body(hq_ref, hf_ref, hi_ref, hg_ref, o_ref, dr_ref, st_in_ref, lb_ref, ng_ref,
             duh_ref, duh_t_ref, dbias_ref, dng_ref, dlb_ref, dst_ref, b_scr, dlb_acc):
        i = pl.program_id(0)

        @pl.when(i == 0)
        def _():
            dst_ref[...] = jnp.zeros_like(dst_ref)
            dbias_ref[...] = jnp.zeros_like(dbias_ref)
            dng_ref[...] = jnp.zeros_like(dng_ref)
            dlb_acc[...] = jnp.zeros_like(dlb_acc)

        masks_st = _sub_masks()
        masks_ts = _sub_masks_ts()
        ng = ng_ref[...]
        zpad = jnp.zeros((HGRN_CHUNK, LANES), F32)
        _, upper = _tri_mats()
        heads = [slice(h * LANES, (h + 1) * LANES) for h in range(4)]
        row = lax.broadcasted_iota(jnp.int32, (HGRN_CHUNK, HG_W), 0)

        chunks = [_hgrn_chunk_inputs(j, hq_ref, hf_ref, hi_ref, hg_ref, lb_ref, b_scr) for j in range(cps)]
        dng = jnp.zeros((1, LANES), F32)
        for ch in chunks:
            o = o_ref[ch["rows"], :]
            dr = dr_ref[ch["rows"], :].astype(F32)
            hg = ch["hg"]
            sgg = _sig(hg)
            dy = dr * (hg * sgg)
            do_h, y_h = [], []
            for sl in heads:
                oh = o[:, sl]
                rs = lax.rsqrt(jnp.mean(oh * oh, axis=1, keepdims=True) + RMS_EPS)
                y_h.append(oh * rs * ng)
                dng = dng + jnp.sum(dy[:, sl] * oh * rs, axis=0, keepdims=True)
                w = dy[:, sl] * ng
                do_h.append(rs * (w - oh * (rs * rs) * jnp.mean(w * oh, axis=1, keepdims=True)))
            ch["do"] = do_h
            ch["dhg"] = dr * jnp.concatenate(y_h, axis=1) * _dsilu(hg, sgg)

        for ch in chunks:
            ch["kst"], ch["kpad"], ch["qt_pad"], ch["v_b"], ch["do_pad"] = [], [], [], [], []
            ch["ats"], ch["d_at"], ch["d_a"] = [], [], []
            for h, sl in enumerate(heads):
                kst = jnp.concatenate([kh[:, sl] for kh in ch["khat"]], axis=0)
                kpad = jnp.concatenate([x for kh in ch["khat"] for x in (kh[:, sl], zpad)], axis=0)
                qt_pad = jnp.concatenate([ch["qt"][:, sl], zpad], axis=0)
                vh = ch["v"][:, sl].astype(BF16)
                v_pad = jnp.concatenate([vh, jnp.zeros_like(vh)], axis=0)
                do_b = ch["do"][h].astype(BF16)
                do_pad = jnp.concatenate([do_b, jnp.zeros_like(do_b)], axis=0)
                ch["kst"].append(kst)
                ch["kpad"].append(kpad)
                ch["qt_pad"].append(qt_pad)
                ch["v_b"].append(vh)
                ch["do_pad"].append(do_pad)
                ch["ats"].append(_dot3(kst, qt_pad, _NT))
                ch["d_at"].append(lax.dot_general(vh, do_pad, _NT, preferred_element_type=F32))
                ch["d_a"].append(lax.dot_general(do_b, v_pad, _NT, preferred_element_type=F32))

        for ch in chunks:
            ch["d_kst"], ch["d_qt"], ch["dv"] = [], [], []
            for h in range(4):
                at = _masked_sum(ch["ats"][h], masks_st, 0)
                d_ats = jnp.concatenate([jnp.where(m, ch["d_at"][h], 0.0) for m in masks_st], axis=0)
                d_a_cat = jnp.concatenate([jnp.where(m, ch["d_a"][h], 0.0) for m in masks_ts], axis=1)
                ch["d_kst"].append(_dot3(d_ats, ch["qt_pad"][h], _NN))
                ch["d_qt"].append(_dot3(d_a_cat, ch["kpad"][h], _NN))
                ch["dv"].append(lax.dot_general(at.astype(BF16), ch["do_pad"][h], _NN, preferred_element_type=F32))

        for j in reversed(range(cps)):
            ch = chunks[j]
            q, k, sg, f, lb = ch["q"], ch["k"], ch["sg"], ch["f"], ch["lb"]
            dq_h, dk_h, dv_h, extra_h = [], [], [], []
            for h, sl in enumerate(heads):
                st_prev = st_in_ref[j, h]
                d_st = dst_ref[h]
                d_st_b = d_st.astype(BF16)
                do_b = ch["do_pad"][h][0:HGRN_CHUNK, :]
                kd, e_last = ch["kd"][:, sl], ch["e_last"][:, sl]
                dv = ch["dv"][h] + _dot(kd, d_st_b, _NT)
                d_qb = _dot(do_b, st_prev, _NN)
                d_kd = lax.dot_general(ch["v_b"][h], d_st_b, _NN, preferred_element_type=F32)
                extra_h.append(jnp.sum(st_prev * d_st, axis=0, keepdims=True) * e_last
                               + jnp.sum(kd * d_kd, axis=0, keepdims=True))
                dst_ref[h] = d_st * e_last + _dot(do_b, ch["qb"][:, sl], _TN)
                dq_h.append(ch["d_qt"][h] * ch["e_q"][:, sl] + d_qb * ch["e_b"][:, sl])
                dkk = d_kd * ch["e_bl"][:, sl]
                for s_ in range(nsub):
                    dkk = dkk + ch["d_kst"][h][HGRN_CHUNK * s_:HGRN_CHUNK * (s_ + 1), :] * ch["e_k"][s_][:, sl]
                dk_h.append(dkk)
                dv_h.append(dv)
            dq = jnp.concatenate(dq_h, axis=1)
            dk = jnp.concatenate(dk_h, axis=1)
            dv = jnp.concatenate(dv_h, axis=1)
            extra = jnp.concatenate(extra_h, axis=1)
            db = q * dq - k * dk + jnp.where(row == HGRN_CHUNK - 1, extra, 0.0)
            dg = _tri_apply(upper, db)
            df = dg / f - dk
            dhf = df * (1.0 - lb) * sg * (1.0 - sg)
            dhq = dq * _dsilu(ch["hq"], _sig(ch["hq"]))
            full = jnp.concatenate([dhq, dhf, dv, ch["dhg"]], axis=1)
            duh_ref[ch["rows"], :] = full.astype(duh_ref.dtype)
            ch["full"] = full
            dbias_ref[...] += jnp.sum(full, axis=0, keepdims=True)
            dlb_acc[...] += jnp.sum(df * (1.0 - sg), axis=0, keepdims=True)
        dng_ref[...] += dng
        _store_transposed(duh_t_ref, [ch["full"] for ch in chunks])

        @pl.when(i == ns - 1)
        def _():
            lb = chunks[0]["lb"]
            d_a0 = dlb_acc[...] * lb * (1.0 - lb)
            r8 = lax.broadcasted_iota(jnp.int32, (8, HG_W), 0)
            dlb_ref[...] = jnp.where(r8 == 0, d_a0, jnp.where(r8 == 1, -d_a0, 0.0))

    col = lambda j: pl.BlockSpec((rows_step, HG_W), lambda i: (ns - 1 - i, j))
    return _call(
        body, name=name, grid=(ns,), ins=[uh, uh, uh, uh, o_pre, d_r, states, lb_raw, norm_g],
        in_specs=[col(0), col(1), col(2), col(3), col(0), col(0),
                  pl.BlockSpec((cps, 4, LANES, LANES), lambda i: (ns - 1 - i, 0, 0, 0)),
                  pl.BlockSpec((2, HG_W), lambda i: (0, 0)), pl.BlockSpec((1, LANES), lambda i: (0, 0))],
        out_specs=[pl.BlockSpec((rows_step, UH_W), lambda i: (ns - 1 - i, 0)),
                   pl.BlockSpec((UH_W, rows_step), lambda i: (0, ns - 1 - i)),
                   pl.BlockSpec((1, UH_W), lambda i: (0, 0)),
                   pl.BlockSpec((1, LANES), lambda i: (0, 0)),
                   pl.BlockSpec((8, HG_W), lambda i: (0, 0))],
        out_shape=[jax.ShapeDtypeStruct((t, UH_W), BF16), jax.ShapeDtypeStruct((UH_W, t), BF16),
                   jax.ShapeDtypeStruct((1, UH_W), F32),
                   jax.ShapeDtypeStruct((1, LANES), F32), jax.ShapeDtypeStruct((8, HG_W), F32)],
        scratch_shapes=[pltpu.VMEM((4, LANES, LANES), F32), pltpu.VMEM((cps, HGRN_CHUNK, HG_W), F32),
                        pltpu.VMEM((1, HG_W), F32)],
        sem=("arbitrary",), comm=comm)


def _ln_bwd_math(dy, xhat, rstd, g):
    dxh = dy * g
    return rstd * (dxh - jnp.mean(dxh, axis=1, keepdims=True)
                   - xhat * jnp.mean(dxh * xhat, axis=1, keepdims=True))


def _mm_rows(a, b, extras, *, name, epilogue, out_shape, out_specs, tb=False, tm=512, tk=1408):
    m, kdim = a.shape
    n = b.shape[0] if tb else b.shape[1]
    tm = _pick(m, (tm, 256, 128))
    tk = _pick(kdim, (tk, 1408, 1024, 768, 512, 256, 128))
    nk = kdim // tk
    b_spec = pl.BlockSpec((n, tk), lambda i, k: (0, k)) if tb else pl.BlockSpec((tk, n), lambda i, k: (k, 0))
    dims = _NT if tb else _NN
    n_ex, n_out = len(extras), len(out_shape)

    def body(*refs):
        a_ref, b_ref = refs[0], refs[1]
        ex_refs = refs[2:2 + n_ex]
        o_refs = refs[2 + n_ex:2 + n_ex + n_out]
        acc_ref = refs[-1]
        i, k = pl.program_id(0), pl.program_id(1)

        @pl.when(k == 0)
        def _():
            acc_ref[...] = jnp.zeros_like(acc_ref)

        acc_ref[...] += _dot(a_ref[...], b_ref[...], dims)

        @pl.when(k == nk - 1)
        def _():
            epilogue(acc_ref[...], ex_refs, o_refs, i == 0)

    return pl.pallas_call(
        body, name=name, grid=(m // tm, nk),
        in_specs=[pl.BlockSpec((tm, tk), lambda i, k: (i, k)), b_spec] + [sp for _, sp in extras],
        out_specs=list(out_specs), out_shape=list(out_shape),
        scratch_shapes=[pltpu.VMEM((tm, n), F32)],
        compiler_params=_cp("arbitrary", "arbitrary"),
    )(a, b, *[arr for arr, _ in extras])


def _rows_specs(tm, d):
    row = pl.BlockSpec((tm, d), lambda i, k: (i, 0))
    vec = pl.BlockSpec((1, d), lambda i, k: (0, 0))
    col = pl.BlockSpec((tm, 1), lambda i, k: (i, 0))
    return row, vec, col


def _mm_ln_fwd(a, b, addend, g, beta, name, tm=512):
    t, d = addend.shape
    tm = _pick(t, (tm, 256, 128))
    row, vec, col = _rows_specs(tm, d)

    def epilogue(acc, ex, outs, first):
        z = acc + ex[0][...]
        mu = jnp.mean(z, axis=1, keepdims=True)
        zc = z - mu
        rstd = lax.rsqrt(jnp.mean(zc * zc, axis=1, keepdims=True) + LN_EPS)
        xhat = zc * rstd
        h = xhat * ex[1][...] + ex[2][...]
        outs[0][...] = h
        outs[1][...] = h.astype(BF16)
        outs[2][...] = xhat
        outs[3][...] = rstd

    return _mm_rows(a, b, [(addend, row), (g, vec), (beta, vec)], name=name, epilogue=epilogue, tm=tm,
                    out_shape=[jax.ShapeDtypeStruct((t, d), F32), jax.ShapeDtypeStruct((t, d), BF16),
                               jax.ShapeDtypeStruct((t, d), F32), jax.ShapeDtypeStruct((t, 1), F32)],
                    out_specs=[row, row, row, col])


def _mm_ln_loss_bwd(a, b, addend, target, g, beta, name, tm=1024):
    t, d = addend.shape
    tm = _pick(t, (tm, 256, 128))
    row, vec, _ = _rows_specs(tm, d)

    def epilogue(acc, ex, outs, first):
        dz_ref, dg_ref, db_ref, loss_ref = outs

        @pl.when(first)
        def _():
            dg_ref[...] = jnp.zeros_like(dg_ref)
            db_ref[...] = jnp.zeros_like(db_ref)
            loss_ref[...] = jnp.zeros_like(loss_ref)

        z = acc + ALPHA * ex[0][...]
        gg = ex[2][...]
        mu = jnp.mean(z, axis=1, keepdims=True)
        zc = z - mu
        rstd = lax.rsqrt(jnp.mean(zc * zc, axis=1, keepdims=True) + LN_EPS)
        xhat = zc * rstd
        err = xhat * gg + ex[3][...] - ex[1][...]
        loss_ref[...] += 0.5 * jnp.sum(jnp.mean(err * err, axis=1, keepdims=True))
        dy = err * (1.0 / d)
        dz_ref[...] = _ln_bwd_math(dy, xhat, rstd, gg)
        dg_ref[...] += jnp.sum(dy * xhat, axis=0, keepdims=True)
        db_ref[...] += jnp.sum(dy, axis=0, keepdims=True)

    return _mm_rows(a, b, [(addend, row), (target, row), (g, vec), (beta, vec)], name=name, epilogue=epilogue,
                    tm=tm,
                    out_shape=[jax.ShapeDtypeStruct((t, d), F32), jax.ShapeDtypeStruct((1, d), F32),
                               jax.ShapeDtypeStruct((1, d), F32), jax.ShapeDtypeStruct((1, LANES), F32)],
                    out_specs=[row, vec, vec, pl.BlockSpec((1, LANES), lambda i, k: (0, 0))])


def _mm_ln_bwd(a, b, addend, xhat, rstd, g, name, tm=1024):
    t, d = addend.shape
    tm = _pick(t, (tm, 256, 128))
    row, vec, col = _rows_specs(tm, d)

    def epilogue(acc, ex, outs, first):
        dz_ref, dg_ref, db_ref = outs

        @pl.when(first)
        def _():
            dg_ref[...] = jnp.zeros_like(dg_ref)
            db_ref[...] = jnp.zeros_like(db_ref)

        dy = acc + ALPHA * ex[0][...]
        xh = ex[1][...]
        dz_ref[...] = _ln_bwd_math(dy, xh, ex[2][...], ex[3][...])
        dg_ref[...] += jnp.sum(dy * xh, axis=0, keepdims=True)
        db_ref[...] += jnp.sum(dy, axis=0, keepdims=True)

    return _mm_rows(a, b, [(addend, row), (xhat, row), (rstd, col), (g, vec)], name=name, epilogue=epilogue, tm=tm,
                    out_shape=[jax.ShapeDtypeStruct((t, d), F32), jax.ShapeDtypeStruct((1, d), F32),
                               jax.ShapeDtypeStruct((1, d), F32)],
                    out_specs=[row, vec, vec])


CONV_TILE = 128
CONV_RB = 32
HALO = 8


def _sum8(x):
    acc = x[0:8]
    for r in range(8, x.shape[0], 8):
        acc = acc + x[r:r + 8]
    return acc


def _conv_fwd(u2, conv_w, conv_b, name):
    t = u2.shape[0]
    tr = _pick(t, (CONV_TILE,))
    hb = tr // HALO
    rb = CONV_RB

    def body(gp_ref, val_ref, prev_ref, w_ref, b_ref, out_ref, ext):
        i = pl.program_id(0)
        ext[0:HALO, :] = jnp.where(i == 0, 0.0, prev_ref[...])
        ext[HALO:, :] = gp_ref[...]
        for c in range(D_FF // LANES):
            ln = slice(c * LANES, (c + 1) * LANES)
            w0, w1, w2, bb = w_ref[0:1, ln], w_ref[1:2, ln], w_ref[2:3, ln], b_ref[:, ln]
            for r0 in range(0, tr, rb):
                gate = (ext[r0 + HALO - 2:r0 + HALO - 2 + rb, ln] * w0 + ext[r0 + HALO - 1:r0 + HALO - 1 + rb, ln] * w1
                        + ext[r0 + HALO:r0 + HALO + rb, ln] * w2 + bb)
                out_ref[r0:r0 + rb, ln] = (gate * _sig(gate) * val_ref[r0:r0 + rb, ln]).astype(out_ref.dtype)

    return pl.pallas_call(
        body, name=name, grid=(t // tr,),
        in_specs=[pl.BlockSpec((tr, D_FF), lambda i: (i, 0)), pl.BlockSpec((tr, D_FF), lambda i: (i, 1)),
                  pl.BlockSpec((HALO, D_FF), lambda i: (jnp.maximum(i * hb - 1, 0), 0)),
                  pl.BlockSpec((3, D_FF), lambda i: (0, 0)), pl.BlockSpec((1, D_FF), lambda i: (0, 0))],
        out_specs=pl.BlockSpec((tr, D_FF), lambda i: (i, 0)),
        out_shape=jax.ShapeDtypeStruct((t, D_FF), BF16),
        scratch_shapes=[pltpu.VMEM((tr + HALO, D_FF), F32)],
        compiler_params=_cp("parallel"),
    )(u2, u2, u2, conv_w, conv_b)


FFN_TILE = 256
FFN_COLS = 256


def _resident(shape):
    return pl.BlockSpec(shape, lambda i: (0,) * len(shape), pipeline_mode=pl.Buffered(1))


def _ffn_fwd(h1b, h1, w_up_t, conv_w, conv_b, w_down, target, ln2_g, ln2_b, name, comm=None):
    t, d = h1.shape
    tr = _pick(t, (FFN_TILE, 128))
    nblk = D_FF // FFN_COLS
    rb = CONV_RB

    def body(a_ref, wup_ref, cw_ref, cb_ref, wd_ref, h1_ref, tgt_ref, g_ref, b_ref,
             u2_ref, hm_ref, dz_ref, dg_ref, db_ref, loss_ref, ext):
        i = pl.program_id(0)

        @pl.when(i == 0)
        def _():
            ext[0:HALO, :] = jnp.zeros((HALO, D_FF), F32)
            dg_ref[...] = jnp.zeros_like(dg_ref)
            db_ref[...] = jnp.zeros_like(db_ref)
            loss_ref[...] = jnp.zeros_like(loss_ref)

        a = a_ref[...]
        for c in range(nblk):
            cs = slice(c * FFN_COLS, (c + 1) * FFN_COLS)
            vs = slice(D_FF + c * FFN_COLS, D_FF + (c + 1) * FFN_COLS)
            gate_pre = lax.dot_general(a, wup_ref[cs, :], _NT, preferred_element_type=F32)
            u2_ref[:, cs] = gate_pre
            ext[HALO:, cs] = gate_pre
            u2_ref[:, vs] = lax.dot_general(a, wup_ref[vs, :], _NT, preferred_element_type=F32)
        acc = jnp.zeros((tr, d), F32)
        for c in range(nblk):
            cs = slice(c * FFN_COLS, (c + 1) * FFN_COLS)
            for sub in range(FFN_COLS // LANES):
                ln = slice(c * FFN_COLS + sub * LANES, c * FFN_COLS + (sub + 1) * LANES)
                vl = slice(D_FF + c * FFN_COLS + sub * LANES, D_FF + c * FFN_COLS + (sub + 1) * LANES)
                w0, w1, w2, bb = cw_ref[0:1, ln], cw_ref[1:2, ln], cw_ref[2:3, ln], cb_ref[:, ln]
                for r0 in range(0, tr, rb):
                    gate = (ext[r0 + HALO - 2:r0 + HALO - 2 + rb, ln] * w0
                            + ext[r0 + HALO - 1:r0 + HALO - 1 + rb, ln] * w1
                            + ext[r0 + HALO:r0 + HALO + rb, ln] * w2 + bb)
                    hm_ref[r0:r0 + rb, ln] = (gate * _sig(gate) * u2_ref[r0:r0 + rb, vl]).astype(hm_ref.dtype)
            acc = acc + lax.dot_general(hm_ref[:, cs], wd_ref[cs, :], _NN, preferred_element_type=F32)
        ext[0:HALO, :] = ext[tr:tr + HALO, :]

        z = acc + ALPHA * h1_ref[...]
        gg = g_ref[...]
        mu = jnp.mean(z, axis=1, keepdims=True)
        zc = z - mu
        rstd = lax.rsqrt(jnp.mean(zc * zc, axis=1, keepdims=True) + LN_EPS)
        xhat = zc * rstd
        err = xhat * gg + b_ref[...] - tgt_ref[...]
        loss_ref[...] += 0.5 * jnp.sum(jnp.mean(err * err, axis=1, keepdims=True))
        dy = err * (1.0 / d)
        dz_ref[...] = _ln_bwd_math(dy, xhat, rstd, gg)
        dg_ref[...] += jnp.sum(dy * xhat, axis=0, keepdims=True)
        db_ref[...] += jnp.sum(dy, axis=0, keepdims=True)

    row = lambda w: pl.BlockSpec((tr, w), lambda i: (i, 0))
    vec = pl.BlockSpec((1, d), lambda i: (0, 0))
    return _call(
        body, name=name, grid=(t // tr,),
        ins=[h1b, w_up_t, conv_w, conv_b, w_down, h1, target, ln2_g, ln2_b],
        in_specs=[row(d), _resident((2 * D_FF, d)), _resident((3, D_FF)), _resident((1, D_FF)),
                  _resident((D_FF, d)), row(d), row(d), vec, vec],
        out_specs=[row(2 * D_FF), row(D_FF), row(d), vec, vec, pl.BlockSpec((1, LANES), lambda i: (0, 0))],
        out_shape=[jax.ShapeDtypeStruct((t, 2 * D_FF), F32), jax.ShapeDtypeStruct((t, D_FF), BF16),
                   jax.ShapeDtypeStruct((t, d), F32), jax.ShapeDtypeStruct((1, d), F32),
                   jax.ShapeDtypeStruct((1, d), F32), jax.ShapeDtypeStruct((1, LANES), F32)],
        scratch_shapes=[pltpu.VMEM((tr + HALO, D_FF), F32)],
        sem=("arbitrary",), comm=comm)


def _conv_bwd(d_hmid, u2, conv_w, conv_b, name, comm=None):
    t = u2.shape[0]
    tr = _pick(t, (CONV_TILE,))
    hb = tr // HALO
    last = t // HALO - 1
    rb = CONV_RB
    re = rb + HALO

    def body(gp_ref, gp_prev_ref, gp_next_ref, val_ref, val_next_ref, dh_ref, dh_next_ref, w_ref, b_ref,
             du_ref, dw_ref, dcb_ref, ext, dg_s):
        i = pl.program_id(0)

        @pl.when(i == 0)
        def _():
            dw_ref[...] = jnp.zeros_like(dw_ref)
            dcb_ref[...] = jnp.zeros_like(dcb_ref)

        ext[0:HALO, :] = jnp.where(i == 0, 0.0, gp_prev_ref[...])
        ext[HALO:HALO + tr, :] = gp_ref[...]
        ext[HALO + tr:, :] = gp_next_ref[...]
        next_in_seq = (i + 1) * tr < t
        for c in range(D_FF // LANES):
            ln = slice(c * LANES, (c + 1) * LANES)
            w0, w1, w2, bb = w_ref[0:1, ln], w_ref[1:2, ln], w_ref[2:3, ln], b_ref[:, ln]
            acc_b = jnp.zeros((8, LANES), F32)
            acc_w = [jnp.zeros((8, LANES), F32) for _ in range(3)]
            for r0 in range(0, tr, rb):
                g_m2 = ext[r0 + HALO - 2:r0 + HALO - 2 + re, ln]
                g_m1 = ext[r0 + HALO - 1:r0 + HALO - 1 + re, ln]
                g_0 = ext[r0 + HALO:r0 + HALO + re, ln]
                gate = g_m2 * w0 + g_m1 * w1 + g_0 * w2 + bb
                sg = _sig(gate)
                if r0 + re <= tr:
                    val = val_ref[r0:r0 + re, ln]
                    dh = dh_ref[r0:r0 + re, ln]
                else:
                    val = jnp.concatenate([val_ref[r0:r0 + rb, ln], val_next_ref[:, ln]], axis=0)
                    dh = jnp.concatenate([dh_ref[r0:r0 + rb, ln],
                                          jnp.where(next_in_seq, dh_next_ref[:, ln], 0.0)], axis=0)
                dgate = dh * val * _dsilu(gate, sg)
                dg_s[:, ln] = dgate
                dg0 = dgate[0:rb]
                d_gp = dg_s[2:2 + rb, ln] * w0 + dg_s[1:1 + rb, ln] * w1 + dg0 * w2
                d_val = dh[0:rb] * (gate[0:rb] * sg[0:rb])
                du_ref[r0:r0 + rb, ln] = d_gp.astype(du_ref.dtype)
                du_ref[r0:r0 + rb, D_FF + c * LANES:D_FF + (c + 1) * LANES] = d_val.astype(du_ref.dtype)
                acc_b = acc_b + _sum8(dg0)
                acc_w[0] = acc_w[0] + _sum8(dg0 * g_m2[0:rb])
                acc_w[1] = acc_w[1] + _sum8(dg0 * g_m1[0:rb])
                acc_w[2] = acc_w[2] + _sum8(dg0 * g_0[0:rb])
            dcb_ref[:, ln] += jnp.sum(acc_b, axis=0, keepdims=True)
            for j in range(3):
                dw_ref[j:j + 1, ln] += jnp.sum(acc_w[j], axis=0, keepdims=True)

    cur = lambda col: pl.BlockSpec((tr, D_FF), lambda i: (i, col))
    nxt = lambda col: pl.BlockSpec((HALO, D_FF), lambda i: (jnp.minimum((i + 1) * hb, last), col))
    return _call(
        body, name=name, grid=(t // tr,), ins=[u2, u2, u2, u2, u2, d_hmid, d_hmid, conv_w, conv_b],
        in_specs=[cur(0), pl.BlockSpec((HALO, D_FF), lambda i: (jnp.maximum(i * hb - 1, 0), 0)), nxt(0),
                  cur(1), nxt(1), cur(0), nxt(0),
                  pl.BlockSpec((3, D_FF), lambda i: (0, 0)), pl.BlockSpec((1, D_FF), lambda i: (0, 0))],
        out_specs=[pl.BlockSpec((tr, 2 * D_FF), lambda i: (i, 0)),
                   pl.BlockSpec((8, D_FF), lambda i: (0, 0)), pl.BlockSpec((1, D_FF), lambda i: (0, 0))],
        out_shape=[jax.ShapeDtypeStruct((t, 2 * D_FF), BF16), jax.ShapeDtypeStruct((8, D_FF), F32),
                   jax.ShapeDtypeStruct((1, D_FF), F32)],
        scratch_shapes=[pltpu.VMEM((tr + 2 * HALO, D_FF), F32), pltpu.VMEM((re, D_FF), F32)],
        sem=("arbitrary",), comm=comm)


def _adamw(w, g, m, v, name):
    rows, cols = w.shape
    tr = _pick(rows, (256, 128, 64, 32, 16, 8))

    def body(w_ref, g_ref, m_ref, v_ref, d_ref, nm_ref, nv_ref):
        d_ref[...], nm_ref[...], nv_ref[...] = _adamw_math(w_ref[...], g_ref[...], m_ref[...], v_ref[...])

    spec = pl.BlockSpec((tr, cols), lambda i: (i, 0))
    shp = jax.ShapeDtypeStruct((rows, cols), F32)
    return pl.pallas_call(
        body, name=name, grid=(rows // tr,),
        in_specs=[spec, spec, spec, spec], out_specs=[spec, spec, spec], out_shape=[shp, shp, shp],
        compiler_params=_cp("parallel"),
    )(w, g, m, v)


def _pad_rows(a, rows):
    return jnp.pad(a, ((0, rows - a.shape[0]), (0, 0)))


SMALL_LAYOUT = (("ln1_g", 1024), ("ln1_b", 1024), ("b_in", 2816), ("sinks", 8), ("hgrn_lb", 1024),
                ("hgrn_norm_g", 128), ("ln2_g", 1024), ("ln2_b", 1024), ("conv_b", 2816), ("loss", 1))
SMALL_SHAPES = {"ln1_g": (1, 1024), "ln1_b": (1, 1024), "b_in": (1, 2816), "sinks": (1, 8), "hgrn_lb": (2, 512),
                "hgrn_norm_g": (1, 128), "ln2_g": (1, 1024), "ln2_b": (1, 1024), "conv_b": (1, 2816),
                "loss": (1,)}


def _pack_small(parts):
    rows = []
    for name, size in SMALL_LAYOUT:
        flat = parts[name].reshape(-1).astype(F32)
        padded = -(-size // LANES) * LANES
        rows.append(jnp.pad(flat, (0, padded - size)).reshape(-1, LANES))
    return _pad_rows(jnp.concatenate(rows, axis=0), SMALL_ROWS)


def _unpack_small(pack):
    out, r = {}, 0
    for name, size in SMALL_LAYOUT:
        nrows = -(-size // LANES)
        out[name] = pack[r:r + nrows].reshape(-1)[:size].reshape(SMALL_SHAPES[name])
        r += nrows
    return out


def _own(full, rows):
    return lax.dynamic_slice_in_dim(full, _me() * rows, rows, axis=0)


def kernel(x, positions, ln1_g, ln1_b, w_in, b_in, sinks, hgrn_lb, hgrn_norm_g, w_o, ln2_g, ln2_b, w_up, conv_w, conv_b, w_down, loss_target, m_ln1_g, m_ln1_b, m_w_in, m_b_in, m_sinks, m_hgrn_lb, m_hgrn_norm_g, m_w_o, m_ln2_g, m_ln2_b, m_w_up, m_conv_w, m_conv_b, m_w_down, v_ln1_g, v_ln1_b, v_w_in, v_b_in, v_sinks, v_hgrn_lb, v_hgrn_norm_g, v_w_o, v_ln2_g, v_ln2_b, v_w_up, v_conv_w, v_conv_b, v_w_down):
    t = x.shape[1]
    x2 = x[0]
    target = loss_target[0]
    pos_col = positions.reshape(t, 1)

    w_in_t_s = w_in[0].T.astype(BF16)
    w_up_t_s = w_up[0].T.astype(BF16)
    w_o_s = w_o[0].astype(BF16)
    w_down_s = w_down[0].astype(BF16)
    (ctab, stab, xb), (w_in_t_g, cw_g) = _prep(
        pos_col, x2, "prep_ag_w_in", _Comm([{"kind": "gather", "arr": w_in_t_s}, {"kind": "gather", "arr": _pad_rows(conv_w[0], 8)}]))
    w_in_t = w_in_t_g.reshape(D_FF, D_MODEL)
    w_a_t, w_h_t = w_in_t[:UA_W], w_in_t[UA_W:]
    conv_w_f = cw_g[:, 0:3].transpose(1, 0, 2).reshape(3, D_FF)

    ua = _mm(xb, w_a_t, tb=True, bias=b_in[:, :UA_W], name="fwd_in_attn")
    uh, (w_down_g,) = _mm(xb, w_h_t, tb=True, bias=b_in[:, UA_W:], name="fwd_in_hgrn",
                          comm=_Comm([{"kind": "gather", "arr": w_down_s}]))
    w_down_f = w_down_g.reshape(D_FF, D_MODEL)
    half_up = SHARD_UP // 2
    (a_out, a_out_t), (w_o_g, w_up_half) = _attn_fwd(
        ua, ctab, stab, sinks, "attn_fwd",
        comm=_Comm([{"kind": "gather", "arr": w_o_s},
                    {"kind": "gather", "arr": w_up_t_s, "rows": (0, half_up), "dst_rows": SHARD_UP}]))
    (r_out, r_out_t, o_pre, states), (w_up_t_g,) = _hgrn_fwd(
        uh, hgrn_lb, hgrn_norm_g, "hgrn_fwd",
        comm=_Comm([{"kind": "gather", "arr": w_up_t_s, "rows": (half_up, half_up), "dst_rows": SHARD_UP,
                     "dst_first": half_up, "into": w_up_half}]))
    w_o_f = w_o_g.reshape(D_MODEL, D_MODEL)
    w_up_t = w_up_t_g.reshape(2 * D_FF, D_MODEL)
    z1 = _mm(a_out, w_o_f[:ATTN_W], addend=x2, addend_scale=ALPHA, name="fwd_o_attn")
    h1, h1b, xhat1, rstd1 = _mm_ln_fwd(r_out, w_o_f[ATTN_W:], z1, ln1_g, ln1_b, "fwd_o_hgrn_ln1")
    u2, hmid, dz2, d_ln2_g, d_ln2_b, loss_part = _ffn_fwd(h1b, h1, w_up_t, conv_w_f, conv_b, w_down_f, target,
                                                         ln2_g, ln2_b, "ffn_fwd")[0]

    d_hmid = _mm(dz2, w_down_f, tb=True, tn=1408, name="bwd_down_dx")
    d_w_down, d_w_down_b = _mm(hmid, dz2, ta=True, out_dtype2=BF16, tm=1408, tk=512, name="bwd_down_dw")
    (d_u2, d_conv_w8, d_conv_b), (recv_down,) = _conv_bwd(
        d_hmid, u2, conv_w_f, conv_b, "conv_bwd",
        comm=_Comm([{"kind": "exchange", "arr": d_w_down_b.reshape(N_DEV, SHARD_DOWN, D_MODEL)}]))
    dz1, d_ln1_g, d_ln1_b = _mm_ln_bwd(d_u2, w_up_t, dz2, xhat1, rstd1, ln1_g, "bwd_up_dx_ln1")
    d_w_up_t, d_w_up_t_b = _mm(d_u2, h1b, ta=True, out_dtype2=BF16, tm=1408, tk=512, name="bwd_up_dw")
    d_a = _mm(dz1, w_o_f[:ATTN_W], tb=True, name="bwd_o_dx_attn")
    d_r = _mm(dz1, w_o_f[ATTN_W:], tb=True, name="bwd_o_dx_hgrn")
    d_w_o_a, d_w_o_a_b = _mm(a_out_t, dz1, out_dtype2=BF16, name="bwd_o_dw_attn")
    d_w_o_r, d_w_o_r_b = _mm(r_out_t, dz1, out_dtype2=BF16, name="bwd_o_dw_hgrn")
    d_w_o = jnp.concatenate([d_w_o_a, d_w_o_r], axis=0)
    d_w_o_b = jnp.concatenate([d_w_o_a_b, d_w_o_r_b], axis=0)
    d_w_up_x = d_w_up_t_b.reshape(N_DEV, SHARD_UP, D_MODEL)
    half = SHARD_UP // 2
    d_cw_x = d_conv_w8.reshape(8, N_DEV, SHARD_IN).transpose(1, 0, 2)
    (d_ua, d_ua_t, d_bias_a, d_sinks), (recv_up_half, recv_o, recv_cw) = _attn_bwd(
        ua, d_a, ctab, stab, sinks, "attn_bwd",
        comm=_Comm([{"kind": "exchange", "arr": d_w_up_x, "rows": (0, half), "dst_rows": SHARD_UP},
                    {"kind": "exchange", "arr": d_w_o_b.reshape(N_DEV, SHARD_O, D_MODEL)},
                    {"kind": "exchange", "arr": d_cw_x}]))
    (d_uh, d_uh_t, d_bias_h, d_norm_g, d_lb8), (recv_up,) = _hgrn_bwd(
        uh, o_pre, d_r, states, hgrn_lb, hgrn_norm_g, "hgrn_bwd",
        comm=_Comm([{"kind": "exchange", "arr": d_w_up_x, "rows": (half, half), "dst_rows": SHARD_UP,
                     "dst_first": half, "into": recv_up_half}]))
    d_w_a_t, d_w_a_t_b = _mm(d_ua_t, xb, out_dtype2=BF16, name="bwd_in_dw_attn")
    d_w_h_t, d_w_h_t_b = _mm(d_uh_t, xb, out_dtype2=BF16, name="bwd_in_dw_hgrn")
    d_w_in_t = jnp.concatenate([d_w_a_t, d_w_h_t], axis=0)
    d_w_in_t_b = jnp.concatenate([d_w_a_t_b, d_w_h_t_b], axis=0)
    small_local = _pack_small({
        "ln1_g": d_ln1_g, "ln1_b": d_ln1_b, "b_in": jnp.concatenate([d_bias_a, d_bias_h], axis=1),
        "sinks": d_sinks[:, :8], "hgrn_lb": d_lb8[0:2], "hgrn_norm_g": d_norm_g, "ln2_g": d_ln2_g,
        "ln2_b": d_ln2_b, "conv_b": d_conv_b, "loss": loss_part[:, :1]})
    dx, (recv_in, small_g) = _mm(d_uh, w_h_t, addend=dz1, addend_scale=ALPHA, name="bwd_in_dx_hgrn",
                                 comm=_Comm([{"kind": "exchange", "arr": d_w_in_t_b.reshape(N_DEV, SHARD_IN, D_MODEL)},
                                             {"kind": "gather", "arr": small_local}]))
    dx = _mm(d_ua, w_a_t, addend=dx, tk=768, name="bwd_in_dx_attn")

    res_in = [r.T for r in _sum_shards_adamw([recv_in], _own(d_w_in_t, SHARD_IN), w_in[0].T, m_w_in[0].T,
                                             v_w_in[0].T, "adamw_w_in")]
    res_up = [r.T for r in _sum_shards_adamw([recv_up], _own(d_w_up_t, SHARD_UP), w_up[0].T,
                                             m_w_up[0].T, v_w_up[0].T, "adamw_w_up")]
    res_o = _sum_shards_adamw([recv_o], _own(d_w_o, SHARD_O), w_o[0], m_w_o[0], v_w_o[0], "adamw_w_o")
    res_down = _sum_shards_adamw([recv_down], _own(d_w_down, SHARD_DOWN), w_down[0], m_w_down[0], v_w_down[0],
                                 "adamw_w_down")
    g_cw = _sum_slots(recv_cw, "sum_conv_w")
    cw8 = lambda a: _pad_rows(a, 8)
    res_cw = (g_cw,) + tuple(_adamw(cw8(conv_w[0]), g_cw, cw8(m_conv_w[0]), cw8(v_conv_w[0]), "adamw_conv_w"))
    big = {"w_in": [r[None] for r in res_in], "w_up": [r[None] for r in res_up],
           "w_o": [r[None] for r in res_o], "w_down": [r[None] for r in res_down],
           "conv_w": [r[None, 0:3] for r in res_cw]}

    small_sum = _sum_slots(small_g, "ar_small_sum")
    gs = _unpack_small(small_sum)
    loss = gs["loss"][0]
    zero1 = jnp.zeros((1,), F32)
    w_small = _pack_small({"ln1_g": ln1_g, "ln1_b": ln1_b, "b_in": b_in, "sinks": sinks, "hgrn_lb": hgrn_lb,
                           "hgrn_norm_g": hgrn_norm_g, "ln2_g": ln2_g, "ln2_b": ln2_b, "conv_b": conv_b,
                           "loss": zero1})
    m_small = _pack_small({"ln1_g": m_ln1_g, "ln1_b": m_ln1_b, "b_in": m_b_in, "sinks": m_sinks,
                           "hgrn_lb": m_hgrn_lb, "hgrn_norm_g": m_hgrn_norm_g, "ln2_g": m_ln2_g,
                           "ln2_b": m_ln2_b, "conv_b": m_conv_b, "loss": zero1})
    v_small = _pack_small({"ln1_g": v_ln1_g, "ln1_b": v_ln1_b, "b_in": v_b_in, "sinks": v_sinks,
                           "hgrn_lb": v_hgrn_lb, "hgrn_norm_g": v_hgrn_norm_g, "ln2_g": v_ln2_g,
                           "ln2_b": v_ln2_b, "conv_b": v_conv_b, "loss": zero1})
    small = [gs] + [_unpack_small(p) for p in _adamw(w_small, small_sum, m_small, v_small, "adamw_small")]

    order = ["ln1_g", "ln1_b", "w_in", "b_in", "sinks", "hgrn_lb", "hgrn_norm_g", "w_o", "ln2_g", "ln2_b",
             "w_up", "conv_w", "conv_b", "w_down"]

    def pick(idx):
        return [big[n][idx] if n in big else small[idx][n] for n in order]

    return (loss, dx[None], *pick(0), *pick(1), *pick(2), *pick(3))
```

```python
import functools

import jax
import jax.numpy as jnp
import numpy as np
from jax import lax
from jax.experimental import pallas as pl
from jax.experimental.pallas import tpu as pltpu

F32 = jnp.float32
BF16 = jnp.bfloat16

N_DEV = 8
D_MODEL = 1024
D_FF = 2816
ATTN_W = 512
KV_W = 128
UA_W = ATTN_W + 2 * KV_W
UH_W = 2048
HG_W = 512
ATTN_BLOCK = 128
HGRN_CHUNK = 64
HGRN_SUB = 16
HGRN_CHUNKS_PER_STEP = 4
EXP_CLAMP = 85.0
NEG_BIG = -1e30
LN_EPS = 1e-5
RMS_EPS = 1e-6
ALPHA = 2.0 ** 0.25
ATTN_SCALE = 0.125
ROPE_THETA = 500000.0

ADAM_LR = 0.001
ADAM_B1 = 0.9
ADAM_B2 = 0.999
ADAM_EPS = 1e-08
ADAM_WD = 0.01
ADAM_STEP = 10

LANES = 128
VMEM_LIMIT_BYTES = 56 * 1024 * 1024

SHARD_IN = D_FF // N_DEV
SHARD_UP = 2 * D_FF // N_DEV
SHARD_O = D_MODEL // N_DEV
SHARD_DOWN = D_FF // N_DEV
SMALL_ROWS = 88

_MESH = pl.DeviceIdType.MESH
_NT = (((1,), (1,)), ((), ()))
_NN = (((1,), (0,)), ((), ()))
_TN = (((0,), (0,)), ((), ()))


def _cp(*sem):
    if sem:
        return pltpu.CompilerParams(dimension_semantics=sem, vmem_limit_bytes=VMEM_LIMIT_BYTES)
    return pltpu.CompilerParams(vmem_limit_bytes=VMEM_LIMIT_BYTES)


def _sig(x):
    return 0.5 * jnp.tanh(0.5 * x) + 0.5


def _dsilu(x, s):
    return s * (1.0 + x * (1.0 - s))


def _dot(a, b, dims):
    return lax.dot_general(a.astype(BF16), b.astype(BF16), dims, preferred_element_type=F32)


def _split(a):
    hi = a.astype(BF16)
    return hi, (a - hi.astype(F32)).astype(BF16)


def _dot3(a, b, dims):
    ah, al = _split(a)
    bh, bl = _split(b)
    d = functools.partial(lax.dot_general, dimension_numbers=dims, preferred_element_type=F32)
    return d(ah, bh) + (d(ah, bl) + d(al, bh))


def _pick(n, pref):
    for t in pref:
        if t <= n and n % t == 0:
            return t
    return n


def _my_coords():
    return lax.axis_index("x"), lax.axis_index("y"), lax.axis_index("c")


def _peer(k):
    x, y, c = _my_coords()
    return (1 - x if k & 4 else x, 1 - y if k & 2 else y, 1 - c if k & 1 else c)


def _me():
    x, y, c = _my_coords()
    return 4 * x + 2 * y + c


class _Comm:
    def __init__(self, items):
        self.items = []
        for it in items:
            arr = it["arr"]
            full = arr.shape[0] if it["kind"] == "gather" else arr.shape[1]
            first, count = it.get("rows", (0, full))
            self.items.append(dict(kind=it["kind"], arr=arr, first=first, count=count,
                                   dst_rows=it.get("dst_rows", count), dst_first=it.get("dst_first", 0),
                                   into=it.get("into")))
        self.n = len(self.items)
        self.arrays = [it["arr"] for it in self.items]
        self.intos = [(a, it["into"]) for a, it in enumerate(self.items) if it["into"] is not None]

    def out_shapes(self):
        return [jax.ShapeDtypeStruct((N_DEV, it["dst_rows"], it["arr"].shape[-1]), it["arr"].dtype)
                for it in self.items]

    def specs(self, n=None):
        return [pl.BlockSpec(memory_space=pl.ANY)] * (self.n if n is None else n)

    def scratch(self):
        return [pltpu.SemaphoreType.DMA(((N_DEV - 1) * self.n,)), pltpu.SemaphoreType.DMA(((N_DEV - 1) * self.n,)),
                pltpu.SemaphoreType.DMA((self.n,))]

    def _src(self, a, ref, dev):
        it = self.items[a]
        blk = ref if it["kind"] == "gather" else ref.at[dev]
        return blk.at[pl.ds(it["first"], it["count"])]

    def _dst(self, a, ref, slot):
        it = self.items[a]
        return ref.at[slot].at[pl.ds(it["dst_first"], it["count"])]

    def _copy(self, a, k, src, dst, sems, me, slot):
        other = jnp.bitwise_xor(me, k)
        idx = a * (N_DEV - 1) + k - 1
        return pltpu.make_async_remote_copy(
            src_ref=self._src(a, src, other), dst_ref=self._dst(a, dst, me if slot == "mine" else other),
            send_sem=sems[0].at[idx], recv_sem=sems[1].at[idx], device_id=_peer(k), device_id_type=_MESH)

    def _pass_on(self, a, k, dst, sems, me):
        slot = self._dst(a, dst, jnp.bitwise_xor(me, k))
        idx = a * (N_DEV - 1) + k
        return pltpu.make_async_remote_copy(
            src_ref=slot, dst_ref=slot, send_sem=sems[0].at[idx], recv_sem=sems[1].at[idx],
            device_id=_peer(1), device_id_type=_MESH)

    def _local(self, a, src, dst, sems, me):
        return pltpu.make_async_copy(self._src(a, src, me), self._dst(a, dst, me), sems[2].at[a])

    def start(self, srcs, dsts, sems):
        me = _me()
        for a, (src, dst) in enumerate(zip(srcs, dsts)):
            direct = (1, 2, 4, 6) if self.items[a]["kind"] == "gather" else range(1, N_DEV)
            self._local(a, src, dst, sems, me).start()
            for k in direct:
                self._copy(a, k, src, dst, sems, me, "mine").start()

    def wait(self, srcs, dsts, sems):
        me = _me()
        for a, (src, dst) in enumerate(zip(srcs, dsts)):
            if self.items[a]["kind"] == "gather":
                for k in (2, 4, 6):
                    self._copy(a, k, src, dst, sems, me, "theirs").wait_recv()
                    self._pass_on(a, k, dst, sems, me).start()
                for k in (1, 3, 5, 7):
                    self._copy(a, k, src, dst, sems, me, "theirs").wait_recv()
                for k in (1, 2, 4, 6):
                    self._copy(a, k, src, dst, sems, me, "mine").wait_send()
                for k in (2, 4, 6):
                    self._pass_on(a, k, dst, sems, me).wait_send()
            else:
                for k in range(1, N_DEV):
                    self._copy(a, k, src, dst, sems, me, "theirs").wait_recv()
                for k in range(1, N_DEV):
                    self._copy(a, k, src, dst, sems, me, "mine").wait_send()
            self._local(a, src, dst, sems, me).wait()


def _call(body, *, name, grid, ins, in_specs, out_specs, out_shape, scratch_shapes=(), sem, comm=None):
    n_in, n_out, n_scr = len(ins), len(out_shape), len(scratch_shapes)
    if comm is None:
        outs = pl.pallas_call(
            body, name=name, grid=grid, in_specs=list(in_specs), out_specs=list(out_specs),
            out_shape=list(out_shape), scratch_shapes=list(scratch_shapes), compiler_params=_cp(*sem))(*ins)
        return list(outs), []
    nc, n_into = comm.n, len(comm.intos)

    def hosted(*refs):
        pos = n_in
        c_in = refs[pos:pos + nc]
        pos += nc + n_into
        outs = refs[pos:pos + n_out]
        pos += n_out
        c_out = refs[pos:pos + nc]
        pos += nc
        scr = refs[pos:pos + n_scr]
        sems = refs[pos + n_scr:]
        ids = [pl.program_id(d) for d in range(len(grid))]
        first = functools.reduce(jnp.logical_and, [i == 0 for i in ids])
        last = functools.reduce(jnp.logical_and, [i == g - 1 for i, g in zip(ids, grid)])

        @pl.when(first)
        def _():
            comm.start(c_in, c_out, sems)

        body(*refs[:n_in], *outs, *scr)

        @pl.when(last)
        def _():
            comm.wait(c_in, c_out, sems)

    aliases = {n_in + nc + j: n_out + a for j, (a, _) in enumerate(comm.intos)}
    outs = pl.pallas_call(
        hosted, name=name, grid=grid, in_specs=list(in_specs) + comm.specs() + comm.specs(n_into),
        out_specs=list(out_specs) + comm.specs(), out_shape=list(out_shape) + comm.out_shapes(),
        scratch_shapes=list(scratch_shapes) + comm.scratch(), input_output_aliases=aliases,
        compiler_params=_cp(*(["arbitrary"] * len(grid))))(*ins, *comm.arrays, *[arr for _, arr in comm.intos])
    return list(outs[:n_out]), list(outs[n_out:])


def _sum_slots(gathered, name):
    _, rows, cols = gathered.shape

    def body(g_ref, out_ref):
        acc = g_ref[0]
        for s in range(1, N_DEV):
            acc = acc + g_ref[s]
        out_ref[...] = acc

    return pl.pallas_call(
        body, name=name,
        out_shape=jax.ShapeDtypeStruct((rows, cols), F32),
        compiler_params=_cp(),
    )(gathered)


def _slot_sum(recv_ref, own_ref, shape):
    me = _me()
    acc = jnp.zeros(shape, F32)
    for s in range(N_DEV):
        acc = acc + jnp.where(me == s, own_ref[...], recv_ref[s].astype(F32))
    return acc


def _adamw_math(w, g, m, v):
    nm = ADAM_B1 * m + (1.0 - ADAM_B1) * g
    nv = ADAM_B2 * v + (1.0 - ADAM_B2) * (g * g)
    m_hat = nm / (1.0 - ADAM_B1 ** ADAM_STEP)
    v_hat = nv / (1.0 - ADAM_B2 ** ADAM_STEP)
    return -ADAM_LR * (m_hat / (jnp.sqrt(v_hat) + ADAM_EPS) + ADAM_WD * w), nm, nv


def _sum_shards_adamw(recvs, own, w, m, v, name):
    rows_p, cols = recvs[0].shape[1], recvs[0].shape[2]
    n_p = len(recvs)
    tr = _pick(rows_p, (176, 128, 64, 32, 16, 8))
    tiles = rows_p // tr

    def body(*refs):
        recv_refs = refs[:n_p]
        own_ref, w_ref, m_ref, v_ref, g_ref, d_ref, nm_ref, nv_ref = refs[n_p:]
        for j in range(n_p):
            @pl.when(pl.program_id(0) == j)
            def _():
                g = _slot_sum(recv_refs[j], own_ref, (tr, cols))
                g_ref[...] = g
                d_ref[...], nm_ref[...], nv_ref[...] = _adamw_math(w_ref[...], g, m_ref[...], v_ref[...])

    spec = pl.BlockSpec((tr, cols), lambda p_, i: (p_ * tiles + i, 0))
    shp = jax.ShapeDtypeStruct((rows_p * n_p, cols), F32)
    return pl.pallas_call(
        body, name=name, grid=(n_p, tiles),
        in_specs=[pl.BlockSpec((N_DEV, tr, cols), functools.partial(lambda p_, i, j: (0, jnp.where(p_ == j, i, 0), 0), j=j))
                  for j in range(n_p)] + [spec, spec, spec, spec],
        out_specs=[spec, spec, spec, spec], out_shape=[shp, shp, shp, shp],
        compiler_params=_cp("arbitrary", "arbitrary"),
    )(*recvs, own, w, m, v)


def _mm(a, b, *, name, ta=False, tb=False, out_dtype=F32, out_dtype2=None, bias=None, addend=None,
        addend_scale=1.0, tm=1024, tn=1024, tk=1024, comm=None):
    kdim, m = a.shape if ta else a.shape[::-1]
    n = b.shape[0] if tb else b.shape[1]
    tm = _pick(m, (tm, 1408, 1024, 768, 512, 256, 128))
    tn = _pick(n, (tn, 1408, 1024, 768, 512, 256, 128))
    tk = _pick(kdim, (tk, 1408, 1024, 768, 512, 256, 128))
    nk = kdim // tk
    a_spec = pl.BlockSpec((tk, tm), lambda i, j, k: (k, i)) if ta else pl.BlockSpec((tm, tk), lambda i, j, k: (i, k))
    b_spec = pl.BlockSpec((tn, tk), lambda i, j, k: (j, k)) if tb else pl.BlockSpec((tk, tn), lambda i, j, k: (k, j))
    ins, specs = [a, b], [a_spec, b_spec]
    if bias is not None:
        ins.append(bias)
        specs.append(pl.BlockSpec((1, tn), lambda i, j, k: (0, j)))
    if addend is not None:
        ins.append(addend)
        specs.append(pl.BlockSpec((tm, tn), lambda i, j, k: (i, j)))
    dims = (((0,) if ta else (1,), (1,) if tb else (0,)), ((), ()))
    has_bias, has_addend, two = bias is not None, addend is not None, out_dtype2 is not None

    def body(*refs):
        a_ref, b_ref = refs[0], refs[1]
        pos = 2
        bias_ref = addend_ref = None
        if has_bias:
            bias_ref = refs[pos]
            pos += 1
        if has_addend:
            addend_ref = refs[pos]
            pos += 1
        o_refs, acc_ref = refs[pos:-1], refs[-1]
        k = pl.program_id(2)

        @pl.when(k == 0)
        def _():
            acc_ref[...] = jnp.zeros_like(acc_ref)

        acc_ref[...] += _dot(a_ref[...], b_ref[...], dims)

        @pl.when(k == nk - 1)
        def _():
            r = acc_ref[...]
            if has_bias:
                r = r + bias_ref[...]
            if has_addend:
                r = r + addend_scale * addend_ref[...].astype(F32)
            for o_ref in o_refs:
                o_ref[...] = r.astype(o_ref.dtype)

    ospec = pl.BlockSpec((tm, tn), lambda i, j, k: (i, j))
    dtypes = [out_dtype] + ([out_dtype2] if two else [])
    outs, couts = _call(
        body, name=name, grid=(m // tm, n // tn, nk), ins=ins, in_specs=specs,
        out_specs=[ospec] * len(dtypes), out_shape=[jax.ShapeDtypeStruct((m, n), d) for d in dtypes],
        scratch_shapes=[pltpu.VMEM((tm, tn), F32)], sem=("parallel", "parallel", "arbitrary"), comm=comm)
    primary = tuple(outs) if two else outs[0]
    return (primary, couts) if comm is not None else primary


def _rope_lane_constants():
    inv_freq = np.float32(ROPE_THETA) ** (-np.arange(8, dtype=np.float32) * np.float32(2.0 / 16.0))
    lane = np.arange(LANES) % 64
    freq = np.where(lane < 16, inv_freq[lane % 8], 0.0).astype(np.float32)
    sign = np.where(lane < 8, -1.0, np.where(lane < 16, 1.0, 0.0)).astype(np.float32)
    return jnp.asarray(freq)[None, :], jnp.asarray(sign)[None, :]


def _prep(pos_col, x2, name, comm):
    t, d = x2.shape
    tr = _pick(t, (512, 256, 128))
    freq, sign = _rope_lane_constants()

    def body(pos_ref, freq_ref, sign_ref, x_ref, c_ref, s_ref, xb_ref):
        ang = pos_ref[...].astype(F32) * freq_ref[...]
        c_ref[...] = jnp.cos(ang)
        s_ref[...] = sign_ref[...] * jnp.sin(ang)
        xb_ref[...] = x_ref[...].astype(BF16)

    tab = pl.BlockSpec((tr, LANES), lambda i: (i, 0))
    return _call(
        body, name=name, grid=(t // tr,), ins=[pos_col, freq, sign, x2],
        in_specs=[pl.BlockSpec((tr, 1), lambda i: (i, 0)), pl.BlockSpec((1, LANES), lambda i: (0, 0)),
                  pl.BlockSpec((1, LANES), lambda i: (0, 0)), pl.BlockSpec((tr, d), lambda i: (i, 0))],
        out_specs=[tab, tab, pl.BlockSpec((tr, d), lambda i: (i, 0))],
        out_shape=[jax.ShapeDtypeStruct((t, LANES), F32), jax.ShapeDtypeStruct((t, LANES), F32),
                   jax.ShapeDtypeStruct((t, d), BF16)],
        sem=("parallel",), comm=comm)


def _swap8(t):
    width = t.shape[1]
    lane = jnp.bitwise_and(lax.broadcasted_iota(jnp.int32, t.shape, 1), 63)
    return jnp.where(lane < 8, pltpu.roll(t, width - 8, 1), jnp.where(lane < 16, pltpu.roll(t, 8, 1), 0.0))


def _rope(t, c, s):
    return t * c + _swap8(t) * s


def _rope_bwd(d, c, s):
    return d * c + _swap8(d * s)


def _tile4(a):
    return jnp.concatenate([a, a, a, a], axis=1)


def _attn_band(n, k_cur, k_prev, v_cur, v_prev, c_cur, s_cur, c_prev, s_prev):
    kband = jnp.concatenate([_rope(k_prev, c_prev, s_prev), _rope(k_cur, c_cur, s_cur)], axis=0)
    vband = jnp.concatenate([v_prev, v_cur], axis=0)
    qi = lax.broadcasted_iota(jnp.int32, (ATTN_BLOCK, 2 * ATTN_BLOCK), 0)
    kj = lax.broadcasted_iota(jnp.int32, (ATTN_BLOCK, 2 * ATTN_BLOCK), 1)
    dist = qi + ATTN_BLOCK - kj
    valid = (dist >= 0) & (dist < ATTN_BLOCK) & (n * ATTN_BLOCK - ATTN_BLOCK + kj >= 0)
    return (kband.astype(BF16), pltpu.roll(kband, 64, 1).astype(BF16),
            vband.astype(BF16), pltpu.roll(vband, 64, 1).astype(BF16), valid)


def _attn_probs(raw, valid, sink, axis):
    s = jnp.where(valid, raw * ATTN_SCALE, NEG_BIG)
    m = jnp.maximum(jnp.max(s, axis=axis, keepdims=True), sink)
    p = jnp.exp(s - m)
    esink = jnp.exp(sink - m)
    z = jnp.sum(p, axis=axis, keepdims=True) + esink
    return p / z, esink / z


def _attn_valid_t(n):
    kj = lax.broadcasted_iota(jnp.int32, (2 * ATTN_BLOCK, ATTN_BLOCK), 0)
    qi = lax.broadcasted_iota(jnp.int32, (2 * ATTN_BLOCK, ATTN_BLOCK), 1)
    dist = qi + ATTN_BLOCK - kj
    return (dist >= 0) & (dist < ATTN_BLOCK) & (n * ATTN_BLOCK - ATTN_BLOCK + kj >= 0)


def _attn_specs(nb):
    def cur(col, width=KV_W):
        return pl.BlockSpec((ATTN_BLOCK, width), lambda n: (jnp.minimum(n, nb - 1), col))

    def prev(col):
        return pl.BlockSpec((ATTN_BLOCK, KV_W), lambda n: (jnp.maximum(n - 1, 0), col))

    ua_specs = [cur(0, ATTN_W), cur(4), prev(4), cur(5), prev(5)]
    tab_specs = [cur(0), cur(0), prev(0), prev(0)]
    return ua_specs, tab_specs


def _attn_fwd(ua, ctab, stab, sinks, name, comm=None):
    t = ua.shape[0]
    nb = t // ATTN_BLOCK
    ua_specs, tab_specs = _attn_specs(nb)

    def body(q_ref, kc_ref, kp_ref, vc_ref, vp_ref, cc_ref, sc_ref, cp_ref, sp_ref, sink_ref, o_ref, o_t_ref):
        n = pl.program_id(0)
        cc, sc = cc_ref[...], sc_ref[...]
        kb, kb_r, vb, vb_r, valid = _attn_band(n, kc_ref[...], kp_ref[...], vc_ref[...], vp_ref[...],
                                               cc, sc, cp_ref[...], sp_ref[...])
        qr = _rope(q_ref[...], _tile4(cc), _tile4(sc))
        lo = lax.broadcasted_iota(jnp.int32, (ATTN_BLOCK, LANES), 1) < 64
        heads = []
        for j in range(4):
            qj = qr[:, j * LANES:(j + 1) * LANES]
            for is_lo in (True, False):
                aligned = is_lo == (j < 2)
                qm = jnp.where(lo if is_lo else jnp.logical_not(lo), qj, 0.0).astype(BF16)
                raw = lax.dot_general(qm, kb if aligned else kb_r, _NT, preferred_element_type=F32)
                heads.append((raw, vb if aligned else vb_r, sink_ref[0, len(heads)]))
        halves = []
        for raw, vv, sink in heads:
            probs, _ = _attn_probs(raw, valid, sink, 1)
            halves.append(lax.dot_general(probs.astype(BF16), vv, _NN, preferred_element_type=F32))
        outs = [jnp.where(lo, halves[2 * j], halves[2 * j + 1]) for j in range(4)]
        o_ref[...] = jnp.concatenate(outs, axis=1).astype(o_ref.dtype)
        for j in range(4):
            o_t_ref[j * LANES:(j + 1) * LANES, :] = outs[j].T.astype(o_t_ref.dtype)

    return _call(
        body, name=name, grid=(nb,), ins=[ua, ua, ua, ua, ua, ctab, stab, ctab, stab, sinks],
        in_specs=ua_specs + tab_specs + [pl.BlockSpec(memory_space=pltpu.SMEM)],
        out_specs=[pl.BlockSpec((ATTN_BLOCK, ATTN_W), lambda n: (n, 0)),
                   pl.BlockSpec((ATTN_W, ATTN_BLOCK), lambda n: (0, n))],
        out_shape=[jax.ShapeDtypeStruct((t, ATTN_W), BF16), jax.ShapeDtypeStruct((ATTN_W, t), BF16)],
        sem=("parallel",), comm=comm)


def _attn_bwd(ua, d_out, ctab, stab, sinks, name, comm=None):
    t = ua.shape[0]
    nb = t // ATTN_BLOCK
    ua_specs, tab_specs = _attn_specs(nb)

    def body(q_ref, kc_ref, kp_ref, vc_ref, vp_ref, cc_ref, sc_ref, cp_ref, sp_ref, do_ref, sink_ref,
             dua_ref, dua_t_ref, dbias_ref, dsink_ref, dq_c, dk_c, dv_c, dq_n, dk_n, dv_n):
        n = pl.program_id(0)

        @pl.when(n == 0)
        def _():
            dq_c[...] = jnp.zeros_like(dq_c)
            dk_c[...] = jnp.zeros_like(dk_c)
            dv_c[...] = jnp.zeros_like(dv_c)
            dbias_ref[...] = jnp.zeros_like(dbias_ref)
            dsink_ref[...] = jnp.zeros_like(dsink_ref)

        @pl.when(n == nb)
        def _():
            dq_n[...] = jnp.zeros_like(dq_n)
            dk_n[...] = jnp.zeros_like(dk_n)
            dv_n[...] = jnp.zeros_like(dv_n)

        @pl.when(n < nb)
        def _():
            cc, sc = cc_ref[...], sc_ref[...]
            kb, kb_r, vb, vb_r, valid = _attn_band(n, kc_ref[...], kp_ref[...], vc_ref[...], vp_ref[...],
                                                   cc, sc, cp_ref[...], sp_ref[...])
            valid_t = _attn_valid_t(n)
            c4, s4 = _tile4(cc), _tile4(sc)
            qr = _rope(q_ref[...], c4, s4)
            do = do_ref[...].astype(F32)
            lane = lax.broadcasted_iota(jnp.int32, (ATTN_BLOCK, LANES), 1)
            lo = lane < 64
            lane_row = lax.broadcasted_iota(jnp.int32, (1, LANES), 1)
            heads = []
            for j in range(4):
                qj = qr[:, j * LANES:(j + 1) * LANES]
                doj = do[:, j * LANES:(j + 1) * LANES]
                for is_lo in (True, False):
                    aligned = is_lo == (j < 2)
                    msk = lo if is_lo else jnp.logical_not(lo)
                    kk = kb if aligned else kb_r
                    vv = vb if aligned else vb_r
                    qm = jnp.where(msk, qj, 0.0).astype(BF16)
                    dom = jnp.where(msk, doj, 0.0).astype(BF16)
                    heads.append(dict(
                        aligned=aligned, kk=kk, qm=qm, dom=dom, sink=sink_ref[0, len(heads)],
                        raw=lax.dot_general(qm, kk, _NT, preferred_element_type=F32),
                        dp=lax.dot_general(dom, vv, _NT, preferred_element_type=F32),
                        raw_t=lax.dot_general(kk, qm, _NT, preferred_element_type=F32),
                        dp_t=lax.dot_general(vv, dom, _NT, preferred_element_type=F32)))
            dk_band = jnp.zeros((2 * ATTN_BLOCK, LANES), F32)
            dv_band = jnp.zeros((2 * ATTN_BLOCK, LANES), F32)
            dsink = jnp.zeros((1, LANES), F32)
            halves = []
            for head, hd in enumerate(heads):
                probs, psink = _attn_probs(hd["raw"], valid, hd["sink"], 1)
                delta = jnp.sum(probs * hd["dp"], axis=1, keepdims=True)
                ds = (probs * (hd["dp"] - delta) * ATTN_SCALE).astype(BF16)
                dsink = dsink + jnp.where(lane_row == head, -jnp.sum(psink * delta), 0.0)
                halves.append(lax.dot_general(ds, hd["kk"], _NN, preferred_element_type=F32))
                probs_t, _ = _attn_probs(hd["raw_t"], valid_t, hd["sink"], 0)
                delta_t = jnp.sum(probs_t * hd["dp_t"], axis=0, keepdims=True)
                ds_t = (probs_t * (hd["dp_t"] - delta_t) * ATTN_SCALE).astype(BF16)
                dk_h = lax.dot_general(ds_t, hd["qm"], _NN, preferred_element_type=F32)
                dv_h = lax.dot_general(probs_t.astype(BF16), hd["dom"], _NN, preferred_element_type=F32)
                if not hd["aligned"]:
                    dk_h = pltpu.roll(dk_h, 64, 1)
                    dv_h = pltpu.roll(dv_h, 64, 1)
                dk_band = dk_band + dk_h
                dv_band = dv_band + dv_h
            dqs = [jnp.where(lo, halves[2 * j], halves[2 * j + 1]) for j in range(4)]
            dq_n[...] = _rope_bwd(jnp.concatenate(dqs, axis=1), c4, s4)
            dk_n[...] = dk_band
            dv_n[...] = dv_band
            dsink_ref[...] += dsink

        dk_prev = _rope_bwd(dk_c[...] + dk_n[0:ATTN_BLOCK, :], cp_ref[...], sp_ref[...])
        dv_prev = dv_c[...] + dv_n[0:ATTN_BLOCK, :]
        full = jnp.concatenate([dq_c[...], dk_prev, dv_prev], axis=1)
        dua_ref[...] = full.astype(dua_ref.dtype)
        for j in range(UA_W // LANES):
            dua_t_ref[j * LANES:(j + 1) * LANES, :] = full[:, j * LANES:(j + 1) * LANES].T.astype(dua_t_ref.dtype)
        dbias_ref[...] += jnp.sum(full, axis=0, keepdims=True)
        dq_c[...] = dq_n[...]
        dk_c[...] = dk_n[ATTN_BLOCK:, :]
        dv_c[...] = dv_n[ATTN_BLOCK:, :]

    return _call(
        body, name=name, grid=(nb + 1,), ins=[ua, ua, ua, ua, ua, ctab, stab, ctab, stab, d_out, sinks],
        in_specs=ua_specs + tab_specs + [
            pl.BlockSpec((ATTN_BLOCK, ATTN_W), lambda n: (jnp.minimum(n, nb - 1), 0)),
            pl.BlockSpec(memory_space=pltpu.SMEM)],
        out_specs=[pl.BlockSpec((ATTN_BLOCK, UA_W), lambda n: (jnp.maximum(n - 1, 0), 0)),
                   pl.BlockSpec((UA_W, ATTN_BLOCK), lambda n: (0, jnp.maximum(n - 1, 0))),
                   pl.BlockSpec((1, UA_W), lambda n: (0, 0)),
                   pl.BlockSpec((1, LANES), lambda n: (0, 0))],
        out_shape=[jax.ShapeDtypeStruct((t, UA_W), BF16), jax.ShapeDtypeStruct((UA_W, t), BF16),
                   jax.ShapeDtypeStruct((1, UA_W), F32),
                   jax.ShapeDtypeStruct((1, LANES), F32)],
        scratch_shapes=[pltpu.VMEM((ATTN_BLOCK, ATTN_W), F32), pltpu.VMEM((ATTN_BLOCK, KV_W), F32),
                        pltpu.VMEM((ATTN_BLOCK, KV_W), F32), pltpu.VMEM((ATTN_BLOCK, ATTN_W), F32),
                        pltpu.VMEM((2 * ATTN_BLOCK, KV_W), F32), pltpu.VMEM((2 * ATTN_BLOCK, KV_W), F32)],
        sem=("arbitrary",), comm=comm)


def _tri_mats():
    r = lax.broadcasted_iota(jnp.int32, (HGRN_CHUNK, LANES), 0)
    c = lax.broadcasted_iota(jnp.int32, (HGRN_CHUNK, LANES), 1)
    lower = ((c <= r) & (c < HGRN_CHUNK)).astype(F32)
    upper = ((c >= r) & (c < HGRN_CHUNK)).astype(F32)
    return lower, upper


def _tri_apply(tri, g):
    pad = jnp.concatenate([g, jnp.zeros_like(g)], axis=0)
    return lax.dot_general(tri, pad, _NN, precision=lax.Precision.HIGHEST, preferred_element_type=F32)


def _sub_masks():
    s = lax.broadcasted_iota(jnp.int32, (HGRN_CHUNK, LANES), 0)
    tt = lax.broadcasted_iota(jnp.int32, (HGRN_CHUNK, LANES), 1)
    return [(tt >= HGRN_SUB * i) & (tt < HGRN_SUB * (i + 1)) & (s <= tt) for i in range(HGRN_CHUNK // HGRN_SUB)]


def _hgrn_gates(hq, hf, lb_ref, b_scr):
    lb = _sig(lb_ref[0:1, :] - lb_ref[1:2, :])
    q = hq * _sig(hq)
    sg = _sig(hf)
    f = lb + (1.0 - lb) * sg
    k = 1.0 - f
    lower, _ = _tri_mats()
    b = _tri_apply(lower, jnp.log(f))
    b_scr[...] = b
    nsub = HGRN_CHUNK // HGRN_SUB
    starts = [jnp.zeros((1, HG_W), F32)] + [b_scr[HGRN_SUB * i - 1:HGRN_SUB * i, :] for i in range(1, nsub)]
    pq = jnp.concatenate([jnp.broadcast_to(p, (HGRN_SUB, HG_W)) for p in starts], axis=0)
    b_last = b_scr[HGRN_CHUNK - 1:HGRN_CHUNK, :]
    e_q = jnp.exp(b - pq)
    e_k = [jnp.exp(jnp.minimum(p - b, EXP_CLAMP)) for p in starts]
    e_b = jnp.exp(b)
    e_bl = jnp.exp(b_last - b)
    e_last = jnp.exp(b_last)
    return q, sg, f, k, lb, e_q, e_k, e_b, e_bl, e_last


def _sub_masks_ts():
    tt = lax.broadcasted_iota(jnp.int32, (HGRN_CHUNK, LANES), 0)
    s = lax.broadcasted_iota(jnp.int32, (HGRN_CHUNK, LANES), 1)
    return [(tt >= HGRN_SUB * i) & (tt < HGRN_SUB * (i + 1)) & (s <= tt) for i in range(HGRN_CHUNK // HGRN_SUB)]


def _masked_sum(blocks, masks, axis):
    step = HGRN_CHUNK if axis == 0 else LANES
    acc = jnp.zeros((HGRN_CHUNK, LANES), F32)
    for i, msk in enumerate(masks):
        blk = blocks[step * i:step * (i + 1), :] if axis == 0 else blocks[:, step * i:step * (i + 1)]
        acc = acc + jnp.where(msk, blk, 0.0)
    return acc


def _store_transposed(out_t_ref, chunk_rows):
    width = chunk_rows[0].shape[1]
    if len(chunk_rows) == 1:
        groups = [jnp.concatenate([chunk_rows[0], jnp.zeros_like(chunk_rows[0])], axis=0)]
    else:
        groups = [jnp.concatenate(chunk_rows[g:g + 2], axis=0) for g in range(0, len(chunk_rows), 2)]
    for g, rows in enumerate(groups):
        for c in range(width // LANES):
            tile = rows[:, c * LANES:(c + 1) * LANES].T.astype(out_t_ref.dtype)
            if len(chunk_rows) == 1:
                out_t_ref[c * LANES:(c + 1) * LANES, :] = tile[:, 0:HGRN_CHUNK]
            else:
                out_t_ref[c * LANES:(c + 1) * LANES, g * LANES:(g + 1) * LANES] = tile


def _hgrn_chunk_inputs(j, hq_ref, hf_ref, hi_ref, hg_ref, lb_ref, b_scr):
    rows = slice(j * HGRN_CHUNK, (j + 1) * HGRN_CHUNK)
    hq, hf, v, hg = hq_ref[rows, :], hf_ref[rows, :], hi_ref[rows, :], hg_ref[rows, :]
    q, sg, f, k, lb, e_q, e_k, e_b, e_bl, e_last = _hgrn_gates(hq, hf, lb_ref, b_scr.at[j])
    return dict(rows=rows, hq=hq, v=v, hg=hg, q=q, sg=sg, f=f, k=k, lb=lb, e_q=e_q, e_k=e_k, e_b=e_b, e_bl=e_bl,
                e_last=e_last, qt=q * e_q, qb=q * e_b, kd=k * e_bl, khat=[k * e for e in e_k])


def _hgrn_fwd(uh, lb_raw, norm_g, name, comm=None):
    t = uh.shape[0]
    nc = t // HGRN_CHUNK
    cps = _pick(nc, (HGRN_CHUNKS_PER_STEP, 2, 1))
    rows_step = cps * HGRN_CHUNK

    def body(hq_ref, hf_ref, hi_ref, hg_ref, lb_ref, ng_ref, r_ref, r_t_ref, o_ref, st_out_ref, st_ref, b_scr):
        @pl.when(pl.program_id(0) == 0)
        def _():
            st_ref[...] = jnp.zeros_like(st_ref)

        masks = _sub_masks_ts()
        ng = ng_ref[...]
        zpad = jnp.zeros((HGRN_CHUNK, LANES), F32)
        heads = [slice(h * LANES, (h + 1) * LANES) for h in range(4)]
        chunks = [_hgrn_chunk_inputs(j, hq_ref, hf_ref, hi_ref, hg_ref, lb_ref, b_scr) for j in range(cps)]
        for ch in chunks:
            ch["scores"] = [_dot3(ch["qt"][:, sl],
                                  jnp.concatenate([x for kh in ch["khat"] for x in (kh[:, sl], zpad)], axis=0), _NT)
                            for sl in heads]
        for j, ch in enumerate(chunks):
            o_heads, y_heads = [], []
            for h, sl in enumerate(heads):
                a_ts = _masked_sum(ch["scores"][h], masks, 1)
                vh = ch["v"][:, sl].astype(BF16)
                v_pad = jnp.concatenate([vh, jnp.zeros_like(vh)], axis=0)
                o_intra = lax.dot_general(a_ts.astype(BF16), v_pad, _NN, preferred_element_type=F32)
                st = st_ref[h]
                st_out_ref[j, h] = st
                o_inter = _dot(ch["qb"][:, sl], st, _NT)
                st_ref[h] = st * ch["e_last"][:, sl] + _dot(vh, ch["kd"][:, sl], _TN)
                oh = o_intra + o_inter
                rs = lax.rsqrt(jnp.mean(oh * oh, axis=1, keepdims=True) + RMS_EPS)
                o_heads.append(oh)
                y_heads.append(oh * rs * ng)
            hg = ch["hg"]
            o_ref[ch["rows"], :] = jnp.concatenate(o_heads, axis=1)
            ch["r"] = jnp.concatenate(y_heads, axis=1) * (hg * _sig(hg))
            r_ref[ch["rows"], :] = ch["r"].astype(r_ref.dtype)
        _store_transposed(r_t_ref, [ch["r"] for ch in chunks])

    col = lambda j: pl.BlockSpec((rows_step, HG_W), lambda c: (c, j))
    return _call(
        body, name=name, grid=(nc // cps,), ins=[uh, uh, uh, uh, lb_raw, norm_g],
        in_specs=[col(0), col(1), col(2), col(3),
                  pl.BlockSpec((2, HG_W), lambda c: (0, 0)), pl.BlockSpec((1, LANES), lambda c: (0, 0))],
        out_specs=[pl.BlockSpec((rows_step, HG_W), lambda c: (c, 0)),
                   pl.BlockSpec((HG_W, rows_step), lambda c: (0, c)),
                   pl.BlockSpec((rows_step, HG_W), lambda c: (c, 0)),
                   pl.BlockSpec((cps, 4, LANES, LANES), lambda c: (c, 0, 0, 0))],
        out_shape=[jax.ShapeDtypeStruct((t, HG_W), BF16), jax.ShapeDtypeStruct((HG_W, t), BF16),
                   jax.ShapeDtypeStruct((t, HG_W), F32), jax.ShapeDtypeStruct((nc, 4, LANES, LANES), F32)],
        scratch_shapes=[pltpu.VMEM((4, LANES, LANES), F32), pltpu.VMEM((cps, HGRN_CHUNK, HG_W), F32)],
        sem=("arbitrary",), comm=comm)


def _hgrn_bwd(uh, o_pre, d_r, states, lb_raw, norm_g, name, comm=None):
    t = uh.shape[0]
    nc = t // HGRN_CHUNK
    cps = _pick(nc, (HGRN_CHUNKS_PER_STEP, 2, 1))
    ns = nc // cps
    rows_step = cps * HGRN_CHUNK
    nsub = HGRN_CHUNK // HGRN_SUB

    def body(hq_ref, hf_ref, hi_ref, hg_ref, o_ref, dr_ref, st_in_ref, lb_ref, ng_ref,
             duh_ref, duh_t_ref, dbias_ref, dng_ref, dlb_ref, dst_ref, b_scr, dlb_acc):
        i = pl.program_id(0)

        @pl.when(i == 0)
        def _():
            dst_ref[...] = jnp.zeros_like(dst_ref)
            dbias_ref[...] = jnp.zeros_like(dbias_ref)
            dng_ref[...] = jnp.zeros_like(dng_ref)
            dlb_acc[...] = jnp.zeros_like(dlb_acc)

        masks_st = _sub_masks()
        masks_ts = _sub_masks_ts()
        ng = ng_ref[...]
        zpad = jnp.zeros((HGRN_CHUNK, LANES), F32)
        _, upper = _tri_mats()
        heads = [slice(h * LANES, (h + 1) * LANES) for h in range(4)]
        row = lax.broadcasted_iota(jnp.int32, (HGRN_CHUNK, HG_W), 0)

        chunks = [_hgrn_chunk_inputs(j, hq_ref, hf_ref, hi_ref, hg_ref, lb_ref, b_scr) for j in range(cps)]
        dng = jnp.zeros((1, LANES), F32)
        for ch in chunks:
            o = o_ref[ch["rows"], :]
            dr = dr_ref[ch["rows"], :].astype(F32)
            hg = ch["hg"]
            sgg = _sig(hg)
            dy = dr * (hg * sgg)
            do_h, y_h = [], []
            for sl in heads:
                oh = o[:, sl]
                rs = lax.rsqrt(jnp.mean(oh * oh, axis=1, keepdims=True) + RMS_EPS)
                y_h.append(oh * rs * ng)
                dng = dng + jnp.sum(dy[:, sl] * oh * rs, axis=0, keepdims=True)
                w = dy[:, sl] * ng
                do_h.append(rs * (w - oh * (rs * rs) * jnp.mean(w * oh, axis=1, keepdims=True)))
            ch["do"] = do_h
            ch["dhg"] = dr * jnp.concatenate(y_h, axis=1) * _dsilu(hg, sgg)

        for ch in chunks:
            ch["kst"], ch["kpad"], ch["qt_pad"], ch["v_b"], ch["do_pad"] = [], [], [], [], []
            ch["ats"], ch["d_at"], ch["d_a"] = [], [], []
            for h, sl in enumerate(heads):
                kst = jnp.concatenate([kh[:, sl] for kh in ch["khat"]], axis=0)
                kpad = jnp.concatenate([x for kh in ch["khat"] for x in (kh[:, sl], zpad)], axis=0)
                qt_pad = jnp.concatenate([ch["qt"][:, sl], zpad], axis=0)
                vh = ch["v"][:, sl].astype(BF16)
                v_pad = jnp.concatenate([vh, jnp.zeros_like(vh)], axis=0)
                do_b = ch["do"][h].astype(BF16)
                do_pad = jnp.concatenate([do_b, jnp.zeros_like(do_b)], axis=0)
                ch["kst"].append(kst)
                ch["kpad"].append(kpad)
                ch["qt_pad"].append(qt_pad)
                ch["v_b"].append(vh)
                ch["do_pad"].append(do_pad)
                ch["ats"].append(_dot3(kst, qt_pad, _NT))
                ch["d_at"].append(lax.dot_general(vh, do_pad, _NT, preferred_element_type=F32))
                ch["d_a"].append(lax.dot_general(do_b, v_pad, _NT, preferred_element_type=F32))

        for ch in chunks:
            ch["d_kst"], ch["d_qt"], ch["dv"] = [], [], []
            for h in range(4):
                at = _masked_sum(ch["ats"][h], masks_st, 0)
                d_ats = jnp.concatenate([jnp.where(m, ch["d_at"][h], 0.0) for m in masks_st], axis=0)
                d_a_cat = jnp.concatenate([jnp.where(m, ch["d_a"][h], 0.0) for m in masks_ts], axis=1)
                ch["d_kst"].append(_dot3(d_ats, ch["qt_pad"][h], _NN))
                ch["d_qt"].append(_dot3(d_a_cat, ch["kpad"][h], _NN))
                ch["dv"].append(lax.dot_general(at.astype(BF16), ch["do_pad"][h], _NN, preferred_element_type=F32))

        for j in reversed(range(cps)):
            ch = chunks[j]
            q, k, sg, f, lb = ch["q"], ch["k"], ch["sg"], ch["f"], ch["lb"]
            dq_h, dk_h, dv_h, extra_h = [], [], [], []
            for h, sl in enumerate(heads):
                st_prev = st_in_ref[j, h]
                d_st = dst_ref[h]
                d_st_b = d_st.astype(BF16)
                do_b = ch["do_pad"][h][0:HGRN_CHUNK, :]
                kd, e_last = ch["kd"][:, sl], ch["e_last"][:, sl]
                dv = ch["dv"][h] + _dot(kd, d_st_b, _NT)
                d_qb = _dot(do_b, st_prev, _NN)
                d_kd = lax.dot_general(ch["v_b"][h], d_st_b, _NN, preferred_element_type=F32)
                extra_h.append(jnp.sum(st_prev * d_st, axis=0, keepdims=True) * e_last
                               + jnp.sum(kd * d_kd, axis=0, keepdims=True))
                dst_ref[h] = d_st * e_last + _dot(do_b, ch["qb"][:, sl], _TN)
                dq_h.append(ch["d_qt"][h] * ch["e_q"][:, sl] + d_qb * ch["e_b"][:, sl])
                dkk = d_kd * ch["e_bl"][:, sl]
                for s_ in range(nsub):
                    dkk = dkk + ch["d_kst"][h][HGRN_CHUNK * s_:HGRN_CHUNK * (s_ + 1), :] * ch["e_k"][s_][:, sl]
                dk_h.append(dkk)
                dv_h.append(dv)
            dq = jnp.concatenate(dq_h, axis=1)
            dk = jnp.concatenate(dk_h, axis=1)
            dv = jnp.concatenate(dv_h, axis=1)
            extra = jnp.concatenate(extra_h, axis=1)
            db = q * dq - k * dk + jnp.where(row == HGRN_CHUNK - 1, extra, 0.0)
            dg = _tri_apply(upper, db)
            df = dg / f - dk
            dhf = df * (1.0 - lb) * sg * (1.0 - sg)
            dhq = dq * _dsilu(ch["hq"], _sig(ch["hq"]))
            full = jnp.concatenate([dhq, dhf, dv, ch["dhg"]], axis=1)
            duh_ref[ch["rows"], :] = full.astype(duh_ref.dtype)
            ch["full"] = full
            dbias_ref[...] += jnp.sum(full, axis=0, keepdims=True)
            dlb_acc[...] += jnp.sum(df * (1.0 - sg), axis=0, keepdims=True)
        dng_ref[...] += dng
        _store_transposed(duh_t_ref, [ch["full"] for ch in chunks])

        @pl.when(i == ns - 1)
        def _():
            lb = chunks[0]["lb"]
            d_a0 = dlb_acc[...] * lb * (1.0 - lb)
            r8 = lax.broadcasted_iota(jnp.int32, (8, HG_W), 0)
            dlb_ref[...] = jnp.where(r8 == 0, d_a0, jnp.where(r8 == 1, -d_a0, 0.0))

    col = lambda j: pl.BlockSpec((rows_step, HG_W), lambda i: (ns - 1 - i, j))
    return _call(
        body, name=name, grid=(ns,), ins=[uh, uh, uh, uh, o_pre, d_r, states, lb_raw, norm_g],
        in_specs=[col(0), col(1), col(2), col(3), col(0), col(0),
                  pl.BlockSpec((cps, 4, LANES, LANES), lambda i: (ns - 1 - i, 0, 0, 0)),
                  pl.BlockSpec((2, HG_W), lambda i: (0, 0)), pl.BlockSpec((1, LANES), lambda i: (0, 0))],
        out_specs=[pl.BlockSpec((rows_step, UH_W), lambda i: (ns - 1 - i, 0)),
                   pl.BlockSpec((UH_W, rows_step), lambda i: (0, ns - 1 - i)),
                   pl.BlockSpec((1, UH_W), lambda i: (0, 0)),
                   pl.BlockSpec((1, LANES), lambda i: (0, 0)),
                   pl.BlockSpec((8, HG_W), lambda i: (0, 0))],
        out_shape=[jax.ShapeDtypeStruct((t, UH_W), BF16), jax.ShapeDtypeStruct((UH_W, t), BF16),
                   jax.ShapeDtypeStruct((1, UH_W), F32),
                   jax.ShapeDtypeStruct((1, LANES), F32), jax.ShapeDtypeStruct((8, HG_W), F32)],
        scratch_shapes=[pltpu.VMEM((4, LANES, LANES), F32), pltpu.VMEM((cps, HGRN_CHUNK, HG_W), F32),
                        pltpu.VMEM((1, HG_W), F32)],
        sem=("arbitrary",), comm=comm)


def _ln_bwd_math(dy, xhat, rstd, g):
    dxh = dy * g
    return rstd * (dxh - jnp.mean(dxh, axis=1, keepdims=True)
                   - xhat * jnp.mean(dxh * xhat, axis=1, keepdims=True))


def _mm_rows(a, b, extras, *, name, epilogue, out_shape, out_specs, tb=False, tm=512, tk=1408):
    m, kdim = a.shape
    n = b.shape[0] if tb else b.shape[1]
    tm = _pick(m, (tm, 256, 128))
    tk = _pick(kdim, (tk, 1408, 1024, 768, 512, 256, 128))
    nk = kdim // tk
    b_spec = pl.BlockSpec((n, tk), lambda i, k: (0, k)) if tb else pl.BlockSpec((tk, n), lambda i, k: (k, 0))
    dims = _NT if tb else _NN
    n_ex, n_out = len(extras), len(out_shape)

    def body(*refs):
        a_ref, b_ref = refs[0], refs[1]
        ex_refs = refs[2:2 + n_ex]
        o_refs = refs[2 + n_ex:2 + n_ex + n_out]
        acc_ref = refs[-1]
        i, k = pl.program_id(0), pl.program_id(1)

        @pl.when(k == 0)
        def _():
            acc_ref[...] = jnp.zeros_like(acc_ref)

        acc_ref[...] += _dot(a_ref[...], b_ref[...], dims)

        @pl.when(k == nk - 1)
        def _():
            epilogue(acc_ref[...], ex_refs, o_refs, i == 0)

    return pl.pallas_call(
        body, name=name, grid=(m // tm, nk),
        in_specs=[pl.BlockSpec((tm, tk), lambda i, k: (i, k)), b_spec] + [sp for _, sp in extras],
        out_specs=list(out_specs), out_shape=list(out_shape),
        scratch_shapes=[pltpu.VMEM((tm, n), F32)],
        compiler_params=_cp("arbitrary", "arbitrary"),
    )(a, b, *[arr for arr, _ in extras])


def _rows_specs(tm, d):
    row = pl.BlockSpec((tm, d), lambda i, k: (i, 0))
    vec = pl.BlockSpec((1, d), lambda i, k: (0, 0))
    col = pl.BlockSpec((tm, 1), lambda i, k: (i, 0))
    return row, vec, col


def _mm_ln_fwd(a, b, addend, g, beta, name, tm=512):
    t, d = addend.shape
    tm = _pick(t, (tm, 256, 128))
    row, vec, col = _rows_specs(tm, d)

    def epilogue(acc, ex, outs, first):
        z = acc + ex[0][...]
        mu = jnp.mean(z, axis=1, keepdims=True)
        zc = z - mu
        rstd = lax.rsqrt(jnp.mean(zc * zc, axis=1, keepdims=True) + LN_EPS)
        xhat = zc * rstd
        h = xhat * ex[1][...] + ex[2][...]
        outs[0][...] = h
        outs[1][...] = h.astype(BF16)
        outs[2][...] = xhat
        outs[3][...] = rstd

    return _mm_rows(a, b, [(addend, row), (g, vec), (beta, vec)], name=name, epilogue=epilogue, tm=tm,
                    out_shape=[jax.ShapeDtypeStruct((t, d), F32), jax.ShapeDtypeStruct((t, d), BF16),
                               jax.ShapeDtypeStruct((t, d), F32), jax.ShapeDtypeStruct((t, 1), F32)],
                    out_specs=[row, row, row, col])


CONV_RB = 32
HALO = 8


def _sum8(x):
    acc = x[0:8]
    for r in range(8, x.shape[0], 8):
        acc = acc + x[r:r + 8]
    return acc


FFN_TILE = 256
FFN_COLS = 256


def _rows_before(win, k):
    return pltpu.roll(win, k, 0)[HALO:]


def _rows_after(win, k):
    n = win.shape[0]
    return pltpu.roll(win, n - k, 0)[0:n - HALO]


def _resident(shape):
    return pl.BlockSpec(shape, lambda i: (0,) * len(shape), pipeline_mode=pl.Buffered(1))


def _ffn_fwd(h1b, h1, w_up_t, conv_w, conv_b, w_down, target, ln2_g, ln2_b, name, comm=None):
    t, d = h1.shape
    tr = _pick(t, (FFN_TILE, 128))
    nblk = D_FF // FFN_COLS
    rb = CONV_RB

    def body(a_ref, wup_ref, cw_ref, cb_ref, wd_ref, h1_ref, tgt_ref, g_ref, b_ref,
             u2_ref, hm_ref, dz_ref, dg_ref, db_ref, loss_ref, ext):
        i = pl.program_id(0)

        @pl.when(i == 0)
        def _():
            ext[0:HALO, :] = jnp.zeros((HALO, D_FF), F32)
            dg_ref[...] = jnp.zeros_like(dg_ref)
            db_ref[...] = jnp.zeros_like(db_ref)
            loss_ref[...] = jnp.zeros_like(loss_ref)

        a = a_ref[...]
        for c in range(nblk):
            cs = slice(c * FFN_COLS, (c + 1) * FFN_COLS)
            vs = slice(D_FF + c * FFN_COLS, D_FF + (c + 1) * FFN_COLS)
            gate_pre = lax.dot_general(a, wup_ref[cs, :], _NT, preferred_element_type=F32)
            u2_ref[:, cs] = gate_pre
            ext[HALO:, cs] = gate_pre
            u2_ref[:, vs] = lax.dot_general(a, wup_ref[vs, :], _NT, preferred_element_type=F32)
        acc = jnp.zeros((tr, d), F32)
        for c in range(nblk):
            cs = slice(c * FFN_COLS, (c + 1) * FFN_COLS)
            for sub in range(FFN_COLS // LANES):
                ln = slice(c * FFN_COLS + sub * LANES, c * FFN_COLS + (sub + 1) * LANES)
                vl = slice(D_FF + c * FFN_COLS + sub * LANES, D_FF + c * FFN_COLS + (sub + 1) * LANES)
                w0, w1, w2, bb = cw_ref[0:1, ln], cw_ref[1:2, ln], cw_ref[2:3, ln], cb_ref[:, ln]
                for r0 in range(0, tr, rb):
                    win = ext[r0:r0 + HALO + rb, ln]
                    gate = _rows_before(win, 2) * w0 + _rows_before(win, 1) * w1 + win[HALO:] * w2 + bb
                    hm_ref[r0:r0 + rb, ln] = (gate * _sig(gate) * u2_ref[r0:r0 + rb, vl]).astype(hm_ref.dtype)
            acc = acc + lax.dot_general(hm_ref[:, cs], wd_ref[cs, :], _NN, preferred_element_type=F32)
        ext[0:HALO, :] = ext[tr:tr + HALO, :]

        z = acc + ALPHA * h1_ref[...]
        gg = g_ref[...]
        mu = jnp.mean(z, axis=1, keepdims=True)
        zc = z - mu
        rstd = lax.rsqrt(jnp.mean(zc * zc, axis=1, keepdims=True) + LN_EPS)
        xhat = zc * rstd
        err = xhat * gg + b_ref[...] - tgt_ref[...]
        loss_ref[...] += 0.5 * jnp.sum(jnp.mean(err * err, axis=1, keepdims=True))
        dy = err * (1.0 / d)
        dz_ref[...] = _ln_bwd_math(dy, xhat, rstd, gg)
        dg_ref[...] += jnp.sum(dy * xhat, axis=0, keepdims=True)
        db_ref[...] += jnp.sum(dy, axis=0, keepdims=True)

    row = lambda w: pl.BlockSpec((tr, w), lambda i: (i, 0))
    vec = pl.BlockSpec((1, d), lambda i: (0, 0))
    return _call(
        body, name=name, grid=(t // tr,),
        ins=[h1b, w_up_t, conv_w, conv_b, w_down, h1, target, ln2_g, ln2_b],
        in_specs=[row(d), _resident((2 * D_FF, d)), _resident((3, D_FF)), _resident((1, D_FF)),
                  _resident((D_FF, d)), row(d), row(d), vec, vec],
        out_specs=[row(2 * D_FF), row(D_FF), row(d), vec, vec, pl.BlockSpec((1, LANES), lambda i: (0, 0))],
        out_shape=[jax.ShapeDtypeStruct((t, 2 * D_FF), F32), jax.ShapeDtypeStruct((t, D_FF), BF16),
                   jax.ShapeDtypeStruct((t, d), F32), jax.ShapeDtypeStruct((1, d), F32),
                   jax.ShapeDtypeStruct((1, d), F32), jax.ShapeDtypeStruct((1, LANES), F32)],
        scratch_shapes=[pltpu.VMEM((tr + HALO, D_FF), F32)],
        sem=("arbitrary",), comm=comm)


def _ffn_bwd(dz2, u2, w_down, w_up_t, conv_w, conv_b, xhat1, rstd1, ln1_g, name, comm=None):
    t, d = dz2.shape
    tr = _pick(t, (FFN_TILE, 128))
    nt = t // tr
    hb = tr // HALO
    nblk = D_FF // FFN_COLS
    rb = CONV_RB

    def body(dz2_ref, u2_ref, gp_prev_ref, wd_ref, wup_ref, cw_ref, cb_ref, xhat_ref, rstd_ref, g1_ref,
             du_ref, dz1_ref, dw_ref, dcb_ref, dg1_ref, db1_ref, head, dh_s, dg_s):
        i = pl.program_id(0)

        @pl.when(i == 0)
        def _():
            dg_s[tr:, :] = jnp.zeros((HALO, D_FF), F32)
            dw_ref[...] = jnp.zeros_like(dw_ref)
            dcb_ref[...] = jnp.zeros_like(dcb_ref)
            dg1_ref[...] = jnp.zeros_like(dg1_ref)
            db1_ref[...] = jnp.zeros_like(db1_ref)

        dz2 = dz2_ref[...]
        dz2_b = dz2.astype(BF16)
        for c in range(nblk):
            cs = slice(c * FFN_COLS, (c + 1) * FFN_COLS)
            dh_s[:, cs] = lax.dot_general(dz2_b, wd_ref[cs, :], _NT, preferred_element_type=F32)
        head[0:HALO, :] = jnp.where(i == nt - 1, 0.0, gp_prev_ref[...])
        head[HALO:, :] = u2_ref[0:rb, 0:D_FF]

        for c in range(D_FF // LANES):
            ln = slice(c * LANES, (c + 1) * LANES)
            vl = slice(D_FF + c * LANES, D_FF + (c + 1) * LANES)
            w0, w1, w2, bb = cw_ref[0:1, ln], cw_ref[1:2, ln], cw_ref[2:3, ln], cb_ref[:, ln]
            acc_b = jnp.zeros((8, LANES), F32)
            acc_w = [jnp.zeros((8, LANES), F32) for _ in range(3)]
            for r0 in range(0, tr, rb):
                win = head[:, ln] if r0 == 0 else u2_ref[r0 - HALO:r0 + rb, ln]
                g_m2, g_m1, g_0 = _rows_before(win, 2), _rows_before(win, 1), win[HALO:]
                gate = g_m2 * w0 + g_m1 * w1 + g_0 * w2 + bb
                sg = _sig(gate)
                dh = dh_s[r0:r0 + rb, ln]
                dgate = dh * u2_ref[r0:r0 + rb, vl] * _dsilu(gate, sg)
                dg_s[r0:r0 + rb, ln] = dgate
                du_ref[r0:r0 + rb, vl] = (dh * (gate * sg)).astype(du_ref.dtype)
                acc_b = acc_b + _sum8(dgate)
                acc_w[0] = acc_w[0] + _sum8(dgate * g_m2)
                acc_w[1] = acc_w[1] + _sum8(dgate * g_m1)
                acc_w[2] = acc_w[2] + _sum8(dgate * g_0)
            dcb_ref[:, ln] += jnp.sum(acc_b, axis=0, keepdims=True)
            for j in range(3):
                dw_ref[j:j + 1, ln] += jnp.sum(acc_w[j], axis=0, keepdims=True)
            for r0 in range(0, tr, rb):
                win = dg_s[r0:r0 + rb + HALO, ln]
                d_gp = _rows_after(win, 2) * w0 + _rows_after(win, 1) * w1 + win[0:rb] * w2
                du_ref[r0:r0 + rb, ln] = d_gp.astype(du_ref.dtype)
        dg_s[tr:, :] = dg_s[0:HALO, :]

        acc = jnp.zeros((tr, d), F32)
        for c in range(nblk):
            cs = slice(c * FFN_COLS, (c + 1) * FFN_COLS)
            vs = slice(D_FF + c * FFN_COLS, D_FF + (c + 1) * FFN_COLS)
            acc = acc + lax.dot_general(du_ref[:, cs], wup_ref[cs, :], _NN, preferred_element_type=F32)
            acc = acc + lax.dot_general(du_ref[:, vs], wup_ref[vs, :], _NN, preferred_element_type=F32)
        dy = acc + ALPHA * dz2
        xh = xhat_ref[...]
        dz1_ref[...] = _ln_bwd_math(dy, xh, rstd_ref[...], g1_ref[...])
        dg1_ref[...] += jnp.sum(dy * xh, axis=0, keepdims=True)
        db1_ref[...] += jnp.sum(dy, axis=0, keepdims=True)

    rev = lambda w: pl.BlockSpec((tr, w), lambda i: (nt - 1 - i, 0))
    vec = pl.BlockSpec((1, d), lambda i: (0, 0))
    return _call(
        body, name=name, grid=(nt,),
        ins=[dz2, u2, u2, w_down, w_up_t, conv_w, conv_b, xhat1, rstd1, ln1_g],
        in_specs=[rev(d), rev(2 * D_FF),
                  pl.BlockSpec((HALO, D_FF), lambda i: (jnp.maximum((nt - 1 - i) * hb - 1, 0), 0)),
                  _resident((D_FF, d)), _resident((2 * D_FF, d)), _resident((3, D_FF)), _resident((1, D_FF)),
                  rev(d), pl.BlockSpec((tr, 1), lambda i: (nt - 1 - i, 0)), vec],
        out_specs=[rev(2 * D_FF), rev(d), pl.BlockSpec((8, D_FF), lambda i: (0, 0)),
                   pl.BlockSpec((1, D_FF), lambda i: (0, 0)), vec, vec],
        out_shape=[jax.ShapeDtypeStruct((t, 2 * D_FF), BF16), jax.ShapeDtypeStruct((t, d), F32),
                   jax.ShapeDtypeStruct((8, D_FF), F32), jax.ShapeDtypeStruct((1, D_FF), F32),
                   jax.ShapeDtypeStruct((1, d), F32), jax.ShapeDtypeStruct((1, d), F32)],
        scratch_shapes=[pltpu.VMEM((HALO + rb, D_FF), F32), pltpu.VMEM((tr, D_FF), F32),
                        pltpu.VMEM((tr + HALO, D_FF), F32)],
        sem=("arbitrary",), comm=comm)


def _adamw(w, g, m, v, name):
    rows, cols = w.shape
    tr = _pick(rows, (256, 128, 64, 32, 16, 8))

    def body(w_ref, g_ref, m_ref, v_ref, d_ref, nm_ref, nv_ref):
        d_ref[...], nm_ref[...], nv_ref[...] = _adamw_math(w_ref[...], g_ref[...], m_ref[...], v_ref[...])

    spec = pl.BlockSpec((tr, cols), lambda i: (i, 0))
    shp = jax.ShapeDtypeStruct((rows, cols), F32)
    return pl.pallas_call(
        body, name=name, grid=(rows // tr,),
        in_specs=[spec, spec, spec, spec], out_specs=[spec, spec, spec], out_shape=[shp, shp, shp],
        compiler_params=_cp("parallel"),
    )(w, g, m, v)


def _pad_rows(a, rows):
    return jnp.pad(a, ((0, rows - a.shape[0]), (0, 0)))


SMALL_LAYOUT = (("ln1_g", 1024), ("ln1_b", 1024), ("b_in", 2816), ("sinks", 8), ("hgrn_lb", 1024),
                ("hgrn_norm_g", 128), ("ln2_g", 1024), ("ln2_b", 1024), ("conv_b", 2816), ("loss", 1))
SMALL_SHAPES = {"ln1_g": (1, 1024), "ln1_b": (1, 1024), "b_in": (1, 2816), "sinks": (1, 8), "hgrn_lb": (2, 512),
                "hgrn_norm_g": (1, 128), "ln2_g": (1, 1024), "ln2_b": (1, 1024), "conv_b": (1, 2816),
                "loss": (1,)}


def _pack_small(parts):
    rows = []
    for name, size in SMALL_LAYOUT:
        flat = parts[name].reshape(-1).astype(F32)
        padded = -(-size // LANES) * LANES
        rows.append(jnp.pad(flat, (0, padded - size)).reshape(-1, LANES))
    return _pad_rows(jnp.concatenate(rows, axis=0), SMALL_ROWS)


def _unpack_small(pack):
    out, r = {}, 0
    for name, size in SMALL_LAYOUT:
        nrows = -(-size // LANES)
        out[name] = pack[r:r + nrows].reshape(-1)[:size].reshape(SMALL_SHAPES[name])
        r += nrows
    return out


def _own(full, rows):
    return lax.dynamic_slice_in_dim(full, _me() * rows, rows, axis=0)


def kernel(x, positions, ln1_g, ln1_b, w_in, b_in, sinks, hgrn_lb, hgrn_norm_g, w_o, ln2_g, ln2_b, w_up, conv_w, conv_b, w_down, loss_target, m_ln1_g, m_ln1_b, m_w_in, m_b_in, m_sinks, m_hgrn_lb, m_hgrn_norm_g, m_w_o, m_ln2_g, m_ln2_b, m_w_up, m_conv_w, m_conv_b, m_w_down, v_ln1_g, v_ln1_b, v_w_in, v_b_in, v_sinks, v_hgrn_lb, v_hgrn_norm_g, v_w_o, v_ln2_g, v_ln2_b, v_w_up, v_conv_w, v_conv_b, v_w_down):
    t = x.shape[1]
    x2 = x[0]
    target = loss_target[0]
    pos_col = positions.reshape(t, 1)

    w_in_t_s = w_in[0].T.astype(BF16)
    w_up_t_s = w_up[0].T.astype(BF16)
    w_o_s = w_o[0].astype(BF16)
    w_down_s = w_down[0].astype(BF16)
    (ctab, stab, xb), (w_in_t_g, cw_g) = _prep(
        pos_col, x2, "prep_ag_w_in", _Comm([{"kind": "gather", "arr": w_in_t_s}, {"kind": "gather", "arr": _pad_rows(conv_w[0], 8)}]))
    w_in_t = w_in_t_g.reshape(D_FF, D_MODEL)
    w_a_t, w_h_t = w_in_t[:UA_W], w_in_t[UA_W:]
    conv_w_f = cw_g[:, 0:3].transpose(1, 0, 2).reshape(3, D_FF)

    ua = _mm(xb, w_a_t, tb=True, bias=b_in[:, :UA_W], name="fwd_in_attn")
    uh, (w_down_g,) = _mm(xb, w_h_t, tb=True, bias=b_in[:, UA_W:], name="fwd_in_hgrn",
                          comm=_Comm([{"kind": "gather", "arr": w_down_s}]))
    w_down_f = w_down_g.reshape(D_FF, D_MODEL)
    half_up = SHARD_UP // 2
    (a_out, a_out_t), (w_o_g, w_up_half) = _attn_fwd(
        ua, ctab, stab, sinks, "attn_fwd",
        comm=_Comm([{"kind": "gather", "arr": w_o_s},
                    {"kind": "gather", "arr": w_up_t_s, "rows": (0, half_up), "dst_rows": SHARD_UP}]))
    (r_out, r_out_t, o_pre, states), (w_up_t_g,) = _hgrn_fwd(
        uh, hgrn_lb, hgrn_norm_g, "hgrn_fwd",
        comm=_Comm([{"kind": "gather", "arr": w_up_t_s, "rows": (half_up, half_up), "dst_rows": SHARD_UP,
                     "dst_first": half_up, "into": w_up_half}]))
    w_o_f = w_o_g.reshape(D_MODEL, D_MODEL)
    w_up_t = w_up_t_g.reshape(2 * D_FF, D_MODEL)
    z1 = _mm(a_out, w_o_f[:ATTN_W], addend=x2, addend_scale=ALPHA, name="fwd_o_attn")
    h1, h1b, xhat1, rstd1 = _mm_ln_fwd(r_out, w_o_f[ATTN_W:], z1, ln1_g, ln1_b, "fwd_o_hgrn_ln1")
    u2, hmid, dz2, d_ln2_g, d_ln2_b, loss_part = _ffn_fwd(h1b, h1, w_up_t, conv_w_f, conv_b, w_down_f, target,
                                                         ln2_g, ln2_b, "ffn_fwd")[0]

    d_w_down, d_w_down_b = _mm(hmid, dz2, ta=True, out_dtype2=BF16, tm=1408, tk=512, name="bwd_down_dw")
    (d_u2, dz1, d_conv_w8, d_conv_b, d_ln1_g, d_ln1_b), (recv_down,) = _ffn_bwd(
        dz2, u2, w_down_f, w_up_t, conv_w_f, conv_b, xhat1, rstd1, ln1_g, "ffn_bwd",
        comm=_Comm([{"kind": "exchange", "arr": d_w_down_b.reshape(N_DEV, SHARD_DOWN, D_MODEL)}]))
    d_w_up_t, d_w_up_t_b = _mm(d_u2, h1b, ta=True, out_dtype2=BF16, tm=1408, tk=512, name="bwd_up_dw")
    d_a = _mm(dz1, w_o_f[:ATTN_W], tb=True, name="bwd_o_dx_attn")
    d_r = _mm(dz1, w_o_f[ATTN_W:], tb=True, name="bwd_o_dx_hgrn")
    d_w_o_a, d_w_o_a_b = _mm(a_out_t, dz1, out_dtype2=BF16, name="bwd_o_dw_attn")
    d_w_o_r, d_w_o_r_b = _mm(r_out_t, dz1, out_dtype2=BF16, name="bwd_o_dw_hgrn")
    d_w_o = jnp.concatenate([d_w_o_a, d_w_o_r], axis=0)
    d_w_o_b = jnp.concatenate([d_w_o_a_b, d_w_o_r_b], axis=0)
    d_w_up_x = d_w_up_t_b.reshape(N_DEV, SHARD_UP, D_MODEL)
    half = SHARD_UP // 2
    d_cw_x = d_conv_w8.reshape(8, N_DEV, SHARD_IN).transpose(1, 0, 2)
    (d_ua, d_ua_t, d_bias_a, d_sinks), (recv_up_half, recv_o, recv_cw) = _attn_bwd(
        ua, d_a, ctab, stab, sinks, "attn_bwd",
        comm=_Comm([{"kind": "exchange", "arr": d_w_up_x, "rows": (0, half), "dst_rows": SHARD_UP},
                    {"kind": "exchange", "arr": d_w_o_b.reshape(N_DEV, SHARD_O, D_MODEL)},
                    {"kind": "exchange", "arr": d_cw_x}]))
    (d_uh, d_uh_t, d_bias_h, d_norm_g, d_lb8), (recv_up,) = _hgrn_bwd(
        uh, o_pre, d_r, states, hgrn_lb, hgrn_norm_g, "hgrn_bwd",
        comm=_Comm([{"kind": "exchange", "arr": d_w_up_x, "rows": (half, half), "dst_rows": SHARD_UP,
                     "dst_first": half, "into": recv_up_half}]))
    d_w_a_t, d_w_a_t_b = _mm(d_ua_t, xb, out_dtype2=BF16, name="bwd_in_dw_attn")
    d_w_h_t, d_w_h_t_b = _mm(d_uh_t, xb, out_dtype2=BF16, name="bwd_in_dw_hgrn")
    d_w_in_t = jnp.concatenate([d_w_a_t, d_w_h_t], axis=0)
    d_w_in_t_b = jnp.concatenate([d_w_a_t_b, d_w_h_t_b], axis=0)
    small_local = _pack_small({
        "ln1_g": d_ln1_g, "ln1_b": d_ln1_b, "b_in": jnp.concatenate([d_bias_a, d_bias_h], axis=1),
        "sinks": d_sinks[:, :8], "hgrn_lb": d_lb8[0:2], "hgrn_norm_g": d_norm_g, "ln2_g": d_ln2_g,
        "ln2_b": d_ln2_b, "conv_b": d_conv_b, "loss": loss_part[:, :1]})
    dx, (recv_in, small_g) = _mm(d_uh, w_h_t, addend=dz1, addend_scale=ALPHA, name="bwd_in_dx_hgrn",
                                 comm=_Comm([{"kind": "exchange", "arr": d_w_in_t_b.reshape(N_DEV, SHARD_IN, D_MODEL)},
                                             {"kind": "gather", "arr": small_local}]))
    dx = _mm(d_ua, w_a_t, addend=dx, tk=768, name="bwd_in_dx_attn")

    res_in = [r.T for r in _sum_shards_adamw([recv_in], _own(d_w_in_t, SHARD_IN), w_in[0].T, m_w_in[0].T,
                                             v_w_in[0].T, "adamw_w_in")]
    res_up = [r.T for r in _sum_shards_adamw([recv_up], _own(d_w_up_t, SHARD_UP), w_up[0].T,
                                             m_w_up[0].T, v_w_up[0].T, "adamw_w_up")]
    res_o = _sum_shards_adamw([recv_o], _own(d_w_o, SHARD_O), w_o[0], m_w_o[0], v_w_o[0], "adamw_w_o")
    res_down = _sum_shards_adamw([recv_down], _own(d_w_down, SHARD_DOWN), w_down[0], m_w_down[0], v_w_down[0],
                                 "adamw_w_down")
    g_cw = _sum_slots(recv_cw, "sum_conv_w")
    cw8 = lambda a: _pad_rows(a, 8)
    res_cw = (g_cw,) + tuple(_adamw(cw8(conv_w[0]), g_cw, cw8(m_conv_w[0]), cw8(v_conv_w[0]), "adamw_conv_w"))
    big = {"w_in": [r[None] for r in res_in], "w_up": [r[None] for r in res_up],
           "w_o": [r[None] for r in res_o], "w_down": [r[None] for r in res_down],
           "conv_w": [r[None, 0:3] for r in res_cw]}

    small_sum = _sum_slots(small_g, "ar_small_sum")
    gs = _unpack_small(small_sum)
    loss = gs["loss"][0]
    zero1 = jnp.zeros((1,), F32)
    w_small = _pack_small({"ln1_g": ln1_g, "ln1_b": ln1_b, "b_in": b_in, "sinks": sinks, "hgrn_lb": hgrn_lb,
                           "hgrn_norm_g": hgrn_norm_g, "ln2_g": ln2_g, "ln2_b": ln2_b, "conv_b": conv_b,
                           "loss": zero1})
    m_small = _pack_small({"ln1_g": m_ln1_g, "ln1_b": m_ln1_b, "b_in": m_b_in, "sinks": m_sinks,
                           "hgrn_lb": m_hgrn_lb, "hgrn_norm_g": m_hgrn_norm_g, "ln2_g": m_ln2_g,
                           "ln2_b": m_ln2_b, "conv_b": m_conv_b, "loss": zero1})
    v_small = _pack_small({"ln1_g": v_ln1_g, "ln1_b": v_ln1_b, "b_in": v_b_in, "sinks": v_sinks,
                           "hgrn_lb": v_hgrn_lb, "hgrn_norm_g": v_hgrn_norm_g, "ln2_g": v_ln2_g,
                           "ln2_b": v_ln2_b, "conv_b": v_conv_b, "loss": zero1})
    small = [gs] + [_unpack_small(p) for p in _adamw(w_small, small_sum, m_small, v_small, "adamw_small")]

    order = ["ln1_g", "ln1_b", "w_in", "b_in", "sinks", "hgrn_lb", "hgrn_norm_g", "w_o", "ln2_g", "ln2_b",
             "w_up", "conv_w", "conv_b", "w_down"]

    def pick(idx):
        return [big[n][idx] if n in big else small[idx][n] for n in order]

    return (loss, dx[None], *pick(0), *pick(1), *pick(2), *pick(3))
```

```python
import functools

import jax
import jax.numpy as jnp
import numpy as np
from jax import lax
from jax.experimental import pallas as pl
from jax.experimental.pallas import tpu as pltpu

F32 = jnp.float32
BF16 = jnp.bfloat16

N_DEV = 8
D_MODEL = 1024
D_FF = 2816
ATTN_W = 512
KV_W = 128
UA_W = ATTN_W + 2 * KV_W
UH_W = 2048
HG_W = 512
ATTN_BLOCK = 128
HGRN_CHUNK = 64
HGRN_SUB = 16
HGRN_CHUNKS_PER_STEP = 4
EXP_CLAMP = 85.0
NEG_BIG = -1e30
LN_EPS = 1e-5
RMS_EPS = 1e-6
ALPHA = 2.0 ** 0.25
ATTN_SCALE = 0.125
ROPE_THETA = 500000.0

ADAM_LR = 0.001
ADAM_B1 = 0.9
ADAM_B2 = 0.999
ADAM_EPS = 1e-08
ADAM_WD = 0.01
ADAM_STEP = 10

LANES = 128
VMEM_LIMIT_BYTES = 56 * 1024 * 1024

SHARD_IN = D_FF // N_DEV
SHARD_UP = 2 * D_FF // N_DEV
SHARD_O = D_MODEL // N_DEV
SHARD_DOWN = D_FF // N_DEV
SMALL_ROWS = 88

_MESH = pl.DeviceIdType.MESH
_NT = (((1,), (1,)), ((), ()))
_NN = (((1,), (0,)), ((), ()))
_TN = (((0,), (0,)), ((), ()))


def _cp(*sem):
    if sem:
        return pltpu.CompilerParams(dimension_semantics=sem, vmem_limit_bytes=VMEM_LIMIT_BYTES)
    return pltpu.CompilerParams(vmem_limit_bytes=VMEM_LIMIT_BYTES)


def _sig(x):
    return 0.5 * jnp.tanh(0.5 * x) + 0.5


def _dsilu(x, s):
    return s * (1.0 + x * (1.0 - s))


def _dot(a, b, dims):
    return lax.dot_general(a.astype(BF16), b.astype(BF16), dims, preferred_element_type=F32)


def _split(a):
    hi = a.astype(BF16)
    return hi, (a - hi.astype(F32)).astype(BF16)


def _dot3(a, b, dims):
    ah, al = _split(a)
    bh, bl = _split(b)
    d = functools.partial(lax.dot_general, dimension_numbers=dims, preferred_element_type=F32)
    return d(ah, bh) + (d(ah, bl) + d(al, bh))


def _pick(n, pref):
    for t in pref:
        if t <= n and n % t == 0:
            return t
    return n


def _my_coords():
    return lax.axis_index("x"), lax.axis_index("y"), lax.axis_index("c")


def _peer(k):
    x, y, c = _my_coords()
    return (1 - x if k & 4 else x, 1 - y if k & 2 else y, 1 - c if k & 1 else c)


def _me():
    x, y, c = _my_coords()
    return 4 * x + 2 * y + c


class _Comm:
    def __init__(self, items):
        self.items = []
        for it in items:
            arr = it["arr"]
            full = arr.shape[0] if it["kind"] == "gather" else arr.shape[1]
            first, count = it.get("rows", (0, full))
            self.items.append(dict(kind=it["kind"], arr=arr, first=first, count=count,
                                   dst_rows=it.get("dst_rows", count), dst_first=it.get("dst_first", 0),
                                   into=it.get("into")))
        self.n = len(self.items)
        self.arrays = [it["arr"] for it in self.items]
        self.intos = [(a, it["into"]) for a, it in enumerate(self.items) if it["into"] is not None]

    def out_shapes(self):
        return [jax.ShapeDtypeStruct((4 if it["kind"] in ("pair4", "chips3") else N_DEV, it["dst_rows"],
                                      it["arr"].shape[-1]), it["arr"].dtype) for it in self.items]

    def specs(self, n=None):
        return [pl.BlockSpec(memory_space=pl.ANY)] * (self.n if n is None else n)

    def scratch(self):
        return [pltpu.SemaphoreType.DMA(((N_DEV - 1) * self.n,)), pltpu.SemaphoreType.DMA(((N_DEV - 1) * self.n,)),
                pltpu.SemaphoreType.DMA((self.n,))]

    def _src(self, a, ref, dev):
        it = self.items[a]
        blk = ref if it["kind"] == "gather" else ref.at[dev]
        return blk.at[pl.ds(it["first"], it["count"])]

    def _dst(self, a, ref, slot):
        it = self.items[a]
        return ref.at[slot].at[pl.ds(it["dst_first"], it["count"])]

    def _copy(self, a, k, src, dst, sems, me, slot):
        other = jnp.bitwise_xor(me, k)
        idx = a * (N_DEV - 1) + k - 1
        return pltpu.make_async_remote_copy(
            src_ref=self._src(a, src, other), dst_ref=self._dst(a, dst, me if slot == "mine" else other),
            send_sem=sems[0].at[idx], recv_sem=sems[1].at[idx], device_id=_peer(k), device_id_type=_MESH)

    def _pass_on(self, a, k, dst, sems, me):
        slot = self._dst(a, dst, jnp.bitwise_xor(me, k))
        idx = a * (N_DEV - 1) + k
        return pltpu.make_async_remote_copy(
            src_ref=slot, dst_ref=slot, send_sem=sems[0].at[idx], recv_sem=sems[1].at[idx],
            device_id=_peer(1), device_id_type=_MESH)

    def _part(self, a, r, src, dst, sems, me):
        it = self.items[a]
        idx = a * (N_DEV - 1) + r
        if it["kind"] == "pair4":
            k, slot = 1, jnp.bitwise_xor(jnp.bitwise_xor(me, 1), 2 * r)
        else:
            k, slot = 2 * r, r
        return pltpu.make_async_remote_copy(
            src_ref=src.at[slot].at[pl.ds(it["first"], it["count"])], dst_ref=self._dst(a, dst, r),
            send_sem=sems[0].at[idx], recv_sem=sems[1].at[idx], device_id=_peer(k), device_id_type=_MESH)

    def _parts(self, a):
        return range(4) if self.items[a]["kind"] == "pair4" else range(1, 4)

    def _local(self, a, src, dst, sems, me):
        return pltpu.make_async_copy(self._src(a, src, me), self._dst(a, dst, me), sems[2].at[a])

    def start(self, srcs, dsts, sems):
        me = _me()
        for a, (src, dst) in enumerate(zip(srcs, dsts)):
            if self.items[a]["kind"] in ("pair4", "chips3"):
                for r in self._parts(a):
                    self._part(a, r, src, dst, sems, me).start()
                continue
            direct = (1, 2, 4, 6) if self.items[a]["kind"] == "gather" else range(1, N_DEV)
            self._local(a, src, dst, sems, me).start()
            for k in direct:
                self._copy(a, k, src, dst, sems, me, "mine").start()

    def wait(self, srcs, dsts, sems):
        me = _me()
        for a, (src, dst) in enumerate(zip(srcs, dsts)):
            if self.items[a]["kind"] in ("pair4", "chips3"):
                for r in self._parts(a):
                    self._part(a, r, src, dst, sems, me).wait_recv()
                for r in self._parts(a):
                    self._part(a, r, src, dst, sems, me).wait_send()
                continue
            if self.items[a]["kind"] == "gather":
                for k in (2, 4, 6):
                    self._copy(a, k, src, dst, sems, me, "theirs").wait_recv()
                    self._pass_on(a, k, dst, sems, me).start()
                for k in (1, 3, 5, 7):
                    self._copy(a, k, src, dst, sems, me, "theirs").wait_recv()
                for k in (1, 2, 4, 6):
                    self._copy(a, k, src, dst, sems, me, "mine").wait_send()
                for k in (2, 4, 6):
                    self._pass_on(a, k, dst, sems, me).wait_send()
            else:
                for k in range(1, N_DEV):
                    self._copy(a, k, src, dst, sems, me, "theirs").wait_recv()
                for k in range(1, N_DEV):
                    self._copy(a, k, src, dst, sems, me, "mine").wait_send()
            self._local(a, src, dst, sems, me).wait()


def _call(body, *, name, grid, ins, in_specs, out_specs, out_shape, scratch_shapes=(), sem, comm=None):
    n_in, n_out, n_scr = len(ins), len(out_shape), len(scratch_shapes)
    if comm is None:
        outs = pl.pallas_call(
            body, name=name, grid=grid, in_specs=list(in_specs), out_specs=list(out_specs),
            out_shape=list(out_shape), scratch_shapes=list(scratch_shapes), compiler_params=_cp(*sem))(*ins)
        return list(outs), []
    nc, n_into = comm.n, len(comm.intos)

    def hosted(*refs):
        pos = n_in
        c_in = refs[pos:pos + nc]
        pos += nc + n_into
        outs = refs[pos:pos + n_out]
        pos += n_out
        c_out = refs[pos:pos + nc]
        pos += nc
        scr = refs[pos:pos + n_scr]
        sems = refs[pos + n_scr:]
        ids = [pl.program_id(d) for d in range(len(grid))]
        first = functools.reduce(jnp.logical_and, [i == 0 for i in ids])
        last = functools.reduce(jnp.logical_and, [i == g - 1 for i, g in zip(ids, grid)])

        @pl.when(first)
        def _():
            comm.start(c_in, c_out, sems)

        body(*refs[:n_in], *outs, *scr)

        @pl.when(last)
        def _():
            comm.wait(c_in, c_out, sems)

    aliases = {n_in + nc + j: n_out + a for j, (a, _) in enumerate(comm.intos)}
    outs = pl.pallas_call(
        hosted, name=name, grid=grid, in_specs=list(in_specs) + comm.specs() + comm.specs(n_into),
        out_specs=list(out_specs) + comm.specs(), out_shape=list(out_shape) + comm.out_shapes(),
        scratch_shapes=list(scratch_shapes) + comm.scratch(), input_output_aliases=aliases,
        compiler_params=_cp(*(["arbitrary"] * len(grid))))(*ins, *comm.arrays, *[arr for _, arr in comm.intos])
    return list(outs[:n_out]), list(outs[n_out:])


def _sum_slots(gathered, name):
    _, rows, cols = gathered.shape

    def body(g_ref, out_ref):
        acc = g_ref[0]
        for s in range(1, N_DEV):
            acc = acc + g_ref[s]
        out_ref[...] = acc

    return pl.pallas_call(
        body, name=name,
        out_shape=jax.ShapeDtypeStruct((rows, cols), F32),
        compiler_params=_cp(),
    )(gathered)


def _slot_sum(recv_ref, own_ref, shape):
    me = _me()
    acc = jnp.zeros(shape, F32)
    for s in range(N_DEV):
        acc = acc + jnp.where(me == s, own_ref[...], recv_ref[s].astype(F32))
    return acc


def _adamw_math(w, g, m, v):
    nm = ADAM_B1 * m + (1.0 - ADAM_B1) * g
    nv = ADAM_B2 * v + (1.0 - ADAM_B2) * (g * g)
    m_hat = nm / (1.0 - ADAM_B1 ** ADAM_STEP)
    v_hat = nv / (1.0 - ADAM_B2 ** ADAM_STEP)
    return -ADAM_LR * (m_hat / (jnp.sqrt(v_hat) + ADAM_EPS) + ADAM_WD * w), nm, nv


def _pair_reduce(from_sibling, mine4, name):
    _, rows, cols = mine4.shape
    tr = _pick(rows, (176, 128, 64, 32, 16, 8))

    def body(sib_ref, mine_ref, own_ref, send_ref):
        own_ref[...] = mine_ref[0] + sib_ref[0].astype(F32)
        send_ref[0] = jnp.zeros((tr, cols), BF16)
        for r in range(1, 4):
            send_ref[r] = (mine_ref[r] + sib_ref[r].astype(F32)).astype(BF16)

    blk = pl.BlockSpec((4, tr, cols), lambda i: (0, i, 0))
    return pl.pallas_call(
        body, name=name, grid=(rows // tr,),
        in_specs=[blk, blk], out_specs=[pl.BlockSpec((tr, cols), lambda i: (i, 0)), blk],
        out_shape=[jax.ShapeDtypeStruct((rows, cols), F32), jax.ShapeDtypeStruct((4, rows, cols), BF16)],
        compiler_params=_cp("parallel"),
    )(from_sibling, mine4)


def _chip_sum_adamw(from_chips, own, w, m, v, name):
    _, rows, cols = from_chips.shape
    tr = _pick(rows, (176, 128, 64, 32, 16, 8))

    def body(recv_ref, own_ref, w_ref, m_ref, v_ref, g_ref, d_ref, nm_ref, nv_ref):
        g = own_ref[...]
        for r in range(1, 4):
            g = g + recv_ref[r].astype(F32)
        g_ref[...] = g
        d_ref[...], nm_ref[...], nv_ref[...] = _adamw_math(w_ref[...], g, m_ref[...], v_ref[...])

    spec = pl.BlockSpec((tr, cols), lambda i: (i, 0))
    shp = jax.ShapeDtypeStruct((rows, cols), F32)
    return pl.pallas_call(
        body, name=name, grid=(rows // tr,),
        in_specs=[pl.BlockSpec((4, tr, cols), lambda i: (0, i, 0)), spec, spec, spec, spec],
        out_specs=[spec, spec, spec, spec], out_shape=[shp, shp, shp, shp],
        compiler_params=_cp("parallel"),
    )(from_chips, own, w, m, v)


def _sum_shards_adamw(recvs, own, w, m, v, name, comm=None):
    rows_p, cols = recvs[0].shape[1], recvs[0].shape[2]
    n_p = len(recvs)
    tr = _pick(rows_p, (176, 128, 64, 32, 16, 8))
    tiles = rows_p // tr

    def body(*refs):
        recv_refs = refs[:n_p]
        own_ref, w_ref, m_ref, v_ref, g_ref, d_ref, nm_ref, nv_ref = refs[n_p:]
        for j in range(n_p):
            @pl.when(pl.program_id(0) == j)
            def _():
                g = _slot_sum(recv_refs[j], own_ref, (tr, cols))
                g_ref[...] = g
                d_ref[...], nm_ref[...], nv_ref[...] = _adamw_math(w_ref[...], g, m_ref[...], v_ref[...])

    spec = pl.BlockSpec((tr, cols), lambda p_, i: (p_ * tiles + i, 0))
    shp = jax.ShapeDtypeStruct((rows_p * n_p, cols), F32)
    outs, couts = _call(
        body, name=name, grid=(n_p, tiles), ins=[*recvs, own, w, m, v],
        in_specs=[pl.BlockSpec((N_DEV, tr, cols), functools.partial(lambda p_, i, j: (0, jnp.where(p_ == j, i, 0), 0), j=j))
                  for j in range(n_p)] + [spec, spec, spec, spec],
        out_specs=[spec, spec, spec, spec], out_shape=[shp, shp, shp, shp],
        sem=("arbitrary", "arbitrary"), comm=comm)
    return outs if comm is None else (outs, couts)


def _mm(a, b, *, name, ta=False, tb=False, out_dtype=F32, out_dtype2=None, bias=None, addend=None,
        addend_scale=1.0, tm=1024, tn=1024, tk=1024, comm=None):
    kdim, m = a.shape if ta else a.shape[::-1]
    n = b.shape[0] if tb else b.shape[1]
    tm = _pick(m, (tm, 1408, 1024, 768, 512, 256, 128))
    tn = _pick(n, (tn, 1408, 1024, 768, 512, 256, 128))
    tk = _pick(kdim, (tk, 1408, 1024, 768, 512, 256, 128))
    nk = kdim // tk
    a_spec = pl.BlockSpec((tk, tm), lambda i, j, k: (k, i)) if ta else pl.BlockSpec((tm, tk), lambda i, j, k: (i, k))
    b_spec = pl.BlockSpec((tn, tk), lambda i, j, k: (j, k)) if tb else pl.BlockSpec((tk, tn), lambda i, j, k: (k, j))
    ins, specs = [a, b], [a_spec, b_spec]
    if bias is not None:
        ins.append(bias)
        specs.append(pl.BlockSpec((1, tn), lambda i, j, k: (0, j)))
    if addend is not None:
        ins.append(addend)
        specs.append(pl.BlockSpec((tm, tn), lambda i, j, k: (i, j)))
    dims = (((0,) if ta else (1,), (1,) if tb else (0,)), ((), ()))
    has_bias, has_addend, two = bias is not None, addend is not None, out_dtype2 is not None

    def body(*refs):
        a_ref, b_ref = refs[0], refs[1]
        pos = 2
        bias_ref = addend_ref = None
        if has_bias:
            bias_ref = refs[pos]
            pos += 1
        if has_addend:
            addend_ref = refs[pos]
            pos += 1
        o_refs, acc_ref = refs[pos:-1], refs[-1]
        k = pl.program_id(2)

        @pl.when(k == 0)
        def _():
            acc_ref[...] = jnp.zeros_like(acc_ref)

        acc_ref[...] += _dot(a_ref[...], b_ref[...], dims)

        @pl.when(k == nk - 1)
        def _():
            r = acc_ref[...]
            if has_bias:
                r = r + bias_ref[...]
            if has_addend:
                r = r + addend_scale * addend_ref[...].astype(F32)
            for o_ref in o_refs:
                o_ref[...] = r.astype(o_ref.dtype)

    ospec = pl.BlockSpec((tm, tn), lambda i, j, k: (i, j))
    dtypes = [out_dtype] + ([out_dtype2] if two else [])
    outs, couts = _call(
        body, name=name, grid=(m // tm, n // tn, nk), ins=ins, in_specs=specs,
        out_specs=[ospec] * len(dtypes), out_shape=[jax.ShapeDtypeStruct((m, n), d) for d in dtypes],
        scratch_shapes=[pltpu.VMEM((tm, tn), F32)], sem=("parallel", "parallel", "arbitrary"), comm=comm)
    primary = tuple(outs) if two else outs[0]
    return (primary, couts) if comm is not None else primary


def _rope_lane_constants():
    inv_freq = np.float32(ROPE_THETA) ** (-np.arange(8, dtype=np.float32) * np.float32(2.0 / 16.0))
    lane = np.arange(LANES) % 64
    freq = np.where(lane < 16, inv_freq[lane % 8], 0.0).astype(np.float32)
    sign = np.where(lane < 8, -1.0, np.where(lane < 16, 1.0, 0.0)).astype(np.float32)
    return jnp.asarray(freq)[None, :], jnp.asarray(sign)[None, :]


def _prep(pos_col, x2, name, comm):
    t, d = x2.shape
    tr = _pick(t, (512, 256, 128))
    freq, sign = _rope_lane_constants()

    def body(pos_ref, freq_ref, sign_ref, x_ref, c_ref, s_ref, xb_ref):
        ang = pos_ref[...].astype(F32) * freq_ref[...]
        c_ref[...] = jnp.cos(ang)
        s_ref[...] = sign_ref[...] * jnp.sin(ang)
        xb_ref[...] = x_ref[...].astype(BF16)

    tab = pl.BlockSpec((tr, LANES), lambda i: (i, 0))
    return _call(
        body, name=name, grid=(t // tr,), ins=[pos_col, freq, sign, x2],
        in_specs=[pl.BlockSpec((tr, 1), lambda i: (i, 0)), pl.BlockSpec((1, LANES), lambda i: (0, 0)),
                  pl.BlockSpec((1, LANES), lambda i: (0, 0)), pl.BlockSpec((tr, d), lambda i: (i, 0))],
        out_specs=[tab, tab, pl.BlockSpec((tr, d), lambda i: (i, 0))],
        out_shape=[jax.ShapeDtypeStruct((t, LANES), F32), jax.ShapeDtypeStruct((t, LANES), F32),
                   jax.ShapeDtypeStruct((t, d), BF16)],
        sem=("parallel",), comm=comm)


def _swap8(t):
    width = t.shape[1]
    lane = jnp.bitwise_and(lax.broadcasted_iota(jnp.int32, t.shape, 1), 63)
    return jnp.where(lane < 8, pltpu.roll(t, width - 8, 1), jnp.where(lane < 16, pltpu.roll(t, 8, 1), 0.0))


def _rope(t, c, s):
    return t * c + _swap8(t) * s


def _rope_bwd(d, c, s):
    return d * c + _swap8(d * s)


def _tile4(a):
    return jnp.concatenate([a, a, a, a], axis=1)


def _attn_band(n, k_cur, k_prev, v_cur, v_prev, c_cur, s_cur, c_prev, s_prev):
    kband = jnp.concatenate([_rope(k_prev, c_prev, s_prev), _rope(k_cur, c_cur, s_cur)], axis=0)
    vband = jnp.concatenate([v_prev, v_cur], axis=0)
    qi = lax.broadcasted_iota(jnp.int32, (ATTN_BLOCK, 2 * ATTN_BLOCK), 0)
    kj = lax.broadcasted_iota(jnp.int32, (ATTN_BLOCK, 2 * ATTN_BLOCK), 1)
    dist = qi + ATTN_BLOCK - kj
    valid = (dist >= 0) & (dist < ATTN_BLOCK) & (n * ATTN_BLOCK - ATTN_BLOCK + kj >= 0)
    return (kband.astype(BF16), pltpu.roll(kband, 64, 1).astype(BF16),
            vband.astype(BF16), pltpu.roll(vband, 64, 1).astype(BF16), valid, kband)


def _attn_probs(raw, valid, sink, axis):
    s = jnp.where(valid, raw * ATTN_SCALE, NEG_BIG)
    m = jnp.maximum(jnp.max(s, axis=axis, keepdims=True), sink)
    p = jnp.exp(s - m)
    esink = jnp.exp(sink - m)
    z = jnp.sum(p, axis=axis, keepdims=True) + esink
    return p / z, esink / z


def _attn_valid_t(n):
    kj = lax.broadcasted_iota(jnp.int32, (2 * ATTN_BLOCK, ATTN_BLOCK), 0)
    qi = lax.broadcasted_iota(jnp.int32, (2 * ATTN_BLOCK, ATTN_BLOCK), 1)
    dist = qi + ATTN_BLOCK - kj
    return (dist >= 0) & (dist < ATTN_BLOCK) & (n * ATTN_BLOCK - ATTN_BLOCK + kj >= 0)


def _attn_specs(nb):
    def cur(col, width=KV_W):
        return pl.BlockSpec((ATTN_BLOCK, width), lambda n: (jnp.minimum(n, nb - 1), col))

    def prev(col):
        return pl.BlockSpec((ATTN_BLOCK, KV_W), lambda n: (jnp.maximum(n - 1, 0), col))

    ua_specs = [cur(0, ATTN_W), cur(4), prev(4), cur(5), prev(5)]
    tab_specs = [cur(0), cur(0), prev(0), prev(0)]
    return ua_specs, tab_specs


def _attn_fwd(ua, ctab, stab, sinks, name, comm=None):
    t = ua.shape[0]
    nb = t // ATTN_BLOCK
    ua_specs, tab_specs = _attn_specs(nb)

    def body(q_ref, kc_ref, kp_ref, vc_ref, vp_ref, cc_ref, sc_ref, cp_ref, sp_ref, sink_ref, o_ref, o_t_ref):
        n = pl.program_id(0)
        cc, sc = cc_ref[...], sc_ref[...]
        kb, kb_r, vb, vb_r, valid, _ = _attn_band(n, kc_ref[...], kp_ref[...], vc_ref[...], vp_ref[...],
                                                  cc, sc, cp_ref[...], sp_ref[...])
        qr = _rope(q_ref[...], _tile4(cc), _tile4(sc))
        lo = lax.broadcasted_iota(jnp.int32, (ATTN_BLOCK, LANES), 1) < 64
        heads = []
        for j in range(4):
            qj = qr[:, j * LANES:(j + 1) * LANES]
            for is_lo in (True, False):
                aligned = is_lo == (j < 2)
                qm = jnp.where(lo if is_lo else jnp.logical_not(lo), qj, 0.0).astype(BF16)
                raw = lax.dot_general(qm, kb if aligned else kb_r, _NT, preferred_element_type=F32)
                heads.append((raw, vb if aligned else vb_r, sink_ref[0, len(heads)]))
        halves = []
        for raw, vv, sink in heads:
            probs, _ = _attn_probs(raw, valid, sink, 1)
            halves.append(lax.dot_general(probs.astype(BF16), vv, _NN, preferred_element_type=F32))
        outs = [jnp.where(lo, halves[2 * j], halves[2 * j + 1]) for j in range(4)]
        o_ref[...] = jnp.concatenate(outs, axis=1).astype(o_ref.dtype)
        for j in range(4):
            o_t_ref[j * LANES:(j + 1) * LANES, :] = outs[j].T.astype(o_t_ref.dtype)

    return _call(
        body, name=name, grid=(nb,), ins=[ua, ua, ua, ua, ua, ctab, stab, ctab, stab, sinks],
        in_specs=ua_specs + tab_specs + [pl.BlockSpec(memory_space=pltpu.SMEM)],
        out_specs=[pl.BlockSpec((ATTN_BLOCK, ATTN_W), lambda n: (n, 0)),
                   pl.BlockSpec((ATTN_W, ATTN_BLOCK), lambda n: (0, n))],
        out_shape=[jax.ShapeDtypeStruct((t, ATTN_W), BF16), jax.ShapeDtypeStruct((ATTN_W, t), BF16)],
        sem=("parallel",), comm=comm)


def _attn_bwd(ua, d_out, ctab, stab, sinks, name, comm=None):
    t = ua.shape[0]
    nb = t // ATTN_BLOCK
    ua_specs, tab_specs = _attn_specs(nb)

    def body(q_ref, kc_ref, kp_ref, vc_ref, vp_ref, cc_ref, sc_ref, cp_ref, sp_ref, do_ref, sink_ref,
             dua_ref, dua_t_ref, dbias_ref, dsink_ref, dq_c, dk_c, dv_c, dq_n, dk_n, dv_n):
        n = pl.program_id(0)

        @pl.when(n == 0)
        def _():
            dq_c[...] = jnp.zeros_like(dq_c)
            dk_c[...] = jnp.zeros_like(dk_c)
            dv_c[...] = jnp.zeros_like(dv_c)
            dbias_ref[...] = jnp.zeros_like(dbias_ref)
            dsink_ref[...] = jnp.zeros_like(dsink_ref)

        @pl.when(n == nb)
        def _():
            dq_n[...] = jnp.zeros_like(dq_n)
            dk_n[...] = jnp.zeros_like(dk_n)
            dv_n[...] = jnp.zeros_like(dv_n)

        @pl.when(n < nb)
        def _():
            cc, sc = cc_ref[...], sc_ref[...]
            kb, kb_r, vb, vb_r, _, kb_f32 = _attn_band(n, kc_ref[...], kp_ref[...], vc_ref[...], vp_ref[...],
                                                       cc, sc, cp_ref[...], sp_ref[...])
            valid_t = _attn_valid_t(n)
            c4, s4 = _tile4(cc), _tile4(sc)
            qr = _rope(q_ref[...], c4, s4)
            do = do_ref[...].astype(F32)
            lane = lax.broadcasted_iota(jnp.int32, (ATTN_BLOCK, LANES), 1)
            lo = lane < 64
            lane_row = lax.broadcasted_iota(jnp.int32, (1, LANES), 1)
            k_t = {False: kb_f32.T.astype(BF16), True: pltpu.roll(kb_f32, 64, 1).T.astype(BF16)}
            heads = []
            for j in range(4):
                qj = qr[:, j * LANES:(j + 1) * LANES]
                doj = do[:, j * LANES:(j + 1) * LANES]
                for is_lo in (True, False):
                    aligned = is_lo == (j < 2)
                    msk = lo if is_lo else jnp.logical_not(lo)
                    kk = kb if aligned else kb_r
                    vv = vb if aligned else vb_r
                    qm = jnp.where(msk, qj, 0.0).astype(BF16)
                    dom = jnp.where(msk, doj, 0.0).astype(BF16)
                    heads.append(dict(
                        aligned=aligned, qm=qm, dom=dom, sink=sink_ref[0, len(heads)],
                        raw_t=lax.dot_general(kk, qm, _NT, preferred_element_type=F32),
                        dp_t=lax.dot_general(vv, dom, _NT, preferred_element_type=F32)))
            dk_band = jnp.zeros((2 * ATTN_BLOCK, LANES), F32)
            dv_band = jnp.zeros((2 * ATTN_BLOCK, LANES), F32)
            dsink = jnp.zeros((1, LANES), F32)
            for head, hd in enumerate(heads):
                probs_t, psink = _attn_probs(hd["raw_t"], valid_t, hd["sink"], 0)
                delta_t = jnp.sum(probs_t * hd["dp_t"], axis=0, keepdims=True)
                hd["ds_t"] = (probs_t * (hd["dp_t"] - delta_t) * ATTN_SCALE).astype(BF16)
                dsink = dsink + jnp.where(lane_row == head, -jnp.sum(psink * delta_t), 0.0)
                dk_h = lax.dot_general(hd["ds_t"], hd["qm"], _NN, preferred_element_type=F32)
                dv_h = lax.dot_general(probs_t.astype(BF16), hd["dom"], _NN, preferred_element_type=F32)
                if not hd["aligned"]:
                    dk_h = pltpu.roll(dk_h, 64, 1)
                    dv_h = pltpu.roll(dv_h, 64, 1)
                dk_band = dk_band + dk_h
                dv_band = dv_band + dv_h
            row_lo = lax.broadcasted_iota(jnp.int32, (LANES, ATTN_BLOCK), 0) < 64
            dq_t = [lax.dot_general(k_t[not hd["aligned"]], hd["ds_t"], _NN, preferred_element_type=F32)
                    for hd in heads]
            dqs = [jnp.where(row_lo, dq_t[2 * j], dq_t[2 * j + 1]).T for j in range(4)]
            dq_n[...] = _rope_bwd(jnp.concatenate(dqs, axis=1), c4, s4)
            dk_n[...] = dk_band
            dv_n[...] = dv_band
            dsink_ref[...] += dsink

        dk_prev = _rope_bwd(dk_c[...] + dk_n[0:ATTN_BLOCK, :], cp_ref[...], sp_ref[...])
        dv_prev = dv_c[...] + dv_n[0:ATTN_BLOCK, :]
        full = jnp.concatenate([dq_c[...], dk_prev, dv_prev], axis=1)
        dua_ref[...] = full.astype(dua_ref.dtype)
        for j in range(UA_W // LANES):
            dua_t_ref[j * LANES:(j + 1) * LANES, :] = full[:, j * LANES:(j + 1) * LANES].T.astype(dua_t_ref.dtype)
        dbias_ref[...] += jnp.sum(full, axis=0, keepdims=True)
        dq_c[...] = dq_n[...]
        dk_c[...] = dk_n[ATTN_BLOCK:, :]
        dv_c[...] = dv_n[ATTN_BLOCK:, :]

    return _call(
        body, name=name, grid=(nb + 1,), ins=[ua, ua, ua, ua, ua, ctab, stab, ctab, stab, d_out, sinks],
        in_specs=ua_specs + tab_specs + [
            pl.BlockSpec((ATTN_BLOCK, ATTN_W), lambda n: (jnp.minimum(n, nb - 1), 0)),
            pl.BlockSpec(memory_space=pltpu.SMEM)],
        out_specs=[pl.BlockSpec((ATTN_BLOCK, UA_W), lambda n: (jnp.maximum(n - 1, 0), 0)),
                   pl.BlockSpec((UA_W, ATTN_BLOCK), lambda n: (0, jnp.maximum(n - 1, 0))),
                   pl.BlockSpec((1, UA_W), lambda n: (0, 0)),
                   pl.BlockSpec((1, LANES), lambda n: (0, 0))],
        out_shape=[jax.ShapeDtypeStruct((t, UA_W), BF16), jax.ShapeDtypeStruct((UA_W, t), BF16),
                   jax.ShapeDtypeStruct((1, UA_W), F32),
                   jax.ShapeDtypeStruct((1, LANES), F32)],
        scratch_shapes=[pltpu.VMEM((ATTN_BLOCK, ATTN_W), F32), pltpu.VMEM((ATTN_BLOCK, KV_W), F32),
                        pltpu.VMEM((ATTN_BLOCK, KV_W), F32), pltpu.VMEM((ATTN_BLOCK, ATTN_W), F32),
                        pltpu.VMEM((2 * ATTN_BLOCK, KV_W), F32), pltpu.VMEM((2 * ATTN_BLOCK, KV_W), F32)],
        sem=("arbitrary",), comm=comm)


def _tri_mats():
    r = lax.broadcasted_iota(jnp.int32, (HGRN_CHUNK, LANES), 0)
    c = lax.broadcasted_iota(jnp.int32, (HGRN_CHUNK, LANES), 1)
    lower = ((c <= r) & (c < HGRN_CHUNK)).astype(F32)
    upper = ((c >= r) & (c < HGRN_CHUNK)).astype(F32)
    return lower, upper


def _tri_apply(tri, g):
    pad = jnp.concatenate([g, jnp.zeros_like(g)], axis=0)
    return lax.dot_general(tri, pad, _NN, precision=lax.Precision.HIGHEST, preferred_element_type=F32)


def _sub_masks():
    s = lax.broadcasted_iota(jnp.int32, (HGRN_CHUNK, LANES), 0)
    tt = lax.broadcasted_iota(jnp.int32, (HGRN_CHUNK, LANES), 1)
    return [(tt >= HGRN_SUB * i) & (tt < HGRN_SUB * (i + 1)) & (s <= tt) for i in range(HGRN_CHUNK // HGRN_SUB)]


def _hgrn_gates(hq, hf, lb_ref, b_scr):
    lb = _sig(lb_ref[0:1, :] - lb_ref[1:2, :])
    q = hq * _sig(hq)
    sg = _sig(hf)
    f = lb + (1.0 - lb) * sg
    k = 1.0 - f
    lower, _ = _tri_mats()
    b = _tri_apply(lower, jnp.log(f))
    b_scr[...] = b
    nsub = HGRN_CHUNK // HGRN_SUB
    starts = [jnp.zeros((1, HG_W), F32)] + [b_scr[HGRN_SUB * i - 1:HGRN_SUB * i, :] for i in range(1, nsub)]
    pq = jnp.concatenate([jnp.broadcast_to(p, (HGRN_SUB, HG_W)) for p in starts], axis=0)
    b_last = b_scr[HGRN_CHUNK - 1:HGRN_CHUNK, :]
    e_q = jnp.exp(b - pq)
    e_k = [jnp.exp(jnp.minimum(p - b, EXP_CLAMP)) for p in starts]
    e_b = jnp.exp(b)
    e_bl = jnp.exp(b_last - b)
    e_last = jnp.exp(b_last)
    return q, sg, f, k, lb, e_q, e_k, e_b, e_bl, e_last


def _sub_masks_ts():
    tt = lax.broadcasted_iota(jnp.int32, (HGRN_CHUNK, LANES), 0)
    s = lax.broadcasted_iota(jnp.int32, (HGRN_CHUNK, LANES), 1)
    return [(tt >= HGRN_SUB * i) & (tt < HGRN_SUB * (i + 1)) & (s <= tt) for i in range(HGRN_CHUNK // HGRN_SUB)]


def _masked_sum(blocks, masks, axis):
    step = HGRN_CHUNK if axis == 0 else LANES
    acc = jnp.zeros((HGRN_CHUNK, LANES), F32)
    for i, msk in enumerate(masks):
        blk = blocks[step * i:step * (i + 1), :] if axis == 0 else blocks[:, step * i:step * (i + 1)]
        acc = acc + jnp.where(msk, blk, 0.0)
    return acc


def _store_transposed(out_t_ref, chunk_rows):
    width = chunk_rows[0].shape[1]
    if len(chunk_rows) == 1:
        groups = [jnp.concatenate([chunk_rows[0], jnp.zeros_like(chunk_rows[0])], axis=0)]
    else:
        groups = [jnp.concatenate(chunk_rows[g:g + 2], axis=0) for g in range(0, len(chunk_rows), 2)]
    for g, rows in enumerate(groups):
        for c in range(width // LANES):
            tile = rows[:, c * LANES:(c + 1) * LANES].T.astype(out_t_ref.dtype)
            if len(chunk_rows) == 1:
                out_t_ref[c * LANES:(c + 1) * LANES, :] = tile[:, 0:HGRN_CHUNK]
            else:
                out_t_ref[c * LANES:(c + 1) * LANES, g * LANES:(g + 1) * LANES] = tile


def _hgrn_chunk_inputs(j, hq_ref, hf_ref, hi_ref, hg_ref, lb_ref, b_scr):
    rows = slice(j * HGRN_CHUNK, (j + 1) * HGRN_CHUNK)
    hq, hf, v, hg = hq_ref[rows, :], hf_ref[rows, :], hi_ref[rows, :], hg_ref[rows, :]
    q, sg, f, k, lb, e_q, e_k, e_b, e_bl, e_last = _hgrn_gates(hq, hf, lb_ref, b_scr.at[j])
    return dict(rows=rows, hq=hq, v=v, hg=hg, q=q, sg=sg, f=f, k=k, lb=lb, e_q=e_q, e_k=e_k, e_b=e_b, e_bl=e_bl,
                e_last=e_last, qt=q * e_q, qb=q * e_b, kd=k * e_bl, khat=[k * e for e in e_k])


def _hgrn_fwd(uh, lb_raw, norm_g, name, comm=None):
    t = uh.shape[0]
    nc = t // HGRN_CHUNK
    cps = _pick(nc, (HGRN_CHUNKS_PER_STEP, 2, 1))
    rows_step = cps * HGRN_CHUNK

    def body(hq_ref, hf_ref, hi_ref, hg_ref, lb_ref, ng_ref, r_ref, r_t_ref, o_ref, st_out_ref, st_ref, b_scr):
        @pl.when(pl.program_id(0) == 0)
        def _():
            st_ref[...] = jnp.zeros_like(st_ref)

        masks = _sub_masks_ts()
        ng = ng_ref[...]
        zpad = jnp.zeros((HGRN_CHUNK, LANES), F32)
        heads = [slice(h * LANES, (h + 1) * LANES) for h in range(4)]
        chunks = [_hgrn_chunk_inputs(j, hq_ref, hf_ref, hi_ref, hg_ref, lb_ref, b_scr) for j in range(cps)]
        for ch in chunks:
            ch["scores"] = [_dot3(ch["qt"][:, sl],
                                  jnp.concatenate([x for kh in ch["khat"] for x in (kh[:, sl], zpad)], axis=0), _NT)
                            for sl in heads]
        for j, ch in enumerate(chunks):
            o_heads, y_heads = [], []
            for h, sl in enumerate(heads):
                a_ts = _masked_sum(ch["scores"][h], masks, 1)
                vh = ch["v"][:, sl].astype(BF16)
                v_pad = jnp.concatenate([vh, jnp.zeros_like(vh)], axis=0)
                o_intra = lax.dot_general(a_ts.astype(BF16), v_pad, _NN, preferred_element_type=F32)
                st = st_ref[h]
                st_out_ref[j, h] = st
                o_inter = _dot(ch["qb"][:, sl], st, _NT)
                st_ref[h] = st * ch["e_last"][:, sl] + _dot(vh, ch["kd"][:, sl], _TN)
                oh = o_intra + o_inter
                rs = lax.rsqrt(jnp.mean(oh * oh, axis=1, keepdims=True) + RMS_EPS)
                o_heads.append(oh)
                y_heads.append(oh * rs * ng)
            hg = ch["hg"]
            o_ref[ch["rows"], :] = jnp.concatenate(o_heads, axis=1)
            ch["r"] = jnp.concatenate(y_heads, axis=1) * (hg * _sig(hg))
            r_ref[ch["rows"], :] = ch["r"].astype(r_ref.dtype)
        _store_transposed(r_t_ref, [ch["r"] for ch in chunks])

    col = lambda j: pl.BlockSpec((rows_step, HG_W), lambda c: (c, j))
    return _call(
        body, name=name, grid=(nc // cps,), ins=[uh, uh, uh, uh, lb_raw, norm_g],
        in_specs=[col(0), col(1), col(2), col(3),
                  pl.BlockSpec((2, HG_W), lambda c: (0, 0)), pl.BlockSpec((1, LANES), lambda c: (0, 0))],
        out_specs=[pl.BlockSpec((rows_step, HG_W), lambda c: (c, 0)),
                   pl.BlockSpec((HG_W, rows_step), lambda c: (0, c)),
                   pl.BlockSpec((rows_step, HG_W), lambda c: (c, 0)),
                   pl.BlockSpec((cps, 4, LANES, LANES), lambda c: (c, 0, 0, 0))],
        out_shape=[jax.ShapeDtypeStruct((t, HG_W), BF16), jax.ShapeDtypeStruct((HG_W, t), BF16),
                   jax.ShapeDtypeStruct((t, HG_W), F32), jax.ShapeDtypeStruct((nc, 4, LANES, LANES), F32)],
        scratch_shapes=[pltpu.VMEM((4, LANES, LANES), F32), pltpu.VMEM((cps, HGRN_CHUNK, HG_W), F32)],
        sem=("arbitrary",), comm=comm)


def _hgrn_bwd(uh, o_pre, d_r, states, lb_raw, norm_g, name, comm=None):
    t = uh.shape[0]
    nc = t // HGRN_CHUNK
    cps = _pick(nc, (HGRN_CHUNKS_PER_STEP, 2, 1))
    ns = nc // cps
    rows_step = cps * HGRN_CHUNK
    nsub = HGRN_CHUNK // HGRN_SUB

    def body(hq_ref, hf_ref, hi_ref, hg_ref, o_ref, dr_ref, st_in_ref, lb_ref, ng_ref,
             duh_ref, duh_t_ref, dbias_ref, dng_ref, dlb_ref, dst_ref, b_scr, dlb_acc):
        i = pl.program_id(0)

        @pl.when(i == 0)
        def _():
            dst_ref[...] = jnp.zeros_like(dst_ref)
            dbias_ref[...] = jnp.zeros_like(dbias_ref)
            dng_ref[...] = jnp.zeros_like(dng_ref)
            dlb_acc[...] = jnp.zeros_like(dlb_acc)

        masks_st = _sub_masks()
        masks_ts = _sub_masks_ts()
        ng = ng_ref[...]
        zpad = jnp.zeros((HGRN_CHUNK, LANES), F32)
        _, upper = _tri_mats()
        heads = [slice(h * LANES, (h + 1) * LANES) for h in range(4)]
        row = lax.broadcasted_iota(jnp.int32, (HGRN_CHUNK, HG_W), 0)

        chunks = [_hgrn_chunk_inputs(j, hq_ref, hf_ref, hi_ref, hg_ref, lb_ref, b_scr) for j in range(cps)]
        dng = jnp.zeros((1, LANES), F32)
        for ch in chunks:
            o = o_ref[ch["rows"], :]
            dr = dr_ref[ch["rows"], :].astype(F32)
            hg = ch["hg"]
            sgg = _sig(hg)
            dy = dr * (hg * sgg)
            do_h, y_h = [], []
            for sl in heads:
                oh = o[:, sl]
                rs = lax.rsqrt(jnp.mean(oh * oh, axis=1, keepdims=True) + RMS_EPS)
                y_h.append(oh * rs * ng)
                dng = dng + jnp.sum(dy[:, sl] * oh * rs, axis=0, keepdims=True)
                w = dy[:, sl] * ng
                do_h.append(rs * (w - oh * (rs * rs) * jnp.mean(w * oh, axis=1, keepdims=True)))
            ch["do"] = do_h
            ch["dhg"] = dr * jnp.concatenate(y_h, axis=1) * _dsilu(hg, sgg)

        for ch in chunks:
            ch["kst"], ch["kpad"], ch["qt_pad"], ch["v_b"], ch["do_pad"] = [], [], [], [], []
            ch["ats"], ch["d_at"], ch["d_a"] = [], [], []
            for h, sl in enumerate(heads):
                kst = jnp.concatenate([kh[:, sl] for kh in ch["khat"]], axis=0)
                kpad = jnp.concatenate([x for kh in ch["khat"] for x in (kh[:, sl], zpad)], axis=0)
                qt_pad = jnp.concatenate([ch["qt"][:, sl], zpad], axis=0)
                vh = ch["v"][:, sl].astype(BF16)
                v_pad = jnp.concatenate([vh, jnp.zeros_like(vh)], axis=0)
                do_b = ch["do"][h].astype(BF16)
                do_pad = jnp.concatenate([do_b, jnp.zeros_like(do_b)], axis=0)
                ch["kst"].append(kst)
                ch["kpad"].append(kpad)
                ch["qt_pad"].append(qt_pad)
                ch["v_b"].append(vh)
                ch["do_pad"].append(do_pad)
                ch["ats"].append(_dot3(kst, qt_pad, _NT))
                ch["d_at"].append(lax.dot_general(vh, do_pad, _NT, preferred_element_type=F32))
                ch["d_a"].append(lax.dot_general(do_b, v_pad, _NT, preferred_element_type=F32))

        for ch in chunks:
            ch["d_kst"], ch["d_qt"], ch["dv"] = [], [], []
            for h in range(4):
                at = _masked_sum(ch["ats"][h], masks_st, 0)
                d_ats = jnp.concatenate([jnp.where(m, ch["d_at"][h], 0.0) for m in masks_st], axis=0)
                d_a_cat = jnp.concatenate([jnp.where(m, ch["d_a"][h], 0.0) for m in masks_ts], axis=1)
                ch["d_kst"].append(_dot3(d_ats, ch["qt_pad"][h], _NN))
                ch["d_qt"].append(_dot3(d_a_cat, ch["kpad"][h], _NN))
                ch["dv"].append(lax.dot_general(at.astype(BF16), ch["do_pad"][h], _NN, preferred_element_type=F32))

        for j in reversed(range(cps)):
            ch = chunks[j]
            q, k, sg, f, lb = ch["q"], ch["k"], ch["sg"], ch["f"], ch["lb"]
            dq_h, dk_h, dv_h, extra_h = [], [], [], []
            for h, sl in enumerate(heads):
                st_prev = st_in_ref[j, h]
                d_st = dst_ref[h]
                d_st_b = d_st.astype(BF16)
                do_b = ch["do_pad"][h][0:HGRN_CHUNK, :]
                kd, e_last = ch["kd"][:, sl], ch["e_last"][:, sl]
                dv = ch["dv"][h] + _dot(kd, d_st_b, _NT)
                d_qb = _dot(do_b, st_prev, _NN)
                d_kd = lax.dot_general(ch["v_b"][h], d_st_b, _NN, preferred_element_type=F32)
                extra_h.append(jnp.sum(st_prev * d_st, axis=0, keepdims=True) * e_last
                               + jnp.sum(kd * d_kd, axis=0, keepdims=True))
                dst_ref[h] = d_st * e_last + _dot(do_b, ch["qb"][:, sl], _TN)
                dq_h.append(ch["d_qt"][h] * ch["e_q"][:, sl] + d_qb * ch["e_b"][:, sl])
                dkk = d_kd * ch["e_bl"][:, sl]
                for s_ in range(nsub):
                    dkk = dkk + ch["d_kst"][h][HGRN_CHUNK * s_:HGRN_CHUNK * (s_ + 1), :] * ch["e_k"][s_][:, sl]
                dk_h.append(dkk)
                dv_h.append(dv)
            dq = jnp.concatenate(dq_h, axis=1)
            dk = jnp.concatenate(dk_h, axis=1)
            dv = jnp.concatenate(dv_h, axis=1)
            extra = jnp.concatenate(extra_h, axis=1)
            db = q * dq - k * dk + jnp.where(row == HGRN_CHUNK - 1, extra, 0.0)
            dg = _tri_apply(upper, db)
            df = dg / f - dk
            dhf = df * (1.0 - lb) * sg * (1.0 - sg)
            dhq = dq * _dsilu(ch["hq"], _sig(ch["hq"]))
            full = jnp.concatenate([dhq, dhf, dv, ch["dhg"]], axis=1)
            duh_ref[ch["rows"], :] = full.astype(duh_ref.dtype)
            ch["full"] = full
            dbias_ref[...] += jnp.sum(full, axis=0, keepdims=True)
            dlb_acc[...] += jnp.sum(df * (1.0 - sg), axis=0, keepdims=True)
        dng_ref[...] += dng
        _store_transposed(duh_t_ref, [ch["full"] for ch in chunks])

        @pl.when(i == ns - 1)
        def _():
            lb = chunks[0]["lb"]
            d_a0 = dlb_acc[...] * lb * (1.0 - lb)
            r8 = lax.broadcasted_iota(jnp.int32, (8, HG_W), 0)
            dlb_ref[...] = jnp.where(r8 == 0, d_a0, jnp.where(r8 == 1, -d_a0, 0.0))

    col = lambda j: pl.BlockSpec((rows_step, HG_W), lambda i: (ns - 1 - i, j))
    return _call(
        body, name=name, grid=(ns,), ins=[uh, uh, uh, uh, o_pre, d_r, states, lb_raw, norm_g],
        in_specs=[col(0), col(1), col(2), col(3), col(0), col(0),
                  pl.BlockSpec((cps, 4, LANES, LANES), lambda i: (ns - 1 - i, 0, 0, 0)),
                  pl.BlockSpec((2, HG_W), lambda i: (0, 0)), pl.BlockSpec((1, LANES), lambda i: (0, 0))],
        out_specs=[pl.BlockSpec((rows_step, UH_W), lambda i: (ns - 1 - i, 0)),
                   pl.BlockSpec((UH_W, rows_step), lambda i: (0, ns - 1 - i)),
                   pl.BlockSpec((1, UH_W), lambda i: (0, 0)),
                   pl.BlockSpec((1, LANES), lambda i: (0, 0)),
                   pl.BlockSpec((8, HG_W), lambda i: (0, 0))],
        out_shape=[jax.ShapeDtypeStruct((t, UH_W), BF16), jax.ShapeDtypeStruct((UH_W, t), BF16),
                   jax.ShapeDtypeStruct((1, UH_W), F32),
                   jax.ShapeDtypeStruct((1, LANES), F32), jax.ShapeDtypeStruct((8, HG_W), F32)],
        scratch_shapes=[pltpu.VMEM((4, LANES, LANES), F32), pltpu.VMEM((cps, HGRN_CHUNK, HG_W), F32),
                        pltpu.VMEM((1, HG_W), F32)],
        sem=("arbitrary",), comm=comm)


def _ln_bwd_math(dy, xhat, rstd, g):
    dxh = dy * g
    return rstd * (dxh - jnp.mean(dxh, axis=1, keepdims=True)
                   - xhat * jnp.mean(dxh * xhat, axis=1, keepdims=True))


def _mm_rows(a, b, extras, *, name, epilogue, out_shape, out_specs, tb=False, tm=512, tk=1408):
    m, kdim = a.shape
    n = b.shape[0] if tb else b.shape[1]
    tm = _pick(m, (tm, 256, 128))
    tk = _pick(kdim, (tk, 1408, 1024, 768, 512, 256, 128))
    nk = kdim // tk
    b_spec = pl.BlockSpec((n, tk), lambda i, k: (0, k)) if tb else pl.BlockSpec((tk, n), lambda i, k: (k, 0))
    dims = _NT if tb else _NN
    n_ex, n_out = len(extras), len(out_shape)

    def body(*refs):
        a_ref, b_ref = refs[0], refs[1]
        ex_refs = refs[2:2 + n_ex]
        o_refs = refs[2 + n_ex:2 + n_ex + n_out]
        acc_ref = refs[-1]
        i, k = pl.program_id(0), pl.program_id(1)

        @pl.when(k == 0)
        def _():
            acc_ref[...] = jnp.zeros_like(acc_ref)

        acc_ref[...] += _dot(a_ref[...], b_ref[...], dims)

        @pl.when(k == nk - 1)
        def _():
            epilogue(acc_ref[...], ex_refs, o_refs, i == 0)

    return pl.pallas_call(
        body, name=name, grid=(m // tm, nk),
        in_specs=[pl.BlockSpec((tm, tk), lambda i, k: (i, k)), b_spec] + [sp for _, sp in extras],
        out_specs=list(out_specs), out_shape=list(out_shape),
        scratch_shapes=[pltpu.VMEM((tm, n), F32)],
        compiler_params=_cp("arbitrary", "arbitrary"),
    )(a, b, *[arr for arr, _ in extras])


def _rows_specs(tm, d):
    row = pl.BlockSpec((tm, d), lambda i, k: (i, 0))
    vec = pl.BlockSpec((1, d), lambda i, k: (0, 0))
    col = pl.BlockSpec((tm, 1), lambda i, k: (i, 0))
    return row, vec, col


def _mm_ln_fwd(a, b, addend, g, beta, name, tm=512):
    t, d = addend.shape
    tm = _pick(t, (tm, 256, 128))
    row, vec, col = _rows_specs(tm, d)

    def epilogue(acc, ex, outs, first):
        z = acc + ex[0][...]
        mu = jnp.mean(z, axis=1, keepdims=True)
        zc = z - mu
        rstd = lax.rsqrt(jnp.mean(zc * zc, axis=1, keepdims=True) + LN_EPS)
        xhat = zc * rstd
        h = xhat * ex[1][...] + ex[2][...]
        outs[0][...] = h
        outs[1][...] = h.astype(BF16)
        outs[2][...] = xhat
        outs[3][...] = rstd

    return _mm_rows(a, b, [(addend, row), (g, vec), (beta, vec)], name=name, epilogue=epilogue, tm=tm,
                    out_shape=[jax.ShapeDtypeStruct((t, d), F32), jax.ShapeDtypeStruct((t, d), BF16),
                               jax.ShapeDtypeStruct((t, d), F32), jax.ShapeDtypeStruct((t, 1), F32)],
                    out_specs=[row, row, row, col])


CONV_RB = 32
HALO = 8


def _sum8(x):
    acc = x[0:8]
    for r in range(8, x.shape[0], 8):
        acc = acc + x[r:r + 8]
    return acc


FFN_TILE = 256
FFN_COLS = 256


def _rows_before(win, k):
    return pltpu.roll(win, k, 0)[HALO:]


def _rows_after(win, k):
    n = win.shape[0]
    return pltpu.roll(win, n - k, 0)[0:n - HALO]


def _resident(shape):
    return pl.BlockSpec(shape, lambda i: (0,) * len(shape), pipeline_mode=pl.Buffered(1))


def _ffn_fwd(h1b, h1, w_up_t, conv_w, conv_b, w_down, target, ln2_g, ln2_b, name, comm=None):
    t, d = h1.shape
    tr = _pick(t, (FFN_TILE, 128))
    nblk = D_FF // FFN_COLS
    rb = CONV_RB

    def body(a_ref, wup_ref, cw_ref, cb_ref, wd_ref, h1_ref, tgt_ref, g_ref, b_ref,
             u2_ref, hm_ref, dz_ref, dg_ref, db_ref, loss_ref, ext):
        i = pl.program_id(0)

        @pl.when(i == 0)
        def _():
            ext[0:HALO, :] = jnp.zeros((HALO, D_FF), F32)
            dg_ref[...] = jnp.zeros_like(dg_ref)
            db_ref[...] = jnp.zeros_like(db_ref)
            loss_ref[...] = jnp.zeros_like(loss_ref)

        a = a_ref[...]
        for c in range(nblk):
            cs = slice(c * FFN_COLS, (c + 1) * FFN_COLS)
            vs = slice(D_FF + c * FFN_COLS, D_FF + (c + 1) * FFN_COLS)
            gate_pre = lax.dot_general(a, wup_ref[cs, :], _NT, preferred_element_type=F32)
            u2_ref[:, cs] = gate_pre
            ext[HALO:, cs] = gate_pre
            u2_ref[:, vs] = lax.dot_general(a, wup_ref[vs, :], _NT, preferred_element_type=F32)
        acc = jnp.zeros((tr, d), F32)
        for c in range(nblk):
            cs = slice(c * FFN_COLS, (c + 1) * FFN_COLS)
            for sub in range(FFN_COLS // LANES):
                ln = slice(c * FFN_COLS + sub * LANES, c * FFN_COLS + (sub + 1) * LANES)
                vl = slice(D_FF + c * FFN_COLS + sub * LANES, D_FF + c * FFN_COLS + (sub + 1) * LANES)
                w0, w1, w2, bb = cw_ref[0:1, ln], cw_ref[1:2, ln], cw_ref[2:3, ln], cb_ref[:, ln]
                for r0 in range(0, tr, rb):
                    win = ext[r0:r0 + HALO + rb, ln]
                    gate = _rows_before(win, 2) * w0 + _rows_before(win, 1) * w1 + win[HALO:] * w2 + bb
                    hm_ref[r0:r0 + rb, ln] = (gate * _sig(gate) * u2_ref[r0:r0 + rb, vl]).astype(hm_ref.dtype)
            acc = acc + lax.dot_general(hm_ref[:, cs], wd_ref[cs, :], _NN, preferred_element_type=F32)
        ext[0:HALO, :] = ext[tr:tr + HALO, :]

        z = acc + ALPHA * h1_ref[...]
        gg = g_ref[...]
        mu = jnp.mean(z, axis=1, keepdims=True)
        zc = z - mu
        rstd = lax.rsqrt(jnp.mean(zc * zc, axis=1, keepdims=True) + LN_EPS)
        xhat = zc * rstd
        err = xhat * gg + b_ref[...] - tgt_ref[...]
        loss_ref[...] += 0.5 * jnp.sum(jnp.mean(err * err, axis=1, keepdims=True))
        dy = err * (1.0 / d)
        dz_ref[...] = _ln_bwd_math(dy, xhat, rstd, gg)
        dg_ref[...] += jnp.sum(dy * xhat, axis=0, keepdims=True)
        db_ref[...] += jnp.sum(dy, axis=0, keepdims=True)

    row = lambda w: pl.BlockSpec((tr, w), lambda i: (i, 0))
    vec = pl.BlockSpec((1, d), lambda i: (0, 0))
    return _call(
        body, name=name, grid=(t // tr,),
        ins=[h1b, w_up_t, conv_w, conv_b, w_down, h1, target, ln2_g, ln2_b],
        in_specs=[row(d), _resident((2 * D_FF, d)), _resident((3, D_FF)), _resident((1, D_FF)),
                  _resident((D_FF, d)), row(d), row(d), vec, vec],
        out_specs=[row(2 * D_FF), row(D_FF), row(d), vec, vec, pl.BlockSpec((1, LANES), lambda i: (0, 0))],
        out_shape=[jax.ShapeDtypeStruct((t, 2 * D_FF), F32), jax.ShapeDtypeStruct((t, D_FF), BF16),
                   jax.ShapeDtypeStruct((t, d), F32), jax.ShapeDtypeStruct((1, d), F32),
                   jax.ShapeDtypeStruct((1, d), F32), jax.ShapeDtypeStruct((1, LANES), F32)],
        scratch_shapes=[pltpu.VMEM((tr + HALO, D_FF), F32)],
        sem=("arbitrary",), comm=comm)


def _ffn_bwd(dz2, u2, w_down, w_up_t, conv_w, conv_b, xhat1, rstd1, ln1_g, name, comm=None):
    t, d = dz2.shape
    tr = _pick(t, (FFN_TILE, 128))
    nt = t // tr
    hb = tr // HALO
    nblk = D_FF // FFN_COLS
    rb = CONV_RB

    def body(dz2_ref, u2_ref, gp_prev_ref, wd_ref, wup_ref, cw_ref, cb_ref, xhat_ref, rstd_ref, g1_ref,
             du_ref, dz1_ref, dw_ref, dcb_ref, dg1_ref, db1_ref, head, dh_s, dg_s):
        i = pl.program_id(0)

        @pl.when(i == 0)
        def _():
            dg_s[tr:, :] = jnp.zeros((HALO, D_FF), F32)
            dw_ref[...] = jnp.zeros_like(dw_ref)
            dcb_ref[...] = jnp.zeros_like(dcb_ref)
            dg1_ref[...] = jnp.zeros_like(dg1_ref)
            db1_ref[...] = jnp.zeros_like(db1_ref)

        dz2 = dz2_ref[...]
        dz2_b = dz2.astype(BF16)
        for c in range(nblk):
            cs = slice(c * FFN_COLS, (c + 1) * FFN_COLS)
            dh_s[:, cs] = lax.dot_general(dz2_b, wd_ref[cs, :], _NT, preferred_element_type=F32)
        head[0:HALO, :] = jnp.where(i == nt - 1, 0.0, gp_prev_ref[...])
        head[HALO:, :] = u2_ref[0:rb, 0:D_FF]

        for c in range(D_FF // LANES):
            ln = slice(c * LANES, (c + 1) * LANES)
            vl = slice(D_FF + c * LANES, D_FF + (c + 1) * LANES)
            w0, w1, w2, bb = cw_ref[0:1, ln], cw_ref[1:2, ln], cw_ref[2:3, ln], cb_ref[:, ln]
            acc_b = jnp.zeros((8, LANES), F32)
            acc_w = [jnp.zeros((8, LANES), F32) for _ in range(3)]
            for r0 in range(0, tr, rb):
                win = head[:, ln] if r0 == 0 else u2_ref[r0 - HALO:r0 + rb, ln]
                g_m2, g_m1, g_0 = _rows_before(win, 2), _rows_before(win, 1), win[HALO:]
                gate = g_m2 * w0 + g_m1 * w1 + g_0 * w2 + bb
                sg = _sig(gate)
                dh = dh_s[r0:r0 + rb, ln]
                dgate = dh * u2_ref[r0:r0 + rb, vl] * _dsilu(gate, sg)
                dg_s[r0:r0 + rb, ln] = dgate
                du_ref[r0:r0 + rb, vl] = (dh * (gate * sg)).astype(du_ref.dtype)
                acc_b = acc_b + _sum8(dgate)
                acc_w[0] = acc_w[0] + _sum8(dgate * g_m2)
                acc_w[1] = acc_w[1] + _sum8(dgate * g_m1)
                acc_w[2] = acc_w[2] + _sum8(dgate * g_0)
            dcb_ref[:, ln] += jnp.sum(acc_b, axis=0, keepdims=True)
            for j in range(3):
                dw_ref[j:j + 1, ln] += jnp.sum(acc_w[j], axis=0, keepdims=True)
            for r0 in range(0, tr, rb):
                win = dg_s[r0:r0 + rb + HALO, ln]
                d_gp = _rows_after(win, 2) * w0 + _rows_after(win, 1) * w1 + win[0:rb] * w2
                du_ref[r0:r0 + rb, ln] = d_gp.astype(du_ref.dtype)
        dg_s[tr:, :] = dg_s[0:HALO, :]

        acc = jnp.zeros((tr, d), F32)
        for c in range(nblk):
            cs = slice(c * FFN_COLS, (c + 1) * FFN_COLS)
            vs = slice(D_FF + c * FFN_COLS, D_FF + (c + 1) * FFN_COLS)
            acc = acc + lax.dot_general(du_ref[:, cs], wup_ref[cs, :], _NN, preferred_element_type=F32)
            acc = acc + lax.dot_general(du_ref[:, vs], wup_ref[vs, :], _NN, preferred_element_type=F32)
        dy = acc + ALPHA * dz2
        xh = xhat_ref[...]
        dz1_ref[...] = _ln_bwd_math(dy, xh, rstd_ref[...], g1_ref[...])
        dg1_ref[...] += jnp.sum(dy * xh, axis=0, keepdims=True)
        db1_ref[...] += jnp.sum(dy, axis=0, keepdims=True)

    rev = lambda w: pl.BlockSpec((tr, w), lambda i: (nt - 1 - i, 0))
    vec = pl.BlockSpec((1, d), lambda i: (0, 0))
    return _call(
        body, name=name, grid=(nt,),
        ins=[dz2, u2, u2, w_down, w_up_t, conv_w, conv_b, xhat1, rstd1, ln1_g],
        in_specs=[rev(d), rev(2 * D_FF),
                  pl.BlockSpec((HALO, D_FF), lambda i: (jnp.maximum((nt - 1 - i) * hb - 1, 0), 0)),
                  _resident((D_FF, d)), _resident((2 * D_FF, d)), _resident((3, D_FF)), _resident((1, D_FF)),
                  rev(d), pl.BlockSpec((tr, 1), lambda i: (nt - 1 - i, 0)), vec],
        out_specs=[rev(2 * D_FF), rev(d), pl.BlockSpec((8, D_FF), lambda i: (0, 0)),
                   pl.BlockSpec((1, D_FF), lambda i: (0, 0)), vec, vec],
        out_shape=[jax.ShapeDtypeStruct((t, 2 * D_FF), BF16), jax.ShapeDtypeStruct((t, d), F32),
                   jax.ShapeDtypeStruct((8, D_FF), F32), jax.ShapeDtypeStruct((1, D_FF), F32),
                   jax.ShapeDtypeStruct((1, d), F32), jax.ShapeDtypeStruct((1, d), F32)],
        scratch_shapes=[pltpu.VMEM((HALO + rb, D_FF), F32), pltpu.VMEM((tr, D_FF), F32),
                        pltpu.VMEM((tr + HALO, D_FF), F32)],
        sem=("arbitrary",), comm=comm)


def _adamw(w, g, m, v, name):
    rows, cols = w.shape
    tr = _pick(rows, (256, 128, 64, 32, 16, 8))

    def body(w_ref, g_ref, m_ref, v_ref, d_ref, nm_ref, nv_ref):
        d_ref[...], nm_ref[...], nv_ref[...] = _adamw_math(w_ref[...], g_ref[...], m_ref[...], v_ref[...])

    spec = pl.BlockSpec((tr, cols), lambda i: (i, 0))
    shp = jax.ShapeDtypeStruct((rows, cols), F32)
    return pl.pallas_call(
        body, name=name, grid=(rows // tr,),
        in_specs=[spec, spec, spec, spec], out_specs=[spec, spec, spec], out_shape=[shp, shp, shp],
        compiler_params=_cp("parallel"),
    )(w, g, m, v)


def _pad_rows(a, rows):
    return jnp.pad(a, ((0, rows - a.shape[0]), (0, 0)))


SMALL_LAYOUT = (("ln1_g", 1024), ("ln1_b", 1024), ("b_in", 2816), ("sinks", 8), ("hgrn_lb", 1024),
                ("hgrn_norm_g", 128), ("ln2_g", 1024), ("ln2_b", 1024), ("conv_b", 2816), ("loss", 1))
SMALL_SHAPES = {"ln1_g": (1, 1024), "ln1_b": (1, 1024), "b_in": (1, 2816), "sinks": (1, 8), "hgrn_lb": (2, 512),
                "hgrn_norm_g": (1, 128), "ln2_g": (1, 1024), "ln2_b": (1, 1024), "conv_b": (1, 2816),
                "loss": (1,)}


def _pack_small(parts):
    rows = []
    for name, size in SMALL_LAYOUT:
        flat = parts[name].reshape(-1).astype(F32)
        padded = -(-size // LANES) * LANES
        rows.append(jnp.pad(flat, (0, padded - size)).reshape(-1, LANES))
    return _pad_rows(jnp.concatenate(rows, axis=0), SMALL_ROWS)


def _unpack_small(pack):
    out, r = {}, 0
    for name, size in SMALL_LAYOUT:
        nrows = -(-size // LANES)
        out[name] = pack[r:r + nrows].reshape(-1)[:size].reshape(SMALL_SHAPES[name])
        r += nrows
    return out


def _own(full, rows, other=0):
    return lax.dynamic_slice_in_dim(full, jnp.bitwise_xor(_me(), other) * rows, rows, axis=0)


def kernel(x, positions, ln1_g, ln1_b, w_in, b_in, sinks, hgrn_lb, hgrn_norm_g, w_o, ln2_g, ln2_b, w_up, conv_w, conv_b, w_down, loss_target, m_ln1_g, m_ln1_b, m_w_in, m_b_in, m_sinks, m_hgrn_lb, m_hgrn_norm_g, m_w_o, m_ln2_g, m_ln2_b, m_w_up, m_conv_w, m_conv_b, m_w_down, v_ln1_g, v_ln1_b, v_w_in, v_b_in, v_sinks, v_hgrn_lb, v_hgrn_norm_g, v_w_o, v_ln2_g, v_ln2_b, v_w_up, v_conv_w, v_conv_b, v_w_down):
    t = x.shape[1]
    x2 = x[0]
    target = loss_target[0]
    pos_col = positions.reshape(t, 1)

    w_in_t_s = w_in[0].T.astype(BF16)
    w_up_t_s = w_up[0].T.astype(BF16)
    w_o_s = w_o[0].astype(BF16)
    w_down_s = w_down[0].astype(BF16)
    (ctab, stab, xb), (w_in_t_g, cw_g) = _prep(
        pos_col, x2, "prep_ag_w_in", _Comm([{"kind": "gather", "arr": w_in_t_s}, {"kind": "gather", "arr": _pad_rows(conv_w[0], 8)}]))
    w_in_t = w_in_t_g.reshape(D_FF, D_MODEL)
    w_a_t, w_h_t = w_in_t[:UA_W], w_in_t[UA_W:]
    conv_w_f = cw_g[:, 0:3].transpose(1, 0, 2).reshape(3, D_FF)

    ua = _mm(xb, w_a_t, tb=True, bias=b_in[:, :UA_W], name="fwd_in_attn")
    uh, (w_down_g,) = _mm(xb, w_h_t, tb=True, bias=b_in[:, UA_W:], name="fwd_in_hgrn",
                          comm=_Comm([{"kind": "gather", "arr": w_down_s}]))
    w_down_f = w_down_g.reshape(D_FF, D_MODEL)
    half_up = SHARD_UP // 2
    (a_out, a_out_t), (w_o_g, w_up_half) = _attn_fwd(
        ua, ctab, stab, sinks, "attn_fwd",
        comm=_Comm([{"kind": "gather", "arr": w_o_s},
                    {"kind": "gather", "arr": w_up_t_s, "rows": (0, half_up), "dst_rows": SHARD_UP}]))
    (r_out, r_out_t, o_pre, states), (w_up_t_g,) = _hgrn_fwd(
        uh, hgrn_lb, hgrn_norm_g, "hgrn_fwd",
        comm=_Comm([{"kind": "gather", "arr": w_up_t_s, "rows": (half_up, half_up), "dst_rows": SHARD_UP,
                     "dst_first": half_up, "into": w_up_half}]))
    w_o_f = w_o_g.reshape(D_MODEL, D_MODEL)
    w_up_t = w_up_t_g.reshape(2 * D_FF, D_MODEL)
    z1 = _mm(a_out, w_o_f[:ATTN_W], addend=x2, addend_scale=ALPHA, name="fwd_o_attn")
    h1, h1b, xhat1, rstd1 = _mm_ln_fwd(r_out, w_o_f[ATTN_W:], z1, ln1_g, ln1_b, "fwd_o_hgrn_ln1")
    u2, hmid, dz2, d_ln2_g, d_ln2_b, loss_part = _ffn_fwd(h1b, h1, w_up_t, conv_w_f, conv_b, w_down_f, target,
                                                         ln2_g, ln2_b, "ffn_fwd")[0]

    d_w_down, d_w_down_b = _mm(hmid, dz2, ta=True, out_dtype2=BF16, tm=1408, tk=512, name="bwd_down_dw")
    (d_u2, dz1, d_conv_w8, d_conv_b, d_ln1_g, d_ln1_b), (recv_down,) = _ffn_bwd(
        dz2, u2, w_down_f, w_up_t, conv_w_f, conv_b, xhat1, rstd1, ln1_g, "ffn_bwd",
        comm=_Comm([{"kind": "exchange", "arr": d_w_down_b.reshape(N_DEV, SHARD_DOWN, D_MODEL)}]))
    d_w_up_t, d_w_up_t_b = _mm(d_u2, h1b, ta=True, out_dtype2=BF16, tm=1408, tk=512, name="bwd_up_dw")
    d_a = _mm(dz1, w_o_f[:ATTN_W], tb=True, name="bwd_o_dx_attn")
    d_r = _mm(dz1, w_o_f[ATTN_W:], tb=True, name="bwd_o_dx_hgrn")
    d_w_o_a, d_w_o_a_b = _mm(a_out_t, dz1, out_dtype2=BF16, name="bwd_o_dw_attn")
    d_w_o_r, d_w_o_r_b = _mm(r_out_t, dz1, out_dtype2=BF16, name="bwd_o_dw_hgrn")
    d_w_o = jnp.concatenate([d_w_o_a, d_w_o_r], axis=0)
    d_w_o_b = jnp.concatenate([d_w_o_a_b, d_w_o_r_b], axis=0)
    d_w_up_x = d_w_up_t_b.reshape(N_DEV, SHARD_UP, D_MODEL)
    half = SHARD_UP // 2
    d_cw_x = d_conv_w8.reshape(8, N_DEV, SHARD_IN).transpose(1, 0, 2)
    (d_ua, d_ua_t, d_bias_a, d_sinks), (recv_up_half, recv_o, recv_cw) = _attn_bwd(
        ua, d_a, ctab, stab, sinks, "attn_bwd",
        comm=_Comm([{"kind": "exchange", "arr": d_w_up_x, "rows": (0, half), "dst_rows": SHARD_UP},
                    {"kind": "exchange", "arr": d_w_o_b.reshape(N_DEV, SHARD_O, D_MODEL)},
                    {"kind": "exchange", "arr": d_cw_x}]))
    (d_uh, d_uh_t, d_bias_h, d_norm_g, d_lb8), (recv_up,) = _hgrn_bwd(
        uh, o_pre, d_r, states, hgrn_lb, hgrn_norm_g, "hgrn_bwd",
        comm=_Comm([{"kind": "exchange", "arr": d_w_up_x, "rows": (half, half), "dst_rows": SHARD_UP,
                     "dst_first": half, "into": recv_up_half}]))
    d_w_a_t, d_w_a_t_b = _mm(d_ua_t, xb, out_dtype2=BF16, name="bwd_in_dw_attn")
    d_w_h_t, d_w_h_t_b = _mm(d_uh_t, xb, out_dtype2=BF16, name="bwd_in_dw_hgrn")
    d_w_in_t = jnp.concatenate([d_w_a_t, d_w_h_t], axis=0)
    d_w_in_t_b = jnp.concatenate([d_w_a_t_b, d_w_h_t_b], axis=0)
    small_local = _pack_small({
        "ln1_g": d_ln1_g, "ln1_b": d_ln1_b, "b_in": jnp.concatenate([d_bias_a, d_bias_h], axis=1),
        "sinks": d_sinks[:, :8], "hgrn_lb": d_lb8[0:2], "hgrn_norm_g": d_norm_g, "ln2_g": d_ln2_g,
        "ln2_b": d_ln2_b, "conv_b": d_conv_b, "loss": loss_part[:, :1]})
    d_w_in_x = d_w_in_t_b.reshape(N_DEV, SHARD_IN, D_MODEL)
    res_up, (from_sibling,) = _sum_shards_adamw(
        [recv_up], _own(d_w_up_t, SHARD_UP), w_up[0].T, m_w_up[0].T, v_w_up[0].T, "adamw_w_up",
        comm=_Comm([{"kind": "pair4", "arr": d_w_in_x}]))
    res_up = [r.T for r in res_up]
    mine4 = jnp.stack([_own(d_w_in_t, SHARD_IN, other=2 * r) for r in range(4)])
    own_in, chip_part = _pair_reduce(from_sibling, mine4, "pair_reduce_w_in")
    dx, (from_chips, small_g) = _mm(d_uh, w_h_t, addend=dz1, addend_scale=ALPHA, name="bwd_in_dx_hgrn",
                                    comm=_Comm([{"kind": "chips3", "arr": chip_part},
                                                {"kind": "gather", "arr": small_local}]))
    dx = _mm(d_ua, w_a_t, addend=dx, tk=768, name="bwd_in_dx_attn")

    res_in = [r.T for r in _chip_sum_adamw(from_chips, own_in, w_in[0].T, m_w_in[0].T, v_w_in[0].T, "adamw_w_in")]
    res_o = _sum_shards_adamw([recv_o], _own(d_w_o, SHARD_O), w_o[0], m_w_o[0], v_w_o[0], "adamw_w_o")
    res_down = _sum_shards_adamw([recv_down], _own(d_w_down, SHARD_DOWN), w_down[0], m_w_down[0], v_w_down[0],
                                 "adamw_w_down")
    g_cw = _sum_slots(recv_cw, "sum_conv_w")
    cw8 = lambda a: _pad_rows(a, 8)
    res_cw = (g_cw,) + tuple(_adamw(cw8(conv_w[0]), g_cw, cw8(m_conv_w[0]), cw8(v_conv_w[0]), "adamw_conv_w"))
    big = {"w_in": [r[None] for r in res_in], "w_up": [r[None] for r in res_up],
           "w_o": [r[None] for r in res_o], "w_down": [r[None] for r in res_down],
           "conv_w": [r[None, 0:3] for r in res_cw]}

    small_sum = _sum_slots(small_g, "ar_small_sum")
    gs = _unpack_small(small_sum)
    loss = gs["loss"][0]
    zero1 = jnp.zeros((1,), F32)
    w_small = _pack_small({"ln1_g": ln1_g, "ln1_b": ln1_b, "b_in": b_in, "sinks": sinks, "hgrn_lb": hgrn_lb,
                           "hgrn_norm_g": hgrn_norm_g, "ln2_g": ln2_g, "ln2_b": ln2_b, "conv_b": conv_b,
                           "loss": zero1})
    m_small = _pack_small({"ln1_g": m_ln1_g, "ln1_b": m_ln1_b, "b_in": m_b_in, "sinks": m_sinks,
                           "hgrn_lb": m_hgrn_lb, "hgrn_norm_g": m_hgrn_norm_g, "ln2_g": m_ln2_g,
                           "ln2_b": m_ln2_b, "conv_b": m_conv_b, "loss": zero1})
    v_small = _pack_small({"ln1_g": v_ln1_g, "ln1_b": v_ln1_b, "b_in": v_b_in, "sinks": v_sinks,
                           "hgrn_lb": v_hgrn_lb, "hgrn_norm_g": v_hgrn_norm_g, "ln2_g": v_ln2_g,
                           "ln2_b": v_ln2_b, "conv_b": v_conv_b, "loss": zero1})
    small = [gs] + [_unpack_small(p) for p in _adamw(w_small, small_sum, m_small, v_small, "adamw_small")]

    order = ["ln1_g", "ln1_b", "w_in", "b_in", "sinks", "hgrn_lb", "hgrn_norm_g", "w_o", "ln2_g", "ln2_b",
             "w_up", "conv_w", "conv_b", "w_down"]

    def pick(idx):
        return [big[n][idx] if n in big else small[idx][n] for n in order]

    return (loss, dx[None], *pick(0), *pick(1), *pick(2), *pick(3))
```

```python
import functools

import jax
import jax.numpy as jnp
import numpy as np
from jax import lax
from jax.experimental import pallas as pl
from jax.experimental.pallas import tpu as pltpu

F32 = jnp.float32
BF16 = jnp.bfloat16

N_DEV = 8
D_MODEL = 1024
D_FF = 2816
ATTN_W = 512
KV_W = 128
UA_W = ATTN_W + 2 * KV_W
UH_W = 2048
HG_W = 512
ATTN_BLOCK = 128
HGRN_CHUNK = 64
HGRN_SUB = 16
HGRN_CHUNKS_PER_STEP = 4
EXP_CLAMP = 85.0
NEG_BIG = -1e30
LN_EPS = 1e-5
RMS_EPS = 1e-6
ALPHA = 2.0 ** 0.25
ATTN_SCALE = 0.125
ROPE_THETA = 500000.0

ADAM_LR = 0.001
ADAM_B1 = 0.9
ADAM_B2 = 0.999
ADAM_EPS = 1e-08
ADAM_WD = 0.01
ADAM_STEP = 10

LANES = 128
VMEM_LIMIT_BYTES = 56 * 1024 * 1024

SHARD_IN = D_FF // N_DEV
SHARD_UP = 2 * D_FF // N_DEV
SHARD_O = D_MODEL // N_DEV
SHARD_DOWN = D_FF // N_DEV
SMALL_ROWS = 88

_MESH = pl.DeviceIdType.MESH
_NT = (((1,), (1,)), ((), ()))
_NN = (((1,), (0,)), ((), ()))
_TN = (((0,), (0,)), ((), ()))


def _cp(*sem):
    if sem:
        return pltpu.CompilerParams(dimension_semantics=sem, vmem_limit_bytes=VMEM_LIMIT_BYTES)
    return pltpu.CompilerParams(vmem_limit_bytes=VMEM_LIMIT_BYTES)


def _sig(x):
    return 0.5 * jnp.tanh(0.5 * x) + 0.5


def _dsilu(x, s):
    return s * (1.0 + x * (1.0 - s))


def _dot(a, b, dims):
    return lax.dot_general(a.astype(BF16), b.astype(BF16), dims, preferred_element_type=F32)


def _split(a):
    hi = a.astype(BF16)
    return hi, (a - hi.astype(F32)).astype(BF16)


def _dot3(a, b, dims):
    ah, al = _split(a)
    bh, bl = _split(b)
    d = functools.partial(lax.dot_general, dimension_numbers=dims, preferred_element_type=F32)
    return d(ah, bh) + (d(ah, bl) + d(al, bh))


def _pick(n, pref):
    for t in pref:
        if t <= n and n % t == 0:
            return t
    return n


def _my_coords():
    return lax.axis_index("x"), lax.axis_index("y"), lax.axis_index("c")


def _peer(k):
    x, y, c = _my_coords()
    return (1 - x if k & 4 else x, 1 - y if k & 2 else y, 1 - c if k & 1 else c)


def _me():
    x, y, c = _my_coords()
    return 4 * x + 2 * y + c


class _Comm:
    def __init__(self, items):
        self.items = []
        for it in items:
            arr = it["arr"]
            full = arr.shape[0] if it["kind"] == "gather" else arr.shape[1]
            first, count = it.get("rows", (0, full))
            self.items.append(dict(kind=it["kind"], arr=arr, first=first, count=count,
                                   dst_rows=it.get("dst_rows", count), dst_first=it.get("dst_first", 0),
                                   into=it.get("into")))
        self.n = len(self.items)
        self.arrays = [it["arr"] for it in self.items]
        self.intos = [(a, it["into"]) for a, it in enumerate(self.items) if it["into"] is not None]

    def out_shapes(self):
        return [jax.ShapeDtypeStruct((4 if it["kind"] in ("pair4", "chips3") else N_DEV, it["dst_rows"],
                                      it["arr"].shape[-1]), it["arr"].dtype) for it in self.items]

    def specs(self, n=None):
        return [pl.BlockSpec(memory_space=pl.ANY)] * (self.n if n is None else n)

    def scratch(self):
        return [pltpu.SemaphoreType.DMA(((N_DEV - 1) * self.n,)), pltpu.SemaphoreType.DMA(((N_DEV - 1) * self.n,)),
                pltpu.SemaphoreType.DMA((self.n,))]

    def _src(self, a, ref, dev):
        it = self.items[a]
        blk = ref if it["kind"] == "gather" else ref.at[dev]
        return blk.at[pl.ds(it["first"], it["count"])]

    def _dst(self, a, ref, slot):
        it = self.items[a]
        return ref.at[slot].at[pl.ds(it["dst_first"], it["count"])]

    def _copy(self, a, k, src, dst, sems, me, slot):
        other = jnp.bitwise_xor(me, k)
        idx = a * (N_DEV - 1) + k - 1
        return pltpu.make_async_remote_copy(
            src_ref=self._src(a, src, other), dst_ref=self._dst(a, dst, me if slot == "mine" else other),
            send_sem=sems[0].at[idx], recv_sem=sems[1].at[idx], device_id=_peer(k), device_id_type=_MESH)

    def _pass_on(self, a, k, dst, sems, me):
        slot = self._dst(a, dst, jnp.bitwise_xor(me, k))
        idx = a * (N_DEV - 1) + k
        return pltpu.make_async_remote_copy(
            src_ref=slot, dst_ref=slot, send_sem=sems[0].at[idx], recv_sem=sems[1].at[idx],
            device_id=_peer(1), device_id_type=_MESH)

    def _part(self, a, r, src, dst, sems, me):
        it = self.items[a]
        idx = a * (N_DEV - 1) + r
        if it["kind"] == "pair4":
            k, slot = 1, jnp.bitwise_xor(jnp.bitwise_xor(me, 1), 2 * r)
        else:
            k, slot = 2 * r, r
        return pltpu.make_async_remote_copy(
            src_ref=src.at[slot].at[pl.ds(it["first"], it["count"])], dst_ref=self._dst(a, dst, r),
            send_sem=sems[0].at[idx], recv_sem=sems[1].at[idx], device_id=_peer(k), device_id_type=_MESH)

    def _parts(self, a):
        return range(4) if self.items[a]["kind"] == "pair4" else range(1, 4)

    def _local(self, a, src, dst, sems, me):
        return pltpu.make_async_copy(self._src(a, src, me), self._dst(a, dst, me), sems[2].at[a])

    def start(self, srcs, dsts, sems):
        me = _me()
        for a, (src, dst) in enumerate(zip(srcs, dsts)):
            if self.items[a]["kind"] in ("pair4", "chips3"):
                for r in self._parts(a):
                    self._part(a, r, src, dst, sems, me).start()
                continue
            direct = (1, 2, 4, 6) if self.items[a]["kind"] == "gather" else range(1, N_DEV)
            self._local(a, src, dst, sems, me).start()
            for k in direct:
                self._copy(a, k, src, dst, sems, me, "mine").start()

    def wait(self, srcs, dsts, sems):
        me = _me()
        for a, (src, dst) in enumerate(zip(srcs, dsts)):
            if self.items[a]["kind"] in ("pair4", "chips3"):
                for r in self._parts(a):
                    self._part(a, r, src, dst, sems, me).wait_recv()
                for r in self._parts(a):
                    self._part(a, r, src, dst, sems, me).wait_send()
                continue
            if self.items[a]["kind"] == "gather":
                for k in (2, 4, 6):
                    self._copy(a, k, src, dst, sems, me, "theirs").wait_recv()
                    self._pass_on(a, k, dst, sems, me).start()
                for k in (1, 3, 5, 7):
                    self._copy(a, k, src, dst, sems, me, "theirs").wait_recv()
                for k in (1, 2, 4, 6):
                    self._copy(a, k, src, dst, sems, me, "mine").wait_send()
                for k in (2, 4, 6):
                    self._pass_on(a, k, dst, sems, me).wait_send()
            else:
                for k in range(1, N_DEV):
                    self._copy(a, k, src, dst, sems, me, "theirs").wait_recv()
                for k in range(1, N_DEV):
                    self._copy(a, k, src, dst, sems, me, "mine").wait_send()
            self._local(a, src, dst, sems, me).wait()


def _call(body, *, name, grid, ins, in_specs, out_specs, out_shape, scratch_shapes=(), sem, comm=None):
    n_in, n_out, n_scr = len(ins), len(out_shape), len(scratch_shapes)
    if comm is None:
        outs = pl.pallas_call(
            body, name=name, grid=grid, in_specs=list(in_specs), out_specs=list(out_specs),
            out_shape=list(out_shape), scratch_shapes=list(scratch_shapes), compiler_params=_cp(*sem))(*ins)
        return list(outs), []
    nc, n_into = comm.n, len(comm.intos)

    def hosted(*refs):
        pos = n_in
        c_in = refs[pos:pos + nc]
        pos += nc + n_into
        outs = refs[pos:pos + n_out]
        pos += n_out
        c_out = refs[pos:pos + nc]
        pos += nc
        scr = refs[pos:pos + n_scr]
        sems = refs[pos + n_scr:]
        ids = [pl.program_id(d) for d in range(len(grid))]
        first = functools.reduce(jnp.logical_and, [i == 0 for i in ids])
        last = functools.reduce(jnp.logical_and, [i == g - 1 for i, g in zip(ids, grid)])

        @pl.when(first)
        def _():
            comm.start(c_in, c_out, sems)

        body(*refs[:n_in], *outs, *scr)

        @pl.when(last)
        def _():
            comm.wait(c_in, c_out, sems)

    aliases = {n_in + nc + j: n_out + a for j, (a, _) in enumerate(comm.intos)}
    outs = pl.pallas_call(
        hosted, name=name, grid=grid, in_specs=list(in_specs) + comm.specs() + comm.specs(n_into),
        out_specs=list(out_specs) + comm.specs(), out_shape=list(out_shape) + comm.out_shapes(),
        scratch_shapes=list(scratch_shapes) + comm.scratch(), input_output_aliases=aliases,
        compiler_params=_cp(*(["arbitrary"] * len(grid))))(*ins, *comm.arrays, *[arr for _, arr in comm.intos])
    return list(outs[:n_out]), list(outs[n_out:])


def _sum_slots(gathered, name):
    _, rows, cols = gathered.shape

    def body(g_ref, out_ref):
        acc = g_ref[0]
        for s in range(1, N_DEV):
            acc = acc + g_ref[s]
        out_ref[...] = acc

    return pl.pallas_call(
        body, name=name,
        out_shape=jax.ShapeDtypeStruct((rows, cols), F32),
        compiler_params=_cp(),
    )(gathered)


def _slot_sum(recv_ref, own_ref, shape):
    me = _me()
    acc = jnp.zeros(shape, F32)
    for s in range(N_DEV):
        acc = acc + jnp.where(me == s, own_ref[...], recv_ref[s].astype(F32))
    return acc


def _adamw_math(w, g, m, v):
    nm = ADAM_B1 * m + (1.0 - ADAM_B1) * g
    nv = ADAM_B2 * v + (1.0 - ADAM_B2) * (g * g)
    m_hat = nm / (1.0 - ADAM_B1 ** ADAM_STEP)
    v_hat = nv / (1.0 - ADAM_B2 ** ADAM_STEP)
    return -ADAM_LR * (m_hat / (jnp.sqrt(v_hat) + ADAM_EPS) + ADAM_WD * w), nm, nv


def _pair_reduce(from_sibling, mine, name):
    _, rows, cols = from_sibling.shape
    tr = _pick(rows, (176, 128, 64, 32, 16, 8))
    tiles = rows // tr
    table = jnp.bitwise_xor(_me(), jnp.arange(0, N_DEV, 2, dtype=jnp.int32))

    def body(tbl_ref, sib_ref, mine_ref, own_ref, send_ref):
        r = pl.program_id(1)
        total = mine_ref[...] + sib_ref[0].astype(F32)
        send_ref[0] = jnp.where(r == 0, 0.0, total).astype(BF16)

        @pl.when(r == 0)
        def _():
            own_ref[...] = total

    grid_spec = pltpu.PrefetchScalarGridSpec(
        num_scalar_prefetch=1, grid=(tiles, 4),
        in_specs=[pl.BlockSpec((1, tr, cols), lambda i, r, tbl: (r, i, 0)),
                  pl.BlockSpec((tr, cols), lambda i, r, tbl: (tbl[r] * tiles + i, 0))],
        out_specs=[pl.BlockSpec((tr, cols), lambda i, r, tbl: (i, 0)),
                   pl.BlockSpec((1, tr, cols), lambda i, r, tbl: (r, i, 0))])
    return pl.pallas_call(
        body, name=name, grid_spec=grid_spec,
        out_shape=[jax.ShapeDtypeStruct((rows, cols), F32), jax.ShapeDtypeStruct((4, rows, cols), BF16)],
        compiler_params=_cp("arbitrary", "arbitrary"),
    )(table, from_sibling, mine)


def _chip_sum_adamw(from_chips, own, w, m, v, name):
    _, rows, cols = from_chips.shape
    tr = _pick(rows, (176, 128, 64, 32, 16, 8))

    def body(recv_ref, own_ref, w_ref, m_ref, v_ref, g_ref, d_ref, nm_ref, nv_ref):
        g = own_ref[...]
        for r in range(1, 4):
            g = g + recv_ref[r].astype(F32)
        g_ref[...] = g
        d_ref[...], nm_ref[...], nv_ref[...] = _adamw_math(w_ref[...], g, m_ref[...], v_ref[...])

    spec = pl.BlockSpec((tr, cols), lambda i: (i, 0))
    shp = jax.ShapeDtypeStruct((rows, cols), F32)
    return pl.pallas_call(
        body, name=name, grid=(rows // tr,),
        in_specs=[pl.BlockSpec((4, tr, cols), lambda i: (0, i, 0)), spec, spec, spec, spec],
        out_specs=[spec, spec, spec, spec], out_shape=[shp, shp, shp, shp],
        compiler_params=_cp("parallel"),
    )(from_chips, own, w, m, v)


def _sum_shards_adamw(recvs, own, w, m, v, name, comm=None):
    rows_p, cols = recvs[0].shape[1], recvs[0].shape[2]
    n_p = len(recvs)
    tr = _pick(rows_p, (176, 128, 64, 32, 16, 8))
    tiles = rows_p // tr

    def body(*refs):
        recv_refs = refs[:n_p]
        own_ref, w_ref, m_ref, v_ref, g_ref, d_ref, nm_ref, nv_ref = refs[n_p:]
        for j in range(n_p):
            @pl.when(pl.program_id(0) == j)
            def _():
                g = _slot_sum(recv_refs[j], own_ref, (tr, cols))
                g_ref[...] = g
                d_ref[...], nm_ref[...], nv_ref[...] = _adamw_math(w_ref[...], g, m_ref[...], v_ref[...])

    spec = pl.BlockSpec((tr, cols), lambda p_, i: (p_ * tiles + i, 0))
    shp = jax.ShapeDtypeStruct((rows_p * n_p, cols), F32)
    outs, couts = _call(
        body, name=name, grid=(n_p, tiles), ins=[*recvs, own, w, m, v],
        in_specs=[pl.BlockSpec((N_DEV, tr, cols), functools.partial(lambda p_, i, j: (0, jnp.where(p_ == j, i, 0), 0), j=j))
                  for j in range(n_p)] + [spec, spec, spec, spec],
        out_specs=[spec, spec, spec, spec], out_shape=[shp, shp, shp, shp],
        sem=("arbitrary", "arbitrary"), comm=comm)
    return outs if comm is None else (outs, couts)


def _mm(a, b, *, name, ta=False, tb=False, out_dtype=F32, out_dtype2=None, bias=None, addend=None,
        addend_scale=1.0, tm=1024, tn=1024, tk=1024, comm=None):
    kdim, m = a.shape if ta else a.shape[::-1]
    n = b.shape[0] if tb else b.shape[1]
    tm = _pick(m, (tm, 1408, 1024, 768, 512, 256, 128))
    tn = _pick(n, (tn, 1408, 1024, 768, 512, 256, 128))
    tk = _pick(kdim, (tk, 1408, 1024, 768, 512, 256, 128))
    nk = kdim // tk
    a_spec = pl.BlockSpec((tk, tm), lambda i, j, k: (k, i)) if ta else pl.BlockSpec((tm, tk), lambda i, j, k: (i, k))
    b_spec = pl.BlockSpec((tn, tk), lambda i, j, k: (j, k)) if tb else pl.BlockSpec((tk, tn), lambda i, j, k: (k, j))
    ins, specs = [a, b], [a_spec, b_spec]
    if bias is not None:
        ins.append(bias)
        specs.append(pl.BlockSpec((1, tn), lambda i, j, k: (0, j)))
    if addend is not None:
        ins.append(addend)
        specs.append(pl.BlockSpec((tm, tn), lambda i, j, k: (i, j)))
    dims = (((0,) if ta else (1,), (1,) if tb else (0,)), ((), ()))
    has_bias, has_addend, two = bias is not None, addend is not None, out_dtype2 is not None

    def body(*refs):
        a_ref, b_ref = refs[0], refs[1]
        pos = 2
        bias_ref = addend_ref = None
        if has_bias:
            bias_ref = refs[pos]
            pos += 1
        if has_addend:
            addend_ref = refs[pos]
            pos += 1
        o_refs, acc_ref = refs[pos:-1], refs[-1]
        k = pl.program_id(2)

        @pl.when(k == 0)
        def _():
            acc_ref[...] = jnp.zeros_like(acc_ref)

        acc_ref[...] += _dot(a_ref[...], b_ref[...], dims)

        @pl.when(k == nk - 1)
        def _():
            r = acc_ref[...]
            if has_bias:
                r = r + bias_ref[...]
            if has_addend:
                r = r + addend_scale * addend_ref[...].astype(F32)
            for o_ref in o_refs:
                o_ref[...] = r.astype(o_ref.dtype)

    ospec = pl.BlockSpec((tm, tn), lambda i, j, k: (i, j))
    dtypes = [out_dtype] + ([out_dtype2] if two else [])
    outs, couts = _call(
        body, name=name, grid=(m // tm, n // tn, nk), ins=ins, in_specs=specs,
        out_specs=[ospec] * len(dtypes), out_shape=[jax.ShapeDtypeStruct((m, n), d) for d in dtypes],
        scratch_shapes=[pltpu.VMEM((tm, tn), F32)], sem=("parallel", "parallel", "arbitrary"), comm=comm)
    primary = tuple(outs) if two else outs[0]
    return (primary, couts) if comm is not None else primary


def _rope_lane_constants():
    inv_freq = np.float32(ROPE_THETA) ** (-np.arange(8, dtype=np.float32) * np.float32(2.0 / 16.0))
    lane = np.arange(LANES) % 64
    freq = np.where(lane < 16, inv_freq[lane % 8], 0.0).astype(np.float32)
    sign = np.where(lane < 8, -1.0, np.where(lane < 16, 1.0, 0.0)).astype(np.float32)
    return jnp.asarray(freq)[None, :], jnp.asarray(sign)[None, :]


def _prep(pos_col, x2, name, comm):
    t, d = x2.shape
    tr = _pick(t, (512, 256, 128))
    freq, sign = _rope_lane_constants()

    def body(pos_ref, freq_ref, sign_ref, x_ref, c_ref, s_ref, xb_ref):
        ang = pos_ref[...].astype(F32) * freq_ref[...]
        c_ref[...] = jnp.cos(ang)
        s_ref[...] = sign_ref[...] * jnp.sin(ang)
        xb_ref[...] = x_ref[...].astype(BF16)

    tab = pl.BlockSpec((tr, LANES), lambda i: (i, 0))
    return _call(
        body, name=name, grid=(t // tr,), ins=[pos_col, freq, sign, x2],
        in_specs=[pl.BlockSpec((tr, 1), lambda i: (i, 0)), pl.BlockSpec((1, LANES), lambda i: (0, 0)),
                  pl.BlockSpec((1, LANES), lambda i: (0, 0)), pl.BlockSpec((tr, d), lambda i: (i, 0))],
        out_specs=[tab, tab, pl.BlockSpec((tr, d), lambda i: (i, 0))],
        out_shape=[jax.ShapeDtypeStruct((t, LANES), F32), jax.ShapeDtypeStruct((t, LANES), F32),
                   jax.ShapeDtypeStruct((t, d), BF16)],
        sem=("parallel",), comm=comm)


def _swap8(t):
    width = t.shape[1]
    lane = jnp.bitwise_and(lax.broadcasted_iota(jnp.int32, t.shape, 1), 63)
    return jnp.where(lane < 8, pltpu.roll(t, width - 8, 1), jnp.where(lane < 16, pltpu.roll(t, 8, 1), 0.0))


def _rope(t, c, s):
    return t * c + _swap8(t) * s


def _rope_bwd(d, c, s):
    return d * c + _swap8(d * s)


def _tile4(a):
    return jnp.concatenate([a, a, a, a], axis=1)


def _attn_band(n, k_cur, k_prev, v_cur, v_prev, c_cur, s_cur, c_prev, s_prev):
    kband = jnp.concatenate([_rope(k_prev, c_prev, s_prev), _rope(k_cur, c_cur, s_cur)], axis=0)
    vband = jnp.concatenate([v_prev, v_cur], axis=0)
    qi = lax.broadcasted_iota(jnp.int32, (ATTN_BLOCK, 2 * ATTN_BLOCK), 0)
    kj = lax.broadcasted_iota(jnp.int32, (ATTN_BLOCK, 2 * ATTN_BLOCK), 1)
    dist = qi + ATTN_BLOCK - kj
    valid = (dist >= 0) & (dist < ATTN_BLOCK) & (n * ATTN_BLOCK - ATTN_BLOCK + kj >= 0)
    return (kband.astype(BF16), pltpu.roll(kband, 64, 1).astype(BF16),
            vband.astype(BF16), pltpu.roll(vband, 64, 1).astype(BF16), valid, kband)


def _attn_probs(raw, valid, sink, axis):
    s = jnp.where(valid, raw * ATTN_SCALE, NEG_BIG)
    m = jnp.maximum(jnp.max(s, axis=axis, keepdims=True), sink)
    p = jnp.exp(s - m)
    esink = jnp.exp(sink - m)
    z = jnp.sum(p, axis=axis, keepdims=True) + esink
    return p / z, esink / z


def _attn_valid_t(n):
    kj = lax.broadcasted_iota(jnp.int32, (2 * ATTN_BLOCK, ATTN_BLOCK), 0)
    qi = lax.broadcasted_iota(jnp.int32, (2 * ATTN_BLOCK, ATTN_BLOCK), 1)
    dist = qi + ATTN_BLOCK - kj
    return (dist >= 0) & (dist < ATTN_BLOCK) & (n * ATTN_BLOCK - ATTN_BLOCK + kj >= 0)


def _attn_specs(nb):
    def cur(col, width=KV_W):
        return pl.BlockSpec((ATTN_BLOCK, width), lambda n: (jnp.minimum(n, nb - 1), col))

    def prev(col):
        return pl.BlockSpec((ATTN_BLOCK, KV_W), lambda n: (jnp.maximum(n - 1, 0), col))

    ua_specs = [cur(0, ATTN_W), cur(4), prev(4), cur(5), prev(5)]
    tab_specs = [cur(0), cur(0), prev(0), prev(0)]
    return ua_specs, tab_specs


def _attn_fwd(ua, ctab, stab, sinks, name, comm=None):
    t = ua.shape[0]
    nb = t // ATTN_BLOCK
    ua_specs, tab_specs = _attn_specs(nb)

    def body(q_ref, kc_ref, kp_ref, vc_ref, vp_ref, cc_ref, sc_ref, cp_ref, sp_ref, sink_ref, o_ref, o_t_ref):
        n = pl.program_id(0)
        cc, sc = cc_ref[...], sc_ref[...]
        kb, kb_r, vb, vb_r, valid, _ = _attn_band(n, kc_ref[...], kp_ref[...], vc_ref[...], vp_ref[...],
                                                  cc, sc, cp_ref[...], sp_ref[...])
        qr = _rope(q_ref[...], _tile4(cc), _tile4(sc))
        lo = lax.broadcasted_iota(jnp.int32, (ATTN_BLOCK, LANES), 1) < 64
        heads = []
        for j in range(4):
            qj = qr[:, j * LANES:(j + 1) * LANES]
            for is_lo in (True, False):
                aligned = is_lo == (j < 2)
                qm = jnp.where(lo if is_lo else jnp.logical_not(lo), qj, 0.0).astype(BF16)
                raw = lax.dot_general(qm, kb if aligned else kb_r, _NT, preferred_element_type=F32)
                heads.append((raw, vb if aligned else vb_r, sink_ref[0, len(heads)]))
        halves = []
        for raw, vv, sink in heads:
            probs, _ = _attn_probs(raw, valid, sink, 1)
            halves.append(lax.dot_general(probs.astype(BF16), vv, _NN, preferred_element_type=F32))
        outs = [jnp.where(lo, halves[2 * j], halves[2 * j + 1]) for j in range(4)]
        o_ref[...] = jnp.concatenate(outs, axis=1).astype(o_ref.dtype)
        for j in range(4):
            o_t_ref[j * LANES:(j + 1) * LANES, :] = outs[j].T.astype(o_t_ref.dtype)

    return _call(
        body, name=name, grid=(nb,), ins=[ua, ua, ua, ua, ua, ctab, stab, ctab, stab, sinks],
        in_specs=ua_specs + tab_specs + [pl.BlockSpec(memory_space=pltpu.SMEM)],
        out_specs=[pl.BlockSpec((ATTN_BLOCK, ATTN_W), lambda n: (n, 0)),
                   pl.BlockSpec((ATTN_W, ATTN_BLOCK), lambda n: (0, n))],
        out_shape=[jax.ShapeDtypeStruct((t, ATTN_W), BF16), jax.ShapeDtypeStruct((ATTN_W, t), BF16)],
        sem=("parallel",), comm=comm)


def _attn_bwd(ua, d_out, ctab, stab, sinks, name, comm=None):
    t = ua.shape[0]
    nb = t // ATTN_BLOCK
    ua_specs, tab_specs = _attn_specs(nb)

    def body(q_ref, kc_ref, kp_ref, vc_ref, vp_ref, cc_ref, sc_ref, cp_ref, sp_ref, do_ref, sink_ref,
             dua_ref, dua_t_ref, dbias_ref, dsink_ref, dq_c, dk_c, dv_c, dq_n, dk_n, dv_n):
        n = pl.program_id(0)

        @pl.when(n == 0)
        def _():
            dq_c[...] = jnp.zeros_like(dq_c)
            dk_c[...] = jnp.zeros_like(dk_c)
            dv_c[...] = jnp.zeros_like(dv_c)
            dbias_ref[...] = jnp.zeros_like(dbias_ref)
            dsink_ref[...] = jnp.zeros_like(dsink_ref)

        @pl.when(n == nb)
        def _():
            dq_n[...] = jnp.zeros_like(dq_n)
            dk_n[...] = jnp.zeros_like(dk_n)
            dv_n[...] = jnp.zeros_like(dv_n)

        @pl.when(n < nb)
        def _():
            cc, sc = cc_ref[...], sc_ref[...]
            kb, kb_r, vb, vb_r, _, kb_f32 = _attn_band(n, kc_ref[...], kp_ref[...], vc_ref[...], vp_ref[...],
                                                       cc, sc, cp_ref[...], sp_ref[...])
            valid_t = _attn_valid_t(n)
            c4, s4 = _tile4(cc), _tile4(sc)
            qr = _rope(q_ref[...], c4, s4)
            do = do_ref[...].astype(F32)
            lane = lax.broadcasted_iota(jnp.int32, (ATTN_BLOCK, LANES), 1)
            lo = lane < 64
            lane_row = lax.broadcasted_iota(jnp.int32, (1, LANES), 1)
            k_t = {False: kb_f32.T.astype(BF16), True: pltpu.roll(kb_f32, 64, 1).T.astype(BF16)}
            heads = []
            for j in range(4):
                qj = qr[:, j * LANES:(j + 1) * LANES]
                doj = do[:, j * LANES:(j + 1) * LANES]
                for is_lo in (True, False):
                    aligned = is_lo == (j < 2)
                    msk = lo if is_lo else jnp.logical_not(lo)
                    kk = kb if aligned else kb_r
                    vv = vb if aligned else vb_r
                    qm = jnp.where(msk, qj, 0.0).astype(BF16)
                    dom = jnp.where(msk, doj, 0.0).astype(BF16)
                    heads.append(dict(
                        aligned=aligned, qm=qm, dom=dom, sink=sink_ref[0, len(heads)],
                        raw_t=lax.dot_general(kk, qm, _NT, preferred_element_type=F32),
                        dp_t=lax.dot_general(vv, dom, _NT, preferred_element_type=F32)))
            dk_band = jnp.zeros((2 * ATTN_BLOCK, LANES), F32)
            dv_band = jnp.zeros((2 * ATTN_BLOCK, LANES), F32)
            dsink = jnp.zeros((1, LANES), F32)
            for head, hd in enumerate(heads):
                probs_t, psink = _attn_probs(hd["raw_t"], valid_t, hd["sink"], 0)
                delta_t = jnp.sum(probs_t * hd["dp_t"], axis=0, keepdims=True)
                hd["ds_t"] = (probs_t * (hd["dp_t"] - delta_t) * ATTN_SCALE).astype(BF16)
                dsink = dsink + jnp.where(lane_row == head, -jnp.sum(psink * delta_t), 0.0)
                dk_h = lax.dot_general(hd["ds_t"], hd["qm"], _NN, preferred_element_type=F32)
                dv_h = lax.dot_general(probs_t.astype(BF16), hd["dom"], _NN, preferred_element_type=F32)
                if not hd["aligned"]:
                    dk_h = pltpu.roll(dk_h, 64, 1)
                    dv_h = pltpu.roll(dv_h, 64, 1)
                dk_band = dk_band + dk_h
                dv_band = dv_band + dv_h
            row_lo = lax.broadcasted_iota(jnp.int32, (LANES, ATTN_BLOCK), 0) < 64
            dq_t = [lax.dot_general(k_t[not hd["aligned"]], hd["ds_t"], _NN, preferred_element_type=F32)
                    for hd in heads]
            dqs = [jnp.where(row_lo, dq_t[2 * j], dq_t[2 * j + 1]).T for j in range(4)]
            dq_n[...] = _rope_bwd(jnp.concatenate(dqs, axis=1), c4, s4)
            dk_n[...] = dk_band
            dv_n[...] = dv_band
            dsink_ref[...] += dsink

        dk_prev = _rope_bwd(dk_c[...] + dk_n[0:ATTN_BLOCK, :], cp_ref[...], sp_ref[...])
        dv_prev = dv_c[...] + dv_n[0:ATTN_BLOCK, :]
        full = jnp.concatenate([dq_c[...], dk_prev, dv_prev], axis=1)
        dua_ref[...] = full.astype(dua_ref.dtype)
        for j in range(UA_W // LANES):
            dua_t_ref[j * LANES:(j + 1) * LANES, :] = full[:, j * LANES:(j + 1) * LANES].T.astype(dua_t_ref.dtype)
        dbias_ref[...] += jnp.sum(full, axis=0, keepdims=True)
        dq_c[...] = dq_n[...]
        dk_c[...] = dk_n[ATTN_BLOCK:, :]
        dv_c[...] = dv_n[ATTN_BLOCK:, :]

    return _call(
        body, name=name, grid=(nb + 1,), ins=[ua, ua, ua, ua, ua, ctab, stab, ctab, stab, d_out, sinks],
        in_specs=ua_specs + tab_specs + [
            pl.BlockSpec((ATTN_BLOCK, ATTN_W), lambda n: (jnp.minimum(n, nb - 1), 0)),
            pl.BlockSpec(memory_space=pltpu.SMEM)],
        out_specs=[pl.BlockSpec((ATTN_BLOCK, UA_W), lambda n: (jnp.maximum(n - 1, 0), 0)),
                   pl.BlockSpec((UA_W, ATTN_BLOCK), lambda n: (0, jnp.maximum(n - 1, 0))),
                   pl.BlockSpec((1, UA_W), lambda n: (0, 0)),
                   pl.BlockSpec((1, LANES), lambda n: (0, 0))],
        out_shape=[jax.ShapeDtypeStruct((t, UA_W), BF16), jax.ShapeDtypeStruct((UA_W, t), BF16),
                   jax.ShapeDtypeStruct((1, UA_W), F32),
                   jax.ShapeDtypeStruct((1, LANES), F32)],
        scratch_shapes=[pltpu.VMEM((ATTN_BLOCK, ATTN_W), F32), pltpu.VMEM((ATTN_BLOCK, KV_W), F32),
                        pltpu.VMEM((ATTN_BLOCK, KV_W), F32), pltpu.VMEM((ATTN_BLOCK, ATTN_W), F32),
                        pltpu.VMEM((2 * ATTN_BLOCK, KV_W), F32), pltpu.VMEM((2 * ATTN_BLOCK, KV_W), F32)],
        sem=("arbitrary",), comm=comm)


def _tri_mats():
    r = lax.broadcasted_iota(jnp.int32, (HGRN_CHUNK, LANES), 0)
    c = lax.broadcasted_iota(jnp.int32, (HGRN_CHUNK, LANES), 1)
    lower = ((c <= r) & (c < HGRN_CHUNK)).astype(F32)
    upper = ((c >= r) & (c < HGRN_CHUNK)).astype(F32)
    return lower, upper


def _tri_apply(tri, g):
    pad = jnp.concatenate([g, jnp.zeros_like(g)], axis=0)
    return lax.dot_general(tri, pad, _NN, precision=lax.Precision.HIGHEST, preferred_element_type=F32)


def _sub_masks():
    s = lax.broadcasted_iota(jnp.int32, (HGRN_CHUNK, LANES), 0)
    tt = lax.broadcasted_iota(jnp.int32, (HGRN_CHUNK, LANES), 1)
    return [(tt >= HGRN_SUB * i) & (tt < HGRN_SUB * (i + 1)) & (s <= tt) for i in range(HGRN_CHUNK // HGRN_SUB)]


def _hgrn_gates(hq, hf, lb_ref, b_scr):
    lb = _sig(lb_ref[0:1, :] - lb_ref[1:2, :])
    q = hq * _sig(hq)
    sg = _sig(hf)
    f = lb + (1.0 - lb) * sg
    k = 1.0 - f
    lower, _ = _tri_mats()
    b = _tri_apply(lower, jnp.log(f))
    b_scr[...] = b
    nsub = HGRN_CHUNK // HGRN_SUB
    starts = [jnp.zeros((1, HG_W), F32)] + [b_scr[HGRN_SUB * i - 1:HGRN_SUB * i, :] for i in range(1, nsub)]
    pq = jnp.concatenate([jnp.broadcast_to(p, (HGRN_SUB, HG_W)) for p in starts], axis=0)
    b_last = b_scr[HGRN_CHUNK - 1:HGRN_CHUNK, :]
    e_q = jnp.exp(b - pq)
    e_k = [jnp.exp(jnp.minimum(p - b, EXP_CLAMP)) for p in starts]
    e_b = jnp.exp(b)
    e_bl = jnp.exp(b_last - b)
    e_last = jnp.exp(b_last)
    return q, sg, f, k, lb, e_q, e_k, e_b, e_bl, e_last


def _sub_masks_ts():
    tt = lax.broadcasted_iota(jnp.int32, (HGRN_CHUNK, LANES), 0)
    s = lax.broadcasted_iota(jnp.int32, (HGRN_CHUNK, LANES), 1)
    return [(tt >= HGRN_SUB * i) & (tt < HGRN_SUB * (i + 1)) & (s <= tt) for i in range(HGRN_CHUNK // HGRN_SUB)]


def _masked_sum(blocks, masks, axis):
    step = HGRN_CHUNK if axis == 0 else LANES
    acc = jnp.zeros((HGRN_CHUNK, LANES), F32)
    for i, msk in enumerate(masks):
        blk = blocks[step * i:step * (i + 1), :] if axis == 0 else blocks[:, step * i:step * (i + 1)]
        acc = acc + jnp.where(msk, blk, 0.0)
    return acc


def _store_transposed(out_t_ref, chunk_rows):
    width = chunk_rows[0].shape[1]
    if len(chunk_rows) == 1:
        groups = [jnp.concatenate([chunk_rows[0], jnp.zeros_like(chunk_rows[0])], axis=0)]
    else:
        groups = [jnp.concatenate(chunk_rows[g:g + 2], axis=0) for g in range(0, len(chunk_rows), 2)]
    for g, rows in enumerate(groups):
        for c in range(width // LANES):
            tile = rows[:, c * LANES:(c + 1) * LANES].T.astype(out_t_ref.dtype)
            if len(chunk_rows) == 1:
                out_t_ref[c * LANES:(c + 1) * LANES, :] = tile[:, 0:HGRN_CHUNK]
            else:
                out_t_ref[c * LANES:(c + 1) * LANES, g * LANES:(g + 1) * LANES] = tile


def _hgrn_chunk_inputs(j, hq_ref, hf_ref, hi_ref, hg_ref, lb_ref, b_scr):
    rows = slice(j * HGRN_CHUNK, (j + 1) * HGRN_CHUNK)
    hq, hf, v, hg = hq_ref[rows, :], hf_ref[rows, :], hi_ref[rows, :], hg_ref[rows, :]
    q, sg, f, k, lb, e_q, e_k, e_b, e_bl, e_last = _hgrn_gates(hq, hf, lb_ref, b_scr.at[j])
    return dict(rows=rows, hq=hq, v=v, hg=hg, q=q, sg=sg, f=f, k=k, lb=lb, e_q=e_q, e_k=e_k, e_b=e_b, e_bl=e_bl,
                e_last=e_last, qt=q * e_q, qb=q * e_b, kd=k * e_bl, khat=[k * e for e in e_k])


def _hgrn_fwd(uh, lb_raw, norm_g, name, comm=None):
    t = uh.shape[0]
    nc = t // HGRN_CHUNK
    cps = _pick(nc, (HGRN_CHUNKS_PER_STEP, 2, 1))
    rows_step = cps * HGRN_CHUNK

    def body(hq_ref, hf_ref, hi_ref, hg_ref, lb_ref, ng_ref, r_ref, r_t_ref, o_ref, st_out_ref, st_ref, b_scr):
        @pl.when(pl.program_id(0) == 0)
        def _():
            st_ref[...] = jnp.zeros_like(st_ref)

        masks = _sub_masks_ts()
        ng = ng_ref[...]
        zpad = jnp.zeros((HGRN_CHUNK, LANES), F32)
        heads = [slice(h * LANES, (h + 1) * LANES) for h in range(4)]
        chunks = [_hgrn_chunk_inputs(j, hq_ref, hf_ref, hi_ref, hg_ref, lb_ref, b_scr) for j in range(cps)]
        for ch in chunks:
            ch["scores"] = [_dot3(ch["qt"][:, sl],
                                  jnp.concatenate([x for kh in ch["khat"] for x in (kh[:, sl], zpad)], axis=0), _NT)
                            for sl in heads]
        for j, ch in enumerate(chunks):
            o_heads, y_heads = [], []
            for h, sl in enumerate(heads):
                a_ts = _masked_sum(ch["scores"][h], masks, 1)
                vh = ch["v"][:, sl].astype(BF16)
                v_pad = jnp.concatenate([vh, jnp.zeros_like(vh)], axis=0)
                o_intra = lax.dot_general(a_ts.astype(BF16), v_pad, _NN, preferred_element_type=F32)
                st = st_ref[h]
                st_out_ref[j, h] = st
                o_inter = _dot(ch["qb"][:, sl], st, _NT)
                st_ref[h] = st * ch["e_last"][:, sl] + _dot(vh, ch["kd"][:, sl], _TN)
                oh = o_intra + o_inter
                rs = lax.rsqrt(jnp.mean(oh * oh, axis=1, keepdims=True) + RMS_EPS)
                o_heads.append(oh)
                y_heads.append(oh * rs * ng)
            hg = ch["hg"]
            o_ref[ch["rows"], :] = jnp.concatenate(o_heads, axis=1)
            ch["r"] = jnp.concatenate(y_heads, axis=1) * (hg * _sig(hg))
            r_ref[ch["rows"], :] = ch["r"].astype(r_ref.dtype)
        _store_transposed(r_t_ref, [ch["r"] for ch in chunks])

    col = lambda j: pl.BlockSpec((rows_step, HG_W), lambda c: (c, j))
    return _call(
        body, name=name, grid=(nc // cps,), ins=[uh, uh, uh, uh, lb_raw, norm_g],
        in_specs=[col(0), col(1), col(2), col(3),
                  pl.BlockSpec((2, HG_W), lambda c: (0, 0)), pl.BlockSpec((1, LANES), lambda c: (0, 0))],
        out_specs=[pl.BlockSpec((rows_step, HG_W), lambda c: (c, 0)),
                   pl.BlockSpec((HG_W, rows_step), lambda c: (0, c)),
                   pl.BlockSpec((rows_step, HG_W), lambda c: (c, 0)),
                   pl.BlockSpec((cps, 4, LANES, LANES), lambda c: (c, 0, 0, 0))],
        out_shape=[jax.ShapeDtypeStruct((t, HG_W), BF16), jax.ShapeDtypeStruct((HG_W, t), BF16),
                   jax.ShapeDtypeStruct((t, HG_W), F32), jax.ShapeDtypeStruct((nc, 4, LANES, LANES), F32)],
        scratch_shapes=[pltpu.VMEM((4, LANES, LANES), F32), pltpu.VMEM((cps, HGRN_CHUNK, HG_W), F32)],
        sem=("arbitrary",), comm=comm)


def _hgrn_bwd(uh, o_pre, d_r, states, lb_raw, norm_g, name, comm=None):
    t = uh.shape[0]
    nc = t // HGRN_CHUNK
    cps = _pick(nc, (HGRN_CHUNKS_PER_STEP, 2, 1))
    ns = nc // cps
    rows_step = cps * HGRN_CHUNK
    nsub = HGRN_CHUNK // HGRN_SUB

    def body(hq_ref, hf_ref, hi_ref, hg_ref, o_ref, dr_ref, st_in_ref, lb_ref, ng_ref,
             duh_ref, duh_t_ref, dbias_ref, dng_ref, dlb_ref, dst_ref, b_scr, dlb_acc):
        i = pl.program_id(0)

        @pl.when(i == 0)
        def _():
            dst_ref[...] = jnp.zeros_like(dst_ref)
            dbias_ref[...] = jnp.zeros_like(dbias_ref)
            dng_ref[...] = jnp.zeros_like(dng_ref)
            dlb_acc[...] = jnp.zeros_like(dlb_acc)

        masks_st = _sub_masks()
        masks_ts = _sub_masks_ts()
        ng = ng_ref[...]
        zpad = jnp.zeros((HGRN_CHUNK, LANES), F32)
        _, upper = _tri_mats()
        heads = [slice(h * LANES, (h + 1) * LANES) for h in range(4)]
        row = lax.broadcasted_iota(jnp.int32, (HGRN_CHUNK, HG_W), 0)

        chunks = [_hgrn_chunk_inputs(j, hq_ref, hf_ref, hi_ref, hg_ref, lb_ref, b_scr) for j in range(cps)]
        dng = jnp.zeros((1, LANES), F32)
        for ch in chunks:
            o = o_ref[ch["rows"], :]
            dr = dr_ref[ch["rows"], :].astype(F32)
            hg = ch["hg"]
            sgg = _sig(hg)
            dy = dr * (hg * sgg)
            do_h, y_h = [], []
            for sl in heads:
                oh = o[:, sl]
                rs = lax.rsqrt(jnp.mean(oh * oh, axis=1, keepdims=True) + RMS_EPS)
                y_h.append(oh * rs * ng)
                dng = dng + jnp.sum(dy[:, sl] * oh * rs, axis=0, keepdims=True)
                w = dy[:, sl] * ng
                do_h.append(rs * (w - oh * (rs * rs) * jnp.mean(w * oh, axis=1, keepdims=True)))
            ch["do"] = do_h
            ch["dhg"] = dr * jnp.concatenate(y_h, axis=1) * _dsilu(hg, sgg)

        for ch in chunks:
            ch["kst"], ch["kpad"], ch["qt_pad"], ch["v_b"], ch["do_pad"] = [], [], [], [], []
            ch["ats"], ch["d_at"], ch["d_a"] = [], [], []
            for h, sl in enumerate(heads):
                kst = jnp.concatenate([kh[:, sl] for kh in ch["khat"]], axis=0)
                kpad = jnp.concatenate([x for kh in ch["khat"] for x in (kh[:, sl], zpad)], axis=0)
                qt_pad = jnp.concatenate([ch["qt"][:, sl], zpad], axis=0)
                vh = ch["v"][:, sl].astype(BF16)
                v_pad = jnp.concatenate([vh, jnp.zeros_like(vh)], axis=0)
                do_b = ch["do"][h].astype(BF16)
                do_pad = jnp.concatenate([do_b, jnp.zeros_like(do_b)], axis=0)
                ch["kst"].append(kst)
                ch["kpad"].append(kpad)
                ch["qt_pad"].append(qt_pad)
                ch["v_b"].append(vh)
                ch["do_pad"].append(do_pad)
                ch["ats"].append(_dot3(kst, qt_pad, _NT))
                ch["d_at"].append(lax.dot_general(vh, do_pad, _NT, preferred_element_type=F32))
                ch["d_a"].append(lax.dot_general(do_b, v_pad, _NT, preferred_element_type=F32))

        for ch in chunks:
            ch["d_kst"], ch["d_qt"], ch["dv"] = [], [], []
            for h in range(4):
                at = _masked_sum(ch["ats"][h], masks_st, 0)
                d_ats = jnp.concatenate([jnp.where(m, ch["d_at"][h], 0.0) for m in masks_st], axis=0)
                d_a_cat = jnp.concatenate([jnp.where(m, ch["d_a"][h], 0.0) for m in masks_ts], axis=1)
                ch["d_kst"].append(_dot3(d_ats, ch["qt_pad"][h], _NN))
                ch["d_qt"].append(_dot3(d_a_cat, ch["kpad"][h], _NN))
                ch["dv"].append(lax.dot_general(at.astype(BF16), ch["do_pad"][h], _NN, preferred_element_type=F32))

        for j in reversed(range(cps)):
            ch = chunks[j]
            q, k, sg, f, lb = ch["q"], ch["k"], ch["sg"], ch["f"], ch["lb"]
            dq_h, dk_h, dv_h, extra_h = [], [], [], []
            for h, sl in enumerate(heads):
                st_prev = st_in_ref[j, h]
                d_st = dst_ref[h]
                d_st_b = d_st.astype(BF16)
                do_b = ch["do_pad"][h][0:HGRN_CHUNK, :]
                kd, e_last = ch["kd"][:, sl], ch["e_last"][:, sl]
                dv = ch["dv"][h] + _dot(kd, d_st_b, _NT)
                d_qb = _dot(do_b, st_prev, _NN)
                d_kd = lax.dot_general(ch["v_b"][h], d_st_b, _NN, preferred_element_type=F32)
                extra_h.append(jnp.sum(st_prev * d_st, axis=0, keepdims=True) * e_last
                               + jnp.sum(kd * d_kd, axis=0, keepdims=True))
                dst_ref[h] = d_st * e_last + _dot(do_b, ch["qb"][:, sl], _TN)
                dq_h.append(ch["d_qt"][h] * ch["e_q"][:, sl] + d_qb * ch["e_b"][:, sl])
                dkk = d_kd * ch["e_bl"][:, sl]
                for s_ in range(nsub):
                    dkk = dkk + ch["d_kst"][h][HGRN_CHUNK * s_:HGRN_CHUNK * (s_ + 1), :] * ch["e_k"][s_][:, sl]
                dk_h.append(dkk)
                dv_h.append(dv)
            dq = jnp.concatenate(dq_h, axis=1)
            dk = jnp.concatenate(dk_h, axis=1)
            dv = jnp.concatenate(dv_h, axis=1)
            extra = jnp.concatenate(extra_h, axis=1)
            db = q * dq - k * dk + jnp.where(row == HGRN_CHUNK - 1, extra, 0.0)
            dg = _tri_apply(upper, db)
            df = dg / f - dk
            dhf = df * (1.0 - lb) * sg * (1.0 - sg)
            dhq = dq * _dsilu(ch["hq"], _sig(ch["hq"]))
            full = jnp.concatenate([dhq, dhf, dv, ch["dhg"]], axis=1)
            duh_ref[ch["rows"], :] = full.astype(duh_ref.dtype)
            ch["full"] = full
            dbias_ref[...] += jnp.sum(full, axis=0, keepdims=True)
            dlb_acc[...] += jnp.sum(df * (1.0 - sg), axis=0, keepdims=True)
        dng_ref[...] += dng
        _store_transposed(duh_t_ref, [ch["full"] for ch in chunks])

        @pl.when(i == ns - 1)
        def _():
            lb = chunks[0]["lb"]
            d_a0 = dlb_acc[...] * lb * (1.0 - lb)
            r8 = lax.broadcasted_iota(jnp.int32, (8, HG_W), 0)
            dlb_ref[...] = jnp.where(r8 == 0, d_a0, jnp.where(r8 == 1, -d_a0, 0.0))

    col = lambda j: pl.BlockSpec((rows_step, HG_W), lambda i: (ns - 1 - i, j))
    return _call(
        body, name=name, grid=(ns,), ins=[uh, uh, uh, uh, o_pre, d_r, states, lb_raw, norm_g],
        in_specs=[col(0), col(1), col(2), col(3), col(0), col(0),
                  pl.BlockSpec((cps, 4, LANES, LANES), lambda i: (ns - 1 - i, 0, 0, 0)),
                  pl.BlockSpec((2, HG_W), lambda i: (0, 0)), pl.BlockSpec((1, LANES), lambda i: (0, 0))],
        out_specs=[pl.BlockSpec((rows_step, UH_W), lambda i: (ns - 1 - i, 0)),
                   pl.BlockSpec((UH_W, rows_step), lambda i: (0, ns - 1 - i)),
                   pl.BlockSpec((1, UH_W), lambda i: (0, 0)),
                   pl.BlockSpec((1, LANES), lambda i: (0, 0)),
                   pl.BlockSpec((8, HG_W), lambda i: (0, 0))],
        out_shape=[jax.ShapeDtypeStruct((t, UH_W), BF16), jax.ShapeDtypeStruct((UH_W, t), BF16),
                   jax.ShapeDtypeStruct((1, UH_W), F32),
                   jax.ShapeDtypeStruct((1, LANES), F32), jax.ShapeDtypeStruct((8, HG_W), F32)],
        scratch_shapes=[pltpu.VMEM((4, LANES, LANES), F32), pltpu.VMEM((cps, HGRN_CHUNK, HG_W), F32),
                        pltpu.VMEM((1, HG_W), F32)],
        sem=("arbitrary",), comm=comm)


def _ln_bwd_math(dy, xhat, rstd, g):
    dxh = dy * g
    return rstd * (dxh - jnp.mean(dxh, axis=1, keepdims=True)
                   - xhat * jnp.mean(dxh * xhat, axis=1, keepdims=True))


def _mm_rows(a, b, extras, *, name, epilogue, out_shape, out_specs, tb=False, tm=512, tk=1408):
    m, kdim = a.shape
    n = b.shape[0] if tb else b.shape[1]
    tm = _pick(m, (tm, 256, 128))
    tk = _pick(kdim, (tk, 1408, 1024, 768, 512, 256, 128))
    nk = kdim // tk
    b_spec = pl.BlockSpec((n, tk), lambda i, k: (0, k)) if tb else pl.BlockSpec((tk, n), lambda i, k: (k, 0))
    dims = _NT if tb else _NN
    n_ex, n_out = len(extras), len(out_shape)

    def body(*refs):
        a_ref, b_ref = refs[0], refs[1]
        ex_refs = refs[2:2 + n_ex]
        o_refs = refs[2 + n_ex:2 + n_ex + n_out]
        acc_ref = refs[-1]
        i, k = pl.program_id(0), pl.program_id(1)

        @pl.when(k == 0)
        def _():
            acc_ref[...] = jnp.zeros_like(acc_ref)

        acc_ref[...] += _dot(a_ref[...], b_ref[...], dims)

        @pl.when(k == nk - 1)
        def _():
            epilogue(acc_ref[...], ex_refs, o_refs, i == 0)

    return pl.pallas_call(
        body, name=name, grid=(m // tm, nk),
        in_specs=[pl.BlockSpec((tm, tk), lambda i, k: (i, k)), b_spec] + [sp for _, sp in extras],
        out_specs=list(out_specs), out_shape=list(out_shape),
        scratch_shapes=[pltpu.VMEM((tm, n), F32)],
        compiler_params=_cp("arbitrary", "arbitrary"),
    )(a, b, *[arr for arr, _ in extras])


def _rows_specs(tm, d):
    row = pl.BlockSpec((tm, d), lambda i, k: (i, 0))
    vec = pl.BlockSpec((1, d), lambda i, k: (0, 0))
    col = pl.BlockSpec((tm, 1), lambda i, k: (i, 0))
    return row, vec, col


def _mm_ln_fwd(a, b, addend, g, beta, name, tm=512):
    t, d = addend.shape
    tm = _pick(t, (tm, 256, 128))
    row, vec, col = _rows_specs(tm, d)

    def epilogue(acc, ex, outs, first):
        z = acc + ex[0][...]
        mu = jnp.mean(z, axis=1, keepdims=True)
        zc = z - mu
        rstd = lax.rsqrt(jnp.mean(zc * zc, axis=1, keepdims=True) + LN_EPS)
        xhat = zc * rstd
        h = xhat * ex[1][...] + ex[2][...]
        outs[0][...] = h
        outs[1][...] = h.astype(BF16)
        outs[2][...] = xhat
        outs[3][...] = rstd

    return _mm_rows(a, b, [(addend, row), (g, vec), (beta, vec)], name=name, epilogue=epilogue, tm=tm,
                    out_shape=[jax.ShapeDtypeStruct((t, d), F32), jax.ShapeDtypeStruct((t, d), BF16),
                               jax.ShapeDtypeStruct((t, d), F32), jax.ShapeDtypeStruct((t, 1), F32)],
                    out_specs=[row, row, row, col])


CONV_RB = 32
HALO = 8


def _sum8(x):
    acc = x[0:8]
    for r in range(8, x.shape[0], 8):
        acc = acc + x[r:r + 8]
    return acc


FFN_TILE = 256
FFN_COLS = 256


def _rows_before(win, k):
    return pltpu.roll(win, k, 0)[HALO:]


def _rows_after(win, k):
    n = win.shape[0]
    return pltpu.roll(win, n - k, 0)[0:n - HALO]


def _resident(shape):
    return pl.BlockSpec(shape, lambda i: (0,) * len(shape), pipeline_mode=pl.Buffered(1))


def _ffn_fwd(h1b, h1, w_up_t, conv_w, conv_b, w_down, target, ln2_g, ln2_b, name, comm=None):
    t, d = h1.shape
    tr = _pick(t, (FFN_TILE, 128))
    nblk = D_FF // FFN_COLS
    rb = CONV_RB

    def body(a_ref, wup_ref, cw_ref, cb_ref, wd_ref, h1_ref, tgt_ref, g_ref, b_ref,
             u2_ref, hm_ref, dz_ref, dg_ref, db_ref, loss_ref, ext):
        i = pl.program_id(0)

        @pl.when(i == 0)
        def _():
            ext[0:HALO, :] = jnp.zeros((HALO, D_FF), F32)
            dg_ref[...] = jnp.zeros_like(dg_ref)
            db_ref[...] = jnp.zeros_like(db_ref)
            loss_ref[...] = jnp.zeros_like(loss_ref)

        a = a_ref[...]
        for c in range(nblk):
            cs = slice(c * FFN_COLS, (c + 1) * FFN_COLS)
            vs = slice(D_FF + c * FFN_COLS, D_FF + (c + 1) * FFN_COLS)
            gate_pre = lax.dot_general(a, wup_ref[cs, :], _NT, preferred_element_type=F32)
            u2_ref[:, cs] = gate_pre
            ext[HALO:, cs] = gate_pre
            u2_ref[:, vs] = lax.dot_general(a, wup_ref[vs, :], _NT, preferred_element_type=F32)
        acc = jnp.zeros((tr, d), F32)
        for c in range(nblk):
            cs = slice(c * FFN_COLS, (c + 1) * FFN_COLS)
            for sub in range(FFN_COLS // LANES):
                ln = slice(c * FFN_COLS + sub * LANES, c * FFN_COLS + (sub + 1) * LANES)
                vl = slice(D_FF + c * FFN_COLS + sub * LANES, D_FF + c * FFN_COLS + (sub + 1) * LANES)
                w0, w1, w2, bb = cw_ref[0:1, ln], cw_ref[1:2, ln], cw_ref[2:3, ln], cb_ref[:, ln]
                for r0 in range(0, tr, rb):
                    win = ext[r0:r0 + HALO + rb, ln]
                    gate = _rows_before(win, 2) * w0 + _rows_before(win, 1) * w1 + win[HALO:] * w2 + bb
                    hm_ref[r0:r0 + rb, ln] = (gate * _sig(gate) * u2_ref[r0:r0 + rb, vl]).astype(hm_ref.dtype)
            acc = acc + lax.dot_general(hm_ref[:, cs], wd_ref[cs, :], _NN, preferred_element_type=F32)
        ext[0:HALO, :] = ext[tr:tr + HALO, :]

        z = acc + ALPHA * h1_ref[...]
        gg = g_ref[...]
        mu = jnp.mean(z, axis=1, keepdims=True)
        zc = z - mu
        rstd = lax.rsqrt(jnp.mean(zc * zc, axis=1, keepdims=True) + LN_EPS)
        xhat = zc * rstd
        err = xhat * gg + b_ref[...] - tgt_ref[...]
        loss_ref[...] += 0.5 * jnp.sum(jnp.mean(err * err, axis=1, keepdims=True))
        dy = err * (1.0 / d)
        dz_ref[...] = _ln_bwd_math(dy, xhat, rstd, gg)
        dg_ref[...] += jnp.sum(dy * xhat, axis=0, keepdims=True)
        db_ref[...] += jnp.sum(dy, axis=0, keepdims=True)

    row = lambda w: pl.BlockSpec((tr, w), lambda i: (i, 0))
    vec = pl.BlockSpec((1, d), lambda i: (0, 0))
    return _call(
        body, name=name, grid=(t // tr,),
        ins=[h1b, w_up_t, conv_w, conv_b, w_down, h1, target, ln2_g, ln2_b],
        in_specs=[row(d), _resident((2 * D_FF, d)), _resident((3, D_FF)), _resident((1, D_FF)),
                  _resident((D_FF, d)), row(d), row(d), vec, vec],
        out_specs=[row(2 * D_FF), row(D_FF), row(d), vec, vec, pl.BlockSpec((1, LANES), lambda i: (0, 0))],
        out_shape=[jax.ShapeDtypeStruct((t, 2 * D_FF), F32), jax.ShapeDtypeStruct((t, D_FF), BF16),
                   jax.ShapeDtypeStruct((t, d), F32), jax.ShapeDtypeStruct((1, d), F32),
                   jax.ShapeDtypeStruct((1, d), F32), jax.ShapeDtypeStruct((1, LANES), F32)],
        scratch_shapes=[pltpu.VMEM((tr + HALO, D_FF), F32)],
        sem=("arbitrary",), comm=comm)


def _ffn_bwd(dz2, u2, w_down, w_up_t, conv_w, conv_b, xhat1, rstd1, ln1_g, name, comm=None):
    t, d = dz2.shape
    tr = _pick(t, (FFN_TILE, 128))
    nt = t // tr
    hb = tr // HALO
    nblk = D_FF // FFN_COLS
    rb = CONV_RB

    def body(dz2_ref, dz2_next_ref, u2_ref, gp_prev_ref, wd_ref, wup_ref, cw_ref, cb_ref, xhat_ref, rstd_ref,
             g1_ref, du_ref, dz1_ref, dw_ref, dcb_ref, dg1_ref, db1_ref, head, dh_s, dg_s):
        i = pl.program_id(0)

        @pl.when(i == 0)
        def _():
            dg_s[tr:, :] = jnp.zeros((HALO, D_FF), F32)
            dw_ref[...] = jnp.zeros_like(dw_ref)
            dcb_ref[...] = jnp.zeros_like(dcb_ref)
            dg1_ref[...] = jnp.zeros_like(dg1_ref)
            db1_ref[...] = jnp.zeros_like(db1_ref)

        dz2 = dz2_ref[...]

        @pl.when(i == 0)
        def _():
            dz2_b = dz2.astype(BF16)
            for c in range(nblk):
                cs = slice(c * FFN_COLS, (c + 1) * FFN_COLS)
                dh_s[:, cs] = lax.dot_general(dz2_b, wd_ref[cs, :], _NT, preferred_element_type=F32)

        dz2_next = dz2_next_ref[...].astype(BF16)
        dh_next = [lax.dot_general(dz2_next, wd_ref[c * FFN_COLS:(c + 1) * FFN_COLS, :], _NT,
                                   preferred_element_type=F32) for c in range(nblk)]
        head[0:HALO, :] = jnp.where(i == nt - 1, 0.0, gp_prev_ref[...])
        head[HALO:, :] = u2_ref[0:rb, 0:D_FF]

        acc = jnp.zeros((tr, d), F32)
        for blk in range(nblk):
            for c in range(blk * FFN_COLS // LANES, (blk + 1) * FFN_COLS // LANES):
                ln = slice(c * LANES, (c + 1) * LANES)
                vl = slice(D_FF + c * LANES, D_FF + (c + 1) * LANES)
                w0, w1, w2, bb = cw_ref[0:1, ln], cw_ref[1:2, ln], cw_ref[2:3, ln], cb_ref[:, ln]
                acc_b = jnp.zeros((8, LANES), F32)
                acc_w = [jnp.zeros((8, LANES), F32) for _ in range(3)]
                for r0 in range(0, tr, rb):
                    win = head[:, ln] if r0 == 0 else u2_ref[r0 - HALO:r0 + rb, ln]
                    g_m2, g_m1, g_0 = _rows_before(win, 2), _rows_before(win, 1), win[HALO:]
                    gate = g_m2 * w0 + g_m1 * w1 + g_0 * w2 + bb
                    sg = _sig(gate)
                    dh = dh_s[r0:r0 + rb, ln]
                    dgate = dh * u2_ref[r0:r0 + rb, vl] * _dsilu(gate, sg)
                    dg_s[r0:r0 + rb, ln] = dgate
                    du_ref[r0:r0 + rb, vl] = (dh * (gate * sg)).astype(du_ref.dtype)
                    acc_b = acc_b + _sum8(dgate)
                    acc_w[0] = acc_w[0] + _sum8(dgate * g_m2)
                    acc_w[1] = acc_w[1] + _sum8(dgate * g_m1)
                    acc_w[2] = acc_w[2] + _sum8(dgate * g_0)
                dcb_ref[:, ln] += jnp.sum(acc_b, axis=0, keepdims=True)
                for j in range(3):
                    dw_ref[j:j + 1, ln] += jnp.sum(acc_w[j], axis=0, keepdims=True)
                for r0 in range(0, tr, rb):
                    win = dg_s[r0:r0 + rb + HALO, ln]
                    d_gp = _rows_after(win, 2) * w0 + _rows_after(win, 1) * w1 + win[0:rb] * w2
                    du_ref[r0:r0 + rb, ln] = d_gp.astype(du_ref.dtype)
            cs = slice(blk * FFN_COLS, (blk + 1) * FFN_COLS)
            vs = slice(D_FF + blk * FFN_COLS, D_FF + (blk + 1) * FFN_COLS)
            acc = acc + lax.dot_general(du_ref[:, cs], wup_ref[cs, :], _NN, preferred_element_type=F32)
            acc = acc + lax.dot_general(du_ref[:, vs], wup_ref[vs, :], _NN, preferred_element_type=F32)
        dg_s[tr:, :] = dg_s[0:HALO, :]
        for c in range(nblk):
            dh_s[:, c * FFN_COLS:(c + 1) * FFN_COLS] = dh_next[c]
        dy = acc + ALPHA * dz2
        xh = xhat_ref[...]
        dz1_ref[...] = _ln_bwd_math(dy, xh, rstd_ref[...], g1_ref[...])
        dg1_ref[...] += jnp.sum(dy * xh, axis=0, keepdims=True)
        db1_ref[...] += jnp.sum(dy, axis=0, keepdims=True)

    rev = lambda w: pl.BlockSpec((tr, w), lambda i: (nt - 1 - i, 0))
    vec = pl.BlockSpec((1, d), lambda i: (0, 0))
    return _call(
        body, name=name, grid=(nt,),
        ins=[dz2, dz2, u2, u2, w_down, w_up_t, conv_w, conv_b, xhat1, rstd1, ln1_g],
        in_specs=[rev(d), pl.BlockSpec((tr, d), lambda i: (jnp.maximum(nt - 2 - i, 0), 0)), rev(2 * D_FF),
                  pl.BlockSpec((HALO, D_FF), lambda i: (jnp.maximum((nt - 1 - i) * hb - 1, 0), 0)),
                  _resident((D_FF, d)), _resident((2 * D_FF, d)), _resident((3, D_FF)), _resident((1, D_FF)),
                  rev(d), pl.BlockSpec((tr, 1), lambda i: (nt - 1 - i, 0)), vec],
        out_specs=[rev(2 * D_FF), rev(d), pl.BlockSpec((8, D_FF), lambda i: (0, 0)),
                   pl.BlockSpec((1, D_FF), lambda i: (0, 0)), vec, vec],
        out_shape=[jax.ShapeDtypeStruct((t, 2 * D_FF), BF16), jax.ShapeDtypeStruct((t, d), F32),
                   jax.ShapeDtypeStruct((8, D_FF), F32), jax.ShapeDtypeStruct((1, D_FF), F32),
                   jax.ShapeDtypeStruct((1, d), F32), jax.ShapeDtypeStruct((1, d), F32)],
        scratch_shapes=[pltpu.VMEM((HALO + rb, D_FF), F32), pltpu.VMEM((tr, D_FF), F32),
                        pltpu.VMEM((tr + HALO, D_FF), F32)],
        sem=("arbitrary",), comm=comm)


def _adamw(w, g, m, v, name):
    rows, cols = w.shape
    tr = _pick(rows, (256, 128, 64, 32, 16, 8))

    def body(w_ref, g_ref, m_ref, v_ref, d_ref, nm_ref, nv_ref):
        d_ref[...], nm_ref[...], nv_ref[...] = _adamw_math(w_ref[...], g_ref[...], m_ref[...], v_ref[...])

    spec = pl.BlockSpec((tr, cols), lambda i: (i, 0))
    shp = jax.ShapeDtypeStruct((rows, cols), F32)
    return pl.pallas_call(
        body, name=name, grid=(rows // tr,),
        in_specs=[spec, spec, spec, spec], out_specs=[spec, spec, spec], out_shape=[shp, shp, shp],
        compiler_params=_cp("parallel"),
    )(w, g, m, v)


def _pad_rows(a, rows):
    return jnp.pad(a, ((0, rows - a.shape[0]), (0, 0)))


SMALL_LAYOUT = (("ln1_g", 1024), ("ln1_b", 1024), ("b_in", 2816), ("sinks", 8), ("hgrn_lb", 1024),
                ("hgrn_norm_g", 128), ("ln2_g", 1024), ("ln2_b", 1024), ("conv_b", 2816), ("loss", 1))
SMALL_SHAPES = {"ln1_g": (1, 1024), "ln1_b": (1, 1024), "b_in": (1, 2816), "sinks": (1, 8), "hgrn_lb": (2, 512),
                "hgrn_norm_g": (1, 128), "ln2_g": (1, 1024), "ln2_b": (1, 1024), "conv_b": (1, 2816),
                "loss": (1,)}


def _pack_small(parts):
    rows = []
    for name, size in SMALL_LAYOUT:
        flat = parts[name].reshape(-1).astype(F32)
        padded = -(-size // LANES) * LANES
        rows.append(jnp.pad(flat, (0, padded - size)).reshape(-1, LANES))
    return _pad_rows(jnp.concatenate(rows, axis=0), SMALL_ROWS)


def _unpack_small(pack):
    out, r = {}, 0
    for name, size in SMALL_LAYOUT:
        nrows = -(-size // LANES)
        out[name] = pack[r:r + nrows].reshape(-1)[:size].reshape(SMALL_SHAPES[name])
        r += nrows
    return out


def _own(full, rows):
    return lax.dynamic_slice_in_dim(full, _me() * rows, rows, axis=0)


def kernel(x, positions, ln1_g, ln1_b, w_in, b_in, sinks, hgrn_lb, hgrn_norm_g, w_o, ln2_g, ln2_b, w_up, conv_w, conv_b, w_down, loss_target, m_ln1_g, m_ln1_b, m_w_in, m_b_in, m_sinks, m_hgrn_lb, m_hgrn_norm_g, m_w_o, m_ln2_g, m_ln2_b, m_w_up, m_conv_w, m_conv_b, m_w_down, v_ln1_g, v_ln1_b, v_w_in, v_b_in, v_sinks, v_hgrn_lb, v_hgrn_norm_g, v_w_o, v_ln2_g, v_ln2_b, v_w_up, v_conv_w, v_conv_b, v_w_down):
    t = x.shape[1]
    x2 = x[0]
    target = loss_target[0]
    pos_col = positions.reshape(t, 1)

    w_in_t_s = w_in[0].T.astype(BF16)
    w_up_t_s = w_up[0].T.astype(BF16)
    w_o_s = w_o[0].astype(BF16)
    w_down_s = w_down[0].astype(BF16)
    (ctab, stab, xb), (w_in_t_g, cw_g) = _prep(
        pos_col, x2, "prep_ag_w_in", _Comm([{"kind": "gather", "arr": w_in_t_s}, {"kind": "gather", "arr": _pad_rows(conv_w[0], 8)}]))
    w_in_t = w_in_t_g.reshape(D_FF, D_MODEL)
    w_a_t, w_h_t = w_in_t[:UA_W], w_in_t[UA_W:]
    conv_w_f = cw_g[:, 0:3].transpose(1, 0, 2).reshape(3, D_FF)

    ua = _mm(xb, w_a_t, tb=True, bias=b_in[:, :UA_W], name="fwd_in_attn")
    uh = _mm(xb, w_h_t, tb=True, bias=b_in[:, UA_W:], name="fwd_in_hgrn")
    half_up = SHARD_UP // 2
    (a_out, a_out_t), (w_o_g, w_up_half) = _attn_fwd(
        ua, ctab, stab, sinks, "attn_fwd",
        comm=_Comm([{"kind": "gather", "arr": w_o_s},
                    {"kind": "gather", "arr": w_up_t_s, "rows": (0, half_up), "dst_rows": SHARD_UP}]))
    (r_out, r_out_t, o_pre, states), (w_up_t_g, w_down_g) = _hgrn_fwd(
        uh, hgrn_lb, hgrn_norm_g, "hgrn_fwd",
        comm=_Comm([{"kind": "gather", "arr": w_up_t_s, "rows": (half_up, half_up), "dst_rows": SHARD_UP,
                     "dst_first": half_up, "into": w_up_half},
                    {"kind": "gather", "arr": w_down_s}]))
    w_down_f = w_down_g.reshape(D_FF, D_MODEL)
    w_o_f = w_o_g.reshape(D_MODEL, D_MODEL)
    w_up_t = w_up_t_g.reshape(2 * D_FF, D_MODEL)
    z1 = _mm(a_out, w_o_f[:ATTN_W], addend=x2, addend_scale=ALPHA, name="fwd_o_attn")
    h1, h1b, xhat1, rstd1 = _mm_ln_fwd(r_out, w_o_f[ATTN_W:], z1, ln1_g, ln1_b, "fwd_o_hgrn_ln1")
    u2, hmid, dz2, d_ln2_g, d_ln2_b, loss_part = _ffn_fwd(h1b, h1, w_up_t, conv_w_f, conv_b, w_down_f, target,
                                                         ln2_g, ln2_b, "ffn_fwd")[0]

    d_w_down, d_w_down_b = _mm(hmid, dz2, ta=True, out_dtype2=BF16, tm=1408, tk=512, name="bwd_down_dw")
    (d_u2, dz1, d_conv_w8, d_conv_b, d_ln1_g, d_ln1_b), (recv_down,) = _ffn_bwd(
        dz2, u2, w_down_f, w_up_t, conv_w_f, conv_b, xhat1, rstd1, ln1_g, "ffn_bwd",
        comm=_Comm([{"kind": "exchange", "arr": d_w_down_b.reshape(N_DEV, SHARD_DOWN, D_MODEL)}]))
    d_w_up_t, d_w_up_t_b = _mm(d_u2, h1b, ta=True, out_dtype2=BF16, tm=1408, tk=512, name="bwd_up_dw")
    d_a = _mm(dz1, w_o_f[:ATTN_W], tb=True, name="bwd_o_dx_attn")
    d_r = _mm(dz1, w_o_f[ATTN_W:], tb=True, name="bwd_o_dx_hgrn")
    d_w_o_a, d_w_o_a_b = _mm(a_out_t, dz1, out_dtype2=BF16, name="bwd_o_dw_attn")
    d_w_o_r, d_w_o_r_b = _mm(r_out_t, dz1, out_dtype2=BF16, name="bwd_o_dw_hgrn")
    d_w_o = jnp.concatenate([d_w_o_a, d_w_o_r], axis=0)
    d_w_o_b = jnp.concatenate([d_w_o_a_b, d_w_o_r_b], axis=0)
    d_w_up_x = d_w_up_t_b.reshape(N_DEV, SHARD_UP, D_MODEL)
    half = SHARD_UP // 2
    d_cw_x = d_conv_w8.reshape(8, N_DEV, SHARD_IN).transpose(1, 0, 2)
    (d_ua, d_ua_t, d_bias_a, d_sinks), (recv_up_half, recv_cw) = _attn_bwd(
        ua, d_a, ctab, stab, sinks, "attn_bwd",
        comm=_Comm([{"kind": "exchange", "arr": d_w_up_x, "rows": (0, half), "dst_rows": SHARD_UP},
                    {"kind": "exchange", "arr": d_cw_x}]))
    (d_uh, d_uh_t, d_bias_h, d_norm_g, d_lb8), (recv_up, recv_o) = _hgrn_bwd(
        uh, o_pre, d_r, states, hgrn_lb, hgrn_norm_g, "hgrn_bwd",
        comm=_Comm([{"kind": "exchange", "arr": d_w_up_x, "rows": (half, half), "dst_rows": SHARD_UP,
                     "dst_first": half, "into": recv_up_half},
                    {"kind": "exchange", "arr": d_w_o_b.reshape(N_DEV, SHARD_O, D_MODEL)}]))
    d_w_a_t, d_w_a_t_b = _mm(d_ua_t, xb, out_dtype2=BF16, name="bwd_in_dw_attn")
    d_w_h_t, d_w_h_t_b = _mm(d_uh_t, xb, out_dtype2=BF16, name="bwd_in_dw_hgrn")
    d_w_in_t = jnp.concatenate([d_w_a_t, d_w_h_t], axis=0)
    d_w_in_t_b = jnp.concatenate([d_w_a_t_b, d_w_h_t_b], axis=0)
    small_local = _pack_small({
        "ln1_g": d_ln1_g, "ln1_b": d_ln1_b, "b_in": jnp.concatenate([d_bias_a, d_bias_h], axis=1),
        "sinks": d_sinks[:, :8], "hgrn_lb": d_lb8[0:2], "hgrn_norm_g": d_norm_g, "ln2_g": d_ln2_g,
        "ln2_b": d_ln2_b, "conv_b": d_conv_b, "loss": loss_part[:, :1]})
    d_w_in_x = d_w_in_t_b.reshape(N_DEV, SHARD_IN, D_MODEL)
    res_up, (from_sibling,) = _sum_shards_adamw(
        [recv_up], _own(d_w_up_t, SHARD_UP), w_up[0].T, m_w_up[0].T, v_w_up[0].T, "adamw_w_up",
        comm=_Comm([{"kind": "pair4", "arr": d_w_in_x}]))
    res_up = [r.T for r in res_up]
    own_in, chip_part = _pair_reduce(from_sibling, d_w_in_t, "pair_reduce_w_in")
    dx, (from_chips, small_g) = _mm(d_uh, w_h_t, addend=dz1, addend_scale=ALPHA, name="bwd_in_dx_hgrn",
                                    comm=_Comm([{"kind": "chips3", "arr": chip_part},
                                                {"kind": "gather", "arr": small_local}]))
    dx = _mm(d_ua, w_a_t, addend=dx, tk=768, name="bwd_in_dx_attn")

    res_in = [r.T for r in _chip_sum_adamw(from_chips, own_in, w_in[0].T, m_w_in[0].T, v_w_in[0].T, "adamw_w_in")]
    res_o = _sum_shards_adamw([recv_o], _own(d_w_o, SHARD_O), w_o[0], m_w_o[0], v_w_o[0], "adamw_w_o")
    res_down = _sum_shards_adamw([recv_down], _own(d_w_down, SHARD_DOWN), w_down[0], m_w_down[0], v_w_down[0],
                                 "adamw_w_down")
    g_cw = _sum_slots(recv_cw, "sum_conv_w")
    cw8 = lambda a: _pad_rows(a, 8)
    res_cw = (g_cw,) + tuple(_adamw(cw8(conv_w[0]), g_cw, cw8(m_conv_w[0]), cw8(v_conv_w[0]), "adamw_conv_w"))
    big = {"w_in": [r[None] for r in res_in], "w_up": [r[None] for r in res_up],
           "w_o": [r[None] for r in res_o], "w_down": [r[None] for r in res_down],
           "conv_w": [r[None, 0:3] for r in res_cw]}

    small_sum = _sum_slots(small_g, "ar_small_sum")
    gs = _unpack_small(small_sum)
    loss = gs["loss"][0]
    zero1 = jnp.zeros((1,), F32)
    w_small = _pack_small({"ln1_g": ln1_g, "ln1_b": ln1_b, "b_in": b_in, "sinks": sinks, "hgrn_lb": hgrn_lb,
                           "hgrn_norm_g": hgrn_norm_g, "ln2_g": ln2_g, "ln2_b": ln2_b, "conv_b": conv_b,
                           "loss": zero1})
    m_small = _pack_small({"ln1_g": m_ln1_g, "ln1_b": m_ln1_b, "b_in": m_b_in, "sinks": m_sinks,
                           "hgrn_lb": m_hgrn_lb, "hgrn_norm_g": m_hgrn_norm_g, "ln2_g": m_ln2_g,
                           "ln2_b": m_ln2_b, "conv_b": m_conv_b, "loss": zero1})
    v_small = _pack_small({"ln1_g": v_ln1_g, "ln1_b": v_ln1_b, "b_in": v_b_in, "sinks": v_sinks,
                           "hgrn_lb": v_hgrn_lb, "hgrn_norm_g": v_hgrn_norm_g, "ln2_g": v_ln2_g,
                           "ln2_b": v_ln2_b, "conv_b": v_conv_b, "loss": zero1})
    small = [gs] + [_unpack_small(p) for p in _adamw(w_small, small_sum, m_small, v_small, "adamw_small")]

    order = ["ln1_g", "ln1_b", "w_in", "b_in", "sinks", "hgrn_lb", "hgrn_norm_g", "w_o", "ln2_g", "ln2_b",
             "w_up", "conv_w", "conv_b", "w_down"]

    def pick(idx):
        return [big[n][idx] if n in big else small[idx][n] for n in order]

    return (loss, dx[None], *pick(0), *pick(1), *pick(2), *pick(3))
```

```python
import functools

import jax
import jax.numpy as jnp
import numpy as np
from jax import lax
from jax.experimental import pallas as pl
from jax.experimental.pallas import tpu as pltpu

F32 = jnp.float32
BF16 = jnp.bfloat16

N_DEV = 8
D_MODEL = 1024
D_FF = 2816
ATTN_W = 512
KV_W = 128
UA_W = ATTN_W + 2 * KV_W
UH_W = 2048
HG_W = 512
ATTN_BLOCK = 128
HGRN_CHUNK = 64
HGRN_SUB = 16
HGRN_CHUNKS_PER_STEP = 4
EXP_CLAMP = 85.0
NEG_BIG = -1e30
LN_EPS = 1e-5
RMS_EPS = 1e-6
ALPHA = 2.0 ** 0.25
ATTN_SCALE = 0.125
ROPE_THETA = 500000.0

ADAM_LR = 0.001
ADAM_B1 = 0.9
ADAM_B2 = 0.999
ADAM_EPS = 1e-08
ADAM_WD = 0.01
ADAM_STEP = 10

LANES = 128
VMEM_LIMIT_BYTES = 56 * 1024 * 1024

SHARD_IN = D_FF // N_DEV
SHARD_UP = 2 * D_FF // N_DEV
SHARD_O = D_MODEL // N_DEV
SHARD_DOWN = D_FF // N_DEV
SMALL_ROWS = 88

_MESH = pl.DeviceIdType.MESH
_NT = (((1,), (1,)), ((), ()))
_NN = (((1,), (0,)), ((), ()))
_TN = (((0,), (0,)), ((), ()))


def _cp(*sem):
    if sem:
        return pltpu.CompilerParams(dimension_semantics=sem, vmem_limit_bytes=VMEM_LIMIT_BYTES)
    return pltpu.CompilerParams(vmem_limit_bytes=VMEM_LIMIT_BYTES)


def _sig(x):
    return 0.5 * jnp.tanh(0.5 * x) + 0.5


def _dsilu(x, s):
    return s * (1.0 + x * (1.0 - s))


def _dot(a, b, dims):
    return lax.dot_general(a.astype(BF16), b.astype(BF16), dims, preferred_element_type=F32)


def _split(a):
    hi = a.astype(BF16)
    return hi, (a - hi.astype(F32)).astype(BF16)


def _dot3(a, b, dims):
    ah, al = _split(a)
    bh, bl = _split(b)
    d = functools.partial(lax.dot_general, dimension_numbers=dims, preferred_element_type=F32)
    return d(ah, bh) + (d(ah, bl) + d(al, bh))


def _pick(n, pref):
    for t in pref:
        if t <= n and n % t == 0:
            return t
    return n


def _my_coords():
    return lax.axis_index("x"), lax.axis_index("y"), lax.axis_index("c")


def _peer(k):
    x, y, c = _my_coords()
    return (1 - x if k & 4 else x, 1 - y if k & 2 else y, 1 - c if k & 1 else c)


def _me():
    x, y, c = _my_coords()
    return 4 * x + 2 * y + c


class _Comm:
    def __init__(self, items):
        self.items = []
        for it in items:
            arr = it["arr"]
            full = arr.shape[0] if it["kind"] == "gather" else arr.shape[1]
            first, count = it.get("rows", (0, full))
            self.items.append(dict(kind=it["kind"], arr=arr, first=first, count=count,
                                   dst_rows=it.get("dst_rows", count), dst_first=it.get("dst_first", 0),
                                   into=it.get("into")))
        self.n = len(self.items)
        self.arrays = [it["arr"] for it in self.items]
        self.intos = [(a, it["into"]) for a, it in enumerate(self.items) if it["into"] is not None]

    def out_shapes(self):
        return [jax.ShapeDtypeStruct((4 if it["kind"] in ("pair4", "chips3") else N_DEV, it["dst_rows"],
                                      it["arr"].shape[-1]), it["arr"].dtype) for it in self.items]

    def specs(self, n=None):
        return [pl.BlockSpec(memory_space=pl.ANY)] * (self.n if n is None else n)

    def scratch(self):
        return [pltpu.SemaphoreType.DMA(((N_DEV - 1) * self.n,)), pltpu.SemaphoreType.DMA(((N_DEV - 1) * self.n,)),
                pltpu.SemaphoreType.DMA((self.n,))]

    def _src(self, a, ref, dev):
        it = self.items[a]
        blk = ref if it["kind"] == "gather" else ref.at[dev]
        return blk.at[pl.ds(it["first"], it["count"])]

    def _dst(self, a, ref, slot):
        it = self.items[a]
        return ref.at[slot].at[pl.ds(it["dst_first"], it["count"])]

    def _copy(self, a, k, src, dst, sems, me, slot):
        other = jnp.bitwise_xor(me, k)
        idx = a * (N_DEV - 1) + k - 1
        return pltpu.make_async_remote_copy(
            src_ref=self._src(a, src, other), dst_ref=self._dst(a, dst, me if slot == "mine" else other),
            send_sem=sems[0].at[idx], recv_sem=sems[1].at[idx], device_id=_peer(k), device_id_type=_MESH)

    def _pass_on(self, a, k, dst, sems, me):
        slot = self._dst(a, dst, jnp.bitwise_xor(me, k))
        idx = a * (N_DEV - 1) + k
        return pltpu.make_async_remote_copy(
            src_ref=slot, dst_ref=slot, send_sem=sems[0].at[idx], recv_sem=sems[1].at[idx],
            device_id=_peer(1), device_id_type=_MESH)

    def _part(self, a, r, src, dst, sems, me):
        it = self.items[a]
        idx = a * (N_DEV - 1) + r
        if it["kind"] == "pair4":
            k, slot = 1, jnp.bitwise_xor(jnp.bitwise_xor(me, 1), 2 * r)
        else:
            k, slot = 2 * r, r
        return pltpu.make_async_remote_copy(
            src_ref=src.at[slot].at[pl.ds(it["first"], it["count"])], dst_ref=self._dst(a, dst, r),
            send_sem=sems[0].at[idx], recv_sem=sems[1].at[idx], device_id=_peer(k), device_id_type=_MESH)

    def _parts(self, a):
        return range(4) if self.items[a]["kind"] == "pair4" else range(1, 4)

    def _local(self, a, src, dst, sems, me):
        return pltpu.make_async_copy(self._src(a, src, me), self._dst(a, dst, me), sems[2].at[a])

    def start(self, srcs, dsts, sems):
        me = _me()
        for a, (src, dst) in enumerate(zip(srcs, dsts)):
            if self.items[a]["kind"] in ("pair4", "chips3"):
                for r in self._parts(a):
                    self._part(a, r, src, dst, sems, me).start()
                continue
            direct = (1, 2, 4, 6) if self.items[a]["kind"] == "gather" else range(1, N_DEV)
            self._local(a, src, dst, sems, me).start()
            for k in direct:
                self._copy(a, k, src, dst, sems, me, "mine").start()

    def wait(self, srcs, dsts, sems):
        me = _me()
        for a, (src, dst) in enumerate(zip(srcs, dsts)):
            if self.items[a]["kind"] in ("pair4", "chips3"):
                for r in self._parts(a):
                    self._part(a, r, src, dst, sems, me).wait_recv()
                for r in self._parts(a):
                    self._part(a, r, src, dst, sems, me).wait_send()
                continue
            if self.items[a]["kind"] == "gather":
                for k in (2, 4, 6):
                    self._copy(a, k, src, dst, sems, me, "theirs").wait_recv()
                    self._pass_on(a, k, dst, sems, me).start()
                for k in (1, 3, 5, 7):
                    self._copy(a, k, src, dst, sems, me, "theirs").wait_recv()
                for k in (1, 2, 4, 6):
                    self._copy(a, k, src, dst, sems, me, "mine").wait_send()
                for k in (2, 4, 6):
                    self._pass_on(a, k, dst, sems, me).wait_send()
            else:
                for k in range(1, N_DEV):
                    self._copy(a, k, src, dst, sems, me, "theirs").wait_recv()
                for k in range(1, N_DEV):
                    self._copy(a, k, src, dst, sems, me, "mine").wait_send()
            self._local(a, src, dst, sems, me).wait()


def _call(body, *, name, grid, ins, in_specs, out_specs, out_shape, scratch_shapes=(), sem, comm=None):
    n_in, n_out, n_scr = len(ins), len(out_shape), len(scratch_shapes)
    if comm is None:
        outs = pl.pallas_call(
            body, name=name, grid=grid, in_specs=list(in_specs), out_specs=list(out_specs),
            out_shape=list(out_shape), scratch_shapes=list(scratch_shapes), compiler_params=_cp(*sem))(*ins)
        return list(outs), []
    nc, n_into = comm.n, len(comm.intos)

    def hosted(*refs):
        pos = n_in
        c_in = refs[pos:pos + nc]
        pos += nc + n_into
        outs = refs[pos:pos + n_out]
        pos += n_out
        c_out = refs[pos:pos + nc]
        pos += nc
        scr = refs[pos:pos + n_scr]
        sems = refs[pos + n_scr:]
        ids = [pl.program_id(d) for d in range(len(grid))]
        first = functools.reduce(jnp.logical_and, [i == 0 for i in ids])
        last = functools.reduce(jnp.logical_and, [i == g - 1 for i, g in zip(ids, grid)])

        @pl.when(first)
        def _():
            comm.start(c_in, c_out, sems)

        body(*refs[:n_in], *outs, *scr)

        @pl.when(last)
        def _():
            comm.wait(c_in, c_out, sems)

    aliases = {n_in + nc + j: n_out + a for j, (a, _) in enumerate(comm.intos)}
    outs = pl.pallas_call(
        hosted, name=name, grid=grid, in_specs=list(in_specs) + comm.specs() + comm.specs(n_into),
        out_specs=list(out_specs) + comm.specs(), out_shape=list(out_shape) + comm.out_shapes(),
        scratch_shapes=list(scratch_shapes) + comm.scratch(), input_output_aliases=aliases,
        compiler_params=_cp(*(["arbitrary"] * len(grid))))(*ins, *comm.arrays, *[arr for _, arr in comm.intos])
    return list(outs[:n_out]), list(outs[n_out:])


def _sum_slots(gathered, name):
    _, rows, cols = gathered.shape

    def body(g_ref, out_ref):
        acc = g_ref[0]
        for s in range(1, N_DEV):
            acc = acc + g_ref[s]
        out_ref[...] = acc

    return pl.pallas_call(
        body, name=name,
        out_shape=jax.ShapeDtypeStruct((rows, cols), F32),
        compiler_params=_cp(),
    )(gathered)


def _slot_sum(recv_ref, own_ref, shape):
    me = _me()
    acc = jnp.zeros(shape, F32)
    for s in range(N_DEV):
        acc = acc + jnp.where(me == s, own_ref[...], recv_ref[s].astype(F32))
    return acc


def _adamw_math(w, g, m, v):
    nm = ADAM_B1 * m + (1.0 - ADAM_B1) * g
    nv = ADAM_B2 * v + (1.0 - ADAM_B2) * (g * g)
    m_hat = nm / (1.0 - ADAM_B1 ** ADAM_STEP)
    v_hat = nv / (1.0 - ADAM_B2 ** ADAM_STEP)
    return -ADAM_LR * (m_hat / (jnp.sqrt(v_hat) + ADAM_EPS) + ADAM_WD * w), nm, nv


def _pair_reduce(from_sibling, mine, name):
    _, rows, cols = from_sibling.shape
    tr = _pick(rows, (176, 128, 64, 32, 16, 8))
    tiles = rows // tr
    table = jnp.bitwise_xor(_me(), jnp.arange(0, N_DEV, 2, dtype=jnp.int32))

    def body(tbl_ref, sib_ref, mine_ref, own_ref, send_ref):
        r = pl.program_id(1)
        total = mine_ref[...] + sib_ref[0].astype(F32)
        send_ref[0] = jnp.where(r == 0, 0.0, total).astype(BF16)

        @pl.when(r == 0)
        def _():
            own_ref[...] = total

    grid_spec = pltpu.PrefetchScalarGridSpec(
        num_scalar_prefetch=1, grid=(tiles, 4),
        in_specs=[pl.BlockSpec((1, tr, cols), lambda i, r, tbl: (r, i, 0)),
                  pl.BlockSpec((tr, cols), lambda i, r, tbl: (tbl[r] * tiles + i, 0))],
        out_specs=[pl.BlockSpec((tr, cols), lambda i, r, tbl: (i, 0)),
                   pl.BlockSpec((1, tr, cols), lambda i, r, tbl: (r, i, 0))])
    return pl.pallas_call(
        body, name=name, grid_spec=grid_spec,
        out_shape=[jax.ShapeDtypeStruct((rows, cols), F32), jax.ShapeDtypeStruct((4, rows, cols), BF16)],
        compiler_params=_cp("arbitrary", "arbitrary"),
    )(table, from_sibling, mine)


def _chip_sum_adamw(from_chips, own, w, m, v, name):
    _, rows, cols = from_chips.shape
    tr = _pick(rows, (176, 128, 64, 32, 16, 8))

    def body(recv_ref, own_ref, w_ref, m_ref, v_ref, g_ref, d_ref, nm_ref, nv_ref):
        g = own_ref[...]
        for r in range(1, 4):
            g = g + recv_ref[r].astype(F32)
        g_ref[...] = g
        d_ref[...], nm_ref[...], nv_ref[...] = _adamw_math(w_ref[...], g, m_ref[...], v_ref[...])

    spec = pl.BlockSpec((tr, cols), lambda i: (i, 0))
    shp = jax.ShapeDtypeStruct((rows, cols), F32)
    return pl.pallas_call(
        body, name=name, grid=(rows // tr,),
        in_specs=[pl.BlockSpec((4, tr, cols), lambda i: (0, i, 0)), spec, spec, spec, spec],
        out_specs=[spec, spec, spec, spec], out_shape=[shp, shp, shp, shp],
        compiler_params=_cp("parallel"),
    )(from_chips, own, w, m, v)


def _sum_shards_adamw(recvs, own, w, m, v, name, comm=None):
    rows_p, cols = recvs[0].shape[1], recvs[0].shape[2]
    n_p = len(recvs)
    tr = _pick(rows_p, (176, 128, 64, 32, 16, 8))
    tiles = rows_p // tr

    def body(*refs):
        recv_refs = refs[:n_p]
        own_ref, w_ref, m_ref, v_ref, g_ref, d_ref, nm_ref, nv_ref = refs[n_p:]
        for j in range(n_p):
            @pl.when(pl.program_id(0) == j)
            def _():
                g = _slot_sum(recv_refs[j], own_ref, (tr, cols))
                g_ref[...] = g
                d_ref[...], nm_ref[...], nv_ref[...] = _adamw_math(w_ref[...], g, m_ref[...], v_ref[...])

    spec = pl.BlockSpec((tr, cols), lambda p_, i: (p_ * tiles + i, 0))
    shp = jax.ShapeDtypeStruct((rows_p * n_p, cols), F32)
    outs, couts = _call(
        body, name=name, grid=(n_p, tiles), ins=[*recvs, own, w, m, v],
        in_specs=[pl.BlockSpec((N_DEV, tr, cols), functools.partial(lambda p_, i, j: (0, jnp.where(p_ == j, i, 0), 0), j=j))
                  for j in range(n_p)] + [spec, spec, spec, spec],
        out_specs=[spec, spec, spec, spec], out_shape=[shp, shp, shp, shp],
        sem=("arbitrary", "arbitrary"), comm=comm)
    return outs if comm is None else (outs, couts)


def _mm(a, b, *, name, ta=False, tb=False, out_dtype=F32, out_dtype2=None, bias=None, addend=None,
        addend_scale=1.0, tm=1024, tn=1024, tk=1024, comm=None, out_rows=None, first_row=0, into=None,
        transpose_out=False):
    kdim, m = a.shape if ta else a.shape[::-1]
    n = b.shape[0] if tb else b.shape[1]
    tm = _pick(m, (tm, 1408, 1024, 768, 512, 256, 128))
    tn = _pick(n, (tn, 1408, 1024, 768, 512, 256, 128))
    tk = _pick(kdim, (tk, 1408, 1024, 768, 512, 256, 128))
    nk = kdim // tk
    a_spec = pl.BlockSpec((tk, tm), lambda i, j, k: (k, i)) if ta else pl.BlockSpec((tm, tk), lambda i, j, k: (i, k))
    b_spec = pl.BlockSpec((tn, tk), lambda i, j, k: (j, k)) if tb else pl.BlockSpec((tk, tn), lambda i, j, k: (k, j))
    ins, specs = [a, b], [a_spec, b_spec]
    if bias is not None:
        ins.append(bias)
        specs.append(pl.BlockSpec((1, tn), lambda i, j, k: (0, j)))
    if addend is not None:
        ins.append(addend)
        specs.append(pl.BlockSpec((tm, tn), lambda i, j, k: (i, j)))
    dims = (((0,) if ta else (1,), (1,) if tb else (0,)), ((), ()))
    has_bias, has_addend, two = bias is not None, addend is not None, out_dtype2 is not None

    def body(*refs):
        a_ref, b_ref = refs[0], refs[1]
        pos = 2
        bias_ref = addend_ref = None
        if has_bias:
            bias_ref = refs[pos]
            pos += 1
        if has_addend:
            addend_ref = refs[pos]
            pos += 1
        o_refs, acc_ref = refs[pos:-1], refs[-1]
        k = pl.program_id(2)

        @pl.when(k == 0)
        def _():
            acc_ref[...] = jnp.zeros_like(acc_ref)

        acc_ref[...] += _dot(a_ref[...], b_ref[...], dims)

        @pl.when(k == nk - 1)
        def _():
            r = acc_ref[...]
            if has_bias:
                r = r + bias_ref[...]
            if has_addend:
                r = r + addend_scale * addend_ref[...].astype(F32)
            if transpose_out:
                r = r.T
            for o_ref in o_refs:
                o_ref[...] = r.astype(o_ref.dtype)

    blk0 = first_row // tm
    dtypes = [out_dtype] + ([out_dtype2] if two else [])
    if transpose_out:
        ospec = pl.BlockSpec((tn, tm), lambda i, j, k: (j, i))
        shapes = [jax.ShapeDtypeStruct((n, m), d) for d in dtypes]
    else:
        ospec = pl.BlockSpec((tm, tn), lambda i, j, k: (i + blk0, j))
        shapes = [jax.ShapeDtypeStruct((m if out_rows is None else out_rows, n), d) for d in dtypes]
    if into is not None:
        n_in = len(ins)
        outs = pl.pallas_call(
            lambda *refs: body(*refs[:n_in], *refs[n_in + len(into):]), name=name, grid=(m // tm, n // tn, nk),
            in_specs=specs + [pl.BlockSpec(memory_space=pl.ANY)] * len(into), out_specs=[ospec] * len(dtypes),
            out_shape=shapes, scratch_shapes=[pltpu.VMEM((tm, tn), F32)],
            input_output_aliases={n_in + j: j for j in range(len(into))},
            compiler_params=_cp("parallel", "parallel", "arbitrary"))(*ins, *into)
        return tuple(outs) if two else outs[0]
    outs, couts = _call(
        body, name=name, grid=(m // tm, n // tn, nk), ins=ins, in_specs=specs,
        out_specs=[ospec] * len(dtypes), out_shape=shapes,
        scratch_shapes=[pltpu.VMEM((tm, tn), F32)], sem=("parallel", "parallel", "arbitrary"), comm=comm)
    primary = tuple(outs) if two else outs[0]
    return (primary, couts) if comm is not None else primary


def _rope_lane_constants():
    inv_freq = np.float32(ROPE_THETA) ** (-np.arange(8, dtype=np.float32) * np.float32(2.0 / 16.0))
    lane = np.arange(LANES) % 64
    freq = np.where(lane < 16, inv_freq[lane % 8], 0.0).astype(np.float32)
    sign = np.where(lane < 8, -1.0, np.where(lane < 16, 1.0, 0.0)).astype(np.float32)
    return jnp.asarray(freq)[None, :], jnp.asarray(sign)[None, :]


def _prep(pos_col, x2, name, comm):
    t, d = x2.shape
    tr = _pick(t, (512, 256, 128))
    freq, sign = _rope_lane_constants()

    def body(pos_ref, freq_ref, sign_ref, x_ref, c_ref, s_ref, xb_ref):
        ang = pos_ref[...].astype(F32) * freq_ref[...]
        c_ref[...] = jnp.cos(ang)
        s_ref[...] = sign_ref[...] * jnp.sin(ang)
        xb_ref[...] = x_ref[...].astype(BF16)

    tab = pl.BlockSpec((tr, LANES), lambda i: (i, 0))
    return _call(
        body, name=name, grid=(t // tr,), ins=[pos_col, freq, sign, x2],
        in_specs=[pl.BlockSpec((tr, 1), lambda i: (i, 0)), pl.BlockSpec((1, LANES), lambda i: (0, 0)),
                  pl.BlockSpec((1, LANES), lambda i: (0, 0)), pl.BlockSpec((tr, d), lambda i: (i, 0))],
        out_specs=[tab, tab, pl.BlockSpec((tr, d), lambda i: (i, 0))],
        out_shape=[jax.ShapeDtypeStruct((t, LANES), F32), jax.ShapeDtypeStruct((t, LANES), F32),
                   jax.ShapeDtypeStruct((t, d), BF16)],
        sem=("parallel",), comm=comm)


def _swap8(t):
    width = t.shape[1]
    lane = jnp.bitwise_and(lax.broadcasted_iota(jnp.int32, t.shape, 1), 63)
    return jnp.where(lane < 8, pltpu.roll(t, width - 8, 1), jnp.where(lane < 16, pltpu.roll(t, 8, 1), 0.0))


def _rope(t, c, s):
    return t * c + _swap8(t) * s


def _rope_bwd(d, c, s):
    return d * c + _swap8(d * s)


def _tile4(a):
    return jnp.concatenate([a, a, a, a], axis=1)


def _attn_band(n, k_cur, k_prev, v_cur, v_prev, c_cur, s_cur, c_prev, s_prev):
    kband = jnp.concatenate([_rope(k_prev, c_prev, s_prev), _rope(k_cur, c_cur, s_cur)], axis=0)
    vband = jnp.concatenate([v_prev, v_cur], axis=0)
    qi = lax.broadcasted_iota(jnp.int32, (ATTN_BLOCK, 2 * ATTN_BLOCK), 0)
    kj = lax.broadcasted_iota(jnp.int32, (ATTN_BLOCK, 2 * ATTN_BLOCK), 1)
    dist = qi + ATTN_BLOCK - kj
    valid = (dist >= 0) & (dist < ATTN_BLOCK) & (n * ATTN_BLOCK - ATTN_BLOCK + kj >= 0)
    return (kband.astype(BF16), pltpu.roll(kband, 64, 1).astype(BF16),
            vband.astype(BF16), pltpu.roll(vband, 64, 1).astype(BF16), valid, kband)


def _attn_probs(raw, valid, sink, axis):
    s = jnp.where(valid, raw * ATTN_SCALE, NEG_BIG)
    m = jnp.maximum(jnp.max(s, axis=axis, keepdims=True), sink)
    p = jnp.exp(s - m)
    esink = jnp.exp(sink - m)
    z = jnp.sum(p, axis=axis, keepdims=True) + esink
    return p / z, esink / z


def _attn_valid_t(n):
    kj = lax.broadcasted_iota(jnp.int32, (2 * ATTN_BLOCK, ATTN_BLOCK), 0)
    qi = lax.broadcasted_iota(jnp.int32, (2 * ATTN_BLOCK, ATTN_BLOCK), 1)
    dist = qi + ATTN_BLOCK - kj
    return (dist >= 0) & (dist < ATTN_BLOCK) & (n * ATTN_BLOCK - ATTN_BLOCK + kj >= 0)


def _attn_specs(nb):
    def cur(col, width=KV_W):
        return pl.BlockSpec((ATTN_BLOCK, width), lambda n: (jnp.minimum(n, nb - 1), col))

    def prev(col):
        return pl.BlockSpec((ATTN_BLOCK, KV_W), lambda n: (jnp.maximum(n - 1, 0), col))

    ua_specs = [cur(0, ATTN_W), cur(4), prev(4), cur(5), prev(5)]
    tab_specs = [cur(0), cur(0), prev(0), prev(0)]
    return ua_specs, tab_specs


def _attn_fwd(ua, ctab, stab, sinks, name, comm=None):
    t = ua.shape[0]
    nb = t // ATTN_BLOCK
    ua_specs, tab_specs = _attn_specs(nb)

    def body(q_ref, kc_ref, kp_ref, vc_ref, vp_ref, cc_ref, sc_ref, cp_ref, sp_ref, sink_ref, o_ref, o_t_ref):
        n = pl.program_id(0)
        cc, sc = cc_ref[...], sc_ref[...]
        kb, kb_r, vb, vb_r, valid, _ = _attn_band(n, kc_ref[...], kp_ref[...], vc_ref[...], vp_ref[...],
                                                  cc, sc, cp_ref[...], sp_ref[...])
        qr = _rope(q_ref[...], _tile4(cc), _tile4(sc))
        lo = lax.broadcasted_iota(jnp.int32, (ATTN_BLOCK, LANES), 1) < 64
        heads = []
        for j in range(4):
            qj = qr[:, j * LANES:(j + 1) * LANES]
            for is_lo in (True, False):
                aligned = is_lo == (j < 2)
                qm = jnp.where(lo if is_lo else jnp.logical_not(lo), qj, 0.0).astype(BF16)
                raw = lax.dot_general(qm, kb if aligned else kb_r, _NT, preferred_element_type=F32)
                heads.append((raw, vb if aligned else vb_r, sink_ref[0, len(heads)]))
        halves = []
        for raw, vv, sink in heads:
            probs, _ = _attn_probs(raw, valid, sink, 1)
            halves.append(lax.dot_general(probs.astype(BF16), vv, _NN, preferred_element_type=F32))
        outs = [jnp.where(lo, halves[2 * j], halves[2 * j + 1]) for j in range(4)]
        o_ref[...] = jnp.concatenate(outs, axis=1).astype(o_ref.dtype)
        for j in range(4):
            o_t_ref[j * LANES:(j + 1) * LANES, :] = outs[j].T.astype(o_t_ref.dtype)

    return _call(
        body, name=name, grid=(nb,), ins=[ua, ua, ua, ua, ua, ctab, stab, ctab, stab, sinks],
        in_specs=ua_specs + tab_specs + [pl.BlockSpec(memory_space=pltpu.SMEM)],
        out_specs=[pl.BlockSpec((ATTN_BLOCK, ATTN_W), lambda n: (n, 0)),
                   pl.BlockSpec((ATTN_W, ATTN_BLOCK), lambda n: (0, n))],
        out_shape=[jax.ShapeDtypeStruct((t, ATTN_W), BF16), jax.ShapeDtypeStruct((ATTN_W, t), BF16)],
        sem=("parallel",), comm=comm)


def _attn_bwd(ua, d_out, ctab, stab, sinks, name, comm=None):
    t = ua.shape[0]
    nb = t // ATTN_BLOCK
    ua_specs, tab_specs = _attn_specs(nb)

    def body(q_ref, kc_ref, kp_ref, vc_ref, vp_ref, cc_ref, sc_ref, cp_ref, sp_ref, do_ref, sink_ref,
             dua_ref, dua_t_ref, dbias_ref, dsink_ref, dq_c, dk_c, dv_c, dq_n, dk_n, dv_n):
        n = pl.program_id(0)

        @pl.when(n == 0)
        def _():
            dq_c[...] = jnp.zeros_like(dq_c)
            dk_c[...] = jnp.zeros_like(dk_c)
            dv_c[...] = jnp.zeros_like(dv_c)
            dbias_ref[...] = jnp.zeros_like(dbias_ref)
            dsink_ref[...] = jnp.zeros_like(dsink_ref)

        @pl.when(n == nb)
        def _():
            dq_n[...] = jnp.zeros_like(dq_n)
            dk_n[...] = jnp.zeros_like(dk_n)
            dv_n[...] = jnp.zeros_like(dv_n)

        @pl.when(n < nb)
        def _():
            cc, sc = cc_ref[...], sc_ref[...]
            kb, kb_r, vb, vb_r, _, kb_f32 = _attn_band(n, kc_ref[...], kp_ref[...], vc_ref[...], vp_ref[...],
                                                       cc, sc, cp_ref[...], sp_ref[...])
            valid_t = _attn_valid_t(n)
            c4, s4 = _tile4(cc), _tile4(sc)
            qr = _rope(q_ref[...], c4, s4)
            do = do_ref[...].astype(F32)
            lane = lax.broadcasted_iota(jnp.int32, (ATTN_BLOCK, LANES), 1)
            lo = lane < 64
            lane_row = lax.broadcasted_iota(jnp.int32, (1, LANES), 1)
            k_t = {False: kb_f32.T.astype(BF16), True: pltpu.roll(kb_f32, 64, 1).T.astype(BF16)}
            heads = []
            for j in range(4):
                qj = qr[:, j * LANES:(j + 1) * LANES]
                doj = do[:, j * LANES:(j + 1) * LANES]
                for is_lo in (True, False):
                    aligned = is_lo == (j < 2)
                    msk = lo if is_lo else jnp.logical_not(lo)
                    kk = kb if aligned else kb_r
                    vv = vb if aligned else vb_r
                    qm = jnp.where(msk, qj, 0.0).astype(BF16)
                    dom = jnp.where(msk, doj, 0.0).astype(BF16)
                    heads.append(dict(
                        aligned=aligned, qm=qm, dom=dom, sink=sink_ref[0, len(heads)],
                        raw_t=lax.dot_general(kk, qm, _NT, preferred_element_type=F32),
                        dp_t=lax.dot_general(vv, dom, _NT, preferred_element_type=F32)))
            dk_band = jnp.zeros((2 * ATTN_BLOCK, LANES), F32)
            dv_band = jnp.zeros((2 * ATTN_BLOCK, LANES), F32)
            dsink = jnp.zeros((1, LANES), F32)
            for head, hd in enumerate(heads):
                probs_t, psink = _attn_probs(hd["raw_t"], valid_t, hd["sink"], 0)
                delta_t = jnp.sum(probs_t * hd["dp_t"], axis=0, keepdims=True)
                hd["ds_t"] = (probs_t * (hd["dp_t"] - delta_t) * ATTN_SCALE).astype(BF16)
                dsink = dsink + jnp.where(lane_row == head, -jnp.sum(psink * delta_t), 0.0)
                dk_h = lax.dot_general(hd["ds_t"], hd["qm"], _NN, preferred_element_type=F32)
                dv_h = lax.dot_general(probs_t.astype(BF16), hd["dom"], _NN, preferred_element_type=F32)
                if not hd["aligned"]:
                    dk_h = pltpu.roll(dk_h, 64, 1)
                    dv_h = pltpu.roll(dv_h, 64, 1)
                dk_band = dk_band + dk_h
                dv_band = dv_band + dv_h
            row_lo = lax.broadcasted_iota(jnp.int32, (LANES, ATTN_BLOCK), 0) < 64
            dq_t = [lax.dot_general(k_t[not hd["aligned"]], hd["ds_t"], _NN, preferred_element_type=F32)
                    for hd in heads]
            dqs = [jnp.where(row_lo, dq_t[2 * j], dq_t[2 * j + 1]).T for j in range(4)]
            dq_n[...] = _rope_bwd(jnp.concatenate(dqs, axis=1), c4, s4)
            dk_n[...] = dk_band
            dv_n[...] = dv_band
            dsink_ref[...] += dsink

        dk_prev = _rope_bwd(dk_c[...] + dk_n[0:ATTN_BLOCK, :], cp_ref[...], sp_ref[...])
        dv_prev = dv_c[...] + dv_n[0:ATTN_BLOCK, :]
        full = jnp.concatenate([dq_c[...], dk_prev, dv_prev], axis=1)
        dua_ref[...] = full.astype(dua_ref.dtype)
        for j in range(UA_W // LANES):
            dua_t_ref[j * LANES:(j + 1) * LANES, :] = full[:, j * LANES:(j + 1) * LANES].T.astype(dua_t_ref.dtype)
        dbias_ref[...] += jnp.sum(full, axis=0, keepdims=True)
        dq_c[...] = dq_n[...]
        dk_c[...] = dk_n[ATTN_BLOCK:, :]
        dv_c[...] = dv_n[ATTN_BLOCK:, :]

    return _call(
        body, name=name, grid=(nb + 1,), ins=[ua, ua, ua, ua, ua, ctab, stab, ctab, stab, d_out, sinks],
        in_specs=ua_specs + tab_specs + [
            pl.BlockSpec((ATTN_BLOCK, ATTN_W), lambda n: (jnp.minimum(n, nb - 1), 0)),
            pl.BlockSpec(memory_space=pltpu.SMEM)],
        out_specs=[pl.BlockSpec((ATTN_BLOCK, UA_W), lambda n: (jnp.maximum(n - 1, 0), 0)),
                   pl.BlockSpec((UA_W, ATTN_BLOCK), lambda n: (0, jnp.maximum(n - 1, 0))),
                   pl.BlockSpec((1, UA_W), lambda n: (0, 0)),
                   pl.BlockSpec((1, LANES), lambda n: (0, 0))],
        out_shape=[jax.ShapeDtypeStruct((t, UA_W), BF16), jax.ShapeDtypeStruct((UA_W, t), BF16),
                   jax.ShapeDtypeStruct((1, UA_W), F32),
                   jax.ShapeDtypeStruct((1, LANES), F32)],
        scratch_shapes=[pltpu.VMEM((ATTN_BLOCK, ATTN_W), F32), pltpu.VMEM((ATTN_BLOCK, KV_W), F32),
                        pltpu.VMEM((ATTN_BLOCK, KV_W), F32), pltpu.VMEM((ATTN_BLOCK, ATTN_W), F32),
                        pltpu.VMEM((2 * ATTN_BLOCK, KV_W), F32), pltpu.VMEM((2 * ATTN_BLOCK, KV_W), F32)],
        sem=("arbitrary",), comm=comm)


def _tri_mats():
    r = lax.broadcasted_iota(jnp.int32, (HGRN_CHUNK, LANES), 0)
    c = lax.broadcasted_iota(jnp.int32, (HGRN_CHUNK, LANES), 1)
    lower = ((c <= r) & (c < HGRN_CHUNK)).astype(F32)
    upper = ((c >= r) & (c < HGRN_CHUNK)).astype(F32)
    return lower, upper


def _tri_apply(tri, g):
    pad = jnp.concatenate([g, jnp.zeros_like(g)], axis=0)
    return lax.dot_general(tri, pad, _NN, precision=lax.Precision.HIGHEST, preferred_element_type=F32)


def _sub_masks():
    s = lax.broadcasted_iota(jnp.int32, (HGRN_CHUNK, LANES), 0)
    tt = lax.broadcasted_iota(jnp.int32, (HGRN_CHUNK, LANES), 1)
    return [(tt >= HGRN_SUB * i) & (tt < HGRN_SUB * (i + 1)) & (s <= tt) for i in range(HGRN_CHUNK // HGRN_SUB)]


def _hgrn_gates(hq, hf, lb_ref, b_scr):
    lb = _sig(lb_ref[0:1, :] - lb_ref[1:2, :])
    q = hq * _sig(hq)
    sg = _sig(hf)
    f = lb + (1.0 - lb) * sg
    k = 1.0 - f
    lower, _ = _tri_mats()
    b = _tri_apply(lower, jnp.log(f))
    b_scr[...] = b
    nsub = HGRN_CHUNK // HGRN_SUB
    starts = [jnp.zeros((1, HG_W), F32)] + [b_scr[HGRN_SUB * i - 1:HGRN_SUB * i, :] for i in range(1, nsub)]
    pq = jnp.concatenate([jnp.broadcast_to(p, (HGRN_SUB, HG_W)) for p in starts], axis=0)
    b_last = b_scr[HGRN_CHUNK - 1:HGRN_CHUNK, :]
    e_q = jnp.exp(b - pq)
    e_k = [jnp.exp(jnp.minimum(p - b, EXP_CLAMP)) for p in starts]
    e_b = jnp.exp(b)
    e_bl = jnp.exp(b_last - b)
    e_last = jnp.exp(b_last)
    return q, sg, f, k, lb, e_q, e_k, e_b, e_bl, e_last


def _sub_masks_ts():
    tt = lax.broadcasted_iota(jnp.int32, (HGRN_CHUNK, LANES), 0)
    s = lax.broadcasted_iota(jnp.int32, (HGRN_CHUNK, LANES), 1)
    return [(tt >= HGRN_SUB * i) & (tt < HGRN_SUB * (i + 1)) & (s <= tt) for i in range(HGRN_CHUNK // HGRN_SUB)]


def _masked_sum(blocks, masks, axis):
    step = HGRN_CHUNK if axis == 0 else LANES
    acc = jnp.zeros((HGRN_CHUNK, LANES), F32)
    for i, msk in enumerate(masks):
        blk = blocks[step * i:step * (i + 1), :] if axis == 0 else blocks[:, step * i:step * (i + 1)]
        acc = acc + jnp.where(msk, blk, 0.0)
    return acc


def _store_transposed(out_t_ref, chunk_rows):
    width = chunk_rows[0].shape[1]
    if len(chunk_rows) == 1:
        groups = [jnp.concatenate([chunk_rows[0], jnp.zeros_like(chunk_rows[0])], axis=0)]
    else:
        groups = [jnp.concatenate(chunk_rows[g:g + 2], axis=0) for g in range(0, len(chunk_rows), 2)]
    for g, rows in enumerate(groups):
        for c in range(width // LANES):
            tile = rows[:, c * LANES:(c + 1) * LANES].T.astype(out_t_ref.dtype)
            if len(chunk_rows) == 1:
                out_t_ref[c * LANES:(c + 1) * LANES, :] = tile[:, 0:HGRN_CHUNK]
            else:
                out_t_ref[c * LANES:(c + 1) * LANES, g * LANES:(g + 1) * LANES] = tile


def _hgrn_chunk_inputs(j, hq_ref, hf_ref, hi_ref, hg_ref, lb_ref, b_scr):
    rows = slice(j * HGRN_CHUNK, (j + 1) * HGRN_CHUNK)
    hq, hf, v, hg = hq_ref[rows, :], hf_ref[rows, :], hi_ref[rows, :], hg_ref[rows, :]
    q, sg, f, k, lb, e_q, e_k, e_b, e_bl, e_last = _hgrn_gates(hq, hf, lb_ref, b_scr.at[j])
    return dict(rows=rows, hq=hq, v=v, hg=hg, q=q, sg=sg, f=f, k=k, lb=lb, e_q=e_q, e_k=e_k, e_b=e_b, e_bl=e_bl,
                e_last=e_last, qt=q * e_q, qb=q * e_b, kd=k * e_bl, khat=[k * e for e in e_k])


def _hgrn_fwd(uh, lb_raw, norm_g, name, comm=None):
    t = uh.shape[0]
    nc = t // HGRN_CHUNK
    cps = _pick(nc, (HGRN_CHUNKS_PER_STEP, 2, 1))
    rows_step = cps * HGRN_CHUNK

    def body(hq_ref, hf_ref, hi_ref, hg_ref, lb_ref, ng_ref, r_ref, r_t_ref, o_ref, st_out_ref, st_ref, b_scr):
        @pl.when(pl.program_id(0) == 0)
        def _():
            st_ref[...] = jnp.zeros_like(st_ref)

        masks = _sub_masks_ts()
        ng = ng_ref[...]
        zpad = jnp.zeros((HGRN_CHUNK, LANES), F32)
        heads = [slice(h * LANES, (h + 1) * LANES) for h in range(4)]
        chunks = [_hgrn_chunk_inputs(j, hq_ref, hf_ref, hi_ref, hg_ref, lb_ref, b_scr) for j in range(cps)]
        for ch in chunks:
            ch["scores"] = [_dot3(ch["qt"][:, sl],
                                  jnp.concatenate([x for kh in ch["khat"] for x in (kh[:, sl], zpad)], axis=0), _NT)
                            for sl in heads]
        for j, ch in enumerate(chunks):
            o_heads, y_heads = [], []
            for h, sl in enumerate(heads):
                a_ts = _masked_sum(ch["scores"][h], masks, 1)
                vh = ch["v"][:, sl].astype(BF16)
                v_pad = jnp.concatenate([vh, jnp.zeros_like(vh)], axis=0)
                o_intra = lax.dot_general(a_ts.astype(BF16), v_pad, _NN, preferred_element_type=F32)
                st = st_ref[h]
                st_out_ref[j, h] = st
                o_inter = _dot(ch["qb"][:, sl], st, _NT)
                st_ref[h] = st * ch["e_last"][:, sl] + _dot(vh, ch["kd"][:, sl], _TN)
                oh = o_intra + o_inter
                rs = lax.rsqrt(jnp.mean(oh * oh, axis=1, keepdims=True) + RMS_EPS)
                o_heads.append(oh)
                y_heads.append(oh * rs * ng)
            hg = ch["hg"]
            o_ref[ch["rows"], :] = jnp.concatenate(o_heads, axis=1)
            ch["r"] = jnp.concatenate(y_heads, axis=1) * (hg * _sig(hg))
            r_ref[ch["rows"], :] = ch["r"].astype(r_ref.dtype)
        _store_transposed(r_t_ref, [ch["r"] for ch in chunks])

    col = lambda j: pl.BlockSpec((rows_step, HG_W), lambda c: (c, j))
    return _call(
        body, name=name, grid=(nc // cps,), ins=[uh, uh, uh, uh, lb_raw, norm_g],
        in_specs=[col(0), col(1), col(2), col(3),
                  pl.BlockSpec((2, HG_W), lambda c: (0, 0)), pl.BlockSpec((1, LANES), lambda c: (0, 0))],
        out_specs=[pl.BlockSpec((rows_step, HG_W), lambda c: (c, 0)),
                   pl.BlockSpec((HG_W, rows_step), lambda c: (0, c)),
                   pl.BlockSpec((rows_step, HG_W), lambda c: (c, 0)),
                   pl.BlockSpec((cps, 4, LANES, LANES), lambda c: (c, 0, 0, 0))],
        out_shape=[jax.ShapeDtypeStruct((t, HG_W), BF16), jax.ShapeDtypeStruct((HG_W, t), BF16),
                   jax.ShapeDtypeStruct((t, HG_W), F32), jax.ShapeDtypeStruct((nc, 4, LANES, LANES), F32)],
        scratch_shapes=[pltpu.VMEM((4, LANES, LANES), F32), pltpu.VMEM((cps, HGRN_CHUNK, HG_W), F32)],
        sem=("arbitrary",), comm=comm)


def _hgrn_bwd(uh, o_pre, d_r, states, lb_raw, norm_g, name, comm=None):
    t = uh.shape[0]
    nc = t // HGRN_CHUNK
    cps = _pick(nc, (HGRN_CHUNKS_PER_STEP, 2, 1))
    ns = nc // cps
    rows_step = cps * HGRN_CHUNK
    nsub = HGRN_CHUNK // HGRN_SUB

    def body(hq_ref, hf_ref, hi_ref, hg_ref, o_ref, dr_ref, st_in_ref, lb_ref, ng_ref,
             duh_ref, duh_t_ref, dbias_ref, dng_ref, dlb_ref, dst_ref, b_scr, dlb_acc):
        i = pl.program_id(0)

        @pl.when(i == 0)
        def _():
            dst_ref[...] = jnp.zeros_like(dst_ref)
            dbias_ref[...] = jnp.zeros_like(dbias_ref)
            dng_ref[...] = jnp.zeros_like(dng_ref)
            dlb_acc[...] = jnp.zeros_like(dlb_acc)

        masks_st = _sub_masks()
        masks_ts = _sub_masks_ts()
        ng = ng_ref[...]
        zpad = jnp.zeros((HGRN_CHUNK, LANES), F32)
        _, upper = _tri_mats()
        heads = [slice(h * LANES, (h + 1) * LANES) for h in range(4)]
        row = lax.broadcasted_iota(jnp.int32, (HGRN_CHUNK, HG_W), 0)

        chunks = [_hgrn_chunk_inputs(j, hq_ref, hf_ref, hi_ref, hg_ref, lb_ref, b_scr) for j in range(cps)]
        dng = jnp.zeros((1, LANES), F32)
        for ch in chunks:
            o = o_ref[ch["rows"], :]
            dr = dr_ref[ch["rows"], :].astype(F32)
            hg = ch["hg"]
            sgg = _sig(hg)
            dy = dr * (hg * sgg)
            do_h, y_h = [], []
            for sl in heads:
                oh = o[:, sl]
                rs = lax.rsqrt(jnp.mean(oh * oh, axis=1, keepdims=True) + RMS_EPS)
                y_h.append(oh * rs * ng)
                dng = dng + jnp.sum(dy[:, sl] * oh * rs, axis=0, keepdims=True)
                w = dy[:, sl] * ng
                do_h.append(rs * (w - oh * (rs * rs) * jnp.mean(w * oh, axis=1, keepdims=True)))
            ch["do"] = do_h
            ch["dhg"] = dr * jnp.concatenate(y_h, axis=1) * _dsilu(hg, sgg)

        for ch in chunks:
            ch["kst"], ch["kpad"], ch["qt_pad"], ch["v_b"], ch["do_pad"] = [], [], [], [], []
            ch["ats"], ch["d_at"], ch["d_a"] = [], [], []
            for h, sl in enumerate(heads):
                kst = jnp.concatenate([kh[:, sl] for kh in ch["khat"]], axis=0)
                kpad = jnp.concatenate([x for kh in ch["khat"] for x in (kh[:, sl], zpad)], axis=0)
                qt_pad = jnp.concatenate([ch["qt"][:, sl], zpad], axis=0)
                vh = ch["v"][:, sl].astype(BF16)
                v_pad = jnp.concatenate([vh, jnp.zeros_like(vh)], axis=0)
                do_b = ch["do"][h].astype(BF16)
                do_pad = jnp.concatenate([do_b, jnp.zeros_like(do_b)], axis=0)
                ch["kst"].append(kst)
                ch["kpad"].append(kpad)
                ch["qt_pad"].append(qt_pad)
                ch["v_b"].append(vh)
                ch["do_pad"].append(do_pad)
                ch["ats"].append(_dot3(kst, qt_pad, _NT))
                ch["d_at"].append(lax.dot_general(vh, do_pad, _NT, preferred_element_type=F32))
                ch["d_a"].append(lax.dot_general(do_b, v_pad, _NT, preferred_element_type=F32))

        for ch in chunks:
            ch["d_kst"], ch["d_qt"], ch["dv"] = [], [], []
            for h in range(4):
                at = _masked_sum(ch["ats"][h], masks_st, 0)
                d_ats = jnp.concatenate([jnp.where(m, ch["d_at"][h], 0.0) for m in masks_st], axis=0)
                d_a_cat = jnp.concatenate([jnp.where(m, ch["d_a"][h], 0.0) for m in masks_ts], axis=1)
                ch["d_kst"].append(_dot3(d_ats, ch["qt_pad"][h], _NN))
                ch["d_qt"].append(_dot3(d_a_cat, ch["kpad"][h], _NN))
                ch["dv"].append(lax.dot_general(at.astype(BF16), ch["do_pad"][h], _NN, preferred_element_type=F32))

        for j in reversed(range(cps)):
            ch = chunks[j]
            q, k, sg, f, lb = ch["q"], ch["k"], ch["sg"], ch["f"], ch["lb"]
            dq_h, dk_h, dv_h, extra_h = [], [], [], []
            for h, sl in enumerate(heads):
                st_prev = st_in_ref[j, h]
                d_st = dst_ref[h]
                d_st_b = d_st.astype(BF16)
                do_b = ch["do_pad"][h][0:HGRN_CHUNK, :]
                kd, e_last = ch["kd"][:, sl], ch["e_last"][:, sl]
                dv = ch["dv"][h] + _dot(kd, d_st_b, _NT)
                d_qb = _dot(do_b, st_prev, _NN)
                d_kd = lax.dot_general(ch["v_b"][h], d_st_b, _NN, preferred_element_type=F32)
                extra_h.append(jnp.sum(st_prev * d_st, axis=0, keepdims=True) * e_last
                               + jnp.sum(kd * d_kd, axis=0, keepdims=True))
                dst_ref[h] = d_st * e_last + _dot(do_b, ch["qb"][:, sl], _TN)
                dq_h.append(ch["d_qt"][h] * ch["e_q"][:, sl] + d_qb * ch["e_b"][:, sl])
                dkk = d_kd * ch["e_bl"][:, sl]
                for s_ in range(nsub):
                    dkk = dkk + ch["d_kst"][h][HGRN_CHUNK * s_:HGRN_CHUNK * (s_ + 1), :] * ch["e_k"][s_][:, sl]
                dk_h.append(dkk)
                dv_h.append(dv)
            dq = jnp.concatenate(dq_h, axis=1)
            dk = jnp.concatenate(dk_h, axis=1)
            dv = jnp.concatenate(dv_h, axis=1)
            extra = jnp.concatenate(extra_h, axis=1)
            db = q * dq - k * dk + jnp.where(row == HGRN_CHUNK - 1, extra, 0.0)
            dg = _tri_apply(upper, db)
            df = dg / f - dk
            dhf = df * (1.0 - lb) * sg * (1.0 - sg)
            dhq = dq * _dsilu(ch["hq"], _sig(ch["hq"]))
            full = jnp.concatenate([dhq, dhf, dv, ch["dhg"]], axis=1)
            duh_ref[ch["rows"], :] = full.astype(duh_ref.dtype)
            ch["full"] = full
            dbias_ref[...] += jnp.sum(full, axis=0, keepdims=True)
            dlb_acc[...] += jnp.sum(df * (1.0 - sg), axis=0, keepdims=True)
        dng_ref[...] += dng
        _store_transposed(duh_t_ref, [ch["full"] for ch in chunks])

        @pl.when(i == ns - 1)
        def _():
            lb = chunks[0]["lb"]
            d_a0 = dlb_acc[...] * lb * (1.0 - lb)
            r8 = lax.broadcasted_iota(jnp.int32, (8, HG_W), 0)
            dlb_ref[...] = jnp.where(r8 == 0, d_a0, jnp.where(r8 == 1, -d_a0, 0.0))

    col = lambda j: pl.BlockSpec((rows_step, HG_W), lambda i: (ns - 1 - i, j))
    return _call(
        body, name=name, grid=(ns,), ins=[uh, uh, uh, uh, o_pre, d_r, states, lb_raw, norm_g],
        in_specs=[col(0), col(1), col(2), col(3), col(0), col(0),
                  pl.BlockSpec((cps, 4, LANES, LANES), lambda i: (ns - 1 - i, 0, 0, 0)),
                  pl.BlockSpec((2, HG_W), lambda i: (0, 0)), pl.BlockSpec((1, LANES), lambda i: (0, 0))],
        out_specs=[pl.BlockSpec((rows_step, UH_W), lambda i: (ns - 1 - i, 0)),
                   pl.BlockSpec((UH_W, rows_step), lambda i: (0, ns - 1 - i)),
                   pl.BlockSpec((1, UH_W), lambda i: (0, 0)),
                   pl.BlockSpec((1, LANES), lambda i: (0, 0)),
                   pl.BlockSpec((8, HG_W), lambda i: (0, 0))],
        out_shape=[jax.ShapeDtypeStruct((t, UH_W), BF16), jax.ShapeDtypeStruct((UH_W, t), BF16),
                   jax.ShapeDtypeStruct((1, UH_W), F32),
                   jax.ShapeDtypeStruct((1, LANES), F32), jax.ShapeDtypeStruct((8, HG_W), F32)],
        scratch_shapes=[pltpu.VMEM((4, LANES, LANES), F32), pltpu.VMEM((cps, HGRN_CHUNK, HG_W), F32),
                        pltpu.VMEM((1, HG_W), F32)],
        sem=("arbitrary",), comm=comm)


def _ln_bwd_math(dy, xhat, rstd, g):
    dxh = dy * g
    return rstd * (dxh - jnp.mean(dxh, axis=1, keepdims=True)
                   - xhat * jnp.mean(dxh * xhat, axis=1, keepdims=True))


def _mm_rows(a, b, extras, *, name, epilogue, out_shape, out_specs, tb=False, tm=512, tk=1408):
    m, kdim = a.shape
    n = b.shape[0] if tb else b.shape[1]
    tm = _pick(m, (tm, 256, 128))
    tk = _pick(kdim, (tk, 1408, 1024, 768, 512, 256, 128))
    nk = kdim // tk
    b_spec = pl.BlockSpec((n, tk), lambda i, k: (0, k)) if tb else pl.BlockSpec((tk, n), lambda i, k: (k, 0))
    dims = _NT if tb else _NN
    n_ex, n_out = len(extras), len(out_shape)

    def body(*refs):
        a_ref, b_ref = refs[0], refs[1]
        ex_refs = refs[2:2 + n_ex]
        o_refs = refs[2 + n_ex:2 + n_ex + n_out]
        acc_ref = refs[-1]
        i, k = pl.program_id(0), pl.program_id(1)

        @pl.when(k == 0)
        def _():
            acc_ref[...] = jnp.zeros_like(acc_ref)

        acc_ref[...] += _dot(a_ref[...], b_ref[...], dims)

        @pl.when(k == nk - 1)
        def _():
            epilogue(acc_ref[...], ex_refs, o_refs, i == 0)

    return pl.pallas_call(
        body, name=name, grid=(m // tm, nk),
        in_specs=[pl.BlockSpec((tm, tk), lambda i, k: (i, k)), b_spec] + [sp for _, sp in extras],
        out_specs=list(out_specs), out_shape=list(out_shape),
        scratch_shapes=[pltpu.VMEM((tm, n), F32)],
        compiler_params=_cp("arbitrary", "arbitrary"),
    )(a, b, *[arr for arr, _ in extras])


def _rows_specs(tm, d):
    row = pl.BlockSpec((tm, d), lambda i, k: (i, 0))
    vec = pl.BlockSpec((1, d), lambda i, k: (0, 0))
    col = pl.BlockSpec((tm, 1), lambda i, k: (i, 0))
    return row, vec, col


def _mm_ln_fwd(a, b, addend, g, beta, name, tm=512):
    t, d = addend.shape
    tm = _pick(t, (tm, 256, 128))
    row, vec, col = _rows_specs(tm, d)

    def epilogue(acc, ex, outs, first):
        z = acc + ex[0][...]
        mu = jnp.mean(z, axis=1, keepdims=True)
        zc = z - mu
        rstd = lax.rsqrt(jnp.mean(zc * zc, axis=1, keepdims=True) + LN_EPS)
        xhat = zc * rstd
        h = xhat * ex[1][...] + ex[2][...]
        outs[0][...] = h
        outs[1][...] = h.astype(BF16)
        outs[2][...] = xhat
        outs[3][...] = rstd
        for r in range(tm // LANES):
            for c in range(d // LANES):
                outs[4][c * LANES:(c + 1) * LANES, r * LANES:(r + 1) * LANES] = (
                    h[r * LANES:(r + 1) * LANES, c * LANES:(c + 1) * LANES].T.astype(BF16))

    return _mm_rows(a, b, [(addend, row), (g, vec), (beta, vec)], name=name, epilogue=epilogue, tm=tm,
                    out_shape=[jax.ShapeDtypeStruct((t, d), F32), jax.ShapeDtypeStruct((t, d), BF16),
                               jax.ShapeDtypeStruct((t, d), F32), jax.ShapeDtypeStruct((t, 1), F32),
                               jax.ShapeDtypeStruct((d, t), BF16)],
                    out_specs=[row, row, row, col, pl.BlockSpec((d, tm), lambda i, k: (0, i))])


CONV_RB = 32
HALO = 8


def _sum8(x):
    acc = x[0:8]
    for r in range(8, x.shape[0], 8):
        acc = acc + x[r:r + 8]
    return acc


FFN_TILE = 256
FFN_COLS = 256


def _rows_before(win, k):
    return pltpu.roll(win, k, 0)[HALO:]


def _rows_after(win, k):
    n = win.shape[0]
    return pltpu.roll(win, n - k, 0)[0:n - HALO]


def _resident(shape):
    return pl.BlockSpec(shape, lambda i: (0,) * len(shape), pipeline_mode=pl.Buffered(1))


def _ffn_fwd(h1b, h1, w_up_t, conv_w, conv_b, w_down, target, ln2_g, ln2_b, name, comm=None):
    t, d = h1.shape
    tr = _pick(t, (FFN_TILE, 128))
    nblk = D_FF // FFN_COLS
    rb = CONV_RB

    def body(a_ref, wup_ref, cw_ref, cb_ref, wd_ref, h1_ref, tgt_ref, g_ref, b_ref,
             u2_ref, hm_ref, dz_ref, dg_ref, db_ref, loss_ref, ext):
        i = pl.program_id(0)

        @pl.when(i == 0)
        def _():
            ext[0:HALO, :] = jnp.zeros((HALO, D_FF), F32)
            dg_ref[...] = jnp.zeros_like(dg_ref)
            db_ref[...] = jnp.zeros_like(db_ref)
            loss_ref[...] = jnp.zeros_like(loss_ref)

        a = a_ref[...]
        for c in range(nblk):
            cs = slice(c * FFN_COLS, (c + 1) * FFN_COLS)
            vs = slice(D_FF + c * FFN_COLS, D_FF + (c + 1) * FFN_COLS)
            gate_pre = lax.dot_general(a, wup_ref[cs, :], _NT, preferred_element_type=F32)
            u2_ref[:, cs] = gate_pre
            ext[HALO:, cs] = gate_pre
            u2_ref[:, vs] = lax.dot_general(a, wup_ref[vs, :], _NT, preferred_element_type=F32)
        acc = jnp.zeros((tr, d), F32)
        for c in range(nblk):
            cs = slice(c * FFN_COLS, (c + 1) * FFN_COLS)
            for sub in range(FFN_COLS // LANES):
                ln = slice(c * FFN_COLS + sub * LANES, c * FFN_COLS + (sub + 1) * LANES)
                vl = slice(D_FF + c * FFN_COLS + sub * LANES, D_FF + c * FFN_COLS + (sub + 1) * LANES)
                w0, w1, w2, bb = cw_ref[0:1, ln], cw_ref[1:2, ln], cw_ref[2:3, ln], cb_ref[:, ln]
                for r0 in range(0, tr, rb):
                    win = ext[r0:r0 + HALO + rb, ln]
                    gate = _rows_before(win, 2) * w0 + _rows_before(win, 1) * w1 + win[HALO:] * w2 + bb
                    hm_ref[r0:r0 + rb, ln] = (gate * _sig(gate) * u2_ref[r0:r0 + rb, vl]).astype(hm_ref.dtype)
            acc = acc + lax.dot_general(hm_ref[:, cs], wd_ref[cs, :], _NN, preferred_element_type=F32)
        ext[0:HALO, :] = ext[tr:tr + HALO, :]

        z = acc + ALPHA * h1_ref[...]
        gg = g_ref[...]
        mu = jnp.mean(z, axis=1, keepdims=True)
        zc = z - mu
        rstd = lax.rsqrt(jnp.mean(zc * zc, axis=1, keepdims=True) + LN_EPS)
        xhat = zc * rstd
        err = xhat * gg + b_ref[...] - tgt_ref[...]
        loss_ref[...] += 0.5 * jnp.sum(jnp.mean(err * err, axis=1, keepdims=True))
        dy = err * (1.0 / d)
        dz_ref[...] = _ln_bwd_math(dy, xhat, rstd, gg)
        dg_ref[...] += jnp.sum(dy * xhat, axis=0, keepdims=True)
        db_ref[...] += jnp.sum(dy, axis=0, keepdims=True)

    row = lambda w: pl.BlockSpec((tr, w), lambda i: (i, 0))
    vec = pl.BlockSpec((1, d), lambda i: (0, 0))
    return _call(
        body, name=name, grid=(t // tr,),
        ins=[h1b, w_up_t, conv_w, conv_b, w_down, h1, target, ln2_g, ln2_b],
        in_specs=[row(d), _resident((2 * D_FF, d)), _resident((3, D_FF)), _resident((1, D_FF)),
                  _resident((D_FF, d)), row(d), row(d), vec, vec],
        out_specs=[row(2 * D_FF), row(D_FF), row(d), vec, vec, pl.BlockSpec((1, LANES), lambda i: (0, 0))],
        out_shape=[jax.ShapeDtypeStruct((t, 2 * D_FF), F32), jax.ShapeDtypeStruct((t, D_FF), BF16),
                   jax.ShapeDtypeStruct((t, d), F32), jax.ShapeDtypeStruct((1, d), F32),
                   jax.ShapeDtypeStruct((1, d), F32), jax.ShapeDtypeStruct((1, LANES), F32)],
        scratch_shapes=[pltpu.VMEM((tr + HALO, D_FF), F32)],
        sem=("arbitrary",), comm=comm)


def _ffn_bwd(dz2, u2, w_down, w_up_t, conv_w, conv_b, xhat1, rstd1, ln1_g, name, comm=None):
    t, d = dz2.shape
    tr = _pick(t, (FFN_TILE, 128))
    nt = t // tr
    hb = tr // HALO
    nblk = D_FF // FFN_COLS
    rb = CONV_RB

    def body(dz2_ref, dz2_next_ref, u2_ref, gp_prev_ref, wd_ref, wup_ref, cw_ref, cb_ref, xhat_ref, rstd_ref,
             g1_ref, du_ref, dz1_ref, dw_ref, dcb_ref, dg1_ref, db1_ref, head, dh_s, dg_s):
        i = pl.program_id(0)

        @pl.when(i == 0)
        def _():
            dg_s[tr:, :] = jnp.zeros((HALO, D_FF), F32)
            dw_ref[...] = jnp.zeros_like(dw_ref)
            dcb_ref[...] = jnp.zeros_like(dcb_ref)
            dg1_ref[...] = jnp.zeros_like(dg1_ref)
            db1_ref[...] = jnp.zeros_like(db1_ref)

        dz2 = dz2_ref[...]

        @pl.when(i == 0)
        def _():
            dz2_b = dz2.astype(BF16)
            for c in range(nblk):
                cs = slice(c * FFN_COLS, (c + 1) * FFN_COLS)
                dh_s[:, cs] = lax.dot_general(dz2_b, wd_ref[cs, :], _NT, preferred_element_type=F32)

        dz2_next = dz2_next_ref[...].astype(BF16)
        dh_next = [lax.dot_general(dz2_next, wd_ref[c * FFN_COLS:(c + 1) * FFN_COLS, :], _NT,
                                   preferred_element_type=F32) for c in range(nblk)]
        head[0:HALO, :] = jnp.where(i == nt - 1, 0.0, gp_prev_ref[...])
        head[HALO:, :] = u2_ref[0:rb, 0:D_FF]

        acc = jnp.zeros((tr, d), F32)
        for blk in range(nblk):
            for c in range(blk * FFN_COLS // LANES, (blk + 1) * FFN_COLS // LANES):
                ln = slice(c * LANES, (c + 1) * LANES)
                vl = slice(D_FF + c * LANES, D_FF + (c + 1) * LANES)
                w0, w1, w2, bb = cw_ref[0:1, ln], cw_ref[1:2, ln], cw_ref[2:3, ln], cb_ref[:, ln]
                acc_b = jnp.zeros((8, LANES), F32)
                acc_w = [jnp.zeros((8, LANES), F32) for _ in range(3)]
                for r0 in range(0, tr, rb):
                    win = head[:, ln] if r0 == 0 else u2_ref[r0 - HALO:r0 + rb, ln]
                    g_m2, g_m1, g_0 = _rows_before(win, 2), _rows_before(win, 1), win[HALO:]
                    gate = g_m2 * w0 + g_m1 * w1 + g_0 * w2 + bb
                    sg = _sig(gate)
                    dh = dh_s[r0:r0 + rb, ln]
                    dgate = dh * u2_ref[r0:r0 + rb, vl] * _dsilu(gate, sg)
                    dg_s[r0:r0 + rb, ln] = dgate
                    du_ref[r0:r0 + rb, vl] = (dh * (gate * sg)).astype(du_ref.dtype)
                    acc_b = acc_b + _sum8(dgate)
                    acc_w[0] = acc_w[0] + _sum8(dgate * g_m2)
                    acc_w[1] = acc_w[1] + _sum8(dgate * g_m1)
                    acc_w[2] = acc_w[2] + _sum8(dgate * g_0)
                dcb_ref[:, ln] += jnp.sum(acc_b, axis=0, keepdims=True)
                for j in range(3):
                    dw_ref[j:j + 1, ln] += jnp.sum(acc_w[j], axis=0, keepdims=True)
                for r0 in range(0, tr, rb):
                    win = dg_s[r0:r0 + rb + HALO, ln]
                    d_gp = _rows_after(win, 2) * w0 + _rows_after(win, 1) * w1 + win[0:rb] * w2
                    du_ref[r0:r0 + rb, ln] = d_gp.astype(du_ref.dtype)
            cs = slice(blk * FFN_COLS, (blk + 1) * FFN_COLS)
            vs = slice(D_FF + blk * FFN_COLS, D_FF + (blk + 1) * FFN_COLS)
            acc = acc + lax.dot_general(du_ref[:, cs], wup_ref[cs, :], _NN, preferred_element_type=F32)
            acc = acc + lax.dot_general(du_ref[:, vs], wup_ref[vs, :], _NN, preferred_element_type=F32)
        dg_s[tr:, :] = dg_s[0:HALO, :]
        for c in range(nblk):
            dh_s[:, c * FFN_COLS:(c + 1) * FFN_COLS] = dh_next[c]
        dy = acc + ALPHA * dz2
        xh = xhat_ref[...]
        dz1_ref[...] = _ln_bwd_math(dy, xh, rstd_ref[...], g1_ref[...])
        dg1_ref[...] += jnp.sum(dy * xh, axis=0, keepdims=True)
        db1_ref[...] += jnp.sum(dy, axis=0, keepdims=True)

    rev = lambda w: pl.BlockSpec((tr, w), lambda i: (nt - 1 - i, 0))
    vec = pl.BlockSpec((1, d), lambda i: (0, 0))
    return _call(
        body, name=name, grid=(nt,),
        ins=[dz2, dz2, u2, u2, w_down, w_up_t, conv_w, conv_b, xhat1, rstd1, ln1_g],
        in_specs=[rev(d), pl.BlockSpec((tr, d), lambda i: (jnp.maximum(nt - 2 - i, 0), 0)), rev(2 * D_FF),
                  pl.BlockSpec((HALO, D_FF), lambda i: (jnp.maximum((nt - 1 - i) * hb - 1, 0), 0)),
                  _resident((D_FF, d)), _resident((2 * D_FF, d)), _resident((3, D_FF)), _resident((1, D_FF)),
                  rev(d), pl.BlockSpec((tr, 1), lambda i: (nt - 1 - i, 0)), vec],
        out_specs=[rev(2 * D_FF), rev(d), pl.BlockSpec((8, D_FF), lambda i: (0, 0)),
                   pl.BlockSpec((1, D_FF), lambda i: (0, 0)), vec, vec],
        out_shape=[jax.ShapeDtypeStruct((t, 2 * D_FF), BF16), jax.ShapeDtypeStruct((t, d), F32),
                   jax.ShapeDtypeStruct((8, D_FF), F32), jax.ShapeDtypeStruct((1, D_FF), F32),
                   jax.ShapeDtypeStruct((1, d), F32), jax.ShapeDtypeStruct((1, d), F32)],
        scratch_shapes=[pltpu.VMEM((HALO + rb, D_FF), F32), pltpu.VMEM((tr, D_FF), F32),
                        pltpu.VMEM((tr + HALO, D_FF), F32)],
        sem=("arbitrary",), comm=comm)


def _adamw(w, g, m, v, name):
    rows, cols = w.shape
    tr = _pick(rows, (256, 128, 64, 32, 16, 8))

    def body(w_ref, g_ref, m_ref, v_ref, d_ref, nm_ref, nv_ref):
        d_ref[...], nm_ref[...], nv_ref[...] = _adamw_math(w_ref[...], g_ref[...], m_ref[...], v_ref[...])

    spec = pl.BlockSpec((tr, cols), lambda i: (i, 0))
    shp = jax.ShapeDtypeStruct((rows, cols), F32)
    return pl.pallas_call(
        body, name=name, grid=(rows // tr,),
        in_specs=[spec, spec, spec, spec], out_specs=[spec, spec, spec], out_shape=[shp, shp, shp],
        compiler_params=_cp("parallel"),
    )(w, g, m, v)


def _pad_rows(a, rows):
    return jnp.pad(a, ((0, rows - a.shape[0]), (0, 0)))


SMALL_LAYOUT = (("ln1_g", 1024), ("ln1_b", 1024), ("b_in", 2816), ("sinks", 8), ("hgrn_lb", 1024),
                ("hgrn_norm_g", 128), ("ln2_g", 1024), ("ln2_b", 1024), ("conv_b", 2816), ("loss", 1))
SMALL_SHAPES = {"ln1_g": (1, 1024), "ln1_b": (1, 1024), "b_in": (1, 2816), "sinks": (1, 8), "hgrn_lb": (2, 512),
                "hgrn_norm_g": (1, 128), "ln2_g": (1, 1024), "ln2_b": (1, 1024), "conv_b": (1, 2816),
                "loss": (1,)}


def _pack_small(parts):
    rows = []
    for name, size in SMALL_LAYOUT:
        flat = parts[name].reshape(-1).astype(F32)
        padded = -(-size // LANES) * LANES
        rows.append(jnp.pad(flat, (0, padded - size)).reshape(-1, LANES))
    return _pad_rows(jnp.concatenate(rows, axis=0), SMALL_ROWS)


def _unpack_small(pack):
    out, r = {}, 0
    for name, size in SMALL_LAYOUT:
        nrows = -(-size // LANES)
        out[name] = pack[r:r + nrows].reshape(-1)[:size].reshape(SMALL_SHAPES[name])
        r += nrows
    return out


def _own(full, rows):
    return lax.dynamic_slice_in_dim(full, _me() * rows, rows, axis=0)


def kernel(x, positions, ln1_g, ln1_b, w_in, b_in, sinks, hgrn_lb, hgrn_norm_g, w_o, ln2_g, ln2_b, w_up, conv_w, conv_b, w_down, loss_target, m_ln1_g, m_ln1_b, m_w_in, m_b_in, m_sinks, m_hgrn_lb, m_hgrn_norm_g, m_w_o, m_ln2_g, m_ln2_b, m_w_up, m_conv_w, m_conv_b, m_w_down, v_ln1_g, v_ln1_b, v_w_in, v_b_in, v_sinks, v_hgrn_lb, v_hgrn_norm_g, v_w_o, v_ln2_g, v_ln2_b, v_w_up, v_conv_w, v_conv_b, v_w_down):
    t = x.shape[1]
    x2 = x[0]
    target = loss_target[0]
    pos_col = positions.reshape(t, 1)

    w_in_t_s = w_in[0].T.astype(BF16)
    w_up_t_s = w_up[0].T.astype(BF16)
    w_o_s = w_o[0].astype(BF16)
    w_down_s = w_down[0].astype(BF16)
    (ctab, stab, xb), (w_in_t_g, cw_g) = _prep(
        pos_col, x2, "prep_ag_w_in", _Comm([{"kind": "gather", "arr": w_in_t_s}, {"kind": "gather", "arr": _pad_rows(conv_w[0], 8)}]))
    w_in_t = w_in_t_g.reshape(D_FF, D_MODEL)
    w_a_t, w_h_t = w_in_t[:UA_W], w_in_t[UA_W:]
    conv_w_f = cw_g[:, 0:3].transpose(1, 0, 2).reshape(3, D_FF)

    ua = _mm(xb, w_a_t, tb=True, bias=b_in[:, :UA_W], name="fwd_in_attn")
    uh = _mm(xb, w_h_t, tb=True, bias=b_in[:, UA_W:], name="fwd_in_hgrn")
    half_up = SHARD_UP // 2
    (a_out, a_out_t), (w_o_g, w_up_half) = _attn_fwd(
        ua, ctab, stab, sinks, "attn_fwd",
        comm=_Comm([{"kind": "gather", "arr": w_o_s},
                    {"kind": "gather", "arr": w_up_t_s, "rows": (0, half_up), "dst_rows": SHARD_UP}]))
    (r_out, r_out_t, o_pre, states), (w_up_t_g, w_down_g) = _hgrn_fwd(
        uh, hgrn_lb, hgrn_norm_g, "hgrn_fwd",
        comm=_Comm([{"kind": "gather", "arr": w_up_t_s, "rows": (half_up, half_up), "dst_rows": SHARD_UP,
                     "dst_first": half_up, "into": w_up_half},
                    {"kind": "gather", "arr": w_down_s}]))
    w_down_f = w_down_g.reshape(D_FF, D_MODEL)
    w_o_f = w_o_g.reshape(D_MODEL, D_MODEL)
    w_up_t = w_up_t_g.reshape(2 * D_FF, D_MODEL)
    z1 = _mm(a_out, w_o_f[:ATTN_W], addend=x2, addend_scale=ALPHA, name="fwd_o_attn")
    h1, h1b, xhat1, rstd1, h1b_t = _mm_ln_fwd(r_out, w_o_f[ATTN_W:], z1, ln1_g, ln1_b, "fwd_o_hgrn_ln1")
    u2, hmid, dz2, d_ln2_g, d_ln2_b, loss_part = _ffn_fwd(h1b, h1, w_up_t, conv_w_f, conv_b, w_down_f, target,
                                                         ln2_g, ln2_b, "ffn_fwd")[0]

    d_w_down, d_w_down_b = _mm(hmid, dz2, ta=True, out_dtype2=BF16, tm=1408, tk=512, name="bwd_down_dw")
    (d_u2, dz1, d_conv_w8, d_conv_b, d_ln1_g, d_ln1_b), (recv_down,) = _ffn_bwd(
        dz2, u2, w_down_f, w_up_t, conv_w_f, conv_b, xhat1, rstd1, ln1_g, "ffn_bwd",
        comm=_Comm([{"kind": "exchange", "arr": d_w_down_b.reshape(N_DEV, SHARD_DOWN, D_MODEL)}]))
    d_w_up_t, d_w_up_t_b = _mm(h1b_t, d_u2, out_dtype2=BF16, tm=1024, tn=512, tk=1024, transpose_out=True,
                               name="bwd_up_dw")
    d_a = _mm(dz1, w_o_f[:ATTN_W], tb=True, name="bwd_o_dx_attn")
    d_r = _mm(dz1, w_o_f[ATTN_W:], tb=True, name="bwd_o_dx_hgrn")
    d_w_o_part = _mm(a_out_t, dz1, out_dtype2=BF16, tm=ATTN_W, out_rows=D_MODEL, name="bwd_o_dw_attn")
    d_w_o, d_w_o_b = _mm(r_out_t, dz1, out_dtype2=BF16, tm=HG_W, out_rows=D_MODEL, first_row=ATTN_W,
                         into=d_w_o_part, name="bwd_o_dw_hgrn")
    d_w_up_x = d_w_up_t_b.reshape(N_DEV, SHARD_UP, D_MODEL)
    half = SHARD_UP // 2
    d_cw_x = d_conv_w8.reshape(8, N_DEV, SHARD_IN).transpose(1, 0, 2)
    (d_ua, d_ua_t, d_bias_a, d_sinks), (recv_up_half, recv_cw) = _attn_bwd(
        ua, d_a, ctab, stab, sinks, "attn_bwd",
        comm=_Comm([{"kind": "exchange", "arr": d_w_up_x, "rows": (0, half), "dst_rows": SHARD_UP},
                    {"kind": "exchange", "arr": d_cw_x}]))
    (d_uh, d_uh_t, d_bias_h, d_norm_g, d_lb8), (recv_up, recv_o) = _hgrn_bwd(
        uh, o_pre, d_r, states, hgrn_lb, hgrn_norm_g, "hgrn_bwd",
        comm=_Comm([{"kind": "exchange", "arr": d_w_up_x, "rows": (half, half), "dst_rows": SHARD_UP,
                     "dst_first": half, "into": recv_up_half},
                    {"kind": "exchange", "arr": d_w_o_b.reshape(N_DEV, SHARD_O, D_MODEL)}]))
    d_w_in_part = _mm(d_ua_t, xb, out_dtype2=BF16, tm=UA_W, tk=t, out_rows=D_FF, name="bwd_in_dw_attn")
    d_w_in_t, d_w_in_t_b = _mm(d_uh_t, xb, out_dtype2=BF16, tm=256, tk=t, out_rows=D_FF, first_row=UA_W,
                               into=d_w_in_part, name="bwd_in_dw_hgrn")
    small_local = _pack_small({
        "ln1_g": d_ln1_g, "ln1_b": d_ln1_b, "b_in": jnp.concatenate([d_bias_a, d_bias_h], axis=1),
        "sinks": d_sinks[:, :8], "hgrn_lb": d_lb8[0:2], "hgrn_norm_g": d_norm_g, "ln2_g": d_ln2_g,
        "ln2_b": d_ln2_b, "conv_b": d_conv_b, "loss": loss_part[:, :1]})
    d_w_in_x = d_w_in_t_b.reshape(N_DEV, SHARD_IN, D_MODEL)
    res_up, (from_sibling,) = _sum_shards_adamw(
        [recv_up], _own(d_w_up_t, SHARD_UP), w_up[0].T, m_w_up[0].T, v_w_up[0].T, "adamw_w_up",
        comm=_Comm([{"kind": "pair4", "arr": d_w_in_x}]))
    res_up = [r.T for r in res_up]
    own_in, chip_part = _pair_reduce(from_sibling, d_w_in_t, "pair_reduce_w_in")
    dx, (from_chips, small_g) = _mm(d_uh, w_h_t, addend=dz1, addend_scale=ALPHA, name="bwd_in_dx_hgrn",
                                    comm=_Comm([{"kind": "chips3", "arr": chip_part},
                                                {"kind": "gather", "arr": small_local}]))
    dx = _mm(d_ua, w_a_t, addend=dx, tk=768, name="bwd_in_dx_attn")

    res_in = [r.T for r in _chip_sum_adamw(from_chips, own_in, w_in[0].T, m_w_in[0].T, v_w_in[0].T, "adamw_w_in")]
    res_o = _sum_shards_adamw([recv_o], _own(d_w_o, SHARD_O), w_o[0], m_w_o[0], v_w_o[0], "adamw_w_o")
    res_down = _sum_shards_adamw([recv_down], _own(d_w_down, SHARD_DOWN), w_down[0], m_w_down[0], v_w_down[0],
                                 "adamw_w_down")
    g_cw = _sum_slots(recv_cw, "sum_conv_w")
    cw8 = lambda a: _pad_rows(a, 8)
    res_cw = (g_cw,) + tuple(_adamw(cw8(conv_w[0]), g_cw, cw8(m_conv_w[0]), cw8(v_conv_w[0]), "adamw_conv_w"))
    big = {"w_in": [r[None] for r in res_in], "w_up": [r[None] for r in res_up],
           "w_o": [r[None] for r in res_o], "w_down": [r[None] for r in res_down],
           "conv_w": [r[None, 0:3] for r in res_cw]}

    small_sum = _sum_slots(small_g, "ar_small_sum")
    gs = _unpack_small(small_sum)
    loss = gs["loss"][0]
    zero1 = jnp.zeros((1,), F32)
    w_small = _pack_small({"ln1_g": ln1_g, "ln1_b": ln1_b, "b_in": b_in, "sinks": sinks, "hgrn_lb": hgrn_lb,
                           "hgrn_norm_g": hgrn_norm_g, "ln2_g": ln2_g, "ln2_b": ln2_b, "conv_b": conv_b,
                           "loss": zero1})
    m_small = _pack_small({"ln1_g": m_ln1_g, "ln1_b": m_ln1_b, "b_in": m_b_in, "sinks": m_sinks,
                           "hgrn_lb": m_hgrn_lb, "hgrn_norm_g": m_hgrn_norm_g, "ln2_g": m_ln2_g,
                           "ln2_b": m_ln2_b, "conv_b": m_conv_b, "loss": zero1})
    v_small = _pack_small({"ln1_g": v_ln1_g, "ln1_b": v_ln1_b, "b_in": v_b_in, "sinks": v_sinks,
                           "hgrn_lb": v_hgrn_lb, "hgrn_norm_g": v_hgrn_norm_g, "ln2_g": v_ln2_g,
                           "ln2_b": v_ln2_b, "conv_b": v_conv_b, "loss": zero1})
    small = [gs] + [_unpack_small(p) for p in _adamw(w_small, small_sum, m_small, v_small, "adamw_small")]

    order = ["ln1_g", "ln1_b", "w_in", "b_in", "sinks", "hgrn_lb", "hgrn_norm_g", "w_o", "ln2_g", "ln2_b",
             "w_up", "conv_w", "conv_b", "w_down"]

    def pick(idx):
        return [big[n][idx] if n in big else small[idx][n] for n in order]

    return (loss, dx[None], *pick(0), *pick(1), *pick(2), *pick(3))
```

```python
import functools

import jax
import jax.numpy as jnp
import numpy as np
from jax import lax
from jax.experimental import pallas as pl
from jax.experimental.pallas import tpu as pltpu

F32 = jnp.float32
BF16 = jnp.bfloat16

N_DEV = 8
D_MODEL = 1024
D_FF = 2816
ATTN_W = 512
KV_W = 128
UA_W = ATTN_W + 2 * KV_W
UH_W = 2048
HG_W = 512
ATTN_BLOCK = 128
HGRN_CHUNK = 64
HGRN_SUB = 16
HGRN_CHUNKS_PER_STEP = 4
EXP_CLAMP = 85.0
NEG_BIG = -1e30
LN_EPS = 1e-5
RMS_EPS = 1e-6
ALPHA = 2.0 ** 0.25
ATTN_SCALE = 0.125
ROPE_THETA = 500000.0

ADAM_LR = 0.001
ADAM_B1 = 0.9
ADAM_B2 = 0.999
ADAM_EPS = 1e-08
ADAM_WD = 0.01
ADAM_STEP = 10

LANES = 128
VMEM_LIMIT_BYTES = 56 * 1024 * 1024

SHARD_IN = D_FF // N_DEV
SHARD_UP = 2 * D_FF // N_DEV
SHARD_O = D_MODEL // N_DEV
SHARD_DOWN = D_FF // N_DEV
SMALL_ROWS = 88

_MESH = pl.DeviceIdType.MESH
_NT = (((1,), (1,)), ((), ()))
_NN = (((1,), (0,)), ((), ()))
_TN = (((0,), (0,)), ((), ()))


def _cp(*sem):
    if sem:
        return pltpu.CompilerParams(dimension_semantics=sem, vmem_limit_bytes=VMEM_LIMIT_BYTES)
    return pltpu.CompilerParams(vmem_limit_bytes=VMEM_LIMIT_BYTES)


def _sig(x):
    return 0.5 * jnp.tanh(0.5 * x) + 0.5


def _dsilu(x, s):
    return s * (1.0 + x * (1.0 - s))


def _dot(a, b, dims):
    return lax.dot_general(a.astype(BF16), b.astype(BF16), dims, preferred_element_type=F32)


def _split(a):
    hi = a.astype(BF16)
    return hi, (a - hi.astype(F32)).astype(BF16)


def _dot3(a, b, dims):
    ah, al = _split(a)
    bh, bl = _split(b)
    d = functools.partial(lax.dot_general, dimension_numbers=dims, preferred_element_type=F32)
    return d(ah, bh) + (d(ah, bl) + d(al, bh))


def _pick(n, pref):
    for t in pref:
        if t <= n and n % t == 0:
            return t
    return n


def _my_coords():
    return lax.axis_index("x"), lax.axis_index("y"), lax.axis_index("c")


def _peer(k):
    x, y, c = _my_coords()
    return (1 - x if k & 4 else x, 1 - y if k & 2 else y, 1 - c if k & 1 else c)


def _me():
    x, y, c = _my_coords()
    return 4 * x + 2 * y + c


class _Comm:
    def __init__(self, items):
        self.items = []
        for it in items:
            arr = it["arr"]
            full = arr.shape[0] if it["kind"] == "gather" else arr.shape[1]
            first, count = it.get("rows", (0, full))
            self.items.append(dict(kind=it["kind"], arr=arr, first=first, count=count,
                                   dst_rows=it.get("dst_rows", count), dst_first=it.get("dst_first", 0),
                                   into=it.get("into")))
        self.n = len(self.items)
        self.arrays = [it["arr"] for it in self.items]
        self.intos = [(a, it["into"]) for a, it in enumerate(self.items) if it["into"] is not None]

    def out_shapes(self):
        return [jax.ShapeDtypeStruct((4 if it["kind"] in ("pair4", "chips3") else N_DEV, it["dst_rows"],
                                      it["arr"].shape[-1]), it["arr"].dtype) for it in self.items]

    def specs(self, n=None):
        return [pl.BlockSpec(memory_space=pl.ANY)] * (self.n if n is None else n)

    def scratch(self):
        return [pltpu.SemaphoreType.DMA(((N_DEV - 1) * self.n,)), pltpu.SemaphoreType.DMA(((N_DEV - 1) * self.n,)),
                pltpu.SemaphoreType.DMA((self.n,))]

    def _src(self, a, ref, dev):
        it = self.items[a]
        blk = ref if it["kind"] == "gather" else ref.at[dev]
        return blk.at[pl.ds(it["first"], it["count"])]

    def _dst(self, a, ref, slot):
        it = self.items[a]
        return ref.at[slot].at[pl.ds(it["dst_first"], it["count"])]

    def _copy(self, a, k, src, dst, sems, me, slot):
        other = jnp.bitwise_xor(me, k)
        idx = a * (N_DEV - 1) + k - 1
        return pltpu.make_async_remote_copy(
            src_ref=self._src(a, src, other), dst_ref=self._dst(a, dst, me if slot == "mine" else other),
            send_sem=sems[0].at[idx], recv_sem=sems[1].at[idx], device_id=_peer(k), device_id_type=_MESH)

    def _pass_on(self, a, k, dst, sems, me):
        slot = self._dst(a, dst, jnp.bitwise_xor(me, k))
        idx = a * (N_DEV - 1) + k
        return pltpu.make_async_remote_copy(
            src_ref=slot, dst_ref=slot, send_sem=sems[0].at[idx], recv_sem=sems[1].at[idx],
            device_id=_peer(1), device_id_type=_MESH)

    def _part(self, a, r, src, dst, sems, me):
        it = self.items[a]
        idx = a * (N_DEV - 1) + r
        if it["kind"] == "pair4":
            k, slot = 1, jnp.bitwise_xor(jnp.bitwise_xor(me, 1), 2 * r)
        else:
            k, slot = 2 * r, r
        return pltpu.make_async_remote_copy(
            src_ref=src.at[slot].at[pl.ds(it["first"], it["count"])], dst_ref=self._dst(a, dst, r),
            send_sem=sems[0].at[idx], recv_sem=sems[1].at[idx], device_id=_peer(k), device_id_type=_MESH)

    def _parts(self, a):
        return range(4) if self.items[a]["kind"] == "pair4" else range(1, 4)

    def _local(self, a, src, dst, sems, me):
        return pltpu.make_async_copy(self._src(a, src, me), self._dst(a, dst, me), sems[2].at[a])

    def start(self, srcs, dsts, sems):
        me = _me()
        for a, (src, dst) in enumerate(zip(srcs, dsts)):
            if self.items[a]["kind"] in ("pair4", "chips3"):
                for r in self._parts(a):
                    self._part(a, r, src, dst, sems, me).start()
                continue
            direct = (1, 2, 4, 6) if self.items[a]["kind"] == "gather" else range(1, N_DEV)
            self._local(a, src, dst, sems, me).start()
            for k in direct:
                self._copy(a, k, src, dst, sems, me, "mine").start()

    def wait(self, srcs, dsts, sems):
        me = _me()
        for a, (src, dst) in enumerate(zip(srcs, dsts)):
            if self.items[a]["kind"] in ("pair4", "chips3"):
                for r in self._parts(a):
                    self._part(a, r, src, dst, sems, me).wait_recv()
                for r in self._parts(a):
                    self._part(a, r, src, dst, sems, me).wait_send()
                continue
            if self.items[a]["kind"] == "gather":
                for k in (2, 4, 6):
                    self._copy(a, k, src, dst, sems, me, "theirs").wait_recv()
                    self._pass_on(a, k, dst, sems, me).start()
                for k in (1, 3, 5, 7):
                    self._copy(a, k, src, dst, sems, me, "theirs").wait_recv()
                for k in (1, 2, 4, 6):
                    self._copy(a, k, src, dst, sems, me, "mine").wait_send()
                for k in (2, 4, 6):
                    self._pass_on(a, k, dst, sems, me).wait_send()
            else:
                for k in range(1, N_DEV):
                    self._copy(a, k, src, dst, sems, me, "theirs").wait_recv()
                for k in range(1, N_DEV):
                    self._copy(a, k, src, dst, sems, me, "mine").wait_send()
            self._local(a, src, dst, sems, me).wait()


def _call(body, *, name, grid, ins, in_specs, out_specs, out_shape, scratch_shapes=(), sem, comm=None):
    n_in, n_out, n_scr = len(ins), len(out_shape), len(scratch_shapes)
    if comm is None:
        outs = pl.pallas_call(
            body, name=name, grid=grid, in_specs=list(in_specs), out_specs=list(out_specs),
            out_shape=list(out_shape), scratch_shapes=list(scratch_shapes), compiler_params=_cp(*sem))(*ins)
        return list(outs), []
    nc, n_into = comm.n, len(comm.intos)

    def hosted(*refs):
        pos = n_in
        c_in = refs[pos:pos + nc]
        pos += nc + n_into
        outs = refs[pos:pos + n_out]
        pos += n_out
        c_out = refs[pos:pos + nc]
        pos += nc
        scr = refs[pos:pos + n_scr]
        sems = refs[pos + n_scr:]
        ids = [pl.program_id(d) for d in range(len(grid))]
        first = functools.reduce(jnp.logical_and, [i == 0 for i in ids])
        last = functools.reduce(jnp.logical_and, [i == g - 1 for i, g in zip(ids, grid)])

        @pl.when(first)
        def _():
            comm.start(c_in, c_out, sems)

        body(*refs[:n_in], *outs, *scr)

        @pl.when(last)
        def _():
            comm.wait(c_in, c_out, sems)

    aliases = {n_in + nc + j: n_out + a for j, (a, _) in enumerate(comm.intos)}
    outs = pl.pallas_call(
        hosted, name=name, grid=grid, in_specs=list(in_specs) + comm.specs() + comm.specs(n_into),
        out_specs=list(out_specs) + comm.specs(), out_shape=list(out_shape) + comm.out_shapes(),
        scratch_shapes=list(scratch_shapes) + comm.scratch(), input_output_aliases=aliases,
        compiler_params=_cp(*(["arbitrary"] * len(grid))))(*ins, *comm.arrays, *[arr for _, arr in comm.intos])
    return list(outs[:n_out]), list(outs[n_out:])


def _sum_slots(gathered, name):
    _, rows, cols = gathered.shape

    def body(g_ref, out_ref):
        acc = g_ref[0]
        for s in range(1, N_DEV):
            acc = acc + g_ref[s]
        out_ref[...] = acc

    return pl.pallas_call(
        body, name=name,
        out_shape=jax.ShapeDtypeStruct((rows, cols), F32),
        compiler_params=_cp(),
    )(gathered)


def _slot_sum(recv_ref, own_ref, shape):
    me = _me()
    acc = jnp.zeros(shape, F32)
    for s in range(N_DEV):
        acc = acc + jnp.where(me == s, own_ref[...], recv_ref[s].astype(F32))
    return acc


def _adamw_math(w, g, m, v):
    nm = ADAM_B1 * m + (1.0 - ADAM_B1) * g
    nv = ADAM_B2 * v + (1.0 - ADAM_B2) * (g * g)
    m_hat = nm / (1.0 - ADAM_B1 ** ADAM_STEP)
    v_hat = nv / (1.0 - ADAM_B2 ** ADAM_STEP)
    return -ADAM_LR * (m_hat / (jnp.sqrt(v_hat) + ADAM_EPS) + ADAM_WD * w), nm, nv


def _pair_reduce(from_sibling, mine, name):
    _, rows, cols = from_sibling.shape
    tr = _pick(rows, (176, 128, 64, 32, 16, 8))
    tiles = rows // tr
    table = jnp.bitwise_xor(_me(), jnp.arange(0, N_DEV, 2, dtype=jnp.int32))

    def body(tbl_ref, sib_ref, mine_ref, own_ref, send_ref):
        r = pl.program_id(1)
        total = mine_ref[...] + sib_ref[0].astype(F32)
        send_ref[0] = jnp.where(r == 0, 0.0, total).astype(BF16)

        @pl.when(r == 0)
        def _():
            own_ref[...] = total

    grid_spec = pltpu.PrefetchScalarGridSpec(
        num_scalar_prefetch=1, grid=(tiles, 4),
        in_specs=[pl.BlockSpec((1, tr, cols), lambda i, r, tbl: (r, i, 0)),
                  pl.BlockSpec((tr, cols), lambda i, r, tbl: (tbl[r] * tiles + i, 0))],
        out_specs=[pl.BlockSpec((tr, cols), lambda i, r, tbl: (i, 0)),
                   pl.BlockSpec((1, tr, cols), lambda i, r, tbl: (r, i, 0))])
    return pl.pallas_call(
        body, name=name, grid_spec=grid_spec,
        out_shape=[jax.ShapeDtypeStruct((rows, cols), F32), jax.ShapeDtypeStruct((4, rows, cols), BF16)],
        compiler_params=_cp("arbitrary", "arbitrary"),
    )(table, from_sibling, mine)


def _chip_sum_adamw(from_chips, own, w, m, v, name):
    _, rows, cols = from_chips.shape
    tr = _pick(rows, (176, 128, 64, 32, 16, 8))

    def body(recv_ref, own_ref, w_ref, m_ref, v_ref, g_ref, d_ref, nm_ref, nv_ref):
        g = own_ref[...]
        for r in range(1, 4):
            g = g + recv_ref[r].astype(F32)
        g_ref[...] = g
        d_ref[...], nm_ref[...], nv_ref[...] = _adamw_math(w_ref[...], g, m_ref[...], v_ref[...])

    spec = pl.BlockSpec((tr, cols), lambda i: (i, 0))
    shp = jax.ShapeDtypeStruct((rows, cols), F32)
    return pl.pallas_call(
        body, name=name, grid=(rows // tr,),
        in_specs=[pl.BlockSpec((4, tr, cols), lambda i: (0, i, 0)), spec, spec, spec, spec],
        out_specs=[spec, spec, spec, spec], out_shape=[shp, shp, shp, shp],
        compiler_params=_cp("parallel"),
    )(from_chips, own, w, m, v)


def _sum_shards_adamw(recvs, own, w, m, v, name, comm=None):
    rows_p, cols = recvs[0].shape[1], recvs[0].shape[2]
    n_p = len(recvs)
    tr = _pick(rows_p, (176, 128, 64, 32, 16, 8))
    tiles = rows_p // tr

    def body(*refs):
        recv_refs = refs[:n_p]
        own_ref, w_ref, m_ref, v_ref, g_ref, d_ref, nm_ref, nv_ref = refs[n_p:]
        for j in range(n_p):
            @pl.when(pl.program_id(0) == j)
            def _():
                g = _slot_sum(recv_refs[j], own_ref, (tr, cols))
                g_ref[...] = g
                d_ref[...], nm_ref[...], nv_ref[...] = _adamw_math(w_ref[...], g, m_ref[...], v_ref[...])

    spec = pl.BlockSpec((tr, cols), lambda p_, i: (p_ * tiles + i, 0))
    shp = jax.ShapeDtypeStruct((rows_p * n_p, cols), F32)
    outs, couts = _call(
        body, name=name, grid=(n_p, tiles), ins=[*recvs, own, w, m, v],
        in_specs=[pl.BlockSpec((N_DEV, tr, cols), functools.partial(lambda p_, i, j: (0, jnp.where(p_ == j, i, 0), 0), j=j))
                  for j in range(n_p)] + [spec, spec, spec, spec],
        out_specs=[spec, spec, spec, spec], out_shape=[shp, shp, shp, shp],
        sem=("arbitrary", "arbitrary"), comm=comm)
    return outs if comm is None else (outs, couts)


def _mm(a, b, *, name, ta=False, tb=False, out_dtype=F32, out_dtype2=None, bias=None, addend=None,
        addend_scale=1.0, tm=1024, tn=1024, tk=1024, comm=None, out_rows=None, first_row=0, into=None):
    kdim, m = a.shape if ta else a.shape[::-1]
    n = b.shape[0] if tb else b.shape[1]
    tm = _pick(m, (tm, 1408, 1024, 768, 512, 256, 128))
    tn = _pick(n, (tn, 1408, 1024, 768, 512, 256, 128))
    tk = _pick(kdim, (tk, 1408, 1024, 768, 512, 256, 128))
    nk = kdim // tk
    a_spec = pl.BlockSpec((tk, tm), lambda i, j, k: (k, i)) if ta else pl.BlockSpec((tm, tk), lambda i, j, k: (i, k))
    b_spec = pl.BlockSpec((tn, tk), lambda i, j, k: (j, k)) if tb else pl.BlockSpec((tk, tn), lambda i, j, k: (k, j))
    ins, specs = [a, b], [a_spec, b_spec]
    if bias is not None:
        ins.append(bias)
        specs.append(pl.BlockSpec((1, tn), lambda i, j, k: (0, j)))
    if addend is not None:
        ins.append(addend)
        specs.append(pl.BlockSpec((tm, tn), lambda i, j, k: (i, j)))
    dims = (((0,) if ta else (1,), (1,) if tb else (0,)), ((), ()))
    has_bias, has_addend, two = bias is not None, addend is not None, out_dtype2 is not None

    def body(*refs):
        a_ref, b_ref = refs[0], refs[1]
        pos = 2
        bias_ref = addend_ref = None
        if has_bias:
            bias_ref = refs[pos]
            pos += 1
        if has_addend:
            addend_ref = refs[pos]
            pos += 1
        o_refs, acc_ref = refs[pos:-1], refs[-1]
        k = pl.program_id(2)

        @pl.when(k == 0)
        def _():
            acc_ref[...] = jnp.zeros_like(acc_ref)

        acc_ref[...] += _dot(a_ref[...], b_ref[...], dims)

        @pl.when(k == nk - 1)
        def _():
            r = acc_ref[...]
            if has_bias:
                r = r + bias_ref[...]
            if has_addend:
                r = r + addend_scale * addend_ref[...].astype(F32)
            for o_ref in o_refs:
                o_ref[...] = r.astype(o_ref.dtype)

    blk0 = first_row // tm
    dtypes = [out_dtype] + ([out_dtype2] if two else [])
    ospec = pl.BlockSpec((tm, tn), lambda i, j, k: (i + blk0, j))
    shapes = [jax.ShapeDtypeStruct((m if out_rows is None else out_rows, n), d) for d in dtypes]
    if into is not None:
        n_in = len(ins)
        outs = pl.pallas_call(
            lambda *refs: body(*refs[:n_in], *refs[n_in + len(into):]), name=name, grid=(m // tm, n // tn, nk),
            in_specs=specs + [pl.BlockSpec(memory_space=pl.ANY)] * len(into), out_specs=[ospec] * len(dtypes),
            out_shape=shapes, scratch_shapes=[pltpu.VMEM((tm, tn), F32)],
            input_output_aliases={n_in + j: j for j in range(len(into))},
            compiler_params=_cp("parallel", "parallel", "arbitrary"))(*ins, *into)
        return tuple(outs) if two else outs[0]
    outs, couts = _call(
        body, name=name, grid=(m // tm, n // tn, nk), ins=ins, in_specs=specs,
        out_specs=[ospec] * len(dtypes), out_shape=shapes,
        scratch_shapes=[pltpu.VMEM((tm, tn), F32)], sem=("parallel", "parallel", "arbitrary"), comm=comm)
    primary = tuple(outs) if two else outs[0]
    return (primary, couts) if comm is not None else primary


def _rope_lane_constants():
    inv_freq = np.float32(ROPE_THETA) ** (-np.arange(8, dtype=np.float32) * np.float32(2.0 / 16.0))
    lane = np.arange(LANES) % 64
    freq = np.where(lane < 16, inv_freq[lane % 8], 0.0).astype(np.float32)
    sign = np.where(lane < 8, -1.0, np.where(lane < 16, 1.0, 0.0)).astype(np.float32)
    return jnp.asarray(freq)[None, :], jnp.asarray(sign)[None, :]


def _prep(pos_col, x2, name, comm):
    t, d = x2.shape
    tr = _pick(t, (512, 256, 128))
    freq, sign = _rope_lane_constants()

    def body(pos_ref, freq_ref, sign_ref, x_ref, c_ref, s_ref, xb_ref):
        ang = pos_ref[...].astype(F32) * freq_ref[...]
        c_ref[...] = jnp.cos(ang)
        s_ref[...] = sign_ref[...] * jnp.sin(ang)
        xb_ref[...] = x_ref[...].astype(BF16)

    tab = pl.BlockSpec((tr, LANES), lambda i: (i, 0))
    return _call(
        body, name=name, grid=(t // tr,), ins=[pos_col, freq, sign, x2],
        in_specs=[pl.BlockSpec((tr, 1), lambda i: (i, 0)), pl.BlockSpec((1, LANES), lambda i: (0, 0)),
                  pl.BlockSpec((1, LANES), lambda i: (0, 0)), pl.BlockSpec((tr, d), lambda i: (i, 0))],
        out_specs=[tab, tab, pl.BlockSpec((tr, d), lambda i: (i, 0))],
        out_shape=[jax.ShapeDtypeStruct((t, LANES), F32), jax.ShapeDtypeStruct((t, LANES), F32),
                   jax.ShapeDtypeStruct((t, d), BF16)],
        sem=("parallel",), comm=comm)


def _swap8(t):
    width = t.shape[1]
    lane = jnp.bitwise_and(lax.broadcasted_iota(jnp.int32, t.shape, 1), 63)
    return jnp.where(lane < 8, pltpu.roll(t, width - 8, 1), jnp.where(lane < 16, pltpu.roll(t, 8, 1), 0.0))


def _rope(t, c, s):
    return t * c + _swap8(t) * s


def _rope_bwd(d, c, s):
    return d * c + _swap8(d * s)


def _tile4(a):
    return jnp.concatenate([a, a, a, a], axis=1)


def _attn_band(n, k_cur, k_prev, v_cur, v_prev, c_cur, s_cur, c_prev, s_prev):
    kband = jnp.concatenate([_rope(k_prev, c_prev, s_prev), _rope(k_cur, c_cur, s_cur)], axis=0)
    vband = jnp.concatenate([v_prev, v_cur], axis=0)
    qi = lax.broadcasted_iota(jnp.int32, (ATTN_BLOCK, 2 * ATTN_BLOCK), 0)
    kj = lax.broadcasted_iota(jnp.int32, (ATTN_BLOCK, 2 * ATTN_BLOCK), 1)
    dist = qi + ATTN_BLOCK - kj
    valid = (dist >= 0) & (dist < ATTN_BLOCK) & (n * ATTN_BLOCK - ATTN_BLOCK + kj >= 0)
    return (kband.astype(BF16), pltpu.roll(kband, 64, 1).astype(BF16),
            vband.astype(BF16), pltpu.roll(vband, 64, 1).astype(BF16), valid, kband)


def _attn_probs(raw, valid, sink, axis):
    s = jnp.where(valid, raw * ATTN_SCALE, NEG_BIG)
    m = jnp.maximum(jnp.max(s, axis=axis, keepdims=True), sink)
    p = jnp.exp(s - m)
    esink = jnp.exp(sink - m)
    z = jnp.sum(p, axis=axis, keepdims=True) + esink
    return p / z, esink / z


def _attn_valid_t(n):
    kj = lax.broadcasted_iota(jnp.int32, (2 * ATTN_BLOCK, ATTN_BLOCK), 0)
    qi = lax.broadcasted_iota(jnp.int32, (2 * ATTN_BLOCK, ATTN_BLOCK), 1)
    dist = qi + ATTN_BLOCK - kj
    return (dist >= 0) & (dist < ATTN_BLOCK) & (n * ATTN_BLOCK - ATTN_BLOCK + kj >= 0)


def _attn_specs(nb):
    def cur(col, width=KV_W):
        return pl.BlockSpec((ATTN_BLOCK, width), lambda n: (jnp.minimum(n, nb - 1), col))

    def prev(col):
        return pl.BlockSpec((ATTN_BLOCK, KV_W), lambda n: (jnp.maximum(n - 1, 0), col))

    ua_specs = [cur(0, ATTN_W), cur(4), prev(4), cur(5), prev(5)]
    tab_specs = [cur(0), cur(0), prev(0), prev(0)]
    return ua_specs, tab_specs


def _attn_fwd(ua, ctab, stab, sinks, name, comm=None):
    t = ua.shape[0]
    nb = t // ATTN_BLOCK
    ua_specs, tab_specs = _attn_specs(nb)

    def body(q_ref, kc_ref, kp_ref, vc_ref, vp_ref, cc_ref, sc_ref, cp_ref, sp_ref, sink_ref, o_ref, o_t_ref):
        n = pl.program_id(0)
        cc, sc = cc_ref[...], sc_ref[...]
        kb, kb_r, vb, vb_r, valid, _ = _attn_band(n, kc_ref[...], kp_ref[...], vc_ref[...], vp_ref[...],
                                                  cc, sc, cp_ref[...], sp_ref[...])
        qr = _rope(q_ref[...], _tile4(cc), _tile4(sc))
        lo = lax.broadcasted_iota(jnp.int32, (ATTN_BLOCK, LANES), 1) < 64
        heads = []
        for j in range(4):
            qj = qr[:, j * LANES:(j + 1) * LANES]
            for is_lo in (True, False):
                aligned = is_lo == (j < 2)
                qm = jnp.where(lo if is_lo else jnp.logical_not(lo), qj, 0.0).astype(BF16)
                raw = lax.dot_general(qm, kb if aligned else kb_r, _NT, preferred_element_type=F32)
                heads.append((raw, vb if aligned else vb_r, sink_ref[0, len(heads)]))
        halves = []
        for raw, vv, sink in heads:
            probs, _ = _attn_probs(raw, valid, sink, 1)
            halves.append(lax.dot_general(probs.astype(BF16), vv, _NN, preferred_element_type=F32))
        outs = [jnp.where(lo, halves[2 * j], halves[2 * j + 1]) for j in range(4)]
        o_ref[...] = jnp.concatenate(outs, axis=1).astype(o_ref.dtype)
        for j in range(4):
            o_t_ref[j * LANES:(j + 1) * LANES, :] = outs[j].T.astype(o_t_ref.dtype)

    return _call(
        body, name=name, grid=(nb,), ins=[ua, ua, ua, ua, ua, ctab, stab, ctab, stab, sinks],
        in_specs=ua_specs + tab_specs + [pl.BlockSpec(memory_space=pltpu.SMEM)],
        out_specs=[pl.BlockSpec((ATTN_BLOCK, ATTN_W), lambda n: (n, 0)),
                   pl.BlockSpec((ATTN_W, ATTN_BLOCK), lambda n: (0, n))],
        out_shape=[jax.ShapeDtypeStruct((t, ATTN_W), BF16), jax.ShapeDtypeStruct((ATTN_W, t), BF16)],
        sem=("parallel",), comm=comm)


def _attn_bwd(ua, d_out, ctab, stab, sinks, name, comm=None):
    t = ua.shape[0]
    nb = t // ATTN_BLOCK
    ua_specs, tab_specs = _attn_specs(nb)

    def body(q_ref, kc_ref, kp_ref, vc_ref, vp_ref, cc_ref, sc_ref, cp_ref, sp_ref, do_ref, sink_ref,
             dua_ref, dua_t_ref, dbias_ref, dsink_ref, dq_c, dk_c, dv_c, dq_n, dk_n, dv_n):
        n = pl.program_id(0)

        @pl.when(n == 0)
        def _():
            dq_c[...] = jnp.zeros_like(dq_c)
            dk_c[...] = jnp.zeros_like(dk_c)
            dv_c[...] = jnp.zeros_like(dv_c)
            dbias_ref[...] = jnp.zeros_like(dbias_ref)
            dsink_ref[...] = jnp.zeros_like(dsink_ref)

        @pl.when(n == nb)
        def _():
            dq_n[...] = jnp.zeros_like(dq_n)
            dk_n[...] = jnp.zeros_like(dk_n)
            dv_n[...] = jnp.zeros_like(dv_n)

        @pl.when(n < nb)
        def _():
            cc, sc = cc_ref[...], sc_ref[...]
            kb, kb_r, vb, vb_r, _, kb_f32 = _attn_band(n, kc_ref[...], kp_ref[...], vc_ref[...], vp_ref[...],
                                                       cc, sc, cp_ref[...], sp_ref[...])
            valid_t = _attn_valid_t(n)
            c4, s4 = _tile4(cc), _tile4(sc)
            qr = _rope(q_ref[...], c4, s4)
            do = do_ref[...].astype(F32)
            lane = lax.broadcasted_iota(jnp.int32, (ATTN_BLOCK, LANES), 1)
            lo = lane < 64
            lane_row = lax.broadcasted_iota(jnp.int32, (1, LANES), 1)
            k_t = {False: kb_f32.T.astype(BF16), True: pltpu.roll(kb_f32, 64, 1).T.astype(BF16)}
            heads = []
            for j in range(4):
                qj = qr[:, j * LANES:(j + 1) * LANES]
                doj = do[:, j * LANES:(j + 1) * LANES]
                for is_lo in (True, False):
                    aligned = is_lo == (j < 2)
                    msk = lo if is_lo else jnp.logical_not(lo)
                    kk = kb if aligned else kb_r
                    vv = vb if aligned else vb_r
                    qm = jnp.where(msk, qj, 0.0).astype(BF16)
                    dom = jnp.where(msk, doj, 0.0).astype(BF16)
                    heads.append(dict(
                        aligned=aligned, qm=qm, dom=dom, sink=sink_ref[0, len(heads)],
                        raw_t=lax.dot_general(kk, qm, _NT, preferred_element_type=F32),
                        dp_t=lax.dot_general(vv, dom, _NT, preferred_element_type=F32)))
            dk_band = jnp.zeros((2 * ATTN_BLOCK, LANES), F32)
            dv_band = jnp.zeros((2 * ATTN_BLOCK, LANES), F32)
            dsink = jnp.zeros((1, LANES), F32)
            for head, hd in enumerate(heads):
                probs_t, psink = _attn_probs(hd["raw_t"], valid_t, hd["sink"], 0)
                delta_t = jnp.sum(probs_t * hd["dp_t"], axis=0, keepdims=True)
                hd["ds_t"] = (probs_t * (hd["dp_t"] - delta_t) * ATTN_SCALE).astype(BF16)
                dsink = dsink + jnp.where(lane_row == head, -jnp.sum(psink * delta_t), 0.0)
                dk_h = lax.dot_general(hd["ds_t"], hd["qm"], _NN, preferred_element_type=F32)
                dv_h = lax.dot_general(probs_t.astype(BF16), hd["dom"], _NN, preferred_element_type=F32)
                if not hd["aligned"]:
                    dk_h = pltpu.roll(dk_h, 64, 1)
                    dv_h = pltpu.roll(dv_h, 64, 1)
                dk_band = dk_band + dk_h
                dv_band = dv_band + dv_h
            row_lo = lax.broadcasted_iota(jnp.int32, (LANES, ATTN_BLOCK), 0) < 64
            dq_t = [lax.dot_general(k_t[not hd["aligned"]], hd["ds_t"], _NN, preferred_element_type=F32)
                    for hd in heads]
            dqs = [jnp.where(row_lo, dq_t[2 * j], dq_t[2 * j + 1]).T for j in range(4)]
            dq_n[...] = _rope_bwd(jnp.concatenate(dqs, axis=1), c4, s4)
            dk_n[...] = dk_band
            dv_n[...] = dv_band
            dsink_ref[...] += dsink

        dk_prev = _rope_bwd(dk_c[...] + dk_n[0:ATTN_BLOCK, :], cp_ref[...], sp_ref[...])
        dv_prev = dv_c[...] + dv_n[0:ATTN_BLOCK, :]
        full = jnp.concatenate([dq_c[...], dk_prev, dv_prev], axis=1)
        dua_ref[...] = full.astype(dua_ref.dtype)
        for j in range(UA_W // LANES):
            dua_t_ref[j * LANES:(j + 1) * LANES, :] = full[:, j * LANES:(j + 1) * LANES].T.astype(dua_t_ref.dtype)
        dbias_ref[...] += jnp.sum(full, axis=0, keepdims=True)
        dq_c[...] = dq_n[...]
        dk_c[...] = dk_n[ATTN_BLOCK:, :]
        dv_c[...] = dv_n[ATTN_BLOCK:, :]

    return _call(
        body, name=name, grid=(nb + 1,), ins=[ua, ua, ua, ua, ua, ctab, stab, ctab, stab, d_out, sinks],
        in_specs=ua_specs + tab_specs + [
            pl.BlockSpec((ATTN_BLOCK, ATTN_W), lambda n: (jnp.minimum(n, nb - 1), 0)),
            pl.BlockSpec(memory_space=pltpu.SMEM)],
        out_specs=[pl.BlockSpec((ATTN_BLOCK, UA_W), lambda n: (jnp.maximum(n - 1, 0), 0)),
                   pl.BlockSpec((UA_W, ATTN_BLOCK), lambda n: (0, jnp.maximum(n - 1, 0))),
                   pl.BlockSpec((1, UA_W), lambda n: (0, 0)),
                   pl.BlockSpec((1, LANES), lambda n: (0, 0))],
        out_shape=[jax.ShapeDtypeStruct((t, UA_W), BF16), jax.ShapeDtypeStruct((UA_W, t), BF16),
                   jax.ShapeDtypeStruct((1, UA_W), F32),
                   jax.ShapeDtypeStruct((1, LANES), F32)],
        scratch_shapes=[pltpu.VMEM((ATTN_BLOCK, ATTN_W), F32), pltpu.VMEM((ATTN_BLOCK, KV_W), F32),
                        pltpu.VMEM((ATTN_BLOCK, KV_W), F32), pltpu.VMEM((ATTN_BLOCK, ATTN_W), F32),
                        pltpu.VMEM((2 * ATTN_BLOCK, KV_W), F32), pltpu.VMEM((2 * ATTN_BLOCK, KV_W), F32)],
        sem=("arbitrary",), comm=comm)


def _tri_mats():
    r = lax.broadcasted_iota(jnp.int32, (HGRN_CHUNK, LANES), 0)
    c = lax.broadcasted_iota(jnp.int32, (HGRN_CHUNK, LANES), 1)
    lower = ((c <= r) & (c < HGRN_CHUNK)).astype(F32)
    upper = ((c >= r) & (c < HGRN_CHUNK)).astype(F32)
    return lower, upper


def _tri_apply(tri, g):
    pad = jnp.concatenate([g, jnp.zeros_like(g)], axis=0)
    return lax.dot_general(tri, pad, _NN, precision=lax.Precision.HIGHEST, preferred_element_type=F32)


def _sub_masks():
    s = lax.broadcasted_iota(jnp.int32, (HGRN_CHUNK, LANES), 0)
    tt = lax.broadcasted_iota(jnp.int32, (HGRN_CHUNK, LANES), 1)
    return [(tt >= HGRN_SUB * i) & (tt < HGRN_SUB * (i + 1)) & (s <= tt) for i in range(HGRN_CHUNK // HGRN_SUB)]


def _hgrn_gates(hq, hf, lb_ref, b_scr):
    lb = _sig(lb_ref[0:1, :] - lb_ref[1:2, :])
    q = hq * _sig(hq)
    sg = _sig(hf)
    f = lb + (1.0 - lb) * sg
    k = 1.0 - f
    lower, _ = _tri_mats()
    b = _tri_apply(lower, jnp.log(f))
    b_scr[...] = b
    nsub = HGRN_CHUNK // HGRN_SUB
    starts = [jnp.zeros((1, HG_W), F32)] + [b_scr[HGRN_SUB * i - 1:HGRN_SUB * i, :] for i in range(1, nsub)]
    pq = jnp.concatenate([jnp.broadcast_to(p, (HGRN_SUB, HG_W)) for p in starts], axis=0)
    b_last = b_scr[HGRN_CHUNK - 1:HGRN_CHUNK, :]
    e_q = jnp.exp(b - pq)
    e_k = [jnp.exp(jnp.minimum(p - b, EXP_CLAMP)) for p in starts]
    e_b = jnp.exp(b)
    e_bl = jnp.exp(b_last - b)
    e_last = jnp.exp(b_last)
    return q, sg, f, k, lb, e_q, e_k, e_b, e_bl, e_last


def _sub_masks_ts():
    tt = lax.broadcasted_iota(jnp.int32, (HGRN_CHUNK, LANES), 0)
    s = lax.broadcasted_iota(jnp.int32, (HGRN_CHUNK, LANES), 1)
    return [(tt >= HGRN_SUB * i) & (tt < HGRN_SUB * (i + 1)) & (s <= tt) for i in range(HGRN_CHUNK // HGRN_SUB)]


def _masked_sum(blocks, masks, axis):
    step = HGRN_CHUNK if axis == 0 else LANES
    acc = jnp.zeros((HGRN_CHUNK, LANES), F32)
    for i, msk in enumerate(masks):
        blk = blocks[step * i:step * (i + 1), :] if axis == 0 else blocks[:, step * i:step * (i + 1)]
        acc = acc + jnp.where(msk, blk, 0.0)
    return acc


def _store_transposed(out_t_ref, chunk_rows):
    width = chunk_rows[0].shape[1]
    if len(chunk_rows) == 1:
        groups = [jnp.concatenate([chunk_rows[0], jnp.zeros_like(chunk_rows[0])], axis=0)]
    else:
        groups = [jnp.concatenate(chunk_rows[g:g + 2], axis=0) for g in range(0, len(chunk_rows), 2)]
    for g, rows in enumerate(groups):
        for c in range(width // LANES):
            tile = rows[:, c * LANES:(c + 1) * LANES].T.astype(out_t_ref.dtype)
            if len(chunk_rows) == 1:
                out_t_ref[c * LANES:(c + 1) * LANES, :] = tile[:, 0:HGRN_CHUNK]
            else:
                out_t_ref[c * LANES:(c + 1) * LANES, g * LANES:(g + 1) * LANES] = tile


def _hgrn_chunk_inputs(j, hq_ref, hf_ref, hi_ref, hg_ref, lb_ref, b_scr):
    rows = slice(j * HGRN_CHUNK, (j + 1) * HGRN_CHUNK)
    hq, hf, v, hg = hq_ref[rows, :], hf_ref[rows, :], hi_ref[rows, :], hg_ref[rows, :]
    q, sg, f, k, lb, e_q, e_k, e_b, e_bl, e_last = _hgrn_gates(hq, hf, lb_ref, b_scr.at[j])
    return dict(rows=rows, hq=hq, v=v, hg=hg, q=q, sg=sg, f=f, k=k, lb=lb, e_q=e_q, e_k=e_k, e_b=e_b, e_bl=e_bl,
                e_last=e_last, qt=q * e_q, qb=q * e_b, kd=k * e_bl, khat=[k * e for e in e_k])


def _hgrn_fwd(uh, lb_raw, norm_g, name, comm=None):
    t = uh.shape[0]
    nc = t // HGRN_CHUNK
    cps = _pick(nc, (HGRN_CHUNKS_PER_STEP, 2, 1))
    rows_step = cps * HGRN_CHUNK

    def body(hq_ref, hf_ref, hi_ref, hg_ref, lb_ref, ng_ref, r_ref, r_t_ref, o_ref, st_out_ref, st_ref, b_scr):
        @pl.when(pl.program_id(0) == 0)
        def _():
            st_ref[...] = jnp.zeros_like(st_ref)

        masks = _sub_masks_ts()
        ng = ng_ref[...]
        zpad = jnp.zeros((HGRN_CHUNK, LANES), F32)
        heads = [slice(h * LANES, (h + 1) * LANES) for h in range(4)]
        chunks = [_hgrn_chunk_inputs(j, hq_ref, hf_ref, hi_ref, hg_ref, lb_ref, b_scr) for j in range(cps)]
        for ch in chunks:
            ch["scores"] = [_dot3(ch["qt"][:, sl],
                                  jnp.concatenate([x for kh in ch["khat"] for x in (kh[:, sl], zpad)], axis=0), _NT)
                            for sl in heads]
        for j, ch in enumerate(chunks):
            o_heads, y_heads = [], []
            for h, sl in enumerate(heads):
                a_ts = _masked_sum(ch["scores"][h], masks, 1)
                vh = ch["v"][:, sl].astype(BF16)
                v_pad = jnp.concatenate([vh, jnp.zeros_like(vh)], axis=0)
                o_intra = lax.dot_general(a_ts.astype(BF16), v_pad, _NN, preferred_element_type=F32)
                st = st_ref[h]
                st_out_ref[j, h] = st
                o_inter = _dot(ch["qb"][:, sl], st, _NT)
                st_ref[h] = st * ch["e_last"][:, sl] + _dot(vh, ch["kd"][:, sl], _TN)
                oh = o_intra + o_inter
                rs = lax.rsqrt(jnp.mean(oh * oh, axis=1, keepdims=True) + RMS_EPS)
                o_heads.append(oh)
                y_heads.append(oh * rs * ng)
            hg = ch["hg"]
            o_ref[ch["rows"], :] = jnp.concatenate(o_heads, axis=1)
            ch["r"] = jnp.concatenate(y_heads, axis=1) * (hg * _sig(hg))
            r_ref[ch["rows"], :] = ch["r"].astype(r_ref.dtype)
        _store_transposed(r_t_ref, [ch["r"] for ch in chunks])

    col = lambda j: pl.BlockSpec((rows_step, HG_W), lambda c: (c, j))
    return _call(
        body, name=name, grid=(nc // cps,), ins=[uh, uh, uh, uh, lb_raw, norm_g],
        in_specs=[col(0), col(1), col(2), col(3),
                  pl.BlockSpec((2, HG_W), lambda c: (0, 0)), pl.BlockSpec((1, LANES), lambda c: (0, 0))],
        out_specs=[pl.BlockSpec((rows_step, HG_W), lambda c: (c, 0)),
                   pl.BlockSpec((HG_W, rows_step), lambda c: (0, c)),
                   pl.BlockSpec((rows_step, HG_W), lambda c: (c, 0)),
                   pl.BlockSpec((cps, 4, LANES, LANES), lambda c: (c, 0, 0, 0))],
        out_shape=[jax.ShapeDtypeStruct((t, HG_W), BF16), jax.ShapeDtypeStruct((HG_W, t), BF16),
                   jax.ShapeDtypeStruct((t, HG_W), F32), jax.ShapeDtypeStruct((nc, 4, LANES, LANES), F32)],
        scratch_shapes=[pltpu.VMEM((4, LANES, LANES), F32), pltpu.VMEM((cps, HGRN_CHUNK, HG_W), F32)],
        sem=("arbitrary",), comm=comm)


def _hgrn_bwd(uh, o_pre, d_r, states, lb_raw, norm_g, name, comm=None):
    t = uh.shape[0]
    nc = t // HGRN_CHUNK
    cps = _pick(nc, (HGRN_CHUNKS_PER_STEP, 2, 1))
    ns = nc // cps
    rows_step = cps * HGRN_CHUNK
    nsub = HGRN_CHUNK // HGRN_SUB

    def body(hq_ref, hf_ref, hi_ref, hg_ref, o_ref, dr_ref, st_in_ref, lb_ref, ng_ref,
             duh_ref, duh_t_ref, dbias_ref, dng_ref, dlb_ref, dst_ref, b_scr, dlb_acc):
        i = pl.program_id(0)

        @pl.when(i == 0)
        def _():
            dst_ref[...] = jnp.zeros_like(dst_ref)
            dbias_ref[...] = jnp.zeros_like(dbias_ref)
            dng_ref[...] = jnp.zeros_like(dng_ref)
            dlb_acc[...] = jnp.zeros_like(dlb_acc)

        masks_st = _sub_masks()
        masks_ts = _sub_masks_ts()
        ng = ng_ref[...]
        zpad = jnp.zeros((HGRN_CHUNK, LANES), F32)
        _, upper = _tri_mats()
        heads = [slice(h * LANES, (h + 1) * LANES) for h in range(4)]
        row = lax.broadcasted_iota(jnp.int32, (HGRN_CHUNK, HG_W), 0)

        chunks = [_hgrn_chunk_inputs(j, hq_ref, hf_ref, hi_ref, hg_ref, lb_ref, b_scr) for j in range(cps)]
        dng = jnp.zeros((1, LANES), F32)
        for ch in chunks:
            o = o_ref[ch["rows"], :]
            dr = dr_ref[ch["rows"], :].astype(F32)
            hg = ch["hg"]
            sgg = _sig(hg)
            dy = dr * (hg * sgg)
            do_h, y_h = [], []
            for sl in heads:
                oh = o[:, sl]
                rs = lax.rsqrt(jnp.mean(oh * oh, axis=1, keepdims=True) + RMS_EPS)
                y_h.append(oh * rs * ng)
                dng = dng + jnp.sum(dy[:, sl] * oh * rs, axis=0, keepdims=True)
                w = dy[:, sl] * ng
                do_h.append(rs * (w - oh * (rs * rs) * jnp.mean(w * oh, axis=1, keepdims=True)))
            ch["do"] = do_h
            ch["dhg"] = dr * jnp.concatenate(y_h, axis=1) * _dsilu(hg, sgg)

        for ch in chunks:
            ch["kst"], ch["kpad"], ch["qt_pad"], ch["v_b"], ch["do_pad"] = [], [], [], [], []
            ch["ats"], ch["d_at"], ch["d_a"] = [], [], []
            for h, sl in enumerate(heads):
                kst = jnp.concatenate([kh[:, sl] for kh in ch["khat"]], axis=0)
                kpad = jnp.concatenate([x for kh in ch["khat"] for x in (kh[:, sl], zpad)], axis=0)
                qt_pad = jnp.concatenate([ch["qt"][:, sl], zpad], axis=0)
                vh = ch["v"][:, sl].astype(BF16)
                v_pad = jnp.concatenate([vh, jnp.zeros_like(vh)], axis=0)
                do_b = ch["do"][h].astype(BF16)
                do_pad = jnp.concatenate([do_b, jnp.zeros_like(do_b)], axis=0)
                ch["kst"].append(kst)
                ch["kpad"].append(kpad)
                ch["qt_pad"].append(qt_pad)
                ch["v_b"].append(vh)
                ch["do_pad"].append(do_pad)
                ch["ats"].append(_dot3(kst, qt_pad, _NT))
                ch["d_at"].append(lax.dot_general(vh, do_pad, _NT, preferred_element_type=F32))
                ch["d_a"].append(lax.dot_general(do_b, v_pad, _NT, preferred_element_type=F32))

        for ch in chunks:
            ch["d_kst"], ch["d_qt"], ch["dv"] = [], [], []
            for h in range(4):
                at = _masked_sum(ch["ats"][h], masks_st, 0)
                d_ats = jnp.concatenate([jnp.where(m, ch["d_at"][h], 0.0) for m in masks_st], axis=0)
                d_a_cat = jnp.concatenate([jnp.where(m, ch["d_a"][h], 0.0) for m in masks_ts], axis=1)
                ch["d_kst"].append(_dot3(d_ats, ch["qt_pad"][h], _NN))
                ch["d_qt"].append(_dot3(d_a_cat, ch["kpad"][h], _NN))
                ch["dv"].append(lax.dot_general(at.astype(BF16), ch["do_pad"][h], _NN, preferred_element_type=F32))

        for j in reversed(range(cps)):
            ch = chunks[j]
            q, k, sg, f, lb = ch["q"], ch["k"], ch["sg"], ch["f"], ch["lb"]
            dq_h, dk_h, dv_h, extra_h = [], [], [], []
            for h, sl in enumerate(heads):
                st_prev = st_in_ref[j, h]
                d_st = dst_ref[h]
                d_st_b = d_st.astype(BF16)
                do_b = ch["do_pad"][h][0:HGRN_CHUNK, :]
                kd, e_last = ch["kd"][:, sl], ch["e_last"][:, sl]
                dv = ch["dv"][h] + _dot(kd, d_st_b, _NT)
                d_qb = _dot(do_b, st_prev, _NN)
                d_kd = lax.dot_general(ch["v_b"][h], d_st_b, _NN, preferred_element_type=F32)
                extra_h.append(jnp.sum(st_prev * d_st, axis=0, keepdims=True) * e_last
                               + jnp.sum(kd * d_kd, axis=0, keepdims=True))
                dst_ref[h] = d_st * e_last + _dot(do_b, ch["qb"][:, sl], _TN)
                dq_h.append(ch["d_qt"][h] * ch["e_q"][:, sl] + d_qb * ch["e_b"][:, sl])
                dkk = d_kd * ch["e_bl"][:, sl]
                for s_ in range(nsub):
                    dkk = dkk + ch["d_kst"][h][HGRN_CHUNK * s_:HGRN_CHUNK * (s_ + 1), :] * ch["e_k"][s_][:, sl]
                dk_h.append(dkk)
                dv_h.append(dv)
            dq = jnp.concatenate(dq_h, axis=1)
            dk = jnp.concatenate(dk_h, axis=1)
            dv = jnp.concatenate(dv_h, axis=1)
            extra = jnp.concatenate(extra_h, axis=1)
            db = q * dq - k * dk + jnp.where(row == HGRN_CHUNK - 1, extra, 0.0)
            dg = _tri_apply(upper, db)
            df = dg / f - dk
            dhf = df * (1.0 - lb) * sg * (1.0 - sg)
            dhq = dq * _dsilu(ch["hq"], _sig(ch["hq"]))
            full = jnp.concatenate([dhq, dhf, dv, ch["dhg"]], axis=1)
            duh_ref[ch["rows"], :] = full.astype(duh_ref.dtype)
            ch["full"] = full
            dbias_ref[...] += jnp.sum(full, axis=0, keepdims=True)
            dlb_acc[...] += jnp.sum(df * (1.0 - sg), axis=0, keepdims=True)
        dng_ref[...] += dng
        _store_transposed(duh_t_ref, [ch["full"] for ch in chunks])

        @pl.when(i == ns - 1)
        def _():
            lb = chunks[0]["lb"]
            d_a0 = dlb_acc[...] * lb * (1.0 - lb)
            r8 = lax.broadcasted_iota(jnp.int32, (8, HG_W), 0)
            dlb_ref[...] = jnp.where(r8 == 0, d_a0, jnp.where(r8 == 1, -d_a0, 0.0))

    col = lambda j: pl.BlockSpec((rows_step, HG_W), lambda i: (ns - 1 - i, j))
    return _call(
        body, name=name, grid=(ns,), ins=[uh, uh, uh, uh, o_pre, d_r, states, lb_raw, norm_g],
        in_specs=[col(0), col(1), col(2), col(3), col(0), col(0),
                  pl.BlockSpec((cps, 4, LANES, LANES), lambda i: (ns - 1 - i, 0, 0, 0)),
                  pl.BlockSpec((2, HG_W), lambda i: (0, 0)), pl.BlockSpec((1, LANES), lambda i: (0, 0))],
        out_specs=[pl.BlockSpec((rows_step, UH_W), lambda i: (ns - 1 - i, 0)),
                   pl.BlockSpec((UH_W, rows_step), lambda i: (0, ns - 1 - i)),
                   pl.BlockSpec((1, UH_W), lambda i: (0, 0)),
                   pl.BlockSpec((1, LANES), lambda i: (0, 0)),
                   pl.BlockSpec((8, HG_W), lambda i: (0, 0))],
        out_shape=[jax.ShapeDtypeStruct((t, UH_W), BF16), jax.ShapeDtypeStruct((UH_W, t), BF16),
                   jax.ShapeDtypeStruct((1, UH_W), F32),
                   jax.ShapeDtypeStruct((1, LANES), F32), jax.ShapeDtypeStruct((8, HG_W), F32)],
        scratch_shapes=[pltpu.VMEM((4, LANES, LANES), F32), pltpu.VMEM((cps, HGRN_CHUNK, HG_W), F32),
                        pltpu.VMEM((1, HG_W), F32)],
        sem=("arbitrary",), comm=comm)


def _ln_bwd_math(dy, xhat, rstd, g):
    dxh = dy * g
    return rstd * (dxh - jnp.mean(dxh, axis=1, keepdims=True)
                   - xhat * jnp.mean(dxh * xhat, axis=1, keepdims=True))


def _mm_rows(a, b, extras, *, name, epilogue, out_shape, out_specs, tb=False, tm=512, tk=1408):
    m, kdim = a.shape
    n = b.shape[0] if tb else b.shape[1]
    tm = _pick(m, (tm, 256, 128))
    tk = _pick(kdim, (tk, 1408, 1024, 768, 512, 256, 128))
    nk = kdim // tk
    b_spec = pl.BlockSpec((n, tk), lambda i, k: (0, k)) if tb else pl.BlockSpec((tk, n), lambda i, k: (k, 0))
    dims = _NT if tb else _NN
    n_ex, n_out = len(extras), len(out_shape)

    def body(*refs):
        a_ref, b_ref = refs[0], refs[1]
        ex_refs = refs[2:2 + n_ex]
        o_refs = refs[2 + n_ex:2 + n_ex + n_out]
        acc_ref = refs[-1]
        i, k = pl.program_id(0), pl.program_id(1)

        @pl.when(k == 0)
        def _():
            acc_ref[...] = jnp.zeros_like(acc_ref)

        acc_ref[...] += _dot(a_ref[...], b_ref[...], dims)

        @pl.when(k == nk - 1)
        def _():
            epilogue(acc_ref[...], ex_refs, o_refs, i == 0)

    return pl.pallas_call(
        body, name=name, grid=(m // tm, nk),
        in_specs=[pl.BlockSpec((tm, tk), lambda i, k: (i, k)), b_spec] + [sp for _, sp in extras],
        out_specs=list(out_specs), out_shape=list(out_shape),
        scratch_shapes=[pltpu.VMEM((tm, n), F32)],
        compiler_params=_cp("arbitrary", "arbitrary"),
    )(a, b, *[arr for arr, _ in extras])


def _rows_specs(tm, d):
    row = pl.BlockSpec((tm, d), lambda i, k: (i, 0))
    vec = pl.BlockSpec((1, d), lambda i, k: (0, 0))
    col = pl.BlockSpec((tm, 1), lambda i, k: (i, 0))
    return row, vec, col


def _mm_ln_fwd(a, b, addend, g, beta, name, tm=512):
    t, d = addend.shape
    tm = _pick(t, (tm, 256, 128))
    row, vec, col = _rows_specs(tm, d)

    def epilogue(acc, ex, outs, first):
        z = acc + ex[0][...]
        mu = jnp.mean(z, axis=1, keepdims=True)
        zc = z - mu
        rstd = lax.rsqrt(jnp.mean(zc * zc, axis=1, keepdims=True) + LN_EPS)
        xhat = zc * rstd
        h = xhat * ex[1][...] + ex[2][...]
        outs[0][...] = h
        outs[1][...] = h.astype(BF16)
        outs[2][...] = xhat
        outs[3][...] = rstd

    return _mm_rows(a, b, [(addend, row), (g, vec), (beta, vec)], name=name, epilogue=epilogue, tm=tm,
                    out_shape=[jax.ShapeDtypeStruct((t, d), F32), jax.ShapeDtypeStruct((t, d), BF16),
                               jax.ShapeDtypeStruct((t, d), F32), jax.ShapeDtypeStruct((t, 1), F32)],
                    out_specs=[row, row, row, col])


CONV_RB = 32
HALO = 8


def _sum8(x):
    acc = x[0:8]
    for r in range(8, x.shape[0], 8):
        acc = acc + x[r:r + 8]
    return acc


FFN_TILE = 256
FFN_COLS = 256


def _rows_before(win, k):
    return pltpu.roll(win, k, 0)[HALO:]


def _rows_after(win, k):
    n = win.shape[0]
    return pltpu.roll(win, n - k, 0)[0:n - HALO]


def _resident(shape):
    return pl.BlockSpec(shape, lambda i: (0,) * len(shape), pipeline_mode=pl.Buffered(1))


def _ffn_fwd(h1b, h1, w_up_t, conv_w, conv_b, w_down, target, ln2_g, ln2_b, name, comm=None):
    t, d = h1.shape
    tr = _pick(t, (FFN_TILE, 128))
    nblk = D_FF // FFN_COLS
    rb = CONV_RB

    def body(a_ref, wup_ref, cw_ref, cb_ref, wd_ref, h1_ref, tgt_ref, g_ref, b_ref,
             u2_ref, hm_ref, dz_ref, dg_ref, db_ref, loss_ref, ext):
        i = pl.program_id(0)

        @pl.when(i == 0)
        def _():
            ext[0:HALO, :] = jnp.zeros((HALO, D_FF), F32)
            dg_ref[...] = jnp.zeros_like(dg_ref)
            db_ref[...] = jnp.zeros_like(db_ref)
            loss_ref[...] = jnp.zeros_like(loss_ref)

        a = a_ref[...]
        for c in range(nblk):
            cs = slice(c * FFN_COLS, (c + 1) * FFN_COLS)
            vs = slice(D_FF + c * FFN_COLS, D_FF + (c + 1) * FFN_COLS)
            gate_pre = lax.dot_general(a, wup_ref[cs, :], _NT, preferred_element_type=F32)
            u2_ref[:, cs] = gate_pre
            ext[HALO:, cs] = gate_pre
            u2_ref[:, vs] = lax.dot_general(a, wup_ref[vs, :], _NT, preferred_element_type=F32)
        acc = jnp.zeros((tr, d), F32)
        for c in range(nblk):
            cs = slice(c * FFN_COLS, (c + 1) * FFN_COLS)
            for sub in range(FFN_COLS // LANES):
                ln = slice(c * FFN_COLS + sub * LANES, c * FFN_COLS + (sub + 1) * LANES)
                vl = slice(D_FF + c * FFN_COLS + sub * LANES, D_FF + c * FFN_COLS + (sub + 1) * LANES)
                w0, w1, w2, bb = cw_ref[0:1, ln], cw_ref[1:2, ln], cw_ref[2:3, ln], cb_ref[:, ln]
                for r0 in range(0, tr, rb):
                    win = ext[r0:r0 + HALO + rb, ln]
                    gate = _rows_before(win, 2) * w0 + _rows_before(win, 1) * w1 + win[HALO:] * w2 + bb
                    hm_ref[r0:r0 + rb, ln] = (gate * _sig(gate) * u2_ref[r0:r0 + rb, vl]).astype(hm_ref.dtype)
            acc = acc + lax.dot_general(hm_ref[:, cs], wd_ref[cs, :], _NN, preferred_element_type=F32)
        ext[0:HALO, :] = ext[tr:tr + HALO, :]

        z = acc + ALPHA * h1_ref[...]
        gg = g_ref[...]
        mu = jnp.mean(z, axis=1, keepdims=True)
        zc = z - mu
        rstd = lax.rsqrt(jnp.mean(zc * zc, axis=1, keepdims=True) + LN_EPS)
        xhat = zc * rstd
        err = xhat * gg + b_ref[...] - tgt_ref[...]
        loss_ref[...] += 0.5 * jnp.sum(jnp.mean(err * err, axis=1, keepdims=True))
        dy = err * (1.0 / d)
        dz_ref[...] = _ln_bwd_math(dy, xhat, rstd, gg)
        dg_ref[...] += jnp.sum(dy * xhat, axis=0, keepdims=True)
        db_ref[...] += jnp.sum(dy, axis=0, keepdims=True)

    row = lambda w: pl.BlockSpec((tr, w), lambda i: (i, 0))
    vec = pl.BlockSpec((1, d), lambda i: (0, 0))
    return _call(
        body, name=name, grid=(t // tr,),
        ins=[h1b, w_up_t, conv_w, conv_b, w_down, h1, target, ln2_g, ln2_b],
        in_specs=[row(d), _resident((2 * D_FF, d)), _resident((3, D_FF)), _resident((1, D_FF)),
                  _resident((D_FF, d)), row(d), row(d), vec, vec],
        out_specs=[row(2 * D_FF), row(D_FF), row(d), vec, vec, pl.BlockSpec((1, LANES), lambda i: (0, 0))],
        out_shape=[jax.ShapeDtypeStruct((t, 2 * D_FF), F32), jax.ShapeDtypeStruct((t, D_FF), BF16),
                   jax.ShapeDtypeStruct((t, d), F32), jax.ShapeDtypeStruct((1, d), F32),
                   jax.ShapeDtypeStruct((1, d), F32), jax.ShapeDtypeStruct((1, LANES), F32)],
        scratch_shapes=[pltpu.VMEM((tr + HALO, D_FF), F32)],
        sem=("arbitrary",), comm=comm)


def _ffn_bwd(dz2, u2, w_down, w_up_t, conv_w, conv_b, xhat1, rstd1, ln1_g, name, comm=None):
    t, d = dz2.shape
    tr = _pick(t, (FFN_TILE, 128))
    nt = t // tr
    hb = tr // HALO
    nblk = D_FF // FFN_COLS
    rb = CONV_RB

    def body(dz2_ref, dz2_next_ref, u2_ref, gp_prev_ref, wd_ref, wup_ref, cw_ref, cb_ref, xhat_ref, rstd_ref,
             g1_ref, du_ref, dz1_ref, dw_ref, dcb_ref, dg1_ref, db1_ref, head, dh_s, dg_s):
        i = pl.program_id(0)

        @pl.when(i == 0)
        def _():
            dg_s[tr:, :] = jnp.zeros((HALO, D_FF), F32)
            dw_ref[...] = jnp.zeros_like(dw_ref)
            dcb_ref[...] = jnp.zeros_like(dcb_ref)
            dg1_ref[...] = jnp.zeros_like(dg1_ref)
            db1_ref[...] = jnp.zeros_like(db1_ref)

        dz2 = dz2_ref[...]

        @pl.when(i == 0)
        def _():
            dz2_b = dz2.astype(BF16)
            for c in range(nblk):
                cs = slice(c * FFN_COLS, (c + 1) * FFN_COLS)
                dh_s[:, cs] = lax.dot_general(dz2_b, wd_ref[cs, :], _NT, preferred_element_type=F32)

        dz2_next = dz2_next_ref[...].astype(BF16)
        dh_next = [lax.dot_general(dz2_next, wd_ref[c * FFN_COLS:(c + 1) * FFN_COLS, :], _NT,
                                   preferred_element_type=F32) for c in range(nblk)]
        head[0:HALO, :] = jnp.where(i == nt - 1, 0.0, gp_prev_ref[...])
        head[HALO:, :] = u2_ref[0:rb, 0:D_FF]

        acc = jnp.zeros((tr, d), F32)
        for blk in range(nblk):
            for c in range(blk * FFN_COLS // LANES, (blk + 1) * FFN_COLS // LANES):
                ln = slice(c * LANES, (c + 1) * LANES)
                vl = slice(D_FF + c * LANES, D_FF + (c + 1) * LANES)
                w0, w1, w2, bb = cw_ref[0:1, ln], cw_ref[1:2, ln], cw_ref[2:3, ln], cb_ref[:, ln]
                acc_b = jnp.zeros((8, LANES), F32)
                acc_w = [jnp.zeros((8, LANES), F32) for _ in range(3)]
                for r0 in range(0, tr, rb):
                    win = head[:, ln] if r0 == 0 else u2_ref[r0 - HALO:r0 + rb, ln]
                    g_m2, g_m1, g_0 = _rows_before(win, 2), _rows_before(win, 1), win[HALO:]
                    gate = g_m2 * w0 + g_m1 * w1 + g_0 * w2 + bb
                    sg = _sig(gate)
                    dh = dh_s[r0:r0 + rb, ln]
                    dgate = dh * u2_ref[r0:r0 + rb, vl] * _dsilu(gate, sg)
                    dg_s[r0:r0 + rb, ln] = dgate
                    du_ref[r0:r0 + rb, vl] = (dh * (gate * sg)).astype(du_ref.dtype)
                    acc_b = acc_b + _sum8(dgate)
                    acc_w[0] = acc_w[0] + _sum8(dgate * g_m2)
                    acc_w[1] = acc_w[1] + _sum8(dgate * g_m1)
                    acc_w[2] = acc_w[2] + _sum8(dgate * g_0)
                dcb_ref[:, ln] += jnp.sum(acc_b, axis=0, keepdims=True)
                for j in range(3):
                    dw_ref[j:j + 1, ln] += jnp.sum(acc_w[j], axis=0, keepdims=True)
                for r0 in range(0, tr, rb):
                    win = dg_s[r0:r0 + rb + HALO, ln]
                    d_gp = _rows_after(win, 2) * w0 + _rows_after(win, 1) * w1 + win[0:rb] * w2
                    du_ref[r0:r0 + rb, ln] = d_gp.astype(du_ref.dtype)
            cs = slice(blk * FFN_COLS, (blk + 1) * FFN_COLS)
            vs = slice(D_FF + blk * FFN_COLS, D_FF + (blk + 1) * FFN_COLS)
            acc = acc + lax.dot_general(du_ref[:, cs], wup_ref[cs, :], _NN, preferred_element_type=F32)
            acc = acc + lax.dot_general(du_ref[:, vs], wup_ref[vs, :], _NN, preferred_element_type=F32)
        dg_s[tr:, :] = dg_s[0:HALO, :]
        for c in range(nblk):
            dh_s[:, c * FFN_COLS:(c + 1) * FFN_COLS] = dh_next[c]
        dy = acc + ALPHA * dz2
        xh = xhat_ref[...]
        dz1_ref[...] = _ln_bwd_math(dy, xh, rstd_ref[...], g1_ref[...])
        dg1_ref[...] += jnp.sum(dy * xh, axis=0, keepdims=True)
        db1_ref[...] += jnp.sum(dy, axis=0, keepdims=True)

    rev = lambda w: pl.BlockSpec((tr, w), lambda i: (nt - 1 - i, 0))
    vec = pl.BlockSpec((1, d), lambda i: (0, 0))
    return _call(
        body, name=name, grid=(nt,),
        ins=[dz2, dz2, u2, u2, w_down, w_up_t, conv_w, conv_b, xhat1, rstd1, ln1_g],
        in_specs=[rev(d), pl.BlockSpec((tr, d), lambda i: (jnp.maximum(nt - 2 - i, 0), 0)), rev(2 * D_FF),
                  pl.BlockSpec((HALO, D_FF), lambda i: (jnp.maximum((nt - 1 - i) * hb - 1, 0), 0)),
                  _resident((D_FF, d)), _resident((2 * D_FF, d)), _resident((3, D_FF)), _resident((1, D_FF)),
                  rev(d), pl.BlockSpec((tr, 1), lambda i: (nt - 1 - i, 0)), vec],
        out_specs=[rev(2 * D_FF), rev(d), pl.BlockSpec((8, D_FF), lambda i: (0, 0)),
                   pl.BlockSpec((1, D_FF), lambda i: (0, 0)), vec, vec],
        out_shape=[jax.ShapeDtypeStruct((t, 2 * D_FF), BF16), jax.ShapeDtypeStruct((t, d), F32),
                   jax.ShapeDtypeStruct((8, D_FF), F32), jax.ShapeDtypeStruct((1, D_FF), F32),
                   jax.ShapeDtypeStruct((1, d), F32), jax.ShapeDtypeStruct((1, d), F32)],
        scratch_shapes=[pltpu.VMEM((HALO + rb, D_FF), F32), pltpu.VMEM((tr, D_FF), F32),
                        pltpu.VMEM((tr + HALO, D_FF), F32)],
        sem=("arbitrary",), comm=comm)


def _adamw(w, g, m, v, name):
    rows, cols = w.shape
    tr = _pick(rows, (256, 128, 64, 32, 16, 8))

    def body(w_ref, g_ref, m_ref, v_ref, d_ref, nm_ref, nv_ref):
        d_ref[...], nm_ref[...], nv_ref[...] = _adamw_math(w_ref[...], g_ref[...], m_ref[...], v_ref[...])

    spec = pl.BlockSpec((tr, cols), lambda i: (i, 0))
    shp = jax.ShapeDtypeStruct((rows, cols), F32)
    return pl.pallas_call(
        body, name=name, grid=(rows // tr,),
        in_specs=[spec, spec, spec, spec], out_specs=[spec, spec, spec], out_shape=[shp, shp, shp],
        compiler_params=_cp("parallel"),
    )(w, g, m, v)


def _pad_rows(a, rows):
    return jnp.pad(a, ((0, rows - a.shape[0]), (0, 0)))


SMALL_LAYOUT = (("ln1_g", 1024), ("ln1_b", 1024), ("b_in", 2816), ("sinks", 8), ("hgrn_lb", 1024),
                ("hgrn_norm_g", 128), ("ln2_g", 1024), ("ln2_b", 1024), ("conv_b", 2816), ("loss", 1))
SMALL_SHAPES = {"ln1_g": (1, 1024), "ln1_b": (1, 1024), "b_in": (1, 2816), "sinks": (1, 8), "hgrn_lb": (2, 512),
                "hgrn_norm_g": (1, 128), "ln2_g": (1, 1024), "ln2_b": (1, 1024), "conv_b": (1, 2816),
                "loss": (1,)}


def _pack_small(parts):
    rows = []
    for name, size in SMALL_LAYOUT:
        flat = parts[name].reshape(-1).astype(F32)
        padded = -(-size // LANES) * LANES
        rows.append(jnp.pad(flat, (0, padded - size)).reshape(-1, LANES))
    return _pad_rows(jnp.concatenate(rows, axis=0), SMALL_ROWS)


def _unpack_small(pack):
    out, r = {}, 0
    for name, size in SMALL_LAYOUT:
        nrows = -(-size // LANES)
        out[name] = pack[r:r + nrows].reshape(-1)[:size].reshape(SMALL_SHAPES[name])
        r += nrows
    return out


def _own(full, rows):
    return lax.dynamic_slice_in_dim(full, _me() * rows, rows, axis=0)


def kernel(x, positions, ln1_g, ln1_b, w_in, b_in, sinks, hgrn_lb, hgrn_norm_g, w_o, ln2_g, ln2_b, w_up, conv_w, conv_b, w_down, loss_target, m_ln1_g, m_ln1_b, m_w_in, m_b_in, m_sinks, m_hgrn_lb, m_hgrn_norm_g, m_w_o, m_ln2_g, m_ln2_b, m_w_up, m_conv_w, m_conv_b, m_w_down, v_ln1_g, v_ln1_b, v_w_in, v_b_in, v_sinks, v_hgrn_lb, v_hgrn_norm_g, v_w_o, v_ln2_g, v_ln2_b, v_w_up, v_conv_w, v_conv_b, v_w_down):
    t = x.shape[1]
    x2 = x[0]
    target = loss_target[0]
    pos_col = positions.reshape(t, 1)

    w_in_t_s = w_in[0].T.astype(BF16)
    w_up_t_s = w_up[0].T.astype(BF16)
    w_o_s = w_o[0].astype(BF16)
    w_down_s = w_down[0].astype(BF16)
    (ctab, stab, xb), (w_in_t_g, cw_g) = _prep(
        pos_col, x2, "prep_ag_w_in", _Comm([{"kind": "gather", "arr": w_in_t_s}, {"kind": "gather", "arr": _pad_rows(conv_w[0], 8)}]))
    w_in_t = w_in_t_g.reshape(D_FF, D_MODEL)
    w_a_t, w_h_t = w_in_t[:UA_W], w_in_t[UA_W:]
    conv_w_f = cw_g[:, 0:3].transpose(1, 0, 2).reshape(3, D_FF)

    ua = _mm(xb, w_a_t, tb=True, bias=b_in[:, :UA_W], name="fwd_in_attn")
    uh = _mm(xb, w_h_t, tb=True, bias=b_in[:, UA_W:], name="fwd_in_hgrn")
    half_up = SHARD_UP // 2
    (a_out, a_out_t), (w_o_g, w_up_half) = _attn_fwd(
        ua, ctab, stab, sinks, "attn_fwd",
        comm=_Comm([{"kind": "gather", "arr": w_o_s},
                    {"kind": "gather", "arr": w_up_t_s, "rows": (0, half_up), "dst_rows": SHARD_UP}]))
    (r_out, r_out_t, o_pre, states), (w_up_t_g, w_down_g) = _hgrn_fwd(
        uh, hgrn_lb, hgrn_norm_g, "hgrn_fwd",
        comm=_Comm([{"kind": "gather", "arr": w_up_t_s, "rows": (half_up, half_up), "dst_rows": SHARD_UP,
                     "dst_first": half_up, "into": w_up_half},
                    {"kind": "gather", "arr": w_down_s}]))
    w_down_f = w_down_g.reshape(D_FF, D_MODEL)
    w_o_f = w_o_g.reshape(D_MODEL, D_MODEL)
    w_up_t = w_up_t_g.reshape(2 * D_FF, D_MODEL)
    z1 = _mm(a_out, w_o_f[:ATTN_W], addend=x2, addend_scale=ALPHA, name="fwd_o_attn")
    h1, h1b, xhat1, rstd1 = _mm_ln_fwd(r_out, w_o_f[ATTN_W:], z1, ln1_g, ln1_b, "fwd_o_hgrn_ln1")
    u2, hmid, dz2, d_ln2_g, d_ln2_b, loss_part = _ffn_fwd(h1b, h1, w_up_t, conv_w_f, conv_b, w_down_f, target,
                                                         ln2_g, ln2_b, "ffn_fwd")[0]

    d_w_down, d_w_down_b = _mm(hmid, dz2, ta=True, out_dtype2=BF16, tm=1408, tk=512, name="bwd_down_dw")
    (d_u2, dz1, d_conv_w8, d_conv_b, d_ln1_g, d_ln1_b), (recv_down,) = _ffn_bwd(
        dz2, u2, w_down_f, w_up_t, conv_w_f, conv_b, xhat1, rstd1, ln1_g, "ffn_bwd",
        comm=_Comm([{"kind": "exchange", "arr": d_w_down_b.reshape(N_DEV, SHARD_DOWN, D_MODEL)}]))
    d_w_up_t, d_w_up_t_b = _mm(d_u2, h1b, ta=True, out_dtype2=BF16, tm=1408, tk=512, name="bwd_up_dw")
    d_a = _mm(dz1, w_o_f[:ATTN_W], tb=True, name="bwd_o_dx_attn")
    d_r = _mm(dz1, w_o_f[ATTN_W:], tb=True, name="bwd_o_dx_hgrn")
    d_w_o_part = _mm(a_out_t, dz1, out_dtype2=BF16, tm=ATTN_W, out_rows=D_MODEL, name="bwd_o_dw_attn")
    d_w_o, d_w_o_b = _mm(r_out_t, dz1, out_dtype2=BF16, tm=HG_W, out_rows=D_MODEL, first_row=ATTN_W,
                         into=d_w_o_part, name="bwd_o_dw_hgrn")
    d_w_up_x = d_w_up_t_b.reshape(N_DEV, SHARD_UP, D_MODEL)
    half = SHARD_UP // 2
    d_cw_x = d_conv_w8.reshape(8, N_DEV, SHARD_IN).transpose(1, 0, 2)
    (d_ua, d_ua_t, d_bias_a, d_sinks), (recv_up_half, recv_cw) = _attn_bwd(
        ua, d_a, ctab, stab, sinks, "attn_bwd",
        comm=_Comm([{"kind": "exchange", "arr": d_w_up_x, "rows": (0, half), "dst_rows": SHARD_UP},
                    {"kind": "exchange", "arr": d_cw_x}]))
    (d_uh, d_uh_t, d_bias_h, d_norm_g, d_lb8), (recv_up, recv_o) = _hgrn_bwd(
        uh, o_pre, d_r, states, hgrn_lb, hgrn_norm_g, "hgrn_bwd",
        comm=_Comm([{"kind": "exchange", "arr": d_w_up_x, "rows": (half, half), "dst_rows": SHARD_UP,
                     "dst_first": half, "into": recv_up_half},
                    {"kind": "exchange", "arr": d_w_o_b.reshape(N_DEV, SHARD_O, D_MODEL)}]))
    d_w_in_part = _mm(d_ua_t, xb, out_dtype2=BF16, tm=UA_W, tk=t, out_rows=D_FF, name="bwd_in_dw_attn")
    d_w_in_t, d_w_in_t_b = _mm(d_uh_t, xb, out_dtype2=BF16, tm=256, tk=t, out_rows=D_FF, first_row=UA_W,
                               into=d_w_in_part, name="bwd_in_dw_hgrn")
    small_local = _pack_small({
        "ln1_g": d_ln1_g, "ln1_b": d_ln1_b, "b_in": jnp.concatenate([d_bias_a, d_bias_h], axis=1),
        "sinks": d_sinks[:, :8], "hgrn_lb": d_lb8[0:2], "hgrn_norm_g": d_norm_g, "ln2_g": d_ln2_g,
        "ln2_b": d_ln2_b, "conv_b": d_conv_b, "loss": loss_part[:, :1]})
    d_w_in_x = d_w_in_t_b.reshape(N_DEV, SHARD_IN, D_MODEL)
    res_up, (from_sibling,) = _sum_shards_adamw(
        [recv_up], _own(d_w_up_t, SHARD_UP), w_up[0].T, m_w_up[0].T, v_w_up[0].T, "adamw_w_up",
        comm=_Comm([{"kind": "pair4", "arr": d_w_in_x}]))
    res_up = [r.T for r in res_up]
    own_in, chip_part = _pair_reduce(from_sibling, d_w_in_t, "pair_reduce_w_in")
    dx, (from_chips, small_g) = _mm(d_uh, w_h_t, addend=dz1, addend_scale=ALPHA, name="bwd_in_dx_hgrn",
                                    comm=_Comm([{"kind": "chips3", "arr": chip_part},
                                                {"kind": "gather", "arr": small_local}]))
    dx = _mm(d_ua, w_a_t, addend=dx, tk=768, name="bwd_in_dx_attn")

    res_in = [r.T for r in _chip_sum_adamw(from_chips, own_in, w_in[0].T, m_w_in[0].T, v_w_in[0].T, "adamw_w_in")]
    res_o = _sum_shards_adamw([recv_o], _own(d_w_o, SHARD_O), w_o[0], m_w_o[0], v_w_o[0], "adamw_w_o")
    res_down = _sum_shards_adamw([recv_down], _own(d_w_down, SHARD_DOWN), w_down[0], m_w_down[0], v_w_down[0],
                                 "adamw_w_down")
    g_cw = _sum_slots(recv_cw, "sum_conv_w")
    cw8 = lambda a: _pad_rows(a, 8)
    res_cw = (g_cw,) + tuple(_adamw(cw8(conv_w[0]), g_cw, cw8(m_conv_w[0]), cw8(v_conv_w[0]), "adamw_conv_w"))
    big = {"w_in": [r[None] for r in res_in], "w_up": [r[None] for r in res_up],
           "w_o": [r[None] for r in res_o], "w_down": [r[None] for r in res_down],
           "conv_w": [r[None, 0:3] for r in res_cw]}

    small_sum = _sum_slots(small_g, "ar_small_sum")
    gs = _unpack_small(small_sum)
    loss = gs["loss"][0]
    zero1 = jnp.zeros((1,), F32)
    w_small = _pack_small({"ln1_g": ln1_g, "ln1_b": ln1_b, "b_in": b_in, "sinks": sinks, "hgrn_lb": hgrn_lb,
                           "hgrn_norm_g": hgrn_norm_g, "ln2_g": ln2_g, "ln2_b": ln2_b, "conv_b": conv_b,
                           "loss": zero1})
    m_small = _pack_small({"ln1_g": m_ln1_g, "ln1_b": m_ln1_b, "b_in": m_b_in, "sinks": m_sinks,
                           "hgrn_lb": m_hgrn_lb, "hgrn_norm_g": m_hgrn_norm_g, "ln2_g": m_ln2_g,
                           "ln2_b": m_ln2_b, "conv_b": m_conv_b, "loss": zero1})
    v_small = _pack_small({"ln1_g": v_ln1_g, "ln1_b": v_ln1_b, "b_in": v_b_in, "sinks": v_sinks,
                           "hgrn_lb": v_hgrn_lb, "hgrn_norm_g": v_hgrn_norm_g, "ln2_g": v_ln2_g,
                           "ln2_b": v_ln2_b, "conv_b": v_conv_b, "loss": zero1})
    small = [gs] + [_unpack_small(p) for p in _adamw(w_small, small_sum, m_small, v_small, "adamw_small")]

    order = ["ln1_g", "ln1_b", "w_in", "b_in", "sinks", "hgrn_lb", "hgrn_norm_g", "w_o", "ln2_g", "ln2_b",
             "w_up", "conv_w", "conv_b", "w_down"]

    def pick(idx):
        return [big[n][idx] if n in big else small[idx][n] for n in order]

    return (loss, dx[None], *pick(0), *pick(1), *pick(2), *pick(3))
```

```python
import functools

import jax
import jax.numpy as jnp
import numpy as np
from jax import lax
from jax.experimental import pallas as pl
from jax.experimental.pallas import tpu as pltpu

F32 = jnp.float32
BF16 = jnp.bfloat16

N_DEV = 8
D_MODEL = 1024
D_FF = 2816
ATTN_W = 512
KV_W = 128
UA_W = ATTN_W + 2 * KV_W
UH_W = 2048
HG_W = 512
ATTN_BLOCK = 128
HGRN_CHUNK = 64
HGRN_SUB = 16
HGRN_CHUNKS_PER_STEP = 4
EXP_CLAMP = 85.0
NEG_BIG = -1e30
LN_EPS = 1e-5
RMS_EPS = 1e-6
ALPHA = 2.0 ** 0.25
ATTN_SCALE = 0.125
ROPE_THETA = 500000.0

ADAM_LR = 0.001
ADAM_B1 = 0.9
ADAM_B2 = 0.999
ADAM_EPS = 1e-08
ADAM_WD = 0.01
ADAM_STEP = 10

LANES = 128
VMEM_LIMIT_BYTES = 56 * 1024 * 1024

SHARD_IN = D_FF // N_DEV
SHARD_UP = 2 * D_FF // N_DEV
SHARD_O = D_MODEL // N_DEV
SHARD_DOWN = D_FF // N_DEV
SMALL_ROWS = 88

_MESH = pl.DeviceIdType.MESH
_NT = (((1,), (1,)), ((), ()))
_NN = (((1,), (0,)), ((), ()))
_TN = (((0,), (0,)), ((), ()))


def _cp(*sem):
    if sem:
        return pltpu.CompilerParams(dimension_semantics=sem, vmem_limit_bytes=VMEM_LIMIT_BYTES)
    return pltpu.CompilerParams(vmem_limit_bytes=VMEM_LIMIT_BYTES)


def _sig(x):
    return 0.5 * jnp.tanh(0.5 * x) + 0.5


def _dsilu(x, s):
    return s * (1.0 + x * (1.0 - s))


def _dot(a, b, dims):
    return lax.dot_general(a.astype(BF16), b.astype(BF16), dims, preferred_element_type=F32)


def _split(a):
    hi = a.astype(BF16)
    return hi, (a - hi.astype(F32)).astype(BF16)


def _dot3(a, b, dims):
    ah, al = _split(a)
    bh, bl = _split(b)
    d = functools.partial(lax.dot_general, dimension_numbers=dims, preferred_element_type=F32)
    return d(ah, bh) + (d(ah, bl) + d(al, bh))


def _pick(n, pref):
    for t in pref:
        if t <= n and n % t == 0:
            return t
    return n


def _my_coords():
    return lax.axis_index("x"), lax.axis_index("y"), lax.axis_index("c")


def _peer(k):
    x, y, c = _my_coords()
    return (1 - x if k & 4 else x, 1 - y if k & 2 else y, 1 - c if k & 1 else c)


def _me():
    x, y, c = _my_coords()
    return 4 * x + 2 * y + c


class _Comm:
    def __init__(self, items):
        self.items = []
        for it in items:
            arr = it["arr"]
            full = arr.shape[0] if it["kind"] == "gather" else arr.shape[1]
            first, count = it.get("rows", (0, full))
            self.items.append(dict(kind=it["kind"], arr=arr, first=first, count=count,
                                   dst_rows=it.get("dst_rows", count), dst_first=it.get("dst_first", 0),
                                   into=it.get("into")))
        self.n = len(self.items)
        self.arrays = [it["arr"] for it in self.items]
        self.intos = [(a, it["into"]) for a, it in enumerate(self.items) if it["into"] is not None]

    def out_shapes(self):
        return [jax.ShapeDtypeStruct((4 if it["kind"] in ("pair4", "chips3") else N_DEV, it["dst_rows"],
                                      it["arr"].shape[-1]), it["arr"].dtype) for it in self.items]

    def specs(self, n=None):
        return [pl.BlockSpec(memory_space=pl.ANY)] * (self.n if n is None else n)

    def scratch(self):
        return [pltpu.SemaphoreType.DMA(((N_DEV - 1) * self.n,)), pltpu.SemaphoreType.DMA(((N_DEV - 1) * self.n,)),
                pltpu.SemaphoreType.DMA((self.n,))]

    def _src(self, a, ref, dev):
        it = self.items[a]
        blk = ref if it["kind"] == "gather" else ref.at[dev]
        return blk.at[pl.ds(it["first"], it["count"])]

    def _dst(self, a, ref, slot):
        it = self.items[a]
        return ref.at[slot].at[pl.ds(it["dst_first"], it["count"])]

    def _copy(self, a, k, src, dst, sems, me, slot):
        other = jnp.bitwise_xor(me, k)
        idx = a * (N_DEV - 1) + k - 1
        return pltpu.make_async_remote_copy(
            src_ref=self._src(a, src, other), dst_ref=self._dst(a, dst, me if slot == "mine" else other),
            send_sem=sems[0].at[idx], recv_sem=sems[1].at[idx], device_id=_peer(k), device_id_type=_MESH)

    def _pass_on(self, a, k, dst, sems, me):
        slot = self._dst(a, dst, jnp.bitwise_xor(me, k))
        idx = a * (N_DEV - 1) + k
        return pltpu.make_async_remote_copy(
            src_ref=slot, dst_ref=slot, send_sem=sems[0].at[idx], recv_sem=sems[1].at[idx],
            device_id=_peer(1), device_id_type=_MESH)

    def _part(self, a, r, src, dst, sems, me):
        it = self.items[a]
        idx = a * (N_DEV - 1) + r
        if it["kind"] == "pair4":
            k, slot = 1, jnp.bitwise_xor(jnp.bitwise_xor(me, 1), 2 * r)
        else:
            k, slot = 2 * r, r
        return pltpu.make_async_remote_copy(
            src_ref=src.at[slot].at[pl.ds(it["first"], it["count"])], dst_ref=self._dst(a, dst, r),
            send_sem=sems[0].at[idx], recv_sem=sems[1].at[idx], device_id=_peer(k), device_id_type=_MESH)

    def _parts(self, a):
        return range(4) if self.items[a]["kind"] == "pair4" else range(1, 4)

    def _local(self, a, src, dst, sems, me):
        return pltpu.make_async_copy(self._src(a, src, me), self._dst(a, dst, me), sems[2].at[a])

    def start(self, srcs, dsts, sems):
        me = _me()
        for a, (src, dst) in enumerate(zip(srcs, dsts)):
            if self.items[a]["kind"] in ("pair4", "chips3"):
                for r in self._parts(a):
                    self._part(a, r, src, dst, sems, me).start()
                continue
            direct = (1, 2, 4, 6) if self.items[a]["kind"] == "gather" else range(1, N_DEV)
            self._local(a, src, dst, sems, me).start()
            for k in direct:
                self._copy(a, k, src, dst, sems, me, "mine").start()

    def wait(self, srcs, dsts, sems):
        me = _me()
        for a, (src, dst) in enumerate(zip(srcs, dsts)):
            if self.items[a]["kind"] in ("pair4", "chips3"):
                for r in self._parts(a):
                    self._part(a, r, src, dst, sems, me).wait_recv()
                for r in self._parts(a):
                    self._part(a, r, src, dst, sems, me).wait_send()
                continue
            if self.items[a]["kind"] == "gather":
                for k in (2, 4, 6):
                    self._copy(a, k, src, dst, sems, me, "theirs").wait_recv()
                    self._pass_on(a, k, dst, sems, me).start()
                for k in (1, 3, 5, 7):
                    self._copy(a, k, src, dst, sems, me, "theirs").wait_recv()
                for k in (1, 2, 4, 6):
                    self._copy(a, k, src, dst, sems, me, "mine").wait_send()
                for k in (2, 4, 6):
                    self._pass_on(a, k, dst, sems, me).wait_send()
            else:
                for k in range(1, N_DEV):
                    self._copy(a, k, src, dst, sems, me, "theirs").wait_recv()
                for k in range(1, N_DEV):
                    self._copy(a, k, src, dst, sems, me, "mine").wait_send()
            self._local(a, src, dst, sems, me).wait()


def _call(body, *, name, grid, ins, in_specs, out_specs, out_shape, scratch_shapes=(), sem, comm=None):
    n_in, n_out, n_scr = len(ins), len(out_shape), len(scratch_shapes)
    if comm is None:
        outs = pl.pallas_call(
            body, name=name, grid=grid, in_specs=list(in_specs), out_specs=list(out_specs),
            out_shape=list(out_shape), scratch_shapes=list(scratch_shapes), compiler_params=_cp(*sem))(*ins)
        return list(outs), []
    nc, n_into = comm.n, len(comm.intos)

    def hosted(*refs):
        pos = n_in
        c_in = refs[pos:pos + nc]
        pos += nc + n_into
        outs = refs[pos:pos + n_out]
        pos += n_out
        c_out = refs[pos:pos + nc]
        pos += nc
        scr = refs[pos:pos + n_scr]
        sems = refs[pos + n_scr:]
        ids = [pl.program_id(d) for d in range(len(grid))]
        first = functools.reduce(jnp.logical_and, [i == 0 for i in ids])
        last = functools.reduce(jnp.logical_and, [i == g - 1 for i, g in zip(ids, grid)])

        @pl.when(first)
        def _():
            comm.start(c_in, c_out, sems)

        body(*refs[:n_in], *outs, *scr)

        @pl.when(last)
        def _():
            comm.wait(c_in, c_out, sems)

    aliases = {n_in + nc + j: n_out + a for j, (a, _) in enumerate(comm.intos)}
    outs = pl.pallas_call(
        hosted, name=name, grid=grid, in_specs=list(in_specs) + comm.specs() + comm.specs(n_into),
        out_specs=list(out_specs) + comm.specs(), out_shape=list(out_shape) + comm.out_shapes(),
        scratch_shapes=list(scratch_shapes) + comm.scratch(), input_output_aliases=aliases,
        compiler_params=_cp(*(["arbitrary"] * len(grid))))(*ins, *comm.arrays, *[arr for _, arr in comm.intos])
    return list(outs[:n_out]), list(outs[n_out:])


def _sum_slots(gathered, name):
    _, rows, cols = gathered.shape

    def body(g_ref, out_ref):
        acc = g_ref[0]
        for s in range(1, N_DEV):
            acc = acc + g_ref[s]
        out_ref[...] = acc

    return pl.pallas_call(
        body, name=name,
        out_shape=jax.ShapeDtypeStruct((rows, cols), F32),
        compiler_params=_cp(),
    )(gathered)


def _slot_sum(recv_ref, own_ref, shape):
    me = _me()
    acc = jnp.zeros(shape, F32)
    for s in range(N_DEV):
        acc = acc + jnp.where(me == s, own_ref[...], recv_ref[s].astype(F32))
    return acc


def _adamw_math(w, g, m, v):
    nm = ADAM_B1 * m + (1.0 - ADAM_B1) * g
    nv = ADAM_B2 * v + (1.0 - ADAM_B2) * (g * g)
    m_hat = nm / (1.0 - ADAM_B1 ** ADAM_STEP)
    v_hat = nv / (1.0 - ADAM_B2 ** ADAM_STEP)
    return -ADAM_LR * (m_hat / (jnp.sqrt(v_hat) + ADAM_EPS) + ADAM_WD * w), nm, nv


def _pair_reduce(from_sibling, mine, name):
    _, rows, cols = from_sibling.shape
    tr = _pick(rows, (176, 128, 64, 32, 16, 8))
    tiles = rows // tr
    table = jnp.bitwise_xor(_me(), jnp.arange(0, N_DEV, 2, dtype=jnp.int32))

    def body(tbl_ref, sib_ref, mine_ref, own_ref, send_ref):
        r = pl.program_id(1)
        total = mine_ref[...] + sib_ref[0].astype(F32)
        send_ref[0] = jnp.where(r == 0, 0.0, total).astype(BF16)

        @pl.when(r == 0)
        def _():
            own_ref[...] = total

    grid_spec = pltpu.PrefetchScalarGridSpec(
        num_scalar_prefetch=1, grid=(tiles, 4),
        in_specs=[pl.BlockSpec((1, tr, cols), lambda i, r, tbl: (r, i, 0)),
                  pl.BlockSpec((tr, cols), lambda i, r, tbl: (tbl[r] * tiles + i, 0))],
        out_specs=[pl.BlockSpec((tr, cols), lambda i, r, tbl: (i, 0)),
                   pl.BlockSpec((1, tr, cols), lambda i, r, tbl: (r, i, 0))])
    return pl.pallas_call(
        body, name=name, grid_spec=grid_spec,
        out_shape=[jax.ShapeDtypeStruct((rows, cols), F32), jax.ShapeDtypeStruct((4, rows, cols), BF16)],
        compiler_params=_cp("arbitrary", "arbitrary"),
    )(table, from_sibling, mine)


def _chip_sum_adamw(from_chips, own, w, m, v, name):
    _, rows, cols = from_chips.shape
    tr = _pick(rows, (176, 128, 64, 32, 16, 8))

    def body(recv_ref, own_ref, w_ref, m_ref, v_ref, g_ref, d_ref, nm_ref, nv_ref):
        g = own_ref[...]
        for r in range(1, 4):
            g = g + recv_ref[r].astype(F32)
        g_ref[...] = g
        d_ref[...], nm_ref[...], nv_ref[...] = _adamw_math(w_ref[...], g, m_ref[...], v_ref[...])

    spec = pl.BlockSpec((tr, cols), lambda i: (i, 0))
    shp = jax.ShapeDtypeStruct((rows, cols), F32)
    return pl.pallas_call(
        body, name=name, grid=(rows // tr,),
        in_specs=[pl.BlockSpec((4, tr, cols), lambda i: (0, i, 0)), spec, spec, spec, spec],
        out_specs=[spec, spec, spec, spec], out_shape=[shp, shp, shp, shp],
        compiler_params=_cp("parallel"),
    )(from_chips, own, w, m, v)


def _sum_shards_adamw(recvs, own, w, m, v, name, comm=None):
    rows_p, cols = recvs[0].shape[1], recvs[0].shape[2]
    n_p = len(recvs)
    tr = _pick(rows_p, (176, 128, 64, 32, 16, 8))
    tiles = rows_p // tr

    def body(*refs):
        recv_refs = refs[:n_p]
        own_ref, w_ref, m_ref, v_ref, g_ref, d_ref, nm_ref, nv_ref = refs[n_p:]
        for j in range(n_p):
            @pl.when(pl.program_id(0) == j)
            def _():
                g = _slot_sum(recv_refs[j], own_ref, (tr, cols))
                g_ref[...] = g
                d_ref[...], nm_ref[...], nv_ref[...] = _adamw_math(w_ref[...], g, m_ref[...], v_ref[...])

    spec = pl.BlockSpec((tr, cols), lambda p_, i: (p_ * tiles + i, 0))
    shp = jax.ShapeDtypeStruct((rows_p * n_p, cols), F32)
    outs, couts = _call(
        body, name=name, grid=(n_p, tiles), ins=[*recvs, own, w, m, v],
        in_specs=[pl.BlockSpec((N_DEV, tr, cols), functools.partial(lambda p_, i, j: (0, jnp.where(p_ == j, i, 0), 0), j=j))
                  for j in range(n_p)] + [spec, spec, spec, spec],
        out_specs=[spec, spec, spec, spec], out_shape=[shp, shp, shp, shp],
        sem=("arbitrary", "arbitrary"), comm=comm)
    return outs if comm is None else (outs, couts)


def _mm(a, b, *, name, ta=False, tb=False, out_dtype=F32, out_dtype2=None, bias=None, addend=None,
        addend_scale=1.0, tm=1024, tn=1024, tk=1024, comm=None, out_rows=None, first_row=0, into=None):
    kdim, m = a.shape if ta else a.shape[::-1]
    n = b.shape[0] if tb else b.shape[1]
    tm = _pick(m, (tm, 1408, 1024, 768, 512, 256, 128))
    tn = _pick(n, (tn, 1408, 1024, 768, 512, 256, 128))
    tk = _pick(kdim, (tk, 1408, 1024, 768, 512, 256, 128))
    nk = kdim // tk
    a_spec = pl.BlockSpec((tk, tm), lambda i, j, k: (k, i)) if ta else pl.BlockSpec((tm, tk), lambda i, j, k: (i, k))
    b_spec = pl.BlockSpec((tn, tk), lambda i, j, k: (j, k)) if tb else pl.BlockSpec((tk, tn), lambda i, j, k: (k, j))
    ins, specs = [a, b], [a_spec, b_spec]
    if bias is not None:
        ins.append(bias)
        specs.append(pl.BlockSpec((1, tn), lambda i, j, k: (0, j)))
    if addend is not None:
        ins.append(addend)
        specs.append(pl.BlockSpec((tm, tn), lambda i, j, k: (i, j)))
    dims = (((0,) if ta else (1,), (1,) if tb else (0,)), ((), ()))
    has_bias, has_addend, two = bias is not None, addend is not None, out_dtype2 is not None

    def body(*refs):
        a_ref, b_ref = refs[0], refs[1]
        pos = 2
        bias_ref = addend_ref = None
        if has_bias:
            bias_ref = refs[pos]
            pos += 1
        if has_addend:
            addend_ref = refs[pos]
            pos += 1
        o_refs, acc_ref = refs[pos:-1], refs[-1]
        k = pl.program_id(2)

        @pl.when(k == 0)
        def _():
            acc_ref[...] = jnp.zeros_like(acc_ref)

        acc_ref[...] += _dot(a_ref[...], b_ref[...], dims)

        @pl.when(k == nk - 1)
        def _():
            r = acc_ref[...]
            if has_bias:
                r = r + bias_ref[...]
            if has_addend:
                r = r + addend_scale * addend_ref[...].astype(F32)
            for o_ref in o_refs:
                o_ref[...] = r.astype(o_ref.dtype)

    blk0 = first_row // tm
    dtypes = [out_dtype] + ([out_dtype2] if two else [])
    ospec = pl.BlockSpec((tm, tn), lambda i, j, k: (i + blk0, j))
    shapes = [jax.ShapeDtypeStruct((m if out_rows is None else out_rows, n), d) for d in dtypes]
    if into is not None:
        n_in = len(ins)
        outs = pl.pallas_call(
            lambda *refs: body(*refs[:n_in], *refs[n_in + len(into):]), name=name, grid=(m // tm, n // tn, nk),
            in_specs=specs + [pl.BlockSpec(memory_space=pl.ANY)] * len(into), out_specs=[ospec] * len(dtypes),
            out_shape=shapes, scratch_shapes=[pltpu.VMEM((tm, tn), F32)],
            input_output_aliases={n_in + j: j for j in range(len(into))},
            compiler_params=_cp("parallel", "parallel", "arbitrary"))(*ins, *into)
        return tuple(outs) if two else outs[0]
    outs, couts = _call(
        body, name=name, grid=(m // tm, n // tn, nk), ins=ins, in_specs=specs,
        out_specs=[ospec] * len(dtypes), out_shape=shapes,
        scratch_shapes=[pltpu.VMEM((tm, tn), F32)], sem=("parallel", "parallel", "arbitrary"), comm=comm)
    primary = tuple(outs) if two else outs[0]
    return (primary, couts) if comm is not None else primary


def _rope_lane_constants():
    inv_freq = np.float32(ROPE_THETA) ** (-np.arange(8, dtype=np.float32) * np.float32(2.0 / 16.0))
    lane = np.arange(LANES) % 64
    freq = np.where(lane < 16, inv_freq[lane % 8], 0.0).astype(np.float32)
    sign = np.where(lane < 8, -1.0, np.where(lane < 16, 1.0, 0.0)).astype(np.float32)
    return jnp.asarray(freq)[None, :], jnp.asarray(sign)[None, :]


def _prep(pos_col, x2, name, comm):
    t, d = x2.shape
    tr = _pick(t, (512, 256, 128))
    freq, sign = _rope_lane_constants()

    def body(pos_ref, freq_ref, sign_ref, x_ref, c_ref, s_ref, xb_ref):
        ang = pos_ref[...].astype(F32) * freq_ref[...]
        c_ref[...] = jnp.cos(ang)
        s_ref[...] = sign_ref[...] * jnp.sin(ang)
        xb_ref[...] = x_ref[...].astype(BF16)

    tab = pl.BlockSpec((tr, LANES), lambda i: (i, 0))
    return _call(
        body, name=name, grid=(t // tr,), ins=[pos_col, freq, sign, x2],
        in_specs=[pl.BlockSpec((tr, 1), lambda i: (i, 0)), pl.BlockSpec((1, LANES), lambda i: (0, 0)),
                  pl.BlockSpec((1, LANES), lambda i: (0, 0)), pl.BlockSpec((tr, d), lambda i: (i, 0))],
        out_specs=[tab, tab, pl.BlockSpec((tr, d), lambda i: (i, 0))],
        out_shape=[jax.ShapeDtypeStruct((t, LANES), F32), jax.ShapeDtypeStruct((t, LANES), F32),
                   jax.ShapeDtypeStruct((t, d), BF16)],
        sem=("parallel",), comm=comm)


def _swap8(t):
    width = t.shape[1]
    lane = jnp.bitwise_and(lax.broadcasted_iota(jnp.int32, t.shape, 1), 63)
    return jnp.where(lane < 8, pltpu.roll(t, width - 8, 1), jnp.where(lane < 16, pltpu.roll(t, 8, 1), 0.0))


def _rope(t, c, s):
    return t * c + _swap8(t) * s


def _rope_bwd(d, c, s):
    return d * c + _swap8(d * s)


def _tile4(a):
    return jnp.concatenate([a, a, a, a], axis=1)


def _attn_band(n, k_cur, k_prev, v_cur, v_prev, c_cur, s_cur, c_prev, s_prev):
    kband = jnp.concatenate([_rope(k_prev, c_prev, s_prev), _rope(k_cur, c_cur, s_cur)], axis=0)
    vband = jnp.concatenate([v_prev, v_cur], axis=0)
    qi = lax.broadcasted_iota(jnp.int32, (ATTN_BLOCK, 2 * ATTN_BLOCK), 0)
    kj = lax.broadcasted_iota(jnp.int32, (ATTN_BLOCK, 2 * ATTN_BLOCK), 1)
    dist = qi + ATTN_BLOCK - kj
    valid = (dist >= 0) & (dist < ATTN_BLOCK) & (n * ATTN_BLOCK - ATTN_BLOCK + kj >= 0)
    return (kband.astype(BF16), pltpu.roll(kband, 64, 1).astype(BF16),
            vband.astype(BF16), pltpu.roll(vband, 64, 1).astype(BF16), valid, kband)


def _attn_probs(raw, valid, sink, axis):
    s = jnp.where(valid, raw * ATTN_SCALE, NEG_BIG)
    m = jnp.maximum(jnp.max(s, axis=axis, keepdims=True), sink)
    p = jnp.exp(s - m)
    esink = jnp.exp(sink - m)
    z = jnp.sum(p, axis=axis, keepdims=True) + esink
    return p / z, esink / z


def _attn_valid_t(n):
    kj = lax.broadcasted_iota(jnp.int32, (2 * ATTN_BLOCK, ATTN_BLOCK), 0)
    qi = lax.broadcasted_iota(jnp.int32, (2 * ATTN_BLOCK, ATTN_BLOCK), 1)
    dist = qi + ATTN_BLOCK - kj
    return (dist >= 0) & (dist < ATTN_BLOCK) & (n * ATTN_BLOCK - ATTN_BLOCK + kj >= 0)


def _attn_specs(nb):
    def cur(col, width=KV_W):
        return pl.BlockSpec((ATTN_BLOCK, width), lambda n: (jnp.minimum(n, nb - 1), col))

    def prev(col):
        return pl.BlockSpec((ATTN_BLOCK, KV_W), lambda n: (jnp.maximum(n - 1, 0), col))

    ua_specs = [cur(0, ATTN_W), cur(4), prev(4), cur(5), prev(5)]
    tab_specs = [cur(0), cur(0), prev(0), prev(0)]
    return ua_specs, tab_specs


def _attn_fwd(ua, ctab, stab, sinks, name, comm=None):
    t = ua.shape[0]
    nb = t // ATTN_BLOCK
    ua_specs, tab_specs = _attn_specs(nb)

    def body(q_ref, kc_ref, kp_ref, vc_ref, vp_ref, cc_ref, sc_ref, cp_ref, sp_ref, sink_ref, o_ref, o_t_ref):
        n = pl.program_id(0)
        cc, sc = cc_ref[...], sc_ref[...]
        kb, kb_r, vb, vb_r, valid, _ = _attn_band(n, kc_ref[...], kp_ref[...], vc_ref[...], vp_ref[...],
                                                  cc, sc, cp_ref[...], sp_ref[...])
        qr = _rope(q_ref[...], _tile4(cc), _tile4(sc))
        lo = lax.broadcasted_iota(jnp.int32, (ATTN_BLOCK, LANES), 1) < 64
        heads = []
        for j in range(4):
            qj = qr[:, j * LANES:(j + 1) * LANES]
            for is_lo in (True, False):
                aligned = is_lo == (j < 2)
                qm = jnp.where(lo if is_lo else jnp.logical_not(lo), qj, 0.0).astype(BF16)
                raw = lax.dot_general(qm, kb if aligned else kb_r, _NT, preferred_element_type=F32)
                heads.append((raw, vb if aligned else vb_r, sink_ref[0, len(heads)]))
        halves = []
        for raw, vv, sink in heads:
            probs, _ = _attn_probs(raw, valid, sink, 1)
            halves.append(lax.dot_general(probs.astype(BF16), vv, _NN, preferred_element_type=F32))
        outs = [jnp.where(lo, halves[2 * j], halves[2 * j + 1]) for j in range(4)]
        o_ref[...] = jnp.concatenate(outs, axis=1).astype(o_ref.dtype)
        for j in range(4):
            o_t_ref[j * LANES:(j + 1) * LANES, :] = outs[j].T.astype(o_t_ref.dtype)

    return _call(
        body, name=name, grid=(nb,), ins=[ua, ua, ua, ua, ua, ctab, stab, ctab, stab, sinks],
        in_specs=ua_specs + tab_specs + [pl.BlockSpec(memory_space=pltpu.SMEM)],
        out_specs=[pl.BlockSpec((ATTN_BLOCK, ATTN_W), lambda n: (n, 0)),
                   pl.BlockSpec((ATTN_W, ATTN_BLOCK), lambda n: (0, n))],
        out_shape=[jax.ShapeDtypeStruct((t, ATTN_W), BF16), jax.ShapeDtypeStruct((ATTN_W, t), BF16)],
        sem=("parallel",), comm=comm)


def _attn_bwd(ua, d_out, ctab, stab, sinks, name, comm=None):
    t = ua.shape[0]
    nb = t // ATTN_BLOCK
    ua_specs, tab_specs = _attn_specs(nb)

    def body(q_ref, kc_ref, kp_ref, vc_ref, vp_ref, cc_ref, sc_ref, cp_ref, sp_ref, do_ref, sink_ref,
             dua_ref, dua_t_ref, dbias_ref, dsink_ref, dq_c, dk_c, dv_c, dq_n, dk_n, dv_n):
        n = pl.program_id(0)

        @pl.when(n == 0)
        def _():
            dq_c[...] = jnp.zeros_like(dq_c)
            dk_c[...] = jnp.zeros_like(dk_c)
            dv_c[...] = jnp.zeros_like(dv_c)
            dbias_ref[...] = jnp.zeros_like(dbias_ref)
            dsink_ref[...] = jnp.zeros_like(dsink_ref)

        @pl.when(n == nb)
        def _():
            dq_n[...] = jnp.zeros_like(dq_n)
            dk_n[...] = jnp.zeros_like(dk_n)
            dv_n[...] = jnp.zeros_like(dv_n)

        @pl.when(n < nb)
        def _():
            cc, sc = cc_ref[...], sc_ref[...]
            kb, kb_r, vb, vb_r, _, kb_f32 = _attn_band(n, kc_ref[...], kp_ref[...], vc_ref[...], vp_ref[...],
                                                       cc, sc, cp_ref[...], sp_ref[...])
            valid_t = _attn_valid_t(n)
            c4, s4 = _tile4(cc), _tile4(sc)
            qr = _rope(q_ref[...], c4, s4)
            do = do_ref[...].astype(F32)
            lane = lax.broadcasted_iota(jnp.int32, (ATTN_BLOCK, LANES), 1)
            lo = lane < 64
            lane_row = lax.broadcasted_iota(jnp.int32, (1, LANES), 1)
            k_t = {False: kb_f32.T.astype(BF16), True: pltpu.roll(kb_f32, 64, 1).T.astype(BF16)}
            heads = []
            for j in range(4):
                qj = qr[:, j * LANES:(j + 1) * LANES]
                doj = do[:, j * LANES:(j + 1) * LANES]
                for is_lo in (True, False):
                    aligned = is_lo == (j < 2)
                    msk = lo if is_lo else jnp.logical_not(lo)
                    kk = kb if aligned else kb_r
                    vv = vb if aligned else vb_r
                    qm = jnp.where(msk, qj, 0.0).astype(BF16)
                    dom = jnp.where(msk, doj, 0.0).astype(BF16)
                    heads.append(dict(
                        aligned=aligned, qm=qm, dom=dom, sink=sink_ref[0, len(heads)],
                        raw_t=lax.dot_general(kk, qm, _NT, preferred_element_type=F32),
                        dp_t=lax.dot_general(vv, dom, _NT, preferred_element_type=F32)))
            dk_band = jnp.zeros((2 * ATTN_BLOCK, LANES), F32)
            dv_band = jnp.zeros((2 * ATTN_BLOCK, LANES), F32)
            dsink = jnp.zeros((1, LANES), F32)
            for head, hd in enumerate(heads):
                probs_t, psink = _attn_probs(hd["raw_t"], valid_t, hd["sink"], 0)
                delta_t = jnp.sum(probs_t * hd["dp_t"], axis=0, keepdims=True)
                hd["ds_t"] = (probs_t * (hd["dp_t"] - delta_t) * ATTN_SCALE).astype(BF16)
                dsink = dsink + jnp.where(lane_row == head, -jnp.sum(psink * delta_t), 0.0)
                dk_h = lax.dot_general(hd["ds_t"], hd["qm"], _NN, preferred_element_type=F32)
                dv_h = lax.dot_general(probs_t.astype(BF16), hd["dom"], _NN, preferred_element_type=F32)
                if not hd["aligned"]:
                    dk_h = pltpu.roll(dk_h, 64, 1)
                    dv_h = pltpu.roll(dv_h, 64, 1)
                dk_band = dk_band + dk_h
                dv_band = dv_band + dv_h
            row_lo = lax.broadcasted_iota(jnp.int32, (LANES, ATTN_BLOCK), 0) < 64
            dq_t = [lax.dot_general(k_t[not hd["aligned"]], hd["ds_t"], _NN, preferred_element_type=F32)
                    for hd in heads]
            dqs = [jnp.where(row_lo, dq_t[2 * j], dq_t[2 * j + 1]).T for j in range(4)]
            dq_n[...] = _rope_bwd(jnp.concatenate(dqs, axis=1), c4, s4)
            dk_n[...] = dk_band
            dv_n[...] = dv_band
            dsink_ref[...] += dsink

        dk_prev = _rope_bwd(dk_c[...] + dk_n[0:ATTN_BLOCK, :], cp_ref[...], sp_ref[...])
        dv_prev = dv_c[...] + dv_n[0:ATTN_BLOCK, :]
        full = jnp.concatenate([dq_c[...], dk_prev, dv_prev], axis=1)
        dua_ref[...] = full.astype(dua_ref.dtype)
        for j in range(UA_W // LANES):
            dua_t_ref[j * LANES:(j + 1) * LANES, :] = full[:, j * LANES:(j + 1) * LANES].T.astype(dua_t_ref.dtype)
        dbias_ref[...] += jnp.sum(full, axis=0, keepdims=True)
        dq_c[...] = dq_n[...]
        dk_c[...] = dk_n[ATTN_BLOCK:, :]
        dv_c[...] = dv_n[ATTN_BLOCK:, :]

    return _call(
        body, name=name, grid=(nb + 1,), ins=[ua, ua, ua, ua, ua, ctab, stab, ctab, stab, d_out, sinks],
        in_specs=ua_specs + tab_specs + [
            pl.BlockSpec((ATTN_BLOCK, ATTN_W), lambda n: (jnp.minimum(n, nb - 1), 0)),
            pl.BlockSpec(memory_space=pltpu.SMEM)],
        out_specs=[pl.BlockSpec((ATTN_BLOCK, UA_W), lambda n: (jnp.maximum(n - 1, 0), 0)),
                   pl.BlockSpec((UA_W, ATTN_BLOCK), lambda n: (0, jnp.maximum(n - 1, 0))),
                   pl.BlockSpec((1, UA_W), lambda n: (0, 0)),
                   pl.BlockSpec((1, LANES), lambda n: (0, 0))],
        out_shape=[jax.ShapeDtypeStruct((t, UA_W), BF16), jax.ShapeDtypeStruct((UA_W, t), BF16),
                   jax.ShapeDtypeStruct((1, UA_W), F32),
                   jax.ShapeDtypeStruct((1, LANES), F32)],
        scratch_shapes=[pltpu.VMEM((ATTN_BLOCK, ATTN_W), F32), pltpu.VMEM((ATTN_BLOCK, KV_W), F32),
                        pltpu.VMEM((ATTN_BLOCK, KV_W), F32), pltpu.VMEM((ATTN_BLOCK, ATTN_W), F32),
                        pltpu.VMEM((2 * ATTN_BLOCK, KV_W), F32), pltpu.VMEM((2 * ATTN_BLOCK, KV_W), F32)],
        sem=("arbitrary",), comm=comm)


def _tri_mats():
    r = lax.broadcasted_iota(jnp.int32, (HGRN_CHUNK, LANES), 0)
    c = lax.broadcasted_iota(jnp.int32, (HGRN_CHUNK, LANES), 1)
    lower = ((c <= r) & (c < HGRN_CHUNK)).astype(F32)
    upper = ((c >= r) & (c < HGRN_CHUNK)).astype(F32)
    return lower, upper


def _tri_apply(tri, g):
    pad = jnp.concatenate([g, jnp.zeros_like(g)], axis=0)
    return lax.dot_general(tri, pad, _NN, precision=lax.Precision.HIGHEST, preferred_element_type=F32)


def _sub_masks():
    s = lax.broadcasted_iota(jnp.int32, (HGRN_CHUNK, LANES), 0)
    tt = lax.broadcasted_iota(jnp.int32, (HGRN_CHUNK, LANES), 1)
    return [(tt >= HGRN_SUB * i) & (tt < HGRN_SUB * (i + 1)) & (s <= tt) for i in range(HGRN_CHUNK // HGRN_SUB)]


def _hgrn_gates(hq, hf, lb_ref, b_scr):
    lb = _sig(lb_ref[0:1, :] - lb_ref[1:2, :])
    q = hq * _sig(hq)
    sg = _sig(hf)
    f = lb + (1.0 - lb) * sg
    k = 1.0 - f
    lower, _ = _tri_mats()
    b = _tri_apply(lower, jnp.log(f))
    b_scr[...] = b
    nsub = HGRN_CHUNK // HGRN_SUB
    starts = [jnp.zeros((1, HG_W), F32)] + [b_scr[HGRN_SUB * i - 1:HGRN_SUB * i, :] for i in range(1, nsub)]
    pq = jnp.concatenate([jnp.broadcast_to(p, (HGRN_SUB, HG_W)) for p in starts], axis=0)
    b_last = b_scr[HGRN_CHUNK - 1:HGRN_CHUNK, :]
    e_q = jnp.exp(b - pq)
    e_k = [jnp.exp(jnp.minimum(p - b, EXP_CLAMP)) for p in starts]
    e_b = jnp.exp(b)
    e_bl = jnp.exp(b_last - b)
    e_last = jnp.exp(b_last)
    return q, sg, f, k, lb, e_q, e_k, e_b, e_bl, e_last


def _sub_masks_ts():
    tt = lax.broadcasted_iota(jnp.int32, (HGRN_CHUNK, LANES), 0)
    s = lax.broadcasted_iota(jnp.int32, (HGRN_CHUNK, LANES), 1)
    return [(tt >= HGRN_SUB * i) & (tt < HGRN_SUB * (i + 1)) & (s <= tt) for i in range(HGRN_CHUNK // HGRN_SUB)]


def _masked_sum(blocks, masks, axis):
    step = HGRN_CHUNK if axis == 0 else LANES
    acc = jnp.zeros((HGRN_CHUNK, LANES), F32)
    for i, msk in enumerate(masks):
        blk = blocks[step * i:step * (i + 1), :] if axis == 0 else blocks[:, step * i:step * (i + 1)]
        acc = acc + jnp.where(msk, blk, 0.0)
    return acc


def _store_transposed(out_t_ref, chunk_rows):
    width = chunk_rows[0].shape[1]
    if len(chunk_rows) == 1:
        groups = [jnp.concatenate([chunk_rows[0], jnp.zeros_like(chunk_rows[0])], axis=0)]
    else:
        groups = [jnp.concatenate(chunk_rows[g:g + 2], axis=0) for g in range(0, len(chunk_rows), 2)]
    for g, rows in enumerate(groups):
        for c in range(width // LANES):
            tile = rows[:, c * LANES:(c + 1) * LANES].T.astype(out_t_ref.dtype)
            if len(chunk_rows) == 1:
                out_t_ref[c * LANES:(c + 1) * LANES, :] = tile[:, 0:HGRN_CHUNK]
            else:
                out_t_ref[c * LANES:(c + 1) * LANES, g * LANES:(g + 1) * LANES] = tile


def _hgrn_chunk_inputs(j, hq_ref, hf_ref, hi_ref, hg_ref, lb_ref, b_scr):
    rows = slice(j * HGRN_CHUNK, (j + 1) * HGRN_CHUNK)
    hq, hf, v, hg = hq_ref[rows, :], hf_ref[rows, :], hi_ref[rows, :], hg_ref[rows, :]
    q, sg, f, k, lb, e_q, e_k, e_b, e_bl, e_last = _hgrn_gates(hq, hf, lb_ref, b_scr.at[j])
    return dict(rows=rows, hq=hq, v=v, hg=hg, q=q, sg=sg, f=f, k=k, lb=lb, e_q=e_q, e_k=e_k, e_b=e_b, e_bl=e_bl,
                e_last=e_last, qt=q * e_q, qb=q * e_b, kd=k * e_bl, khat=[k * e for e in e_k])


def _hgrn_fwd(uh, lb_raw, norm_g, name, comm=None):
    t = uh.shape[0]
    nc = t // HGRN_CHUNK
    cps = _pick(nc, (HGRN_CHUNKS_PER_STEP, 2, 1))
    rows_step = cps * HGRN_CHUNK

    def body(hq_ref, hf_ref, hi_ref, hg_ref, lb_ref, ng_ref, r_ref, r_t_ref, o_ref, st_out_ref, st_ref, b_scr):
        @pl.when(pl.program_id(0) == 0)
        def _():
            st_ref[...] = jnp.zeros_like(st_ref)

        masks = _sub_masks_ts()
        ng = ng_ref[...]
        zpad = jnp.zeros((HGRN_CHUNK, LANES), F32)
        heads = [slice(h * LANES, (h + 1) * LANES) for h in range(4)]
        chunks = [_hgrn_chunk_inputs(j, hq_ref, hf_ref, hi_ref, hg_ref, lb_ref, b_scr) for j in range(cps)]
        for ch in chunks:
            ch["scores"] = [_dot3(ch["qt"][:, sl],
                                  jnp.concatenate([x for kh in ch["khat"] for x in (kh[:, sl], zpad)], axis=0), _NT)
                            for sl in heads]
        for j, ch in enumerate(chunks):
            o_heads, y_heads = [], []
            for h, sl in enumerate(heads):
                a_ts = _masked_sum(ch["scores"][h], masks, 1)
                vh = ch["v"][:, sl].astype(BF16)
                v_pad = jnp.concatenate([vh, jnp.zeros_like(vh)], axis=0)
                o_intra = lax.dot_general(a_ts.astype(BF16), v_pad, _NN, preferred_element_type=F32)
                st = st_ref[h]
                st_out_ref[j, h] = st
                o_inter = _dot(ch["qb"][:, sl], st, _NT)
                st_ref[h] = st * ch["e_last"][:, sl] + _dot(vh, ch["kd"][:, sl], _TN)
                oh = o_intra + o_inter
                rs = lax.rsqrt(jnp.mean(oh * oh, axis=1, keepdims=True) + RMS_EPS)
                o_heads.append(oh)
                y_heads.append(oh * rs * ng)
            hg = ch["hg"]
            o_ref[ch["rows"], :] = jnp.concatenate(o_heads, axis=1)
            ch["r"] = jnp.concatenate(y_heads, axis=1) * (hg * _sig(hg))
            r_ref[ch["rows"], :] = ch["r"].astype(r_ref.dtype)
        _store_transposed(r_t_ref, [ch["r"] for ch in chunks])

    col = lambda j: pl.BlockSpec((rows_step, HG_W), lambda c: (c, j))
    return _call(
        body, name=name, grid=(nc // cps,), ins=[uh, uh, uh, uh, lb_raw, norm_g],
        in_specs=[col(0), col(1), col(2), col(3),
                  pl.BlockSpec((2, HG_W), lambda c: (0, 0)), pl.BlockSpec((1, LANES), lambda c: (0, 0))],
        out_specs=[pl.BlockSpec((rows_step, HG_W), lambda c: (c, 0)),
                   pl.BlockSpec((HG_W, rows_step), lambda c: (0, c)),
                   pl.BlockSpec((rows_step, HG_W), lambda c: (c, 0)),
                   pl.BlockSpec((cps, 4, LANES, LANES), lambda c: (c, 0, 0, 0))],
        out_shape=[jax.ShapeDtypeStruct((t, HG_W), BF16), jax.ShapeDtypeStruct((HG_W, t), BF16),
                   jax.ShapeDtypeStruct((t, HG_W), F32), jax.ShapeDtypeStruct((nc, 4, LANES, LANES), F32)],
        scratch_shapes=[pltpu.VMEM((4, LANES, LANES), F32), pltpu.VMEM((cps, HGRN_CHUNK, HG_W), F32)],
        sem=("arbitrary",), comm=comm)


def _hgrn_bwd(uh, o_pre, d_r, states, lb_raw, norm_g, name, comm=None):
    t = uh.shape[0]
    nc = t // HGRN_CHUNK
    cps = _pick(nc, (HGRN_CHUNKS_PER_STEP, 2, 1))
    ns = nc // cps
    rows_step = cps * HGRN_CHUNK
    nsub = HGRN_CHUNK // HGRN_SUB

    def body(hq_ref, hf_ref, hi_ref, hg_ref, o_ref, dr_ref, st_in_ref, lb_ref, ng_ref,
             duh_ref, duh_t_ref, dbias_ref, dng_ref, dlb_ref, dst_ref, b_scr, dlb_acc):
        i = pl.program_id(0)

        @pl.when(i == 0)
        def _():
            dst_ref[...] = jnp.zeros_like(dst_ref)
            dbias_ref[...] = jnp.zeros_like(dbias_ref)
            dng_ref[...] = jnp.zeros_like(dng_ref)
            dlb_acc[...] = jnp.zeros_like(dlb_acc)

        masks_st = _sub_masks()
        masks_ts = _sub_masks_ts()
        ng = ng_ref[...]
        zpad = jnp.zeros((HGRN_CHUNK, LANES), F32)
        _, upper = _tri_mats()
        heads = [slice(h * LANES, (h + 1) * LANES) for h in range(4)]
        row = lax.broadcasted_iota(jnp.int32, (HGRN_CHUNK, HG_W), 0)

        chunks = [_hgrn_chunk_inputs(j, hq_ref, hf_ref, hi_ref, hg_ref, lb_ref, b_scr) for j in range(cps)]
        dng = jnp.zeros((1, LANES), F32)
        for ch in chunks:
            o = o_ref[ch["rows"], :]
            dr = dr_ref[ch["rows"], :].astype(F32)
            hg = ch["hg"]
            sgg = _sig(hg)
            dy = dr * (hg * sgg)
            do_h, y_h = [], []
            for sl in heads:
                oh = o[:, sl]
                rs = lax.rsqrt(jnp.mean(oh * oh, axis=1, keepdims=True) + RMS_EPS)
                y_h.append(oh * rs * ng)
                dng = dng + jnp.sum(dy[:, sl] * oh * rs, axis=0, keepdims=True)
                w = dy[:, sl] * ng
                do_h.append(rs * (w - oh * (rs * rs) * jnp.mean(w * oh, axis=1, keepdims=True)))
            ch["do"] = do_h
            ch["dhg"] = dr * jnp.concatenate(y_h, axis=1) * _dsilu(hg, sgg)

        for ch in chunks:
            ch["kst"], ch["kpad"], ch["qt_pad"], ch["v_b"], ch["do_pad"] = [], [], [], [], []
            ch["ats"], ch["d_at"], ch["d_a"] = [], [], []
            for h, sl in enumerate(heads):
                kst = jnp.concatenate([kh[:, sl] for kh in ch["khat"]], axis=0)
                kpad = jnp.concatenate([x for kh in ch["khat"] for x in (kh[:, sl], zpad)], axis=0)
                qt_pad = jnp.concatenate([ch["qt"][:, sl], zpad], axis=0)
                vh = ch["v"][:, sl].astype(BF16)
                v_pad = jnp.concatenate([vh, jnp.zeros_like(vh)], axis=0)
                do_b = ch["do"][h].astype(BF16)
                do_pad = jnp.concatenate([do_b, jnp.zeros_like(do_b)], axis=0)
                ch["kst"].append(kst)
                ch["kpad"].append(kpad)
                ch["qt_pad"].append(qt_pad)
                ch["v_b"].append(vh)
                ch["do_pad"].append(do_pad)
                ch["ats"].append(_dot3(kst, qt_pad, _NT))
                ch["d_at"].append(lax.dot_general(vh, do_pad, _NT, preferred_element_type=F32))
                ch["d_a"].append(lax.dot_general(do_b, v_pad, _NT, preferred_element_type=F32))

        for ch in chunks:
            ch["d_kst"], ch["d_qt"], ch["dv"] = [], [], []
            for h in range(4):
                at = _masked_sum(ch["ats"][h], masks_st, 0)
                d_ats = jnp.concatenate([jnp.where(m, ch["d_at"][h], 0.0) for m in masks_st], axis=0)
                d_a_cat = jnp.concatenate([jnp.where(m, ch["d_a"][h], 0.0) for m in masks_ts], axis=1)
                ch["d_kst"].append(_dot3(d_ats, ch["qt_pad"][h], _NN))
                ch["d_qt"].append(_dot3(d_a_cat, ch["kpad"][h], _NN))
                ch["dv"].append(lax.dot_general(at.astype(BF16), ch["do_pad"][h], _NN, preferred_element_type=F32))

        for j in reversed(range(cps)):
            ch = chunks[j]
            q, k, sg, f, lb = ch["q"], ch["k"], ch["sg"], ch["f"], ch["lb"]
            dq_h, dk_h, dv_h, extra_h = [], [], [], []
            for h, sl in enumerate(heads):
                st_prev = st_in_ref[j, h]
                d_st = dst_ref[h]
                d_st_b = d_st.astype(BF16)
                do_b = ch["do_pad"][h][0:HGRN_CHUNK, :]
                kd, e_last = ch["kd"][:, sl], ch["e_last"][:, sl]
                dv = ch["dv"][h] + _dot(kd, d_st_b, _NT)
                d_qb = _dot(do_b, st_prev, _NN)
                d_kd = lax.dot_general(ch["v_b"][h], d_st_b, _NN, preferred_element_type=F32)
                extra_h.append(jnp.sum(st_prev * d_st, axis=0, keepdims=True) * e_last
                               + jnp.sum(kd * d_kd, axis=0, keepdims=True))
                dst_ref[h] = d_st * e_last + _dot(do_b, ch["qb"][:, sl], _TN)
                dq_h.append(ch["d_qt"][h] * ch["e_q"][:, sl] + d_qb * ch["e_b"][:, sl])
                dkk = d_kd * ch["e_bl"][:, sl]
                for s_ in range(nsub):
                    dkk = dkk + ch["d_kst"][h][HGRN_CHUNK * s_:HGRN_CHUNK * (s_ + 1), :] * ch["e_k"][s_][:, sl]
                dk_h.append(dkk)
                dv_h.append(dv)
            dq = jnp.concatenate(dq_h, axis=1)
            dk = jnp.concatenate(dk_h, axis=1)
            dv = jnp.concatenate(dv_h, axis=1)
            extra = jnp.concatenate(extra_h, axis=1)
            db = q * dq - k * dk + jnp.where(row == HGRN_CHUNK - 1, extra, 0.0)
            dg = _tri_apply(upper, db)
            df = dg / f - dk
            dhf = df * (1.0 - lb) * sg * (1.0 - sg)
            dhq = dq * _dsilu(ch["hq"], _sig(ch["hq"]))
            full = jnp.concatenate([dhq, dhf, dv, ch["dhg"]], axis=1)
            duh_ref[ch["rows"], :] = full.astype(duh_ref.dtype)
            ch["full"] = full
            dbias_ref[...] += jnp.sum(full, axis=0, keepdims=True)
            dlb_acc[...] += jnp.sum(df * (1.0 - sg), axis=0, keepdims=True)
        dng_ref[...] += dng
        _store_transposed(duh_t_ref, [ch["full"] for ch in chunks])

        @pl.when(i == ns - 1)
        def _():
            lb = chunks[0]["lb"]
            d_a0 = dlb_acc[...] * lb * (1.0 - lb)
            r8 = lax.broadcasted_iota(jnp.int32, (8, HG_W), 0)
            dlb_ref[...] = jnp.where(r8 == 0, d_a0, jnp.where(r8 == 1, -d_a0, 0.0))

    col = lambda j: pl.BlockSpec((rows_step, HG_W), lambda i: (ns - 1 - i, j))
    return _call(
        body, name=name, grid=(ns,), ins=[uh, uh, uh, uh, o_pre, d_r, states, lb_raw, norm_g],
        in_specs=[col(0), col(1), col(2), col(3), col(0), col(0),
                  pl.BlockSpec((cps, 4, LANES, LANES), lambda i: (ns - 1 - i, 0, 0, 0)),
                  pl.BlockSpec((2, HG_W), lambda i: (0, 0)), pl.BlockSpec((1, LANES), lambda i: (0, 0))],
        out_specs=[pl.BlockSpec((rows_step, UH_W), lambda i: (ns - 1 - i, 0)),
                   pl.BlockSpec((UH_W, rows_step), lambda i: (0, ns - 1 - i)),
                   pl.BlockSpec((1, UH_W), lambda i: (0, 0)),
                   pl.BlockSpec((1, LANES), lambda i: (0, 0)),
                   pl.BlockSpec((8, HG_W), lambda i: (0, 0))],
        out_shape=[jax.ShapeDtypeStruct((t, UH_W), BF16), jax.ShapeDtypeStruct((UH_W, t), BF16),
                   jax.ShapeDtypeStruct((1, UH_W), F32),
                   jax.ShapeDtypeStruct((1, LANES), F32), jax.ShapeDtypeStruct((8, HG_W), F32)],
        scratch_shapes=[pltpu.VMEM((4, LANES, LANES), F32), pltpu.VMEM((cps, HGRN_CHUNK, HG_W), F32),
                        pltpu.VMEM((1, HG_W), F32)],
        sem=("arbitrary",), comm=comm)


def _ln_bwd_math(dy, xhat, rstd, g):
    dxh = dy * g
    return rstd * (dxh - jnp.mean(dxh, axis=1, keepdims=True)
                   - xhat * jnp.mean(dxh * xhat, axis=1, keepdims=True))


def _mm_rows(a, b, extras, *, name, epilogue, out_shape, out_specs, tb=False, tm=512, tk=1408):
    m, kdim = a.shape
    n = b.shape[0] if tb else b.shape[1]
    tm = _pick(m, (tm, 256, 128))
    tk = _pick(kdim, (tk, 1408, 1024, 768, 512, 256, 128))
    nk = kdim // tk
    b_spec = pl.BlockSpec((n, tk), lambda i, k: (0, k)) if tb else pl.BlockSpec((tk, n), lambda i, k: (k, 0))
    dims = _NT if tb else _NN
    n_ex, n_out = len(extras), len(out_shape)

    def body(*refs):
        a_ref, b_ref = refs[0], refs[1]
        ex_refs = refs[2:2 + n_ex]
        o_refs = refs[2 + n_ex:2 + n_ex + n_out]
        acc_ref = refs[-1]
        i, k = pl.program_id(0), pl.program_id(1)

        @pl.when(k == 0)
        def _():
            acc_ref[...] = jnp.zeros_like(acc_ref)

        acc_ref[...] += _dot(a_ref[...], b_ref[...], dims)

        @pl.when(k == nk - 1)
        def _():
            epilogue(acc_ref[...], ex_refs, o_refs, i == 0)

    return pl.pallas_call(
        body, name=name, grid=(m // tm, nk),
        in_specs=[pl.BlockSpec((tm, tk), lambda i, k: (i, k)), b_spec] + [sp for _, sp in extras],
        out_specs=list(out_specs), out_shape=list(out_shape),
        scratch_shapes=[pltpu.VMEM((tm, n), F32)],
        compiler_params=_cp("arbitrary", "arbitrary"),
    )(a, b, *[arr for arr, _ in extras])


def _rows_specs(tm, d):
    row = pl.BlockSpec((tm, d), lambda i, k: (i, 0))
    vec = pl.BlockSpec((1, d), lambda i, k: (0, 0))
    col = pl.BlockSpec((tm, 1), lambda i, k: (i, 0))
    return row, vec, col


def _mm_ln_fwd(a, b, addend, g, beta, name, tm=512):
    t, d = addend.shape
    tm = _pick(t, (tm, 256, 128))
    row, vec, col = _rows_specs(tm, d)

    def epilogue(acc, ex, outs, first):
        z = acc + ex[0][...]
        mu = jnp.mean(z, axis=1, keepdims=True)
        zc = z - mu
        rstd = lax.rsqrt(jnp.mean(zc * zc, axis=1, keepdims=True) + LN_EPS)
        xhat = zc * rstd
        h = xhat * ex[1][...] + ex[2][...]
        outs[0][...] = h
        outs[1][...] = h.astype(BF16)
        outs[2][...] = xhat
        outs[3][...] = rstd

    return _mm_rows(a, b, [(addend, row), (g, vec), (beta, vec)], name=name, epilogue=epilogue, tm=tm,
                    out_shape=[jax.ShapeDtypeStruct((t, d), F32), jax.ShapeDtypeStruct((t, d), BF16),
                               jax.ShapeDtypeStruct((t, d), F32), jax.ShapeDtypeStruct((t, 1), F32)],
                    out_specs=[row, row, row, col])


CONV_RB = 32
HALO = 8


def _sum8(x):
    acc = x[0:8]
    for r in range(8, x.shape[0], 8):
        acc = acc + x[r:r + 8]
    return acc


FFN_TILE = 256
FFN_COLS = 256


def _rows_before(win, k):
    return pltpu.roll(win, k, 0)[HALO:]


def _rows_after(win, k):
    n = win.shape[0]
    return pltpu.roll(win, n - k, 0)[0:n - HALO]


def _resident(shape):
    return pl.BlockSpec(shape, lambda i: (0,) * len(shape), pipeline_mode=pl.Buffered(1))


def _ffn_fwd(h1b, h1, w_up_t, conv_w, conv_b, w_down, target, ln2_g, ln2_b, name, comm=None):
    t, d = h1.shape
    tr = _pick(t, (FFN_TILE, 128))
    nblk = D_FF // FFN_COLS
    rb = CONV_RB

    def body(a_ref, wup_ref, cw_ref, cb_ref, wd_ref, h1_ref, tgt_ref, g_ref, b_ref,
             u2_ref, hm_ref, dz_ref, dg_ref, db_ref, loss_ref, ext):
        i = pl.program_id(0)

        @pl.when(i == 0)
        def _():
            ext[0:HALO, :] = jnp.zeros((HALO, D_FF), F32)
            dg_ref[...] = jnp.zeros_like(dg_ref)
            db_ref[...] = jnp.zeros_like(db_ref)
            loss_ref[...] = jnp.zeros_like(loss_ref)

        a = a_ref[...]
        for c in range(nblk):
            cs = slice(c * FFN_COLS, (c + 1) * FFN_COLS)
            vs = slice(D_FF + c * FFN_COLS, D_FF + (c + 1) * FFN_COLS)
            gate_pre = lax.dot_general(a, wup_ref[cs, :], _NT, preferred_element_type=F32)
            u2_ref[:, cs] = gate_pre
            ext[HALO:, cs] = gate_pre
            u2_ref[:, vs] = lax.dot_general(a, wup_ref[vs, :], _NT, preferred_element_type=F32)
        acc = jnp.zeros((tr, d), F32)
        for c in range(nblk):
            cs = slice(c * FFN_COLS, (c + 1) * FFN_COLS)
            for sub in range(FFN_COLS // LANES):
                ln = slice(c * FFN_COLS + sub * LANES, c * FFN_COLS + (sub + 1) * LANES)
                vl = slice(D_FF + c * FFN_COLS + sub * LANES, D_FF + c * FFN_COLS + (sub + 1) * LANES)
                w0, w1, w2, bb = cw_ref[0:1, ln], cw_ref[1:2, ln], cw_ref[2:3, ln], cb_ref[:, ln]
                for r0 in range(0, tr, rb):
                    win = ext[r0:r0 + HALO + rb, ln]
                    gate = _rows_before(win, 2) * w0 + _rows_before(win, 1) * w1 + win[HALO:] * w2 + bb
                    hm_ref[r0:r0 + rb, ln] = (gate * _sig(gate) * u2_ref[r0:r0 + rb, vl]).astype(hm_ref.dtype)
            acc = acc + lax.dot_general(hm_ref[:, cs], wd_ref[cs, :], _NN, preferred_element_type=F32)
        ext[0:HALO, :] = ext[tr:tr + HALO, :]

        z = acc + ALPHA * h1_ref[...]
        gg = g_ref[...]
        mu = jnp.mean(z, axis=1, keepdims=True)
        zc = z - mu
        rstd = lax.rsqrt(jnp.mean(zc * zc, axis=1, keepdims=True) + LN_EPS)
        xhat = zc * rstd
        err = xhat * gg + b_ref[...] - tgt_ref[...]
        loss_ref[...] += 0.5 * jnp.sum(jnp.mean(err * err, axis=1, keepdims=True))
        dy = err * (1.0 / d)
        dz_ref[...] = _ln_bwd_math(dy, xhat, rstd, gg)
        dg_ref[...] += jnp.sum(dy * xhat, axis=0, keepdims=True)
        db_ref[...] += jnp.sum(dy, axis=0, keepdims=True)

    row = lambda w: pl.BlockSpec((tr, w), lambda i: (i, 0))
    vec = pl.BlockSpec((1, d), lambda i: (0, 0))
    return _call(
        body, name=name, grid=(t // tr,),
        ins=[h1b, w_up_t, conv_w, conv_b, w_down, h1, target, ln2_g, ln2_b],
        in_specs=[row(d), _resident((2 * D_FF, d)), _resident((3, D_FF)), _resident((1, D_FF)),
                  _resident((D_FF, d)), row(d), row(d), vec, vec],
        out_specs=[row(2 * D_FF), row(D_FF), row(d), vec, vec, pl.BlockSpec((1, LANES), lambda i: (0, 0))],
        out_shape=[jax.ShapeDtypeStruct((t, 2 * D_FF), F32), jax.ShapeDtypeStruct((t, D_FF), BF16),
                   jax.ShapeDtypeStruct((t, d), F32), jax.ShapeDtypeStruct((1, d), F32),
                   jax.ShapeDtypeStruct((1, d), F32), jax.ShapeDtypeStruct((1, LANES), F32)],
        scratch_shapes=[pltpu.VMEM((tr + HALO, D_FF), F32)],
        sem=("arbitrary",), comm=comm)


def _ffn_bwd(dz2, u2, w_down, w_up_t, conv_w, conv_b, xhat1, rstd1, ln1_g, name, comm=None):
    t, d = dz2.shape
    tr = _pick(t, (FFN_TILE, 128))
    nt = t // tr
    hb = tr // HALO
    nblk = D_FF // FFN_COLS
    rb = CONV_RB

    def body(dz2_ref, dz2_next_ref, u2_ref, gp_prev_ref, wd_ref, wup_ref, cw_ref, cb_ref, xhat_ref, rstd_ref,
             g1_ref, du_ref, dz1_ref, dw_ref, dcb_ref, dg1_ref, db1_ref, head, dh_s, dg_s):
        i = pl.program_id(0)

        @pl.when(i == 0)
        def _():
            dg_s[tr:, :] = jnp.zeros((HALO, D_FF), F32)
            dw_ref[...] = jnp.zeros_like(dw_ref)
            dcb_ref[...] = jnp.zeros_like(dcb_ref)
            dg1_ref[...] = jnp.zeros_like(dg1_ref)
            db1_ref[...] = jnp.zeros_like(db1_ref)

        dz2 = dz2_ref[...]

        @pl.when(i == 0)
        def _():
            dz2_b = dz2.astype(BF16)
            for c in range(nblk):
                cs = slice(c * FFN_COLS, (c + 1) * FFN_COLS)
                dh_s[:, cs] = lax.dot_general(dz2_b, wd_ref[cs, :], _NT, preferred_element_type=F32)

        dz2_next = dz2_next_ref[...].astype(BF16)
        dh_next = [lax.dot_general(dz2_next, wd_ref[c * FFN_COLS:(c + 1) * FFN_COLS, :], _NT,
                                   preferred_element_type=F32) for c in range(nblk)]
        head[0:HALO, :] = jnp.where(i == nt - 1, 0.0, gp_prev_ref[...])
        head[HALO:, :] = u2_ref[0:rb, 0:D_FF]

        acc = jnp.zeros((tr, d), F32)
        for blk in range(nblk):
            for c in range(blk * FFN_COLS // LANES, (blk + 1) * FFN_COLS // LANES):
                ln = slice(c * LANES, (c + 1) * LANES)
                vl = slice(D_FF + c * LANES, D_FF + (c + 1) * LANES)
                w0, w1, w2, bb = cw_ref[0:1, ln], cw_ref[1:2, ln], cw_ref[2:3, ln], cb_ref[:, ln]
                acc_b = jnp.zeros((8, LANES), F32)
                acc_w = [jnp.zeros((8, LANES), F32) for _ in range(3)]
                for r0 in range(0, tr, rb):
                    win = head[:, ln] if r0 == 0 else u2_ref[r0 - HALO:r0 + rb, ln]
                    g_m2, g_m1, g_0 = _rows_before(win, 2), _rows_before(win, 1), win[HALO:]
                    gate = g_m2 * w0 + g_m1 * w1 + g_0 * w2 + bb
                    sg = _sig(gate)
                    dh = dh_s[r0:r0 + rb, ln]
                    dgate = dh * u2_ref[r0:r0 + rb, vl] * _dsilu(gate, sg)
                    dg_s[r0:r0 + rb, ln] = dgate
                    du_ref[r0:r0 + rb, vl] = (dh * (gate * sg)).astype(du_ref.dtype)
                    acc_b = acc_b + _sum8(dgate)
                    acc_w[0] = acc_w[0] + _sum8(dgate * g_m2)
                    acc_w[1] = acc_w[1] + _sum8(dgate * g_m1)
                    acc_w[2] = acc_w[2] + _sum8(dgate * g_0)
                dcb_ref[:, ln] += jnp.sum(acc_b, axis=0, keepdims=True)
                for j in range(3):
                    dw_ref[j:j + 1, ln] += jnp.sum(acc_w[j], axis=0, keepdims=True)
                for r0 in range(0, tr, rb):
                    win = dg_s[r0:r0 + rb + HALO, ln]
                    d_gp = _rows_after(win, 2) * w0 + _rows_after(win, 1) * w1 + win[0:rb] * w2
                    du_ref[r0:r0 + rb, ln] = d_gp.astype(du_ref.dtype)
            cs = slice(blk * FFN_COLS, (blk + 1) * FFN_COLS)
            vs = slice(D_FF + blk * FFN_COLS, D_FF + (blk + 1) * FFN_COLS)
            acc = acc + lax.dot_general(du_ref[:, cs], wup_ref[cs, :], _NN, preferred_element_type=F32)
            acc = acc + lax.dot_general(du_ref[:, vs], wup_ref[vs, :], _NN, preferred_element_type=F32)
        dg_s[tr:, :] = dg_s[0:HALO, :]
        for c in range(nblk):
            dh_s[:, c * FFN_COLS:(c + 1) * FFN_COLS] = dh_next[c]
        dy = acc + ALPHA * dz2
        xh = xhat_ref[...]
        dz1_ref[...] = _ln_bwd_math(dy, xh, rstd_ref[...], g1_ref[...])
        dg1_ref[...] += jnp.sum(dy * xh, axis=0, keepdims=True)
        db1_ref[...] += jnp.sum(dy, axis=0, keepdims=True)

    rev = lambda w: pl.BlockSpec((tr, w), lambda i: (nt - 1 - i, 0))
    vec = pl.BlockSpec((1, d), lambda i: (0, 0))
    return _call(
        body, name=name, grid=(nt,),
        ins=[dz2, dz2, u2, u2, w_down, w_up_t, conv_w, conv_b, xhat1, rstd1, ln1_g],
        in_specs=[rev(d), pl.BlockSpec((tr, d), lambda i: (jnp.maximum(nt - 2 - i, 0), 0)), rev(2 * D_FF),
                  pl.BlockSpec((HALO, D_FF), lambda i: (jnp.maximum((nt - 1 - i) * hb - 1, 0), 0)),
                  _resident((D_FF, d)), _resident((2 * D_FF, d)), _resident((3, D_FF)), _resident((1, D_FF)),
                  rev(d), pl.BlockSpec((tr, 1), lambda i: (nt - 1 - i, 0)), vec],
        out_specs=[rev(2 * D_FF), rev(d), pl.BlockSpec((8, D_FF), lambda i: (0, 0)),
                   pl.BlockSpec((1, D_FF), lambda i: (0, 0)), vec, vec],
        out_shape=[jax.ShapeDtypeStruct((t, 2 * D_FF), BF16), jax.ShapeDtypeStruct((t, d), F32),
                   jax.ShapeDtypeStruct((8, D_FF), F32), jax.ShapeDtypeStruct((1, D_FF), F32),
                   jax.ShapeDtypeStruct((1, d), F32), jax.ShapeDtypeStruct((1, d), F32)],
        scratch_shapes=[pltpu.VMEM((HALO + rb, D_FF), F32), pltpu.VMEM((tr, D_FF), F32),
                        pltpu.VMEM((tr + HALO, D_FF), F32)],
        sem=("arbitrary",), comm=comm)


def _adamw(w, g, m, v, name):
    rows, cols = w.shape
    tr = _pick(rows, (256, 128, 64, 32, 16, 8))

    def body(w_ref, g_ref, m_ref, v_ref, d_ref, nm_ref, nv_ref):
        d_ref[...], nm_ref[...], nv_ref[...] = _adamw_math(w_ref[...], g_ref[...], m_ref[...], v_ref[...])

    spec = pl.BlockSpec((tr, cols), lambda i: (i, 0))
    shp = jax.ShapeDtypeStruct((rows, cols), F32)
    return pl.pallas_call(
        body, name=name, grid=(rows // tr,),
        in_specs=[spec, spec, spec, spec], out_specs=[spec, spec, spec], out_shape=[shp, shp, shp],
        compiler_params=_cp("parallel"),
    )(w, g, m, v)


def _pad_rows(a, rows):
    return jnp.pad(a, ((0, rows - a.shape[0]), (0, 0)))


SMALL_LAYOUT = (("ln1_g", 1024), ("ln1_b", 1024), ("b_in", 2816), ("sinks", 8), ("hgrn_lb", 1024),
                ("hgrn_norm_g", 128), ("ln2_g", 1024), ("ln2_b", 1024), ("conv_b", 2816), ("loss", 1))
SMALL_SHAPES = {"ln1_g": (1, 1024), "ln1_b": (1, 1024), "b_in": (1, 2816), "sinks": (1, 8), "hgrn_lb": (2, 512),
                "hgrn_norm_g": (1, 128), "ln2_g": (1, 1024), "ln2_b": (1, 1024), "conv_b": (1, 2816),
                "loss": (1,)}


def _pack_small(parts):
    rows = []
    for name, size in SMALL_LAYOUT:
        flat = parts[name].reshape(-1).astype(F32)
        padded = -(-size // LANES) * LANES
        rows.append(jnp.pad(flat, (0, padded - size)).reshape(-1, LANES))
    return _pad_rows(jnp.concatenate(rows, axis=0), SMALL_ROWS)


def _unpack_small(pack):
    out, r = {}, 0
    for name, size in SMALL_LAYOUT:
        nrows = -(-size // LANES)
        out[name] = pack[r:r + nrows].reshape(-1)[:size].reshape(SMALL_SHAPES[name])
        r += nrows
    return out


def _own(full, rows):
    return lax.dynamic_slice_in_dim(full, _me() * rows, rows, axis=0)


def kernel(x, positions, ln1_g, ln1_b, w_in, b_in, sinks, hgrn_lb, hgrn_norm_g, w_o, ln2_g, ln2_b, w_up, conv_w, conv_b, w_down, loss_target, m_ln1_g, m_ln1_b, m_w_in, m_b_in, m_sinks, m_hgrn_lb, m_hgrn_norm_g, m_w_o, m_ln2_g, m_ln2_b, m_w_up, m_conv_w, m_conv_b, m_w_down, v_ln1_g, v_ln1_b, v_w_in, v_b_in, v_sinks, v_hgrn_lb, v_hgrn_norm_g, v_w_o, v_ln2_g, v_ln2_b, v_w_up, v_conv_w, v_conv_b, v_w_down):
    t = x.shape[1]
    x2 = x[0]
    target = loss_target[0]
    pos_col = positions.reshape(t, 1)

    w_in_t_s = w_in[0].T.astype(BF16)
    w_up_t_s = w_up[0].T.astype(BF16)
    w_o_s = w_o[0].astype(BF16)
    w_down_s = w_down[0].astype(BF16)
    (ctab, stab, xb), (w_in_t_g, cw_g) = _prep(
        pos_col, x2, "prep_ag_w_in", _Comm([{"kind": "gather", "arr": w_in_t_s}, {"kind": "gather", "arr": _pad_rows(conv_w[0], 8)}]))
    w_in_t = w_in_t_g.reshape(D_FF, D_MODEL)
    w_a_t, w_h_t = w_in_t[:UA_W], w_in_t[UA_W:]
    conv_w_f = cw_g[:, 0:3].transpose(1, 0, 2).reshape(3, D_FF)

    ua = _mm(xb, w_a_t, tb=True, bias=b_in[:, :UA_W], name="fwd_in_attn")
    uh = _mm(xb, w_h_t, tb=True, bias=b_in[:, UA_W:], name="fwd_in_hgrn")
    half_up = SHARD_UP // 2
    (a_out, a_out_t), (w_o_g, w_up_half) = _attn_fwd(
        ua, ctab, stab, sinks, "attn_fwd",
        comm=_Comm([{"kind": "gather", "arr": w_o_s},
                    {"kind": "gather", "arr": w_up_t_s, "rows": (0, half_up), "dst_rows": SHARD_UP}]))
    (r_out, r_out_t, o_pre, states), (w_up_t_g, w_down_g) = _hgrn_fwd(
        uh, hgrn_lb, hgrn_norm_g, "hgrn_fwd",
        comm=_Comm([{"kind": "gather", "arr": w_up_t_s, "rows": (half_up, half_up), "dst_rows": SHARD_UP,
                     "dst_first": half_up, "into": w_up_half},
                    {"kind": "gather", "arr": w_down_s}]))
    w_down_f = w_down_g.reshape(D_FF, D_MODEL)
    w_o_f = w_o_g.reshape(D_MODEL, D_MODEL)
    w_up_t = w_up_t_g.reshape(2 * D_FF, D_MODEL)
    z1 = _mm(a_out, w_o_f[:ATTN_W], addend=x2, addend_scale=ALPHA, name="fwd_o_attn")
    h1, h1b, xhat1, rstd1 = _mm_ln_fwd(r_out, w_o_f[ATTN_W:], z1, ln1_g, ln1_b, "fwd_o_hgrn_ln1")
    u2, hmid, dz2, d_ln2_g, d_ln2_b, loss_part = _ffn_fwd(h1b, h1, w_up_t, conv_w_f, conv_b, w_down_f, target,
                                                         ln2_g, ln2_b, "ffn_fwd")[0]

    d_w_down, d_w_down_b = _mm(hmid, dz2, ta=True, out_dtype2=BF16, tm=1408, tk=1024, name="bwd_down_dw")
    (d_u2, dz1, d_conv_w8, d_conv_b, d_ln1_g, d_ln1_b), (recv_down,) = _ffn_bwd(
        dz2, u2, w_down_f, w_up_t, conv_w_f, conv_b, xhat1, rstd1, ln1_g, "ffn_bwd",
        comm=_Comm([{"kind": "exchange", "arr": d_w_down_b.reshape(N_DEV, SHARD_DOWN, D_MODEL)}]))
    d_w_up_t, d_w_up_t_b = _mm(d_u2, h1b, ta=True, out_dtype2=BF16, tm=1408, tk=1024, name="bwd_up_dw")
    d_a = _mm(dz1, w_o_f[:ATTN_W], tb=True, name="bwd_o_dx_attn")
    d_r = _mm(dz1, w_o_f[ATTN_W:], tb=True, name="bwd_o_dx_hgrn")
    d_w_o_part = _mm(a_out_t, dz1, out_dtype2=BF16, tm=ATTN_W, out_rows=D_MODEL, name="bwd_o_dw_attn")
    d_w_o, d_w_o_b = _mm(r_out_t, dz1, out_dtype2=BF16, tm=HG_W, out_rows=D_MODEL, first_row=ATTN_W,
                         into=d_w_o_part, name="bwd_o_dw_hgrn")
    d_w_up_x = d_w_up_t_b.reshape(N_DEV, SHARD_UP, D_MODEL)
    half = SHARD_UP // 2
    d_cw_x = d_conv_w8.reshape(8, N_DEV, SHARD_IN).transpose(1, 0, 2)
    (d_ua, d_ua_t, d_bias_a, d_sinks), (recv_up_half, recv_cw) = _attn_bwd(
        ua, d_a, ctab, stab, sinks, "attn_bwd",
        comm=_Comm([{"kind": "exchange", "arr": d_w_up_x, "rows": (0, half), "dst_rows": SHARD_UP},
                    {"kind": "exchange", "arr": d_cw_x}]))
    (d_uh, d_uh_t, d_bias_h, d_norm_g, d_lb8), (recv_up, recv_o) = _hgrn_bwd(
        uh, o_pre, d_r, states, hgrn_lb, hgrn_norm_g, "hgrn_bwd",
        comm=_Comm([{"kind": "exchange", "arr": d_w_up_x, "rows": (half, half), "dst_rows": SHARD_UP,
                     "dst_first": half, "into": recv_up_half},
                    {"kind": "exchange", "arr": d_w_o_b.reshape(N_DEV, SHARD_O, D_MODEL)}]))
    d_w_in_part = _mm(d_ua_t, xb, out_dtype2=BF16, tm=UA_W, tk=t, out_rows=D_FF, name="bwd_in_dw_attn")
    d_w_in_t, d_w_in_t_b = _mm(d_uh_t, xb, out_dtype2=BF16, tm=256, tk=t, out_rows=D_FF, first_row=UA_W,
                               into=d_w_in_part, name="bwd_in_dw_hgrn")
    small_local = _pack_small({
        "ln1_g": d_ln1_g, "ln1_b": d_ln1_b, "b_in": jnp.concatenate([d_bias_a, d_bias_h], axis=1),
        "sinks": d_sinks[:, :8], "hgrn_lb": d_lb8[0:2], "hgrn_norm_g": d_norm_g, "ln2_g": d_ln2_g,
        "ln2_b": d_ln2_b, "conv_b": d_conv_b, "loss": loss_part[:, :1]})
    d_w_in_x = d_w_in_t_b.reshape(N_DEV, SHARD_IN, D_MODEL)
    res_up, (from_sibling,) = _sum_shards_adamw(
        [recv_up], _own(d_w_up_t, SHARD_UP), w_up[0].T, m_w_up[0].T, v_w_up[0].T, "adamw_w_up",
        comm=_Comm([{"kind": "pair4", "arr": d_w_in_x}]))
    res_up = [r.T for r in res_up]
    own_in, chip_part = _pair_reduce(from_sibling, d_w_in_t, "pair_reduce_w_in")
    dx, (from_chips, small_g) = _mm(d_uh, w_h_t, addend=dz1, addend_scale=ALPHA, name="bwd_in_dx_hgrn",
                                    comm=_Comm([{"kind": "chips3", "arr": chip_part},
                                                {"kind": "gather", "arr": small_local}]))
    dx = _mm(d_ua, w_a_t, addend=dx, tk=768, name="bwd_in_dx_attn")

    res_in = [r.T for r in _chip_sum_adamw(from_chips, own_in, w_in[0].T, m_w_in[0].T, v_w_in[0].T, "adamw_w_in")]
    res_o = _sum_shards_adamw([recv_o], _own(d_w_o, SHARD_O), w_o[0], m_w_o[0], v_w_o[0], "adamw_w_o")
    res_down = _sum_shards_adamw([recv_down], _own(d_w_down, SHARD_DOWN), w_down[0], m_w_down[0], v_w_down[0],
                                 "adamw_w_down")
    g_cw = _sum_slots(recv_cw, "sum_conv_w")
    cw8 = lambda a: _pad_rows(a, 8)
    res_cw = (g_cw,) + tuple(_adamw(cw8(conv_w[0]), g_cw, cw8(m_conv_w[0]), cw8(v_conv_w[0]), "adamw_conv_w"))
    big = {"w_in": [r[None] for r in res_in], "w_up": [r[None] for r in res_up],
           "w_o": [r[None] for r in res_o], "w_down": [r[None] for r in res_down],
           "conv_w": [r[None, 0:3] for r in res_cw]}

    small_sum = _sum_slots(small_g, "ar_small_sum")
    gs = _unpack_small(small_sum)
    loss = gs["loss"][0]
    zero1 = jnp.zeros((1,), F32)
    w_small = _pack_small({"ln1_g": ln1_g, "ln1_b": ln1_b, "b_in": b_in, "sinks": sinks, "hgrn_lb": hgrn_lb,
                           "hgrn_norm_g": hgrn_norm_g, "ln2_g": ln2_g, "ln2_b": ln2_b, "conv_b": conv_b,
                           "loss": zero1})
    m_small = _pack_small({"ln1_g": m_ln1_g, "ln1_b": m_ln1_b, "b_in": m_b_in, "sinks": m_sinks,
                           "hgrn_lb": m_hgrn_lb, "hgrn_norm_g": m_hgrn_norm_g, "ln2_g": m_ln2_g,
                           "ln2_b": m_ln2_b, "conv_b": m_conv_b, "loss": zero1})
    v_small = _pack_small({"ln1_g": v_ln1_g, "ln1_b": v_ln1_b, "b_in": v_b_in, "sinks": v_sinks,
                           "hgrn_lb": v_hgrn_lb, "hgrn_norm_g": v_hgrn_norm_g, "ln2_g": v_ln2_g,
                           "ln2_b": v_ln2_b, "conv_b": v_conv_b, "loss": zero1})
    small = [gs] + [_unpack_small(p) for p in _adamw(w_small, small_sum, m_small, v_small, "adamw_small")]

    order = ["ln1_g", "ln1_b", "w_in", "b_in", "sinks", "hgrn_lb", "hgrn_norm_g", "w_o", "ln2_g", "ln2_b",
             "w_up", "conv_w", "conv_b", "w_down"]

    def pick(idx):
        return [big[n][idx] if n in big else small[idx][n] for n in order]

    return (loss, dx[None], *pick(0), *pick(1), *pick(2), *pick(3))
```

```python
import functools

import jax
import jax.numpy as jnp
import numpy as np
from jax import lax
from jax.experimental import pallas as pl
from jax.experimental.pallas import tpu as pltpu

F32 = jnp.float32
BF16 = jnp.bfloat16

N_DEV = 8
D_MODEL = 1024
D_FF = 2816
ATTN_W = 512
KV_W = 128
UA_W = ATTN_W + 2 * KV_W
UH_W = 2048
HG_W = 512
ATTN_BLOCK = 128
HGRN_CHUNK = 64
HGRN_SUB = 16
HGRN_CHUNKS_PER_STEP = 4
EXP_CLAMP = 85.0
NEG_BIG = -1e30
LN_EPS = 1e-5
RMS_EPS = 1e-6
ALPHA = 2.0 ** 0.25
ATTN_SCALE = 0.125
ROPE_THETA = 500000.0

ADAM_LR = 0.001
ADAM_B1 = 0.9
ADAM_B2 = 0.999
ADAM_EPS = 1e-08
ADAM_WD = 0.01
ADAM_STEP = 10

LANES = 128
VMEM_LIMIT_BYTES = 56 * 1024 * 1024

SHARD_IN = D_FF // N_DEV
SHARD_UP = 2 * D_FF // N_DEV
SHARD_O = D_MODEL // N_DEV
SHARD_DOWN = D_FF // N_DEV
SMALL_ROWS = 88

_MESH = pl.DeviceIdType.MESH
_NT = (((1,), (1,)), ((), ()))
_NN = (((1,), (0,)), ((), ()))
_TN = (((0,), (0,)), ((), ()))


def _cp(*sem):
    if sem:
        return pltpu.CompilerParams(dimension_semantics=sem, vmem_limit_bytes=VMEM_LIMIT_BYTES)
    return pltpu.CompilerParams(vmem_limit_bytes=VMEM_LIMIT_BYTES)


def _sig(x):
    return 0.5 * jnp.tanh(0.5 * x) + 0.5


def _dsilu(x, s):
    return s * (1.0 + x * (1.0 - s))


def _dot(a, b, dims):
    return lax.dot_general(a.astype(BF16), b.astype(BF16), dims, preferred_element_type=F32)


def _split(a):
    hi = a.astype(BF16)
    return hi, (a - hi.astype(F32)).astype(BF16)


def _dot3(a, b, dims):
    ah, al = _split(a)
    bh, bl = _split(b)
    d = functools.partial(lax.dot_general, dimension_numbers=dims, preferred_element_type=F32)
    return d(ah, bh) + (d(ah, bl) + d(al, bh))


def _pick(n, pref):
    for t in pref:
        if t <= n and n % t == 0:
            return t
    return n


def _my_coords():
    return lax.axis_index("x"), lax.axis_index("y"), lax.axis_index("c")


def _peer(k):
    x, y, c = _my_coords()
    return (1 - x if k & 4 else x, 1 - y if k & 2 else y, 1 - c if k & 1 else c)


def _me():
    x, y, c = _my_coords()
    return 4 * x + 2 * y + c


class _Comm:
    def __init__(self, items):
        self.items = []
        for it in items:
            arr = it["arr"]
            full = arr.shape[0] if it["kind"] == "gather" else arr.shape[1]
            first, count = it.get("rows", (0, full))
            self.items.append(dict(kind=it["kind"], arr=arr, first=first, count=count,
                                   dst_rows=it.get("dst_rows", count), dst_first=it.get("dst_first", 0),
                                   into=it.get("into")))
        self.n = len(self.items)
        self.arrays = [it["arr"] for it in self.items]
        self.intos = [(a, it["into"]) for a, it in enumerate(self.items) if it["into"] is not None]

    def out_shapes(self):
        return [jax.ShapeDtypeStruct((4 if it["kind"] in ("pair4", "chips3") else N_DEV, it["dst_rows"],
                                      it["arr"].shape[-1]), it["arr"].dtype) for it in self.items]

    def specs(self, n=None):
        return [pl.BlockSpec(memory_space=pl.ANY)] * (self.n if n is None else n)

    def scratch(self):
        return [pltpu.SemaphoreType.DMA(((N_DEV - 1) * self.n,)), pltpu.SemaphoreType.DMA(((N_DEV - 1) * self.n,)),
                pltpu.SemaphoreType.DMA((self.n,))]

    def _src(self, a, ref, dev):
        it = self.items[a]
        blk = ref if it["kind"] == "gather" else ref.at[dev]
        return blk.at[pl.ds(it["first"], it["count"])]

    def _dst(self, a, ref, slot):
        it = self.items[a]
        return ref.at[slot].at[pl.ds(it["dst_first"], it["count"])]

    def _copy(self, a, k, src, dst, sems, me, slot):
        other = jnp.bitwise_xor(me, k)
        idx = a * (N_DEV - 1) + k - 1
        return pltpu.make_async_remote_copy(
            src_ref=self._src(a, src, other), dst_ref=self._dst(a, dst, me if slot == "mine" else other),
            send_sem=sems[0].at[idx], recv_sem=sems[1].at[idx], device_id=_peer(k), device_id_type=_MESH)

    def _pass_on(self, a, k, dst, sems, me):
        slot = self._dst(a, dst, jnp.bitwise_xor(me, k))
        idx = a * (N_DEV - 1) + k
        return pltpu.make_async_remote_copy(
            src_ref=slot, dst_ref=slot, send_sem=sems[0].at[idx], recv_sem=sems[1].at[idx],
            device_id=_peer(1), device_id_type=_MESH)

    def _part(self, a, r, src, dst, sems, me):
        it = self.items[a]
        idx = a * (N_DEV - 1) + r
        if it["kind"] == "pair4":
            k, slot = 1, jnp.bitwise_xor(jnp.bitwise_xor(me, 1), 2 * r)
        else:
            k, slot = 2 * r, r
        return pltpu.make_async_remote_copy(
            src_ref=src.at[slot].at[pl.ds(it["first"], it["count"])], dst_ref=self._dst(a, dst, r),
            send_sem=sems[0].at[idx], recv_sem=sems[1].at[idx], device_id=_peer(k), device_id_type=_MESH)

    def _parts(self, a):
        return range(4) if self.items[a]["kind"] == "pair4" else range(1, 4)

    def _local(self, a, src, dst, sems, me):
        return pltpu.make_async_copy(self._src(a, src, me), self._dst(a, dst, me), sems[2].at[a])

    def start(self, srcs, dsts, sems):
        me = _me()
        for a, (src, dst) in enumerate(zip(srcs, dsts)):
            if self.items[a]["kind"] in ("pair4", "chips3"):
                for r in self._parts(a):
                    self._part(a, r, src, dst, sems, me).start()
                continue
            direct = (1, 2, 4, 6) if self.items[a]["kind"] == "gather" else range(1, N_DEV)
            self._local(a, src, dst, sems, me).start()
            for k in direct:
                self._copy(a, k, src, dst, sems, me, "mine").start()

    def wait(self, srcs, dsts, sems):
        me = _me()
        for a, (src, dst) in enumerate(zip(srcs, dsts)):
            if self.items[a]["kind"] in ("pair4", "chips3"):
                for r in self._parts(a):
                    self._part(a, r, src, dst, sems, me).wait_recv()
                for r in self._parts(a):
                    self._part(a, r, src, dst, sems, me).wait_send()
                continue
            if self.items[a]["kind"] == "gather":
                for k in (2, 4, 6):
                    self._copy(a, k, src, dst, sems, me, "theirs").wait_recv()
                    self._pass_on(a, k, dst, sems, me).start()
                for k in (1, 3, 5, 7):
                    self._copy(a, k, src, dst, sems, me, "theirs").wait_recv()
                for k in (1, 2, 4, 6):
                    self._copy(a, k, src, dst, sems, me, "mine").wait_send()
                for k in (2, 4, 6):
                    self._pass_on(a, k, dst, sems, me).wait_send()
            else:
                for k in range(1, N_DEV):
                    self._copy(a, k, src, dst, sems, me, "theirs").wait_recv()
                for k in range(1, N_DEV):
                    self._copy(a, k, src, dst, sems, me, "mine").wait_send()
            self._local(a, src, dst, sems, me).wait()


def _call(body, *, name, grid, ins, in_specs, out_specs, out_shape, scratch_shapes=(), sem, comm=None):
    n_in, n_out, n_scr = len(ins), len(out_shape), len(scratch_shapes)
    if comm is None:
        outs = pl.pallas_call(
            body, name=name, grid=grid, in_specs=list(in_specs), out_specs=list(out_specs),
            out_shape=list(out_shape), scratch_shapes=list(scratch_shapes), compiler_params=_cp(*sem))(*ins)
        return list(outs), []
    nc, n_into = comm.n, len(comm.intos)

    def hosted(*refs):
        pos = n_in
        c_in = refs[pos:pos + nc]
        pos += nc + n_into
        outs = refs[pos:pos + n_out]
        pos += n_out
        c_out = refs[pos:pos + nc]
        pos += nc
        scr = refs[pos:pos + n_scr]
        sems = refs[pos + n_scr:]
        ids = [pl.program_id(d) for d in range(len(grid))]
        first = functools.reduce(jnp.logical_and, [i == 0 for i in ids])
        last = functools.reduce(jnp.logical_and, [i == g - 1 for i, g in zip(ids, grid)])

        @pl.when(first)
        def _():
            comm.start(c_in, c_out, sems)

        body(*refs[:n_in], *outs, *scr)

        @pl.when(last)
        def _():
            comm.wait(c_in, c_out, sems)

    aliases = {n_in + nc + j: n_out + a for j, (a, _) in enumerate(comm.intos)}
    outs = pl.pallas_call(
        hosted, name=name, grid=grid, in_specs=list(in_specs) + comm.specs() + comm.specs(n_into),
        out_specs=list(out_specs) + comm.specs(), out_shape=list(out_shape) + comm.out_shapes(),
        scratch_shapes=list(scratch_shapes) + comm.scratch(), input_output_aliases=aliases,
        compiler_params=_cp(*(["arbitrary"] * len(grid))))(*ins, *comm.arrays, *[arr for _, arr in comm.intos])
    return list(outs[:n_out]), list(outs[n_out:])


def _sum_slots(gathered, name):
    _, rows, cols = gathered.shape

    def body(g_ref, out_ref):
        acc = g_ref[0]
        for s in range(1, N_DEV):
            acc = acc + g_ref[s]
        out_ref[...] = acc

    return pl.pallas_call(
        body, name=name,
        out_shape=jax.ShapeDtypeStruct((rows, cols), F32),
        compiler_params=_cp(),
    )(gathered)


def _slot_sum(recv_ref, own_ref, shape):
    me = _me()
    acc = jnp.zeros(shape, F32)
    for s in range(N_DEV):
        acc = acc + jnp.where(me == s, own_ref[...], recv_ref[s].astype(F32))
    return acc


def _adamw_math(w, g, m, v):
    nm = ADAM_B1 * m + (1.0 - ADAM_B1) * g
    nv = ADAM_B2 * v + (1.0 - ADAM_B2) * (g * g)
    m_hat = nm / (1.0 - ADAM_B1 ** ADAM_STEP)
    v_hat = nv / (1.0 - ADAM_B2 ** ADAM_STEP)
    return -ADAM_LR * (m_hat / (jnp.sqrt(v_hat) + ADAM_EPS) + ADAM_WD * w), nm, nv


def _pair_reduce(from_sibling, mine, name):
    _, rows, cols = from_sibling.shape
    tr = _pick(rows, (176, 128, 64, 32, 16, 8))
    tiles = rows // tr
    table = jnp.bitwise_xor(_me(), jnp.arange(0, N_DEV, 2, dtype=jnp.int32))

    def body(tbl_ref, sib_ref, mine_ref, own_ref, send_ref):
        r = pl.program_id(1)
        total = mine_ref[...] + sib_ref[0].astype(F32)
        send_ref[0] = jnp.where(r == 0, 0.0, total).astype(BF16)

        @pl.when(r == 0)
        def _():
            own_ref[...] = total

    grid_spec = pltpu.PrefetchScalarGridSpec(
        num_scalar_prefetch=1, grid=(tiles, 4),
        in_specs=[pl.BlockSpec((1, tr, cols), lambda i, r, tbl: (r, i, 0)),
                  pl.BlockSpec((tr, cols), lambda i, r, tbl: (tbl[r] * tiles + i, 0))],
        out_specs=[pl.BlockSpec((tr, cols), lambda i, r, tbl: (i, 0)),
                   pl.BlockSpec((1, tr, cols), lambda i, r, tbl: (r, i, 0))])
    return pl.pallas_call(
        body, name=name, grid_spec=grid_spec,
        out_shape=[jax.ShapeDtypeStruct((rows, cols), F32), jax.ShapeDtypeStruct((4, rows, cols), BF16)],
        compiler_params=_cp("arbitrary", "arbitrary"),
    )(table, from_sibling, mine)


def _chip_sum_adamw(from_chips, own, w, m, v, name):
    _, rows, cols = from_chips.shape
    tr = _pick(rows, (176, 128, 64, 32, 16, 8))

    def body(recv_ref, own_ref, w_ref, m_ref, v_ref, g_ref, d_ref, nm_ref, nv_ref):
        g = own_ref[...]
        for r in range(1, 4):
            g = g + recv_ref[r].astype(F32)
        g_ref[...] = g
        d_ref[...], nm_ref[...], nv_ref[...] = _adamw_math(w_ref[...], g, m_ref[...], v_ref[...])

    spec = pl.BlockSpec((tr, cols), lambda i: (i, 0))
    shp = jax.ShapeDtypeStruct((rows, cols), F32)
    return pl.pallas_call(
        body, name=name, grid=(rows // tr,),
        in_specs=[pl.BlockSpec((4, tr, cols), lambda i: (0, i, 0)), spec, spec, spec, spec],
        out_specs=[spec, spec, spec, spec], out_shape=[shp, shp, shp, shp],
        compiler_params=_cp("parallel"),
    )(from_chips, own, w, m, v)


def _sum_shards_adamw(recvs, own, w, m, v, name, comm=None):
    rows_p, cols = recvs[0].shape[1], recvs[0].shape[2]
    n_p = len(recvs)
    tr = _pick(rows_p, (176, 128, 64, 32, 16, 8))
    tiles = rows_p // tr

    def body(*refs):
        recv_refs = refs[:n_p]
        own_ref, w_ref, m_ref, v_ref, g_ref, d_ref, nm_ref, nv_ref = refs[n_p:]
        for j in range(n_p):
            @pl.when(pl.program_id(0) == j)
            def _():
                g = _slot_sum(recv_refs[j], own_ref, (tr, cols))
                g_ref[...] = g
                d_ref[...], nm_ref[...], nv_ref[...] = _adamw_math(w_ref[...], g, m_ref[...], v_ref[...])

    spec = pl.BlockSpec((tr, cols), lambda p_, i: (p_ * tiles + i, 0))
    shp = jax.ShapeDtypeStruct((rows_p * n_p, cols), F32)
    outs, couts = _call(
        body, name=name, grid=(n_p, tiles), ins=[*recvs, own, w, m, v],
        in_specs=[pl.BlockSpec((N_DEV, tr, cols), functools.partial(lambda p_, i, j: (0, jnp.where(p_ == j, i, 0), 0), j=j))
                  for j in range(n_p)] + [spec, spec, spec, spec],
        out_specs=[spec, spec, spec, spec], out_shape=[shp, shp, shp, shp],
        sem=("arbitrary", "arbitrary"), comm=comm)
    return outs if comm is None else (outs, couts)


def _mm(a, b, *, name, ta=False, tb=False, out_dtype=F32, out_dtype2=None, bias=None, addend=None,
        addend_scale=1.0, tm=1024, tn=1024, tk=1024, comm=None, out_rows=None, first_row=0, into=None):
    kdim, m = a.shape if ta else a.shape[::-1]
    n = b.shape[0] if tb else b.shape[1]
    tm = _pick(m, (tm, 1408, 1024, 768, 512, 256, 128))
    tn = _pick(n, (tn, 1408, 1024, 768, 512, 256, 128))
    tk = _pick(kdim, (tk, 1408, 1024, 768, 512, 256, 128))
    nk = kdim // tk
    a_spec = pl.BlockSpec((tk, tm), lambda i, j, k: (k, i)) if ta else pl.BlockSpec((tm, tk), lambda i, j, k: (i, k))
    b_spec = pl.BlockSpec((tn, tk), lambda i, j, k: (j, k)) if tb else pl.BlockSpec((tk, tn), lambda i, j, k: (k, j))
    ins, specs = [a, b], [a_spec, b_spec]
    if bias is not None:
        ins.append(bias)
        specs.append(pl.BlockSpec((1, tn), lambda i, j, k: (0, j)))
    if addend is not None:
        ins.append(addend)
        specs.append(pl.BlockSpec((tm, tn), lambda i, j, k: (i, j)))
    dims = (((0,) if ta else (1,), (1,) if tb else (0,)), ((), ()))
    has_bias, has_addend, two = bias is not None, addend is not None, out_dtype2 is not None

    def body(*refs):
        a_ref, b_ref = refs[0], refs[1]
        pos = 2
        bias_ref = addend_ref = None
        if has_bias:
            bias_ref = refs[pos]
            pos += 1
        if has_addend:
            addend_ref = refs[pos]
            pos += 1
        o_refs, acc_ref = refs[pos:-1], refs[-1]
        k = pl.program_id(2)

        @pl.when(k == 0)
        def _():
            acc_ref[...] = jnp.zeros_like(acc_ref)

        acc_ref[...] += _dot(a_ref[...], b_ref[...], dims)

        @pl.when(k == nk - 1)
        def _():
            r = acc_ref[...]
            if has_bias:
                r = r + bias_ref[...]
            if has_addend:
                r = r + addend_scale * addend_ref[...].astype(F32)
            for o_ref in o_refs:
                o_ref[...] = r.astype(o_ref.dtype)

    blk0 = first_row // tm
    dtypes = [out_dtype] + ([out_dtype2] if two else [])
    ospec = pl.BlockSpec((tm, tn), lambda i, j, k: (i + blk0, j))
    shapes = [jax.ShapeDtypeStruct((m if out_rows is None else out_rows, n), d) for d in dtypes]
    if into is not None:
        n_in = len(ins)
        outs = pl.pallas_call(
            lambda *refs: body(*refs[:n_in], *refs[n_in + len(into):]), name=name, grid=(m // tm, n // tn, nk),
            in_specs=specs + [pl.BlockSpec(memory_space=pl.ANY)] * len(into), out_specs=[ospec] * len(dtypes),
            out_shape=shapes, scratch_shapes=[pltpu.VMEM((tm, tn), F32)],
            input_output_aliases={n_in + j: j for j in range(len(into))},
            compiler_params=_cp("parallel", "parallel", "arbitrary"))(*ins, *into)
        return tuple(outs) if two else outs[0]
    outs, couts = _call(
        body, name=name, grid=(m // tm, n // tn, nk), ins=ins, in_specs=specs,
        out_specs=[ospec] * len(dtypes), out_shape=shapes,
        scratch_shapes=[pltpu.VMEM((tm, tn), F32)], sem=("parallel", "parallel", "arbitrary"), comm=comm)
    primary = tuple(outs) if two else outs[0]
    return (primary, couts) if comm is not None else primary


def _rope_lane_constants():
    inv_freq = np.float32(ROPE_THETA) ** (-np.arange(8, dtype=np.float32) * np.float32(2.0 / 16.0))
    lane = np.arange(LANES) % 64
    freq = np.where(lane < 16, inv_freq[lane % 8], 0.0).astype(np.float32)
    sign = np.where(lane < 8, -1.0, np.where(lane < 16, 1.0, 0.0)).astype(np.float32)
    return jnp.asarray(freq)[None, :], jnp.asarray(sign)[None, :]


def _prep(pos_col, x2, name, comm):
    t, d = x2.shape
    tr = _pick(t, (512, 256, 128))
    freq, sign = _rope_lane_constants()

    def body(pos_ref, freq_ref, sign_ref, x_ref, c_ref, s_ref, xb_ref):
        ang = pos_ref[...].astype(F32) * freq_ref[...]
        c_ref[...] = jnp.cos(ang)
        s_ref[...] = sign_ref[...] * jnp.sin(ang)
        xb_ref[...] = x_ref[...].astype(BF16)

    tab = pl.BlockSpec((tr, LANES), lambda i: (i, 0))
    return _call(
        body, name=name, grid=(t // tr,), ins=[pos_col, freq, sign, x2],
        in_specs=[pl.BlockSpec((tr, 1), lambda i: (i, 0)), pl.BlockSpec((1, LANES), lambda i: (0, 0)),
                  pl.BlockSpec((1, LANES), lambda i: (0, 0)), pl.BlockSpec((tr, d), lambda i: (i, 0))],
        out_specs=[tab, tab, pl.BlockSpec((tr, d), lambda i: (i, 0))],
        out_shape=[jax.ShapeDtypeStruct((t, LANES), F32), jax.ShapeDtypeStruct((t, LANES), F32),
                   jax.ShapeDtypeStruct((t, d), BF16)],
        sem=("parallel",), comm=comm)


def _swap8(t):
    width = t.shape[1]
    lane = jnp.bitwise_and(lax.broadcasted_iota(jnp.int32, t.shape, 1), 63)
    return jnp.where(lane < 8, pltpu.roll(t, width - 8, 1), jnp.where(lane < 16, pltpu.roll(t, 8, 1), 0.0))


def _rope(t, c, s):
    return t * c + _swap8(t) * s


def _rope_bwd(d, c, s):
    return d * c + _swap8(d * s)


def _tile4(a):
    return jnp.concatenate([a, a, a, a], axis=1)


def _attn_band(n, k_cur, k_prev, v_cur, v_prev, c_cur, s_cur, c_prev, s_prev):
    kband = jnp.concatenate([_rope(k_prev, c_prev, s_prev), _rope(k_cur, c_cur, s_cur)], axis=0)
    vband = jnp.concatenate([v_prev, v_cur], axis=0)
    qi = lax.broadcasted_iota(jnp.int32, (ATTN_BLOCK, 2 * ATTN_BLOCK), 0)
    kj = lax.broadcasted_iota(jnp.int32, (ATTN_BLOCK, 2 * ATTN_BLOCK), 1)
    dist = qi + ATTN_BLOCK - kj
    valid = (dist >= 0) & (dist < ATTN_BLOCK) & (n * ATTN_BLOCK - ATTN_BLOCK + kj >= 0)
    return (kband.astype(BF16), pltpu.roll(kband, 64, 1).astype(BF16),
            vband.astype(BF16), pltpu.roll(vband, 64, 1).astype(BF16), valid, kband)


def _attn_probs(raw, valid, sink, axis):
    s = jnp.where(valid, raw * ATTN_SCALE, NEG_BIG)
    m = jnp.maximum(jnp.max(s, axis=axis, keepdims=True), sink)
    p = jnp.exp(s - m)
    esink = jnp.exp(sink - m)
    z = jnp.sum(p, axis=axis, keepdims=True) + esink
    return p / z, esink / z


def _attn_valid_t(n):
    kj = lax.broadcasted_iota(jnp.int32, (2 * ATTN_BLOCK, ATTN_BLOCK), 0)
    qi = lax.broadcasted_iota(jnp.int32, (2 * ATTN_BLOCK, ATTN_BLOCK), 1)
    dist = qi + ATTN_BLOCK - kj
    return (dist >= 0) & (dist < ATTN_BLOCK) & (n * ATTN_BLOCK - ATTN_BLOCK + kj >= 0)


def _attn_specs(nb):
    def cur(col, width=KV_W):
        return pl.BlockSpec((ATTN_BLOCK, width), lambda n: (jnp.minimum(n, nb - 1), col))

    def prev(col):
        return pl.BlockSpec((ATTN_BLOCK, KV_W), lambda n: (jnp.maximum(n - 1, 0), col))

    ua_specs = [cur(0, ATTN_W), cur(4), prev(4), cur(5), prev(5)]
    tab_specs = [cur(0), cur(0), prev(0), prev(0)]
    return ua_specs, tab_specs


def _attn_fwd(ua, ctab, stab, sinks, name, comm=None):
    t = ua.shape[0]
    nb = t // ATTN_BLOCK
    ua_specs, tab_specs = _attn_specs(nb)

    def body(q_ref, kc_ref, kp_ref, vc_ref, vp_ref, cc_ref, sc_ref, cp_ref, sp_ref, sink_ref, o_ref, o_t_ref):
        n = pl.program_id(0)
        cc, sc = cc_ref[...], sc_ref[...]
        kb, kb_r, vb, vb_r, valid, _ = _attn_band(n, kc_ref[...], kp_ref[...], vc_ref[...], vp_ref[...],
                                                  cc, sc, cp_ref[...], sp_ref[...])
        qr = _rope(q_ref[...], _tile4(cc), _tile4(sc))
        lo = lax.broadcasted_iota(jnp.int32, (ATTN_BLOCK, LANES), 1) < 64
        heads = []
        for j in range(4):
            qj = qr[:, j * LANES:(j + 1) * LANES]
            for is_lo in (True, False):
                aligned = is_lo == (j < 2)
                qm = jnp.where(lo if is_lo else jnp.logical_not(lo), qj, 0.0).astype(BF16)
                raw = lax.dot_general(qm, kb if aligned else kb_r, _NT, preferred_element_type=F32)
                heads.append((raw, vb if aligned else vb_r, sink_ref[0, len(heads)]))
        halves = []
        for raw, vv, sink in heads:
            probs, _ = _attn_probs(raw, valid, sink, 1)
            halves.append(lax.dot_general(probs.astype(BF16), vv, _NN, preferred_element_type=F32))
        outs = [jnp.where(lo, halves[2 * j], halves[2 * j + 1]) for j in range(4)]
        o_ref[...] = jnp.concatenate(outs, axis=1).astype(o_ref.dtype)
        for j in range(4):
            o_t_ref[j * LANES:(j + 1) * LANES, :] = outs[j].T.astype(o_t_ref.dtype)

    return _call(
        body, name=name, grid=(nb,), ins=[ua, ua, ua, ua, ua, ctab, stab, ctab, stab, sinks],
        in_specs=ua_specs + tab_specs + [pl.BlockSpec(memory_space=pltpu.SMEM)],
        out_specs=[pl.BlockSpec((ATTN_BLOCK, ATTN_W), lambda n: (n, 0)),
                   pl.BlockSpec((ATTN_W, ATTN_BLOCK), lambda n: (0, n))],
        out_shape=[jax.ShapeDtypeStruct((t, ATTN_W), BF16), jax.ShapeDtypeStruct((ATTN_W, t), BF16)],
        sem=("parallel",), comm=comm)


def _attn_bwd(ua, d_out, ctab, stab, sinks, name, comm=None):
    t = ua.shape[0]
    nb = t // ATTN_BLOCK
    ua_specs, tab_specs = _attn_specs(nb)

    def body(q_ref, kc_ref, kp_ref, vc_ref, vp_ref, cc_ref, sc_ref, cp_ref, sp_ref, do_ref, sink_ref,
             dua_ref, dua_t_ref, dbias_ref, dsink_ref, dq_c, dk_c, dv_c, dq_n, dk_n, dv_n):
        n = pl.program_id(0)

        @pl.when(n == 0)
        def _():
            dq_c[...] = jnp.zeros_like(dq_c)
            dk_c[...] = jnp.zeros_like(dk_c)
            dv_c[...] = jnp.zeros_like(dv_c)
            dbias_ref[...] = jnp.zeros_like(dbias_ref)
            dsink_ref[...] = jnp.zeros_like(dsink_ref)

        @pl.when(n == nb)
        def _():
            dq_n[...] = jnp.zeros_like(dq_n)
            dk_n[...] = jnp.zeros_like(dk_n)
            dv_n[...] = jnp.zeros_like(dv_n)

        @pl.when(n < nb)
        def _():
            cc, sc = cc_ref[...], sc_ref[...]
            kb, kb_r, vb, vb_r, _, kb_f32 = _attn_band(n, kc_ref[...], kp_ref[...], vc_ref[...], vp_ref[...],
                                                       cc, sc, cp_ref[...], sp_ref[...])
            valid_t = _attn_valid_t(n)
            c4, s4 = _tile4(cc), _tile4(sc)
            qr = _rope(q_ref[...], c4, s4)
            do = do_ref[...].astype(F32)
            lane = lax.broadcasted_iota(jnp.int32, (ATTN_BLOCK, LANES), 1)
            lo = lane < 64
            lane_row = lax.broadcasted_iota(jnp.int32, (1, LANES), 1)
            k_t = {False: kb_f32.T.astype(BF16), True: pltpu.roll(kb_f32, 64, 1).T.astype(BF16)}
            heads = []
            for j in range(4):
                qj = qr[:, j * LANES:(j + 1) * LANES]
                doj = do[:, j * LANES:(j + 1) * LANES]
                for is_lo in (True, False):
                    aligned = is_lo == (j < 2)
                    msk = lo if is_lo else jnp.logical_not(lo)
                    kk = kb if aligned else kb_r
                    vv = vb if aligned else vb_r
                    qm = jnp.where(msk, qj, 0.0).astype(BF16)
                    dom = jnp.where(msk, doj, 0.0).astype(BF16)
                    heads.append(dict(
                        aligned=aligned, qm=qm, dom=dom, sink=sink_ref[0, len(heads)],
                        raw_t=lax.dot_general(kk, qm, _NT, preferred_element_type=F32),
                        dp_t=lax.dot_general(vv, dom, _NT, preferred_element_type=F32)))
            dk_band = jnp.zeros((2 * ATTN_BLOCK, LANES), F32)
            dv_band = jnp.zeros((2 * ATTN_BLOCK, LANES), F32)
            dsink = jnp.zeros((1, LANES), F32)
            for head, hd in enumerate(heads):
                probs_t, psink = _attn_probs(hd["raw_t"], valid_t, hd["sink"], 0)
                delta_t = jnp.sum(probs_t * hd["dp_t"], axis=0, keepdims=True)
                hd["ds_t"] = (probs_t * (hd["dp_t"] - delta_t) * ATTN_SCALE).astype(BF16)
                dsink = dsink + jnp.where(lane_row == head, -jnp.sum(psink * delta_t), 0.0)
                dk_h = lax.dot_general(hd["ds_t"], hd["qm"], _NN, preferred_element_type=F32)
                dv_h = lax.dot_general(probs_t.astype(BF16), hd["dom"], _NN, preferred_element_type=F32)
                if not hd["aligned"]:
                    dk_h = pltpu.roll(dk_h, 64, 1)
                    dv_h = pltpu.roll(dv_h, 64, 1)
                dk_band = dk_band + dk_h
                dv_band = dv_band + dv_h
            row_lo = lax.broadcasted_iota(jnp.int32, (LANES, ATTN_BLOCK), 0) < 64
            dq_t = [lax.dot_general(k_t[not hd["aligned"]], hd["ds_t"], _NN, preferred_element_type=F32)
                    for hd in heads]
            dqs = [jnp.where(row_lo, dq_t[2 * j], dq_t[2 * j + 1]).T for j in range(4)]
            dq_n[...] = _rope_bwd(jnp.concatenate(dqs, axis=1), c4, s4)
            dk_n[...] = dk_band
            dv_n[...] = dv_band
            dsink_ref[...] += dsink

        dk_prev = _rope_bwd(dk_c[...] + dk_n[0:ATTN_BLOCK, :], cp_ref[...], sp_ref[...])
        dv_prev = dv_c[...] + dv_n[0:ATTN_BLOCK, :]
        full = jnp.concatenate([dq_c[...], dk_prev, dv_prev], axis=1)
        dua_ref[...] = full.astype(dua_ref.dtype)
        for j in range(UA_W // LANES):
            dua_t_ref[j * LANES:(j + 1) * LANES, :] = full[:, j * LANES:(j + 1) * LANES].T.astype(dua_t_ref.dtype)
        dbias_ref[...] += jnp.sum(full, axis=0, keepdims=True)
        dq_c[...] = dq_n[...]
        dk_c[...] = dk_n[ATTN_BLOCK:, :]
        dv_c[...] = dv_n[ATTN_BLOCK:, :]

    return _call(
        body, name=name, grid=(nb + 1,), ins=[ua, ua, ua, ua, ua, ctab, stab, ctab, stab, d_out, sinks],
        in_specs=ua_specs + tab_specs + [
            pl.BlockSpec((ATTN_BLOCK, ATTN_W), lambda n: (jnp.minimum(n, nb - 1), 0)),
            pl.BlockSpec(memory_space=pltpu.SMEM)],
        out_specs=[pl.BlockSpec((ATTN_BLOCK, UA_W), lambda n: (jnp.maximum(n - 1, 0), 0)),
                   pl.BlockSpec((UA_W, ATTN_BLOCK), lambda n: (0, jnp.maximum(n - 1, 0))),
                   pl.BlockSpec((1, UA_W), lambda n: (0, 0)),
                   pl.BlockSpec((1, LANES), lambda n: (0, 0))],
        out_shape=[jax.ShapeDtypeStruct((t, UA_W), BF16), jax.ShapeDtypeStruct((UA_W, t), BF16),
                   jax.ShapeDtypeStruct((1, UA_W), F32),
                   jax.ShapeDtypeStruct((1, LANES), F32)],
        scratch_shapes=[pltpu.VMEM((ATTN_BLOCK, ATTN_W), F32), pltpu.VMEM((ATTN_BLOCK, KV_W), F32),
                        pltpu.VMEM((ATTN_BLOCK, KV_W), F32), pltpu.VMEM((ATTN_BLOCK, ATTN_W), F32),
                        pltpu.VMEM((2 * ATTN_BLOCK, KV_W), F32), pltpu.VMEM((2 * ATTN_BLOCK, KV_W), F32)],
        sem=("arbitrary",), comm=comm)


def _tri_mats():
    r = lax.broadcasted_iota(jnp.int32, (HGRN_CHUNK, LANES), 0)
    c = lax.broadcasted_iota(jnp.int32, (HGRN_CHUNK, LANES), 1)
    lower = ((c <= r) & (c < HGRN_CHUNK)).astype(F32)
    upper = ((c >= r) & (c < HGRN_CHUNK)).astype(F32)
    return lower, upper


def _tri_apply(tri, g):
    pad = jnp.concatenate([g, jnp.zeros_like(g)], axis=0)
    return lax.dot_general(tri, pad, _NN, precision=lax.Precision.HIGHEST, preferred_element_type=F32)


def _sub_masks():
    s = lax.broadcasted_iota(jnp.int32, (HGRN_CHUNK, LANES), 0)
    tt = lax.broadcasted_iota(jnp.int32, (HGRN_CHUNK, LANES), 1)
    return [(tt >= HGRN_SUB * i) & (tt < HGRN_SUB * (i + 1)) & (s <= tt) for i in range(HGRN_CHUNK // HGRN_SUB)]


def _hgrn_gates(hq, hf, lb_ref, b_scr):
    lb = _sig(lb_ref[0:1, :] - lb_ref[1:2, :])
    q = hq * _sig(hq)
    sg = _sig(hf)
    f = lb + (1.0 - lb) * sg
    k = 1.0 - f
    lower, _ = _tri_mats()
    b = _tri_apply(lower, jnp.log(f))
    b_scr[...] = b
    nsub = HGRN_CHUNK // HGRN_SUB
    starts = [jnp.zeros((1, HG_W), F32)] + [b_scr[HGRN_SUB * i - 1:HGRN_SUB * i, :] for i in range(1, nsub)]
    pq = jnp.concatenate([jnp.broadcast_to(p, (HGRN_SUB, HG_W)) for p in starts], axis=0)
    b_last = b_scr[HGRN_CHUNK - 1:HGRN_CHUNK, :]
    e_q = jnp.exp(b - pq)
    e_k = [jnp.exp(jnp.minimum(p - b, EXP_CLAMP)) for p in starts]
    e_b = jnp.exp(b)
    e_bl = jnp.exp(b_last - b)
    e_last = jnp.exp(b_last)
    return q, sg, f, k, lb, e_q, e_k, e_b, e_bl, e_last


def _sub_masks_ts():
    tt = lax.broadcasted_iota(jnp.int32, (HGRN_CHUNK, LANES), 0)
    s = lax.broadcasted_iota(jnp.int32, (HGRN_CHUNK, LANES), 1)
    return [(tt >= HGRN_SUB * i) & (tt < HGRN_SUB * (i + 1)) & (s <= tt) for i in range(HGRN_CHUNK // HGRN_SUB)]


def _masked_sum(blocks, masks, axis):
    step = HGRN_CHUNK if axis == 0 else LANES
    acc = jnp.zeros((HGRN_CHUNK, LANES), F32)
    for i, msk in enumerate(masks):
        blk = blocks[step * i:step * (i + 1), :] if axis == 0 else blocks[:, step * i:step * (i + 1)]
        acc = acc + jnp.where(msk, blk, 0.0)
    return acc


def _store_transposed(out_t_ref, chunk_rows):
    width = chunk_rows[0].shape[1]
    if len(chunk_rows) == 1:
        groups = [jnp.concatenate([chunk_rows[0], jnp.zeros_like(chunk_rows[0])], axis=0)]
    else:
        groups = [jnp.concatenate(chunk_rows[g:g + 2], axis=0) for g in range(0, len(chunk_rows), 2)]
    for g, rows in enumerate(groups):
        for c in range(width // LANES):
            tile = rows[:, c * LANES:(c + 1) * LANES].T.astype(out_t_ref.dtype)
            if len(chunk_rows) == 1:
                out_t_ref[c * LANES:(c + 1) * LANES, :] = tile[:, 0:HGRN_CHUNK]
            else:
                out_t_ref[c * LANES:(c + 1) * LANES, g * LANES:(g + 1) * LANES] = tile


def _hgrn_chunk_inputs(j, hq_ref, hf_ref, hi_ref, hg_ref, lb_ref, b_scr):
    rows = slice(j * HGRN_CHUNK, (j + 1) * HGRN_CHUNK)
    hq, hf, v, hg = hq_ref[rows, :], hf_ref[rows, :], hi_ref[rows, :], hg_ref[rows, :]
    q, sg, f, k, lb, e_q, e_k, e_b, e_bl, e_last = _hgrn_gates(hq, hf, lb_ref, b_scr.at[j])
    return dict(rows=rows, hq=hq, v=v, hg=hg, q=q, sg=sg, f=f, k=k, lb=lb, e_q=e_q, e_k=e_k, e_b=e_b, e_bl=e_bl,
                e_last=e_last, qt=q * e_q, qb=q * e_b, kd=k * e_bl, khat=[k * e for e in e_k])


def _hgrn_fwd(uh, lb_raw, norm_g, name, comm=None):
    t = uh.shape[0]
    nc = t // HGRN_CHUNK
    cps = _pick(nc, (HGRN_CHUNKS_PER_STEP, 2, 1))
    rows_step = cps * HGRN_CHUNK

    def body(hq_ref, hf_ref, hi_ref, hg_ref, lb_ref, ng_ref, r_ref, r_t_ref, o_ref, st_out_ref, st_ref, b_scr):
        @pl.when(pl.program_id(0) == 0)
        def _():
            st_ref[...] = jnp.zeros_like(st_ref)

        masks = _sub_masks_ts()
        ng = ng_ref[...]
        zpad = jnp.zeros((HGRN_CHUNK, LANES), F32)
        heads = [slice(h * LANES, (h + 1) * LANES) for h in range(4)]
        chunks = [_hgrn_chunk_inputs(j, hq_ref, hf_ref, hi_ref, hg_ref, lb_ref, b_scr) for j in range(cps)]
        for ch in chunks:
            ch["scores"] = [_dot3(ch["qt"][:, sl],
                                  jnp.concatenate([x for kh in ch["khat"] for x in (kh[:, sl], zpad)], axis=0), _NT)
                            for sl in heads]
        for j, ch in enumerate(chunks):
            o_heads, y_heads = [], []
            for h, sl in enumerate(heads):
                a_ts = _masked_sum(ch["scores"][h], masks, 1)
                vh = ch["v"][:, sl].astype(BF16)
                v_pad = jnp.concatenate([vh, jnp.zeros_like(vh)], axis=0)
                o_intra = lax.dot_general(a_ts.astype(BF16), v_pad, _NN, preferred_element_type=F32)
                st = st_ref[h]
                st_out_ref[j, h] = st
                o_inter = _dot(ch["qb"][:, sl], st, _NT)
                st_ref[h] = st * ch["e_last"][:, sl] + _dot(vh, ch["kd"][:, sl], _TN)
                oh = o_intra + o_inter
                rs = lax.rsqrt(jnp.mean(oh * oh, axis=1, keepdims=True) + RMS_EPS)
                o_heads.append(oh)
                y_heads.append(oh * rs * ng)
            hg = ch["hg"]
            o_ref[ch["rows"], :] = jnp.concatenate(o_heads, axis=1)
            ch["r"] = jnp.concatenate(y_heads, axis=1) * (hg * _sig(hg))
            r_ref[ch["rows"], :] = ch["r"].astype(r_ref.dtype)
        _store_transposed(r_t_ref, [ch["r"] for ch in chunks])

    col = lambda j: pl.BlockSpec((rows_step, HG_W), lambda c: (c, j))
    return _call(
        body, name=name, grid=(nc // cps,), ins=[uh, uh, uh, uh, lb_raw, norm_g],
        in_specs=[col(0), col(1), col(2), col(3),
                  pl.BlockSpec((2, HG_W), lambda c: (0, 0)), pl.BlockSpec((1, LANES), lambda c: (0, 0))],
        out_specs=[pl.BlockSpec((rows_step, HG_W), lambda c: (c, 0)),
                   pl.BlockSpec((HG_W, rows_step), lambda c: (0, c)),
                   pl.BlockSpec((rows_step, HG_W), lambda c: (c, 0)),
                   pl.BlockSpec((cps, 4, LANES, LANES), lambda c: (c, 0, 0, 0))],
        out_shape=[jax.ShapeDtypeStruct((t, HG_W), BF16), jax.ShapeDtypeStruct((HG_W, t), BF16),
                   jax.ShapeDtypeStruct((t, HG_W), F32), jax.ShapeDtypeStruct((nc, 4, LANES, LANES), F32)],
        scratch_shapes=[pltpu.VMEM((4, LANES, LANES), F32), pltpu.VMEM((cps, HGRN_CHUNK, HG_W), F32)],
        sem=("arbitrary",), comm=comm)


def _hgrn_bwd(uh, o_pre, d_r, states, lb_raw, norm_g, name, comm=None):
    t = uh.shape[0]
    nc = t // HGRN_CHUNK
    cps = _pick(nc, (HGRN_CHUNKS_PER_STEP, 2, 1))
    ns = nc // cps
    rows_step = cps * HGRN_CHUNK
    nsub = HGRN_CHUNK // HGRN_SUB

    def body(hq_ref, hf_ref, hi_ref, hg_ref, o_ref, dr_ref, st_in_ref, lb_ref, ng_ref,
             duh_ref, duh_t_ref, dbias_ref, dng_ref, dlb_ref, dst_ref, b_scr, dlb_acc):
        i = pl.program_id(0)

        @pl.when(i == 0)
        def _():
            dst_ref[...] = jnp.zeros_like(dst_ref)
            dbias_ref[...] = jnp.zeros_like(dbias_ref)
            dng_ref[...] = jnp.zeros_like(dng_ref)
            dlb_acc[...] = jnp.zeros_like(dlb_acc)

        masks_st = _sub_masks()
        masks_ts = _sub_masks_ts()
        ng = ng_ref[...]
        zpad = jnp.zeros((HGRN_CHUNK, LANES), F32)
        _, upper = _tri_mats()
        heads = [slice(h * LANES, (h + 1) * LANES) for h in range(4)]
        row = lax.broadcasted_iota(jnp.int32, (HGRN_CHUNK, HG_W), 0)

        chunks = [_hgrn_chunk_inputs(j, hq_ref, hf_ref, hi_ref, hg_ref, lb_ref, b_scr) for j in range(cps)]
        dng = jnp.zeros((1, LANES), F32)
        for ch in chunks:
            o = o_ref[ch["rows"], :]
            dr = dr_ref[ch["rows"], :].astype(F32)
            hg = ch["hg"]
            sgg = _sig(hg)
            dy = dr * (hg * sgg)
            do_h, y_h = [], []
            for sl in heads:
                oh = o[:, sl]
                rs = lax.rsqrt(jnp.mean(oh * oh, axis=1, keepdims=True) + RMS_EPS)
                y_h.append(oh * rs * ng)
                dng = dng + jnp.sum(dy[:, sl] * oh * rs, axis=0, keepdims=True)
                w = dy[:, sl] * ng
                do_h.append(rs * (w - oh * (rs * rs) * jnp.mean(w * oh, axis=1, keepdims=True)))
            ch["do"] = do_h
            ch["dhg"] = dr * jnp.concatenate(y_h, axis=1) * _dsilu(hg, sgg)

        for ch in chunks:
            ch["kst"], ch["kpad"], ch["qt_pad"], ch["v_b"], ch["do_pad"] = [], [], [], [], []
            ch["ats"], ch["d_at"], ch["d_a"] = [], [], []
            for h, sl in enumerate(heads):
                kst = jnp.concatenate([kh[:, sl] for kh in ch["khat"]], axis=0)
                kpad = jnp.concatenate([x for kh in ch["khat"] for x in (kh[:, sl], zpad)], axis=0)
                qt_pad = jnp.concatenate([ch["qt"][:, sl], zpad], axis=0)
                vh = ch["v"][:, sl].astype(BF16)
                v_pad = jnp.concatenate([vh, jnp.zeros_like(vh)], axis=0)
                do_b = ch["do"][h].astype(BF16)
                do_pad = jnp.concatenate([do_b, jnp.zeros_like(do_b)], axis=0)
                ch["kst"].append(kst)
                ch["kpad"].append(kpad)
                ch["qt_pad"].append(qt_pad)
                ch["v_b"].append(vh)
                ch["do_pad"].append(do_pad)
                ch["ats"].append(_dot3(kst, qt_pad, _NT))
                ch["d_at"].append(lax.dot_general(vh, do_pad, _NT, preferred_element_type=F32))
                ch["d_a"].append(lax.dot_general(do_b, v_pad, _NT, preferred_element_type=F32))

        for ch in chunks:
            ch["d_kst"], ch["d_qt"], ch["dv"] = [], [], []
            for h in range(4):
                at = _masked_sum(ch["ats"][h], masks_st, 0)
                d_ats = jnp.concatenate([jnp.where(m, ch["d_at"][h], 0.0) for m in masks_st], axis=0)
                d_a_cat = jnp.concatenate([jnp.where(m, ch["d_a"][h], 0.0) for m in masks_ts], axis=1)
                ch["d_kst"].append(_dot3(d_ats, ch["qt_pad"][h], _NN))
                ch["d_qt"].append(_dot3(d_a_cat, ch["kpad"][h], _NN))
                ch["dv"].append(lax.dot_general(at.astype(BF16), ch["do_pad"][h], _NN, preferred_element_type=F32))

        for j in reversed(range(cps)):
            ch = chunks[j]
            q, k, sg, f, lb = ch["q"], ch["k"], ch["sg"], ch["f"], ch["lb"]
            dq_h, dk_h, dv_h, extra_h = [], [], [], []
            for h, sl in enumerate(heads):
                st_prev = st_in_ref[j, h]
                d_st = dst_ref[h]
                d_st_b = d_st.astype(BF16)
                do_b = ch["do_pad"][h][0:HGRN_CHUNK, :]
                kd, e_last = ch["kd"][:, sl], ch["e_last"][:, sl]
                dv = ch["dv"][h] + _dot(kd, d_st_b, _NT)
                d_qb = _dot(do_b, st_prev, _NN)
                d_kd = lax.dot_general(ch["v_b"][h], d_st_b, _NN, preferred_element_type=F32)
                extra_h.append(jnp.sum(st_prev * d_st, axis=0, keepdims=True) * e_last
                               + jnp.sum(kd * d_kd, axis=0, keepdims=True))
                dst_ref[h] = d_st * e_last + _dot(do_b, ch["qb"][:, sl], _TN)
                dq_h.append(ch["d_qt"][h] * ch["e_q"][:, sl] + d_qb * ch["e_b"][:, sl])
                dkk = d_kd * ch["e_bl"][:, sl]
                for s_ in range(nsub):
                    dkk = dkk + ch["d_kst"][h][HGRN_CHUNK * s_:HGRN_CHUNK * (s_ + 1), :] * ch["e_k"][s_][:, sl]
                dk_h.append(dkk)
                dv_h.append(dv)
            dq = jnp.concatenate(dq_h, axis=1)
            dk = jnp.concatenate(dk_h, axis=1)
            dv = jnp.concatenate(dv_h, axis=1)
            extra = jnp.concatenate(extra_h, axis=1)
            db = q * dq - k * dk + jnp.where(row == HGRN_CHUNK - 1, extra, 0.0)
            dg = _tri_apply(upper, db)
            df = dg / f - dk
            dhf = df * (1.0 - lb) * sg * (1.0 - sg)
            dhq = dq * _dsilu(ch["hq"], _sig(ch["hq"]))
            full = jnp.concatenate([dhq, dhf, dv, ch["dhg"]], axis=1)
            duh_ref[ch["rows"], :] = full.astype(duh_ref.dtype)
            ch["full"] = full
            dbias_ref[...] += jnp.sum(full, axis=0, keepdims=True)
            dlb_acc[...] += jnp.sum(df * (1.0 - sg), axis=0, keepdims=True)
        dng_ref[...] += dng
        _store_transposed(duh_t_ref, [ch["full"] for ch in chunks])

        @pl.when(i == ns - 1)
        def _():
            lb = chunks[0]["lb"]
            d_a0 = dlb_acc[...] * lb * (1.0 - lb)
            r8 = lax.broadcasted_iota(jnp.int32, (8, HG_W), 0)
            dlb_ref[...] = jnp.where(r8 == 0, d_a0, jnp.where(r8 == 1, -d_a0, 0.0))

    col = lambda j: pl.BlockSpec((rows_step, HG_W), lambda i: (ns - 1 - i, j))
    return _call(
        body, name=name, grid=(ns,), ins=[uh, uh, uh, uh, o_pre, d_r, states, lb_raw, norm_g],
        in_specs=[col(0), col(1), col(2), col(3), col(0), col(d_r.shape[1] // HG_W - 1),
                  pl.BlockSpec((cps, 4, LANES, LANES), lambda i: (ns - 1 - i, 0, 0, 0)),
                  pl.BlockSpec((2, HG_W), lambda i: (0, 0)), pl.BlockSpec((1, LANES), lambda i: (0, 0))],
        out_specs=[pl.BlockSpec((rows_step, UH_W), lambda i: (ns - 1 - i, 0)),
                   pl.BlockSpec((UH_W, rows_step), lambda i: (0, ns - 1 - i)),
                   pl.BlockSpec((1, UH_W), lambda i: (0, 0)),
                   pl.BlockSpec((1, LANES), lambda i: (0, 0)),
                   pl.BlockSpec((8, HG_W), lambda i: (0, 0))],
        out_shape=[jax.ShapeDtypeStruct((t, UH_W), BF16), jax.ShapeDtypeStruct((UH_W, t), BF16),
                   jax.ShapeDtypeStruct((1, UH_W), F32),
                   jax.ShapeDtypeStruct((1, LANES), F32), jax.ShapeDtypeStruct((8, HG_W), F32)],
        scratch_shapes=[pltpu.VMEM((4, LANES, LANES), F32), pltpu.VMEM((cps, HGRN_CHUNK, HG_W), F32),
                        pltpu.VMEM((1, HG_W), F32)],
        sem=("arbitrary",), comm=comm)


def _ln_bwd_math(dy, xhat, rstd, g):
    dxh = dy * g
    return rstd * (dxh - jnp.mean(dxh, axis=1, keepdims=True)
                   - xhat * jnp.mean(dxh * xhat, axis=1, keepdims=True))


def _mm_rows(a, b, extras, *, name, epilogue, out_shape, out_specs, tb=False, tm=512, tk=1408, pair2=None):
    m, kdim = a.shape
    n = b.shape[0] if tb else b.shape[1]
    tm = _pick(m, (tm, 256, 128))
    tk = _pick(kdim, (tk, 1408, 1024, 768, 512, 256, 128))
    nk = kdim // tk
    b_spec = pl.BlockSpec((n, tk), lambda i, k: (0, k)) if tb else pl.BlockSpec((tk, n), lambda i, k: (k, 0))
    dims = _NT if tb else _NN
    n_ex, n_out, n_p2 = len(extras), len(out_shape), (0 if pair2 is None else 2)

    def body(*refs):
        a_ref, b_ref = refs[0], refs[1]
        p2_refs = refs[2:2 + n_p2]
        ex_refs = refs[2 + n_p2:2 + n_p2 + n_ex]
        o_refs = refs[2 + n_p2 + n_ex:2 + n_p2 + n_ex + n_out]
        acc_ref = refs[-1]
        i, k = pl.program_id(0), pl.program_id(1)

        @pl.when(k == 0)
        def _():
            if n_p2:
                acc_ref[...] = _dot(p2_refs[0][...], p2_refs[1][...], _NN)
            else:
                acc_ref[...] = jnp.zeros_like(acc_ref)

        acc_ref[...] += _dot(a_ref[...], b_ref[...], dims)

        @pl.when(k == nk - 1)
        def _():
            epilogue(acc_ref[...], ex_refs, o_refs, i == 0)

    p2_specs, p2_ins = [], []
    if pair2 is not None:
        k2 = pair2[0].shape[1]
        p2_specs = [pl.BlockSpec((tm, k2), lambda i, k: (i, 0)), pl.BlockSpec((k2, n), lambda i, k: (0, 0))]
        p2_ins = list(pair2)
    return pl.pallas_call(
        body, name=name, grid=(m // tm, nk),
        in_specs=[pl.BlockSpec((tm, tk), lambda i, k: (i, k)), b_spec] + p2_specs + [sp for _, sp in extras],
        out_specs=list(out_specs), out_shape=list(out_shape),
        scratch_shapes=[pltpu.VMEM((tm, n), F32)],
        compiler_params=_cp("arbitrary", "arbitrary"),
    )(a, b, *p2_ins, *[arr for arr, _ in extras])


def _rows_specs(tm, d):
    row = pl.BlockSpec((tm, d), lambda i, k: (i, 0))
    vec = pl.BlockSpec((1, d), lambda i, k: (0, 0))
    col = pl.BlockSpec((tm, 1), lambda i, k: (i, 0))
    return row, vec, col


def _mm_ln_fwd(a, b, pair2, addend, g, beta, name, tm=512):
    t, d = addend.shape
    tm = _pick(t, (tm, 256, 128))
    row, vec, col = _rows_specs(tm, d)

    def epilogue(acc, ex, outs, first):
        z = acc + ALPHA * ex[0][...]
        mu = jnp.mean(z, axis=1, keepdims=True)
        zc = z - mu
        rstd = lax.rsqrt(jnp.mean(zc * zc, axis=1, keepdims=True) + LN_EPS)
        xhat = zc * rstd
        h = xhat * ex[1][...] + ex[2][...]
        outs[0][...] = h
        outs[1][...] = h.astype(BF16)
        outs[2][...] = xhat
        outs[3][...] = rstd

    return _mm_rows(a, b, [(addend, row), (g, vec), (beta, vec)], name=name, epilogue=epilogue, tm=tm, pair2=pair2,
                    out_shape=[jax.ShapeDtypeStruct((t, d), F32), jax.ShapeDtypeStruct((t, d), BF16),
                               jax.ShapeDtypeStruct((t, d), F32), jax.ShapeDtypeStruct((t, 1), F32)],
                    out_specs=[row, row, row, col])


CONV_RB = 32
HALO = 8


def _sum8(x):
    acc = x[0:8]
    for r in range(8, x.shape[0], 8):
        acc = acc + x[r:r + 8]
    return acc


FFN_TILE = 256
FFN_COLS = 256


def _rows_before(win, k):
    return pltpu.roll(win, k, 0)[HALO:]


def _rows_after(win, k):
    n = win.shape[0]
    return pltpu.roll(win, n - k, 0)[0:n - HALO]


def _resident(shape):
    return pl.BlockSpec(shape, lambda i: (0,) * len(shape), pipeline_mode=pl.Buffered(1))


def _ffn_fwd(h1b, h1, w_up_t, conv_w, conv_b, w_down, target, ln2_g, ln2_b, name, comm=None):
    t, d = h1.shape
    tr = _pick(t, (FFN_TILE, 128))
    nblk = D_FF // FFN_COLS
    rb = CONV_RB

    def body(a_ref, wup_ref, cw_ref, cb_ref, wd_ref, h1_ref, tgt_ref, g_ref, b_ref,
             u2_ref, hm_ref, dz_ref, dg_ref, db_ref, loss_ref, ext):
        i = pl.program_id(0)

        @pl.when(i == 0)
        def _():
            ext[0:HALO, :] = jnp.zeros((HALO, D_FF), F32)
            dg_ref[...] = jnp.zeros_like(dg_ref)
            db_ref[...] = jnp.zeros_like(db_ref)
            loss_ref[...] = jnp.zeros_like(loss_ref)

        a = a_ref[...]
        for c in range(nblk):
            cs = slice(c * FFN_COLS, (c + 1) * FFN_COLS)
            vs = slice(D_FF + c * FFN_COLS, D_FF + (c + 1) * FFN_COLS)
            gate_pre = lax.dot_general(a, wup_ref[cs, :], _NT, preferred_element_type=F32)
            u2_ref[:, cs] = gate_pre
            ext[HALO:, cs] = gate_pre
            u2_ref[:, vs] = lax.dot_general(a, wup_ref[vs, :], _NT, preferred_element_type=F32)
        acc = jnp.zeros((tr, d), F32)
        for c in range(nblk):
            cs = slice(c * FFN_COLS, (c + 1) * FFN_COLS)
            for sub in range(FFN_COLS // LANES):
                ln = slice(c * FFN_COLS + sub * LANES, c * FFN_COLS + (sub + 1) * LANES)
                vl = slice(D_FF + c * FFN_COLS + sub * LANES, D_FF + c * FFN_COLS + (sub + 1) * LANES)
                w0, w1, w2, bb = cw_ref[0:1, ln], cw_ref[1:2, ln], cw_ref[2:3, ln], cb_ref[:, ln]
                for r0 in range(0, tr, rb):
                    win = ext[r0:r0 + HALO + rb, ln]
                    gate = _rows_before(win, 2) * w0 + _rows_before(win, 1) * w1 + win[HALO:] * w2 + bb
                    hm_ref[r0:r0 + rb, ln] = (gate * _sig(gate) * u2_ref[r0:r0 + rb, vl]).astype(hm_ref.dtype)
            acc = acc + lax.dot_general(hm_ref[:, cs], wd_ref[cs, :], _NN, preferred_element_type=F32)
        ext[0:HALO, :] = ext[tr:tr + HALO, :]

        z = acc + ALPHA * h1_ref[...]
        gg = g_ref[...]
        mu = jnp.mean(z, axis=1, keepdims=True)
        zc = z - mu
        rstd = lax.rsqrt(jnp.mean(zc * zc, axis=1, keepdims=True) + LN_EPS)
        xhat = zc * rstd
        err = xhat * gg + b_ref[...] - tgt_ref[...]
        loss_ref[...] += 0.5 * jnp.sum(jnp.mean(err * err, axis=1, keepdims=True))
        dy = err * (1.0 / d)
        dz_ref[...] = _ln_bwd_math(dy, xhat, rstd, gg)
        dg_ref[...] += jnp.sum(dy * xhat, axis=0, keepdims=True)
        db_ref[...] += jnp.sum(dy, axis=0, keepdims=True)

    row = lambda w: pl.BlockSpec((tr, w), lambda i: (i, 0))
    vec = pl.BlockSpec((1, d), lambda i: (0, 0))
    return _call(
        body, name=name, grid=(t // tr,),
        ins=[h1b, w_up_t, conv_w, conv_b, w_down, h1, target, ln2_g, ln2_b],
        in_specs=[row(d), _resident((2 * D_FF, d)), _resident((3, D_FF)), _resident((1, D_FF)),
                  _resident((D_FF, d)), row(d), row(d), vec, vec],
        out_specs=[row(2 * D_FF), row(D_FF), row(d), vec, vec, pl.BlockSpec((1, LANES), lambda i: (0, 0))],
        out_shape=[jax.ShapeDtypeStruct((t, 2 * D_FF), F32), jax.ShapeDtypeStruct((t, D_FF), BF16),
                   jax.ShapeDtypeStruct((t, d), F32), jax.ShapeDtypeStruct((1, d), F32),
                   jax.ShapeDtypeStruct((1, d), F32), jax.ShapeDtypeStruct((1, LANES), F32)],
        scratch_shapes=[pltpu.VMEM((tr + HALO, D_FF), F32)],
        sem=("arbitrary",), comm=comm)


def _ffn_bwd(dz2, u2, w_down, w_up_t, conv_w, conv_b, xhat1, rstd1, ln1_g, name, comm=None):
    t, d = dz2.shape
    tr = _pick(t, (FFN_TILE, 128))
    nt = t // tr
    hb = tr // HALO
    nblk = D_FF // FFN_COLS
    rb = CONV_RB

    def body(dz2_ref, dz2_next_ref, u2_ref, gp_prev_ref, wd_ref, wup_ref, cw_ref, cb_ref, xhat_ref, rstd_ref,
             g1_ref, du_ref, dz1_ref, dw_ref, dcb_ref, dg1_ref, db1_ref, head, dh_s, dg_s):
        i = pl.program_id(0)

        @pl.when(i == 0)
        def _():
            dg_s[tr:, :] = jnp.zeros((HALO, D_FF), F32)
            dw_ref[...] = jnp.zeros_like(dw_ref)
            dcb_ref[...] = jnp.zeros_like(dcb_ref)
            dg1_ref[...] = jnp.zeros_like(dg1_ref)
            db1_ref[...] = jnp.zeros_like(db1_ref)

        dz2 = dz2_ref[...]

        @pl.when(i == 0)
        def _():
            dz2_b = dz2.astype(BF16)
            for c in range(nblk):
                cs = slice(c * FFN_COLS, (c + 1) * FFN_COLS)
                dh_s[:, cs] = lax.dot_general(dz2_b, wd_ref[cs, :], _NT, preferred_element_type=F32)

        dz2_next = dz2_next_ref[...].astype(BF16)
        dh_next = [lax.dot_general(dz2_next, wd_ref[c * FFN_COLS:(c + 1) * FFN_COLS, :], _NT,
                                   preferred_element_type=F32) for c in range(nblk)]
        head[0:HALO, :] = jnp.where(i == nt - 1, 0.0, gp_prev_ref[...])
        head[HALO:, :] = u2_ref[0:rb, 0:D_FF]

        acc = jnp.zeros((tr, d), F32)
        for blk in range(nblk):
            for c in range(blk * FFN_COLS // LANES, (blk + 1) * FFN_COLS // LANES):
                ln = slice(c * LANES, (c + 1) * LANES)
                vl = slice(D_FF + c * LANES, D_FF + (c + 1) * LANES)
                w0, w1, w2, bb = cw_ref[0:1, ln], cw_ref[1:2, ln], cw_ref[2:3, ln], cb_ref[:, ln]
                acc_b = jnp.zeros((8, LANES), F32)
                acc_w = [jnp.zeros((8, LANES), F32) for _ in range(3)]
                for r0 in range(0, tr, rb):
                    win = head[:, ln] if r0 == 0 else u2_ref[r0 - HALO:r0 + rb, ln]
                    g_m2, g_m1, g_0 = _rows_before(win, 2), _rows_before(win, 1), win[HALO:]
                    gate = g_m2 * w0 + g_m1 * w1 + g_0 * w2 + bb
                    sg = _sig(gate)
                    dh = dh_s[r0:r0 + rb, ln]
                    dgate = dh * u2_ref[r0:r0 + rb, vl] * _dsilu(gate, sg)
                    dg_s[r0:r0 + rb, ln] = dgate
                    du_ref[r0:r0 + rb, vl] = (dh * (gate * sg)).astype(du_ref.dtype)
                    acc_b = acc_b + _sum8(dgate)
                    acc_w[0] = acc_w[0] + _sum8(dgate * g_m2)
                    acc_w[1] = acc_w[1] + _sum8(dgate * g_m1)
                    acc_w[2] = acc_w[2] + _sum8(dgate * g_0)
                dcb_ref[:, ln] += jnp.sum(acc_b, axis=0, keepdims=True)
                for j in range(3):
                    dw_ref[j:j + 1, ln] += jnp.sum(acc_w[j], axis=0, keepdims=True)
                for r0 in range(0, tr, rb):
                    win = dg_s[r0:r0 + rb + HALO, ln]
                    d_gp = _rows_after(win, 2) * w0 + _rows_after(win, 1) * w1 + win[0:rb] * w2
                    du_ref[r0:r0 + rb, ln] = d_gp.astype(du_ref.dtype)
            cs = slice(blk * FFN_COLS, (blk + 1) * FFN_COLS)
            vs = slice(D_FF + blk * FFN_COLS, D_FF + (blk + 1) * FFN_COLS)
            acc = acc + lax.dot_general(du_ref[:, cs], wup_ref[cs, :], _NN, preferred_element_type=F32)
            acc = acc + lax.dot_general(du_ref[:, vs], wup_ref[vs, :], _NN, preferred_element_type=F32)
        dg_s[tr:, :] = dg_s[0:HALO, :]
        for c in range(nblk):
            dh_s[:, c * FFN_COLS:(c + 1) * FFN_COLS] = dh_next[c]
        dy = acc + ALPHA * dz2
        xh = xhat_ref[...]
        dz1_ref[...] = _ln_bwd_math(dy, xh, rstd_ref[...], g1_ref[...])
        dg1_ref[...] += jnp.sum(dy * xh, axis=0, keepdims=True)
        db1_ref[...] += jnp.sum(dy, axis=0, keepdims=True)

    rev = lambda w: pl.BlockSpec((tr, w), lambda i: (nt - 1 - i, 0))
    vec = pl.BlockSpec((1, d), lambda i: (0, 0))
    return _call(
        body, name=name, grid=(nt,),
        ins=[dz2, dz2, u2, u2, w_down, w_up_t, conv_w, conv_b, xhat1, rstd1, ln1_g],
        in_specs=[rev(d), pl.BlockSpec((tr, d), lambda i: (jnp.maximum(nt - 2 - i, 0), 0)), rev(2 * D_FF),
                  pl.BlockSpec((HALO, D_FF), lambda i: (jnp.maximum((nt - 1 - i) * hb - 1, 0), 0)),
                  _resident((D_FF, d)), _resident((2 * D_FF, d)), _resident((3, D_FF)), _resident((1, D_FF)),
                  rev(d), pl.BlockSpec((tr, 1), lambda i: (nt - 1 - i, 0)), vec],
        out_specs=[rev(2 * D_FF), rev(d), pl.BlockSpec((8, D_FF), lambda i: (0, 0)),
                   pl.BlockSpec((1, D_FF), lambda i: (0, 0)), vec, vec],
        out_shape=[jax.ShapeDtypeStruct((t, 2 * D_FF), BF16), jax.ShapeDtypeStruct((t, d), F32),
                   jax.ShapeDtypeStruct((8, D_FF), F32), jax.ShapeDtypeStruct((1, D_FF), F32),
                   jax.ShapeDtypeStruct((1, d), F32), jax.ShapeDtypeStruct((1, d), F32)],
        scratch_shapes=[pltpu.VMEM((HALO + rb, D_FF), F32), pltpu.VMEM((tr, D_FF), F32),
                        pltpu.VMEM((tr + HALO, D_FF), F32)],
        sem=("arbitrary",), comm=comm)


def _adamw(w, g, m, v, name):
    rows, cols = w.shape
    tr = _pick(rows, (256, 128, 64, 32, 16, 8))

    def body(w_ref, g_ref, m_ref, v_ref, d_ref, nm_ref, nv_ref):
        d_ref[...], nm_ref[...], nv_ref[...] = _adamw_math(w_ref[...], g_ref[...], m_ref[...], v_ref[...])

    spec = pl.BlockSpec((tr, cols), lambda i: (i, 0))
    shp = jax.ShapeDtypeStruct((rows, cols), F32)
    return pl.pallas_call(
        body, name=name, grid=(rows // tr,),
        in_specs=[spec, spec, spec, spec], out_specs=[spec, spec, spec], out_shape=[shp, shp, shp],
        compiler_params=_cp("parallel"),
    )(w, g, m, v)


def _pad_rows(a, rows):
    return jnp.pad(a, ((0, rows - a.shape[0]), (0, 0)))


SMALL_LAYOUT = (("ln1_g", 1024), ("ln1_b", 1024), ("b_in", 2816), ("sinks", 8), ("hgrn_lb", 1024),
                ("hgrn_norm_g", 128), ("ln2_g", 1024), ("ln2_b", 1024), ("conv_b", 2816), ("loss", 1))
SMALL_SHAPES = {"ln1_g": (1, 1024), "ln1_b": (1, 1024), "b_in": (1, 2816), "sinks": (1, 8), "hgrn_lb": (2, 512),
                "hgrn_norm_g": (1, 128), "ln2_g": (1, 1024), "ln2_b": (1, 1024), "conv_b": (1, 2816),
                "loss": (1,)}


def _pack_small(parts):
    rows = []
    for name, size in SMALL_LAYOUT:
        flat = parts[name].reshape(-1).astype(F32)
        padded = -(-size // LANES) * LANES
        rows.append(jnp.pad(flat, (0, padded - size)).reshape(-1, LANES))
    return _pad_rows(jnp.concatenate(rows, axis=0), SMALL_ROWS)


def _unpack_small(pack):
    out, r = {}, 0
    for name, size in SMALL_LAYOUT:
        nrows = -(-size // LANES)
        out[name] = pack[r:r + nrows].reshape(-1)[:size].reshape(SMALL_SHAPES[name])
        r += nrows
    return out


def _own(full, rows):
    return lax.dynamic_slice_in_dim(full, _me() * rows, rows, axis=0)


def kernel(x, positions, ln1_g, ln1_b, w_in, b_in, sinks, hgrn_lb, hgrn_norm_g, w_o, ln2_g, ln2_b, w_up, conv_w, conv_b, w_down, loss_target, m_ln1_g, m_ln1_b, m_w_in, m_b_in, m_sinks, m_hgrn_lb, m_hgrn_norm_g, m_w_o, m_ln2_g, m_ln2_b, m_w_up, m_conv_w, m_conv_b, m_w_down, v_ln1_g, v_ln1_b, v_w_in, v_b_in, v_sinks, v_hgrn_lb, v_hgrn_norm_g, v_w_o, v_ln2_g, v_ln2_b, v_w_up, v_conv_w, v_conv_b, v_w_down):
    t = x.shape[1]
    x2 = x[0]
    target = loss_target[0]
    pos_col = positions.reshape(t, 1)

    w_in_t_s = w_in[0].T.astype(BF16)
    w_up_t_s = w_up[0].T.astype(BF16)
    w_o_s = w_o[0].astype(BF16)
    w_down_s = w_down[0].astype(BF16)
    (ctab, stab, xb), (w_in_t_g, cw_g) = _prep(
        pos_col, x2, "prep_ag_w_in", _Comm([{"kind": "gather", "arr": w_in_t_s}, {"kind": "gather", "arr": _pad_rows(conv_w[0], 8)}]))
    w_in_t = w_in_t_g.reshape(D_FF, D_MODEL)
    w_a_t, w_h_t = w_in_t[:UA_W], w_in_t[UA_W:]
    conv_w_f = cw_g[:, 0:3].transpose(1, 0, 2).reshape(3, D_FF)

    ua = _mm(xb, w_a_t, tb=True, bias=b_in[:, :UA_W], name="fwd_in_attn")
    uh = _mm(xb, w_h_t, tb=True, bias=b_in[:, UA_W:], name="fwd_in_hgrn")
    half_up = SHARD_UP // 2
    (a_out, a_out_t), (w_o_g, w_up_half) = _attn_fwd(
        ua, ctab, stab, sinks, "attn_fwd",
        comm=_Comm([{"kind": "gather", "arr": w_o_s},
                    {"kind": "gather", "arr": w_up_t_s, "rows": (0, half_up), "dst_rows": SHARD_UP}]))
    (r_out, r_out_t, o_pre, states), (w_up_t_g, w_down_g) = _hgrn_fwd(
        uh, hgrn_lb, hgrn_norm_g, "hgrn_fwd",
        comm=_Comm([{"kind": "gather", "arr": w_up_t_s, "rows": (half_up, half_up), "dst_rows": SHARD_UP,
                     "dst_first": half_up, "into": w_up_half},
                    {"kind": "gather", "arr": w_down_s}]))
    w_down_f = w_down_g.reshape(D_FF, D_MODEL)
    w_o_f = w_o_g.reshape(D_MODEL, D_MODEL)
    w_up_t = w_up_t_g.reshape(2 * D_FF, D_MODEL)
    h1, h1b, xhat1, rstd1 = _mm_ln_fwd(r_out, w_o_f[ATTN_W:], (a_out, w_o_f[:ATTN_W]), x2, ln1_g, ln1_b,
                                       "fwd_o_ln1")
    u2, hmid, dz2, d_ln2_g, d_ln2_b, loss_part = _ffn_fwd(h1b, h1, w_up_t, conv_w_f, conv_b, w_down_f, target,
                                                         ln2_g, ln2_b, "ffn_fwd")[0]

    d_w_down, d_w_down_b = _mm(hmid, dz2, ta=True, out_dtype2=BF16, tm=1408, tk=1024, name="bwd_down_dw")
    (d_u2, dz1, d_conv_w8, d_conv_b, d_ln1_g, d_ln1_b), (recv_down,) = _ffn_bwd(
        dz2, u2, w_down_f, w_up_t, conv_w_f, conv_b, xhat1, rstd1, ln1_g, "ffn_bwd",
        comm=_Comm([{"kind": "exchange", "arr": d_w_down_b.reshape(N_DEV, SHARD_DOWN, D_MODEL)}]))
    d_w_up_t, d_w_up_t_b = _mm(d_u2, h1b, ta=True, out_dtype2=BF16, tm=1408, tk=1024, name="bwd_up_dw")
    d_ar = _mm(dz1, w_o_f, tb=True, name="bwd_o_dx")
    d_w_o_part = _mm(a_out_t, dz1, out_dtype2=BF16, tm=ATTN_W, out_rows=D_MODEL, name="bwd_o_dw_attn")
    d_w_o, d_w_o_b = _mm(r_out_t, dz1, out_dtype2=BF16, tm=HG_W, out_rows=D_MODEL, first_row=ATTN_W,
                         into=d_w_o_part, name="bwd_o_dw_hgrn")
    d_w_up_x = d_w_up_t_b.reshape(N_DEV, SHARD_UP, D_MODEL)
    half = SHARD_UP // 2
    d_cw_x = d_conv_w8.reshape(8, N_DEV, SHARD_IN).transpose(1, 0, 2)
    (d_ua, d_ua_t, d_bias_a, d_sinks), (recv_up_half, recv_cw) = _attn_bwd(
        ua, d_ar, ctab, stab, sinks, "attn_bwd",
        comm=_Comm([{"kind": "exchange", "arr": d_w_up_x, "rows": (0, half), "dst_rows": SHARD_UP},
                    {"kind": "exchange", "arr": d_cw_x}]))
    (d_uh, d_uh_t, d_bias_h, d_norm_g, d_lb8), (recv_up, recv_o) = _hgrn_bwd(
        uh, o_pre, d_ar, states, hgrn_lb, hgrn_norm_g, "hgrn_bwd",
        comm=_Comm([{"kind": "exchange", "arr": d_w_up_x, "rows": (half, half), "dst_rows": SHARD_UP,
                     "dst_first": half, "into": recv_up_half},
                    {"kind": "exchange", "arr": d_w_o_b.reshape(N_DEV, SHARD_O, D_MODEL)}]))
    d_w_in_part = _mm(d_ua_t, xb, out_dtype2=BF16, tm=UA_W, tk=t, out_rows=D_FF, name="bwd_in_dw_attn")
    d_w_in_t, d_w_in_t_b = _mm(d_uh_t, xb, out_dtype2=BF16, tm=256, tk=t, out_rows=D_FF, first_row=UA_W,
                               into=d_w_in_part, name="bwd_in_dw_hgrn")
    small_local = _pack_small({
        "ln1_g": d_ln1_g, "ln1_b": d_ln1_b, "b_in": jnp.concatenate([d_bias_a, d_bias_h], axis=1),
        "sinks": d_sinks[:, :8], "hgrn_lb": d_lb8[0:2], "hgrn_norm_g": d_norm_g, "ln2_g": d_ln2_g,
        "ln2_b": d_ln2_b, "conv_b": d_conv_b, "loss": loss_part[:, :1]})
    d_w_in_x = d_w_in_t_b.reshape(N_DEV, SHARD_IN, D_MODEL)
    res_up, (from_sibling,) = _sum_shards_adamw(
        [recv_up], _own(d_w_up_t, SHARD_UP), w_up[0].T, m_w_up[0].T, v_w_up[0].T, "adamw_w_up",
        comm=_Comm([{"kind": "pair4", "arr": d_w_in_x}]))
    res_up = [r.T for r in res_up]
    own_in, chip_part = _pair_reduce(from_sibling, d_w_in_t, "pair_reduce_w_in")
    dx, (from_chips, small_g) = _mm(d_uh, w_h_t, addend=dz1, addend_scale=ALPHA, name="bwd_in_dx_hgrn",
                                    comm=_Comm([{"kind": "chips3", "arr": chip_part},
                                                {"kind": "gather", "arr": small_local}]))
    dx = _mm(d_ua, w_a_t, addend=dx, tk=768, name="bwd_in_dx_attn")

    res_in = [r.T for r in _chip_sum_adamw(from_chips, own_in, w_in[0].T, m_w_in[0].T, v_w_in[0].T, "adamw_w_in")]
    res_o = _sum_shards_adamw([recv_o], _own(d_w_o, SHARD_O), w_o[0], m_w_o[0], v_w_o[0], "adamw_w_o")
    res_down = _sum_shards_adamw([recv_down], _own(d_w_down, SHARD_DOWN), w_down[0], m_w_down[0], v_w_down[0],
                                 "adamw_w_down")
    g_cw = _sum_slots(recv_cw, "sum_conv_w")
    cw8 = lambda a: _pad_rows(a, 8)
    res_cw = (g_cw,) + tuple(_adamw(cw8(conv_w[0]), g_cw, cw8(m_conv_w[0]), cw8(v_conv_w[0]), "adamw_conv_w"))
    big = {"w_in": [r[None] for r in res_in], "w_up": [r[None] for r in res_up],
           "w_o": [r[None] for r in res_o], "w_down": [r[None] for r in res_down],
           "conv_w": [r[None, 0:3] for r in res_cw]}

    small_sum = _sum_slots(small_g, "ar_small_sum")
    gs = _unpack_small(small_sum)
    loss = gs["loss"][0]
    zero1 = jnp.zeros((1,), F32)
    w_small = _pack_small({"ln1_g": ln1_g, "ln1_b": ln1_b, "b_in": b_in, "sinks": sinks, "hgrn_lb": hgrn_lb,
                           "hgrn_norm_g": hgrn_norm_g, "ln2_g": ln2_g, "ln2_b": ln2_b, "conv_b": conv_b,
                           "loss": zero1})
    m_small = _pack_small({"ln1_g": m_ln1_g, "ln1_b": m_ln1_b, "b_in": m_b_in, "sinks": m_sinks,
                           "hgrn_lb": m_hgrn_lb, "hgrn_norm_g": m_hgrn_norm_g, "ln2_g": m_ln2_g,
                           "ln2_b": m_ln2_b, "conv_b": m_conv_b, "loss": zero1})
    v_small = _pack_small({"ln1_g": v_ln1_g, "ln1_b": v_ln1_b, "b_in": v_b_in, "sinks": v_sinks,
                           "hgrn_lb": v_hgrn_lb, "hgrn_norm_g": v_hgrn_norm_g, "ln2_g": v_ln2_g,
                           "ln2_b": v_ln2_b, "conv_b": v_conv_b, "loss": zero1})
    small = [gs] + [_unpack_small(p) for p in _adamw(w_small, small_sum, m_small, v_small, "adamw_small")]

    order = ["ln1_g", "ln1_b", "w_in", "b_in", "sinks", "hgrn_lb", "hgrn_norm_g", "w_o", "ln2_g", "ln2_b",
             "w_up", "conv_w", "conv_b", "w_down"]

    def pick(idx):
        return [big[n][idx] if n in big else small[idx][n] for n in order]

    return (loss, dx[None], *pick(0), *pick(1), *pick(2), *pick(3))
```

```python
import functools

import jax
import jax.numpy as jnp
import numpy as np
from jax import lax
from jax.experimental import pallas as pl
from jax.experimental.pallas import tpu as pltpu

F32 = jnp.float32
BF16 = jnp.bfloat16

N_DEV = 8
D_MODEL = 1024
D_FF = 2816
ATTN_W = 512
KV_W = 128
UA_W = ATTN_W + 2 * KV_W
UH_W = 2048
HG_W = 512
ATTN_BLOCK = 128
HGRN_CHUNK = 64
HGRN_SUB = 16
HGRN_CHUNKS_PER_STEP = 4
EXP_CLAMP = 85.0
NEG_BIG = -1e30
LN_EPS = 1e-5
RMS_EPS = 1e-6
ALPHA = 2.0 ** 0.25
ATTN_SCALE = 0.125
ROPE_THETA = 500000.0

ADAM_LR = 0.001
ADAM_B1 = 0.9
ADAM_B2 = 0.999
ADAM_EPS = 1e-08
ADAM_WD = 0.01
ADAM_STEP = 10

LANES = 128
VMEM_LIMIT_BYTES = 56 * 1024 * 1024

SHARD_IN = D_FF // N_DEV
SHARD_UP = 2 * D_FF // N_DEV
SHARD_O = D_MODEL // N_DEV
SHARD_DOWN = D_FF // N_DEV
SMALL_ROWS = 88

_MESH = pl.DeviceIdType.MESH
_NT = (((1,), (1,)), ((), ()))
_NN = (((1,), (0,)), ((), ()))
_TN = (((0,), (0,)), ((), ()))


def _cp(*sem):
    if sem:
        return pltpu.CompilerParams(dimension_semantics=sem, vmem_limit_bytes=VMEM_LIMIT_BYTES)
    return pltpu.CompilerParams(vmem_limit_bytes=VMEM_LIMIT_BYTES)


def _sig(x):
    return 0.5 * jnp.tanh(0.5 * x) + 0.5


def _dsilu(x, s):
    return s * (1.0 + x * (1.0 - s))


def _dot(a, b, dims):
    return lax.dot_general(a.astype(BF16), b.astype(BF16), dims, preferred_element_type=F32)


def _split(a):
    hi = a.astype(BF16)
    return hi, (a - hi.astype(F32)).astype(BF16)


def _dot3(a, b, dims):
    ah, al = _split(a)
    bh, bl = _split(b)
    d = functools.partial(lax.dot_general, dimension_numbers=dims, preferred_element_type=F32)
    return d(ah, bh) + (d(ah, bl) + d(al, bh))


def _pick(n, pref):
    for t in pref:
        if t <= n and n % t == 0:
            return t
    return n


def _my_coords():
    return lax.axis_index("x"), lax.axis_index("y"), lax.axis_index("c")


def _peer(k):
    x, y, c = _my_coords()
    return (1 - x if k & 4 else x, 1 - y if k & 2 else y, 1 - c if k & 1 else c)


def _me():
    x, y, c = _my_coords()
    return 4 * x + 2 * y + c


class _Comm:
    def __init__(self, items):
        self.items = []
        for it in items:
            arr = it["arr"]
            full = arr.shape[0] if it["kind"] == "gather" else arr.shape[1]
            first, count = it.get("rows", (0, full))
            self.items.append(dict(kind=it["kind"], arr=arr, first=first, count=count,
                                   dst_rows=it.get("dst_rows", count), dst_first=it.get("dst_first", 0),
                                   into=it.get("into")))
        self.n = len(self.items)
        self.arrays = [it["arr"] for it in self.items]
        self.intos = [(a, it["into"]) for a, it in enumerate(self.items) if it["into"] is not None]

    def out_shapes(self):
        return [jax.ShapeDtypeStruct((4 if it["kind"] in ("pair4", "chips3") else N_DEV, it["dst_rows"],
                                      it["arr"].shape[-1]), it["arr"].dtype) for it in self.items]

    def specs(self, n=None):
        return [pl.BlockSpec(memory_space=pl.ANY)] * (self.n if n is None else n)

    def scratch(self):
        return [pltpu.SemaphoreType.DMA(((N_DEV - 1) * self.n,)), pltpu.SemaphoreType.DMA(((N_DEV - 1) * self.n,)),
                pltpu.SemaphoreType.DMA((self.n,))]

    def _src(self, a, ref, dev):
        it = self.items[a]
        blk = ref if it["kind"] == "gather" else ref.at[dev]
        return blk.at[pl.ds(it["first"], it["count"])]

    def _dst(self, a, ref, slot):
        it = self.items[a]
        return ref.at[slot].at[pl.ds(it["dst_first"], it["count"])]

    def _copy(self, a, k, src, dst, sems, me, slot):
        other = jnp.bitwise_xor(me, k)
        idx = a * (N_DEV - 1) + k - 1
        return pltpu.make_async_remote_copy(
            src_ref=self._src(a, src, other), dst_ref=self._dst(a, dst, me if slot == "mine" else other),
            send_sem=sems[0].at[idx], recv_sem=sems[1].at[idx], device_id=_peer(k), device_id_type=_MESH)

    def _pass_on(self, a, k, dst, sems, me):
        slot = self._dst(a, dst, jnp.bitwise_xor(me, k))
        idx = a * (N_DEV - 1) + k
        return pltpu.make_async_remote_copy(
            src_ref=slot, dst_ref=slot, send_sem=sems[0].at[idx], recv_sem=sems[1].at[idx],
            device_id=_peer(1), device_id_type=_MESH)

    def _part(self, a, r, src, dst, sems, me):
        it = self.items[a]
        idx = a * (N_DEV - 1) + r
        if it["kind"] == "pair4":
            k, slot = 1, jnp.bitwise_xor(jnp.bitwise_xor(me, 1), 2 * r)
        else:
            k, slot = 2 * r, r
        return pltpu.make_async_remote_copy(
            src_ref=src.at[slot].at[pl.ds(it["first"], it["count"])], dst_ref=self._dst(a, dst, r),
            send_sem=sems[0].at[idx], recv_sem=sems[1].at[idx], device_id=_peer(k), device_id_type=_MESH)

    def _parts(self, a):
        return range(4) if self.items[a]["kind"] == "pair4" else range(1, 4)

    def _local(self, a, src, dst, sems, me):
        return pltpu.make_async_copy(self._src(a, src, me), self._dst(a, dst, me), sems[2].at[a])

    def start(self, srcs, dsts, sems):
        me = _me()
        for a, (src, dst) in enumerate(zip(srcs, dsts)):
            if self.items[a]["kind"] in ("pair4", "chips3"):
                for r in self._parts(a):
                    self._part(a, r, src, dst, sems, me).start()
                continue
            direct = (1, 2, 4, 6) if self.items[a]["kind"] == "gather" else range(1, N_DEV)
            self._local(a, src, dst, sems, me).start()
            for k in direct:
                self._copy(a, k, src, dst, sems, me, "mine").start()

    def wait(self, srcs, dsts, sems):
        me = _me()
        for a, (src, dst) in enumerate(zip(srcs, dsts)):
            if self.items[a]["kind"] in ("pair4", "chips3"):
                for r in self._parts(a):
                    self._part(a, r, src, dst, sems, me).wait_recv()
                for r in self._parts(a):
                    self._part(a, r, src, dst, sems, me).wait_send()
                continue
            if self.items[a]["kind"] == "gather":
                for k in (2, 4, 6):
                    self._copy(a, k, src, dst, sems, me, "theirs").wait_recv()
                    self._pass_on(a, k, dst, sems, me).start()
                for k in (1, 3, 5, 7):
                    self._copy(a, k, src, dst, sems, me, "theirs").wait_recv()
                for k in (1, 2, 4, 6):
                    self._copy(a, k, src, dst, sems, me, "mine").wait_send()
                for k in (2, 4, 6):
                    self._pass_on(a, k, dst, sems, me).wait_send()
            else:
                for k in range(1, N_DEV):
                    self._copy(a, k, src, dst, sems, me, "theirs").wait_recv()
                for k in range(1, N_DEV):
                    self._copy(a, k, src, dst, sems, me, "mine").wait_send()
            self._local(a, src, dst, sems, me).wait()


def _call(body, *, name, grid, ins, in_specs, out_specs, out_shape, scratch_shapes=(), sem, comm=None):
    n_in, n_out, n_scr = len(ins), len(out_shape), len(scratch_shapes)
    if comm is None:
        outs = pl.pallas_call(
            body, name=name, grid=grid, in_specs=list(in_specs), out_specs=list(out_specs),
            out_shape=list(out_shape), scratch_shapes=list(scratch_shapes), compiler_params=_cp(*sem))(*ins)
        return list(outs), []
    nc, n_into = comm.n, len(comm.intos)

    def hosted(*refs):
        pos = n_in
        c_in = refs[pos:pos + nc]
        pos += nc + n_into
        outs = refs[pos:pos + n_out]
        pos += n_out
        c_out = refs[pos:pos + nc]
        pos += nc
        scr = refs[pos:pos + n_scr]
        sems = refs[pos + n_scr:]
        ids = [pl.program_id(d) for d in range(len(grid))]
        first = functools.reduce(jnp.logical_and, [i == 0 for i in ids])
        last = functools.reduce(jnp.logical_and, [i == g - 1 for i, g in zip(ids, grid)])

        @pl.when(first)
        def _():
            comm.start(c_in, c_out, sems)

        body(*refs[:n_in], *outs, *scr)

        @pl.when(last)
        def _():
            comm.wait(c_in, c_out, sems)

    aliases = {n_in + nc + j: n_out + a for j, (a, _) in enumerate(comm.intos)}
    outs = pl.pallas_call(
        hosted, name=name, grid=grid, in_specs=list(in_specs) + comm.specs() + comm.specs(n_into),
        out_specs=list(out_specs) + comm.specs(), out_shape=list(out_shape) + comm.out_shapes(),
        scratch_shapes=list(scratch_shapes) + comm.scratch(), input_output_aliases=aliases,
        compiler_params=_cp(*(["arbitrary"] * len(grid))))(*ins, *comm.arrays, *[arr for _, arr in comm.intos])
    return list(outs[:n_out]), list(outs[n_out:])


def _sum_slots(gathered, name):
    _, rows, cols = gathered.shape

    def body(g_ref, out_ref):
        acc = g_ref[0]
        for s in range(1, N_DEV):
            acc = acc + g_ref[s]
        out_ref[...] = acc

    return pl.pallas_call(
        body, name=name,
        out_shape=jax.ShapeDtypeStruct((rows, cols), F32),
        compiler_params=_cp(),
    )(gathered)


def _slot_sum(recv_ref, own_ref, shape):
    me = _me()
    acc = jnp.zeros(shape, F32)
    for s in range(N_DEV):
        acc = acc + jnp.where(me == s, own_ref[...], recv_ref[s].astype(F32))
    return acc


def _adamw_math(w, g, m, v):
    nm = ADAM_B1 * m + (1.0 - ADAM_B1) * g
    nv = ADAM_B2 * v + (1.0 - ADAM_B2) * (g * g)
    m_hat = nm / (1.0 - ADAM_B1 ** ADAM_STEP)
    v_hat = nv / (1.0 - ADAM_B2 ** ADAM_STEP)
    return -ADAM_LR * (m_hat / (jnp.sqrt(v_hat) + ADAM_EPS) + ADAM_WD * w), nm, nv


def _pair_reduce(from_sibling, mine, name):
    _, rows, cols = from_sibling.shape
    tr = _pick(rows, (176, 128, 64, 32, 16, 8))
    tiles = rows // tr
    table = jnp.bitwise_xor(_me(), jnp.arange(0, N_DEV, 2, dtype=jnp.int32))

    def body(tbl_ref, sib_ref, mine_ref, own_ref, send_ref):
        r = pl.program_id(1)
        total = mine_ref[...] + sib_ref[0].astype(F32)
        send_ref[0] = jnp.where(r == 0, 0.0, total).astype(BF16)

        @pl.when(r == 0)
        def _():
            own_ref[...] = total

    grid_spec = pltpu.PrefetchScalarGridSpec(
        num_scalar_prefetch=1, grid=(tiles, 4),
        in_specs=[pl.BlockSpec((1, tr, cols), lambda i, r, tbl: (r, i, 0)),
                  pl.BlockSpec((tr, cols), lambda i, r, tbl: (tbl[r] * tiles + i, 0))],
        out_specs=[pl.BlockSpec((tr, cols), lambda i, r, tbl: (i, 0)),
                   pl.BlockSpec((1, tr, cols), lambda i, r, tbl: (r, i, 0))])
    return pl.pallas_call(
        body, name=name, grid_spec=grid_spec,
        out_shape=[jax.ShapeDtypeStruct((rows, cols), F32), jax.ShapeDtypeStruct((4, rows, cols), BF16)],
        compiler_params=_cp("arbitrary", "arbitrary"),
    )(table, from_sibling, mine)


def _chip_sum_adamw(from_chips, own, w, m, v, name):
    _, rows, cols = from_chips.shape
    tr = _pick(rows, (176, 128, 64, 32, 16, 8))

    def body(recv_ref, own_ref, w_ref, m_ref, v_ref, g_ref, d_ref, nm_ref, nv_ref):
        g = own_ref[...]
        for r in range(1, 4):
            g = g + recv_ref[r].astype(F32)
        g_ref[...] = g
        d_ref[...], nm_ref[...], nv_ref[...] = _adamw_math(w_ref[...], g, m_ref[...], v_ref[...])

    spec = pl.BlockSpec((tr, cols), lambda i: (i, 0))
    shp = jax.ShapeDtypeStruct((rows, cols), F32)
    return pl.pallas_call(
        body, name=name, grid=(rows // tr,),
        in_specs=[pl.BlockSpec((4, tr, cols), lambda i: (0, i, 0)), spec, spec, spec, spec],
        out_specs=[spec, spec, spec, spec], out_shape=[shp, shp, shp, shp],
        compiler_params=_cp("parallel"),
    )(from_chips, own, w, m, v)


def _sum_shards_adamw(recvs, own, w, m, v, name, comm=None):
    rows_p, cols = recvs[0].shape[1], recvs[0].shape[2]
    n_p = len(recvs)
    tr = _pick(rows_p, (176, 128, 64, 32, 16, 8))
    tiles = rows_p // tr

    def body(*refs):
        recv_refs = refs[:n_p]
        own_ref, w_ref, m_ref, v_ref, g_ref, d_ref, nm_ref, nv_ref = refs[n_p:]
        for j in range(n_p):
            @pl.when(pl.program_id(0) == j)
            def _():
                g = _slot_sum(recv_refs[j], own_ref, (tr, cols))
                g_ref[...] = g
                d_ref[...], nm_ref[...], nv_ref[...] = _adamw_math(w_ref[...], g, m_ref[...], v_ref[...])

    spec = pl.BlockSpec((tr, cols), lambda p_, i: (p_ * tiles + i, 0))
    shp = jax.ShapeDtypeStruct((rows_p * n_p, cols), F32)
    outs, couts = _call(
        body, name=name, grid=(n_p, tiles), ins=[*recvs, own, w, m, v],
        in_specs=[pl.BlockSpec((N_DEV, tr, cols), functools.partial(lambda p_, i, j: (0, jnp.where(p_ == j, i, 0), 0), j=j))
                  for j in range(n_p)] + [spec, spec, spec, spec],
        out_specs=[spec, spec, spec, spec], out_shape=[shp, shp, shp, shp],
        sem=("arbitrary", "arbitrary"), comm=comm)
    return outs if comm is None else (outs, couts)


def _mm(a, b, *, name, ta=False, tb=False, out_dtype=F32, out_dtype2=None, bias=None, addend=None,
        addend_scale=1.0, tm=1024, tn=1024, tk=1024, comm=None, out_rows=None, first_row=0, into=None):
    kdim, m = a.shape if ta else a.shape[::-1]
    n = b.shape[0] if tb else b.shape[1]
    tm = _pick(m, (tm, 1408, 1024, 768, 512, 256, 128))
    tn = _pick(n, (tn, 1408, 1024, 768, 512, 256, 128))
    tk = _pick(kdim, (tk, 1408, 1024, 768, 512, 256, 128))
    nk = kdim // tk
    a_spec = pl.BlockSpec((tk, tm), lambda i, j, k: (k, i)) if ta else pl.BlockSpec((tm, tk), lambda i, j, k: (i, k))
    b_spec = pl.BlockSpec((tn, tk), lambda i, j, k: (j, k)) if tb else pl.BlockSpec((tk, tn), lambda i, j, k: (k, j))
    ins, specs = [a, b], [a_spec, b_spec]
    if bias is not None:
        ins.append(bias)
        specs.append(pl.BlockSpec((1, tn), lambda i, j, k: (0, j)))
    if addend is not None:
        ins.append(addend)
        specs.append(pl.BlockSpec((tm, tn), lambda i, j, k: (i, j)))
    dims = (((0,) if ta else (1,), (1,) if tb else (0,)), ((), ()))
    has_bias, has_addend, two = bias is not None, addend is not None, out_dtype2 is not None

    def body(*refs):
        a_ref, b_ref = refs[0], refs[1]
        pos = 2
        bias_ref = addend_ref = None
        if has_bias:
            bias_ref = refs[pos]
            pos += 1
        if has_addend:
            addend_ref = refs[pos]
            pos += 1
        o_refs, acc_ref = refs[pos:-1], refs[-1]
        k = pl.program_id(2)

        @pl.when(k == 0)
        def _():
            acc_ref[...] = jnp.zeros_like(acc_ref)

        acc_ref[...] += _dot(a_ref[...], b_ref[...], dims)

        @pl.when(k == nk - 1)
        def _():
            r = acc_ref[...]
            if has_bias:
                r = r + bias_ref[...]
            if has_addend:
                r = r + addend_scale * addend_ref[...].astype(F32)
            for o_ref in o_refs:
                o_ref[...] = r.astype(o_ref.dtype)

    blk0 = first_row // tm
    dtypes = [out_dtype] + ([out_dtype2] if two else [])
    ospec = pl.BlockSpec((tm, tn), lambda i, j, k: (i + blk0, j))
    shapes = [jax.ShapeDtypeStruct((m if out_rows is None else out_rows, n), d) for d in dtypes]
    if into is not None:
        n_in = len(ins)
        outs = pl.pallas_call(
            lambda *refs: body(*refs[:n_in], *refs[n_in + len(into):]), name=name, grid=(m // tm, n // tn, nk),
            in_specs=specs + [pl.BlockSpec(memory_space=pl.ANY)] * len(into), out_specs=[ospec] * len(dtypes),
            out_shape=shapes, scratch_shapes=[pltpu.VMEM((tm, tn), F32)],
            input_output_aliases={n_in + j: j for j in range(len(into))},
            compiler_params=_cp("parallel", "parallel", "arbitrary"))(*ins, *into)
        return tuple(outs) if two else outs[0]
    outs, couts = _call(
        body, name=name, grid=(m // tm, n // tn, nk), ins=ins, in_specs=specs,
        out_specs=[ospec] * len(dtypes), out_shape=shapes,
        scratch_shapes=[pltpu.VMEM((tm, tn), F32)], sem=("parallel", "parallel", "arbitrary"), comm=comm)
    primary = tuple(outs) if two else outs[0]
    return (primary, couts) if comm is not None else primary


def _rope_lane_constants():
    inv_freq = np.float32(ROPE_THETA) ** (-np.arange(8, dtype=np.float32) * np.float32(2.0 / 16.0))
    lane = np.arange(LANES) % 64
    freq = np.where(lane < 16, inv_freq[lane % 8], 0.0).astype(np.float32)
    sign = np.where(lane < 8, -1.0, np.where(lane < 16, 1.0, 0.0)).astype(np.float32)
    return jnp.asarray(freq)[None, :], jnp.asarray(sign)[None, :]


def _prep(pos_col, x2, name, comm):
    t, d = x2.shape
    tr = _pick(t, (512, 256, 128))
    freq, sign = _rope_lane_constants()

    def body(pos_ref, freq_ref, sign_ref, x_ref, c_ref, s_ref, xb_ref):
        ang = pos_ref[...].astype(F32) * freq_ref[...]
        c_ref[...] = jnp.cos(ang)
        s_ref[...] = sign_ref[...] * jnp.sin(ang)
        xb_ref[...] = x_ref[...].astype(BF16)

    tab = pl.BlockSpec((tr, LANES), lambda i: (i, 0))
    return _call(
        body, name=name, grid=(t // tr,), ins=[pos_col, freq, sign, x2],
        in_specs=[pl.BlockSpec((tr, 1), lambda i: (i, 0)), pl.BlockSpec((1, LANES), lambda i: (0, 0)),
                  pl.BlockSpec((1, LANES), lambda i: (0, 0)), pl.BlockSpec((tr, d), lambda i: (i, 0))],
        out_specs=[tab, tab, pl.BlockSpec((tr, d), lambda i: (i, 0))],
        out_shape=[jax.ShapeDtypeStruct((t, LANES), F32), jax.ShapeDtypeStruct((t, LANES), F32),
                   jax.ShapeDtypeStruct((t, d), BF16)],
        sem=("parallel",), comm=comm)


def _swap8(t):
    width = t.shape[1]
    lane = jnp.bitwise_and(lax.broadcasted_iota(jnp.int32, t.shape, 1), 63)
    return jnp.where(lane < 8, pltpu.roll(t, width - 8, 1), jnp.where(lane < 16, pltpu.roll(t, 8, 1), 0.0))


def _rope(t, c, s):
    return t * c + _swap8(t) * s


def _rope_bwd(d, c, s):
    return d * c + _swap8(d * s)


def _tile4(a):
    return jnp.concatenate([a, a, a, a], axis=1)


def _attn_band(n, k_cur, k_prev, v_cur, v_prev, c_cur, s_cur, c_prev, s_prev):
    kband = jnp.concatenate([_rope(k_prev, c_prev, s_prev), _rope(k_cur, c_cur, s_cur)], axis=0)
    vband = jnp.concatenate([v_prev, v_cur], axis=0)
    qi = lax.broadcasted_iota(jnp.int32, (ATTN_BLOCK, 2 * ATTN_BLOCK), 0)
    kj = lax.broadcasted_iota(jnp.int32, (ATTN_BLOCK, 2 * ATTN_BLOCK), 1)
    dist = qi + ATTN_BLOCK - kj
    valid = (dist >= 0) & (dist < ATTN_BLOCK) & (n * ATTN_BLOCK - ATTN_BLOCK + kj >= 0)
    return (kband.astype(BF16), pltpu.roll(kband, 64, 1).astype(BF16),
            vband.astype(BF16), pltpu.roll(vband, 64, 1).astype(BF16), valid, kband)


def _attn_probs(raw, valid, sink, axis):
    s = jnp.where(valid, raw * ATTN_SCALE, NEG_BIG)
    m = jnp.maximum(jnp.max(s, axis=axis, keepdims=True), sink)
    p = jnp.exp(s - m)
    esink = jnp.exp(sink - m)
    z = jnp.sum(p, axis=axis, keepdims=True) + esink
    return p / z, esink / z


def _attn_valid_t(n):
    kj = lax.broadcasted_iota(jnp.int32, (2 * ATTN_BLOCK, ATTN_BLOCK), 0)
    qi = lax.broadcasted_iota(jnp.int32, (2 * ATTN_BLOCK, ATTN_BLOCK), 1)
    dist = qi + ATTN_BLOCK - kj
    return (dist >= 0) & (dist < ATTN_BLOCK) & (n * ATTN_BLOCK - ATTN_BLOCK + kj >= 0)


def _attn_specs(nb):
    def cur(col, width=KV_W):
        return pl.BlockSpec((ATTN_BLOCK, width), lambda n: (jnp.minimum(n, nb - 1), col))

    def prev(col):
        return pl.BlockSpec((ATTN_BLOCK, KV_W), lambda n: (jnp.maximum(n - 1, 0), col))

    ua_specs = [cur(0, ATTN_W), cur(4), prev(4), cur(5), prev(5)]
    tab_specs = [cur(0), cur(0), prev(0), prev(0)]
    return ua_specs, tab_specs


def _attn_fwd(ua, ctab, stab, sinks, name, comm=None):
    t = ua.shape[0]
    nb = t // ATTN_BLOCK
    ua_specs, tab_specs = _attn_specs(nb)

    def body(q_ref, kc_ref, kp_ref, vc_ref, vp_ref, cc_ref, sc_ref, cp_ref, sp_ref, sink_ref, o_ref, o_t_ref):
        n = pl.program_id(0)
        cc, sc = cc_ref[...], sc_ref[...]
        kb, kb_r, vb, vb_r, valid, _ = _attn_band(n, kc_ref[...], kp_ref[...], vc_ref[...], vp_ref[...],
                                                  cc, sc, cp_ref[...], sp_ref[...])
        qr = _rope(q_ref[...], _tile4(cc), _tile4(sc))
        lo = lax.broadcasted_iota(jnp.int32, (ATTN_BLOCK, LANES), 1) < 64
        heads = []
        for j in range(4):
            qj = qr[:, j * LANES:(j + 1) * LANES]
            for is_lo in (True, False):
                aligned = is_lo == (j < 2)
                qm = jnp.where(lo if is_lo else jnp.logical_not(lo), qj, 0.0).astype(BF16)
                raw = lax.dot_general(qm, kb if aligned else kb_r, _NT, preferred_element_type=F32)
                heads.append((raw, vb if aligned else vb_r, sink_ref[0, len(heads)]))
        halves = []
        for raw, vv, sink in heads:
            probs, _ = _attn_probs(raw, valid, sink, 1)
            halves.append(lax.dot_general(probs.astype(BF16), vv, _NN, preferred_element_type=F32))
        outs = [jnp.where(lo, halves[2 * j], halves[2 * j + 1]) for j in range(4)]
        o_ref[...] = jnp.concatenate(outs, axis=1).astype(o_ref.dtype)
        for j in range(4):
            o_t_ref[j * LANES:(j + 1) * LANES, :] = outs[j].T.astype(o_t_ref.dtype)

    return _call(
        body, name=name, grid=(nb,), ins=[ua, ua, ua, ua, ua, ctab, stab, ctab, stab, sinks],
        in_specs=ua_specs + tab_specs + [pl.BlockSpec(memory_space=pltpu.SMEM)],
        out_specs=[pl.BlockSpec((ATTN_BLOCK, ATTN_W), lambda n: (n, 0)),
                   pl.BlockSpec((ATTN_W, ATTN_BLOCK), lambda n: (0, n))],
        out_shape=[jax.ShapeDtypeStruct((t, ATTN_W), BF16), jax.ShapeDtypeStruct((ATTN_W, t), BF16)],
        sem=("parallel",), comm=comm)


def _attn_bwd(ua, d_out, ctab, stab, sinks, name, comm=None):
    t = ua.shape[0]
    nb = t // ATTN_BLOCK
    ua_specs, tab_specs = _attn_specs(nb)

    def body(q_ref, kc_ref, kp_ref, vc_ref, vp_ref, cc_ref, sc_ref, cp_ref, sp_ref, do_ref, sink_ref,
             dua_ref, dua_t_ref, dbias_ref, dsink_ref, dq_c, dk_c, dv_c, dq_n, dk_n, dv_n):
        n = pl.program_id(0)

        @pl.when(n == 0)
        def _():
            dq_c[...] = jnp.zeros_like(dq_c)
            dk_c[...] = jnp.zeros_like(dk_c)
            dv_c[...] = jnp.zeros_like(dv_c)
            dbias_ref[...] = jnp.zeros_like(dbias_ref)
            dsink_ref[...] = jnp.zeros_like(dsink_ref)

        @pl.when(n == nb)
        def _():
            dq_n[...] = jnp.zeros_like(dq_n)
            dk_n[...] = jnp.zeros_like(dk_n)
            dv_n[...] = jnp.zeros_like(dv_n)

        @pl.when(n < nb)
        def _():
            cc, sc = cc_ref[...], sc_ref[...]
            kb, kb_r, vb, vb_r, _, kb_f32 = _attn_band(n, kc_ref[...], kp_ref[...], vc_ref[...], vp_ref[...],
                                                       cc, sc, cp_ref[...], sp_ref[...])
            valid_t = _attn_valid_t(n)
            c4, s4 = _tile4(cc), _tile4(sc)
            qr = _rope(q_ref[...], c4, s4)
            do = do_ref[...].astype(F32)
            lane = lax.broadcasted_iota(jnp.int32, (ATTN_BLOCK, LANES), 1)
            lo = lane < 64
            lane_row = lax.broadcasted_iota(jnp.int32, (1, LANES), 1)
            k_t = {False: kb_f32.T.astype(BF16), True: pltpu.roll(kb_f32, 64, 1).T.astype(BF16)}
            heads = []
            for j in range(4):
                qj = qr[:, j * LANES:(j + 1) * LANES]
                doj = do[:, j * LANES:(j + 1) * LANES]
                for is_lo in (True, False):
                    aligned = is_lo == (j < 2)
                    msk = lo if is_lo else jnp.logical_not(lo)
                    kk = kb if aligned else kb_r
                    vv = vb if aligned else vb_r
                    qm = jnp.where(msk, qj, 0.0).astype(BF16)
                    dom = jnp.where(msk, doj, 0.0).astype(BF16)
                    heads.append(dict(
                        aligned=aligned, qm=qm, dom=dom, sink=sink_ref[0, len(heads)],
                        raw_t=lax.dot_general(kk, qm, _NT, preferred_element_type=F32),
                        dp_t=lax.dot_general(vv, dom, _NT, preferred_element_type=F32)))
            dk_band = jnp.zeros((2 * ATTN_BLOCK, LANES), F32)
            dv_band = jnp.zeros((2 * ATTN_BLOCK, LANES), F32)
            dsink = jnp.zeros((1, LANES), F32)
            for head, hd in enumerate(heads):
                probs_t, psink = _attn_probs(hd["raw_t"], valid_t, hd["sink"], 0)
                delta_t = jnp.sum(probs_t * hd["dp_t"], axis=0, keepdims=True)
                hd["ds_t"] = (probs_t * (hd["dp_t"] - delta_t) * ATTN_SCALE).astype(BF16)
                dsink = dsink + jnp.where(lane_row == head, -jnp.sum(psink * delta_t), 0.0)
                dk_h = lax.dot_general(hd["ds_t"], hd["qm"], _NN, preferred_element_type=F32)
                dv_h = lax.dot_general(probs_t.astype(BF16), hd["dom"], _NN, preferred_element_type=F32)
                if not hd["aligned"]:
                    dk_h = pltpu.roll(dk_h, 64, 1)
                    dv_h = pltpu.roll(dv_h, 64, 1)
                dk_band = dk_band + dk_h
                dv_band = dv_band + dv_h
            row_lo = lax.broadcasted_iota(jnp.int32, (LANES, ATTN_BLOCK), 0) < 64
            dq_t = [lax.dot_general(k_t[not hd["aligned"]], hd["ds_t"], _NN, preferred_element_type=F32)
                    for hd in heads]
            dqs = [jnp.where(row_lo, dq_t[2 * j], dq_t[2 * j + 1]).T for j in range(4)]
            dq_n[...] = _rope_bwd(jnp.concatenate(dqs, axis=1), c4, s4)
            dk_n[...] = dk_band
            dv_n[...] = dv_band
            dsink_ref[...] += dsink

        dk_prev = _rope_bwd(dk_c[...] + dk_n[0:ATTN_BLOCK, :], cp_ref[...], sp_ref[...])
        dv_prev = dv_c[...] + dv_n[0:ATTN_BLOCK, :]
        full = jnp.concatenate([dq_c[...], dk_prev, dv_prev], axis=1)
        dua_ref[...] = full.astype(dua_ref.dtype)
        for j in range(UA_W // LANES):
            dua_t_ref[j * LANES:(j + 1) * LANES, :] = full[:, j * LANES:(j + 1) * LANES].T.astype(dua_t_ref.dtype)
        dbias_ref[...] += jnp.sum(full, axis=0, keepdims=True)
        dq_c[...] = dq_n[...]
        dk_c[...] = dk_n[ATTN_BLOCK:, :]
        dv_c[...] = dv_n[ATTN_BLOCK:, :]

    return _call(
        body, name=name, grid=(nb + 1,), ins=[ua, ua, ua, ua, ua, ctab, stab, ctab, stab, d_out, sinks],
        in_specs=ua_specs + tab_specs + [
            pl.BlockSpec((ATTN_BLOCK, ATTN_W), lambda n: (jnp.minimum(n, nb - 1), 0)),
            pl.BlockSpec(memory_space=pltpu.SMEM)],
        out_specs=[pl.BlockSpec((ATTN_BLOCK, UA_W), lambda n: (jnp.maximum(n - 1, 0), 0)),
                   pl.BlockSpec((UA_W, ATTN_BLOCK), lambda n: (0, jnp.maximum(n - 1, 0))),
                   pl.BlockSpec((1, UA_W), lambda n: (0, 0)),
                   pl.BlockSpec((1, LANES), lambda n: (0, 0))],
        out_shape=[jax.ShapeDtypeStruct((t, UA_W), BF16), jax.ShapeDtypeStruct((UA_W, t), BF16),
                   jax.ShapeDtypeStruct((1, UA_W), F32),
                   jax.ShapeDtypeStruct((1, LANES), F32)],
        scratch_shapes=[pltpu.VMEM((ATTN_BLOCK, ATTN_W), F32), pltpu.VMEM((ATTN_BLOCK, KV_W), F32),
                        pltpu.VMEM((ATTN_BLOCK, KV_W), F32), pltpu.VMEM((ATTN_BLOCK, ATTN_W), F32),
                        pltpu.VMEM((2 * ATTN_BLOCK, KV_W), F32), pltpu.VMEM((2 * ATTN_BLOCK, KV_W), F32)],
        sem=("arbitrary",), comm=comm)


def _tri_mats():
    r = lax.broadcasted_iota(jnp.int32, (HGRN_CHUNK, LANES), 0)
    c = lax.broadcasted_iota(jnp.int32, (HGRN_CHUNK, LANES), 1)
    lower = ((c <= r) & (c < HGRN_CHUNK)).astype(F32)
    upper = ((c >= r) & (c < HGRN_CHUNK)).astype(F32)
    return lower, upper


def _tri_apply(tri, g):
    pad = jnp.concatenate([g, jnp.zeros_like(g)], axis=0)
    return lax.dot_general(tri, pad, _NN, precision=lax.Precision.HIGHEST, preferred_element_type=F32)


def _sub_masks():
    s = lax.broadcasted_iota(jnp.int32, (HGRN_CHUNK, LANES), 0)
    tt = lax.broadcasted_iota(jnp.int32, (HGRN_CHUNK, LANES), 1)
    return [(tt >= HGRN_SUB * i) & (tt < HGRN_SUB * (i + 1)) & (s <= tt) for i in range(HGRN_CHUNK // HGRN_SUB)]


def _hgrn_gates(hq, hf, lb_ref, b_scr):
    lb = _sig(lb_ref[0:1, :] - lb_ref[1:2, :])
    q = hq * _sig(hq)
    sg = _sig(hf)
    f = lb + (1.0 - lb) * sg
    k = 1.0 - f
    lower, _ = _tri_mats()
    b = _tri_apply(lower, jnp.log(f))
    b_scr[...] = b
    nsub = HGRN_CHUNK // HGRN_SUB
    starts = [jnp.zeros((1, HG_W), F32)] + [b_scr[HGRN_SUB * i - 1:HGRN_SUB * i, :] for i in range(1, nsub)]
    pq = jnp.concatenate([jnp.broadcast_to(p, (HGRN_SUB, HG_W)) for p in starts], axis=0)
    b_last = b_scr[HGRN_CHUNK - 1:HGRN_CHUNK, :]
    e_q = jnp.exp(b - pq)
    e_k = [jnp.exp(jnp.minimum(p - b, EXP_CLAMP)) for p in starts]
    e_b = jnp.exp(b)
    e_bl = jnp.exp(b_last - b)
    e_last = jnp.exp(b_last)
    return q, sg, f, k, lb, e_q, e_k, e_b, e_bl, e_last


def _sub_masks_ts():
    tt = lax.broadcasted_iota(jnp.int32, (HGRN_CHUNK, LANES), 0)
    s = lax.broadcasted_iota(jnp.int32, (HGRN_CHUNK, LANES), 1)
    return [(tt >= HGRN_SUB * i) & (tt < HGRN_SUB * (i + 1)) & (s <= tt) for i in range(HGRN_CHUNK // HGRN_SUB)]


def _masked_sum(blocks, masks, axis):
    step = HGRN_CHUNK if axis == 0 else LANES
    acc = jnp.zeros((HGRN_CHUNK, LANES), F32)
    for i, msk in enumerate(masks):
        blk = blocks[step * i:step * (i + 1), :] if axis == 0 else blocks[:, step * i:step * (i + 1)]
        acc = acc + jnp.where(msk, blk, 0.0)
    return acc


def _store_transposed(out_t_ref, chunk_rows):
    width = chunk_rows[0].shape[1]
    if len(chunk_rows) == 1:
        groups = [jnp.concatenate([chunk_rows[0], jnp.zeros_like(chunk_rows[0])], axis=0)]
    else:
        groups = [jnp.concatenate(chunk_rows[g:g + 2], axis=0) for g in range(0, len(chunk_rows), 2)]
    for g, rows in enumerate(groups):
        for c in range(width // LANES):
            tile = rows[:, c * LANES:(c + 1) * LANES].T.astype(out_t_ref.dtype)
            if len(chunk_rows) == 1:
                out_t_ref[c * LANES:(c + 1) * LANES, :] = tile[:, 0:HGRN_CHUNK]
            else:
                out_t_ref[c * LANES:(c + 1) * LANES, g * LANES:(g + 1) * LANES] = tile


def _hgrn_chunk_inputs(j, hq_ref, hf_ref, hi_ref, hg_ref, lb_ref, b_scr):
    rows = slice(j * HGRN_CHUNK, (j + 1) * HGRN_CHUNK)
    hq, hf, v, hg = hq_ref[rows, :], hf_ref[rows, :], hi_ref[rows, :], hg_ref[rows, :]
    q, sg, f, k, lb, e_q, e_k, e_b, e_bl, e_last = _hgrn_gates(hq, hf, lb_ref, b_scr.at[j])
    return dict(rows=rows, hq=hq, v=v, hg=hg, q=q, sg=sg, f=f, k=k, lb=lb, e_q=e_q, e_k=e_k, e_b=e_b, e_bl=e_bl,
                e_last=e_last, qt=q * e_q, qb=q * e_b, kd=k * e_bl, khat=[k * e for e in e_k])


def _hgrn_fwd(uh, lb_raw, norm_g, name, comm=None):
    t = uh.shape[0]
    nc = t // HGRN_CHUNK
    cps = _pick(nc, (HGRN_CHUNKS_PER_STEP, 2, 1))
    rows_step = cps * HGRN_CHUNK

    def body(hq_ref, hf_ref, hi_ref, hg_ref, lb_ref, ng_ref, r_ref, r_t_ref, o_ref, st_out_ref, st_ref, b_scr):
        @pl.when(pl.program_id(0) == 0)
        def _():
            st_ref[...] = jnp.zeros_like(st_ref)

        masks = _sub_masks_ts()
        ng = ng_ref[...]
        zpad = jnp.zeros((HGRN_CHUNK, LANES), F32)
        heads = [slice(h * LANES, (h + 1) * LANES) for h in range(4)]
        chunks = [_hgrn_chunk_inputs(j, hq_ref, hf_ref, hi_ref, hg_ref, lb_ref, b_scr) for j in range(cps)]
        for ch in chunks:
            ch["scores"] = [_dot3(ch["qt"][:, sl],
                                  jnp.concatenate([x for kh in ch["khat"] for x in (kh[:, sl], zpad)], axis=0), _NT)
                            for sl in heads]
        for j, ch in enumerate(chunks):
            o_heads, y_heads = [], []
            for h, sl in enumerate(heads):
                a_ts = _masked_sum(ch["scores"][h], masks, 1)
                vh = ch["v"][:, sl].astype(BF16)
                v_pad = jnp.concatenate([vh, jnp.zeros_like(vh)], axis=0)
                o_intra = lax.dot_general(a_ts.astype(BF16), v_pad, _NN, preferred_element_type=F32)
                st = st_ref[h]
                st_out_ref[j, h] = st
                o_inter = _dot(ch["qb"][:, sl], st, _NT)
                st_ref[h] = st * ch["e_last"][:, sl] + _dot(vh, ch["kd"][:, sl], _TN)
                oh = o_intra + o_inter
                rs = lax.rsqrt(jnp.mean(oh * oh, axis=1, keepdims=True) + RMS_EPS)
                o_heads.append(oh)
                y_heads.append(oh * rs * ng)
            hg = ch["hg"]
            o_ref[ch["rows"], :] = jnp.concatenate(o_heads, axis=1)
            ch["r"] = jnp.concatenate(y_heads, axis=1) * (hg * _sig(hg))
            r_ref[ch["rows"], :] = ch["r"].astype(r_ref.dtype)
        _store_transposed(r_t_ref, [ch["r"] for ch in chunks])

    col = lambda j: pl.BlockSpec((rows_step, HG_W), lambda c: (c, j))
    return _call(
        body, name=name, grid=(nc // cps,), ins=[uh, uh, uh, uh, lb_raw, norm_g],
        in_specs=[col(0), col(1), col(2), col(3),
                  pl.BlockSpec((2, HG_W), lambda c: (0, 0)), pl.BlockSpec((1, LANES), lambda c: (0, 0))],
        out_specs=[pl.BlockSpec((rows_step, HG_W), lambda c: (c, 0)),
                   pl.BlockSpec((HG_W, rows_step), lambda c: (0, c)),
                   pl.BlockSpec((rows_step, HG_W), lambda c: (c, 0)),
                   pl.BlockSpec((cps, 4, LANES, LANES), lambda c: (c, 0, 0, 0))],
        out_shape=[jax.ShapeDtypeStruct((t, HG_W), BF16), jax.ShapeDtypeStruct((HG_W, t), BF16),
                   jax.ShapeDtypeStruct((t, HG_W), F32), jax.ShapeDtypeStruct((nc, 4, LANES, LANES), F32)],
        scratch_shapes=[pltpu.VMEM((4, LANES, LANES), F32), pltpu.VMEM((cps, HGRN_CHUNK, HG_W), F32)],
        sem=("arbitrary",), comm=comm)


def _hgrn_bwd(uh, o_pre, d_r, states, lb_raw, norm_g, name, comm=None):
    t = uh.shape[0]
    nc = t // HGRN_CHUNK
    cps = _pick(nc, (HGRN_CHUNKS_PER_STEP, 2, 1))
    ns = nc // cps
    rows_step = cps * HGRN_CHUNK
    nsub = HGRN_CHUNK // HGRN_SUB

    def body(hq_ref, hf_ref, hi_ref, hg_ref, o_ref, dr_ref, st_in_ref, lb_ref, ng_ref,
             duh_ref, duh_t_ref, dbias_ref, dng_ref, dlb_ref, dst_ref, b_scr, dlb_acc):
        i = pl.program_id(0)

        @pl.when(i == 0)
        def _():
            dst_ref[...] = jnp.zeros_like(dst_ref)
            dbias_ref[...] = jnp.zeros_like(dbias_ref)
            dng_ref[...] = jnp.zeros_like(dng_ref)
            dlb_acc[...] = jnp.zeros_like(dlb_acc)

        masks_st = _sub_masks()
        masks_ts = _sub_masks_ts()
        ng = ng_ref[...]
        zpad = jnp.zeros((HGRN_CHUNK, LANES), F32)
        _, upper = _tri_mats()
        heads = [slice(h * LANES, (h + 1) * LANES) for h in range(4)]
        row = lax.broadcasted_iota(jnp.int32, (HGRN_CHUNK, HG_W), 0)

        chunks = [_hgrn_chunk_inputs(j, hq_ref, hf_ref, hi_ref, hg_ref, lb_ref, b_scr) for j in range(cps)]
        dng = jnp.zeros((1, LANES), F32)
        for ch in chunks:
            o = o_ref[ch["rows"], :]
            dr = dr_ref[ch["rows"], :].astype(F32)
            hg = ch["hg"]
            sgg = _sig(hg)
            dy = dr * (hg * sgg)
            do_h, y_h = [], []
            for sl in heads:
                oh = o[:, sl]
                rs = lax.rsqrt(jnp.mean(oh * oh, axis=1, keepdims=True) + RMS_EPS)
                y_h.append(oh * rs * ng)
                dng = dng + jnp.sum(dy[:, sl] * oh * rs, axis=0, keepdims=True)
                w = dy[:, sl] * ng
                do_h.append(rs * (w - oh * (rs * rs) * jnp.mean(w * oh, axis=1, keepdims=True)))
            ch["do"] = do_h
            ch["dhg"] = dr * jnp.concatenate(y_h, axis=1) * _dsilu(hg, sgg)

        for ch in chunks:
            ch["kst"], ch["kpad"], ch["qt_pad"], ch["v_b"], ch["do_pad"] = [], [], [], [], []
            ch["ats"], ch["d_at"], ch["d_a"] = [], [], []
            for h, sl in enumerate(heads):
                kst = jnp.concatenate([kh[:, sl] for kh in ch["khat"]], axis=0)
                kpad = jnp.concatenate([x for kh in ch["khat"] for x in (kh[:, sl], zpad)], axis=0)
                qt_pad = jnp.concatenate([ch["qt"][:, sl], zpad], axis=0)
                vh = ch["v"][:, sl].astype(BF16)
                v_pad = jnp.concatenate([vh, jnp.zeros_like(vh)], axis=0)
                do_b = ch["do"][h].astype(BF16)
                do_pad = jnp.concatenate([do_b, jnp.zeros_like(do_b)], axis=0)
                ch["kst"].append(kst)
                ch["kpad"].append(kpad)
                ch["qt_pad"].append(qt_pad)
                ch["v_b"].append(vh)
                ch["do_pad"].append(do_pad)
                ch["ats"].append(_dot3(kst, qt_pad, _NT))
                ch["d_at"].append(lax.dot_general(vh, do_pad, _NT, preferred_element_type=F32))
                ch["d_a"].append(lax.dot_general(do_b, v_pad, _NT, preferred_element_type=F32))

        for ch in chunks:
            ch["d_kst"], ch["d_qt"], ch["dv"] = [], [], []
            for h in range(4):
                at = _masked_sum(ch["ats"][h], masks_st, 0)
                d_ats = jnp.concatenate([jnp.where(m, ch["d_at"][h], 0.0) for m in masks_st], axis=0)
                d_a_cat = jnp.concatenate([jnp.where(m, ch["d_a"][h], 0.0) for m in masks_ts], axis=1)
                ch["d_kst"].append(_dot3(d_ats, ch["qt_pad"][h], _NN))
                ch["d_qt"].append(_dot3(d_a_cat, ch["kpad"][h], _NN))
                ch["dv"].append(lax.dot_general(at.astype(BF16), ch["do_pad"][h], _NN, preferred_element_type=F32))

        for j in reversed(range(cps)):
            ch = chunks[j]
            q, k, sg, f, lb = ch["q"], ch["k"], ch["sg"], ch["f"], ch["lb"]
            dq_h, dk_h, dv_h, extra_h = [], [], [], []
            for h, sl in enumerate(heads):
                st_prev = st_in_ref[j, h]
                d_st = dst_ref[h]
                d_st_b = d_st.astype(BF16)
                do_b = ch["do_pad"][h][0:HGRN_CHUNK, :]
                kd, e_last = ch["kd"][:, sl], ch["e_last"][:, sl]
                dv = ch["dv"][h] + _dot(kd, d_st_b, _NT)
                d_qb = _dot(do_b, st_prev, _NN)
                d_kd = lax.dot_general(ch["v_b"][h], d_st_b, _NN, preferred_element_type=F32)
                extra_h.append(jnp.sum(st_prev * d_st, axis=0, keepdims=True) * e_last
                               + jnp.sum(kd * d_kd, axis=0, keepdims=True))
                dst_ref[h] = d_st * e_last + _dot(do_b, ch["qb"][:, sl], _TN)
                dq_h.append(ch["d_qt"][h] * ch["e_q"][:, sl] + d_qb * ch["e_b"][:, sl])
                dkk = d_kd * ch["e_bl"][:, sl]
                for s_ in range(nsub):
                    dkk = dkk + ch["d_kst"][h][HGRN_CHUNK * s_:HGRN_CHUNK * (s_ + 1), :] * ch["e_k"][s_][:, sl]
                dk_h.append(dkk)
                dv_h.append(dv)
            dq = jnp.concatenate(dq_h, axis=1)
            dk = jnp.concatenate(dk_h, axis=1)
            dv = jnp.concatenate(dv_h, axis=1)
            extra = jnp.concatenate(extra_h, axis=1)
            db = q * dq - k * dk + jnp.where(row == HGRN_CHUNK - 1, extra, 0.0)
            dg = _tri_apply(upper, db)
            df = dg / f - dk
            dhf = df * (1.0 - lb) * sg * (1.0 - sg)
            dhq = dq * _dsilu(ch["hq"], _sig(ch["hq"]))
            full = jnp.concatenate([dhq, dhf, dv, ch["dhg"]], axis=1)
            duh_ref[ch["rows"], :] = full.astype(duh_ref.dtype)
            ch["full"] = full
            dbias_ref[...] += jnp.sum(full, axis=0, keepdims=True)
            dlb_acc[...] += jnp.sum(df * (1.0 - sg), axis=0, keepdims=True)
        dng_ref[...] += dng
        _store_transposed(duh_t_ref, [ch["full"] for ch in chunks])

        @pl.when(i == ns - 1)
        def _():
            lb = chunks[0]["lb"]
            d_a0 = dlb_acc[...] * lb * (1.0 - lb)
            r8 = lax.broadcasted_iota(jnp.int32, (8, HG_W), 0)
            dlb_ref[...] = jnp.where(r8 == 0, d_a0, jnp.where(r8 == 1, -d_a0, 0.0))

    col = lambda j: pl.BlockSpec((rows_step, HG_W), lambda i: (ns - 1 - i, j))
    return _call(
        body, name=name, grid=(ns,), ins=[uh, uh, uh, uh, o_pre, d_r, states, lb_raw, norm_g],
        in_specs=[col(0), col(1), col(2), col(3), col(0), col(d_r.shape[1] // HG_W - 1),
                  pl.BlockSpec((cps, 4, LANES, LANES), lambda i: (ns - 1 - i, 0, 0, 0)),
                  pl.BlockSpec((2, HG_W), lambda i: (0, 0)), pl.BlockSpec((1, LANES), lambda i: (0, 0))],
        out_specs=[pl.BlockSpec((rows_step, UH_W), lambda i: (ns - 1 - i, 0)),
                   pl.BlockSpec((UH_W, rows_step), lambda i: (0, ns - 1 - i)),
                   pl.BlockSpec((1, UH_W), lambda i: (0, 0)),
                   pl.BlockSpec((1, LANES), lambda i: (0, 0)),
                   pl.BlockSpec((8, HG_W), lambda i: (0, 0))],
        out_shape=[jax.ShapeDtypeStruct((t, UH_W), BF16), jax.ShapeDtypeStruct((UH_W, t), BF16),
                   jax.ShapeDtypeStruct((1, UH_W), F32),
                   jax.ShapeDtypeStruct((1, LANES), F32), jax.ShapeDtypeStruct((8, HG_W), F32)],
        scratch_shapes=[pltpu.VMEM((4, LANES, LANES), F32), pltpu.VMEM((cps, HGRN_CHUNK, HG_W), F32),
                        pltpu.VMEM((1, HG_W), F32)],
        sem=("arbitrary",), comm=comm)


def _ln_bwd_math(dy, xhat, rstd, g):
    dxh = dy * g
    return rstd * (dxh - jnp.mean(dxh, axis=1, keepdims=True)
                   - xhat * jnp.mean(dxh * xhat, axis=1, keepdims=True))


def _mm_rows(a, b, extras, *, name, epilogue, out_shape, out_specs, tb=False, tm=512, tk=1408, pair2=None):
    m, kdim = a.shape
    n = b.shape[0] if tb else b.shape[1]
    tm = _pick(m, (tm, 256, 128))
    tk = _pick(kdim, (tk, 1408, 1024, 768, 512, 256, 128))
    nk = kdim // tk
    b_spec = pl.BlockSpec((n, tk), lambda i, k: (0, k)) if tb else pl.BlockSpec((tk, n), lambda i, k: (k, 0))
    dims = _NT if tb else _NN
    n_ex, n_out, n_p2 = len(extras), len(out_shape), (0 if pair2 is None else 2)

    def body(*refs):
        a_ref, b_ref = refs[0], refs[1]
        p2_refs = refs[2:2 + n_p2]
        ex_refs = refs[2 + n_p2:2 + n_p2 + n_ex]
        o_refs = refs[2 + n_p2 + n_ex:2 + n_p2 + n_ex + n_out]
        acc_ref = refs[-1]
        i, k = pl.program_id(0), pl.program_id(1)

        @pl.when(k == 0)
        def _():
            if n_p2:
                acc_ref[...] = _dot(p2_refs[0][...], p2_refs[1][...], _NN)
            else:
                acc_ref[...] = jnp.zeros_like(acc_ref)

        acc_ref[...] += _dot(a_ref[...], b_ref[...], dims)

        @pl.when(k == nk - 1)
        def _():
            epilogue(acc_ref[...], ex_refs, o_refs, i == 0)

    p2_specs, p2_ins = [], []
    if pair2 is not None:
        k2 = pair2[0].shape[1]
        p2_specs = [pl.BlockSpec((tm, k2), lambda i, k: (i, 0)), pl.BlockSpec((k2, n), lambda i, k: (0, 0))]
        p2_ins = list(pair2)
    return pl.pallas_call(
        body, name=name, grid=(m // tm, nk),
        in_specs=[pl.BlockSpec((tm, tk), lambda i, k: (i, k)), b_spec] + p2_specs + [sp for _, sp in extras],
        out_specs=list(out_specs), out_shape=list(out_shape),
        scratch_shapes=[pltpu.VMEM((tm, n), F32)],
        compiler_params=_cp("arbitrary", "arbitrary"),
    )(a, b, *p2_ins, *[arr for arr, _ in extras])


def _rows_specs(tm, d):
    row = pl.BlockSpec((tm, d), lambda i, k: (i, 0))
    vec = pl.BlockSpec((1, d), lambda i, k: (0, 0))
    col = pl.BlockSpec((tm, 1), lambda i, k: (i, 0))
    return row, vec, col


def _mm_ln_fwd(a, b, pair2, addend, g, beta, name, tm=512):
    t, d = addend.shape
    tm = _pick(t, (tm, 256, 128))
    row, vec, col = _rows_specs(tm, d)

    def epilogue(acc, ex, outs, first):
        z = acc + ALPHA * ex[0][...]
        mu = jnp.mean(z, axis=1, keepdims=True)
        zc = z - mu
        rstd = lax.rsqrt(jnp.mean(zc * zc, axis=1, keepdims=True) + LN_EPS)
        xhat = zc * rstd
        h = xhat * ex[1][...] + ex[2][...]
        outs[0][...] = h
        outs[1][...] = h.astype(BF16)
        outs[2][...] = xhat
        outs[3][...] = rstd

    return _mm_rows(a, b, [(addend, row), (g, vec), (beta, vec)], name=name, epilogue=epilogue, tm=tm, pair2=pair2,
                    out_shape=[jax.ShapeDtypeStruct((t, d), F32), jax.ShapeDtypeStruct((t, d), BF16),
                               jax.ShapeDtypeStruct((t, d), F32), jax.ShapeDtypeStruct((t, 1), F32)],
                    out_specs=[row, row, row, col])


CONV_RB = 32
HALO = 8


def _sum8(x):
    acc = x[0:8]
    for r in range(8, x.shape[0], 8):
        acc = acc + x[r:r + 8]
    return acc


FFN_TILE = 256
FFN_COLS = 256


def _rows_before(win, k):
    return pltpu.roll(win, k, 0)[HALO:]


def _rows_after(win, k):
    n = win.shape[0]
    return pltpu.roll(win, n - k, 0)[0:n - HALO]


def _resident(shape):
    return pl.BlockSpec(shape, lambda i: (0,) * len(shape), pipeline_mode=pl.Buffered(1))


def _ffn_fwd(h1b, h1, w_up_t, conv_w, conv_b, w_down, target, ln2_g, ln2_b, name, comm=None):
    t, d = h1.shape
    tr = _pick(t, (FFN_TILE, 128))
    nblk = D_FF // FFN_COLS
    rb = CONV_RB

    def body(a_ref, wup_ref, cw_ref, cb_ref, wd_ref, h1_ref, tgt_ref, g_ref, b_ref,
             u2_ref, hm_ref, dz_ref, dg_ref, db_ref, loss_ref, ext):
        i = pl.program_id(0)

        @pl.when(i == 0)
        def _():
            ext[0:HALO, :] = jnp.zeros((HALO, D_FF), F32)
            dg_ref[...] = jnp.zeros_like(dg_ref)
            db_ref[...] = jnp.zeros_like(db_ref)
            loss_ref[...] = jnp.zeros_like(loss_ref)

        a = a_ref[...]
        for c in range(nblk):
            cs = slice(c * FFN_COLS, (c + 1) * FFN_COLS)
            vs = slice(D_FF + c * FFN_COLS, D_FF + (c + 1) * FFN_COLS)
            gate_pre = lax.dot_general(a, wup_ref[cs, :], _NT, preferred_element_type=F32)
            u2_ref[:, cs] = gate_pre
            ext[HALO:, cs] = gate_pre
            u2_ref[:, vs] = lax.dot_general(a, wup_ref[vs, :], _NT, preferred_element_type=F32)
        acc = jnp.zeros((tr, d), F32)
        for c in range(nblk):
            cs = slice(c * FFN_COLS, (c + 1) * FFN_COLS)
            for sub in range(FFN_COLS // LANES):
                ln = slice(c * FFN_COLS + sub * LANES, c * FFN_COLS + (sub + 1) * LANES)
                vl = slice(D_FF + c * FFN_COLS + sub * LANES, D_FF + c * FFN_COLS + (sub + 1) * LANES)
                w0, w1, w2, bb = cw_ref[0:1, ln], cw_ref[1:2, ln], cw_ref[2:3, ln], cb_ref[:, ln]
                for r0 in range(0, tr, rb):
                    win = ext[r0:r0 + HALO + rb, ln]
                    gate = _rows_before(win, 2) * w0 + _rows_before(win, 1) * w1 + win[HALO:] * w2 + bb
                    hm_ref[r0:r0 + rb, ln] = (gate * _sig(gate) * u2_ref[r0:r0 + rb, vl]).astype(hm_ref.dtype)
            acc = acc + lax.dot_general(hm_ref[:, cs], wd_ref[cs, :], _NN, preferred_element_type=F32)
        ext[0:HALO, :] = ext[tr:tr + HALO, :]

        z = acc + ALPHA * h1_ref[...]
        gg = g_ref[...]
        mu = jnp.mean(z, axis=1, keepdims=True)
        zc = z - mu
        rstd = lax.rsqrt(jnp.mean(zc * zc, axis=1, keepdims=True) + LN_EPS)
        xhat = zc * rstd
        err = xhat * gg + b_ref[...] - tgt_ref[...]
        loss_ref[...] += 0.5 * jnp.sum(jnp.mean(err * err, axis=1, keepdims=True))
        dy = err * (1.0 / d)
        dz_ref[...] = _ln_bwd_math(dy, xhat, rstd, gg)
        dg_ref[...] += jnp.sum(dy * xhat, axis=0, keepdims=True)
        db_ref[...] += jnp.sum(dy, axis=0, keepdims=True)

    row = lambda w: pl.BlockSpec((tr, w), lambda i: (i, 0))
    vec = pl.BlockSpec((1, d), lambda i: (0, 0))
    return _call(
        body, name=name, grid=(t // tr,),
        ins=[h1b, w_up_t, conv_w, conv_b, w_down, h1, target, ln2_g, ln2_b],
        in_specs=[row(d), _resident((2 * D_FF, d)), _resident((3, D_FF)), _resident((1, D_FF)),
                  _resident((D_FF, d)), row(d), row(d), vec, vec],
        out_specs=[row(2 * D_FF), row(D_FF), row(d), vec, vec, pl.BlockSpec((1, LANES), lambda i: (0, 0))],
        out_shape=[jax.ShapeDtypeStruct((t, 2 * D_FF), F32), jax.ShapeDtypeStruct((t, D_FF), BF16),
                   jax.ShapeDtypeStruct((t, d), F32), jax.ShapeDtypeStruct((1, d), F32),
                   jax.ShapeDtypeStruct((1, d), F32), jax.ShapeDtypeStruct((1, LANES), F32)],
        scratch_shapes=[pltpu.VMEM((tr + HALO, D_FF), F32)],
        sem=("arbitrary",), comm=comm)


def _ffn_bwd(dz2, u2, w_down, w_up_t, conv_w, conv_b, xhat1, rstd1, ln1_g, name, comm=None):
    t, d = dz2.shape
    tr = _pick(t, (FFN_TILE, 128))
    nt = t // tr
    hb = tr // HALO
    nblk = D_FF // FFN_COLS
    rb = CONV_RB

    def body(dz2_ref, dz2_next_ref, u2_ref, gp_prev_ref, wd_ref, wup_ref, cw_ref, cb_ref, xhat_ref, rstd_ref,
             g1_ref, du_ref, dz1_ref, dw_ref, dcb_ref, dg1_ref, db1_ref, head, dh_s, dg_s):
        i = pl.program_id(0)

        @pl.when(i == 0)
        def _():
            dg_s[tr:, :] = jnp.zeros((HALO, D_FF), F32)
            dw_ref[...] = jnp.zeros_like(dw_ref)
            dcb_ref[...] = jnp.zeros_like(dcb_ref)
            dg1_ref[...] = jnp.zeros_like(dg1_ref)
            db1_ref[...] = jnp.zeros_like(db1_ref)

        dz2 = dz2_ref[...]

        @pl.when(i == 0)
        def _():
            dz2_b = dz2.astype(BF16)
            for c in range(nblk):
                cs = slice(c * FFN_COLS, (c + 1) * FFN_COLS)
                dh_s[:, cs] = lax.dot_general(dz2_b, wd_ref[cs, :], _NT, preferred_element_type=F32)

        dz2_next = dz2_next_ref[...].astype(BF16)
        dh_next = [lax.dot_general(dz2_next, wd_ref[c * FFN_COLS:(c + 1) * FFN_COLS, :], _NT,
                                   preferred_element_type=F32) for c in range(nblk)]
        head[0:HALO, :] = jnp.where(i == nt - 1, 0.0, gp_prev_ref[...])
        head[HALO:, :] = u2_ref[0:rb, 0:D_FF]

        acc = jnp.zeros((tr, d), F32)
        for blk in range(nblk):
            for c in range(blk * FFN_COLS // LANES, (blk + 1) * FFN_COLS // LANES):
                ln = slice(c * LANES, (c + 1) * LANES)
                vl = slice(D_FF + c * LANES, D_FF + (c + 1) * LANES)
                w0, w1, w2, bb = cw_ref[0:1, ln], cw_ref[1:2, ln], cw_ref[2:3, ln], cb_ref[:, ln]
                acc_b = jnp.zeros((8, LANES), F32)
                acc_w = [jnp.zeros((8, LANES), F32) for _ in range(3)]
                for r0 in range(0, tr, rb):
                    win = head[:, ln] if r0 == 0 else u2_ref[r0 - HALO:r0 + rb, ln]
                    g_m2, g_m1, g_0 = _rows_before(win, 2), _rows_before(win, 1), win[HALO:]
                    gate = g_m2 * w0 + g_m1 * w1 + g_0 * w2 + bb
                    sg = _sig(gate)
                    dh = dh_s[r0:r0 + rb, ln]
                    dgate = dh * u2_ref[r0:r0 + rb, vl] * _dsilu(gate, sg)
                    dg_s[r0:r0 + rb, ln] = dgate
                    du_ref[r0:r0 + rb, vl] = (dh * (gate * sg)).astype(du_ref.dtype)
                    acc_b = acc_b + _sum8(dgate)
                    acc_w[0] = acc_w[0] + _sum8(dgate * g_m2)
                    acc_w[1] = acc_w[1] + _sum8(dgate * g_m1)
                    acc_w[2] = acc_w[2] + _sum8(dgate * g_0)
                dcb_ref[:, ln] += jnp.sum(acc_b, axis=0, keepdims=True)
                for j in range(3):
                    dw_ref[j:j + 1, ln] += jnp.sum(acc_w[j], axis=0, keepdims=True)
                for r0 in range(0, tr, rb):
                    win = dg_s[r0:r0 + rb + HALO, ln]
                    d_gp = _rows_after(win, 2) * w0 + _rows_after(win, 1) * w1 + win[0:rb] * w2
                    du_ref[r0:r0 + rb, ln] = d_gp.astype(du_ref.dtype)
            cs = slice(blk * FFN_COLS, (blk + 1) * FFN_COLS)
            vs = slice(D_FF + blk * FFN_COLS, D_FF + (blk + 1) * FFN_COLS)
            acc = acc + lax.dot_general(du_ref[:, cs], wup_ref[cs, :], _NN, preferred_element_type=F32)
            acc = acc + lax.dot_general(du_ref[:, vs], wup_ref[vs, :], _NN, preferred_element_type=F32)
        dg_s[tr:, :] = dg_s[0:HALO, :]
        for c in range(nblk):
            dh_s[:, c * FFN_COLS:(c + 1) * FFN_COLS] = dh_next[c]
        dy = acc + ALPHA * dz2
        xh = xhat_ref[...]
        dz1_ref[...] = _ln_bwd_math(dy, xh, rstd_ref[...], g1_ref[...])
        dg1_ref[...] += jnp.sum(dy * xh, axis=0, keepdims=True)
        db1_ref[...] += jnp.sum(dy, axis=0, keepdims=True)

    rev = lambda w: pl.BlockSpec((tr, w), lambda i: (nt - 1 - i, 0))
    vec = pl.BlockSpec((1, d), lambda i: (0, 0))
    return _call(
        body, name=name, grid=(nt,),
        ins=[dz2, dz2, u2, u2, w_down, w_up_t, conv_w, conv_b, xhat1, rstd1, ln1_g],
        in_specs=[rev(d), pl.BlockSpec((tr, d), lambda i: (jnp.maximum(nt - 2 - i, 0), 0)), rev(2 * D_FF),
                  pl.BlockSpec((HALO, D_FF), lambda i: (jnp.maximum((nt - 1 - i) * hb - 1, 0), 0)),
                  _resident((D_FF, d)), _resident((2 * D_FF, d)), _resident((3, D_FF)), _resident((1, D_FF)),
                  rev(d), pl.BlockSpec((tr, 1), lambda i: (nt - 1 - i, 0)), vec],
        out_specs=[rev(2 * D_FF), rev(d), pl.BlockSpec((8, D_FF), lambda i: (0, 0)),
                   pl.BlockSpec((1, D_FF), lambda i: (0, 0)), vec, vec],
        out_shape=[jax.ShapeDtypeStruct((t, 2 * D_FF), BF16), jax.ShapeDtypeStruct((t, d), F32),
                   jax.ShapeDtypeStruct((8, D_FF), F32), jax.ShapeDtypeStruct((1, D_FF), F32),
                   jax.ShapeDtypeStruct((1, d), F32), jax.ShapeDtypeStruct((1, d), F32)],
        scratch_shapes=[pltpu.VMEM((HALO + rb, D_FF), F32), pltpu.VMEM((tr, D_FF), F32),
                        pltpu.VMEM((tr + HALO, D_FF), F32)],
        sem=("arbitrary",), comm=comm)


def _adamw(w, g, m, v, name):
    rows, cols = w.shape
    tr = _pick(rows, (256, 128, 64, 32, 16, 8))

    def body(w_ref, g_ref, m_ref, v_ref, d_ref, nm_ref, nv_ref):
        d_ref[...], nm_ref[...], nv_ref[...] = _adamw_math(w_ref[...], g_ref[...], m_ref[...], v_ref[...])

    spec = pl.BlockSpec((tr, cols), lambda i: (i, 0))
    shp = jax.ShapeDtypeStruct((rows, cols), F32)
    return pl.pallas_call(
        body, name=name, grid=(rows // tr,),
        in_specs=[spec, spec, spec, spec], out_specs=[spec, spec, spec], out_shape=[shp, shp, shp],
        compiler_params=_cp("parallel"),
    )(w, g, m, v)


def _pad_rows(a, rows):
    return jnp.pad(a, ((0, rows - a.shape[0]), (0, 0)))


SMALL_LAYOUT = (("ln1_g", 1024), ("ln1_b", 1024), ("b_in", 2816), ("sinks", 8), ("hgrn_lb", 1024),
                ("hgrn_norm_g", 128), ("ln2_g", 1024), ("ln2_b", 1024), ("conv_b", 2816), ("loss", 1))
SMALL_SHAPES = {"ln1_g": (1, 1024), "ln1_b": (1, 1024), "b_in": (1, 2816), "sinks": (1, 8), "hgrn_lb": (2, 512),
                "hgrn_norm_g": (1, 128), "ln2_g": (1, 1024), "ln2_b": (1, 1024), "conv_b": (1, 2816),
                "loss": (1,)}


def _pack_small(parts):
    rows = []
    for name, size in SMALL_LAYOUT:
        flat = parts[name].reshape(-1).astype(F32)
        padded = -(-size // LANES) * LANES
        rows.append(jnp.pad(flat, (0, padded - size)).reshape(-1, LANES))
    return _pad_rows(jnp.concatenate(rows, axis=0), SMALL_ROWS)


def _small_update(small_g, ws, ms, vs, name):
    names = [n for n, _ in SMALL_LAYOUT if n != "loss"]
    first, r = {}, 0
    for n, size in SMALL_LAYOUT:
        first[n] = r
        r += -(-size // LANES)
    npar = len(names)

    def body(*refs):
        g_ref = refs[0]
        w_refs, m_refs, v_refs = (refs[1 + q * npar:1 + (q + 1) * npar] for q in range(3))
        outs = refs[1 + 3 * npar:-1]
        sum_ref = refs[-1]
        acc = g_ref[0]
        for s in range(1, N_DEV):
            acc = acc + g_ref[s]
        sum_ref[...] = acc
        outs[0][...] = sum_ref[first["loss"]:first["loss"] + 1, 0:1]
        for p, n in enumerate(names):
            g_out, d_out, m_out, v_out = outs[1 + 4 * p:5 + 4 * p]
            rows, cols = SMALL_SHAPES[n]
            if cols < LANES:
                g_out[...] = sum_ref[first[n]:first[n] + 1, 0:cols]
            else:
                per = cols // LANES
                for h in range(rows):
                    for j in range(per):
                        rr = first[n] + h * per + j
                        g_out[h:h + 1, j * LANES:(j + 1) * LANES] = sum_ref[rr:rr + 1, :]
            d_out[...], m_out[...], v_out[...] = _adamw_math(w_refs[p][...], g_out[...], m_refs[p][...],
                                                            v_refs[p][...])

    out_shape = [jax.ShapeDtypeStruct((1, 1), F32)]
    for n in names:
        out_shape += [jax.ShapeDtypeStruct(SMALL_SHAPES[n], F32)] * 4
    res = pl.pallas_call(
        body, name=name, out_shape=out_shape,
        scratch_shapes=[pltpu.VMEM((SMALL_ROWS, LANES), F32)],
        compiler_params=_cp(),
    )(small_g, *[ws[n] for n in names], *[ms[n] for n in names], *[vs[n] for n in names])
    return res[0], {n: res[1 + 4 * p:5 + 4 * p] for p, n in enumerate(names)}


def _own(full, rows):
    return lax.dynamic_slice_in_dim(full, _me() * rows, rows, axis=0)


def kernel(x, positions, ln1_g, ln1_b, w_in, b_in, sinks, hgrn_lb, hgrn_norm_g, w_o, ln2_g, ln2_b, w_up, conv_w, conv_b, w_down, loss_target, m_ln1_g, m_ln1_b, m_w_in, m_b_in, m_sinks, m_hgrn_lb, m_hgrn_norm_g, m_w_o, m_ln2_g, m_ln2_b, m_w_up, m_conv_w, m_conv_b, m_w_down, v_ln1_g, v_ln1_b, v_w_in, v_b_in, v_sinks, v_hgrn_lb, v_hgrn_norm_g, v_w_o, v_ln2_g, v_ln2_b, v_w_up, v_conv_w, v_conv_b, v_w_down):
    t = x.shape[1]
    x2 = x[0]
    target = loss_target[0]
    pos_col = positions.reshape(t, 1)

    w_in_t_s = w_in[0].T.astype(BF16)
    w_up_t_s = w_up[0].T.astype(BF16)
    w_o_s = w_o[0].astype(BF16)
    w_down_s = w_down[0].astype(BF16)
    (ctab, stab, xb), (w_in_t_g, cw_g) = _prep(
        pos_col, x2, "prep_ag_w_in", _Comm([{"kind": "gather", "arr": w_in_t_s}, {"kind": "gather", "arr": _pad_rows(conv_w[0], 8)}]))
    w_in_t = w_in_t_g.reshape(D_FF, D_MODEL)
    w_a_t, w_h_t = w_in_t[:UA_W], w_in_t[UA_W:]
    conv_w_f = cw_g[:, 0:3].transpose(1, 0, 2).reshape(3, D_FF)

    ua = _mm(xb, w_a_t, tb=True, bias=b_in[:, :UA_W], name="fwd_in_attn")
    uh = _mm(xb, w_h_t, tb=True, bias=b_in[:, UA_W:], name="fwd_in_hgrn")
    half_up = SHARD_UP // 2
    (a_out, a_out_t), (w_o_g, w_up_half) = _attn_fwd(
        ua, ctab, stab, sinks, "attn_fwd",
        comm=_Comm([{"kind": "gather", "arr": w_o_s},
                    {"kind": "gather", "arr": w_up_t_s, "rows": (0, half_up), "dst_rows": SHARD_UP}]))
    (r_out, r_out_t, o_pre, states), (w_up_t_g, w_down_g) = _hgrn_fwd(
        uh, hgrn_lb, hgrn_norm_g, "hgrn_fwd",
        comm=_Comm([{"kind": "gather", "arr": w_up_t_s, "rows": (half_up, half_up), "dst_rows": SHARD_UP,
                     "dst_first": half_up, "into": w_up_half},
                    {"kind": "gather", "arr": w_down_s}]))
    w_down_f = w_down_g.reshape(D_FF, D_MODEL)
    w_o_f = w_o_g.reshape(D_MODEL, D_MODEL)
    w_up_t = w_up_t_g.reshape(2 * D_FF, D_MODEL)
    h1, h1b, xhat1, rstd1 = _mm_ln_fwd(r_out, w_o_f[ATTN_W:], (a_out, w_o_f[:ATTN_W]), x2, ln1_g, ln1_b,
                                       "fwd_o_ln1")
    u2, hmid, dz2, d_ln2_g, d_ln2_b, loss_part = _ffn_fwd(h1b, h1, w_up_t, conv_w_f, conv_b, w_down_f, target,
                                                         ln2_g, ln2_b, "ffn_fwd")[0]

    d_w_down, d_w_down_b = _mm(hmid, dz2, ta=True, out_dtype2=BF16, tm=1408, tk=1024, name="bwd_down_dw")
    (d_u2, dz1, d_conv_w8, d_conv_b, d_ln1_g, d_ln1_b), (recv_down,) = _ffn_bwd(
        dz2, u2, w_down_f, w_up_t, conv_w_f, conv_b, xhat1, rstd1, ln1_g, "ffn_bwd",
        comm=_Comm([{"kind": "exchange", "arr": d_w_down_b.reshape(N_DEV, SHARD_DOWN, D_MODEL)}]))
    d_w_up_t, d_w_up_t_b = _mm(d_u2, h1b, ta=True, out_dtype2=BF16, tm=1408, tk=1024, name="bwd_up_dw")
    d_ar = _mm(dz1, w_o_f, tb=True, name="bwd_o_dx")
    d_w_o_part = _mm(a_out_t, dz1, out_dtype2=BF16, tm=ATTN_W, out_rows=D_MODEL, name="bwd_o_dw_attn")
    d_w_o, d_w_o_b = _mm(r_out_t, dz1, out_dtype2=BF16, tm=HG_W, out_rows=D_MODEL, first_row=ATTN_W,
                         into=d_w_o_part, name="bwd_o_dw_hgrn")
    d_w_up_x = d_w_up_t_b.reshape(N_DEV, SHARD_UP, D_MODEL)
    half = SHARD_UP // 2
    d_cw_x = d_conv_w8.reshape(8, N_DEV, SHARD_IN).transpose(1, 0, 2)
    (d_ua, d_ua_t, d_bias_a, d_sinks), (recv_up_half, recv_cw) = _attn_bwd(
        ua, d_ar, ctab, stab, sinks, "attn_bwd",
        comm=_Comm([{"kind": "exchange", "arr": d_w_up_x, "rows": (0, half), "dst_rows": SHARD_UP},
                    {"kind": "exchange", "arr": d_cw_x}]))
    (d_uh, d_uh_t, d_bias_h, d_norm_g, d_lb8), (recv_up, recv_o) = _hgrn_bwd(
        uh, o_pre, d_ar, states, hgrn_lb, hgrn_norm_g, "hgrn_bwd",
        comm=_Comm([{"kind": "exchange", "arr": d_w_up_x, "rows": (half, half), "dst_rows": SHARD_UP,
                     "dst_first": half, "into": recv_up_half},
                    {"kind": "exchange", "arr": d_w_o_b.reshape(N_DEV, SHARD_O, D_MODEL)}]))
    d_w_in_part = _mm(d_ua_t, xb, out_dtype2=BF16, tm=UA_W, tk=t, out_rows=D_FF, name="bwd_in_dw_attn")
    d_w_in_t, d_w_in_t_b = _mm(d_uh_t, xb, out_dtype2=BF16, tm=256, tk=t, out_rows=D_FF, first_row=UA_W,
                               into=d_w_in_part, name="bwd_in_dw_hgrn")
    small_local = _pack_small({
        "ln1_g": d_ln1_g, "ln1_b": d_ln1_b, "b_in": jnp.concatenate([d_bias_a, d_bias_h], axis=1),
        "sinks": d_sinks[:, :8], "hgrn_lb": d_lb8[0:2], "hgrn_norm_g": d_norm_g, "ln2_g": d_ln2_g,
        "ln2_b": d_ln2_b, "conv_b": d_conv_b, "loss": loss_part[:, :1]})
    d_w_in_x = d_w_in_t_b.reshape(N_DEV, SHARD_IN, D_MODEL)
    res_up, (from_sibling,) = _sum_shards_adamw(
        [recv_up], _own(d_w_up_t, SHARD_UP), w_up[0].T, m_w_up[0].T, v_w_up[0].T, "adamw_w_up",
        comm=_Comm([{"kind": "pair4", "arr": d_w_in_x}]))
    res_up = [r.T for r in res_up]
    own_in, chip_part = _pair_reduce(from_sibling, d_w_in_t, "pair_reduce_w_in")
    dx, (from_chips, small_g) = _mm(d_uh, w_h_t, addend=dz1, addend_scale=ALPHA, name="bwd_in_dx_hgrn",
                                    comm=_Comm([{"kind": "chips3", "arr": chip_part},
                                                {"kind": "gather", "arr": small_local}]))
    dx = _mm(d_ua, w_a_t, addend=dx, tk=768, name="bwd_in_dx_attn")

    res_in = [r.T for r in _chip_sum_adamw(from_chips, own_in, w_in[0].T, m_w_in[0].T, v_w_in[0].T, "adamw_w_in")]
    res_o = _sum_shards_adamw([recv_o], _own(d_w_o, SHARD_O), w_o[0], m_w_o[0], v_w_o[0], "adamw_w_o")
    res_down = _sum_shards_adamw([recv_down], _own(d_w_down, SHARD_DOWN), w_down[0], m_w_down[0], v_w_down[0],
                                 "adamw_w_down")
    g_cw = _sum_slots(recv_cw, "sum_conv_w")
    cw8 = lambda a: _pad_rows(a, 8)
    res_cw = (g_cw,) + tuple(_adamw(cw8(conv_w[0]), g_cw, cw8(m_conv_w[0]), cw8(v_conv_w[0]), "adamw_conv_w"))
    big = {"w_in": [r[None] for r in res_in], "w_up": [r[None] for r in res_up],
           "w_o": [r[None] for r in res_o], "w_down": [r[None] for r in res_down],
           "conv_w": [r[None, 0:3] for r in res_cw]}

    loss11, small = _small_update(
        small_g,
        {"ln1_g": ln1_g, "ln1_b": ln1_b, "b_in": b_in, "sinks": sinks, "hgrn_lb": hgrn_lb,
         "hgrn_norm_g": hgrn_norm_g, "ln2_g": ln2_g, "ln2_b": ln2_b, "conv_b": conv_b},
        {"ln1_g": m_ln1_g, "ln1_b": m_ln1_b, "b_in": m_b_in, "sinks": m_sinks, "hgrn_lb": m_hgrn_lb,
         "hgrn_norm_g": m_hgrn_norm_g, "ln2_g": m_ln2_g, "ln2_b": m_ln2_b, "conv_b": m_conv_b},
        {"ln1_g": v_ln1_g, "ln1_b": v_ln1_b, "b_in": v_b_in, "sinks": v_sinks, "hgrn_lb": v_hgrn_lb,
         "hgrn_norm_g": v_hgrn_norm_g, "ln2_g": v_ln2_g, "ln2_b": v_ln2_b, "conv_b": v_conv_b},
        "adamw_small")
    loss = loss11[0, 0]

    order = ["ln1_g", "ln1_b", "w_in", "b_in", "sinks", "hgrn_lb", "hgrn_norm_g", "w_o", "ln2_g", "ln2_b",
             "w_up", "conv_w", "conv_b", "w_down"]

    def pick(idx):
        return [big[n][idx] if n in big else small[n][idx] for n in order]

    return (loss, dx[None], *pick(0), *pick(1), *pick(2), *pick(3))
```

```python
import functools

import jax
import jax.numpy as jnp
import numpy as np
from jax import lax
from jax.experimental import pallas as pl
from jax.experimental.pallas import tpu as pltpu

F32 = jnp.float32
BF16 = jnp.bfloat16

N_DEV = 8
D_MODEL = 1024
D_FF = 2816
ATTN_W = 512
KV_W = 128
UA_W = ATTN_W + 2 * KV_W
UH_W = 2048
HG_W = 512
ATTN_BLOCK = 128
HGRN_CHUNK = 64
HGRN_SUB = 16
HGRN_CHUNKS_PER_STEP = 4
EXP_CLAMP = 85.0
NEG_BIG = -1e30
LN_EPS = 1e-5
RMS_EPS = 1e-6
ALPHA = 2.0 ** 0.25
ATTN_SCALE = 0.125
ROPE_THETA = 500000.0

ADAM_LR = 0.001
ADAM_B1 = 0.9
ADAM_B2 = 0.999
ADAM_EPS = 1e-08
ADAM_WD = 0.01
ADAM_STEP = 10

LANES = 128
VMEM_LIMIT_BYTES = 56 * 1024 * 1024

SHARD_IN = D_FF // N_DEV
SHARD_UP = 2 * D_FF // N_DEV
SHARD_O = D_MODEL // N_DEV
SHARD_DOWN = D_FF // N_DEV
SMALL_ROWS = 88

_MESH = pl.DeviceIdType.MESH
_NT = (((1,), (1,)), ((), ()))
_NN = (((1,), (0,)), ((), ()))
_TN = (((0,), (0,)), ((), ()))


def _cp(*sem):
    if sem:
        return pltpu.CompilerParams(dimension_semantics=sem, vmem_limit_bytes=VMEM_LIMIT_BYTES)
    return pltpu.CompilerParams(vmem_limit_bytes=VMEM_LIMIT_BYTES)


def _sig(x):
    return 0.5 * jnp.tanh(0.5 * x) + 0.5


def _dsilu(x, s):
    return s * (1.0 + x * (1.0 - s))


def _dot(a, b, dims):
    return lax.dot_general(a.astype(BF16), b.astype(BF16), dims, preferred_element_type=F32)


def _split(a):
    hi = a.astype(BF16)
    return hi, (a - hi.astype(F32)).astype(BF16)


def _dot3(a, b, dims):
    ah, al = _split(a)
    bh, bl = _split(b)
    d = functools.partial(lax.dot_general, dimension_numbers=dims, preferred_element_type=F32)
    return d(ah, bh) + (d(ah, bl) + d(al, bh))


def _pick(n, pref):
    for t in pref:
        if t <= n and n % t == 0:
            return t
    return n


def _my_coords():
    return lax.axis_index("x"), lax.axis_index("y"), lax.axis_index("c")


def _peer(k):
    x, y, c = _my_coords()
    return (1 - x if k & 4 else x, 1 - y if k & 2 else y, 1 - c if k & 1 else c)


def _me():
    x, y, c = _my_coords()
    return 4 * x + 2 * y + c


class _Comm:
    def __init__(self, items):
        self.items = []
        for it in items:
            arr = it["arr"]
            full = arr.shape[0] if it["kind"] == "gather" else arr.shape[1]
            first, count = it.get("rows", (0, full))
            self.items.append(dict(kind=it["kind"], arr=arr, first=first, count=count,
                                   dst_rows=it.get("dst_rows", count), dst_first=it.get("dst_first", 0),
                                   into=it.get("into")))
        self.n = len(self.items)
        self.arrays = [it["arr"] for it in self.items]
        self.intos = [(a, it["into"]) for a, it in enumerate(self.items) if it["into"] is not None]

    def out_shapes(self):
        return [jax.ShapeDtypeStruct((4 if it["kind"] in ("pair4", "chips3") else N_DEV, it["dst_rows"],
                                      it["arr"].shape[-1]), it["arr"].dtype) for it in self.items]

    def specs(self, n=None):
        return [pl.BlockSpec(memory_space=pl.ANY)] * (self.n if n is None else n)

    def scratch(self):
        return [pltpu.SemaphoreType.DMA(((N_DEV - 1) * self.n,)), pltpu.SemaphoreType.DMA(((N_DEV - 1) * self.n,)),
                pltpu.SemaphoreType.DMA((self.n,))]

    def _src(self, a, ref, dev):
        it = self.items[a]
        blk = ref if it["kind"] == "gather" else ref.at[dev]
        return blk.at[pl.ds(it["first"], it["count"])]

    def _dst(self, a, ref, slot):
        it = self.items[a]
        return ref.at[slot].at[pl.ds(it["dst_first"], it["count"])]

    def _copy(self, a, k, src, dst, sems, me, slot):
        other = jnp.bitwise_xor(me, k)
        idx = a * (N_DEV - 1) + k - 1
        return pltpu.make_async_remote_copy(
            src_ref=self._src(a, src, other), dst_ref=self._dst(a, dst, me if slot == "mine" else other),
            send_sem=sems[0].at[idx], recv_sem=sems[1].at[idx], device_id=_peer(k), device_id_type=_MESH)

    def _pass_on(self, a, k, dst, sems, me):
        slot = self._dst(a, dst, jnp.bitwise_xor(me, k))
        idx = a * (N_DEV - 1) + k
        return pltpu.make_async_remote_copy(
            src_ref=slot, dst_ref=slot, send_sem=sems[0].at[idx], recv_sem=sems[1].at[idx],
            device_id=_peer(1), device_id_type=_MESH)

    def _part(self, a, r, src, dst, sems, me):
        it = self.items[a]
        idx = a * (N_DEV - 1) + r
        if it["kind"] == "pair4":
            k, slot = 1, jnp.bitwise_xor(jnp.bitwise_xor(me, 1), 2 * r)
        else:
            k, slot = 2 * r, r
        return pltpu.make_async_remote_copy(
            src_ref=src.at[slot].at[pl.ds(it["first"], it["count"])], dst_ref=self._dst(a, dst, r),
            send_sem=sems[0].at[idx], recv_sem=sems[1].at[idx], device_id=_peer(k), device_id_type=_MESH)

    def _parts(self, a):
        return range(4) if self.items[a]["kind"] == "pair4" else range(1, 4)

    def _local(self, a, src, dst, sems, me):
        return pltpu.make_async_copy(self._src(a, src, me), self._dst(a, dst, me), sems[2].at[a])

    def start(self, srcs, dsts, sems):
        me = _me()
        for a, (src, dst) in enumerate(zip(srcs, dsts)):
            if self.items[a]["kind"] in ("pair4", "chips3"):
                for r in self._parts(a):
                    self._part(a, r, src, dst, sems, me).start()
                continue
            direct = (1, 2, 4, 6) if self.items[a]["kind"] == "gather" else range(1, N_DEV)
            self._local(a, src, dst, sems, me).start()
            for k in direct:
                self._copy(a, k, src, dst, sems, me, "mine").start()

    def wait(self, srcs, dsts, sems):
        me = _me()
        for a, (src, dst) in enumerate(zip(srcs, dsts)):
            if self.items[a]["kind"] in ("pair4", "chips3"):
                for r in self._parts(a):
                    self._part(a, r, src, dst, sems, me).wait_recv()
                for r in self._parts(a):
                    self._part(a, r, src, dst, sems, me).wait_send()
                continue
            if self.items[a]["kind"] == "gather":
                for k in (2, 4, 6):
                    self._copy(a, k, src, dst, sems, me, "theirs").wait_recv()
                    self._pass_on(a, k, dst, sems, me).start()
                for k in (1, 3, 5, 7):
                    self._copy(a, k, src, dst, sems, me, "theirs").wait_recv()
                for k in (1, 2, 4, 6):
                    self._copy(a, k, src, dst, sems, me, "mine").wait_send()
                for k in (2, 4, 6):
                    self._pass_on(a, k, dst, sems, me).wait_send()
            else:
                for k in range(1, N_DEV):
                    self._copy(a, k, src, dst, sems, me, "theirs").wait_recv()
                for k in range(1, N_DEV):
                    self._copy(a, k, src, dst, sems, me, "mine").wait_send()
            self._local(a, src, dst, sems, me).wait()


def _call(body, *, name, grid, ins, in_specs, out_specs, out_shape, scratch_shapes=(), sem, comm=None):
    n_in, n_out, n_scr = len(ins), len(out_shape), len(scratch_shapes)
    if comm is None:
        outs = pl.pallas_call(
            body, name=name, grid=grid, in_specs=list(in_specs), out_specs=list(out_specs),
            out_shape=list(out_shape), scratch_shapes=list(scratch_shapes), compiler_params=_cp(*sem))(*ins)
        return list(outs), []
    nc, n_into = comm.n, len(comm.intos)

    def hosted(*refs):
        pos = n_in
        c_in = refs[pos:pos + nc]
        pos += nc + n_into
        outs = refs[pos:pos + n_out]
        pos += n_out
        c_out = refs[pos:pos + nc]
        pos += nc
        scr = refs[pos:pos + n_scr]
        sems = refs[pos + n_scr:]
        ids = [pl.program_id(d) for d in range(len(grid))]
        first = functools.reduce(jnp.logical_and, [i == 0 for i in ids])
        last = functools.reduce(jnp.logical_and, [i == g - 1 for i, g in zip(ids, grid)])

        @pl.when(first)
        def _():
            comm.start(c_in, c_out, sems)

        body(*refs[:n_in], *outs, *scr)

        @pl.when(last)
        def _():
            comm.wait(c_in, c_out, sems)

    aliases = {n_in + nc + j: n_out + a for j, (a, _) in enumerate(comm.intos)}
    outs = pl.pallas_call(
        hosted, name=name, grid=grid, in_specs=list(in_specs) + comm.specs() + comm.specs(n_into),
        out_specs=list(out_specs) + comm.specs(), out_shape=list(out_shape) + comm.out_shapes(),
        scratch_shapes=list(scratch_shapes) + comm.scratch(), input_output_aliases=aliases,
        compiler_params=_cp(*(["arbitrary"] * len(grid))))(*ins, *comm.arrays, *[arr for _, arr in comm.intos])
    return list(outs[:n_out]), list(outs[n_out:])


def _slot_sum(recv_ref, own_ref, shape):
    me = _me()
    acc = jnp.zeros(shape, F32)
    for s in range(N_DEV):
        acc = acc + jnp.where(me == s, own_ref[...], recv_ref[s].astype(F32))
    return acc


def _adamw_math(w, g, m, v):
    nm = ADAM_B1 * m + (1.0 - ADAM_B1) * g
    nv = ADAM_B2 * v + (1.0 - ADAM_B2) * (g * g)
    m_hat = nm / (1.0 - ADAM_B1 ** ADAM_STEP)
    v_hat = nv / (1.0 - ADAM_B2 ** ADAM_STEP)
    return -ADAM_LR * (m_hat / (jnp.sqrt(v_hat) + ADAM_EPS) + ADAM_WD * w), nm, nv


def _pair_reduce(from_sibling, mine, name):
    _, rows, cols = from_sibling.shape
    tr = _pick(rows, (176, 128, 64, 32, 16, 8))
    tiles = rows // tr
    table = jnp.bitwise_xor(_me(), jnp.arange(0, N_DEV, 2, dtype=jnp.int32))

    def body(tbl_ref, sib_ref, mine_ref, own_ref, send_ref):
        r = pl.program_id(1)
        total = mine_ref[...] + sib_ref[0].astype(F32)
        send_ref[0] = jnp.where(r == 0, 0.0, total).astype(BF16)

        @pl.when(r == 0)
        def _():
            own_ref[...] = total

    grid_spec = pltpu.PrefetchScalarGridSpec(
        num_scalar_prefetch=1, grid=(tiles, 4),
        in_specs=[pl.BlockSpec((1, tr, cols), lambda i, r, tbl: (r, i, 0)),
                  pl.BlockSpec((tr, cols), lambda i, r, tbl: (tbl[r] * tiles + i, 0))],
        out_specs=[pl.BlockSpec((tr, cols), lambda i, r, tbl: (i, 0)),
                   pl.BlockSpec((1, tr, cols), lambda i, r, tbl: (r, i, 0))])
    return pl.pallas_call(
        body, name=name, grid_spec=grid_spec,
        out_shape=[jax.ShapeDtypeStruct((rows, cols), F32), jax.ShapeDtypeStruct((4, rows, cols), BF16)],
        compiler_params=_cp("arbitrary", "arbitrary"),
    )(table, from_sibling, mine)


def _chip_sum_adamw(from_chips, own, w, m, v, name):
    _, rows, cols = from_chips.shape
    tr = _pick(rows, (176, 128, 64, 32, 16, 8))

    def body(recv_ref, own_ref, w_ref, m_ref, v_ref, g_ref, d_ref, nm_ref, nv_ref):
        g = own_ref[...]
        for r in range(1, 4):
            g = g + recv_ref[r].astype(F32)
        g_ref[...] = g
        d_ref[...], nm_ref[...], nv_ref[...] = _adamw_math(w_ref[...], g, m_ref[...], v_ref[...])

    spec = pl.BlockSpec((tr, cols), lambda i: (i, 0))
    shp = jax.ShapeDtypeStruct((rows, cols), F32)
    return pl.pallas_call(
        body, name=name, grid=(rows // tr,),
        in_specs=[pl.BlockSpec((4, tr, cols), lambda i: (0, i, 0)), spec, spec, spec, spec],
        out_specs=[spec, spec, spec, spec], out_shape=[shp, shp, shp, shp],
        compiler_params=_cp("parallel"),
    )(from_chips, own, w, m, v)


def _sum_shards_adamw(recvs, own, w, m, v, name, comm=None):
    rows_p, cols = recvs[0].shape[1], recvs[0].shape[2]
    n_p = len(recvs)
    tr = _pick(rows_p, (176, 128, 64, 32, 16, 8))
    tiles = rows_p // tr

    def body(*refs):
        recv_refs = refs[:n_p]
        own_ref, w_ref, m_ref, v_ref, g_ref, d_ref, nm_ref, nv_ref = refs[n_p:]
        for j in range(n_p):
            @pl.when(pl.program_id(0) == j)
            def _():
                g = _slot_sum(recv_refs[j], own_ref, (tr, cols))
                g_ref[...] = g
                d_ref[...], nm_ref[...], nv_ref[...] = _adamw_math(w_ref[...], g, m_ref[...], v_ref[...])

    spec = pl.BlockSpec((tr, cols), lambda p_, i: (p_ * tiles + i, 0))
    shp = jax.ShapeDtypeStruct((rows_p * n_p, cols), F32)
    outs, couts = _call(
        body, name=name, grid=(n_p, tiles), ins=[*recvs, own, w, m, v],
        in_specs=[pl.BlockSpec((N_DEV, tr, cols), functools.partial(lambda p_, i, j: (0, jnp.where(p_ == j, i, 0), 0), j=j))
                  for j in range(n_p)] + [spec, spec, spec, spec],
        out_specs=[spec, spec, spec, spec], out_shape=[shp, shp, shp, shp],
        sem=("arbitrary", "arbitrary"), comm=comm)
    return outs if comm is None else (outs, couts)


def _mm(a, b, *, name, ta=False, tb=False, out_dtype=F32, out_dtype2=None, bias=None, addend=None,
        addend_scale=1.0, tm=1024, tn=1024, tk=1024, comm=None, out_rows=None, first_row=0, into=None):
    kdim, m = a.shape if ta else a.shape[::-1]
    n = b.shape[0] if tb else b.shape[1]
    tm = _pick(m, (tm, 1408, 1024, 768, 512, 256, 128))
    tn = _pick(n, (tn, 1408, 1024, 768, 512, 256, 128))
    tk = _pick(kdim, (tk, 1408, 1024, 768, 512, 256, 128))
    nk = kdim // tk
    a_spec = pl.BlockSpec((tk, tm), lambda i, j, k: (k, i)) if ta else pl.BlockSpec((tm, tk), lambda i, j, k: (i, k))
    b_spec = pl.BlockSpec((tn, tk), lambda i, j, k: (j, k)) if tb else pl.BlockSpec((tk, tn), lambda i, j, k: (k, j))
    ins, specs = [a, b], [a_spec, b_spec]
    if bias is not None:
        ins.append(bias)
        specs.append(pl.BlockSpec((1, tn), lambda i, j, k: (0, j)))
    if addend is not None:
        ins.append(addend)
        specs.append(pl.BlockSpec((tm, tn), lambda i, j, k: (i, j)))
    dims = (((0,) if ta else (1,), (1,) if tb else (0,)), ((), ()))
    has_bias, has_addend, two = bias is not None, addend is not None, out_dtype2 is not None

    def body(*refs):
        a_ref, b_ref = refs[0], refs[1]
        pos = 2
        bias_ref = addend_ref = None
        if has_bias:
            bias_ref = refs[pos]
            pos += 1
        if has_addend:
            addend_ref = refs[pos]
            pos += 1
        o_refs, acc_ref = refs[pos:-1], refs[-1]
        k = pl.program_id(2)

        @pl.when(k == 0)
        def _():
            acc_ref[...] = jnp.zeros_like(acc_ref)

        acc_ref[...] += _dot(a_ref[...], b_ref[...], dims)

        @pl.when(k == nk - 1)
        def _():
            r = acc_ref[...]
            if has_bias:
                r = r + bias_ref[...]
            if has_addend:
                r = r + addend_scale * addend_ref[...].astype(F32)
            for o_ref in o_refs:
                o_ref[...] = r.astype(o_ref.dtype)

    blk0 = first_row // tm
    dtypes = [out_dtype] + ([out_dtype2] if two else [])
    ospec = pl.BlockSpec((tm, tn), lambda i, j, k: (i + blk0, j))
    shapes = [jax.ShapeDtypeStruct((m if out_rows is None else out_rows, n), d) for d in dtypes]
    if into is not None:
        n_in = len(ins)
        outs = pl.pallas_call(
            lambda *refs: body(*refs[:n_in], *refs[n_in + len(into):]), name=name, grid=(m // tm, n // tn, nk),
            in_specs=specs + [pl.BlockSpec(memory_space=pl.ANY)] * len(into), out_specs=[ospec] * len(dtypes),
            out_shape=shapes, scratch_shapes=[pltpu.VMEM((tm, tn), F32)],
            input_output_aliases={n_in + j: j for j in range(len(into))},
            compiler_params=_cp("parallel", "parallel", "arbitrary"))(*ins, *into)
        return tuple(outs) if two else outs[0]
    outs, couts = _call(
        body, name=name, grid=(m // tm, n // tn, nk), ins=ins, in_specs=specs,
        out_specs=[ospec] * len(dtypes), out_shape=shapes,
        scratch_shapes=[pltpu.VMEM((tm, tn), F32)], sem=("parallel", "parallel", "arbitrary"), comm=comm)
    primary = tuple(outs) if two else outs[0]
    return (primary, couts) if comm is not None else primary


def _rope_lane_constants():
    inv_freq = np.float32(ROPE_THETA) ** (-np.arange(8, dtype=np.float32) * np.float32(2.0 / 16.0))
    lane = np.arange(LANES) % 64
    freq = np.where(lane < 16, inv_freq[lane % 8], 0.0).astype(np.float32)
    sign = np.where(lane < 8, -1.0, np.where(lane < 16, 1.0, 0.0)).astype(np.float32)
    return jnp.asarray(freq)[None, :], jnp.asarray(sign)[None, :]


def _prep(pos_col, x2, name, comm):
    t, d = x2.shape
    tr = _pick(t, (512, 256, 128))
    freq, sign = _rope_lane_constants()

    def body(pos_ref, freq_ref, sign_ref, x_ref, c_ref, s_ref, xb_ref):
        ang = pos_ref[...].astype(F32) * freq_ref[...]
        c_ref[...] = jnp.cos(ang)
        s_ref[...] = sign_ref[...] * jnp.sin(ang)
        xb_ref[...] = x_ref[...].astype(BF16)

    tab = pl.BlockSpec((tr, LANES), lambda i: (i, 0))
    return _call(
        body, name=name, grid=(t // tr,), ins=[pos_col, freq, sign, x2],
        in_specs=[pl.BlockSpec((tr, 1), lambda i: (i, 0)), pl.BlockSpec((1, LANES), lambda i: (0, 0)),
                  pl.BlockSpec((1, LANES), lambda i: (0, 0)), pl.BlockSpec((tr, d), lambda i: (i, 0))],
        out_specs=[tab, tab, pl.BlockSpec((tr, d), lambda i: (i, 0))],
        out_shape=[jax.ShapeDtypeStruct((t, LANES), F32), jax.ShapeDtypeStruct((t, LANES), F32),
                   jax.ShapeDtypeStruct((t, d), BF16)],
        sem=("parallel",), comm=comm)


def _swap8(t):
    width = t.shape[1]
    lane = jnp.bitwise_and(lax.broadcasted_iota(jnp.int32, t.shape, 1), 63)
    return jnp.where(lane < 8, pltpu.roll(t, width - 8, 1), jnp.where(lane < 16, pltpu.roll(t, 8, 1), 0.0))


def _rope(t, c, s):
    return t * c + _swap8(t) * s


def _rope_bwd(d, c, s):
    return d * c + _swap8(d * s)


def _tile4(a):
    return jnp.concatenate([a, a, a, a], axis=1)


def _attn_band(n, k_cur, k_prev, v_cur, v_prev, c_cur, s_cur, c_prev, s_prev):
    kband = jnp.concatenate([_rope(k_prev, c_prev, s_prev), _rope(k_cur, c_cur, s_cur)], axis=0)
    vband = jnp.concatenate([v_prev, v_cur], axis=0)
    qi = lax.broadcasted_iota(jnp.int32, (ATTN_BLOCK, 2 * ATTN_BLOCK), 0)
    kj = lax.broadcasted_iota(jnp.int32, (ATTN_BLOCK, 2 * ATTN_BLOCK), 1)
    dist = qi + ATTN_BLOCK - kj
    valid = (dist >= 0) & (dist < ATTN_BLOCK) & (n * ATTN_BLOCK - ATTN_BLOCK + kj >= 0)
    return (kband.astype(BF16), pltpu.roll(kband, 64, 1).astype(BF16),
            vband.astype(BF16), pltpu.roll(vband, 64, 1).astype(BF16), valid, kband)


def _attn_probs(raw, valid, sink, axis):
    s = jnp.where(valid, raw * ATTN_SCALE, NEG_BIG)
    m = jnp.maximum(jnp.max(s, axis=axis, keepdims=True), sink)
    p = jnp.exp(s - m)
    esink = jnp.exp(sink - m)
    z = jnp.sum(p, axis=axis, keepdims=True) + esink
    return p / z, esink / z


def _attn_valid_t(n):
    kj = lax.broadcasted_iota(jnp.int32, (2 * ATTN_BLOCK, ATTN_BLOCK), 0)
    qi = lax.broadcasted_iota(jnp.int32, (2 * ATTN_BLOCK, ATTN_BLOCK), 1)
    dist = qi + ATTN_BLOCK - kj
    return (dist >= 0) & (dist < ATTN_BLOCK) & (n * ATTN_BLOCK - ATTN_BLOCK + kj >= 0)


def _attn_specs(nb):
    def cur(col, width=KV_W):
        return pl.BlockSpec((ATTN_BLOCK, width), lambda n: (jnp.minimum(n, nb - 1), col))

    def prev(col):
        return pl.BlockSpec((ATTN_BLOCK, KV_W), lambda n: (jnp.maximum(n - 1, 0), col))

    ua_specs = [cur(0, ATTN_W), cur(4), prev(4), cur(5), prev(5)]
    tab_specs = [cur(0), cur(0), prev(0), prev(0)]
    return ua_specs, tab_specs


def _attn_fwd(ua, ctab, stab, sinks, name, comm=None):
    t = ua.shape[0]
    nb = t // ATTN_BLOCK
    ua_specs, tab_specs = _attn_specs(nb)

    def body(q_ref, kc_ref, kp_ref, vc_ref, vp_ref, cc_ref, sc_ref, cp_ref, sp_ref, sink_ref, o_ref, o_t_ref):
        n = pl.program_id(0)
        cc, sc = cc_ref[...], sc_ref[...]
        kb, kb_r, vb, vb_r, valid, _ = _attn_band(n, kc_ref[...], kp_ref[...], vc_ref[...], vp_ref[...],
                                                  cc, sc, cp_ref[...], sp_ref[...])
        qr = _rope(q_ref[...], _tile4(cc), _tile4(sc))
        lo = lax.broadcasted_iota(jnp.int32, (ATTN_BLOCK, LANES), 1) < 64
        heads = []
        for j in range(4):
            qj = qr[:, j * LANES:(j + 1) * LANES]
            for is_lo in (True, False):
                aligned = is_lo == (j < 2)
                qm = jnp.where(lo if is_lo else jnp.logical_not(lo), qj, 0.0).astype(BF16)
                raw = lax.dot_general(qm, kb if aligned else kb_r, _NT, preferred_element_type=F32)
                heads.append((raw, vb if aligned else vb_r, sink_ref[0, len(heads)]))
        halves = []
        for raw, vv, sink in heads:
            probs, _ = _attn_probs(raw, valid, sink, 1)
            halves.append(lax.dot_general(probs.astype(BF16), vv, _NN, preferred_element_type=F32))
        outs = [jnp.where(lo, halves[2 * j], halves[2 * j + 1]) for j in range(4)]
        o_ref[...] = jnp.concatenate(outs, axis=1).astype(o_ref.dtype)
        for j in range(4):
            o_t_ref[j * LANES:(j + 1) * LANES, :] = outs[j].T.astype(o_t_ref.dtype)

    return _call(
        body, name=name, grid=(nb,), ins=[ua, ua, ua, ua, ua, ctab, stab, ctab, stab, sinks],
        in_specs=ua_specs + tab_specs + [pl.BlockSpec(memory_space=pltpu.SMEM)],
        out_specs=[pl.BlockSpec((ATTN_BLOCK, ATTN_W), lambda n: (n, 0)),
                   pl.BlockSpec((ATTN_W, ATTN_BLOCK), lambda n: (0, n))],
        out_shape=[jax.ShapeDtypeStruct((t, ATTN_W), BF16), jax.ShapeDtypeStruct((ATTN_W, t), BF16)],
        sem=("parallel",), comm=comm)


def _attn_bwd(ua, d_out, ctab, stab, sinks, name, comm=None):
    t = ua.shape[0]
    nb = t // ATTN_BLOCK
    ua_specs, tab_specs = _attn_specs(nb)

    def body(q_ref, kc_ref, kp_ref, vc_ref, vp_ref, cc_ref, sc_ref, cp_ref, sp_ref, do_ref, sink_ref,
             dua_ref, dua_t_ref, dbias_ref, dsink_ref, dq_c, dk_c, dv_c, dq_n, dk_n, dv_n):
        n = pl.program_id(0)

        @pl.when(n == 0)
        def _():
            dq_c[...] = jnp.zeros_like(dq_c)
            dk_c[...] = jnp.zeros_like(dk_c)
            dv_c[...] = jnp.zeros_like(dv_c)
            dbias_ref[...] = jnp.zeros_like(dbias_ref)
            dsink_ref[...] = jnp.zeros_like(dsink_ref)

        @pl.when(n == nb)
        def _():
            dq_n[...] = jnp.zeros_like(dq_n)
            dk_n[...] = jnp.zeros_like(dk_n)
            dv_n[...] = jnp.zeros_like(dv_n)

        @pl.when(n < nb)
        def _():
            cc, sc = cc_ref[...], sc_ref[...]
            kb, kb_r, vb, vb_r, _, kb_f32 = _attn_band(n, kc_ref[...], kp_ref[...], vc_ref[...], vp_ref[...],
                                                       cc, sc, cp_ref[...], sp_ref[...])
            valid_t = _attn_valid_t(n)
            c4, s4 = _tile4(cc), _tile4(sc)
            qr = _rope(q_ref[...], c4, s4)
            do = do_ref[...].astype(F32)
            lane = lax.broadcasted_iota(jnp.int32, (ATTN_BLOCK, LANES), 1)
            lo = lane < 64
            lane_row = lax.broadcasted_iota(jnp.int32, (1, LANES), 1)
            k_t = {False: kb_f32.T.astype(BF16), True: pltpu.roll(kb_f32, 64, 1).T.astype(BF16)}
            heads = []
            for j in range(4):
                qj = qr[:, j * LANES:(j + 1) * LANES]
                doj = do[:, j * LANES:(j + 1) * LANES]
                for is_lo in (True, False):
                    aligned = is_lo == (j < 2)
                    msk = lo if is_lo else jnp.logical_not(lo)
                    kk = kb if aligned else kb_r
                    vv = vb if aligned else vb_r
                    qm = jnp.where(msk, qj, 0.0).astype(BF16)
                    dom = jnp.where(msk, doj, 0.0).astype(BF16)
                    heads.append(dict(
                        aligned=aligned, qm=qm, dom=dom, sink=sink_ref[0, len(heads)],
                        raw_t=lax.dot_general(kk, qm, _NT, preferred_element_type=F32),
                        dp_t=lax.dot_general(vv, dom, _NT, preferred_element_type=F32)))
            dk_band = jnp.zeros((2 * ATTN_BLOCK, LANES), F32)
            dv_band = jnp.zeros((2 * ATTN_BLOCK, LANES), F32)
            dsink = jnp.zeros((1, LANES), F32)
            for head, hd in enumerate(heads):
                probs_t, psink = _attn_probs(hd["raw_t"], valid_t, hd["sink"], 0)
                delta_t = jnp.sum(probs_t * hd["dp_t"], axis=0, keepdims=True)
                hd["ds_t"] = (probs_t * (hd["dp_t"] - delta_t) * ATTN_SCALE).astype(BF16)
                dsink = dsink + jnp.where(lane_row == head, -jnp.sum(psink * delta_t), 0.0)
                dk_h = lax.dot_general(hd["ds_t"], hd["qm"], _NN, preferred_element_type=F32)
                dv_h = lax.dot_general(probs_t.astype(BF16), hd["dom"], _NN, preferred_element_type=F32)
                if not hd["aligned"]:
                    dk_h = pltpu.roll(dk_h, 64, 1)
                    dv_h = pltpu.roll(dv_h, 64, 1)
                dk_band = dk_band + dk_h
                dv_band = dv_band + dv_h
            row_lo = lax.broadcasted_iota(jnp.int32, (LANES, ATTN_BLOCK), 0) < 64
            dq_t = [lax.dot_general(k_t[not hd["aligned"]], hd["ds_t"], _NN, preferred_element_type=F32)
                    for hd in heads]
            dqs = [jnp.where(row_lo, dq_t[2 * j], dq_t[2 * j + 1]).T for j in range(4)]
            dq_n[...] = _rope_bwd(jnp.concatenate(dqs, axis=1), c4, s4)
            dk_n[...] = dk_band
            dv_n[...] = dv_band
            dsink_ref[...] += dsink

        dk_prev = _rope_bwd(dk_c[...] + dk_n[0:ATTN_BLOCK, :], cp_ref[...], sp_ref[...])
        dv_prev = dv_c[...] + dv_n[0:ATTN_BLOCK, :]
        full = jnp.concatenate([dq_c[...], dk_prev, dv_prev], axis=1)
        dua_ref[...] = full.astype(dua_ref.dtype)
        for j in range(UA_W // LANES):
            dua_t_ref[j * LANES:(j + 1) * LANES, :] = full[:, j * LANES:(j + 1) * LANES].T.astype(dua_t_ref.dtype)
        dbias_ref[...] += jnp.sum(full, axis=0, keepdims=True)
        dq_c[...] = dq_n[...]
        dk_c[...] = dk_n[ATTN_BLOCK:, :]
        dv_c[...] = dv_n[ATTN_BLOCK:, :]

    return _call(
        body, name=name, grid=(nb + 1,), ins=[ua, ua, ua, ua, ua, ctab, stab, ctab, stab, d_out, sinks],
        in_specs=ua_specs + tab_specs + [
            pl.BlockSpec((ATTN_BLOCK, ATTN_W), lambda n: (jnp.minimum(n, nb - 1), 0)),
            pl.BlockSpec(memory_space=pltpu.SMEM)],
        out_specs=[pl.BlockSpec((ATTN_BLOCK, UA_W), lambda n: (jnp.maximum(n - 1, 0), 0)),
                   pl.BlockSpec((UA_W, ATTN_BLOCK), lambda n: (0, jnp.maximum(n - 1, 0))),
                   pl.BlockSpec((1, UA_W), lambda n: (0, 0)),
                   pl.BlockSpec((1, LANES), lambda n: (0, 0))],
        out_shape=[jax.ShapeDtypeStruct((t, UA_W), BF16), jax.ShapeDtypeStruct((UA_W, t), BF16),
                   jax.ShapeDtypeStruct((1, UA_W), F32),
                   jax.ShapeDtypeStruct((1, LANES), F32)],
        scratch_shapes=[pltpu.VMEM((ATTN_BLOCK, ATTN_W), F32), pltpu.VMEM((ATTN_BLOCK, KV_W), F32),
                        pltpu.VMEM((ATTN_BLOCK, KV_W), F32), pltpu.VMEM((ATTN_BLOCK, ATTN_W), F32),
                        pltpu.VMEM((2 * ATTN_BLOCK, KV_W), F32), pltpu.VMEM((2 * ATTN_BLOCK, KV_W), F32)],
        sem=("arbitrary",), comm=comm)


def _tri_mats():
    r = lax.broadcasted_iota(jnp.int32, (HGRN_CHUNK, LANES), 0)
    c = lax.broadcasted_iota(jnp.int32, (HGRN_CHUNK, LANES), 1)
    lower = ((c <= r) & (c < HGRN_CHUNK)).astype(F32)
    upper = ((c >= r) & (c < HGRN_CHUNK)).astype(F32)
    return lower, upper


def _tri_apply(tri, g):
    pad = jnp.concatenate([g, jnp.zeros_like(g)], axis=0)
    return lax.dot_general(tri, pad, _NN, precision=lax.Precision.HIGHEST, preferred_element_type=F32)


def _sub_masks():
    s = lax.broadcasted_iota(jnp.int32, (HGRN_CHUNK, LANES), 0)
    tt = lax.broadcasted_iota(jnp.int32, (HGRN_CHUNK, LANES), 1)
    return [(tt >= HGRN_SUB * i) & (tt < HGRN_SUB * (i + 1)) & (s <= tt) for i in range(HGRN_CHUNK // HGRN_SUB)]


def _hgrn_gates(hq, hf, lb_ref, b_scr):
    lb = _sig(lb_ref[0:1, :] - lb_ref[1:2, :])
    q = hq * _sig(hq)
    sg = _sig(hf)
    f = lb + (1.0 - lb) * sg
    k = 1.0 - f
    lower, _ = _tri_mats()
    b = _tri_apply(lower, jnp.log(f))
    b_scr[...] = b
    nsub = HGRN_CHUNK // HGRN_SUB
    starts = [jnp.zeros((1, HG_W), F32)] + [b_scr[HGRN_SUB * i - 1:HGRN_SUB * i, :] for i in range(1, nsub)]
    pq = jnp.concatenate([jnp.broadcast_to(p, (HGRN_SUB, HG_W)) for p in starts], axis=0)
    b_last = b_scr[HGRN_CHUNK - 1:HGRN_CHUNK, :]
    e_q = jnp.exp(b - pq)
    e_k = [jnp.exp(jnp.minimum(p - b, EXP_CLAMP)) for p in starts]
    e_b = jnp.exp(b)
    e_bl = jnp.exp(b_last - b)
    e_last = jnp.exp(b_last)
    return q, sg, f, k, lb, e_q, e_k, e_b, e_bl, e_last


def _sub_masks_ts():
    tt = lax.broadcasted_iota(jnp.int32, (HGRN_CHUNK, LANES), 0)
    s = lax.broadcasted_iota(jnp.int32, (HGRN_CHUNK, LANES), 1)
    return [(tt >= HGRN_SUB * i) & (tt < HGRN_SUB * (i + 1)) & (s <= tt) for i in range(HGRN_CHUNK // HGRN_SUB)]


def _masked_sum(blocks, masks, axis):
    step = HGRN_CHUNK if axis == 0 else LANES
    acc = jnp.zeros((HGRN_CHUNK, LANES), F32)
    for i, msk in enumerate(masks):
        blk = blocks[step * i:step * (i + 1), :] if axis == 0 else blocks[:, step * i:step * (i + 1)]
        acc = acc + jnp.where(msk, blk, 0.0)
    return acc


def _store_transposed(out_t_ref, chunk_rows):
    width = chunk_rows[0].shape[1]
    if len(chunk_rows) == 1:
        groups = [jnp.concatenate([chunk_rows[0], jnp.zeros_like(chunk_rows[0])], axis=0)]
    else:
        groups = [jnp.concatenate(chunk_rows[g:g + 2], axis=0) for g in range(0, len(chunk_rows), 2)]
    for g, rows in enumerate(groups):
        for c in range(width // LANES):
            tile = rows[:, c * LANES:(c + 1) * LANES].T.astype(out_t_ref.dtype)
            if len(chunk_rows) == 1:
                out_t_ref[c * LANES:(c + 1) * LANES, :] = tile[:, 0:HGRN_CHUNK]
            else:
                out_t_ref[c * LANES:(c + 1) * LANES, g * LANES:(g + 1) * LANES] = tile


def _hgrn_chunk_inputs(j, hq_ref, hf_ref, hi_ref, hg_ref, lb_ref, b_scr):
    rows = slice(j * HGRN_CHUNK, (j + 1) * HGRN_CHUNK)
    hq, hf, v, hg = hq_ref[rows, :], hf_ref[rows, :], hi_ref[rows, :], hg_ref[rows, :]
    q, sg, f, k, lb, e_q, e_k, e_b, e_bl, e_last = _hgrn_gates(hq, hf, lb_ref, b_scr.at[j])
    return dict(rows=rows, hq=hq, v=v, hg=hg, q=q, sg=sg, f=f, k=k, lb=lb, e_q=e_q, e_k=e_k, e_b=e_b, e_bl=e_bl,
                e_last=e_last, qt=q * e_q, qb=q * e_b, kd=k * e_bl, khat=[k * e for e in e_k])


def _hgrn_fwd(uh, lb_raw, norm_g, name, comm=None):
    t = uh.shape[0]
    nc = t // HGRN_CHUNK
    cps = _pick(nc, (HGRN_CHUNKS_PER_STEP, 2, 1))
    rows_step = cps * HGRN_CHUNK

    def body(hq_ref, hf_ref, hi_ref, hg_ref, lb_ref, ng_ref, r_ref, r_t_ref, o_ref, st_out_ref, st_ref, b_scr):
        @pl.when(pl.program_id(0) == 0)
        def _():
            st_ref[...] = jnp.zeros_like(st_ref)

        masks = _sub_masks_ts()
        ng = ng_ref[...]
        zpad = jnp.zeros((HGRN_CHUNK, LANES), F32)
        heads = [slice(h * LANES, (h + 1) * LANES) for h in range(4)]
        chunks = [_hgrn_chunk_inputs(j, hq_ref, hf_ref, hi_ref, hg_ref, lb_ref, b_scr) for j in range(cps)]
        for ch in chunks:
            ch["scores"] = [_dot3(ch["qt"][:, sl],
                                  jnp.concatenate([x for kh in ch["khat"] for x in (kh[:, sl], zpad)], axis=0), _NT)
                            for sl in heads]
        for j, ch in enumerate(chunks):
            o_heads, y_heads = [], []
            for h, sl in enumerate(heads):
                a_ts = _masked_sum(ch["scores"][h], masks, 1)
                vh = ch["v"][:, sl].astype(BF16)
                v_pad = jnp.concatenate([vh, jnp.zeros_like(vh)], axis=0)
                o_intra = lax.dot_general(a_ts.astype(BF16), v_pad, _NN, preferred_element_type=F32)
                st = st_ref[h]
                st_out_ref[j, h] = st
                o_inter = _dot(ch["qb"][:, sl], st, _NT)
                st_ref[h] = st * ch["e_last"][:, sl] + _dot(vh, ch["kd"][:, sl], _TN)
                oh = o_intra + o_inter
                rs = lax.rsqrt(jnp.mean(oh * oh, axis=1, keepdims=True) + RMS_EPS)
                o_heads.append(oh)
                y_heads.append(oh * rs * ng)
            hg = ch["hg"]
            o_ref[ch["rows"], :] = jnp.concatenate(o_heads, axis=1)
            ch["r"] = jnp.concatenate(y_heads, axis=1) * (hg * _sig(hg))
            r_ref[ch["rows"], :] = ch["r"].astype(r_ref.dtype)
        _store_transposed(r_t_ref, [ch["r"] for ch in chunks])

    col = lambda j: pl.BlockSpec((rows_step, HG_W), lambda c: (c, j))
    return _call(
        body, name=name, grid=(nc // cps,), ins=[uh, uh, uh, uh, lb_raw, norm_g],
        in_specs=[col(0), col(1), col(2), col(3),
                  pl.BlockSpec((2, HG_W), lambda c: (0, 0)), pl.BlockSpec((1, LANES), lambda c: (0, 0))],
        out_specs=[pl.BlockSpec((rows_step, HG_W), lambda c: (c, 0)),
                   pl.BlockSpec((HG_W, rows_step), lambda c: (0, c)),
                   pl.BlockSpec((rows_step, HG_W), lambda c: (c, 0)),
                   pl.BlockSpec((cps, 4, LANES, LANES), lambda c: (c, 0, 0, 0))],
        out_shape=[jax.ShapeDtypeStruct((t, HG_W), BF16), jax.ShapeDtypeStruct((HG_W, t), BF16),
                   jax.ShapeDtypeStruct((t, HG_W), F32), jax.ShapeDtypeStruct((nc, 4, LANES, LANES), F32)],
        scratch_shapes=[pltpu.VMEM((4, LANES, LANES), F32), pltpu.VMEM((cps, HGRN_CHUNK, HG_W), F32)],
        sem=("arbitrary",), comm=comm)


def _hgrn_bwd(uh, o_pre, d_r, states, lb_raw, norm_g, name, comm=None):
    t = uh.shape[0]
    nc = t // HGRN_CHUNK
    cps = _pick(nc, (HGRN_CHUNKS_PER_STEP, 2, 1))
    ns = nc // cps
    rows_step = cps * HGRN_CHUNK
    nsub = HGRN_CHUNK // HGRN_SUB

    def body(hq_ref, hf_ref, hi_ref, hg_ref, o_ref, dr_ref, st_in_ref, lb_ref, ng_ref,
             duh_ref, duh_t_ref, dbias_ref, dng_ref, dlb_ref, dst_ref, b_scr, dlb_acc):
        i = pl.program_id(0)

        @pl.when(i == 0)
        def _():
            dst_ref[...] = jnp.zeros_like(dst_ref)
            dbias_ref[...] = jnp.zeros_like(dbias_ref)
            dng_ref[...] = jnp.zeros_like(dng_ref)
            dlb_acc[...] = jnp.zeros_like(dlb_acc)

        masks_st = _sub_masks()
        masks_ts = _sub_masks_ts()
        ng = ng_ref[...]
        zpad = jnp.zeros((HGRN_CHUNK, LANES), F32)
        _, upper = _tri_mats()
        heads = [slice(h * LANES, (h + 1) * LANES) for h in range(4)]
        row = lax.broadcasted_iota(jnp.int32, (HGRN_CHUNK, HG_W), 0)

        chunks = [_hgrn_chunk_inputs(j, hq_ref, hf_ref, hi_ref, hg_ref, lb_ref, b_scr) for j in range(cps)]
        dng = jnp.zeros((1, LANES), F32)
        for ch in chunks:
            o = o_ref[ch["rows"], :]
            dr = dr_ref[ch["rows"], :].astype(F32)
            hg = ch["hg"]
            sgg = _sig(hg)
            dy = dr * (hg * sgg)
            do_h, y_h = [], []
            for sl in heads:
                oh = o[:, sl]
                rs = lax.rsqrt(jnp.mean(oh * oh, axis=1, keepdims=True) + RMS_EPS)
                y_h.append(oh * rs * ng)
                dng = dng + jnp.sum(dy[:, sl] * oh * rs, axis=0, keepdims=True)
                w = dy[:, sl] * ng
                do_h.append(rs * (w - oh * (rs * rs) * jnp.mean(w * oh, axis=1, keepdims=True)))
            ch["do"] = do_h
            ch["dhg"] = dr * jnp.concatenate(y_h, axis=1) * _dsilu(hg, sgg)

        for ch in chunks:
            ch["kst"], ch["kpad"], ch["qt_pad"], ch["v_b"], ch["do_pad"] = [], [], [], [], []
            ch["ats"], ch["d_at"], ch["d_a"] = [], [], []
            for h, sl in enumerate(heads):
                kst = jnp.concatenate([kh[:, sl] for kh in ch["khat"]], axis=0)
                kpad = jnp.concatenate([x for kh in ch["khat"] for x in (kh[:, sl], zpad)], axis=0)
                qt_pad = jnp.concatenate([ch["qt"][:, sl], zpad], axis=0)
                vh = ch["v"][:, sl].astype(BF16)
                v_pad = jnp.concatenate([vh, jnp.zeros_like(vh)], axis=0)
                do_b = ch["do"][h].astype(BF16)
                do_pad = jnp.concatenate([do_b, jnp.zeros_like(do_b)], axis=0)
                ch["kst"].append(kst)
                ch["kpad"].append(kpad)
                ch["qt_pad"].append(qt_pad)
                ch["v_b"].append(vh)
                ch["do_pad"].append(do_pad)
                ch["ats"].append(_dot3(kst, qt_pad, _NT))
                ch["d_at"].append(lax.dot_general(vh, do_pad, _NT, preferred_element_type=F32))
                ch["d_a"].append(lax.dot_general(do_b, v_pad, _NT, preferred_element_type=F32))

        for ch in chunks:
            ch["d_kst"], ch["d_qt"], ch["dv"] = [], [], []
            for h in range(4):
                at = _masked_sum(ch["ats"][h], masks_st, 0)
                d_ats = jnp.concatenate([jnp.where(m, ch["d_at"][h], 0.0) for m in masks_st], axis=0)
                d_a_cat = jnp.concatenate([jnp.where(m, ch["d_a"][h], 0.0) for m in masks_ts], axis=1)
                ch["d_kst"].append(_dot3(d_ats, ch["qt_pad"][h], _NN))
                ch["d_qt"].append(_dot3(d_a_cat, ch["kpad"][h], _NN))
                ch["dv"].append(lax.dot_general(at.astype(BF16), ch["do_pad"][h], _NN, preferred_element_type=F32))

        for j in reversed(range(cps)):
            ch = chunks[j]
            q, k, sg, f, lb = ch["q"], ch["k"], ch["sg"], ch["f"], ch["lb"]
            dq_h, dk_h, dv_h, extra_h = [], [], [], []
            for h, sl in enumerate(heads):
                st_prev = st_in_ref[j, h]
                d_st = dst_ref[h]
                d_st_b = d_st.astype(BF16)
                do_b = ch["do_pad"][h][0:HGRN_CHUNK, :]
                kd, e_last = ch["kd"][:, sl], ch["e_last"][:, sl]
                dv = ch["dv"][h] + _dot(kd, d_st_b, _NT)
                d_qb = _dot(do_b, st_prev, _NN)
                d_kd = lax.dot_general(ch["v_b"][h], d_st_b, _NN, preferred_element_type=F32)
                extra_h.append(jnp.sum(st_prev * d_st, axis=0, keepdims=True) * e_last
                               + jnp.sum(kd * d_kd, axis=0, keepdims=True))
                dst_ref[h] = d_st * e_last + _dot(do_b, ch["qb"][:, sl], _TN)
                dq_h.append(ch["d_qt"][h] * ch["e_q"][:, sl] + d_qb * ch["e_b"][:, sl])
                dkk = d_kd * ch["e_bl"][:, sl]
                for s_ in range(nsub):
                    dkk = dkk + ch["d_kst"][h][HGRN_CHUNK * s_:HGRN_CHUNK * (s_ + 1), :] * ch["e_k"][s_][:, sl]
                dk_h.append(dkk)
                dv_h.append(dv)
            dq = jnp.concatenate(dq_h, axis=1)
            dk = jnp.concatenate(dk_h, axis=1)
            dv = jnp.concatenate(dv_h, axis=1)
            extra = jnp.concatenate(extra_h, axis=1)
            db = q * dq - k * dk + jnp.where(row == HGRN_CHUNK - 1, extra, 0.0)
            dg = _tri_apply(upper, db)
            df = dg / f - dk
            dhf = df * (1.0 - lb) * sg * (1.0 - sg)
            dhq = dq * _dsilu(ch["hq"], _sig(ch["hq"]))
            full = jnp.concatenate([dhq, dhf, dv, ch["dhg"]], axis=1)
            duh_ref[ch["rows"], :] = full.astype(duh_ref.dtype)
            ch["full"] = full
            dbias_ref[...] += jnp.sum(full, axis=0, keepdims=True)
            dlb_acc[...] += jnp.sum(df * (1.0 - sg), axis=0, keepdims=True)
        dng_ref[...] += dng
        _store_transposed(duh_t_ref, [ch["full"] for ch in chunks])

        @pl.when(i == ns - 1)
        def _():
            lb = chunks[0]["lb"]
            d_a0 = dlb_acc[...] * lb * (1.0 - lb)
            r8 = lax.broadcasted_iota(jnp.int32, (8, HG_W), 0)
            dlb_ref[...] = jnp.where(r8 == 0, d_a0, jnp.where(r8 == 1, -d_a0, 0.0))

    col = lambda j: pl.BlockSpec((rows_step, HG_W), lambda i: (ns - 1 - i, j))
    return _call(
        body, name=name, grid=(ns,), ins=[uh, uh, uh, uh, o_pre, d_r, states, lb_raw, norm_g],
        in_specs=[col(0), col(1), col(2), col(3), col(0), col(d_r.shape[1] // HG_W - 1),
                  pl.BlockSpec((cps, 4, LANES, LANES), lambda i: (ns - 1 - i, 0, 0, 0)),
                  pl.BlockSpec((2, HG_W), lambda i: (0, 0)), pl.BlockSpec((1, LANES), lambda i: (0, 0))],
        out_specs=[pl.BlockSpec((rows_step, UH_W), lambda i: (ns - 1 - i, 0)),
                   pl.BlockSpec((UH_W, rows_step), lambda i: (0, ns - 1 - i)),
                   pl.BlockSpec((1, UH_W), lambda i: (0, 0)),
                   pl.BlockSpec((1, LANES), lambda i: (0, 0)),
                   pl.BlockSpec((8, HG_W), lambda i: (0, 0))],
        out_shape=[jax.ShapeDtypeStruct((t, UH_W), BF16), jax.ShapeDtypeStruct((UH_W, t), BF16),
                   jax.ShapeDtypeStruct((1, UH_W), F32),
                   jax.ShapeDtypeStruct((1, LANES), F32), jax.ShapeDtypeStruct((8, HG_W), F32)],
        scratch_shapes=[pltpu.VMEM((4, LANES, LANES), F32), pltpu.VMEM((cps, HGRN_CHUNK, HG_W), F32),
                        pltpu.VMEM((1, HG_W), F32)],
        sem=("arbitrary",), comm=comm)


def _ln_bwd_math(dy, xhat, rstd, g):
    dxh = dy * g
    return rstd * (dxh - jnp.mean(dxh, axis=1, keepdims=True)
                   - xhat * jnp.mean(dxh * xhat, axis=1, keepdims=True))


def _mm_rows(a, b, extras, *, name, epilogue, out_shape, out_specs, tb=False, tm=512, tk=1408, pair2=None):
    m, kdim = a.shape
    n = b.shape[0] if tb else b.shape[1]
    tm = _pick(m, (tm, 256, 128))
    tk = _pick(kdim, (tk, 1408, 1024, 768, 512, 256, 128))
    nk = kdim // tk
    b_spec = pl.BlockSpec((n, tk), lambda i, k: (0, k)) if tb else pl.BlockSpec((tk, n), lambda i, k: (k, 0))
    dims = _NT if tb else _NN
    n_ex, n_out, n_p2 = len(extras), len(out_shape), (0 if pair2 is None else 2)

    def body(*refs):
        a_ref, b_ref = refs[0], refs[1]
        p2_refs = refs[2:2 + n_p2]
        ex_refs = refs[2 + n_p2:2 + n_p2 + n_ex]
        o_refs = refs[2 + n_p2 + n_ex:2 + n_p2 + n_ex + n_out]
        acc_ref = refs[-1]
        i, k = pl.program_id(0), pl.program_id(1)

        @pl.when(k == 0)
        def _():
            if n_p2:
                acc_ref[...] = _dot(p2_refs[0][...], p2_refs[1][...], _NN)
            else:
                acc_ref[...] = jnp.zeros_like(acc_ref)

        acc_ref[...] += _dot(a_ref[...], b_ref[...], dims)

        @pl.when(k == nk - 1)
        def _():
            epilogue(acc_ref[...], ex_refs, o_refs, i == 0)

    p2_specs, p2_ins = [], []
    if pair2 is not None:
        k2 = pair2[0].shape[1]
        p2_specs = [pl.BlockSpec((tm, k2), lambda i, k: (i, 0)), pl.BlockSpec((k2, n), lambda i, k: (0, 0))]
        p2_ins = list(pair2)
    return pl.pallas_call(
        body, name=name, grid=(m // tm, nk),
        in_specs=[pl.BlockSpec((tm, tk), lambda i, k: (i, k)), b_spec] + p2_specs + [sp for _, sp in extras],
        out_specs=list(out_specs), out_shape=list(out_shape),
        scratch_shapes=[pltpu.VMEM((tm, n), F32)],
        compiler_params=_cp("arbitrary", "arbitrary"),
    )(a, b, *p2_ins, *[arr for arr, _ in extras])


def _rows_specs(tm, d):
    row = pl.BlockSpec((tm, d), lambda i, k: (i, 0))
    vec = pl.BlockSpec((1, d), lambda i, k: (0, 0))
    col = pl.BlockSpec((tm, 1), lambda i, k: (i, 0))
    return row, vec, col


def _mm_ln_fwd(a, b, pair2, addend, g, beta, name, tm=512):
    t, d = addend.shape
    tm = _pick(t, (tm, 256, 128))
    row, vec, col = _rows_specs(tm, d)

    def epilogue(acc, ex, outs, first):
        z = acc + ALPHA * ex[0][...]
        mu = jnp.mean(z, axis=1, keepdims=True)
        zc = z - mu
        rstd = lax.rsqrt(jnp.mean(zc * zc, axis=1, keepdims=True) + LN_EPS)
        xhat = zc * rstd
        h = xhat * ex[1][...] + ex[2][...]
        outs[0][...] = h
        outs[1][...] = h.astype(BF16)
        outs[2][...] = xhat
        outs[3][...] = rstd

    return _mm_rows(a, b, [(addend, row), (g, vec), (beta, vec)], name=name, epilogue=epilogue, tm=tm, pair2=pair2,
                    out_shape=[jax.ShapeDtypeStruct((t, d), F32), jax.ShapeDtypeStruct((t, d), BF16),
                               jax.ShapeDtypeStruct((t, d), F32), jax.ShapeDtypeStruct((t, 1), F32)],
                    out_specs=[row, row, row, col])


CONV_RB = 32
HALO = 8


def _sum8(x):
    acc = x[0:8]
    for r in range(8, x.shape[0], 8):
        acc = acc + x[r:r + 8]
    return acc


FFN_TILE = 256
FFN_COLS = 256


def _rows_before(win, k):
    return pltpu.roll(win, k, 0)[HALO:]


def _rows_after(win, k):
    n = win.shape[0]
    return pltpu.roll(win, n - k, 0)[0:n - HALO]


def _resident(shape):
    return pl.BlockSpec(shape, lambda i: (0,) * len(shape), pipeline_mode=pl.Buffered(1))


def _ffn_fwd(h1b, h1, w_up_t, conv_w, conv_b, w_down, target, ln2_g, ln2_b, name, comm=None):
    t, d = h1.shape
    tr = _pick(t, (FFN_TILE, 128))
    nblk = D_FF // FFN_COLS
    rb = CONV_RB

    def body(a_ref, wup_ref, cw_ref, cb_ref, wd_ref, h1_ref, tgt_ref, g_ref, b_ref,
             u2_ref, hm_ref, dz_ref, dg_ref, db_ref, loss_ref, ext):
        i = pl.program_id(0)

        @pl.when(i == 0)
        def _():
            ext[0:HALO, :] = jnp.zeros((HALO, D_FF), F32)
            dg_ref[...] = jnp.zeros_like(dg_ref)
            db_ref[...] = jnp.zeros_like(db_ref)
            loss_ref[...] = jnp.zeros_like(loss_ref)

        a = a_ref[...]
        for c in range(nblk):
            cs = slice(c * FFN_COLS, (c + 1) * FFN_COLS)
            vs = slice(D_FF + c * FFN_COLS, D_FF + (c + 1) * FFN_COLS)
            gate_pre = lax.dot_general(a, wup_ref[cs, :], _NT, preferred_element_type=F32)
            u2_ref[:, cs] = gate_pre
            ext[HALO:, cs] = gate_pre
            u2_ref[:, vs] = lax.dot_general(a, wup_ref[vs, :], _NT, preferred_element_type=F32)
        acc = jnp.zeros((tr, d), F32)
        for c in range(nblk):
            cs = slice(c * FFN_COLS, (c + 1) * FFN_COLS)
            for sub in range(FFN_COLS // LANES):
                ln = slice(c * FFN_COLS + sub * LANES, c * FFN_COLS + (sub + 1) * LANES)
                vl = slice(D_FF + c * FFN_COLS + sub * LANES, D_FF + c * FFN_COLS + (sub + 1) * LANES)
                w0, w1, w2, bb = cw_ref[0:1, ln], cw_ref[1:2, ln], cw_ref[2:3, ln], cb_ref[:, ln]
                for r0 in range(0, tr, rb):
                    win = ext[r0:r0 + HALO + rb, ln]
                    gate = _rows_before(win, 2) * w0 + _rows_before(win, 1) * w1 + win[HALO:] * w2 + bb
                    hm_ref[r0:r0 + rb, ln] = (gate * _sig(gate) * u2_ref[r0:r0 + rb, vl]).astype(hm_ref.dtype)
            acc = acc + lax.dot_general(hm_ref[:, cs], wd_ref[cs, :], _NN, preferred_element_type=F32)
        ext[0:HALO, :] = ext[tr:tr + HALO, :]

        z = acc + ALPHA * h1_ref[...]
        gg = g_ref[...]
        mu = jnp.mean(z, axis=1, keepdims=True)
        zc = z - mu
        rstd = lax.rsqrt(jnp.mean(zc * zc, axis=1, keepdims=True) + LN_EPS)
        xhat = zc * rstd
        err = xhat * gg + b_ref[...] - tgt_ref[...]
        loss_ref[...] += 0.5 * jnp.sum(jnp.mean(err * err, axis=1, keepdims=True))
        dy = err * (1.0 / d)
        dz_ref[...] = _ln_bwd_math(dy, xhat, rstd, gg)
        dg_ref[...] += jnp.sum(dy * xhat, axis=0, keepdims=True)
        db_ref[...] += jnp.sum(dy, axis=0, keepdims=True)

    row = lambda w: pl.BlockSpec((tr, w), lambda i: (i, 0))
    vec = pl.BlockSpec((1, d), lambda i: (0, 0))
    return _call(
        body, name=name, grid=(t // tr,),
        ins=[h1b, w_up_t, conv_w, conv_b, w_down, h1, target, ln2_g, ln2_b],
        in_specs=[row(d), _resident((2 * D_FF, d)), _resident((3, D_FF)), _resident((1, D_FF)),
                  _resident((D_FF, d)), row(d), row(d), vec, vec],
        out_specs=[row(2 * D_FF), row(D_FF), row(d), vec, vec, pl.BlockSpec((1, LANES), lambda i: (0, 0))],
        out_shape=[jax.ShapeDtypeStruct((t, 2 * D_FF), F32), jax.ShapeDtypeStruct((t, D_FF), BF16),
                   jax.ShapeDtypeStruct((t, d), F32), jax.ShapeDtypeStruct((1, d), F32),
                   jax.ShapeDtypeStruct((1, d), F32), jax.ShapeDtypeStruct((1, LANES), F32)],
        scratch_shapes=[pltpu.VMEM((tr + HALO, D_FF), F32)],
        sem=("arbitrary",), comm=comm)


def _ffn_bwd(dz2, u2, w_down, w_up_t, conv_w, conv_b, xhat1, rstd1, ln1_g, name, comm=None):
    t, d = dz2.shape
    tr = _pick(t, (FFN_TILE, 128))
    nt = t // tr
    hb = tr // HALO
    nblk = D_FF // FFN_COLS
    rb = CONV_RB

    def body(dz2_ref, dz2_next_ref, u2_ref, gp_prev_ref, wd_ref, wup_ref, cw_ref, cb_ref, xhat_ref, rstd_ref,
             g1_ref, du_ref, dz1_ref, dw_ref, dcb_ref, dg1_ref, db1_ref, head, dh_s, dg_s):
        i = pl.program_id(0)

        @pl.when(i == 0)
        def _():
            dg_s[tr:, :] = jnp.zeros((HALO, D_FF), F32)
            dw_ref[...] = jnp.zeros_like(dw_ref)
            dcb_ref[...] = jnp.zeros_like(dcb_ref)
            dg1_ref[...] = jnp.zeros_like(dg1_ref)
            db1_ref[...] = jnp.zeros_like(db1_ref)

        dz2 = dz2_ref[...]

        @pl.when(i == 0)
        def _():
            dz2_b = dz2.astype(BF16)
            for c in range(nblk):
                cs = slice(c * FFN_COLS, (c + 1) * FFN_COLS)
                dh_s[:, cs] = lax.dot_general(dz2_b, wd_ref[cs, :], _NT, preferred_element_type=F32)

        dz2_next = dz2_next_ref[...].astype(BF16)
        dh_next = [lax.dot_general(dz2_next, wd_ref[c * FFN_COLS:(c + 1) * FFN_COLS, :], _NT,
                                   preferred_element_type=F32) for c in range(nblk)]
        head[0:HALO, :] = jnp.where(i == nt - 1, 0.0, gp_prev_ref[...])
        head[HALO:, :] = u2_ref[0:rb, 0:D_FF]

        acc = jnp.zeros((tr, d), F32)
        for blk in range(nblk):
            for c in range(blk * FFN_COLS // LANES, (blk + 1) * FFN_COLS // LANES):
                ln = slice(c * LANES, (c + 1) * LANES)
                vl = slice(D_FF + c * LANES, D_FF + (c + 1) * LANES)
                w0, w1, w2, bb = cw_ref[0:1, ln], cw_ref[1:2, ln], cw_ref[2:3, ln], cb_ref[:, ln]
                acc_b = jnp.zeros((8, LANES), F32)
                acc_w = [jnp.zeros((8, LANES), F32) for _ in range(3)]
                for r0 in range(0, tr, rb):
                    win = head[:, ln] if r0 == 0 else u2_ref[r0 - HALO:r0 + rb, ln]
                    g_m2, g_m1, g_0 = _rows_before(win, 2), _rows_before(win, 1), win[HALO:]
                    gate = g_m2 * w0 + g_m1 * w1 + g_0 * w2 + bb
                    sg = _sig(gate)
                    dh = dh_s[r0:r0 + rb, ln]
                    dgate = dh * u2_ref[r0:r0 + rb, vl] * _dsilu(gate, sg)
                    dg_s[r0:r0 + rb, ln] = dgate
                    du_ref[r0:r0 + rb, vl] = (dh * (gate * sg)).astype(du_ref.dtype)
                    acc_b = acc_b + _sum8(dgate)
                    acc_w[0] = acc_w[0] + _sum8(dgate * g_m2)
                    acc_w[1] = acc_w[1] + _sum8(dgate * g_m1)
                    acc_w[2] = acc_w[2] + _sum8(dgate * g_0)
                dcb_ref[:, ln] += jnp.sum(acc_b, axis=0, keepdims=True)
                for j in range(3):
                    dw_ref[j:j + 1, ln] += jnp.sum(acc_w[j], axis=0, keepdims=True)
                for r0 in range(0, tr, rb):
                    win = dg_s[r0:r0 + rb + HALO, ln]
                    d_gp = _rows_after(win, 2) * w0 + _rows_after(win, 1) * w1 + win[0:rb] * w2
                    du_ref[r0:r0 + rb, ln] = d_gp.astype(du_ref.dtype)
            cs = slice(blk * FFN_COLS, (blk + 1) * FFN_COLS)
            vs = slice(D_FF + blk * FFN_COLS, D_FF + (blk + 1) * FFN_COLS)
            acc = acc + lax.dot_general(du_ref[:, cs], wup_ref[cs, :], _NN, preferred_element_type=F32)
            acc = acc + lax.dot_general(du_ref[:, vs], wup_ref[vs, :], _NN, preferred_element_type=F32)
        dg_s[tr:, :] = dg_s[0:HALO, :]
        for c in range(nblk):
            dh_s[:, c * FFN_COLS:(c + 1) * FFN_COLS] = dh_next[c]
        dy = acc + ALPHA * dz2
        xh = xhat_ref[...]
        dz1_ref[...] = _ln_bwd_math(dy, xh, rstd_ref[...], g1_ref[...])
        dg1_ref[...] += jnp.sum(dy * xh, axis=0, keepdims=True)
        db1_ref[...] += jnp.sum(dy, axis=0, keepdims=True)

    rev = lambda w: pl.BlockSpec((tr, w), lambda i: (nt - 1 - i, 0))
    vec = pl.BlockSpec((1, d), lambda i: (0, 0))
    return _call(
        body, name=name, grid=(nt,),
        ins=[dz2, dz2, u2, u2, w_down, w_up_t, conv_w, conv_b, xhat1, rstd1, ln1_g],
        in_specs=[rev(d), pl.BlockSpec((tr, d), lambda i: (jnp.maximum(nt - 2 - i, 0), 0)), rev(2 * D_FF),
                  pl.BlockSpec((HALO, D_FF), lambda i: (jnp.maximum((nt - 1 - i) * hb - 1, 0), 0)),
                  _resident((D_FF, d)), _resident((2 * D_FF, d)), _resident((3, D_FF)), _resident((1, D_FF)),
                  rev(d), pl.BlockSpec((tr, 1), lambda i: (nt - 1 - i, 0)), vec],
        out_specs=[rev(2 * D_FF), rev(d), pl.BlockSpec((8, D_FF), lambda i: (0, 0)),
                   pl.BlockSpec((1, D_FF), lambda i: (0, 0)), vec, vec],
        out_shape=[jax.ShapeDtypeStruct((t, 2 * D_FF), BF16), jax.ShapeDtypeStruct((t, d), F32),
                   jax.ShapeDtypeStruct((8, D_FF), F32), jax.ShapeDtypeStruct((1, D_FF), F32),
                   jax.ShapeDtypeStruct((1, d), F32), jax.ShapeDtypeStruct((1, d), F32)],
        scratch_shapes=[pltpu.VMEM((HALO + rb, D_FF), F32), pltpu.VMEM((tr, D_FF), F32),
                        pltpu.VMEM((tr + HALO, D_FF), F32)],
        sem=("arbitrary",), comm=comm)


def _pad_rows(a, rows):
    return jnp.pad(a, ((0, rows - a.shape[0]), (0, 0)))


SMALL_LAYOUT = (("ln1_g", 1024), ("ln1_b", 1024), ("b_in", 2816), ("sinks", 8), ("hgrn_lb", 1024),
                ("hgrn_norm_g", 128), ("ln2_g", 1024), ("ln2_b", 1024), ("conv_b", 2816), ("loss", 1))
SMALL_SHAPES = {"ln1_g": (1, 1024), "ln1_b": (1, 1024), "b_in": (1, 2816), "sinks": (1, 8), "hgrn_lb": (2, 512),
                "hgrn_norm_g": (1, 128), "ln2_g": (1, 1024), "ln2_b": (1, 1024), "conv_b": (1, 2816),
                "loss": (1,)}


def _pack_small(parts):
    rows = []
    for name, size in SMALL_LAYOUT:
        flat = parts[name].reshape(-1).astype(F32)
        padded = -(-size // LANES) * LANES
        rows.append(jnp.pad(flat, (0, padded - size)).reshape(-1, LANES))
    return _pad_rows(jnp.concatenate(rows, axis=0), SMALL_ROWS)


def _small_update(small_g, ws, ms, vs, name):
    names = [n for n, _ in SMALL_LAYOUT if n != "loss"]
    first, r = {}, 0
    for n, size in SMALL_LAYOUT:
        first[n] = r
        r += -(-size // LANES)
    npar = len(names)

    def body(*refs):
        g_ref = refs[0]
        w_refs, m_refs, v_refs = (refs[1 + q * npar:1 + (q + 1) * npar] for q in range(3))
        outs = refs[1 + 3 * npar:-1]
        sum_ref = refs[-1]
        acc = g_ref[0]
        for s in range(1, N_DEV):
            acc = acc + g_ref[s]
        sum_ref[...] = acc
        outs[0][...] = sum_ref[first["loss"]:first["loss"] + 1, 0:1]
        for p, n in enumerate(names):
            g_out, d_out, m_out, v_out = outs[1 + 4 * p:5 + 4 * p]
            rows, cols = SMALL_SHAPES[n]
            if cols < LANES:
                g_out[...] = sum_ref[first[n]:first[n] + 1, 0:cols]
            else:
                per = cols // LANES
                for h in range(rows):
                    for j in range(per):
                        rr = first[n] + h * per + j
                        g_out[h:h + 1, j * LANES:(j + 1) * LANES] = sum_ref[rr:rr + 1, :]
            d_out[...], m_out[...], v_out[...] = _adamw_math(w_refs[p][...], g_out[...], m_refs[p][...],
                                                            v_refs[p][...])

    out_shape = [jax.ShapeDtypeStruct((1, 1), F32)]
    for n in names:
        out_shape += [jax.ShapeDtypeStruct(SMALL_SHAPES[n], F32)] * 4
    res = pl.pallas_call(
        body, name=name, out_shape=out_shape,
        scratch_shapes=[pltpu.VMEM((SMALL_ROWS, LANES), F32)],
        compiler_params=_cp(),
    )(small_g, *[ws[n] for n in names], *[ms[n] for n in names], *[vs[n] for n in names])
    return res[0], {n: res[1 + 4 * p:5 + 4 * p] for p, n in enumerate(names)}


def _conv_w_update(recv, w, m, v, name):
    taps, cols = w.shape

    def body(r_ref, w_ref, m_ref, v_ref, g_ref, d_ref, nm_ref, nv_ref):
        acc = r_ref[0]
        for s in range(1, N_DEV):
            acc = acc + r_ref[s]
        g = acc[0:taps]
        g_ref[...] = g
        d_ref[...], nm_ref[...], nv_ref[...] = _adamw_math(w_ref[...], g, m_ref[...], v_ref[...])

    shp = jax.ShapeDtypeStruct((taps, cols), F32)
    return pl.pallas_call(body, name=name, out_shape=[shp, shp, shp, shp], compiler_params=_cp())(recv, w, m, v)


def _own(full, rows):
    return lax.dynamic_slice_in_dim(full, _me() * rows, rows, axis=0)


def kernel(x, positions, ln1_g, ln1_b, w_in, b_in, sinks, hgrn_lb, hgrn_norm_g, w_o, ln2_g, ln2_b, w_up, conv_w, conv_b, w_down, loss_target, m_ln1_g, m_ln1_b, m_w_in, m_b_in, m_sinks, m_hgrn_lb, m_hgrn_norm_g, m_w_o, m_ln2_g, m_ln2_b, m_w_up, m_conv_w, m_conv_b, m_w_down, v_ln1_g, v_ln1_b, v_w_in, v_b_in, v_sinks, v_hgrn_lb, v_hgrn_norm_g, v_w_o, v_ln2_g, v_ln2_b, v_w_up, v_conv_w, v_conv_b, v_w_down):
    t = x.shape[1]
    x2 = x[0]
    target = loss_target[0]
    pos_col = positions.reshape(t, 1)

    w_in_t_s = w_in[0].T.astype(BF16)
    w_up_t_s = w_up[0].T.astype(BF16)
    w_o_s = w_o[0].astype(BF16)
    w_down_s = w_down[0].astype(BF16)
    (ctab, stab, xb), (w_in_t_g, cw_g) = _prep(
        pos_col, x2, "prep_ag_w_in", _Comm([{"kind": "gather", "arr": w_in_t_s}, {"kind": "gather", "arr": _pad_rows(conv_w[0], 8)}]))
    w_in_t = w_in_t_g.reshape(D_FF, D_MODEL)
    w_a_t, w_h_t = w_in_t[:UA_W], w_in_t[UA_W:]
    conv_w_f = cw_g[:, 0:3].transpose(1, 0, 2).reshape(3, D_FF)

    ua = _mm(xb, w_a_t, tb=True, bias=b_in[:, :UA_W], name="fwd_in_attn")
    uh, (w_down_g,) = _mm(xb, w_h_t, tb=True, bias=b_in[:, UA_W:], name="fwd_in_hgrn",
                          comm=_Comm([{"kind": "gather", "arr": w_down_s}]))
    half_up = SHARD_UP // 2
    (a_out, a_out_t), (w_o_g, w_up_half) = _attn_fwd(
        ua, ctab, stab, sinks, "attn_fwd",
        comm=_Comm([{"kind": "gather", "arr": w_o_s},
                    {"kind": "gather", "arr": w_up_t_s, "rows": (0, half_up), "dst_rows": SHARD_UP}]))
    (r_out, r_out_t, o_pre, states), (w_up_t_g,) = _hgrn_fwd(
        uh, hgrn_lb, hgrn_norm_g, "hgrn_fwd",
        comm=_Comm([{"kind": "gather", "arr": w_up_t_s, "rows": (half_up, half_up), "dst_rows": SHARD_UP,
                     "dst_first": half_up, "into": w_up_half}]))
    w_down_f = w_down_g.reshape(D_FF, D_MODEL)
    w_o_f = w_o_g.reshape(D_MODEL, D_MODEL)
    w_up_t = w_up_t_g.reshape(2 * D_FF, D_MODEL)
    h1, h1b, xhat1, rstd1 = _mm_ln_fwd(r_out, w_o_f[ATTN_W:], (a_out, w_o_f[:ATTN_W]), x2, ln1_g, ln1_b,
                                       "fwd_o_ln1")
    u2, hmid, dz2, d_ln2_g, d_ln2_b, loss_part = _ffn_fwd(h1b, h1, w_up_t, conv_w_f, conv_b, w_down_f, target,
                                                         ln2_g, ln2_b, "ffn_fwd")[0]

    d_w_down, d_w_down_b = _mm(hmid, dz2, ta=True, out_dtype2=BF16, tm=1408, tk=1024, name="bwd_down_dw")
    (d_u2, dz1, d_conv_w8, d_conv_b, d_ln1_g, d_ln1_b), (recv_down,) = _ffn_bwd(
        dz2, u2, w_down_f, w_up_t, conv_w_f, conv_b, xhat1, rstd1, ln1_g, "ffn_bwd",
        comm=_Comm([{"kind": "exchange", "arr": d_w_down_b.reshape(N_DEV, SHARD_DOWN, D_MODEL)}]))
    d_w_up_t, d_w_up_t_b = _mm(d_u2, h1b, ta=True, out_dtype2=BF16, tm=1408, tk=1024, name="bwd_up_dw")
    d_ar = _mm(dz1, w_o_f, tb=True, name="bwd_o_dx")
    d_w_o_part = _mm(a_out_t, dz1, out_dtype2=BF16, tm=ATTN_W, out_rows=D_MODEL, name="bwd_o_dw_attn")
    d_w_o, d_w_o_b = _mm(r_out_t, dz1, out_dtype2=BF16, tm=HG_W, out_rows=D_MODEL, first_row=ATTN_W,
                         into=d_w_o_part, name="bwd_o_dw_hgrn")
    d_w_up_x = d_w_up_t_b.reshape(N_DEV, SHARD_UP, D_MODEL)
    half = SHARD_UP // 2
    d_cw_x = d_conv_w8.reshape(8, N_DEV, SHARD_IN).transpose(1, 0, 2)
    (d_ua, d_ua_t, d_bias_a, d_sinks), (recv_up_half, recv_cw) = _attn_bwd(
        ua, d_ar, ctab, stab, sinks, "attn_bwd",
        comm=_Comm([{"kind": "exchange", "arr": d_w_up_x, "rows": (0, half), "dst_rows": SHARD_UP},
                    {"kind": "exchange", "arr": d_cw_x}]))
    (d_uh, d_uh_t, d_bias_h, d_norm_g, d_lb8), (recv_up, recv_o) = _hgrn_bwd(
        uh, o_pre, d_ar, states, hgrn_lb, hgrn_norm_g, "hgrn_bwd",
        comm=_Comm([{"kind": "exchange", "arr": d_w_up_x, "rows": (half, half), "dst_rows": SHARD_UP,
                     "dst_first": half, "into": recv_up_half},
                    {"kind": "exchange", "arr": d_w_o_b.reshape(N_DEV, SHARD_O, D_MODEL)}]))
    d_w_in_part = _mm(d_ua_t, xb, out_dtype2=BF16, tm=UA_W, tk=t, out_rows=D_FF, name="bwd_in_dw_attn")
    d_w_in_t, d_w_in_t_b = _mm(d_uh_t, xb, out_dtype2=BF16, tm=256, tk=t, out_rows=D_FF, first_row=UA_W,
                               into=d_w_in_part, name="bwd_in_dw_hgrn")
    small_local = _pack_small({
        "ln1_g": d_ln1_g, "ln1_b": d_ln1_b, "b_in": jnp.concatenate([d_bias_a, d_bias_h], axis=1),
        "sinks": d_sinks[:, :8], "hgrn_lb": d_lb8[0:2], "hgrn_norm_g": d_norm_g, "ln2_g": d_ln2_g,
        "ln2_b": d_ln2_b, "conv_b": d_conv_b, "loss": loss_part[:, :1]})
    d_w_in_x = d_w_in_t_b.reshape(N_DEV, SHARD_IN, D_MODEL)
    res_up, (from_sibling,) = _sum_shards_adamw(
        [recv_up], _own(d_w_up_t, SHARD_UP), w_up[0].T, m_w_up[0].T, v_w_up[0].T, "adamw_w_up",
        comm=_Comm([{"kind": "pair4", "arr": d_w_in_x}]))
    res_up = [r.T for r in res_up]
    own_in, chip_part = _pair_reduce(from_sibling, d_w_in_t, "pair_reduce_w_in")
    dx, (from_chips, small_g) = _mm(d_uh, w_h_t, addend=dz1, addend_scale=ALPHA, name="bwd_in_dx_hgrn",
                                    comm=_Comm([{"kind": "chips3", "arr": chip_part},
                                                {"kind": "gather", "arr": small_local}]))
    dx = _mm(d_ua, w_a_t, addend=dx, tk=768, name="bwd_in_dx_attn")

    res_in = [r.T for r in _chip_sum_adamw(from_chips, own_in, w_in[0].T, m_w_in[0].T, v_w_in[0].T, "adamw_w_in")]
    res_o = _sum_shards_adamw([recv_o], _own(d_w_o, SHARD_O), w_o[0], m_w_o[0], v_w_o[0], "adamw_w_o")
    res_down = _sum_shards_adamw([recv_down], _own(d_w_down, SHARD_DOWN), w_down[0], m_w_down[0], v_w_down[0],
                                 "adamw_w_down")
    res_cw = _conv_w_update(recv_cw, conv_w[0], m_conv_w[0], v_conv_w[0], "adamw_conv_w")
    big = {"w_in": [r[None] for r in res_in], "w_up": [r[None] for r in res_up],
           "w_o": [r[None] for r in res_o], "w_down": [r[None] for r in res_down],
           "conv_w": [r[None] for r in res_cw]}

    loss11, small = _small_update(
        small_g,
        {"ln1_g": ln1_g, "ln1_b": ln1_b, "b_in": b_in, "sinks": sinks, "hgrn_lb": hgrn_lb,
         "hgrn_norm_g": hgrn_norm_g, "ln2_g": ln2_g, "ln2_b": ln2_b, "conv_b": conv_b},
        {"ln1_g": m_ln1_g, "ln1_b": m_ln1_b, "b_in": m_b_in, "sinks": m_sinks, "hgrn_lb": m_hgrn_lb,
         "hgrn_norm_g": m_hgrn_norm_g, "ln2_g": m_ln2_g, "ln2_b": m_ln2_b, "conv_b": m_conv_b},
        {"ln1_g": v_ln1_g, "ln1_b": v_ln1_b, "b_in": v_b_in, "sinks": v_sinks, "hgrn_lb": v_hgrn_lb,
         "hgrn_norm_g": v_hgrn_norm_g, "ln2_g": v_ln2_g, "ln2_b": v_ln2_b, "conv_b": v_conv_b},
        "adamw_small")
    loss = loss11[0, 0]

    order = ["ln1_g", "ln1_b", "w_in", "b_in", "sinks", "hgrn_lb", "hgrn_norm_g", "w_o", "ln2_g", "ln2_b",
             "w_up", "conv_w", "conv_b", "w_down"]

    def pick(idx):
        return [big[n][idx] if n in big else small[n][idx] for n in order]

    return (loss, dx[None], *pick(0), *pick(1), *pick(2), *pick(3))
```

```python
import functools

import jax
import jax.numpy as jnp
import numpy as np
from jax import lax
from jax.experimental import pallas as pl
from jax.experimental.pallas import tpu as pltpu

F32 = jnp.float32
BF16 = jnp.bfloat16

N_DEV = 8
D_MODEL = 1024
D_FF = 2816
ATTN_W = 512
KV_W = 128
UA_W = ATTN_W + 2 * KV_W
UH_W = 2048
HG_W = 512
ATTN_BLOCK = 128
HGRN_CHUNK = 64
HGRN_SUB = 16
HGRN_CHUNKS_PER_STEP = 4
EXP_CLAMP = 85.0
NEG_BIG = -1e30
LN_EPS = 1e-5
RMS_EPS = 1e-6
ALPHA = 2.0 ** 0.25
ATTN_SCALE = 0.125
ROPE_THETA = 500000.0

ADAM_LR = 0.001
ADAM_B1 = 0.9
ADAM_B2 = 0.999
ADAM_EPS = 1e-08
ADAM_WD = 0.01
ADAM_STEP = 10

LANES = 128
VMEM_LIMIT_BYTES = 56 * 1024 * 1024

SHARD_IN = D_FF // N_DEV
SHARD_UP = 2 * D_FF // N_DEV
SHARD_O = D_MODEL // N_DEV
SHARD_DOWN = D_FF // N_DEV
SMALL_ROWS = 88

_MESH = pl.DeviceIdType.MESH
_NT = (((1,), (1,)), ((), ()))
_NN = (((1,), (0,)), ((), ()))
_TN = (((0,), (0,)), ((), ()))


def _cp(*sem):
    if sem:
        return pltpu.CompilerParams(dimension_semantics=sem, vmem_limit_bytes=VMEM_LIMIT_BYTES)
    return pltpu.CompilerParams(vmem_limit_bytes=VMEM_LIMIT_BYTES)


def _sig(x):
    return 0.5 * jnp.tanh(0.5 * x) + 0.5


def _dsilu(x, s):
    return s * (1.0 + x * (1.0 - s))


def _dot(a, b, dims):
    return lax.dot_general(a.astype(BF16), b.astype(BF16), dims, preferred_element_type=F32)


def _split(a):
    hi = a.astype(BF16)
    return hi, (a - hi.astype(F32)).astype(BF16)


def _dot3(a, b, dims):
    ah, al = _split(a)
    bh, bl = _split(b)
    d = functools.partial(lax.dot_general, dimension_numbers=dims, preferred_element_type=F32)
    return d(ah, bh) + (d(ah, bl) + d(al, bh))


def _pick(n, pref):
    for t in pref:
        if t <= n and n % t == 0:
            return t
    return n


def _my_coords():
    return lax.axis_index("x"), lax.axis_index("y"), lax.axis_index("c")


def _peer(k):
    x, y, c = _my_coords()
    return (1 - x if k & 4 else x, 1 - y if k & 2 else y, 1 - c if k & 1 else c)


def _me():
    x, y, c = _my_coords()
    return 4 * x + 2 * y + c


class _Comm:
    def __init__(self, items):
        self.items = []
        for it in items:
            arr = it["arr"]
            full = arr.shape[0] if it["kind"] == "gather" else arr.shape[1]
            first, count = it.get("rows", (0, full))
            self.items.append(dict(kind=it["kind"], arr=arr, first=first, count=count,
                                   dst_rows=it.get("dst_rows", count), dst_first=it.get("dst_first", 0),
                                   into=it.get("into")))
        self.n = len(self.items)
        self.arrays = [it["arr"] for it in self.items]
        self.intos = [(a, it["into"]) for a, it in enumerate(self.items) if it["into"] is not None]

    def out_shapes(self):
        return [jax.ShapeDtypeStruct((4 if it["kind"] in ("pair4", "chips3") else N_DEV, it["dst_rows"],
                                      it["arr"].shape[-1]), it["arr"].dtype) for it in self.items]

    def specs(self, n=None):
        return [pl.BlockSpec(memory_space=pl.ANY)] * (self.n if n is None else n)

    def scratch(self):
        return [pltpu.SemaphoreType.DMA(((N_DEV - 1) * self.n,)), pltpu.SemaphoreType.DMA(((N_DEV - 1) * self.n,)),
                pltpu.SemaphoreType.DMA((self.n,))]

    def _src(self, a, ref, dev):
        it = self.items[a]
        blk = ref if it["kind"] == "gather" else ref.at[dev]
        return blk.at[pl.ds(it["first"], it["count"])]

    def _dst(self, a, ref, slot):
        it = self.items[a]
        return ref.at[slot].at[pl.ds(it["dst_first"], it["count"])]

    def _copy(self, a, k, src, dst, sems, me, slot):
        other = jnp.bitwise_xor(me, k)
        idx = a * (N_DEV - 1) + k - 1
        return pltpu.make_async_remote_copy(
            src_ref=self._src(a, src, other), dst_ref=self._dst(a, dst, me if slot == "mine" else other),
            send_sem=sems[0].at[idx], recv_sem=sems[1].at[idx], device_id=_peer(k), device_id_type=_MESH)

    def _pass_on(self, a, k, dst, sems, me):
        slot = self._dst(a, dst, jnp.bitwise_xor(me, k))
        idx = a * (N_DEV - 1) + k
        return pltpu.make_async_remote_copy(
            src_ref=slot, dst_ref=slot, send_sem=sems[0].at[idx], recv_sem=sems[1].at[idx],
            device_id=_peer(1), device_id_type=_MESH)

    def _part(self, a, r, src, dst, sems, me):
        it = self.items[a]
        idx = a * (N_DEV - 1) + r
        if it["kind"] == "pair4":
            k, slot = 1, jnp.bitwise_xor(jnp.bitwise_xor(me, 1), 2 * r)
        else:
            k, slot = 2 * r, r
        return pltpu.make_async_remote_copy(
            src_ref=src.at[slot].at[pl.ds(it["first"], it["count"])], dst_ref=self._dst(a, dst, r),
            send_sem=sems[0].at[idx], recv_sem=sems[1].at[idx], device_id=_peer(k), device_id_type=_MESH)

    def _parts(self, a):
        return range(4) if self.items[a]["kind"] == "pair4" else range(1, 4)

    def _local(self, a, src, dst, sems, me):
        return pltpu.make_async_copy(self._src(a, src, me), self._dst(a, dst, me), sems[2].at[a])

    def start(self, srcs, dsts, sems):
        me = _me()
        for a, (src, dst) in enumerate(zip(srcs, dsts)):
            if self.items[a]["kind"] in ("pair4", "chips3"):
                for r in self._parts(a):
                    self._part(a, r, src, dst, sems, me).start()
                continue
            direct = (1, 2, 4, 6) if self.items[a]["kind"] == "gather" else range(1, N_DEV)
            self._local(a, src, dst, sems, me).start()
            for k in direct:
                self._copy(a, k, src, dst, sems, me, "mine").start()

    def wait(self, srcs, dsts, sems):
        me = _me()
        for a, (src, dst) in enumerate(zip(srcs, dsts)):
            if self.items[a]["kind"] in ("pair4", "chips3"):
                for r in self._parts(a):
                    self._part(a, r, src, dst, sems, me).wait_recv()
                for r in self._parts(a):
                    self._part(a, r, src, dst, sems, me).wait_send()
                continue
            if self.items[a]["kind"] == "gather":
                for k in (2, 4, 6):
                    self._copy(a, k, src, dst, sems, me, "theirs").wait_recv()
                    self._pass_on(a, k, dst, sems, me).start()
                for k in (1, 3, 5, 7):
                    self._copy(a, k, src, dst, sems, me, "theirs").wait_recv()
                for k in (1, 2, 4, 6):
                    self._copy(a, k, src, dst, sems, me, "mine").wait_send()
                for k in (2, 4, 6):
                    self._pass_on(a, k, dst, sems, me).wait_send()
            else:
                for k in range(1, N_DEV):
                    self._copy(a, k, src, dst, sems, me, "theirs").wait_recv()
                for k in range(1, N_DEV):
                    self._copy(a, k, src, dst, sems, me, "mine").wait_send()
            self._local(a, src, dst, sems, me).wait()


def _call(body, *, name, grid, ins, in_specs, out_specs, out_shape, scratch_shapes=(), sem, comm=None):
    n_in, n_out, n_scr = len(ins), len(out_shape), len(scratch_shapes)
    if comm is None:
        outs = pl.pallas_call(
            body, name=name, grid=grid, in_specs=list(in_specs), out_specs=list(out_specs),
            out_shape=list(out_shape), scratch_shapes=list(scratch_shapes), compiler_params=_cp(*sem))(*ins)
        return list(outs), []
    nc, n_into = comm.n, len(comm.intos)

    def hosted(*refs):
        pos = n_in
        c_in = refs[pos:pos + nc]
        pos += nc + n_into
        outs = refs[pos:pos + n_out]
        pos += n_out
        c_out = refs[pos:pos + nc]
        pos += nc
        scr = refs[pos:pos + n_scr]
        sems = refs[pos + n_scr:]
        ids = [pl.program_id(d) for d in range(len(grid))]
        first = functools.reduce(jnp.logical_and, [i == 0 for i in ids])
        last = functools.reduce(jnp.logical_and, [i == g - 1 for i, g in zip(ids, grid)])

        @pl.when(first)
        def _():
            comm.start(c_in, c_out, sems)

        body(*refs[:n_in], *outs, *scr)

        @pl.when(last)
        def _():
            comm.wait(c_in, c_out, sems)

    aliases = {n_in + nc + j: n_out + a for j, (a, _) in enumerate(comm.intos)}
    outs = pl.pallas_call(
        hosted, name=name, grid=grid, in_specs=list(in_specs) + comm.specs() + comm.specs(n_into),
        out_specs=list(out_specs) + comm.specs(), out_shape=list(out_shape) + comm.out_shapes(),
        scratch_shapes=list(scratch_shapes) + comm.scratch(), input_output_aliases=aliases,
        compiler_params=_cp(*(["arbitrary"] * len(grid))))(*ins, *comm.arrays, *[arr for _, arr in comm.intos])
    return list(outs[:n_out]), list(outs[n_out:])


def _slot_sum(recv_ref, own_ref, shape):
    me = _me()
    acc = jnp.zeros(shape, F32)
    for s in range(N_DEV):
        acc = acc + jnp.where(me == s, own_ref[...], recv_ref[s].astype(F32))
    return acc


def _adamw_math(w, g, m, v):
    nm = ADAM_B1 * m + (1.0 - ADAM_B1) * g
    nv = ADAM_B2 * v + (1.0 - ADAM_B2) * (g * g)
    m_hat = nm / (1.0 - ADAM_B1 ** ADAM_STEP)
    v_hat = nv / (1.0 - ADAM_B2 ** ADAM_STEP)
    return -ADAM_LR * (m_hat / (jnp.sqrt(v_hat) + ADAM_EPS) + ADAM_WD * w), nm, nv


def _pair_reduce(from_sibling, mine, name):
    _, rows, cols = from_sibling.shape
    tr = _pick(rows, (176, 128, 64, 32, 16, 8))
    tiles = rows // tr
    table = jnp.bitwise_xor(_me(), jnp.arange(0, N_DEV, 2, dtype=jnp.int32))

    def body(tbl_ref, sib_ref, mine_ref, own_ref, send_ref):
        r = pl.program_id(1)
        total = mine_ref[...] + sib_ref[0].astype(F32)
        send_ref[0] = jnp.where(r == 0, 0.0, total).astype(BF16)

        @pl.when(r == 0)
        def _():
            own_ref[...] = total

    grid_spec = pltpu.PrefetchScalarGridSpec(
        num_scalar_prefetch=1, grid=(tiles, 4),
        in_specs=[pl.BlockSpec((1, tr, cols), lambda i, r, tbl: (r, i, 0)),
                  pl.BlockSpec((tr, cols), lambda i, r, tbl: (tbl[r] * tiles + i, 0))],
        out_specs=[pl.BlockSpec((tr, cols), lambda i, r, tbl: (i, 0)),
                   pl.BlockSpec((1, tr, cols), lambda i, r, tbl: (r, i, 0))])
    return pl.pallas_call(
        body, name=name, grid_spec=grid_spec,
        out_shape=[jax.ShapeDtypeStruct((rows, cols), F32), jax.ShapeDtypeStruct((4, rows, cols), BF16)],
        compiler_params=_cp("arbitrary", "arbitrary"),
    )(table, from_sibling, mine)


def _chip_sum_adamw(from_chips, own, w, m, v, name):
    _, rows, cols = from_chips.shape
    tr = _pick(rows, (176, 128, 64, 32, 16, 8))

    def body(recv_ref, own_ref, w_ref, m_ref, v_ref, g_ref, d_ref, nm_ref, nv_ref):
        g = own_ref[...]
        for r in range(1, 4):
            g = g + recv_ref[r].astype(F32)
        g_ref[...] = g
        d_ref[...], nm_ref[...], nv_ref[...] = _adamw_math(w_ref[...], g, m_ref[...], v_ref[...])

    spec = pl.BlockSpec((tr, cols), lambda i: (i, 0))
    shp = jax.ShapeDtypeStruct((rows, cols), F32)
    return pl.pallas_call(
        body, name=name, grid=(rows // tr,),
        in_specs=[pl.BlockSpec((4, tr, cols), lambda i: (0, i, 0)), spec, spec, spec, spec],
        out_specs=[spec, spec, spec, spec], out_shape=[shp, shp, shp, shp],
        compiler_params=_cp("parallel"),
    )(from_chips, own, w, m, v)


def _sum_shards_adamw(recvs, own, w, m, v, name, comm=None):
    rows_p, cols = recvs[0].shape[1], recvs[0].shape[2]
    n_p = len(recvs)
    tr = _pick(rows_p, (176, 128, 64, 32, 16, 8))
    tiles = rows_p // tr

    def body(*refs):
        recv_refs = refs[:n_p]
        own_ref, w_ref, m_ref, v_ref, g_ref, d_ref, nm_ref, nv_ref = refs[n_p:]
        for j in range(n_p):
            @pl.when(pl.program_id(0) == j)
            def _():
                g = _slot_sum(recv_refs[j], own_ref, (tr, cols))
                g_ref[...] = g
                d_ref[...], nm_ref[...], nv_ref[...] = _adamw_math(w_ref[...], g, m_ref[...], v_ref[...])

    spec = pl.BlockSpec((tr, cols), lambda p_, i: (p_ * tiles + i, 0))
    own_spec = pl.BlockSpec((tr, cols), lambda p_, i: (_me() * (n_p * tiles) + p_ * tiles + i, 0))
    shp = jax.ShapeDtypeStruct((rows_p * n_p, cols), F32)
    outs, couts = _call(
        body, name=name, grid=(n_p, tiles), ins=[*recvs, own, w, m, v],
        in_specs=[pl.BlockSpec((N_DEV, tr, cols), functools.partial(lambda p_, i, j: (0, jnp.where(p_ == j, i, 0), 0), j=j))
                  for j in range(n_p)] + [own_spec, spec, spec, spec],
        out_specs=[spec, spec, spec, spec], out_shape=[shp, shp, shp, shp],
        sem=("arbitrary", "arbitrary"), comm=comm)
    return outs if comm is None else (outs, couts)


def _mm(a, b, *, name, ta=False, tb=False, out_dtype=F32, out_dtype2=None, bias=None, addend=None,
        addend_scale=1.0, tm=1024, tn=1024, tk=1024, comm=None, out_rows=None, first_row=0, into=None):
    kdim, m = a.shape if ta else a.shape[::-1]
    n = b.shape[0] if tb else b.shape[1]
    tm = _pick(m, (tm, 1408, 1024, 768, 512, 256, 128))
    tn = _pick(n, (tn, 1408, 1024, 768, 512, 256, 128))
    tk = _pick(kdim, (tk, 1408, 1024, 768, 512, 256, 128))
    nk = kdim // tk
    a_spec = pl.BlockSpec((tk, tm), lambda i, j, k: (k, i)) if ta else pl.BlockSpec((tm, tk), lambda i, j, k: (i, k))
    b_spec = pl.BlockSpec((tn, tk), lambda i, j, k: (j, k)) if tb else pl.BlockSpec((tk, tn), lambda i, j, k: (k, j))
    ins, specs = [a, b], [a_spec, b_spec]
    if bias is not None:
        ins.append(bias)
        specs.append(pl.BlockSpec((1, tn), lambda i, j, k: (0, j)))
    if addend is not None:
        ins.append(addend)
        specs.append(pl.BlockSpec((tm, tn), lambda i, j, k: (i, j)))
    dims = (((0,) if ta else (1,), (1,) if tb else (0,)), ((), ()))
    has_bias, has_addend, two = bias is not None, addend is not None, out_dtype2 is not None

    def body(*refs):
        a_ref, b_ref = refs[0], refs[1]
        pos = 2
        bias_ref = addend_ref = None
        if has_bias:
            bias_ref = refs[pos]
            pos += 1
        if has_addend:
            addend_ref = refs[pos]
            pos += 1
        o_refs, acc_ref = refs[pos:-1], refs[-1]
        k = pl.program_id(2)

        @pl.when(k == 0)
        def _():
            acc_ref[...] = jnp.zeros_like(acc_ref)

        acc_ref[...] += _dot(a_ref[...], b_ref[...], dims)

        @pl.when(k == nk - 1)
        def _():
            r = acc_ref[...]
            if has_bias:
                r = r + bias_ref[...]
            if has_addend:
                r = r + addend_scale * addend_ref[...].astype(F32)
            for o_ref in o_refs:
                o_ref[...] = r.astype(o_ref.dtype)

    blk0 = first_row // tm
    dtypes = [out_dtype] + ([out_dtype2] if two else [])
    ospec = pl.BlockSpec((tm, tn), lambda i, j, k: (i + blk0, j))
    shapes = [jax.ShapeDtypeStruct((m if out_rows is None else out_rows, n), d) for d in dtypes]
    if into is not None:
        n_in = len(ins)
        outs = pl.pallas_call(
            lambda *refs: body(*refs[:n_in], *refs[n_in + len(into):]), name=name, grid=(m // tm, n // tn, nk),
            in_specs=specs + [pl.BlockSpec(memory_space=pl.ANY)] * len(into), out_specs=[ospec] * len(dtypes),
            out_shape=shapes, scratch_shapes=[pltpu.VMEM((tm, tn), F32)],
            input_output_aliases={n_in + j: j for j in range(len(into))},
            compiler_params=_cp("parallel", "parallel", "arbitrary"))(*ins, *into)
        return tuple(outs) if two else outs[0]
    outs, couts = _call(
        body, name=name, grid=(m // tm, n // tn, nk), ins=ins, in_specs=specs,
        out_specs=[ospec] * len(dtypes), out_shape=shapes,
        scratch_shapes=[pltpu.VMEM((tm, tn), F32)], sem=("parallel", "parallel", "arbitrary"), comm=comm)
    primary = tuple(outs) if two else outs[0]
    return (primary, couts) if comm is not None else primary


def _rope_lane_constants():
    inv_freq = np.float32(ROPE_THETA) ** (-np.arange(8, dtype=np.float32) * np.float32(2.0 / 16.0))
    lane = np.arange(LANES) % 64
    freq = np.where(lane < 16, inv_freq[lane % 8], 0.0).astype(np.float32)
    sign = np.where(lane < 8, -1.0, np.where(lane < 16, 1.0, 0.0)).astype(np.float32)
    return jnp.asarray(freq)[None, :], jnp.asarray(sign)[None, :]


def _prep(pos_col, x2, name, comm):
    t, d = x2.shape
    tr = _pick(t, (512, 256, 128))
    freq, sign = _rope_lane_constants()

    def body(pos_ref, freq_ref, sign_ref, x_ref, c_ref, s_ref, xb_ref):
        ang = pos_ref[...].astype(F32) * freq_ref[...]
        c_ref[...] = jnp.cos(ang)
        s_ref[...] = sign_ref[...] * jnp.sin(ang)
        xb_ref[...] = x_ref[...].astype(BF16)

    tab = pl.BlockSpec((tr, LANES), lambda i: (i, 0))
    return _call(
        body, name=name, grid=(t // tr,), ins=[pos_col, freq, sign, x2],
        in_specs=[pl.BlockSpec((tr, 1), lambda i: (i, 0)), pl.BlockSpec((1, LANES), lambda i: (0, 0)),
                  pl.BlockSpec((1, LANES), lambda i: (0, 0)), pl.BlockSpec((tr, d), lambda i: (i, 0))],
        out_specs=[tab, tab, pl.BlockSpec((tr, d), lambda i: (i, 0))],
        out_shape=[jax.ShapeDtypeStruct((t, LANES), F32), jax.ShapeDtypeStruct((t, LANES), F32),
                   jax.ShapeDtypeStruct((t, d), BF16)],
        sem=("parallel",), comm=comm)


def _swap8(t):
    width = t.shape[1]
    lane = jnp.bitwise_and(lax.broadcasted_iota(jnp.int32, t.shape, 1), 63)
    return jnp.where(lane < 8, pltpu.roll(t, width - 8, 1), jnp.where(lane < 16, pltpu.roll(t, 8, 1), 0.0))


def _rope(t, c, s):
    return t * c + _swap8(t) * s


def _rope_bwd(d, c, s):
    return d * c + _swap8(d * s)


def _tile4(a):
    return jnp.concatenate([a, a, a, a], axis=1)


def _attn_band(n, k_cur, k_prev, v_cur, v_prev, c_cur, s_cur, c_prev, s_prev):
    kband = jnp.concatenate([_rope(k_prev, c_prev, s_prev), _rope(k_cur, c_cur, s_cur)], axis=0)
    vband = jnp.concatenate([v_prev, v_cur], axis=0)
    qi = lax.broadcasted_iota(jnp.int32, (ATTN_BLOCK, 2 * ATTN_BLOCK), 0)
    kj = lax.broadcasted_iota(jnp.int32, (ATTN_BLOCK, 2 * ATTN_BLOCK), 1)
    dist = qi + ATTN_BLOCK - kj
    valid = (dist >= 0) & (dist < ATTN_BLOCK) & (n * ATTN_BLOCK - ATTN_BLOCK + kj >= 0)
    return (kband.astype(BF16), pltpu.roll(kband, 64, 1).astype(BF16),
            vband.astype(BF16), pltpu.roll(vband, 64, 1).astype(BF16), valid, kband)


def _attn_probs(raw, valid, sink, axis):
    s = jnp.where(valid, raw * ATTN_SCALE, NEG_BIG)
    m = jnp.maximum(jnp.max(s, axis=axis, keepdims=True), sink)
    p = jnp.exp(s - m)
    esink = jnp.exp(sink - m)
    z = jnp.sum(p, axis=axis, keepdims=True) + esink
    return p / z, esink / z


def _attn_valid_t(n):
    kj = lax.broadcasted_iota(jnp.int32, (2 * ATTN_BLOCK, ATTN_BLOCK), 0)
    qi = lax.broadcasted_iota(jnp.int32, (2 * ATTN_BLOCK, ATTN_BLOCK), 1)
    dist = qi + ATTN_BLOCK - kj
    return (dist >= 0) & (dist < ATTN_BLOCK) & (n * ATTN_BLOCK - ATTN_BLOCK + kj >= 0)


def _attn_specs(nb):
    def cur(col, width=KV_W):
        return pl.BlockSpec((ATTN_BLOCK, width), lambda n: (jnp.minimum(n, nb - 1), col))

    def prev(col):
        return pl.BlockSpec((ATTN_BLOCK, KV_W), lambda n: (jnp.maximum(n - 1, 0), col))

    ua_specs = [cur(0, ATTN_W), cur(4), prev(4), cur(5), prev(5)]
    tab_specs = [cur(0), cur(0), prev(0), prev(0)]
    return ua_specs, tab_specs


def _attn_fwd(ua, ctab, stab, sinks, name, comm=None):
    t = ua.shape[0]
    nb = t // ATTN_BLOCK
    ua_specs, tab_specs = _attn_specs(nb)

    def body(q_ref, kc_ref, kp_ref, vc_ref, vp_ref, cc_ref, sc_ref, cp_ref, sp_ref, sink_ref, o_ref, o_t_ref):
        n = pl.program_id(0)
        cc, sc = cc_ref[...], sc_ref[...]
        kb, kb_r, vb, vb_r, valid, _ = _attn_band(n, kc_ref[...], kp_ref[...], vc_ref[...], vp_ref[...],
                                                  cc, sc, cp_ref[...], sp_ref[...])
        qr = _rope(q_ref[...], _tile4(cc), _tile4(sc))
        lo = lax.broadcasted_iota(jnp.int32, (ATTN_BLOCK, LANES), 1) < 64
        heads = []
        for j in range(4):
            qj = qr[:, j * LANES:(j + 1) * LANES]
            for is_lo in (True, False):
                aligned = is_lo == (j < 2)
                qm = jnp.where(lo if is_lo else jnp.logical_not(lo), qj, 0.0).astype(BF16)
                raw = lax.dot_general(qm, kb if aligned else kb_r, _NT, preferred_element_type=F32)
                heads.append((raw, vb if aligned else vb_r, sink_ref[0, len(heads)]))
        halves = []
        for raw, vv, sink in heads:
            probs, _ = _attn_probs(raw, valid, sink, 1)
            halves.append(lax.dot_general(probs.astype(BF16), vv, _NN, preferred_element_type=F32))
        outs = [jnp.where(lo, halves[2 * j], halves[2 * j + 1]) for j in range(4)]
        o_ref[...] = jnp.concatenate(outs, axis=1).astype(o_ref.dtype)
        for j in range(4):
            o_t_ref[j * LANES:(j + 1) * LANES, :] = outs[j].T.astype(o_t_ref.dtype)

    return _call(
        body, name=name, grid=(nb,), ins=[ua, ua, ua, ua, ua, ctab, stab, ctab, stab, sinks],
        in_specs=ua_specs + tab_specs + [pl.BlockSpec(memory_space=pltpu.SMEM)],
        out_specs=[pl.BlockSpec((ATTN_BLOCK, ATTN_W), lambda n: (n, 0)),
                   pl.BlockSpec((ATTN_W, ATTN_BLOCK), lambda n: (0, n))],
        out_shape=[jax.ShapeDtypeStruct((t, ATTN_W), BF16), jax.ShapeDtypeStruct((ATTN_W, t), BF16)],
        sem=("parallel",), comm=comm)


def _attn_bwd(ua, d_out, ctab, stab, sinks, name, comm=None):
    t = ua.shape[0]
    nb = t // ATTN_BLOCK
    ua_specs, tab_specs = _attn_specs(nb)

    def body(q_ref, kc_ref, kp_ref, vc_ref, vp_ref, cc_ref, sc_ref, cp_ref, sp_ref, do_ref, sink_ref,
             dua_ref, dua_t_ref, dbias_ref, dsink_ref, dq_c, dk_c, dv_c, dq_n, dk_n, dv_n):
        n = pl.program_id(0)

        @pl.when(n == 0)
        def _():
            dq_c[...] = jnp.zeros_like(dq_c)
            dk_c[...] = jnp.zeros_like(dk_c)
            dv_c[...] = jnp.zeros_like(dv_c)
            dbias_ref[...] = jnp.zeros_like(dbias_ref)
            dsink_ref[...] = jnp.zeros_like(dsink_ref)

        @pl.when(n == nb)
        def _():
            dq_n[...] = jnp.zeros_like(dq_n)
            dk_n[...] = jnp.zeros_like(dk_n)
            dv_n[...] = jnp.zeros_like(dv_n)

        @pl.when(n < nb)
        def _():
            cc, sc = cc_ref[...], sc_ref[...]
            kb, kb_r, vb, vb_r, _, kb_f32 = _attn_band(n, kc_ref[...], kp_ref[...], vc_ref[...], vp_ref[...],
                                                       cc, sc, cp_ref[...], sp_ref[...])
            valid_t = _attn_valid_t(n)
            c4, s4 = _tile4(cc), _tile4(sc)
            qr = _rope(q_ref[...], c4, s4)
            do = do_ref[...].astype(F32)
            lane = lax.broadcasted_iota(jnp.int32, (ATTN_BLOCK, LANES), 1)
            lo = lane < 64
            lane_row = lax.broadcasted_iota(jnp.int32, (1, LANES), 1)
            k_t = {False: kb_f32.T.astype(BF16), True: pltpu.roll(kb_f32, 64, 1).T.astype(BF16)}
            heads = []
            for j in range(4):
                qj = qr[:, j * LANES:(j + 1) * LANES]
                doj = do[:, j * LANES:(j + 1) * LANES]
                for is_lo in (True, False):
                    aligned = is_lo == (j < 2)
                    msk = lo if is_lo else jnp.logical_not(lo)
                    kk = kb if aligned else kb_r
                    vv = vb if aligned else vb_r
                    qm = jnp.where(msk, qj, 0.0).astype(BF16)
                    dom = jnp.where(msk, doj, 0.0).astype(BF16)
                    heads.append(dict(
                        aligned=aligned, qm=qm, dom=dom, sink=sink_ref[0, len(heads)],
                        raw_t=lax.dot_general(kk, qm, _NT, preferred_element_type=F32),
                        dp_t=lax.dot_general(vv, dom, _NT, preferred_element_type=F32)))
            dk_band = jnp.zeros((2 * ATTN_BLOCK, LANES), F32)
            dv_band = jnp.zeros((2 * ATTN_BLOCK, LANES), F32)
            dsink = jnp.zeros((1, LANES), F32)
            for head, hd in enumerate(heads):
                probs_t, psink = _attn_probs(hd["raw_t"], valid_t, hd["sink"], 0)
                delta_t = jnp.sum(probs_t * hd["dp_t"], axis=0, keepdims=True)
                hd["ds_t"] = (probs_t * (hd["dp_t"] - delta_t) * ATTN_SCALE).astype(BF16)
                dsink = dsink + jnp.where(lane_row == head, -jnp.sum(psink * delta_t), 0.0)
                dk_h = lax.dot_general(hd["ds_t"], hd["qm"], _NN, preferred_element_type=F32)
                dv_h = lax.dot_general(probs_t.astype(BF16), hd["dom"], _NN, preferred_element_type=F32)
                if not hd["aligned"]:
                    dk_h = pltpu.roll(dk_h, 64, 1)
                    dv_h = pltpu.roll(dv_h, 64, 1)
                dk_band = dk_band + dk_h
                dv_band = dv_band + dv_h
            row_lo = lax.broadcasted_iota(jnp.int32, (LANES, ATTN_BLOCK), 0) < 64
            dq_t = [lax.dot_general(k_t[not hd["aligned"]], hd["ds_t"], _NN, preferred_element_type=F32)
                    for hd in heads]
            dqs = [jnp.where(row_lo, dq_t[2 * j], dq_t[2 * j + 1]).T for j in range(4)]
            dq_n[...] = _rope_bwd(jnp.concatenate(dqs, axis=1), c4, s4)
            dk_n[...] = dk_band
            dv_n[...] = dv_band
            dsink_ref[...] += dsink

        dk_prev = _rope_bwd(dk_c[...] + dk_n[0:ATTN_BLOCK, :], cp_ref[...], sp_ref[...])
        dv_prev = dv_c[...] + dv_n[0:ATTN_BLOCK, :]
        full = jnp.concatenate([dq_c[...], dk_prev, dv_prev], axis=1)
        dua_ref[...] = full.astype(dua_ref.dtype)
        for j in range(UA_W // LANES):
            dua_t_ref[j * LANES:(j + 1) * LANES, :] = full[:, j * LANES:(j + 1) * LANES].T.astype(dua_t_ref.dtype)
        dbias_ref[...] += jnp.sum(full, axis=0, keepdims=True)
        dq_c[...] = dq_n[...]
        dk_c[...] = dk_n[ATTN_BLOCK:, :]
        dv_c[...] = dv_n[ATTN_BLOCK:, :]

    return _call(
        body, name=name, grid=(nb + 1,), ins=[ua, ua, ua, ua, ua, ctab, stab, ctab, stab, d_out, sinks],
        in_specs=ua_specs + tab_specs + [
            pl.BlockSpec((ATTN_BLOCK, ATTN_W), lambda n: (jnp.minimum(n, nb - 1), 0)),
            pl.BlockSpec(memory_space=pltpu.SMEM)],
        out_specs=[pl.BlockSpec((ATTN_BLOCK, UA_W), lambda n: (jnp.maximum(n - 1, 0), 0)),
                   pl.BlockSpec((UA_W, ATTN_BLOCK), lambda n: (0, jnp.maximum(n - 1, 0))),
                   pl.BlockSpec((1, UA_W), lambda n: (0, 0)),
                   pl.BlockSpec((1, LANES), lambda n: (0, 0))],
        out_shape=[jax.ShapeDtypeStruct((t, UA_W), BF16), jax.ShapeDtypeStruct((UA_W, t), BF16),
                   jax.ShapeDtypeStruct((1, UA_W), F32),
                   jax.ShapeDtypeStruct((1, LANES), F32)],
        scratch_shapes=[pltpu.VMEM((ATTN_BLOCK, ATTN_W), F32), pltpu.VMEM((ATTN_BLOCK, KV_W), F32),
                        pltpu.VMEM((ATTN_BLOCK, KV_W), F32), pltpu.VMEM((ATTN_BLOCK, ATTN_W), F32),
                        pltpu.VMEM((2 * ATTN_BLOCK, KV_W), F32), pltpu.VMEM((2 * ATTN_BLOCK, KV_W), F32)],
        sem=("arbitrary",), comm=comm)


def _tri_mats():
    r = lax.broadcasted_iota(jnp.int32, (HGRN_CHUNK, LANES), 0)
    c = lax.broadcasted_iota(jnp.int32, (HGRN_CHUNK, LANES), 1)
    lower = ((c <= r) & (c < HGRN_CHUNK)).astype(F32)
    upper = ((c >= r) & (c < HGRN_CHUNK)).astype(F32)
    return lower, upper


def _tri_apply(tri, g):
    pad = jnp.concatenate([g, jnp.zeros_like(g)], axis=0)
    return lax.dot_general(tri, pad, _NN, precision=lax.Precision.HIGHEST, preferred_element_type=F32)


def _sub_masks():
    s = lax.broadcasted_iota(jnp.int32, (HGRN_CHUNK, LANES), 0)
    tt = lax.broadcasted_iota(jnp.int32, (HGRN_CHUNK, LANES), 1)
    return [(tt >= HGRN_SUB * i) & (tt < HGRN_SUB * (i + 1)) & (s <= tt) for i in range(HGRN_CHUNK // HGRN_SUB)]


def _hgrn_gates(hq, hf, lb_ref, b_scr):
    lb = _sig(lb_ref[0:1, :] - lb_ref[1:2, :])
    q = hq * _sig(hq)
    sg = _sig(hf)
    f = lb + (1.0 - lb) * sg
    k = 1.0 - f
    lower, _ = _tri_mats()
    b = _tri_apply(lower, jnp.log(f))
    b_scr[...] = b
    nsub = HGRN_CHUNK // HGRN_SUB
    starts = [jnp.zeros((1, HG_W), F32)] + [b_scr[HGRN_SUB * i - 1:HGRN_SUB * i, :] for i in range(1, nsub)]
    pq = jnp.concatenate([jnp.broadcast_to(p, (HGRN_SUB, HG_W)) for p in starts], axis=0)
    b_last = b_scr[HGRN_CHUNK - 1:HGRN_CHUNK, :]
    e_q = jnp.exp(b - pq)
    e_k = [jnp.exp(jnp.minimum(p - b, EXP_CLAMP)) for p in starts]
    e_b = jnp.exp(b)
    e_bl = jnp.exp(b_last - b)
    e_last = jnp.exp(b_last)
    return q, sg, f, k, lb, e_q, e_k, e_b, e_bl, e_last


def _sub_masks_ts():
    tt = lax.broadcasted_iota(jnp.int32, (HGRN_CHUNK, LANES), 0)
    s = lax.broadcasted_iota(jnp.int32, (HGRN_CHUNK, LANES), 1)
    return [(tt >= HGRN_SUB * i) & (tt < HGRN_SUB * (i + 1)) & (s <= tt) for i in range(HGRN_CHUNK // HGRN_SUB)]


def _masked_sum(blocks, masks, axis):
    step = HGRN_CHUNK if axis == 0 else LANES
    acc = jnp.zeros((HGRN_CHUNK, LANES), F32)
    for i, msk in enumerate(masks):
        blk = blocks[step * i:step * (i + 1), :] if axis == 0 else blocks[:, step * i:step * (i + 1)]
        acc = acc + jnp.where(msk, blk, 0.0)
    return acc


def _store_transposed(out_t_ref, chunk_rows):
    width = chunk_rows[0].shape[1]
    if len(chunk_rows) == 1:
        groups = [jnp.concatenate([chunk_rows[0], jnp.zeros_like(chunk_rows[0])], axis=0)]
    else:
        groups = [jnp.concatenate(chunk_rows[g:g + 2], axis=0) for g in range(0, len(chunk_rows), 2)]
    for g, rows in enumerate(groups):
        for c in range(width // LANES):
            tile = rows[:, c * LANES:(c + 1) * LANES].T.astype(out_t_ref.dtype)
            if len(chunk_rows) == 1:
                out_t_ref[c * LANES:(c + 1) * LANES, :] = tile[:, 0:HGRN_CHUNK]
            else:
                out_t_ref[c * LANES:(c + 1) * LANES, g * LANES:(g + 1) * LANES] = tile


def _hgrn_chunk_inputs(j, hq_ref, hf_ref, hi_ref, hg_ref, lb_ref, b_scr):
    rows = slice(j * HGRN_CHUNK, (j + 1) * HGRN_CHUNK)
    hq, hf, v, hg = hq_ref[rows, :], hf_ref[rows, :], hi_ref[rows, :], hg_ref[rows, :]
    q, sg, f, k, lb, e_q, e_k, e_b, e_bl, e_last = _hgrn_gates(hq, hf, lb_ref, b_scr.at[j])
    return dict(rows=rows, hq=hq, v=v, hg=hg, q=q, sg=sg, f=f, k=k, lb=lb, e_q=e_q, e_k=e_k, e_b=e_b, e_bl=e_bl,
                e_last=e_last, qt=q * e_q, qb=q * e_b, kd=k * e_bl, khat=[k * e for e in e_k])


def _hgrn_fwd(uh, lb_raw, norm_g, name, comm=None):
    t = uh.shape[0]
    nc = t // HGRN_CHUNK
    cps = _pick(nc, (HGRN_CHUNKS_PER_STEP, 2, 1))
    rows_step = cps * HGRN_CHUNK

    def body(hq_ref, hf_ref, hi_ref, hg_ref, lb_ref, ng_ref, r_ref, r_t_ref, o_ref, st_out_ref, st_ref, b_scr):
        @pl.when(pl.program_id(0) == 0)
        def _():
            st_ref[...] = jnp.zeros_like(st_ref)

        masks = _sub_masks_ts()
        ng = ng_ref[...]
        zpad = jnp.zeros((HGRN_CHUNK, LANES), F32)
        heads = [slice(h * LANES, (h + 1) * LANES) for h in range(4)]
        chunks = [_hgrn_chunk_inputs(j, hq_ref, hf_ref, hi_ref, hg_ref, lb_ref, b_scr) for j in range(cps)]
        for ch in chunks:
            ch["scores"] = [_dot3(ch["qt"][:, sl],
                                  jnp.concatenate([x for kh in ch["khat"] for x in (kh[:, sl], zpad)], axis=0), _NT)
                            for sl in heads]
        for j, ch in enumerate(chunks):
            o_heads, y_heads = [], []
            for h, sl in enumerate(heads):
                a_ts = _masked_sum(ch["scores"][h], masks, 1)
                vh = ch["v"][:, sl].astype(BF16)
                v_pad = jnp.concatenate([vh, jnp.zeros_like(vh)], axis=0)
                o_intra = lax.dot_general(a_ts.astype(BF16), v_pad, _NN, preferred_element_type=F32)
                st = st_ref[h]
                st_out_ref[j, h] = st
                o_inter = _dot(ch["qb"][:, sl], st, _NT)
                st_ref[h] = st * ch["e_last"][:, sl] + _dot(vh, ch["kd"][:, sl], _TN)
                oh = o_intra + o_inter
                rs = lax.rsqrt(jnp.mean(oh * oh, axis=1, keepdims=True) + RMS_EPS)
                o_heads.append(oh)
                y_heads.append(oh * rs * ng)
            hg = ch["hg"]
            o_ref[ch["rows"], :] = jnp.concatenate(o_heads, axis=1)
            ch["r"] = jnp.concatenate(y_heads, axis=1) * (hg * _sig(hg))
            r_ref[ch["rows"], :] = ch["r"].astype(r_ref.dtype)
        _store_transposed(r_t_ref, [ch["r"] for ch in chunks])

    col = lambda j: pl.BlockSpec((rows_step, HG_W), lambda c: (c, j))
    return _call(
        body, name=name, grid=(nc // cps,), ins=[uh, uh, uh, uh, lb_raw, norm_g],
        in_specs=[col(0), col(1), col(2), col(3),
                  pl.BlockSpec((2, HG_W), lambda c: (0, 0)), pl.BlockSpec((1, LANES), lambda c: (0, 0))],
        out_specs=[pl.BlockSpec((rows_step, HG_W), lambda c: (c, 0)),
                   pl.BlockSpec((HG_W, rows_step), lambda c: (0, c)),
                   pl.BlockSpec((rows_step, HG_W), lambda c: (c, 0)),
                   pl.BlockSpec((cps, 4, LANES, LANES), lambda c: (c, 0, 0, 0))],
        out_shape=[jax.ShapeDtypeStruct((t, HG_W), BF16), jax.ShapeDtypeStruct((HG_W, t), BF16),
                   jax.ShapeDtypeStruct((t, HG_W), F32), jax.ShapeDtypeStruct((nc, 4, LANES, LANES), F32)],
        scratch_shapes=[pltpu.VMEM((4, LANES, LANES), F32), pltpu.VMEM((cps, HGRN_CHUNK, HG_W), F32)],
        sem=("arbitrary",), comm=comm)


def _hgrn_bwd(uh, o_pre, d_r, states, lb_raw, norm_g, name, comm=None):
    t = uh.shape[0]
    nc = t // HGRN_CHUNK
    cps = _pick(nc, (HGRN_CHUNKS_PER_STEP, 2, 1))
    ns = nc // cps
    rows_step = cps * HGRN_CHUNK
    nsub = HGRN_CHUNK // HGRN_SUB

    def body(hq_ref, hf_ref, hi_ref, hg_ref, o_ref, dr_ref, st_in_ref, lb_ref, ng_ref,
             duh_ref, duh_t_ref, dbias_ref, dng_ref, dlb_ref, dst_ref, b_scr, dlb_acc):
        i = pl.program_id(0)

        @pl.when(i == 0)
        def _():
            dst_ref[...] = jnp.zeros_like(dst_ref)
            dbias_ref[...] = jnp.zeros_like(dbias_ref)
            dng_ref[...] = jnp.zeros_like(dng_ref)
            dlb_acc[...] = jnp.zeros_like(dlb_acc)

        masks_st = _sub_masks()
        masks_ts = _sub_masks_ts()
        ng = ng_ref[...]
        zpad = jnp.zeros((HGRN_CHUNK, LANES), F32)
        _, upper = _tri_mats()
        heads = [slice(h * LANES, (h + 1) * LANES) for h in range(4)]
        row = lax.broadcasted_iota(jnp.int32, (HGRN_CHUNK, HG_W), 0)

        chunks = [_hgrn_chunk_inputs(j, hq_ref, hf_ref, hi_ref, hg_ref, lb_ref, b_scr) for j in range(cps)]
        dng = jnp.zeros((1, LANES), F32)
        for ch in chunks:
            o = o_ref[ch["rows"], :]
            dr = dr_ref[ch["rows"], :].astype(F32)
            hg = ch["hg"]
            sgg = _sig(hg)
            dy = dr * (hg * sgg)
            do_h, y_h = [], []
            for sl in heads:
                oh = o[:, sl]
                rs = lax.rsqrt(jnp.mean(oh * oh, axis=1, keepdims=True) + RMS_EPS)
                y_h.append(oh * rs * ng)
                dng = dng + jnp.sum(dy[:, sl] * oh * rs, axis=0, keepdims=True)
                w = dy[:, sl] * ng
                do_h.append(rs * (w - oh * (rs * rs) * jnp.mean(w * oh, axis=1, keepdims=True)))
            ch["do"] = do_h
            ch["dhg"] = dr * jnp.concatenate(y_h, axis=1) * _dsilu(hg, sgg)

        for ch in chunks:
            ch["kst"], ch["kpad"], ch["qt_pad"], ch["v_b"], ch["do_pad"] = [], [], [], [], []
            ch["ats"], ch["d_at"], ch["d_a"] = [], [], []
            for h, sl in enumerate(heads):
                kst = jnp.concatenate([kh[:, sl] for kh in ch["khat"]], axis=0)
                kpad = jnp.concatenate([x for kh in ch["khat"] for x in (kh[:, sl], zpad)], axis=0)
                qt_pad = jnp.concatenate([ch["qt"][:, sl], zpad], axis=0)
                vh = ch["v"][:, sl].astype(BF16)
                v_pad = jnp.concatenate([vh, jnp.zeros_like(vh)], axis=0)
                do_b = ch["do"][h].astype(BF16)
                do_pad = jnp.concatenate([do_b, jnp.zeros_like(do_b)], axis=0)
                ch["kst"].append(kst)
                ch["kpad"].append(kpad)
                ch["qt_pad"].append(qt_pad)
                ch["v_b"].append(vh)
                ch["do_pad"].append(do_pad)
                ch["ats"].append(_dot3(kst, qt_pad, _NT))
                ch["d_at"].append(lax.dot_general(vh, do_pad, _NT, preferred_element_type=F32))
                ch["d_a"].append(lax.dot_general(do_b, v_pad, _NT, preferred_element_type=F32))

        for ch in chunks:
            ch["d_kst"], ch["d_qt"], ch["dv"] = [], [], []
            for h in range(4):
                at = _masked_sum(ch["ats"][h], masks_st, 0)
                d_ats = jnp.concatenate([jnp.where(m, ch["d_at"][h], 0.0) for m in masks_st], axis=0)
                d_a_cat = jnp.concatenate([jnp.where(m, ch["d_a"][h], 0.0) for m in masks_ts], axis=1)
                ch["d_kst"].append(_dot3(d_ats, ch["qt_pad"][h], _NN))
                ch["d_qt"].append(_dot3(d_a_cat, ch["kpad"][h], _NN))
                ch["dv"].append(lax.dot_general(at.astype(BF16), ch["do_pad"][h], _NN, preferred_element_type=F32))

        for j in reversed(range(cps)):
            ch = chunks[j]
            q, k, sg, f, lb = ch["q"], ch["k"], ch["sg"], ch["f"], ch["lb"]
            dq_h, dk_h, dv_h, extra_h = [], [], [], []
            for h, sl in enumerate(heads):
                st_prev = st_in_ref[j, h]
                d_st = dst_ref[h]
                d_st_b = d_st.astype(BF16)
                do_b = ch["do_pad"][h][0:HGRN_CHUNK, :]
                kd, e_last = ch["kd"][:, sl], ch["e_last"][:, sl]
                dv = ch["dv"][h] + _dot(kd, d_st_b, _NT)
                d_qb = _dot(do_b, st_prev, _NN)
                d_kd = lax.dot_general(ch["v_b"][h], d_st_b, _NN, preferred_element_type=F32)
                extra_h.append(jnp.sum(st_prev * d_st, axis=0, keepdims=True) * e_last
                               + jnp.sum(kd * d_kd, axis=0, keepdims=True))
                dst_ref[h] = d_st * e_last + _dot(do_b, ch["qb"][:, sl], _TN)
                dq_h.append(ch["d_qt"][h] * ch["e_q"][:, sl] + d_qb * ch["e_b"][:, sl])
                dkk = d_kd * ch["e_bl"][:, sl]
                for s_ in range(nsub):
                    dkk = dkk + ch["d_kst"][h][HGRN_CHUNK * s_:HGRN_CHUNK * (s_ + 1), :] * ch["e_k"][s_][:, sl]
                dk_h.append(dkk)
                dv_h.append(dv)
            dq = jnp.concatenate(dq_h, axis=1)
            dk = jnp.concatenate(dk_h, axis=1)
            dv = jnp.concatenate(dv_h, axis=1)
            extra = jnp.concatenate(extra_h, axis=1)
            db = q * dq - k * dk + jnp.where(row == HGRN_CHUNK - 1, extra, 0.0)
            dg = _tri_apply(upper, db)
            df = dg / f - dk
            dhf = df * (1.0 - lb) * sg * (1.0 - sg)
            dhq = dq * _dsilu(ch["hq"], _sig(ch["hq"]))
            full = jnp.concatenate([dhq, dhf, dv, ch["dhg"]], axis=1)
            duh_ref[ch["rows"], :] = full.astype(duh_ref.dtype)
            ch["full"] = full
            dbias_ref[...] += jnp.sum(full, axis=0, keepdims=True)
            dlb_acc[...] += jnp.sum(df * (1.0 - sg), axis=0, keepdims=True)
        dng_ref[...] += dng
        _store_transposed(duh_t_ref, [ch["full"] for ch in chunks])

        @pl.when(i == ns - 1)
        def _():
            lb = chunks[0]["lb"]
            d_a0 = dlb_acc[...] * lb * (1.0 - lb)
            r8 = lax.broadcasted_iota(jnp.int32, (8, HG_W), 0)
            dlb_ref[...] = jnp.where(r8 == 0, d_a0, jnp.where(r8 == 1, -d_a0, 0.0))

    col = lambda j: pl.BlockSpec((rows_step, HG_W), lambda i: (ns - 1 - i, j))
    return _call(
        body, name=name, grid=(ns,), ins=[uh, uh, uh, uh, o_pre, d_r, states, lb_raw, norm_g],
        in_specs=[col(0), col(1), col(2), col(3), col(0), col(d_r.shape[1] // HG_W - 1),
                  pl.BlockSpec((cps, 4, LANES, LANES), lambda i: (ns - 1 - i, 0, 0, 0)),
                  pl.BlockSpec((2, HG_W), lambda i: (0, 0)), pl.BlockSpec((1, LANES), lambda i: (0, 0))],
        out_specs=[pl.BlockSpec((rows_step, UH_W), lambda i: (ns - 1 - i, 0)),
                   pl.BlockSpec((UH_W, rows_step), lambda i: (0, ns - 1 - i)),
                   pl.BlockSpec((1, UH_W), lambda i: (0, 0)),
                   pl.BlockSpec((1, LANES), lambda i: (0, 0)),
                   pl.BlockSpec((8, HG_W), lambda i: (0, 0))],
        out_shape=[jax.ShapeDtypeStruct((t, UH_W), BF16), jax.ShapeDtypeStruct((UH_W, t), BF16),
                   jax.ShapeDtypeStruct((1, UH_W), F32),
                   jax.ShapeDtypeStruct((1, LANES), F32), jax.ShapeDtypeStruct((8, HG_W), F32)],
        scratch_shapes=[pltpu.VMEM((4, LANES, LANES), F32), pltpu.VMEM((cps, HGRN_CHUNK, HG_W), F32),
                        pltpu.VMEM((1, HG_W), F32)],
        sem=("arbitrary",), comm=comm)


def _ln_bwd_math(dy, xhat, rstd, g):
    dxh = dy * g
    return rstd * (dxh - jnp.mean(dxh, axis=1, keepdims=True)
                   - xhat * jnp.mean(dxh * xhat, axis=1, keepdims=True))


def _mm_rows(a, b, extras, *, name, epilogue, out_shape, out_specs, tb=False, tm=512, tk=1408, pair2=None):
    m, kdim = a.shape
    n = b.shape[0] if tb else b.shape[1]
    tm = _pick(m, (tm, 256, 128))
    tk = _pick(kdim, (tk, 1408, 1024, 768, 512, 256, 128))
    nk = kdim // tk
    b_spec = pl.BlockSpec((n, tk), lambda i, k: (0, k)) if tb else pl.BlockSpec((tk, n), lambda i, k: (k, 0))
    dims = _NT if tb else _NN
    n_ex, n_out, n_p2 = len(extras), len(out_shape), (0 if pair2 is None else 2)

    def body(*refs):
        a_ref, b_ref = refs[0], refs[1]
        p2_refs = refs[2:2 + n_p2]
        ex_refs = refs[2 + n_p2:2 + n_p2 + n_ex]
        o_refs = refs[2 + n_p2 + n_ex:2 + n_p2 + n_ex + n_out]
        acc_ref = refs[-1]
        i, k = pl.program_id(0), pl.program_id(1)

        @pl.when(k == 0)
        def _():
            if n_p2:
                acc_ref[...] = _dot(p2_refs[0][...], p2_refs[1][...], _NN)
            else:
                acc_ref[...] = jnp.zeros_like(acc_ref)

        acc_ref[...] += _dot(a_ref[...], b_ref[...], dims)

        @pl.when(k == nk - 1)
        def _():
            epilogue(acc_ref[...], ex_refs, o_refs, i == 0)

    p2_specs, p2_ins = [], []
    if pair2 is not None:
        k2 = pair2[0].shape[1]
        p2_specs = [pl.BlockSpec((tm, k2), lambda i, k: (i, 0)), pl.BlockSpec((k2, n), lambda i, k: (0, 0))]
        p2_ins = list(pair2)
    return pl.pallas_call(
        body, name=name, grid=(m // tm, nk),
        in_specs=[pl.BlockSpec((tm, tk), lambda i, k: (i, k)), b_spec] + p2_specs + [sp for _, sp in extras],
        out_specs=list(out_specs), out_shape=list(out_shape),
        scratch_shapes=[pltpu.VMEM((tm, n), F32)],
        compiler_params=_cp("arbitrary", "arbitrary"),
    )(a, b, *p2_ins, *[arr for arr, _ in extras])


def _rows_specs(tm, d):
    row = pl.BlockSpec((tm, d), lambda i, k: (i, 0))
    vec = pl.BlockSpec((1, d), lambda i, k: (0, 0))
    col = pl.BlockSpec((tm, 1), lambda i, k: (i, 0))
    return row, vec, col


def _mm_ln_fwd(a, b, pair2, addend, g, beta, name, tm=512):
    t, d = addend.shape
    tm = _pick(t, (tm, 256, 128))
    row, vec, col = _rows_specs(tm, d)

    def epilogue(acc, ex, outs, first):
        z = acc + ALPHA * ex[0][...]
        mu = jnp.mean(z, axis=1, keepdims=True)
        zc = z - mu
        rstd = lax.rsqrt(jnp.mean(zc * zc, axis=1, keepdims=True) + LN_EPS)
        xhat = zc * rstd
        h = xhat * ex[1][...] + ex[2][...]
        outs[0][...] = h
        outs[1][...] = h.astype(BF16)
        outs[2][...] = xhat
        outs[3][...] = rstd

    return _mm_rows(a, b, [(addend, row), (g, vec), (beta, vec)], name=name, epilogue=epilogue, tm=tm, pair2=pair2,
                    out_shape=[jax.ShapeDtypeStruct((t, d), F32), jax.ShapeDtypeStruct((t, d), BF16),
                               jax.ShapeDtypeStruct((t, d), F32), jax.ShapeDtypeStruct((t, 1), F32)],
                    out_specs=[row, row, row, col])


CONV_RB = 32
HALO = 8


def _sum8(x):
    acc = x[0:8]
    for r in range(8, x.shape[0], 8):
        acc = acc + x[r:r + 8]
    return acc


FFN_TILE = 256
FFN_COLS = 256


def _rows_before(win, k):
    return pltpu.roll(win, k, 0)[HALO:]


def _rows_after(win, k):
    n = win.shape[0]
    return pltpu.roll(win, n - k, 0)[0:n - HALO]


def _resident(shape):
    return pl.BlockSpec(shape, lambda i: (0,) * len(shape), pipeline_mode=pl.Buffered(1))


def _ffn_fwd(h1b, h1, w_up_t, conv_w, conv_b, w_down, target, ln2_g, ln2_b, name, comm=None):
    t, d = h1.shape
    tr = _pick(t, (FFN_TILE, 128))
    nblk = D_FF // FFN_COLS
    rb = CONV_RB

    def body(a_ref, wup_ref, cw_ref, cb_ref, wd_ref, h1_ref, tgt_ref, g_ref, b_ref,
             u2_ref, hm_ref, dz_ref, dg_ref, db_ref, loss_ref, ext):
        i = pl.program_id(0)

        @pl.when(i == 0)
        def _():
            ext[0:HALO, :] = jnp.zeros((HALO, D_FF), F32)
            dg_ref[...] = jnp.zeros_like(dg_ref)
            db_ref[...] = jnp.zeros_like(db_ref)
            loss_ref[...] = jnp.zeros_like(loss_ref)

        a = a_ref[...]
        for c in range(nblk):
            cs = slice(c * FFN_COLS, (c + 1) * FFN_COLS)
            vs = slice(D_FF + c * FFN_COLS, D_FF + (c + 1) * FFN_COLS)
            gate_pre = lax.dot_general(a, wup_ref[cs, :], _NT, preferred_element_type=F32)
            u2_ref[:, cs] = gate_pre
            ext[HALO:, cs] = gate_pre
            u2_ref[:, vs] = lax.dot_general(a, wup_ref[vs, :], _NT, preferred_element_type=F32)
        acc = jnp.zeros((tr, d), F32)
        for c in range(nblk):
            cs = slice(c * FFN_COLS, (c + 1) * FFN_COLS)
            for sub in range(FFN_COLS // LANES):
                ln = slice(c * FFN_COLS + sub * LANES, c * FFN_COLS + (sub + 1) * LANES)
                vl = slice(D_FF + c * FFN_COLS + sub * LANES, D_FF + c * FFN_COLS + (sub + 1) * LANES)
                w0, w1, w2, bb = cw_ref[0:1, ln], cw_ref[1:2, ln], cw_ref[2:3, ln], cb_ref[:, ln]
                for r0 in range(0, tr, rb):
                    win = ext[r0:r0 + HALO + rb, ln]
                    gate = _rows_before(win, 2) * w0 + _rows_before(win, 1) * w1 + win[HALO:] * w2 + bb
                    hm_ref[r0:r0 + rb, ln] = (gate * _sig(gate) * u2_ref[r0:r0 + rb, vl]).astype(hm_ref.dtype)
            acc = acc + lax.dot_general(hm_ref[:, cs], wd_ref[cs, :], _NN, preferred_element_type=F32)
        ext[0:HALO, :] = ext[tr:tr + HALO, :]

        z = acc + ALPHA * h1_ref[...]
        gg = g_ref[...]
        mu = jnp.mean(z, axis=1, keepdims=True)
        zc = z - mu
        rstd = lax.rsqrt(jnp.mean(zc * zc, axis=1, keepdims=True) + LN_EPS)
        xhat = zc * rstd
        err = xhat * gg + b_ref[...] - tgt_ref[...]
        loss_ref[...] += 0.5 * jnp.sum(jnp.mean(err * err, axis=1, keepdims=True))
        dy = err * (1.0 / d)
        dz_ref[...] = _ln_bwd_math(dy, xhat, rstd, gg)
        dg_ref[...] += jnp.sum(dy * xhat, axis=0, keepdims=True)
        db_ref[...] += jnp.sum(dy, axis=0, keepdims=True)

    row = lambda w: pl.BlockSpec((tr, w), lambda i: (i, 0))
    vec = pl.BlockSpec((1, d), lambda i: (0, 0))
    return _call(
        body, name=name, grid=(t // tr,),
        ins=[h1b, w_up_t, conv_w, conv_b, w_down, h1, target, ln2_g, ln2_b],
        in_specs=[row(d), _resident((2 * D_FF, d)), _resident((3, D_FF)), _resident((1, D_FF)),
                  _resident((D_FF, d)), row(d), row(d), vec, vec],
        out_specs=[row(2 * D_FF), row(D_FF), row(d), vec, vec, pl.BlockSpec((1, LANES), lambda i: (0, 0))],
        out_shape=[jax.ShapeDtypeStruct((t, 2 * D_FF), F32), jax.ShapeDtypeStruct((t, D_FF), BF16),
                   jax.ShapeDtypeStruct((t, d), F32), jax.ShapeDtypeStruct((1, d), F32),
                   jax.ShapeDtypeStruct((1, d), F32), jax.ShapeDtypeStruct((1, LANES), F32)],
        scratch_shapes=[pltpu.VMEM((tr + HALO, D_FF), F32)],
        sem=("arbitrary",), comm=comm)


def _ffn_bwd(dz2, u2, w_down, w_up_t, conv_w, conv_b, xhat1, rstd1, ln1_g, name, comm=None):
    t, d = dz2.shape
    tr = _pick(t, (FFN_TILE, 128))
    nt = t // tr
    hb = tr // HALO
    nblk = D_FF // FFN_COLS
    rb = CONV_RB

    def body(dz2_ref, dz2_next_ref, u2_ref, gp_prev_ref, wd_ref, wup_ref, cw_ref, cb_ref, xhat_ref, rstd_ref,
             g1_ref, du_ref, dz1_ref, dw_ref, dcb_ref, dg1_ref, db1_ref, head, dh_s, dg_s):
        i = pl.program_id(0)

        @pl.when(i == 0)
        def _():
            dg_s[tr:, :] = jnp.zeros((HALO, D_FF), F32)
            dw_ref[...] = jnp.zeros_like(dw_ref)
            dcb_ref[...] = jnp.zeros_like(dcb_ref)
            dg1_ref[...] = jnp.zeros_like(dg1_ref)
            db1_ref[...] = jnp.zeros_like(db1_ref)

        dz2 = dz2_ref[...]

        @pl.when(i == 0)
        def _():
            dz2_b = dz2.astype(BF16)
            for c in range(nblk):
                cs = slice(c * FFN_COLS, (c + 1) * FFN_COLS)
                dh_s[:, cs] = lax.dot_general(dz2_b, wd_ref[cs, :], _NT, preferred_element_type=F32)

        dz2_next = dz2_next_ref[...].astype(BF16)
        dh_next = [lax.dot_general(dz2_next, wd_ref[c * FFN_COLS:(c + 1) * FFN_COLS, :], _NT,
                                   preferred_element_type=F32) for c in range(nblk)]
        head[0:HALO, :] = jnp.where(i == nt - 1, 0.0, gp_prev_ref[...])
        head[HALO:, :] = u2_ref[0:rb, 0:D_FF]

        acc = jnp.zeros((tr, d), F32)
        for blk in range(nblk):
            for c in range(blk * FFN_COLS // LANES, (blk + 1) * FFN_COLS // LANES):
                ln = slice(c * LANES, (c + 1) * LANES)
                vl = slice(D_FF + c * LANES, D_FF + (c + 1) * LANES)
                w0, w1, w2, bb = cw_ref[0:1, ln], cw_ref[1:2, ln], cw_ref[2:3, ln], cb_ref[:, ln]
                acc_b = jnp.zeros((8, LANES), F32)
                acc_w = [jnp.zeros((8, LANES), F32) for _ in range(3)]
                for r0 in range(0, tr, rb):
                    win = head[:, ln] if r0 == 0 else u2_ref[r0 - HALO:r0 + rb, ln]
                    g_m2, g_m1, g_0 = _rows_before(win, 2), _rows_before(win, 1), win[HALO:]
                    gate = g_m2 * w0 + g_m1 * w1 + g_0 * w2 + bb
                    sg = _sig(gate)
                    dh = dh_s[r0:r0 + rb, ln]
                    dgate = dh * u2_ref[r0:r0 + rb, vl] * _dsilu(gate, sg)
                    dg_s[r0:r0 + rb, ln] = dgate
                    du_ref[r0:r0 + rb, vl] = (dh * (gate * sg)).astype(du_ref.dtype)
                    acc_b = acc_b + _sum8(dgate)
                    acc_w[0] = acc_w[0] + _sum8(dgate * g_m2)
                    acc_w[1] = acc_w[1] + _sum8(dgate * g_m1)
                    acc_w[2] = acc_w[2] + _sum8(dgate * g_0)
                dcb_ref[:, ln] += jnp.sum(acc_b, axis=0, keepdims=True)
                for j in range(3):
                    dw_ref[j:j + 1, ln] += jnp.sum(acc_w[j], axis=0, keepdims=True)
                for r0 in range(0, tr, rb):
                    win = dg_s[r0:r0 + rb + HALO, ln]
                    d_gp = _rows_after(win, 2) * w0 + _rows_after(win, 1) * w1 + win[0:rb] * w2
                    du_ref[r0:r0 + rb, ln] = d_gp.astype(du_ref.dtype)
            cs = slice(blk * FFN_COLS, (blk + 1) * FFN_COLS)
            vs = slice(D_FF + blk * FFN_COLS, D_FF + (blk + 1) * FFN_COLS)
            acc = acc + lax.dot_general(du_ref[:, cs], wup_ref[cs, :], _NN, preferred_element_type=F32)
            acc = acc + lax.dot_general(du_ref[:, vs], wup_ref[vs, :], _NN, preferred_element_type=F32)
        dg_s[tr:, :] = dg_s[0:HALO, :]
        for c in range(nblk):
            dh_s[:, c * FFN_COLS:(c + 1) * FFN_COLS] = dh_next[c]
        dy = acc + ALPHA * dz2
        xh = xhat_ref[...]
        dz1_ref[...] = _ln_bwd_math(dy, xh, rstd_ref[...], g1_ref[...])
        dg1_ref[...] += jnp.sum(dy * xh, axis=0, keepdims=True)
        db1_ref[...] += jnp.sum(dy, axis=0, keepdims=True)

    rev = lambda w: pl.BlockSpec((tr, w), lambda i: (nt - 1 - i, 0))
    vec = pl.BlockSpec((1, d), lambda i: (0, 0))
    return _call(
        body, name=name, grid=(nt,),
        ins=[dz2, dz2, u2, u2, w_down, w_up_t, conv_w, conv_b, xhat1, rstd1, ln1_g],
        in_specs=[rev(d), pl.BlockSpec((tr, d), lambda i: (jnp.maximum(nt - 2 - i, 0), 0)), rev(2 * D_FF),
                  pl.BlockSpec((HALO, D_FF), lambda i: (jnp.maximum((nt - 1 - i) * hb - 1, 0), 0)),
                  _resident((D_FF, d)), _resident((2 * D_FF, d)), _resident((3, D_FF)), _resident((1, D_FF)),
                  rev(d), pl.BlockSpec((tr, 1), lambda i: (nt - 1 - i, 0)), vec],
        out_specs=[rev(2 * D_FF), rev(d), pl.BlockSpec((8, D_FF), lambda i: (0, 0)),
                   pl.BlockSpec((1, D_FF), lambda i: (0, 0)), vec, vec],
        out_shape=[jax.ShapeDtypeStruct((t, 2 * D_FF), BF16), jax.ShapeDtypeStruct((t, d), F32),
                   jax.ShapeDtypeStruct((8, D_FF), F32), jax.ShapeDtypeStruct((1, D_FF), F32),
                   jax.ShapeDtypeStruct((1, d), F32), jax.ShapeDtypeStruct((1, d), F32)],
        scratch_shapes=[pltpu.VMEM((HALO + rb, D_FF), F32), pltpu.VMEM((tr, D_FF), F32),
                        pltpu.VMEM((tr + HALO, D_FF), F32)],
        sem=("arbitrary",), comm=comm)


def _pad_rows(a, rows):
    return jnp.pad(a, ((0, rows - a.shape[0]), (0, 0)))


SMALL_LAYOUT = (("ln1_g", 1024), ("ln1_b", 1024), ("b_in", 2816), ("sinks", 8), ("hgrn_lb", 1024),
                ("hgrn_norm_g", 128), ("ln2_g", 1024), ("ln2_b", 1024), ("conv_b", 2816), ("loss", 1))
SMALL_SHAPES = {"ln1_g": (1, 1024), "ln1_b": (1, 1024), "b_in": (1, 2816), "sinks": (1, 8), "hgrn_lb": (2, 512),
                "hgrn_norm_g": (1, 128), "ln2_g": (1, 1024), "ln2_b": (1, 1024), "conv_b": (1, 2816),
                "loss": (1,)}


def _pack_small(parts):
    rows = []
    for name, size in SMALL_LAYOUT:
        flat = parts[name].reshape(-1).astype(F32)
        padded = -(-size // LANES) * LANES
        rows.append(jnp.pad(flat, (0, padded - size)).reshape(-1, LANES))
    return _pad_rows(jnp.concatenate(rows, axis=0), SMALL_ROWS)


def _small_update(small_g, ws, ms, vs, name):
    names = [n for n, _ in SMALL_LAYOUT if n != "loss"]
    first, r = {}, 0
    for n, size in SMALL_LAYOUT:
        first[n] = r
        r += -(-size // LANES)
    npar = len(names)

    def body(*refs):
        g_ref = refs[0]
        w_refs, m_refs, v_refs = (refs[1 + q * npar:1 + (q + 1) * npar] for q in range(3))
        outs = refs[1 + 3 * npar:-1]
        sum_ref = refs[-1]
        acc = g_ref[0]
        for s in range(1, N_DEV):
            acc = acc + g_ref[s]
        sum_ref[...] = acc
        outs[0][...] = sum_ref[first["loss"]:first["loss"] + 1, 0:1]
        for p, n in enumerate(names):
            g_out, d_out, m_out, v_out = outs[1 + 4 * p:5 + 4 * p]
            rows, cols = SMALL_SHAPES[n]
            if cols < LANES:
                g_out[...] = sum_ref[first[n]:first[n] + 1, 0:cols]
            else:
                per = cols // LANES
                for h in range(rows):
                    for j in range(per):
                        rr = first[n] + h * per + j
                        g_out[h:h + 1, j * LANES:(j + 1) * LANES] = sum_ref[rr:rr + 1, :]
            d_out[...], m_out[...], v_out[...] = _adamw_math(w_refs[p][...], g_out[...], m_refs[p][...],
                                                            v_refs[p][...])

    out_shape = [jax.ShapeDtypeStruct((1, 1), F32)]
    for n in names:
        out_shape += [jax.ShapeDtypeStruct(SMALL_SHAPES[n], F32)] * 4
    res = pl.pallas_call(
        body, name=name, out_shape=out_shape,
        scratch_shapes=[pltpu.VMEM((SMALL_ROWS, LANES), F32)],
        compiler_params=_cp(),
    )(small_g, *[ws[n] for n in names], *[ms[n] for n in names], *[vs[n] for n in names])
    return res[0], {n: res[1 + 4 * p:5 + 4 * p] for p, n in enumerate(names)}


def _conv_w_update(recv, w, m, v, name):
    taps, cols = w.shape

    def body(r_ref, w_ref, m_ref, v_ref, g_ref, d_ref, nm_ref, nv_ref):
        acc = r_ref[0]
        for s in range(1, N_DEV):
            acc = acc + r_ref[s]
        g = acc[0:taps]
        res = (g,) + _adamw_math(w_ref[...], g, m_ref[...], v_ref[...])
        for o_ref, val in zip((g_ref, d_ref, nm_ref, nv_ref), res):
            for k in range(taps):
                o_ref[k] = val[k:k + 1]

    shp = jax.ShapeDtypeStruct((taps, 1, cols), F32)
    outs = pl.pallas_call(body, name=name, out_shape=[shp, shp, shp, shp], compiler_params=_cp())(recv, w, m, v)
    return [o.transpose(1, 0, 2) for o in outs]


def kernel(x, positions, ln1_g, ln1_b, w_in, b_in, sinks, hgrn_lb, hgrn_norm_g, w_o, ln2_g, ln2_b, w_up, conv_w, conv_b, w_down, loss_target, m_ln1_g, m_ln1_b, m_w_in, m_b_in, m_sinks, m_hgrn_lb, m_hgrn_norm_g, m_w_o, m_ln2_g, m_ln2_b, m_w_up, m_conv_w, m_conv_b, m_w_down, v_ln1_g, v_ln1_b, v_w_in, v_b_in, v_sinks, v_hgrn_lb, v_hgrn_norm_g, v_w_o, v_ln2_g, v_ln2_b, v_w_up, v_conv_w, v_conv_b, v_w_down):
    t = x.shape[1]
    x2 = x[0]
    target = loss_target[0]
    pos_col = positions.reshape(t, 1)

    w_in_t_s = w_in[0].T.astype(BF16)
    w_up_t_s = w_up[0].T.astype(BF16)
    w_o_s = w_o[0].astype(BF16)
    w_down_s = w_down[0].astype(BF16)
    (ctab, stab, xb), (w_in_t_g, cw_g) = _prep(
        pos_col, x2, "prep_ag_w_in", _Comm([{"kind": "gather", "arr": w_in_t_s}, {"kind": "gather", "arr": _pad_rows(conv_w[0], 8)}]))
    w_in_t = w_in_t_g.reshape(D_FF, D_MODEL)
    w_a_t, w_h_t = w_in_t[:UA_W], w_in_t[UA_W:]
    conv_w_f = cw_g[:, 0:3].transpose(1, 0, 2).reshape(3, D_FF)

    ua = _mm(xb, w_a_t, tb=True, bias=b_in[:, :UA_W], name="fwd_in_attn")
    uh = _mm(xb, w_h_t, tb=True, bias=b_in[:, UA_W:], name="fwd_in_hgrn")
    half_up = SHARD_UP // 2
    (a_out, a_out_t), (w_o_g, w_up_half) = _attn_fwd(
        ua, ctab, stab, sinks, "attn_fwd",
        comm=_Comm([{"kind": "gather", "arr": w_o_s},
                    {"kind": "gather", "arr": w_up_t_s, "rows": (0, half_up), "dst_rows": SHARD_UP}]))
    (r_out, r_out_t, o_pre, states), (w_up_t_g, w_down_g) = _hgrn_fwd(
        uh, hgrn_lb, hgrn_norm_g, "hgrn_fwd",
        comm=_Comm([{"kind": "gather", "arr": w_up_t_s, "rows": (half_up, half_up), "dst_rows": SHARD_UP,
                     "dst_first": half_up, "into": w_up_half},
                    {"kind": "gather", "arr": w_down_s}]))
    w_down_f = w_down_g.reshape(D_FF, D_MODEL)
    w_o_f = w_o_g.reshape(D_MODEL, D_MODEL)
    w_up_t = w_up_t_g.reshape(2 * D_FF, D_MODEL)
    h1, h1b, xhat1, rstd1 = _mm_ln_fwd(r_out, w_o_f[ATTN_W:], (a_out, w_o_f[:ATTN_W]), x2, ln1_g, ln1_b,
                                       "fwd_o_ln1")
    u2, hmid, dz2, d_ln2_g, d_ln2_b, loss_part = _ffn_fwd(h1b, h1, w_up_t, conv_w_f, conv_b, w_down_f, target,
                                                         ln2_g, ln2_b, "ffn_fwd")[0]

    d_w_down, d_w_down_b = _mm(hmid, dz2, ta=True, out_dtype2=BF16, tm=1408, tk=1024, name="bwd_down_dw")
    (d_u2, dz1, d_conv_w8, d_conv_b, d_ln1_g, d_ln1_b), (recv_down,) = _ffn_bwd(
        dz2, u2, w_down_f, w_up_t, conv_w_f, conv_b, xhat1, rstd1, ln1_g, "ffn_bwd",
        comm=_Comm([{"kind": "exchange", "arr": d_w_down_b.reshape(N_DEV, SHARD_DOWN, D_MODEL)}]))
    d_w_up_t, d_w_up_t_b = _mm(d_u2, h1b, ta=True, out_dtype2=BF16, tm=1408, tk=1024, name="bwd_up_dw")
    d_ar = _mm(dz1, w_o_f, tb=True, name="bwd_o_dx")
    d_w_o_part = _mm(a_out_t, dz1, out_dtype2=BF16, tm=ATTN_W, out_rows=D_MODEL, name="bwd_o_dw_attn")
    d_w_o, d_w_o_b = _mm(r_out_t, dz1, out_dtype2=BF16, tm=HG_W, out_rows=D_MODEL, first_row=ATTN_W,
                         into=d_w_o_part, name="bwd_o_dw_hgrn")
    d_w_up_x = d_w_up_t_b.reshape(N_DEV, SHARD_UP, D_MODEL)
    half = SHARD_UP // 2
    d_cw_x = d_conv_w8.reshape(8, N_DEV, SHARD_IN).transpose(1, 0, 2)
    (d_ua, d_ua_t, d_bias_a, d_sinks), (recv_up_half, recv_cw) = _attn_bwd(
        ua, d_ar, ctab, stab, sinks, "attn_bwd",
        comm=_Comm([{"kind": "exchange", "arr": d_w_up_x, "rows": (0, half), "dst_rows": SHARD_UP},
                    {"kind": "exchange", "arr": d_cw_x}]))
    (d_uh, d_uh_t, d_bias_h, d_norm_g, d_lb8), (recv_up, recv_o) = _hgrn_bwd(
        uh, o_pre, d_ar, states, hgrn_lb, hgrn_norm_g, "hgrn_bwd",
        comm=_Comm([{"kind": "exchange", "arr": d_w_up_x, "rows": (half, half), "dst_rows": SHARD_UP,
                     "dst_first": half, "into": recv_up_half},
                    {"kind": "exchange", "arr": d_w_o_b.reshape(N_DEV, SHARD_O, D_MODEL)}]))
    d_w_in_part = _mm(d_ua_t, xb, out_dtype2=BF16, tm=UA_W, tk=t, out_rows=D_FF, name="bwd_in_dw_attn")
    d_w_in_t, d_w_in_t_b = _mm(d_uh_t, xb, out_dtype2=BF16, tm=256, tk=t, out_rows=D_FF, first_row=UA_W,
                               into=d_w_in_part, name="bwd_in_dw_hgrn")
    small_local = _pack_small({
        "ln1_g": d_ln1_g, "ln1_b": d_ln1_b, "b_in": jnp.concatenate([d_bias_a, d_bias_h], axis=1),
        "sinks": d_sinks[:, :8], "hgrn_lb": d_lb8[0:2], "hgrn_norm_g": d_norm_g, "ln2_g": d_ln2_g,
        "ln2_b": d_ln2_b, "conv_b": d_conv_b, "loss": loss_part[:, :1]})
    d_w_in_x = d_w_in_t_b.reshape(N_DEV, SHARD_IN, D_MODEL)
    res_up, (from_sibling,) = _sum_shards_adamw(
        [recv_up], d_w_up_t, w_up[0].T, m_w_up[0].T, v_w_up[0].T, "adamw_w_up",
        comm=_Comm([{"kind": "pair4", "arr": d_w_in_x}]))
    res_up = [r.T for r in res_up]
    own_in, chip_part = _pair_reduce(from_sibling, d_w_in_t, "pair_reduce_w_in")
    dx, (from_chips, small_g) = _mm(d_uh, w_h_t, addend=dz1, addend_scale=ALPHA, name="bwd_in_dx_hgrn",
                                    comm=_Comm([{"kind": "chips3", "arr": chip_part},
                                                {"kind": "gather", "arr": small_local}]))
    dx = _mm(d_ua, w_a_t, addend=dx, tk=768, name="bwd_in_dx_attn")

    res_in = [r.T for r in _chip_sum_adamw(from_chips, own_in, w_in[0].T, m_w_in[0].T, v_w_in[0].T, "adamw_w_in")]
    res_o = _sum_shards_adamw([recv_o], d_w_o, w_o[0], m_w_o[0], v_w_o[0], "adamw_w_o")
    res_down = _sum_shards_adamw([recv_down], d_w_down, w_down[0], m_w_down[0], v_w_down[0],
                                 "adamw_w_down")
    res_cw = _conv_w_update(recv_cw, conv_w[0], m_conv_w[0], v_conv_w[0], "adamw_conv_w")
    big = {"w_in": [r[None] for r in res_in], "w_up": [r[None] for r in res_up],
           "w_o": [r[None] for r in res_o], "w_down": [r[None] for r in res_down],
           "conv_w": list(res_cw)}

    loss11, small = _small_update(
        small_g,
        {"ln1_g": ln1_g, "ln1_b": ln1_b, "b_in": b_in, "sinks": sinks, "hgrn_lb": hgrn_lb,
         "hgrn_norm_g": hgrn_norm_g, "ln2_g": ln2_g, "ln2_b": ln2_b, "conv_b": conv_b},
        {"ln1_g": m_ln1_g, "ln1_b": m_ln1_b, "b_in": m_b_in, "sinks": m_sinks, "hgrn_lb": m_hgrn_lb,
         "hgrn_norm_g": m_hgrn_norm_g, "ln2_g": m_ln2_g, "ln2_b": m_ln2_b, "conv_b": m_conv_b},
        {"ln1_g": v_ln1_g, "ln1_b": v_ln1_b, "b_in": v_b_in, "sinks": v_sinks, "hgrn_lb": v_hgrn_lb,
         "hgrn_norm_g": v_hgrn_norm_g, "ln2_g": v_ln2_g, "ln2_b": v_ln2_b, "conv_b": v_conv_b},
        "adamw_small")
    loss = loss11[0, 0]

    order = ["ln1_g", "ln1_b", "w_in", "b_in", "sinks", "hgrn_lb", "hgrn_norm_g", "w_o", "ln2_g", "ln2_b",
             "w_up", "conv_w", "conv_b", "w_down"]

    def pick(idx):
        return [big[n][idx] if n in big else small[n][idx] for n in order]

    return (loss, dx[None], *pick(0), *pick(1), *pick(2), *pick(3))
```

```python
import functools

import jax
import jax.numpy as jnp
import numpy as np
from jax import lax
from jax.experimental import pallas as pl
from jax.experimental.pallas import tpu as pltpu

F32 = jnp.float32
BF16 = jnp.bfloat16

N_DEV = 8
D_MODEL = 1024
D_FF = 2816
ATTN_W = 512
KV_W = 128
UA_W = ATTN_W + 2 * KV_W
UH_W = 2048
HG_W = 512
ATTN_BLOCK = 128
HGRN_CHUNK = 64
HGRN_SUB = 16
HGRN_CHUNKS_PER_STEP = 4
EXP_CLAMP = 85.0
NEG_BIG = -1e30
LN_EPS = 1e-5
RMS_EPS = 1e-6
ALPHA = 2.0 ** 0.25
ATTN_SCALE = 0.125
ROPE_THETA = 500000.0

ADAM_LR = 0.001
ADAM_B1 = 0.9
ADAM_B2 = 0.999
ADAM_EPS = 1e-08
ADAM_WD = 0.01
ADAM_STEP = 10

LANES = 128
VMEM_LIMIT_BYTES = 56 * 1024 * 1024

SHARD_IN = D_FF // N_DEV
SHARD_UP = 2 * D_FF // N_DEV
SHARD_O = D_MODEL // N_DEV
SHARD_DOWN = D_FF // N_DEV
SMALL_ROWS = 88

_MESH = pl.DeviceIdType.MESH
_NT = (((1,), (1,)), ((), ()))
_NN = (((1,), (0,)), ((), ()))
_TN = (((0,), (0,)), ((), ()))


def _cp(*sem):
    if sem:
        return pltpu.CompilerParams(dimension_semantics=sem, vmem_limit_bytes=VMEM_LIMIT_BYTES)
    return pltpu.CompilerParams(vmem_limit_bytes=VMEM_LIMIT_BYTES)


def _sig(x):
    return 0.5 * jnp.tanh(0.5 * x) + 0.5


def _dsilu(x, s):
    return s * (1.0 + x * (1.0 - s))


def _dot(a, b, dims):
    return lax.dot_general(a.astype(BF16), b.astype(BF16), dims, preferred_element_type=F32)


def _split(a):
    hi = a.astype(BF16)
    return hi, (a - hi.astype(F32)).astype(BF16)


def _dot3(a, b, dims):
    ah, al = _split(a)
    bh, bl = _split(b)
    d = functools.partial(lax.dot_general, dimension_numbers=dims, preferred_element_type=F32)
    return d(ah, bh) + (d(ah, bl) + d(al, bh))


def _pick(n, pref):
    for t in pref:
        if t <= n and n % t == 0:
            return t
    return n


def _my_coords():
    return lax.axis_index("x"), lax.axis_index("y"), lax.axis_index("c")


def _peer(k):
    x, y, c = _my_coords()
    return (1 - x if k & 4 else x, 1 - y if k & 2 else y, 1 - c if k & 1 else c)


def _me():
    x, y, c = _my_coords()
    return 4 * x + 2 * y + c


class _Comm:
    def __init__(self, items):
        self.items = []
        for it in items:
            arr = it["arr"]
            full = arr.shape[0] if it["kind"] == "gather" else arr.shape[1]
            first, count = it.get("rows", (0, full))
            self.items.append(dict(kind=it["kind"], arr=arr, first=first, count=count,
                                   dst_rows=it.get("dst_rows", count), dst_first=it.get("dst_first", 0),
                                   into=it.get("into")))
        self.n = len(self.items)
        self.arrays = [it["arr"] for it in self.items]
        self.intos = [(a, it["into"]) for a, it in enumerate(self.items) if it["into"] is not None]

    def out_shapes(self):
        return [jax.ShapeDtypeStruct((4 if it["kind"] in ("pair4", "chips3") else N_DEV, it["dst_rows"],
                                      it["arr"].shape[-1]), it["arr"].dtype) for it in self.items]

    def specs(self, n=None):
        return [pl.BlockSpec(memory_space=pl.ANY)] * (self.n if n is None else n)

    def scratch(self):
        return [pltpu.SemaphoreType.DMA(((N_DEV - 1) * self.n,)), pltpu.SemaphoreType.DMA(((N_DEV - 1) * self.n,)),
                pltpu.SemaphoreType.DMA((self.n,))]

    def _src(self, a, ref, dev):
        it = self.items[a]
        blk = ref if it["kind"] == "gather" else ref.at[dev]
        return blk.at[pl.ds(it["first"], it["count"])]

    def _dst(self, a, ref, slot):
        it = self.items[a]
        return ref.at[slot].at[pl.ds(it["dst_first"], it["count"])]

    def _copy(self, a, k, src, dst, sems, me, slot):
        other = jnp.bitwise_xor(me, k)
        idx = a * (N_DEV - 1) + k - 1
        return pltpu.make_async_remote_copy(
            src_ref=self._src(a, src, other), dst_ref=self._dst(a, dst, me if slot == "mine" else other),
            send_sem=sems[0].at[idx], recv_sem=sems[1].at[idx], device_id=_peer(k), device_id_type=_MESH)

    def _pass_on(self, a, k, dst, sems, me):
        slot = self._dst(a, dst, jnp.bitwise_xor(me, k))
        idx = a * (N_DEV - 1) + k
        return pltpu.make_async_remote_copy(
            src_ref=slot, dst_ref=slot, send_sem=sems[0].at[idx], recv_sem=sems[1].at[idx],
            device_id=_peer(1), device_id_type=_MESH)

    def _part(self, a, r, src, dst, sems, me):
        it = self.items[a]
        idx = a * (N_DEV - 1) + r
        if it["kind"] == "pair4":
            k, slot = 1, jnp.bitwise_xor(jnp.bitwise_xor(me, 1), 2 * r)
        else:
            k, slot = 2 * r, r
        return pltpu.make_async_remote_copy(
            src_ref=src.at[slot].at[pl.ds(it["first"], it["count"])], dst_ref=self._dst(a, dst, r),
            send_sem=sems[0].at[idx], recv_sem=sems[1].at[idx], device_id=_peer(k), device_id_type=_MESH)

    def _parts(self, a):
        return range(4) if self.items[a]["kind"] == "pair4" else range(1, 4)

    def _local(self, a, src, dst, sems, me):
        return pltpu.make_async_copy(self._src(a, src, me), self._dst(a, dst, me), sems[2].at[a])

    def start(self, srcs, dsts, sems):
        me = _me()
        for a, (src, dst) in enumerate(zip(srcs, dsts)):
            if self.items[a]["kind"] in ("pair4", "chips3"):
                for r in self._parts(a):
                    self._part(a, r, src, dst, sems, me).start()
                continue
            direct = (1, 2, 4, 6) if self.items[a]["kind"] == "gather" else range(1, N_DEV)
            self._local(a, src, dst, sems, me).start()
            for k in direct:
                self._copy(a, k, src, dst, sems, me, "mine").start()

    def wait(self, srcs, dsts, sems):
        me = _me()
        for a, (src, dst) in enumerate(zip(srcs, dsts)):
            if self.items[a]["kind"] in ("pair4", "chips3"):
                for r in self._parts(a):
                    self._part(a, r, src, dst, sems, me).wait_recv()
                for r in self._parts(a):
                    self._part(a, r, src, dst, sems, me).wait_send()
                continue
            if self.items[a]["kind"] == "gather":
                for k in (2, 4, 6):
                    self._copy(a, k, src, dst, sems, me, "theirs").wait_recv()
                    self._pass_on(a, k, dst, sems, me).start()
                for k in (1, 3, 5, 7):
                    self._copy(a, k, src, dst, sems, me, "theirs").wait_recv()
                for k in (1, 2, 4, 6):
                    self._copy(a, k, src, dst, sems, me, "mine").wait_send()
                for k in (2, 4, 6):
                    self._pass_on(a, k, dst, sems, me).wait_send()
            else:
                for k in range(1, N_DEV):
                    self._copy(a, k, src, dst, sems, me, "theirs").wait_recv()
                for k in range(1, N_DEV):
                    self._copy(a, k, src, dst, sems, me, "mine").wait_send()
            self._local(a, src, dst, sems, me).wait()


def _call(body, *, name, grid, ins, in_specs, out_specs, out_shape, scratch_shapes=(), sem, comm=None):
    n_in, n_out, n_scr = len(ins), len(out_shape), len(scratch_shapes)
    if comm is None:
        outs = pl.pallas_call(
            body, name=name, grid=grid, in_specs=list(in_specs), out_specs=list(out_specs),
            out_shape=list(out_shape), scratch_shapes=list(scratch_shapes), compiler_params=_cp(*sem))(*ins)
        return list(outs), []
    nc, n_into = comm.n, len(comm.intos)

    def hosted(*refs):
        pos = n_in
        c_in = refs[pos:pos + nc]
        pos += nc + n_into
        outs = refs[pos:pos + n_out]
        pos += n_out
        c_out = refs[pos:pos + nc]
        pos += nc
        scr = refs[pos:pos + n_scr]
        sems = refs[pos + n_scr:]
        ids = [pl.program_id(d) for d in range(len(grid))]
        first = functools.reduce(jnp.logical_and, [i == 0 for i in ids])
        last = functools.reduce(jnp.logical_and, [i == g - 1 for i, g in zip(ids, grid)])

        @pl.when(first)
        def _():
            comm.start(c_in, c_out, sems)

        body(*refs[:n_in], *outs, *scr)

        @pl.when(last)
        def _():
            comm.wait(c_in, c_out, sems)

    aliases = {n_in + nc + j: n_out + a for j, (a, _) in enumerate(comm.intos)}
    outs = pl.pallas_call(
        hosted, name=name, grid=grid, in_specs=list(in_specs) + comm.specs() + comm.specs(n_into),
        out_specs=list(out_specs) + comm.specs(), out_shape=list(out_shape) + comm.out_shapes(),
        scratch_shapes=list(scratch_shapes) + comm.scratch(), input_output_aliases=aliases,
        compiler_params=_cp(*(["arbitrary"] * len(grid))))(*ins, *comm.arrays, *[arr for _, arr in comm.intos])
    return list(outs[:n_out]), list(outs[n_out:])


def _slot_sum(recv_ref, own_ref, shape):
    me = _me()
    acc = jnp.zeros(shape, F32)
    for s in range(N_DEV):
        acc = acc + jnp.where(me == s, own_ref[...], recv_ref[s].astype(F32))
    return acc


def _adamw_math(w, g, m, v):
    nm = ADAM_B1 * m + (1.0 - ADAM_B1) * g
    nv = ADAM_B2 * v + (1.0 - ADAM_B2) * (g * g)
    m_hat = nm / (1.0 - ADAM_B1 ** ADAM_STEP)
    v_hat = nv / (1.0 - ADAM_B2 ** ADAM_STEP)
    return -ADAM_LR * (m_hat / (jnp.sqrt(v_hat) + ADAM_EPS) + ADAM_WD * w), nm, nv


def _pair_reduce(from_sibling, mine, name):
    _, rows, cols = from_sibling.shape
    tr = _pick(rows, (176, 128, 64, 32, 16, 8))
    tiles = rows // tr
    table = jnp.bitwise_xor(_me(), jnp.arange(0, N_DEV, 2, dtype=jnp.int32))

    def body(tbl_ref, sib_ref, mine_ref, own_ref, send_ref):
        r = pl.program_id(1)
        total = mine_ref[...] + sib_ref[0].astype(F32)
        send_ref[0] = jnp.where(r == 0, 0.0, total).astype(BF16)

        @pl.when(r == 0)
        def _():
            own_ref[...] = total

    grid_spec = pltpu.PrefetchScalarGridSpec(
        num_scalar_prefetch=1, grid=(tiles, 4),
        in_specs=[pl.BlockSpec((1, tr, cols), lambda i, r, tbl: (r, i, 0)),
                  pl.BlockSpec((tr, cols), lambda i, r, tbl: (tbl[r] * tiles + i, 0))],
        out_specs=[pl.BlockSpec((tr, cols), lambda i, r, tbl: (i, 0)),
                   pl.BlockSpec((1, tr, cols), lambda i, r, tbl: (r, i, 0))])
    return pl.pallas_call(
        body, name=name, grid_spec=grid_spec,
        out_shape=[jax.ShapeDtypeStruct((rows, cols), F32), jax.ShapeDtypeStruct((4, rows, cols), BF16)],
        compiler_params=_cp("arbitrary", "arbitrary"),
    )(table, from_sibling, mine)


def _chip_sum_adamw(from_chips, own, w, m, v, name):
    _, rows, cols = from_chips.shape
    tr = _pick(rows, (176, 128, 64, 32, 16, 8))

    def body(recv_ref, own_ref, w_ref, m_ref, v_ref, g_ref, d_ref, nm_ref, nv_ref):
        g = own_ref[...]
        for r in range(1, 4):
            g = g + recv_ref[r].astype(F32)
        g_ref[...] = g
        d_ref[...], nm_ref[...], nv_ref[...] = _adamw_math(w_ref[...], g, m_ref[...], v_ref[...])

    spec = pl.BlockSpec((tr, cols), lambda i: (i, 0))
    shp = jax.ShapeDtypeStruct((rows, cols), F32)
    return pl.pallas_call(
        body, name=name, grid=(rows // tr,),
        in_specs=[pl.BlockSpec((4, tr, cols), lambda i: (0, i, 0)), spec, spec, spec, spec],
        out_specs=[spec, spec, spec, spec], out_shape=[shp, shp, shp, shp],
        compiler_params=_cp("parallel"),
    )(from_chips, own, w, m, v)


def _sum_shards_adamw(recvs, own, w, m, v, name, comm=None):
    rows_p, cols = recvs[0].shape[1], recvs[0].shape[2]
    n_p = len(recvs)
    tr = _pick(rows_p, (176, 128, 64, 32, 16, 8))
    tiles = rows_p // tr

    def body(*refs):
        recv_refs = refs[:n_p]
        own_ref, w_ref, m_ref, v_ref, g_ref, d_ref, nm_ref, nv_ref = refs[n_p:]
        for j in range(n_p):
            @pl.when(pl.program_id(0) == j)
            def _():
                g = _slot_sum(recv_refs[j], own_ref, (tr, cols))
                g_ref[...] = g
                d_ref[...], nm_ref[...], nv_ref[...] = _adamw_math(w_ref[...], g, m_ref[...], v_ref[...])

    spec = pl.BlockSpec((tr, cols), lambda p_, i: (p_ * tiles + i, 0))
    own_spec = pl.BlockSpec((tr, cols), lambda p_, i: (_me() * (n_p * tiles) + p_ * tiles + i, 0))
    shp = jax.ShapeDtypeStruct((rows_p * n_p, cols), F32)
    outs, couts = _call(
        body, name=name, grid=(n_p, tiles), ins=[*recvs, own, w, m, v],
        in_specs=[pl.BlockSpec((N_DEV, tr, cols), functools.partial(lambda p_, i, j: (0, jnp.where(p_ == j, i, 0), 0), j=j))
                  for j in range(n_p)] + [own_spec, spec, spec, spec],
        out_specs=[spec, spec, spec, spec], out_shape=[shp, shp, shp, shp],
        sem=("arbitrary", "arbitrary"), comm=comm)
    return outs if comm is None else (outs, couts)


def _mm(a, b, *, name, ta=False, tb=False, out_dtype=F32, out_dtype2=None, bias=None, addend=None,
        addend_scale=1.0, tm=1024, tn=1024, tk=1024, comm=None, out_rows=None, first_row=0, into=None):
    kdim, m = a.shape if ta else a.shape[::-1]
    n = b.shape[0] if tb else b.shape[1]
    tm = _pick(m, (tm, 1408, 1024, 768, 512, 256, 128))
    tn = _pick(n, (tn, 1408, 1024, 768, 512, 256, 128))
    tk = _pick(kdim, (tk, 1408, 1024, 768, 512, 256, 128))
    nk = kdim // tk
    a_spec = pl.BlockSpec((tk, tm), lambda i, j, k: (k, i)) if ta else pl.BlockSpec((tm, tk), lambda i, j, k: (i, k))
    b_spec = pl.BlockSpec((tn, tk), lambda i, j, k: (j, k)) if tb else pl.BlockSpec((tk, tn), lambda i, j, k: (k, j))
    ins, specs = [a, b], [a_spec, b_spec]
    if bias is not None:
        ins.append(bias)
        specs.append(pl.BlockSpec((1, tn), lambda i, j, k: (0, j)))
    if addend is not None:
        ins.append(addend)
        specs.append(pl.BlockSpec((tm, tn), lambda i, j, k: (i, j)))
    dims = (((0,) if ta else (1,), (1,) if tb else (0,)), ((), ()))
    has_bias, has_addend, two = bias is not None, addend is not None, out_dtype2 is not None

    def body(*refs):
        a_ref, b_ref = refs[0], refs[1]
        pos = 2
        bias_ref = addend_ref = None
        if has_bias:
            bias_ref = refs[pos]
            pos += 1
        if has_addend:
            addend_ref = refs[pos]
            pos += 1
        o_refs, acc_ref = refs[pos:-1], refs[-1]
        k = pl.program_id(2)

        @pl.when(k == 0)
        def _():
            acc_ref[...] = jnp.zeros_like(acc_ref)

        acc_ref[...] += _dot(a_ref[...], b_ref[...], dims)

        @pl.when(k == nk - 1)
        def _():
            r = acc_ref[...]
            if has_bias:
                r = r + bias_ref[...]
            if has_addend:
                r = r + addend_scale * addend_ref[...].astype(F32)
            for o_ref in o_refs:
                o_ref[...] = r.astype(o_ref.dtype)

    blk0 = first_row // tm
    dtypes = [out_dtype] + ([out_dtype2] if two else [])
    ospec = pl.BlockSpec((tm, tn), lambda i, j, k: (i + blk0, j))
    shapes = [jax.ShapeDtypeStruct((m if out_rows is None else out_rows, n), d) for d in dtypes]
    if into is not None:
        n_in = len(ins)
        outs = pl.pallas_call(
            lambda *refs: body(*refs[:n_in], *refs[n_in + len(into):]), name=name, grid=(m // tm, n // tn, nk),
            in_specs=specs + [pl.BlockSpec(memory_space=pl.ANY)] * len(into), out_specs=[ospec] * len(dtypes),
            out_shape=shapes, scratch_shapes=[pltpu.VMEM((tm, tn), F32)],
            input_output_aliases={n_in + j: j for j in range(len(into))},
            compiler_params=_cp("parallel", "parallel", "arbitrary"))(*ins, *into)
        return tuple(outs) if two else outs[0]
    outs, couts = _call(
        body, name=name, grid=(m // tm, n // tn, nk), ins=ins, in_specs=specs,
        out_specs=[ospec] * len(dtypes), out_shape=shapes,
        scratch_shapes=[pltpu.VMEM((tm, tn), F32)], sem=("parallel", "parallel", "arbitrary"), comm=comm)
    primary = tuple(outs) if two else outs[0]
    return (primary, couts) if comm is not None else primary


def _rope_lane_constants():
    inv_freq = np.float32(ROPE_THETA) ** (-np.arange(8, dtype=np.float32) * np.float32(2.0 / 16.0))
    lane = np.arange(LANES) % 64
    freq = np.where(lane < 16, inv_freq[lane % 8], 0.0).astype(np.float32)
    sign = np.where(lane < 8, -1.0, np.where(lane < 16, 1.0, 0.0)).astype(np.float32)
    return jnp.asarray(freq)[None, :], jnp.asarray(sign)[None, :]


def _prep(pos_col, x2, name, comm):
    t, d = x2.shape
    tr = _pick(t, (512, 256, 128))
    freq, sign = _rope_lane_constants()

    def body(pos_ref, freq_ref, sign_ref, x_ref, c_ref, s_ref, xb_ref):
        ang = pos_ref[...].astype(F32) * freq_ref[...]
        c_ref[...] = jnp.cos(ang)
        s_ref[...] = sign_ref[...] * jnp.sin(ang)
        xb_ref[...] = x_ref[...].astype(BF16)

    tab = pl.BlockSpec((tr, LANES), lambda i: (i, 0))
    return _call(
        body, name=name, grid=(t // tr,), ins=[pos_col, freq, sign, x2],
        in_specs=[pl.BlockSpec((tr, 1), lambda i: (i, 0)), pl.BlockSpec((1, LANES), lambda i: (0, 0)),
                  pl.BlockSpec((1, LANES), lambda i: (0, 0)), pl.BlockSpec((tr, d), lambda i: (i, 0))],
        out_specs=[tab, tab, pl.BlockSpec((tr, d), lambda i: (i, 0))],
        out_shape=[jax.ShapeDtypeStruct((t, LANES), F32), jax.ShapeDtypeStruct((t, LANES), F32),
                   jax.ShapeDtypeStruct((t, d), BF16)],
        sem=("parallel",), comm=comm)


def _swap8(t):
    width = t.shape[1]
    lane = jnp.bitwise_and(lax.broadcasted_iota(jnp.int32, t.shape, 1), 63)
    return jnp.where(lane < 8, pltpu.roll(t, width - 8, 1), jnp.where(lane < 16, pltpu.roll(t, 8, 1), 0.0))


def _rope(t, c, s):
    return t * c + _swap8(t) * s


def _rope_bwd(d, c, s):
    return d * c + _swap8(d * s)


def _tile4(a):
    return jnp.concatenate([a, a, a, a], axis=1)


def _attn_band(n, k_cur, k_prev, v_cur, v_prev, c_cur, s_cur, c_prev, s_prev):
    kband = jnp.concatenate([_rope(k_prev, c_prev, s_prev), _rope(k_cur, c_cur, s_cur)], axis=0)
    vband = jnp.concatenate([v_prev, v_cur], axis=0)
    qi = lax.broadcasted_iota(jnp.int32, (ATTN_BLOCK, 2 * ATTN_BLOCK), 0)
    kj = lax.broadcasted_iota(jnp.int32, (ATTN_BLOCK, 2 * ATTN_BLOCK), 1)
    dist = qi + ATTN_BLOCK - kj
    valid = (dist >= 0) & (dist < ATTN_BLOCK) & (n * ATTN_BLOCK - ATTN_BLOCK + kj >= 0)
    return (kband.astype(BF16), pltpu.roll(kband, 64, 1).astype(BF16),
            vband.astype(BF16), pltpu.roll(vband, 64, 1).astype(BF16), valid, kband)


def _attn_probs(raw, valid, sink, axis):
    s = jnp.where(valid, raw * ATTN_SCALE, NEG_BIG)
    m = jnp.maximum(jnp.max(s, axis=axis, keepdims=True), sink)
    p = jnp.exp(s - m)
    esink = jnp.exp(sink - m)
    z = jnp.sum(p, axis=axis, keepdims=True) + esink
    return p / z, esink / z


def _attn_valid_t(n):
    kj = lax.broadcasted_iota(jnp.int32, (2 * ATTN_BLOCK, ATTN_BLOCK), 0)
    qi = lax.broadcasted_iota(jnp.int32, (2 * ATTN_BLOCK, ATTN_BLOCK), 1)
    dist = qi + ATTN_BLOCK - kj
    return (dist >= 0) & (dist < ATTN_BLOCK) & (n * ATTN_BLOCK - ATTN_BLOCK + kj >= 0)


def _attn_specs(nb):
    def cur(col, width=KV_W):
        return pl.BlockSpec((ATTN_BLOCK, width), lambda n: (jnp.minimum(n, nb - 1), col))

    def prev(col):
        return pl.BlockSpec((ATTN_BLOCK, KV_W), lambda n: (jnp.maximum(n - 1, 0), col))

    ua_specs = [cur(0, ATTN_W), cur(4), prev(4), cur(5), prev(5)]
    tab_specs = [cur(0), cur(0), prev(0), prev(0)]
    return ua_specs, tab_specs


def _attn_fwd(ua, ctab, stab, sinks, name, comm=None):
    t = ua.shape[0]
    nb = t // ATTN_BLOCK
    ua_specs, tab_specs = _attn_specs(nb)

    def body(q_ref, kc_ref, kp_ref, vc_ref, vp_ref, cc_ref, sc_ref, cp_ref, sp_ref, sink_ref, o_ref, o_t_ref):
        n = pl.program_id(0)
        cc, sc = cc_ref[...], sc_ref[...]
        kb, kb_r, vb, vb_r, valid, _ = _attn_band(n, kc_ref[...], kp_ref[...], vc_ref[...], vp_ref[...],
                                                  cc, sc, cp_ref[...], sp_ref[...])
        qr = _rope(q_ref[...], _tile4(cc), _tile4(sc))
        lo = lax.broadcasted_iota(jnp.int32, (ATTN_BLOCK, LANES), 1) < 64
        heads = []
        for j in range(4):
            qj = qr[:, j * LANES:(j + 1) * LANES]
            for is_lo in (True, False):
                aligned = is_lo == (j < 2)
                qm = jnp.where(lo if is_lo else jnp.logical_not(lo), qj, 0.0).astype(BF16)
                raw = lax.dot_general(qm, kb if aligned else kb_r, _NT, preferred_element_type=F32)
                heads.append((raw, vb if aligned else vb_r, sink_ref[0, len(heads)]))
        halves = []
        for raw, vv, sink in heads:
            probs, _ = _attn_probs(raw, valid, sink, 1)
            halves.append(lax.dot_general(probs.astype(BF16), vv, _NN, preferred_element_type=F32))
        outs = [jnp.where(lo, halves[2 * j], halves[2 * j + 1]) for j in range(4)]
        o_ref[...] = jnp.concatenate(outs, axis=1).astype(o_ref.dtype)
        for j in range(4):
            o_t_ref[j * LANES:(j + 1) * LANES, :] = outs[j].T.astype(o_t_ref.dtype)

    return _call(
        body, name=name, grid=(nb,), ins=[ua, ua, ua, ua, ua, ctab, stab, ctab, stab, sinks],
        in_specs=ua_specs + tab_specs + [pl.BlockSpec(memory_space=pltpu.SMEM)],
        out_specs=[pl.BlockSpec((ATTN_BLOCK, ATTN_W), lambda n: (n, 0)),
                   pl.BlockSpec((ATTN_W, ATTN_BLOCK), lambda n: (0, n))],
        out_shape=[jax.ShapeDtypeStruct((t, ATTN_W), BF16), jax.ShapeDtypeStruct((ATTN_W, t), BF16)],
        sem=("parallel",), comm=comm)


def _attn_bwd(ua, d_out, ctab, stab, sinks, name, comm=None):
    t = ua.shape[0]
    nb = t // ATTN_BLOCK
    ua_specs, tab_specs = _attn_specs(nb)

    def body(q_ref, kc_ref, kp_ref, vc_ref, vp_ref, cc_ref, sc_ref, cp_ref, sp_ref, do_ref, sink_ref,
             dua_ref, dua_t_ref, dbias_ref, dsink_ref, dq_c, dk_c, dv_c, dq_n, dk_n, dv_n):
        n = pl.program_id(0)

        @pl.when(n == 0)
        def _():
            dq_c[...] = jnp.zeros_like(dq_c)
            dk_c[...] = jnp.zeros_like(dk_c)
            dv_c[...] = jnp.zeros_like(dv_c)
            dbias_ref[...] = jnp.zeros_like(dbias_ref)
            dsink_ref[...] = jnp.zeros_like(dsink_ref)

        @pl.when(n == nb)
        def _():
            dq_n[...] = jnp.zeros_like(dq_n)
            dk_n[...] = jnp.zeros_like(dk_n)
            dv_n[...] = jnp.zeros_like(dv_n)

        @pl.when(n < nb)
        def _():
            cc, sc = cc_ref[...], sc_ref[...]
            kb, kb_r, vb, vb_r, _, kb_f32 = _attn_band(n, kc_ref[...], kp_ref[...], vc_ref[...], vp_ref[...],
                                                       cc, sc, cp_ref[...], sp_ref[...])
            valid_t = _attn_valid_t(n)
            c4, s4 = _tile4(cc), _tile4(sc)
            qr = _rope(q_ref[...], c4, s4)
            do = do_ref[...].astype(F32)
            lane = lax.broadcasted_iota(jnp.int32, (ATTN_BLOCK, LANES), 1)
            lo = lane < 64
            lane_row = lax.broadcasted_iota(jnp.int32, (1, LANES), 1)
            k_t = {False: kb_f32.T.astype(BF16), True: pltpu.roll(kb_f32, 64, 1).T.astype(BF16)}
            heads = []
            for j in range(4):
                qj = qr[:, j * LANES:(j + 1) * LANES]
                doj = do[:, j * LANES:(j + 1) * LANES]
                for is_lo in (True, False):
                    aligned = is_lo == (j < 2)
                    msk = lo if is_lo else jnp.logical_not(lo)
                    kk = kb if aligned else kb_r
                    vv = vb if aligned else vb_r
                    qm = jnp.where(msk, qj, 0.0).astype(BF16)
                    dom = jnp.where(msk, doj, 0.0).astype(BF16)
                    heads.append(dict(
                        aligned=aligned, qm=qm, dom=dom, sink=sink_ref[0, len(heads)],
                        raw_t=lax.dot_general(kk, qm, _NT, preferred_element_type=F32),
                        dp_t=lax.dot_general(vv, dom, _NT, preferred_element_type=F32)))
            dk_band = jnp.zeros((2 * ATTN_BLOCK, LANES), F32)
            dv_band = jnp.zeros((2 * ATTN_BLOCK, LANES), F32)
            dsink = jnp.zeros((1, LANES), F32)
            for head, hd in enumerate(heads):
                probs_t, psink = _attn_probs(hd["raw_t"], valid_t, hd["sink"], 0)
                delta_t = jnp.sum(probs_t * hd["dp_t"], axis=0, keepdims=True)
                hd["ds_t"] = (probs_t * (hd["dp_t"] - delta_t) * ATTN_SCALE).astype(BF16)
                dsink = dsink + jnp.where(lane_row == head, -jnp.sum(psink * delta_t), 0.0)
                dk_h = lax.dot_general(hd["ds_t"], hd["qm"], _NN, preferred_element_type=F32)
                dv_h = lax.dot_general(probs_t.astype(BF16), hd["dom"], _NN, preferred_element_type=F32)
                if not hd["aligned"]:
                    dk_h = pltpu.roll(dk_h, 64, 1)
                    dv_h = pltpu.roll(dv_h, 64, 1)
                dk_band = dk_band + dk_h
                dv_band = dv_band + dv_h
            row_lo = lax.broadcasted_iota(jnp.int32, (LANES, ATTN_BLOCK), 0) < 64
            dq_t = [lax.dot_general(k_t[not hd["aligned"]], hd["ds_t"], _NN, preferred_element_type=F32)
                    for hd in heads]
            dqs = [jnp.where(row_lo, dq_t[2 * j], dq_t[2 * j + 1]).T for j in range(4)]
            dq_n[...] = _rope_bwd(jnp.concatenate(dqs, axis=1), c4, s4)
            dk_n[...] = dk_band
            dv_n[...] = dv_band
            dsink_ref[...] += dsink

        dk_prev = _rope_bwd(dk_c[...] + dk_n[0:ATTN_BLOCK, :], cp_ref[...], sp_ref[...])
        dv_prev = dv_c[...] + dv_n[0:ATTN_BLOCK, :]
        full = jnp.concatenate([dq_c[...], dk_prev, dv_prev], axis=1)
        dua_ref[...] = full.astype(dua_ref.dtype)
        for j in range(UA_W // LANES):
            dua_t_ref[j * LANES:(j + 1) * LANES, :] = full[:, j * LANES:(j + 1) * LANES].T.astype(dua_t_ref.dtype)
        dbias_ref[...] += jnp.sum(full, axis=0, keepdims=True)
        dq_c[...] = dq_n[...]
        dk_c[...] = dk_n[ATTN_BLOCK:, :]
        dv_c[...] = dv_n[ATTN_BLOCK:, :]

    return _call(
        body, name=name, grid=(nb + 1,), ins=[ua, ua, ua, ua, ua, ctab, stab, ctab, stab, d_out, sinks],
        in_specs=ua_specs + tab_specs + [
            pl.BlockSpec((ATTN_BLOCK, ATTN_W), lambda n: (jnp.minimum(n, nb - 1), 0)),
            pl.BlockSpec(memory_space=pltpu.SMEM)],
        out_specs=[pl.BlockSpec((ATTN_BLOCK, UA_W), lambda n: (jnp.maximum(n - 1, 0), 0)),
                   pl.BlockSpec((UA_W, ATTN_BLOCK), lambda n: (0, jnp.maximum(n - 1, 0))),
                   pl.BlockSpec((1, UA_W), lambda n: (0, 0)),
                   pl.BlockSpec((1, LANES), lambda n: (0, 0))],
        out_shape=[jax.ShapeDtypeStruct((t, UA_W), BF16), jax.ShapeDtypeStruct((UA_W, t), BF16),
                   jax.ShapeDtypeStruct((1, UA_W), F32),
                   jax.ShapeDtypeStruct((1, LANES), F32)],
        scratch_shapes=[pltpu.VMEM((ATTN_BLOCK, ATTN_W), F32), pltpu.VMEM((ATTN_BLOCK, KV_W), F32),
                        pltpu.VMEM((ATTN_BLOCK, KV_W), F32), pltpu.VMEM((ATTN_BLOCK, ATTN_W), F32),
                        pltpu.VMEM((2 * ATTN_BLOCK, KV_W), F32), pltpu.VMEM((2 * ATTN_BLOCK, KV_W), F32)],
        sem=("arbitrary",), comm=comm)


def _tri_mats():
    r = lax.broadcasted_iota(jnp.int32, (HGRN_CHUNK, LANES), 0)
    c = lax.broadcasted_iota(jnp.int32, (HGRN_CHUNK, LANES), 1)
    lower = ((c <= r) & (c < HGRN_CHUNK)).astype(F32)
    upper = ((c >= r) & (c < HGRN_CHUNK)).astype(F32)
    return lower, upper


def _tri_apply(tri, g):
    pad = jnp.concatenate([g, jnp.zeros_like(g)], axis=0)
    return lax.dot_general(tri, pad, _NN, precision=lax.Precision.HIGHEST, preferred_element_type=F32)


def _sub_masks():
    s = lax.broadcasted_iota(jnp.int32, (HGRN_CHUNK, LANES), 0)
    tt = lax.broadcasted_iota(jnp.int32, (HGRN_CHUNK, LANES), 1)
    return [(tt >= HGRN_SUB * i) & (tt < HGRN_SUB * (i + 1)) & (s <= tt) for i in range(HGRN_CHUNK // HGRN_SUB)]


def _hgrn_gates(hq, hf, lb_ref, b_scr):
    lb = _sig(lb_ref[0:1, :] - lb_ref[1:2, :])
    q = hq * _sig(hq)
    sg = _sig(hf)
    f = lb + (1.0 - lb) * sg
    k = 1.0 - f
    lower, _ = _tri_mats()
    b = _tri_apply(lower, jnp.log(f))
    b_scr[...] = b
    nsub = HGRN_CHUNK // HGRN_SUB
    starts = [jnp.zeros((1, HG_W), F32)] + [b_scr[HGRN_SUB * i - 1:HGRN_SUB * i, :] for i in range(1, nsub)]
    pq = jnp.concatenate([jnp.broadcast_to(p, (HGRN_SUB, HG_W)) for p in starts], axis=0)
    b_last = b_scr[HGRN_CHUNK - 1:HGRN_CHUNK, :]
    e_q = jnp.exp(b - pq)
    e_k = [jnp.exp(jnp.minimum(p - b, EXP_CLAMP)) for p in starts]
    e_b = jnp.exp(b)
    e_bl = jnp.exp(b_last - b)
    e_last = jnp.exp(b_last)
    return q, sg, f, k, lb, e_q, e_k, e_b, e_bl, e_last


def _sub_masks_ts():
    tt = lax.broadcasted_iota(jnp.int32, (HGRN_CHUNK, LANES), 0)
    s = lax.broadcasted_iota(jnp.int32, (HGRN_CHUNK, LANES), 1)
    return [(tt >= HGRN_SUB * i) & (tt < HGRN_SUB * (i + 1)) & (s <= tt) for i in range(HGRN_CHUNK // HGRN_SUB)]


def _masked_sum(blocks, masks, axis):
    step = HGRN_CHUNK if axis == 0 else LANES
    acc = jnp.zeros((HGRN_CHUNK, LANES), F32)
    for i, msk in enumerate(masks):
        blk = blocks[step * i:step * (i + 1), :] if axis == 0 else blocks[:, step * i:step * (i + 1)]
        acc = acc + jnp.where(msk, blk, 0.0)
    return acc


def _store_transposed(out_t_ref, chunk_rows):
    width = chunk_rows[0].shape[1]
    if len(chunk_rows) == 1:
        groups = [jnp.concatenate([chunk_rows[0], jnp.zeros_like(chunk_rows[0])], axis=0)]
    else:
        groups = [jnp.concatenate(chunk_rows[g:g + 2], axis=0) for g in range(0, len(chunk_rows), 2)]
    for g, rows in enumerate(groups):
        for c in range(width // LANES):
            tile = rows[:, c * LANES:(c + 1) * LANES].T.astype(out_t_ref.dtype)
            if len(chunk_rows) == 1:
                out_t_ref[c * LANES:(c + 1) * LANES, :] = tile[:, 0:HGRN_CHUNK]
            else:
                out_t_ref[c * LANES:(c + 1) * LANES, g * LANES:(g + 1) * LANES] = tile


def _hgrn_chunk_inputs(j, hq_ref, hf_ref, hi_ref, hg_ref, lb_ref, b_scr):
    rows = slice(j * HGRN_CHUNK, (j + 1) * HGRN_CHUNK)
    hq, hf, v, hg = hq_ref[rows, :], hf_ref[rows, :], hi_ref[rows, :], hg_ref[rows, :]
    q, sg, f, k, lb, e_q, e_k, e_b, e_bl, e_last = _hgrn_gates(hq, hf, lb_ref, b_scr.at[j])
    return dict(rows=rows, hq=hq, v=v, hg=hg, q=q, sg=sg, f=f, k=k, lb=lb, e_q=e_q, e_k=e_k, e_b=e_b, e_bl=e_bl,
                e_last=e_last, qt=q * e_q, qb=q * e_b, kd=k * e_bl, khat=[k * e for e in e_k])


def _hgrn_fwd(uh, lb_raw, norm_g, name, comm=None):
    t = uh.shape[0]
    nc = t // HGRN_CHUNK
    cps = _pick(nc, (HGRN_CHUNKS_PER_STEP, 2, 1))
    rows_step = cps * HGRN_CHUNK

    def body(hq_ref, hf_ref, hi_ref, hg_ref, lb_ref, ng_ref, r_ref, r_t_ref, o_ref, st_out_ref, st_ref, b_scr):
        @pl.when(pl.program_id(0) == 0)
        def _():
            st_ref[...] = jnp.zeros_like(st_ref)

        masks = _sub_masks_ts()
        ng = ng_ref[...]
        zpad = jnp.zeros((HGRN_CHUNK, LANES), F32)
        heads = [slice(h * LANES, (h + 1) * LANES) for h in range(4)]
        chunks = [_hgrn_chunk_inputs(j, hq_ref, hf_ref, hi_ref, hg_ref, lb_ref, b_scr) for j in range(cps)]
        for ch in chunks:
            ch["scores"] = [_dot3(ch["qt"][:, sl],
                                  jnp.concatenate([x for kh in ch["khat"] for x in (kh[:, sl], zpad)], axis=0), _NT)
                            for sl in heads]
        for j, ch in enumerate(chunks):
            o_heads, y_heads = [], []
            for h, sl in enumerate(heads):
                a_ts = _masked_sum(ch["scores"][h], masks, 1)
                vh = ch["v"][:, sl].astype(BF16)
                v_pad = jnp.concatenate([vh, jnp.zeros_like(vh)], axis=0)
                o_intra = lax.dot_general(a_ts.astype(BF16), v_pad, _NN, preferred_element_type=F32)
                st = st_ref[h]
                st_out_ref[j, h] = st
                o_inter = _dot(ch["qb"][:, sl], st, _NT)
                st_ref[h] = st * ch["e_last"][:, sl] + _dot(vh, ch["kd"][:, sl], _TN)
                oh = o_intra + o_inter
                rs = lax.rsqrt(jnp.mean(oh * oh, axis=1, keepdims=True) + RMS_EPS)
                o_heads.append(oh)
                y_heads.append(oh * rs * ng)
            hg = ch["hg"]
            o_ref[ch["rows"], :] = jnp.concatenate(o_heads, axis=1)
            ch["r"] = jnp.concatenate(y_heads, axis=1) * (hg * _sig(hg))
            r_ref[ch["rows"], :] = ch["r"].astype(r_ref.dtype)
        _store_transposed(r_t_ref, [ch["r"] for ch in chunks])

    col = lambda j: pl.BlockSpec((rows_step, HG_W), lambda c: (c, j))
    return _call(
        body, name=name, grid=(nc // cps,), ins=[uh, uh, uh, uh, lb_raw, norm_g],
        in_specs=[col(0), col(1), col(2), col(3),
                  pl.BlockSpec((2, HG_W), lambda c: (0, 0)), pl.BlockSpec((1, LANES), lambda c: (0, 0))],
        out_specs=[pl.BlockSpec((rows_step, HG_W), lambda c: (c, 0)),
                   pl.BlockSpec((HG_W, rows_step), lambda c: (0, c)),
                   pl.BlockSpec((rows_step, HG_W), lambda c: (c, 0)),
                   pl.BlockSpec((cps, 4, LANES, LANES), lambda c: (c, 0, 0, 0))],
        out_shape=[jax.ShapeDtypeStruct((t, HG_W), BF16), jax.ShapeDtypeStruct((HG_W, t), BF16),
                   jax.ShapeDtypeStruct((t, HG_W), F32), jax.ShapeDtypeStruct((nc, 4, LANES, LANES), F32)],
        scratch_shapes=[pltpu.VMEM((4, LANES, LANES), F32), pltpu.VMEM((cps, HGRN_CHUNK, HG_W), F32)],
        sem=("arbitrary",), comm=comm)


def _hgrn_bwd(uh, o_pre, d_r, states, lb_raw, norm_g, name, comm=None):
    t = uh.shape[0]
    nc = t // HGRN_CHUNK
    cps = _pick(nc, (HGRN_CHUNKS_PER_STEP, 2, 1))
    ns = nc // cps
    rows_step = cps * HGRN_CHUNK
    nsub = HGRN_CHUNK // HGRN_SUB

    def body(hq_ref, hf_ref, hi_ref, hg_ref, o_ref, dr_ref, st_in_ref, lb_ref, ng_ref,
             duh_ref, duh_t_ref, dbias_ref, dng_ref, dlb_ref, dst_ref, b_scr, dlb_acc):
        i = pl.program_id(0)

        @pl.when(i == 0)
        def _():
            dst_ref[...] = jnp.zeros_like(dst_ref)
            dbias_ref[...] = jnp.zeros_like(dbias_ref)
            dng_ref[...] = jnp.zeros_like(dng_ref)
            dlb_acc[...] = jnp.zeros_like(dlb_acc)

        masks_st = _sub_masks()
        masks_ts = _sub_masks_ts()
        ng = ng_ref[...]
        zpad = jnp.zeros((HGRN_CHUNK, LANES), F32)
        _, upper = _tri_mats()
        heads = [slice(h * LANES, (h + 1) * LANES) for h in range(4)]
        row = lax.broadcasted_iota(jnp.int32, (HGRN_CHUNK, HG_W), 0)

        chunks = [_hgrn_chunk_inputs(j, hq_ref, hf_ref, hi_ref, hg_ref, lb_ref, b_scr) for j in range(cps)]
        dng = jnp.zeros((1, LANES), F32)
        for ch in chunks:
            o = o_ref[ch["rows"], :]
            dr = dr_ref[ch["rows"], :].astype(F32)
            hg = ch["hg"]
            sgg = _sig(hg)
            dy = dr * (hg * sgg)
            do_h, y_h = [], []
            for sl in heads:
                oh = o[:, sl]
                rs = lax.rsqrt(jnp.mean(oh * oh, axis=1, keepdims=True) + RMS_EPS)
                y_h.append(oh * rs * ng)
                dng = dng + jnp.sum(dy[:, sl] * oh * rs, axis=0, keepdims=True)
                w = dy[:, sl] * ng
                do_h.append(rs * (w - oh * (rs * rs) * jnp.mean(w * oh, axis=1, keepdims=True)))
            ch["do"] = do_h
            ch["dhg"] = dr * jnp.concatenate(y_h, axis=1) * _dsilu(hg, sgg)

        for ch in chunks:
            ch["kst"], ch["kpad"], ch["qt_pad"], ch["v_b"], ch["do_pad"] = [], [], [], [], []
            ch["ats"], ch["d_at"], ch["d_a"] = [], [], []
            for h, sl in enumerate(heads):
                kst = jnp.concatenate([kh[:, sl] for kh in ch["khat"]], axis=0)
                kpad = jnp.concatenate([x for kh in ch["khat"] for x in (kh[:, sl], zpad)], axis=0)
                qt_pad = jnp.concatenate([ch["qt"][:, sl], zpad], axis=0)
                vh = ch["v"][:, sl].astype(BF16)
                v_pad = jnp.concatenate([vh, jnp.zeros_like(vh)], axis=0)
                do_b = ch["do"][h].astype(BF16)
                do_pad = jnp.concatenate([do_b, jnp.zeros_like(do_b)], axis=0)
                ch["kst"].append(kst)
                ch["kpad"].append(kpad)
                ch["qt_pad"].append(qt_pad)
                ch["v_b"].append(vh)
                ch["do_pad"].append(do_pad)
                ch["ats"].append(_dot3(kst, qt_pad, _NT))
                ch["d_at"].append(lax.dot_general(vh, do_pad, _NT, preferred_element_type=F32))
                ch["d_a"].append(lax.dot_general(do_b, v_pad, _NT, preferred_element_type=F32))

        for ch in chunks:
            ch["d_kst"], ch["d_qt"], ch["dv"] = [], [], []
            for h in range(4):
                at = _masked_sum(ch["ats"][h], masks_st, 0)
                d_ats = jnp.concatenate([jnp.where(m, ch["d_at"][h], 0.0) for m in masks_st], axis=0)
                d_a_cat = jnp.concatenate([jnp.where(m, ch["d_a"][h], 0.0) for m in masks_ts], axis=1)
                ch["d_kst"].append(_dot3(d_ats, ch["qt_pad"][h], _NN))
                ch["d_qt"].append(_dot3(d_a_cat, ch["kpad"][h], _NN))
                ch["dv"].append(lax.dot_general(at.astype(BF16), ch["do_pad"][h], _NN, preferred_element_type=F32))

        for j in reversed(range(cps)):
            ch = chunks[j]
            q, k, sg, f, lb = ch["q"], ch["k"], ch["sg"], ch["f"], ch["lb"]
            dq_h, dk_h, dv_h, extra_h = [], [], [], []
            for h, sl in enumerate(heads):
                st_prev = st_in_ref[j, h]
                d_st = dst_ref[h]
                d_st_b = d_st.astype(BF16)
                do_b = ch["do_pad"][h][0:HGRN_CHUNK, :]
                kd, e_last = ch["kd"][:, sl], ch["e_last"][:, sl]
                dv = ch["dv"][h] + _dot(kd, d_st_b, _NT)
                d_qb = _dot(do_b, st_prev, _NN)
                d_kd = lax.dot_general(ch["v_b"][h], d_st_b, _NN, preferred_element_type=F32)
                extra_h.append(jnp.sum(st_prev * d_st, axis=0, keepdims=True) * e_last
                               + jnp.sum(kd * d_kd, axis=0, keepdims=True))
                dst_ref[h] = d_st * e_last + _dot(do_b, ch["qb"][:, sl], _TN)
                dq_h.append(ch["d_qt"][h] * ch["e_q"][:, sl] + d_qb * ch["e_b"][:, sl])
                dkk = d_kd * ch["e_bl"][:, sl]
                for s_ in range(nsub):
                    dkk = dkk + ch["d_kst"][h][HGRN_CHUNK * s_:HGRN_CHUNK * (s_ + 1), :] * ch["e_k"][s_][:, sl]
                dk_h.append(dkk)
                dv_h.append(dv)
            dq = jnp.concatenate(dq_h, axis=1)
            dk = jnp.concatenate(dk_h, axis=1)
            dv = jnp.concatenate(dv_h, axis=1)
            extra = jnp.concatenate(extra_h, axis=1)
            db = q * dq - k * dk + jnp.where(row == HGRN_CHUNK - 1, extra, 0.0)
            dg = _tri_apply(upper, db)
            df = dg / f - dk
            dhf = df * (1.0 - lb) * sg * (1.0 - sg)
            dhq = dq * _dsilu(ch["hq"], _sig(ch["hq"]))
            full = jnp.concatenate([dhq, dhf, dv, ch["dhg"]], axis=1)
            duh_ref[ch["rows"], :] = full.astype(duh_ref.dtype)
            ch["full"] = full
            dbias_ref[...] += jnp.sum(full, axis=0, keepdims=True)
            dlb_acc[...] += jnp.sum(df * (1.0 - sg), axis=0, keepdims=True)
        dng_ref[...] += dng
        _store_transposed(duh_t_ref, [ch["full"] for ch in chunks])

        @pl.when(i == ns - 1)
        def _():
            lb = chunks[0]["lb"]
            d_a0 = dlb_acc[...] * lb * (1.0 - lb)
            r8 = lax.broadcasted_iota(jnp.int32, (8, HG_W), 0)
            dlb_ref[...] = jnp.where(r8 == 0, d_a0, jnp.where(r8 == 1, -d_a0, 0.0))

    col = lambda j: pl.BlockSpec((rows_step, HG_W), lambda i: (ns - 1 - i, j))
    return _call(
        body, name=name, grid=(ns,), ins=[uh, uh, uh, uh, o_pre, d_r, states, lb_raw, norm_g],
        in_specs=[col(0), col(1), col(2), col(3), col(0), col(d_r.shape[1] // HG_W - 1),
                  pl.BlockSpec((cps, 4, LANES, LANES), lambda i: (ns - 1 - i, 0, 0, 0)),
                  pl.BlockSpec((2, HG_W), lambda i: (0, 0)), pl.BlockSpec((1, LANES), lambda i: (0, 0))],
        out_specs=[pl.BlockSpec((rows_step, UH_W), lambda i: (ns - 1 - i, 0)),
                   pl.BlockSpec((UH_W, rows_step), lambda i: (0, ns - 1 - i)),
                   pl.BlockSpec((1, UH_W), lambda i: (0, 0)),
                   pl.BlockSpec((1, LANES), lambda i: (0, 0)),
                   pl.BlockSpec((8, HG_W), lambda i: (0, 0))],
        out_shape=[jax.ShapeDtypeStruct((t, UH_W), BF16), jax.ShapeDtypeStruct((UH_W, t), BF16),
                   jax.ShapeDtypeStruct((1, UH_W), F32),
                   jax.ShapeDtypeStruct((1, LANES), F32), jax.ShapeDtypeStruct((8, HG_W), F32)],
        scratch_shapes=[pltpu.VMEM((4, LANES, LANES), F32), pltpu.VMEM((cps, HGRN_CHUNK, HG_W), F32),
                        pltpu.VMEM((1, HG_W), F32)],
        sem=("arbitrary",), comm=comm)


def _ln_bwd_math(dy, xhat, rstd, g):
    dxh = dy * g
    return rstd * (dxh - jnp.mean(dxh, axis=1, keepdims=True)
                   - xhat * jnp.mean(dxh * xhat, axis=1, keepdims=True))


def _mm_stacked(a1, a2, b, *, name, tk=1024):
    (m1, kdim), m2, n = a1.shape, a2.shape[0], b.shape[1]
    tk = _pick(kdim, (tk, 512, 256, 128))
    nk = kdim // tk

    def body(a1_ref, a2_ref, b_ref, o_ref, ob_ref, acc_ref):
        k = pl.program_id(0)

        @pl.when(k == 0)
        def _():
            acc_ref[...] = jnp.zeros_like(acc_ref)

        bb = b_ref[...].astype(BF16)
        acc_ref[0:m1] += _dot(a1_ref[...], bb, _NN)
        acc_ref[m1:] += _dot(a2_ref[...], bb, _NN)

        @pl.when(k == nk - 1)
        def _():
            r = acc_ref[...]
            o_ref[...] = r
            ob_ref[...] = r.astype(BF16)

    ospec = pl.BlockSpec((m1 + m2, n), lambda k: (0, 0))
    return pl.pallas_call(
        body, name=name, grid=(nk,),
        in_specs=[pl.BlockSpec((m1, tk), lambda k: (0, k)), pl.BlockSpec((m2, tk), lambda k: (0, k)),
                  pl.BlockSpec((tk, n), lambda k: (k, 0))],
        out_specs=[ospec, ospec],
        out_shape=[jax.ShapeDtypeStruct((m1 + m2, n), F32), jax.ShapeDtypeStruct((m1 + m2, n), BF16)],
        scratch_shapes=[pltpu.VMEM((m1 + m2, n), F32)],
        compiler_params=_cp("arbitrary"))(a1, a2, b)


def _mm_rows(a, b, extras, *, name, epilogue, out_shape, out_specs, tb=False, tm=512, tk=1408, pair2=None):
    m, kdim = a.shape
    n = b.shape[0] if tb else b.shape[1]
    tm = _pick(m, (tm, 256, 128))
    tk = _pick(kdim, (tk, 1408, 1024, 768, 512, 256, 128))
    nk = kdim // tk
    b_spec = pl.BlockSpec((n, tk), lambda i, k: (0, k)) if tb else pl.BlockSpec((tk, n), lambda i, k: (k, 0))
    dims = _NT if tb else _NN
    n_ex, n_out, n_p2 = len(extras), len(out_shape), (0 if pair2 is None else 2)

    def body(*refs):
        a_ref, b_ref = refs[0], refs[1]
        p2_refs = refs[2:2 + n_p2]
        ex_refs = refs[2 + n_p2:2 + n_p2 + n_ex]
        o_refs = refs[2 + n_p2 + n_ex:2 + n_p2 + n_ex + n_out]
        acc_ref = refs[-1]
        i, k = pl.program_id(0), pl.program_id(1)

        @pl.when(k == 0)
        def _():
            if n_p2:
                acc_ref[...] = _dot(p2_refs[0][...], p2_refs[1][...], _NN)
            else:
                acc_ref[...] = jnp.zeros_like(acc_ref)

        acc_ref[...] += _dot(a_ref[...], b_ref[...], dims)

        @pl.when(k == nk - 1)
        def _():
            epilogue(acc_ref[...], ex_refs, o_refs, i == 0)

    p2_specs, p2_ins = [], []
    if pair2 is not None:
        k2 = pair2[0].shape[1]
        p2_specs = [pl.BlockSpec((tm, k2), lambda i, k: (i, 0)), pl.BlockSpec((k2, n), lambda i, k: (0, 0))]
        p2_ins = list(pair2)
    return pl.pallas_call(
        body, name=name, grid=(m // tm, nk),
        in_specs=[pl.BlockSpec((tm, tk), lambda i, k: (i, k)), b_spec] + p2_specs + [sp for _, sp in extras],
        out_specs=list(out_specs), out_shape=list(out_shape),
        scratch_shapes=[pltpu.VMEM((tm, n), F32)],
        compiler_params=_cp("arbitrary", "arbitrary"),
    )(a, b, *p2_ins, *[arr for arr, _ in extras])


def _rows_specs(tm, d):
    row = pl.BlockSpec((tm, d), lambda i, k: (i, 0))
    vec = pl.BlockSpec((1, d), lambda i, k: (0, 0))
    col = pl.BlockSpec((tm, 1), lambda i, k: (i, 0))
    return row, vec, col


def _mm_ln_fwd(a, b, pair2, addend, g, beta, name, tm=512):
    t, d = addend.shape
    tm = _pick(t, (tm, 256, 128))
    row, vec, col = _rows_specs(tm, d)

    def epilogue(acc, ex, outs, first):
        z = acc + ALPHA * ex[0][...]
        mu = jnp.mean(z, axis=1, keepdims=True)
        zc = z - mu
        rstd = lax.rsqrt(jnp.mean(zc * zc, axis=1, keepdims=True) + LN_EPS)
        xhat = zc * rstd
        h = xhat * ex[1][...] + ex[2][...]
        outs[0][...] = h
        outs[1][...] = h.astype(BF16)
        outs[2][...] = xhat
        outs[3][...] = rstd

    return _mm_rows(a, b, [(addend, row), (g, vec), (beta, vec)], name=name, epilogue=epilogue, tm=tm, pair2=pair2,
                    out_shape=[jax.ShapeDtypeStruct((t, d), F32), jax.ShapeDtypeStruct((t, d), BF16),
                               jax.ShapeDtypeStruct((t, d), F32), jax.ShapeDtypeStruct((t, 1), F32)],
                    out_specs=[row, row, row, col])


CONV_RB = 32
HALO = 8


def _sum8(x):
    acc = x[0:8]
    for r in range(8, x.shape[0], 8):
        acc = acc + x[r:r + 8]
    return acc


FFN_TILE = 256
FFN_COLS = 256


def _rows_before(win, k):
    return pltpu.roll(win, k, 0)[HALO:]


def _rows_after(win, k):
    n = win.shape[0]
    return pltpu.roll(win, n - k, 0)[0:n - HALO]


def _resident(shape):
    return pl.BlockSpec(shape, lambda i: (0,) * len(shape), pipeline_mode=pl.Buffered(1))


def _ffn_fwd(h1b, h1, w_up_t, conv_w, conv_b, w_down, target, ln2_g, ln2_b, name, comm=None):
    t, d = h1.shape
    tr = _pick(t, (FFN_TILE, 128))
    nblk = D_FF // FFN_COLS
    rb = CONV_RB

    def body(a_ref, wup_ref, cw_ref, cb_ref, wd_ref, h1_ref, tgt_ref, g_ref, b_ref,
             u2_ref, hm_ref, dz_ref, dg_ref, db_ref, loss_ref, ext):
        i = pl.program_id(0)

        @pl.when(i == 0)
        def _():
            ext[0:HALO, :] = jnp.zeros((HALO, D_FF), F32)
            dg_ref[...] = jnp.zeros_like(dg_ref)
            db_ref[...] = jnp.zeros_like(db_ref)
            loss_ref[...] = jnp.zeros_like(loss_ref)

        a = a_ref[...]
        for c in range(nblk):
            cs = slice(c * FFN_COLS, (c + 1) * FFN_COLS)
            vs = slice(D_FF + c * FFN_COLS, D_FF + (c + 1) * FFN_COLS)
            gate_pre = lax.dot_general(a, wup_ref[cs, :], _NT, preferred_element_type=F32)
            u2_ref[:, cs] = gate_pre
            ext[HALO:, cs] = gate_pre
            u2_ref[:, vs] = lax.dot_general(a, wup_ref[vs, :], _NT, preferred_element_type=F32)
        acc = jnp.zeros((tr, d), F32)
        for c in range(nblk):
            cs = slice(c * FFN_COLS, (c + 1) * FFN_COLS)
            for sub in range(FFN_COLS // LANES):
                ln = slice(c * FFN_COLS + sub * LANES, c * FFN_COLS + (sub + 1) * LANES)
                vl = slice(D_FF + c * FFN_COLS + sub * LANES, D_FF + c * FFN_COLS + (sub + 1) * LANES)
                w0, w1, w2, bb = cw_ref[0:1, ln], cw_ref[1:2, ln], cw_ref[2:3, ln], cb_ref[:, ln]
                for r0 in range(0, tr, rb):
                    win = ext[r0:r0 + HALO + rb, ln]
                    gate = _rows_before(win, 2) * w0 + _rows_before(win, 1) * w1 + win[HALO:] * w2 + bb
                    hm_ref[r0:r0 + rb, ln] = (gate * _sig(gate) * u2_ref[r0:r0 + rb, vl]).astype(hm_ref.dtype)
            acc = acc + lax.dot_general(hm_ref[:, cs], wd_ref[cs, :], _NN, preferred_element_type=F32)
        ext[0:HALO, :] = ext[tr:tr + HALO, :]

        z = acc + ALPHA * h1_ref[...]
        gg = g_ref[...]
        mu = jnp.mean(z, axis=1, keepdims=True)
        zc = z - mu
        rstd = lax.rsqrt(jnp.mean(zc * zc, axis=1, keepdims=True) + LN_EPS)
        xhat = zc * rstd
        err = xhat * gg + b_ref[...] - tgt_ref[...]
        loss_ref[...] += 0.5 * jnp.sum(jnp.mean(err * err, axis=1, keepdims=True))
        dy = err * (1.0 / d)
        dz_ref[...] = _ln_bwd_math(dy, xhat, rstd, gg)
        dg_ref[...] += jnp.sum(dy * xhat, axis=0, keepdims=True)
        db_ref[...] += jnp.sum(dy, axis=0, keepdims=True)

    row = lambda w: pl.BlockSpec((tr, w), lambda i: (i, 0))
    vec = pl.BlockSpec((1, d), lambda i: (0, 0))
    return _call(
        body, name=name, grid=(t // tr,),
        ins=[h1b, w_up_t, conv_w, conv_b, w_down, h1, target, ln2_g, ln2_b],
        in_specs=[row(d), _resident((2 * D_FF, d)), _resident((3, D_FF)), _resident((1, D_FF)),
                  _resident((D_FF, d)), row(d), row(d), vec, vec],
        out_specs=[row(2 * D_FF), row(D_FF), row(d), vec, vec, pl.BlockSpec((1, LANES), lambda i: (0, 0))],
        out_shape=[jax.ShapeDtypeStruct((t, 2 * D_FF), F32), jax.ShapeDtypeStruct((t, D_FF), BF16),
                   jax.ShapeDtypeStruct((t, d), F32), jax.ShapeDtypeStruct((1, d), F32),
                   jax.ShapeDtypeStruct((1, d), F32), jax.ShapeDtypeStruct((1, LANES), F32)],
        scratch_shapes=[pltpu.VMEM((tr + HALO, D_FF), F32)],
        sem=("arbitrary",), comm=comm)


def _ffn_bwd(dz2, u2, w_down, w_up_t, conv_w, conv_b, xhat1, rstd1, ln1_g, name, comm=None):
    t, d = dz2.shape
    tr = _pick(t, (FFN_TILE, 128))
    nt = t // tr
    hb = tr // HALO
    nblk = D_FF // FFN_COLS
    rb = CONV_RB

    def body(dz2_ref, dz2_next_ref, u2_ref, gp_prev_ref, wd_ref, wup_ref, cw_ref, cb_ref, xhat_ref, rstd_ref,
             g1_ref, du_ref, dz1_ref, dw_ref, dcb_ref, dg1_ref, db1_ref, head, dh_s, dg_s):
        i = pl.program_id(0)

        @pl.when(i == 0)
        def _():
            dg_s[tr:, :] = jnp.zeros((HALO, D_FF), F32)
            dw_ref[...] = jnp.zeros_like(dw_ref)
            dcb_ref[...] = jnp.zeros_like(dcb_ref)
            dg1_ref[...] = jnp.zeros_like(dg1_ref)
            db1_ref[...] = jnp.zeros_like(db1_ref)

        dz2 = dz2_ref[...]

        @pl.when(i == 0)
        def _():
            dz2_b = dz2.astype(BF16)
            for c in range(nblk):
                cs = slice(c * FFN_COLS, (c + 1) * FFN_COLS)
                dh_s[:, cs] = lax.dot_general(dz2_b, wd_ref[cs, :], _NT, preferred_element_type=F32)

        dz2_next = dz2_next_ref[...].astype(BF16)
        dh_next = [lax.dot_general(dz2_next, wd_ref[c * FFN_COLS:(c + 1) * FFN_COLS, :], _NT,
                                   preferred_element_type=F32) for c in range(nblk)]
        head[0:HALO, :] = jnp.where(i == nt - 1, 0.0, gp_prev_ref[...])
        head[HALO:, :] = u2_ref[0:rb, 0:D_FF]

        acc = jnp.zeros((tr, d), F32)
        for blk in range(nblk):
            for c in range(blk * FFN_COLS // LANES, (blk + 1) * FFN_COLS // LANES):
                ln = slice(c * LANES, (c + 1) * LANES)
                vl = slice(D_FF + c * LANES, D_FF + (c + 1) * LANES)
                w0, w1, w2, bb = cw_ref[0:1, ln], cw_ref[1:2, ln], cw_ref[2:3, ln], cb_ref[:, ln]
                acc_b = jnp.zeros((8, LANES), F32)
                acc_w = [jnp.zeros((8, LANES), F32) for _ in range(3)]
                for r0 in range(0, tr, rb):
                    win = head[:, ln] if r0 == 0 else u2_ref[r0 - HALO:r0 + rb, ln]
                    g_m2, g_m1, g_0 = _rows_before(win, 2), _rows_before(win, 1), win[HALO:]
                    gate = g_m2 * w0 + g_m1 * w1 + g_0 * w2 + bb
                    sg = _sig(gate)
                    dh = dh_s[r0:r0 + rb, ln]
                    dgate = dh * u2_ref[r0:r0 + rb, vl] * _dsilu(gate, sg)
                    dg_s[r0:r0 + rb, ln] = dgate
                    du_ref[r0:r0 + rb, vl] = (dh * (gate * sg)).astype(du_ref.dtype)
                    acc_b = acc_b + _sum8(dgate)
                    acc_w[0] = acc_w[0] + _sum8(dgate * g_m2)
                    acc_w[1] = acc_w[1] + _sum8(dgate * g_m1)
                    acc_w[2] = acc_w[2] + _sum8(dgate * g_0)
                dcb_ref[:, ln] += jnp.sum(acc_b, axis=0, keepdims=True)
                for j in range(3):
                    dw_ref[j:j + 1, ln] += jnp.sum(acc_w[j], axis=0, keepdims=True)
                for r0 in range(0, tr, rb):
                    win = dg_s[r0:r0 + rb + HALO, ln]
                    d_gp = _rows_after(win, 2) * w0 + _rows_after(win, 1) * w1 + win[0:rb] * w2
                    du_ref[r0:r0 + rb, ln] = d_gp.astype(du_ref.dtype)
            cs = slice(blk * FFN_COLS, (blk + 1) * FFN_COLS)
            vs = slice(D_FF + blk * FFN_COLS, D_FF + (blk + 1) * FFN_COLS)
            acc = acc + lax.dot_general(du_ref[:, cs], wup_ref[cs, :], _NN, preferred_element_type=F32)
            acc = acc + lax.dot_general(du_ref[:, vs], wup_ref[vs, :], _NN, preferred_element_type=F32)
        dg_s[tr:, :] = dg_s[0:HALO, :]
        for c in range(nblk):
            dh_s[:, c * FFN_COLS:(c + 1) * FFN_COLS] = dh_next[c]
        dy = acc + ALPHA * dz2
        xh = xhat_ref[...]
        dz1_ref[...] = _ln_bwd_math(dy, xh, rstd_ref[...], g1_ref[...])
        dg1_ref[...] += jnp.sum(dy * xh, axis=0, keepdims=True)
        db1_ref[...] += jnp.sum(dy, axis=0, keepdims=True)

    rev = lambda w: pl.BlockSpec((tr, w), lambda i: (nt - 1 - i, 0))
    vec = pl.BlockSpec((1, d), lambda i: (0, 0))
    return _call(
        body, name=name, grid=(nt,),
        ins=[dz2, dz2, u2, u2, w_down, w_up_t, conv_w, conv_b, xhat1, rstd1, ln1_g],
        in_specs=[rev(d), pl.BlockSpec((tr, d), lambda i: (jnp.maximum(nt - 2 - i, 0), 0)), rev(2 * D_FF),
                  pl.BlockSpec((HALO, D_FF), lambda i: (jnp.maximum((nt - 1 - i) * hb - 1, 0), 0)),
                  _resident((D_FF, d)), _resident((2 * D_FF, d)), _resident((3, D_FF)), _resident((1, D_FF)),
                  rev(d), pl.BlockSpec((tr, 1), lambda i: (nt - 1 - i, 0)), vec],
        out_specs=[rev(2 * D_FF), rev(d), pl.BlockSpec((8, D_FF), lambda i: (0, 0)),
                   pl.BlockSpec((1, D_FF), lambda i: (0, 0)), vec, vec],
        out_shape=[jax.ShapeDtypeStruct((t, 2 * D_FF), BF16), jax.ShapeDtypeStruct((t, d), F32),
                   jax.ShapeDtypeStruct((8, D_FF), F32), jax.ShapeDtypeStruct((1, D_FF), F32),
                   jax.ShapeDtypeStruct((1, d), F32), jax.ShapeDtypeStruct((1, d), F32)],
        scratch_shapes=[pltpu.VMEM((HALO + rb, D_FF), F32), pltpu.VMEM((tr, D_FF), F32),
                        pltpu.VMEM((tr + HALO, D_FF), F32)],
        sem=("arbitrary",), comm=comm)


def _pad_rows(a, rows):
    return jnp.pad(a, ((0, rows - a.shape[0]), (0, 0)))


SMALL_LAYOUT = (("ln1_g", 1024), ("ln1_b", 1024), ("b_in", 2816), ("sinks", 8), ("hgrn_lb", 1024),
                ("hgrn_norm_g", 128), ("ln2_g", 1024), ("ln2_b", 1024), ("conv_b", 2816), ("loss", 1))
SMALL_SHAPES = {"ln1_g": (1, 1024), "ln1_b": (1, 1024), "b_in": (1, 2816), "sinks": (1, 8), "hgrn_lb": (2, 512),
                "hgrn_norm_g": (1, 128), "ln2_g": (1, 1024), "ln2_b": (1, 1024), "conv_b": (1, 2816),
                "loss": (1,)}


def _pack_small(parts):
    rows = []
    for name, size in SMALL_LAYOUT:
        flat = parts[name].reshape(-1).astype(F32)
        padded = -(-size // LANES) * LANES
        rows.append(jnp.pad(flat, (0, padded - size)).reshape(-1, LANES))
    return _pad_rows(jnp.concatenate(rows, axis=0), SMALL_ROWS)


def _small_update(small_g, ws, ms, vs, name):
    names = [n for n, _ in SMALL_LAYOUT if n != "loss"]
    first, r = {}, 0
    for n, size in SMALL_LAYOUT:
        first[n] = r
        r += -(-size // LANES)
    npar = len(names)

    def body(*refs):
        g_ref = refs[0]
        w_refs, m_refs, v_refs = (refs[1 + q * npar:1 + (q + 1) * npar] for q in range(3))
        outs = refs[1 + 3 * npar:-1]
        sum_ref = refs[-1]
        acc = g_ref[0]
        for s in range(1, N_DEV):
            acc = acc + g_ref[s]
        sum_ref[...] = acc
        outs[0][...] = sum_ref[first["loss"]:first["loss"] + 1, 0:1]
        for p, n in enumerate(names):
            g_out, d_out, m_out, v_out = outs[1 + 4 * p:5 + 4 * p]
            rows, cols = SMALL_SHAPES[n]
            if cols < LANES:
                g_out[...] = sum_ref[first[n]:first[n] + 1, 0:cols]
            else:
                per = cols // LANES
                for h in range(rows):
                    for j in range(per):
                        rr = first[n] + h * per + j
                        g_out[h:h + 1, j * LANES:(j + 1) * LANES] = sum_ref[rr:rr + 1, :]
            d_out[...], m_out[...], v_out[...] = _adamw_math(w_refs[p][...], g_out[...], m_refs[p][...],
                                                            v_refs[p][...])

    out_shape = [jax.ShapeDtypeStruct((1, 1), F32)]
    for n in names:
        out_shape += [jax.ShapeDtypeStruct(SMALL_SHAPES[n], F32)] * 4
    res = pl.pallas_call(
        body, name=name, out_shape=out_shape,
        scratch_shapes=[pltpu.VMEM((SMALL_ROWS, LANES), F32)],
        compiler_params=_cp(),
    )(small_g, *[ws[n] for n in names], *[ms[n] for n in names], *[vs[n] for n in names])
    return res[0], {n: res[1 + 4 * p:5 + 4 * p] for p, n in enumerate(names)}


def _conv_w_update(recv, w, m, v, name):
    taps, cols = w.shape

    def body(r_ref, w_ref, m_ref, v_ref, g_ref, d_ref, nm_ref, nv_ref):
        acc = r_ref[0]
        for s in range(1, N_DEV):
            acc = acc + r_ref[s]
        g = acc[0:taps]
        res = (g,) + _adamw_math(w_ref[...], g, m_ref[...], v_ref[...])
        for o_ref, val in zip((g_ref, d_ref, nm_ref, nv_ref), res):
            for k in range(taps):
                o_ref[k] = val[k:k + 1]

    shp = jax.ShapeDtypeStruct((taps, 1, cols), F32)
    outs = pl.pallas_call(body, name=name, out_shape=[shp, shp, shp, shp], compiler_params=_cp())(recv, w, m, v)
    return [o.transpose(1, 0, 2) for o in outs]


def kernel(x, positions, ln1_g, ln1_b, w_in, b_in, sinks, hgrn_lb, hgrn_norm_g, w_o, ln2_g, ln2_b, w_up, conv_w, conv_b, w_down, loss_target, m_ln1_g, m_ln1_b, m_w_in, m_b_in, m_sinks, m_hgrn_lb, m_hgrn_norm_g, m_w_o, m_ln2_g, m_ln2_b, m_w_up, m_conv_w, m_conv_b, m_w_down, v_ln1_g, v_ln1_b, v_w_in, v_b_in, v_sinks, v_hgrn_lb, v_hgrn_norm_g, v_w_o, v_ln2_g, v_ln2_b, v_w_up, v_conv_w, v_conv_b, v_w_down):
    t = x.shape[1]
    x2 = x[0]
    target = loss_target[0]
    pos_col = positions.reshape(t, 1)

    w_in_t_s = w_in[0].T.astype(BF16)
    w_up_t_s = w_up[0].T.astype(BF16)
    w_o_s = w_o[0].astype(BF16)
    w_down_s = w_down[0].astype(BF16)
    (ctab, stab, xb), (w_in_t_g, cw_g) = _prep(
        pos_col, x2, "prep_ag_w_in", _Comm([{"kind": "gather", "arr": w_in_t_s}, {"kind": "gather", "arr": _pad_rows(conv_w[0], 8)}]))
    w_in_t = w_in_t_g.reshape(D_FF, D_MODEL)
    w_a_t, w_h_t = w_in_t[:UA_W], w_in_t[UA_W:]
    conv_w_f = cw_g[:, 0:3].transpose(1, 0, 2).reshape(3, D_FF)

    ua = _mm(xb, w_a_t, tb=True, bias=b_in[:, :UA_W], name="fwd_in_attn")
    uh = _mm(xb, w_h_t, tb=True, bias=b_in[:, UA_W:], name="fwd_in_hgrn")
    half_up = SHARD_UP // 2
    (a_out, a_out_t), (w_o_g, w_up_half) = _attn_fwd(
        ua, ctab, stab, sinks, "attn_fwd",
        comm=_Comm([{"kind": "gather", "arr": w_o_s},
                    {"kind": "gather", "arr": w_up_t_s, "rows": (0, half_up), "dst_rows": SHARD_UP}]))
    (r_out, r_out_t, o_pre, states), (w_up_t_g, w_down_g) = _hgrn_fwd(
        uh, hgrn_lb, hgrn_norm_g, "hgrn_fwd",
        comm=_Comm([{"kind": "gather", "arr": w_up_t_s, "rows": (half_up, half_up), "dst_rows": SHARD_UP,
                     "dst_first": half_up, "into": w_up_half},
                    {"kind": "gather", "arr": w_down_s}]))
    w_down_f = w_down_g.reshape(D_FF, D_MODEL)
    w_o_f = w_o_g.reshape(D_MODEL, D_MODEL)
    w_up_t = w_up_t_g.reshape(2 * D_FF, D_MODEL)
    h1, h1b, xhat1, rstd1 = _mm_ln_fwd(r_out, w_o_f[ATTN_W:], (a_out, w_o_f[:ATTN_W]), x2, ln1_g, ln1_b,
                                       "fwd_o_ln1")
    u2, hmid, dz2, d_ln2_g, d_ln2_b, loss_part = _ffn_fwd(h1b, h1, w_up_t, conv_w_f, conv_b, w_down_f, target,
                                                         ln2_g, ln2_b, "ffn_fwd")[0]

    d_w_down, d_w_down_b = _mm(hmid, dz2, ta=True, out_dtype2=BF16, tm=1408, tk=1024, name="bwd_down_dw")
    (d_u2, dz1, d_conv_w8, d_conv_b, d_ln1_g, d_ln1_b), (recv_down,) = _ffn_bwd(
        dz2, u2, w_down_f, w_up_t, conv_w_f, conv_b, xhat1, rstd1, ln1_g, "ffn_bwd",
        comm=_Comm([{"kind": "exchange", "arr": d_w_down_b.reshape(N_DEV, SHARD_DOWN, D_MODEL)}]))
    d_w_up_t, d_w_up_t_b = _mm(d_u2, h1b, ta=True, out_dtype2=BF16, tm=1408, tk=1024, name="bwd_up_dw")
    d_ar = _mm(dz1, w_o_f, tb=True, name="bwd_o_dx")
    d_w_o, d_w_o_b = _mm_stacked(a_out_t, r_out_t, dz1, name="bwd_o_dw")
    d_w_up_x = d_w_up_t_b.reshape(N_DEV, SHARD_UP, D_MODEL)
    half = SHARD_UP // 2
    d_cw_x = d_conv_w8.reshape(8, N_DEV, SHARD_IN).transpose(1, 0, 2)
    (d_ua, d_ua_t, d_bias_a, d_sinks), (recv_up_half, recv_cw) = _attn_bwd(
        ua, d_ar, ctab, stab, sinks, "attn_bwd",
        comm=_Comm([{"kind": "exchange", "arr": d_w_up_x, "rows": (0, half), "dst_rows": SHARD_UP},
                    {"kind": "exchange", "arr": d_cw_x}]))
    (d_uh, d_uh_t, d_bias_h, d_norm_g, d_lb8), (recv_up, recv_o) = _hgrn_bwd(
        uh, o_pre, d_ar, states, hgrn_lb, hgrn_norm_g, "hgrn_bwd",
        comm=_Comm([{"kind": "exchange", "arr": d_w_up_x, "rows": (half, half), "dst_rows": SHARD_UP,
                     "dst_first": half, "into": recv_up_half},
                    {"kind": "exchange", "arr": d_w_o_b.reshape(N_DEV, SHARD_O, D_MODEL)}]))
    d_w_in_part = _mm(d_ua_t, xb, out_dtype2=BF16, tm=UA_W, tk=t, out_rows=D_FF, name="bwd_in_dw_attn")
    d_w_in_t, d_w_in_t_b = _mm(d_uh_t, xb, out_dtype2=BF16, tm=256, tk=t, out_rows=D_FF, first_row=UA_W,
                               into=d_w_in_part, name="bwd_in_dw_hgrn")
    small_local = _pack_small({
        "ln1_g": d_ln1_g, "ln1_b": d_ln1_b, "b_in": jnp.concatenate([d_bias_a, d_bias_h], axis=1),
        "sinks": d_sinks[:, :8], "hgrn_lb": d_lb8[0:2], "hgrn_norm_g": d_norm_g, "ln2_g": d_ln2_g,
        "ln2_b": d_ln2_b, "conv_b": d_conv_b, "loss": loss_part[:, :1]})
    d_w_in_x = d_w_in_t_b.reshape(N_DEV, SHARD_IN, D_MODEL)
    res_up, (from_sibling,) = _sum_shards_adamw(
        [recv_up], d_w_up_t, w_up[0].T, m_w_up[0].T, v_w_up[0].T, "adamw_w_up",
        comm=_Comm([{"kind": "pair4", "arr": d_w_in_x}]))
    res_up = [r.T for r in res_up]
    own_in, chip_part = _pair_reduce(from_sibling, d_w_in_t, "pair_reduce_w_in")
    dx, (from_chips, small_g) = _mm(d_uh, w_h_t, addend=dz1, addend_scale=ALPHA, name="bwd_in_dx_hgrn",
                                    comm=_Comm([{"kind": "chips3", "arr": chip_part},
                                                {"kind": "gather", "arr": small_local}]))
    dx = _mm(d_ua, w_a_t, addend=dx, tk=768, name="bwd_in_dx_attn")

    res_in = [r.T for r in _chip_sum_adamw(from_chips, own_in, w_in[0].T, m_w_in[0].T, v_w_in[0].T, "adamw_w_in")]
    res_o = _sum_shards_adamw([recv_o], d_w_o, w_o[0], m_w_o[0], v_w_o[0], "adamw_w_o")
    res_down = _sum_shards_adamw([recv_down], d_w_down, w_down[0], m_w_down[0], v_w_down[0],
                                 "adamw_w_down")
    res_cw = _conv_w_update(recv_cw, conv_w[0], m_conv_w[0], v_conv_w[0], "adamw_conv_w")
    big = {"w_in": [r[None] for r in res_in], "w_up": [r[None] for r in res_up],
           "w_o": [r[None] for r in res_o], "w_down": [r[None] for r in res_down],
           "conv_w": list(res_cw)}

    loss11, small = _small_update(
        small_g,
        {"ln1_g": ln1_g, "ln1_b": ln1_b, "b_in": b_in, "sinks": sinks, "hgrn_lb": hgrn_lb,
         "hgrn_norm_g": hgrn_norm_g, "ln2_g": ln2_g, "ln2_b": ln2_b, "conv_b": conv_b},
        {"ln1_g": m_ln1_g, "ln1_b": m_ln1_b, "b_in": m_b_in, "sinks": m_sinks, "hgrn_lb": m_hgrn_lb,
         "hgrn_norm_g": m_hgrn_norm_g, "ln2_g": m_ln2_g, "ln2_b": m_ln2_b, "conv_b": m_conv_b},
        {"ln1_g": v_ln1_g, "ln1_b": v_ln1_b, "b_in": v_b_in, "sinks": v_sinks, "hgrn_lb": v_hgrn_lb,
         "hgrn_norm_g": v_hgrn_norm_g, "ln2_g": v_ln2_g, "ln2_b": v_ln2_b, "conv_b": v_conv_b},
        "adamw_small")
    loss = loss11[0, 0]

    order = ["ln1_g", "ln1_b", "w_in", "b_in", "sinks", "hgrn_lb", "hgrn_norm_g", "w_o", "ln2_g", "ln2_b",
             "w_up", "conv_w", "conv_b", "w_down"]

    def pick(idx):
        return [big[n][idx] if n in big else small[n][idx] for n in order]

    return (loss, dx[None], *pick(0), *pick(1), *pick(2), *pick(3))
```

```python
import functools

import jax
import jax.numpy as jnp
import numpy as np
from jax import lax
from jax.experimental import pallas as pl
from jax.experimental.pallas import tpu as pltpu

F32 = jnp.float32
BF16 = jnp.bfloat16

N_DEV = 8
D_MODEL = 1024
D_FF = 2816
ATTN_W = 512
KV_W = 128
UA_W = ATTN_W + 2 * KV_W
UH_W = 2048
HG_W = 512
ATTN_BLOCK = 128
HGRN_CHUNK = 64
HGRN_SUB = 16
HGRN_CHUNKS_PER_STEP = 4
EXP_CLAMP = 85.0
NEG_BIG = -1e30
LN_EPS = 1e-5
RMS_EPS = 1e-6
ALPHA = 2.0 ** 0.25
ATTN_SCALE = 0.125
ROPE_THETA = 500000.0

ADAM_LR = 0.001
ADAM_B1 = 0.9
ADAM_B2 = 0.999
ADAM_EPS = 1e-08
ADAM_WD = 0.01
ADAM_STEP = 10

LANES = 128
VMEM_LIMIT_BYTES = 56 * 1024 * 1024

SHARD_IN = D_FF // N_DEV
SHARD_UP = 2 * D_FF // N_DEV
SHARD_O = D_MODEL // N_DEV
SHARD_DOWN = D_FF // N_DEV
SMALL_ROWS = 88

_MESH = pl.DeviceIdType.MESH
_NT = (((1,), (1,)), ((), ()))
_NN = (((1,), (0,)), ((), ()))
_TN = (((0,), (0,)), ((), ()))


def _cp(*sem):
    if sem:
        return pltpu.CompilerParams(dimension_semantics=sem, vmem_limit_bytes=VMEM_LIMIT_BYTES)
    return pltpu.CompilerParams(vmem_limit_bytes=VMEM_LIMIT_BYTES)


def _sig(x):
    return 0.5 * jnp.tanh(0.5 * x) + 0.5


def _dsilu(x, s):
    return s * (1.0 + x * (1.0 - s))


def _dot(a, b, dims):
    return lax.dot_general(a.astype(BF16), b.astype(BF16), dims, preferred_element_type=F32)


def _split(a):
    hi = a.astype(BF16)
    return hi, (a - hi.astype(F32)).astype(BF16)


def _dot3(a, b, dims):
    ah, al = _split(a)
    bh, bl = _split(b)
    d = functools.partial(lax.dot_general, dimension_numbers=dims, preferred_element_type=F32)
    return d(ah, bh) + (d(ah, bl) + d(al, bh))


def _pick(n, pref):
    for t in pref:
        if t <= n and n % t == 0:
            return t
    return n


def _my_coords():
    return lax.axis_index("x"), lax.axis_index("y"), lax.axis_index("c")


def _peer(k):
    x, y, c = _my_coords()
    return (1 - x if k & 4 else x, 1 - y if k & 2 else y, 1 - c if k & 1 else c)


def _me():
    x, y, c = _my_coords()
    return 4 * x + 2 * y + c


class _Comm:
    def __init__(self, items):
        self.items = []
        for it in items:
            arr = it["arr"]
            full = arr.shape[0] if it["kind"] == "gather" else arr.shape[1]
            first, count = it.get("rows", (0, full))
            self.items.append(dict(kind=it["kind"], arr=arr, first=first, count=count,
                                   dst_rows=it.get("dst_rows", count), dst_first=it.get("dst_first", 0),
                                   into=it.get("into")))
        self.n = len(self.items)
        self.arrays = [it["arr"] for it in self.items]
        self.intos = [(a, it["into"]) for a, it in enumerate(self.items) if it["into"] is not None]

    def out_shapes(self):
        return [jax.ShapeDtypeStruct((4 if it["kind"] in ("pair4", "chips3") else N_DEV, it["dst_rows"],
                                      it["arr"].shape[-1]), it["arr"].dtype) for it in self.items]

    def specs(self, n=None):
        return [pl.BlockSpec(memory_space=pl.ANY)] * (self.n if n is None else n)

    def scratch(self):
        return [pltpu.SemaphoreType.DMA(((N_DEV - 1) * self.n,)), pltpu.SemaphoreType.DMA(((N_DEV - 1) * self.n,)),
                pltpu.SemaphoreType.DMA((self.n,))]

    def _src(self, a, ref, dev):
        it = self.items[a]
        blk = ref if it["kind"] == "gather" else ref.at[dev]
        return blk.at[pl.ds(it["first"], it["count"])]

    def _dst(self, a, ref, slot):
        it = self.items[a]
        return ref.at[slot].at[pl.ds(it["dst_first"], it["count"])]

    def _copy(self, a, k, src, dst, sems, me, slot):
        other = jnp.bitwise_xor(me, k)
        idx = a * (N_DEV - 1) + k - 1
        return pltpu.make_async_remote_copy(
            src_ref=self._src(a, src, other), dst_ref=self._dst(a, dst, me if slot == "mine" else other),
            send_sem=sems[0].at[idx], recv_sem=sems[1].at[idx], device_id=_peer(k), device_id_type=_MESH)

    def _pass_on(self, a, k, dst, sems, me):
        slot = self._dst(a, dst, jnp.bitwise_xor(me, k))
        idx = a * (N_DEV - 1) + k
        return pltpu.make_async_remote_copy(
            src_ref=slot, dst_ref=slot, send_sem=sems[0].at[idx], recv_sem=sems[1].at[idx],
            device_id=_peer(1), device_id_type=_MESH)

    def _part(self, a, r, src, dst, sems, me):
        it = self.items[a]
        idx = a * (N_DEV - 1) + r
        if it["kind"] == "pair4":
            k, slot = 1, jnp.bitwise_xor(jnp.bitwise_xor(me, 1), 2 * r)
        else:
            k, slot = 2 * r, r
        return pltpu.make_async_remote_copy(
            src_ref=src.at[slot].at[pl.ds(it["first"], it["count"])], dst_ref=self._dst(a, dst, r),
            send_sem=sems[0].at[idx], recv_sem=sems[1].at[idx], device_id=_peer(k), device_id_type=_MESH)

    def _parts(self, a):
        return range(4) if self.items[a]["kind"] == "pair4" else range(1, 4)

    def _local(self, a, src, dst, sems, me):
        return pltpu.make_async_copy(self._src(a, src, me), self._dst(a, dst, me), sems[2].at[a])

    def start(self, srcs, dsts, sems):
        me = _me()
        for a, (src, dst) in enumerate(zip(srcs, dsts)):
            if self.items[a]["kind"] in ("pair4", "chips3"):
                for r in self._parts(a):
                    self._part(a, r, src, dst, sems, me).start()
                continue
            direct = (1, 2, 4, 6) if self.items[a]["kind"] == "gather" else range(1, N_DEV)
            self._local(a, src, dst, sems, me).start()
            for k in direct:
                self._copy(a, k, src, dst, sems, me, "mine").start()

    def wait(self, srcs, dsts, sems):
        me = _me()
        for a, (src, dst) in enumerate(zip(srcs, dsts)):
            if self.items[a]["kind"] in ("pair4", "chips3"):
                for r in self._parts(a):
                    self._part(a, r, src, dst, sems, me).wait_recv()
                for r in self._parts(a):
                    self._part(a, r, src, dst, sems, me).wait_send()
                continue
            if self.items[a]["kind"] == "gather":
                for k in (2, 4, 6):
                    self._copy(a, k, src, dst, sems, me, "theirs").wait_recv()
                    self._pass_on(a, k, dst, sems, me).start()
                for k in (1, 3, 5, 7):
                    self._copy(a, k, src, dst, sems, me, "theirs").wait_recv()
                for k in (1, 2, 4, 6):
                    self._copy(a, k, src, dst, sems, me, "mine").wait_send()
                for k in (2, 4, 6):
                    self._pass_on(a, k, dst, sems, me).wait_send()
            else:
                for k in range(1, N_DEV):
                    self._copy(a, k, src, dst, sems, me, "theirs").wait_recv()
                for k in range(1, N_DEV):
                    self._copy(a, k, src, dst, sems, me, "mine").wait_send()
            self._local(a, src, dst, sems, me).wait()


def _call(body, *, name, grid, ins, in_specs, out_specs, out_shape, scratch_shapes=(), sem, comm=None):
    n_in, n_out, n_scr = len(ins), len(out_shape), len(scratch_shapes)
    if comm is None:
        outs = pl.pallas_call(
            body, name=name, grid=grid, in_specs=list(in_specs), out_specs=list(out_specs),
            out_shape=list(out_shape), scratch_shapes=list(scratch_shapes), compiler_params=_cp(*sem))(*ins)
        return list(outs), []
    nc, n_into = comm.n, len(comm.intos)

    def hosted(*refs):
        pos = n_in
        c_in = refs[pos:pos + nc]
        pos += nc + n_into
        outs = refs[pos:pos + n_out]
        pos += n_out
        c_out = refs[pos:pos + nc]
        pos += nc
        scr = refs[pos:pos + n_scr]
        sems = refs[pos + n_scr:]
        ids = [pl.program_id(d) for d in range(len(grid))]
        first = functools.reduce(jnp.logical_and, [i == 0 for i in ids])
        last = functools.reduce(jnp.logical_and, [i == g - 1 for i, g in zip(ids, grid)])

        @pl.when(first)
        def _():
            comm.start(c_in, c_out, sems)

        body(*refs[:n_in], *outs, *scr)

        @pl.when(last)
        def _():
            comm.wait(c_in, c_out, sems)

    aliases = {n_in + nc + j: n_out + a for j, (a, _) in enumerate(comm.intos)}
    outs = pl.pallas_call(
        hosted, name=name, grid=grid, in_specs=list(in_specs) + comm.specs() + comm.specs(n_into),
        out_specs=list(out_specs) + comm.specs(), out_shape=list(out_shape) + comm.out_shapes(),
        scratch_shapes=list(scratch_shapes) + comm.scratch(), input_output_aliases=aliases,
        compiler_params=_cp(*(["arbitrary"] * len(grid))))(*ins, *comm.arrays, *[arr for _, arr in comm.intos])
    return list(outs[:n_out]), list(outs[n_out:])


def _slot_sum(recv_ref, own_ref, shape):
    me = _me()
    acc = jnp.zeros(shape, F32)
    for s in range(N_DEV):
        acc = acc + jnp.where(me == s, own_ref[...], recv_ref[s].astype(F32))
    return acc


def _adamw_math(w, g, m, v):
    nm = ADAM_B1 * m + (1.0 - ADAM_B1) * g
    nv = ADAM_B2 * v + (1.0 - ADAM_B2) * (g * g)
    m_hat = nm / (1.0 - ADAM_B1 ** ADAM_STEP)
    v_hat = nv / (1.0 - ADAM_B2 ** ADAM_STEP)
    return -ADAM_LR * (m_hat / (jnp.sqrt(v_hat) + ADAM_EPS) + ADAM_WD * w), nm, nv


def _pair_reduce(from_sibling, mine, name):
    _, rows, cols = from_sibling.shape
    tr = _pick(rows, (176, 128, 64, 32, 16, 8))
    tiles = rows // tr
    table = jnp.bitwise_xor(_me(), jnp.arange(0, N_DEV, 2, dtype=jnp.int32))

    def body(tbl_ref, sib_ref, mine_ref, own_ref, send_ref):
        r = pl.program_id(1)
        total = mine_ref[...] + sib_ref[0].astype(F32)
        send_ref[0] = jnp.where(r == 0, 0.0, total).astype(BF16)

        @pl.when(r == 0)
        def _():
            own_ref[...] = total

    grid_spec = pltpu.PrefetchScalarGridSpec(
        num_scalar_prefetch=1, grid=(tiles, 4),
        in_specs=[pl.BlockSpec((1, tr, cols), lambda i, r, tbl: (r, i, 0)),
                  pl.BlockSpec((tr, cols), lambda i, r, tbl: (tbl[r] * tiles + i, 0))],
        out_specs=[pl.BlockSpec((tr, cols), lambda i, r, tbl: (i, 0)),
                   pl.BlockSpec((1, tr, cols), lambda i, r, tbl: (r, i, 0))])
    return pl.pallas_call(
        body, name=name, grid_spec=grid_spec,
        out_shape=[jax.ShapeDtypeStruct((rows, cols), F32), jax.ShapeDtypeStruct((4, rows, cols), BF16)],
        compiler_params=_cp("arbitrary", "arbitrary"),
    )(table, from_sibling, mine)


def _chip_sum_adamw(from_chips, own, w, m, v, name):
    _, rows, cols = from_chips.shape
    tr = _pick(rows, (176, 128, 64, 32, 16, 8))

    def body(recv_ref, own_ref, w_ref, m_ref, v_ref, g_ref, d_ref, nm_ref, nv_ref):
        g = own_ref[...]
        for r in range(1, 4):
            g = g + recv_ref[r].astype(F32)
        g_ref[...] = g
        d_ref[...], nm_ref[...], nv_ref[...] = _adamw_math(w_ref[...], g, m_ref[...], v_ref[...])

    spec = pl.BlockSpec((tr, cols), lambda i: (i, 0))
    shp = jax.ShapeDtypeStruct((rows, cols), F32)
    return pl.pallas_call(
        body, name=name, grid=(rows // tr,),
        in_specs=[pl.BlockSpec((4, tr, cols), lambda i: (0, i, 0)), spec, spec, spec, spec],
        out_specs=[spec, spec, spec, spec], out_shape=[shp, shp, shp, shp],
        compiler_params=_cp("parallel"),
    )(from_chips, own, w, m, v)


def _sum_shards_adamw(recvs, own, w, m, v, name, comm=None):
    rows_p, cols = recvs[0].shape[1], recvs[0].shape[2]
    n_p = len(recvs)
    tr = _pick(rows_p, (176, 128, 64, 32, 16, 8))
    tiles = rows_p // tr

    def body(*refs):
        recv_refs = refs[:n_p]
        own_ref, w_ref, m_ref, v_ref, g_ref, d_ref, nm_ref, nv_ref = refs[n_p:]
        for j in range(n_p):
            @pl.when(pl.program_id(0) == j)
            def _():
                g = _slot_sum(recv_refs[j], own_ref, (tr, cols))
                g_ref[...] = g
                d_ref[...], nm_ref[...], nv_ref[...] = _adamw_math(w_ref[...], g, m_ref[...], v_ref[...])

    spec = pl.BlockSpec((tr, cols), lambda p_, i: (p_ * tiles + i, 0))
    own_spec = pl.BlockSpec((tr, cols), lambda p_, i: (_me() * (n_p * tiles) + p_ * tiles + i, 0))
    shp = jax.ShapeDtypeStruct((rows_p * n_p, cols), F32)
    outs, couts = _call(
        body, name=name, grid=(n_p, tiles), ins=[*recvs, own, w, m, v],
        in_specs=[pl.BlockSpec((N_DEV, tr, cols), functools.partial(lambda p_, i, j: (0, jnp.where(p_ == j, i, 0), 0), j=j))
                  for j in range(n_p)] + [own_spec, spec, spec, spec],
        out_specs=[spec, spec, spec, spec], out_shape=[shp, shp, shp, shp],
        sem=("arbitrary", "arbitrary"), comm=comm)
    return outs if comm is None else (outs, couts)


def _mm(a, b, *, name, ta=False, tb=False, out_dtype=F32, out_dtype2=None, bias=None, addend=None,
        addend_scale=1.0, tm=1024, tn=1024, tk=1024, comm=None, out_rows=None, first_row=0, into=None,
        b_first=0):
    kdim, m = a.shape if ta else a.shape[::-1]
    n = b.shape[0] if tb else b.shape[1]
    tm = _pick(m, (tm, 1408, 1024, 768, 512, 256, 128))
    tn = _pick(n, (tn, 1408, 1024, 768, 512, 256, 128))
    tk = _pick(kdim, (tk, 1408, 1024, 768, 512, 256, 128))
    nk = kdim // tk
    a_spec = pl.BlockSpec((tk, tm), lambda i, j, k: (k, i)) if ta else pl.BlockSpec((tm, tk), lambda i, j, k: (i, k))
    assert b_first % tk == 0 and not (tb and b_first)
    kb0 = b_first // tk
    b_spec = pl.BlockSpec((tn, tk), lambda i, j, k: (j, k)) if tb else pl.BlockSpec((tk, tn), lambda i, j, k: (k + kb0, j))
    ins, specs = [a, b], [a_spec, b_spec]
    if bias is not None:
        ins.append(bias)
        specs.append(pl.BlockSpec((1, tn), lambda i, j, k: (0, j)))
    if addend is not None:
        ins.append(addend)
        specs.append(pl.BlockSpec((tm, tn), lambda i, j, k: (i, j)))
    dims = (((0,) if ta else (1,), (1,) if tb else (0,)), ((), ()))
    has_bias, has_addend, two = bias is not None, addend is not None, out_dtype2 is not None

    def body(*refs):
        a_ref, b_ref = refs[0], refs[1]
        pos = 2
        bias_ref = addend_ref = None
        if has_bias:
            bias_ref = refs[pos]
            pos += 1
        if has_addend:
            addend_ref = refs[pos]
            pos += 1
        o_refs, acc_ref = refs[pos:-1], refs[-1]
        k = pl.program_id(2)

        @pl.when(k == 0)
        def _():
            acc_ref[...] = jnp.zeros_like(acc_ref)

        acc_ref[...] += _dot(a_ref[...], b_ref[...], dims)

        @pl.when(k == nk - 1)
        def _():
            r = acc_ref[...]
            if has_bias:
                r = r + bias_ref[...]
            if has_addend:
                r = r + addend_scale * addend_ref[...].astype(F32)
            for o_ref in o_refs:
                o_ref[...] = r.astype(o_ref.dtype)

    blk0 = first_row // tm
    dtypes = [out_dtype] + ([out_dtype2] if two else [])
    ospec = pl.BlockSpec((tm, tn), lambda i, j, k: (i + blk0, j))
    shapes = [jax.ShapeDtypeStruct((m if out_rows is None else out_rows, n), d) for d in dtypes]
    if into is not None:
        n_in = len(ins)
        outs = pl.pallas_call(
            lambda *refs: body(*refs[:n_in], *refs[n_in + len(into):]), name=name, grid=(m // tm, n // tn, nk),
            in_specs=specs + [pl.BlockSpec(memory_space=pl.ANY)] * len(into), out_specs=[ospec] * len(dtypes),
            out_shape=shapes, scratch_shapes=[pltpu.VMEM((tm, tn), F32)],
            input_output_aliases={n_in + j: j for j in range(len(into))},
            compiler_params=_cp("parallel", "parallel", "arbitrary"))(*ins, *into)
        return tuple(outs) if two else outs[0]
    outs, couts = _call(
        body, name=name, grid=(m // tm, n // tn, nk), ins=ins, in_specs=specs,
        out_specs=[ospec] * len(dtypes), out_shape=shapes,
        scratch_shapes=[pltpu.VMEM((tm, tn), F32)], sem=("parallel", "parallel", "arbitrary"), comm=comm)
    primary = tuple(outs) if two else outs[0]
    return (primary, couts) if comm is not None else primary


def _rope_lane_constants():
    inv_freq = np.float32(ROPE_THETA) ** (-np.arange(8, dtype=np.float32) * np.float32(2.0 / 16.0))
    lane = np.arange(LANES) % 64
    freq = np.where(lane < 16, inv_freq[lane % 8], 0.0).astype(np.float32)
    sign = np.where(lane < 8, -1.0, np.where(lane < 16, 1.0, 0.0)).astype(np.float32)
    return jnp.asarray(freq)[None, :], jnp.asarray(sign)[None, :]


def _prep(pos_col, x2, name, comm):
    t, d = x2.shape
    tr = _pick(t, (512, 256, 128))
    freq, sign = _rope_lane_constants()

    def body(pos_ref, freq_ref, sign_ref, x_ref, c_ref, s_ref, xb_ref):
        ang = pos_ref[...].astype(F32) * freq_ref[...]
        c_ref[...] = jnp.cos(ang)
        s_ref[...] = sign_ref[...] * jnp.sin(ang)
        xb_ref[...] = x_ref[...].astype(BF16)

    tab = pl.BlockSpec((tr, LANES), lambda i: (i, 0))
    return _call(
        body, name=name, grid=(t // tr,), ins=[pos_col, freq, sign, x2],
        in_specs=[pl.BlockSpec((tr, 1), lambda i: (i, 0)), pl.BlockSpec((1, LANES), lambda i: (0, 0)),
                  pl.BlockSpec((1, LANES), lambda i: (0, 0)), pl.BlockSpec((tr, d), lambda i: (i, 0))],
        out_specs=[tab, tab, pl.BlockSpec((tr, d), lambda i: (i, 0))],
        out_shape=[jax.ShapeDtypeStruct((t, LANES), F32), jax.ShapeDtypeStruct((t, LANES), F32),
                   jax.ShapeDtypeStruct((t, d), BF16)],
        sem=("parallel",), comm=comm)


def _swap8(t):
    width = t.shape[1]
    lane = jnp.bitwise_and(lax.broadcasted_iota(jnp.int32, t.shape, 1), 63)
    return jnp.where(lane < 8, pltpu.roll(t, width - 8, 1), jnp.where(lane < 16, pltpu.roll(t, 8, 1), 0.0))


def _rope(t, c, s):
    return t * c + _swap8(t) * s


def _rope_bwd(d, c, s):
    return d * c + _swap8(d * s)


def _tile4(a):
    return jnp.concatenate([a, a, a, a], axis=1)


def _attn_band(n, k_cur, k_prev, v_cur, v_prev, c_cur, s_cur, c_prev, s_prev):
    kband = jnp.concatenate([_rope(k_prev, c_prev, s_prev), _rope(k_cur, c_cur, s_cur)], axis=0)
    vband = jnp.concatenate([v_prev, v_cur], axis=0)
    qi = lax.broadcasted_iota(jnp.int32, (ATTN_BLOCK, 2 * ATTN_BLOCK), 0)
    kj = lax.broadcasted_iota(jnp.int32, (ATTN_BLOCK, 2 * ATTN_BLOCK), 1)
    dist = qi + ATTN_BLOCK - kj
    valid = (dist >= 0) & (dist < ATTN_BLOCK) & (n * ATTN_BLOCK - ATTN_BLOCK + kj >= 0)
    return (kband.astype(BF16), pltpu.roll(kband, 64, 1).astype(BF16),
            vband.astype(BF16), pltpu.roll(vband, 64, 1).astype(BF16), valid, kband)


def _attn_probs(raw, valid, sink, axis):
    s = jnp.where(valid, raw * ATTN_SCALE, NEG_BIG)
    m = jnp.maximum(jnp.max(s, axis=axis, keepdims=True), sink)
    p = jnp.exp(s - m)
    esink = jnp.exp(sink - m)
    z = jnp.sum(p, axis=axis, keepdims=True) + esink
    return p / z, esink / z


def _attn_valid_t(n):
    kj = lax.broadcasted_iota(jnp.int32, (2 * ATTN_BLOCK, ATTN_BLOCK), 0)
    qi = lax.broadcasted_iota(jnp.int32, (2 * ATTN_BLOCK, ATTN_BLOCK), 1)
    dist = qi + ATTN_BLOCK - kj
    return (dist >= 0) & (dist < ATTN_BLOCK) & (n * ATTN_BLOCK - ATTN_BLOCK + kj >= 0)


def _attn_specs(nb):
    def cur(col, width=KV_W):
        return pl.BlockSpec((ATTN_BLOCK, width), lambda n: (jnp.minimum(n, nb - 1), col))

    def prev(col):
        return pl.BlockSpec((ATTN_BLOCK, KV_W), lambda n: (jnp.maximum(n - 1, 0), col))

    ua_specs = [cur(0, ATTN_W), cur(4), prev(4), cur(5), prev(5)]
    tab_specs = [cur(0), cur(0), prev(0), prev(0)]
    return ua_specs, tab_specs


def _attn_fwd(ua, ctab, stab, sinks, name, comm=None):
    t = ua.shape[0]
    nb = t // ATTN_BLOCK
    ua_specs, tab_specs = _attn_specs(nb)

    def body(q_ref, kc_ref, kp_ref, vc_ref, vp_ref, cc_ref, sc_ref, cp_ref, sp_ref, sink_ref, o_ref, o_t_ref):
        n = pl.program_id(0)
        cc, sc = cc_ref[...], sc_ref[...]
        kb, kb_r, vb, vb_r, valid, _ = _attn_band(n, kc_ref[...], kp_ref[...], vc_ref[...], vp_ref[...],
                                                  cc, sc, cp_ref[...], sp_ref[...])
        qr = _rope(q_ref[...], _tile4(cc), _tile4(sc))
        lo = lax.broadcasted_iota(jnp.int32, (ATTN_BLOCK, LANES), 1) < 64
        heads = []
        for j in range(4):
            qj = qr[:, j * LANES:(j + 1) * LANES]
            for is_lo in (True, False):
                aligned = is_lo == (j < 2)
                qm = jnp.where(lo if is_lo else jnp.logical_not(lo), qj, 0.0).astype(BF16)
                raw = lax.dot_general(qm, kb if aligned else kb_r, _NT, preferred_element_type=F32)
                heads.append((raw, vb if aligned else vb_r, sink_ref[0, len(heads)]))
        halves = []
        for raw, vv, sink in heads:
            probs, _ = _attn_probs(raw, valid, sink, 1)
            halves.append(lax.dot_general(probs.astype(BF16), vv, _NN, preferred_element_type=F32))
        outs = [jnp.where(lo, halves[2 * j], halves[2 * j + 1]) for j in range(4)]
        o_ref[...] = jnp.concatenate(outs, axis=1).astype(o_ref.dtype)
        for j in range(4):
            o_t_ref[j * LANES:(j + 1) * LANES, :] = outs[j].T.astype(o_t_ref.dtype)

    return _call(
        body, name=name, grid=(nb,), ins=[ua, ua, ua, ua, ua, ctab, stab, ctab, stab, sinks],
        in_specs=ua_specs + tab_specs + [pl.BlockSpec(memory_space=pltpu.SMEM)],
        out_specs=[pl.BlockSpec((ATTN_BLOCK, ATTN_W), lambda n: (n, 0)),
                   pl.BlockSpec((ATTN_W, ATTN_BLOCK), lambda n: (0, n))],
        out_shape=[jax.ShapeDtypeStruct((t, ATTN_W), BF16), jax.ShapeDtypeStruct((ATTN_W, t), BF16)],
        sem=("parallel",), comm=comm)


def _attn_bwd(ua, d_out, ctab, stab, sinks, name, comm=None):
    t = ua.shape[0]
    nb = t // ATTN_BLOCK
    ua_specs, tab_specs = _attn_specs(nb)

    def body(q_ref, kc_ref, kp_ref, vc_ref, vp_ref, cc_ref, sc_ref, cp_ref, sp_ref, do_ref, sink_ref,
             dua_ref, dua_t_ref, dbias_ref, dsink_ref, dq_c, dk_c, dv_c, dq_n, dk_n, dv_n):
        n = pl.program_id(0)

        @pl.when(n == 0)
        def _():
            dq_c[...] = jnp.zeros_like(dq_c)
            dk_c[...] = jnp.zeros_like(dk_c)
            dv_c[...] = jnp.zeros_like(dv_c)
            dbias_ref[...] = jnp.zeros_like(dbias_ref)
            dsink_ref[...] = jnp.zeros_like(dsink_ref)

        @pl.when(n == nb)
        def _():
            dq_n[...] = jnp.zeros_like(dq_n)
            dk_n[...] = jnp.zeros_like(dk_n)
            dv_n[...] = jnp.zeros_like(dv_n)

        @pl.when(n < nb)
        def _():
            cc, sc = cc_ref[...], sc_ref[...]
            kb, kb_r, vb, vb_r, _, kb_f32 = _attn_band(n, kc_ref[...], kp_ref[...], vc_ref[...], vp_ref[...],
                                                       cc, sc, cp_ref[...], sp_ref[...])
            valid_t = _attn_valid_t(n)
            c4, s4 = _tile4(cc), _tile4(sc)
            qr = _rope(q_ref[...], c4, s4)
            do = do_ref[...].astype(F32)
            lane = lax.broadcasted_iota(jnp.int32, (ATTN_BLOCK, LANES), 1)
            lo = lane < 64
            lane_row = lax.broadcasted_iota(jnp.int32, (1, LANES), 1)
            k_t = {False: kb_f32.T.astype(BF16), True: pltpu.roll(kb_f32, 64, 1).T.astype(BF16)}
            heads = []
            for j in range(4):
                qj = qr[:, j * LANES:(j + 1) * LANES]
                doj = do[:, j * LANES:(j + 1) * LANES]
                for is_lo in (True, False):
                    aligned = is_lo == (j < 2)
                    msk = lo if is_lo else jnp.logical_not(lo)
                    kk = kb if aligned else kb_r
                    vv = vb if aligned else vb_r
                    qm = jnp.where(msk, qj, 0.0).astype(BF16)
                    dom = jnp.where(msk, doj, 0.0).astype(BF16)
                    heads.append(dict(
                        aligned=aligned, qm=qm, dom=dom, sink=sink_ref[0, len(heads)],
                        raw_t=lax.dot_general(kk, qm, _NT, preferred_element_type=F32),
                        dp_t=lax.dot_general(vv, dom, _NT, preferred_element_type=F32)))
            dk_band = jnp.zeros((2 * ATTN_BLOCK, LANES), F32)
            dv_band = jnp.zeros((2 * ATTN_BLOCK, LANES), F32)
            dsink = jnp.zeros((1, LANES), F32)
            for head, hd in enumerate(heads):
                probs_t, psink = _attn_probs(hd["raw_t"], valid_t, hd["sink"], 0)
                delta_t = jnp.sum(probs_t * hd["dp_t"], axis=0, keepdims=True)
                hd["ds_t"] = (probs_t * (hd["dp_t"] - delta_t) * ATTN_SCALE).astype(BF16)
                dsink = dsink + jnp.where(lane_row == head, -jnp.sum(psink * delta_t), 0.0)
                dk_h = lax.dot_general(hd["ds_t"], hd["qm"], _NN, preferred_element_type=F32)
                dv_h = lax.dot_general(probs_t.astype(BF16), hd["dom"], _NN, preferred_element_type=F32)
                if not hd["aligned"]:
                    dk_h = pltpu.roll(dk_h, 64, 1)
                    dv_h = pltpu.roll(dv_h, 64, 1)
                dk_band = dk_band + dk_h
                dv_band = dv_band + dv_h
            row_lo = lax.broadcasted_iota(jnp.int32, (LANES, ATTN_BLOCK), 0) < 64
            dq_t = [lax.dot_general(k_t[not hd["aligned"]], hd["ds_t"], _NN, preferred_element_type=F32)
                    for hd in heads]
            dqs = [jnp.where(row_lo, dq_t[2 * j], dq_t[2 * j + 1]).T for j in range(4)]
            dq_n[...] = _rope_bwd(jnp.concatenate(dqs, axis=1), c4, s4)
            dk_n[...] = dk_band
            dv_n[...] = dv_band
            dsink_ref[...] += dsink

        dk_prev = _rope_bwd(dk_c[...] + dk_n[0:ATTN_BLOCK, :], cp_ref[...], sp_ref[...])
        dv_prev = dv_c[...] + dv_n[0:ATTN_BLOCK, :]
        full = jnp.concatenate([dq_c[...], dk_prev, dv_prev], axis=1)
        dua_ref[...] = full.astype(dua_ref.dtype)
        for j in range(UA_W // LANES):
            dua_t_ref[j * LANES:(j + 1) * LANES, :] = full[:, j * LANES:(j + 1) * LANES].T.astype(dua_t_ref.dtype)
        dbias_ref[...] += jnp.sum(full, axis=0, keepdims=True)
        dq_c[...] = dq_n[...]
        dk_c[...] = dk_n[ATTN_BLOCK:, :]
        dv_c[...] = dv_n[ATTN_BLOCK:, :]

    return _call(
        body, name=name, grid=(nb + 1,), ins=[ua, ua, ua, ua, ua, ctab, stab, ctab, stab, d_out, sinks],
        in_specs=ua_specs + tab_specs + [
            pl.BlockSpec((ATTN_BLOCK, ATTN_W), lambda n: (jnp.minimum(n, nb - 1), 0)),
            pl.BlockSpec(memory_space=pltpu.SMEM)],
        out_specs=[pl.BlockSpec((ATTN_BLOCK, UA_W), lambda n: (jnp.maximum(n - 1, 0), 0)),
                   pl.BlockSpec((UA_W, ATTN_BLOCK), lambda n: (0, jnp.maximum(n - 1, 0))),
                   pl.BlockSpec((1, UA_W), lambda n: (0, 0)),
                   pl.BlockSpec((1, LANES), lambda n: (0, 0))],
        out_shape=[jax.ShapeDtypeStruct((t, UA_W), BF16), jax.ShapeDtypeStruct((UA_W, t), BF16),
                   jax.ShapeDtypeStruct((1, UA_W), F32),
                   jax.ShapeDtypeStruct((1, LANES), F32)],
        scratch_shapes=[pltpu.VMEM((ATTN_BLOCK, ATTN_W), F32), pltpu.VMEM((ATTN_BLOCK, KV_W), F32),
                        pltpu.VMEM((ATTN_BLOCK, KV_W), F32), pltpu.VMEM((ATTN_BLOCK, ATTN_W), F32),
                        pltpu.VMEM((2 * ATTN_BLOCK, KV_W), F32), pltpu.VMEM((2 * ATTN_BLOCK, KV_W), F32)],
        sem=("arbitrary",), comm=comm)


def _tri_mats():
    r = lax.broadcasted_iota(jnp.int32, (HGRN_CHUNK, LANES), 0)
    c = lax.broadcasted_iota(jnp.int32, (HGRN_CHUNK, LANES), 1)
    lower = ((c <= r) & (c < HGRN_CHUNK)).astype(F32)
    upper = ((c >= r) & (c < HGRN_CHUNK)).astype(F32)
    return lower, upper


def _tri_apply(tri, g):
    pad = jnp.concatenate([g, jnp.zeros_like(g)], axis=0)
    return lax.dot_general(tri, pad, _NN, precision=lax.Precision.HIGHEST, preferred_element_type=F32)


def _sub_masks():
    s = lax.broadcasted_iota(jnp.int32, (HGRN_CHUNK, LANES), 0)
    tt = lax.broadcasted_iota(jnp.int32, (HGRN_CHUNK, LANES), 1)
    return [(tt >= HGRN_SUB * i) & (tt < HGRN_SUB * (i + 1)) & (s <= tt) for i in range(HGRN_CHUNK // HGRN_SUB)]


def _hgrn_gates(hq, hf, lb_ref, b_scr):
    lb = _sig(lb_ref[0:1, :] - lb_ref[1:2, :])
    q = hq * _sig(hq)
    sg = _sig(hf)
    f = lb + (1.0 - lb) * sg
    k = 1.0 - f
    lower, _ = _tri_mats()
    b = _tri_apply(lower, jnp.log(f))
    b_scr[...] = b
    nsub = HGRN_CHUNK // HGRN_SUB
    starts = [jnp.zeros((1, HG_W), F32)] + [b_scr[HGRN_SUB * i - 1:HGRN_SUB * i, :] for i in range(1, nsub)]
    pq = jnp.concatenate([jnp.broadcast_to(p, (HGRN_SUB, HG_W)) for p in starts], axis=0)
    b_last = b_scr[HGRN_CHUNK - 1:HGRN_CHUNK, :]
    e_q = jnp.exp(b - pq)
    e_k = [jnp.exp(jnp.minimum(p - b, EXP_CLAMP)) for p in starts]
    e_b = jnp.exp(b)
    e_bl = jnp.exp(b_last - b)
    e_last = jnp.exp(b_last)
    return q, sg, f, k, lb, e_q, e_k, e_b, e_bl, e_last


def _sub_masks_ts():
    tt = lax.broadcasted_iota(jnp.int32, (HGRN_CHUNK, LANES), 0)
    s = lax.broadcasted_iota(jnp.int32, (HGRN_CHUNK, LANES), 1)
    return [(tt >= HGRN_SUB * i) & (tt < HGRN_SUB * (i + 1)) & (s <= tt) for i in range(HGRN_CHUNK // HGRN_SUB)]


def _masked_sum(blocks, masks, axis):
    step = HGRN_CHUNK if axis == 0 else LANES
    acc = jnp.zeros((HGRN_CHUNK, LANES), F32)
    for i, msk in enumerate(masks):
        blk = blocks[step * i:step * (i + 1), :] if axis == 0 else blocks[:, step * i:step * (i + 1)]
        acc = acc + jnp.where(msk, blk, 0.0)
    return acc


def _store_transposed(out_t_ref, chunk_rows):
    width = chunk_rows[0].shape[1]
    if len(chunk_rows) == 1:
        groups = [jnp.concatenate([chunk_rows[0], jnp.zeros_like(chunk_rows[0])], axis=0)]
    else:
        groups = [jnp.concatenate(chunk_rows[g:g + 2], axis=0) for g in range(0, len(chunk_rows), 2)]
    for g, rows in enumerate(groups):
        for c in range(width // LANES):
            tile = rows[:, c * LANES:(c + 1) * LANES].T.astype(out_t_ref.dtype)
            if len(chunk_rows) == 1:
                out_t_ref[c * LANES:(c + 1) * LANES, :] = tile[:, 0:HGRN_CHUNK]
            else:
                out_t_ref[c * LANES:(c + 1) * LANES, g * LANES:(g + 1) * LANES] = tile


def _hgrn_chunk_inputs(j, hq_ref, hf_ref, hi_ref, hg_ref, lb_ref, b_scr):
    rows = slice(j * HGRN_CHUNK, (j + 1) * HGRN_CHUNK)
    hq, hf, v, hg = hq_ref[rows, :], hf_ref[rows, :], hi_ref[rows, :], hg_ref[rows, :]
    q, sg, f, k, lb, e_q, e_k, e_b, e_bl, e_last = _hgrn_gates(hq, hf, lb_ref, b_scr.at[j])
    return dict(rows=rows, hq=hq, v=v, hg=hg, q=q, sg=sg, f=f, k=k, lb=lb, e_q=e_q, e_k=e_k, e_b=e_b, e_bl=e_bl,
                e_last=e_last, qt=q * e_q, qb=q * e_b, kd=k * e_bl, khat=[k * e for e in e_k])


def _hgrn_fwd(uh, lb_raw, norm_g, name, comm=None):
    t = uh.shape[0]
    nc = t // HGRN_CHUNK
    cps = _pick(nc, (HGRN_CHUNKS_PER_STEP, 2, 1))
    rows_step = cps * HGRN_CHUNK

    def body(hq_ref, hf_ref, hi_ref, hg_ref, lb_ref, ng_ref, r_ref, r_t_ref, o_ref, st_out_ref, st_ref, b_scr):
        @pl.when(pl.program_id(0) == 0)
        def _():
            st_ref[...] = jnp.zeros_like(st_ref)

        masks = _sub_masks_ts()
        ng = ng_ref[...]
        zpad = jnp.zeros((HGRN_CHUNK, LANES), F32)
        heads = [slice(h * LANES, (h + 1) * LANES) for h in range(4)]
        chunks = [_hgrn_chunk_inputs(j, hq_ref, hf_ref, hi_ref, hg_ref, lb_ref, b_scr) for j in range(cps)]
        for ch in chunks:
            ch["scores"] = [_dot3(ch["qt"][:, sl],
                                  jnp.concatenate([x for kh in ch["khat"] for x in (kh[:, sl], zpad)], axis=0), _NT)
                            for sl in heads]
        for j, ch in enumerate(chunks):
            o_heads, y_heads = [], []
            for h, sl in enumerate(heads):
                a_ts = _masked_sum(ch["scores"][h], masks, 1)
                vh = ch["v"][:, sl].astype(BF16)
                v_pad = jnp.concatenate([vh, jnp.zeros_like(vh)], axis=0)
                o_intra = lax.dot_general(a_ts.astype(BF16), v_pad, _NN, preferred_element_type=F32)
                st = st_ref[h]
                st_out_ref[j, h] = st
                o_inter = _dot(ch["qb"][:, sl], st, _NT)
                st_ref[h] = st * ch["e_last"][:, sl] + _dot(vh, ch["kd"][:, sl], _TN)
                oh = o_intra + o_inter
                rs = lax.rsqrt(jnp.mean(oh * oh, axis=1, keepdims=True) + RMS_EPS)
                o_heads.append(oh)
                y_heads.append(oh * rs * ng)
            hg = ch["hg"]
            o_ref[ch["rows"], :] = jnp.concatenate(o_heads, axis=1)
            ch["r"] = jnp.concatenate(y_heads, axis=1) * (hg * _sig(hg))
            r_ref[ch["rows"], :] = ch["r"].astype(r_ref.dtype)
        _store_transposed(r_t_ref, [ch["r"] for ch in chunks])

    col = lambda j: pl.BlockSpec((rows_step, HG_W), lambda c: (c, j))
    return _call(
        body, name=name, grid=(nc // cps,), ins=[uh, uh, uh, uh, lb_raw, norm_g],
        in_specs=[col(0), col(1), col(2), col(3),
                  pl.BlockSpec((2, HG_W), lambda c: (0, 0)), pl.BlockSpec((1, LANES), lambda c: (0, 0))],
        out_specs=[pl.BlockSpec((rows_step, HG_W), lambda c: (c, 0)),
                   pl.BlockSpec((HG_W, rows_step), lambda c: (0, c)),
                   pl.BlockSpec((rows_step, HG_W), lambda c: (c, 0)),
                   pl.BlockSpec((cps, 4, LANES, LANES), lambda c: (c, 0, 0, 0))],
        out_shape=[jax.ShapeDtypeStruct((t, HG_W), BF16), jax.ShapeDtypeStruct((HG_W, t), BF16),
                   jax.ShapeDtypeStruct((t, HG_W), F32), jax.ShapeDtypeStruct((nc, 4, LANES, LANES), F32)],
        scratch_shapes=[pltpu.VMEM((4, LANES, LANES), F32), pltpu.VMEM((cps, HGRN_CHUNK, HG_W), F32)],
        sem=("arbitrary",), comm=comm)


def _hgrn_bwd(uh, o_pre, d_r, states, lb_raw, norm_g, name, comm=None):
    t = uh.shape[0]
    nc = t // HGRN_CHUNK
    cps = _pick(nc, (HGRN_CHUNKS_PER_STEP, 2, 1))
    ns = nc // cps
    rows_step = cps * HGRN_CHUNK
    nsub = HGRN_CHUNK // HGRN_SUB

    def body(hq_ref, hf_ref, hi_ref, hg_ref, o_ref, dr_ref, st_in_ref, lb_ref, ng_ref,
             duh_ref, duh_t_ref, dbias_ref, dng_ref, dlb_ref, dst_ref, b_scr, dlb_acc):
        i = pl.program_id(0)

        @pl.when(i == 0)
        def _():
            dst_ref[...] = jnp.zeros_like(dst_ref)
            dbias_ref[...] = jnp.zeros_like(dbias_ref)
            dng_ref[...] = jnp.zeros_like(dng_ref)
            dlb_acc[...] = jnp.zeros_like(dlb_acc)

        masks_st = _sub_masks()
        masks_ts = _sub_masks_ts()
        ng = ng_ref[...]
        zpad = jnp.zeros((HGRN_CHUNK, LANES), F32)
        _, upper = _tri_mats()
        heads = [slice(h * LANES, (h + 1) * LANES) for h in range(4)]
        row = lax.broadcasted_iota(jnp.int32, (HGRN_CHUNK, HG_W), 0)

        chunks = [_hgrn_chunk_inputs(j, hq_ref, hf_ref, hi_ref, hg_ref, lb_ref, b_scr) for j in range(cps)]
        dng = jnp.zeros((1, LANES), F32)
        for ch in chunks:
            o = o_ref[ch["rows"], :]
            dr = dr_ref[ch["rows"], :].astype(F32)
            hg = ch["hg"]
            sgg = _sig(hg)
            dy = dr * (hg * sgg)
            do_h, y_h = [], []
            for sl in heads:
                oh = o[:, sl]
                rs = lax.rsqrt(jnp.mean(oh * oh, axis=1, keepdims=True) + RMS_EPS)
                y_h.append(oh * rs * ng)
                dng = dng + jnp.sum(dy[:, sl] * oh * rs, axis=0, keepdims=True)
                w = dy[:, sl] * ng
                do_h.append(rs * (w - oh * (rs * rs) * jnp.mean(w * oh, axis=1, keepdims=True)))
            ch["do"] = do_h
            ch["dhg"] = dr * jnp.concatenate(y_h, axis=1) * _dsilu(hg, sgg)

        for ch in chunks:
            ch["kst"], ch["kpad"], ch["qt_pad"], ch["v_b"], ch["do_pad"] = [], [], [], [], []
            ch["ats"], ch["d_at"], ch["d_a"] = [], [], []
            for h, sl in enumerate(heads):
                kst = jnp.concatenate([kh[:, sl] for kh in ch["khat"]], axis=0)
                kpad = jnp.concatenate([x for kh in ch["khat"] for x in (kh[:, sl], zpad)], axis=0)
                qt_pad = jnp.concatenate([ch["qt"][:, sl], zpad], axis=0)
                vh = ch["v"][:, sl].astype(BF16)
                v_pad = jnp.concatenate([vh, jnp.zeros_like(vh)], axis=0)
                do_b = ch["do"][h].astype(BF16)
                do_pad = jnp.concatenate([do_b, jnp.zeros_like(do_b)], axis=0)
                ch["kst"].append(kst)
                ch["kpad"].append(kpad)
                ch["qt_pad"].append(qt_pad)
                ch["v_b"].append(vh)
                ch["do_pad"].append(do_pad)
                ch["ats"].append(_dot3(kst, qt_pad, _NT))
                ch["d_at"].append(lax.dot_general(vh, do_pad, _NT, preferred_element_type=F32))
                ch["d_a"].append(lax.dot_general(do_b, v_pad, _NT, preferred_element_type=F32))

        for ch in chunks:
            ch["d_kst"], ch["d_qt"], ch["dv"] = [], [], []
            for h in range(4):
                at = _masked_sum(ch["ats"][h], masks_st, 0)
                d_ats = jnp.concatenate([jnp.where(m, ch["d_at"][h], 0.0) for m in masks_st], axis=0)
                d_a_cat = jnp.concatenate([jnp.where(m, ch["d_a"][h], 0.0) for m in masks_ts], axis=1)
                ch["d_kst"].append(_dot3(d_ats, ch["qt_pad"][h], _NN))
                ch["d_qt"].append(_dot3(d_a_cat, ch["kpad"][h], _NN))
                ch["dv"].append(lax.dot_general(at.astype(BF16), ch["do_pad"][h], _NN, preferred_element_type=F32))

        for j in reversed(range(cps)):
            ch = chunks[j]
            q, k, sg, f, lb = ch["q"], ch["k"], ch["sg"], ch["f"], ch["lb"]
            dq_h, dk_h, dv_h, extra_h = [], [], [], []
            for h, sl in enumerate(heads):
                st_prev = st_in_ref[j, h]
                d_st = dst_ref[h]
                d_st_b = d_st.astype(BF16)
                do_b = ch["do_pad"][h][0:HGRN_CHUNK, :]
                kd, e_last = ch["kd"][:, sl], ch["e_last"][:, sl]
                dv = ch["dv"][h] + _dot(kd, d_st_b, _NT)
                d_qb = _dot(do_b, st_prev, _NN)
                d_kd = lax.dot_general(ch["v_b"][h], d_st_b, _NN, preferred_element_type=F32)
                extra_h.append(jnp.sum(st_prev * d_st, axis=0, keepdims=True) * e_last
                               + jnp.sum(kd * d_kd, axis=0, keepdims=True))
                dst_ref[h] = d_st * e_last + _dot(do_b, ch["qb"][:, sl], _TN)
                dq_h.append(ch["d_qt"][h] * ch["e_q"][:, sl] + d_qb * ch["e_b"][:, sl])
                dkk = d_kd * ch["e_bl"][:, sl]
                for s_ in range(nsub):
                    dkk = dkk + ch["d_kst"][h][HGRN_CHUNK * s_:HGRN_CHUNK * (s_ + 1), :] * ch["e_k"][s_][:, sl]
                dk_h.append(dkk)
                dv_h.append(dv)
            dq = jnp.concatenate(dq_h, axis=1)
            dk = jnp.concatenate(dk_h, axis=1)
            dv = jnp.concatenate(dv_h, axis=1)
            extra = jnp.concatenate(extra_h, axis=1)
            db = q * dq - k * dk + jnp.where(row == HGRN_CHUNK - 1, extra, 0.0)
            dg = _tri_apply(upper, db)
            df = dg / f - dk
            dhf = df * (1.0 - lb) * sg * (1.0 - sg)
            dhq = dq * _dsilu(ch["hq"], _sig(ch["hq"]))
            full = jnp.concatenate([dhq, dhf, dv, ch["dhg"]], axis=1)
            duh_ref[ch["rows"], :] = full.astype(duh_ref.dtype)
            ch["full"] = full
            dbias_ref[...] += jnp.sum(full, axis=0, keepdims=True)
            dlb_acc[...] += jnp.sum(df * (1.0 - sg), axis=0, keepdims=True)
        dng_ref[...] += dng
        _store_transposed(duh_t_ref, [ch["full"] for ch in chunks])

        @pl.when(i == ns - 1)
        def _():
            lb = chunks[0]["lb"]
            d_a0 = dlb_acc[...] * lb * (1.0 - lb)
            r8 = lax.broadcasted_iota(jnp.int32, (8, HG_W), 0)
            dlb_ref[...] = jnp.where(r8 == 0, d_a0, jnp.where(r8 == 1, -d_a0, 0.0))

    col = lambda j: pl.BlockSpec((rows_step, HG_W), lambda i: (ns - 1 - i, j))
    return _call(
        body, name=name, grid=(ns,), ins=[uh, uh, uh, uh, o_pre, d_r, states, lb_raw, norm_g],
        in_specs=[col(0), col(1), col(2), col(3), col(0), col(d_r.shape[1] // HG_W - 1),
                  pl.BlockSpec((cps, 4, LANES, LANES), lambda i: (ns - 1 - i, 0, 0, 0)),
                  pl.BlockSpec((2, HG_W), lambda i: (0, 0)), pl.BlockSpec((1, LANES), lambda i: (0, 0))],
        out_specs=[pl.BlockSpec((rows_step, UH_W), lambda i: (ns - 1 - i, 0)),
                   pl.BlockSpec((UH_W, rows_step), lambda i: (0, ns - 1 - i)),
                   pl.BlockSpec((1, UH_W), lambda i: (0, 0)),
                   pl.BlockSpec((1, LANES), lambda i: (0, 0)),
                   pl.BlockSpec((8, HG_W), lambda i: (0, 0))],
        out_shape=[jax.ShapeDtypeStruct((t, UH_W), BF16), jax.ShapeDtypeStruct((UH_W, t), BF16),
                   jax.ShapeDtypeStruct((1, UH_W), F32),
                   jax.ShapeDtypeStruct((1, LANES), F32), jax.ShapeDtypeStruct((8, HG_W), F32)],
        scratch_shapes=[pltpu.VMEM((4, LANES, LANES), F32), pltpu.VMEM((cps, HGRN_CHUNK, HG_W), F32),
                        pltpu.VMEM((1, HG_W), F32)],
        sem=("arbitrary",), comm=comm)


def _ln_bwd_math(dy, xhat, rstd, g):
    dxh = dy * g
    return rstd * (dxh - jnp.mean(dxh, axis=1, keepdims=True)
                   - xhat * jnp.mean(dxh * xhat, axis=1, keepdims=True))


def _mm_stacked(a1, a2, b, *, name, tk=1024):
    (m1, kdim), m2, n = a1.shape, a2.shape[0], b.shape[1]
    tk = _pick(kdim, (tk, 512, 256, 128))
    nk = kdim // tk

    def body(a1_ref, a2_ref, b_ref, o_ref, ob_ref, acc_ref):
        k = pl.program_id(0)

        @pl.when(k == 0)
        def _():
            acc_ref[...] = jnp.zeros_like(acc_ref)

        bb = b_ref[...].astype(BF16)
        acc_ref[0:m1] += _dot(a1_ref[...], bb, _NN)
        acc_ref[m1:] += _dot(a2_ref[...], bb, _NN)

        @pl.when(k == nk - 1)
        def _():
            r = acc_ref[...]
            o_ref[...] = r
            ob_ref[...] = r.astype(BF16)

    ospec = pl.BlockSpec((m1 + m2, n), lambda k: (0, 0))
    return pl.pallas_call(
        body, name=name, grid=(nk,),
        in_specs=[pl.BlockSpec((m1, tk), lambda k: (0, k)), pl.BlockSpec((m2, tk), lambda k: (0, k)),
                  pl.BlockSpec((tk, n), lambda k: (k, 0))],
        out_specs=[ospec, ospec],
        out_shape=[jax.ShapeDtypeStruct((m1 + m2, n), F32), jax.ShapeDtypeStruct((m1 + m2, n), BF16)],
        scratch_shapes=[pltpu.VMEM((m1 + m2, n), F32)],
        compiler_params=_cp("arbitrary"))(a1, a2, b)


def _mm_rows(a, b, extras, *, name, epilogue, out_shape, out_specs, tb=False, tm=512, tk=1408, pair2=None):
    m, kdim = a.shape
    n = b.shape[0] if tb else b.shape[1]
    tm = _pick(m, (tm, 256, 128))
    tk = _pick(kdim, (tk, 1408, 1024, 768, 512, 256, 128))
    nk = kdim // tk
    b_spec = pl.BlockSpec((n, tk), lambda i, k: (0, k)) if tb else pl.BlockSpec((tk, n), lambda i, k: (k, 0))
    dims = _NT if tb else _NN
    n_ex, n_out, n_p2 = len(extras), len(out_shape), (0 if pair2 is None else 2)

    def body(*refs):
        a_ref, b_ref = refs[0], refs[1]
        p2_refs = refs[2:2 + n_p2]
        ex_refs = refs[2 + n_p2:2 + n_p2 + n_ex]
        o_refs = refs[2 + n_p2 + n_ex:2 + n_p2 + n_ex + n_out]
        acc_ref = refs[-1]
        i, k = pl.program_id(0), pl.program_id(1)

        @pl.when(k == 0)
        def _():
            if n_p2:
                acc_ref[...] = _dot(p2_refs[0][...], p2_refs[1][...], _NN)
            else:
                acc_ref[...] = jnp.zeros_like(acc_ref)

        acc_ref[...] += _dot(a_ref[...], b_ref[...], dims)

        @pl.when(k == nk - 1)
        def _():
            epilogue(acc_ref[...], ex_refs, o_refs, i == 0)

    p2_specs, p2_ins = [], []
    if pair2 is not None:
        k2 = pair2[0].shape[1]
        p2_specs = [pl.BlockSpec((tm, k2), lambda i, k: (i, 0)), pl.BlockSpec((k2, n), lambda i, k: (0, 0))]
        p2_ins = list(pair2)
    return pl.pallas_call(
        body, name=name, grid=(m // tm, nk),
        in_specs=[pl.BlockSpec((tm, tk), lambda i, k: (i, k)), b_spec] + p2_specs + [sp for _, sp in extras],
        out_specs=list(out_specs), out_shape=list(out_shape),
        scratch_shapes=[pltpu.VMEM((tm, n), F32)],
        compiler_params=_cp("arbitrary", "arbitrary"),
    )(a, b, *p2_ins, *[arr for arr, _ in extras])


def _rows_specs(tm, d):
    row = pl.BlockSpec((tm, d), lambda i, k: (i, 0))
    vec = pl.BlockSpec((1, d), lambda i, k: (0, 0))
    col = pl.BlockSpec((tm, 1), lambda i, k: (i, 0))
    return row, vec, col


def _mm_split_fwd(a, b_t, bias, split, name, tm=512):
    t, n = a.shape[0], b_t.shape[0]

    def epilogue(acc, ex, outs, first):
        z = acc + ex[0][...]
        outs[0][...] = z[:, :split]
        outs[1][...] = z[:, split:]

    return _mm_rows(a, b_t, [(bias, pl.BlockSpec((1, n), lambda i, k: (0, 0)))], name=name, epilogue=epilogue,
                    tb=True, tm=tm, tk=a.shape[1],
                    out_shape=[jax.ShapeDtypeStruct((t, split), F32), jax.ShapeDtypeStruct((t, n - split), F32)],
                    out_specs=[pl.BlockSpec((tm, split), lambda i, k: (i, 0)),
                               pl.BlockSpec((tm, n - split), lambda i, k: (i, 0))])


def _mm_ln_fwd(a, b, pair2, addend, g, beta, name, tm=512):
    t, d = addend.shape
    tm = _pick(t, (tm, 256, 128))
    row, vec, col = _rows_specs(tm, d)

    def epilogue(acc, ex, outs, first):
        z = acc + ALPHA * ex[0][...]
        mu = jnp.mean(z, axis=1, keepdims=True)
        zc = z - mu
        rstd = lax.rsqrt(jnp.mean(zc * zc, axis=1, keepdims=True) + LN_EPS)
        xhat = zc * rstd
        h = xhat * ex[1][...] + ex[2][...]
        outs[0][...] = h
        outs[1][...] = h.astype(BF16)
        outs[2][...] = xhat
        outs[3][...] = rstd

    return _mm_rows(a, b, [(addend, row), (g, vec), (beta, vec)], name=name, epilogue=epilogue, tm=tm, pair2=pair2,
                    out_shape=[jax.ShapeDtypeStruct((t, d), F32), jax.ShapeDtypeStruct((t, d), BF16),
                               jax.ShapeDtypeStruct((t, d), F32), jax.ShapeDtypeStruct((t, 1), F32)],
                    out_specs=[row, row, row, col])


CONV_RB = 32
HALO = 8


def _sum8(x):
    acc = x[0:8]
    for r in range(8, x.shape[0], 8):
        acc = acc + x[r:r + 8]
    return acc


FFN_TILE = 256
FFN_COLS = 256


def _rows_before(win, k):
    return pltpu.roll(win, k, 0)[HALO:]


def _rows_after(win, k):
    n = win.shape[0]
    return pltpu.roll(win, n - k, 0)[0:n - HALO]


def _resident(shape):
    return pl.BlockSpec(shape, lambda i: (0,) * len(shape), pipeline_mode=pl.Buffered(1))


def _ffn_fwd(h1b, h1, w_up_t, conv_w, conv_b, w_down, target, ln2_g, ln2_b, name, comm=None):
    t, d = h1.shape
    tr = _pick(t, (FFN_TILE, 128))
    nblk = D_FF // FFN_COLS
    rb = CONV_RB

    def body(a_ref, wup_ref, cw_ref, cb_ref, wd_ref, h1_ref, tgt_ref, g_ref, b_ref,
             u2_ref, hm_ref, dz_ref, dg_ref, db_ref, loss_ref, ext):
        i = pl.program_id(0)

        @pl.when(i == 0)
        def _():
            ext[0:HALO, :] = jnp.zeros((HALO, D_FF), F32)
            dg_ref[...] = jnp.zeros_like(dg_ref)
            db_ref[...] = jnp.zeros_like(db_ref)
            loss_ref[...] = jnp.zeros_like(loss_ref)

        a = a_ref[...]
        for c in range(nblk):
            cs = slice(c * FFN_COLS, (c + 1) * FFN_COLS)
            vs = slice(D_FF + c * FFN_COLS, D_FF + (c + 1) * FFN_COLS)
            gate_pre = lax.dot_general(a, wup_ref[cs, :], _NT, preferred_element_type=F32)
            u2_ref[:, cs] = gate_pre
            ext[HALO:, cs] = gate_pre
            u2_ref[:, vs] = lax.dot_general(a, wup_ref[vs, :], _NT, preferred_element_type=F32)
        acc = jnp.zeros((tr, d), F32)
        for c in range(nblk):
            cs = slice(c * FFN_COLS, (c + 1) * FFN_COLS)
            for sub in range(FFN_COLS // LANES):
                ln = slice(c * FFN_COLS + sub * LANES, c * FFN_COLS + (sub + 1) * LANES)
                vl = slice(D_FF + c * FFN_COLS + sub * LANES, D_FF + c * FFN_COLS + (sub + 1) * LANES)
                w0, w1, w2, bb = cw_ref[0:1, ln], cw_ref[1:2, ln], cw_ref[2:3, ln], cb_ref[:, ln]
                for r0 in range(0, tr, rb):
                    win = ext[r0:r0 + HALO + rb, ln]
                    gate = _rows_before(win, 2) * w0 + _rows_before(win, 1) * w1 + win[HALO:] * w2 + bb
                    hm_ref[r0:r0 + rb, ln] = (gate * _sig(gate) * u2_ref[r0:r0 + rb, vl]).astype(hm_ref.dtype)
            acc = acc + lax.dot_general(hm_ref[:, cs], wd_ref[cs, :], _NN, preferred_element_type=F32)
        ext[0:HALO, :] = ext[tr:tr + HALO, :]

        z = acc + ALPHA * h1_ref[...]
        gg = g_ref[...]
        mu = jnp.mean(z, axis=1, keepdims=True)
        zc = z - mu
        rstd = lax.rsqrt(jnp.mean(zc * zc, axis=1, keepdims=True) + LN_EPS)
        xhat = zc * rstd
        err = xhat * gg + b_ref[...] - tgt_ref[...]
        loss_ref[...] += 0.5 * jnp.sum(jnp.mean(err * err, axis=1, keepdims=True))
        dy = err * (1.0 / d)
        dz_ref[...] = _ln_bwd_math(dy, xhat, rstd, gg)
        dg_ref[...] += jnp.sum(dy * xhat, axis=0, keepdims=True)
        db_ref[...] += jnp.sum(dy, axis=0, keepdims=True)

    row = lambda w: pl.BlockSpec((tr, w), lambda i: (i, 0))
    vec = pl.BlockSpec((1, d), lambda i: (0, 0))
    return _call(
        body, name=name, grid=(t // tr,),
        ins=[h1b, w_up_t, conv_w, conv_b, w_down, h1, target, ln2_g, ln2_b],
        in_specs=[row(d), _resident((2 * D_FF, d)), _resident((3, D_FF)), _resident((1, D_FF)),
                  _resident((D_FF, d)), row(d), row(d), vec, vec],
        out_specs=[row(2 * D_FF), row(D_FF), row(d), vec, vec, pl.BlockSpec((1, LANES), lambda i: (0, 0))],
        out_shape=[jax.ShapeDtypeStruct((t, 2 * D_FF), F32), jax.ShapeDtypeStruct((t, D_FF), BF16),
                   jax.ShapeDtypeStruct((t, d), F32), jax.ShapeDtypeStruct((1, d), F32),
                   jax.ShapeDtypeStruct((1, d), F32), jax.ShapeDtypeStruct((1, LANES), F32)],
        scratch_shapes=[pltpu.VMEM((tr + HALO, D_FF), F32)],
        sem=("arbitrary",), comm=comm)


def _ffn_bwd(dz2, u2, w_down, w_up_t, conv_w, conv_b, xhat1, rstd1, ln1_g, name, comm=None):
    t, d = dz2.shape
    tr = _pick(t, (FFN_TILE, 128))
    nt = t // tr
    hb = tr // HALO
    nblk = D_FF // FFN_COLS
    rb = CONV_RB

    def body(dz2_ref, dz2_next_ref, u2_ref, gp_prev_ref, wd_ref, wup_ref, cw_ref, cb_ref, xhat_ref, rstd_ref,
             g1_ref, du_ref, dz1_ref, dw_ref, dcb_ref, dg1_ref, db1_ref, head, dh_s, dg_s):
        i = pl.program_id(0)

        @pl.when(i == 0)
        def _():
            dg_s[tr:, :] = jnp.zeros((HALO, D_FF), F32)
            dw_ref[...] = jnp.zeros_like(dw_ref)
            dcb_ref[...] = jnp.zeros_like(dcb_ref)
            dg1_ref[...] = jnp.zeros_like(dg1_ref)
            db1_ref[...] = jnp.zeros_like(db1_ref)

        dz2 = dz2_ref[...]

        @pl.when(i == 0)
        def _():
            dz2_b = dz2.astype(BF16)
            for c in range(nblk):
                cs = slice(c * FFN_COLS, (c + 1) * FFN_COLS)
                dh_s[:, cs] = lax.dot_general(dz2_b, wd_ref[cs, :], _NT, preferred_element_type=F32)

        dz2_next = dz2_next_ref[...].astype(BF16)
        dh_next = [lax.dot_general(dz2_next, wd_ref[c * FFN_COLS:(c + 1) * FFN_COLS, :], _NT,
                                   preferred_element_type=F32) for c in range(nblk)]
        head[0:HALO, :] = jnp.where(i == nt - 1, 0.0, gp_prev_ref[...])
        head[HALO:, :] = u2_ref[0:rb, 0:D_FF]

        acc = jnp.zeros((tr, d), F32)
        for blk in range(nblk):
            for c in range(blk * FFN_COLS // LANES, (blk + 1) * FFN_COLS // LANES):
                ln = slice(c * LANES, (c + 1) * LANES)
                vl = slice(D_FF + c * LANES, D_FF + (c + 1) * LANES)
                w0, w1, w2, bb = cw_ref[0:1, ln], cw_ref[1:2, ln], cw_ref[2:3, ln], cb_ref[:, ln]
                acc_b = jnp.zeros((8, LANES), F32)
                acc_w = [jnp.zeros((8, LANES), F32) for _ in range(3)]
                for r0 in range(0, tr, rb):
                    win = head[:, ln] if r0 == 0 else u2_ref[r0 - HALO:r0 + rb, ln]
                    g_m2, g_m1, g_0 = _rows_before(win, 2), _rows_before(win, 1), win[HALO:]
                    gate = g_m2 * w0 + g_m1 * w1 + g_0 * w2 + bb
                    sg = _sig(gate)
                    dh = dh_s[r0:r0 + rb, ln]
                    dgate = dh * u2_ref[r0:r0 + rb, vl] * _dsilu(gate, sg)
                    dg_s[r0:r0 + rb, ln] = dgate
                    du_ref[r0:r0 + rb, vl] = (dh * (gate * sg)).astype(du_ref.dtype)
                    acc_b = acc_b + _sum8(dgate)
                    acc_w[0] = acc_w[0] + _sum8(dgate * g_m2)
                    acc_w[1] = acc_w[1] + _sum8(dgate * g_m1)
                    acc_w[2] = acc_w[2] + _sum8(dgate * g_0)
                dcb_ref[:, ln] += jnp.sum(acc_b, axis=0, keepdims=True)
                for j in range(3):
                    dw_ref[j:j + 1, ln] += jnp.sum(acc_w[j], axis=0, keepdims=True)
                for r0 in range(0, tr, rb):
                    win = dg_s[r0:r0 + rb + HALO, ln]
                    d_gp = _rows_after(win, 2) * w0 + _rows_after(win, 1) * w1 + win[0:rb] * w2
                    du_ref[r0:r0 + rb, ln] = d_gp.astype(du_ref.dtype)
            cs = slice(blk * FFN_COLS, (blk + 1) * FFN_COLS)
            vs = slice(D_FF + blk * FFN_COLS, D_FF + (blk + 1) * FFN_COLS)
            acc = acc + lax.dot_general(du_ref[:, cs], wup_ref[cs, :], _NN, preferred_element_type=F32)
            acc = acc + lax.dot_general(du_ref[:, vs], wup_ref[vs, :], _NN, preferred_element_type=F32)
        dg_s[tr:, :] = dg_s[0:HALO, :]
        for c in range(nblk):
            dh_s[:, c * FFN_COLS:(c + 1) * FFN_COLS] = dh_next[c]
        dy = acc + ALPHA * dz2
        xh = xhat_ref[...]
        dz1_ref[...] = _ln_bwd_math(dy, xh, rstd_ref[...], g1_ref[...])
        dg1_ref[...] += jnp.sum(dy * xh, axis=0, keepdims=True)
        db1_ref[...] += jnp.sum(dy, axis=0, keepdims=True)

    rev = lambda w: pl.BlockSpec((tr, w), lambda i: (nt - 1 - i, 0))
    vec = pl.BlockSpec((1, d), lambda i: (0, 0))
    return _call(
        body, name=name, grid=(nt,),
        ins=[dz2, dz2, u2, u2, w_down, w_up_t, conv_w, conv_b, xhat1, rstd1, ln1_g],
        in_specs=[rev(d), pl.BlockSpec((tr, d), lambda i: (jnp.maximum(nt - 2 - i, 0), 0)), rev(2 * D_FF),
                  pl.BlockSpec((HALO, D_FF), lambda i: (jnp.maximum((nt - 1 - i) * hb - 1, 0), 0)),
                  _resident((D_FF, d)), _resident((2 * D_FF, d)), _resident((3, D_FF)), _resident((1, D_FF)),
                  rev(d), pl.BlockSpec((tr, 1), lambda i: (nt - 1 - i, 0)), vec],
        out_specs=[rev(2 * D_FF), rev(d), pl.BlockSpec((8, D_FF), lambda i: (0, 0)),
                   pl.BlockSpec((1, D_FF), lambda i: (0, 0)), vec, vec],
        out_shape=[jax.ShapeDtypeStruct((t, 2 * D_FF), BF16), jax.ShapeDtypeStruct((t, d), F32),
                   jax.ShapeDtypeStruct((8, D_FF), F32), jax.ShapeDtypeStruct((1, D_FF), F32),
                   jax.ShapeDtypeStruct((1, d), F32), jax.ShapeDtypeStruct((1, d), F32)],
        scratch_shapes=[pltpu.VMEM((HALO + rb, D_FF), F32), pltpu.VMEM((tr, D_FF), F32),
                        pltpu.VMEM((tr + HALO, D_FF), F32)],
        sem=("arbitrary",), comm=comm)


def _pad_rows(a, rows):
    return jnp.pad(a, ((0, rows - a.shape[0]), (0, 0)))


SMALL_LAYOUT = (("ln1_g", 1024), ("ln1_b", 1024), ("b_in", 2816), ("sinks", 8), ("hgrn_lb", 1024),
                ("hgrn_norm_g", 128), ("ln2_g", 1024), ("ln2_b", 1024), ("conv_b", 2816), ("loss", 1))
SMALL_SHAPES = {"ln1_g": (1, 1024), "ln1_b": (1, 1024), "b_in": (1, 2816), "sinks": (1, 8), "hgrn_lb": (2, 512),
                "hgrn_norm_g": (1, 128), "ln2_g": (1, 1024), "ln2_b": (1, 1024), "conv_b": (1, 2816),
                "loss": (1,)}


def _pack_small(parts):
    rows = []
    for name, size in SMALL_LAYOUT:
        flat = parts[name].reshape(-1).astype(F32)
        padded = -(-size // LANES) * LANES
        rows.append(jnp.pad(flat, (0, padded - size)).reshape(-1, LANES))
    return _pad_rows(jnp.concatenate(rows, axis=0), SMALL_ROWS)


def _small_update(small_g, ws, ms, vs, name):
    names = [n for n, _ in SMALL_LAYOUT if n != "loss"]
    first, r = {}, 0
    for n, size in SMALL_LAYOUT:
        first[n] = r
        r += -(-size // LANES)
    npar = len(names)

    def body(*refs):
        g_ref = refs[0]
        w_refs, m_refs, v_refs = (refs[1 + q * npar:1 + (q + 1) * npar] for q in range(3))
        outs = refs[1 + 3 * npar:-1]
        sum_ref = refs[-1]
        acc = g_ref[0]
        for s in range(1, N_DEV):
            acc = acc + g_ref[s]
        sum_ref[...] = acc
        outs[0][...] = sum_ref[first["loss"]:first["loss"] + 1, 0:1]
        for p, n in enumerate(names):
            g_out, d_out, m_out, v_out = outs[1 + 4 * p:5 + 4 * p]
            rows, cols = SMALL_SHAPES[n]
            if cols < LANES:
                g_out[...] = sum_ref[first[n]:first[n] + 1, 0:cols]
            else:
                per = cols // LANES
                for h in range(rows):
                    for j in range(per):
                        rr = first[n] + h * per + j
                        g_out[h:h + 1, j * LANES:(j + 1) * LANES] = sum_ref[rr:rr + 1, :]
            d_out[...], m_out[...], v_out[...] = _adamw_math(w_refs[p][...], g_out[...], m_refs[p][...],
                                                            v_refs[p][...])

    out_shape = [jax.ShapeDtypeStruct((1, 1), F32)]
    for n in names:
        out_shape += [jax.ShapeDtypeStruct(SMALL_SHAPES[n], F32)] * 4
    res = pl.pallas_call(
        body, name=name, out_shape=out_shape,
        scratch_shapes=[pltpu.VMEM((SMALL_ROWS, LANES), F32)],
        compiler_params=_cp(),
    )(small_g, *[ws[n] for n in names], *[ms[n] for n in names], *[vs[n] for n in names])
    return res[0], {n: res[1 + 4 * p:5 + 4 * p] for p, n in enumerate(names)}


def _conv_w_update(recv, w, m, v, name):
    taps, cols = w.shape

    def body(r_ref, w_ref, m_ref, v_ref, g_ref, d_ref, nm_ref, nv_ref):
        acc = r_ref[0]
        for s in range(1, N_DEV):
            acc = acc + r_ref[s]
        g = acc[0:taps]
        res = (g,) + _adamw_math(w_ref[...], g, m_ref[...], v_ref[...])
        for o_ref, val in zip((g_ref, d_ref, nm_ref, nv_ref), res):
            for k in range(taps):
                o_ref[k] = val[k:k + 1]

    shp = jax.ShapeDtypeStruct((taps, 1, cols), F32)
    outs = pl.pallas_call(body, name=name, out_shape=[shp, shp, shp, shp], compiler_params=_cp())(recv, w, m, v)
    return [o.transpose(1, 0, 2) for o in outs]


def kernel(x, positions, ln1_g, ln1_b, w_in, b_in, sinks, hgrn_lb, hgrn_norm_g, w_o, ln2_g, ln2_b, w_up, conv_w, conv_b, w_down, loss_target, m_ln1_g, m_ln1_b, m_w_in, m_b_in, m_sinks, m_hgrn_lb, m_hgrn_norm_g, m_w_o, m_ln2_g, m_ln2_b, m_w_up, m_conv_w, m_conv_b, m_w_down, v_ln1_g, v_ln1_b, v_w_in, v_b_in, v_sinks, v_hgrn_lb, v_hgrn_norm_g, v_w_o, v_ln2_g, v_ln2_b, v_w_up, v_conv_w, v_conv_b, v_w_down):
    t = x.shape[1]
    x2 = x[0]
    target = loss_target[0]
    pos_col = positions.reshape(t, 1)

    w_in_t_s = w_in[0].T.astype(BF16)
    w_up_t_s = w_up[0].T.astype(BF16)
    w_o_s = w_o[0].astype(BF16)
    w_down_s = w_down[0].astype(BF16)
    (ctab, stab, xb), (w_in_t_g, cw_g) = _prep(
        pos_col, x2, "prep_ag_w_in", _Comm([{"kind": "gather", "arr": w_in_t_s}, {"kind": "gather", "arr": _pad_rows(conv_w[0], 8)}]))
    w_in_t = w_in_t_g.reshape(D_FF, D_MODEL)
    conv_w_f = cw_g[:, 0:3].transpose(1, 0, 2).reshape(3, D_FF)

    ua, uh = _mm_split_fwd(xb, w_in_t, b_in, UA_W, "fwd_in")
    half_up = SHARD_UP // 2
    (a_out, a_out_t), (w_o_g, w_up_half) = _attn_fwd(
        ua, ctab, stab, sinks, "attn_fwd",
        comm=_Comm([{"kind": "gather", "arr": w_o_s},
                    {"kind": "gather", "arr": w_up_t_s, "rows": (0, half_up), "dst_rows": SHARD_UP}]))
    (r_out, r_out_t, o_pre, states), (w_up_t_g, w_down_g) = _hgrn_fwd(
        uh, hgrn_lb, hgrn_norm_g, "hgrn_fwd",
        comm=_Comm([{"kind": "gather", "arr": w_up_t_s, "rows": (half_up, half_up), "dst_rows": SHARD_UP,
                     "dst_first": half_up, "into": w_up_half},
                    {"kind": "gather", "arr": w_down_s}]))
    w_down_f = w_down_g.reshape(D_FF, D_MODEL)
    w_o_f = w_o_g.reshape(D_MODEL, D_MODEL)
    w_up_t = w_up_t_g.reshape(2 * D_FF, D_MODEL)
    h1, h1b, xhat1, rstd1 = _mm_ln_fwd(r_out, w_o_f[ATTN_W:], (a_out, w_o_f[:ATTN_W]), x2, ln1_g, ln1_b,
                                       "fwd_o_ln1")
    u2, hmid, dz2, d_ln2_g, d_ln2_b, loss_part = _ffn_fwd(h1b, h1, w_up_t, conv_w_f, conv_b, w_down_f, target,
                                                         ln2_g, ln2_b, "ffn_fwd")[0]

    d_w_down, d_w_down_b = _mm(hmid, dz2, ta=True, out_dtype2=BF16, tm=1408, tk=1024, name="bwd_down_dw")
    (d_u2, dz1, d_conv_w8, d_conv_b, d_ln1_g, d_ln1_b), (recv_down,) = _ffn_bwd(
        dz2, u2, w_down_f, w_up_t, conv_w_f, conv_b, xhat1, rstd1, ln1_g, "ffn_bwd",
        comm=_Comm([{"kind": "exchange", "arr": d_w_down_b.reshape(N_DEV, SHARD_DOWN, D_MODEL)}]))
    d_w_up_t, d_w_up_t_b = _mm(d_u2, h1b, ta=True, out_dtype2=BF16, tm=1408, tk=1024, name="bwd_up_dw")
    d_ar = _mm(dz1, w_o_f, tb=True, name="bwd_o_dx")
    d_w_o, d_w_o_b = _mm_stacked(a_out_t, r_out_t, dz1, name="bwd_o_dw")
    d_w_up_x = d_w_up_t_b.reshape(N_DEV, SHARD_UP, D_MODEL)
    half = SHARD_UP // 2
    d_cw_x = d_conv_w8.reshape(8, N_DEV, SHARD_IN).transpose(1, 0, 2)
    (d_ua, d_ua_t, d_bias_a, d_sinks), (recv_up_half, recv_cw) = _attn_bwd(
        ua, d_ar, ctab, stab, sinks, "attn_bwd",
        comm=_Comm([{"kind": "exchange", "arr": d_w_up_x, "rows": (0, half), "dst_rows": SHARD_UP},
                    {"kind": "exchange", "arr": d_cw_x}]))
    (d_uh, d_uh_t, d_bias_h, d_norm_g, d_lb8), (recv_up, recv_o) = _hgrn_bwd(
        uh, o_pre, d_ar, states, hgrn_lb, hgrn_norm_g, "hgrn_bwd",
        comm=_Comm([{"kind": "exchange", "arr": d_w_up_x, "rows": (half, half), "dst_rows": SHARD_UP,
                     "dst_first": half, "into": recv_up_half},
                    {"kind": "exchange", "arr": d_w_o_b.reshape(N_DEV, SHARD_O, D_MODEL)}]))
    d_w_in_part = _mm(d_ua_t, xb, out_dtype2=BF16, tm=UA_W, tk=t, out_rows=D_FF, name="bwd_in_dw_attn")
    d_w_in_t, d_w_in_t_b = _mm(d_uh_t, xb, out_dtype2=BF16, tm=256, tk=t, out_rows=D_FF, first_row=UA_W,
                               into=d_w_in_part, name="bwd_in_dw_hgrn")
    small_local = _pack_small({
        "ln1_g": d_ln1_g, "ln1_b": d_ln1_b, "b_in": jnp.concatenate([d_bias_a, d_bias_h], axis=1),
        "sinks": d_sinks[:, :8], "hgrn_lb": d_lb8[0:2], "hgrn_norm_g": d_norm_g, "ln2_g": d_ln2_g,
        "ln2_b": d_ln2_b, "conv_b": d_conv_b, "loss": loss_part[:, :1]})
    d_w_in_x = d_w_in_t_b.reshape(N_DEV, SHARD_IN, D_MODEL)
    res_up, (from_sibling,) = _sum_shards_adamw(
        [recv_up], d_w_up_t, w_up[0].T, m_w_up[0].T, v_w_up[0].T, "adamw_w_up",
        comm=_Comm([{"kind": "pair4", "arr": d_w_in_x}]))
    res_up = [r.T for r in res_up]
    own_in, chip_part = _pair_reduce(from_sibling, d_w_in_t, "pair_reduce_w_in")
    dx, (from_chips, small_g) = _mm(d_uh, w_in_t, b_first=UA_W, tk=256, addend=dz1, addend_scale=ALPHA,
                                    name="bwd_in_dx_hgrn",
                                    comm=_Comm([{"kind": "chips3", "arr": chip_part},
                                                {"kind": "gather", "arr": small_local}]))
    dx = _mm(d_ua, w_in_t, addend=dx, tk=UA_W, name="bwd_in_dx_attn")

    res_in = [r.T for r in _chip_sum_adamw(from_chips, own_in, w_in[0].T, m_w_in[0].T, v_w_in[0].T, "adamw_w_in")]
    res_o = _sum_shards_adamw([recv_o], d_w_o, w_o[0], m_w_o[0], v_w_o[0], "adamw_w_o")
    res_down = _sum_shards_adamw([recv_down], d_w_down, w_down[0], m_w_down[0], v_w_down[0],
                                 "adamw_w_down")
    res_cw = _conv_w_update(recv_cw, conv_w[0], m_conv_w[0], v_conv_w[0], "adamw_conv_w")
    big = {"w_in": [r[None] for r in res_in], "w_up": [r[None] for r in res_up],
           "w_o": [r[None] for r in res_o], "w_down": [r[None] for r in res_down],
           "conv_w": list(res_cw)}

    loss11, small = _small_update(
        small_g,
        {"ln1_g": ln1_g, "ln1_b": ln1_b, "b_in": b_in, "sinks": sinks, "hgrn_lb": hgrn_lb,
         "hgrn_norm_g": hgrn_norm_g, "ln2_g": ln2_g, "ln2_b": ln2_b, "conv_b": conv_b},
        {"ln1_g": m_ln1_g, "ln1_b": m_ln1_b, "b_in": m_b_in, "sinks": m_sinks, "hgrn_lb": m_hgrn_lb,
         "hgrn_norm_g": m_hgrn_norm_g, "ln2_g": m_ln2_g, "ln2_b": m_ln2_b, "conv_b": m_conv_b},
        {"ln1_g": v_ln1_g, "ln1_b": v_ln1_b, "b_in": v_b_in, "sinks": v_sinks, "hgrn_lb": v_hgrn_lb,
         "hgrn_norm_g": v_hgrn_norm_g, "ln2_g": v_ln2_g, "ln2_b": v_ln2_b, "conv_b": v_conv_b},
        "adamw_small")
    loss = loss11[0, 0]

    order = ["ln1_g", "ln1_b", "w_in", "b_in", "sinks", "hgrn_lb", "hgrn_norm_g", "w_o", "ln2_g", "ln2_b",
             "w_up", "conv_w", "conv_b", "w_down"]

    def pick(idx):
        return [big[n][idx] if n in big else small[n][idx] for n in order]

    return (loss, dx[None], *pick(0), *pick(1), *pick(2), *pick(3))
```

```python
import functools

import jax
import jax.numpy as jnp
import numpy as np
from jax import lax
from jax.experimental import pallas as pl
from jax.experimental.pallas import tpu as pltpu

F32 = jnp.float32
BF16 = jnp.bfloat16

N_DEV = 8
D_MODEL = 1024
D_FF = 2816
ATTN_W = 512
KV_W = 128
UA_W = ATTN_W + 2 * KV_W
UH_W = 2048
HG_W = 512
ATTN_BLOCK = 128
HGRN_CHUNK = 64
HGRN_SUB = 16
HGRN_CHUNKS_PER_STEP = 4
EXP_CLAMP = 85.0
NEG_BIG = -1e30
LN_EPS = 1e-5
RMS_EPS = 1e-6
ALPHA = 2.0 ** 0.25
ATTN_SCALE = 0.125
ROPE_THETA = 500000.0

ADAM_LR = 0.001
ADAM_B1 = 0.9
ADAM_B2 = 0.999
ADAM_EPS = 1e-08
ADAM_WD = 0.01
ADAM_STEP = 10

LANES = 128
VMEM_LIMIT_BYTES = 56 * 1024 * 1024

SHARD_IN = D_FF // N_DEV
SHARD_UP = 2 * D_FF // N_DEV
SHARD_O = D_MODEL // N_DEV
SHARD_DOWN = D_FF // N_DEV
SMALL_ROWS = 88

_MESH = pl.DeviceIdType.MESH
_NT = (((1,), (1,)), ((), ()))
_NN = (((1,), (0,)), ((), ()))
_TN = (((0,), (0,)), ((), ()))


def _cp(*sem):
    if sem:
        return pltpu.CompilerParams(dimension_semantics=sem, vmem_limit_bytes=VMEM_LIMIT_BYTES)
    return pltpu.CompilerParams(vmem_limit_bytes=VMEM_LIMIT_BYTES)


def _sig(x):
    return 0.5 * jnp.tanh(0.5 * x) + 0.5


def _dsilu(x, s):
    return s * (1.0 + x * (1.0 - s))


def _dot(a, b, dims):
    return lax.dot_general(a.astype(BF16), b.astype(BF16), dims, preferred_element_type=F32)


def _split(a):
    hi = a.astype(BF16)
    return hi, (a - hi.astype(F32)).astype(BF16)


def _dot3(a, b, dims):
    ah, al = _split(a)
    bh, bl = _split(b)
    d = functools.partial(lax.dot_general, dimension_numbers=dims, preferred_element_type=F32)
    return d(ah, bh) + (d(ah, bl) + d(al, bh))


def _pick(n, pref):
    for t in pref:
        if t <= n and n % t == 0:
            return t
    return n


def _my_coords():
    return lax.axis_index("x"), lax.axis_index("y"), lax.axis_index("c")


def _peer(k):
    x, y, c = _my_coords()
    return (1 - x if k & 4 else x, 1 - y if k & 2 else y, 1 - c if k & 1 else c)


def _me():
    x, y, c = _my_coords()
    return 4 * x + 2 * y + c


class _Comm:
    def __init__(self, items):
        self.items = []
        for it in items:
            arr = it["arr"]
            full = arr.shape[0] if it["kind"] == "gather" else arr.shape[1]
            first, count = it.get("rows", (0, full))
            self.items.append(dict(kind=it["kind"], arr=arr, first=first, count=count,
                                   dst_rows=it.get("dst_rows", count), dst_first=it.get("dst_first", 0),
                                   into=it.get("into")))
        self.n = len(self.items)
        self.arrays = [it["arr"] for it in self.items]
        self.intos = [(a, it["into"]) for a, it in enumerate(self.items) if it["into"] is not None]

    def out_shapes(self):
        return [jax.ShapeDtypeStruct((4 if it["kind"] in ("pair4", "chips3") else N_DEV, it["dst_rows"],
                                      it["arr"].shape[-1]), it["arr"].dtype) for it in self.items]

    def specs(self, n=None):
        return [pl.BlockSpec(memory_space=pl.ANY)] * (self.n if n is None else n)

    def scratch(self):
        return [pltpu.SemaphoreType.DMA(((N_DEV - 1) * self.n,)), pltpu.SemaphoreType.DMA(((N_DEV - 1) * self.n,)),
                pltpu.SemaphoreType.DMA((self.n,))]

    def _src(self, a, ref, dev):
        it = self.items[a]
        blk = ref if it["kind"] == "gather" else ref.at[dev]
        return blk.at[pl.ds(it["first"], it["count"])]

    def _dst(self, a, ref, slot):
        it = self.items[a]
        return ref.at[slot].at[pl.ds(it["dst_first"], it["count"])]

    def _copy(self, a, k, src, dst, sems, me, slot):
        other = jnp.bitwise_xor(me, k)
        idx = a * (N_DEV - 1) + k - 1
        return pltpu.make_async_remote_copy(
            src_ref=self._src(a, src, other), dst_ref=self._dst(a, dst, me if slot == "mine" else other),
            send_sem=sems[0].at[idx], recv_sem=sems[1].at[idx], device_id=_peer(k), device_id_type=_MESH)

    def _pass_on(self, a, k, dst, sems, me):
        slot = self._dst(a, dst, jnp.bitwise_xor(me, k))
        idx = a * (N_DEV - 1) + k
        return pltpu.make_async_remote_copy(
            src_ref=slot, dst_ref=slot, send_sem=sems[0].at[idx], recv_sem=sems[1].at[idx],
            device_id=_peer(1), device_id_type=_MESH)

    def _part(self, a, r, src, dst, sems, me):
        it = self.items[a]
        idx = a * (N_DEV - 1) + r
        if it["kind"] == "pair4":
            k, slot = 1, jnp.bitwise_xor(jnp.bitwise_xor(me, 1), 2 * r)
        else:
            k, slot = 2 * r, r
        return pltpu.make_async_remote_copy(
            src_ref=src.at[slot].at[pl.ds(it["first"], it["count"])], dst_ref=self._dst(a, dst, r),
            send_sem=sems[0].at[idx], recv_sem=sems[1].at[idx], device_id=_peer(k), device_id_type=_MESH)

    def _parts(self, a):
        return range(4) if self.items[a]["kind"] == "pair4" else range(1, 4)

    def _local(self, a, src, dst, sems, me):
        return pltpu.make_async_copy(self._src(a, src, me), self._dst(a, dst, me), sems[2].at[a])

    def start(self, srcs, dsts, sems):
        me = _me()
        for a, (src, dst) in enumerate(zip(srcs, dsts)):
            if self.items[a]["kind"] in ("pair4", "chips3"):
                for r in self._parts(a):
                    self._part(a, r, src, dst, sems, me).start()
                continue
            direct = (1, 2, 4, 6) if self.items[a]["kind"] == "gather" else range(1, N_DEV)
            self._local(a, src, dst, sems, me).start()
            for k in direct:
                self._copy(a, k, src, dst, sems, me, "mine").start()

    def wait(self, srcs, dsts, sems):
        me = _me()
        for a, (src, dst) in enumerate(zip(srcs, dsts)):
            if self.items[a]["kind"] in ("pair4", "chips3"):
                for r in self._parts(a):
                    self._part(a, r, src, dst, sems, me).wait_recv()
                for r in self._parts(a):
                    self._part(a, r, src, dst, sems, me).wait_send()
                continue
            if self.items[a]["kind"] == "gather":
                for k in (2, 4, 6):
                    self._copy(a, k, src, dst, sems, me, "theirs").wait_recv()
                    self._pass_on(a, k, dst, sems, me).start()
                for k in (1, 3, 5, 7):
                    self._copy(a, k, src, dst, sems, me, "theirs").wait_recv()
                for k in (1, 2, 4, 6):
                    self._copy(a, k, src, dst, sems, me, "mine").wait_send()
                for k in (2, 4, 6):
                    self._pass_on(a, k, dst, sems, me).wait_send()
            else:
                for k in range(1, N_DEV):
                    self._copy(a, k, src, dst, sems, me, "theirs").wait_recv()
                for k in range(1, N_DEV):
                    self._copy(a, k, src, dst, sems, me, "mine").wait_send()
            self._local(a, src, dst, sems, me).wait()


def _call(body, *, name, grid, ins, in_specs, out_specs, out_shape, scratch_shapes=(), sem, comm=None):
    n_in, n_out, n_scr = len(ins), len(out_shape), len(scratch_shapes)
    if comm is None:
        outs = pl.pallas_call(
            body, name=name, grid=grid, in_specs=list(in_specs), out_specs=list(out_specs),
            out_shape=list(out_shape), scratch_shapes=list(scratch_shapes), compiler_params=_cp(*sem))(*ins)
        return list(outs), []
    nc, n_into = comm.n, len(comm.intos)

    def hosted(*refs):
        pos = n_in
        c_in = refs[pos:pos + nc]
        pos += nc + n_into
        outs = refs[pos:pos + n_out]
        pos += n_out
        c_out = refs[pos:pos + nc]
        pos += nc
        scr = refs[pos:pos + n_scr]
        sems = refs[pos + n_scr:]
        ids = [pl.program_id(d) for d in range(len(grid))]
        first = functools.reduce(jnp.logical_and, [i == 0 for i in ids])
        last = functools.reduce(jnp.logical_and, [i == g - 1 for i, g in zip(ids, grid)])

        @pl.when(first)
        def _():
            comm.start(c_in, c_out, sems)

        body(*refs[:n_in], *outs, *scr)

        @pl.when(last)
        def _():
            comm.wait(c_in, c_out, sems)

    aliases = {n_in + nc + j: n_out + a for j, (a, _) in enumerate(comm.intos)}
    outs = pl.pallas_call(
        hosted, name=name, grid=grid, in_specs=list(in_specs) + comm.specs() + comm.specs(n_into),
        out_specs=list(out_specs) + comm.specs(), out_shape=list(out_shape) + comm.out_shapes(),
        scratch_shapes=list(scratch_shapes) + comm.scratch(), input_output_aliases=aliases,
        compiler_params=_cp(*(["arbitrary"] * len(grid))))(*ins, *comm.arrays, *[arr for _, arr in comm.intos])
    return list(outs[:n_out]), list(outs[n_out:])


def _slot_sum(recv_ref, own_ref, shape):
    me = _me()
    acc = jnp.zeros(shape, F32)
    for s in range(N_DEV):
        acc = acc + jnp.where(me == s, own_ref[...], recv_ref[s].astype(F32))
    return acc


def _adamw_math(w, g, m, v):
    nm = ADAM_B1 * m + (1.0 - ADAM_B1) * g
    nv = ADAM_B2 * v + (1.0 - ADAM_B2) * (g * g)
    m_hat = nm / (1.0 - ADAM_B1 ** ADAM_STEP)
    v_hat = nv / (1.0 - ADAM_B2 ** ADAM_STEP)
    return -ADAM_LR * (m_hat / (jnp.sqrt(v_hat) + ADAM_EPS) + ADAM_WD * w), nm, nv


def _pair_reduce(from_sibling, mine, name):
    _, rows, cols = from_sibling.shape
    tr = _pick(rows, (176, 128, 64, 32, 16, 8))
    tiles = rows // tr
    table = jnp.bitwise_xor(_me(), jnp.arange(0, N_DEV, 2, dtype=jnp.int32))

    def body(tbl_ref, sib_ref, mine_ref, own_ref, send_ref):
        r = pl.program_id(1)
        total = mine_ref[...] + sib_ref[0].astype(F32)
        send_ref[0] = jnp.where(r == 0, 0.0, total).astype(BF16)

        @pl.when(r == 0)
        def _():
            own_ref[...] = total

    grid_spec = pltpu.PrefetchScalarGridSpec(
        num_scalar_prefetch=1, grid=(tiles, 4),
        in_specs=[pl.BlockSpec((1, tr, cols), lambda i, r, tbl: (r, i, 0)),
                  pl.BlockSpec((tr, cols), lambda i, r, tbl: (tbl[r] * tiles + i, 0))],
        out_specs=[pl.BlockSpec((tr, cols), lambda i, r, tbl: (i, 0)),
                   pl.BlockSpec((1, tr, cols), lambda i, r, tbl: (r, i, 0))])
    return pl.pallas_call(
        body, name=name, grid_spec=grid_spec,
        out_shape=[jax.ShapeDtypeStruct((rows, cols), F32), jax.ShapeDtypeStruct((4, rows, cols), BF16)],
        compiler_params=_cp("arbitrary", "arbitrary"),
    )(table, from_sibling, mine)


def _chip_sum_adamw(from_chips, own, w, m, v, name):
    _, rows, cols = from_chips.shape
    tr = _pick(rows, (176, 128, 64, 32, 16, 8))

    def body(recv_ref, own_ref, w_ref, m_ref, v_ref, g_ref, d_ref, nm_ref, nv_ref):
        g = own_ref[...]
        for r in range(1, 4):
            g = g + recv_ref[r].astype(F32)
        g_ref[...] = g
        d_ref[...], nm_ref[...], nv_ref[...] = _adamw_math(w_ref[...], g, m_ref[...], v_ref[...])

    spec = pl.BlockSpec((tr, cols), lambda i: (i, 0))
    shp = jax.ShapeDtypeStruct((rows, cols), F32)
    return pl.pallas_call(
        body, name=name, grid=(rows // tr,),
        in_specs=[pl.BlockSpec((4, tr, cols), lambda i: (0, i, 0)), spec, spec, spec, spec],
        out_specs=[spec, spec, spec, spec], out_shape=[shp, shp, shp, shp],
        compiler_params=_cp("parallel"),
    )(from_chips, own, w, m, v)


def _sum_shards_adamw(recvs, own, w, m, v, name, comm=None):
    rows_p, cols = recvs[0].shape[1], recvs[0].shape[2]
    n_p = len(recvs)
    tr = _pick(rows_p, (176, 128, 64, 32, 16, 8))
    tiles = rows_p // tr

    def body(*refs):
        recv_refs = refs[:n_p]
        own_ref, w_ref, m_ref, v_ref, g_ref, d_ref, nm_ref, nv_ref = refs[n_p:]
        for j in range(n_p):
            @pl.when(pl.program_id(0) == j)
            def _():
                g = _slot_sum(recv_refs[j], own_ref, (tr, cols))
                g_ref[...] = g
                d_ref[...], nm_ref[...], nv_ref[...] = _adamw_math(w_ref[...], g, m_ref[...], v_ref[...])

    spec = pl.BlockSpec((tr, cols), lambda p_, i: (p_ * tiles + i, 0))
    own_spec = pl.BlockSpec((tr, cols), lambda p_, i: (_me() * (n_p * tiles) + p_ * tiles + i, 0))
    shp = jax.ShapeDtypeStruct((rows_p * n_p, cols), F32)
    outs, couts = _call(
        body, name=name, grid=(n_p, tiles), ins=[*recvs, own, w, m, v],
        in_specs=[pl.BlockSpec((N_DEV, tr, cols), functools.partial(lambda p_, i, j: (0, jnp.where(p_ == j, i, 0), 0), j=j))
                  for j in range(n_p)] + [own_spec, spec, spec, spec],
        out_specs=[spec, spec, spec, spec], out_shape=[shp, shp, shp, shp],
        sem=("arbitrary", "arbitrary"), comm=comm)
    return outs if comm is None else (outs, couts)


def _mm(a, b, *, name, ta=False, tb=False, out_dtype=F32, out_dtype2=None, bias=None, addend=None,
        addend_scale=1.0, tm=1024, tn=1024, tk=1024, comm=None, out_rows=None, first_row=0, into=None,
        b_first=0):
    kdim, m = a.shape if ta else a.shape[::-1]
    n = b.shape[0] if tb else b.shape[1]
    tm = _pick(m, (tm, 1408, 1024, 768, 512, 256, 128))
    tn = _pick(n, (tn, 1408, 1024, 768, 512, 256, 128))
    tk = _pick(kdim, (tk, 1408, 1024, 768, 512, 256, 128))
    nk = kdim // tk
    a_spec = pl.BlockSpec((tk, tm), lambda i, j, k: (k, i)) if ta else pl.BlockSpec((tm, tk), lambda i, j, k: (i, k))
    assert b_first % tk == 0 and not (tb and b_first)
    kb0 = b_first // tk
    b_spec = pl.BlockSpec((tn, tk), lambda i, j, k: (j, k)) if tb else pl.BlockSpec((tk, tn), lambda i, j, k: (k + kb0, j))
    ins, specs = [a, b], [a_spec, b_spec]
    if bias is not None:
        ins.append(bias)
        specs.append(pl.BlockSpec((1, tn), lambda i, j, k: (0, j)))
    if addend is not None:
        ins.append(addend)
        specs.append(pl.BlockSpec((tm, tn), lambda i, j, k: (i, j)))
    dims = (((0,) if ta else (1,), (1,) if tb else (0,)), ((), ()))
    has_bias, has_addend, two = bias is not None, addend is not None, out_dtype2 is not None

    def body(*refs):
        a_ref, b_ref = refs[0], refs[1]
        pos = 2
        bias_ref = addend_ref = None
        if has_bias:
            bias_ref = refs[pos]
            pos += 1
        if has_addend:
            addend_ref = refs[pos]
            pos += 1
        o_refs, acc_ref = refs[pos:-1], refs[-1]
        k = pl.program_id(2)

        @pl.when(k == 0)
        def _():
            acc_ref[...] = jnp.zeros_like(acc_ref)

        acc_ref[...] += _dot(a_ref[...], b_ref[...], dims)

        @pl.when(k == nk - 1)
        def _():
            r = acc_ref[...]
            if has_bias:
                r = r + bias_ref[...]
            if has_addend:
                r = r + addend_scale * addend_ref[...].astype(F32)
            for o_ref in o_refs:
                o_ref[...] = r.astype(o_ref.dtype)

    blk0 = first_row // tm
    dtypes = [out_dtype] + ([out_dtype2] if two else [])
    ospec = pl.BlockSpec((tm, tn), lambda i, j, k: (i + blk0, j))
    shapes = [jax.ShapeDtypeStruct((m if out_rows is None else out_rows, n), d) for d in dtypes]
    if into is not None:
        n_in = len(ins)
        outs = pl.pallas_call(
            lambda *refs: body(*refs[:n_in], *refs[n_in + len(into):]), name=name, grid=(m // tm, n // tn, nk),
            in_specs=specs + [pl.BlockSpec(memory_space=pl.ANY)] * len(into), out_specs=[ospec] * len(dtypes),
            out_shape=shapes, scratch_shapes=[pltpu.VMEM((tm, tn), F32)],
            input_output_aliases={n_in + j: j for j in range(len(into))},
            compiler_params=_cp("parallel", "parallel", "arbitrary"))(*ins, *into)
        return tuple(outs) if two else outs[0]
    outs, couts = _call(
        body, name=name, grid=(m // tm, n // tn, nk), ins=ins, in_specs=specs,
        out_specs=[ospec] * len(dtypes), out_shape=shapes,
        scratch_shapes=[pltpu.VMEM((tm, tn), F32)], sem=("parallel", "parallel", "arbitrary"), comm=comm)
    primary = tuple(outs) if two else outs[0]
    return (primary, couts) if comm is not None else primary


def _rope_lane_constants():
    inv_freq = np.float32(ROPE_THETA) ** (-np.arange(8, dtype=np.float32) * np.float32(2.0 / 16.0))
    lane = np.arange(LANES) % 64
    freq = np.where(lane < 16, inv_freq[lane % 8], 0.0).astype(np.float32)
    sign = np.where(lane < 8, -1.0, np.where(lane < 16, 1.0, 0.0)).astype(np.float32)
    return jnp.asarray(freq)[None, :], jnp.asarray(sign)[None, :]


def _prep(pos_col, x2, name, comm):
    t, d = x2.shape
    tr = _pick(t, (512, 256, 128))
    freq, sign = _rope_lane_constants()

    def body(pos_ref, freq_ref, sign_ref, x_ref, c_ref, s_ref, xb_ref):
        ang = pos_ref[...].astype(F32) * freq_ref[...]
        c_ref[...] = jnp.cos(ang)
        s_ref[...] = sign_ref[...] * jnp.sin(ang)
        xb_ref[...] = x_ref[...].astype(BF16)

    tab = pl.BlockSpec((tr, LANES), lambda i: (i, 0))
    return _call(
        body, name=name, grid=(t // tr,), ins=[pos_col, freq, sign, x2],
        in_specs=[pl.BlockSpec((tr, 1), lambda i: (i, 0)), pl.BlockSpec((1, LANES), lambda i: (0, 0)),
                  pl.BlockSpec((1, LANES), lambda i: (0, 0)), pl.BlockSpec((tr, d), lambda i: (i, 0))],
        out_specs=[tab, tab, pl.BlockSpec((tr, d), lambda i: (i, 0))],
        out_shape=[jax.ShapeDtypeStruct((t, LANES), F32), jax.ShapeDtypeStruct((t, LANES), F32),
                   jax.ShapeDtypeStruct((t, d), BF16)],
        sem=("parallel",), comm=comm)


def _swap8(t):
    width = t.shape[1]
    lane = jnp.bitwise_and(lax.broadcasted_iota(jnp.int32, t.shape, 1), 63)
    return jnp.where(lane < 8, pltpu.roll(t, width - 8, 1), jnp.where(lane < 16, pltpu.roll(t, 8, 1), 0.0))


def _rope(t, c, s):
    return t * c + _swap8(t) * s


def _rope_bwd(d, c, s):
    return d * c + _swap8(d * s)


def _tile4(a):
    return jnp.concatenate([a, a, a, a], axis=1)


def _attn_band(n, k_cur, k_prev, v_cur, v_prev, c_cur, s_cur, c_prev, s_prev):
    kband = jnp.concatenate([_rope(k_prev, c_prev, s_prev), _rope(k_cur, c_cur, s_cur)], axis=0)
    vband = jnp.concatenate([v_prev, v_cur], axis=0)
    qi = lax.broadcasted_iota(jnp.int32, (ATTN_BLOCK, 2 * ATTN_BLOCK), 0)
    kj = lax.broadcasted_iota(jnp.int32, (ATTN_BLOCK, 2 * ATTN_BLOCK), 1)
    dist = qi + ATTN_BLOCK - kj
    valid = (dist >= 0) & (dist < ATTN_BLOCK) & (n * ATTN_BLOCK - ATTN_BLOCK + kj >= 0)
    return (kband.astype(BF16), pltpu.roll(kband, 64, 1).astype(BF16),
            vband.astype(BF16), pltpu.roll(vband, 64, 1).astype(BF16), valid, kband)


def _attn_probs(raw, valid, sink, axis):
    s = jnp.where(valid, raw * ATTN_SCALE, NEG_BIG)
    m = jnp.maximum(jnp.max(s, axis=axis, keepdims=True), sink)
    p = jnp.exp(s - m)
    esink = jnp.exp(sink - m)
    z = jnp.sum(p, axis=axis, keepdims=True) + esink
    return p / z, esink / z


def _attn_valid_t(n):
    kj = lax.broadcasted_iota(jnp.int32, (2 * ATTN_BLOCK, ATTN_BLOCK), 0)
    qi = lax.broadcasted_iota(jnp.int32, (2 * ATTN_BLOCK, ATTN_BLOCK), 1)
    dist = qi + ATTN_BLOCK - kj
    return (dist >= 0) & (dist < ATTN_BLOCK) & (n * ATTN_BLOCK - ATTN_BLOCK + kj >= 0)


def _attn_specs(nb):
    def cur(col, width=KV_W):
        return pl.BlockSpec((ATTN_BLOCK, width), lambda n: (jnp.minimum(n, nb - 1), col))

    def prev(col):
        return pl.BlockSpec((ATTN_BLOCK, KV_W), lambda n: (jnp.maximum(n - 1, 0), col))

    ua_specs = [cur(0, ATTN_W), cur(4), prev(4), cur(5), prev(5)]
    tab_specs = [cur(0), cur(0), prev(0), prev(0)]
    return ua_specs, tab_specs


def _attn_fwd(ua, ctab, stab, sinks, name, comm=None):
    t = ua.shape[0]
    nb = t // ATTN_BLOCK
    ua_specs, tab_specs = _attn_specs(nb)

    def body(q_ref, kc_ref, kp_ref, vc_ref, vp_ref, cc_ref, sc_ref, cp_ref, sp_ref, sink_ref, o_ref, o_t_ref):
        n = pl.program_id(0)
        cc, sc = cc_ref[...], sc_ref[...]
        kb, kb_r, vb, vb_r, valid, _ = _attn_band(n, kc_ref[...], kp_ref[...], vc_ref[...], vp_ref[...],
                                                  cc, sc, cp_ref[...], sp_ref[...])
        qr = _rope(q_ref[...], _tile4(cc), _tile4(sc))
        lo = lax.broadcasted_iota(jnp.int32, (ATTN_BLOCK, LANES), 1) < 64
        heads = []
        for j in range(4):
            qj = qr[:, j * LANES:(j + 1) * LANES]
            for is_lo in (True, False):
                aligned = is_lo == (j < 2)
                qm = jnp.where(lo if is_lo else jnp.logical_not(lo), qj, 0.0).astype(BF16)
                raw = lax.dot_general(qm, kb if aligned else kb_r, _NT, preferred_element_type=F32)
                heads.append((raw, vb if aligned else vb_r, sink_ref[0, len(heads)]))
        halves = []
        for raw, vv, sink in heads:
            probs, _ = _attn_probs(raw, valid, sink, 1)
            halves.append(lax.dot_general(probs.astype(BF16), vv, _NN, preferred_element_type=F32))
        outs = [jnp.where(lo, halves[2 * j], halves[2 * j + 1]) for j in range(4)]
        o_ref[...] = jnp.concatenate(outs, axis=1).astype(o_ref.dtype)
        for j in range(4):
            o_t_ref[j * LANES:(j + 1) * LANES, :] = outs[j].T.astype(o_t_ref.dtype)

    return _call(
        body, name=name, grid=(nb,), ins=[ua, ua, ua, ua, ua, ctab, stab, ctab, stab, sinks],
        in_specs=ua_specs + tab_specs + [pl.BlockSpec(memory_space=pltpu.SMEM)],
        out_specs=[pl.BlockSpec((ATTN_BLOCK, ATTN_W), lambda n: (n, 0)),
                   pl.BlockSpec((ATTN_W, ATTN_BLOCK), lambda n: (0, n))],
        out_shape=[jax.ShapeDtypeStruct((t, ATTN_W), BF16), jax.ShapeDtypeStruct((ATTN_W, t), BF16)],
        sem=("parallel",), comm=comm)


def _attn_bwd(ua, d_out, ctab, stab, sinks, name, comm=None):
    t = ua.shape[0]
    nb = t // ATTN_BLOCK
    ua_specs, tab_specs = _attn_specs(nb)

    def body(q_ref, kc_ref, kp_ref, vc_ref, vp_ref, cc_ref, sc_ref, cp_ref, sp_ref, do_ref, sink_ref,
             dua_ref, dua_t_ref, dbias_ref, dsink_ref, dq_c, dk_c, dv_c, dq_n, dk_n, dv_n):
        n = pl.program_id(0)

        @pl.when(n == 0)
        def _():
            dq_c[...] = jnp.zeros_like(dq_c)
            dk_c[...] = jnp.zeros_like(dk_c)
            dv_c[...] = jnp.zeros_like(dv_c)
            dbias_ref[...] = jnp.zeros_like(dbias_ref)
            dsink_ref[...] = jnp.zeros_like(dsink_ref)

        @pl.when(n == nb)
        def _():
            dq_n[...] = jnp.zeros_like(dq_n)
            dk_n[...] = jnp.zeros_like(dk_n)
            dv_n[...] = jnp.zeros_like(dv_n)

        @pl.when(n < nb)
        def _():
            cc, sc = cc_ref[...], sc_ref[...]
            kb, kb_r, vb, vb_r, _, kb_f32 = _attn_band(n, kc_ref[...], kp_ref[...], vc_ref[...], vp_ref[...],
                                                       cc, sc, cp_ref[...], sp_ref[...])
            valid_t = _attn_valid_t(n)
            c4, s4 = _tile4(cc), _tile4(sc)
            qr = _rope(q_ref[...], c4, s4)
            do = do_ref[...].astype(F32)
            lane = lax.broadcasted_iota(jnp.int32, (ATTN_BLOCK, LANES), 1)
            lo = lane < 64
            lane_row = lax.broadcasted_iota(jnp.int32, (1, LANES), 1)
            k_t = {False: kb_f32.T.astype(BF16), True: pltpu.roll(kb_f32, 64, 1).T.astype(BF16)}
            heads = []
            for j in range(4):
                qj = qr[:, j * LANES:(j + 1) * LANES]
                doj = do[:, j * LANES:(j + 1) * LANES]
                for is_lo in (True, False):
                    aligned = is_lo == (j < 2)
                    msk = lo if is_lo else jnp.logical_not(lo)
                    kk = kb if aligned else kb_r
                    vv = vb if aligned else vb_r
                    qm = jnp.where(msk, qj, 0.0).astype(BF16)
                    dom = jnp.where(msk, doj, 0.0).astype(BF16)
                    heads.append(dict(
                        aligned=aligned, qm=qm, dom=dom, sink=sink_ref[0, len(heads)],
                        raw_t=lax.dot_general(kk, qm, _NT, preferred_element_type=F32),
                        dp_t=lax.dot_general(vv, dom, _NT, preferred_element_type=F32)))
            dk_band = jnp.zeros((2 * ATTN_BLOCK, LANES), F32)
            dv_band = jnp.zeros((2 * ATTN_BLOCK, LANES), F32)
            dsink = jnp.zeros((1, LANES), F32)
            for head, hd in enumerate(heads):
                probs_t, psink = _attn_probs(hd["raw_t"], valid_t, hd["sink"], 0)
                delta_t = jnp.sum(probs_t * hd["dp_t"], axis=0, keepdims=True)
                hd["ds_t"] = (probs_t * (hd["dp_t"] - delta_t) * ATTN_SCALE).astype(BF16)
                dsink = dsink + jnp.where(lane_row == head, -jnp.sum(psink * delta_t), 0.0)
                dk_h = lax.dot_general(hd["ds_t"], hd["qm"], _NN, preferred_element_type=F32)
                dv_h = lax.dot_general(probs_t.astype(BF16), hd["dom"], _NN, preferred_element_type=F32)
                if not hd["aligned"]:
                    dk_h = pltpu.roll(dk_h, 64, 1)
                    dv_h = pltpu.roll(dv_h, 64, 1)
                dk_band = dk_band + dk_h
                dv_band = dv_band + dv_h
            row_lo = lax.broadcasted_iota(jnp.int32, (LANES, ATTN_BLOCK), 0) < 64
            dq_t = [lax.dot_general(k_t[not hd["aligned"]], hd["ds_t"], _NN, preferred_element_type=F32)
                    for hd in heads]
            dqs = [jnp.where(row_lo, dq_t[2 * j], dq_t[2 * j + 1]).T for j in range(4)]
            dq_n[...] = _rope_bwd(jnp.concatenate(dqs, axis=1), c4, s4)
            dk_n[...] = dk_band
            dv_n[...] = dv_band
            dsink_ref[...] += dsink

        dk_prev = _rope_bwd(dk_c[...] + dk_n[0:ATTN_BLOCK, :], cp_ref[...], sp_ref[...])
        dv_prev = dv_c[...] + dv_n[0:ATTN_BLOCK, :]
        full = jnp.concatenate([dq_c[...], dk_prev, dv_prev], axis=1)
        dua_ref[...] = full.astype(dua_ref.dtype)
        for j in range(UA_W // LANES):
            dua_t_ref[j * LANES:(j + 1) * LANES, :] = full[:, j * LANES:(j + 1) * LANES].T.astype(dua_t_ref.dtype)
        dbias_ref[...] += jnp.sum(full, axis=0, keepdims=True)
        dq_c[...] = dq_n[...]
        dk_c[...] = dk_n[ATTN_BLOCK:, :]
        dv_c[...] = dv_n[ATTN_BLOCK:, :]

    return _call(
        body, name=name, grid=(nb + 1,), ins=[ua, ua, ua, ua, ua, ctab, stab, ctab, stab, d_out, sinks],
        in_specs=ua_specs + tab_specs + [
            pl.BlockSpec((ATTN_BLOCK, ATTN_W), lambda n: (jnp.minimum(n, nb - 1), 0)),
            pl.BlockSpec(memory_space=pltpu.SMEM)],
        out_specs=[pl.BlockSpec((ATTN_BLOCK, UA_W), lambda n: (jnp.maximum(n - 1, 0), 0)),
                   pl.BlockSpec((UA_W, ATTN_BLOCK), lambda n: (0, jnp.maximum(n - 1, 0))),
                   pl.BlockSpec((1, UA_W), lambda n: (0, 0)),
                   pl.BlockSpec((1, LANES), lambda n: (0, 0))],
        out_shape=[jax.ShapeDtypeStruct((t, UA_W), BF16), jax.ShapeDtypeStruct((UA_W, t), BF16),
                   jax.ShapeDtypeStruct((1, UA_W), F32),
                   jax.ShapeDtypeStruct((1, LANES), F32)],
        scratch_shapes=[pltpu.VMEM((ATTN_BLOCK, ATTN_W), F32), pltpu.VMEM((ATTN_BLOCK, KV_W), F32),
                        pltpu.VMEM((ATTN_BLOCK, KV_W), F32), pltpu.VMEM((ATTN_BLOCK, ATTN_W), F32),
                        pltpu.VMEM((2 * ATTN_BLOCK, KV_W), F32), pltpu.VMEM((2 * ATTN_BLOCK, KV_W), F32)],
        sem=("arbitrary",), comm=comm)


def _tri_mats():
    r = lax.broadcasted_iota(jnp.int32, (HGRN_CHUNK, LANES), 0)
    c = lax.broadcasted_iota(jnp.int32, (HGRN_CHUNK, LANES), 1)
    lower = ((c <= r) & (c < HGRN_CHUNK)).astype(F32)
    upper = ((c >= r) & (c < HGRN_CHUNK)).astype(F32)
    return lower, upper


def _tri_apply(tri, g):
    pad = jnp.concatenate([g, jnp.zeros_like(g)], axis=0)
    return lax.dot_general(tri, pad, _NN, precision=lax.Precision.HIGHEST, preferred_element_type=F32)


def _sub_masks():
    s = lax.broadcasted_iota(jnp.int32, (HGRN_CHUNK, LANES), 0)
    tt = lax.broadcasted_iota(jnp.int32, (HGRN_CHUNK, LANES), 1)
    return [(tt >= HGRN_SUB * i) & (tt < HGRN_SUB * (i + 1)) & (s <= tt) for i in range(HGRN_CHUNK // HGRN_SUB)]


def _hgrn_gates(hq, hf, lb_ref, b_scr):
    lb = _sig(lb_ref[0:1, :] - lb_ref[1:2, :])
    q = hq * _sig(hq)
    sg = _sig(hf)
    f = lb + (1.0 - lb) * sg
    k = 1.0 - f
    lower, _ = _tri_mats()
    b = _tri_apply(lower, jnp.log(f))
    b_scr[...] = b
    nsub = HGRN_CHUNK // HGRN_SUB
    starts = [jnp.zeros((1, HG_W), F32)] + [b_scr[HGRN_SUB * i - 1:HGRN_SUB * i, :] for i in range(1, nsub)]
    pq = jnp.concatenate([jnp.broadcast_to(p, (HGRN_SUB, HG_W)) for p in starts], axis=0)
    b_last = b_scr[HGRN_CHUNK - 1:HGRN_CHUNK, :]
    e_q = jnp.exp(b - pq)
    e_k = [jnp.exp(jnp.minimum(p - b, EXP_CLAMP)) for p in starts]
    e_b = jnp.exp(b)
    e_bl = jnp.exp(b_last - b)
    e_last = jnp.exp(b_last)
    return q, sg, f, k, lb, e_q, e_k, e_b, e_bl, e_last


def _sub_masks_ts():
    tt = lax.broadcasted_iota(jnp.int32, (HGRN_CHUNK, LANES), 0)
    s = lax.broadcasted_iota(jnp.int32, (HGRN_CHUNK, LANES), 1)
    return [(tt >= HGRN_SUB * i) & (tt < HGRN_SUB * (i + 1)) & (s <= tt) for i in range(HGRN_CHUNK // HGRN_SUB)]


def _masked_sum(blocks, masks, axis):
    step = HGRN_CHUNK if axis == 0 else LANES
    acc = jnp.zeros((HGRN_CHUNK, LANES), F32)
    for i, msk in enumerate(masks):
        blk = blocks[step * i:step * (i + 1), :] if axis == 0 else blocks[:, step * i:step * (i + 1)]
        acc = acc + jnp.where(msk, blk, 0.0)
    return acc


def _store_transposed(out_t_ref, chunk_rows):
    width = chunk_rows[0].shape[1]
    if len(chunk_rows) == 1:
        groups = [jnp.concatenate([chunk_rows[0], jnp.zeros_like(chunk_rows[0])], axis=0)]
    else:
        groups = [jnp.concatenate(chunk_rows[g:g + 2], axis=0) for g in range(0, len(chunk_rows), 2)]
    for g, rows in enumerate(groups):
        for c in range(width // LANES):
            tile = rows[:, c * LANES:(c + 1) * LANES].T.astype(out_t_ref.dtype)
            if len(chunk_rows) == 1:
                out_t_ref[c * LANES:(c + 1) * LANES, :] = tile[:, 0:HGRN_CHUNK]
            else:
                out_t_ref[c * LANES:(c + 1) * LANES, g * LANES:(g + 1) * LANES] = tile


def _hgrn_chunk_inputs(j, hq_ref, hf_ref, hi_ref, hg_ref, lb_ref, b_scr):
    rows = slice(j * HGRN_CHUNK, (j + 1) * HGRN_CHUNK)
    hq, hf, v, hg = hq_ref[rows, :], hf_ref[rows, :], hi_ref[rows, :], hg_ref[rows, :]
    q, sg, f, k, lb, e_q, e_k, e_b, e_bl, e_last = _hgrn_gates(hq, hf, lb_ref, b_scr.at[j])
    return dict(rows=rows, hq=hq, v=v, hg=hg, q=q, sg=sg, f=f, k=k, lb=lb, e_q=e_q, e_k=e_k, e_b=e_b, e_bl=e_bl,
                e_last=e_last, qt=q * e_q, qb=q * e_b, kd=k * e_bl, khat=[k * e for e in e_k])


def _hgrn_fwd(uh, lb_raw, norm_g, name, comm=None):
    t = uh.shape[0]
    nc = t // HGRN_CHUNK
    cps = _pick(nc, (HGRN_CHUNKS_PER_STEP, 2, 1))
    rows_step = cps * HGRN_CHUNK

    def body(hq_ref, hf_ref, hi_ref, hg_ref, lb_ref, ng_ref, r_ref, r_t_ref, o_ref, st_out_ref, st_ref, b_scr):
        @pl.when(pl.program_id(0) == 0)
        def _():
            st_ref[...] = jnp.zeros_like(st_ref)

        masks = _sub_masks_ts()
        ng = ng_ref[...]
        zpad = jnp.zeros((HGRN_CHUNK, LANES), F32)
        heads = [slice(h * LANES, (h + 1) * LANES) for h in range(4)]
        chunks = [_hgrn_chunk_inputs(j, hq_ref, hf_ref, hi_ref, hg_ref, lb_ref, b_scr) for j in range(cps)]
        for ch in chunks:
            ch["scores"] = [_dot3(ch["qt"][:, sl],
                                  jnp.concatenate([x for kh in ch["khat"] for x in (kh[:, sl], zpad)], axis=0), _NT)
                            for sl in heads]
        for j, ch in enumerate(chunks):
            o_heads, y_heads = [], []
            for h, sl in enumerate(heads):
                a_ts = _masked_sum(ch["scores"][h], masks, 1)
                vh = ch["v"][:, sl].astype(BF16)
                v_pad = jnp.concatenate([vh, jnp.zeros_like(vh)], axis=0)
                o_intra = lax.dot_general(a_ts.astype(BF16), v_pad, _NN, preferred_element_type=F32)
                st = st_ref[h]
                st_out_ref[j, h] = st
                o_inter = _dot(ch["qb"][:, sl], st, _NT)
                st_ref[h] = st * ch["e_last"][:, sl] + _dot(vh, ch["kd"][:, sl], _TN)
                oh = o_intra + o_inter
                rs = lax.rsqrt(jnp.mean(oh * oh, axis=1, keepdims=True) + RMS_EPS)
                o_heads.append(oh)
                y_heads.append(oh * rs * ng)
            hg = ch["hg"]
            o_ref[ch["rows"], :] = jnp.concatenate(o_heads, axis=1)
            ch["r"] = jnp.concatenate(y_heads, axis=1) * (hg * _sig(hg))
            r_ref[ch["rows"], :] = ch["r"].astype(r_ref.dtype)
        _store_transposed(r_t_ref, [ch["r"] for ch in chunks])

    col = lambda j: pl.BlockSpec((rows_step, HG_W), lambda c: (c, j))
    return _call(
        body, name=name, grid=(nc // cps,), ins=[uh, uh, uh, uh, lb_raw, norm_g],
        in_specs=[col(0), col(1), col(2), col(3),
                  pl.BlockSpec((2, HG_W), lambda c: (0, 0)), pl.BlockSpec((1, LANES), lambda c: (0, 0))],
        out_specs=[pl.BlockSpec((rows_step, HG_W), lambda c: (c, 0)),
                   pl.BlockSpec((HG_W, rows_step), lambda c: (0, c)),
                   pl.BlockSpec((rows_step, HG_W), lambda c: (c, 0)),
                   pl.BlockSpec((cps, 4, LANES, LANES), lambda c: (c, 0, 0, 0))],
        out_shape=[jax.ShapeDtypeStruct((t, HG_W), BF16), jax.ShapeDtypeStruct((HG_W, t), BF16),
                   jax.ShapeDtypeStruct((t, HG_W), F32), jax.ShapeDtypeStruct((nc, 4, LANES, LANES), F32)],
        scratch_shapes=[pltpu.VMEM((4, LANES, LANES), F32), pltpu.VMEM((cps, HGRN_CHUNK, HG_W), F32)],
        sem=("arbitrary",), comm=comm)


def _hgrn_bwd(uh, o_pre, d_r, states, lb_raw, norm_g, name, comm=None):
    t = uh.shape[0]
    nc = t // HGRN_CHUNK
    cps = _pick(nc, (HGRN_CHUNKS_PER_STEP, 2, 1))
    ns = nc // cps
    rows_step = cps * HGRN_CHUNK
    nsub = HGRN_CHUNK // HGRN_SUB

    def body(hq_ref, hf_ref, hi_ref, hg_ref, o_ref, dr_ref, st_in_ref, lb_ref, ng_ref,
             duh_ref, duh_t_ref, dbias_ref, dng_ref, dlb_ref, dst_ref, b_scr, dlb_acc):
        i = pl.program_id(0)

        @pl.when(i == 0)
        def _():
            dst_ref[...] = jnp.zeros_like(dst_ref)
            dbias_ref[...] = jnp.zeros_like(dbias_ref)
            dng_ref[...] = jnp.zeros_like(dng_ref)
            dlb_acc[...] = jnp.zeros_like(dlb_acc)

        masks_st = _sub_masks()
        masks_ts = _sub_masks_ts()
        ng = ng_ref[...]
        zpad = jnp.zeros((HGRN_CHUNK, LANES), F32)
        _, upper = _tri_mats()
        heads = [slice(h * LANES, (h + 1) * LANES) for h in range(4)]
        row = lax.broadcasted_iota(jnp.int32, (HGRN_CHUNK, HG_W), 0)

        chunks = [_hgrn_chunk_inputs(j, hq_ref, hf_ref, hi_ref, hg_ref, lb_ref, b_scr) for j in range(cps)]
        dng = jnp.zeros((1, LANES), F32)
        for ch in chunks:
            o = o_ref[ch["rows"], :]
            dr = dr_ref[ch["rows"], :].astype(F32)
            hg = ch["hg"]
            sgg = _sig(hg)
            dy = dr * (hg * sgg)
            do_h, y_h = [], []
            for sl in heads:
                oh = o[:, sl]
                rs = lax.rsqrt(jnp.mean(oh * oh, axis=1, keepdims=True) + RMS_EPS)
                y_h.append(oh * rs * ng)
                dng = dng + jnp.sum(dy[:, sl] * oh * rs, axis=0, keepdims=True)
                w = dy[:, sl] * ng
                do_h.append(rs * (w - oh * (rs * rs) * jnp.mean(w * oh, axis=1, keepdims=True)))
            ch["do"] = do_h
            ch["dhg"] = dr * jnp.concatenate(y_h, axis=1) * _dsilu(hg, sgg)

        for ch in chunks:
            ch["kst"], ch["kpad"], ch["qt_pad"], ch["v_b"], ch["do_pad"] = [], [], [], [], []
            ch["ats"], ch["d_at"], ch["d_a"] = [], [], []
            for h, sl in enumerate(heads):
                kst = jnp.concatenate([kh[:, sl] for kh in ch["khat"]], axis=0)
                kpad = jnp.concatenate([x for kh in ch["khat"] for x in (kh[:, sl], zpad)], axis=0)
                qt_pad = jnp.concatenate([ch["qt"][:, sl], zpad], axis=0)
                vh = ch["v"][:, sl].astype(BF16)
                v_pad = jnp.concatenate([vh, jnp.zeros_like(vh)], axis=0)
                do_b = ch["do"][h].astype(BF16)
                do_pad = jnp.concatenate([do_b, jnp.zeros_like(do_b)], axis=0)
                ch["kst"].append(kst)
                ch["kpad"].append(kpad)
                ch["qt_pad"].append(qt_pad)
                ch["v_b"].append(vh)
                ch["do_pad"].append(do_pad)
                ch["ats"].append(_dot3(kst, qt_pad, _NT))
                ch["d_at"].append(lax.dot_general(vh, do_pad, _NT, preferred_element_type=F32))
                ch["d_a"].append(lax.dot_general(do_b, v_pad, _NT, preferred_element_type=F32))

        for ch in chunks:
            ch["d_kst"], ch["d_qt"], ch["dv"] = [], [], []
            for h in range(4):
                at = _masked_sum(ch["ats"][h], masks_st, 0)
                d_ats = jnp.concatenate([jnp.where(m, ch["d_at"][h], 0.0) for m in masks_st], axis=0)
                d_a_cat = jnp.concatenate([jnp.where(m, ch["d_a"][h], 0.0) for m in masks_ts], axis=1)
                ch["d_kst"].append(_dot3(d_ats, ch["qt_pad"][h], _NN))
                ch["d_qt"].append(_dot3(d_a_cat, ch["kpad"][h], _NN))
                ch["dv"].append(lax.dot_general(at.astype(BF16), ch["do_pad"][h], _NN, preferred_element_type=F32))

        for j in reversed(range(cps)):
            ch = chunks[j]
            q, k, sg, f, lb = ch["q"], ch["k"], ch["sg"], ch["f"], ch["lb"]
            dq_h, dk_h, dv_h, extra_h = [], [], [], []
            for h, sl in enumerate(heads):
                st_prev = st_in_ref[j, h]
                d_st = dst_ref[h]
                d_st_b = d_st.astype(BF16)
                do_b = ch["do_pad"][h][0:HGRN_CHUNK, :]
                kd, e_last = ch["kd"][:, sl], ch["e_last"][:, sl]
                dv = ch["dv"][h] + _dot(kd, d_st_b, _NT)
                d_qb = _dot(do_b, st_prev, _NN)
                d_kd = lax.dot_general(ch["v_b"][h], d_st_b, _NN, preferred_element_type=F32)
                extra_h.append(jnp.sum(st_prev * d_st, axis=0, keepdims=True) * e_last
                               + jnp.sum(kd * d_kd, axis=0, keepdims=True))
                dst_ref[h] = d_st * e_last + _dot(do_b, ch["qb"][:, sl], _TN)
                dq_h.append(ch["d_qt"][h] * ch["e_q"][:, sl] + d_qb * ch["e_b"][:, sl])
                dkk = d_kd * ch["e_bl"][:, sl]
                for s_ in range(nsub):
                    dkk = dkk + ch["d_kst"][h][HGRN_CHUNK * s_:HGRN_CHUNK * (s_ + 1), :] * ch["e_k"][s_][:, sl]
                dk_h.append(dkk)
                dv_h.append(dv)
            dq = jnp.concatenate(dq_h, axis=1)
            dk = jnp.concatenate(dk_h, axis=1)
            dv = jnp.concatenate(dv_h, axis=1)
            extra = jnp.concatenate(extra_h, axis=1)
            db = q * dq - k * dk + jnp.where(row == HGRN_CHUNK - 1, extra, 0.0)
            dg = _tri_apply(upper, db)
            df = dg / f - dk
            dhf = df * (1.0 - lb) * sg * (1.0 - sg)
            dhq = dq * _dsilu(ch["hq"], _sig(ch["hq"]))
            full = jnp.concatenate([dhq, dhf, dv, ch["dhg"]], axis=1)
            duh_ref[ch["rows"], :] = full.astype(duh_ref.dtype)
            ch["full"] = full
            dbias_ref[...] += jnp.sum(full, axis=0, keepdims=True)
            dlb_acc[...] += jnp.sum(df * (1.0 - sg), axis=0, keepdims=True)
        dng_ref[...] += dng
        _store_transposed(duh_t_ref, [ch["full"] for ch in chunks])

        @pl.when(i == ns - 1)
        def _():
            lb = chunks[0]["lb"]
            d_a0 = dlb_acc[...] * lb * (1.0 - lb)
            r8 = lax.broadcasted_iota(jnp.int32, (8, HG_W), 0)
            dlb_ref[...] = jnp.where(r8 == 0, d_a0, jnp.where(r8 == 1, -d_a0, 0.0))

    col = lambda j: pl.BlockSpec((rows_step, HG_W), lambda i: (ns - 1 - i, j))
    return _call(
        body, name=name, grid=(ns,), ins=[uh, uh, uh, uh, o_pre, d_r, states, lb_raw, norm_g],
        in_specs=[col(0), col(1), col(2), col(3), col(0), col(d_r.shape[1] // HG_W - 1),
                  pl.BlockSpec((cps, 4, LANES, LANES), lambda i: (ns - 1 - i, 0, 0, 0)),
                  pl.BlockSpec((2, HG_W), lambda i: (0, 0)), pl.BlockSpec((1, LANES), lambda i: (0, 0))],
        out_specs=[pl.BlockSpec((rows_step, UH_W), lambda i: (ns - 1 - i, 0)),
                   pl.BlockSpec((UH_W, rows_step), lambda i: (0, ns - 1 - i)),
                   pl.BlockSpec((1, UH_W), lambda i: (0, 0)),
                   pl.BlockSpec((1, LANES), lambda i: (0, 0)),
                   pl.BlockSpec((8, HG_W), lambda i: (0, 0))],
        out_shape=[jax.ShapeDtypeStruct((t, UH_W), BF16), jax.ShapeDtypeStruct((UH_W, t), BF16),
                   jax.ShapeDtypeStruct((1, UH_W), F32),
                   jax.ShapeDtypeStruct((1, LANES), F32), jax.ShapeDtypeStruct((8, HG_W), F32)],
        scratch_shapes=[pltpu.VMEM((4, LANES, LANES), F32), pltpu.VMEM((cps, HGRN_CHUNK, HG_W), F32),
                        pltpu.VMEM((1, HG_W), F32)],
        sem=("arbitrary",), comm=comm)


def _ln_bwd_math(dy, xhat, rstd, g):
    dxh = dy * g
    return rstd * (dxh - jnp.mean(dxh, axis=1, keepdims=True)
                   - xhat * jnp.mean(dxh * xhat, axis=1, keepdims=True))


def _mm_stacked(a1, a2, b, *, name, tk=1024):
    (m1, kdim), m2, n = a1.shape, a2.shape[0], b.shape[1]
    tk = _pick(kdim, (tk, 512, 256, 128))
    nk = kdim // tk

    def body(a1_ref, a2_ref, b_ref, o_ref, ob_ref, acc_ref):
        k = pl.program_id(0)

        @pl.when(k == 0)
        def _():
            acc_ref[...] = jnp.zeros_like(acc_ref)

        bb = b_ref[...].astype(BF16)
        acc_ref[0:m1] += _dot(a1_ref[...], bb, _NN)
        acc_ref[m1:] += _dot(a2_ref[...], bb, _NN)

        @pl.when(k == nk - 1)
        def _():
            r = acc_ref[...]
            o_ref[...] = r
            ob_ref[...] = r.astype(BF16)

    ospec = pl.BlockSpec((m1 + m2, n), lambda k: (0, 0))
    return pl.pallas_call(
        body, name=name, grid=(nk,),
        in_specs=[pl.BlockSpec((m1, tk), lambda k: (0, k)), pl.BlockSpec((m2, tk), lambda k: (0, k)),
                  pl.BlockSpec((tk, n), lambda k: (k, 0))],
        out_specs=[ospec, ospec],
        out_shape=[jax.ShapeDtypeStruct((m1 + m2, n), F32), jax.ShapeDtypeStruct((m1 + m2, n), BF16)],
        scratch_shapes=[pltpu.VMEM((m1 + m2, n), F32)],
        compiler_params=_cp("arbitrary"))(a1, a2, b)


def _mm_rows(a, b, extras, *, name, epilogue, out_shape, out_specs, tb=False, tm=512, tk=1408, pair2=None):
    m, kdim = a.shape
    n = b.shape[0] if tb else b.shape[1]
    tm = _pick(m, (tm, 256, 128))
    tk = _pick(kdim, (tk, 1408, 1024, 768, 512, 256, 128))
    nk = kdim // tk
    b_spec = pl.BlockSpec((n, tk), lambda i, k: (0, k)) if tb else pl.BlockSpec((tk, n), lambda i, k: (k, 0))
    dims = _NT if tb else _NN
    n_ex, n_out, n_p2 = len(extras), len(out_shape), (0 if pair2 is None else 2)

    def body(*refs):
        a_ref, b_ref = refs[0], refs[1]
        p2_refs = refs[2:2 + n_p2]
        ex_refs = refs[2 + n_p2:2 + n_p2 + n_ex]
        o_refs = refs[2 + n_p2 + n_ex:2 + n_p2 + n_ex + n_out]
        acc_ref = refs[-1]
        i, k = pl.program_id(0), pl.program_id(1)

        @pl.when(k == 0)
        def _():
            if n_p2:
                acc_ref[...] = _dot(p2_refs[0][...], p2_refs[1][...], _NN)
            else:
                acc_ref[...] = jnp.zeros_like(acc_ref)

        acc_ref[...] += _dot(a_ref[...], b_ref[...], dims)

        @pl.when(k == nk - 1)
        def _():
            epilogue(acc_ref[...], ex_refs, o_refs, i == 0)

    p2_specs, p2_ins = [], []
    if pair2 is not None:
        k2 = pair2[0].shape[1]
        p2_specs = [pl.BlockSpec((tm, k2), lambda i, k: (i, 0)), pl.BlockSpec((k2, n), lambda i, k: (0, 0))]
        p2_ins = list(pair2)
    return pl.pallas_call(
        body, name=name, grid=(m // tm, nk),
        in_specs=[pl.BlockSpec((tm, tk), lambda i, k: (i, k)), b_spec] + p2_specs + [sp for _, sp in extras],
        out_specs=list(out_specs), out_shape=list(out_shape),
        scratch_shapes=[pltpu.VMEM((tm, n), F32)],
        compiler_params=_cp("arbitrary", "arbitrary"),
    )(a, b, *p2_ins, *[arr for arr, _ in extras])


def _rows_specs(tm, d):
    row = pl.BlockSpec((tm, d), lambda i, k: (i, 0))
    vec = pl.BlockSpec((1, d), lambda i, k: (0, 0))
    col = pl.BlockSpec((tm, 1), lambda i, k: (i, 0))
    return row, vec, col


def _mm_split_fwd(a, b_t, bias, split, name, tm=512):
    t, n = a.shape[0], b_t.shape[0]

    def epilogue(acc, ex, outs, first):
        z = acc + ex[0][...]
        outs[0][...] = z[:, :split]
        outs[1][...] = z[:, split:]

    return _mm_rows(a, b_t, [(bias, pl.BlockSpec((1, n), lambda i, k: (0, 0)))], name=name, epilogue=epilogue,
                    tb=True, tm=tm, tk=a.shape[1],
                    out_shape=[jax.ShapeDtypeStruct((t, split), F32), jax.ShapeDtypeStruct((t, n - split), F32)],
                    out_specs=[pl.BlockSpec((tm, split), lambda i, k: (i, 0)),
                               pl.BlockSpec((tm, n - split), lambda i, k: (i, 0))])


def _mm_ln_fwd(a, b, pair2, addend, g, beta, name, tm=512):
    t, d = addend.shape
    tm = _pick(t, (tm, 256, 128))
    row, vec, col = _rows_specs(tm, d)

    def epilogue(acc, ex, outs, first):
        z = acc + ALPHA * ex[0][...]
        mu = jnp.mean(z, axis=1, keepdims=True)
        zc = z - mu
        rstd = lax.rsqrt(jnp.mean(zc * zc, axis=1, keepdims=True) + LN_EPS)
        xhat = zc * rstd
        h = xhat * ex[1][...] + ex[2][...]
        outs[0][...] = h
        outs[1][...] = h.astype(BF16)
        outs[2][...] = xhat
        outs[3][...] = rstd

    return _mm_rows(a, b, [(addend, row), (g, vec), (beta, vec)], name=name, epilogue=epilogue, tm=tm, pair2=pair2,
                    out_shape=[jax.ShapeDtypeStruct((t, d), F32), jax.ShapeDtypeStruct((t, d), BF16),
                               jax.ShapeDtypeStruct((t, d), F32), jax.ShapeDtypeStruct((t, 1), F32)],
                    out_specs=[row, row, row, col])


CONV_RB = 32
HALO = 8


def _sum8(x):
    acc = x[0:8]
    for r in range(8, x.shape[0], 8):
        acc = acc + x[r:r + 8]
    return acc


FFN_TILE = 256
FFN_COLS = 256


def _rows_before(win, k):
    return pltpu.roll(win, k, 0)[HALO:]


def _rows_after(win, k):
    n = win.shape[0]
    return pltpu.roll(win, n - k, 0)[0:n - HALO]


def _resident(shape):
    return pl.BlockSpec(shape, lambda i: (0,) * len(shape), pipeline_mode=pl.Buffered(1))


def _ffn_fwd(h1b, h1, w_up_t, conv_w, conv_b, w_down, target, ln2_g, ln2_b, name, comm=None):
    t, d = h1.shape
    tr = _pick(t, (FFN_TILE, 128))
    nblk = D_FF // FFN_COLS
    rb = CONV_RB

    def body(a_ref, wup_ref, cw_ref, cb_ref, wd_ref, h1_ref, tgt_ref, g_ref, b_ref,
             u2_ref, hm_ref, dz_ref, dg_ref, db_ref, loss_ref, ext):
        i = pl.program_id(0)

        @pl.when(i == 0)
        def _():
            ext[0:HALO, :] = jnp.zeros((HALO, D_FF), F32)
            dg_ref[...] = jnp.zeros_like(dg_ref)
            db_ref[...] = jnp.zeros_like(db_ref)
            loss_ref[...] = jnp.zeros_like(loss_ref)

        a = a_ref[...]
        for c in range(nblk):
            cs = slice(c * FFN_COLS, (c + 1) * FFN_COLS)
            vs = slice(D_FF + c * FFN_COLS, D_FF + (c + 1) * FFN_COLS)
            gate_pre = lax.dot_general(a, wup_ref[cs, :], _NT, preferred_element_type=F32)
            u2_ref[:, cs] = gate_pre
            ext[HALO:, cs] = gate_pre
            u2_ref[:, vs] = lax.dot_general(a, wup_ref[vs, :], _NT, preferred_element_type=F32)
        acc = jnp.zeros((tr, d), F32)
        for c in range(nblk):
            cs = slice(c * FFN_COLS, (c + 1) * FFN_COLS)
            for sub in range(FFN_COLS // LANES):
                ln = slice(c * FFN_COLS + sub * LANES, c * FFN_COLS + (sub + 1) * LANES)
                vl = slice(D_FF + c * FFN_COLS + sub * LANES, D_FF + c * FFN_COLS + (sub + 1) * LANES)
                w0, w1, w2, bb = cw_ref[0:1, ln], cw_ref[1:2, ln], cw_ref[2:3, ln], cb_ref[:, ln]
                for r0 in range(0, tr, rb):
                    win = ext[r0:r0 + HALO + rb, ln]
                    gate = _rows_before(win, 2) * w0 + _rows_before(win, 1) * w1 + win[HALO:] * w2 + bb
                    hm_ref[r0:r0 + rb, ln] = (gate * _sig(gate) * u2_ref[r0:r0 + rb, vl]).astype(hm_ref.dtype)
            acc = acc + lax.dot_general(hm_ref[:, cs], wd_ref[cs, :], _NN, preferred_element_type=F32)
        ext[0:HALO, :] = ext[tr:tr + HALO, :]

        z = acc + ALPHA * h1_ref[...]
        gg = g_ref[...]
        mu = jnp.mean(z, axis=1, keepdims=True)
        zc = z - mu
        rstd = lax.rsqrt(jnp.mean(zc * zc, axis=1, keepdims=True) + LN_EPS)
        xhat = zc * rstd
        err = xhat * gg + b_ref[...] - tgt_ref[...]
        loss_ref[...] += 0.5 * jnp.sum(jnp.mean(err * err, axis=1, keepdims=True))
        dy = err * (1.0 / d)
        dz_ref[...] = _ln_bwd_math(dy, xhat, rstd, gg)
        dg_ref[...] += jnp.sum(dy * xhat, axis=0, keepdims=True)
        db_ref[...] += jnp.sum(dy, axis=0, keepdims=True)

    row = lambda w: pl.BlockSpec((tr, w), lambda i: (i, 0))
    vec = pl.BlockSpec((1, d), lambda i: (0, 0))
    return _call(
        body, name=name, grid=(t // tr,),
        ins=[h1b, w_up_t, conv_w, conv_b, w_down, h1, target, ln2_g, ln2_b],
        in_specs=[row(d), _resident((2 * D_FF, d)), _resident((3, D_FF)), _resident((1, D_FF)),
                  _resident((D_FF, d)), row(d), row(d), vec, vec],
        out_specs=[row(2 * D_FF), row(D_FF), row(d), vec, vec, pl.BlockSpec((1, LANES), lambda i: (0, 0))],
        out_shape=[jax.ShapeDtypeStruct((t, 2 * D_FF), F32), jax.ShapeDtypeStruct((t, D_FF), BF16),
                   jax.ShapeDtypeStruct((t, d), F32), jax.ShapeDtypeStruct((1, d), F32),
                   jax.ShapeDtypeStruct((1, d), F32), jax.ShapeDtypeStruct((1, LANES), F32)],
        scratch_shapes=[pltpu.VMEM((tr + HALO, D_FF), F32)],
        sem=("arbitrary",), comm=comm)


def _ffn_bwd(dz2, u2, w_down, w_up_t, conv_w, conv_b, xhat1, rstd1, ln1_g, name, comm=None):
    t, d = dz2.shape
    tr = _pick(t, (FFN_TILE, 128))
    nt = t // tr
    hb = tr // HALO
    nblk = D_FF // FFN_COLS
    rb = CONV_RB

    def body(dz2_ref, dz2_next_ref, u2_ref, gp_prev_ref, wd_ref, wup_ref, cw_ref, cb_ref, xhat_ref, rstd_ref,
             g1_ref, du_ref, dz1_ref, dz1b_ref, dw_ref, dcb_ref, dg1_ref, db1_ref, head, dh_s, dg_s):
        i = pl.program_id(0)

        @pl.when(i == 0)
        def _():
            dg_s[tr:, :] = jnp.zeros((HALO, D_FF), F32)
            dw_ref[...] = jnp.zeros_like(dw_ref)
            dcb_ref[...] = jnp.zeros_like(dcb_ref)
            dg1_ref[...] = jnp.zeros_like(dg1_ref)
            db1_ref[...] = jnp.zeros_like(db1_ref)

        dz2 = dz2_ref[...]

        @pl.when(i == 0)
        def _():
            dz2_b = dz2.astype(BF16)
            for c in range(nblk):
                cs = slice(c * FFN_COLS, (c + 1) * FFN_COLS)
                dh_s[:, cs] = lax.dot_general(dz2_b, wd_ref[cs, :], _NT, preferred_element_type=F32)

        dz2_next = dz2_next_ref[...].astype(BF16)
        dh_next = [lax.dot_general(dz2_next, wd_ref[c * FFN_COLS:(c + 1) * FFN_COLS, :], _NT,
                                   preferred_element_type=F32) for c in range(nblk)]
        head[0:HALO, :] = jnp.where(i == nt - 1, 0.0, gp_prev_ref[...])
        head[HALO:, :] = u2_ref[0:rb, 0:D_FF]

        acc = jnp.zeros((tr, d), F32)
        for blk in range(nblk):
            for c in range(blk * FFN_COLS // LANES, (blk + 1) * FFN_COLS // LANES):
                ln = slice(c * LANES, (c + 1) * LANES)
                vl = slice(D_FF + c * LANES, D_FF + (c + 1) * LANES)
                w0, w1, w2, bb = cw_ref[0:1, ln], cw_ref[1:2, ln], cw_ref[2:3, ln], cb_ref[:, ln]
                acc_b = jnp.zeros((8, LANES), F32)
                acc_w = [jnp.zeros((8, LANES), F32) for _ in range(3)]
                for r0 in range(0, tr, rb):
                    win = head[:, ln] if r0 == 0 else u2_ref[r0 - HALO:r0 + rb, ln]
                    g_m2, g_m1, g_0 = _rows_before(win, 2), _rows_before(win, 1), win[HALO:]
                    gate = g_m2 * w0 + g_m1 * w1 + g_0 * w2 + bb
                    sg = _sig(gate)
                    dh = dh_s[r0:r0 + rb, ln]
                    dgate = dh * u2_ref[r0:r0 + rb, vl] * _dsilu(gate, sg)
                    dg_s[r0:r0 + rb, ln] = dgate
                    du_ref[r0:r0 + rb, vl] = (dh * (gate * sg)).astype(du_ref.dtype)
                    acc_b = acc_b + _sum8(dgate)
                    acc_w[0] = acc_w[0] + _sum8(dgate * g_m2)
                    acc_w[1] = acc_w[1] + _sum8(dgate * g_m1)
                    acc_w[2] = acc_w[2] + _sum8(dgate * g_0)
                dcb_ref[:, ln] += jnp.sum(acc_b, axis=0, keepdims=True)
                for j in range(3):
                    dw_ref[j:j + 1, ln] += jnp.sum(acc_w[j], axis=0, keepdims=True)
                for r0 in range(0, tr, rb):
                    win = dg_s[r0:r0 + rb + HALO, ln]
                    d_gp = _rows_after(win, 2) * w0 + _rows_after(win, 1) * w1 + win[0:rb] * w2
                    du_ref[r0:r0 + rb, ln] = d_gp.astype(du_ref.dtype)
            cs = slice(blk * FFN_COLS, (blk + 1) * FFN_COLS)
            vs = slice(D_FF + blk * FFN_COLS, D_FF + (blk + 1) * FFN_COLS)
            acc = acc + lax.dot_general(du_ref[:, cs], wup_ref[cs, :], _NN, preferred_element_type=F32)
            acc = acc + lax.dot_general(du_ref[:, vs], wup_ref[vs, :], _NN, preferred_element_type=F32)
        dg_s[tr:, :] = dg_s[0:HALO, :]
        for c in range(nblk):
            dh_s[:, c * FFN_COLS:(c + 1) * FFN_COLS] = dh_next[c]
        dy = acc + ALPHA * dz2
        xh = xhat_ref[...]
        dz1 = _ln_bwd_math(dy, xh, rstd_ref[...], g1_ref[...])
        dz1_ref[...] = dz1
        dz1b_ref[...] = dz1.astype(BF16)
        dg1_ref[...] += jnp.sum(dy * xh, axis=0, keepdims=True)
        db1_ref[...] += jnp.sum(dy, axis=0, keepdims=True)

    rev = lambda w: pl.BlockSpec((tr, w), lambda i: (nt - 1 - i, 0))
    vec = pl.BlockSpec((1, d), lambda i: (0, 0))
    return _call(
        body, name=name, grid=(nt,),
        ins=[dz2, dz2, u2, u2, w_down, w_up_t, conv_w, conv_b, xhat1, rstd1, ln1_g],
        in_specs=[rev(d), pl.BlockSpec((tr, d), lambda i: (jnp.maximum(nt - 2 - i, 0), 0)), rev(2 * D_FF),
                  pl.BlockSpec((HALO, D_FF), lambda i: (jnp.maximum((nt - 1 - i) * hb - 1, 0), 0)),
                  _resident((D_FF, d)), _resident((2 * D_FF, d)), _resident((3, D_FF)), _resident((1, D_FF)),
                  rev(d), pl.BlockSpec((tr, 1), lambda i: (nt - 1 - i, 0)), vec],
        out_specs=[rev(2 * D_FF), rev(d), rev(d), pl.BlockSpec((8, D_FF), lambda i: (0, 0)),
                   pl.BlockSpec((1, D_FF), lambda i: (0, 0)), vec, vec],
        out_shape=[jax.ShapeDtypeStruct((t, 2 * D_FF), BF16), jax.ShapeDtypeStruct((t, d), F32),
                   jax.ShapeDtypeStruct((t, d), BF16),
                   jax.ShapeDtypeStruct((8, D_FF), F32), jax.ShapeDtypeStruct((1, D_FF), F32),
                   jax.ShapeDtypeStruct((1, d), F32), jax.ShapeDtypeStruct((1, d), F32)],
        scratch_shapes=[pltpu.VMEM((HALO + rb, D_FF), F32), pltpu.VMEM((tr, D_FF), F32),
                        pltpu.VMEM((tr + HALO, D_FF), F32)],
        sem=("arbitrary",), comm=comm)


def _pad_rows(a, rows):
    return jnp.pad(a, ((0, rows - a.shape[0]), (0, 0)))


SMALL_LAYOUT = (("ln1_g", 1024), ("ln1_b", 1024), ("b_in", 2816), ("sinks", 8), ("hgrn_lb", 1024),
                ("hgrn_norm_g", 128), ("ln2_g", 1024), ("ln2_b", 1024), ("conv_b", 2816), ("loss", 1))
SMALL_SHAPES = {"ln1_g": (1, 1024), "ln1_b": (1, 1024), "b_in": (1, 2816), "sinks": (1, 8), "hgrn_lb": (2, 512),
                "hgrn_norm_g": (1, 128), "ln2_g": (1, 1024), "ln2_b": (1, 1024), "conv_b": (1, 2816),
                "loss": (1,)}


def _pack_small(parts):
    rows = []
    for name, size in SMALL_LAYOUT:
        flat = parts[name].reshape(-1).astype(F32)
        padded = -(-size // LANES) * LANES
        rows.append(jnp.pad(flat, (0, padded - size)).reshape(-1, LANES))
    return _pad_rows(jnp.concatenate(rows, axis=0), SMALL_ROWS)


def _small_update(small_g, ws, ms, vs, name):
    names = [n for n, _ in SMALL_LAYOUT if n != "loss"]
    first, r = {}, 0
    for n, size in SMALL_LAYOUT:
        first[n] = r
        r += -(-size // LANES)
    npar = len(names)

    def body(*refs):
        g_ref = refs[0]
        w_refs, m_refs, v_refs = (refs[1 + q * npar:1 + (q + 1) * npar] for q in range(3))
        outs = refs[1 + 3 * npar:-1]
        sum_ref = refs[-1]
        acc = g_ref[0]
        for s in range(1, N_DEV):
            acc = acc + g_ref[s]
        sum_ref[...] = acc
        outs[0][...] = sum_ref[first["loss"]:first["loss"] + 1, 0:1]
        for p, n in enumerate(names):
            g_out, d_out, m_out, v_out = outs[1 + 4 * p:5 + 4 * p]
            rows, cols = SMALL_SHAPES[n]
            if cols < LANES:
                g_out[...] = sum_ref[first[n]:first[n] + 1, 0:cols]
            else:
                per = cols // LANES
                for h in range(rows):
                    for j in range(per):
                        rr = first[n] + h * per + j
                        g_out[h:h + 1, j * LANES:(j + 1) * LANES] = sum_ref[rr:rr + 1, :]
            d_out[...], m_out[...], v_out[...] = _adamw_math(w_refs[p][...], g_out[...], m_refs[p][...],
                                                            v_refs[p][...])

    out_shape = [jax.ShapeDtypeStruct((1, 1), F32)]
    for n in names:
        out_shape += [jax.ShapeDtypeStruct(SMALL_SHAPES[n], F32)] * 4
    res = pl.pallas_call(
        body, name=name, out_shape=out_shape,
        scratch_shapes=[pltpu.VMEM((SMALL_ROWS, LANES), F32)],
        compiler_params=_cp(),
    )(small_g, *[ws[n] for n in names], *[ms[n] for n in names], *[vs[n] for n in names])
    return res[0], {n: res[1 + 4 * p:5 + 4 * p] for p, n in enumerate(names)}


def _conv_w_update(recv, w, m, v, name):
    taps, cols = w.shape

    def body(r_ref, w_ref, m_ref, v_ref, g_ref, d_ref, nm_ref, nv_ref):
        acc = r_ref[0]
        for s in range(1, N_DEV):
            acc = acc + r_ref[s]
        g = acc[0:taps]
        res = (g,) + _adamw_math(w_ref[...], g, m_ref[...], v_ref[...])
        for o_ref, val in zip((g_ref, d_ref, nm_ref, nv_ref), res):
            for k in range(taps):
                o_ref[k] = val[k:k + 1]

    shp = jax.ShapeDtypeStruct((taps, 1, cols), F32)
    outs = pl.pallas_call(body, name=name, out_shape=[shp, shp, shp, shp], compiler_params=_cp())(recv, w, m, v)
    return [o.transpose(1, 0, 2) for o in outs]


def kernel(x, positions, ln1_g, ln1_b, w_in, b_in, sinks, hgrn_lb, hgrn_norm_g, w_o, ln2_g, ln2_b, w_up, conv_w, conv_b, w_down, loss_target, m_ln1_g, m_ln1_b, m_w_in, m_b_in, m_sinks, m_hgrn_lb, m_hgrn_norm_g, m_w_o, m_ln2_g, m_ln2_b, m_w_up, m_conv_w, m_conv_b, m_w_down, v_ln1_g, v_ln1_b, v_w_in, v_b_in, v_sinks, v_hgrn_lb, v_hgrn_norm_g, v_w_o, v_ln2_g, v_ln2_b, v_w_up, v_conv_w, v_conv_b, v_w_down):
    t = x.shape[1]
    x2 = x[0]
    target = loss_target[0]
    pos_col = positions.reshape(t, 1)

    w_in_t_s = w_in[0].T.astype(BF16)
    w_up_t_s = w_up[0].T.astype(BF16)
    w_o_s = w_o[0].astype(BF16)
    w_down_s = w_down[0].astype(BF16)
    (ctab, stab, xb), (w_in_t_g, cw_g) = _prep(
        pos_col, x2, "prep_ag_w_in", _Comm([{"kind": "gather", "arr": w_in_t_s}, {"kind": "gather", "arr": _pad_rows(conv_w[0], 8)}]))
    w_in_t = w_in_t_g.reshape(D_FF, D_MODEL)
    conv_w_f = cw_g[:, 0:3].transpose(1, 0, 2).reshape(3, D_FF)

    ua, uh = _mm_split_fwd(xb, w_in_t, b_in, UA_W, "fwd_in")
    half_up = SHARD_UP // 2
    (a_out, a_out_t), (w_o_g, w_up_half) = _attn_fwd(
        ua, ctab, stab, sinks, "attn_fwd",
        comm=_Comm([{"kind": "gather", "arr": w_o_s},
                    {"kind": "gather", "arr": w_up_t_s, "rows": (0, half_up), "dst_rows": SHARD_UP}]))
    (r_out, r_out_t, o_pre, states), (w_up_t_g, w_down_g) = _hgrn_fwd(
        uh, hgrn_lb, hgrn_norm_g, "hgrn_fwd",
        comm=_Comm([{"kind": "gather", "arr": w_up_t_s, "rows": (half_up, half_up), "dst_rows": SHARD_UP,
                     "dst_first": half_up, "into": w_up_half},
                    {"kind": "gather", "arr": w_down_s}]))
    w_down_f = w_down_g.reshape(D_FF, D_MODEL)
    w_o_f = w_o_g.reshape(D_MODEL, D_MODEL)
    w_up_t = w_up_t_g.reshape(2 * D_FF, D_MODEL)
    h1, h1b, xhat1, rstd1 = _mm_ln_fwd(r_out, w_o_f[ATTN_W:], (a_out, w_o_f[:ATTN_W]), x2, ln1_g, ln1_b,
                                       "fwd_o_ln1")
    u2, hmid, dz2, d_ln2_g, d_ln2_b, loss_part = _ffn_fwd(h1b, h1, w_up_t, conv_w_f, conv_b, w_down_f, target,
                                                         ln2_g, ln2_b, "ffn_fwd")[0]

    d_w_down, d_w_down_b = _mm(hmid, dz2, ta=True, out_dtype2=BF16, tm=1408, tk=1024, name="bwd_down_dw")
    (d_u2, dz1, dz1b, d_conv_w8, d_conv_b, d_ln1_g, d_ln1_b), (recv_down,) = _ffn_bwd(
        dz2, u2, w_down_f, w_up_t, conv_w_f, conv_b, xhat1, rstd1, ln1_g, "ffn_bwd",
        comm=_Comm([{"kind": "exchange", "arr": d_w_down_b.reshape(N_DEV, SHARD_DOWN, D_MODEL)}]))
    d_w_up_t, d_w_up_t_b = _mm(d_u2, h1b, ta=True, out_dtype2=BF16, tm=1408, tk=1024, name="bwd_up_dw")
    d_ar = _mm(dz1b, w_o_f, tb=True, name="bwd_o_dx")
    d_w_o, d_w_o_b = _mm_stacked(a_out_t, r_out_t, dz1b, name="bwd_o_dw")
    d_w_up_x = d_w_up_t_b.reshape(N_DEV, SHARD_UP, D_MODEL)
    half = SHARD_UP // 2
    d_cw_x = d_conv_w8.reshape(8, N_DEV, SHARD_IN).transpose(1, 0, 2)
    (d_ua, d_ua_t, d_bias_a, d_sinks), (recv_up_half, recv_cw) = _attn_bwd(
        ua, d_ar, ctab, stab, sinks, "attn_bwd",
        comm=_Comm([{"kind": "exchange", "arr": d_w_up_x, "rows": (0, half), "dst_rows": SHARD_UP},
                    {"kind": "exchange", "arr": d_cw_x}]))
    (d_uh, d_uh_t, d_bias_h, d_norm_g, d_lb8), (recv_up, recv_o) = _hgrn_bwd(
        uh, o_pre, d_ar, states, hgrn_lb, hgrn_norm_g, "hgrn_bwd",
        comm=_Comm([{"kind": "exchange", "arr": d_w_up_x, "rows": (half, half), "dst_rows": SHARD_UP,
                     "dst_first": half, "into": recv_up_half},
                    {"kind": "exchange", "arr": d_w_o_b.reshape(N_DEV, SHARD_O, D_MODEL)}]))
    d_w_in_part = _mm(d_ua_t, xb, out_dtype2=BF16, tm=UA_W, tk=t, out_rows=D_FF, name="bwd_in_dw_attn")
    d_w_in_t, d_w_in_t_b = _mm(d_uh_t, xb, out_dtype2=BF16, tm=256, tk=t, out_rows=D_FF, first_row=UA_W,
                               into=d_w_in_part, name="bwd_in_dw_hgrn")
    small_local = _pack_small({
        "ln1_g": d_ln1_g, "ln1_b": d_ln1_b, "b_in": jnp.concatenate([d_bias_a, d_bias_h], axis=1),
        "sinks": d_sinks[:, :8], "hgrn_lb": d_lb8[0:2], "hgrn_norm_g": d_norm_g, "ln2_g": d_ln2_g,
        "ln2_b": d_ln2_b, "conv_b": d_conv_b, "loss": loss_part[:, :1]})
    d_w_in_x = d_w_in_t_b.reshape(N_DEV, SHARD_IN, D_MODEL)
    res_up, (from_sibling,) = _sum_shards_adamw(
        [recv_up], d_w_up_t, w_up[0].T, m_w_up[0].T, v_w_up[0].T, "adamw_w_up",
        comm=_Comm([{"kind": "pair4", "arr": d_w_in_x}]))
    res_up = [r.T for r in res_up]
    own_in, chip_part = _pair_reduce(from_sibling, d_w_in_t, "pair_reduce_w_in")
    dx, (from_chips, small_g) = _mm(d_uh, w_in_t, b_first=UA_W, tk=256, addend=dz1, addend_scale=ALPHA,
                                    name="bwd_in_dx_hgrn",
                                    comm=_Comm([{"kind": "chips3", "arr": chip_part},
                                                {"kind": "gather", "arr": small_local}]))
    dx = _mm(d_ua, w_in_t, addend=dx, tk=UA_W, name="bwd_in_dx_attn")

    res_in = [r.T for r in _chip_sum_adamw(from_chips, own_in, w_in[0].T, m_w_in[0].T, v_w_in[0].T, "adamw_w_in")]
    res_o = _sum_shards_adamw([recv_o], d_w_o, w_o[0], m_w_o[0], v_w_o[0], "adamw_w_o")
    res_down = _sum_shards_adamw([recv_down], d_w_down, w_down[0], m_w_down[0], v_w_down[0],
                                 "adamw_w_down")
    res_cw = _conv_w_update(recv_cw, conv_w[0], m_conv_w[0], v_conv_w[0], "adamw_conv_w")
    big = {"w_in": [r[None] for r in res_in], "w_up": [r[None] for r in res_up],
           "w_o": [r[None] for r in res_o], "w_down": [r[None] for r in res_down],
           "conv_w": list(res_cw)}

    loss11, small = _small_update(
        small_g,
        {"ln1_g": ln1_g, "ln1_b": ln1_b, "b_in": b_in, "sinks": sinks, "hgrn_lb": hgrn_lb,
         "hgrn_norm_g": hgrn_norm_g, "ln2_g": ln2_g, "ln2_b": ln2_b, "conv_b": conv_b},
        {"ln1_g": m_ln1_g, "ln1_b": m_ln1_b, "b_in": m_b_in, "sinks": m_sinks, "hgrn_lb": m_hgrn_lb,
         "hgrn_norm_g": m_hgrn_norm_g, "ln2_g": m_ln2_g, "ln2_b": m_ln2_b, "conv_b": m_conv_b},
        {"ln1_g": v_ln1_g, "ln1_b": v_ln1_b, "b_in": v_b_in, "sinks": v_sinks, "hgrn_lb": v_hgrn_lb,
         "hgrn_norm_g": v_hgrn_norm_g, "ln2_g": v_ln2_g, "ln2_b": v_ln2_b, "conv_b": v_conv_b},
        "adamw_small")
    loss = loss11[0, 0]

    order = ["ln1_g", "ln1_b", "w_in", "b_in", "sinks", "hgrn_lb", "hgrn_norm_g", "w_o", "ln2_g", "ln2_b",
             "w_up", "conv_w", "conv_b", "w_down"]

    def pick(idx):
        return [big[n][idx] if n in big else small[n][idx] for n in order]

    return (loss, dx[None], *pick(0), *pick(1), *pick(2), *pick(3))
```
